```python
import jax, jax.numpy as jnp
from jax import lax
import numpy as np

D_MODEL = 1024
BATCH = 8
SEQ = 4096
DEPTH = 1

N_META = 16
POOL_WIDTH = D_MODEL // 2
POOL_WINDOWS = (2, 4, 8, 16)
POOL_GROUP = POOL_WIDTH // len(POOL_WINDOWS)
N_HEADS = 8
HEAD_DIM = 64
ATTN_WIDTH = N_HEADS * HEAD_DIM
Q_BLOCK = 128
RMS_EPS = 1e-6
IN_SIZES = (POOL_WIDTH, POOL_WIDTH, ATTN_WIDTH, ATTN_WIDTH, ATTN_WIDTH, ATTN_WIDTH, N_HEADS, D_MODEL, D_MODEL)
N_IN = 2 * POOL_WIDTH + 4 * ATTN_WIDTH + N_HEADS + 2 * D_MODEL

kernel_name = "gated_pool_forgetting_attn_hybrid"


def _split_points():
    pts, acc = [], 0
    for s in IN_SIZES[:-1]:
        acc += s
        pts.append(acc)
    return pts


def rmsnorm(x, g):
    xf = x.astype(jnp.float32)
    xf = xf * lax.rsqrt(jnp.mean(xf * xf, axis=-1, keepdims=True) + RMS_EPS)
    return (xf * g.astype(jnp.float32)).astype(x.dtype)


def causal_multiscale_pool(u, pool_w, pool_scale):
    B, L, _ = u.shape
    groups = jnp.split(u, len(POOL_WINDOWS), axis=-1)
    pos = jnp.arange(L, dtype=jnp.float32)[:, None]
    pooled = []
    for w, ug in zip(POOL_WINDOWS, groups):
        uf = ug.astype(jnp.float32)
        c = jnp.cumsum(uf, axis=1)
        c_prev = jnp.pad(c, ((0, 0), (w, 0), (0, 0)))[:, :L]
        count = jnp.minimum(pos + 1.0, float(w))
        pooled.append((c - c_prev) / count - uf)
    p = jnp.stack(pooled, axis=2).astype(u.dtype)
    y = jnp.einsum('blgc,gcd->blgd', p, pool_w)
    return y.reshape(B, L, POOL_WIDTH) * pool_scale


def forgetting_attention(q, k, v, log_f):
    B, L, H, Dh = q.shape
    pad = (Q_BLOCK - L % Q_BLOCK) % Q_BLOCK
    padw = ((0, 0), (pad, 0), (0, 0), (0, 0))
    qp, kp, vp = jnp.pad(q, padw), jnp.pad(k, padw), jnp.pad(v, padw)
    c = jnp.cumsum(log_f, axis=1)
    c = jnp.transpose(jnp.pad(c, ((0, 0), (pad, 0), (0, 0))), (0, 2, 1))
    Lp = L + pad
    scale = 1.0 / np.sqrt(HEAD_DIM)
    outs = []
    for i in range(Lp // Q_BLOCK):
        q0, q1 = i * Q_BLOCK, (i + 1) * Q_BLOCK
        qb, kb, vb = qp[:, q0:q1], kp[:, :q1], vp[:, :q1]
        s = jnp.einsum('bqhd,bkhd->bhqk', qb, kb).astype(jnp.float32) * scale
        s = s + c[:, :, q0:q1, None] - c[:, :, None, :q1]
        q_idx = jnp.arange(q0, q1)[:, None]
        k_idx = jnp.arange(q1)[None, :]
        valid = (k_idx <= q_idx) & ((k_idx >= pad) | (k_idx == q_idx))
        s = jnp.where(valid, s, -jnp.inf)
        p = jax.nn.softmax(s, axis=-1)
        outs.append(jnp.einsum('bhqk,bkhd->bqhd', p.astype(vb.dtype), vb))
    o = jnp.concatenate(outs, axis=1)[:, pad:]
    return o


def hybrid_layer(x, norm_g, w_in, b_forget, pool_w, pool_scale, w_up_pool, w_up_attn, w_out):
    B, L, _ = x.shape
    h = rmsnorm(x, norm_g)
    proj = jnp.einsum('bld,dn->bln', h, w_in)
    u_pool, z_pool, q, k, v, z_attn, f_logit, g_pool, g_attn = jnp.split(proj, _split_points(), axis=-1)
    y_pool = causal_multiscale_pool(u_pool, pool_w, pool_scale) * jax.nn.silu(z_pool)
    q = q.reshape(B, L, N_HEADS, HEAD_DIM)
    k = k.reshape(B, L, N_HEADS, HEAD_DIM)
    v = v.reshape(B, L, N_HEADS, HEAD_DIM)
    log_f = jax.nn.log_sigmoid((f_logit + b_forget).astype(jnp.float32))
    y_attn = forgetting_attention(q, k, v, log_f).reshape(B, L, ATTN_WIDTH) * jax.nn.silu(z_attn)
    merged = (jax.nn.sigmoid(g_pool) * jnp.einsum('blc,cd->bld', y_pool, w_up_pool)
              + jax.nn.sigmoid(g_attn) * jnp.einsum('blc,cd->bld', y_attn, w_up_attn))
    return x + jnp.einsum('bld,de->ble', merged, w_out)


def _fwd_setup_inputs(seed: int = 0) -> dict:
    key = jax.random.key(seed)
    ks = jax.random.split(key, 12)
    f0 = sum(IN_SIZES[:6])
    x = jax.random.normal(ks[0], (BATCH, SEQ, D_MODEL), jnp.float32)
    meta_tokens = jax.random.normal(ks[1], (N_META, D_MODEL), jnp.float32)
    norm_g = 1.0 + 0.02 * jax.random.normal(ks[2], (DEPTH, D_MODEL), jnp.float32)
    w_in = jax.random.normal(ks[3], (DEPTH, D_MODEL, N_IN), jnp.float32) * D_MODEL ** -0.5
    w_in = w_in.at[:, :, f0:f0 + N_HEADS].multiply(0.1)
    b_forget = (jnp.linspace(1.0, 6.0, N_HEADS, dtype=jnp.float32)[None, :]
                + 0.1 * jax.random.normal(ks[4], (DEPTH, N_HEADS), jnp.float32))
    pool_w = jax.random.normal(ks[5], (DEPTH, len(POOL_WINDOWS), POOL_GROUP, POOL_GROUP), jnp.float32) * POOL_GROUP ** -0.5
    pool_scale = 1.0 + 0.1 * jax.random.normal(ks[6], (DEPTH, POOL_WIDTH), jnp.float32)
    w_up_pool = jax.random.normal(ks[7], (DEPTH, POOL_WIDTH, D_MODEL), jnp.float32) * POOL_WIDTH ** -0.5
    w_up_attn = jax.random.normal(ks[8], (DEPTH, ATTN_WIDTH, D_MODEL), jnp.float32) * ATTN_WIDTH ** -0.5
    w_out = jax.random.normal(ks[9], (DEPTH, D_MODEL, D_MODEL), jnp.float32) * D_MODEL ** -0.5
    final_norm_g = 1.0 + 0.02 * jax.random.normal(ks[10], (D_MODEL,), jnp.float32)
    return {"x": x, "meta_tokens": meta_tokens, "norm_g": norm_g, "w_in": w_in,
            "b_forget": b_forget, "pool_w": pool_w, "pool_scale": pool_scale,
            "w_up_pool": w_up_pool, "w_up_attn": w_up_attn, "w_out": w_out,
            "final_norm_g": final_norm_g}


def _fwd_reference(x, meta_tokens, norm_g, w_in, b_forget, pool_w, pool_scale, w_up_pool, w_up_attn, w_out, final_norm_g):
    B = x.shape[0]
    meta = jnp.broadcast_to(meta_tokens.astype(x.dtype)[None], (B, N_META, D_MODEL))
    h = jnp.concatenate([meta, x], axis=1)
    for l in range(DEPTH):
        h = hybrid_layer(h, norm_g[l], w_in[l], b_forget[l], pool_w[l], pool_scale[l],
                         w_up_pool[l], w_up_attn[l], w_out[l])
    h = rmsnorm(h, final_norm_g)
    return h[:, N_META:]


import jax as _jax
import jax.numpy as _jnp

TWIN_FORMAT = 'train_step'
FWD_PARAMS = ['x', 'meta_tokens', 'norm_g', 'w_in', 'b_forget', 'pool_w', 'pool_scale', 'w_up_pool', 'w_up_attn', 'w_out', 'final_norm_g']
TWIN_WEIGHTS = ['meta_tokens', 'norm_g', 'w_in', 'b_forget', 'pool_w', 'pool_scale', 'w_up_pool', 'w_up_attn', 'w_out', 'final_norm_g']
TWIN_DIFF_INPUT = 'x'
TWIN_INPUTS = ['x', 'meta_tokens', 'norm_g', 'w_in', 'b_forget', 'pool_w', 'pool_scale', 'w_up_pool', 'w_up_attn', 'w_out', 'final_norm_g', 'loss_target', 'm_meta_tokens', 'm_norm_g', 'm_w_in', 'm_b_forget', 'm_pool_w', 'm_pool_scale', 'm_w_up_pool', 'm_w_up_attn', 'm_w_out', 'm_final_norm_g', 'v_meta_tokens', 'v_norm_g', 'v_w_in', 'v_b_forget', 'v_pool_w', 'v_pool_scale', 'v_w_up_pool', 'v_w_up_attn', 'v_w_out', 'v_final_norm_g']
TWIN_OUTPUTS = ['loss', 'grad_x', 'grad_meta_tokens', 'grad_norm_g', 'grad_w_in', 'grad_b_forget', 'grad_pool_w', 'grad_pool_scale', 'grad_w_up_pool', 'grad_w_up_attn', 'grad_w_out', 'grad_final_norm_g', 'delta_meta_tokens', 'delta_norm_g', 'delta_w_in', 'delta_b_forget', 'delta_pool_w', 'delta_pool_scale', 'delta_w_up_pool', 'delta_w_up_attn', 'delta_w_out', 'delta_final_norm_g', 'new_m_meta_tokens', 'new_m_norm_g', 'new_m_w_in', 'new_m_b_forget', 'new_m_pool_w', 'new_m_pool_scale', 'new_m_w_up_pool', 'new_m_w_up_attn', 'new_m_w_out', 'new_m_final_norm_g', 'new_v_meta_tokens', 'new_v_norm_g', 'new_v_w_in', 'new_v_b_forget', 'new_v_pool_w', 'new_v_pool_scale', 'new_v_w_up_pool', 'new_v_w_up_attn', 'new_v_w_out', 'new_v_final_norm_g']
TWIN_LEAF_KINDS = {'loss': 'loss', 'grad_x': 'grad_x', 'grad_meta_tokens': 'grad_w', 'grad_norm_g': 'grad_w', 'grad_w_in': 'grad_w', 'grad_b_forget': 'grad_w', 'grad_pool_w': 'grad_w', 'grad_pool_scale': 'grad_w', 'grad_w_up_pool': 'grad_w', 'grad_w_up_attn': 'grad_w', 'grad_w_out': 'grad_w', 'grad_final_norm_g': 'grad_w', 'delta_meta_tokens': 'delta_w', 'delta_norm_g': 'delta_w', 'delta_w_in': 'delta_w', 'delta_b_forget': 'delta_w', 'delta_pool_w': 'delta_w', 'delta_pool_scale': 'delta_w', 'delta_w_up_pool': 'delta_w', 'delta_w_up_attn': 'delta_w', 'delta_w_out': 'delta_w', 'delta_final_norm_g': 'delta_w', 'new_m_meta_tokens': 'new_m', 'new_m_norm_g': 'new_m', 'new_m_w_in': 'new_m', 'new_m_b_forget': 'new_m', 'new_m_pool_w': 'new_m', 'new_m_pool_scale': 'new_m', 'new_m_w_up_pool': 'new_m', 'new_m_w_up_attn': 'new_m', 'new_m_w_out': 'new_m', 'new_m_final_norm_g': 'new_m', 'new_v_meta_tokens': 'new_v', 'new_v_norm_g': 'new_v', 'new_v_w_in': 'new_v', 'new_v_b_forget': 'new_v', 'new_v_pool_w': 'new_v', 'new_v_pool_scale': 'new_v', 'new_v_w_up_pool': 'new_v', 'new_v_w_up_attn': 'new_v', 'new_v_w_out': 'new_v', 'new_v_final_norm_g': 'new_v'}


def _forward(args):
    return _fwd_reference(*[args[k] for k in FWD_PARAMS])


def _output_shape():
    def fwd():
        inp = _fwd_setup_inputs(0)
        return _fwd_reference(*[inp[k] for k in FWD_PARAMS])
    out = _jax.eval_shape(fwd)
    return out.shape, out.dtype

N_MICROBATCH = 1
ADAM_LR = 0.001
ADAM_B1 = 0.9
ADAM_B2 = 0.999
ADAM_EPS = 1e-08
ADAM_WD = 0.01
ADAM_STEP = 10
PER_EXAMPLE_BATCH_AXIS = {'x': 0, 'loss_target': 0}
SHARED_INPUTS = []
_WEIGHT_DTYPES = {'meta_tokens': _jnp.float32, 'norm_g': _jnp.float32, 'w_in': _jnp.float32, 'b_forget': _jnp.float32, 'pool_w': _jnp.float32, 'pool_scale': _jnp.float32, 'w_up_pool': _jnp.float32, 'w_up_attn': _jnp.float32, 'w_out': _jnp.float32, 'final_norm_g': _jnp.float32}
MOMENT_SCALE = {'meta_tokens': 1.342616e-03, 'norm_g': 7.952257e-02, 'w_in': 3.493546e-02, 'b_forget': 7.059853e-02, 'pool_w': 6.843494e-02, 'pool_scale': 6.987245e-02, 'w_up_pool': 4.857428e-02, 'w_up_attn': 1.592652e-02, 'w_out': 5.105222e-02, 'final_norm_g': 3.201590e+01}


def _to_microbatches(a, axis):
    t = _jnp.moveaxis(a, axis, 0)
    t = t.reshape((N_MICROBATCH, t.shape[0] // N_MICROBATCH) + t.shape[1:])
    return _jnp.moveaxis(t, 1, axis + 1)


def setup_inputs(seed: int = 0) -> dict:
    inp = _fwd_setup_inputs(seed)
    key = _jax.random.fold_in(_jax.random.key(seed), 7919)
    shape, _ = _output_shape()
    out = dict(inp)
    out["loss_target"] = _jax.random.normal(_jax.random.fold_in(key, 0), shape, _jnp.float32)
    for i, name in enumerate(TWIN_WEIGHTS):
        w = inp[name].astype(_jnp.float32)
        if MOMENT_SCALE is None:
            s = _jnp.sqrt(_jnp.mean(_jnp.square(w)) + 1e-30)
        else:
            s = MOMENT_SCALE[name]
        km, kv = _jax.random.split(_jax.random.fold_in(key, i + 1))
        out[name] = w
        out["m_" + name] = s * _jax.random.normal(km, w.shape, _jnp.float32)
        out["v_" + name] = (s * s) * _jax.random.uniform(kv, w.shape, _jnp.float32, 0.5, 1.5)
    if N_MICROBATCH > 1:
        for name, axis in PER_EXAMPLE_BATCH_AXIS.items():
            out[name] = _to_microbatches(out[name], axis)
    return {'x': out['x'], 'meta_tokens': out['meta_tokens'], 'norm_g': out['norm_g'], 'w_in': out['w_in'], 'b_forget': out['b_forget'], 'pool_w': out['pool_w'], 'pool_scale': out['pool_scale'], 'w_up_pool': out['w_up_pool'], 'w_up_attn': out['w_up_attn'], 'w_out': out['w_out'], 'final_norm_g': out['final_norm_g'], 'loss_target': out['loss_target'], 'm_meta_tokens': out['m_meta_tokens'], 'm_norm_g': out['m_norm_g'], 'm_w_in': out['m_w_in'], 'm_b_forget': out['m_b_forget'], 'm_pool_w': out['m_pool_w'], 'm_pool_scale': out['m_pool_scale'], 'm_w_up_pool': out['m_w_up_pool'], 'm_w_up_attn': out['m_w_up_attn'], 'm_w_out': out['m_w_out'], 'm_final_norm_g': out['m_final_norm_g'], 'v_meta_tokens': out['v_meta_tokens'], 'v_norm_g': out['v_norm_g'], 'v_w_in': out['v_w_in'], 'v_b_forget': out['v_b_forget'], 'v_pool_w': out['v_pool_w'], 'v_pool_scale': out['v_pool_scale'], 'v_w_up_pool': out['v_w_up_pool'], 'v_w_up_attn': out['v_w_up_attn'], 'v_w_out': out['v_w_out'], 'v_final_norm_g': out['v_final_norm_g']}


def _loss(weights, diff, rest, loss_target):
    with _jax.named_scope("forward"):
        args = {**rest, TWIN_DIFF_INPUT: diff, **{k: w.astype(_WEIGHT_DTYPES[k]) for k, w in weights.items()}}
        y = _forward(args)
    with _jax.named_scope("loss_head"):
        err = _jnp.square(y.astype(_jnp.float32) - loss_target)
        return 0.5 * _jnp.sum(_jnp.mean(err, axis=-1)) if err.ndim else 0.5 * err


def _adamw(w, g, m, v):
    m = ADAM_B1 * m + (1.0 - ADAM_B1) * g
    v = ADAM_B2 * v + (1.0 - ADAM_B2) * _jnp.square(g)
    m_hat = m / (1.0 - ADAM_B1 ** ADAM_STEP)
    v_hat = v / (1.0 - ADAM_B2 ** ADAM_STEP)
    delta = -ADAM_LR * (m_hat / (_jnp.sqrt(v_hat) + ADAM_EPS) + ADAM_WD * w)
    return delta, m, v


def reference(x, meta_tokens, norm_g, w_in, b_forget, pool_w, pool_scale, w_up_pool, w_up_attn, w_out, final_norm_g, loss_target, m_meta_tokens, m_norm_g, m_w_in, m_b_forget, m_pool_w, m_pool_scale, m_w_up_pool, m_w_up_attn, m_w_out, m_final_norm_g, v_meta_tokens, v_norm_g, v_w_in, v_b_forget, v_pool_w, v_pool_scale, v_w_up_pool, v_w_up_attn, v_w_out, v_final_norm_g):
    given = dict(x=x, meta_tokens=meta_tokens, norm_g=norm_g, w_in=w_in, b_forget=b_forget, pool_w=pool_w, pool_scale=pool_scale, w_up_pool=w_up_pool, w_up_attn=w_up_attn, w_out=w_out, final_norm_g=final_norm_g, loss_target=loss_target, m_meta_tokens=m_meta_tokens, m_norm_g=m_norm_g, m_w_in=m_w_in, m_b_forget=m_b_forget, m_pool_w=m_pool_w, m_pool_scale=m_pool_scale, m_w_up_pool=m_w_up_pool, m_w_up_attn=m_w_up_attn, m_w_out=m_w_out, m_final_norm_g=m_final_norm_g, v_meta_tokens=v_meta_tokens, v_norm_g=v_norm_g, v_w_in=v_w_in, v_b_forget=v_b_forget, v_pool_w=v_pool_w, v_pool_scale=v_pool_scale, v_w_up_pool=v_w_up_pool, v_w_up_attn=v_w_up_attn, v_w_out=v_w_out, v_final_norm_g=v_final_norm_g)
    weights = {n: given[n] for n in TWIN_WEIGHTS}
    shared = {n: given[n] for n in SHARED_INPUTS}
    per_example = {n: given[n] for n in ['x']}
    grad_fn = _jax.value_and_grad(_loss, argnums=(0, 1))

    def one_microbatch(ex, loss_target):
        ex = dict(ex)
        diff = ex.pop(TWIN_DIFF_INPUT)
        return grad_fn(weights, diff, {**shared, **ex}, loss_target)

    if N_MICROBATCH == 1:
        loss, (grad_w, grad_x) = one_microbatch(per_example, given["loss_target"])
    else:
        def body(carry, xs):
            loss_sum, grad_sum = carry
            l_k, (gw_k, gx_k) = one_microbatch(xs[0], xs[1])
            with _jax.named_scope("update"):
                return (loss_sum + l_k, _jax.tree.map(_jnp.add, grad_sum, gw_k)), gx_k

        init = (_jnp.zeros((), _jnp.float32), _jax.tree.map(_jnp.zeros_like, weights))
        (loss, grad_w), grad_x = _jax.lax.scan(body, init, (per_example, given["loss_target"]))
    with _jax.named_scope("update"):
        delta_w, new_m, new_v = {}, {}, {}
        for n in TWIN_WEIGHTS:
            delta_w[n], new_m[n], new_v[n] = _adamw(weights[n], grad_w[n], given["m_" + n], given["v_" + n])
    return (loss, grad_x, *[grad_w[n] for n in TWIN_WEIGHTS], *[delta_w[n] for n in TWIN_WEIGHTS],
            *[new_m[n] for n in TWIN_WEIGHTS], *[new_v[n] for n in TWIN_WEIGHTS])
```

```python
import jax
import jax.numpy as jnp
from jax import lax
from jax.experimental import pallas as pl
from jax.experimental.pallas import tpu as pltpu

F32 = jnp.float32
BF16 = jnp.bfloat16
MESH = pl.DeviceIdType.MESH
ANY = pl.BlockSpec(memory_space=pl.ANY)
HIGHEST = lax.Precision.HIGHEST

D_MODEL = 1024
N_META = 16
POOL_WIDTH = 512
POOL_GROUP = 128
POOL_WINDOWS = (2, 4, 8, 16)
N_HEADS = 8
HEAD_DIM = 64
ATTN_WIDTH = 512
RMS_EPS = 1e-6
N_CHIPS = 4

ADAM_LR = 0.001
ADAM_B1 = 0.9
ADAM_B2 = 0.999
ADAM_EPS = 1e-08
ADAM_WD = 0.01
ADAM_STEP = 10

LANES = 128
ROW_TILE = 256
PAD_ROWS = ROW_TILE - N_META
NEG = -1e30

E_U, E_ZP, E_ZA, E_GP, E_GA, E_COLS = 0, 512, 1024, 1536, 2560, 3584
QKV_COLS = 3 * ATTN_WIDTH
MAIN_COLS = E_COLS + QKV_COLS
F_COLS = LANES
VMEM_LIMIT = 48 * 1024 * 1024


def _params(sem=None, vmem=VMEM_LIMIT):
    return pltpu.CompilerParams(dimension_semantics=sem, vmem_limit_bytes=vmem)


def _row_tile(n, target):
    best = 16
    for t in range(16, target + 1, 16):
        if n % t == 0:
            best = t
    return best


def _sigmoid(x):
    return 1.0 / (1.0 + jnp.exp(-x))


def _gather_shards(shards):
    n = len(shards)

    def body(*refs):
        ins, outs = refs[:n], refs[n:2 * n]
        send, recv, local = refs[2 * n:]
        x, y, c = lax.axis_index("x"), lax.axis_index("y"), lax.axis_index("c")
        me = 2 * x + y
        chips = [(1 - x, y), (x, 1 - y), (1 - x, 1 - y)]
        started = []
        for t in range(n):
            mine = pltpu.make_async_copy(ins[t], outs[t].at[me], local.at[t])
            mine.start()
            started.append(mine)
        remote = []
        for t in range(n):
            for k, (px, py) in enumerate(chips):
                cp = pltpu.make_async_remote_copy(
                    src_ref=ins[t], dst_ref=outs[t].at[me], send_sem=send.at[3 * t + k], recv_sem=recv.at[3 * t + k],
                    device_id=(px, py, c), device_id_type=MESH)
                cp.start()
                remote.append(cp)
        for t in range(n):
            for k, (px, py) in enumerate(chips):
                pltpu.make_async_remote_copy(
                    src_ref=ins[t], dst_ref=outs[t].at[2 * px + py], send_sem=send.at[3 * t + k],
                    recv_sem=recv.at[3 * t + k], device_id=(px, py, c), device_id_type=MESH).wait_recv()
        for cp in remote:
            cp.wait_send()
        for cp in started:
            cp.wait()

    return pl.pallas_call(
        body, name="gather_weight_shards",
        in_specs=[ANY] * n, out_specs=[ANY] * n,
        out_shape=[jax.ShapeDtypeStruct((N_CHIPS,) + s.shape, s.dtype) for s in shards],
        scratch_shapes=[pltpu.SemaphoreType.DMA((3 * n,)), pltpu.SemaphoreType.DMA((3 * n,)),
                        pltpu.SemaphoreType.DMA((n,))],
    )(*shards)


def _sibling_exchange(g, s):
    def body(g_ref, s_ref, rb_ref, sr_ref, send, recv):
        x, y, c = lax.axis_index("x"), lax.axis_index("y"), lax.axis_index("c")
        sib = (x, y, 1 - c)
        big = pltpu.make_async_remote_copy(src_ref=g_ref.at[1 - c], dst_ref=rb_ref, send_sem=send.at[0],
                                           recv_sem=recv.at[0], device_id=sib, device_id_type=MESH)
        small = pltpu.make_async_remote_copy(src_ref=s_ref, dst_ref=sr_ref, send_sem=send.at[1],
                                             recv_sem=recv.at[1], device_id=sib, device_id_type=MESH)
        big.start()
        small.start()
        big.wait()
        small.wait()

    return pl.pallas_call(
        body, name="grad_sibling_exchange",
        in_specs=[ANY, ANY], out_specs=[ANY, ANY],
        out_shape=[jax.ShapeDtypeStruct(g.shape[1:], g.dtype), jax.ShapeDtypeStruct(s.shape, s.dtype)],
        scratch_shapes=[pltpu.SemaphoreType.DMA((2,)), pltpu.SemaphoreType.DMA((2,))],
    )(g, s)


def _chip_scatter(h, s):
    def body(h_ref, s_ref, q_ref, sq_ref, send, recv, local):
        x, y, c = lax.axis_index("x"), lax.axis_index("y"), lax.axis_index("c")
        me = 2 * x + y
        chips = [(1 - x, y), (x, 1 - y), (1 - x, 1 - y)]
        own_h = pltpu.make_async_copy(h_ref.at[me], q_ref.at[me], local.at[0])
        own_s = pltpu.make_async_copy(s_ref, sq_ref.at[me], local.at[1])
        own_h.start()
        own_s.start()
        remote = []
        for k, (px, py) in enumerate(chips):
            cp = pltpu.make_async_remote_copy(src_ref=h_ref.at[2 * px + py], dst_ref=q_ref.at[me], send_sem=send.at[k],
                                              recv_sem=recv.at[k], device_id=(px, py, c), device_id_type=MESH)
            cs = pltpu.make_async_remote_copy(src_ref=s_ref, dst_ref=sq_ref.at[me], send_sem=send.at[3 + k],
                                              recv_sem=recv.at[3 + k], device_id=(px, py, c), device_id_type=MESH)
            cp.start()
            cs.start()
            remote += [cp, cs]
        for k, (px, py) in enumerate(chips):
            pltpu.make_async_remote_copy(src_ref=h_ref.at[me], dst_ref=q_ref.at[2 * px + py], send_sem=send.at[k],
                                         recv_sem=recv.at[k], device_id=(px, py, c), device_id_type=MESH).wait_recv()
            pltpu.make_async_remote_copy(src_ref=s_ref, dst_ref=sq_ref.at[2 * px + py], send_sem=send.at[3 + k],
                                         recv_sem=recv.at[3 + k], device_id=(px, py, c), device_id_type=MESH).wait_recv()
        for cp in remote:
            cp.wait_send()
        own_h.wait()
        own_s.wait()

    return pl.pallas_call(
        body, name="grad_chip_scatter",
        in_specs=[ANY, ANY], out_specs=[ANY, ANY],
        out_shape=[jax.ShapeDtypeStruct(h.shape, h.dtype), jax.ShapeDtypeStruct((N_CHIPS,) + s.shape, s.dtype)],
        scratch_shapes=[pltpu.SemaphoreType.DMA((6,)), pltpu.SemaphoreType.DMA((6,)), pltpu.SemaphoreType.DMA((2,))],
    )(h, s)


def _sibling_allgather(t):
    def body(t_ref, o_ref, send, recv, local):
        x, y, c = lax.axis_index("x"), lax.axis_index("y"), lax.axis_index("c")
        own = pltpu.make_async_copy(t_ref, o_ref.at[c], local.at[0])
        own.start()
        cp = pltpu.make_async_remote_copy(src_ref=t_ref, dst_ref=o_ref.at[c], send_sem=send.at[0], recv_sem=recv.at[0],
                                          device_id=(x, y, 1 - c), device_id_type=MESH)
        cp.start()
        pltpu.make_async_remote_copy(src_ref=t_ref, dst_ref=o_ref.at[1 - c], send_sem=send.at[0], recv_sem=recv.at[0],
                                     device_id=(x, y, 1 - c), device_id_type=MESH).wait_recv()
        cp.wait_send()
        own.wait()

    return pl.pallas_call(
        body, name="grad_sibling_allgather",
        in_specs=[ANY], out_specs=ANY,
        out_shape=jax.ShapeDtypeStruct((2,) + t.shape, t.dtype),
        scratch_shapes=[pltpu.SemaphoreType.DMA((1,)), pltpu.SemaphoreType.DMA((1,)), pltpu.SemaphoreType.DMA((1,))],
    )(t)


def _add_sibling_half(g, rb, s, sr, core):
    rows, srows = g.shape[2], s.shape[0]

    def body(core_ref, g_ref, rb_ref, s_ref, sr_ref, h_ref, sc_ref):
        h_ref[...] = g_ref[...] + rb_ref[...]
        sc_ref[...] = s_ref[...] + sr_ref[...]

    return pl.pallas_call(
        body, name="grad_add_sibling",
        grid_spec=pltpu.PrefetchScalarGridSpec(
            num_scalar_prefetch=1, grid=(N_CHIPS,),
            in_specs=[pl.BlockSpec((None, None, rows, LANES), lambda j, cr: (cr[0], j, 0, 0)),
                      pl.BlockSpec((None, rows, LANES), lambda j, cr: (j, 0, 0)),
                      pl.BlockSpec((srows, LANES), lambda j, cr: (0, 0)),
                      pl.BlockSpec((srows, LANES), lambda j, cr: (0, 0))],
            out_specs=[pl.BlockSpec((None, rows, LANES), lambda j, cr: (j, 0, 0)),
                       pl.BlockSpec((srows, LANES), lambda j, cr: (0, 0))]),
        out_shape=[jax.ShapeDtypeStruct(rb.shape, F32), jax.ShapeDtypeStruct(s.shape, F32)],
        compiler_params=_params(("arbitrary",)),
    )(core, g, rb, s, sr)


def _add_chips(q, sq):
    rows, srows = q.shape[1], sq.shape[1]

    def body(q_ref, sq_ref, t_ref, st_ref):
        t_ref[...] = ((q_ref[0] + q_ref[1]) + q_ref[2]) + q_ref[3]
        st_ref[...] = ((sq_ref[0] + sq_ref[1]) + sq_ref[2]) + sq_ref[3]

    return pl.pallas_call(
        body, name="grad_add_chips",
        out_shape=[jax.ShapeDtypeStruct((rows, LANES), F32), jax.ShapeDtypeStruct((srows, LANES), F32)],
        compiler_params=_params(),
    )(q, sq)


def _mm(a, b, out_dtype, name, tm, tn):
    m, k = a.shape
    n = b.shape[1]

    def body(a_ref, b_ref, o_ref):
        o_ref[...] = jnp.dot(a_ref[...], b_ref[...], preferred_element_type=F32).astype(out_dtype)

    return pl.pallas_call(
        body, name=name, grid=(n // tn, m // tm),
        in_specs=[pl.BlockSpec((tm, k), lambda j, i: (i, 0)), pl.BlockSpec((k, tn), lambda j, i: (0, j))],
        out_specs=pl.BlockSpec((tm, tn), lambda j, i: (i, j)),
        out_shape=jax.ShapeDtypeStruct((m, n), out_dtype),
        compiler_params=_params(("parallel", "parallel")),
    )(a, b)


def _mm_tn(a, b, name, tm, tn, tk):
    k, m = a.shape
    n = b.shape[1]

    def body(a_ref, b_ref, o_ref):
        @pl.when(pl.program_id(2) == 0)
        def _():
            o_ref[...] = jnp.zeros_like(o_ref)
        o_ref[...] += lax.dot_general(a_ref[...].astype(BF16), b_ref[...].astype(BF16), (((0,), (0,)), ((), ())),
                                      preferred_element_type=F32)

    return pl.pallas_call(
        body, name=name, grid=(m // tm, n // tn, k // tk),
        in_specs=[pl.BlockSpec((tk, tm), lambda i, j, kk: (kk, i)), pl.BlockSpec((tk, tn), lambda i, j, kk: (kk, j))],
        out_specs=pl.BlockSpec((tm, tn), lambda i, j, kk: (i, j)),
        out_shape=jax.ShapeDtypeStruct((m, n), F32),
        compiler_params=_params(("parallel", "parallel", "arbitrary")),
    )(a, b)


def _tokens_spec():
    return pl.BlockSpec((ROW_TILE, D_MODEL), lambda i: (jnp.maximum(i - 1, 0), 0))


def _rmsnorm_fwd(x2, metapad, g1):
    nb = x2.shape[0] // ROW_TILE + 1

    def body(x_ref, mp_ref, g_ref, hn_ref):
        h = jnp.where(pl.program_id(0) == 0, mp_ref[...], x_ref[...])
        r = lax.rsqrt(jnp.mean(h * h, axis=-1, keepdims=True) + RMS_EPS)
        hn_ref[...] = ((h * r) * g_ref[...]).astype(BF16)

    return pl.pallas_call(
        body, name="rmsnorm_fwd", grid=(nb,),
        in_specs=[_tokens_spec(), pl.BlockSpec((ROW_TILE, D_MODEL), lambda i: (0, 0)),
                  pl.BlockSpec((1, D_MODEL), lambda i: (0, 0))],
        out_specs=pl.BlockSpec((ROW_TILE, D_MODEL), lambda i: (i, 0)),
        out_shape=jax.ShapeDtypeStruct((nb * ROW_TILE, D_MODEL), BF16),
        compiler_params=_params(("parallel",)),
    )(x2, metapad, g1)


def _valid_gate_mask(i):
    row = i * ROW_TILE + lax.broadcasted_iota(jnp.int32, (ROW_TILE, F_COLS), 0)
    col = lax.broadcasted_iota(jnp.int32, (ROW_TILE, F_COLS), 1)
    return (row >= PAD_ROWS) & (col < N_HEADS)


def _gates_fwd(f, bfg):
    nb = f.shape[0] // ROW_TILE

    def body(f_ref, b_ref, c_ref, carry):
        i = pl.program_id(0)

        @pl.when(i == 0)
        def _():
            carry[...] = jnp.zeros_like(carry)

        logit = f_ref[...] + b_ref[...]
        lf = jnp.minimum(logit, 0.0) - jnp.log1p(jnp.exp(-jnp.abs(logit)))
        lf = jnp.where(_valid_gate_mask(i), lf, 0.0)
        r_i = lax.broadcasted_iota(jnp.int32, (ROW_TILE, ROW_TILE), 0)
        c_i = lax.broadcasted_iota(jnp.int32, (ROW_TILE, ROW_TILE), 1)
        tri = (c_i <= r_i).astype(F32)
        c_ref[...] = jnp.dot(tri, lf, precision=HIGHEST, preferred_element_type=F32) + carry[...]
        carry[...] = carry[...] + jnp.sum(lf, axis=0, keepdims=True)

    return pl.pallas_call(
        body, name="gates_fwd", grid=(nb,),
        in_specs=[pl.BlockSpec((ROW_TILE, F_COLS), lambda i: (i, 0)), pl.BlockSpec((1, F_COLS), lambda i: (0, 0))],
        out_specs=pl.BlockSpec((ROW_TILE, F_COLS), lambda i: (i, 0)),
        out_shape=jax.ShapeDtypeStruct(f.shape, F32),
        scratch_shapes=[pltpu.VMEM((1, F_COLS), F32)],
        compiler_params=_params(("arbitrary",)),
    )(f, bfg)


def _pool_counts(i):
    row = i * ROW_TILE + lax.broadcasted_iota(jnp.int32, (ROW_TILE, 1), 0)
    return jnp.maximum(row - PAD_ROWS, 0)


def _trailing_sums(xc, levels):
    n = xc.shape[0]
    acc = xc
    for lv in range(levels):
        acc = acc + pltpu.roll(acc, 1 << lv, 0)
    return acc


def _leading_sums(xc, levels):
    n = xc.shape[0]
    acc = xc
    for lv in range(levels):
        acc = acc + pltpu.roll(acc, n - (1 << lv), 0)
    return acc


def _pool_p(u_cur, u_prev, i):
    pos = _pool_counts(i)
    ps, invs = [], []
    for g, w in enumerate(POOL_WINDOWS):
        sl = slice(g * POOL_GROUP, (g + 1) * POOL_GROUP)
        cur = u_cur[:, sl]
        xc = jnp.concatenate([u_prev[:, sl], cur], axis=0)
        win = _trailing_sums(xc, g + 1)[ROW_TILE:, :]
        inv = 1.0 / jnp.minimum(pos + 1, w).astype(F32)
        ps.append(win * inv - cur)
        invs.append(inv)
    return ps, invs


def _pool_fwd(e, pw, scale):
    nb = e.shape[0] // ROW_TILE

    def body(uc_ref, up_ref, z_ref, pw_ref, sc_ref, y_ref):
        i = pl.program_id(0)
        u_cur = uc_ref[...]
        u_prev = jnp.where(i == 0, 0.0, up_ref[...])
        ps, _ = _pool_p(u_cur, u_prev, i)
        z = z_ref[...]
        gate = z * _sigmoid(z)
        for g in range(len(POOL_WINDOWS)):
            sl = slice(g * POOL_GROUP, (g + 1) * POOL_GROUP)
            yraw = jnp.dot(ps[g].astype(BF16), pw_ref[g], preferred_element_type=F32)
            y_ref[:, sl] = ((yraw * sc_ref[:, sl]) * gate[:, sl]).astype(BF16)

    blk = (ROW_TILE, POOL_WIDTH)
    return pl.pallas_call(
        body, name="pool_fwd", grid=(nb,),
        in_specs=[pl.BlockSpec(blk, lambda i: (i, 0)), pl.BlockSpec(blk, lambda i: (jnp.maximum(i - 1, 0), 0)),
                  pl.BlockSpec(blk, lambda i: (i, 1)),
                  pl.BlockSpec((len(POOL_WINDOWS), POOL_GROUP, POOL_GROUP), lambda i: (0, 0, 0)),
                  pl.BlockSpec((1, POOL_WIDTH), lambda i: (0, 0))],
        out_specs=pl.BlockSpec(blk, lambda i: (i, 0)),
        out_shape=jax.ShapeDtypeStruct((e.shape[0], POOL_WIDTH), BF16),
        compiler_params=_params(("parallel",)),
    )(e, e, e, pw, scale)


def _score_mask(i, j):
    qidx = i * ROW_TILE + lax.broadcasted_iota(jnp.int32, (ROW_TILE, ROW_TILE), 0)
    kidx = j * ROW_TILE + lax.broadcasted_iota(jnp.int32, (ROW_TILE, ROW_TILE), 1)
    return (kidx <= qidx) & ((kidx >= PAD_ROWS) | (kidx == qidx))


def _head_halves(a):
    first = lax.broadcasted_iota(jnp.int32, a.shape, 1) < HEAD_DIM
    zero = jnp.zeros_like(a)
    return jnp.where(first, a, zero), jnp.where(first, zero, a)


def _dot_nt(a, b):
    return lax.dot_general(a, b, (((1,), (1,)), ((), ())), preferred_element_type=F32)


def _dot_tn(a, b):
    return lax.dot_general(a, b, (((0,), (0,)), ((), ())), preferred_element_type=F32)


def _attn_fwd(qkv, bias):
    lp = qkv.shape[0]
    nb = lp // ROW_TILE
    n_pairs = N_HEADS // 2

    def body(q_ref, k_ref, v_ref, b_ref, o_ref, lse_ref):
        i = pl.program_id(1)
        qh = _head_halves(q_ref[...])

        def step(j, carry):
            rows = pl.ds(pl.multiple_of(j * ROW_TILE, ROW_TILE), ROW_TILE)
            kb, vb = k_ref[rows, :], v_ref[rows, :]
            valid = _score_mask(i, j)
            out = []
            for hd in range(2):
                m, l, acc = carry[hd]
                s = _dot_nt(qh[hd], kb) - b_ref[0, 0, j, hd:hd + 1, :]
                s = jnp.where(valid, s, NEG)
                m_new = jnp.maximum(m, jnp.max(s, axis=1, keepdims=True))
                alpha = jnp.exp(m - m_new)
                p = jnp.exp(s - m_new)
                l = alpha * l + jnp.sum(p, axis=1, keepdims=True)
                acc = alpha * acc + jnp.dot(p.astype(BF16), vb, preferred_element_type=F32)
                out.append((m_new, l, acc))
            return tuple(out)

        init = tuple((jnp.full((ROW_TILE, 1), NEG, F32), jnp.zeros((ROW_TILE, 1), F32),
                      jnp.zeros((ROW_TILE, LANES), F32)) for _ in range(2))
        (m0, l0, a0), (m1, l1, a1) = lax.fori_loop(0, i + 1, step, init)
        first = lax.broadcasted_iota(jnp.int32, (ROW_TILE, LANES), 1) < HEAD_DIM
        o_ref[...] = jnp.where(first, a0 / l0, a1 / l1)
        lse_ref[...] = jnp.where(first, m0 + jnp.log(l0), m1 + jnp.log(l1))

    return pl.pallas_call(
        body, name="attn_fwd", grid=(n_pairs, nb),
        in_specs=[pl.BlockSpec((ROW_TILE, LANES), lambda hp, i: (i, hp)),
                  pl.BlockSpec((lp, LANES), lambda hp, i: (0, n_pairs + hp)),
                  pl.BlockSpec((lp, LANES), lambda hp, i: (0, 2 * n_pairs + hp)),
                  pl.BlockSpec((1, 1, nb, 2, ROW_TILE), lambda hp, i: (i, hp, 0, 0, 0))],
        out_specs=[pl.BlockSpec((ROW_TILE, LANES), lambda hp, i: (i, hp)),
                   pl.BlockSpec((ROW_TILE, LANES), lambda hp, i: (i, hp))],
        out_shape=[jax.ShapeDtypeStruct((lp, ATTN_WIDTH), F32), jax.ShapeDtypeStruct((lp, ATTN_WIDTH), F32)],
        compiler_params=_params(("parallel", "parallel")),
    )(qkv, qkv, qkv, bias)


def _merge_fwd(y_pool, o, e, wup_p, wup_a):
    lp = o.shape[0]
    nb = lp // ROW_TILE

    def body(yp_ref, o_ref, e_ref, wp_ref, wa_ref, mg_ref, ya_ref):
        za = e_ref[:, E_ZA:E_GP]
        ya = (o_ref[...] * (za * _sigmoid(za))).astype(BF16)
        ya_ref[...] = ya
        a_pool = jnp.dot(yp_ref[...], wp_ref[...], preferred_element_type=F32)
        a_attn = jnp.dot(ya, wa_ref[...], preferred_element_type=F32)
        mg_ref[...] = (_sigmoid(e_ref[:, E_GP:E_GA]) * a_pool + _sigmoid(e_ref[:, E_GA:E_COLS]) * a_attn).astype(BF16)

    return pl.pallas_call(
        body, name="merge_fwd", grid=(nb,),
        in_specs=[pl.BlockSpec((ROW_TILE, POOL_WIDTH), lambda i: (i, 0)),
                  pl.BlockSpec((ROW_TILE, ATTN_WIDTH), lambda i: (i, 0)),
                  pl.BlockSpec((ROW_TILE, E_COLS), lambda i: (i, 0)),
                  pl.BlockSpec((POOL_WIDTH, D_MODEL), lambda i: (0, 0)),
                  pl.BlockSpec((ATTN_WIDTH, D_MODEL), lambda i: (0, 0))],
        out_specs=[pl.BlockSpec((ROW_TILE, D_MODEL), lambda i: (i, 0)),
                   pl.BlockSpec((ROW_TILE, ATTN_WIDTH), lambda i: (i, 0))],
        out_shape=[jax.ShapeDtypeStruct((lp, D_MODEL), BF16), jax.ShapeDtypeStruct((lp, ATTN_WIDTH), BF16)],
        compiler_params=_params(("parallel",)),
    )(y_pool, o, e, wup_p, wup_a)


def _head_fwd_bwd(merged, w_out, x2, metapad, gf, target):
    lp = merged.shape[0]
    nb = lp // ROW_TILE

    def body(mg_ref, w_ref, x_ref, mp_ref, g_ref, t_ref, dh_ref, loss_ref, dg_ref):
        i = pl.program_id(0)

        @pl.when(i == 0)
        def _():
            loss_ref[...] = jnp.zeros_like(loss_ref)
            dg_ref[...] = jnp.zeros_like(dg_ref)

        h0 = jnp.where(i == 0, mp_ref[...], x_ref[...])
        h1 = h0 + jnp.dot(mg_ref[...], w_ref[...], preferred_element_type=F32)
        r = lax.rsqrt(jnp.mean(h1 * h1, axis=-1, keepdims=True) + RMS_EPS)
        xhat = h1 * r
        g = g_ref[...]
        err = jnp.where(i == 0, 0.0, xhat * g - t_ref[...])
        loss_ref[...] += 0.5 * jnp.sum(jnp.mean(err * err, axis=-1, keepdims=True))
        dy = err / D_MODEL
        dg_ref[...] += jnp.sum(dy * xhat, axis=0, keepdims=True)
        dxhat = dy * g
        dh_ref[...] = r * (dxhat - xhat * jnp.mean(dxhat * xhat, axis=-1, keepdims=True))

    return pl.pallas_call(
        body, name="head_fwd_bwd", grid=(nb,),
        in_specs=[pl.BlockSpec((ROW_TILE, D_MODEL), lambda i: (i, 0)),
                  pl.BlockSpec((D_MODEL, D_MODEL), lambda i: (0, 0)),
                  _tokens_spec(), pl.BlockSpec((ROW_TILE, D_MODEL), lambda i: (0, 0)),
                  pl.BlockSpec((1, D_MODEL), lambda i: (0, 0)), _tokens_spec()],
        out_specs=[pl.BlockSpec((ROW_TILE, D_MODEL), lambda i: (i, 0)),
                   pl.BlockSpec((1, LANES), lambda i: (0, 0)), pl.BlockSpec((1, D_MODEL), lambda i: (0, 0))],
        out_shape=[jax.ShapeDtypeStruct((lp, D_MODEL), F32), jax.ShapeDtypeStruct((1, LANES), F32),
                   jax.ShapeDtypeStruct((1, D_MODEL), F32)],
        compiler_params=_params(("arbitrary",)),
    )(merged, w_out, x2, metapad, gf, target)


def _per_head_rowsum(t):
    head = lax.broadcasted_iota(jnp.int32, t.shape, 1) // HEAD_DIM
    out = jnp.zeros_like(t)
    for h in range(N_HEADS):
        sel = head == h
        out = jnp.where(sel, jnp.sum(jnp.where(sel, t, 0.0), axis=1, keepdims=True), out)
    return out


def _merge_bwd(dh1, w_out, y_pool, y_attn, wup_p, wup_a, e, o):
    lp = o.shape[0]
    nb = lp // ROW_TILE

    def body(dh_ref, wo_ref, yp_ref, ya_ref, wp_ref, wa_ref, e_ref, o_ref,
             dap_ref, daa_ref, dg_ref, dza_ref, do_ref, delta_ref, dyp_ref):
        dmerged = _dot_nt(dh_ref[...].astype(BF16), wo_ref[...])
        a_pool = jnp.dot(yp_ref[...], wp_ref[...], preferred_element_type=F32)
        a_attn = jnp.dot(ya_ref[...], wa_ref[...], preferred_element_type=F32)
        sp = _sigmoid(e_ref[:, E_GP:E_GA])
        sa = _sigmoid(e_ref[:, E_GA:E_COLS])
        dap = (dmerged * sp).astype(BF16)
        daa = (dmerged * sa).astype(BF16)
        dap_ref[...] = dap
        daa_ref[...] = daa
        dg_ref[:, :D_MODEL] = (dmerged * a_pool * (sp * (1.0 - sp))).astype(BF16)
        dg_ref[:, D_MODEL:] = (dmerged * a_attn * (sa * (1.0 - sa))).astype(BF16)
        dyp_ref[...] = _dot_nt(dap, wp_ref[...])
        dya = _dot_nt(daa, wa_ref[...])
        za = e_ref[:, E_ZA:E_GP]
        sz = _sigmoid(za)
        o = o_ref[...]
        do = dya * (za * sz)
        do_ref[...] = do.astype(BF16)
        dza_ref[...] = (dya * o * (sz * (1.0 + za * (1.0 - sz)))).astype(BF16)
        delta_ref[...] = _per_head_rowsum(do * o)

    row = lambda w: pl.BlockSpec((ROW_TILE, w), lambda i: (i, 0))
    full = lambda a: pl.BlockSpec(a.shape, lambda i: (0, 0))
    return pl.pallas_call(
        body, name="merge_bwd", grid=(nb,),
        in_specs=[row(D_MODEL), full(w_out), row(POOL_WIDTH), row(ATTN_WIDTH), full(wup_p), full(wup_a),
                  row(E_COLS), row(ATTN_WIDTH)],
        out_specs=[row(D_MODEL), row(D_MODEL), row(2 * D_MODEL), row(ATTN_WIDTH), row(ATTN_WIDTH),
                   row(ATTN_WIDTH), row(POOL_WIDTH)],
        out_shape=[jax.ShapeDtypeStruct((lp, D_MODEL), BF16), jax.ShapeDtypeStruct((lp, D_MODEL), BF16),
                   jax.ShapeDtypeStruct((lp, 2 * D_MODEL), BF16), jax.ShapeDtypeStruct((lp, ATTN_WIDTH), BF16),
                   jax.ShapeDtypeStruct((lp, ATTN_WIDTH), BF16), jax.ShapeDtypeStruct((lp, ATTN_WIDTH), F32),
                   jax.ShapeDtypeStruct((lp, POOL_WIDTH), F32)],
        compiler_params=_params(("parallel",)),
    )(dh1, w_out, y_pool, y_attn, wup_p, wup_a, e, o)


def _pool_bwd_local(e, dy_pool, pw, scale):
    lp = e.shape[0]
    nb = lp // ROW_TILE
    ng = len(POOL_WINDOWS)

    def body(uc_ref, up_ref, z_ref, dy_ref, pw_ref, sc_ref, dpc_ref, dz_ref, dsc_ref, dpw_ref):
        i = pl.program_id(0)

        @pl.when(i == 0)
        def _():
            dsc_ref[...] = jnp.zeros_like(dsc_ref)
            dpw_ref[...] = jnp.zeros_like(dpw_ref)

        u_cur = uc_ref[...]
        u_prev = jnp.where(i == 0, 0.0, up_ref[...])
        ps, invs = _pool_p(u_cur, u_prev, i)
        z = z_ref[...]
        sz = _sigmoid(z)
        dy = dy_ref[...]
        dypre = dy * (z * sz)
        dsilu = sz * (1.0 + z * (1.0 - sz))
        for g in range(ng):
            sl = slice(g * POOL_GROUP, (g + 1) * POOL_GROUP)
            pb = ps[g].astype(BF16)
            w = pw_ref[g]
            yraw = jnp.dot(pb, w, preferred_element_type=F32)
            sc = sc_ref[:, sl]
            dz_ref[:, sl] = (dy[:, sl] * (yraw * sc) * dsilu[:, sl]).astype(BF16)
            dsc_ref[:, sl] += jnp.sum(dypre[:, sl] * yraw, axis=0, keepdims=True)
            dyraw = (dypre[:, sl] * sc).astype(BF16)
            dpw_ref[g] += _dot_tn(pb, dyraw)
            dpc_ref[:, sl] = _dot_nt(dyraw, w) * invs[g]

    blk = (ROW_TILE, POOL_WIDTH)
    return pl.pallas_call(
        body, name="pool_bwd_local", grid=(nb,),
        in_specs=[pl.BlockSpec(blk, lambda i: (i, 0)), pl.BlockSpec(blk, lambda i: (jnp.maximum(i - 1, 0), 0)),
                  pl.BlockSpec(blk, lambda i: (i, 1)), pl.BlockSpec(blk, lambda i: (i, 0)),
                  pl.BlockSpec((ng, POOL_GROUP, POOL_GROUP), lambda i: (0, 0, 0)),
                  pl.BlockSpec((1, POOL_WIDTH), lambda i: (0, 0))],
        out_specs=[pl.BlockSpec(blk, lambda i: (i, 0)), pl.BlockSpec(blk, lambda i: (i, 0)),
                   pl.BlockSpec((1, POOL_WIDTH), lambda i: (0, 0)),
                   pl.BlockSpec((ng, POOL_GROUP, POOL_GROUP), lambda i: (0, 0, 0))],
        out_shape=[jax.ShapeDtypeStruct((lp, POOL_WIDTH), F32), jax.ShapeDtypeStruct((lp, POOL_WIDTH), BF16),
                   jax.ShapeDtypeStruct((1, POOL_WIDTH), F32),
                   jax.ShapeDtypeStruct((ng, POOL_GROUP, POOL_GROUP), F32)],
        compiler_params=_params(("arbitrary",)),
    )(e, e, e, dy_pool, pw, scale)


def _pool_bwd_window(dpc):
    lp = dpc.shape[0]
    nb = lp // ROW_TILE

    def body(cur_ref, nxt_ref, du_ref):
        i = pl.program_id(0)
        cur = cur_ref[...]
        nxt = jnp.where(i == nb - 1, 0.0, nxt_ref[...])
        pos = _pool_counts(i)
        for g, w in enumerate(POOL_WINDOWS):
            sl = slice(g * POOL_GROUP, (g + 1) * POOL_GROUP)
            xc = jnp.concatenate([cur[:, sl], nxt[:, sl]], axis=0)
            win = _leading_sums(xc, g + 1)[:ROW_TILE, :]
            dp = cur[:, sl] * jnp.minimum(pos + 1, w).astype(F32)
            du_ref[:, sl] = (win - dp).astype(BF16)

    blk = (ROW_TILE, POOL_WIDTH)
    return pl.pallas_call(
        body, name="pool_bwd_window", grid=(nb,),
        in_specs=[pl.BlockSpec(blk, lambda i: (i, 0)), pl.BlockSpec(blk, lambda i: (jnp.minimum(i + 1, nb - 1), 0))],
        out_specs=pl.BlockSpec(blk, lambda i: (i, 0)),
        out_shape=jax.ShapeDtypeStruct((lp, POOL_WIDTH), BF16),
        compiler_params=_params(("parallel",)),
    )(dpc, dpc)


def _attn_bwd(qkv, do, lse, delta, bias):
    lp = qkv.shape[0]
    nb = lp // ROW_TILE
    n_pairs = N_HEADS // 2

    def body(q_ref, k_ref, v_ref, do_ref, lse_ref, dl_ref, b_ref, dq_ref, dk_ref, dv_ref, dc_ref, dcq_ref):
        j = pl.program_id(1)

        @pl.when(j == 0)
        def _():
            dq_ref[...] = jnp.zeros_like(dq_ref)
            dcq_ref[...] = jnp.zeros_like(dcq_ref)

        first = lax.broadcasted_iota(jnp.int32, (ROW_TILE, LANES), 1) < HEAD_DIM

        kb, vb = k_ref[...], v_ref[...]
        kh = _head_halves(kb)

        def step(i, carry):
            rows = pl.ds(pl.multiple_of(i * ROW_TILE, ROW_TILE), ROW_TILE)
            qh = _head_halves(q_ref[rows, :])
            doh = _head_halves(do_ref[rows, :])
            lse_i, dl_i = lse_ref[rows, :], dl_ref[rows, :]
            valid = _score_mask(i, j)
            dq_i = jnp.zeros((ROW_TILE, LANES), F32)
            out, rowsums = [], []
            for hd in range(2):
                dk_acc, dv_acc, dc_acc = carry[hd]
                col = slice(hd * HEAD_DIM, hd * HEAD_DIM + 1)
                s = _dot_nt(qh[hd], kb) - b_ref[i, 0, 0, hd:hd + 1, :]
                s = jnp.where(valid, s, NEG)
                p = jnp.exp(s - lse_i[:, col])
                dp = _dot_nt(doh[hd], vb)
                ds = p * (dp - dl_i[:, col])
                dsb = ds.astype(BF16)
                dv_acc = dv_acc + _dot_tn(p.astype(BF16), doh[hd])
                dk_acc = dk_acc + _dot_tn(dsb, qh[hd])
                dc_acc = dc_acc - jnp.sum(ds, axis=0, keepdims=True)
                rowsums.append(jnp.sum(ds, axis=1, keepdims=True))
                dq_i = dq_i + jnp.dot(dsb, kh[hd], preferred_element_type=F32)
                out.append((dk_acc, dv_acc, dc_acc))
            dq_ref[rows, :] += dq_i
            dcq_ref[rows, :] += jnp.where(first, rowsums[0], rowsums[1])
            return tuple(out)

        init = tuple((jnp.zeros((ROW_TILE, LANES), F32), jnp.zeros((ROW_TILE, LANES), F32),
                      jnp.zeros((1, ROW_TILE), F32)) for _ in range(2))
        (dk0, dv0, dc0), (dk1, dv1, dc1) = lax.fori_loop(j, nb, step, init)
        dk_ref[...] = (dk0 + dk1).astype(BF16)
        dv_ref[...] = (dv0 + dv1).astype(BF16)
        dc_ref[0, 0, 0:1, :] = dc0
        dc_ref[0, 0, 1:2, :] = dc1

    whole = lambda col: pl.BlockSpec((lp, LANES), col)
    return pl.pallas_call(
        body, name="attn_bwd", grid=(n_pairs, nb),
        in_specs=[whole(lambda hp, j: (0, hp)),
                  pl.BlockSpec((ROW_TILE, LANES), lambda hp, j: (j, n_pairs + hp)),
                  pl.BlockSpec((ROW_TILE, LANES), lambda hp, j: (j, 2 * n_pairs + hp)),
                  whole(lambda hp, j: (0, hp)), whole(lambda hp, j: (0, hp)), whole(lambda hp, j: (0, hp)),
                  pl.BlockSpec((nb, 1, 1, 2, ROW_TILE), lambda hp, j: (0, hp, j, 0, 0))],
        out_specs=[whole(lambda hp, j: (0, hp)),
                   pl.BlockSpec((ROW_TILE, LANES), lambda hp, j: (j, hp)),
                   pl.BlockSpec((ROW_TILE, LANES), lambda hp, j: (j, hp)),
                   pl.BlockSpec((1, 1, 2, ROW_TILE), lambda hp, j: (hp, j, 0, 0)),
                   whole(lambda hp, j: (0, hp))],
        out_shape=[jax.ShapeDtypeStruct((lp, ATTN_WIDTH), F32), jax.ShapeDtypeStruct((lp, ATTN_WIDTH), BF16),
                   jax.ShapeDtypeStruct((lp, ATTN_WIDTH), BF16),
                   jax.ShapeDtypeStruct((n_pairs, nb, 2, ROW_TILE), F32),
                   jax.ShapeDtypeStruct((lp, ATTN_WIDTH), F32)],
        compiler_params=_params(("parallel", "arbitrary")),
    )(qkv, qkv, qkv, do, lse, delta, bias)


def _gates_bwd(dc, f, bfg):
    nb = f.shape[0] // ROW_TILE

    def body(dc_ref, f_ref, b_ref, df_ref, db_ref, carry):
        step = pl.program_id(0)
        i = nb - 1 - step

        @pl.when(step == 0)
        def _():
            carry[...] = jnp.zeros_like(carry)
            db_ref[...] = jnp.zeros_like(db_ref)

        dcb = dc_ref[...]
        r_i = lax.broadcasted_iota(jnp.int32, (ROW_TILE, ROW_TILE), 0)
        c_i = lax.broadcasted_iota(jnp.int32, (ROW_TILE, ROW_TILE), 1)
        upper = (c_i >= r_i).astype(F32)
        dlf = jnp.dot(upper, dcb, precision=HIGHEST, preferred_element_type=F32) + carry[...]
        carry[...] = carry[...] + jnp.sum(dcb, axis=0, keepdims=True)
        logit = f_ref[...] + b_ref[...]
        dlogit = jnp.where(_valid_gate_mask(i), dlf * _sigmoid(-logit), 0.0)
        df_ref[...] = dlogit.astype(BF16)
        db_ref[...] += jnp.sum(dlogit, axis=0, keepdims=True)

    blk = pl.BlockSpec((ROW_TILE, F_COLS), lambda s: (nb - 1 - s, 0))
    one = pl.BlockSpec((1, F_COLS), lambda s: (0, 0))
    return pl.pallas_call(
        body, name="gates_bwd", grid=(nb,),
        in_specs=[blk, blk, one], out_specs=[blk, one],
        out_shape=[jax.ShapeDtypeStruct(f.shape, BF16), jax.ShapeDtypeStruct((1, F_COLS), F32)],
        scratch_shapes=[pltpu.VMEM((1, F_COLS), F32)],
        compiler_params=_params(("arbitrary",)),
    )(dc, f, bfg)


def _input_bwd(dproj, df, w_main, w_f, x2, metapad, dh1, g1):
    lp = dproj.shape[0]
    nb = lp // ROW_TILE

    def body(dp_ref, df_ref, w_ref, wf_ref, x_ref, mp_ref, dh_ref, g_ref, gx_ref, g0_ref, dg_ref):
        i = pl.program_id(0)

        @pl.when(i == 0)
        def _():
            dg_ref[...] = jnp.zeros_like(dg_ref)

        dhn = _dot_nt(dp_ref[...], w_ref[...]) + _dot_nt(df_ref[...], wf_ref[...])
        h0 = jnp.where(i == 0, mp_ref[...], x_ref[...])
        r = lax.rsqrt(jnp.mean(h0 * h0, axis=-1, keepdims=True) + RMS_EPS)
        xhat = h0 * r
        dg_ref[...] += jnp.sum(dhn * xhat, axis=0, keepdims=True)
        dxhat = dhn * g_ref[...]
        dh0 = dh_ref[...] + r * (dxhat - xhat * jnp.mean(dxhat * xhat, axis=-1, keepdims=True))
        gx_ref[...] = dh0

        @pl.when(i == 0)
        def _():
            g0_ref[...] = dh0

    return pl.pallas_call(
        body, name="input_bwd", grid=(nb,),
        in_specs=[pl.BlockSpec((ROW_TILE, MAIN_COLS), lambda i: (i, 0)), pl.BlockSpec((ROW_TILE, F_COLS), lambda i: (i, 0)),
                  pl.BlockSpec((D_MODEL, MAIN_COLS), lambda i: (0, 0)), pl.BlockSpec((D_MODEL, F_COLS), lambda i: (0, 0)),
                  _tokens_spec(), pl.BlockSpec((ROW_TILE, D_MODEL), lambda i: (0, 0)),
                  pl.BlockSpec((ROW_TILE, D_MODEL), lambda i: (i, 0)), pl.BlockSpec((1, D_MODEL), lambda i: (0, 0))],
        out_specs=[_tokens_spec(), pl.BlockSpec((ROW_TILE, D_MODEL), lambda i: (0, 0)),
                   pl.BlockSpec((1, D_MODEL), lambda i: (0, 0))],
        out_shape=[jax.ShapeDtypeStruct(x2.shape, F32), jax.ShapeDtypeStruct((ROW_TILE, D_MODEL), F32),
                   jax.ShapeDtypeStruct((1, D_MODEL), F32)],
        compiler_params=_params(("arbitrary",), vmem=56 * 1024 * 1024),
    )(dproj, df, w_main, w_f, x2, metapad, dh1, g1)


def _adamw(w, g, m, v, name):
    rows, cols = w.shape
    tr = rows if rows % 8 else _row_tile8(rows)

    def body(w_ref, g_ref, m_ref, v_ref, d_ref, mo_ref, vo_ref):
        g_ = g_ref[...]
        m_new = ADAM_B1 * m_ref[...] + (1.0 - ADAM_B1) * g_
        v_new = ADAM_B2 * v_ref[...] + (1.0 - ADAM_B2) * (g_ * g_)
        m_hat = m_new / (1.0 - ADAM_B1 ** ADAM_STEP)
        v_hat = v_new / (1.0 - ADAM_B2 ** ADAM_STEP)
        d_ref[...] = -ADAM_LR * (m_hat / (jnp.sqrt(v_hat) + ADAM_EPS) + ADAM_WD * w_ref[...])
        mo_ref[...] = m_new
        vo_ref[...] = v_new

    blk = pl.BlockSpec((tr, cols), lambda i: (i, 0))
    return pl.pallas_call(
        body, name=name, grid=(rows // tr,),
        in_specs=[blk] * 4, out_specs=[blk] * 3,
        out_shape=[jax.ShapeDtypeStruct(w.shape, F32)] * 3,
        compiler_params=_params(("parallel",)),
    )(w, g, m, v)


def _row_tile8(rows):
    best = rows
    for t in range(8, 257, 8):
        if rows % t == 0:
            best = t
    return best


def kernel(x, meta_tokens, norm_g, w_in, b_forget, pool_w, pool_scale, w_up_pool, w_up_attn, w_out, final_norm_g, loss_target, m_meta_tokens, m_norm_g, m_w_in, m_b_forget, m_pool_w, m_pool_scale, m_w_up_pool, m_w_up_attn, m_w_out, m_final_norm_g, v_meta_tokens, v_norm_g, v_w_in, v_b_forget, v_pool_w, v_pool_scale, v_w_up_pool, v_w_up_attn, v_w_out, v_final_norm_g):
    seq = x.shape[1]
    assert seq % ROW_TILE == 0 and x.shape[0] == 1
    lp = seq + ROW_TILE
    nb = lp // ROW_TILE
    core = jnp.reshape(lax.axis_index("c"), (1,)).astype(jnp.int32)
    x2 = x[0]
    target = loss_target[0]

    wg_in, wg_up_p, wg_up_a, wg_out, meta_g = _gather_shards(
        [w_in[0].astype(BF16), w_up_pool[0].astype(BF16), w_up_attn[0].astype(BF16), w_out[0].astype(BF16), meta_tokens])
    n_in = wg_in.shape[2] * N_CHIPS
    w_ref = jnp.transpose(wg_in, (1, 0, 2)).reshape(D_MODEL, n_in)
    cols = lambda a, b: w_ref[:, a:b]
    w_main = jnp.concatenate([cols(0, 512), cols(512, 1024), cols(2560, 3072), cols(3080, 4104), cols(4104, 5128),
                              cols(1024, 1536) * (HEAD_DIM ** -0.5), cols(1536, 2048), cols(2048, 2560)], axis=1)
    w_f = jnp.pad(cols(3072, 3080), ((0, 0), (0, F_COLS - N_HEADS)))
    wup_p = jnp.transpose(wg_up_p, (1, 0, 2)).reshape(POOL_WIDTH, D_MODEL)
    wup_a = jnp.transpose(wg_up_a, (1, 0, 2)).reshape(ATTN_WIDTH, D_MODEL)
    wout = wg_out.reshape(D_MODEL, D_MODEL)
    meta_full = jnp.transpose(meta_g, (1, 0, 2)).reshape(N_META, D_MODEL)
    metapad = jnp.pad(meta_full, ((PAD_ROWS, 0), (0, 0)))
    bfg = jnp.pad(b_forget, ((0, 0), (0, F_COLS - N_HEADS)))
    pw_b = pool_w[0].astype(BF16)
    gf = final_norm_g.reshape(1, D_MODEL)

    tm = _row_tile(lp, 1100)
    hn = _rmsnorm_fwd(x2, metapad, norm_g)
    e = _mm(hn, w_main[:, :E_COLS], F32, "in_proj_gates", tm, 512)
    qkv = _mm(hn, w_main[:, E_COLS:], BF16, "in_proj_qkv", tm, 512)
    f = _mm(hn, w_f, F32, "in_proj_forget", tm, F_COLS)
    c = _gates_fwd(f, bfg)
    c_t = jnp.transpose(c[:, :N_HEADS])
    bias = c_t[None, :, :] - jnp.transpose(c_t[:, ::ROW_TILE])[:, :, None]
    bias = jnp.transpose(bias.reshape(nb, N_HEADS // 2, 2, nb, ROW_TILE), (0, 1, 3, 2, 4))
    y_pool = _pool_fwd(e, pw_b, pool_scale)
    o, lse = _attn_fwd(qkv, bias)
    merged, y_attn = _merge_fwd(y_pool, o, e, wup_p, wup_a)
    dh1, loss_part, dgf = _head_fwd_bwd(merged, wout, x2, metapad, gf, target)
    loss = lax.psum(loss_part[0, 0], ("x", "y", "c"))

    dap, daa, dgate, dza, do, delta, dy_pool = _merge_bwd(dh1, wout, y_pool, y_attn, wup_p, wup_a, e, o)
    dpc, dzp, dscale, dpw = _pool_bwd_local(e, dy_pool, pw_b, pool_scale)
    du = _pool_bwd_window(dpc)
    dq, dk, dv, dc4, dcq = _attn_bwd(qkv, do, lse, delta, bias)
    dc = jnp.transpose(dc4, (1, 3, 0, 2)).reshape(lp, N_HEADS) + dcq[:, ::HEAD_DIM]
    df, db = _gates_bwd(jnp.pad(dc, ((0, 0), (0, F_COLS - N_HEADS))), f, bfg)
    dproj = jnp.concatenate([du, dzp, dza, dgate, dq.astype(BF16), dk, dv], axis=1)
    tk = _row_tile(lp, 1100)
    dw_main = _mm_tn(hn, dproj, "grad_w_in_main", 512, 1024, tk)
    dw_f = _mm_tn(hn, df, "grad_w_in_forget", 512, F_COLS, tk)
    dw_out = _mm_tn(merged, dh1, "grad_w_out", 512, 1024, tk)
    dw_up_p = _mm_tn(y_pool, dap, "grad_w_up_pool", 512, 1024, tk)
    dw_up_a = _mm_tn(y_attn, daa, "grad_w_up_attn", 512, 1024, tk)
    grad_x, g_block0, dg1 = _input_bwd(dproj, df, w_main, w_f, x2, metapad, dh1, norm_g)
    dmeta = g_block0[PAD_ROWS:]

    m_ = lambda a, b: dw_main[:, a:b]
    dw_in = jnp.concatenate([m_(0, 512), m_(512, 1024), m_(3584, 4096) * (HEAD_DIM ** -0.5), m_(4096, 4608), m_(4608, 5120),
                             m_(1024, 1536), dw_f[:, :N_HEADS], m_(1536, 2560), m_(2560, 3584)], axis=1)
    sh_in = n_in // N_CHIPS
    sh_d = D_MODEL // N_CHIPS

    def split(a, rows, ncols):
        return jnp.transpose(a.reshape(2, rows // 2, N_CHIPS, ncols), (0, 2, 1, 3)).reshape(2, N_CHIPS, -1, LANES)

    parts = [split(dw_in, D_MODEL, sh_in), split(dw_up_p, POOL_WIDTH, sh_d), split(dw_up_a, ATTN_WIDTH, sh_d),
             jnp.transpose(dw_out.reshape(N_CHIPS, 2, sh_d // 2, D_MODEL), (1, 0, 2, 3)).reshape(2, N_CHIPS, -1, LANES),
             split(dmeta, N_META, sh_d)]
    sizes = [p.shape[2] for p in parts]
    g_all = jnp.concatenate(parts, axis=2)
    small_parts = [dg1.reshape(-1, LANES), dgf.reshape(-1, LANES), dscale.reshape(-1, LANES), db,
                   dpw.reshape(-1, LANES)]
    small_sizes = [p.shape[0] for p in small_parts]
    pad_small = (-sum(small_sizes)) % 8
    small = jnp.concatenate(small_parts + [jnp.zeros((pad_small, LANES), F32)], axis=0)

    rb, sr = _sibling_exchange(g_all, small)
    h, sc = _add_sibling_half(g_all, rb, small, sr, core)
    q, sq = _chip_scatter(h, sc)
    t, st = _add_chips(q, sq)
    red = _sibling_allgather(t)

    offs = [0]
    for s_ in sizes:
        offs.append(offs[-1] + s_)
    piece = lambda k: red[:, offs[k]:offs[k + 1]]
    g_w_in = piece(0).reshape(D_MODEL, sh_in)
    g_w_up_p = piece(1).reshape(POOL_WIDTH, sh_d)
    g_w_up_a = piece(2).reshape(ATTN_WIDTH, sh_d)
    g_w_out = piece(3).reshape(sh_d, D_MODEL)
    g_meta = piece(4).reshape(N_META, sh_d)
    soffs = [0]
    for s_ in small_sizes:
        soffs.append(soffs[-1] + s_)
    spiece = lambda k: st[soffs[k]:soffs[k + 1]]
    g_norm = spiece(0).reshape(1, D_MODEL)
    g_final = spiece(1).reshape(1, D_MODEL)
    g_scale = spiece(2).reshape(1, POOL_WIDTH)
    g_bf = spiece(3)
    g_pw = spiece(4)

    def pad_lanes(a):
        return jnp.pad(a, ((0, 0), (0, F_COLS - N_HEADS)))

    upd = [
        ("meta_tokens", meta_tokens, g_meta, m_meta_tokens, v_meta_tokens),
        ("norm_g", norm_g, g_norm, m_norm_g, v_norm_g),
        ("w_in", w_in[0], g_w_in, m_w_in[0], v_w_in[0]),
        ("b_forget", pad_lanes(b_forget), g_bf, pad_lanes(m_b_forget), pad_lanes(v_b_forget)),
        ("pool_w", pool_w.reshape(-1, LANES), g_pw, m_pool_w.reshape(-1, LANES), v_pool_w.reshape(-1, LANES)),
        ("pool_scale", pool_scale, g_scale, m_pool_scale, v_pool_scale),
        ("w_up_pool", w_up_pool[0], g_w_up_p, m_w_up_pool[0], v_w_up_pool[0]),
        ("w_up_attn", w_up_attn[0], g_w_up_a, m_w_up_attn[0], v_w_up_attn[0]),
        ("w_out", w_out[0], g_w_out, m_w_out[0], v_w_out[0]),
        ("final_norm_g", gf, g_final, m_final_norm_g.reshape(1, D_MODEL), v_final_norm_g.reshape(1, D_MODEL)),
    ]
    shapes = [meta_tokens.shape, norm_g.shape, w_in.shape, b_forget.shape, pool_w.shape, pool_scale.shape,
              w_up_pool.shape, w_up_attn.shape, w_out.shape, final_norm_g.shape]
    grads, deltas, new_ms, new_vs = [], [], [], []
    for (name, w_, g_, m_in, v_in), shp in zip(upd, shapes):
        d_, mn_, vn_ = _adamw(w_, g_, m_in, v_in, "adamw_" + name)
        if name == "b_forget":
            g_, d_, mn_, vn_ = (a[:, :N_HEADS] for a in (g_, d_, mn_, vn_))
        grads.append(g_.reshape(shp))
        deltas.append(d_.reshape(shp))
        new_ms.append(mn_.reshape(shp))
        new_vs.append(vn_.reshape(shp))

    return (loss, grad_x.reshape(x.shape), *grads, *deltas, *new_ms, *new_vs)
```

```python
import jax
import jax.numpy as jnp
from jax import lax
from jax.experimental import pallas as pl
from jax.experimental.pallas import tpu as pltpu

F32 = jnp.float32
BF16 = jnp.bfloat16
MESH = pl.DeviceIdType.MESH
ANY = pl.BlockSpec(memory_space=pl.ANY)
HIGHEST = lax.Precision.HIGHEST

D_MODEL = 1024
N_META = 16
POOL_WIDTH = 512
POOL_GROUP = 128
POOL_WINDOWS = (2, 4, 8, 16)
N_HEADS = 8
HEAD_DIM = 64
ATTN_WIDTH = 512
RMS_EPS = 1e-6
N_CHIPS = 4

ADAM_LR = 0.001
ADAM_B1 = 0.9
ADAM_B2 = 0.999
ADAM_EPS = 1e-08
ADAM_WD = 0.01
ADAM_STEP = 10

LANES = 128
ROW_TILE = 256
KV_TILE = 512
PAD_ROWS = ROW_TILE - N_META
NEG = -1e30

E_U, E_ZP, E_ZA, E_GP, E_GA, E_COLS = 0, 512, 1024, 1536, 2560, 3584
QKV_COLS = 3 * ATTN_WIDTH
MAIN_COLS = E_COLS + QKV_COLS
F_COLS = LANES
VMEM_LIMIT = 48 * 1024 * 1024


def _params(sem=None, vmem=VMEM_LIMIT):
    return pltpu.CompilerParams(dimension_semantics=sem, vmem_limit_bytes=vmem)


def _row_tile(n, target):
    best = 16
    for t in range(16, target + 1, 16):
        if n % t == 0:
            best = t
    return best


def _sigmoid(x):
    return 1.0 / (1.0 + jnp.exp(-x))


def _gather_shards(shards):
    n = len(shards)

    def body(*refs):
        ins, outs = refs[:n], refs[n:2 * n]
        send, recv, fsend, frecv, local = refs[2 * n:]
        x, y, c = lax.axis_index("x"), lax.axis_index("y"), lax.axis_index("c")
        me = 2 * x + y
        sibling = (x, y, 1 - c)
        chips = [(1 - x, y), (x, 1 - y), (1 - x, 1 - y)]

        def half(ref, t, which):
            rows = shards[t].shape[0] // 2
            return ref.at[pl.ds(pl.multiple_of(which * rows, rows), rows)]

        def over_ici(t, k, chip, src_slot_ref, dst_slot):
            return pltpu.make_async_remote_copy(
                src_ref=half(src_slot_ref, t, c), dst_ref=half(outs[t].at[dst_slot], t, c), send_sem=send.at[3 * t + k],
                recv_sem=recv.at[3 * t + k], device_id=(*chip, c), device_id_type=MESH)

        def to_sibling(t, k, slot, which):
            return pltpu.make_async_remote_copy(
                src_ref=half(outs[t].at[slot], t, which), dst_ref=half(outs[t].at[slot], t, which),
                send_sem=fsend.at[3 * t + k], recv_sem=frecv.at[3 * t + k], device_id=sibling, device_id_type=MESH)

        started = []
        for t in range(n):
            mine = pltpu.make_async_copy(ins[t], outs[t].at[me], local.at[t])
            mine.start()
            started.append(mine)
        sent = []
        for t in range(n):
            for k, chip in enumerate(chips):
                cp = over_ici(t, k, chip, ins[t], me)
                cp.start()
                sent.append(cp)
        for t in range(n):
            for k, (px, py) in enumerate(chips):
                over_ici(t, k, (px, py), ins[t], 2 * px + py).wait_recv()
                fw = to_sibling(t, k, 2 * px + py, c)
                fw.start()
                sent.append(fw)
        for t in range(n):
            for k, (px, py) in enumerate(chips):
                to_sibling(t, k, 2 * px + py, 1 - c).wait_recv()
        for cp in sent:
            cp.wait_send()
        for cp in started:
            cp.wait()

    return pl.pallas_call(
        body, name="gather_weight_shards",
        in_specs=[ANY] * n, out_specs=[ANY] * n,
        out_shape=[jax.ShapeDtypeStruct((N_CHIPS,) + s.shape, s.dtype) for s in shards],
        scratch_shapes=[pltpu.SemaphoreType.DMA((3 * n,)), pltpu.SemaphoreType.DMA((3 * n,)),
                        pltpu.SemaphoreType.DMA((3 * n,)), pltpu.SemaphoreType.DMA((3 * n,)),
                        pltpu.SemaphoreType.DMA((n,))],
    )(*shards)


def _sibling_exchange(g, s):
    def body(g_ref, s_ref, rb_ref, sr_ref, send, recv):
        x, y, c = lax.axis_index("x"), lax.axis_index("y"), lax.axis_index("c")
        sib = (x, y, 1 - c)
        big = pltpu.make_async_remote_copy(src_ref=g_ref.at[1 - c], dst_ref=rb_ref, send_sem=send.at[0],
                                           recv_sem=recv.at[0], device_id=sib, device_id_type=MESH)
        small = pltpu.make_async_remote_copy(src_ref=s_ref, dst_ref=sr_ref, send_sem=send.at[1],
                                             recv_sem=recv.at[1], device_id=sib, device_id_type=MESH)
        big.start()
        small.start()
        big.wait()
        small.wait()

    return pl.pallas_call(
        body, name="grad_sibling_exchange",
        in_specs=[ANY, ANY], out_specs=[ANY, ANY],
        out_shape=[jax.ShapeDtypeStruct(g.shape[1:], g.dtype), jax.ShapeDtypeStruct(s.shape, s.dtype)],
        scratch_shapes=[pltpu.SemaphoreType.DMA((2,)), pltpu.SemaphoreType.DMA((2,))],
    )(g, s)


def _chip_scatter(h, s):
    def body(h_ref, s_ref, q_ref, sq_ref, send, recv, local):
        x, y, c = lax.axis_index("x"), lax.axis_index("y"), lax.axis_index("c")
        me = 2 * x + y
        chips = [(1 - x, y), (x, 1 - y), (1 - x, 1 - y)]
        own_h = pltpu.make_async_copy(h_ref.at[me], q_ref.at[me], local.at[0])
        own_s = pltpu.make_async_copy(s_ref, sq_ref.at[me], local.at[1])
        own_h.start()
        own_s.start()
        remote = []
        for k, (px, py) in enumerate(chips):
            cp = pltpu.make_async_remote_copy(src_ref=h_ref.at[2 * px + py], dst_ref=q_ref.at[me], send_sem=send.at[k],
                                              recv_sem=recv.at[k], device_id=(px, py, c), device_id_type=MESH)
            cs = pltpu.make_async_remote_copy(src_ref=s_ref, dst_ref=sq_ref.at[me], send_sem=send.at[3 + k],
                                              recv_sem=recv.at[3 + k], device_id=(px, py, c), device_id_type=MESH)
            cp.start()
            cs.start()
            remote += [cp, cs]
        for k, (px, py) in enumerate(chips):
            pltpu.make_async_remote_copy(src_ref=h_ref.at[me], dst_ref=q_ref.at[2 * px + py], send_sem=send.at[k],
                                         recv_sem=recv.at[k], device_id=(px, py, c), device_id_type=MESH).wait_recv()
            pltpu.make_async_remote_copy(src_ref=s_ref, dst_ref=sq_ref.at[2 * px + py], send_sem=send.at[3 + k],
                                         recv_sem=recv.at[3 + k], device_id=(px, py, c), device_id_type=MESH).wait_recv()
        for cp in remote:
            cp.wait_send()
        own_h.wait()
        own_s.wait()

    return pl.pallas_call(
        body, name="grad_chip_scatter",
        in_specs=[ANY, ANY], out_specs=[ANY, ANY],
        out_shape=[jax.ShapeDtypeStruct(h.shape, h.dtype), jax.ShapeDtypeStruct((N_CHIPS,) + s.shape, s.dtype)],
        scratch_shapes=[pltpu.SemaphoreType.DMA((6,)), pltpu.SemaphoreType.DMA((6,)), pltpu.SemaphoreType.DMA((2,))],
    )(h, s)


def _sibling_allgather(t):
    def body(t_ref, o_ref, send, recv, local):
        x, y, c = lax.axis_index("x"), lax.axis_index("y"), lax.axis_index("c")
        own = pltpu.make_async_copy(t_ref, o_ref.at[c], local.at[0])
        own.start()
        cp = pltpu.make_async_remote_copy(src_ref=t_ref, dst_ref=o_ref.at[c], send_sem=send.at[0], recv_sem=recv.at[0],
                                          device_id=(x, y, 1 - c), device_id_type=MESH)
        cp.start()
        pltpu.make_async_remote_copy(src_ref=t_ref, dst_ref=o_ref.at[1 - c], send_sem=send.at[0], recv_sem=recv.at[0],
                                     device_id=(x, y, 1 - c), device_id_type=MESH).wait_recv()
        cp.wait_send()
        own.wait()

    return pl.pallas_call(
        body, name="grad_sibling_allgather",
        in_specs=[ANY], out_specs=ANY,
        out_shape=jax.ShapeDtypeStruct((2,) + t.shape, t.dtype),
        scratch_shapes=[pltpu.SemaphoreType.DMA((1,)), pltpu.SemaphoreType.DMA((1,)), pltpu.SemaphoreType.DMA((1,))],
    )(t)


def _add_sibling_half(g, rb, s, sr, core):
    rows, srows = g.shape[2], s.shape[0]

    def body(core_ref, g_ref, rb_ref, s_ref, sr_ref, h_ref, sc_ref):
        h_ref[...] = (g_ref[...] + rb_ref[...]).astype(BF16)
        sc_ref[...] = s_ref[...] + sr_ref[...]

    return pl.pallas_call(
        body, name="grad_add_sibling",
        grid_spec=pltpu.PrefetchScalarGridSpec(
            num_scalar_prefetch=1, grid=(N_CHIPS,),
            in_specs=[pl.BlockSpec((None, None, rows, LANES), lambda j, cr: (cr[0], j, 0, 0)),
                      pl.BlockSpec((None, rows, LANES), lambda j, cr: (j, 0, 0)),
                      pl.BlockSpec((srows, LANES), lambda j, cr: (0, 0)),
                      pl.BlockSpec((srows, LANES), lambda j, cr: (0, 0))],
            out_specs=[pl.BlockSpec((None, rows, LANES), lambda j, cr: (j, 0, 0)),
                       pl.BlockSpec((srows, LANES), lambda j, cr: (0, 0))]),
        out_shape=[jax.ShapeDtypeStruct(rb.shape, BF16), jax.ShapeDtypeStruct(s.shape, F32)],
        compiler_params=_params(("arbitrary",)),
    )(core, g, rb, s, sr)


def _add_chips(q, sq):
    rows, srows = q.shape[1], sq.shape[1]

    def body(q_ref, sq_ref, t_ref, st_ref):
        t_ref[...] = ((q_ref[0].astype(F32) + q_ref[1].astype(F32)) + q_ref[2].astype(F32)) + q_ref[3].astype(F32)
        st_ref[...] = ((sq_ref[0] + sq_ref[1]) + sq_ref[2]) + sq_ref[3]

    return pl.pallas_call(
        body, name="grad_add_chips",
        out_shape=[jax.ShapeDtypeStruct((rows, LANES), F32), jax.ShapeDtypeStruct((srows, LANES), F32)],
        compiler_params=_params(),
    )(q, sq)


def _mm(a, b, out_dtype, name, m, tm, tn):
    k = a.shape[1]
    n = b.shape[1]

    def body(a_ref, b_ref, o_ref):
        o_ref[...] = jnp.dot(a_ref[...], b_ref[...], preferred_element_type=F32).astype(out_dtype)

    return pl.pallas_call(
        body, name=name, grid=(n // tn, m // tm),
        in_specs=[pl.BlockSpec((tm, k), lambda j, i: (i, 0)), pl.BlockSpec((k, tn), lambda j, i: (0, j))],
        out_specs=pl.BlockSpec((tm, tn), lambda j, i: (i, j)),
        out_shape=jax.ShapeDtypeStruct((m, n), out_dtype),
        compiler_params=_params(("parallel", "parallel")),
    )(a, b)


def _mm_tn(a, b, name, tm, tn, tk):
    m = a.shape[1]
    k, n = b.shape

    def body(a_ref, b_ref, o_ref):
        @pl.when(pl.program_id(2) == 0)
        def _():
            o_ref[...] = jnp.zeros_like(o_ref)
        o_ref[...] += lax.dot_general(a_ref[...].astype(BF16), b_ref[...].astype(BF16), (((0,), (0,)), ((), ())),
                                      preferred_element_type=F32)

    return pl.pallas_call(
        body, name=name, grid=(m // tm, n // tn, k // tk),
        in_specs=[pl.BlockSpec((tk, tm), lambda i, j, kk: (kk, i)), pl.BlockSpec((tk, tn), lambda i, j, kk: (kk, j))],
        out_specs=pl.BlockSpec((tm, tn), lambda i, j, kk: (i, j)),
        out_shape=jax.ShapeDtypeStruct((m, n), F32),
        compiler_params=_params(("parallel", "parallel", "arbitrary")),
    )(a, b)


def _tokens_spec():
    return pl.BlockSpec((ROW_TILE, D_MODEL), lambda i: (jnp.maximum(i - 1, 0), 0))


def _rmsnorm_fwd(x2, metapad, g1, lk):
    nb = x2.shape[0] // ROW_TILE + 1

    def body(x_ref, mp_ref, g_ref, hn_ref):
        i = pl.program_id(0)
        h = jnp.where(i == 0, mp_ref[...], jnp.where(i >= nb, 0.0, x_ref[...]))
        r = lax.rsqrt(jnp.mean(h * h, axis=-1, keepdims=True) + RMS_EPS)
        hn_ref[...] = ((h * r) * g_ref[...]).astype(BF16)

    return pl.pallas_call(
        body, name="rmsnorm_fwd", grid=(lk // ROW_TILE,),
        in_specs=[pl.BlockSpec((ROW_TILE, D_MODEL), lambda i: (jnp.clip(i - 1, 0, nb - 2), 0)),
                  pl.BlockSpec((ROW_TILE, D_MODEL), lambda i: (0, 0)),
                  pl.BlockSpec((1, D_MODEL), lambda i: (0, 0))],
        out_specs=pl.BlockSpec((ROW_TILE, D_MODEL), lambda i: (i, 0)),
        out_shape=jax.ShapeDtypeStruct((lk, D_MODEL), BF16),
        compiler_params=_params(("parallel",)),
    )(x2, metapad, g1)


def _valid_gate_mask(i):
    row = i * ROW_TILE + lax.broadcasted_iota(jnp.int32, (ROW_TILE, F_COLS), 0)
    col = lax.broadcasted_iota(jnp.int32, (ROW_TILE, F_COLS), 1)
    return (row >= PAD_ROWS) & (col < N_HEADS)


def _gates_fwd(f, bfg):
    nb = f.shape[0] // ROW_TILE

    def body(f_ref, b_ref, c_ref, carry):
        i = pl.program_id(0)

        @pl.when(i == 0)
        def _():
            carry[...] = jnp.zeros_like(carry)

        logit = f_ref[...] + b_ref[...]
        lf = jnp.minimum(logit, 0.0) - jnp.log1p(jnp.exp(-jnp.abs(logit)))
        lf = jnp.where(_valid_gate_mask(i), lf, 0.0)
        r_i = lax.broadcasted_iota(jnp.int32, (ROW_TILE, ROW_TILE), 0)
        c_i = lax.broadcasted_iota(jnp.int32, (ROW_TILE, ROW_TILE), 1)
        tri = (c_i <= r_i).astype(F32)
        c_ref[...] = jnp.dot(tri, lf, precision=HIGHEST, preferred_element_type=F32) + carry[...]
        carry[...] = carry[...] + jnp.sum(lf, axis=0, keepdims=True)

    return pl.pallas_call(
        body, name="gates_fwd", grid=(nb,),
        in_specs=[pl.BlockSpec((ROW_TILE, F_COLS), lambda i: (i, 0)), pl.BlockSpec((1, F_COLS), lambda i: (0, 0))],
        out_specs=pl.BlockSpec((ROW_TILE, F_COLS), lambda i: (i, 0)),
        out_shape=jax.ShapeDtypeStruct(f.shape, F32),
        scratch_shapes=[pltpu.VMEM((1, F_COLS), F32)],
        compiler_params=_params(("arbitrary",)),
    )(f, bfg)


def _pool_counts(i):
    row = i * ROW_TILE + lax.broadcasted_iota(jnp.int32, (ROW_TILE, 1), 0)
    return jnp.maximum(row - PAD_ROWS, 0)


def _trailing_sums(xc, levels):
    n = xc.shape[0]
    acc = xc
    for lv in range(levels):
        acc = acc + pltpu.roll(acc, 1 << lv, 0)
    return acc


def _leading_sums(xc, levels):
    n = xc.shape[0]
    acc = xc
    for lv in range(levels):
        acc = acc + pltpu.roll(acc, n - (1 << lv), 0)
    return acc


def _pool_p(u_cur, u_prev, i):
    pos = _pool_counts(i)
    ps, invs = [], []
    for g, w in enumerate(POOL_WINDOWS):
        sl = slice(g * POOL_GROUP, (g + 1) * POOL_GROUP)
        cur = u_cur[:, sl]
        xc = jnp.concatenate([u_prev[:, sl], cur], axis=0)
        win = _trailing_sums(xc, g + 1)[ROW_TILE:, :]
        inv = 1.0 / jnp.minimum(pos + 1, w).astype(F32)
        ps.append(win * inv - cur)
        invs.append(inv)
    return ps, invs


def _pool_fwd(e, pw, scale):
    nb = e.shape[0] // ROW_TILE

    def body(uc_ref, up_ref, z_ref, pw_ref, sc_ref, y_ref):
        i = pl.program_id(0)
        u_cur = uc_ref[...]
        u_prev = jnp.where(i == 0, 0.0, up_ref[...])
        ps, _ = _pool_p(u_cur, u_prev, i)
        z = z_ref[...]
        gate = z * _sigmoid(z)
        for g in range(len(POOL_WINDOWS)):
            sl = slice(g * POOL_GROUP, (g + 1) * POOL_GROUP)
            yraw = jnp.dot(ps[g].astype(BF16), pw_ref[g], preferred_element_type=F32)
            y_ref[:, sl] = ((yraw * sc_ref[:, sl]) * gate[:, sl]).astype(BF16)

    blk = (ROW_TILE, POOL_WIDTH)
    return pl.pallas_call(
        body, name="pool_fwd", grid=(nb,),
        in_specs=[pl.BlockSpec(blk, lambda i: (i, 0)), pl.BlockSpec(blk, lambda i: (jnp.maximum(i - 1, 0), 0)),
                  pl.BlockSpec(blk, lambda i: (i, 1)),
                  pl.BlockSpec((len(POOL_WINDOWS), POOL_GROUP, POOL_GROUP), lambda i: (0, 0, 0)),
                  pl.BlockSpec((1, POOL_WIDTH), lambda i: (0, 0))],
        out_specs=pl.BlockSpec(blk, lambda i: (i, 0)),
        out_shape=jax.ShapeDtypeStruct((e.shape[0], POOL_WIDTH), BF16),
        compiler_params=_params(("parallel",)),
    )(e, e, e, pw, scale)


def _stack_heads(a):
    first = lax.broadcasted_iota(jnp.int32, a.shape, 1) < HEAD_DIM
    zero = jnp.zeros_like(a)
    return jnp.concatenate([jnp.where(first, a, zero), jnp.where(first, zero, a)], axis=0)


def _unstack_heads(a):
    rows = a.shape[0] // 2
    first = lax.broadcasted_iota(jnp.int32, (rows, LANES), 1) < HEAD_DIM
    return jnp.where(first, a[:rows], a[rows:])


def _causal(i, jj, stacked):
    r = lax.broadcasted_iota(jnp.int32, (stacked * ROW_TILE, KV_TILE), 0)
    if stacked == 2:
        r = jnp.where(r >= ROW_TILE, r - ROW_TILE, r)
    kidx = jj * KV_TILE + lax.broadcasted_iota(jnp.int32, (stacked * ROW_TILE, KV_TILE), 1)
    return kidx <= i * ROW_TILE + r


def _stack_rows(b):
    n = b.shape[1]
    return jnp.concatenate([jnp.broadcast_to(b[0:1], (ROW_TILE, n)), jnp.broadcast_to(b[1:2], (ROW_TILE, n))], axis=0)


def _dot_nt(a, b):
    return lax.dot_general(a, b, (((1,), (1,)), ((), ())), preferred_element_type=F32)


def _dot_tn(a, b):
    return lax.dot_general(a, b, (((0,), (0,)), ((), ())), preferred_element_type=F32)


def _attn_fwd(qkv, bias, nb):
    lk = qkv.shape[0]
    lp = nb * ROW_TILE
    nkb = lk // KV_TILE
    n_pairs = N_HEADS // 2

    def body(q_ref, k_ref, v_ref, b_ref, o_ref, lse_ref):
        i = pl.program_id(1)
        qs = _stack_heads(q_ref[...])
        last = (i * ROW_TILE) // KV_TILE

        def block(jj, carry, masked):
            m, l, acc = carry
            rows = pl.ds(pl.multiple_of(jj * KV_TILE, KV_TILE), KV_TILE)
            s = _dot_nt(qs, k_ref[rows, :]) - _stack_rows(b_ref[0, 0, jj])
            if masked:
                s = jnp.where(_causal(i, jj, 2), s, NEG)
            m_new = jnp.maximum(m, jnp.max(s, axis=1, keepdims=True))
            alpha = jnp.exp(m - m_new)
            p = jnp.exp(s - m_new)
            l = alpha * l + jnp.sum(p, axis=1, keepdims=True)
            acc = alpha * acc + jnp.dot(p.astype(BF16), v_ref[rows, :], preferred_element_type=F32)
            return m_new, l, acc

        init = (jnp.full((2 * ROW_TILE, 1), NEG, F32), jnp.zeros((2 * ROW_TILE, 1), F32),
                jnp.zeros((2 * ROW_TILE, LANES), F32))
        carry = lax.fori_loop(0, last, lambda jj, c: block(jj, c, False), init)
        m, l, acc = block(last, carry, True)
        o_ref[...] = _unstack_heads(acc / l)
        lse_ref[...] = _unstack_heads(jnp.broadcast_to(m + jnp.log(l), (2 * ROW_TILE, LANES)))

    return pl.pallas_call(
        body, name="attn_fwd", grid=(n_pairs, nb),
        in_specs=[pl.BlockSpec((ROW_TILE, LANES), lambda hp, i: (i, hp)),
                  pl.BlockSpec((lk, LANES), lambda hp, i: (0, n_pairs + hp)),
                  pl.BlockSpec((lk, LANES), lambda hp, i: (0, 2 * n_pairs + hp)),
                  pl.BlockSpec((1, 1, nkb, 2, KV_TILE), lambda hp, i: (i, hp, 0, 0, 0))],
        out_specs=[pl.BlockSpec((ROW_TILE, LANES), lambda hp, i: (i, hp)),
                   pl.BlockSpec((ROW_TILE, LANES), lambda hp, i: (i, hp))],
        out_shape=[jax.ShapeDtypeStruct((lp, ATTN_WIDTH), F32), jax.ShapeDtypeStruct((lp, ATTN_WIDTH), F32)],
        compiler_params=_params(("parallel", "parallel")),
    )(qkv, qkv, qkv, bias)


def _merge_fwd(y_pool, o, e, wup_p, wup_a):
    lp = o.shape[0]
    nb = lp // ROW_TILE

    def body(yp_ref, o_ref, e_ref, wp_ref, wa_ref, mg_ref, ya_ref):
        za = e_ref[:, E_ZA:E_GP]
        ya = (o_ref[...] * (za * _sigmoid(za))).astype(BF16)
        ya_ref[...] = ya
        a_pool = jnp.dot(yp_ref[...], wp_ref[...], preferred_element_type=F32)
        a_attn = jnp.dot(ya, wa_ref[...], preferred_element_type=F32)
        mg_ref[...] = (_sigmoid(e_ref[:, E_GP:E_GA]) * a_pool + _sigmoid(e_ref[:, E_GA:E_COLS]) * a_attn).astype(BF16)

    return pl.pallas_call(
        body, name="merge_fwd", grid=(nb,),
        in_specs=[pl.BlockSpec((ROW_TILE, POOL_WIDTH), lambda i: (i, 0)),
                  pl.BlockSpec((ROW_TILE, ATTN_WIDTH), lambda i: (i, 0)),
                  pl.BlockSpec((ROW_TILE, E_COLS), lambda i: (i, 0)),
                  pl.BlockSpec((POOL_WIDTH, D_MODEL), lambda i: (0, 0)),
                  pl.BlockSpec((ATTN_WIDTH, D_MODEL), lambda i: (0, 0))],
        out_specs=[pl.BlockSpec((ROW_TILE, D_MODEL), lambda i: (i, 0)),
                   pl.BlockSpec((ROW_TILE, ATTN_WIDTH), lambda i: (i, 0))],
        out_shape=[jax.ShapeDtypeStruct((lp, D_MODEL), BF16), jax.ShapeDtypeStruct((lp, ATTN_WIDTH), BF16)],
        compiler_params=_params(("parallel",)),
    )(y_pool, o, e, wup_p, wup_a)


def _head_fwd_bwd(merged, w_out, x2, metapad, gf, target):
    lp = merged.shape[0]
    nb = lp // ROW_TILE

    def body(mg_ref, w_ref, x_ref, mp_ref, g_ref, t_ref, dh_ref, loss_ref, dg_ref):
        i = pl.program_id(0)

        @pl.when(i == 0)
        def _():
            loss_ref[...] = jnp.zeros_like(loss_ref)
            dg_ref[...] = jnp.zeros_like(dg_ref)

        h0 = jnp.where(i == 0, mp_ref[...], x_ref[...])
        h1 = h0 + jnp.dot(mg_ref[...], w_ref[...], preferred_element_type=F32)
        r = lax.rsqrt(jnp.mean(h1 * h1, axis=-1, keepdims=True) + RMS_EPS)
        xhat = h1 * r
        g = g_ref[...]
        err = jnp.where(i == 0, 0.0, xhat * g - t_ref[...])
        loss_ref[...] += 0.5 * jnp.sum(jnp.mean(err * err, axis=-1, keepdims=True))
        dy = err / D_MODEL
        dg_ref[...] += jnp.sum(dy * xhat, axis=0, keepdims=True)
        dxhat = dy * g
        dh_ref[...] = r * (dxhat - xhat * jnp.mean(dxhat * xhat, axis=-1, keepdims=True))

    return pl.pallas_call(
        body, name="head_fwd_bwd", grid=(nb,),
        in_specs=[pl.BlockSpec((ROW_TILE, D_MODEL), lambda i: (i, 0)),
                  pl.BlockSpec((D_MODEL, D_MODEL), lambda i: (0, 0)),
                  _tokens_spec(), pl.BlockSpec((ROW_TILE, D_MODEL), lambda i: (0, 0)),
                  pl.BlockSpec((1, D_MODEL), lambda i: (0, 0)), _tokens_spec()],
        out_specs=[pl.BlockSpec((ROW_TILE, D_MODEL), lambda i: (i, 0)),
                   pl.BlockSpec((1, LANES), lambda i: (0, 0)), pl.BlockSpec((1, D_MODEL), lambda i: (0, 0))],
        out_shape=[jax.ShapeDtypeStruct((lp, D_MODEL), F32), jax.ShapeDtypeStruct((1, LANES), F32),
                   jax.ShapeDtypeStruct((1, D_MODEL), F32)],
        compiler_params=_params(("arbitrary",)),
    )(merged, w_out, x2, metapad, gf, target)


def _per_head_rowsum(t):
    head = lax.broadcasted_iota(jnp.int32, t.shape, 1) // HEAD_DIM
    out = jnp.zeros_like(t)
    for h in range(N_HEADS):
        sel = head == h
        out = jnp.where(sel, jnp.sum(jnp.where(sel, t, 0.0), axis=1, keepdims=True), out)
    return out


def _merge_bwd(dh1, w_out, y_pool, y_attn, wup_p, wup_a, e, o):
    lp = o.shape[0]
    nb = lp // ROW_TILE

    def body(dh_ref, wo_ref, yp_ref, ya_ref, wp_ref, wa_ref, e_ref, o_ref,
             dap_ref, daa_ref, dg_ref, dza_ref, do_ref, delta_ref, dyp_ref):
        dmerged = _dot_nt(dh_ref[...].astype(BF16), wo_ref[...])
        a_pool = jnp.dot(yp_ref[...], wp_ref[...], preferred_element_type=F32)
        a_attn = jnp.dot(ya_ref[...], wa_ref[...], preferred_element_type=F32)
        sp = _sigmoid(e_ref[:, E_GP:E_GA])
        sa = _sigmoid(e_ref[:, E_GA:E_COLS])
        dap = (dmerged * sp).astype(BF16)
        daa = (dmerged * sa).astype(BF16)
        dap_ref[...] = dap
        daa_ref[...] = daa
        dg_ref[:, :D_MODEL] = (dmerged * a_pool * (sp * (1.0 - sp))).astype(BF16)
        dg_ref[:, D_MODEL:] = (dmerged * a_attn * (sa * (1.0 - sa))).astype(BF16)
        dyp_ref[...] = _dot_nt(dap, wp_ref[...])
        dya = _dot_nt(daa, wa_ref[...])
        za = e_ref[:, E_ZA:E_GP]
        sz = _sigmoid(za)
        o = o_ref[...]
        do = dya * (za * sz)
        do_ref[...] = do.astype(BF16)
        dza_ref[...] = (dya * o * (sz * (1.0 + za * (1.0 - sz)))).astype(BF16)
        delta_ref[...] = _per_head_rowsum(do * o)

    row = lambda w: pl.BlockSpec((ROW_TILE, w), lambda i: (i, 0))
    full = lambda a: pl.BlockSpec(a.shape, lambda i: (0, 0))
    return pl.pallas_call(
        body, name="merge_bwd", grid=(nb,),
        in_specs=[row(D_MODEL), full(w_out), row(POOL_WIDTH), row(ATTN_WIDTH), full(wup_p), full(wup_a),
                  row(E_COLS), row(ATTN_WIDTH)],
        out_specs=[row(D_MODEL), row(D_MODEL), row(2 * D_MODEL), row(ATTN_WIDTH), row(ATTN_WIDTH),
                   row(ATTN_WIDTH), row(POOL_WIDTH)],
        out_shape=[jax.ShapeDtypeStruct((lp, D_MODEL), BF16), jax.ShapeDtypeStruct((lp, D_MODEL), BF16),
                   jax.ShapeDtypeStruct((lp, 2 * D_MODEL), BF16), jax.ShapeDtypeStruct((lp, ATTN_WIDTH), BF16),
                   jax.ShapeDtypeStruct((lp, ATTN_WIDTH), BF16), jax.ShapeDtypeStruct((lp, ATTN_WIDTH), F32),
                   jax.ShapeDtypeStruct((lp, POOL_WIDTH), F32)],
        compiler_params=_params(("parallel",)),
    )(dh1, w_out, y_pool, y_attn, wup_p, wup_a, e, o)


def _pool_bwd_local(e, dy_pool, pw, scale):
    lp = e.shape[0]
    nb = lp // ROW_TILE
    ng = len(POOL_WINDOWS)

    def body(uc_ref, up_ref, z_ref, dy_ref, pw_ref, sc_ref, dpc_ref, dz_ref, dsc_ref, dpw_ref):
        i = pl.program_id(0)

        @pl.when(i == 0)
        def _():
            dsc_ref[...] = jnp.zeros_like(dsc_ref)
            dpw_ref[...] = jnp.zeros_like(dpw_ref)

        u_cur = uc_ref[...]
        u_prev = jnp.where(i == 0, 0.0, up_ref[...])
        ps, invs = _pool_p(u_cur, u_prev, i)
        z = z_ref[...]
        sz = _sigmoid(z)
        dy = dy_ref[...]
        dypre = dy * (z * sz)
        dsilu = sz * (1.0 + z * (1.0 - sz))
        for g in range(ng):
            sl = slice(g * POOL_GROUP, (g + 1) * POOL_GROUP)
            pb = ps[g].astype(BF16)
            w = pw_ref[g]
            yraw = jnp.dot(pb, w, preferred_element_type=F32)
            sc = sc_ref[:, sl]
            dz_ref[:, sl] = (dy[:, sl] * (yraw * sc) * dsilu[:, sl]).astype(BF16)
            dsc_ref[:, sl] += jnp.sum(dypre[:, sl] * yraw, axis=0, keepdims=True)
            dyraw = (dypre[:, sl] * sc).astype(BF16)
            dpw_ref[g] += _dot_tn(pb, dyraw)
            dpc_ref[:, sl] = _dot_nt(dyraw, w) * invs[g]

    blk = (ROW_TILE, POOL_WIDTH)
    return pl.pallas_call(
        body, name="pool_bwd_local", grid=(nb,),
        in_specs=[pl.BlockSpec(blk, lambda i: (i, 0)), pl.BlockSpec(blk, lambda i: (jnp.maximum(i - 1, 0), 0)),
                  pl.BlockSpec(blk, lambda i: (i, 1)), pl.BlockSpec(blk, lambda i: (i, 0)),
                  pl.BlockSpec((ng, POOL_GROUP, POOL_GROUP), lambda i: (0, 0, 0)),
                  pl.BlockSpec((1, POOL_WIDTH), lambda i: (0, 0))],
        out_specs=[pl.BlockSpec(blk, lambda i: (i, 0)), pl.BlockSpec(blk, lambda i: (i, 0)),
                   pl.BlockSpec((1, POOL_WIDTH), lambda i: (0, 0)),
                   pl.BlockSpec((ng, POOL_GROUP, POOL_GROUP), lambda i: (0, 0, 0))],
        out_shape=[jax.ShapeDtypeStruct((lp, POOL_WIDTH), F32), jax.ShapeDtypeStruct((lp, POOL_WIDTH), BF16),
                   jax.ShapeDtypeStruct((1, POOL_WIDTH), F32),
                   jax.ShapeDtypeStruct((ng, POOL_GROUP, POOL_GROUP), F32)],
        compiler_params=_params(("arbitrary",)),
    )(e, e, e, dy_pool, pw, scale)


def _pool_bwd_window(dpc):
    lp = dpc.shape[0]
    nb = lp // ROW_TILE

    def body(cur_ref, nxt_ref, du_ref):
        i = pl.program_id(0)
        cur = cur_ref[...]
        nxt = jnp.where(i == nb - 1, 0.0, nxt_ref[...])
        pos = _pool_counts(i)
        for g, w in enumerate(POOL_WINDOWS):
            sl = slice(g * POOL_GROUP, (g + 1) * POOL_GROUP)
            xc = jnp.concatenate([cur[:, sl], nxt[:, sl]], axis=0)
            win = _leading_sums(xc, g + 1)[:ROW_TILE, :]
            dp = cur[:, sl] * jnp.minimum(pos + 1, w).astype(F32)
            du_ref[:, sl] = (win - dp).astype(BF16)

    blk = (ROW_TILE, POOL_WIDTH)
    return pl.pallas_call(
        body, name="pool_bwd_window", grid=(nb,),
        in_specs=[pl.BlockSpec(blk, lambda i: (i, 0)), pl.BlockSpec(blk, lambda i: (jnp.minimum(i + 1, nb - 1), 0))],
        out_specs=pl.BlockSpec(blk, lambda i: (i, 0)),
        out_shape=jax.ShapeDtypeStruct((lp, POOL_WIDTH), BF16),
        compiler_params=_params(("parallel",)),
    )(dpc, dpc)


def _attn_bwd(qkv, do, lse, delta, bias, nb):
    lk = qkv.shape[0]
    lp = nb * ROW_TILE
    nkb = lk // KV_TILE
    n_pairs = N_HEADS // 2
    per_kv = KV_TILE // ROW_TILE

    def body(q_ref, k_ref, v_ref, do_ref, lse_ref, dl_ref, b_ref, dq_ref, dk_ref, dv_ref, dc_ref, dcq_ref):
        jj = pl.program_id(1)

        @pl.when(jj == 0)
        def _():
            dq_ref[...] = jnp.zeros_like(dq_ref)
            dcq_ref[...] = jnp.zeros_like(dcq_ref)

        kb, vb = k_ref[...], v_ref[...]

        def block(i, carry, masked):
            dk_acc, dv_acc, dc_acc = carry
            rows = pl.ds(pl.multiple_of(i * ROW_TILE, ROW_TILE), ROW_TILE)
            qs = _stack_heads(q_ref[rows, :])
            dos = _stack_heads(do_ref[rows, :])
            lse_i, dl_i = lse_ref[rows, :], dl_ref[rows, :]
            s = _dot_nt(qs, kb)
            dp = _dot_nt(dos, vb)
            if masked:
                valid = _causal(i, jj, 1)
            ps, dss, dcs, rowsums = [], [], [], []
            for hd in range(2):
                half = slice(hd * ROW_TILE, (hd + 1) * ROW_TILE)
                col = slice(hd * HEAD_DIM, hd * HEAD_DIM + 1)
                sh = s[half] - b_ref[i, 0, 0, hd:hd + 1, :]
                if masked:
                    sh = jnp.where(valid, sh, NEG)
                p = jnp.exp(sh - lse_i[:, col])
                ds = p * (dp[half] - dl_i[:, col])
                ps.append(p.astype(BF16))
                dss.append(ds.astype(BF16))
                dcs.append(jnp.sum(ds, axis=0, keepdims=True))
                rowsums.append(jnp.sum(ds, axis=1, keepdims=True))
            dsb = jnp.concatenate(dss, axis=0)
            dv_acc = dv_acc + _dot_tn(jnp.concatenate(ps, axis=0), dos)
            dk_acc = dk_acc + _dot_tn(dsb, qs)
            dc_acc = dc_acc - jnp.concatenate(dcs, axis=0)
            dq_ref[rows, :] += _unstack_heads(jnp.dot(dsb, kb, preferred_element_type=F32))
            dcq_ref[rows, :] += _unstack_heads(jnp.broadcast_to(jnp.concatenate(rowsums, axis=0), (2 * ROW_TILE, LANES)))
            return dk_acc, dv_acc, dc_acc

        init = (jnp.zeros((KV_TILE, LANES), F32), jnp.zeros((KV_TILE, LANES), F32), jnp.zeros((2, KV_TILE), F32))
        first_q = per_kv * jj
        diag_end = jnp.minimum(first_q + per_kv, nb)
        carry = lax.fori_loop(first_q, diag_end, lambda i, c: block(i, c, True), init)
        dk, dv, dc = lax.fori_loop(diag_end, nb, lambda i, c: block(i, c, False), carry)
        dk_ref[...] = dk.astype(BF16)
        dv_ref[...] = dv.astype(BF16)
        dc_ref[0, 0] = dc

    whole = lambda rows: pl.BlockSpec((rows, LANES), lambda hp, jj: (0, hp))
    kv_blk = lambda off: pl.BlockSpec((KV_TILE, LANES), lambda hp, jj: (jj, off + hp))
    return pl.pallas_call(
        body, name="attn_bwd", grid=(n_pairs, nkb),
        in_specs=[whole(lk), kv_blk(n_pairs), kv_blk(2 * n_pairs), whole(lp), whole(lp), whole(lp),
                  pl.BlockSpec((nb, 1, 1, 2, KV_TILE), lambda hp, jj: (0, hp, jj, 0, 0))],
        out_specs=[whole(lp), kv_blk(0), kv_blk(0),
                   pl.BlockSpec((1, 1, 2, KV_TILE), lambda hp, jj: (hp, jj, 0, 0)), whole(lp)],
        out_shape=[jax.ShapeDtypeStruct((lp, ATTN_WIDTH), F32), jax.ShapeDtypeStruct((lk, ATTN_WIDTH), BF16),
                   jax.ShapeDtypeStruct((lk, ATTN_WIDTH), BF16),
                   jax.ShapeDtypeStruct((n_pairs, nkb, 2, KV_TILE), F32),
                   jax.ShapeDtypeStruct((lp, ATTN_WIDTH), F32)],
        compiler_params=_params(("parallel", "arbitrary")),
    )(qkv, qkv, qkv, do, lse, delta, bias)


def _gates_bwd(dc, f, bfg):
    nb = f.shape[0] // ROW_TILE

    def body(dc_ref, f_ref, b_ref, df_ref, db_ref, carry):
        step = pl.program_id(0)
        i = nb - 1 - step

        @pl.when(step == 0)
        def _():
            carry[...] = jnp.zeros_like(carry)
            db_ref[...] = jnp.zeros_like(db_ref)

        dcb = dc_ref[...]
        r_i = lax.broadcasted_iota(jnp.int32, (ROW_TILE, ROW_TILE), 0)
        c_i = lax.broadcasted_iota(jnp.int32, (ROW_TILE, ROW_TILE), 1)
        upper = (c_i >= r_i).astype(F32)
        dlf = jnp.dot(upper, dcb, precision=HIGHEST, preferred_element_type=F32) + carry[...]
        carry[...] = carry[...] + jnp.sum(dcb, axis=0, keepdims=True)
        logit = f_ref[...] + b_ref[...]
        dlogit = jnp.where(_valid_gate_mask(i), dlf * _sigmoid(-logit), 0.0)
        df_ref[...] = dlogit.astype(BF16)
        db_ref[...] += jnp.sum(dlogit, axis=0, keepdims=True)

    blk = pl.BlockSpec((ROW_TILE, F_COLS), lambda s: (nb - 1 - s, 0))
    one = pl.BlockSpec((1, F_COLS), lambda s: (0, 0))
    return pl.pallas_call(
        body, name="gates_bwd", grid=(nb,),
        in_specs=[blk, blk, one], out_specs=[blk, one],
        out_shape=[jax.ShapeDtypeStruct(f.shape, BF16), jax.ShapeDtypeStruct((1, F_COLS), F32)],
        scratch_shapes=[pltpu.VMEM((1, F_COLS), F32)],
        compiler_params=_params(("arbitrary",)),
    )(dc, f, bfg)


def _input_bwd(dproj, df, w_main, w_f, x2, metapad, dh1, g1):
    lp = dproj.shape[0]
    nb = lp // ROW_TILE

    def body(dp_ref, df_ref, w_ref, wf_ref, x_ref, mp_ref, dh_ref, g_ref, gx_ref, g0_ref, dg_ref):
        i = pl.program_id(0)

        @pl.when(i == 0)
        def _():
            dg_ref[...] = jnp.zeros_like(dg_ref)

        dhn = _dot_nt(dp_ref[...], w_ref[...]) + _dot_nt(df_ref[...], wf_ref[...])
        h0 = jnp.where(i == 0, mp_ref[...], x_ref[...])
        r = lax.rsqrt(jnp.mean(h0 * h0, axis=-1, keepdims=True) + RMS_EPS)
        xhat = h0 * r
        dg_ref[...] += jnp.sum(dhn * xhat, axis=0, keepdims=True)
        dxhat = dhn * g_ref[...]
        dh0 = dh_ref[...] + r * (dxhat - xhat * jnp.mean(dxhat * xhat, axis=-1, keepdims=True))
        gx_ref[...] = dh0

        @pl.when(i == 0)
        def _():
            g0_ref[...] = dh0

    return pl.pallas_call(
        body, name="input_bwd", grid=(nb,),
        in_specs=[pl.BlockSpec((ROW_TILE, MAIN_COLS), lambda i: (i, 0)), pl.BlockSpec((ROW_TILE, F_COLS), lambda i: (i, 0)),
                  pl.BlockSpec((D_MODEL, MAIN_COLS), lambda i: (0, 0)), pl.BlockSpec((D_MODEL, F_COLS), lambda i: (0, 0)),
                  _tokens_spec(), pl.BlockSpec((ROW_TILE, D_MODEL), lambda i: (0, 0)),
                  pl.BlockSpec((ROW_TILE, D_MODEL), lambda i: (i, 0)), pl.BlockSpec((1, D_MODEL), lambda i: (0, 0))],
        out_specs=[_tokens_spec(), pl.BlockSpec((ROW_TILE, D_MODEL), lambda i: (0, 0)),
                   pl.BlockSpec((1, D_MODEL), lambda i: (0, 0))],
        out_shape=[jax.ShapeDtypeStruct(x2.shape, F32), jax.ShapeDtypeStruct((ROW_TILE, D_MODEL), F32),
                   jax.ShapeDtypeStruct((1, D_MODEL), F32)],
        compiler_params=_params(("arbitrary",), vmem=56 * 1024 * 1024),
    )(dproj, df, w_main, w_f, x2, metapad, dh1, g1)


def _adamw(w, g, m, v, name):
    rows, cols = w.shape
    tr = rows if rows % 8 else _row_tile8(rows)

    def body(w_ref, g_ref, m_ref, v_ref, d_ref, mo_ref, vo_ref):
        g_ = g_ref[...]
        m_new = ADAM_B1 * m_ref[...] + (1.0 - ADAM_B1) * g_
        v_new = ADAM_B2 * v_ref[...] + (1.0 - ADAM_B2) * (g_ * g_)
        m_hat = m_new / (1.0 - ADAM_B1 ** ADAM_STEP)
        v_hat = v_new / (1.0 - ADAM_B2 ** ADAM_STEP)
        d_ref[...] = -ADAM_LR * (m_hat / (jnp.sqrt(v_hat) + ADAM_EPS) + ADAM_WD * w_ref[...])
        mo_ref[...] = m_new
        vo_ref[...] = v_new

    blk = pl.BlockSpec((tr, cols), lambda i: (i, 0))
    return pl.pallas_call(
        body, name=name, grid=(rows // tr,),
        in_specs=[blk] * 4, out_specs=[blk] * 3,
        out_shape=[jax.ShapeDtypeStruct(w.shape, F32)] * 3,
        compiler_params=_params(("parallel",)),
    )(w, g, m, v)


def _row_tile8(rows):
    best = rows
    for t in range(8, 257, 8):
        if rows % t == 0:
            best = t
    return best


def kernel(x, meta_tokens, norm_g, w_in, b_forget, pool_w, pool_scale, w_up_pool, w_up_attn, w_out, final_norm_g, loss_target, m_meta_tokens, m_norm_g, m_w_in, m_b_forget, m_pool_w, m_pool_scale, m_w_up_pool, m_w_up_attn, m_w_out, m_final_norm_g, v_meta_tokens, v_norm_g, v_w_in, v_b_forget, v_pool_w, v_pool_scale, v_w_up_pool, v_w_up_attn, v_w_out, v_final_norm_g):
    seq = x.shape[1]
    assert seq % ROW_TILE == 0 and x.shape[0] == 1
    lp = seq + ROW_TILE
    nb = lp // ROW_TILE
    core = jnp.reshape(lax.axis_index("c"), (1,)).astype(jnp.int32)
    x2 = x[0]
    target = loss_target[0]

    wg_in, wg_up_p, wg_up_a, wg_out, meta_g = _gather_shards(
        [w_in[0].astype(BF16), w_up_pool[0].astype(BF16), w_up_attn[0].astype(BF16), w_out[0].astype(BF16), meta_tokens])
    n_in = wg_in.shape[2] * N_CHIPS
    w_ref = jnp.transpose(wg_in, (1, 0, 2)).reshape(D_MODEL, n_in)
    cols = lambda a, b: w_ref[:, a:b]
    w_main = jnp.concatenate([cols(0, 512), cols(512, 1024), cols(2560, 3072), cols(3080, 4104), cols(4104, 5128),
                              cols(1024, 1536) * (HEAD_DIM ** -0.5), cols(1536, 2048), cols(2048, 2560)], axis=1)
    w_f = jnp.pad(cols(3072, 3080), ((0, 0), (0, F_COLS - N_HEADS)))
    wup_p = jnp.transpose(wg_up_p, (1, 0, 2)).reshape(POOL_WIDTH, D_MODEL)
    wup_a = jnp.transpose(wg_up_a, (1, 0, 2)).reshape(ATTN_WIDTH, D_MODEL)
    wout = wg_out.reshape(D_MODEL, D_MODEL)
    meta_full = jnp.transpose(meta_g, (1, 0, 2)).reshape(N_META, D_MODEL)
    metapad = jnp.pad(meta_full, ((PAD_ROWS, 0), (0, 0)))
    bfg = jnp.pad(b_forget, ((0, 0), (0, F_COLS - N_HEADS)))
    pw_b = pool_w[0].astype(BF16)
    gf = final_norm_g.reshape(1, D_MODEL)

    tm = _row_tile(lp, 1100)
    lk = -(-lp // KV_TILE) * KV_TILE
    nkb = lk // KV_TILE
    hn = _rmsnorm_fwd(x2, metapad, norm_g, lk)
    e = _mm(hn, w_main[:, :E_COLS], F32, "in_proj_gates", lp, tm, 512)
    qkv = _mm(hn, w_main[:, E_COLS:], BF16, "in_proj_qkv", lk, _row_tile(lk, 1200), 512)
    f = _mm(hn, w_f, F32, "in_proj_forget", lp, tm, F_COLS)
    c = _gates_fwd(f, bfg)
    c_t = jnp.transpose(c[:, :N_HEADS])
    bias = c_t[None, :, :] - jnp.transpose(c_t[:, ::ROW_TILE])[:, :, None]
    bias = jnp.where(jnp.arange(lp) < PAD_ROWS, -NEG, bias)
    bias = jnp.pad(bias, ((0, 0), (0, 0), (0, lk - lp)))
    bias = jnp.transpose(bias.reshape(nb, N_HEADS // 2, 2, nkb, KV_TILE), (0, 1, 3, 2, 4))
    y_pool = _pool_fwd(e, pw_b, pool_scale)
    o, lse = _attn_fwd(qkv, bias, nb)
    merged, y_attn = _merge_fwd(y_pool, o, e, wup_p, wup_a)
    dh1, loss_part, dgf = _head_fwd_bwd(merged, wout, x2, metapad, gf, target)
    loss = lax.psum(loss_part[0, 0], ("x", "y", "c"))

    dap, daa, dgate, dza, do, delta, dy_pool = _merge_bwd(dh1, wout, y_pool, y_attn, wup_p, wup_a, e, o)
    dpc, dzp, dscale, dpw = _pool_bwd_local(e, dy_pool, pw_b, pool_scale)
    du = _pool_bwd_window(dpc)
    dq, dk, dv, dc4, dcq = _attn_bwd(qkv, do, lse, delta, bias, nb)
    dc = jnp.transpose(dc4, (1, 3, 0, 2)).reshape(lk, N_HEADS)[:lp] + dcq[:, ::HEAD_DIM]
    df, db = _gates_bwd(jnp.pad(dc, ((0, 0), (0, F_COLS - N_HEADS))), f, bfg)
    dproj = jnp.concatenate([du, dzp, dza, dgate, dq.astype(BF16), dk[:lp], dv[:lp]], axis=1)
    tk = _row_tile(lp, 1100)
    dw_main = _mm_tn(hn, dproj, "grad_w_in_main", 512, 1024, tk)
    dw_f = _mm_tn(hn, df, "grad_w_in_forget", 512, F_COLS, tk)
    dw_out = _mm_tn(merged, dh1, "grad_w_out", 512, 1024, tk)
    dw_up_p = _mm_tn(y_pool, dap, "grad_w_up_pool", 512, 1024, tk)
    dw_up_a = _mm_tn(y_attn, daa, "grad_w_up_attn", 512, 1024, tk)
    grad_x, g_block0, dg1 = _input_bwd(dproj, df, w_main, w_f, x2, metapad, dh1, norm_g)
    dmeta = g_block0[PAD_ROWS:]

    m_ = lambda a, b: dw_main[:, a:b]
    dw_in = jnp.concatenate([m_(0, 512), m_(512, 1024), m_(3584, 4096) * (HEAD_DIM ** -0.5), m_(4096, 4608), m_(4608, 5120),
                             m_(1024, 1536), dw_f[:, :N_HEADS], m_(1536, 2560), m_(2560, 3584)], axis=1)
    sh_in = n_in // N_CHIPS
    sh_d = D_MODEL // N_CHIPS

    def split(a, rows, ncols):
        return jnp.transpose(a.reshape(2, rows // 2, N_CHIPS, ncols), (0, 2, 1, 3)).reshape(2, N_CHIPS, -1, LANES)

    parts = [split(dw_in, D_MODEL, sh_in), split(dw_up_p, POOL_WIDTH, sh_d), split(dw_up_a, ATTN_WIDTH, sh_d),
             jnp.transpose(dw_out.reshape(N_CHIPS, 2, sh_d // 2, D_MODEL), (1, 0, 2, 3)).reshape(2, N_CHIPS, -1, LANES),
             split(dmeta, N_META, sh_d)]
    sizes = [p.shape[2] for p in parts]
    pad_rows = (-sum(sizes)) % 16
    g_all = jnp.concatenate(parts + [jnp.zeros((2, N_CHIPS, pad_rows, LANES), F32)], axis=2)

    def pad8(a):
        return jnp.pad(a, ((0, (-a.shape[0]) % 8), (0, 0)))

    small_parts = [pad8(a) for a in (dg1.reshape(-1, LANES), dgf.reshape(-1, LANES), dscale.reshape(-1, LANES), db,
                                     dpw.reshape(-1, LANES))]
    small_sizes = [p.shape[0] for p in small_parts]
    small = jnp.concatenate(small_parts, axis=0)

    rb, sr = _sibling_exchange(g_all, small)
    h, sc = _add_sibling_half(g_all, rb, small, sr, core)
    q, sq = _chip_scatter(h, sc)
    t, st = _add_chips(q, sq)
    red = _sibling_allgather(t)

    offs = [0]
    for s_ in sizes:
        offs.append(offs[-1] + s_)
    piece = lambda k: red[:, offs[k]:offs[k + 1]]
    g_w_in = piece(0).reshape(D_MODEL, sh_in)
    g_w_up_p = piece(1).reshape(POOL_WIDTH, sh_d)
    g_w_up_a = piece(2).reshape(ATTN_WIDTH, sh_d)
    g_w_out = piece(3).reshape(sh_d, D_MODEL)
    g_meta = piece(4).reshape(N_META, sh_d)
    soffs = [0]
    for s_ in small_sizes:
        soffs.append(soffs[-1] + s_)
    spiece = lambda k, rows: st[soffs[k]:soffs[k] + rows]
    g_norm = spiece(0, D_MODEL // LANES).reshape(1, D_MODEL)
    g_final = spiece(1, D_MODEL // LANES).reshape(1, D_MODEL)
    g_scale = spiece(2, POOL_WIDTH // LANES).reshape(1, POOL_WIDTH)
    g_bf = spiece(3, 1)
    g_pw = spiece(4, POOL_WIDTH)

    def pad_lanes(a):
        return jnp.pad(a, ((0, 0), (0, F_COLS - N_HEADS)))

    upd = [
        ("meta_tokens", meta_tokens, g_meta, m_meta_tokens, v_meta_tokens),
        ("norm_g", norm_g, g_norm, m_norm_g, v_norm_g),
        ("w_in", w_in[0], g_w_in, m_w_in[0], v_w_in[0]),
        ("b_forget", pad_lanes(b_forget), g_bf, pad_lanes(m_b_forget), pad_lanes(v_b_forget)),
        ("pool_w", pool_w.reshape(-1, LANES), g_pw, m_pool_w.reshape(-1, LANES), v_pool_w.reshape(-1, LANES)),
        ("pool_scale", pool_scale, g_scale, m_pool_scale, v_pool_scale),
        ("w_up_pool", w_up_pool[0], g_w_up_p, m_w_up_pool[0], v_w_up_pool[0]),
        ("w_up_attn", w_up_attn[0], g_w_up_a, m_w_up_attn[0], v_w_up_attn[0]),
        ("w_out", w_out[0], g_w_out, m_w_out[0], v_w_out[0]),
        ("final_norm_g", gf, g_final, m_final_norm_g.reshape(1, D_MODEL), v_final_norm_g.reshape(1, D_MODEL)),
    ]
    shapes = [meta_tokens.shape, norm_g.shape, w_in.shape, b_forget.shape, pool_w.shape, pool_scale.shape,
              w_up_pool.shape, w_up_attn.shape, w_out.shape, final_norm_g.shape]
    grads, deltas, new_ms, new_vs = [], [], [], []
    for (name, w_, g_, m_in, v_in), shp in zip(upd, shapes):
        d_, mn_, vn_ = _adamw(w_, g_, m_in, v_in, "adamw_" + name)
        if name == "b_forget":
            g_, d_, mn_, vn_ = (a[:, :N_HEADS] for a in (g_, d_, mn_, vn_))
        grads.append(g_.reshape(shp))
        deltas.append(d_.reshape(shp))
        new_ms.append(mn_.reshape(shp))
        new_vs.append(vn_.reshape(shp))

    return (loss, grad_x.reshape(x.shape), *grads, *deltas, *new_ms, *new_vs)
```

```python
import jax
import jax.numpy as jnp
from jax import lax
from jax.experimental import pallas as pl
from jax.experimental.pallas import tpu as pltpu

F32 = jnp.float32
BF16 = jnp.bfloat16
MESH = pl.DeviceIdType.MESH
HIGHEST = lax.Precision.HIGHEST
HBM = pl.BlockSpec(memory_space=pltpu.HBM)

D_MODEL = 1024
N_META = 16
POOL_WIDTH = 512
POOL_GROUP = 128
POOL_WINDOWS = (2, 4, 8, 16)
N_HEADS = 8
HEAD_DIM = 64
ATTN_WIDTH = 512
RMS_EPS = 1e-6
N_CHIPS = 4

ADAM_LR = 0.001
ADAM_B1 = 0.9
ADAM_B2 = 0.999
ADAM_EPS = 1e-08
ADAM_WD = 0.01
ADAM_STEP = 10

LANES = 128
ROW_TILE = 256
KV_TILE = 512
PAD_ROWS = ROW_TILE - N_META
NEG = -1e30

E_U, E_ZP, E_ZA, E_GP, E_GA, E_COLS = 0, 512, 1024, 1536, 2560, 3584
QKV_COLS = 3 * ATTN_WIDTH
MAIN_COLS = E_COLS + QKV_COLS
F_COLS = LANES
REF_U, REF_Q, REF_ZA, REF_F, REF_GP = 0, 1024, 2560, 3072, 3080
QKV_ROW0 = REF_Q
VMEM_LIMIT = 48 * 1024 * 1024


def _params(sem=None, vmem=VMEM_LIMIT):
    return pltpu.CompilerParams(dimension_semantics=sem, vmem_limit_bytes=vmem)


def _in_hbm(*arrays):
    return [pltpu.with_memory_space_constraint(a, pltpu.HBM) for a in arrays]


def _row_tile(n, target):
    best = 16
    for t in range(16, target + 1, 16):
        if n % t == 0:
            best = t
    return best


def _sigmoid(x):
    return 1.0 / (1.0 + jnp.exp(-x))


def _half(ref, axis, which):
    n = ref.shape[axis] // 2
    idx = [slice(None)] * len(ref.shape)
    idx[axis] = pl.ds(pl.multiple_of(which * n, n), n)
    return ref.at[tuple(idx)]


def _half_shape(shape, axis):
    s = list(shape)
    s[axis] //= 2
    return tuple(s)


def _gather_shards(shards, axes):
    n = len(shards)

    def body(*refs):
        ins, outs = refs[:n], refs[n:2 * n]
        send, recv, fsend, frecv, local = refs[2 * n:]
        x, y, c = lax.axis_index("x"), lax.axis_index("y"), lax.axis_index("c")
        me = 2 * x + y
        sibling = (x, y, 1 - c)
        chips = [(1 - x, y), (x, 1 - y), (1 - x, 1 - y)]

        def over_ici(t, k, chip, dst_slot):
            return pltpu.make_async_remote_copy(
                src_ref=_half(ins[t], axes[t], c), dst_ref=_half(outs[t].at[dst_slot], axes[t], c),
                send_sem=send.at[3 * t + k], recv_sem=recv.at[3 * t + k], device_id=(*chip, c), device_id_type=MESH)

        def to_sibling(t, k, slot, which):
            return pltpu.make_async_remote_copy(
                src_ref=_half(outs[t].at[slot], axes[t], which), dst_ref=_half(outs[t].at[slot], axes[t], which),
                send_sem=fsend.at[3 * t + k], recv_sem=frecv.at[3 * t + k], device_id=sibling, device_id_type=MESH)

        started = []
        for t in range(n):
            mine = pltpu.make_async_copy(ins[t], outs[t].at[me], local.at[t])
            mine.start()
            started.append(mine)
        sent = []
        for t in range(n):
            for k, chip in enumerate(chips):
                cp = over_ici(t, k, chip, me)
                cp.start()
                sent.append(cp)
        for t in range(n):
            for k, (px, py) in enumerate(chips):
                over_ici(t, k, (px, py), 2 * px + py).wait_recv()
                fw = to_sibling(t, k, 2 * px + py, c)
                fw.start()
                sent.append(fw)
        for t in range(n):
            for k, (px, py) in enumerate(chips):
                to_sibling(t, k, 2 * px + py, 1 - c).wait_recv()
        for cp in sent:
            cp.wait_send()
        for cp in started:
            cp.wait()

    return pl.pallas_call(
        body, name="gather_weight_shards",
        in_specs=[HBM] * n, out_specs=[HBM] * n,
        out_shape=[jax.ShapeDtypeStruct((N_CHIPS,) + s.shape, s.dtype) for s in shards],
        scratch_shapes=[pltpu.SemaphoreType.DMA((3 * n,)), pltpu.SemaphoreType.DMA((3 * n,)),
                        pltpu.SemaphoreType.DMA((3 * n,)), pltpu.SemaphoreType.DMA((3 * n,)),
                        pltpu.SemaphoreType.DMA((n,))],
    )(*_in_hbm(*shards))


def _sibling_exchange(gs, axes, small):
    n = len(gs)

    def body(*refs):
        ins, s_ref = refs[:n], refs[n]
        outs, sr_ref = refs[n + 1:2 * n + 1], refs[2 * n + 1]
        send, recv = refs[2 * n + 2:]
        x, y, c = lax.axis_index("x"), lax.axis_index("y"), lax.axis_index("c")
        sib = (x, y, 1 - c)
        cps = [pltpu.make_async_remote_copy(src_ref=_half(ins[t], axes[t], 1 - c), dst_ref=outs[t], send_sem=send.at[t],
                                            recv_sem=recv.at[t], device_id=sib, device_id_type=MESH) for t in range(n)]
        cps.append(pltpu.make_async_remote_copy(src_ref=s_ref, dst_ref=sr_ref, send_sem=send.at[n], recv_sem=recv.at[n],
                                                device_id=sib, device_id_type=MESH))
        for cp in cps:
            cp.start()
        for cp in cps:
            cp.wait()

    return pl.pallas_call(
        body, name="grad_sibling_exchange",
        in_specs=[HBM] * (n + 1), out_specs=[HBM] * (n + 1),
        out_shape=[jax.ShapeDtypeStruct(_half_shape(g.shape, a), g.dtype) for g, a in zip(gs, axes)]
        + [jax.ShapeDtypeStruct(small.shape, small.dtype)],
        scratch_shapes=[pltpu.SemaphoreType.DMA((n + 1,)), pltpu.SemaphoreType.DMA((n + 1,))],
    )(*_in_hbm(*gs, small))


def _add_sibling_half(gs, rbs, axes, small, sr, core):
    n = len(gs)

    def body(core_ref, *refs):
        g_refs, rb_refs = refs[:n], refs[n:2 * n]
        s_ref, sr_ref = refs[2 * n], refs[2 * n + 1]
        h_refs, sc_ref = refs[2 * n + 2:3 * n + 2], refs[3 * n + 2]
        for t in range(n):
            h_refs[t][...] = (g_refs[t][...] + rb_refs[t][...]).astype(BF16)
        sc_ref[...] = s_ref[...] + sr_ref[...]

    def mine(rb, axis):
        blk = (None,) + rb.shape[1:]
        if axis == 2:
            return pl.BlockSpec(blk, lambda j, cr: (j, 0, cr[0]))
        return pl.BlockSpec(blk, lambda j, cr: (j, cr[0], 0))

    chunk = lambda rb: pl.BlockSpec((None,) + rb.shape[1:], lambda j, cr: (j, 0, 0))
    whole = pl.BlockSpec(small.shape, lambda j, cr: (0, 0))
    return pl.pallas_call(
        body, name="grad_add_sibling",
        grid_spec=pltpu.PrefetchScalarGridSpec(
            num_scalar_prefetch=1, grid=(N_CHIPS,),
            in_specs=[mine(rb, a) for rb, a in zip(rbs, axes)] + [chunk(rb) for rb in rbs] + [whole, whole],
            out_specs=[chunk(rb) for rb in rbs] + [whole]),
        out_shape=[jax.ShapeDtypeStruct(rb.shape, BF16) for rb in rbs] + [jax.ShapeDtypeStruct(small.shape, F32)],
        compiler_params=_params(("arbitrary",)),
    )(core, *gs, *rbs, small, sr)


def _chip_scatter(hs, sc):
    n = len(hs)

    def body(*refs):
        h_refs, s_ref = refs[:n], refs[n]
        q_refs, sq_ref = refs[n + 1:2 * n + 1], refs[2 * n + 1]
        send, recv, local = refs[2 * n + 2:]
        x, y, c = lax.axis_index("x"), lax.axis_index("y"), lax.axis_index("c")
        me = 2 * x + y
        chips = [(1 - x, y), (x, 1 - y), (1 - x, 1 - y)]

        def copy(t, k, chip, src_slot, dst_slot):
            src = h_refs[t].at[src_slot] if t < n else s_ref
            dst = (q_refs[t] if t < n else sq_ref).at[dst_slot]
            return pltpu.make_async_remote_copy(src_ref=src, dst_ref=dst, send_sem=send.at[3 * t + k],
                                                recv_sem=recv.at[3 * t + k], device_id=(*chip, c), device_id_type=MESH)

        own = [pltpu.make_async_copy(h_refs[t].at[me], q_refs[t].at[me], local.at[t]) for t in range(n)]
        own.append(pltpu.make_async_copy(s_ref, sq_ref.at[me], local.at[n]))
        for cp in own:
            cp.start()
        sent = []
        for t in range(n + 1):
            for k, (px, py) in enumerate(chips):
                cp = copy(t, k, (px, py), 2 * px + py, me)
                cp.start()
                sent.append(cp)
        for t in range(n + 1):
            for k, (px, py) in enumerate(chips):
                copy(t, k, (px, py), me, 2 * px + py).wait_recv()
        for cp in sent:
            cp.wait_send()
        for cp in own:
            cp.wait()

    return pl.pallas_call(
        body, name="grad_chip_scatter",
        in_specs=[HBM] * (n + 1), out_specs=[HBM] * (n + 1),
        out_shape=[jax.ShapeDtypeStruct(h.shape, h.dtype) for h in hs]
        + [jax.ShapeDtypeStruct((N_CHIPS,) + sc.shape, sc.dtype)],
        scratch_shapes=[pltpu.SemaphoreType.DMA((3 * (n + 1),)), pltpu.SemaphoreType.DMA((3 * (n + 1),)),
                        pltpu.SemaphoreType.DMA((n + 1,))],
    )(*_in_hbm(*hs, sc))


def _reduce_allgather(qs, axes, sq):
    n = len(qs)
    shard_shapes = [tuple(d * 2 if i == a - 1 else d for i, d in enumerate(q.shape[1:])) for q, a in zip(qs, axes)]

    def body(*refs):
        q_refs, sq_ref = refs[:n], refs[n]
        o_refs, st_ref = refs[n + 1:2 * n + 1], refs[2 * n + 1]
        send, recv = refs[2 * n + 2:]
        x, y, c = lax.axis_index("x"), lax.axis_index("y"), lax.axis_index("c")

        def swap(t, which):
            return pltpu.make_async_remote_copy(
                src_ref=_half(o_refs[t], axes[t] - 1, which), dst_ref=_half(o_refs[t], axes[t] - 1, which),
                send_sem=send.at[t], recv_sem=recv.at[t], device_id=(x, y, 1 - c), device_id_type=MESH)

        sent = []
        for t in range(n):
            q = q_refs[t]
            total = ((q[0].astype(F32) + q[1].astype(F32)) + q[2].astype(F32)) + q[3].astype(F32)
            _half(o_refs[t], axes[t] - 1, c)[...] = total
            cp = swap(t, c)
            cp.start()
            sent.append(cp)
        st_ref[...] = ((sq_ref[0] + sq_ref[1]) + sq_ref[2]) + sq_ref[3]
        for t in range(n):
            swap(t, 1 - c).wait_recv()
        for cp in sent:
            cp.wait_send()

    vmem = pl.BlockSpec(memory_space=pltpu.VMEM)
    return pl.pallas_call(
        body, name="grad_reduce_allgather",
        in_specs=[vmem] * (n + 1), out_specs=[vmem] * (n + 1),
        out_shape=[jax.ShapeDtypeStruct(s, F32) for s in shard_shapes] + [jax.ShapeDtypeStruct(sq.shape[1:], F32)],
        scratch_shapes=[pltpu.SemaphoreType.DMA((n,)), pltpu.SemaphoreType.DMA((n,))],
        compiler_params=_params(),
    )(*qs, sq)


def _dot_nt(a, b):
    return lax.dot_general(a, b, (((1,), (1,)), ((), ())), preferred_element_type=F32)


def _dot_tn(a, b):
    return lax.dot_general(a, b, (((0,), (0,)), ((), ())), preferred_element_type=F32)


def _mm_nt(a, bt, out_dtype, name, m, n, tm, tn, row_block=0, scale_first=None):
    k = a.shape[1]

    def body(a_ref, b_ref, o_ref):
        r = _dot_nt(a_ref[...], b_ref[...])
        if scale_first is not None:
            r = r * jnp.where(pl.program_id(0) == 0, scale_first, 1.0)
        o_ref[...] = r.astype(out_dtype)

    return pl.pallas_call(
        body, name=name, grid=(n // tn, m // tm),
        in_specs=[pl.BlockSpec((tm, k), lambda j, i: (i, 0)), pl.BlockSpec((tn, k), lambda j, i: (row_block + j, 0))],
        out_specs=pl.BlockSpec((tm, tn), lambda j, i: (i, j)),
        out_shape=jax.ShapeDtypeStruct((m, n), out_dtype),
        compiler_params=_params(("parallel", "parallel")),
    )(a, bt)


def _mm_tn(a, b, name, k, tm, tn, tk, chunks=1):
    m, n = a.shape[1], b.shape[1]

    def body(a_ref, b_ref, o_ref):
        @pl.when(pl.program_id(2) == 0)
        def _():
            o_ref[...] = jnp.zeros_like(o_ref)
        o_ref[...] += _dot_tn(a_ref[...].astype(BF16), b_ref[...].astype(BF16))

    if chunks > 1:
        assert tn * chunks == n
        out_spec = pl.BlockSpec((None, tm, tn), lambda i, j, kk: (j, i, 0))
        out_shape = jax.ShapeDtypeStruct((chunks, m, tn), F32)
    else:
        out_spec = pl.BlockSpec((tm, tn), lambda i, j, kk: (i, j))
        out_shape = jax.ShapeDtypeStruct((m, n), F32)
    return pl.pallas_call(
        body, name=name, grid=(m // tm, n // tn, k // tk),
        in_specs=[pl.BlockSpec((tk, tm), lambda i, j, kk: (kk, i)), pl.BlockSpec((tk, tn), lambda i, j, kk: (kk, j))],
        out_specs=out_spec, out_shape=out_shape,
        compiler_params=_params(("parallel", "parallel", "arbitrary")),
    )(a, b)


def _tokens_spec():
    return pl.BlockSpec((ROW_TILE, D_MODEL), lambda i: (jnp.maximum(i - 1, 0), 0))


def _rmsnorm_fwd(x2, metapad, g1, lk):
    nb = x2.shape[0] // ROW_TILE + 1

    def body(x_ref, mp_ref, g_ref, hn_ref):
        i = pl.program_id(0)
        h = jnp.where(i == 0, mp_ref[...], jnp.where(i >= nb, 0.0, x_ref[...]))
        r = lax.rsqrt(jnp.mean(h * h, axis=-1, keepdims=True) + RMS_EPS)
        hn_ref[...] = ((h * r) * g_ref[...]).astype(BF16)

    return pl.pallas_call(
        body, name="rmsnorm_fwd", grid=(lk // ROW_TILE,),
        in_specs=[pl.BlockSpec((ROW_TILE, D_MODEL), lambda i: (jnp.clip(i - 1, 0, nb - 2), 0)),
                  pl.BlockSpec((ROW_TILE, D_MODEL), lambda i: (0, 0)),
                  pl.BlockSpec((1, D_MODEL), lambda i: (0, 0))],
        out_specs=pl.BlockSpec((ROW_TILE, D_MODEL), lambda i: (i, 0)),
        out_shape=jax.ShapeDtypeStruct((lk, D_MODEL), BF16),
        compiler_params=_params(("parallel",)),
    )(x2, metapad, g1)


def _valid_gate_mask(i):
    row = i * ROW_TILE + lax.broadcasted_iota(jnp.int32, (ROW_TILE, F_COLS), 0)
    col = lax.broadcasted_iota(jnp.int32, (ROW_TILE, F_COLS), 1)
    return (row >= PAD_ROWS) & (col < N_HEADS)


def _gates_fwd(f, bfg):
    nb = f.shape[0] // ROW_TILE

    def body(f_ref, b_ref, c_ref, carry):
        i = pl.program_id(0)

        @pl.when(i == 0)
        def _():
            carry[...] = jnp.zeros_like(carry)

        logit = f_ref[...] + b_ref[...]
        lf = jnp.minimum(logit, 0.0) - jnp.log1p(jnp.exp(-jnp.abs(logit)))
        lf = jnp.where(_valid_gate_mask(i), lf, 0.0)
        r_i = lax.broadcasted_iota(jnp.int32, (ROW_TILE, ROW_TILE), 0)
        c_i = lax.broadcasted_iota(jnp.int32, (ROW_TILE, ROW_TILE), 1)
        tri = (c_i <= r_i).astype(F32)
        c_ref[...] = jnp.dot(tri, lf, precision=HIGHEST, preferred_element_type=F32) + carry[...]
        carry[...] = carry[...] + jnp.sum(lf, axis=0, keepdims=True)

    return pl.pallas_call(
        body, name="gates_fwd", grid=(nb,),
        in_specs=[pl.BlockSpec((ROW_TILE, F_COLS), lambda i: (i, 0)), pl.BlockSpec((1, F_COLS), lambda i: (0, 0))],
        out_specs=pl.BlockSpec((ROW_TILE, F_COLS), lambda i: (i, 0)),
        out_shape=jax.ShapeDtypeStruct(f.shape, F32),
        scratch_shapes=[pltpu.VMEM((1, F_COLS), F32)],
        compiler_params=_params(("arbitrary",)),
    )(f, bfg)


def _pool_counts(i):
    row = i * ROW_TILE + lax.broadcasted_iota(jnp.int32, (ROW_TILE, 1), 0)
    return jnp.maximum(row - PAD_ROWS, 0)


def _trailing_sums(xc, levels):
    acc = xc
    for lv in range(levels):
        acc = acc + pltpu.roll(acc, 1 << lv, 0)
    return acc


def _leading_sums(xc, levels):
    n = xc.shape[0]
    acc = xc
    for lv in range(levels):
        acc = acc + pltpu.roll(acc, n - (1 << lv), 0)
    return acc


def _pool_p(u_cur, u_prev, i):
    pos = _pool_counts(i)
    ps, invs = [], []
    for g, w in enumerate(POOL_WINDOWS):
        sl = slice(g * POOL_GROUP, (g + 1) * POOL_GROUP)
        cur = u_cur[:, sl]
        xc = jnp.concatenate([u_prev[:, sl], cur], axis=0)
        win = _trailing_sums(xc, g + 1)[ROW_TILE:, :]
        inv = 1.0 / jnp.minimum(pos + 1, w).astype(F32)
        ps.append(win * inv - cur)
        invs.append(inv)
    return ps, invs


def _pool_fwd(e, pw, scale):
    nb = e.shape[0] // ROW_TILE

    def body(uc_ref, up_ref, z_ref, pw_ref, sc_ref, y_ref):
        i = pl.program_id(0)
        u_cur = uc_ref[...]
        u_prev = jnp.where(i == 0, 0.0, up_ref[...])
        ps, _ = _pool_p(u_cur, u_prev, i)
        z = z_ref[...]
        gate = z * _sigmoid(z)
        for g in range(len(POOL_WINDOWS)):
            sl = slice(g * POOL_GROUP, (g + 1) * POOL_GROUP)
            yraw = jnp.dot(ps[g].astype(BF16), pw_ref[g], preferred_element_type=F32)
            y_ref[:, sl] = ((yraw * sc_ref[:, sl]) * gate[:, sl]).astype(BF16)

    blk = (ROW_TILE, POOL_WIDTH)
    return pl.pallas_call(
        body, name="pool_fwd", grid=(nb,),
        in_specs=[pl.BlockSpec(blk, lambda i: (i, 0)), pl.BlockSpec(blk, lambda i: (jnp.maximum(i - 1, 0), 0)),
                  pl.BlockSpec(blk, lambda i: (i, 1)),
                  pl.BlockSpec((len(POOL_WINDOWS), POOL_GROUP, POOL_GROUP), lambda i: (0, 0, 0)),
                  pl.BlockSpec((1, POOL_WIDTH), lambda i: (0, 0))],
        out_specs=pl.BlockSpec(blk, lambda i: (i, 0)),
        out_shape=jax.ShapeDtypeStruct((e.shape[0], POOL_WIDTH), BF16),
        compiler_params=_params(("parallel",)),
    )(e, e, e, pw, scale)


def _stack_heads(a):
    first = lax.broadcasted_iota(jnp.int32, a.shape, 1) < HEAD_DIM
    zero = jnp.zeros_like(a)
    return jnp.concatenate([jnp.where(first, a, zero), jnp.where(first, zero, a)], axis=0)


def _unstack_heads(a):
    rows = a.shape[0] // 2
    first = lax.broadcasted_iota(jnp.int32, (rows, LANES), 1) < HEAD_DIM
    return jnp.where(first, a[:rows], a[rows:])


def _causal(i, jj, stacked):
    r = lax.broadcasted_iota(jnp.int32, (stacked * ROW_TILE, KV_TILE), 0)
    if stacked == 2:
        r = jnp.where(r >= ROW_TILE, r - ROW_TILE, r)
    kidx = jj * KV_TILE + lax.broadcasted_iota(jnp.int32, (stacked * ROW_TILE, KV_TILE), 1)
    return kidx <= i * ROW_TILE + r


def _stack_rows(b):
    n = b.shape[1]
    return jnp.concatenate([jnp.broadcast_to(b[0:1], (ROW_TILE, n)), jnp.broadcast_to(b[1:2], (ROW_TILE, n))], axis=0)


def _attn_fwd(qkv, bias, nb):
    lk = qkv.shape[0]
    lp = nb * ROW_TILE
    nkb = lk // KV_TILE
    n_pairs = N_HEADS // 2

    def body(q_ref, k_ref, v_ref, b_ref, o_ref, lse_ref):
        i = pl.program_id(1)
        qs = _stack_heads(q_ref[...])
        last = (i * ROW_TILE) // KV_TILE

        def block(jj, carry, masked):
            m, l, acc = carry
            rows = pl.ds(pl.multiple_of(jj * KV_TILE, KV_TILE), KV_TILE)
            s = _dot_nt(qs, k_ref[rows, :]) - _stack_rows(b_ref[0, 0, jj])
            if masked:
                s = jnp.where(_causal(i, jj, 2), s, NEG)
            m_new = jnp.maximum(m, jnp.max(s, axis=1, keepdims=True))
            alpha = jnp.exp(m - m_new)
            p = jnp.exp(s - m_new)
            l = alpha * l + jnp.sum(p, axis=1, keepdims=True)
            acc = alpha * acc + jnp.dot(p.astype(BF16), v_ref[rows, :], preferred_element_type=F32)
            return m_new, l, acc

        init = (jnp.full((2 * ROW_TILE, 1), NEG, F32), jnp.zeros((2 * ROW_TILE, 1), F32),
                jnp.zeros((2 * ROW_TILE, LANES), F32))
        carry = lax.fori_loop(0, last, lambda jj, c: block(jj, c, False), init)
        m, l, acc = block(last, carry, True)
        o_ref[...] = _unstack_heads(acc / l)
        lse_ref[...] = _unstack_heads(jnp.broadcast_to(m + jnp.log(l), (2 * ROW_TILE, LANES)))

    return pl.pallas_call(
        body, name="attn_fwd", grid=(n_pairs, nb),
        in_specs=[pl.BlockSpec((ROW_TILE, LANES), lambda hp, i: (i, hp)),
                  pl.BlockSpec((lk, LANES), lambda hp, i: (0, n_pairs + hp)),
                  pl.BlockSpec((lk, LANES), lambda hp, i: (0, 2 * n_pairs + hp)),
                  pl.BlockSpec((1, 1, nkb, 2, KV_TILE), lambda hp, i: (i, hp, 0, 0, 0))],
        out_specs=[pl.BlockSpec((ROW_TILE, LANES), lambda hp, i: (i, hp)),
                   pl.BlockSpec((ROW_TILE, LANES), lambda hp, i: (i, hp))],
        out_shape=[jax.ShapeDtypeStruct((lp, ATTN_WIDTH), F32), jax.ShapeDtypeStruct((lp, ATTN_WIDTH), F32)],
        compiler_params=_params(("parallel", "parallel")),
    )(qkv, qkv, qkv, bias)


def _merge_fwd(y_pool, o, e, wup_p, wup_a):
    lp = o.shape[0]
    nb = lp // ROW_TILE

    def body(yp_ref, o_ref, e_ref, wp_ref, wa_ref, mg_ref, ya_ref):
        za = e_ref[:, E_ZA:E_GP]
        ya = (o_ref[...] * (za * _sigmoid(za))).astype(BF16)
        ya_ref[...] = ya
        a_pool = jnp.dot(yp_ref[...], wp_ref[...], preferred_element_type=F32)
        a_attn = jnp.dot(ya, wa_ref[...], preferred_element_type=F32)
        mg_ref[...] = (_sigmoid(e_ref[:, E_GP:E_GA]) * a_pool + _sigmoid(e_ref[:, E_GA:E_COLS]) * a_attn).astype(BF16)

    return pl.pallas_call(
        body, name="merge_fwd", grid=(nb,),
        in_specs=[pl.BlockSpec((ROW_TILE, POOL_WIDTH), lambda i: (i, 0)),
                  pl.BlockSpec((ROW_TILE, ATTN_WIDTH), lambda i: (i, 0)),
                  pl.BlockSpec((ROW_TILE, E_COLS), lambda i: (i, 0)),
                  pl.BlockSpec((POOL_WIDTH, D_MODEL), lambda i: (0, 0)),
                  pl.BlockSpec((ATTN_WIDTH, D_MODEL), lambda i: (0, 0))],
        out_specs=[pl.BlockSpec((ROW_TILE, D_MODEL), lambda i: (i, 0)),
                   pl.BlockSpec((ROW_TILE, ATTN_WIDTH), lambda i: (i, 0))],
        out_shape=[jax.ShapeDtypeStruct((lp, D_MODEL), BF16), jax.ShapeDtypeStruct((lp, ATTN_WIDTH), BF16)],
        compiler_params=_params(("parallel",)),
    )(y_pool, o, e, wup_p, wup_a)


def _head_fwd_bwd(merged, w_out, x2, metapad, gf, target):
    lp = merged.shape[0]
    nb = lp // ROW_TILE

    def body(mg_ref, w_ref, x_ref, mp_ref, g_ref, t_ref, dh_ref, loss_ref, dg_ref):
        i = pl.program_id(0)

        @pl.when(i == 0)
        def _():
            loss_ref[...] = jnp.zeros_like(loss_ref)
            dg_ref[...] = jnp.zeros_like(dg_ref)

        h0 = jnp.where(i == 0, mp_ref[...], x_ref[...])
        h1 = h0 + jnp.dot(mg_ref[...], w_ref[...], preferred_element_type=F32)
        r = lax.rsqrt(jnp.mean(h1 * h1, axis=-1, keepdims=True) + RMS_EPS)
        xhat = h1 * r
        g = g_ref[...]
        err = jnp.where(i == 0, 0.0, xhat * g - t_ref[...])
        loss_ref[...] += 0.5 * jnp.sum(jnp.mean(err * err, axis=-1, keepdims=True))
        dy = err / D_MODEL
        dg_ref[...] += jnp.sum(dy * xhat, axis=0, keepdims=True)
        dxhat = dy * g
        dh_ref[...] = r * (dxhat - xhat * jnp.mean(dxhat * xhat, axis=-1, keepdims=True))

    return pl.pallas_call(
        body, name="head_fwd_bwd", grid=(nb,),
        in_specs=[pl.BlockSpec((ROW_TILE, D_MODEL), lambda i: (i, 0)),
                  pl.BlockSpec((D_MODEL, D_MODEL), lambda i: (0, 0)),
                  _tokens_spec(), pl.BlockSpec((ROW_TILE, D_MODEL), lambda i: (0, 0)),
                  pl.BlockSpec((1, D_MODEL), lambda i: (0, 0)), _tokens_spec()],
        out_specs=[pl.BlockSpec((ROW_TILE, D_MODEL), lambda i: (i, 0)),
                   pl.BlockSpec((1, LANES), lambda i: (0, 0)), pl.BlockSpec((1, D_MODEL), lambda i: (0, 0))],
        out_shape=[jax.ShapeDtypeStruct((lp, D_MODEL), F32), jax.ShapeDtypeStruct((1, LANES), F32),
                   jax.ShapeDtypeStruct((1, D_MODEL), F32)],
        compiler_params=_params(("arbitrary",)),
    )(merged, w_out, x2, metapad, gf, target)


def _per_head_rowsum(t):
    head = lax.broadcasted_iota(jnp.int32, t.shape, 1) // HEAD_DIM
    out = jnp.zeros_like(t)
    for h in range(N_HEADS):
        sel = head == h
        out = jnp.where(sel, jnp.sum(jnp.where(sel, t, 0.0), axis=1, keepdims=True), out)
    return out


def _merge_bwd(dh1, w_out, y_pool, y_attn, wup_p, wup_a, e, o):
    lp = o.shape[0]
    nb = lp // ROW_TILE

    def body(dh_ref, wo_ref, yp_ref, ya_ref, wp_ref, wa_ref, e_ref, o_ref,
             dap_ref, daa_ref, dg_ref, dza_ref, do_ref, delta_ref, dyp_ref):
        dmerged = _dot_nt(dh_ref[...].astype(BF16), wo_ref[...])
        a_pool = jnp.dot(yp_ref[...], wp_ref[...], preferred_element_type=F32)
        a_attn = jnp.dot(ya_ref[...], wa_ref[...], preferred_element_type=F32)
        sp = _sigmoid(e_ref[:, E_GP:E_GA])
        sa = _sigmoid(e_ref[:, E_GA:E_COLS])
        dap = (dmerged * sp).astype(BF16)
        daa = (dmerged * sa).astype(BF16)
        dap_ref[...] = dap
        daa_ref[...] = daa
        dg_ref[:, :D_MODEL] = (dmerged * a_pool * (sp * (1.0 - sp))).astype(BF16)
        dg_ref[:, D_MODEL:] = (dmerged * a_attn * (sa * (1.0 - sa))).astype(BF16)
        dyp_ref[...] = _dot_nt(dap, wp_ref[...])
        dya = _dot_nt(daa, wa_ref[...])
        za = e_ref[:, E_ZA:E_GP]
        sz = _sigmoid(za)
        o = o_ref[...]
        do = dya * (za * sz)
        do_ref[...] = do.astype(BF16)
        dza_ref[...] = (dya * o * (sz * (1.0 + za * (1.0 - sz)))).astype(BF16)
        delta_ref[...] = _per_head_rowsum(do * o)

    row = lambda w: pl.BlockSpec((ROW_TILE, w), lambda i: (i, 0))
    full = lambda a: pl.BlockSpec(a.shape, lambda i: (0, 0))
    return pl.pallas_call(
        body, name="merge_bwd", grid=(nb,),
        in_specs=[row(D_MODEL), full(w_out), row(POOL_WIDTH), row(ATTN_WIDTH), full(wup_p), full(wup_a),
                  row(E_COLS), row(ATTN_WIDTH)],
        out_specs=[row(D_MODEL), row(D_MODEL), row(2 * D_MODEL), row(ATTN_WIDTH), row(ATTN_WIDTH),
                   row(ATTN_WIDTH), row(POOL_WIDTH)],
        out_shape=[jax.ShapeDtypeStruct((lp, D_MODEL), BF16), jax.ShapeDtypeStruct((lp, D_MODEL), BF16),
                   jax.ShapeDtypeStruct((lp, 2 * D_MODEL), BF16), jax.ShapeDtypeStruct((lp, ATTN_WIDTH), BF16),
                   jax.ShapeDtypeStruct((lp, ATTN_WIDTH), BF16), jax.ShapeDtypeStruct((lp, ATTN_WIDTH), F32),
                   jax.ShapeDtypeStruct((lp, POOL_WIDTH), F32)],
        compiler_params=_params(("parallel",)),
    )(dh1, w_out, y_pool, y_attn, wup_p, wup_a, e, o)


def _pool_bwd_local(e, dy_pool, pw, scale):
    lp = e.shape[0]
    nb = lp // ROW_TILE
    ng = len(POOL_WINDOWS)

    def body(uc_ref, up_ref, z_ref, dy_ref, pw_ref, sc_ref, dpc_ref, dz_ref, dsc_ref, dpw_ref):
        i = pl.program_id(0)

        @pl.when(i == 0)
        def _():
            dsc_ref[...] = jnp.zeros_like(dsc_ref)
            dpw_ref[...] = jnp.zeros_like(dpw_ref)

        u_cur = uc_ref[...]
        u_prev = jnp.where(i == 0, 0.0, up_ref[...])
        ps, invs = _pool_p(u_cur, u_prev, i)
        z = z_ref[...]
        sz = _sigmoid(z)
        dy = dy_ref[...]
        dypre = dy * (z * sz)
        dsilu = sz * (1.0 + z * (1.0 - sz))
        for g in range(ng):
            sl = slice(g * POOL_GROUP, (g + 1) * POOL_GROUP)
            pb = ps[g].astype(BF16)
            w = pw_ref[g]
            yraw = jnp.dot(pb, w, preferred_element_type=F32)
            sc = sc_ref[:, sl]
            dz_ref[:, sl] = (dy[:, sl] * (yraw * sc) * dsilu[:, sl]).astype(BF16)
            dsc_ref[:, sl] += jnp.sum(dypre[:, sl] * yraw, axis=0, keepdims=True)
            dyraw = (dypre[:, sl] * sc).astype(BF16)
            dpw_ref[g] += _dot_tn(pb, dyraw)
            dpc_ref[:, sl] = _dot_nt(dyraw, w) * invs[g]

    blk = (ROW_TILE, POOL_WIDTH)
    return pl.pallas_call(
        body, name="pool_bwd_local", grid=(nb,),
        in_specs=[pl.BlockSpec(blk, lambda i: (i, 0)), pl.BlockSpec(blk, lambda i: (jnp.maximum(i - 1, 0), 0)),
                  pl.BlockSpec(blk, lambda i: (i, 1)), pl.BlockSpec(blk, lambda i: (i, 0)),
                  pl.BlockSpec((ng, POOL_GROUP, POOL_GROUP), lambda i: (0, 0, 0)),
                  pl.BlockSpec((1, POOL_WIDTH), lambda i: (0, 0))],
        out_specs=[pl.BlockSpec(blk, lambda i: (i, 0)), pl.BlockSpec(blk, lambda i: (i, 0)),
                   pl.BlockSpec((1, POOL_WIDTH), lambda i: (0, 0)),
                   pl.BlockSpec((ng, POOL_GROUP, POOL_GROUP), lambda i: (0, 0, 0))],
        out_shape=[jax.ShapeDtypeStruct((lp, POOL_WIDTH), F32), jax.ShapeDtypeStruct((lp, POOL_WIDTH), BF16),
                   jax.ShapeDtypeStruct((1, POOL_WIDTH), F32),
                   jax.ShapeDtypeStruct((ng, POOL_GROUP, POOL_GROUP), F32)],
        compiler_params=_params(("arbitrary",)),
    )(e, e, e, dy_pool, pw, scale)


def _pool_bwd_window(dpc):
    lp = dpc.shape[0]
    nb = lp // ROW_TILE

    def body(cur_ref, nxt_ref, du_ref):
        i = pl.program_id(0)
        cur = cur_ref[...]
        nxt = jnp.where(i == nb - 1, 0.0, nxt_ref[...])
        pos = _pool_counts(i)
        for g, w in enumerate(POOL_WINDOWS):
            sl = slice(g * POOL_GROUP, (g + 1) * POOL_GROUP)
            xc = jnp.concatenate([cur[:, sl], nxt[:, sl]], axis=0)
            win = _leading_sums(xc, g + 1)[:ROW_TILE, :]
            dp = cur[:, sl] * jnp.minimum(pos + 1, w).astype(F32)
            du_ref[:, sl] = (win - dp).astype(BF16)

    blk = (ROW_TILE, POOL_WIDTH)
    return pl.pallas_call(
        body, name="pool_bwd_window", grid=(nb,),
        in_specs=[pl.BlockSpec(blk, lambda i: (i, 0)), pl.BlockSpec(blk, lambda i: (jnp.minimum(i + 1, nb - 1), 0))],
        out_specs=pl.BlockSpec(blk, lambda i: (i, 0)),
        out_shape=jax.ShapeDtypeStruct((lp, POOL_WIDTH), BF16),
        compiler_params=_params(("parallel",)),
    )(dpc, dpc)


def _attn_bwd(qkv, do, lse, delta, bias, nb):
    lk = qkv.shape[0]
    lp = nb * ROW_TILE
    nkb = lk // KV_TILE
    n_pairs = N_HEADS // 2
    per_kv = KV_TILE // ROW_TILE

    def body(q_ref, k_ref, v_ref, do_ref, lse_ref, dl_ref, b_ref, dq_ref, dk_ref, dv_ref, dc_ref, dcq_ref):
        jj = pl.program_id(1)

        @pl.when(jj == 0)
        def _():
            dq_ref[...] = jnp.zeros_like(dq_ref)
            dcq_ref[...] = jnp.zeros_like(dcq_ref)

        kb, vb = k_ref[...], v_ref[...]

        def block(i, carry, masked):
            dk_acc, dv_acc, dc_acc = carry
            rows = pl.ds(pl.multiple_of(i * ROW_TILE, ROW_TILE), ROW_TILE)
            qs = _stack_heads(q_ref[rows, :])
            dos = _stack_heads(do_ref[rows, :])
            lse_i, dl_i = lse_ref[rows, :], dl_ref[rows, :]
            s = _dot_nt(qs, kb)
            dp = _dot_nt(dos, vb)
            if masked:
                valid = _causal(i, jj, 1)
            ps, dss, dcs, rowsums = [], [], [], []
            for hd in range(2):
                half = slice(hd * ROW_TILE, (hd + 1) * ROW_TILE)
                col = slice(hd * HEAD_DIM, hd * HEAD_DIM + 1)
                sh = s[half] - b_ref[i, 0, 0, hd:hd + 1, :]
                if masked:
                    sh = jnp.where(valid, sh, NEG)
                p = jnp.exp(sh - lse_i[:, col])
                ds = p * (dp[half] - dl_i[:, col])
                ps.append(p.astype(BF16))
                dss.append(ds.astype(BF16))
                dcs.append(jnp.sum(ds, axis=0, keepdims=True))
                rowsums.append(jnp.sum(ds, axis=1, keepdims=True))
            dsb = jnp.concatenate(dss, axis=0)
            dv_acc = dv_acc + _dot_tn(jnp.concatenate(ps, axis=0), dos)
            dk_acc = dk_acc + _dot_tn(dsb, qs)
            dc_acc = dc_acc - jnp.concatenate(dcs, axis=0)
            dq_ref[rows, :] += _unstack_heads(jnp.dot(dsb, kb, preferred_element_type=F32))
            dcq_ref[rows, :] += _unstack_heads(jnp.broadcast_to(jnp.concatenate(rowsums, axis=0), (2 * ROW_TILE, LANES)))
            return dk_acc, dv_acc, dc_acc

        init = (jnp.zeros((KV_TILE, LANES), F32), jnp.zeros((KV_TILE, LANES), F32), jnp.zeros((2, KV_TILE), F32))
        first_q = per_kv * jj
        diag_end = jnp.minimum(first_q + per_kv, nb)
        carry = lax.fori_loop(first_q, diag_end, lambda i, c: block(i, c, True), init)
        dk, dv, dc = lax.fori_loop(diag_end, nb, lambda i, c: block(i, c, False), carry)
        dk_ref[...] = dk.astype(BF16)
        dv_ref[...] = dv.astype(BF16)
        dc_ref[0, 0] = dc

    whole = lambda rows: pl.BlockSpec((rows, LANES), lambda hp, jj: (0, hp))
    kv_blk = lambda off: pl.BlockSpec((KV_TILE, LANES), lambda hp, jj: (jj, off + hp))
    return pl.pallas_call(
        body, name="attn_bwd", grid=(n_pairs, nkb),
        in_specs=[whole(lk), kv_blk(n_pairs), kv_blk(2 * n_pairs), whole(lp), whole(lp), whole(lp),
                  pl.BlockSpec((nb, 1, 1, 2, KV_TILE), lambda hp, jj: (0, hp, jj, 0, 0))],
        out_specs=[whole(lp), kv_blk(0), kv_blk(0),
                   pl.BlockSpec((1, 1, 2, KV_TILE), lambda hp, jj: (hp, jj, 0, 0)), whole(lp)],
        out_shape=[jax.ShapeDtypeStruct((lp, ATTN_WIDTH), F32), jax.ShapeDtypeStruct((lk, ATTN_WIDTH), BF16),
                   jax.ShapeDtypeStruct((lk, ATTN_WIDTH), BF16),
                   jax.ShapeDtypeStruct((n_pairs, nkb, 2, KV_TILE), F32),
                   jax.ShapeDtypeStruct((lp, ATTN_WIDTH), F32)],
        compiler_params=_params(("parallel", "arbitrary")),
    )(qkv, qkv, qkv, do, lse, delta, bias)


def _gates_bwd(dc, f, bfg):
    nb = f.shape[0] // ROW_TILE

    def body(dc_ref, f_ref, b_ref, df_ref, db_ref, carry):
        step = pl.program_id(0)
        i = nb - 1 - step

        @pl.when(step == 0)
        def _():
            carry[...] = jnp.zeros_like(carry)
            db_ref[...] = jnp.zeros_like(db_ref)

        dcb = dc_ref[...]
        r_i = lax.broadcasted_iota(jnp.int32, (ROW_TILE, ROW_TILE), 0)
        c_i = lax.broadcasted_iota(jnp.int32, (ROW_TILE, ROW_TILE), 1)
        upper = (c_i >= r_i).astype(F32)
        dlf = jnp.dot(upper, dcb, precision=HIGHEST, preferred_element_type=F32) + carry[...]
        carry[...] = carry[...] + jnp.sum(dcb, axis=0, keepdims=True)
        logit = f_ref[...] + b_ref[...]
        dlogit = jnp.where(_valid_gate_mask(i), dlf * _sigmoid(-logit), 0.0)
        df_ref[...] = dlogit.astype(BF16)
        db_ref[...] += jnp.sum(dlogit, axis=0, keepdims=True)

    blk = pl.BlockSpec((ROW_TILE, F_COLS), lambda s: (nb - 1 - s, 0))
    one = pl.BlockSpec((1, F_COLS), lambda s: (0, 0))
    return pl.pallas_call(
        body, name="gates_bwd", grid=(nb,),
        in_specs=[blk, blk, one], out_specs=[blk, one],
        out_shape=[jax.ShapeDtypeStruct(f.shape, BF16), jax.ShapeDtypeStruct((1, F_COLS), F32)],
        scratch_shapes=[pltpu.VMEM((1, F_COLS), F32)],
        compiler_params=_params(("arbitrary",)),
    )(dc, f, bfg)


def _input_bwd(dproj, df, wt_e, wt, wt_f, x2, metapad, dh1, g1):
    nb = x2.shape[0] // ROW_TILE + 1
    qkv_rows = QKV_ROW0 + QKV_COLS

    def body(dp_ref, df_ref, we_ref, w_ref, wf_ref, x_ref, mp_ref, dh_ref, g_ref, gx_ref, g0_ref, dg_ref):
        i = pl.program_id(0)

        @pl.when(i == 0)
        def _():
            dg_ref[...] = jnp.zeros_like(dg_ref)

        dhn = (jnp.dot(dp_ref[:, :E_COLS], we_ref[...], preferred_element_type=F32)
               + jnp.dot(dp_ref[:, E_COLS:], w_ref[QKV_ROW0:, :], preferred_element_type=F32)
               + jnp.dot(df_ref[...], wf_ref[...], preferred_element_type=F32))
        h0 = jnp.where(i == 0, mp_ref[...], x_ref[...])
        r = lax.rsqrt(jnp.mean(h0 * h0, axis=-1, keepdims=True) + RMS_EPS)
        xhat = h0 * r
        dg_ref[...] += jnp.sum(dhn * xhat, axis=0, keepdims=True)
        dxhat = dhn * g_ref[...]
        dh0 = dh_ref[...] + r * (dxhat - xhat * jnp.mean(dxhat * xhat, axis=-1, keepdims=True))
        gx_ref[...] = dh0

        @pl.when(i == 0)
        def _():
            g0_ref[...] = dh0

    const = lambda shape: pl.BlockSpec(shape, lambda i: (0, 0))
    return pl.pallas_call(
        body, name="input_bwd", grid=(nb,),
        in_specs=[pl.BlockSpec((ROW_TILE, MAIN_COLS), lambda i: (i, 0)), pl.BlockSpec((ROW_TILE, F_COLS), lambda i: (i, 0)),
                  const((E_COLS, D_MODEL)), const((qkv_rows, D_MODEL)), const((F_COLS, D_MODEL)),
                  _tokens_spec(), const((ROW_TILE, D_MODEL)),
                  pl.BlockSpec((ROW_TILE, D_MODEL), lambda i: (i, 0)), const((1, D_MODEL))],
        out_specs=[_tokens_spec(), const((ROW_TILE, D_MODEL)), const((1, D_MODEL))],
        out_shape=[jax.ShapeDtypeStruct(x2.shape, F32), jax.ShapeDtypeStruct((ROW_TILE, D_MODEL), F32),
                   jax.ShapeDtypeStruct((1, D_MODEL), F32)],
        compiler_params=_params(("arbitrary",), vmem=56 * 1024 * 1024),
    )(dproj, df, wt_e, wt, wt_f, x2, metapad, dh1, g1)


def _adamw(w, g, m, v, name):
    rows, cols = w.shape
    if rows % 8 == 0:
        tr, tc = _row_tile8(rows), cols
    else:
        tr, tc = rows, (2 * LANES if cols % (2 * LANES) == 0 and rows > 8 else cols)

    def body(w_ref, g_ref, m_ref, v_ref, d_ref, mo_ref, vo_ref):
        g_ = g_ref[...]
        m_new = ADAM_B1 * m_ref[...] + (1.0 - ADAM_B1) * g_
        v_new = ADAM_B2 * v_ref[...] + (1.0 - ADAM_B2) * (g_ * g_)
        m_hat = m_new / (1.0 - ADAM_B1 ** ADAM_STEP)
        v_hat = v_new / (1.0 - ADAM_B2 ** ADAM_STEP)
        d_ref[...] = -ADAM_LR * (m_hat / (jnp.sqrt(v_hat) + ADAM_EPS) + ADAM_WD * w_ref[...])
        mo_ref[...] = m_new
        vo_ref[...] = v_new

    blk = pl.BlockSpec((tr, tc), lambda i, j: (i, j))
    return pl.pallas_call(
        body, name=name, grid=(rows // tr, cols // tc),
        in_specs=[blk] * 4, out_specs=[blk] * 3,
        out_shape=[jax.ShapeDtypeStruct(w.shape, F32)] * 3,
        compiler_params=_params(("parallel", "parallel")),
    )(w, g, m, v)


def _row_tile8(rows):
    best = rows
    for t in range(8, 257, 8):
        if rows % t == 0:
            best = t
    return best


def kernel(x, meta_tokens, norm_g, w_in, b_forget, pool_w, pool_scale, w_up_pool, w_up_attn, w_out, final_norm_g, loss_target, m_meta_tokens, m_norm_g, m_w_in, m_b_forget, m_pool_w, m_pool_scale, m_w_up_pool, m_w_up_attn, m_w_out, m_final_norm_g, v_meta_tokens, v_norm_g, v_w_in, v_b_forget, v_pool_w, v_pool_scale, v_w_up_pool, v_w_up_attn, v_w_out, v_final_norm_g):
    seq = x.shape[1]
    assert seq % ROW_TILE == 0 and x.shape[0] == 1
    lp = seq + ROW_TILE
    nb = lp // ROW_TILE
    lk = -(-lp // KV_TILE) * KV_TILE
    nkb = lk // KV_TILE
    core = jnp.reshape(lax.axis_index("c"), (1,)).astype(jnp.int32)
    x2 = x[0]
    target = loss_target[0]
    sh_d = D_MODEL // N_CHIPS

    w_in_t = jnp.transpose(w_in[0])
    shard_axes = [2, 1, 1, 1, 1]
    wg_in, wg_up_p, wg_up_a, wg_out, meta_g = _gather_shards(
        [w_in_t.astype(BF16), w_up_pool[0].astype(BF16), w_up_attn[0].astype(BF16), w_out[0].astype(BF16), meta_tokens],
        [a - 1 for a in shard_axes])
    wt = wg_in.reshape(-1, D_MODEL)
    wt_e = jnp.concatenate([wt[REF_U:REF_Q], wt[REF_ZA:REF_F], wt[REF_GP:]], axis=0)
    wt_f = jnp.pad(wt[REF_F:REF_GP], ((0, F_COLS - N_HEADS), (0, 0)))
    wup_p = jnp.transpose(wg_up_p, (1, 0, 2)).reshape(POOL_WIDTH, D_MODEL)
    wup_a = jnp.transpose(wg_up_a, (1, 0, 2)).reshape(ATTN_WIDTH, D_MODEL)
    wout = wg_out.reshape(D_MODEL, D_MODEL)
    meta_full = jnp.transpose(meta_g, (1, 0, 2)).reshape(N_META, D_MODEL)
    metapad = jnp.pad(meta_full, ((PAD_ROWS, 0), (0, 0)))
    bfg = jnp.pad(b_forget, ((0, 0), (0, F_COLS - N_HEADS)))
    pw_b = pool_w[0].astype(BF16)
    gf = final_norm_g.reshape(1, D_MODEL)

    tm = _row_tile(lp, 1100)
    hn = _rmsnorm_fwd(x2, metapad, norm_g, lk)
    e = _mm_nt(hn, wt_e, F32, "in_proj_gates", lp, E_COLS, tm, 512)
    qkv = _mm_nt(hn, wt, BF16, "in_proj_qkv", lk, QKV_COLS, _row_tile(lk, 1200), 512, row_block=QKV_ROW0 // 512,
                 scale_first=HEAD_DIM ** -0.5)
    f = _mm_nt(hn, wt_f, F32, "in_proj_forget", lp, F_COLS, tm, F_COLS)
    c = _gates_fwd(f, bfg)
    c_t = jnp.transpose(c[:, :N_HEADS])
    bias = c_t[None, :, :] - jnp.transpose(c_t[:, ::ROW_TILE])[:, :, None]
    bias = jnp.where(jnp.arange(lp) < PAD_ROWS, -NEG, bias)
    bias = jnp.pad(bias, ((0, 0), (0, 0), (0, lk - lp)))
    bias = jnp.transpose(bias.reshape(nb, N_HEADS // 2, 2, nkb, KV_TILE), (0, 1, 3, 2, 4))
    y_pool = _pool_fwd(e, pw_b, pool_scale)
    o, lse = _attn_fwd(qkv, bias, nb)
    merged, y_attn = _merge_fwd(y_pool, o, e, wup_p, wup_a)
    dh1, loss_part, dgf = _head_fwd_bwd(merged, wout, x2, metapad, gf, target)

    dap, daa, dgate, dza, do, delta, dy_pool = _merge_bwd(dh1, wout, y_pool, y_attn, wup_p, wup_a, e, o)
    dpc, dzp, dscale, dpw = _pool_bwd_local(e, dy_pool, pw_b, pool_scale)
    du = _pool_bwd_window(dpc)
    dq, dk, dv, dc4, dcq = _attn_bwd(qkv, do, lse, delta, bias, nb)
    dc = jnp.transpose(dc4, (1, 3, 0, 2)).reshape(lk, N_HEADS)[:lp] + dcq[:, ::HEAD_DIM]
    df, db = _gates_bwd(jnp.pad(dc, ((0, 0), (0, F_COLS - N_HEADS))), f, bfg)
    dproj = jnp.concatenate([du, dzp, dza, dgate, (dq * HEAD_DIM ** -0.5).astype(BF16), dk[:lp], dv[:lp]], axis=1)
    tk = _row_tile(lp, 1100)
    dwt_main = _mm_tn(dproj, hn, "grad_w_in_main", lp, 1024, D_MODEL, tk)
    dwt_f = _mm_tn(df, hn, "grad_w_in_forget", lp, F_COLS, D_MODEL, tk)
    dw_out = _mm_tn(merged, dh1, "grad_w_out", lp, 512, D_MODEL, tk)
    dw_up_p = _mm_tn(y_pool, dap, "grad_w_up_pool", lp, 512, sh_d, tk, chunks=N_CHIPS)
    dw_up_a = _mm_tn(y_attn, daa, "grad_w_up_attn", lp, 512, sh_d, tk, chunks=N_CHIPS)
    grad_x, g_block0, dg1 = _input_bwd(dproj, df, wt_e, wt, wt_f, x2, metapad, dh1, norm_g)
    dmeta = g_block0[PAD_ROWS:]

    r_ = lambda a, b: dwt_main[a:b]
    dwt = jnp.concatenate([r_(E_U, E_ZA), r_(E_COLS, MAIN_COLS), r_(E_ZA, E_GP), dwt_f[:N_HEADS], r_(E_GP, E_COLS)],
                          axis=0)
    gs = [dwt.reshape(N_CHIPS, -1, D_MODEL), dw_up_p, dw_up_a, dw_out.reshape(N_CHIPS, sh_d, D_MODEL),
          jnp.transpose(dmeta.reshape(N_META, N_CHIPS, sh_d), (1, 0, 2))]

    def pad8(a):
        return jnp.pad(a, ((0, (-a.shape[0]) % 8), (0, 0)))

    small_parts = [pad8(a) for a in (dg1.reshape(-1, LANES), dgf.reshape(-1, LANES), dscale.reshape(-1, LANES), db,
                                     dpw.reshape(-1, LANES), loss_part)]
    small = jnp.concatenate(small_parts, axis=0)
    soffs = [0]
    for p in small_parts:
        soffs.append(soffs[-1] + p.shape[0])

    *rbs, sr = _sibling_exchange(gs, shard_axes, small)
    *hs, sc = _add_sibling_half(gs, rbs, shard_axes, small, sr, core)
    *qs, sq = _chip_scatter(hs, sc)
    g_w_in_t, g_w_up_p, g_w_up_a, g_w_out, g_meta, st = _reduce_allgather(qs, shard_axes, sq)

    spiece = lambda k, rows: st[soffs[k]:soffs[k] + rows]
    g_norm = spiece(0, D_MODEL // LANES).reshape(1, D_MODEL)
    g_final = spiece(1, D_MODEL // LANES).reshape(1, D_MODEL)
    g_scale = spiece(2, POOL_WIDTH // LANES).reshape(1, POOL_WIDTH)
    g_bf = spiece(3, 1)
    g_pw = spiece(4, POOL_WIDTH)
    loss = st[soffs[5], 0]

    def pad_lanes(a):
        return jnp.pad(a, ((0, 0), (0, F_COLS - N_HEADS)))

    upd = [
        ("meta_tokens", meta_tokens, g_meta, m_meta_tokens, v_meta_tokens),
        ("norm_g", norm_g, g_norm, m_norm_g, v_norm_g),
        ("w_in", w_in_t, g_w_in_t, jnp.transpose(m_w_in[0]), jnp.transpose(v_w_in[0])),
        ("b_forget", pad_lanes(b_forget), g_bf, pad_lanes(m_b_forget), pad_lanes(v_b_forget)),
        ("pool_w", pool_w.reshape(-1, LANES), g_pw, m_pool_w.reshape(-1, LANES), v_pool_w.reshape(-1, LANES)),
        ("pool_scale", pool_scale, g_scale, m_pool_scale, v_pool_scale),
        ("w_up_pool", w_up_pool[0], g_w_up_p, m_w_up_pool[0], v_w_up_pool[0]),
        ("w_up_attn", w_up_attn[0], g_w_up_a, m_w_up_attn[0], v_w_up_attn[0]),
        ("w_out", w_out[0], g_w_out, m_w_out[0], v_w_out[0]),
        ("final_norm_g", gf, g_final, m_final_norm_g.reshape(1, D_MODEL), v_final_norm_g.reshape(1, D_MODEL)),
    ]
    shapes = [meta_tokens.shape, norm_g.shape, w_in.shape, b_forget.shape, pool_w.shape, pool_scale.shape,
              w_up_pool.shape, w_up_attn.shape, w_out.shape, final_norm_g.shape]
    grads, deltas, new_ms, new_vs = [], [], [], []
    for (name, w_, g_, m_in, v_in), shp in zip(upd, shapes):
        d_, mn_, vn_ = _adamw(w_, g_, m_in, v_in, "adamw_" + name)
        res = (g_, d_, mn_, vn_)
        if name == "b_forget":
            res = tuple(a[:, :N_HEADS] for a in res)
        if name == "w_in":
            res = tuple(jnp.transpose(a) for a in res)
        for lst, a in zip((grads, deltas, new_ms, new_vs), res):
            lst.append(a.reshape(shp))

    return (loss, grad_x.reshape(x.shape), *grads, *deltas, *new_ms, *new_vs)
```

```python
import jax
import jax.numpy as jnp
from jax import lax
from jax.experimental import pallas as pl
from jax.experimental.pallas import tpu as pltpu

F32 = jnp.float32
BF16 = jnp.bfloat16
MESH = pl.DeviceIdType.MESH
HIGHEST = lax.Precision.HIGHEST
HBM = pl.BlockSpec(memory_space=pltpu.HBM)

D_MODEL = 1024
N_META = 16
POOL_WIDTH = 512
POOL_GROUP = 128
POOL_WINDOWS = (2, 4, 8, 16)
N_HEADS = 8
HEAD_DIM = 64
ATTN_WIDTH = 512
RMS_EPS = 1e-6
N_CHIPS = 4

ADAM_LR = 0.001
ADAM_B1 = 0.9
ADAM_B2 = 0.999
ADAM_EPS = 1e-08
ADAM_WD = 0.01
ADAM_STEP = 10

LANES = 128
ROW_TILE = 256
KV_TILE = 512
PAD_ROWS = ROW_TILE - N_META
NEG = -1e30

E_U, E_ZP, E_ZA, E_GP, E_GA, E_COLS = 0, 512, 1024, 1536, 2560, 3584
QKV_COLS = 3 * ATTN_WIDTH
MAIN_COLS = E_COLS + QKV_COLS
F_COLS = LANES
REF_U, REF_Q, REF_ZA, REF_F, REF_GP = 0, 1024, 2560, 3072, 3080
QKV_ROW0 = REF_Q
VMEM_LIMIT = 48 * 1024 * 1024


def _params(sem=None, vmem=VMEM_LIMIT):
    return pltpu.CompilerParams(dimension_semantics=sem, vmem_limit_bytes=vmem)


def _in_hbm(*arrays):
    return [pltpu.with_memory_space_constraint(a, pltpu.HBM) for a in arrays]


def _row_tile(n, target):
    best = 16
    for t in range(16, target + 1, 16):
        if n % t == 0:
            best = t
    return best


def _sigmoid(x):
    return 1.0 / (1.0 + jnp.exp(-x))


def _half(ref, axis, which):
    n = ref.shape[axis] // 2
    idx = [slice(None)] * len(ref.shape)
    idx[axis] = pl.ds(pl.multiple_of(which * n, n), n)
    return ref.at[tuple(idx)]


def _half_shape(shape, axis):
    s = list(shape)
    s[axis] //= 2
    return tuple(s)


def _gather_shards(shards, axes):
    n = len(shards)

    def body(*refs):
        ins, outs = refs[:n], refs[n:2 * n]
        send, recv, fsend, frecv, local = refs[2 * n:]
        x, y, c = lax.axis_index("x"), lax.axis_index("y"), lax.axis_index("c")
        me = 2 * x + y
        sibling = (x, y, 1 - c)
        chips = [(1 - x, y), (x, 1 - y), (1 - x, 1 - y)]

        def over_ici(t, k, chip, dst_slot):
            return pltpu.make_async_remote_copy(
                src_ref=_half(ins[t], axes[t], c), dst_ref=_half(outs[t].at[dst_slot], axes[t], c),
                send_sem=send.at[3 * t + k], recv_sem=recv.at[3 * t + k], device_id=(*chip, c), device_id_type=MESH)

        def to_sibling(t, k, slot, which):
            return pltpu.make_async_remote_copy(
                src_ref=_half(outs[t].at[slot], axes[t], which), dst_ref=_half(outs[t].at[slot], axes[t], which),
                send_sem=fsend.at[3 * t + k], recv_sem=frecv.at[3 * t + k], device_id=sibling, device_id_type=MESH)

        started = []
        for t in range(n):
            mine = pltpu.make_async_copy(ins[t], outs[t].at[me], local.at[t])
            mine.start()
            started.append(mine)
        sent = []
        for t in range(n):
            for k, chip in enumerate(chips):
                cp = over_ici(t, k, chip, me)
                cp.start()
                sent.append(cp)
        for t in range(n):
            for k, (px, py) in enumerate(chips):
                over_ici(t, k, (px, py), 2 * px + py).wait_recv()
                fw = to_sibling(t, k, 2 * px + py, c)
                fw.start()
                sent.append(fw)
        for t in range(n):
            for k, (px, py) in enumerate(chips):
                to_sibling(t, k, 2 * px + py, 1 - c).wait_recv()
        for cp in sent:
            cp.wait_send()
        for cp in started:
            cp.wait()

    return pl.pallas_call(
        body, name="gather_weight_shards",
        in_specs=[HBM] * n, out_specs=[HBM] * n,
        out_shape=[jax.ShapeDtypeStruct((N_CHIPS,) + s.shape, s.dtype) for s in shards],
        scratch_shapes=[pltpu.SemaphoreType.DMA((3 * n,)), pltpu.SemaphoreType.DMA((3 * n,)),
                        pltpu.SemaphoreType.DMA((3 * n,)), pltpu.SemaphoreType.DMA((3 * n,)),
                        pltpu.SemaphoreType.DMA((n,))],
    )(*_in_hbm(*shards))


def _sibling_exchange(gs, axes, small):
    n = len(gs)

    def body(*refs):
        ins, s_ref = refs[:n], refs[n]
        outs, sr_ref = refs[n + 1:2 * n + 1], refs[2 * n + 1]
        send, recv = refs[2 * n + 2:]
        x, y, c = lax.axis_index("x"), lax.axis_index("y"), lax.axis_index("c")
        sib = (x, y, 1 - c)
        cps = [pltpu.make_async_remote_copy(src_ref=_half(ins[t], axes[t], 1 - c), dst_ref=outs[t], send_sem=send.at[t],
                                            recv_sem=recv.at[t], device_id=sib, device_id_type=MESH) for t in range(n)]
        cps.append(pltpu.make_async_remote_copy(src_ref=s_ref, dst_ref=sr_ref, send_sem=send.at[n], recv_sem=recv.at[n],
                                                device_id=sib, device_id_type=MESH))
        for cp in cps:
            cp.start()
        for cp in cps:
            cp.wait()

    return pl.pallas_call(
        body, name="grad_sibling_exchange",
        in_specs=[HBM] * (n + 1), out_specs=[HBM] * (n + 1),
        out_shape=[jax.ShapeDtypeStruct(_half_shape(g.shape, a), g.dtype) for g, a in zip(gs, axes)]
        + [jax.ShapeDtypeStruct(small.shape, small.dtype)],
        scratch_shapes=[pltpu.SemaphoreType.DMA((n + 1,)), pltpu.SemaphoreType.DMA((n + 1,))],
    )(*_in_hbm(*gs, small))


def _add_sibling_half(gs, rbs, axes, small, sr, core):
    n = len(gs)

    def body(core_ref, *refs):
        g_refs, rb_refs = refs[:n], refs[n:2 * n]
        s_ref, sr_ref = refs[2 * n], refs[2 * n + 1]
        h_refs, sc_ref = refs[2 * n + 2:3 * n + 2], refs[3 * n + 2]
        for t in range(n):
            h_refs[t][...] = (g_refs[t][...] + rb_refs[t][...]).astype(BF16)
        sc_ref[...] = s_ref[...] + sr_ref[...]

    def mine(rb, axis):
        blk = (None,) + rb.shape[1:]
        if axis == 2:
            return pl.BlockSpec(blk, lambda j, cr: (j, 0, cr[0]))
        return pl.BlockSpec(blk, lambda j, cr: (j, cr[0], 0))

    chunk = lambda rb: pl.BlockSpec((None,) + rb.shape[1:], lambda j, cr: (j, 0, 0))
    whole = pl.BlockSpec(small.shape, lambda j, cr: (0, 0))
    return pl.pallas_call(
        body, name="grad_add_sibling",
        grid_spec=pltpu.PrefetchScalarGridSpec(
            num_scalar_prefetch=1, grid=(N_CHIPS,),
            in_specs=[mine(rb, a) for rb, a in zip(rbs, axes)] + [chunk(rb) for rb in rbs] + [whole, whole],
            out_specs=[chunk(rb) for rb in rbs] + [whole]),
        out_shape=[jax.ShapeDtypeStruct(rb.shape, BF16) for rb in rbs] + [jax.ShapeDtypeStruct(small.shape, F32)],
        compiler_params=_params(("arbitrary",)),
    )(core, *gs, *rbs, small, sr)


def _chip_scatter_plan(h_refs, s_ref, q_refs, sq_ref, send, recv, local):
    n = len(h_refs)
    x, y, c = lax.axis_index("x"), lax.axis_index("y"), lax.axis_index("c")
    me = 2 * x + y
    chips = [(1 - x, y), (x, 1 - y), (1 - x, 1 - y)]

    def copy(t, k, chip, src_slot, dst_slot):
        src = h_refs[t].at[src_slot] if t < n else s_ref
        dst = (q_refs[t] if t < n else sq_ref).at[dst_slot]
        return pltpu.make_async_remote_copy(src_ref=src, dst_ref=dst, send_sem=send.at[3 * t + k],
                                            recv_sem=recv.at[3 * t + k], device_id=(*chip, c), device_id_type=MESH)

    own = [pltpu.make_async_copy(h_refs[t].at[me], q_refs[t].at[me], local.at[t]) for t in range(n)]
    own.append(pltpu.make_async_copy(s_ref, sq_ref.at[me], local.at[n]))
    sends = [copy(t, k, (px, py), 2 * px + py, me) for t in range(n + 1) for k, (px, py) in enumerate(chips)]
    recvs = [copy(t, k, (px, py), me, 2 * px + py) for t in range(n + 1) for k, (px, py) in enumerate(chips)]
    return own, sends, recvs


def _small_allreduce(buf):
    def body(b_ref, o_ref, sib_buf, chip_buf, send, recv):
        x, y, c = lax.axis_index("x"), lax.axis_index("y"), lax.axis_index("c")
        me = 2 * x + y
        chips = [(1 - x, y), (x, 1 - y), (1 - x, 1 - y)]
        swap = pltpu.make_async_remote_copy(src_ref=b_ref, dst_ref=sib_buf, send_sem=send.at[0], recv_sem=recv.at[0],
                                            device_id=(x, y, 1 - c), device_id_type=MESH)
        swap.start()
        swap.wait()
        chip_buf[me] = b_ref[...] + sib_buf[...]

        def copy(k, chip, slot):
            return pltpu.make_async_remote_copy(src_ref=chip_buf.at[slot], dst_ref=chip_buf.at[slot], send_sem=send.at[1 + k],
                                                recv_sem=recv.at[1 + k], device_id=(*chip, c), device_id_type=MESH)

        sends = [copy(k, chip, me) for k, chip in enumerate(chips)]
        for cp in sends:
            cp.start()
        for k, (px, py) in enumerate(chips):
            copy(k, (px, py), 2 * px + py).wait_recv()
        for cp in sends:
            cp.wait_send()
        o_ref[...] = ((chip_buf[0] + chip_buf[1]) + chip_buf[2]) + chip_buf[3]

    vmem = pl.BlockSpec(memory_space=pltpu.VMEM)
    return pl.pallas_call(
        body, name="small_allreduce", in_specs=[vmem], out_specs=vmem,
        out_shape=jax.ShapeDtypeStruct(buf.shape, F32),
        scratch_shapes=[pltpu.VMEM(buf.shape, F32), pltpu.VMEM((N_CHIPS,) + buf.shape, F32),
                        pltpu.SemaphoreType.DMA((4,)), pltpu.SemaphoreType.DMA((4,))],
        compiler_params=_params(),
    )(buf)


def _reduce_allgather(qs, axes, sq):
    n = len(qs)
    shard_shapes = [tuple(d * 2 if i == a - 1 else d for i, d in enumerate(q.shape[1:])) for q, a in zip(qs, axes)]

    def body(*refs):
        q_refs, sq_ref = refs[:n], refs[n]
        o_refs, st_ref = refs[n + 1:2 * n + 1], refs[2 * n + 1]
        send, recv = refs[2 * n + 2:]
        x, y, c = lax.axis_index("x"), lax.axis_index("y"), lax.axis_index("c")

        def swap(t, which):
            return pltpu.make_async_remote_copy(
                src_ref=_half(o_refs[t], axes[t] - 1, which), dst_ref=_half(o_refs[t], axes[t] - 1, which),
                send_sem=send.at[t], recv_sem=recv.at[t], device_id=(x, y, 1 - c), device_id_type=MESH)

        sent = []
        for t in range(n):
            q = q_refs[t]
            total = ((q[0].astype(F32) + q[1].astype(F32)) + q[2].astype(F32)) + q[3].astype(F32)
            _half(o_refs[t], axes[t] - 1, c)[...] = total
            cp = swap(t, c)
            cp.start()
            sent.append(cp)
        st_ref[...] = ((sq_ref[0] + sq_ref[1]) + sq_ref[2]) + sq_ref[3]
        for t in range(n):
            swap(t, 1 - c).wait_recv()
        for cp in sent:
            cp.wait_send()

    vmem = pl.BlockSpec(memory_space=pltpu.VMEM)
    return pl.pallas_call(
        body, name="grad_reduce_allgather",
        in_specs=[vmem] * (n + 1), out_specs=[vmem] * (n + 1),
        out_shape=[jax.ShapeDtypeStruct(s, F32) for s in shard_shapes] + [jax.ShapeDtypeStruct(sq.shape[1:], F32)],
        scratch_shapes=[pltpu.SemaphoreType.DMA((n,)), pltpu.SemaphoreType.DMA((n,))],
        compiler_params=_params(),
    )(*qs, sq)


def _dot_nt(a, b):
    return lax.dot_general(a, b, (((1,), (1,)), ((), ())), preferred_element_type=F32)


def _dot_tn(a, b):
    return lax.dot_general(a, b, (((0,), (0,)), ((), ())), preferred_element_type=F32)


def _mm_nt(a, bt, out_dtype, name, m, n, tm, tn, row_block=0, scale_first=None):
    k = a.shape[1]

    def body(a_ref, b_ref, o_ref):
        r = _dot_nt(a_ref[...], b_ref[...])
        if scale_first is not None:
            r = r * jnp.where(pl.program_id(0) == 0, scale_first, 1.0)
        o_ref[...] = r.astype(out_dtype)

    return pl.pallas_call(
        body, name=name, grid=(n // tn, m // tm),
        in_specs=[pl.BlockSpec((tm, k), lambda j, i: (i, 0)), pl.BlockSpec((tn, k), lambda j, i: (row_block + j, 0))],
        out_specs=pl.BlockSpec((tm, tn), lambda j, i: (i, j)),
        out_shape=jax.ShapeDtypeStruct((m, n), out_dtype),
        compiler_params=_params(("parallel", "parallel")),
    )(a, bt)


def _mm_tn(a, b, name, k, tm, tn, tk, chunks=1):
    m, n = a.shape[1], b.shape[1]
    cw = n // chunks

    def body(a_ref, b_ref, o_ref):
        @pl.when(pl.program_id(2) == 0)
        def _():
            o_ref[...] = jnp.zeros_like(o_ref)
        r = _dot_tn(a_ref[...].astype(BF16), b_ref[...].astype(BF16))
        if chunks > 1:
            for c in range(chunks):
                o_ref[c] += r[:, c * cw:(c + 1) * cw]
        else:
            o_ref[...] += r

    if chunks > 1:
        assert tn == n
        out_spec = pl.BlockSpec((chunks, tm, cw), lambda i, j, kk: (0, i, 0))
        out_shape = jax.ShapeDtypeStruct((chunks, m, cw), F32)
    else:
        out_spec = pl.BlockSpec((tm, tn), lambda i, j, kk: (i, j))
        out_shape = jax.ShapeDtypeStruct((m, n), F32)
    return pl.pallas_call(
        body, name=name, grid=(m // tm, n // tn, k // tk),
        in_specs=[pl.BlockSpec((tk, tm), lambda i, j, kk: (kk, i)), pl.BlockSpec((tk, tn), lambda i, j, kk: (kk, j))],
        out_specs=out_spec, out_shape=out_shape,
        compiler_params=_params(("parallel", "parallel", "arbitrary")),
    )(a, b)


def _tokens_spec():
    return pl.BlockSpec((ROW_TILE, D_MODEL), lambda i: (jnp.maximum(i - 1, 0), 0))


def _rmsnorm_fwd(x2, metapad, g1, lk):
    nb = x2.shape[0] // ROW_TILE + 1

    def body(x_ref, mp_ref, g_ref, hn_ref):
        i = pl.program_id(0)
        h = jnp.where(i == 0, mp_ref[...], jnp.where(i >= nb, 0.0, x_ref[...]))
        r = lax.rsqrt(jnp.mean(h * h, axis=-1, keepdims=True) + RMS_EPS)
        hn_ref[...] = ((h * r) * g_ref[...]).astype(BF16)

    return pl.pallas_call(
        body, name="rmsnorm_fwd", grid=(lk // ROW_TILE,),
        in_specs=[pl.BlockSpec((ROW_TILE, D_MODEL), lambda i: (jnp.clip(i - 1, 0, nb - 2), 0)),
                  pl.BlockSpec((ROW_TILE, D_MODEL), lambda i: (0, 0)),
                  pl.BlockSpec((1, D_MODEL), lambda i: (0, 0))],
        out_specs=pl.BlockSpec((ROW_TILE, D_MODEL), lambda i: (i, 0)),
        out_shape=jax.ShapeDtypeStruct((lk, D_MODEL), BF16),
        compiler_params=_params(("parallel",)),
    )(x2, metapad, g1)


def _valid_gate_mask(i):
    row = i * ROW_TILE + lax.broadcasted_iota(jnp.int32, (ROW_TILE, F_COLS), 0)
    col = lax.broadcasted_iota(jnp.int32, (ROW_TILE, F_COLS), 1)
    return (row >= PAD_ROWS) & (col < N_HEADS)


def _gates_fwd(f, bfg):
    nb = f.shape[0] // ROW_TILE

    def body(f_ref, b_ref, c_ref, carry):
        i = pl.program_id(0)

        @pl.when(i == 0)
        def _():
            carry[...] = jnp.zeros_like(carry)

        logit = f_ref[...] + b_ref[...]
        lf = jnp.minimum(logit, 0.0) - jnp.log1p(jnp.exp(-jnp.abs(logit)))
        lf = jnp.where(_valid_gate_mask(i), lf, 0.0)
        r_i = lax.broadcasted_iota(jnp.int32, (ROW_TILE, ROW_TILE), 0)
        c_i = lax.broadcasted_iota(jnp.int32, (ROW_TILE, ROW_TILE), 1)
        tri = (c_i <= r_i).astype(F32)
        c_ref[...] = jnp.dot(tri, lf, precision=HIGHEST, preferred_element_type=F32) + carry[...]
        carry[...] = carry[...] + jnp.sum(lf, axis=0, keepdims=True)

    return pl.pallas_call(
        body, name="gates_fwd", grid=(nb,),
        in_specs=[pl.BlockSpec((ROW_TILE, F_COLS), lambda i: (i, 0)), pl.BlockSpec((1, F_COLS), lambda i: (0, 0))],
        out_specs=pl.BlockSpec((ROW_TILE, F_COLS), lambda i: (i, 0)),
        out_shape=jax.ShapeDtypeStruct(f.shape, F32),
        scratch_shapes=[pltpu.VMEM((1, F_COLS), F32)],
        compiler_params=_params(("arbitrary",)),
    )(f, bfg)


def _pool_counts(i):
    row = i * ROW_TILE + lax.broadcasted_iota(jnp.int32, (ROW_TILE, 1), 0)
    return jnp.maximum(row - PAD_ROWS, 0)


def _trailing_sums(xc, levels):
    acc = xc
    for lv in range(levels):
        acc = acc + pltpu.roll(acc, 1 << lv, 0)
    return acc


def _leading_sums(xc, levels):
    n = xc.shape[0]
    acc = xc
    for lv in range(levels):
        acc = acc + pltpu.roll(acc, n - (1 << lv), 0)
    return acc


def _pool_p(u_cur, u_prev, i):
    pos = _pool_counts(i)
    ps, invs = [], []
    for g, w in enumerate(POOL_WINDOWS):
        sl = slice(g * POOL_GROUP, (g + 1) * POOL_GROUP)
        cur = u_cur[:, sl]
        xc = jnp.concatenate([u_prev[:, sl], cur], axis=0)
        win = _trailing_sums(xc, g + 1)[ROW_TILE:, :]
        inv = 1.0 / jnp.minimum(pos + 1, w).astype(F32)
        ps.append(win * inv - cur)
        invs.append(inv)
    return ps, invs


def _pool_fwd(e, pw, scale):
    nb = e.shape[0] // ROW_TILE

    def body(uc_ref, up_ref, z_ref, pw_ref, sc_ref, y_ref):
        i = pl.program_id(0)
        u_cur = uc_ref[...]
        u_prev = jnp.where(i == 0, 0.0, up_ref[...])
        ps, _ = _pool_p(u_cur, u_prev, i)
        z = z_ref[...]
        gate = z * _sigmoid(z)
        for g in range(len(POOL_WINDOWS)):
            sl = slice(g * POOL_GROUP, (g + 1) * POOL_GROUP)
            yraw = jnp.dot(ps[g].astype(BF16), pw_ref[g], preferred_element_type=F32)
            y_ref[:, sl] = ((yraw * sc_ref[:, sl]) * gate[:, sl]).astype(BF16)

    blk = (ROW_TILE, POOL_WIDTH)
    return pl.pallas_call(
        body, name="pool_fwd", grid=(nb,),
        in_specs=[pl.BlockSpec(blk, lambda i: (i, 0)), pl.BlockSpec(blk, lambda i: (jnp.maximum(i - 1, 0), 0)),
                  pl.BlockSpec(blk, lambda i: (i, 1)),
                  pl.BlockSpec((len(POOL_WINDOWS), POOL_GROUP, POOL_GROUP), lambda i: (0, 0, 0)),
                  pl.BlockSpec((1, POOL_WIDTH), lambda i: (0, 0))],
        out_specs=pl.BlockSpec(blk, lambda i: (i, 0)),
        out_shape=jax.ShapeDtypeStruct((e.shape[0], POOL_WIDTH), BF16),
        compiler_params=_params(("parallel",)),
    )(e, e, e, pw, scale)


def _stack_heads(a):
    first = lax.broadcasted_iota(jnp.int32, a.shape, 1) < HEAD_DIM
    zero = jnp.zeros_like(a)
    return jnp.concatenate([jnp.where(first, a, zero), jnp.where(first, zero, a)], axis=0)


def _unstack_heads(a):
    rows = a.shape[0] // 2
    first = lax.broadcasted_iota(jnp.int32, (rows, LANES), 1) < HEAD_DIM
    return jnp.where(first, a[:rows], a[rows:])


def _causal(i, jj, stacked):
    r = lax.broadcasted_iota(jnp.int32, (stacked * ROW_TILE, KV_TILE), 0)
    if stacked == 2:
        r = jnp.where(r >= ROW_TILE, r - ROW_TILE, r)
    kidx = jj * KV_TILE + lax.broadcasted_iota(jnp.int32, (stacked * ROW_TILE, KV_TILE), 1)
    return kidx <= i * ROW_TILE + r


def _stack_rows(b):
    n = b.shape[1]
    return jnp.concatenate([jnp.broadcast_to(b[0:1], (ROW_TILE, n)), jnp.broadcast_to(b[1:2], (ROW_TILE, n))], axis=0)


def _attn_fwd(qkv, bias, nb):
    lk = qkv.shape[0]
    lp = nb * ROW_TILE
    nkb = lk // KV_TILE
    n_pairs = N_HEADS // 2

    def body(q_ref, k_ref, v_ref, b_ref, o_ref, lse_ref):
        i = pl.program_id(1)
        qs = _stack_heads(q_ref[...])
        last = (i * ROW_TILE) // KV_TILE

        def block(jj, carry, masked):
            m, l, acc = carry
            rows = pl.ds(pl.multiple_of(jj * KV_TILE, KV_TILE), KV_TILE)
            s = _dot_nt(qs, k_ref[rows, :]) - _stack_rows(b_ref[0, 0, jj])
            if masked:
                s = jnp.where(_causal(i, jj, 2), s, NEG)
            m_new = jnp.maximum(m, jnp.max(s, axis=1, keepdims=True))
            alpha = jnp.exp(m - m_new)
            p = jnp.exp(s - m_new)
            l = alpha * l + jnp.sum(p, axis=1, keepdims=True)
            acc = alpha * acc + jnp.dot(p.astype(BF16), v_ref[rows, :], preferred_element_type=F32)
            return m_new, l, acc

        init = (jnp.full((2 * ROW_TILE, 1), NEG, F32), jnp.zeros((2 * ROW_TILE, 1), F32),
                jnp.zeros((2 * ROW_TILE, LANES), F32))
        carry = lax.fori_loop(0, last, lambda jj, c: block(jj, c, False), init)
        m, l, acc = block(last, carry, True)
        o_ref[...] = _unstack_heads(acc / l)
        lse_ref[...] = _unstack_heads(jnp.broadcast_to(m + jnp.log(l), (2 * ROW_TILE, LANES)))

    return pl.pallas_call(
        body, name="attn_fwd", grid=(n_pairs, nb),
        in_specs=[pl.BlockSpec((ROW_TILE, LANES), lambda hp, i: (i, hp)),
                  pl.BlockSpec((lk, LANES), lambda hp, i: (0, n_pairs + hp)),
                  pl.BlockSpec((lk, LANES), lambda hp, i: (0, 2 * n_pairs + hp)),
                  pl.BlockSpec((1, 1, nkb, 2, KV_TILE), lambda hp, i: (i, hp, 0, 0, 0))],
        out_specs=[pl.BlockSpec((ROW_TILE, LANES), lambda hp, i: (i, hp)),
                   pl.BlockSpec((ROW_TILE, LANES), lambda hp, i: (i, hp))],
        out_shape=[jax.ShapeDtypeStruct((lp, ATTN_WIDTH), F32), jax.ShapeDtypeStruct((lp, ATTN_WIDTH), F32)],
        compiler_params=_params(("parallel", "parallel")),
    )(qkv, qkv, qkv, bias)


def _merge_fwd(y_pool, o, e, wup_p, wup_a):
    lp = o.shape[0]
    nb = lp // ROW_TILE

    def body(yp_ref, o_ref, e_ref, wp_ref, wa_ref, mg_ref, ya_ref):
        za = e_ref[:, E_ZA:E_GP]
        ya = (o_ref[...] * (za * _sigmoid(za))).astype(BF16)
        ya_ref[...] = ya
        a_pool = jnp.dot(yp_ref[...], wp_ref[...], preferred_element_type=F32)
        a_attn = jnp.dot(ya, wa_ref[...], preferred_element_type=F32)
        mg_ref[...] = (_sigmoid(e_ref[:, E_GP:E_GA]) * a_pool + _sigmoid(e_ref[:, E_GA:E_COLS]) * a_attn).astype(BF16)

    return pl.pallas_call(
        body, name="merge_fwd", grid=(nb,),
        in_specs=[pl.BlockSpec((ROW_TILE, POOL_WIDTH), lambda i: (i, 0)),
                  pl.BlockSpec((ROW_TILE, ATTN_WIDTH), lambda i: (i, 0)),
                  pl.BlockSpec((ROW_TILE, E_COLS), lambda i: (i, 0)),
                  pl.BlockSpec((POOL_WIDTH, D_MODEL), lambda i: (0, 0)),
                  pl.BlockSpec((ATTN_WIDTH, D_MODEL), lambda i: (0, 0))],
        out_specs=[pl.BlockSpec((ROW_TILE, D_MODEL), lambda i: (i, 0)),
                   pl.BlockSpec((ROW_TILE, ATTN_WIDTH), lambda i: (i, 0))],
        out_shape=[jax.ShapeDtypeStruct((lp, D_MODEL), BF16), jax.ShapeDtypeStruct((lp, ATTN_WIDTH), BF16)],
        compiler_params=_params(("parallel",)),
    )(y_pool, o, e, wup_p, wup_a)


def _head_fwd_bwd(merged, w_out, x2, metapad, gf, target):
    lp = merged.shape[0]
    nb = lp // ROW_TILE

    def body(mg_ref, w_ref, x_ref, mp_ref, g_ref, t_ref, dh_ref, loss_ref, dg_ref):
        i = pl.program_id(0)

        @pl.when(i == 0)
        def _():
            loss_ref[...] = jnp.zeros_like(loss_ref)
            dg_ref[...] = jnp.zeros_like(dg_ref)

        h0 = jnp.where(i == 0, mp_ref[...], x_ref[...])
        h1 = h0 + jnp.dot(mg_ref[...], w_ref[...], preferred_element_type=F32)
        r = lax.rsqrt(jnp.mean(h1 * h1, axis=-1, keepdims=True) + RMS_EPS)
        xhat = h1 * r
        g = g_ref[...]
        err = jnp.where(i == 0, 0.0, xhat * g - t_ref[...])
        loss_ref[...] += 0.5 * jnp.sum(jnp.mean(err * err, axis=-1, keepdims=True))
        dy = err / D_MODEL
        dg_ref[...] += jnp.sum(dy * xhat, axis=0, keepdims=True)
        dxhat = dy * g
        dh_ref[...] = r * (dxhat - xhat * jnp.mean(dxhat * xhat, axis=-1, keepdims=True))

    return pl.pallas_call(
        body, name="head_fwd_bwd", grid=(nb,),
        in_specs=[pl.BlockSpec((ROW_TILE, D_MODEL), lambda i: (i, 0)),
                  pl.BlockSpec((D_MODEL, D_MODEL), lambda i: (0, 0)),
                  _tokens_spec(), pl.BlockSpec((ROW_TILE, D_MODEL), lambda i: (0, 0)),
                  pl.BlockSpec((1, D_MODEL), lambda i: (0, 0)), _tokens_spec()],
        out_specs=[pl.BlockSpec((ROW_TILE, D_MODEL), lambda i: (i, 0)),
                   pl.BlockSpec((1, LANES), lambda i: (0, 0)), pl.BlockSpec((1, D_MODEL), lambda i: (0, 0))],
        out_shape=[jax.ShapeDtypeStruct((lp, D_MODEL), F32), jax.ShapeDtypeStruct((1, LANES), F32),
                   jax.ShapeDtypeStruct((1, D_MODEL), F32)],
        compiler_params=_params(("arbitrary",)),
    )(merged, w_out, x2, metapad, gf, target)


def _per_head_rowsum(t):
    head = lax.broadcasted_iota(jnp.int32, t.shape, 1) // HEAD_DIM
    out = jnp.zeros_like(t)
    for h in range(N_HEADS):
        sel = head == h
        out = jnp.where(sel, jnp.sum(jnp.where(sel, t, 0.0), axis=1, keepdims=True), out)
    return out


def _merge_bwd(dh1, w_out, y_pool, y_attn, wup_p, wup_a, e, o):
    lp = o.shape[0]
    nb = lp // ROW_TILE

    def body(dh_ref, wo_ref, yp_ref, ya_ref, wp_ref, wa_ref, e_ref, o_ref,
             dap_ref, daa_ref, dg_ref, dza_ref, do_ref, delta_ref, dyp_ref):
        dmerged = _dot_nt(dh_ref[...].astype(BF16), wo_ref[...])
        a_pool = jnp.dot(yp_ref[...], wp_ref[...], preferred_element_type=F32)
        a_attn = jnp.dot(ya_ref[...], wa_ref[...], preferred_element_type=F32)
        sp = _sigmoid(e_ref[:, E_GP:E_GA])
        sa = _sigmoid(e_ref[:, E_GA:E_COLS])
        dap = (dmerged * sp).astype(BF16)
        daa = (dmerged * sa).astype(BF16)
        dap_ref[...] = dap
        daa_ref[...] = daa
        dg_ref[:, :D_MODEL] = (dmerged * a_pool * (sp * (1.0 - sp))).astype(BF16)
        dg_ref[:, D_MODEL:] = (dmerged * a_attn * (sa * (1.0 - sa))).astype(BF16)
        dyp_ref[...] = _dot_nt(dap, wp_ref[...])
        dya = _dot_nt(daa, wa_ref[...])
        za = e_ref[:, E_ZA:E_GP]
        sz = _sigmoid(za)
        o = o_ref[...]
        do = dya * (za * sz)
        do_ref[...] = do.astype(BF16)
        dza_ref[...] = (dya * o * (sz * (1.0 + za * (1.0 - sz)))).astype(BF16)
        delta_ref[...] = _per_head_rowsum(do * o)

    row = lambda w: pl.BlockSpec((ROW_TILE, w), lambda i: (i, 0))
    full = lambda a: pl.BlockSpec(a.shape, lambda i: (0, 0))
    return pl.pallas_call(
        body, name="merge_bwd", grid=(nb,),
        in_specs=[row(D_MODEL), full(w_out), row(POOL_WIDTH), row(ATTN_WIDTH), full(wup_p), full(wup_a),
                  row(E_COLS), row(ATTN_WIDTH)],
        out_specs=[row(D_MODEL), row(D_MODEL), row(2 * D_MODEL), row(ATTN_WIDTH), row(ATTN_WIDTH),
                   row(ATTN_WIDTH), row(POOL_WIDTH)],
        out_shape=[jax.ShapeDtypeStruct((lp, D_MODEL), BF16), jax.ShapeDtypeStruct((lp, D_MODEL), BF16),
                   jax.ShapeDtypeStruct((lp, 2 * D_MODEL), BF16), jax.ShapeDtypeStruct((lp, ATTN_WIDTH), BF16),
                   jax.ShapeDtypeStruct((lp, ATTN_WIDTH), BF16), jax.ShapeDtypeStruct((lp, ATTN_WIDTH), F32),
                   jax.ShapeDtypeStruct((lp, POOL_WIDTH), F32)],
        compiler_params=_params(("parallel",)),
    )(dh1, w_out, y_pool, y_attn, wup_p, wup_a, e, o)


def _pool_bwd_local(e, dy_pool, pw, scale):
    lp = e.shape[0]
    nb = lp // ROW_TILE
    ng = len(POOL_WINDOWS)

    def body(uc_ref, up_ref, z_ref, dy_ref, pw_ref, sc_ref, dpc_ref, dz_ref, dsc_ref, dpw_ref):
        i = pl.program_id(0)

        @pl.when(i == 0)
        def _():
            dsc_ref[...] = jnp.zeros_like(dsc_ref)
            dpw_ref[...] = jnp.zeros_like(dpw_ref)

        u_cur = uc_ref[...]
        u_prev = jnp.where(i == 0, 0.0, up_ref[...])
        ps, invs = _pool_p(u_cur, u_prev, i)
        z = z_ref[...]
        sz = _sigmoid(z)
        dy = dy_ref[...]
        dypre = dy * (z * sz)
        dsilu = sz * (1.0 + z * (1.0 - sz))
        for g in range(ng):
            sl = slice(g * POOL_GROUP, (g + 1) * POOL_GROUP)
            pb = ps[g].astype(BF16)
            w = pw_ref[g]
            yraw = jnp.dot(pb, w, preferred_element_type=F32)
            sc = sc_ref[:, sl]
            dz_ref[:, sl] = (dy[:, sl] * (yraw * sc) * dsilu[:, sl]).astype(BF16)
            dsc_ref[:, sl] += jnp.sum(dypre[:, sl] * yraw, axis=0, keepdims=True)
            dyraw = (dypre[:, sl] * sc).astype(BF16)
            dpw_ref[g] += _dot_tn(pb, dyraw)
            dpc_ref[:, sl] = _dot_nt(dyraw, w) * invs[g]

    blk = (ROW_TILE, POOL_WIDTH)
    return pl.pallas_call(
        body, name="pool_bwd_local", grid=(nb,),
        in_specs=[pl.BlockSpec(blk, lambda i: (i, 0)), pl.BlockSpec(blk, lambda i: (jnp.maximum(i - 1, 0), 0)),
                  pl.BlockSpec(blk, lambda i: (i, 1)), pl.BlockSpec(blk, lambda i: (i, 0)),
                  pl.BlockSpec((ng, POOL_GROUP, POOL_GROUP), lambda i: (0, 0, 0)),
                  pl.BlockSpec((1, POOL_WIDTH), lambda i: (0, 0))],
        out_specs=[pl.BlockSpec(blk, lambda i: (i, 0)), pl.BlockSpec(blk, lambda i: (i, 0)),
                   pl.BlockSpec((1, POOL_WIDTH), lambda i: (0, 0)),
                   pl.BlockSpec((ng, POOL_GROUP, POOL_GROUP), lambda i: (0, 0, 0))],
        out_shape=[jax.ShapeDtypeStruct((lp, POOL_WIDTH), F32), jax.ShapeDtypeStruct((lp, POOL_WIDTH), BF16),
                   jax.ShapeDtypeStruct((1, POOL_WIDTH), F32),
                   jax.ShapeDtypeStruct((ng, POOL_GROUP, POOL_GROUP), F32)],
        compiler_params=_params(("arbitrary",)),
    )(e, e, e, dy_pool, pw, scale)


def _pool_bwd_window(dpc):
    lp = dpc.shape[0]
    nb = lp // ROW_TILE

    def body(cur_ref, nxt_ref, du_ref):
        i = pl.program_id(0)
        cur = cur_ref[...]
        nxt = jnp.where(i == nb - 1, 0.0, nxt_ref[...])
        pos = _pool_counts(i)
        for g, w in enumerate(POOL_WINDOWS):
            sl = slice(g * POOL_GROUP, (g + 1) * POOL_GROUP)
            xc = jnp.concatenate([cur[:, sl], nxt[:, sl]], axis=0)
            win = _leading_sums(xc, g + 1)[:ROW_TILE, :]
            dp = cur[:, sl] * jnp.minimum(pos + 1, w).astype(F32)
            du_ref[:, sl] = (win - dp).astype(BF16)

    blk = (ROW_TILE, POOL_WIDTH)
    return pl.pallas_call(
        body, name="pool_bwd_window", grid=(nb,),
        in_specs=[pl.BlockSpec(blk, lambda i: (i, 0)), pl.BlockSpec(blk, lambda i: (jnp.minimum(i + 1, nb - 1), 0))],
        out_specs=pl.BlockSpec(blk, lambda i: (i, 0)),
        out_shape=jax.ShapeDtypeStruct((lp, POOL_WIDTH), BF16),
        compiler_params=_params(("parallel",)),
    )(dpc, dpc)


def _attn_bwd(qkv, do, lse, delta, bias, nb):
    lk = qkv.shape[0]
    lp = nb * ROW_TILE
    nkb = lk // KV_TILE
    n_pairs = N_HEADS // 2
    per_kv = KV_TILE // ROW_TILE

    def body(q_ref, k_ref, v_ref, do_ref, lse_ref, dl_ref, b_ref, dq_ref, dk_ref, dv_ref, dc_ref, dcq_ref):
        jj = pl.program_id(1)

        @pl.when(jj == 0)
        def _():
            dq_ref[...] = jnp.zeros_like(dq_ref)
            dcq_ref[...] = jnp.zeros_like(dcq_ref)

        kb, vb = k_ref[...], v_ref[...]

        def block(i, carry, masked):
            dk_acc, dv_acc, dc_acc = carry
            rows = pl.ds(pl.multiple_of(i * ROW_TILE, ROW_TILE), ROW_TILE)
            qs = _stack_heads(q_ref[rows, :])
            dos = _stack_heads(do_ref[rows, :])
            lse_i, dl_i = lse_ref[rows, :], dl_ref[rows, :]
            s = _dot_nt(qs, kb)
            dp = _dot_nt(dos, vb)
            if masked:
                valid = _causal(i, jj, 1)
            ps, dss, dcs, rowsums = [], [], [], []
            for hd in range(2):
                half = slice(hd * ROW_TILE, (hd + 1) * ROW_TILE)
                col = slice(hd * HEAD_DIM, hd * HEAD_DIM + 1)
                sh = s[half] - b_ref[i, 0, 0, hd:hd + 1, :]
                if masked:
                    sh = jnp.where(valid, sh, NEG)
                p = jnp.exp(sh - lse_i[:, col])
                ds = p * (dp[half] - dl_i[:, col])
                ps.append(p.astype(BF16))
                dss.append(ds.astype(BF16))
                dcs.append(jnp.sum(ds, axis=0, keepdims=True))
                rowsums.append(jnp.sum(ds, axis=1, keepdims=True))
            dsb = jnp.concatenate(dss, axis=0)
            dv_acc = dv_acc + _dot_tn(jnp.concatenate(ps, axis=0), dos)
            dk_acc = dk_acc + _dot_tn(dsb, qs)
            dc_acc = dc_acc - jnp.concatenate(dcs, axis=0)
            dq_ref[rows, :] += _unstack_heads(jnp.dot(dsb, kb, preferred_element_type=F32))
            dcq_ref[rows, :] += _unstack_heads(jnp.broadcast_to(jnp.concatenate(rowsums, axis=0), (2 * ROW_TILE, LANES)))
            return dk_acc, dv_acc, dc_acc

        init = (jnp.zeros((KV_TILE, LANES), F32), jnp.zeros((KV_TILE, LANES), F32), jnp.zeros((2, KV_TILE), F32))
        first_q = per_kv * jj
        diag_end = jnp.minimum(first_q + per_kv, nb)
        carry = lax.fori_loop(first_q, diag_end, lambda i, c: block(i, c, True), init)
        dk, dv, dc = lax.fori_loop(diag_end, nb, lambda i, c: block(i, c, False), carry)
        dk_ref[...] = dk.astype(BF16)
        dv_ref[...] = dv.astype(BF16)
        dc_ref[0, 0] = dc

    whole = lambda rows: pl.BlockSpec((rows, LANES), lambda hp, jj: (0, hp))
    kv_blk = lambda off: pl.BlockSpec((KV_TILE, LANES), lambda hp, jj: (jj, off + hp))
    return pl.pallas_call(
        body, name="attn_bwd", grid=(n_pairs, nkb),
        in_specs=[whole(lk), kv_blk(n_pairs), kv_blk(2 * n_pairs), whole(lp), whole(lp), whole(lp),
                  pl.BlockSpec((nb, 1, 1, 2, KV_TILE), lambda hp, jj: (0, hp, jj, 0, 0))],
        out_specs=[whole(lp), kv_blk(0), kv_blk(0),
                   pl.BlockSpec((1, 1, 2, KV_TILE), lambda hp, jj: (hp, jj, 0, 0)), whole(lp)],
        out_shape=[jax.ShapeDtypeStruct((lp, ATTN_WIDTH), F32), jax.ShapeDtypeStruct((lk, ATTN_WIDTH), BF16),
                   jax.ShapeDtypeStruct((lk, ATTN_WIDTH), BF16),
                   jax.ShapeDtypeStruct((n_pairs, nkb, 2, KV_TILE), F32),
                   jax.ShapeDtypeStruct((lp, ATTN_WIDTH), F32)],
        compiler_params=_params(("parallel", "arbitrary")),
    )(qkv, qkv, qkv, do, lse, delta, bias)


def _gates_bwd(dc, f, bfg):
    nb = f.shape[0] // ROW_TILE

    def body(dc_ref, f_ref, b_ref, df_ref, db_ref, carry):
        step = pl.program_id(0)
        i = nb - 1 - step

        @pl.when(step == 0)
        def _():
            carry[...] = jnp.zeros_like(carry)
            db_ref[...] = jnp.zeros_like(db_ref)

        dcb = dc_ref[...]
        r_i = lax.broadcasted_iota(jnp.int32, (ROW_TILE, ROW_TILE), 0)
        c_i = lax.broadcasted_iota(jnp.int32, (ROW_TILE, ROW_TILE), 1)
        upper = (c_i >= r_i).astype(F32)
        dlf = jnp.dot(upper, dcb, precision=HIGHEST, preferred_element_type=F32) + carry[...]
        carry[...] = carry[...] + jnp.sum(dcb, axis=0, keepdims=True)
        logit = f_ref[...] + b_ref[...]
        dlogit = jnp.where(_valid_gate_mask(i), dlf * _sigmoid(-logit), 0.0)
        df_ref[...] = dlogit.astype(BF16)
        db_ref[...] += jnp.sum(dlogit, axis=0, keepdims=True)

    blk = pl.BlockSpec((ROW_TILE, F_COLS), lambda s: (nb - 1 - s, 0))
    one = pl.BlockSpec((1, F_COLS), lambda s: (0, 0))
    return pl.pallas_call(
        body, name="gates_bwd", grid=(nb,),
        in_specs=[blk, blk, one], out_specs=[blk, one],
        out_shape=[jax.ShapeDtypeStruct(f.shape, BF16), jax.ShapeDtypeStruct((1, F_COLS), F32)],
        scratch_shapes=[pltpu.VMEM((1, F_COLS), F32)],
        compiler_params=_params(("arbitrary",)),
    )(dc, f, bfg)


def _input_bwd(dproj, df, wt_e, wt, wt_f, x2, metapad, dh1, g1, hs, sc):
    nb = x2.shape[0] // ROW_TILE + 1
    qkv_rows = QKV_ROW0 + QKV_COLS
    n = len(hs)

    def body(dp_ref, df_ref, we_ref, w_ref, wf_ref, x_ref, mp_ref, dh_ref, g_ref, *rest):
        h_refs, s_ref = rest[:n], rest[n]
        gx_ref, g0_ref, dg_ref = rest[n + 1:n + 4]
        q_refs, sq_ref = rest[n + 4:2 * n + 4], rest[2 * n + 4]
        sems = rest[2 * n + 5:]
        i = pl.program_id(0)

        @pl.when(i == 0)
        def _():
            dg_ref[...] = jnp.zeros_like(dg_ref)
            own, sends, _ = _chip_scatter_plan(h_refs, s_ref, q_refs, sq_ref, *sems)
            for cp in own + sends:
                cp.start()

        dhn = (jnp.dot(dp_ref[:, :E_COLS], we_ref[...], preferred_element_type=F32)
               + jnp.dot(dp_ref[:, E_COLS:], w_ref[QKV_ROW0:, :], preferred_element_type=F32)
               + jnp.dot(df_ref[...], wf_ref[...], preferred_element_type=F32))
        h0 = jnp.where(i == 0, mp_ref[...], x_ref[...])
        r = lax.rsqrt(jnp.mean(h0 * h0, axis=-1, keepdims=True) + RMS_EPS)
        xhat = h0 * r
        dg_ref[...] += jnp.sum(dhn * xhat, axis=0, keepdims=True)
        dxhat = dhn * g_ref[...]
        dh0 = dh_ref[...] + r * (dxhat - xhat * jnp.mean(dxhat * xhat, axis=-1, keepdims=True))
        gx_ref[...] = dh0

        @pl.when(i == 0)
        def _():
            g0_ref[...] = dh0

        @pl.when(i == nb - 1)
        def _():
            own, sends, recvs = _chip_scatter_plan(h_refs, s_ref, q_refs, sq_ref, *sems)
            for cp in recvs:
                cp.wait_recv()
            for cp in sends:
                cp.wait_send()
            for cp in own:
                cp.wait()

    const = lambda shape: pl.BlockSpec(shape, lambda i: (0, 0))
    res = pl.pallas_call(
        body, name="input_bwd", grid=(nb,),
        in_specs=[pl.BlockSpec((ROW_TILE, MAIN_COLS), lambda i: (i, 0)), pl.BlockSpec((ROW_TILE, F_COLS), lambda i: (i, 0)),
                  const((E_COLS, D_MODEL)), const((qkv_rows, D_MODEL)), const((F_COLS, D_MODEL)),
                  _tokens_spec(), const((ROW_TILE, D_MODEL)),
                  pl.BlockSpec((ROW_TILE, D_MODEL), lambda i: (i, 0)), const((1, D_MODEL))] + [HBM] * (n + 1),
        out_specs=[_tokens_spec(), const((ROW_TILE, D_MODEL)), const((1, D_MODEL))] + [HBM] * (n + 1),
        out_shape=[jax.ShapeDtypeStruct(x2.shape, F32), jax.ShapeDtypeStruct((ROW_TILE, D_MODEL), F32),
                   jax.ShapeDtypeStruct((1, D_MODEL), F32)]
        + [jax.ShapeDtypeStruct(h.shape, h.dtype) for h in hs] + [jax.ShapeDtypeStruct((N_CHIPS,) + sc.shape, sc.dtype)],
        scratch_shapes=[pltpu.SemaphoreType.DMA((3 * (n + 1),)), pltpu.SemaphoreType.DMA((3 * (n + 1),)),
                        pltpu.SemaphoreType.DMA((n + 1,))],
        compiler_params=_params(("arbitrary",), vmem=56 * 1024 * 1024),
    )(dproj, df, wt_e, wt, wt_f, x2, metapad, dh1, g1, *_in_hbm(*hs, sc))
    return res[:3], res[3:3 + n], res[3 + n]


def _adamw(w, g, m, v, name):
    rows, cols = w.shape
    if rows % 8 == 0:
        tr, tc = _row_tile8(rows), cols
    else:
        tr, tc = rows, (2 * LANES if cols % (2 * LANES) == 0 and rows > 8 else cols)

    def body(w_ref, g_ref, m_ref, v_ref, d_ref, mo_ref, vo_ref):
        g_ = g_ref[...]
        m_new = ADAM_B1 * m_ref[...] + (1.0 - ADAM_B1) * g_
        v_new = ADAM_B2 * v_ref[...] + (1.0 - ADAM_B2) * (g_ * g_)
        m_hat = m_new / (1.0 - ADAM_B1 ** ADAM_STEP)
        v_hat = v_new / (1.0 - ADAM_B2 ** ADAM_STEP)
        d_ref[...] = -ADAM_LR * (m_hat / (jnp.sqrt(v_hat) + ADAM_EPS) + ADAM_WD * w_ref[...])
        mo_ref[...] = m_new
        vo_ref[...] = v_new

    blk = pl.BlockSpec((tr, tc), lambda i, j: (i, j))
    return pl.pallas_call(
        body, name=name, grid=(rows // tr, cols // tc),
        in_specs=[blk] * 4, out_specs=[blk] * 3,
        out_shape=[jax.ShapeDtypeStruct(w.shape, F32)] * 3,
        compiler_params=_params(("parallel", "parallel")),
    )(w, g, m, v)


def _row_tile8(rows):
    best = rows
    for t in range(8, 257, 8):
        if rows % t == 0:
            best = t
    return best


def kernel(x, meta_tokens, norm_g, w_in, b_forget, pool_w, pool_scale, w_up_pool, w_up_attn, w_out, final_norm_g, loss_target, m_meta_tokens, m_norm_g, m_w_in, m_b_forget, m_pool_w, m_pool_scale, m_w_up_pool, m_w_up_attn, m_w_out, m_final_norm_g, v_meta_tokens, v_norm_g, v_w_in, v_b_forget, v_pool_w, v_pool_scale, v_w_up_pool, v_w_up_attn, v_w_out, v_final_norm_g):
    seq = x.shape[1]
    assert seq % ROW_TILE == 0 and x.shape[0] == 1
    lp = seq + ROW_TILE
    nb = lp // ROW_TILE
    lk = -(-lp // KV_TILE) * KV_TILE
    nkb = lk // KV_TILE
    core = jnp.reshape(lax.axis_index("c"), (1,)).astype(jnp.int32)
    x2 = x[0]
    target = loss_target[0]
    sh_d = D_MODEL // N_CHIPS

    w_in_t = jnp.transpose(w_in[0])
    shard_axes = [2, 1, 1, 1, 1]
    wg_in, wg_up_p, wg_up_a, wg_out, meta_g = _gather_shards(
        [w_in_t.astype(BF16), w_up_pool[0].astype(BF16), w_up_attn[0].astype(BF16), w_out[0].astype(BF16), meta_tokens],
        [a - 1 for a in shard_axes])
    wt = wg_in.reshape(-1, D_MODEL)
    wt_e = jnp.concatenate([wt[REF_U:REF_Q], wt[REF_ZA:REF_F], wt[REF_GP:]], axis=0)
    wt_f = jnp.pad(wt[REF_F:REF_GP], ((0, F_COLS - N_HEADS), (0, 0)))
    wup_p = jnp.transpose(wg_up_p, (1, 0, 2)).reshape(POOL_WIDTH, D_MODEL)
    wup_a = jnp.transpose(wg_up_a, (1, 0, 2)).reshape(ATTN_WIDTH, D_MODEL)
    wout = wg_out.reshape(D_MODEL, D_MODEL)
    meta_full = jnp.transpose(meta_g, (1, 0, 2)).reshape(N_META, D_MODEL)
    metapad = jnp.pad(meta_full, ((PAD_ROWS, 0), (0, 0)))
    bfg = jnp.pad(b_forget, ((0, 0), (0, F_COLS - N_HEADS)))
    pw_b = pool_w[0].astype(BF16)
    gf = final_norm_g.reshape(1, D_MODEL)

    tm = _row_tile(lp, 1100)
    hn = _rmsnorm_fwd(x2, metapad, norm_g, lk)
    e = _mm_nt(hn, wt_e, F32, "in_proj_gates", lp, E_COLS, tm, 512)
    qkv = _mm_nt(hn, wt, BF16, "in_proj_qkv", lk, QKV_COLS, _row_tile(lk, 1200), 512, row_block=QKV_ROW0 // 512,
                 scale_first=HEAD_DIM ** -0.5)
    f = _mm_nt(hn, wt_f, F32, "in_proj_forget", lp, F_COLS, tm, F_COLS)
    c = _gates_fwd(f, bfg)
    c_t = jnp.transpose(c[:, :N_HEADS])
    bias = c_t[None, :, :] - jnp.transpose(c_t[:, ::ROW_TILE])[:, :, None]
    bias = jnp.where(jnp.arange(lp) < PAD_ROWS, -NEG, bias)
    bias = jnp.pad(bias, ((0, 0), (0, 0), (0, lk - lp)))
    bias = jnp.transpose(bias.reshape(nb, N_HEADS // 2, 2, nkb, KV_TILE), (0, 1, 3, 2, 4))
    y_pool = _pool_fwd(e, pw_b, pool_scale)
    o, lse = _attn_fwd(qkv, bias, nb)
    merged, y_attn = _merge_fwd(y_pool, o, e, wup_p, wup_a)
    dh1, loss_part, dgf = _head_fwd_bwd(merged, wout, x2, metapad, gf, target)

    dap, daa, dgate, dza, do, delta, dy_pool = _merge_bwd(dh1, wout, y_pool, y_attn, wup_p, wup_a, e, o)
    dpc, dzp, dscale, dpw = _pool_bwd_local(e, dy_pool, pw_b, pool_scale)
    du = _pool_bwd_window(dpc)
    dq, dk, dv, dc4, dcq = _attn_bwd(qkv, do, lse, delta, bias, nb)
    dc = jnp.transpose(dc4, (1, 3, 0, 2)).reshape(lk, N_HEADS)[:lp] + dcq[:, ::HEAD_DIM]
    df, db = _gates_bwd(jnp.pad(dc, ((0, 0), (0, F_COLS - N_HEADS))), f, bfg)
    dproj = jnp.concatenate([du, dzp, dza, dgate, (dq * HEAD_DIM ** -0.5).astype(BF16), dk[:lp], dv[:lp]], axis=1)
    tk = _row_tile(lp, 1100)
    dwt_main = _mm_tn(dproj, hn, "grad_w_in_main", lp, 1024, D_MODEL, tk)
    dwt_f = _mm_tn(df, hn, "grad_w_in_forget", lp, F_COLS, D_MODEL, tk)
    dw_out = _mm_tn(merged, dh1, "grad_w_out", lp, 512, D_MODEL, tk)
    dw_up_p = _mm_tn(y_pool, dap, "grad_w_up_pool", lp, 512, D_MODEL, tk, chunks=N_CHIPS)
    dw_up_a = _mm_tn(y_attn, daa, "grad_w_up_attn", lp, 512, D_MODEL, tk, chunks=N_CHIPS)
    r_ = lambda a, b: dwt_main[a:b]
    dwt = jnp.concatenate([r_(E_U, E_ZA), r_(E_COLS, MAIN_COLS), r_(E_ZA, E_GP), dwt_f[:N_HEADS], r_(E_GP, E_COLS)],
                          axis=0)
    gs = [dwt.reshape(N_CHIPS, -1, D_MODEL), dw_up_p, dw_up_a, dw_out.reshape(N_CHIPS, sh_d, D_MODEL)]
    grad_axes = shard_axes[:len(gs)]

    def pad8(a):
        return jnp.pad(a, ((0, (-a.shape[0]) % 8), (0, 0)))

    small_parts = [pad8(a) for a in (dgf.reshape(-1, LANES), dscale.reshape(-1, LANES), db, dpw.reshape(-1, LANES),
                                     loss_part)]
    small = jnp.concatenate(small_parts, axis=0)
    soffs = [0]
    for p in small_parts:
        soffs.append(soffs[-1] + p.shape[0])

    *rbs, sr = _sibling_exchange(gs, grad_axes, small)
    *hs, sc = _add_sibling_half(gs, rbs, grad_axes, small, sr, core)
    (grad_x, g_block0, dg1), qs, sq = _input_bwd(dproj, df, wt_e, wt, wt_f, x2, metapad, dh1, norm_g, hs, sc)
    g_w_in_t, g_w_up_p, g_w_up_a, g_w_out, st = _reduce_allgather(qs, grad_axes, sq)
    late = _small_allreduce(jnp.concatenate([pad8(dg1.reshape(-1, LANES)), g_block0[PAD_ROWS:].reshape(-1, LANES)], axis=0))
    n_norm = D_MODEL // LANES
    g_norm = late[:n_norm].reshape(1, D_MODEL)
    chip = 2 * lax.axis_index("x") + lax.axis_index("y")
    g_meta = lax.dynamic_slice_in_dim(late[n_norm:].reshape(N_META, D_MODEL), chip * sh_d, sh_d, axis=1)

    spiece = lambda k, rows: st[soffs[k]:soffs[k] + rows]
    g_final = spiece(0, D_MODEL // LANES).reshape(1, D_MODEL)
    g_scale = spiece(1, POOL_WIDTH // LANES).reshape(1, POOL_WIDTH)
    g_bf = spiece(2, 1)
    g_pw = spiece(3, POOL_WIDTH)
    loss = st[soffs[4], 0]

    def pad_lanes(a):
        return jnp.pad(a, ((0, 0), (0, F_COLS - N_HEADS)))

    upd = [
        ("meta_tokens", meta_tokens, g_meta, m_meta_tokens, v_meta_tokens),
        ("norm_g", norm_g, g_norm, m_norm_g, v_norm_g),
        ("w_in", w_in_t, g_w_in_t, jnp.transpose(m_w_in[0]), jnp.transpose(v_w_in[0])),
        ("b_forget", pad_lanes(b_forget), g_bf, pad_lanes(m_b_forget), pad_lanes(v_b_forget)),
        ("pool_w", pool_w.reshape(-1, LANES), g_pw, m_pool_w.reshape(-1, LANES), v_pool_w.reshape(-1, LANES)),
        ("pool_scale", pool_scale, g_scale, m_pool_scale, v_pool_scale),
        ("w_up_pool", w_up_pool[0], g_w_up_p, m_w_up_pool[0], v_w_up_pool[0]),
        ("w_up_attn", w_up_attn[0], g_w_up_a, m_w_up_attn[0], v_w_up_attn[0]),
        ("w_out", w_out[0], g_w_out, m_w_out[0], v_w_out[0]),
        ("final_norm_g", gf, g_final, m_final_norm_g.reshape(1, D_MODEL), v_final_norm_g.reshape(1, D_MODEL)),
    ]
    shapes = [meta_tokens.shape, norm_g.shape, w_in.shape, b_forget.shape, pool_w.shape, pool_scale.shape,
              w_up_pool.shape, w_up_attn.shape, w_out.shape, final_norm_g.shape]
    grads, deltas, new_ms, new_vs = [], [], [], []
    for (name, w_, g_, m_in, v_in), shp in zip(upd, shapes):
        d_, mn_, vn_ = _adamw(w_, g_, m_in, v_in, "adamw_" + name)
        res = (g_, d_, mn_, vn_)
        if name == "b_forget":
            res = tuple(a[:, :N_HEADS] for a in res)
        if name == "w_in":
            res = tuple(jnp.transpose(a) for a in res)
        for lst, a in zip((grads, deltas, new_ms, new_vs), res):
            lst.append(a.reshape(shp))

    return (loss, grad_x.reshape(x.shape), *grads, *deltas, *new_ms, *new_vs)
```

```python
import jax
import jax.numpy as jnp
from jax import lax
from jax.experimental import pallas as pl
from jax.experimental.pallas import tpu as pltpu

F32 = jnp.float32
BF16 = jnp.bfloat16
MESH = pl.DeviceIdType.MESH
HIGHEST = lax.Precision.HIGHEST
HBM = pl.BlockSpec(memory_space=pltpu.HBM)

D_MODEL = 1024
N_META = 16
POOL_WIDTH = 512
POOL_GROUP = 128
POOL_WINDOWS = (2, 4, 8, 16)
N_HEADS = 8
HEAD_DIM = 64
ATTN_WIDTH = 512
RMS_EPS = 1e-6
N_CHIPS = 4

ADAM_LR = 0.001
ADAM_B1 = 0.9
ADAM_B2 = 0.999
ADAM_EPS = 1e-08
ADAM_WD = 0.01
ADAM_STEP = 10

LANES = 128
ROW_TILE = 256
KV_TILE = 512
KV_TILE_BWD = 1024
PAD_ROWS = ROW_TILE - N_META
NEG = -1e30

E_U, E_ZP, E_ZA, E_GP, E_GA, E_COLS = 0, 512, 1024, 1536, 2560, 3584
QKV_COLS = 3 * ATTN_WIDTH
MAIN_COLS = E_COLS + QKV_COLS
F_COLS = LANES
REF_U, REF_Q, REF_ZA, REF_F, REF_GP = 0, 1024, 2560, 3072, 3080
QKV_ROW0 = REF_Q
VMEM_LIMIT = 48 * 1024 * 1024


def _params(sem=None, vmem=VMEM_LIMIT):
    return pltpu.CompilerParams(dimension_semantics=sem, vmem_limit_bytes=vmem)


def _in_hbm(*arrays):
    return [pltpu.with_memory_space_constraint(a, pltpu.HBM) for a in arrays]


def _row_tile(n, target):
    best = 16
    for t in range(16, target + 1, 16):
        if n % t == 0:
            best = t
    return best


def _sigmoid(x):
    return 1.0 / (1.0 + jnp.exp(-x))


def _half(ref, axis, which):
    n = ref.shape[axis] // 2
    idx = [slice(None)] * len(ref.shape)
    idx[axis] = pl.ds(pl.multiple_of(which * n, n), n)
    return ref.at[tuple(idx)]


def _half_shape(shape, axis):
    s = list(shape)
    s[axis] //= 2
    return tuple(s)


def _gather_shards(shards, axes):
    n = len(shards)

    def body(*refs):
        ins, outs = refs[:n], refs[n:2 * n]
        send, recv, fsend, frecv, local = refs[2 * n:]
        x, y, c = lax.axis_index("x"), lax.axis_index("y"), lax.axis_index("c")
        me = 2 * x + y
        sibling = (x, y, 1 - c)
        chips = [(1 - x, y), (x, 1 - y), (1 - x, 1 - y)]

        def over_ici(t, k, chip, dst_slot):
            return pltpu.make_async_remote_copy(
                src_ref=_half(ins[t], axes[t], c), dst_ref=_half(outs[t].at[dst_slot], axes[t], c),
                send_sem=send.at[3 * t + k], recv_sem=recv.at[3 * t + k], device_id=(*chip, c), device_id_type=MESH)

        def to_sibling(t, k, slot, which):
            return pltpu.make_async_remote_copy(
                src_ref=_half(outs[t].at[slot], axes[t], which), dst_ref=_half(outs[t].at[slot], axes[t], which),
                send_sem=fsend.at[3 * t + k], recv_sem=frecv.at[3 * t + k], device_id=sibling, device_id_type=MESH)

        started = []
        for t in range(n):
            mine = pltpu.make_async_copy(ins[t], outs[t].at[me], local.at[t])
            mine.start()
            started.append(mine)
        sent = []
        for t in range(n):
            for k, chip in enumerate(chips):
                cp = over_ici(t, k, chip, me)
                cp.start()
                sent.append(cp)
        for t in range(n):
            for k, (px, py) in enumerate(chips):
                over_ici(t, k, (px, py), 2 * px + py).wait_recv()
                fw = to_sibling(t, k, 2 * px + py, c)
                fw.start()
                sent.append(fw)
        for t in range(n):
            for k, (px, py) in enumerate(chips):
                to_sibling(t, k, 2 * px + py, 1 - c).wait_recv()
        for cp in sent:
            cp.wait_send()
        for cp in started:
            cp.wait()

    return pl.pallas_call(
        body, name="gather_weight_shards",
        in_specs=[HBM] * n, out_specs=[HBM] * n,
        out_shape=[jax.ShapeDtypeStruct((N_CHIPS,) + s.shape, s.dtype) for s in shards],
        scratch_shapes=[pltpu.SemaphoreType.DMA((3 * n,)), pltpu.SemaphoreType.DMA((3 * n,)),
                        pltpu.SemaphoreType.DMA((3 * n,)), pltpu.SemaphoreType.DMA((3 * n,)),
                        pltpu.SemaphoreType.DMA((n,))],
    )(*_in_hbm(*shards))


def _sibling_exchange(gs, axes, small):
    n = len(gs)

    def body(*refs):
        ins, s_ref = refs[:n], refs[n]
        outs, sr_ref = refs[n + 1:2 * n + 1], refs[2 * n + 1]
        send, recv = refs[2 * n + 2:]
        x, y, c = lax.axis_index("x"), lax.axis_index("y"), lax.axis_index("c")
        sib = (x, y, 1 - c)
        cps = [pltpu.make_async_remote_copy(src_ref=_half(ins[t], axes[t], 1 - c), dst_ref=outs[t], send_sem=send.at[t],
                                            recv_sem=recv.at[t], device_id=sib, device_id_type=MESH) for t in range(n)]
        cps.append(pltpu.make_async_remote_copy(src_ref=s_ref, dst_ref=sr_ref, send_sem=send.at[n], recv_sem=recv.at[n],
                                                device_id=sib, device_id_type=MESH))
        for cp in cps:
            cp.start()
        for cp in cps:
            cp.wait()

    return pl.pallas_call(
        body, name="grad_sibling_exchange",
        in_specs=[HBM] * (n + 1), out_specs=[HBM] * (n + 1),
        out_shape=[jax.ShapeDtypeStruct(_half_shape(g.shape, a), g.dtype) for g, a in zip(gs, axes)]
        + [jax.ShapeDtypeStruct(small.shape, small.dtype)],
        scratch_shapes=[pltpu.SemaphoreType.DMA((n + 1,)), pltpu.SemaphoreType.DMA((n + 1,))],
    )(*_in_hbm(*gs, small))


def _add_sibling_half(gs, rbs, axes, small, sr, core):
    n = len(gs)

    def body(core_ref, *refs):
        g_refs, rb_refs = refs[:n], refs[n:2 * n]
        s_ref, sr_ref = refs[2 * n], refs[2 * n + 1]
        h_refs, sc_ref = refs[2 * n + 2:3 * n + 2], refs[3 * n + 2]
        for t in range(n):
            h_refs[t][...] = (g_refs[t][...] + rb_refs[t][...]).astype(BF16)
        sc_ref[...] = s_ref[...] + sr_ref[...]

    def mine(rb, axis):
        blk = (None,) + rb.shape[1:]
        if axis == 2:
            return pl.BlockSpec(blk, lambda j, cr: (j, 0, cr[0]))
        return pl.BlockSpec(blk, lambda j, cr: (j, cr[0], 0))

    chunk = lambda rb: pl.BlockSpec((None,) + rb.shape[1:], lambda j, cr: (j, 0, 0))
    whole = pl.BlockSpec(small.shape, lambda j, cr: (0, 0))
    return pl.pallas_call(
        body, name="grad_add_sibling",
        grid_spec=pltpu.PrefetchScalarGridSpec(
            num_scalar_prefetch=1, grid=(N_CHIPS,),
            in_specs=[mine(rb, a) for rb, a in zip(rbs, axes)] + [chunk(rb) for rb in rbs] + [whole, whole],
            out_specs=[chunk(rb) for rb in rbs] + [whole]),
        out_shape=[jax.ShapeDtypeStruct(rb.shape, BF16) for rb in rbs] + [jax.ShapeDtypeStruct(small.shape, F32)],
        compiler_params=_params(("arbitrary",)),
    )(core, *gs, *rbs, small, sr)


def _chip_scatter_plan(h_refs, s_ref, q_refs, sq_ref, send, recv, local):
    n = len(h_refs)
    x, y, c = lax.axis_index("x"), lax.axis_index("y"), lax.axis_index("c")
    me = 2 * x + y
    chips = [(1 - x, y), (x, 1 - y), (1 - x, 1 - y)]

    def copy(t, k, chip, src_slot, dst_slot):
        src = h_refs[t].at[src_slot] if t < n else s_ref
        dst = (q_refs[t] if t < n else sq_ref).at[dst_slot]
        return pltpu.make_async_remote_copy(src_ref=src, dst_ref=dst, send_sem=send.at[3 * t + k],
                                            recv_sem=recv.at[3 * t + k], device_id=(*chip, c), device_id_type=MESH)

    own = [pltpu.make_async_copy(h_refs[t].at[me], q_refs[t].at[me], local.at[t]) for t in range(n)]
    own.append(pltpu.make_async_copy(s_ref, sq_ref.at[me], local.at[n]))
    sends = [copy(t, k, (px, py), 2 * px + py, me) for t in range(n + 1) for k, (px, py) in enumerate(chips)]
    recvs = [copy(t, k, (px, py), me, 2 * px + py) for t in range(n + 1) for k, (px, py) in enumerate(chips)]
    return own, sends, recvs


def _small_allreduce(buf):
    def body(b_ref, o_ref, sib_buf, chip_buf, send, recv):
        x, y, c = lax.axis_index("x"), lax.axis_index("y"), lax.axis_index("c")
        me = 2 * x + y
        chips = [(1 - x, y), (x, 1 - y), (1 - x, 1 - y)]
        swap = pltpu.make_async_remote_copy(src_ref=b_ref, dst_ref=sib_buf, send_sem=send.at[0], recv_sem=recv.at[0],
                                            device_id=(x, y, 1 - c), device_id_type=MESH)
        swap.start()
        swap.wait()
        chip_buf[me] = b_ref[...] + sib_buf[...]

        def copy(k, chip, slot):
            return pltpu.make_async_remote_copy(src_ref=chip_buf.at[slot], dst_ref=chip_buf.at[slot], send_sem=send.at[1 + k],
                                                recv_sem=recv.at[1 + k], device_id=(*chip, c), device_id_type=MESH)

        sends = [copy(k, chip, me) for k, chip in enumerate(chips)]
        for cp in sends:
            cp.start()
        for k, (px, py) in enumerate(chips):
            copy(k, (px, py), 2 * px + py).wait_recv()
        for cp in sends:
            cp.wait_send()
        o_ref[...] = ((chip_buf[0] + chip_buf[1]) + chip_buf[2]) + chip_buf[3]

    vmem = pl.BlockSpec(memory_space=pltpu.VMEM)
    return pl.pallas_call(
        body, name="small_allreduce", in_specs=[vmem], out_specs=vmem,
        out_shape=jax.ShapeDtypeStruct(buf.shape, F32),
        scratch_shapes=[pltpu.VMEM(buf.shape, F32), pltpu.VMEM((N_CHIPS,) + buf.shape, F32),
                        pltpu.SemaphoreType.DMA((4,)), pltpu.SemaphoreType.DMA((4,))],
        compiler_params=_params(),
    )(buf)


def _reduce_allgather(qs, axes, sq):
    n = len(qs)
    shard_shapes = [tuple(d * 2 if i == a - 1 else d for i, d in enumerate(q.shape[1:])) for q, a in zip(qs, axes)]

    def body(*refs):
        q_refs, sq_ref = refs[:n], refs[n]
        o_refs, st_ref = refs[n + 1:2 * n + 1], refs[2 * n + 1]
        send, recv = refs[2 * n + 2:]
        x, y, c = lax.axis_index("x"), lax.axis_index("y"), lax.axis_index("c")

        def swap(t, which):
            return pltpu.make_async_remote_copy(
                src_ref=_half(o_refs[t], axes[t] - 1, which), dst_ref=_half(o_refs[t], axes[t] - 1, which),
                send_sem=send.at[t], recv_sem=recv.at[t], device_id=(x, y, 1 - c), device_id_type=MESH)

        sent = []
        for t in range(n):
            q = q_refs[t]
            total = ((q[0].astype(F32) + q[1].astype(F32)) + q[2].astype(F32)) + q[3].astype(F32)
            _half(o_refs[t], axes[t] - 1, c)[...] = total
            cp = swap(t, c)
            cp.start()
            sent.append(cp)
        st_ref[...] = ((sq_ref[0] + sq_ref[1]) + sq_ref[2]) + sq_ref[3]
        for t in range(n):
            swap(t, 1 - c).wait_recv()
        for cp in sent:
            cp.wait_send()

    vmem = pl.BlockSpec(memory_space=pltpu.VMEM)
    return pl.pallas_call(
        body, name="grad_reduce_allgather",
        in_specs=[vmem] * (n + 1), out_specs=[vmem] * (n + 1),
        out_shape=[jax.ShapeDtypeStruct(s, F32) for s in shard_shapes] + [jax.ShapeDtypeStruct(sq.shape[1:], F32)],
        scratch_shapes=[pltpu.SemaphoreType.DMA((n,)), pltpu.SemaphoreType.DMA((n,))],
        compiler_params=_params(),
    )(*qs, sq)


def _dot_nt(a, b):
    return lax.dot_general(a, b, (((1,), (1,)), ((), ())), preferred_element_type=F32)


def _dot_tn(a, b):
    return lax.dot_general(a, b, (((0,), (0,)), ((), ())), preferred_element_type=F32)


def _mm_nt(a, bt, out_dtype, name, m, n, tm, tn, row_block=0, scale_first=None):
    k = a.shape[1]

    def body(a_ref, b_ref, o_ref):
        r = _dot_nt(a_ref[...], b_ref[...])
        if scale_first is not None:
            r = r * jnp.where(pl.program_id(0) == 0, scale_first, 1.0)
        o_ref[...] = r.astype(out_dtype)

    return pl.pallas_call(
        body, name=name, grid=(n // tn, m // tm),
        in_specs=[pl.BlockSpec((tm, k), lambda j, i: (i, 0)), pl.BlockSpec((tn, k), lambda j, i: (row_block + j, 0))],
        out_specs=pl.BlockSpec((tm, tn), lambda j, i: (i, j)),
        out_shape=jax.ShapeDtypeStruct((m, n), out_dtype),
        compiler_params=_params(("parallel", "parallel")),
    )(a, bt)


def _mm_tn(a, b, name, k, tm, tn, tk, chunks=1):
    m, n = a.shape[1], b.shape[1]
    cw = n // chunks

    def body(a_ref, b_ref, o_ref):
        @pl.when(pl.program_id(2) == 0)
        def _():
            o_ref[...] = jnp.zeros_like(o_ref)
        r = _dot_tn(a_ref[...].astype(BF16), b_ref[...].astype(BF16))
        if chunks > 1:
            for c in range(chunks):
                o_ref[c] += r[:, c * cw:(c + 1) * cw]
        else:
            o_ref[...] += r

    if chunks > 1:
        assert tn == n
        out_spec = pl.BlockSpec((chunks, tm, cw), lambda i, j, kk: (0, i, 0))
        out_shape = jax.ShapeDtypeStruct((chunks, m, cw), F32)
    else:
        out_spec = pl.BlockSpec((tm, tn), lambda i, j, kk: (i, j))
        out_shape = jax.ShapeDtypeStruct((m, n), F32)
    return pl.pallas_call(
        body, name=name, grid=(m // tm, n // tn, k // tk),
        in_specs=[pl.BlockSpec((tk, tm), lambda i, j, kk: (kk, i)), pl.BlockSpec((tk, tn), lambda i, j, kk: (kk, j))],
        out_specs=out_spec, out_shape=out_shape,
        compiler_params=_params(("parallel", "parallel", "arbitrary")),
    )(a, b)


def _tokens_spec():
    return pl.BlockSpec((ROW_TILE, D_MODEL), lambda i: (jnp.maximum(i - 1, 0), 0))


def _rmsnorm_fwd(x2, metapad, g1, lk):
    nb = x2.shape[0] // ROW_TILE + 1

    def body(x_ref, mp_ref, g_ref, hn_ref):
        i = pl.program_id(0)
        h = jnp.where(i == 0, mp_ref[...], jnp.where(i >= nb, 0.0, x_ref[...]))
        r = lax.rsqrt(jnp.mean(h * h, axis=-1, keepdims=True) + RMS_EPS)
        hn_ref[...] = ((h * r) * g_ref[...]).astype(BF16)

    return pl.pallas_call(
        body, name="rmsnorm_fwd", grid=(lk // ROW_TILE,),
        in_specs=[pl.BlockSpec((ROW_TILE, D_MODEL), lambda i: (jnp.clip(i - 1, 0, nb - 2), 0)),
                  pl.BlockSpec((ROW_TILE, D_MODEL), lambda i: (0, 0)),
                  pl.BlockSpec((1, D_MODEL), lambda i: (0, 0))],
        out_specs=pl.BlockSpec((ROW_TILE, D_MODEL), lambda i: (i, 0)),
        out_shape=jax.ShapeDtypeStruct((lk, D_MODEL), BF16),
        compiler_params=_params(("parallel",)),
    )(x2, metapad, g1)


def _valid_gate_mask(i):
    row = i * ROW_TILE + lax.broadcasted_iota(jnp.int32, (ROW_TILE, F_COLS), 0)
    col = lax.broadcasted_iota(jnp.int32, (ROW_TILE, F_COLS), 1)
    return (row >= PAD_ROWS) & (col < N_HEADS)


def _gates_fwd(f, bfg):
    nb = f.shape[0] // ROW_TILE

    def body(f_ref, b_ref, c_ref, carry):
        i = pl.program_id(0)

        @pl.when(i == 0)
        def _():
            carry[...] = jnp.zeros_like(carry)

        logit = f_ref[...] + b_ref[...]
        lf = jnp.minimum(logit, 0.0) - jnp.log1p(jnp.exp(-jnp.abs(logit)))
        lf = jnp.where(_valid_gate_mask(i), lf, 0.0)
        r_i = lax.broadcasted_iota(jnp.int32, (ROW_TILE, ROW_TILE), 0)
        c_i = lax.broadcasted_iota(jnp.int32, (ROW_TILE, ROW_TILE), 1)
        tri = (c_i <= r_i).astype(F32)
        c_ref[...] = jnp.dot(tri, lf, precision=HIGHEST, preferred_element_type=F32) + carry[...]
        carry[...] = carry[...] + jnp.sum(lf, axis=0, keepdims=True)

    return pl.pallas_call(
        body, name="gates_fwd", grid=(nb,),
        in_specs=[pl.BlockSpec((ROW_TILE, F_COLS), lambda i: (i, 0)), pl.BlockSpec((1, F_COLS), lambda i: (0, 0))],
        out_specs=pl.BlockSpec((ROW_TILE, F_COLS), lambda i: (i, 0)),
        out_shape=jax.ShapeDtypeStruct(f.shape, F32),
        scratch_shapes=[pltpu.VMEM((1, F_COLS), F32)],
        compiler_params=_params(("arbitrary",)),
    )(f, bfg)


def _pool_counts(i):
    row = i * ROW_TILE + lax.broadcasted_iota(jnp.int32, (ROW_TILE, 1), 0)
    return jnp.maximum(row - PAD_ROWS, 0)


def _trailing_sums(xc, levels):
    acc = xc
    for lv in range(levels):
        acc = acc + pltpu.roll(acc, 1 << lv, 0)
    return acc


def _leading_sums(xc, levels):
    n = xc.shape[0]
    acc = xc
    for lv in range(levels):
        acc = acc + pltpu.roll(acc, n - (1 << lv), 0)
    return acc


def _pool_p(u_cur, u_prev, i):
    pos = _pool_counts(i)
    ps, invs = [], []
    for g, w in enumerate(POOL_WINDOWS):
        sl = slice(g * POOL_GROUP, (g + 1) * POOL_GROUP)
        cur = u_cur[:, sl]
        xc = jnp.concatenate([u_prev[:, sl], cur], axis=0)
        win = _trailing_sums(xc, g + 1)[ROW_TILE:, :]
        inv = 1.0 / jnp.minimum(pos + 1, w).astype(F32)
        ps.append(win * inv - cur)
        invs.append(inv)
    return ps, invs


def _pool_fwd(e, pw, scale):
    nb = e.shape[0] // ROW_TILE

    def body(uc_ref, up_ref, z_ref, pw_ref, sc_ref, y_ref):
        i = pl.program_id(0)
        u_cur = uc_ref[...]
        u_prev = jnp.where(i == 0, 0.0, up_ref[...])
        ps, _ = _pool_p(u_cur, u_prev, i)
        z = z_ref[...]
        gate = z * _sigmoid(z)
        for g in range(len(POOL_WINDOWS)):
            sl = slice(g * POOL_GROUP, (g + 1) * POOL_GROUP)
            yraw = jnp.dot(ps[g].astype(BF16), pw_ref[g], preferred_element_type=F32)
            y_ref[:, sl] = ((yraw * sc_ref[:, sl]) * gate[:, sl]).astype(BF16)

    blk = (ROW_TILE, POOL_WIDTH)
    return pl.pallas_call(
        body, name="pool_fwd", grid=(nb,),
        in_specs=[pl.BlockSpec(blk, lambda i: (i, 0)), pl.BlockSpec(blk, lambda i: (jnp.maximum(i - 1, 0), 0)),
                  pl.BlockSpec(blk, lambda i: (i, 1)),
                  pl.BlockSpec((len(POOL_WINDOWS), POOL_GROUP, POOL_GROUP), lambda i: (0, 0, 0)),
                  pl.BlockSpec((1, POOL_WIDTH), lambda i: (0, 0))],
        out_specs=pl.BlockSpec(blk, lambda i: (i, 0)),
        out_shape=jax.ShapeDtypeStruct((e.shape[0], POOL_WIDTH), BF16),
        compiler_params=_params(("parallel",)),
    )(e, e, e, pw, scale)


def _stack_heads(a):
    first = lax.broadcasted_iota(jnp.int32, a.shape, 1) < HEAD_DIM
    zero = jnp.zeros_like(a)
    return jnp.concatenate([jnp.where(first, a, zero), jnp.where(first, zero, a)], axis=0)


def _unstack_heads(a):
    rows = a.shape[0] // 2
    first = lax.broadcasted_iota(jnp.int32, (rows, LANES), 1) < HEAD_DIM
    return jnp.where(first, a[:rows], a[rows:])


def _causal(i, jj, stacked, kv_tile):
    r = lax.broadcasted_iota(jnp.int32, (stacked * ROW_TILE, kv_tile), 0)
    if stacked == 2:
        r = jnp.where(r >= ROW_TILE, r - ROW_TILE, r)
    kidx = jj * kv_tile + lax.broadcasted_iota(jnp.int32, (stacked * ROW_TILE, kv_tile), 1)
    return kidx <= i * ROW_TILE + r


def _stack_rows(b):
    n = b.shape[1]
    return jnp.concatenate([jnp.broadcast_to(b[0:1], (ROW_TILE, n)), jnp.broadcast_to(b[1:2], (ROW_TILE, n))], axis=0)


def _attn_fwd(qkv, bias, nb):
    lk = qkv.shape[0]
    lp = nb * ROW_TILE
    nkb = lk // KV_TILE
    n_pairs = N_HEADS // 2

    def body(q_ref, k_ref, v_ref, b_ref, o_ref, lse_ref):
        i = pl.program_id(1)
        qs = _stack_heads(q_ref[...])
        last = (i * ROW_TILE) // KV_TILE

        def block(jj, carry, masked):
            m, l, acc = carry
            rows = pl.ds(pl.multiple_of(jj * KV_TILE, KV_TILE), KV_TILE)
            s = _dot_nt(qs, k_ref[rows, :]) - _stack_rows(b_ref[0, 0, jj])
            if masked:
                s = jnp.where(_causal(i, jj, 2, KV_TILE), s, NEG)
            m_new = jnp.maximum(m, jnp.max(s, axis=1, keepdims=True))
            alpha = jnp.exp(m - m_new)
            p = jnp.exp(s - m_new)
            l = alpha * l + jnp.sum(p, axis=1, keepdims=True)
            acc = alpha * acc + jnp.dot(p.astype(BF16), v_ref[rows, :], preferred_element_type=F32)
            return m_new, l, acc

        init = (jnp.full((2 * ROW_TILE, 1), NEG, F32), jnp.zeros((2 * ROW_TILE, 1), F32),
                jnp.zeros((2 * ROW_TILE, LANES), F32))
        carry = lax.fori_loop(0, last, lambda jj, c: block(jj, c, False), init)
        m, l, acc = block(last, carry, True)
        o_ref[...] = _unstack_heads(acc / l)
        lse_ref[...] = _unstack_heads(jnp.broadcast_to(m + jnp.log(l), (2 * ROW_TILE, LANES)))

    return pl.pallas_call(
        body, name="attn_fwd", grid=(n_pairs, nb),
        in_specs=[pl.BlockSpec((ROW_TILE, LANES), lambda hp, i: (i, hp)),
                  pl.BlockSpec((lk, LANES), lambda hp, i: (0, n_pairs + hp)),
                  pl.BlockSpec((lk, LANES), lambda hp, i: (0, 2 * n_pairs + hp)),
                  pl.BlockSpec((1, 1, nkb, 2, KV_TILE), lambda hp, i: (i, hp, 0, 0, 0))],
        out_specs=[pl.BlockSpec((ROW_TILE, LANES), lambda hp, i: (i, hp)),
                   pl.BlockSpec((ROW_TILE, LANES), lambda hp, i: (i, hp))],
        out_shape=[jax.ShapeDtypeStruct((lp, ATTN_WIDTH), F32), jax.ShapeDtypeStruct((lp, ATTN_WIDTH), F32)],
        compiler_params=_params(("parallel", "parallel")),
    )(qkv, qkv, qkv, bias)


def _merge_fwd(y_pool, o, e, wup_p, wup_a):
    lp = o.shape[0]
    nb = lp // ROW_TILE

    def body(yp_ref, o_ref, e_ref, wp_ref, wa_ref, mg_ref, ya_ref):
        za = e_ref[:, E_ZA:E_GP]
        ya = (o_ref[...] * (za * _sigmoid(za))).astype(BF16)
        ya_ref[...] = ya
        a_pool = jnp.dot(yp_ref[...], wp_ref[...], preferred_element_type=F32)
        a_attn = jnp.dot(ya, wa_ref[...], preferred_element_type=F32)
        mg_ref[...] = (_sigmoid(e_ref[:, E_GP:E_GA]) * a_pool + _sigmoid(e_ref[:, E_GA:E_COLS]) * a_attn).astype(BF16)

    return pl.pallas_call(
        body, name="merge_fwd", grid=(nb,),
        in_specs=[pl.BlockSpec((ROW_TILE, POOL_WIDTH), lambda i: (i, 0)),
                  pl.BlockSpec((ROW_TILE, ATTN_WIDTH), lambda i: (i, 0)),
                  pl.BlockSpec((ROW_TILE, E_COLS), lambda i: (i, 0)),
                  pl.BlockSpec((POOL_WIDTH, D_MODEL), lambda i: (0, 0)),
                  pl.BlockSpec((ATTN_WIDTH, D_MODEL), lambda i: (0, 0))],
        out_specs=[pl.BlockSpec((ROW_TILE, D_MODEL), lambda i: (i, 0)),
                   pl.BlockSpec((ROW_TILE, ATTN_WIDTH), lambda i: (i, 0))],
        out_shape=[jax.ShapeDtypeStruct((lp, D_MODEL), BF16), jax.ShapeDtypeStruct((lp, ATTN_WIDTH), BF16)],
        compiler_params=_params(("parallel",)),
    )(y_pool, o, e, wup_p, wup_a)


def _head_fwd_bwd(merged, w_out, x2, metapad, gf, target):
    lp = merged.shape[0]
    nb = lp // ROW_TILE

    def body(mg_ref, w_ref, x_ref, mp_ref, g_ref, t_ref, dh_ref, loss_ref, dg_ref):
        i = pl.program_id(0)

        @pl.when(i == 0)
        def _():
            loss_ref[...] = jnp.zeros_like(loss_ref)
            dg_ref[...] = jnp.zeros_like(dg_ref)

        h0 = jnp.where(i == 0, mp_ref[...], x_ref[...])
        h1 = h0 + jnp.dot(mg_ref[...], w_ref[...], preferred_element_type=F32)
        r = lax.rsqrt(jnp.mean(h1 * h1, axis=-1, keepdims=True) + RMS_EPS)
        xhat = h1 * r
        g = g_ref[...]
        err = jnp.where(i == 0, 0.0, xhat * g - t_ref[...])
        loss_ref[...] += 0.5 * jnp.sum(jnp.mean(err * err, axis=-1, keepdims=True))
        dy = err / D_MODEL
        dg_ref[...] += jnp.sum(dy * xhat, axis=0, keepdims=True)
        dxhat = dy * g
        dh_ref[...] = r * (dxhat - xhat * jnp.mean(dxhat * xhat, axis=-1, keepdims=True))

    return pl.pallas_call(
        body, name="head_fwd_bwd", grid=(nb,),
        in_specs=[pl.BlockSpec((ROW_TILE, D_MODEL), lambda i: (i, 0)),
                  pl.BlockSpec((D_MODEL, D_MODEL), lambda i: (0, 0)),
                  _tokens_spec(), pl.BlockSpec((ROW_TILE, D_MODEL), lambda i: (0, 0)),
                  pl.BlockSpec((1, D_MODEL), lambda i: (0, 0)), _tokens_spec()],
        out_specs=[pl.BlockSpec((ROW_TILE, D_MODEL), lambda i: (i, 0)),
                   pl.BlockSpec((1, LANES), lambda i: (0, 0)), pl.BlockSpec((1, D_MODEL), lambda i: (0, 0))],
        out_shape=[jax.ShapeDtypeStruct((lp, D_MODEL), F32), jax.ShapeDtypeStruct((1, LANES), F32),
                   jax.ShapeDtypeStruct((1, D_MODEL), F32)],
        compiler_params=_params(("arbitrary",)),
    )(merged, w_out, x2, metapad, gf, target)


def _per_head_rowsum(t):
    head = lax.broadcasted_iota(jnp.int32, t.shape, 1) // HEAD_DIM
    out = jnp.zeros_like(t)
    for h in range(N_HEADS):
        sel = head == h
        out = jnp.where(sel, jnp.sum(jnp.where(sel, t, 0.0), axis=1, keepdims=True), out)
    return out


def _merge_bwd(dh1, w_out, y_pool, y_attn, wup_p, wup_a, e, o):
    lp = o.shape[0]
    nb = lp // ROW_TILE

    def body(dh_ref, wo_ref, yp_ref, ya_ref, wp_ref, wa_ref, e_ref, o_ref,
             dap_ref, daa_ref, dg_ref, dza_ref, do_ref, delta_ref, dyp_ref):
        dmerged = _dot_nt(dh_ref[...].astype(BF16), wo_ref[...])
        a_pool = jnp.dot(yp_ref[...], wp_ref[...], preferred_element_type=F32)
        a_attn = jnp.dot(ya_ref[...], wa_ref[...], preferred_element_type=F32)
        sp = _sigmoid(e_ref[:, E_GP:E_GA])
        sa = _sigmoid(e_ref[:, E_GA:E_COLS])
        dap = (dmerged * sp).astype(BF16)
        daa = (dmerged * sa).astype(BF16)
        dap_ref[...] = dap
        daa_ref[...] = daa
        dg_ref[:, :D_MODEL] = (dmerged * a_pool * (sp * (1.0 - sp))).astype(BF16)
        dg_ref[:, D_MODEL:] = (dmerged * a_attn * (sa * (1.0 - sa))).astype(BF16)
        dyp_ref[...] = _dot_nt(dap, wp_ref[...])
        dya = _dot_nt(daa, wa_ref[...])
        za = e_ref[:, E_ZA:E_GP]
        sz = _sigmoid(za)
        o = o_ref[...]
        do = dya * (za * sz)
        do_ref[...] = do.astype(BF16)
        dza_ref[...] = (dya * o * (sz * (1.0 + za * (1.0 - sz)))).astype(BF16)
        delta_ref[...] = _per_head_rowsum(do * o)

    row = lambda w: pl.BlockSpec((ROW_TILE, w), lambda i: (i, 0))
    full = lambda a: pl.BlockSpec(a.shape, lambda i: (0, 0))
    return pl.pallas_call(
        body, name="merge_bwd", grid=(nb,),
        in_specs=[row(D_MODEL), full(w_out), row(POOL_WIDTH), row(ATTN_WIDTH), full(wup_p), full(wup_a),
                  row(E_COLS), row(ATTN_WIDTH)],
        out_specs=[row(D_MODEL), row(D_MODEL), row(2 * D_MODEL), row(ATTN_WIDTH), row(ATTN_WIDTH),
                   row(ATTN_WIDTH), row(POOL_WIDTH)],
        out_shape=[jax.ShapeDtypeStruct((lp, D_MODEL), BF16), jax.ShapeDtypeStruct((lp, D_MODEL), BF16),
                   jax.ShapeDtypeStruct((lp, 2 * D_MODEL), BF16), jax.ShapeDtypeStruct((lp, ATTN_WIDTH), BF16),
                   jax.ShapeDtypeStruct((lp, ATTN_WIDTH), BF16), jax.ShapeDtypeStruct((lp, ATTN_WIDTH), F32),
                   jax.ShapeDtypeStruct((lp, POOL_WIDTH), F32)],
        compiler_params=_params(("parallel",)),
    )(dh1, w_out, y_pool, y_attn, wup_p, wup_a, e, o)


def _pool_bwd_local(e, dy_pool, pw, scale):
    lp = e.shape[0]
    nb = lp // ROW_TILE
    ng = len(POOL_WINDOWS)

    def body(uc_ref, up_ref, z_ref, dy_ref, pw_ref, sc_ref, dpc_ref, dz_ref, dsc_ref, dpw_ref):
        i = pl.program_id(0)

        @pl.when(i == 0)
        def _():
            dsc_ref[...] = jnp.zeros_like(dsc_ref)
            dpw_ref[...] = jnp.zeros_like(dpw_ref)

        u_cur = uc_ref[...]
        u_prev = jnp.where(i == 0, 0.0, up_ref[...])
        ps, invs = _pool_p(u_cur, u_prev, i)
        z = z_ref[...]
        sz = _sigmoid(z)
        dy = dy_ref[...]
        dypre = dy * (z * sz)
        dsilu = sz * (1.0 + z * (1.0 - sz))
        for g in range(ng):
            sl = slice(g * POOL_GROUP, (g + 1) * POOL_GROUP)
            pb = ps[g].astype(BF16)
            w = pw_ref[g]
            yraw = jnp.dot(pb, w, preferred_element_type=F32)
            sc = sc_ref[:, sl]
            dz_ref[:, sl] = (dy[:, sl] * (yraw * sc) * dsilu[:, sl]).astype(BF16)
            dsc_ref[:, sl] += jnp.sum(dypre[:, sl] * yraw, axis=0, keepdims=True)
            dyraw = (dypre[:, sl] * sc).astype(BF16)
            dpw_ref[g] += _dot_tn(pb, dyraw)
            dpc_ref[:, sl] = _dot_nt(dyraw, w) * invs[g]

    blk = (ROW_TILE, POOL_WIDTH)
    return pl.pallas_call(
        body, name="pool_bwd_local", grid=(nb,),
        in_specs=[pl.BlockSpec(blk, lambda i: (i, 0)), pl.BlockSpec(blk, lambda i: (jnp.maximum(i - 1, 0), 0)),
                  pl.BlockSpec(blk, lambda i: (i, 1)), pl.BlockSpec(blk, lambda i: (i, 0)),
                  pl.BlockSpec((ng, POOL_GROUP, POOL_GROUP), lambda i: (0, 0, 0)),
                  pl.BlockSpec((1, POOL_WIDTH), lambda i: (0, 0))],
        out_specs=[pl.BlockSpec(blk, lambda i: (i, 0)), pl.BlockSpec(blk, lambda i: (i, 0)),
                   pl.BlockSpec((1, POOL_WIDTH), lambda i: (0, 0)),
                   pl.BlockSpec((ng, POOL_GROUP, POOL_GROUP), lambda i: (0, 0, 0))],
        out_shape=[jax.ShapeDtypeStruct((lp, POOL_WIDTH), F32), jax.ShapeDtypeStruct((lp, POOL_WIDTH), BF16),
                   jax.ShapeDtypeStruct((1, POOL_WIDTH), F32),
                   jax.ShapeDtypeStruct((ng, POOL_GROUP, POOL_GROUP), F32)],
        compiler_params=_params(("arbitrary",)),
    )(e, e, e, dy_pool, pw, scale)


def _pool_bwd_window(dpc):
    lp = dpc.shape[0]
    nb = lp // ROW_TILE

    def body(cur_ref, nxt_ref, du_ref):
        i = pl.program_id(0)
        cur = cur_ref[...]
        nxt = jnp.where(i == nb - 1, 0.0, nxt_ref[...])
        pos = _pool_counts(i)
        for g, w in enumerate(POOL_WINDOWS):
            sl = slice(g * POOL_GROUP, (g + 1) * POOL_GROUP)
            xc = jnp.concatenate([cur[:, sl], nxt[:, sl]], axis=0)
            win = _leading_sums(xc, g + 1)[:ROW_TILE, :]
            dp = cur[:, sl] * jnp.minimum(pos + 1, w).astype(F32)
            du_ref[:, sl] = (win - dp).astype(BF16)

    blk = (ROW_TILE, POOL_WIDTH)
    return pl.pallas_call(
        body, name="pool_bwd_window", grid=(nb,),
        in_specs=[pl.BlockSpec(blk, lambda i: (i, 0)), pl.BlockSpec(blk, lambda i: (jnp.minimum(i + 1, nb - 1), 0))],
        out_specs=pl.BlockSpec(blk, lambda i: (i, 0)),
        out_shape=jax.ShapeDtypeStruct((lp, POOL_WIDTH), BF16),
        compiler_params=_params(("parallel",)),
    )(dpc, dpc)


def _attn_bwd(qkv, do, lse, delta, bias, nb):
    lk = qkv.shape[0]
    lp = nb * ROW_TILE
    nkb = lk // KV_TILE_BWD
    n_pairs = N_HEADS // 2
    per_kv = KV_TILE_BWD // ROW_TILE

    def body(q_ref, k_ref, v_ref, do_ref, lse_ref, dl_ref, b_ref, dq_ref, dk_ref, dv_ref, dc_ref, dcq_ref):
        jj = pl.program_id(1)

        @pl.when(jj == 0)
        def _():
            dq_ref[...] = jnp.zeros_like(dq_ref)
            dcq_ref[...] = jnp.zeros_like(dcq_ref)

        kb, vb = k_ref[...], v_ref[...]

        def block(i, carry, masked):
            dk_acc, dv_acc, dc_acc = carry
            rows = pl.ds(pl.multiple_of(i * ROW_TILE, ROW_TILE), ROW_TILE)
            qs = _stack_heads(q_ref[rows, :])
            dos = _stack_heads(do_ref[rows, :])
            lse_i, dl_i = lse_ref[rows, :], dl_ref[rows, :]
            s = _dot_nt(qs, kb)
            dp = _dot_nt(dos, vb)
            if masked:
                valid = _causal(i, jj, 1, KV_TILE_BWD)
            ps, dss, dcs, rowsums = [], [], [], []
            for hd in range(2):
                half = slice(hd * ROW_TILE, (hd + 1) * ROW_TILE)
                col = slice(hd * HEAD_DIM, hd * HEAD_DIM + 1)
                sh = s[half] - b_ref[i, 0, 0, hd:hd + 1, :]
                if masked:
                    sh = jnp.where(valid, sh, NEG)
                p = jnp.exp(sh - lse_i[:, col])
                ds = p * (dp[half] - dl_i[:, col])
                ps.append(p.astype(BF16))
                dss.append(ds.astype(BF16))
                dcs.append(jnp.sum(ds, axis=0, keepdims=True))
                rowsums.append(jnp.sum(ds, axis=1, keepdims=True))
            dsb = jnp.concatenate(dss, axis=0)
            dv_acc = dv_acc + _dot_tn(jnp.concatenate(ps, axis=0), dos)
            dk_acc = dk_acc + _dot_tn(dsb, qs)
            dc_acc = dc_acc - jnp.concatenate(dcs, axis=0)
            dq_ref[rows, :] += _unstack_heads(jnp.dot(dsb, kb, preferred_element_type=F32))
            dcq_ref[rows, :] += _unstack_heads(jnp.broadcast_to(jnp.concatenate(rowsums, axis=0), (2 * ROW_TILE, LANES)))
            return dk_acc, dv_acc, dc_acc

        init = (jnp.zeros((KV_TILE_BWD, LANES), F32), jnp.zeros((KV_TILE_BWD, LANES), F32),
                jnp.zeros((2, KV_TILE_BWD), F32))
        first_q = per_kv * jj
        diag_end = jnp.minimum(first_q + per_kv, nb)
        carry = lax.fori_loop(first_q, diag_end, lambda i, c: block(i, c, True), init)
        dk, dv, dc = lax.fori_loop(diag_end, nb, lambda i, c: block(i, c, False), carry)
        dk_ref[...] = dk.astype(BF16)
        dv_ref[...] = dv.astype(BF16)
        dc_ref[0, 0] = dc

    whole = lambda rows: pl.BlockSpec((rows, LANES), lambda hp, jj: (0, hp))
    kv_blk = lambda off: pl.BlockSpec((KV_TILE_BWD, LANES), lambda hp, jj: (jj, off + hp))
    return pl.pallas_call(
        body, name="attn_bwd", grid=(n_pairs, nkb),
        in_specs=[whole(lk), kv_blk(n_pairs), kv_blk(2 * n_pairs), whole(lp), whole(lp), whole(lp),
                  pl.BlockSpec((nb, 1, 1, 2, KV_TILE_BWD), lambda hp, jj: (0, hp, jj, 0, 0))],
        out_specs=[whole(lp), kv_blk(0), kv_blk(0),
                   pl.BlockSpec((1, 1, 2, KV_TILE_BWD), lambda hp, jj: (hp, jj, 0, 0)), whole(lp)],
        out_shape=[jax.ShapeDtypeStruct((lp, ATTN_WIDTH), F32), jax.ShapeDtypeStruct((lk, ATTN_WIDTH), BF16),
                   jax.ShapeDtypeStruct((lk, ATTN_WIDTH), BF16),
                   jax.ShapeDtypeStruct((n_pairs, nkb, 2, KV_TILE_BWD), F32),
                   jax.ShapeDtypeStruct((lp, ATTN_WIDTH), F32)],
        compiler_params=_params(("parallel", "arbitrary")),
    )(qkv, qkv, qkv, do, lse, delta, bias)


def _gates_bwd(dc, dcq, f, bfg):
    nb = f.shape[0] // ROW_TILE

    def body(dc_ref, dcq_ref, f_ref, b_ref, df_ref, db_ref, carry):
        step = pl.program_id(0)
        i = nb - 1 - step

        @pl.when(step == 0)
        def _():
            carry[...] = jnp.zeros_like(carry)
            db_ref[...] = jnp.zeros_like(db_ref)

        pick = (lax.broadcasted_iota(jnp.int32, (ATTN_WIDTH, F_COLS), 0)
                == HEAD_DIM * lax.broadcasted_iota(jnp.int32, (ATTN_WIDTH, F_COLS), 1)).astype(F32)
        dcb = dc_ref[...] + jnp.dot(dcq_ref[...], pick, precision=HIGHEST, preferred_element_type=F32)
        r_i = lax.broadcasted_iota(jnp.int32, (ROW_TILE, ROW_TILE), 0)
        c_i = lax.broadcasted_iota(jnp.int32, (ROW_TILE, ROW_TILE), 1)
        upper = (c_i >= r_i).astype(F32)
        dlf = jnp.dot(upper, dcb, precision=HIGHEST, preferred_element_type=F32) + carry[...]
        carry[...] = carry[...] + jnp.sum(dcb, axis=0, keepdims=True)
        logit = f_ref[...] + b_ref[...]
        dlogit = jnp.where(_valid_gate_mask(i), dlf * _sigmoid(-logit), 0.0)
        df_ref[...] = dlogit.astype(BF16)
        db_ref[...] += jnp.sum(dlogit, axis=0, keepdims=True)

    blk = pl.BlockSpec((ROW_TILE, F_COLS), lambda s: (nb - 1 - s, 0))
    wide = pl.BlockSpec((ROW_TILE, ATTN_WIDTH), lambda s: (nb - 1 - s, 0))
    one = pl.BlockSpec((1, F_COLS), lambda s: (0, 0))
    return pl.pallas_call(
        body, name="gates_bwd", grid=(nb,),
        in_specs=[blk, wide, blk, one], out_specs=[blk, one],
        out_shape=[jax.ShapeDtypeStruct(f.shape, BF16), jax.ShapeDtypeStruct((1, F_COLS), F32)],
        scratch_shapes=[pltpu.VMEM((1, F_COLS), F32)],
        compiler_params=_params(("arbitrary",)),
    )(dc, dcq, f, bfg)


def _input_bwd(dproj, df, wt_e, wt, wt_f, x2, metapad, dh1, g1, hs, sc):
    nb = x2.shape[0] // ROW_TILE + 1
    qkv_rows = QKV_ROW0 + QKV_COLS
    n = len(hs)

    def body(dp_ref, df_ref, we_ref, w_ref, wf_ref, x_ref, mp_ref, dh_ref, g_ref, *rest):
        h_refs, s_ref = rest[:n], rest[n]
        gx_ref, g0_ref, dg_ref = rest[n + 1:n + 4]
        q_refs, sq_ref = rest[n + 4:2 * n + 4], rest[2 * n + 4]
        sems = rest[2 * n + 5:]
        i = pl.program_id(0)

        @pl.when(i == 0)
        def _():
            dg_ref[...] = jnp.zeros_like(dg_ref)
            own, sends, _ = _chip_scatter_plan(h_refs, s_ref, q_refs, sq_ref, *sems)
            for cp in own + sends:
                cp.start()

        dhn = (jnp.dot(dp_ref[:, :E_COLS], we_ref[...], preferred_element_type=F32)
               + jnp.dot(dp_ref[:, E_COLS:], w_ref[QKV_ROW0:, :], preferred_element_type=F32)
               + jnp.dot(df_ref[...], wf_ref[...], preferred_element_type=F32))
        h0 = jnp.where(i == 0, mp_ref[...], x_ref[...])
        r = lax.rsqrt(jnp.mean(h0 * h0, axis=-1, keepdims=True) + RMS_EPS)
        xhat = h0 * r
        dg_ref[...] += jnp.sum(dhn * xhat, axis=0, keepdims=True)
        dxhat = dhn * g_ref[...]
        dh0 = dh_ref[...] + r * (dxhat - xhat * jnp.mean(dxhat * xhat, axis=-1, keepdims=True))
        gx_ref[...] = dh0

        @pl.when(i == 0)
        def _():
            g0_ref[...] = dh0

        @pl.when(i == nb - 1)
        def _():
            own, sends, recvs = _chip_scatter_plan(h_refs, s_ref, q_refs, sq_ref, *sems)
            for cp in recvs:
                cp.wait_recv()
            for cp in sends:
                cp.wait_send()
            for cp in own:
                cp.wait()

    const = lambda shape: pl.BlockSpec(shape, lambda i: (0, 0))
    res = pl.pallas_call(
        body, name="input_bwd", grid=(nb,),
        in_specs=[pl.BlockSpec((ROW_TILE, MAIN_COLS), lambda i: (i, 0)), pl.BlockSpec((ROW_TILE, F_COLS), lambda i: (i, 0)),
                  const((E_COLS, D_MODEL)), const((qkv_rows, D_MODEL)), const((F_COLS, D_MODEL)),
                  _tokens_spec(), const((ROW_TILE, D_MODEL)),
                  pl.BlockSpec((ROW_TILE, D_MODEL), lambda i: (i, 0)), const((1, D_MODEL))] + [HBM] * (n + 1),
        out_specs=[_tokens_spec(), const((ROW_TILE, D_MODEL)), const((1, D_MODEL))] + [HBM] * (n + 1),
        out_shape=[jax.ShapeDtypeStruct(x2.shape, F32), jax.ShapeDtypeStruct((ROW_TILE, D_MODEL), F32),
                   jax.ShapeDtypeStruct((1, D_MODEL), F32)]
        + [jax.ShapeDtypeStruct(h.shape, h.dtype) for h in hs] + [jax.ShapeDtypeStruct((N_CHIPS,) + sc.shape, sc.dtype)],
        scratch_shapes=[pltpu.SemaphoreType.DMA((3 * (n + 1),)), pltpu.SemaphoreType.DMA((3 * (n + 1),)),
                        pltpu.SemaphoreType.DMA((n + 1,))],
        compiler_params=_params(("arbitrary",), vmem=56 * 1024 * 1024),
    )(dproj, df, wt_e, wt, wt_f, x2, metapad, dh1, g1, *_in_hbm(*hs, sc))
    return res[:3], res[3:3 + n], res[3 + n]


def _adamw(w, g, m, v, name):
    rows, cols = w.shape
    if rows % 8 == 0:
        tr, tc = _row_tile8(rows), cols
    else:
        tr, tc = rows, (2 * LANES if cols % (2 * LANES) == 0 and rows > 8 else cols)

    def body(w_ref, g_ref, m_ref, v_ref, d_ref, mo_ref, vo_ref):
        g_ = g_ref[...]
        m_new = ADAM_B1 * m_ref[...] + (1.0 - ADAM_B1) * g_
        v_new = ADAM_B2 * v_ref[...] + (1.0 - ADAM_B2) * (g_ * g_)
        m_hat = m_new / (1.0 - ADAM_B1 ** ADAM_STEP)
        v_hat = v_new / (1.0 - ADAM_B2 ** ADAM_STEP)
        d_ref[...] = -ADAM_LR * (m_hat / (jnp.sqrt(v_hat) + ADAM_EPS) + ADAM_WD * w_ref[...])
        mo_ref[...] = m_new
        vo_ref[...] = v_new

    blk = pl.BlockSpec((tr, tc), lambda i, j: (i, j))
    return pl.pallas_call(
        body, name=name, grid=(rows // tr, cols // tc),
        in_specs=[blk] * 4, out_specs=[blk] * 3,
        out_shape=[jax.ShapeDtypeStruct(w.shape, F32)] * 3,
        compiler_params=_params(("parallel", "parallel")),
    )(w, g, m, v)


def _row_tile8(rows):
    best = rows
    for t in range(8, 257, 8):
        if rows % t == 0:
            best = t
    return best


def kernel(x, meta_tokens, norm_g, w_in, b_forget, pool_w, pool_scale, w_up_pool, w_up_attn, w_out, final_norm_g, loss_target, m_meta_tokens, m_norm_g, m_w_in, m_b_forget, m_pool_w, m_pool_scale, m_w_up_pool, m_w_up_attn, m_w_out, m_final_norm_g, v_meta_tokens, v_norm_g, v_w_in, v_b_forget, v_pool_w, v_pool_scale, v_w_up_pool, v_w_up_attn, v_w_out, v_final_norm_g):
    seq = x.shape[1]
    assert seq % ROW_TILE == 0 and x.shape[0] == 1
    lp = seq + ROW_TILE
    nb = lp // ROW_TILE
    lk = -(-lp // KV_TILE_BWD) * KV_TILE_BWD
    core = jnp.reshape(lax.axis_index("c"), (1,)).astype(jnp.int32)
    x2 = x[0]
    target = loss_target[0]
    sh_d = D_MODEL // N_CHIPS

    w_in_t = jnp.transpose(w_in[0])
    shard_axes = [2, 1, 1, 1, 1]
    wg_in, wg_up_p, wg_up_a, wg_out, meta_g = _gather_shards(
        [w_in_t.astype(BF16), w_up_pool[0].astype(BF16), w_up_attn[0].astype(BF16), w_out[0].astype(BF16), meta_tokens],
        [a - 1 for a in shard_axes])
    wt = wg_in.reshape(-1, D_MODEL)
    wt_e = jnp.concatenate([wt[REF_U:REF_Q], wt[REF_ZA:REF_F], wt[REF_GP:]], axis=0)
    wt_f = jnp.pad(wt[REF_F:REF_GP], ((0, F_COLS - N_HEADS), (0, 0)))
    wup_p = jnp.transpose(wg_up_p, (1, 0, 2)).reshape(POOL_WIDTH, D_MODEL)
    wup_a = jnp.transpose(wg_up_a, (1, 0, 2)).reshape(ATTN_WIDTH, D_MODEL)
    wout = wg_out.reshape(D_MODEL, D_MODEL)
    meta_full = jnp.transpose(meta_g, (1, 0, 2)).reshape(N_META, D_MODEL)
    metapad = jnp.pad(meta_full, ((PAD_ROWS, 0), (0, 0)))
    bfg = jnp.pad(b_forget, ((0, 0), (0, F_COLS - N_HEADS)))
    pw_b = pool_w[0].astype(BF16)
    gf = final_norm_g.reshape(1, D_MODEL)

    tm = _row_tile(lp, 1100)
    hn = _rmsnorm_fwd(x2, metapad, norm_g, lk)
    e = _mm_nt(hn, wt_e, F32, "in_proj_gates", lp, E_COLS, tm, 512)
    qkv = _mm_nt(hn, wt, BF16, "in_proj_qkv", lk, QKV_COLS, _row_tile(lk, 1200), 512, row_block=QKV_ROW0 // 512,
                 scale_first=HEAD_DIM ** -0.5)
    f = _mm_nt(hn, wt_f, F32, "in_proj_forget", lp, F_COLS, tm, F_COLS)
    c = _gates_fwd(f, bfg)
    c_t = jnp.transpose(c[:, :N_HEADS])
    bias = c_t[None, :, :] - jnp.transpose(c_t[:, ::ROW_TILE])[:, :, None]
    bias = jnp.where(jnp.arange(lp) < PAD_ROWS, -NEG, bias)
    bias = jnp.pad(bias, ((0, 0), (0, 0), (0, lk - lp)))
    by_kv = lambda t: jnp.transpose(bias.reshape(nb, N_HEADS // 2, 2, lk // t, t), (0, 1, 3, 2, 4))
    bias, bias_bwd = by_kv(KV_TILE), by_kv(KV_TILE_BWD)
    y_pool = _pool_fwd(e, pw_b, pool_scale)
    o, lse = _attn_fwd(qkv, bias, nb)
    merged, y_attn = _merge_fwd(y_pool, o, e, wup_p, wup_a)
    dh1, loss_part, dgf = _head_fwd_bwd(merged, wout, x2, metapad, gf, target)

    dap, daa, dgate, dza, do, delta, dy_pool = _merge_bwd(dh1, wout, y_pool, y_attn, wup_p, wup_a, e, o)
    dpc, dzp, dscale, dpw = _pool_bwd_local(e, dy_pool, pw_b, pool_scale)
    du = _pool_bwd_window(dpc)
    dq, dk, dv, dc4, dcq = _attn_bwd(qkv, do, lse, delta, bias_bwd, nb)
    dc = jnp.transpose(dc4, (1, 3, 0, 2)).reshape(lk, N_HEADS)[:lp]
    df, db = _gates_bwd(jnp.pad(dc, ((0, 0), (0, F_COLS - N_HEADS))), dcq, f, bfg)
    dproj = jnp.concatenate([du, dzp, dza, dgate, (dq * HEAD_DIM ** -0.5).astype(BF16), dk[:lp], dv[:lp]], axis=1)
    tk = _row_tile(lp, 1100)
    dwt_main = _mm_tn(dproj, hn, "grad_w_in_main", lp, 1024, D_MODEL, tk)
    dwt_f = _mm_tn(df, hn, "grad_w_in_forget", lp, F_COLS, D_MODEL, tk)
    dw_out = _mm_tn(merged, dh1, "grad_w_out", lp, 512, D_MODEL, tk)
    dw_up_p = _mm_tn(y_pool, dap, "grad_w_up_pool", lp, 512, D_MODEL, tk, chunks=N_CHIPS)
    dw_up_a = _mm_tn(y_attn, daa, "grad_w_up_attn", lp, 512, D_MODEL, tk, chunks=N_CHIPS)
    r_ = lambda a, b: dwt_main[a:b]
    dwt = jnp.concatenate([r_(E_U, E_ZA), r_(E_COLS, MAIN_COLS), r_(E_ZA, E_GP), dwt_f[:N_HEADS], r_(E_GP, E_COLS)],
                          axis=0)
    gs = [dwt.reshape(N_CHIPS, -1, D_MODEL), dw_up_p, dw_up_a, dw_out.reshape(N_CHIPS, sh_d, D_MODEL)]
    grad_axes = shard_axes[:len(gs)]

    def pad8(a):
        return jnp.pad(a, ((0, (-a.shape[0]) % 8), (0, 0)))

    small_parts = [pad8(a) for a in (dgf.reshape(-1, LANES), dscale.reshape(-1, LANES), db, dpw.reshape(-1, LANES),
                                     loss_part)]
    small = jnp.concatenate(small_parts, axis=0)
    soffs = [0]
    for p in small_parts:
        soffs.append(soffs[-1] + p.shape[0])

    *rbs, sr = _sibling_exchange(gs, grad_axes, small)
    *hs, sc = _add_sibling_half(gs, rbs, grad_axes, small, sr, core)
    (grad_x, g_block0, dg1), qs, sq = _input_bwd(dproj, df, wt_e, wt, wt_f, x2, metapad, dh1, norm_g, hs, sc)
    g_w_in_t, g_w_up_p, g_w_up_a, g_w_out, st = _reduce_allgather(qs, grad_axes, sq)
    late = _small_allreduce(jnp.concatenate([pad8(dg1.reshape(-1, LANES)), g_block0[PAD_ROWS:].reshape(-1, LANES)], axis=0))
    n_norm = D_MODEL // LANES
    g_norm = late[:n_norm].reshape(1, D_MODEL)
    chip = 2 * lax.axis_index("x") + lax.axis_index("y")
    g_meta = lax.dynamic_slice_in_dim(late[n_norm:].reshape(N_META, D_MODEL), chip * sh_d, sh_d, axis=1)

    spiece = lambda k, rows: st[soffs[k]:soffs[k] + rows]
    g_final = spiece(0, D_MODEL // LANES).reshape(1, D_MODEL)
    g_scale = spiece(1, POOL_WIDTH // LANES).reshape(1, POOL_WIDTH)
    g_bf = spiece(2, 1)
    g_pw = spiece(3, POOL_WIDTH)
    loss = st[soffs[4], 0]

    def pad_lanes(a):
        return jnp.pad(a, ((0, 0), (0, F_COLS - N_HEADS)))

    upd = [
        ("meta_tokens", meta_tokens, g_meta, m_meta_tokens, v_meta_tokens),
        ("norm_g", norm_g, g_norm, m_norm_g, v_norm_g),
        ("w_in", w_in_t, g_w_in_t, jnp.transpose(m_w_in[0]), jnp.transpose(v_w_in[0])),
        ("b_forget", pad_lanes(b_forget), g_bf, pad_lanes(m_b_forget), pad_lanes(v_b_forget)),
        ("pool_w", pool_w.reshape(-1, LANES), g_pw, m_pool_w.reshape(-1, LANES), v_pool_w.reshape(-1, LANES)),
        ("pool_scale", pool_scale, g_scale, m_pool_scale, v_pool_scale),
        ("w_up_pool", w_up_pool[0], g_w_up_p, m_w_up_pool[0], v_w_up_pool[0]),
        ("w_up_attn", w_up_attn[0], g_w_up_a, m_w_up_attn[0], v_w_up_attn[0]),
        ("w_out", w_out[0], g_w_out, m_w_out[0], v_w_out[0]),
        ("final_norm_g", gf, g_final, m_final_norm_g.reshape(1, D_MODEL), v_final_norm_g.reshape(1, D_MODEL)),
    ]
    shapes = [meta_tokens.shape, norm_g.shape, w_in.shape, b_forget.shape, pool_w.shape, pool_scale.shape,
              w_up_pool.shape, w_up_attn.shape, w_out.shape, final_norm_g.shape]
    grads, deltas, new_ms, new_vs = [], [], [], []
    for (name, w_, g_, m_in, v_in), shp in zip(upd, shapes):
        d_, mn_, vn_ = _adamw(w_, g_, m_in, v_in, "adamw_" + name)
        res = (g_, d_, mn_, vn_)
        if name == "b_forget":
            res = tuple(a[:, :N_HEADS] for a in res)
        if name == "w_in":
            res = tuple(jnp.transpose(a) for a in res)
        for lst, a in zip((grads, deltas, new_ms, new_vs), res):
            lst.append(a.reshape(shp))

    return (loss, grad_x.reshape(x.shape), *grads, *deltas, *new_ms, *new_vs)
```

```python
import jax
import jax.numpy as jnp
from jax import lax
from jax.experimental import pallas as pl
from jax.experimental.pallas import tpu as pltpu

F32 = jnp.float32
BF16 = jnp.bfloat16
MESH = pl.DeviceIdType.MESH
HIGHEST = lax.Precision.HIGHEST
HBM = pl.BlockSpec(memory_space=pltpu.HBM)

D_MODEL = 1024
N_META = 16
POOL_WIDTH = 512
POOL_GROUP = 128
POOL_WINDOWS = (2, 4, 8, 16)
N_HEADS = 8
HEAD_DIM = 64
ATTN_WIDTH = 512
RMS_EPS = 1e-6
N_CHIPS = 4

ADAM_LR = 0.001
ADAM_B1 = 0.9
ADAM_B2 = 0.999
ADAM_EPS = 1e-08
ADAM_WD = 0.01
ADAM_STEP = 10

LANES = 128
ROW_TILE = 256
KV_TILE = 512
KV_TILE_BWD = 1024
PAD_ROWS = ROW_TILE - N_META
NEG = -1e30

E_U, E_ZP, E_ZA, E_GP, E_GA, E_COLS = 0, 512, 1024, 1536, 2560, 3584
QKV_COLS = 3 * ATTN_WIDTH
MAIN_COLS = E_COLS + QKV_COLS
F_COLS = LANES
REF_U, REF_Q, REF_ZA, REF_F, REF_GP = 0, 1024, 2560, 3072, 3080
QKV_ROW0 = REF_Q
VMEM_LIMIT = 48 * 1024 * 1024


def _params(sem=None, vmem=VMEM_LIMIT):
    return pltpu.CompilerParams(dimension_semantics=sem, vmem_limit_bytes=vmem)


def _in_hbm(*arrays):
    return [pltpu.with_memory_space_constraint(a, pltpu.HBM) for a in arrays]


def _row_tile(n, target):
    best = 16
    for t in range(16, target + 1, 16):
        if n % t == 0:
            best = t
    return best


def _sigmoid(x):
    return 1.0 / (1.0 + jnp.exp(-x))


def _half(ref, axis, which):
    n = ref.shape[axis] // 2
    idx = [slice(None)] * len(ref.shape)
    idx[axis] = pl.ds(pl.multiple_of(which * n, n), n)
    return ref.at[tuple(idx)]


def _half_shape(shape, axis):
    s = list(shape)
    s[axis] //= 2
    return tuple(s)


def _gather_start(ins, outs, axes, send, recv, fsend, frecv, local):
    x, y, c = lax.axis_index("x"), lax.axis_index("y"), lax.axis_index("c")
    me = 2 * x + y
    for t in range(len(ins)):
        pltpu.make_async_copy(ins[t], outs[t].at[me], local.at[t]).start()
        for k, chip in enumerate([(1 - x, y), (x, 1 - y), (1 - x, 1 - y)]):
            pltpu.make_async_remote_copy(
                src_ref=_half(ins[t], axes[t], c), dst_ref=_half(outs[t].at[me], axes[t], c),
                send_sem=send.at[3 * t + k], recv_sem=recv.at[3 * t + k], device_id=(*chip, c), device_id_type=MESH).start()


def _gather_finish(ins, outs, axes, send, recv, fsend, frecv, local):
    x, y, c = lax.axis_index("x"), lax.axis_index("y"), lax.axis_index("c")
    me = 2 * x + y
    sibling = (x, y, 1 - c)
    chips = [(1 - x, y), (x, 1 - y), (1 - x, 1 - y)]
    n = len(ins)

    def over_ici(t, k, chip, dst_slot):
        return pltpu.make_async_remote_copy(
            src_ref=_half(ins[t], axes[t], c), dst_ref=_half(outs[t].at[dst_slot], axes[t], c),
            send_sem=send.at[3 * t + k], recv_sem=recv.at[3 * t + k], device_id=(*chip, c), device_id_type=MESH)

    def to_sibling(t, k, slot, which):
        return pltpu.make_async_remote_copy(
            src_ref=_half(outs[t].at[slot], axes[t], which), dst_ref=_half(outs[t].at[slot], axes[t], which),
            send_sem=fsend.at[3 * t + k], recv_sem=frecv.at[3 * t + k], device_id=sibling, device_id_type=MESH)

    forwards = []
    for t in range(n):
        for k, (px, py) in enumerate(chips):
            over_ici(t, k, (px, py), 2 * px + py).wait_recv()
            fw = to_sibling(t, k, 2 * px + py, c)
            fw.start()
            forwards.append(fw)
    for t in range(n):
        for k, (px, py) in enumerate(chips):
            to_sibling(t, k, 2 * px + py, 1 - c).wait_recv()
    for t in range(n):
        for k, chip in enumerate(chips):
            over_ici(t, k, chip, me).wait_send()
    for fw in forwards:
        fw.wait_send()
    for t in range(n):
        pltpu.make_async_copy(ins[t], outs[t].at[me], local.at[t]).wait()


def _gather_sems(n):
    return [pltpu.SemaphoreType.DMA((3 * n,)), pltpu.SemaphoreType.DMA((3 * n,)), pltpu.SemaphoreType.DMA((3 * n,)),
            pltpu.SemaphoreType.DMA((3 * n,)), pltpu.SemaphoreType.DMA((n,))]


def _gathered_shapes(shards):
    return [jax.ShapeDtypeStruct((N_CHIPS,) + s.shape, s.dtype) for s in shards]


def _sibling_exchange(gs, axes, small):
    n = len(gs)

    def body(*refs):
        ins, s_ref = refs[:n], refs[n]
        outs, sr_ref = refs[n + 1:2 * n + 1], refs[2 * n + 1]
        send, recv = refs[2 * n + 2:]
        x, y, c = lax.axis_index("x"), lax.axis_index("y"), lax.axis_index("c")
        sib = (x, y, 1 - c)
        cps = [pltpu.make_async_remote_copy(src_ref=_half(ins[t], axes[t], 1 - c), dst_ref=outs[t], send_sem=send.at[t],
                                            recv_sem=recv.at[t], device_id=sib, device_id_type=MESH) for t in range(n)]
        cps.append(pltpu.make_async_remote_copy(src_ref=s_ref, dst_ref=sr_ref, send_sem=send.at[n], recv_sem=recv.at[n],
                                                device_id=sib, device_id_type=MESH))
        for cp in cps:
            cp.start()
        for cp in cps:
            cp.wait()

    return pl.pallas_call(
        body, name="grad_sibling_exchange",
        in_specs=[HBM] * (n + 1), out_specs=[HBM] * (n + 1),
        out_shape=[jax.ShapeDtypeStruct(_half_shape(g.shape, a), g.dtype) for g, a in zip(gs, axes)]
        + [jax.ShapeDtypeStruct(small.shape, small.dtype)],
        scratch_shapes=[pltpu.SemaphoreType.DMA((n + 1,)), pltpu.SemaphoreType.DMA((n + 1,))],
    )(*_in_hbm(*gs, small))


def _add_sibling_half(gs, rbs, axes, small, sr, core):
    n = len(gs)

    def body(core_ref, *refs):
        g_refs, rb_refs = refs[:n], refs[n:2 * n]
        s_ref, sr_ref = refs[2 * n], refs[2 * n + 1]
        h_refs, sc_ref = refs[2 * n + 2:3 * n + 2], refs[3 * n + 2]
        for t in range(n):
            h_refs[t][...] = (g_refs[t][...] + rb_refs[t][...]).astype(BF16)
        sc_ref[...] = s_ref[...] + sr_ref[...]

    def mine(rb, axis):
        blk = (None,) + rb.shape[1:]
        if axis == 2:
            return pl.BlockSpec(blk, lambda j, cr: (j, 0, cr[0]))
        return pl.BlockSpec(blk, lambda j, cr: (j, cr[0], 0))

    chunk = lambda rb: pl.BlockSpec((None,) + rb.shape[1:], lambda j, cr: (j, 0, 0))
    whole = pl.BlockSpec(small.shape, lambda j, cr: (0, 0))
    return pl.pallas_call(
        body, name="grad_add_sibling",
        grid_spec=pltpu.PrefetchScalarGridSpec(
            num_scalar_prefetch=1, grid=(N_CHIPS,),
            in_specs=[mine(rb, a) for rb, a in zip(rbs, axes)] + [chunk(rb) for rb in rbs] + [whole, whole],
            out_specs=[chunk(rb) for rb in rbs] + [whole]),
        out_shape=[jax.ShapeDtypeStruct(rb.shape, BF16) for rb in rbs] + [jax.ShapeDtypeStruct(small.shape, F32)],
        compiler_params=_params(("arbitrary",)),
    )(core, *gs, *rbs, small, sr)


def _chip_scatter_plan(h_refs, s_ref, q_refs, sq_ref, send, recv, local):
    n = len(h_refs)
    x, y, c = lax.axis_index("x"), lax.axis_index("y"), lax.axis_index("c")
    me = 2 * x + y
    chips = [(1 - x, y), (x, 1 - y), (1 - x, 1 - y)]

    def copy(t, k, chip, src_slot, dst_slot):
        src = h_refs[t].at[src_slot] if t < n else s_ref
        dst = (q_refs[t] if t < n else sq_ref).at[dst_slot]
        return pltpu.make_async_remote_copy(src_ref=src, dst_ref=dst, send_sem=send.at[3 * t + k],
                                            recv_sem=recv.at[3 * t + k], device_id=(*chip, c), device_id_type=MESH)

    own = [pltpu.make_async_copy(h_refs[t].at[me], q_refs[t].at[me], local.at[t]) for t in range(n)]
    own.append(pltpu.make_async_copy(s_ref, sq_ref.at[me], local.at[n]))
    sends = [copy(t, k, (px, py), 2 * px + py, me) for t in range(n + 1) for k, (px, py) in enumerate(chips)]
    recvs = [copy(t, k, (px, py), me, 2 * px + py) for t in range(n + 1) for k, (px, py) in enumerate(chips)]
    return own, sends, recvs


def _small_allreduce(buf):
    def body(b_ref, o_ref, sib_buf, chip_buf, send, recv):
        x, y, c = lax.axis_index("x"), lax.axis_index("y"), lax.axis_index("c")
        me = 2 * x + y
        chips = [(1 - x, y), (x, 1 - y), (1 - x, 1 - y)]
        swap = pltpu.make_async_remote_copy(src_ref=b_ref, dst_ref=sib_buf, send_sem=send.at[0], recv_sem=recv.at[0],
                                            device_id=(x, y, 1 - c), device_id_type=MESH)
        swap.start()
        swap.wait()
        chip_buf[me] = b_ref[...] + sib_buf[...]

        def copy(k, chip, slot):
            return pltpu.make_async_remote_copy(src_ref=chip_buf.at[slot], dst_ref=chip_buf.at[slot], send_sem=send.at[1 + k],
                                                recv_sem=recv.at[1 + k], device_id=(*chip, c), device_id_type=MESH)

        sends = [copy(k, chip, me) for k, chip in enumerate(chips)]
        for cp in sends:
            cp.start()
        for k, (px, py) in enumerate(chips):
            copy(k, (px, py), 2 * px + py).wait_recv()
        for cp in sends:
            cp.wait_send()
        o_ref[...] = ((chip_buf[0] + chip_buf[1]) + chip_buf[2]) + chip_buf[3]

    vmem = pl.BlockSpec(memory_space=pltpu.VMEM)
    return pl.pallas_call(
        body, name="small_allreduce", in_specs=[vmem], out_specs=vmem,
        out_shape=jax.ShapeDtypeStruct(buf.shape, F32),
        scratch_shapes=[pltpu.VMEM(buf.shape, F32), pltpu.VMEM((N_CHIPS,) + buf.shape, F32),
                        pltpu.SemaphoreType.DMA((4,)), pltpu.SemaphoreType.DMA((4,))],
        compiler_params=_params(),
    )(buf)


def _reduce_allgather(qs, axes, sq):
    n = len(qs)
    shard_shapes = [tuple(d * 2 if i == a - 1 else d for i, d in enumerate(q.shape[1:])) for q, a in zip(qs, axes)]

    def body(*refs):
        q_refs, sq_ref = refs[:n], refs[n]
        o_refs, st_ref = refs[n + 1:2 * n + 1], refs[2 * n + 1]
        send, recv = refs[2 * n + 2:]
        x, y, c = lax.axis_index("x"), lax.axis_index("y"), lax.axis_index("c")

        def swap(t, which):
            return pltpu.make_async_remote_copy(
                src_ref=_half(o_refs[t], axes[t] - 1, which), dst_ref=_half(o_refs[t], axes[t] - 1, which),
                send_sem=send.at[t], recv_sem=recv.at[t], device_id=(x, y, 1 - c), device_id_type=MESH)

        sent = []
        for t in range(n):
            q = q_refs[t]
            total = ((q[0].astype(F32) + q[1].astype(F32)) + q[2].astype(F32)) + q[3].astype(F32)
            _half(o_refs[t], axes[t] - 1, c)[...] = total
            cp = swap(t, c)
            cp.start()
            sent.append(cp)
        st_ref[...] = ((sq_ref[0] + sq_ref[1]) + sq_ref[2]) + sq_ref[3]
        for t in range(n):
            swap(t, 1 - c).wait_recv()
        for cp in sent:
            cp.wait_send()

    vmem = pl.BlockSpec(memory_space=pltpu.VMEM)
    return pl.pallas_call(
        body, name="grad_reduce_allgather",
        in_specs=[vmem] * (n + 1), out_specs=[vmem] * (n + 1),
        out_shape=[jax.ShapeDtypeStruct(s, F32) for s in shard_shapes] + [jax.ShapeDtypeStruct(sq.shape[1:], F32)],
        scratch_shapes=[pltpu.SemaphoreType.DMA((n,)), pltpu.SemaphoreType.DMA((n,))],
        compiler_params=_params(),
    )(*qs, sq)


def _dot_nt(a, b):
    return lax.dot_general(a, b, (((1,), (1,)), ((), ())), preferred_element_type=F32)


def _dot_tn(a, b):
    return lax.dot_general(a, b, (((0,), (0,)), ((), ())), preferred_element_type=F32)


def _mm_nt(a, bt, out_dtype, name, m, n, tm, tn, row_block=0, scale_first=None, gather=None):
    k = a.shape[1]
    shards, axes = gather if gather is not None else ([], [])
    ng = len(shards)
    grid = (n // tn, m // tm)

    def body(a_ref, b_ref, *rest):
        ins, o_ref, outs, sems = rest[:ng], rest[ng], rest[ng + 1:2 * ng + 1], rest[2 * ng + 1:]
        first = (pl.program_id(0) == 0) & (pl.program_id(1) == 0)
        last = (pl.program_id(0) == grid[0] - 1) & (pl.program_id(1) == grid[1] - 1)
        if ng:
            @pl.when(first)
            def _():
                _gather_start(ins, outs, axes, *sems)

        r = _dot_nt(a_ref[...], b_ref[...])
        if scale_first is not None:
            r = r * jnp.where(pl.program_id(0) == 0, scale_first, 1.0)
        o_ref[...] = r.astype(out_dtype)

        if ng:
            @pl.when(last)
            def _():
                _gather_finish(ins, outs, axes, *sems)

    res = pl.pallas_call(
        body, name=name, grid=grid,
        in_specs=[pl.BlockSpec((tm, k), lambda j, i: (i, 0)), pl.BlockSpec((tn, k), lambda j, i: (row_block + j, 0))]
        + [HBM] * ng,
        out_specs=[pl.BlockSpec((tm, tn), lambda j, i: (i, j))] + [HBM] * ng,
        out_shape=[jax.ShapeDtypeStruct((m, n), out_dtype)] + _gathered_shapes(shards),
        scratch_shapes=_gather_sems(ng) if ng else [],
        compiler_params=_params(("arbitrary", "arbitrary") if ng else ("parallel", "parallel")),
    )(a, bt, *_in_hbm(*shards))
    return (res[0], res[1:]) if ng else res[0]


def _mm_tn(a, b, name, k, tm, tn, tk, chunks=1):
    m, n = a.shape[1], b.shape[1]
    cw = n // chunks

    def body(a_ref, b_ref, o_ref):
        @pl.when(pl.program_id(2) == 0)
        def _():
            o_ref[...] = jnp.zeros_like(o_ref)
        r = _dot_tn(a_ref[...].astype(BF16), b_ref[...].astype(BF16))
        if chunks > 1:
            for c in range(chunks):
                o_ref[c] += r[:, c * cw:(c + 1) * cw]
        else:
            o_ref[...] += r

    if chunks > 1:
        assert tn == n
        out_spec = pl.BlockSpec((chunks, tm, cw), lambda i, j, kk: (0, i, 0))
        out_shape = jax.ShapeDtypeStruct((chunks, m, cw), F32)
    else:
        out_spec = pl.BlockSpec((tm, tn), lambda i, j, kk: (i, j))
        out_shape = jax.ShapeDtypeStruct((m, n), F32)
    return pl.pallas_call(
        body, name=name, grid=(m // tm, n // tn, k // tk),
        in_specs=[pl.BlockSpec((tk, tm), lambda i, j, kk: (kk, i)), pl.BlockSpec((tk, tn), lambda i, j, kk: (kk, j))],
        out_specs=out_spec, out_shape=out_shape,
        compiler_params=_params(("parallel", "parallel", "arbitrary")),
    )(a, b)


def _tokens_spec():
    return pl.BlockSpec((ROW_TILE, D_MODEL), lambda i: (jnp.maximum(i - 1, 0), 0))


def _rmsnorm_fwd(x2, g1, lk, shards, axes):
    nb = x2.shape[0] // ROW_TILE + 1
    nblk = lk // ROW_TILE
    ng = len(shards)
    meta_cols = shards[-1].shape[1]

    def body(x_ref, g_ref, *rest):
        ins, hn_ref, outs = rest[:ng], rest[ng], rest[ng + 1:2 * ng + 1]
        sems, meta_buf, meta_sem = rest[2 * ng + 1:2 * ng + 6], rest[2 * ng + 6], rest[2 * ng + 7]
        s = pl.program_id(0)
        blk = (s + 1) % nblk

        @pl.when(s == 0)
        def _():
            _gather_start(ins, outs, axes, *sems)

        def normed(h):
            r = lax.rsqrt(jnp.mean(h * h, axis=-1, keepdims=True) + RMS_EPS)
            return ((h * r) * g_ref[...]).astype(BF16)

        @pl.when(s < nblk - 1)
        def _():
            hn_ref[...] = normed(jnp.where(blk >= nb, 0.0, x_ref[...]))

        @pl.when(s == nblk - 1)
        def _():
            _gather_finish(ins, outs, axes, *sems)
            fetch = pltpu.make_async_copy(outs[-1], meta_buf, meta_sem.at[0])
            fetch.start()
            fetch.wait()
            meta = jnp.concatenate([meta_buf[j] for j in range(N_CHIPS)], axis=1)
            hn_ref[...] = normed(jnp.concatenate([jnp.zeros((PAD_ROWS, D_MODEL), F32), meta], axis=0))

    res = pl.pallas_call(
        body, name="rmsnorm_fwd", grid=(nblk,),
        in_specs=[pl.BlockSpec((ROW_TILE, D_MODEL), lambda s: (jnp.clip((s + 1) % nblk - 1, 0, nb - 2), 0)),
                  pl.BlockSpec((1, D_MODEL), lambda s: (0, 0))] + [HBM] * ng,
        out_specs=[pl.BlockSpec((ROW_TILE, D_MODEL), lambda s: ((s + 1) % nblk, 0))] + [HBM] * ng,
        out_shape=[jax.ShapeDtypeStruct((lk, D_MODEL), BF16)] + _gathered_shapes(shards),
        scratch_shapes=_gather_sems(ng) + [pltpu.VMEM((N_CHIPS, N_META, meta_cols), F32), pltpu.SemaphoreType.DMA((1,))],
        compiler_params=_params(("arbitrary",)),
    )(x2, g1, *_in_hbm(*shards))
    return res[0], res[1:]


def _valid_gate_mask(i):
    row = i * ROW_TILE + lax.broadcasted_iota(jnp.int32, (ROW_TILE, F_COLS), 0)
    col = lax.broadcasted_iota(jnp.int32, (ROW_TILE, F_COLS), 1)
    return (row >= PAD_ROWS) & (col < N_HEADS)


def _gates_fwd(f, bfg):
    nb = f.shape[0] // ROW_TILE

    def body(f_ref, b_ref, c_ref, carry):
        i = pl.program_id(0)

        @pl.when(i == 0)
        def _():
            carry[...] = jnp.zeros_like(carry)

        logit = f_ref[...] + b_ref[...]
        lf = jnp.minimum(logit, 0.0) - jnp.log1p(jnp.exp(-jnp.abs(logit)))
        lf = jnp.where(_valid_gate_mask(i), lf, 0.0)
        r_i = lax.broadcasted_iota(jnp.int32, (ROW_TILE, ROW_TILE), 0)
        c_i = lax.broadcasted_iota(jnp.int32, (ROW_TILE, ROW_TILE), 1)
        tri = (c_i <= r_i).astype(F32)
        c_ref[...] = jnp.dot(tri, lf, precision=HIGHEST, preferred_element_type=F32) + carry[...]
        carry[...] = carry[...] + jnp.sum(lf, axis=0, keepdims=True)

    return pl.pallas_call(
        body, name="gates_fwd", grid=(nb,),
        in_specs=[pl.BlockSpec((ROW_TILE, F_COLS), lambda i: (i, 0)), pl.BlockSpec((1, F_COLS), lambda i: (0, 0))],
        out_specs=pl.BlockSpec((ROW_TILE, F_COLS), lambda i: (i, 0)),
        out_shape=jax.ShapeDtypeStruct(f.shape, F32),
        scratch_shapes=[pltpu.VMEM((1, F_COLS), F32)],
        compiler_params=_params(("arbitrary",)),
    )(f, bfg)


def _pool_counts(i):
    row = i * ROW_TILE + lax.broadcasted_iota(jnp.int32, (ROW_TILE, 1), 0)
    return jnp.maximum(row - PAD_ROWS, 0)


def _trailing_sums(xc, levels):
    acc = xc
    for lv in range(levels):
        acc = acc + pltpu.roll(acc, 1 << lv, 0)
    return acc


def _leading_sums(xc, levels):
    n = xc.shape[0]
    acc = xc
    for lv in range(levels):
        acc = acc + pltpu.roll(acc, n - (1 << lv), 0)
    return acc


def _pool_p(u_cur, u_prev, i):
    pos = _pool_counts(i)
    ps, invs = [], []
    for g, w in enumerate(POOL_WINDOWS):
        sl = slice(g * POOL_GROUP, (g + 1) * POOL_GROUP)
        cur = u_cur[:, sl]
        xc = jnp.concatenate([u_prev[:, sl], cur], axis=0)
        win = _trailing_sums(xc, g + 1)[ROW_TILE:, :]
        inv = 1.0 / jnp.minimum(pos + 1, w).astype(F32)
        ps.append(win * inv - cur)
        invs.append(inv)
    return ps, invs


def _pool_fwd(e, pw, scale):
    nb = e.shape[0] // ROW_TILE

    def body(uc_ref, up_ref, z_ref, pw_ref, sc_ref, y_ref):
        i = pl.program_id(0)
        u_cur = uc_ref[...]
        u_prev = jnp.where(i == 0, 0.0, up_ref[...])
        ps, _ = _pool_p(u_cur, u_prev, i)
        z = z_ref[...]
        gate = z * _sigmoid(z)
        for g in range(len(POOL_WINDOWS)):
            sl = slice(g * POOL_GROUP, (g + 1) * POOL_GROUP)
            yraw = jnp.dot(ps[g].astype(BF16), pw_ref[g], preferred_element_type=F32)
            y_ref[:, sl] = ((yraw * sc_ref[:, sl]) * gate[:, sl]).astype(BF16)

    blk = (ROW_TILE, POOL_WIDTH)
    return pl.pallas_call(
        body, name="pool_fwd", grid=(nb,),
        in_specs=[pl.BlockSpec(blk, lambda i: (i, 0)), pl.BlockSpec(blk, lambda i: (jnp.maximum(i - 1, 0), 0)),
                  pl.BlockSpec(blk, lambda i: (i, 1)),
                  pl.BlockSpec((len(POOL_WINDOWS), POOL_GROUP, POOL_GROUP), lambda i: (0, 0, 0)),
                  pl.BlockSpec((1, POOL_WIDTH), lambda i: (0, 0))],
        out_specs=pl.BlockSpec(blk, lambda i: (i, 0)),
        out_shape=jax.ShapeDtypeStruct((e.shape[0], POOL_WIDTH), BF16),
        compiler_params=_params(("parallel",)),
    )(e, e, e, pw, scale)


def _stack_heads(a):
    first = lax.broadcasted_iota(jnp.int32, a.shape, 1) < HEAD_DIM
    zero = jnp.zeros_like(a)
    return jnp.concatenate([jnp.where(first, a, zero), jnp.where(first, zero, a)], axis=0)


def _unstack_heads(a):
    rows = a.shape[0] // 2
    first = lax.broadcasted_iota(jnp.int32, (rows, LANES), 1) < HEAD_DIM
    return jnp.where(first, a[:rows], a[rows:])


def _causal(i, jj, stacked, kv_tile):
    r = lax.broadcasted_iota(jnp.int32, (stacked * ROW_TILE, kv_tile), 0)
    if stacked == 2:
        r = jnp.where(r >= ROW_TILE, r - ROW_TILE, r)
    kidx = jj * kv_tile + lax.broadcasted_iota(jnp.int32, (stacked * ROW_TILE, kv_tile), 1)
    return kidx <= i * ROW_TILE + r


def _stack_rows(b):
    n = b.shape[1]
    return jnp.concatenate([jnp.broadcast_to(b[0:1], (ROW_TILE, n)), jnp.broadcast_to(b[1:2], (ROW_TILE, n))], axis=0)


def _attn_fwd(qkv, bias, nb):
    lk = qkv.shape[0]
    lp = nb * ROW_TILE
    nkb = lk // KV_TILE
    n_pairs = N_HEADS // 2

    def body(q_ref, k_ref, v_ref, b_ref, o_ref, lse_ref):
        i = pl.program_id(1)
        qs = _stack_heads(q_ref[...])
        last = (i * ROW_TILE) // KV_TILE

        def block(jj, carry, masked):
            m, l, acc = carry
            rows = pl.ds(pl.multiple_of(jj * KV_TILE, KV_TILE), KV_TILE)
            s = _dot_nt(qs, k_ref[rows, :]) - _stack_rows(b_ref[0, 0, jj])
            if masked:
                s = jnp.where(_causal(i, jj, 2, KV_TILE), s, NEG)
            m_new = jnp.maximum(m, jnp.max(s, axis=1, keepdims=True))
            alpha = jnp.exp(m - m_new)
            p = jnp.exp(s - m_new)
            l = alpha * l + jnp.sum(p, axis=1, keepdims=True)
            acc = alpha * acc + jnp.dot(p.astype(BF16), v_ref[rows, :], preferred_element_type=F32)
            return m_new, l, acc

        init = (jnp.full((2 * ROW_TILE, 1), NEG, F32), jnp.zeros((2 * ROW_TILE, 1), F32),
                jnp.zeros((2 * ROW_TILE, LANES), F32))
        carry = lax.fori_loop(0, last, lambda jj, c: block(jj, c, False), init)
        m, l, acc = block(last, carry, True)
        o_ref[...] = _unstack_heads(acc / l)
        lse_ref[...] = _unstack_heads(jnp.broadcast_to(m + jnp.log(l), (2 * ROW_TILE, LANES)))

    return pl.pallas_call(
        body, name="attn_fwd", grid=(n_pairs, nb),
        in_specs=[pl.BlockSpec((ROW_TILE, LANES), lambda hp, i: (i, hp)),
                  pl.BlockSpec((lk, LANES), lambda hp, i: (0, n_pairs + hp)),
                  pl.BlockSpec((lk, LANES), lambda hp, i: (0, 2 * n_pairs + hp)),
                  pl.BlockSpec((1, 1, nkb, 2, KV_TILE), lambda hp, i: (i, hp, 0, 0, 0))],
        out_specs=[pl.BlockSpec((ROW_TILE, LANES), lambda hp, i: (i, hp)),
                   pl.BlockSpec((ROW_TILE, LANES), lambda hp, i: (i, hp))],
        out_shape=[jax.ShapeDtypeStruct((lp, ATTN_WIDTH), F32), jax.ShapeDtypeStruct((lp, ATTN_WIDTH), F32)],
        compiler_params=_params(("parallel", "parallel")),
    )(qkv, qkv, qkv, bias)


def _merge_fwd(y_pool, o, e, wup_p, wup_a):
    lp = o.shape[0]
    nb = lp // ROW_TILE

    def body(yp_ref, o_ref, e_ref, wp_ref, wa_ref, mg_ref, ya_ref):
        za = e_ref[:, E_ZA:E_GP]
        ya = (o_ref[...] * (za * _sigmoid(za))).astype(BF16)
        ya_ref[...] = ya
        a_pool = jnp.dot(yp_ref[...], wp_ref[...], preferred_element_type=F32)
        a_attn = jnp.dot(ya, wa_ref[...], preferred_element_type=F32)
        mg_ref[...] = (_sigmoid(e_ref[:, E_GP:E_GA]) * a_pool + _sigmoid(e_ref[:, E_GA:E_COLS]) * a_attn).astype(BF16)

    return pl.pallas_call(
        body, name="merge_fwd", grid=(nb,),
        in_specs=[pl.BlockSpec((ROW_TILE, POOL_WIDTH), lambda i: (i, 0)),
                  pl.BlockSpec((ROW_TILE, ATTN_WIDTH), lambda i: (i, 0)),
                  pl.BlockSpec((ROW_TILE, E_COLS), lambda i: (i, 0)),
                  pl.BlockSpec((POOL_WIDTH, D_MODEL), lambda i: (0, 0)),
                  pl.BlockSpec((ATTN_WIDTH, D_MODEL), lambda i: (0, 0))],
        out_specs=[pl.BlockSpec((ROW_TILE, D_MODEL), lambda i: (i, 0)),
                   pl.BlockSpec((ROW_TILE, ATTN_WIDTH), lambda i: (i, 0))],
        out_shape=[jax.ShapeDtypeStruct((lp, D_MODEL), BF16), jax.ShapeDtypeStruct((lp, ATTN_WIDTH), BF16)],
        compiler_params=_params(("parallel",)),
    )(y_pool, o, e, wup_p, wup_a)


def _head_fwd_bwd(merged, w_out, x2, metapad, gf, target):
    lp = merged.shape[0]
    nb = lp // ROW_TILE

    def body(mg_ref, w_ref, x_ref, mp_ref, g_ref, t_ref, dh_ref, loss_ref, dg_ref):
        i = pl.program_id(0)

        @pl.when(i == 0)
        def _():
            loss_ref[...] = jnp.zeros_like(loss_ref)
            dg_ref[...] = jnp.zeros_like(dg_ref)

        h0 = jnp.where(i == 0, mp_ref[...], x_ref[...])
        h1 = h0 + jnp.dot(mg_ref[...], w_ref[...], preferred_element_type=F32)
        r = lax.rsqrt(jnp.mean(h1 * h1, axis=-1, keepdims=True) + RMS_EPS)
        xhat = h1 * r
        g = g_ref[...]
        err = jnp.where(i == 0, 0.0, xhat * g - t_ref[...])
        loss_ref[...] += 0.5 * jnp.sum(jnp.mean(err * err, axis=-1, keepdims=True))
        dy = err / D_MODEL
        dg_ref[...] += jnp.sum(dy * xhat, axis=0, keepdims=True)
        dxhat = dy * g
        dh_ref[...] = r * (dxhat - xhat * jnp.mean(dxhat * xhat, axis=-1, keepdims=True))

    return pl.pallas_call(
        body, name="head_fwd_bwd", grid=(nb,),
        in_specs=[pl.BlockSpec((ROW_TILE, D_MODEL), lambda i: (i, 0)),
                  pl.BlockSpec((D_MODEL, D_MODEL), lambda i: (0, 0)),
                  _tokens_spec(), pl.BlockSpec((ROW_TILE, D_MODEL), lambda i: (0, 0)),
                  pl.BlockSpec((1, D_MODEL), lambda i: (0, 0)), _tokens_spec()],
        out_specs=[pl.BlockSpec((ROW_TILE, D_MODEL), lambda i: (i, 0)),
                   pl.BlockSpec((1, LANES), lambda i: (0, 0)), pl.BlockSpec((1, D_MODEL), lambda i: (0, 0))],
        out_shape=[jax.ShapeDtypeStruct((lp, D_MODEL), F32), jax.ShapeDtypeStruct((1, LANES), F32),
                   jax.ShapeDtypeStruct((1, D_MODEL), F32)],
        compiler_params=_params(("arbitrary",)),
    )(merged, w_out, x2, metapad, gf, target)


def _per_head_rowsum(t):
    head = lax.broadcasted_iota(jnp.int32, t.shape, 1) // HEAD_DIM
    out = jnp.zeros_like(t)
    for h in range(N_HEADS):
        sel = head == h
        out = jnp.where(sel, jnp.sum(jnp.where(sel, t, 0.0), axis=1, keepdims=True), out)
    return out


def _merge_bwd(dh1, w_out, y_pool, y_attn, wup_p, wup_a, e, o):
    lp = o.shape[0]
    nb = lp // ROW_TILE

    def body(dh_ref, wo_ref, yp_ref, ya_ref, wp_ref, wa_ref, e_ref, o_ref,
             dap_ref, daa_ref, dg_ref, dza_ref, do_ref, delta_ref, dyp_ref):
        dmerged = _dot_nt(dh_ref[...].astype(BF16), wo_ref[...])
        a_pool = jnp.dot(yp_ref[...], wp_ref[...], preferred_element_type=F32)
        a_attn = jnp.dot(ya_ref[...], wa_ref[...], preferred_element_type=F32)
        sp = _sigmoid(e_ref[:, E_GP:E_GA])
        sa = _sigmoid(e_ref[:, E_GA:E_COLS])
        dap = (dmerged * sp).astype(BF16)
        daa = (dmerged * sa).astype(BF16)
        dap_ref[...] = dap
        daa_ref[...] = daa
        dg_ref[:, :D_MODEL] = (dmerged * a_pool * (sp * (1.0 - sp))).astype(BF16)
        dg_ref[:, D_MODEL:] = (dmerged * a_attn * (sa * (1.0 - sa))).astype(BF16)
        dyp_ref[...] = _dot_nt(dap, wp_ref[...])
        dya = _dot_nt(daa, wa_ref[...])
        za = e_ref[:, E_ZA:E_GP]
        sz = _sigmoid(za)
        o = o_ref[...]
        do = dya * (za * sz)
        do_ref[...] = do.astype(BF16)
        dza_ref[...] = (dya * o * (sz * (1.0 + za * (1.0 - sz)))).astype(BF16)
        delta_ref[...] = _per_head_rowsum(do * o)

    row = lambda w: pl.BlockSpec((ROW_TILE, w), lambda i: (i, 0))
    full = lambda a: pl.BlockSpec(a.shape, lambda i: (0, 0))
    return pl.pallas_call(
        body, name="merge_bwd", grid=(nb,),
        in_specs=[row(D_MODEL), full(w_out), row(POOL_WIDTH), row(ATTN_WIDTH), full(wup_p), full(wup_a),
                  row(E_COLS), row(ATTN_WIDTH)],
        out_specs=[row(D_MODEL), row(D_MODEL), row(2 * D_MODEL), row(ATTN_WIDTH), row(ATTN_WIDTH),
                   row(ATTN_WIDTH), row(POOL_WIDTH)],
        out_shape=[jax.ShapeDtypeStruct((lp, D_MODEL), BF16), jax.ShapeDtypeStruct((lp, D_MODEL), BF16),
                   jax.ShapeDtypeStruct((lp, 2 * D_MODEL), BF16), jax.ShapeDtypeStruct((lp, ATTN_WIDTH), BF16),
                   jax.ShapeDtypeStruct((lp, ATTN_WIDTH), BF16), jax.ShapeDtypeStruct((lp, ATTN_WIDTH), F32),
                   jax.ShapeDtypeStruct((lp, POOL_WIDTH), F32)],
        compiler_params=_params(("parallel",)),
    )(dh1, w_out, y_pool, y_attn, wup_p, wup_a, e, o)


def _pool_bwd_local(e, dy_pool, pw, scale):
    lp = e.shape[0]
    nb = lp // ROW_TILE
    ng = len(POOL_WINDOWS)

    def body(uc_ref, up_ref, z_ref, dy_ref, pw_ref, sc_ref, dpc_ref, dz_ref, dsc_ref, dpw_ref):
        i = pl.program_id(0)

        @pl.when(i == 0)
        def _():
            dsc_ref[...] = jnp.zeros_like(dsc_ref)
            dpw_ref[...] = jnp.zeros_like(dpw_ref)

        u_cur = uc_ref[...]
        u_prev = jnp.where(i == 0, 0.0, up_ref[...])
        ps, invs = _pool_p(u_cur, u_prev, i)
        z = z_ref[...]
        sz = _sigmoid(z)
        dy = dy_ref[...]
        dypre = dy * (z * sz)
        dsilu = sz * (1.0 + z * (1.0 - sz))
        for g in range(ng):
            sl = slice(g * POOL_GROUP, (g + 1) * POOL_GROUP)
            pb = ps[g].astype(BF16)
            w = pw_ref[g]
            yraw = jnp.dot(pb, w, preferred_element_type=F32)
            sc = sc_ref[:, sl]
            dz_ref[:, sl] = (dy[:, sl] * (yraw * sc) * dsilu[:, sl]).astype(BF16)
            dsc_ref[:, sl] += jnp.sum(dypre[:, sl] * yraw, axis=0, keepdims=True)
            dyraw = (dypre[:, sl] * sc).astype(BF16)
            dpw_ref[g] += _dot_tn(pb, dyraw)
            dpc_ref[:, sl] = _dot_nt(dyraw, w) * invs[g]

    blk = (ROW_TILE, POOL_WIDTH)
    return pl.pallas_call(
        body, name="pool_bwd_local", grid=(nb,),
        in_specs=[pl.BlockSpec(blk, lambda i: (i, 0)), pl.BlockSpec(blk, lambda i: (jnp.maximum(i - 1, 0), 0)),
                  pl.BlockSpec(blk, lambda i: (i, 1)), pl.BlockSpec(blk, lambda i: (i, 0)),
                  pl.BlockSpec((ng, POOL_GROUP, POOL_GROUP), lambda i: (0, 0, 0)),
                  pl.BlockSpec((1, POOL_WIDTH), lambda i: (0, 0))],
        out_specs=[pl.BlockSpec(blk, lambda i: (i, 0)), pl.BlockSpec(blk, lambda i: (i, 0)),
                   pl.BlockSpec((1, POOL_WIDTH), lambda i: (0, 0)),
                   pl.BlockSpec((ng, POOL_GROUP, POOL_GROUP), lambda i: (0, 0, 0))],
        out_shape=[jax.ShapeDtypeStruct((lp, POOL_WIDTH), F32), jax.ShapeDtypeStruct((lp, POOL_WIDTH), BF16),
                   jax.ShapeDtypeStruct((1, POOL_WIDTH), F32),
                   jax.ShapeDtypeStruct((ng, POOL_GROUP, POOL_GROUP), F32)],
        compiler_params=_params(("arbitrary",)),
    )(e, e, e, dy_pool, pw, scale)


def _pool_bwd_window(dpc):
    lp = dpc.shape[0]
    nb = lp // ROW_TILE

    def body(cur_ref, nxt_ref, du_ref):
        i = pl.program_id(0)
        cur = cur_ref[...]
        nxt = jnp.where(i == nb - 1, 0.0, nxt_ref[...])
        pos = _pool_counts(i)
        for g, w in enumerate(POOL_WINDOWS):
            sl = slice(g * POOL_GROUP, (g + 1) * POOL_GROUP)
            xc = jnp.concatenate([cur[:, sl], nxt[:, sl]], axis=0)
            win = _leading_sums(xc, g + 1)[:ROW_TILE, :]
            dp = cur[:, sl] * jnp.minimum(pos + 1, w).astype(F32)
            du_ref[:, sl] = (win - dp).astype(BF16)

    blk = (ROW_TILE, POOL_WIDTH)
    return pl.pallas_call(
        body, name="pool_bwd_window", grid=(nb,),
        in_specs=[pl.BlockSpec(blk, lambda i: (i, 0)), pl.BlockSpec(blk, lambda i: (jnp.minimum(i + 1, nb - 1), 0))],
        out_specs=pl.BlockSpec(blk, lambda i: (i, 0)),
        out_shape=jax.ShapeDtypeStruct((lp, POOL_WIDTH), BF16),
        compiler_params=_params(("parallel",)),
    )(dpc, dpc)


def _attn_bwd(qkv, do, lse, delta, bias, nb):
    lk = qkv.shape[0]
    lp = nb * ROW_TILE
    nkb = lk // KV_TILE_BWD
    n_pairs = N_HEADS // 2
    per_kv = KV_TILE_BWD // ROW_TILE

    def body(q_ref, k_ref, v_ref, do_ref, lse_ref, dl_ref, b_ref, dq_ref, dk_ref, dv_ref, dc_ref, dcq_ref):
        jj = pl.program_id(1)

        @pl.when(jj == 0)
        def _():
            dq_ref[...] = jnp.zeros_like(dq_ref)
            dcq_ref[...] = jnp.zeros_like(dcq_ref)

        kb, vb = k_ref[...], v_ref[...]

        def block(i, carry, masked):
            dk_acc, dv_acc, dc_acc = carry
            rows = pl.ds(pl.multiple_of(i * ROW_TILE, ROW_TILE), ROW_TILE)
            qs = _stack_heads(q_ref[rows, :])
            dos = _stack_heads(do_ref[rows, :])
            lse_i, dl_i = lse_ref[rows, :], dl_ref[rows, :]
            s = _dot_nt(qs, kb)
            dp = _dot_nt(dos, vb)
            if masked:
                valid = _causal(i, jj, 1, KV_TILE_BWD)
            ps, dss, dcs, rowsums = [], [], [], []
            for hd in range(2):
                half = slice(hd * ROW_TILE, (hd + 1) * ROW_TILE)
                col = slice(hd * HEAD_DIM, hd * HEAD_DIM + 1)
                sh = s[half] - b_ref[i, 0, 0, hd:hd + 1, :]
                if masked:
                    sh = jnp.where(valid, sh, NEG)
                p = jnp.exp(sh - lse_i[:, col])
                ds = p * (dp[half] - dl_i[:, col])
                ps.append(p.astype(BF16))
                dss.append(ds.astype(BF16))
                dcs.append(jnp.sum(ds, axis=0, keepdims=True))
                rowsums.append(jnp.sum(ds, axis=1, keepdims=True))
            dsb = jnp.concatenate(dss, axis=0)
            dv_acc = dv_acc + _dot_tn(jnp.concatenate(ps, axis=0), dos)
            dk_acc = dk_acc + _dot_tn(dsb, qs)
            dc_acc = dc_acc - jnp.concatenate(dcs, axis=0)
            dq_ref[rows, :] += _unstack_heads(jnp.dot(dsb, kb, preferred_element_type=F32))
            dcq_ref[rows, :] += _unstack_heads(jnp.broadcast_to(jnp.concatenate(rowsums, axis=0), (2 * ROW_TILE, LANES)))
            return dk_acc, dv_acc, dc_acc

        init = (jnp.zeros((KV_TILE_BWD, LANES), F32), jnp.zeros((KV_TILE_BWD, LANES), F32),
                jnp.zeros((2, KV_TILE_BWD), F32))
        first_q = per_kv * jj
        diag_end = jnp.minimum(first_q + per_kv, nb)
        carry = lax.fori_loop(first_q, diag_end, lambda i, c: block(i, c, True), init)
        dk, dv, dc = lax.fori_loop(diag_end, nb, lambda i, c: block(i, c, False), carry)
        dk_ref[...] = dk.astype(BF16)
        dv_ref[...] = dv.astype(BF16)
        dc_ref[0, 0] = dc

    whole = lambda rows: pl.BlockSpec((rows, LANES), lambda hp, jj: (0, hp))
    kv_blk = lambda off: pl.BlockSpec((KV_TILE_BWD, LANES), lambda hp, jj: (jj, off + hp))
    return pl.pallas_call(
        body, name="attn_bwd", grid=(n_pairs, nkb),
        in_specs=[whole(lk), kv_blk(n_pairs), kv_blk(2 * n_pairs), whole(lp), whole(lp), whole(lp),
                  pl.BlockSpec((nb, 1, 1, 2, KV_TILE_BWD), lambda hp, jj: (0, hp, jj, 0, 0))],
        out_specs=[whole(lp), kv_blk(0), kv_blk(0),
                   pl.BlockSpec((1, 1, 2, KV_TILE_BWD), lambda hp, jj: (hp, jj, 0, 0)), whole(lp)],
        out_shape=[jax.ShapeDtypeStruct((lp, ATTN_WIDTH), F32), jax.ShapeDtypeStruct((lk, ATTN_WIDTH), BF16),
                   jax.ShapeDtypeStruct((lk, ATTN_WIDTH), BF16),
                   jax.ShapeDtypeStruct((n_pairs, nkb, 2, KV_TILE_BWD), F32),
                   jax.ShapeDtypeStruct((lp, ATTN_WIDTH), F32)],
        compiler_params=_params(("parallel", "arbitrary")),
    )(qkv, qkv, qkv, do, lse, delta, bias)


def _gates_bwd(dc, dcq, f, bfg):
    nb = f.shape[0] // ROW_TILE

    def body(dc_ref, dcq_ref, f_ref, b_ref, df_ref, db_ref, carry):
        step = pl.program_id(0)
        i = nb - 1 - step

        @pl.when(step == 0)
        def _():
            carry[...] = jnp.zeros_like(carry)
            db_ref[...] = jnp.zeros_like(db_ref)

        pick = (lax.broadcasted_iota(jnp.int32, (ATTN_WIDTH, F_COLS), 0)
                == HEAD_DIM * lax.broadcasted_iota(jnp.int32, (ATTN_WIDTH, F_COLS), 1)).astype(F32)
        dcb = dc_ref[...] + jnp.dot(dcq_ref[...], pick, precision=HIGHEST, preferred_element_type=F32)
        r_i = lax.broadcasted_iota(jnp.int32, (ROW_TILE, ROW_TILE), 0)
        c_i = lax.broadcasted_iota(jnp.int32, (ROW_TILE, ROW_TILE), 1)
        upper = (c_i >= r_i).astype(F32)
        dlf = jnp.dot(upper, dcb, precision=HIGHEST, preferred_element_type=F32) + carry[...]
        carry[...] = carry[...] + jnp.sum(dcb, axis=0, keepdims=True)
        logit = f_ref[...] + b_ref[...]
        dlogit = jnp.where(_valid_gate_mask(i), dlf * _sigmoid(-logit), 0.0)
        df_ref[...] = dlogit.astype(BF16)
        db_ref[...] += jnp.sum(dlogit, axis=0, keepdims=True)

    blk = pl.BlockSpec((ROW_TILE, F_COLS), lambda s: (nb - 1 - s, 0))
    wide = pl.BlockSpec((ROW_TILE, ATTN_WIDTH), lambda s: (nb - 1 - s, 0))
    one = pl.BlockSpec((1, F_COLS), lambda s: (0, 0))
    return pl.pallas_call(
        body, name="gates_bwd", grid=(nb,),
        in_specs=[blk, wide, blk, one], out_specs=[blk, one],
        out_shape=[jax.ShapeDtypeStruct(f.shape, BF16), jax.ShapeDtypeStruct((1, F_COLS), F32)],
        scratch_shapes=[pltpu.VMEM((1, F_COLS), F32)],
        compiler_params=_params(("arbitrary",)),
    )(dc, dcq, f, bfg)


def _input_bwd(dproj, df, wt_e, wt, wt_f, x2, metapad, dh1, g1, hs, sc):
    nb = x2.shape[0] // ROW_TILE + 1
    qkv_rows = QKV_ROW0 + QKV_COLS
    n = len(hs)

    def body(dp_ref, df_ref, we_ref, w_ref, wf_ref, x_ref, mp_ref, dh_ref, g_ref, *rest):
        h_refs, s_ref = rest[:n], rest[n]
        gx_ref, g0_ref, dg_ref = rest[n + 1:n + 4]
        q_refs, sq_ref = rest[n + 4:2 * n + 4], rest[2 * n + 4]
        sems = rest[2 * n + 5:]
        i = pl.program_id(0)

        @pl.when(i == 0)
        def _():
            dg_ref[...] = jnp.zeros_like(dg_ref)
            own, sends, _ = _chip_scatter_plan(h_refs, s_ref, q_refs, sq_ref, *sems)
            for cp in own + sends:
                cp.start()

        dhn = (jnp.dot(dp_ref[:, :E_COLS], we_ref[...], preferred_element_type=F32)
               + jnp.dot(dp_ref[:, E_COLS:], w_ref[QKV_ROW0:, :], preferred_element_type=F32)
               + jnp.dot(df_ref[...], wf_ref[...], preferred_element_type=F32))
        h0 = jnp.where(i == 0, mp_ref[...], x_ref[...])
        r = lax.rsqrt(jnp.mean(h0 * h0, axis=-1, keepdims=True) + RMS_EPS)
        xhat = h0 * r
        dg_ref[...] += jnp.sum(dhn * xhat, axis=0, keepdims=True)
        dxhat = dhn * g_ref[...]
        dh0 = dh_ref[...] + r * (dxhat - xhat * jnp.mean(dxhat * xhat, axis=-1, keepdims=True))
        gx_ref[...] = dh0

        @pl.when(i == 0)
        def _():
            g0_ref[...] = dh0

        @pl.when(i == nb - 1)
        def _():
            own, sends, recvs = _chip_scatter_plan(h_refs, s_ref, q_refs, sq_ref, *sems)
            for cp in recvs:
                cp.wait_recv()
            for cp in sends:
                cp.wait_send()
            for cp in own:
                cp.wait()

    const = lambda shape: pl.BlockSpec(shape, lambda i: (0, 0))
    res = pl.pallas_call(
        body, name="input_bwd", grid=(nb,),
        in_specs=[pl.BlockSpec((ROW_TILE, MAIN_COLS), lambda i: (i, 0)), pl.BlockSpec((ROW_TILE, F_COLS), lambda i: (i, 0)),
                  const((E_COLS, D_MODEL)), const((qkv_rows, D_MODEL)), const((F_COLS, D_MODEL)),
                  _tokens_spec(), const((ROW_TILE, D_MODEL)),
                  pl.BlockSpec((ROW_TILE, D_MODEL), lambda i: (i, 0)), const((1, D_MODEL))] + [HBM] * (n + 1),
        out_specs=[_tokens_spec(), const((ROW_TILE, D_MODEL)), const((1, D_MODEL))] + [HBM] * (n + 1),
        out_shape=[jax.ShapeDtypeStruct(x2.shape, F32), jax.ShapeDtypeStruct((ROW_TILE, D_MODEL), F32),
                   jax.ShapeDtypeStruct((1, D_MODEL), F32)]
        + [jax.ShapeDtypeStruct(h.shape, h.dtype) for h in hs] + [jax.ShapeDtypeStruct((N_CHIPS,) + sc.shape, sc.dtype)],
        scratch_shapes=[pltpu.SemaphoreType.DMA((3 * (n + 1),)), pltpu.SemaphoreType.DMA((3 * (n + 1),)),
                        pltpu.SemaphoreType.DMA((n + 1,))],
        compiler_params=_params(("arbitrary",), vmem=56 * 1024 * 1024),
    )(dproj, df, wt_e, wt, wt_f, x2, metapad, dh1, g1, *_in_hbm(*hs, sc))
    return res[:3], res[3:3 + n], res[3 + n]


def _adamw(w, g, m, v, name):
    rows, cols = w.shape
    if rows % 8 == 0:
        tr, tc = _row_tile8(rows), cols
    else:
        tr, tc = rows, (2 * LANES if cols % (2 * LANES) == 0 and rows > 8 else cols)

    def body(w_ref, g_ref, m_ref, v_ref, d_ref, mo_ref, vo_ref):
        g_ = g_ref[...]
        m_new = ADAM_B1 * m_ref[...] + (1.0 - ADAM_B1) * g_
        v_new = ADAM_B2 * v_ref[...] + (1.0 - ADAM_B2) * (g_ * g_)
        m_hat = m_new / (1.0 - ADAM_B1 ** ADAM_STEP)
        v_hat = v_new / (1.0 - ADAM_B2 ** ADAM_STEP)
        d_ref[...] = -ADAM_LR * (m_hat / (jnp.sqrt(v_hat) + ADAM_EPS) + ADAM_WD * w_ref[...])
        mo_ref[...] = m_new
        vo_ref[...] = v_new

    blk = pl.BlockSpec((tr, tc), lambda i, j: (i, j))
    return pl.pallas_call(
        body, name=name, grid=(rows // tr, cols // tc),
        in_specs=[blk] * 4, out_specs=[blk] * 3,
        out_shape=[jax.ShapeDtypeStruct(w.shape, F32)] * 3,
        compiler_params=_params(("parallel", "parallel")),
    )(w, g, m, v)


def _row_tile8(rows):
    best = rows
    for t in range(8, 257, 8):
        if rows % t == 0:
            best = t
    return best


def kernel(x, meta_tokens, norm_g, w_in, b_forget, pool_w, pool_scale, w_up_pool, w_up_attn, w_out, final_norm_g, loss_target, m_meta_tokens, m_norm_g, m_w_in, m_b_forget, m_pool_w, m_pool_scale, m_w_up_pool, m_w_up_attn, m_w_out, m_final_norm_g, v_meta_tokens, v_norm_g, v_w_in, v_b_forget, v_pool_w, v_pool_scale, v_w_up_pool, v_w_up_attn, v_w_out, v_final_norm_g):
    seq = x.shape[1]
    assert seq % ROW_TILE == 0 and x.shape[0] == 1
    lp = seq + ROW_TILE
    nb = lp // ROW_TILE
    lk = -(-lp // KV_TILE_BWD) * KV_TILE_BWD
    core = jnp.reshape(lax.axis_index("c"), (1,)).astype(jnp.int32)
    x2 = x[0]
    target = loss_target[0]
    sh_d = D_MODEL // N_CHIPS

    w_in_t = jnp.transpose(w_in[0])
    gf = final_norm_g.reshape(1, D_MODEL)
    bfg = jnp.pad(b_forget, ((0, 0), (0, F_COLS - N_HEADS)))
    pw_b = pool_w[0].astype(BF16)
    hn, (wg_in, meta_g) = _rmsnorm_fwd(x2, norm_g, lk, [w_in_t.astype(BF16), meta_tokens], [1, 0])
    wt = wg_in.reshape(-1, D_MODEL)
    wt_e = jnp.concatenate([wt[REF_U:REF_Q], wt[REF_ZA:REF_F], wt[REF_GP:]], axis=0)
    wt_f = jnp.pad(wt[REF_F:REF_GP], ((0, F_COLS - N_HEADS), (0, 0)))
    meta_full = jnp.transpose(meta_g, (1, 0, 2)).reshape(N_META, D_MODEL)
    metapad = jnp.pad(meta_full, ((PAD_ROWS, 0), (0, 0)))

    tm = _row_tile(lp, 1100)
    e, (wg_up_p, wg_up_a, wg_out) = _mm_nt(
        hn, wt_e, F32, "in_proj_gates", lp, E_COLS, tm, 512,
        gather=([w_up_pool[0].astype(BF16), w_up_attn[0].astype(BF16), w_out[0].astype(BF16)], [0, 0, 0]))
    wup_p = jnp.transpose(wg_up_p, (1, 0, 2)).reshape(POOL_WIDTH, D_MODEL)
    wup_a = jnp.transpose(wg_up_a, (1, 0, 2)).reshape(ATTN_WIDTH, D_MODEL)
    wout = wg_out.reshape(D_MODEL, D_MODEL)
    qkv = _mm_nt(hn, wt, BF16, "in_proj_qkv", lk, QKV_COLS, _row_tile(lk, 1300), 512, row_block=QKV_ROW0 // 512,
                 scale_first=HEAD_DIM ** -0.5)
    f = _mm_nt(hn, wt_f, F32, "in_proj_forget", lp, F_COLS, tm, F_COLS)
    c = _gates_fwd(f, bfg)
    c_t = jnp.transpose(c[:, :N_HEADS])
    bias = c_t[None, :, :] - jnp.transpose(c_t[:, ::ROW_TILE])[:, :, None]
    bias = jnp.where(jnp.arange(lp) < PAD_ROWS, -NEG, bias)
    bias = jnp.pad(bias, ((0, 0), (0, 0), (0, lk - lp)))
    by_kv = lambda t: jnp.transpose(bias.reshape(nb, N_HEADS // 2, 2, lk // t, t), (0, 1, 3, 2, 4))
    bias, bias_bwd = by_kv(KV_TILE), by_kv(KV_TILE_BWD)
    y_pool = _pool_fwd(e, pw_b, pool_scale)
    o, lse = _attn_fwd(qkv, bias, nb)
    merged, y_attn = _merge_fwd(y_pool, o, e, wup_p, wup_a)
    dh1, loss_part, dgf = _head_fwd_bwd(merged, wout, x2, metapad, gf, target)

    dap, daa, dgate, dza, do, delta, dy_pool = _merge_bwd(dh1, wout, y_pool, y_attn, wup_p, wup_a, e, o)
    dpc, dzp, dscale, dpw = _pool_bwd_local(e, dy_pool, pw_b, pool_scale)
    du = _pool_bwd_window(dpc)
    dq, dk, dv, dc4, dcq = _attn_bwd(qkv, do, lse, delta, bias_bwd, nb)
    dc = jnp.transpose(dc4, (1, 3, 0, 2)).reshape(lk, N_HEADS)[:lp]
    df, db = _gates_bwd(jnp.pad(dc, ((0, 0), (0, F_COLS - N_HEADS))), dcq, f, bfg)
    dproj = jnp.concatenate([du, dzp, dza, dgate, (dq * HEAD_DIM ** -0.5).astype(BF16), dk[:lp], dv[:lp]], axis=1)
    tk = _row_tile(lp, 1100)
    dwt_main = _mm_tn(dproj, hn, "grad_w_in_main", lp, 1024, D_MODEL, tk)
    dwt_f = _mm_tn(df, hn, "grad_w_in_forget", lp, F_COLS, D_MODEL, tk)
    dw_out = _mm_tn(merged, dh1, "grad_w_out", lp, 512, D_MODEL, tk)
    dw_up_p = _mm_tn(y_pool, dap, "grad_w_up_pool", lp, 512, D_MODEL, tk, chunks=N_CHIPS)
    dw_up_a = _mm_tn(y_attn, daa, "grad_w_up_attn", lp, 512, D_MODEL, tk, chunks=N_CHIPS)
    r_ = lambda a, b: dwt_main[a:b]
    dwt = jnp.concatenate([r_(E_U, E_ZA), r_(E_COLS, MAIN_COLS), r_(E_ZA, E_GP), dwt_f[:N_HEADS], r_(E_GP, E_COLS)],
                          axis=0)
    gs = [dwt.reshape(N_CHIPS, -1, D_MODEL), dw_up_p, dw_up_a, dw_out.reshape(N_CHIPS, sh_d, D_MODEL)]
    grad_axes = [2, 1, 1, 1]

    def pad8(a):
        return jnp.pad(a, ((0, (-a.shape[0]) % 8), (0, 0)))

    small_parts = [pad8(a) for a in (dgf.reshape(-1, LANES), dscale.reshape(-1, LANES), db, dpw.reshape(-1, LANES),
                                     loss_part)]
    small = jnp.concatenate(small_parts, axis=0)
    soffs = [0]
    for p in small_parts:
        soffs.append(soffs[-1] + p.shape[0])

    *rbs, sr = _sibling_exchange(gs, grad_axes, small)
    *hs, sc = _add_sibling_half(gs, rbs, grad_axes, small, sr, core)
    (grad_x, g_block0, dg1), qs, sq = _input_bwd(dproj, df, wt_e, wt, wt_f, x2, metapad, dh1, norm_g, hs, sc)
    g_w_in_t, g_w_up_p, g_w_up_a, g_w_out, st = _reduce_allgather(qs, grad_axes, sq)
    late = _small_allreduce(jnp.concatenate([pad8(dg1.reshape(-1, LANES)), g_block0[PAD_ROWS:].reshape(-1, LANES)], axis=0))
    n_norm = D_MODEL // LANES
    g_norm = late[:n_norm].reshape(1, D_MODEL)
    chip = 2 * lax.axis_index("x") + lax.axis_index("y")
    g_meta = lax.dynamic_slice_in_dim(late[n_norm:].reshape(N_META, D_MODEL), chip * sh_d, sh_d, axis=1)

    spiece = lambda k, rows: st[soffs[k]:soffs[k] + rows]
    g_final = spiece(0, D_MODEL // LANES).reshape(1, D_MODEL)
    g_scale = spiece(1, POOL_WIDTH // LANES).reshape(1, POOL_WIDTH)
    g_bf = spiece(2, 1)
    g_pw = spiece(3, POOL_WIDTH)
    loss = st[soffs[4], 0]

    def pad_lanes(a):
        return jnp.pad(a, ((0, 0), (0, F_COLS - N_HEADS)))

    upd = [
        ("meta_tokens", meta_tokens, g_meta, m_meta_tokens, v_meta_tokens),
        ("norm_g", norm_g, g_norm, m_norm_g, v_norm_g),
        ("w_in", w_in_t, g_w_in_t, jnp.transpose(m_w_in[0]), jnp.transpose(v_w_in[0])),
        ("b_forget", pad_lanes(b_forget), g_bf, pad_lanes(m_b_forget), pad_lanes(v_b_forget)),
        ("pool_w", pool_w.reshape(-1, LANES), g_pw, m_pool_w.reshape(-1, LANES), v_pool_w.reshape(-1, LANES)),
        ("pool_scale", pool_scale, g_scale, m_pool_scale, v_pool_scale),
        ("w_up_pool", w_up_pool[0], g_w_up_p, m_w_up_pool[0], v_w_up_pool[0]),
        ("w_up_attn", w_up_attn[0], g_w_up_a, m_w_up_attn[0], v_w_up_attn[0]),
        ("w_out", w_out[0], g_w_out, m_w_out[0], v_w_out[0]),
        ("final_norm_g", gf, g_final, m_final_norm_g.reshape(1, D_MODEL), v_final_norm_g.reshape(1, D_MODEL)),
    ]
    shapes = [meta_tokens.shape, norm_g.shape, w_in.shape, b_forget.shape, pool_w.shape, pool_scale.shape,
              w_up_pool.shape, w_up_attn.shape, w_out.shape, final_norm_g.shape]
    grads, deltas, new_ms, new_vs = [], [], [], []
    for (name, w_, g_, m_in, v_in), shp in zip(upd, shapes):
        d_, mn_, vn_ = _adamw(w_, g_, m_in, v_in, "adamw_" + name)
        res = (g_, d_, mn_, vn_)
        if name == "b_forget":
            res = tuple(a[:, :N_HEADS] for a in res)
        if name == "w_in":
            res = tuple(jnp.transpose(a) for a in res)
        for lst, a in zip((grads, deltas, new_ms, new_vs), res):
            lst.append(a.reshape(shp))

    return (loss, grad_x.reshape(x.shape), *grads, *deltas, *new_ms, *new_vs)
```

```python
import jax
import jax.numpy as jnp
from jax import lax
from jax.experimental import pallas as pl
from jax.experimental.pallas import tpu as pltpu

F32 = jnp.float32
BF16 = jnp.bfloat16
MESH = pl.DeviceIdType.MESH
HIGHEST = lax.Precision.HIGHEST
HBM = pl.BlockSpec(memory_space=pltpu.HBM)

D_MODEL = 1024
N_META = 16
POOL_WIDTH = 512
POOL_GROUP = 128
POOL_WINDOWS = (2, 4, 8, 16)
N_HEADS = 8
HEAD_DIM = 64
ATTN_WIDTH = 512
RMS_EPS = 1e-6
N_CHIPS = 4

ADAM_LR = 0.001
ADAM_B1 = 0.9
ADAM_B2 = 0.999
ADAM_EPS = 1e-08
ADAM_WD = 0.01
ADAM_STEP = 10

LANES = 128
ROW_TILE = 256
KV_TILE = 512
KV_TILE_BWD = 1024
PAD_ROWS = ROW_TILE - N_META
NEG = -1e30

E_U, E_ZP, E_ZA, E_GP, E_GA, E_COLS = 0, 512, 1024, 1536, 2560, 3584
QKV_COLS = 3 * ATTN_WIDTH
MAIN_COLS = E_COLS + QKV_COLS
F_COLS = LANES
REF_U, REF_Q, REF_ZA, REF_F, REF_GP = 0, 1024, 2560, 3072, 3080
QKV_ROW0 = REF_Q
VMEM_LIMIT = 48 * 1024 * 1024


def _params(sem=None, vmem=VMEM_LIMIT):
    return pltpu.CompilerParams(dimension_semantics=sem, vmem_limit_bytes=vmem)


def _in_hbm(*arrays):
    return [pltpu.with_memory_space_constraint(a, pltpu.HBM) for a in arrays]


def _row_tile(n, target):
    best = 16
    for t in range(16, target + 1, 16):
        if n % t == 0:
            best = t
    return best


def _sigmoid(x):
    return 1.0 / (1.0 + jnp.exp(-x))


def _half(ref, axis, which):
    n = ref.shape[axis] // 2
    idx = [slice(None)] * len(ref.shape)
    idx[axis] = pl.ds(pl.multiple_of(which * n, n), n)
    return ref.at[tuple(idx)]


def _half_shape(shape, axis):
    s = list(shape)
    s[axis] //= 2
    return tuple(s)


def _gather_start(ins, outs, axes, send, recv, fsend, frecv, local):
    x, y, c = lax.axis_index("x"), lax.axis_index("y"), lax.axis_index("c")
    me = 2 * x + y
    for t in range(len(ins)):
        pltpu.make_async_copy(ins[t], outs[t].at[me], local.at[t]).start()
        for k, chip in enumerate([(1 - x, y), (x, 1 - y), (1 - x, 1 - y)]):
            pltpu.make_async_remote_copy(
                src_ref=_half(ins[t], axes[t], c), dst_ref=_half(outs[t].at[me], axes[t], c),
                send_sem=send.at[3 * t + k], recv_sem=recv.at[3 * t + k], device_id=(*chip, c), device_id_type=MESH).start()


def _gather_finish(ins, outs, axes, send, recv, fsend, frecv, local):
    x, y, c = lax.axis_index("x"), lax.axis_index("y"), lax.axis_index("c")
    me = 2 * x + y
    sibling = (x, y, 1 - c)
    chips = [(1 - x, y), (x, 1 - y), (1 - x, 1 - y)]
    n = len(ins)

    def over_ici(t, k, chip, dst_slot):
        return pltpu.make_async_remote_copy(
            src_ref=_half(ins[t], axes[t], c), dst_ref=_half(outs[t].at[dst_slot], axes[t], c),
            send_sem=send.at[3 * t + k], recv_sem=recv.at[3 * t + k], device_id=(*chip, c), device_id_type=MESH)

    def to_sibling(t, k, slot, which):
        return pltpu.make_async_remote_copy(
            src_ref=_half(outs[t].at[slot], axes[t], which), dst_ref=_half(outs[t].at[slot], axes[t], which),
            send_sem=fsend.at[3 * t + k], recv_sem=frecv.at[3 * t + k], device_id=sibling, device_id_type=MESH)

    forwards = []
    for t in range(n):
        for k, (px, py) in enumerate(chips):
            over_ici(t, k, (px, py), 2 * px + py).wait_recv()
            fw = to_sibling(t, k, 2 * px + py, c)
            fw.start()
            forwards.append(fw)
    for t in range(n):
        for k, (px, py) in enumerate(chips):
            to_sibling(t, k, 2 * px + py, 1 - c).wait_recv()
    for t in range(n):
        for k, chip in enumerate(chips):
            over_ici(t, k, chip, me).wait_send()
    for fw in forwards:
        fw.wait_send()
    for t in range(n):
        pltpu.make_async_copy(ins[t], outs[t].at[me], local.at[t]).wait()


def _gather_sems(n):
    return [pltpu.SemaphoreType.DMA((3 * n,)), pltpu.SemaphoreType.DMA((3 * n,)), pltpu.SemaphoreType.DMA((3 * n,)),
            pltpu.SemaphoreType.DMA((3 * n,)), pltpu.SemaphoreType.DMA((n,))]


def _gathered_shapes(shards):
    return [jax.ShapeDtypeStruct((N_CHIPS,) + s.shape, s.dtype) for s in shards]


def _swap_copies(srcs, dsts, axes, send, recv):
    x, y, c = lax.axis_index("x"), lax.axis_index("y"), lax.axis_index("c")
    return [pltpu.make_async_remote_copy(
        src_ref=srcs[t] if axes[t] is None else _half(srcs[t], axes[t], 1 - c), dst_ref=dsts[t],
        send_sem=send.at[t], recv_sem=recv.at[t], device_id=(x, y, 1 - c), device_id_type=MESH) for t in range(len(srcs))]


def _swap_shapes(srcs, axes):
    return [jax.ShapeDtypeStruct(g.shape if a is None else _half_shape(g.shape, a), g.dtype) for g, a in zip(srcs, axes)]


def _sibling_exchange(gs, axes):
    n = len(gs)

    def body(*refs):
        cps = _swap_copies(refs[:n], refs[n:2 * n], axes, *refs[2 * n:])
        for cp in cps:
            cp.start()
        for cp in cps:
            cp.wait()

    return pl.pallas_call(
        body, name="grad_sibling_exchange",
        in_specs=[HBM] * n, out_specs=[HBM] * n, out_shape=_swap_shapes(gs, axes),
        scratch_shapes=[pltpu.SemaphoreType.DMA((n,)), pltpu.SemaphoreType.DMA((n,))],
    )(*_in_hbm(*gs))


def _add_halves_2d(parts, rbs, core):
    n = len(parts)
    rows, w = rbs[0].shape
    tr = 512
    last_rows = rbs[-1].shape[0]

    def body(core_ref, *refs):
        p_refs, r_refs, h_refs = refs[:n], refs[n:2 * n], refs[2 * n:]
        for t in range(n - 1):
            h_refs[t][...] = (p_refs[t][...] + r_refs[t][...]).astype(BF16)

        @pl.when(pl.program_id(0) == 0)
        def _():
            h_refs[n - 1][...] = (p_refs[n - 1][...] + r_refs[n - 1][...]).astype(BF16)

    mine = lambda r: pl.BlockSpec((r, w), lambda i, cr: (i, cr[0]))
    tile = lambda r: pl.BlockSpec((r, w), lambda i, cr: (i, 0))
    mine_small = pl.BlockSpec((last_rows, w), lambda i, cr: (0, cr[0]))
    tile_small = pl.BlockSpec((last_rows, w), lambda i, cr: (0, 0))
    return pl.pallas_call(
        body, name="grad_add_sibling_w_in",
        grid_spec=pltpu.PrefetchScalarGridSpec(
            num_scalar_prefetch=1, grid=(rows // tr,),
            in_specs=[mine(tr)] * (n - 1) + [mine_small] + [tile(tr)] * (n - 1) + [tile_small],
            out_specs=[tile(tr)] * (n - 1) + [tile_small]),
        out_shape=[jax.ShapeDtypeStruct(r.shape, BF16) for r in rbs],
        compiler_params=_params(("arbitrary",)),
    )(core, *parts, *rbs)


def _add_sibling_half(gs, rbs, axes, small, sr, core):
    n = len(gs)

    def body(core_ref, *refs):
        g_refs, rb_refs = refs[:n], refs[n:2 * n]
        s_ref, sr_ref = refs[2 * n], refs[2 * n + 1]
        h_refs, sc_ref = refs[2 * n + 2:3 * n + 2], refs[3 * n + 2]
        for t in range(n):
            h_refs[t][...] = (g_refs[t][...] + rb_refs[t][...]).astype(BF16)
        sc_ref[...] = s_ref[...] + sr_ref[...]

    def mine(rb, axis):
        blk = (None,) + rb.shape[1:]
        if axis == 2:
            return pl.BlockSpec(blk, lambda j, cr: (j, 0, cr[0]))
        return pl.BlockSpec(blk, lambda j, cr: (j, cr[0], 0))

    chunk = lambda rb: pl.BlockSpec((None,) + rb.shape[1:], lambda j, cr: (j, 0, 0))
    whole = pl.BlockSpec(small.shape, lambda j, cr: (0, 0))
    return pl.pallas_call(
        body, name="grad_add_sibling",
        grid_spec=pltpu.PrefetchScalarGridSpec(
            num_scalar_prefetch=1, grid=(N_CHIPS,),
            in_specs=[mine(rb, a) for rb, a in zip(rbs, axes)] + [chunk(rb) for rb in rbs] + [whole, whole],
            out_specs=[chunk(rb) for rb in rbs] + [whole]),
        out_shape=[jax.ShapeDtypeStruct(rb.shape, BF16) for rb in rbs] + [jax.ShapeDtypeStruct(small.shape, F32)],
        compiler_params=_params(("arbitrary",)),
    )(core, *gs, *rbs, small, sr)


def _chip_scatter_plan(h_refs, s_ref, q_refs, sq_ref, send, recv, local):
    n = len(h_refs)
    x, y, c = lax.axis_index("x"), lax.axis_index("y"), lax.axis_index("c")
    me = 2 * x + y
    chips = [(1 - x, y), (x, 1 - y), (1 - x, 1 - y)]

    def copy(t, k, chip, src_slot, dst_slot):
        src = h_refs[t].at[src_slot] if t < n else s_ref
        dst = (q_refs[t] if t < n else sq_ref).at[dst_slot]
        return pltpu.make_async_remote_copy(src_ref=src, dst_ref=dst, send_sem=send.at[3 * t + k],
                                            recv_sem=recv.at[3 * t + k], device_id=(*chip, c), device_id_type=MESH)

    own = [pltpu.make_async_copy(h_refs[t].at[me], q_refs[t].at[me], local.at[t]) for t in range(n)]
    own.append(pltpu.make_async_copy(s_ref, sq_ref.at[me], local.at[n]))
    sends = [copy(t, k, (px, py), 2 * px + py, me) for t in range(n + 1) for k, (px, py) in enumerate(chips)]
    recvs = [copy(t, k, (px, py), me, 2 * px + py) for t in range(n + 1) for k, (px, py) in enumerate(chips)]
    return own, sends, recvs


def _small_allreduce(buf):
    def body(b_ref, o_ref, sib_buf, chip_buf, send, recv):
        x, y, c = lax.axis_index("x"), lax.axis_index("y"), lax.axis_index("c")
        me = 2 * x + y
        chips = [(1 - x, y), (x, 1 - y), (1 - x, 1 - y)]
        swap = pltpu.make_async_remote_copy(src_ref=b_ref, dst_ref=sib_buf, send_sem=send.at[0], recv_sem=recv.at[0],
                                            device_id=(x, y, 1 - c), device_id_type=MESH)
        swap.start()
        swap.wait()
        chip_buf[me] = b_ref[...] + sib_buf[...]

        def copy(k, chip, slot):
            return pltpu.make_async_remote_copy(src_ref=chip_buf.at[slot], dst_ref=chip_buf.at[slot], send_sem=send.at[1 + k],
                                                recv_sem=recv.at[1 + k], device_id=(*chip, c), device_id_type=MESH)

        sends = [copy(k, chip, me) for k, chip in enumerate(chips)]
        for cp in sends:
            cp.start()
        for k, (px, py) in enumerate(chips):
            copy(k, (px, py), 2 * px + py).wait_recv()
        for cp in sends:
            cp.wait_send()
        o_ref[...] = ((chip_buf[0] + chip_buf[1]) + chip_buf[2]) + chip_buf[3]

    vmem = pl.BlockSpec(memory_space=pltpu.VMEM)
    return pl.pallas_call(
        body, name="small_allreduce", in_specs=[vmem], out_specs=vmem,
        out_shape=jax.ShapeDtypeStruct(buf.shape, F32),
        scratch_shapes=[pltpu.VMEM(buf.shape, F32), pltpu.VMEM((N_CHIPS,) + buf.shape, F32),
                        pltpu.SemaphoreType.DMA((4,)), pltpu.SemaphoreType.DMA((4,))],
        compiler_params=_params(),
    )(buf)


def _reduce_allgather(qs, axes, sq):
    n = len(qs)
    shard_shapes = [tuple(d * 2 if i == a - 1 else d for i, d in enumerate(q.shape[1:])) for q, a in zip(qs, axes)]

    def body(*refs):
        q_refs, sq_ref = refs[:n], refs[n]
        o_refs, st_ref = refs[n + 1:2 * n + 1], refs[2 * n + 1]
        send, recv = refs[2 * n + 2:]
        x, y, c = lax.axis_index("x"), lax.axis_index("y"), lax.axis_index("c")

        def swap(t, which):
            return pltpu.make_async_remote_copy(
                src_ref=_half(o_refs[t], axes[t] - 1, which), dst_ref=_half(o_refs[t], axes[t] - 1, which),
                send_sem=send.at[t], recv_sem=recv.at[t], device_id=(x, y, 1 - c), device_id_type=MESH)

        sent = []
        for t in range(n):
            q = q_refs[t]
            total = ((q[0].astype(F32) + q[1].astype(F32)) + q[2].astype(F32)) + q[3].astype(F32)
            _half(o_refs[t], axes[t] - 1, c)[...] = total
            cp = swap(t, c)
            cp.start()
            sent.append(cp)
        st_ref[...] = ((sq_ref[0] + sq_ref[1]) + sq_ref[2]) + sq_ref[3]
        for t in range(n):
            swap(t, 1 - c).wait_recv()
        for cp in sent:
            cp.wait_send()

    vmem = pl.BlockSpec(memory_space=pltpu.VMEM)
    return pl.pallas_call(
        body, name="grad_reduce_allgather",
        in_specs=[vmem] * (n + 1), out_specs=[vmem] * (n + 1),
        out_shape=[jax.ShapeDtypeStruct(s, F32) for s in shard_shapes] + [jax.ShapeDtypeStruct(sq.shape[1:], F32)],
        scratch_shapes=[pltpu.SemaphoreType.DMA((n,)), pltpu.SemaphoreType.DMA((n,))],
        compiler_params=_params(),
    )(*qs, sq)


def _dot_nt(a, b):
    return lax.dot_general(a, b, (((1,), (1,)), ((), ())), preferred_element_type=F32)


def _dot_tn(a, b):
    return lax.dot_general(a, b, (((0,), (0,)), ((), ())), preferred_element_type=F32)


def _mm_nt(a, bt, out_dtype, name, m, n, tm, tn, row_block=0, scale_first=None, gather=None):
    k = a.shape[1]
    shards, axes = gather if gather is not None else ([], [])
    ng = len(shards)
    grid = (m // tm, n // tn)

    def body(a_ref, b_ref, *rest):
        ins, o_ref, outs, sems = rest[:ng], rest[ng], rest[ng + 1:2 * ng + 1], rest[2 * ng + 1:]
        first = (pl.program_id(0) == 0) & (pl.program_id(1) == 0)
        last = (pl.program_id(0) == grid[0] - 1) & (pl.program_id(1) == grid[1] - 1)
        if ng:
            @pl.when(first)
            def _():
                _gather_start(ins, outs, axes, *sems)

        r = _dot_nt(a_ref[...], b_ref[...])
        if scale_first is not None:
            r = r * jnp.where(pl.program_id(1) == 0, scale_first, 1.0)
        o_ref[...] = r.astype(out_dtype)

        if ng:
            @pl.when(last)
            def _():
                _gather_finish(ins, outs, axes, *sems)

    res = pl.pallas_call(
        body, name=name, grid=grid,
        in_specs=[pl.BlockSpec((tm, k), lambda i, j: (i, 0)), pl.BlockSpec((tn, k), lambda i, j: (row_block + j, 0))]
        + [HBM] * ng,
        out_specs=[pl.BlockSpec((tm, tn), lambda i, j: (i, j))] + [HBM] * ng,
        out_shape=[jax.ShapeDtypeStruct((m, n), out_dtype)] + _gathered_shapes(shards),
        scratch_shapes=_gather_sems(ng) if ng else [],
        compiler_params=_params(("arbitrary", "arbitrary") if ng else ("parallel", "parallel")),
    )(a, bt, *_in_hbm(*shards))
    return (res[0], res[1:]) if ng else res[0]


def _mm_tn(a, b, name, k, tm, tn, tk, chunks=1, m=None, m_off=0, swap=None):
    m = a.shape[1] if m is None else m
    n = b.shape[1]
    cw = n // chunks
    srcs, axes = swap if swap is not None else ([], [])
    ns = len(srcs)
    grid = (m // tm, n // tn, k // tk)

    def body(a_ref, b_ref, *rest):
        s_refs, o_ref, d_refs, sems = rest[:ns], rest[ns], rest[ns + 1:2 * ns + 1], rest[2 * ns + 1:]
        ids = [pl.program_id(d) for d in range(3)]
        if ns:
            @pl.when((ids[0] == 0) & (ids[1] == 0) & (ids[2] == 0))
            def _():
                for cp in _swap_copies(s_refs, d_refs, axes, *sems):
                    cp.start()

        @pl.when(ids[2] == 0)
        def _():
            o_ref[...] = jnp.zeros_like(o_ref)
        r = _dot_tn(a_ref[...].astype(BF16), b_ref[...].astype(BF16))
        if chunks > 1:
            for c in range(chunks):
                o_ref[c] += r[:, c * cw:(c + 1) * cw]
        else:
            o_ref[...] += r

        if ns:
            @pl.when((ids[0] == grid[0] - 1) & (ids[1] == grid[1] - 1) & (ids[2] == grid[2] - 1))
            def _():
                for cp in _swap_copies(s_refs, d_refs, axes, *sems):
                    cp.wait()

    if chunks > 1:
        assert tn == n
        out_spec = pl.BlockSpec((chunks, tm, cw), lambda i, j, kk: (0, i, 0))
        out_shape = jax.ShapeDtypeStruct((chunks, m, cw), F32)
    else:
        out_spec = pl.BlockSpec((tm, tn), lambda i, j, kk: (i, j))
        out_shape = jax.ShapeDtypeStruct((m, n), F32)
    res = pl.pallas_call(
        body, name=name, grid=grid,
        in_specs=[pl.BlockSpec((tk, tm), lambda i, j, kk: (kk, m_off + i)), pl.BlockSpec((tk, tn), lambda i, j, kk: (kk, j))]
        + [HBM] * ns,
        out_specs=[out_spec] + [HBM] * ns, out_shape=[out_shape] + _swap_shapes(srcs, axes),
        scratch_shapes=[pltpu.SemaphoreType.DMA((ns,)), pltpu.SemaphoreType.DMA((ns,))] if ns else [],
        compiler_params=_params(("arbitrary",) * 3 if ns else ("parallel", "parallel", "arbitrary")),
    )(a, b, *_in_hbm(*srcs))
    return (res[0], res[1:]) if ns else res[0]


def _tokens_spec():
    return pl.BlockSpec((ROW_TILE, D_MODEL), lambda i: (jnp.maximum(i - 1, 0), 0))


def _rmsnorm_fwd(x2, g1, lk, shards, axes):
    nb = x2.shape[0] // ROW_TILE + 1
    nblk = lk // ROW_TILE
    ng = len(shards)
    meta_cols = shards[-1].shape[1]

    def body(x_ref, g_ref, *rest):
        ins, hn_ref, outs = rest[:ng], rest[ng], rest[ng + 1:2 * ng + 1]
        sems, meta_buf, meta_sem = rest[2 * ng + 1:2 * ng + 6], rest[2 * ng + 6], rest[2 * ng + 7]
        s = pl.program_id(0)
        blk = (s + 1) % nblk

        @pl.when(s == 0)
        def _():
            _gather_start(ins, outs, axes, *sems)

        def normed(h):
            r = lax.rsqrt(jnp.mean(h * h, axis=-1, keepdims=True) + RMS_EPS)
            return ((h * r) * g_ref[...]).astype(BF16)

        @pl.when(s < nblk - 1)
        def _():
            hn_ref[...] = normed(jnp.where(blk >= nb, 0.0, x_ref[...]))

        @pl.when(s == nblk - 1)
        def _():
            _gather_finish(ins, outs, axes, *sems)
            fetch = pltpu.make_async_copy(outs[-1], meta_buf, meta_sem.at[0])
            fetch.start()
            fetch.wait()
            meta = jnp.concatenate([meta_buf[j] for j in range(N_CHIPS)], axis=1)
            hn_ref[...] = normed(jnp.concatenate([jnp.zeros((PAD_ROWS, D_MODEL), F32), meta], axis=0))

    res = pl.pallas_call(
        body, name="rmsnorm_fwd", grid=(nblk,),
        in_specs=[pl.BlockSpec((ROW_TILE, D_MODEL), lambda s: (jnp.clip((s + 1) % nblk - 1, 0, nb - 2), 0)),
                  pl.BlockSpec((1, D_MODEL), lambda s: (0, 0))] + [HBM] * ng,
        out_specs=[pl.BlockSpec((ROW_TILE, D_MODEL), lambda s: ((s + 1) % nblk, 0))] + [HBM] * ng,
        out_shape=[jax.ShapeDtypeStruct((lk, D_MODEL), BF16)] + _gathered_shapes(shards),
        scratch_shapes=_gather_sems(ng) + [pltpu.VMEM((N_CHIPS, N_META, meta_cols), F32), pltpu.SemaphoreType.DMA((1,))],
        compiler_params=_params(("arbitrary",)),
    )(x2, g1, *_in_hbm(*shards))
    return res[0], res[1:]


def _valid_gate_mask(i):
    row = i * ROW_TILE + lax.broadcasted_iota(jnp.int32, (ROW_TILE, F_COLS), 0)
    col = lax.broadcasted_iota(jnp.int32, (ROW_TILE, F_COLS), 1)
    return (row >= PAD_ROWS) & (col < N_HEADS)


def _gates_fwd(f, bfg):
    nb = f.shape[0] // ROW_TILE

    def body(f_ref, b_ref, c_ref, carry):
        i = pl.program_id(0)

        @pl.when(i == 0)
        def _():
            carry[...] = jnp.zeros_like(carry)

        logit = f_ref[...] + b_ref[...]
        lf = jnp.minimum(logit, 0.0) - jnp.log1p(jnp.exp(-jnp.abs(logit)))
        lf = jnp.where(_valid_gate_mask(i), lf, 0.0)
        r_i = lax.broadcasted_iota(jnp.int32, (ROW_TILE, ROW_TILE), 0)
        c_i = lax.broadcasted_iota(jnp.int32, (ROW_TILE, ROW_TILE), 1)
        tri = (c_i <= r_i).astype(F32)
        c_ref[...] = jnp.dot(tri, lf, precision=HIGHEST, preferred_element_type=F32) + carry[...]
        carry[...] = carry[...] + jnp.sum(lf, axis=0, keepdims=True)

    return pl.pallas_call(
        body, name="gates_fwd", grid=(nb,),
        in_specs=[pl.BlockSpec((ROW_TILE, F_COLS), lambda i: (i, 0)), pl.BlockSpec((1, F_COLS), lambda i: (0, 0))],
        out_specs=pl.BlockSpec((ROW_TILE, F_COLS), lambda i: (i, 0)),
        out_shape=jax.ShapeDtypeStruct(f.shape, F32),
        scratch_shapes=[pltpu.VMEM((1, F_COLS), F32)],
        compiler_params=_params(("arbitrary",)),
    )(f, bfg)


def _pool_counts(i):
    row = i * ROW_TILE + lax.broadcasted_iota(jnp.int32, (ROW_TILE, 1), 0)
    return jnp.maximum(row - PAD_ROWS, 0)


def _trailing_sums(xc, levels):
    acc = xc
    for lv in range(levels):
        acc = acc + pltpu.roll(acc, 1 << lv, 0)
    return acc


def _leading_sums(xc, levels):
    n = xc.shape[0]
    acc = xc
    for lv in range(levels):
        acc = acc + pltpu.roll(acc, n - (1 << lv), 0)
    return acc


def _pool_p(u_cur, u_prev, i):
    pos = _pool_counts(i)
    ps, invs = [], []
    for g, w in enumerate(POOL_WINDOWS):
        sl = slice(g * POOL_GROUP, (g + 1) * POOL_GROUP)
        cur = u_cur[:, sl]
        xc = jnp.concatenate([u_prev[:, sl], cur], axis=0)
        win = _trailing_sums(xc, g + 1)[ROW_TILE:, :]
        inv = 1.0 / jnp.minimum(pos + 1, w).astype(F32)
        ps.append(win * inv - cur)
        invs.append(inv)
    return ps, invs


def _pool_fwd(e, pw, scale):
    nb = e.shape[0] // ROW_TILE

    def body(uc_ref, up_ref, z_ref, pw_ref, sc_ref, y_ref):
        i = pl.program_id(0)
        u_cur = uc_ref[...]
        u_prev = jnp.where(i == 0, 0.0, up_ref[...])
        ps, _ = _pool_p(u_cur, u_prev, i)
        z = z_ref[...]
        gate = z * _sigmoid(z)
        for g in range(len(POOL_WINDOWS)):
            sl = slice(g * POOL_GROUP, (g + 1) * POOL_GROUP)
            yraw = jnp.dot(ps[g].astype(BF16), pw_ref[g], preferred_element_type=F32)
            y_ref[:, sl] = ((yraw * sc_ref[:, sl]) * gate[:, sl]).astype(BF16)

    blk = (ROW_TILE, POOL_WIDTH)
    return pl.pallas_call(
        body, name="pool_fwd", grid=(nb,),
        in_specs=[pl.BlockSpec(blk, lambda i: (i, 0)), pl.BlockSpec(blk, lambda i: (jnp.maximum(i - 1, 0), 0)),
                  pl.BlockSpec(blk, lambda i: (i, 1)),
                  pl.BlockSpec((len(POOL_WINDOWS), POOL_GROUP, POOL_GROUP), lambda i: (0, 0, 0)),
                  pl.BlockSpec((1, POOL_WIDTH), lambda i: (0, 0))],
        out_specs=pl.BlockSpec(blk, lambda i: (i, 0)),
        out_shape=jax.ShapeDtypeStruct((e.shape[0], POOL_WIDTH), BF16),
        compiler_params=_params(("parallel",)),
    )(e, e, e, pw, scale)


def _stack_heads(a):
    first = lax.broadcasted_iota(jnp.int32, a.shape, 1) < HEAD_DIM
    zero = jnp.zeros_like(a)
    return jnp.concatenate([jnp.where(first, a, zero), jnp.where(first, zero, a)], axis=0)


def _unstack_heads(a):
    rows = a.shape[0] // 2
    first = lax.broadcasted_iota(jnp.int32, (rows, LANES), 1) < HEAD_DIM
    return jnp.where(first, a[:rows], a[rows:])


def _causal(i, jj, stacked, kv_tile):
    r = lax.broadcasted_iota(jnp.int32, (stacked * ROW_TILE, kv_tile), 0)
    if stacked == 2:
        r = jnp.where(r >= ROW_TILE, r - ROW_TILE, r)
    kidx = jj * kv_tile + lax.broadcasted_iota(jnp.int32, (stacked * ROW_TILE, kv_tile), 1)
    return kidx <= i * ROW_TILE + r


def _stack_rows(b):
    n = b.shape[1]
    return jnp.concatenate([jnp.broadcast_to(b[0:1], (ROW_TILE, n)), jnp.broadcast_to(b[1:2], (ROW_TILE, n))], axis=0)


def _attn_fwd(qkv, bias, nb):
    lk = qkv.shape[0]
    lp = nb * ROW_TILE
    nkb = lk // KV_TILE
    n_pairs = N_HEADS // 2

    def body(q_ref, k_ref, v_ref, b_ref, o_ref, lse_ref):
        i = pl.program_id(1)
        qs = _stack_heads(q_ref[...])
        last = (i * ROW_TILE) // KV_TILE

        def block(jj, carry, masked):
            m, l, acc = carry
            rows = pl.ds(pl.multiple_of(jj * KV_TILE, KV_TILE), KV_TILE)
            s = _dot_nt(qs, k_ref[rows, :]) - _stack_rows(b_ref[0, 0, jj])
            if masked:
                s = jnp.where(_causal(i, jj, 2, KV_TILE), s, NEG)
            m_new = jnp.maximum(m, jnp.max(s, axis=1, keepdims=True))
            alpha = jnp.exp(m - m_new)
            p = jnp.exp(s - m_new)
            l = alpha * l + jnp.sum(p, axis=1, keepdims=True)
            acc = alpha * acc + jnp.dot(p.astype(BF16), v_ref[rows, :], preferred_element_type=F32)
            return m_new, l, acc

        init = (jnp.full((2 * ROW_TILE, 1), NEG, F32), jnp.zeros((2 * ROW_TILE, 1), F32),
                jnp.zeros((2 * ROW_TILE, LANES), F32))
        carry = lax.fori_loop(0, last, lambda jj, c: block(jj, c, False), init)
        m, l, acc = block(last, carry, True)
        o_ref[...] = _unstack_heads(acc / l)
        lse_ref[...] = _unstack_heads(jnp.broadcast_to(m + jnp.log(l), (2 * ROW_TILE, LANES)))

    return pl.pallas_call(
        body, name="attn_fwd", grid=(n_pairs, nb),
        in_specs=[pl.BlockSpec((ROW_TILE, LANES), lambda hp, i: (i, hp)),
                  pl.BlockSpec((lk, LANES), lambda hp, i: (0, n_pairs + hp)),
                  pl.BlockSpec((lk, LANES), lambda hp, i: (0, 2 * n_pairs + hp)),
                  pl.BlockSpec((1, 1, nkb, 2, KV_TILE), lambda hp, i: (i, hp, 0, 0, 0))],
        out_specs=[pl.BlockSpec((ROW_TILE, LANES), lambda hp, i: (i, hp)),
                   pl.BlockSpec((ROW_TILE, LANES), lambda hp, i: (i, hp))],
        out_shape=[jax.ShapeDtypeStruct((lp, ATTN_WIDTH), F32), jax.ShapeDtypeStruct((lp, ATTN_WIDTH), F32)],
        compiler_params=_params(("parallel", "parallel")),
    )(qkv, qkv, qkv, bias)


def _merge_fwd(y_pool, o, e, wup_p, wup_a):
    lp = o.shape[0]
    nb = lp // ROW_TILE

    def body(yp_ref, o_ref, e_ref, wp_ref, wa_ref, mg_ref, ya_ref):
        za = e_ref[:, E_ZA:E_GP]
        ya = (o_ref[...] * (za * _sigmoid(za))).astype(BF16)
        ya_ref[...] = ya
        a_pool = jnp.dot(yp_ref[...], wp_ref[...], preferred_element_type=F32)
        a_attn = jnp.dot(ya, wa_ref[...], preferred_element_type=F32)
        mg_ref[...] = (_sigmoid(e_ref[:, E_GP:E_GA]) * a_pool + _sigmoid(e_ref[:, E_GA:E_COLS]) * a_attn).astype(BF16)

    return pl.pallas_call(
        body, name="merge_fwd", grid=(nb,),
        in_specs=[pl.BlockSpec((ROW_TILE, POOL_WIDTH), lambda i: (i, 0)),
                  pl.BlockSpec((ROW_TILE, ATTN_WIDTH), lambda i: (i, 0)),
                  pl.BlockSpec((ROW_TILE, E_COLS), lambda i: (i, 0)),
                  pl.BlockSpec((POOL_WIDTH, D_MODEL), lambda i: (0, 0)),
                  pl.BlockSpec((ATTN_WIDTH, D_MODEL), lambda i: (0, 0))],
        out_specs=[pl.BlockSpec((ROW_TILE, D_MODEL), lambda i: (i, 0)),
                   pl.BlockSpec((ROW_TILE, ATTN_WIDTH), lambda i: (i, 0))],
        out_shape=[jax.ShapeDtypeStruct((lp, D_MODEL), BF16), jax.ShapeDtypeStruct((lp, ATTN_WIDTH), BF16)],
        compiler_params=_params(("parallel",)),
    )(y_pool, o, e, wup_p, wup_a)


def _head_fwd_bwd(merged, w_out, x2, metapad, gf, target):
    lp = merged.shape[0]
    nb = lp // ROW_TILE

    def body(mg_ref, w_ref, x_ref, mp_ref, g_ref, t_ref, dh_ref, loss_ref, dg_ref):
        i = pl.program_id(0)

        @pl.when(i == 0)
        def _():
            loss_ref[...] = jnp.zeros_like(loss_ref)
            dg_ref[...] = jnp.zeros_like(dg_ref)

        h0 = jnp.where(i == 0, mp_ref[...], x_ref[...])
        h1 = h0 + jnp.dot(mg_ref[...], w_ref[...], preferred_element_type=F32)
        r = lax.rsqrt(jnp.mean(h1 * h1, axis=-1, keepdims=True) + RMS_EPS)
        xhat = h1 * r
        g = g_ref[...]
        err = jnp.where(i == 0, 0.0, xhat * g - t_ref[...])
        loss_ref[...] += 0.5 * jnp.sum(jnp.mean(err * err, axis=-1, keepdims=True))
        dy = err / D_MODEL
        dg_ref[...] += jnp.sum(dy * xhat, axis=0, keepdims=True)
        dxhat = dy * g
        dh_ref[...] = r * (dxhat - xhat * jnp.mean(dxhat * xhat, axis=-1, keepdims=True))

    return pl.pallas_call(
        body, name="head_fwd_bwd", grid=(nb,),
        in_specs=[pl.BlockSpec((ROW_TILE, D_MODEL), lambda i: (i, 0)),
                  pl.BlockSpec((D_MODEL, D_MODEL), lambda i: (0, 0)),
                  _tokens_spec(), pl.BlockSpec((ROW_TILE, D_MODEL), lambda i: (0, 0)),
                  pl.BlockSpec((1, D_MODEL), lambda i: (0, 0)), _tokens_spec()],
        out_specs=[pl.BlockSpec((ROW_TILE, D_MODEL), lambda i: (i, 0)),
                   pl.BlockSpec((1, LANES), lambda i: (0, 0)), pl.BlockSpec((1, D_MODEL), lambda i: (0, 0))],
        out_shape=[jax.ShapeDtypeStruct((lp, D_MODEL), F32), jax.ShapeDtypeStruct((1, LANES), F32),
                   jax.ShapeDtypeStruct((1, D_MODEL), F32)],
        compiler_params=_params(("arbitrary",)),
    )(merged, w_out, x2, metapad, gf, target)


def _per_head_rowsum(t):
    head = lax.broadcasted_iota(jnp.int32, t.shape, 1) // HEAD_DIM
    out = jnp.zeros_like(t)
    for h in range(N_HEADS):
        sel = head == h
        out = jnp.where(sel, jnp.sum(jnp.where(sel, t, 0.0), axis=1, keepdims=True), out)
    return out


def _merge_bwd(dh1, w_out, y_pool, y_attn, wup_p, wup_a, e, o):
    lp = o.shape[0]
    nb = lp // ROW_TILE

    def body(dh_ref, wo_ref, yp_ref, ya_ref, wp_ref, wa_ref, e_ref, o_ref,
             dap_ref, daa_ref, dg_ref, dza_ref, do_ref, delta_ref, dyp_ref):
        dmerged = _dot_nt(dh_ref[...].astype(BF16), wo_ref[...])
        a_pool = jnp.dot(yp_ref[...], wp_ref[...], preferred_element_type=F32)
        a_attn = jnp.dot(ya_ref[...], wa_ref[...], preferred_element_type=F32)
        sp = _sigmoid(e_ref[:, E_GP:E_GA])
        sa = _sigmoid(e_ref[:, E_GA:E_COLS])
        dap = (dmerged * sp).astype(BF16)
        daa = (dmerged * sa).astype(BF16)
        dap_ref[...] = dap
        daa_ref[...] = daa
        dg_ref[:, :D_MODEL] = (dmerged * a_pool * (sp * (1.0 - sp))).astype(BF16)
        dg_ref[:, D_MODEL:] = (dmerged * a_attn * (sa * (1.0 - sa))).astype(BF16)
        dyp_ref[...] = _dot_nt(dap, wp_ref[...])
        dya = _dot_nt(daa, wa_ref[...])
        za = e_ref[:, E_ZA:E_GP]
        sz = _sigmoid(za)
        o = o_ref[...]
        do = dya * (za * sz)
        do_ref[...] = do.astype(BF16)
        dza_ref[...] = (dya * o * (sz * (1.0 + za * (1.0 - sz)))).astype(BF16)
        delta_ref[...] = _per_head_rowsum(do * o)

    row = lambda w: pl.BlockSpec((ROW_TILE, w), lambda i: (i, 0))
    full = lambda a: pl.BlockSpec(a.shape, lambda i: (0, 0))
    return pl.pallas_call(
        body, name="merge_bwd", grid=(nb,),
        in_specs=[row(D_MODEL), full(w_out), row(POOL_WIDTH), row(ATTN_WIDTH), full(wup_p), full(wup_a),
                  row(E_COLS), row(ATTN_WIDTH)],
        out_specs=[row(D_MODEL), row(D_MODEL), row(2 * D_MODEL), row(ATTN_WIDTH), row(ATTN_WIDTH),
                   row(ATTN_WIDTH), row(POOL_WIDTH)],
        out_shape=[jax.ShapeDtypeStruct((lp, D_MODEL), BF16), jax.ShapeDtypeStruct((lp, D_MODEL), BF16),
                   jax.ShapeDtypeStruct((lp, 2 * D_MODEL), BF16), jax.ShapeDtypeStruct((lp, ATTN_WIDTH), BF16),
                   jax.ShapeDtypeStruct((lp, ATTN_WIDTH), BF16), jax.ShapeDtypeStruct((lp, ATTN_WIDTH), F32),
                   jax.ShapeDtypeStruct((lp, POOL_WIDTH), F32)],
        compiler_params=_params(("parallel",)),
    )(dh1, w_out, y_pool, y_attn, wup_p, wup_a, e, o)


def _pool_bwd_local(e, dy_pool, pw, scale):
    lp = e.shape[0]
    nb = lp // ROW_TILE
    ng = len(POOL_WINDOWS)

    def body(uc_ref, up_ref, z_ref, dy_ref, pw_ref, sc_ref, dpc_ref, dz_ref, dsc_ref, dpw_ref):
        i = pl.program_id(0)

        @pl.when(i == 0)
        def _():
            dsc_ref[...] = jnp.zeros_like(dsc_ref)
            dpw_ref[...] = jnp.zeros_like(dpw_ref)

        u_cur = uc_ref[...]
        u_prev = jnp.where(i == 0, 0.0, up_ref[...])
        ps, invs = _pool_p(u_cur, u_prev, i)
        z = z_ref[...]
        sz = _sigmoid(z)
        dy = dy_ref[...]
        dypre = dy * (z * sz)
        dsilu = sz * (1.0 + z * (1.0 - sz))
        for g in range(ng):
            sl = slice(g * POOL_GROUP, (g + 1) * POOL_GROUP)
            pb = ps[g].astype(BF16)
            w = pw_ref[g]
            yraw = jnp.dot(pb, w, preferred_element_type=F32)
            sc = sc_ref[:, sl]
            dz_ref[:, sl] = (dy[:, sl] * (yraw * sc) * dsilu[:, sl]).astype(BF16)
            dsc_ref[:, sl] += jnp.sum(dypre[:, sl] * yraw, axis=0, keepdims=True)
            dyraw = (dypre[:, sl] * sc).astype(BF16)
            dpw_ref[g] += _dot_tn(pb, dyraw)
            dpc_ref[:, sl] = _dot_nt(dyraw, w) * invs[g]

    blk = (ROW_TILE, POOL_WIDTH)
    return pl.pallas_call(
        body, name="pool_bwd_local", grid=(nb,),
        in_specs=[pl.BlockSpec(blk, lambda i: (i, 0)), pl.BlockSpec(blk, lambda i: (jnp.maximum(i - 1, 0), 0)),
                  pl.BlockSpec(blk, lambda i: (i, 1)), pl.BlockSpec(blk, lambda i: (i, 0)),
                  pl.BlockSpec((ng, POOL_GROUP, POOL_GROUP), lambda i: (0, 0, 0)),
                  pl.BlockSpec((1, POOL_WIDTH), lambda i: (0, 0))],
        out_specs=[pl.BlockSpec(blk, lambda i: (i, 0)), pl.BlockSpec(blk, lambda i: (i, 0)),
                   pl.BlockSpec((1, POOL_WIDTH), lambda i: (0, 0)),
                   pl.BlockSpec((ng, POOL_GROUP, POOL_GROUP), lambda i: (0, 0, 0))],
        out_shape=[jax.ShapeDtypeStruct((lp, POOL_WIDTH), F32), jax.ShapeDtypeStruct((lp, POOL_WIDTH), BF16),
                   jax.ShapeDtypeStruct((1, POOL_WIDTH), F32),
                   jax.ShapeDtypeStruct((ng, POOL_GROUP, POOL_GROUP), F32)],
        compiler_params=_params(("arbitrary",)),
    )(e, e, e, dy_pool, pw, scale)


def _pool_bwd_window(dpc):
    lp = dpc.shape[0]
    nb = lp // ROW_TILE

    def body(cur_ref, nxt_ref, du_ref):
        i = pl.program_id(0)
        cur = cur_ref[...]
        nxt = jnp.where(i == nb - 1, 0.0, nxt_ref[...])
        pos = _pool_counts(i)
        for g, w in enumerate(POOL_WINDOWS):
            sl = slice(g * POOL_GROUP, (g + 1) * POOL_GROUP)
            xc = jnp.concatenate([cur[:, sl], nxt[:, sl]], axis=0)
            win = _leading_sums(xc, g + 1)[:ROW_TILE, :]
            dp = cur[:, sl] * jnp.minimum(pos + 1, w).astype(F32)
            du_ref[:, sl] = (win - dp).astype(BF16)

    blk = (ROW_TILE, POOL_WIDTH)
    return pl.pallas_call(
        body, name="pool_bwd_window", grid=(nb,),
        in_specs=[pl.BlockSpec(blk, lambda i: (i, 0)), pl.BlockSpec(blk, lambda i: (jnp.minimum(i + 1, nb - 1), 0))],
        out_specs=pl.BlockSpec(blk, lambda i: (i, 0)),
        out_shape=jax.ShapeDtypeStruct((lp, POOL_WIDTH), BF16),
        compiler_params=_params(("parallel",)),
    )(dpc, dpc)


def _attn_bwd(qkv, do, lse, delta, bias, nb):
    lk = qkv.shape[0]
    lp = nb * ROW_TILE
    nkb = lk // KV_TILE_BWD
    n_pairs = N_HEADS // 2
    per_kv = KV_TILE_BWD // ROW_TILE

    def body(q_ref, k_ref, v_ref, do_ref, lse_ref, dl_ref, b_ref, dq_ref, dk_ref, dv_ref, dc_ref, dcq_ref):
        jj = pl.program_id(1)

        @pl.when(jj == 0)
        def _():
            dq_ref[...] = jnp.zeros_like(dq_ref)
            dcq_ref[...] = jnp.zeros_like(dcq_ref)

        kb, vb = k_ref[...], v_ref[...]

        def block(i, carry, masked):
            dk_acc, dv_acc, dc_acc = carry
            rows = pl.ds(pl.multiple_of(i * ROW_TILE, ROW_TILE), ROW_TILE)
            qs = _stack_heads(q_ref[rows, :])
            dos = _stack_heads(do_ref[rows, :])
            lse_i, dl_i = lse_ref[rows, :], dl_ref[rows, :]
            s = _dot_nt(qs, kb)
            dp = _dot_nt(dos, vb)
            if masked:
                valid = _causal(i, jj, 1, KV_TILE_BWD)
            ps, dss, dcs, rowsums = [], [], [], []
            for hd in range(2):
                half = slice(hd * ROW_TILE, (hd + 1) * ROW_TILE)
                col = slice(hd * HEAD_DIM, hd * HEAD_DIM + 1)
                sh = s[half] - b_ref[i, 0, 0, hd:hd + 1, :]
                if masked:
                    sh = jnp.where(valid, sh, NEG)
                p = jnp.exp(sh - lse_i[:, col])
                ds = p * (dp[half] - dl_i[:, col])
                ps.append(p.astype(BF16))
                dss.append(ds.astype(BF16))
                dcs.append(jnp.sum(ds, axis=0, keepdims=True))
                rowsums.append(jnp.sum(ds, axis=1, keepdims=True))
            dsb = jnp.concatenate(dss, axis=0)
            dv_acc = dv_acc + _dot_tn(jnp.concatenate(ps, axis=0), dos)
            dk_acc = dk_acc + _dot_tn(dsb, qs)
            dc_acc = dc_acc - jnp.concatenate(dcs, axis=0)
            dq_ref[rows, :] += _unstack_heads(jnp.dot(dsb, kb, preferred_element_type=F32))
            dcq_ref[rows, :] += _unstack_heads(jnp.broadcast_to(jnp.concatenate(rowsums, axis=0), (2 * ROW_TILE, LANES)))
            return dk_acc, dv_acc, dc_acc

        init = (jnp.zeros((KV_TILE_BWD, LANES), F32), jnp.zeros((KV_TILE_BWD, LANES), F32),
                jnp.zeros((2, KV_TILE_BWD), F32))
        first_q = per_kv * jj
        diag_end = jnp.minimum(first_q + per_kv, nb)
        carry = lax.fori_loop(first_q, diag_end, lambda i, c: block(i, c, True), init)
        dk, dv, dc = lax.fori_loop(diag_end, nb, lambda i, c: block(i, c, False), carry)
        dk_ref[...] = dk.astype(BF16)
        dv_ref[...] = dv.astype(BF16)
        dc_ref[0, 0] = dc

    whole = lambda rows: pl.BlockSpec((rows, LANES), lambda hp, jj: (0, hp))
    kv_blk = lambda off: pl.BlockSpec((KV_TILE_BWD, LANES), lambda hp, jj: (jj, off + hp))
    return pl.pallas_call(
        body, name="attn_bwd", grid=(n_pairs, nkb),
        in_specs=[whole(lk), kv_blk(n_pairs), kv_blk(2 * n_pairs), whole(lp), whole(lp), whole(lp),
                  pl.BlockSpec((nb, 1, 1, 2, KV_TILE_BWD), lambda hp, jj: (0, hp, jj, 0, 0))],
        out_specs=[whole(lp), kv_blk(0), kv_blk(0),
                   pl.BlockSpec((1, 1, 2, KV_TILE_BWD), lambda hp, jj: (hp, jj, 0, 0)), whole(lp)],
        out_shape=[jax.ShapeDtypeStruct((lp, ATTN_WIDTH), F32), jax.ShapeDtypeStruct((lk, ATTN_WIDTH), BF16),
                   jax.ShapeDtypeStruct((lk, ATTN_WIDTH), BF16),
                   jax.ShapeDtypeStruct((n_pairs, nkb, 2, KV_TILE_BWD), F32),
                   jax.ShapeDtypeStruct((lp, ATTN_WIDTH), F32)],
        compiler_params=_params(("parallel", "arbitrary")),
    )(qkv, qkv, qkv, do, lse, delta, bias)


def _gates_bwd(dc, dcq, f, bfg):
    nb = f.shape[0] // ROW_TILE

    def body(dc_ref, dcq_ref, f_ref, b_ref, df_ref, db_ref, carry):
        step = pl.program_id(0)
        i = nb - 1 - step

        @pl.when(step == 0)
        def _():
            carry[...] = jnp.zeros_like(carry)
            db_ref[...] = jnp.zeros_like(db_ref)

        dcb = dc_ref[...]
        lane = lax.broadcasted_iota(jnp.int32, (ROW_TILE, F_COLS), 1)
        for h in range(N_HEADS):
            dcb = dcb + jnp.where(lane == h, dcq_ref[:, HEAD_DIM * h:HEAD_DIM * h + 1], 0.0)
        r_i = lax.broadcasted_iota(jnp.int32, (ROW_TILE, ROW_TILE), 0)
        c_i = lax.broadcasted_iota(jnp.int32, (ROW_TILE, ROW_TILE), 1)
        upper = (c_i >= r_i).astype(F32)
        dlf = jnp.dot(upper, dcb, precision=HIGHEST, preferred_element_type=F32) + carry[...]
        carry[...] = carry[...] + jnp.sum(dcb, axis=0, keepdims=True)
        logit = f_ref[...] + b_ref[...]
        dlogit = jnp.where(_valid_gate_mask(i), dlf * _sigmoid(-logit), 0.0)
        df_ref[...] = dlogit.astype(BF16)
        db_ref[...] += jnp.sum(dlogit, axis=0, keepdims=True)

    blk = pl.BlockSpec((ROW_TILE, F_COLS), lambda s: (nb - 1 - s, 0))
    wide = pl.BlockSpec((ROW_TILE, ATTN_WIDTH), lambda s: (nb - 1 - s, 0))
    one = pl.BlockSpec((1, F_COLS), lambda s: (0, 0))
    return pl.pallas_call(
        body, name="gates_bwd", grid=(nb,),
        in_specs=[blk, wide, blk, one], out_specs=[blk, one],
        out_shape=[jax.ShapeDtypeStruct(f.shape, BF16), jax.ShapeDtypeStruct((1, F_COLS), F32)],
        scratch_shapes=[pltpu.VMEM((1, F_COLS), F32)],
        compiler_params=_params(("arbitrary",)),
    )(dc, dcq, f, bfg)


def _input_bwd(dproj, df, wt_e, wt, wt_f, x2, metapad, dh1, g1, hs, sc):
    nb = x2.shape[0] // ROW_TILE + 1
    qkv_rows = QKV_ROW0 + QKV_COLS
    n = len(hs)

    def body(dp_ref, df_ref, we_ref, w_ref, wf_ref, x_ref, mp_ref, dh_ref, g_ref, *rest):
        h_refs, s_ref = rest[:n], rest[n]
        gx_ref, g0_ref, dg_ref = rest[n + 1:n + 4]
        q_refs, sq_ref = rest[n + 4:2 * n + 4], rest[2 * n + 4]
        sems = rest[2 * n + 5:]
        i = pl.program_id(0)

        @pl.when(i == 0)
        def _():
            dg_ref[...] = jnp.zeros_like(dg_ref)
            own, sends, _ = _chip_scatter_plan(h_refs, s_ref, q_refs, sq_ref, *sems)
            for cp in own + sends:
                cp.start()

        dhn = (jnp.dot(dp_ref[:, :E_COLS], we_ref[...], preferred_element_type=F32)
               + jnp.dot(dp_ref[:, E_COLS:], w_ref[QKV_ROW0:, :], preferred_element_type=F32)
               + jnp.dot(df_ref[...], wf_ref[...], preferred_element_type=F32))
        h0 = jnp.where(i == 0, mp_ref[...], x_ref[...])
        r = lax.rsqrt(jnp.mean(h0 * h0, axis=-1, keepdims=True) + RMS_EPS)
        xhat = h0 * r
        dg_ref[...] += jnp.sum(dhn * xhat, axis=0, keepdims=True)
        dxhat = dhn * g_ref[...]
        dh0 = dh_ref[...] + r * (dxhat - xhat * jnp.mean(dxhat * xhat, axis=-1, keepdims=True))
        gx_ref[...] = dh0

        @pl.when(i == 0)
        def _():
            g0_ref[...] = dh0

        @pl.when(i == nb - 1)
        def _():
            own, sends, recvs = _chip_scatter_plan(h_refs, s_ref, q_refs, sq_ref, *sems)
            for cp in recvs:
                cp.wait_recv()
            for cp in sends:
                cp.wait_send()
            for cp in own:
                cp.wait()

    const = lambda shape: pl.BlockSpec(shape, lambda i: (0, 0))
    res = pl.pallas_call(
        body, name="input_bwd", grid=(nb,),
        in_specs=[pl.BlockSpec((ROW_TILE, MAIN_COLS), lambda i: (i, 0)), pl.BlockSpec((ROW_TILE, F_COLS), lambda i: (i, 0)),
                  const((E_COLS, D_MODEL)), const((qkv_rows, D_MODEL)), const((F_COLS, D_MODEL)),
                  _tokens_spec(), const((ROW_TILE, D_MODEL)),
                  pl.BlockSpec((ROW_TILE, D_MODEL), lambda i: (i, 0)), const((1, D_MODEL))] + [HBM] * (n + 1),
        out_specs=[_tokens_spec(), const((ROW_TILE, D_MODEL)), const((1, D_MODEL))] + [HBM] * (n + 1),
        out_shape=[jax.ShapeDtypeStruct(x2.shape, F32), jax.ShapeDtypeStruct((ROW_TILE, D_MODEL), F32),
                   jax.ShapeDtypeStruct((1, D_MODEL), F32)]
        + [jax.ShapeDtypeStruct(h.shape, h.dtype) for h in hs] + [jax.ShapeDtypeStruct((N_CHIPS,) + sc.shape, sc.dtype)],
        scratch_shapes=[pltpu.SemaphoreType.DMA((3 * (n + 1),)), pltpu.SemaphoreType.DMA((3 * (n + 1),)),
                        pltpu.SemaphoreType.DMA((n + 1,))],
        compiler_params=_params(("arbitrary",), vmem=56 * 1024 * 1024),
    )(dproj, df, wt_e, wt, wt_f, x2, metapad, dh1, g1, *_in_hbm(*hs, sc))
    return res[:3], res[3:3 + n], res[3 + n]


def _adamw(w, g, m, v, name):
    rows, cols = w.shape
    if rows % 8 == 0:
        tr, tc = _row_tile8(rows), cols
    else:
        tr, tc = rows, (2 * LANES if cols % (2 * LANES) == 0 and rows > 8 else cols)

    def body(w_ref, g_ref, m_ref, v_ref, d_ref, mo_ref, vo_ref):
        g_ = g_ref[...]
        m_new = ADAM_B1 * m_ref[...] + (1.0 - ADAM_B1) * g_
        v_new = ADAM_B2 * v_ref[...] + (1.0 - ADAM_B2) * (g_ * g_)
        m_hat = m_new / (1.0 - ADAM_B1 ** ADAM_STEP)
        v_hat = v_new / (1.0 - ADAM_B2 ** ADAM_STEP)
        d_ref[...] = -ADAM_LR * (m_hat / (jnp.sqrt(v_hat) + ADAM_EPS) + ADAM_WD * w_ref[...])
        mo_ref[...] = m_new
        vo_ref[...] = v_new

    blk = pl.BlockSpec((tr, tc), lambda i, j: (i, j))
    return pl.pallas_call(
        body, name=name, grid=(rows // tr, cols // tc),
        in_specs=[blk] * 4, out_specs=[blk] * 3,
        out_shape=[jax.ShapeDtypeStruct(w.shape, F32)] * 3,
        compiler_params=_params(("parallel", "parallel")),
    )(w, g, m, v)


def _row_tile8(rows):
    best = rows
    for t in range(8, 257, 8):
        if rows % t == 0:
            best = t
    return best


def kernel(x, meta_tokens, norm_g, w_in, b_forget, pool_w, pool_scale, w_up_pool, w_up_attn, w_out, final_norm_g, loss_target, m_meta_tokens, m_norm_g, m_w_in, m_b_forget, m_pool_w, m_pool_scale, m_w_up_pool, m_w_up_attn, m_w_out, m_final_norm_g, v_meta_tokens, v_norm_g, v_w_in, v_b_forget, v_pool_w, v_pool_scale, v_w_up_pool, v_w_up_attn, v_w_out, v_final_norm_g):
    seq = x.shape[1]
    assert seq % ROW_TILE == 0 and x.shape[0] == 1
    lp = seq + ROW_TILE
    nb = lp // ROW_TILE
    lk = -(-lp // KV_TILE_BWD) * KV_TILE_BWD
    core = jnp.reshape(lax.axis_index("c"), (1,)).astype(jnp.int32)
    x2 = x[0]
    target = loss_target[0]
    sh_d = D_MODEL // N_CHIPS

    w_in_t = jnp.transpose(w_in[0])
    gf = final_norm_g.reshape(1, D_MODEL)
    bfg = jnp.pad(b_forget, ((0, 0), (0, F_COLS - N_HEADS)))
    pw_b = pool_w[0].astype(BF16)
    hn, (wg_in, meta_g) = _rmsnorm_fwd(x2, norm_g, lk, [w_in_t.astype(BF16), meta_tokens], [1, 0])
    wt = wg_in.reshape(-1, D_MODEL)
    wt_e = jnp.concatenate([wt[REF_U:REF_Q], wt[REF_ZA:REF_F], wt[REF_GP:]], axis=0)
    wt_f = jnp.pad(wt[REF_F:REF_GP], ((0, F_COLS - N_HEADS), (0, 0)))
    meta_full = jnp.transpose(meta_g, (1, 0, 2)).reshape(N_META, D_MODEL)
    metapad = jnp.pad(meta_full, ((PAD_ROWS, 0), (0, 0)))

    tm = _row_tile(lp, 1100)
    e, (wg_up_p, wg_up_a, wg_out) = _mm_nt(
        hn, wt_e, F32, "in_proj_gates", lp, E_COLS, tm, 512,
        gather=([w_up_pool[0].astype(BF16), w_up_attn[0].astype(BF16), w_out[0].astype(BF16)], [0, 0, 0]))
    wup_p = jnp.transpose(wg_up_p, (1, 0, 2)).reshape(POOL_WIDTH, D_MODEL)
    wup_a = jnp.transpose(wg_up_a, (1, 0, 2)).reshape(ATTN_WIDTH, D_MODEL)
    wout = wg_out.reshape(D_MODEL, D_MODEL)
    qkv = _mm_nt(hn, wt, BF16, "in_proj_qkv", lk, QKV_COLS, _row_tile(lk, 1300), 512, row_block=QKV_ROW0 // 512,
                 scale_first=HEAD_DIM ** -0.5)
    f = _mm_nt(hn, wt_f, F32, "in_proj_forget", lp, F_COLS, tm, F_COLS)
    c = _gates_fwd(f, bfg)
    c_t = jnp.transpose(c[:, :N_HEADS])
    bias = c_t[None, :, :] - jnp.transpose(c_t[:, ::ROW_TILE])[:, :, None]
    bias = jnp.where(jnp.arange(lp) < PAD_ROWS, -NEG, bias)
    bias = jnp.pad(bias, ((0, 0), (0, 0), (0, lk - lp)))
    by_kv = lambda t: jnp.transpose(bias.reshape(nb, N_HEADS // 2, 2, lk // t, t), (0, 1, 3, 2, 4))
    bias, bias_bwd = by_kv(KV_TILE), by_kv(KV_TILE_BWD)
    y_pool = _pool_fwd(e, pw_b, pool_scale)
    o, lse = _attn_fwd(qkv, bias, nb)
    merged, y_attn = _merge_fwd(y_pool, o, e, wup_p, wup_a)
    dh1, loss_part, dgf = _head_fwd_bwd(merged, wout, x2, metapad, gf, target)

    dap, daa, dgate, dza, do, delta, dy_pool = _merge_bwd(dh1, wout, y_pool, y_attn, wup_p, wup_a, e, o)
    dpc, dzp, dscale, dpw = _pool_bwd_local(e, dy_pool, pw_b, pool_scale)
    du = _pool_bwd_window(dpc)
    dq, dk, dv, dc4, dcq = _attn_bwd(qkv, do, lse, delta, bias_bwd, nb)
    dc = jnp.transpose(dc4, (1, 3, 0, 2)).reshape(lk, N_HEADS)[:lp]
    df, db = _gates_bwd(jnp.pad(dc, ((0, 0), (0, F_COLS - N_HEADS))), dcq, f, bfg)
    dproj = jnp.concatenate([du, dzp, dza, dgate, (dq * HEAD_DIM ** -0.5).astype(BF16), dk[:lp], dv[:lp]], axis=1)
    tk = _row_tile(lp, 1100)
    dw_out = _mm_tn(merged, dh1, "grad_w_out", lp, 512, D_MODEL, tk)
    dw_up_p = _mm_tn(y_pool, dap, "grad_w_up_pool", lp, 512, D_MODEL, tk, chunks=N_CHIPS)
    dw_up_a = _mm_tn(y_attn, daa, "grad_w_up_attn", lp, 512, D_MODEL, tk, chunks=N_CHIPS)
    dwt_f = _mm_tn(df, hn, "grad_w_in_forget", lp, F_COLS, D_MODEL, tk)

    def pad8(a):
        return jnp.pad(a, ((0, (-a.shape[0]) % 8), (0, 0)))

    small_parts = [pad8(a) for a in (dgf.reshape(-1, LANES), dscale.reshape(-1, LANES), db, dpw.reshape(-1, LANES),
                                     loss_part)]
    small = jnp.concatenate(small_parts, axis=0)
    soffs = [0]
    for p in small_parts:
        soffs.append(soffs[-1] + p.shape[0])

    gs_rows = [dw_up_p, dw_up_a, dw_out.reshape(N_CHIPS, sh_d, D_MODEL)]
    half = MAIN_COLS // 2
    dwt_a, (*rb_rows, rb_f, sr) = _mm_tn(dproj, hn, "grad_w_in_a", lp, half // 2, D_MODEL, tk, m=half, m_off=0,
                                         swap=(gs_rows + [dwt_f, small], [1, 1, 1, 1, None]))
    dwt_b, (rb_a,) = _mm_tn(dproj, hn, "grad_w_in_b", lp, half // 2, D_MODEL, tk, m=half, m_off=2, swap=([dwt_a], [1]))
    (rb_b,) = _sibling_exchange([dwt_b], [1])
    *hs_rows, sc = _add_sibling_half(gs_rows, rb_rows, [1, 1, 1], small, sr, core)
    h_a, h_b, h_f = _add_halves_2d([dwt_a, dwt_b, dwt_f], [rb_a, rb_b, rb_f], core)
    h_in = jnp.concatenate([h_a[E_U:E_ZA], h_b[D_MODEL:], h_a[E_ZA:E_GP], h_f[:N_HEADS], h_a[E_GP:], h_b[:D_MODEL]], axis=0)
    hs = [h_in.reshape(N_CHIPS, -1, D_MODEL // 2)] + hs_rows
    grad_axes = [2, 1, 1, 1]
    (grad_x, g_block0, dg1), qs, sq = _input_bwd(dproj, df, wt_e, wt, wt_f, x2, metapad, dh1, norm_g, hs, sc)
    g_w_in_t, g_w_up_p, g_w_up_a, g_w_out, st = _reduce_allgather(qs, grad_axes, sq)
    late = _small_allreduce(jnp.concatenate([pad8(dg1.reshape(-1, LANES)), g_block0[PAD_ROWS:].reshape(-1, LANES)], axis=0))
    n_norm = D_MODEL // LANES
    g_norm = late[:n_norm].reshape(1, D_MODEL)
    chip = 2 * lax.axis_index("x") + lax.axis_index("y")
    g_meta = lax.dynamic_slice_in_dim(late[n_norm:].reshape(N_META, D_MODEL), chip * sh_d, sh_d, axis=1)

    spiece = lambda k, rows: st[soffs[k]:soffs[k] + rows]
    g_final = spiece(0, D_MODEL // LANES).reshape(1, D_MODEL)
    g_scale = spiece(1, POOL_WIDTH // LANES).reshape(1, POOL_WIDTH)
    g_bf = spiece(2, 1)
    g_pw = spiece(3, POOL_WIDTH)
    loss = st[soffs[4], 0]

    def pad_lanes(a):
        return jnp.pad(a, ((0, 0), (0, F_COLS - N_HEADS)))

    upd = [
        ("meta_tokens", meta_tokens, g_meta, m_meta_tokens, v_meta_tokens),
        ("norm_g", norm_g, g_norm, m_norm_g, v_norm_g),
        ("w_in", w_in_t, g_w_in_t, jnp.transpose(m_w_in[0]), jnp.transpose(v_w_in[0])),
        ("b_forget", pad_lanes(b_forget), g_bf, pad_lanes(m_b_forget), pad_lanes(v_b_forget)),
        ("pool_w", pool_w.reshape(-1, LANES), g_pw, m_pool_w.reshape(-1, LANES), v_pool_w.reshape(-1, LANES)),
        ("pool_scale", pool_scale, g_scale, m_pool_scale, v_pool_scale),
        ("w_up_pool", w_up_pool[0], g_w_up_p, m_w_up_pool[0], v_w_up_pool[0]),
        ("w_up_attn", w_up_attn[0], g_w_up_a, m_w_up_attn[0], v_w_up_attn[0]),
        ("w_out", w_out[0], g_w_out, m_w_out[0], v_w_out[0]),
        ("final_norm_g", gf, g_final, m_final_norm_g.reshape(1, D_MODEL), v_final_norm_g.reshape(1, D_MODEL)),
    ]
    shapes = [meta_tokens.shape, norm_g.shape, w_in.shape, b_forget.shape, pool_w.shape, pool_scale.shape,
              w_up_pool.shape, w_up_attn.shape, w_out.shape, final_norm_g.shape]
    grads, deltas, new_ms, new_vs = [], [], [], []
    for (name, w_, g_, m_in, v_in), shp in zip(upd, shapes):
        d_, mn_, vn_ = _adamw(w_, g_, m_in, v_in, "adamw_" + name)
        res = (g_, d_, mn_, vn_)
        if name == "b_forget":
            res = tuple(a[:, :N_HEADS] for a in res)
        if name == "w_in":
            res = tuple(jnp.transpose(a) for a in res)
        for lst, a in zip((grads, deltas, new_ms, new_vs), res):
            lst.append(a.reshape(shp))

    return (loss, grad_x.reshape(x.shape), *grads, *deltas, *new_ms, *new_vs)
```

```python
import jax
import jax.numpy as jnp
from jax import lax
from jax.experimental import pallas as pl
from jax.experimental.pallas import tpu as pltpu

F32 = jnp.float32
BF16 = jnp.bfloat16
MESH = pl.DeviceIdType.MESH
HIGHEST = lax.Precision.HIGHEST
HBM = pl.BlockSpec(memory_space=pltpu.HBM)

D_MODEL = 1024
N_META = 16
POOL_WIDTH = 512
POOL_GROUP = 128
POOL_WINDOWS = (2, 4, 8, 16)
N_HEADS = 8
HEAD_DIM = 64
ATTN_WIDTH = 512
RMS_EPS = 1e-6
N_CHIPS = 4

ADAM_LR = 0.001
ADAM_B1 = 0.9
ADAM_B2 = 0.999
ADAM_EPS = 1e-08
ADAM_WD = 0.01
ADAM_STEP = 10

LANES = 128
ROW_TILE = 256
KV_TILE = 512
KV_TILE_BWD = 1024
PAD_ROWS = ROW_TILE - N_META
NEG = -1e30

E_U, E_ZP, E_ZA, E_GP, E_GA, E_COLS = 0, 512, 1024, 1536, 2560, 3584
QKV_COLS = 3 * ATTN_WIDTH
MAIN_COLS = E_COLS + QKV_COLS
F_COLS = LANES
REF_U, REF_Q, REF_ZA, REF_F, REF_GP = 0, 1024, 2560, 3072, 3080
VMEM_LIMIT = 48 * 1024 * 1024


def _params(sem=None, vmem=VMEM_LIMIT):
    return pltpu.CompilerParams(dimension_semantics=sem, vmem_limit_bytes=vmem)


def _in_hbm(*arrays):
    return [pltpu.with_memory_space_constraint(a, pltpu.HBM) for a in arrays]


def _row_tile(n, target):
    best = 16
    for t in range(16, target + 1, 16):
        if n % t == 0:
            best = t
    return best


def _sigmoid(x):
    return 1.0 / (1.0 + jnp.exp(-x))


def _half(ref, axis, which):
    n = ref.shape[axis] // 2
    idx = [slice(None)] * len(ref.shape)
    idx[axis] = pl.ds(pl.multiple_of(which * n, n), n)
    return ref.at[tuple(idx)]


def _half_shape(shape, axis):
    s = list(shape)
    s[axis] //= 2
    return tuple(s)


def _gather_start(ins, outs, axes, send, recv, fsend, frecv, local):
    x, y, c = lax.axis_index("x"), lax.axis_index("y"), lax.axis_index("c")
    me = 2 * x + y
    for t in range(len(ins)):
        pltpu.make_async_copy(ins[t], outs[t].at[me], local.at[t]).start()
        for k, chip in enumerate([(1 - x, y), (x, 1 - y), (1 - x, 1 - y)]):
            pltpu.make_async_remote_copy(
                src_ref=_half(ins[t], axes[t], c), dst_ref=_half(outs[t].at[me], axes[t], c),
                send_sem=send.at[3 * t + k], recv_sem=recv.at[3 * t + k], device_id=(*chip, c), device_id_type=MESH).start()


def _gather_finish(ins, outs, axes, send, recv, fsend, frecv, local):
    x, y, c = lax.axis_index("x"), lax.axis_index("y"), lax.axis_index("c")
    me = 2 * x + y
    sibling = (x, y, 1 - c)
    chips = [(1 - x, y), (x, 1 - y), (1 - x, 1 - y)]
    n = len(ins)

    def over_ici(t, k, chip, dst_slot):
        return pltpu.make_async_remote_copy(
            src_ref=_half(ins[t], axes[t], c), dst_ref=_half(outs[t].at[dst_slot], axes[t], c),
            send_sem=send.at[3 * t + k], recv_sem=recv.at[3 * t + k], device_id=(*chip, c), device_id_type=MESH)

    def to_sibling(t, k, slot, which):
        return pltpu.make_async_remote_copy(
            src_ref=_half(outs[t].at[slot], axes[t], which), dst_ref=_half(outs[t].at[slot], axes[t], which),
            send_sem=fsend.at[3 * t + k], recv_sem=frecv.at[3 * t + k], device_id=sibling, device_id_type=MESH)

    forwards = []
    for t in range(n):
        for k, (px, py) in enumerate(chips):
            over_ici(t, k, (px, py), 2 * px + py).wait_recv()
            fw = to_sibling(t, k, 2 * px + py, c)
            fw.start()
            forwards.append(fw)
    for t in range(n):
        for k, (px, py) in enumerate(chips):
            to_sibling(t, k, 2 * px + py, 1 - c).wait_recv()
    for t in range(n):
        for k, chip in enumerate(chips):
            over_ici(t, k, chip, me).wait_send()
    for fw in forwards:
        fw.wait_send()
    for t in range(n):
        pltpu.make_async_copy(ins[t], outs[t].at[me], local.at[t]).wait()


def _gather_sems(n):
    return [pltpu.SemaphoreType.DMA((3 * n,)), pltpu.SemaphoreType.DMA((3 * n,)), pltpu.SemaphoreType.DMA((3 * n,)),
            pltpu.SemaphoreType.DMA((3 * n,)), pltpu.SemaphoreType.DMA((n,))]


def _gathered_shapes(shards):
    return [jax.ShapeDtypeStruct((N_CHIPS,) + s.shape, s.dtype) for s in shards]


def _swap_copies(srcs, dsts, axes, send, recv):
    x, y, c = lax.axis_index("x"), lax.axis_index("y"), lax.axis_index("c")
    return [pltpu.make_async_remote_copy(
        src_ref=srcs[t] if axes[t] is None else _half(srcs[t], axes[t], 1 - c), dst_ref=dsts[t],
        send_sem=send.at[t], recv_sem=recv.at[t], device_id=(x, y, 1 - c), device_id_type=MESH) for t in range(len(srcs))]


def _swap_shapes(srcs, axes):
    return [jax.ShapeDtypeStruct(g.shape if a is None else _half_shape(g.shape, a), g.dtype) for g, a in zip(srcs, axes)]


def _sibling_exchange(gs, axes):
    n = len(gs)

    def body(*refs):
        cps = _swap_copies(refs[:n], refs[n:2 * n], axes, *refs[2 * n:])
        for cp in cps:
            cp.start()
        for cp in cps:
            cp.wait()

    return pl.pallas_call(
        body, name="grad_sibling_exchange",
        in_specs=[HBM] * n, out_specs=[HBM] * n, out_shape=_swap_shapes(gs, axes),
        scratch_shapes=[pltpu.SemaphoreType.DMA((n,)), pltpu.SemaphoreType.DMA((n,))],
    )(*_in_hbm(*gs))


def _add_halves_2d(parts, rbs, core):
    n = len(parts)
    rows, w = rbs[0].shape
    tr = 512
    last_rows = rbs[-1].shape[0]

    def body(core_ref, *refs):
        p_refs, r_refs, h_refs = refs[:n], refs[n:2 * n], refs[2 * n:]
        for t in range(n - 1):
            h_refs[t][...] = (p_refs[t][...] + r_refs[t][...]).astype(BF16)

        @pl.when(pl.program_id(0) == 0)
        def _():
            h_refs[n - 1][...] = (p_refs[n - 1][...] + r_refs[n - 1][...]).astype(BF16)

    mine = lambda r: pl.BlockSpec((r, w), lambda i, cr: (i, cr[0]))
    tile = lambda r: pl.BlockSpec((r, w), lambda i, cr: (i, 0))
    mine_small = pl.BlockSpec((last_rows, w), lambda i, cr: (0, cr[0]))
    tile_small = pl.BlockSpec((last_rows, w), lambda i, cr: (0, 0))
    return pl.pallas_call(
        body, name="grad_add_sibling_w_in",
        grid_spec=pltpu.PrefetchScalarGridSpec(
            num_scalar_prefetch=1, grid=(rows // tr,),
            in_specs=[mine(tr)] * (n - 1) + [mine_small] + [tile(tr)] * (n - 1) + [tile_small],
            out_specs=[tile(tr)] * (n - 1) + [tile_small]),
        out_shape=[jax.ShapeDtypeStruct(r.shape, BF16) for r in rbs],
        compiler_params=_params(("arbitrary",)),
    )(core, *parts, *rbs)


def _add_sibling_half(gs, rbs, axes, small, sr, core):
    n = len(gs)

    def body(core_ref, *refs):
        g_refs, rb_refs = refs[:n], refs[n:2 * n]
        s_ref, sr_ref = refs[2 * n], refs[2 * n + 1]
        h_refs, sc_ref = refs[2 * n + 2:3 * n + 2], refs[3 * n + 2]
        for t in range(n):
            h_refs[t][...] = (g_refs[t][...] + rb_refs[t][...]).astype(BF16)
        sc_ref[...] = s_ref[...] + sr_ref[...]

    def mine(rb, axis):
        blk = (None,) + rb.shape[1:]
        if axis == 2:
            return pl.BlockSpec(blk, lambda j, cr: (j, 0, cr[0]))
        return pl.BlockSpec(blk, lambda j, cr: (j, cr[0], 0))

    chunk = lambda rb: pl.BlockSpec((None,) + rb.shape[1:], lambda j, cr: (j, 0, 0))
    whole = pl.BlockSpec(small.shape, lambda j, cr: (0, 0))
    return pl.pallas_call(
        body, name="grad_add_sibling",
        grid_spec=pltpu.PrefetchScalarGridSpec(
            num_scalar_prefetch=1, grid=(N_CHIPS,),
            in_specs=[mine(rb, a) for rb, a in zip(rbs, axes)] + [chunk(rb) for rb in rbs] + [whole, whole],
            out_specs=[chunk(rb) for rb in rbs] + [whole]),
        out_shape=[jax.ShapeDtypeStruct(rb.shape, BF16) for rb in rbs] + [jax.ShapeDtypeStruct(small.shape, F32)],
        compiler_params=_params(("arbitrary",)),
    )(core, *gs, *rbs, small, sr)


def _chip_scatter_plan(h_refs, s_ref, q_refs, sq_ref, send, recv, local):
    n = len(h_refs)
    x, y, c = lax.axis_index("x"), lax.axis_index("y"), lax.axis_index("c")
    me = 2 * x + y
    chips = [(1 - x, y), (x, 1 - y), (1 - x, 1 - y)]

    def copy(t, k, chip, src_slot, dst_slot):
        src = h_refs[t].at[src_slot] if t < n else s_ref
        dst = (q_refs[t] if t < n else sq_ref).at[dst_slot]
        return pltpu.make_async_remote_copy(src_ref=src, dst_ref=dst, send_sem=send.at[3 * t + k],
                                            recv_sem=recv.at[3 * t + k], device_id=(*chip, c), device_id_type=MESH)

    own = [pltpu.make_async_copy(h_refs[t].at[me], q_refs[t].at[me], local.at[t]) for t in range(n)]
    own.append(pltpu.make_async_copy(s_ref, sq_ref.at[me], local.at[n]))
    sends = [copy(t, k, (px, py), 2 * px + py, me) for t in range(n + 1) for k, (px, py) in enumerate(chips)]
    recvs = [copy(t, k, (px, py), me, 2 * px + py) for t in range(n + 1) for k, (px, py) in enumerate(chips)]
    return own, sends, recvs


def _small_allreduce(buf):
    def body(b_ref, o_ref, sib_buf, chip_buf, send, recv):
        x, y, c = lax.axis_index("x"), lax.axis_index("y"), lax.axis_index("c")
        me = 2 * x + y
        chips = [(1 - x, y), (x, 1 - y), (1 - x, 1 - y)]
        swap = pltpu.make_async_remote_copy(src_ref=b_ref, dst_ref=sib_buf, send_sem=send.at[0], recv_sem=recv.at[0],
                                            device_id=(x, y, 1 - c), device_id_type=MESH)
        swap.start()
        swap.wait()
        chip_buf[me] = b_ref[...] + sib_buf[...]

        def copy(k, chip, slot):
            return pltpu.make_async_remote_copy(src_ref=chip_buf.at[slot], dst_ref=chip_buf.at[slot], send_sem=send.at[1 + k],
                                                recv_sem=recv.at[1 + k], device_id=(*chip, c), device_id_type=MESH)

        sends = [copy(k, chip, me) for k, chip in enumerate(chips)]
        for cp in sends:
            cp.start()
        for k, (px, py) in enumerate(chips):
            copy(k, (px, py), 2 * px + py).wait_recv()
        for cp in sends:
            cp.wait_send()
        o_ref[...] = ((chip_buf[0] + chip_buf[1]) + chip_buf[2]) + chip_buf[3]

    vmem = pl.BlockSpec(memory_space=pltpu.VMEM)
    return pl.pallas_call(
        body, name="small_allreduce", in_specs=[vmem], out_specs=vmem,
        out_shape=jax.ShapeDtypeStruct(buf.shape, F32),
        scratch_shapes=[pltpu.VMEM(buf.shape, F32), pltpu.VMEM((N_CHIPS,) + buf.shape, F32),
                        pltpu.SemaphoreType.DMA((4,)), pltpu.SemaphoreType.DMA((4,))],
        compiler_params=_params(),
    )(buf)


def _reduce_allgather(qs, axes, sq):
    n = len(qs)
    shard_shapes = [tuple(d * 2 if i == a - 1 else d for i, d in enumerate(q.shape[1:])) for q, a in zip(qs, axes)]

    def body(*refs):
        q_refs, sq_ref = refs[:n], refs[n]
        o_refs, st_ref = refs[n + 1:2 * n + 1], refs[2 * n + 1]
        send, recv = refs[2 * n + 2:]
        x, y, c = lax.axis_index("x"), lax.axis_index("y"), lax.axis_index("c")

        def swap(t, which):
            return pltpu.make_async_remote_copy(
                src_ref=_half(o_refs[t], axes[t] - 1, which), dst_ref=_half(o_refs[t], axes[t] - 1, which),
                send_sem=send.at[t], recv_sem=recv.at[t], device_id=(x, y, 1 - c), device_id_type=MESH)

        sent = []
        for t in range(n):
            q = q_refs[t]
            total = ((q[0].astype(F32) + q[1].astype(F32)) + q[2].astype(F32)) + q[3].astype(F32)
            _half(o_refs[t], axes[t] - 1, c)[...] = total
            cp = swap(t, c)
            cp.start()
            sent.append(cp)
        st_ref[...] = ((sq_ref[0] + sq_ref[1]) + sq_ref[2]) + sq_ref[3]
        for t in range(n):
            swap(t, 1 - c).wait_recv()
        for cp in sent:
            cp.wait_send()

    vmem = pl.BlockSpec(memory_space=pltpu.VMEM)
    return pl.pallas_call(
        body, name="grad_reduce_allgather",
        in_specs=[vmem] * (n + 1), out_specs=[vmem] * (n + 1),
        out_shape=[jax.ShapeDtypeStruct(s, F32) for s in shard_shapes] + [jax.ShapeDtypeStruct(sq.shape[1:], F32)],
        scratch_shapes=[pltpu.SemaphoreType.DMA((n,)), pltpu.SemaphoreType.DMA((n,))],
        compiler_params=_params(),
    )(*qs, sq)


def _dot_nt(a, b):
    return lax.dot_general(a, b, (((1,), (1,)), ((), ())), preferred_element_type=F32)


def _dot_tn(a, b):
    return lax.dot_general(a, b, (((0,), (0,)), ((), ())), preferred_element_type=F32)


def _mm_nt(a, bt, out_dtype, name, m, n, tm, tn, row_block=0, scale_first=None, gather=None):
    k = a.shape[1]
    shards, axes = gather if gather is not None else ([], [])
    ng = len(shards)
    grid = (m // tm, n // tn)

    def body(a_ref, b_ref, *rest):
        ins, o_ref, outs, sems = rest[:ng], rest[ng], rest[ng + 1:2 * ng + 1], rest[2 * ng + 1:]
        first = (pl.program_id(0) == 0) & (pl.program_id(1) == 0)
        last = (pl.program_id(0) == grid[0] - 1) & (pl.program_id(1) == grid[1] - 1)
        if ng:
            @pl.when(first)
            def _():
                _gather_start(ins, outs, axes, *sems)

        r = _dot_nt(a_ref[...], b_ref[...])
        if scale_first is not None:
            r = r * jnp.where(pl.program_id(1) == 0, scale_first, 1.0)
        o_ref[...] = r.astype(out_dtype)

        if ng:
            @pl.when(last)
            def _():
                _gather_finish(ins, outs, axes, *sems)

    res = pl.pallas_call(
        body, name=name, grid=grid,
        in_specs=[pl.BlockSpec((tm, k), lambda i, j: (i, 0)), pl.BlockSpec((tn, k), lambda i, j: (row_block + j, 0))]
        + [HBM] * ng,
        out_specs=[pl.BlockSpec((tm, tn), lambda i, j: (i, j))] + [HBM] * ng,
        out_shape=[jax.ShapeDtypeStruct((m, n), out_dtype)] + _gathered_shapes(shards),
        scratch_shapes=_gather_sems(ng) if ng else [],
        compiler_params=_params(("arbitrary", "arbitrary") if ng else ("parallel", "parallel")),
    )(a, bt, *_in_hbm(*shards))
    return (res[0], res[1:]) if ng else res[0]


def _mm_tn(a, b, name, k, tm, tn, tk, chunks=1, m=None, m_off=0, swap=None):
    m = a.shape[1] if m is None else m
    n = b.shape[1]
    cw = n // chunks
    srcs, axes = swap if swap is not None else ([], [])
    ns = len(srcs)
    grid = (m // tm, n // tn, k // tk)

    def body(a_ref, b_ref, *rest):
        s_refs, o_ref, d_refs, sems = rest[:ns], rest[ns], rest[ns + 1:2 * ns + 1], rest[2 * ns + 1:]
        ids = [pl.program_id(d) for d in range(3)]
        if ns:
            @pl.when((ids[0] == 0) & (ids[1] == 0) & (ids[2] == 0))
            def _():
                for cp in _swap_copies(s_refs, d_refs, axes, *sems):
                    cp.start()

        @pl.when(ids[2] == 0)
        def _():
            o_ref[...] = jnp.zeros_like(o_ref)
        r = _dot_tn(a_ref[...].astype(BF16), b_ref[...].astype(BF16))
        if chunks > 1:
            for c in range(chunks):
                o_ref[c] += r[:, c * cw:(c + 1) * cw]
        else:
            o_ref[...] += r

        if ns:
            @pl.when((ids[0] == grid[0] - 1) & (ids[1] == grid[1] - 1) & (ids[2] == grid[2] - 1))
            def _():
                for cp in _swap_copies(s_refs, d_refs, axes, *sems):
                    cp.wait()

    if chunks > 1:
        assert tn == n
        out_spec = pl.BlockSpec((chunks, tm, cw), lambda i, j, kk: (0, i, 0))
        out_shape = jax.ShapeDtypeStruct((chunks, m, cw), F32)
    else:
        out_spec = pl.BlockSpec((tm, tn), lambda i, j, kk: (i, j))
        out_shape = jax.ShapeDtypeStruct((m, n), F32)
    res = pl.pallas_call(
        body, name=name, grid=grid,
        in_specs=[pl.BlockSpec((tk, tm), lambda i, j, kk: (kk, m_off + i)), pl.BlockSpec((tk, tn), lambda i, j, kk: (kk, j))]
        + [HBM] * ns,
        out_specs=[out_spec] + [HBM] * ns, out_shape=[out_shape] + _swap_shapes(srcs, axes),
        scratch_shapes=[pltpu.SemaphoreType.DMA((ns,)), pltpu.SemaphoreType.DMA((ns,))] if ns else [],
        compiler_params=_params(("arbitrary",) * 3 if ns else ("parallel", "parallel", "arbitrary")),
    )(a, b, *_in_hbm(*srcs))
    return (res[0], res[1:]) if ns else res[0]


def _tokens_spec():
    return pl.BlockSpec((ROW_TILE, D_MODEL), lambda i: (jnp.maximum(i - 1, 0), 0))


def _rmsnorm_fwd(x2, g1, lk, shards, axes):
    nb = x2.shape[0] // ROW_TILE + 1
    nblk = lk // ROW_TILE
    ng = len(shards)
    meta_cols = shards[-1].shape[1]

    def body(x_ref, g_ref, *rest):
        ins, hn_ref, outs = rest[:ng], rest[ng], rest[ng + 1:2 * ng + 1]
        sems, meta_buf, meta_sem = rest[2 * ng + 1:2 * ng + 6], rest[2 * ng + 6], rest[2 * ng + 7]
        s = pl.program_id(0)
        blk = (s + 1) % nblk

        @pl.when(s == 0)
        def _():
            _gather_start(ins, outs, axes, *sems)

        def normed(h):
            r = lax.rsqrt(jnp.mean(h * h, axis=-1, keepdims=True) + RMS_EPS)
            return ((h * r) * g_ref[...]).astype(BF16)

        @pl.when(s < nblk - 1)
        def _():
            hn_ref[...] = normed(jnp.where(blk >= nb, 0.0, x_ref[...]))

        @pl.when(s == nblk - 1)
        def _():
            _gather_finish(ins, outs, axes, *sems)
            fetch = pltpu.make_async_copy(outs[-1], meta_buf, meta_sem.at[0])
            fetch.start()
            fetch.wait()
            meta = jnp.concatenate([meta_buf[j] for j in range(N_CHIPS)], axis=1)
            hn_ref[...] = normed(jnp.concatenate([jnp.zeros((PAD_ROWS, D_MODEL), F32), meta], axis=0))

    res = pl.pallas_call(
        body, name="rmsnorm_fwd", grid=(nblk,),
        in_specs=[pl.BlockSpec((ROW_TILE, D_MODEL), lambda s: (jnp.clip((s + 1) % nblk - 1, 0, nb - 2), 0)),
                  pl.BlockSpec((1, D_MODEL), lambda s: (0, 0))] + [HBM] * ng,
        out_specs=[pl.BlockSpec((ROW_TILE, D_MODEL), lambda s: ((s + 1) % nblk, 0))] + [HBM] * ng,
        out_shape=[jax.ShapeDtypeStruct((lk, D_MODEL), BF16)] + _gathered_shapes(shards),
        scratch_shapes=_gather_sems(ng) + [pltpu.VMEM((N_CHIPS, N_META, meta_cols), F32), pltpu.SemaphoreType.DMA((1,))],
        compiler_params=_params(("arbitrary",)),
    )(x2, g1, *_in_hbm(*shards))
    return res[0], res[1:]


def _valid_gate_mask(i):
    row = i * ROW_TILE + lax.broadcasted_iota(jnp.int32, (ROW_TILE, F_COLS), 0)
    col = lax.broadcasted_iota(jnp.int32, (ROW_TILE, F_COLS), 1)
    return (row >= PAD_ROWS) & (col < N_HEADS)


def _gates_fwd(f, bfg):
    nb = f.shape[0] // ROW_TILE

    def body(f_ref, b_ref, c_ref, carry):
        i = pl.program_id(0)

        @pl.when(i == 0)
        def _():
            carry[...] = jnp.zeros_like(carry)

        logit = f_ref[...] + b_ref[...]
        lf = jnp.minimum(logit, 0.0) - jnp.log1p(jnp.exp(-jnp.abs(logit)))
        lf = jnp.where(_valid_gate_mask(i), lf, 0.0)
        r_i = lax.broadcasted_iota(jnp.int32, (ROW_TILE, ROW_TILE), 0)
        c_i = lax.broadcasted_iota(jnp.int32, (ROW_TILE, ROW_TILE), 1)
        tri = (c_i <= r_i).astype(F32)
        c_ref[...] = jnp.dot(tri, lf, precision=HIGHEST, preferred_element_type=F32) + carry[...]
        carry[...] = carry[...] + jnp.sum(lf, axis=0, keepdims=True)

    return pl.pallas_call(
        body, name="gates_fwd", grid=(nb,),
        in_specs=[pl.BlockSpec((ROW_TILE, F_COLS), lambda i: (i, 0)), pl.BlockSpec((1, F_COLS), lambda i: (0, 0))],
        out_specs=pl.BlockSpec((ROW_TILE, F_COLS), lambda i: (i, 0)),
        out_shape=jax.ShapeDtypeStruct(f.shape, F32),
        scratch_shapes=[pltpu.VMEM((1, F_COLS), F32)],
        compiler_params=_params(("arbitrary",)),
    )(f, bfg)


def _pool_counts(i):
    row = i * ROW_TILE + lax.broadcasted_iota(jnp.int32, (ROW_TILE, 1), 0)
    return jnp.maximum(row - PAD_ROWS, 0)


def _trailing_sums(xc, levels):
    acc = xc
    for lv in range(levels):
        acc = acc + pltpu.roll(acc, 1 << lv, 0)
    return acc


def _leading_sums(xc, levels):
    n = xc.shape[0]
    acc = xc
    for lv in range(levels):
        acc = acc + pltpu.roll(acc, n - (1 << lv), 0)
    return acc


def _pool_p(u_cur, u_prev, i):
    pos = _pool_counts(i)
    ps, invs = [], []
    for g, w in enumerate(POOL_WINDOWS):
        sl = slice(g * POOL_GROUP, (g + 1) * POOL_GROUP)
        cur = u_cur[:, sl]
        xc = jnp.concatenate([u_prev[:, sl], cur], axis=0)
        win = _trailing_sums(xc, g + 1)[ROW_TILE:, :]
        inv = 1.0 / jnp.minimum(pos + 1, w).astype(F32)
        ps.append(win * inv - cur)
        invs.append(inv)
    return ps, invs


def _pool_fwd(e, pw, scale):
    nb = e.shape[0] // ROW_TILE

    def body(uc_ref, up_ref, z_ref, pw_ref, sc_ref, y_ref):
        i = pl.program_id(0)
        u_cur = uc_ref[...].astype(F32)
        u_prev = jnp.where(i == 0, 0.0, up_ref[...].astype(F32))
        ps, _ = _pool_p(u_cur, u_prev, i)
        z = z_ref[...].astype(F32)
        gate = z * _sigmoid(z)
        for g in range(len(POOL_WINDOWS)):
            sl = slice(g * POOL_GROUP, (g + 1) * POOL_GROUP)
            yraw = jnp.dot(ps[g].astype(BF16), pw_ref[g], preferred_element_type=F32)
            y_ref[:, sl] = ((yraw * sc_ref[:, sl]) * gate[:, sl]).astype(BF16)

    blk = (ROW_TILE, POOL_WIDTH)
    return pl.pallas_call(
        body, name="pool_fwd", grid=(nb,),
        in_specs=[pl.BlockSpec(blk, lambda i: (i, 0)), pl.BlockSpec(blk, lambda i: (jnp.maximum(i - 1, 0), 0)),
                  pl.BlockSpec(blk, lambda i: (i, 1)),
                  pl.BlockSpec((len(POOL_WINDOWS), POOL_GROUP, POOL_GROUP), lambda i: (0, 0, 0)),
                  pl.BlockSpec((1, POOL_WIDTH), lambda i: (0, 0))],
        out_specs=pl.BlockSpec(blk, lambda i: (i, 0)),
        out_shape=jax.ShapeDtypeStruct((e.shape[0], POOL_WIDTH), BF16),
        compiler_params=_params(("parallel",)),
    )(e, e, e, pw, scale)


def _stack_heads(a):
    first = lax.broadcasted_iota(jnp.int32, a.shape, 1) < HEAD_DIM
    zero = jnp.zeros_like(a)
    return jnp.concatenate([jnp.where(first, a, zero), jnp.where(first, zero, a)], axis=0)


def _unstack_heads(a):
    rows = a.shape[0] // 2
    first = lax.broadcasted_iota(jnp.int32, (rows, LANES), 1) < HEAD_DIM
    return jnp.where(first, a[:rows], a[rows:])


def _causal(i, jj, stacked, kv_tile):
    r = lax.broadcasted_iota(jnp.int32, (stacked * ROW_TILE, kv_tile), 0)
    if stacked == 2:
        r = jnp.where(r >= ROW_TILE, r - ROW_TILE, r)
    kidx = jj * kv_tile + lax.broadcasted_iota(jnp.int32, (stacked * ROW_TILE, kv_tile), 1)
    return kidx <= i * ROW_TILE + r


def _stack_rows(b):
    n = b.shape[1]
    return jnp.concatenate([jnp.broadcast_to(b[0:1], (ROW_TILE, n)), jnp.broadcast_to(b[1:2], (ROW_TILE, n))], axis=0)


def _attn_fwd(qkv, bias, nb):
    lk = qkv.shape[0]
    lp = nb * ROW_TILE
    nkb = lk // KV_TILE
    n_pairs = N_HEADS // 2

    def body(q_ref, k_ref, v_ref, b_ref, o_ref, lse_ref):
        i = pl.program_id(1)
        qs = _stack_heads(q_ref[...])
        last = (i * ROW_TILE) // KV_TILE

        def block(jj, carry, masked):
            m, l, acc = carry
            rows = pl.ds(pl.multiple_of(jj * KV_TILE, KV_TILE), KV_TILE)
            s = _dot_nt(qs, k_ref[rows, :]) - _stack_rows(b_ref[0, 0, jj])
            if masked:
                s = jnp.where(_causal(i, jj, 2, KV_TILE), s, NEG)
            m_new = jnp.maximum(m, jnp.max(s, axis=1, keepdims=True))
            alpha = jnp.exp(m - m_new)
            p = jnp.exp(s - m_new)
            l = alpha * l + jnp.sum(p, axis=1, keepdims=True)
            acc = alpha * acc + jnp.dot(p.astype(BF16), v_ref[rows, :], preferred_element_type=F32)
            return m_new, l, acc

        init = (jnp.full((2 * ROW_TILE, 1), NEG, F32), jnp.zeros((2 * ROW_TILE, 1), F32),
                jnp.zeros((2 * ROW_TILE, LANES), F32))
        carry = lax.fori_loop(0, last, lambda jj, c: block(jj, c, False), init)
        m, l, acc = block(last, carry, True)
        o_ref[...] = _unstack_heads(acc / l)
        lse_ref[...] = _unstack_heads(jnp.broadcast_to(m + jnp.log(l), (2 * ROW_TILE, LANES)))

    return pl.pallas_call(
        body, name="attn_fwd", grid=(n_pairs, nb),
        in_specs=[pl.BlockSpec((ROW_TILE, LANES), lambda hp, i: (i, hp)),
                  pl.BlockSpec((lk, LANES), lambda hp, i: (0, n_pairs + hp)),
                  pl.BlockSpec((lk, LANES), lambda hp, i: (0, 2 * n_pairs + hp)),
                  pl.BlockSpec((1, 1, nkb, 2, KV_TILE), lambda hp, i: (i, hp, 0, 0, 0))],
        out_specs=[pl.BlockSpec((ROW_TILE, LANES), lambda hp, i: (i, hp)),
                   pl.BlockSpec((ROW_TILE, LANES), lambda hp, i: (i, hp))],
        out_shape=[jax.ShapeDtypeStruct((lp, ATTN_WIDTH), F32), jax.ShapeDtypeStruct((lp, ATTN_WIDTH), F32)],
        compiler_params=_params(("parallel", "parallel")),
    )(qkv, qkv, qkv, bias)


def _merge_fwd(y_pool, o, e, wup_p, wup_a):
    lp = o.shape[0]
    nb = lp // ROW_TILE

    def body(yp_ref, o_ref, e_ref, wp_ref, wa_ref, mg_ref, ya_ref):
        za = e_ref[:, E_ZA:E_GP].astype(F32)
        ya = (o_ref[...] * (za * _sigmoid(za))).astype(BF16)
        ya_ref[...] = ya
        a_pool = jnp.dot(yp_ref[...], wp_ref[...], preferred_element_type=F32)
        a_attn = jnp.dot(ya, wa_ref[...], preferred_element_type=F32)
        mg_ref[...] = (_sigmoid(e_ref[:, E_GP:E_GA].astype(F32)) * a_pool
                       + _sigmoid(e_ref[:, E_GA:E_COLS].astype(F32)) * a_attn).astype(BF16)

    return pl.pallas_call(
        body, name="merge_fwd", grid=(nb,),
        in_specs=[pl.BlockSpec((ROW_TILE, POOL_WIDTH), lambda i: (i, 0)),
                  pl.BlockSpec((ROW_TILE, ATTN_WIDTH), lambda i: (i, 0)),
                  pl.BlockSpec((ROW_TILE, E_COLS), lambda i: (i, 0)),
                  pl.BlockSpec((POOL_WIDTH, D_MODEL), lambda i: (0, 0)),
                  pl.BlockSpec((ATTN_WIDTH, D_MODEL), lambda i: (0, 0))],
        out_specs=[pl.BlockSpec((ROW_TILE, D_MODEL), lambda i: (i, 0)),
                   pl.BlockSpec((ROW_TILE, ATTN_WIDTH), lambda i: (i, 0))],
        out_shape=[jax.ShapeDtypeStruct((lp, D_MODEL), BF16), jax.ShapeDtypeStruct((lp, ATTN_WIDTH), BF16)],
        compiler_params=_params(("parallel",)),
    )(y_pool, o, e, wup_p, wup_a)


def _head_fwd_bwd(merged, w_out, x2, metapad, gf, target):
    lp = merged.shape[0]
    nb = lp // ROW_TILE

    def body(mg_ref, w_ref, x_ref, mp_ref, g_ref, t_ref, dh_ref, loss_ref, dg_ref):
        i = pl.program_id(0)

        @pl.when(i == 0)
        def _():
            loss_ref[...] = jnp.zeros_like(loss_ref)
            dg_ref[...] = jnp.zeros_like(dg_ref)

        h0 = jnp.where(i == 0, mp_ref[...], x_ref[...])
        h1 = h0 + jnp.dot(mg_ref[...], w_ref[...], preferred_element_type=F32)
        r = lax.rsqrt(jnp.mean(h1 * h1, axis=-1, keepdims=True) + RMS_EPS)
        xhat = h1 * r
        g = g_ref[...]
        err = jnp.where(i == 0, 0.0, xhat * g - t_ref[...])
        loss_ref[...] += 0.5 * jnp.sum(jnp.mean(err * err, axis=-1, keepdims=True))
        dy = err / D_MODEL
        dg_ref[...] += jnp.sum(dy * xhat, axis=0, keepdims=True)
        dxhat = dy * g
        dh_ref[...] = r * (dxhat - xhat * jnp.mean(dxhat * xhat, axis=-1, keepdims=True))

    return pl.pallas_call(
        body, name="head_fwd_bwd", grid=(nb,),
        in_specs=[pl.BlockSpec((ROW_TILE, D_MODEL), lambda i: (i, 0)),
                  pl.BlockSpec((D_MODEL, D_MODEL), lambda i: (0, 0)),
                  _tokens_spec(), pl.BlockSpec((ROW_TILE, D_MODEL), lambda i: (0, 0)),
                  pl.BlockSpec((1, D_MODEL), lambda i: (0, 0)), _tokens_spec()],
        out_specs=[pl.BlockSpec((ROW_TILE, D_MODEL), lambda i: (i, 0)),
                   pl.BlockSpec((1, LANES), lambda i: (0, 0)), pl.BlockSpec((1, D_MODEL), lambda i: (0, 0))],
        out_shape=[jax.ShapeDtypeStruct((lp, D_MODEL), F32), jax.ShapeDtypeStruct((1, LANES), F32),
                   jax.ShapeDtypeStruct((1, D_MODEL), F32)],
        compiler_params=_params(("arbitrary",)),
    )(merged, w_out, x2, metapad, gf, target)


def _per_head_rowsum(t):
    head = lax.broadcasted_iota(jnp.int32, t.shape, 1) // HEAD_DIM
    out = jnp.zeros_like(t)
    for h in range(N_HEADS):
        sel = head == h
        out = jnp.where(sel, jnp.sum(jnp.where(sel, t, 0.0), axis=1, keepdims=True), out)
    return out


def _merge_bwd(dh1, w_out, y_pool, y_attn, wup_p, wup_a, e, o):
    lp = o.shape[0]
    nb = lp // ROW_TILE

    def body(dh_ref, wo_ref, yp_ref, ya_ref, wp_ref, wa_ref, e_ref, o_ref,
             dap_ref, daa_ref, dg_ref, dza_ref, do_ref, delta_ref, dyp_ref):
        dmerged = _dot_nt(dh_ref[...].astype(BF16), wo_ref[...])
        a_pool = jnp.dot(yp_ref[...], wp_ref[...], preferred_element_type=F32)
        a_attn = jnp.dot(ya_ref[...], wa_ref[...], preferred_element_type=F32)
        sp = _sigmoid(e_ref[:, E_GP:E_GA].astype(F32))
        sa = _sigmoid(e_ref[:, E_GA:E_COLS].astype(F32))
        dap = (dmerged * sp).astype(BF16)
        daa = (dmerged * sa).astype(BF16)
        dap_ref[...] = dap
        daa_ref[...] = daa
        dg_ref[:, :D_MODEL] = (dmerged * a_pool * (sp * (1.0 - sp))).astype(BF16)
        dg_ref[:, D_MODEL:] = (dmerged * a_attn * (sa * (1.0 - sa))).astype(BF16)
        dyp_ref[...] = _dot_nt(dap, wp_ref[...])
        dya = _dot_nt(daa, wa_ref[...])
        za = e_ref[:, E_ZA:E_GP].astype(F32)
        sz = _sigmoid(za)
        o = o_ref[...]
        do = dya * (za * sz)
        do_ref[...] = do.astype(BF16)
        dza_ref[...] = (dya * o * (sz * (1.0 + za * (1.0 - sz)))).astype(BF16)
        delta_ref[...] = _per_head_rowsum(do * o)

    row = lambda w: pl.BlockSpec((ROW_TILE, w), lambda i: (i, 0))
    full = lambda a: pl.BlockSpec(a.shape, lambda i: (0, 0))
    return pl.pallas_call(
        body, name="merge_bwd", grid=(nb,),
        in_specs=[row(D_MODEL), full(w_out), row(POOL_WIDTH), row(ATTN_WIDTH), full(wup_p), full(wup_a),
                  row(E_COLS), row(ATTN_WIDTH)],
        out_specs=[row(D_MODEL), row(D_MODEL), row(2 * D_MODEL), row(ATTN_WIDTH), row(ATTN_WIDTH),
                   row(ATTN_WIDTH), row(POOL_WIDTH)],
        out_shape=[jax.ShapeDtypeStruct((lp, D_MODEL), BF16), jax.ShapeDtypeStruct((lp, D_MODEL), BF16),
                   jax.ShapeDtypeStruct((lp, 2 * D_MODEL), BF16), jax.ShapeDtypeStruct((lp, ATTN_WIDTH), BF16),
                   jax.ShapeDtypeStruct((lp, ATTN_WIDTH), BF16), jax.ShapeDtypeStruct((lp, ATTN_WIDTH), F32),
                   jax.ShapeDtypeStruct((lp, POOL_WIDTH), F32)],
        compiler_params=_params(("parallel",)),
    )(dh1, w_out, y_pool, y_attn, wup_p, wup_a, e, o)


def _pool_bwd_local(e, dy_pool, pw, scale):
    lp = e.shape[0]
    nb = lp // ROW_TILE
    ng = len(POOL_WINDOWS)

    def body(uc_ref, up_ref, z_ref, dy_ref, pw_ref, sc_ref, dpc_ref, dz_ref, dsc_ref, dpw_ref):
        i = pl.program_id(0)

        @pl.when(i == 0)
        def _():
            dsc_ref[...] = jnp.zeros_like(dsc_ref)
            dpw_ref[...] = jnp.zeros_like(dpw_ref)

        u_cur = uc_ref[...].astype(F32)
        u_prev = jnp.where(i == 0, 0.0, up_ref[...].astype(F32))
        ps, invs = _pool_p(u_cur, u_prev, i)
        z = z_ref[...].astype(F32)
        sz = _sigmoid(z)
        dy = dy_ref[...]
        dypre = dy * (z * sz)
        dsilu = sz * (1.0 + z * (1.0 - sz))
        for g in range(ng):
            sl = slice(g * POOL_GROUP, (g + 1) * POOL_GROUP)
            pb = ps[g].astype(BF16)
            w = pw_ref[g]
            yraw = jnp.dot(pb, w, preferred_element_type=F32)
            sc = sc_ref[:, sl]
            dz_ref[:, sl] = (dy[:, sl] * (yraw * sc) * dsilu[:, sl]).astype(BF16)
            dsc_ref[:, sl] += jnp.sum(dypre[:, sl] * yraw, axis=0, keepdims=True)
            dyraw = (dypre[:, sl] * sc).astype(BF16)
            dpw_ref[g] += _dot_tn(pb, dyraw)
            dpc_ref[:, sl] = _dot_nt(dyraw, w) * invs[g]

    blk = (ROW_TILE, POOL_WIDTH)
    return pl.pallas_call(
        body, name="pool_bwd_local", grid=(nb,),
        in_specs=[pl.BlockSpec(blk, lambda i: (i, 0)), pl.BlockSpec(blk, lambda i: (jnp.maximum(i - 1, 0), 0)),
                  pl.BlockSpec(blk, lambda i: (i, 1)), pl.BlockSpec(blk, lambda i: (i, 0)),
                  pl.BlockSpec((ng, POOL_GROUP, POOL_GROUP), lambda i: (0, 0, 0)),
                  pl.BlockSpec((1, POOL_WIDTH), lambda i: (0, 0))],
        out_specs=[pl.BlockSpec(blk, lambda i: (i, 0)), pl.BlockSpec(blk, lambda i: (i, 0)),
                   pl.BlockSpec((1, POOL_WIDTH), lambda i: (0, 0)),
                   pl.BlockSpec((ng, POOL_GROUP, POOL_GROUP), lambda i: (0, 0, 0))],
        out_shape=[jax.ShapeDtypeStruct((lp, POOL_WIDTH), F32), jax.ShapeDtypeStruct((lp, POOL_WIDTH), BF16),
                   jax.ShapeDtypeStruct((1, POOL_WIDTH), F32),
                   jax.ShapeDtypeStruct((ng, POOL_GROUP, POOL_GROUP), F32)],
        compiler_params=_params(("arbitrary",)),
    )(e, e, e, dy_pool, pw, scale)


def _pool_bwd_window(dpc):
    lp = dpc.shape[0]
    nb = lp // ROW_TILE

    def body(cur_ref, nxt_ref, du_ref):
        i = pl.program_id(0)
        cur = cur_ref[...]
        nxt = jnp.where(i == nb - 1, 0.0, nxt_ref[...])
        pos = _pool_counts(i)
        for g, w in enumerate(POOL_WINDOWS):
            sl = slice(g * POOL_GROUP, (g + 1) * POOL_GROUP)
            xc = jnp.concatenate([cur[:, sl], nxt[:, sl]], axis=0)
            win = _leading_sums(xc, g + 1)[:ROW_TILE, :]
            dp = cur[:, sl] * jnp.minimum(pos + 1, w).astype(F32)
            du_ref[:, sl] = (win - dp).astype(BF16)

    blk = (ROW_TILE, POOL_WIDTH)
    return pl.pallas_call(
        body, name="pool_bwd_window", grid=(nb,),
        in_specs=[pl.BlockSpec(blk, lambda i: (i, 0)), pl.BlockSpec(blk, lambda i: (jnp.minimum(i + 1, nb - 1), 0))],
        out_specs=pl.BlockSpec(blk, lambda i: (i, 0)),
        out_shape=jax.ShapeDtypeStruct((lp, POOL_WIDTH), BF16),
        compiler_params=_params(("parallel",)),
    )(dpc, dpc)


def _attn_bwd(qkv, do, lse, delta, bias, nb):
    lk = qkv.shape[0]
    lp = nb * ROW_TILE
    nkb = lk // KV_TILE_BWD
    n_pairs = N_HEADS // 2
    per_kv = KV_TILE_BWD // ROW_TILE

    def body(q_ref, k_ref, v_ref, do_ref, lse_ref, dl_ref, b_ref, dq_ref, dk_ref, dv_ref, dc_ref, dcq_ref):
        jj = pl.program_id(1)

        @pl.when(jj == 0)
        def _():
            dq_ref[...] = jnp.zeros_like(dq_ref)
            dcq_ref[...] = jnp.zeros_like(dcq_ref)

        kb, vb = k_ref[...], v_ref[...]

        def block(i, carry, masked):
            dk_acc, dv_acc, dc_acc = carry
            rows = pl.ds(pl.multiple_of(i * ROW_TILE, ROW_TILE), ROW_TILE)
            qs = _stack_heads(q_ref[rows, :])
            dos = _stack_heads(do_ref[rows, :])
            lse_i, dl_i = lse_ref[rows, :], dl_ref[rows, :]
            s = _dot_nt(qs, kb)
            dp = _dot_nt(dos, vb)
            if masked:
                valid = _causal(i, jj, 1, KV_TILE_BWD)
            ps, dss, dcs, rowsums = [], [], [], []
            for hd in range(2):
                half = slice(hd * ROW_TILE, (hd + 1) * ROW_TILE)
                col = slice(hd * HEAD_DIM, hd * HEAD_DIM + 1)
                sh = s[half] - b_ref[i, 0, 0, hd:hd + 1, :]
                if masked:
                    sh = jnp.where(valid, sh, NEG)
                p = jnp.exp(sh - lse_i[:, col])
                ds = p * (dp[half] - dl_i[:, col])
                ps.append(p.astype(BF16))
                dss.append(ds.astype(BF16))
                dcs.append(jnp.sum(ds, axis=0, keepdims=True))
                rowsums.append(jnp.sum(ds, axis=1, keepdims=True))
            dsb = jnp.concatenate(dss, axis=0)
            dv_acc = dv_acc + _dot_tn(jnp.concatenate(ps, axis=0), dos)
            dk_acc = dk_acc + _dot_tn(dsb, qs)
            dc_acc = dc_acc - jnp.concatenate(dcs, axis=0)
            dq_ref[rows, :] += _unstack_heads(jnp.dot(dsb, kb, preferred_element_type=F32))
            dcq_ref[rows, :] += _unstack_heads(jnp.broadcast_to(jnp.concatenate(rowsums, axis=0), (2 * ROW_TILE, LANES)))
            return dk_acc, dv_acc, dc_acc

        init = (jnp.zeros((KV_TILE_BWD, LANES), F32), jnp.zeros((KV_TILE_BWD, LANES), F32),
                jnp.zeros((2, KV_TILE_BWD), F32))
        first_q = per_kv * jj
        diag_end = jnp.minimum(first_q + per_kv, nb)
        carry = lax.fori_loop(first_q, diag_end, lambda i, c: block(i, c, True), init)
        dk, dv, dc = lax.fori_loop(diag_end, nb, lambda i, c: block(i, c, False), carry)
        dk_ref[...] = dk.astype(BF16)
        dv_ref[...] = dv.astype(BF16)
        dc_ref[0, 0] = dc

    whole = lambda rows: pl.BlockSpec((rows, LANES), lambda hp, jj: (0, hp))
    kv_blk = lambda off: pl.BlockSpec((KV_TILE_BWD, LANES), lambda hp, jj: (jj, off + hp))
    return pl.pallas_call(
        body, name="attn_bwd", grid=(n_pairs, nkb),
        in_specs=[whole(lk), kv_blk(n_pairs), kv_blk(2 * n_pairs), whole(lp), whole(lp), whole(lp),
                  pl.BlockSpec((nb, 1, 1, 2, KV_TILE_BWD), lambda hp, jj: (0, hp, jj, 0, 0))],
        out_specs=[whole(lp), kv_blk(0), kv_blk(0),
                   pl.BlockSpec((1, 1, 2, KV_TILE_BWD), lambda hp, jj: (hp, jj, 0, 0)), whole(lp)],
        out_shape=[jax.ShapeDtypeStruct((lp, ATTN_WIDTH), F32), jax.ShapeDtypeStruct((lk, ATTN_WIDTH), BF16),
                   jax.ShapeDtypeStruct((lk, ATTN_WIDTH), BF16),
                   jax.ShapeDtypeStruct((n_pairs, nkb, 2, KV_TILE_BWD), F32),
                   jax.ShapeDtypeStruct((lp, ATTN_WIDTH), F32)],
        compiler_params=_params(("parallel", "arbitrary")),
    )(qkv, qkv, qkv, do, lse, delta, bias)


def _gates_bwd(dc, dcq, f, bfg):
    nb = f.shape[0] // ROW_TILE

    def body(dc_ref, dcq_ref, f_ref, b_ref, df_ref, db_ref, carry):
        step = pl.program_id(0)
        i = nb - 1 - step

        @pl.when(step == 0)
        def _():
            carry[...] = jnp.zeros_like(carry)
            db_ref[...] = jnp.zeros_like(db_ref)

        dcb = dc_ref[...]
        lane = lax.broadcasted_iota(jnp.int32, (ROW_TILE, F_COLS), 1)
        for h in range(N_HEADS):
            dcb = dcb + jnp.where(lane == h, dcq_ref[:, HEAD_DIM * h:HEAD_DIM * h + 1], 0.0)
        r_i = lax.broadcasted_iota(jnp.int32, (ROW_TILE, ROW_TILE), 0)
        c_i = lax.broadcasted_iota(jnp.int32, (ROW_TILE, ROW_TILE), 1)
        upper = (c_i >= r_i).astype(F32)
        dlf = jnp.dot(upper, dcb, precision=HIGHEST, preferred_element_type=F32) + carry[...]
        carry[...] = carry[...] + jnp.sum(dcb, axis=0, keepdims=True)
        logit = f_ref[...] + b_ref[...]
        dlogit = jnp.where(_valid_gate_mask(i), dlf * _sigmoid(-logit), 0.0)
        df_ref[...] = dlogit.astype(BF16)
        db_ref[...] += jnp.sum(dlogit, axis=0, keepdims=True)

    blk = pl.BlockSpec((ROW_TILE, F_COLS), lambda s: (nb - 1 - s, 0))
    wide = pl.BlockSpec((ROW_TILE, ATTN_WIDTH), lambda s: (nb - 1 - s, 0))
    one = pl.BlockSpec((1, F_COLS), lambda s: (0, 0))
    return pl.pallas_call(
        body, name="gates_bwd", grid=(nb,),
        in_specs=[blk, wide, blk, one], out_specs=[blk, one],
        out_shape=[jax.ShapeDtypeStruct(f.shape, BF16), jax.ShapeDtypeStruct((1, F_COLS), F32)],
        scratch_shapes=[pltpu.VMEM((1, F_COLS), F32)],
        compiler_params=_params(("arbitrary",)),
    )(dc, dcq, f, bfg)


def _input_bwd(dproj, df, wt_e, wt_qkv, wt_f, x2, metapad, dh1, g1, hs, sc):
    nb = x2.shape[0] // ROW_TILE + 1
    n = len(hs)

    def body(dp_ref, df_ref, we_ref, w_ref, wf_ref, x_ref, mp_ref, dh_ref, g_ref, *rest):
        h_refs, s_ref = rest[:n], rest[n]
        gx_ref, g0_ref, dg_ref = rest[n + 1:n + 4]
        q_refs, sq_ref = rest[n + 4:2 * n + 4], rest[2 * n + 4]
        sems = rest[2 * n + 5:]
        i = pl.program_id(0)

        @pl.when(i == 0)
        def _():
            dg_ref[...] = jnp.zeros_like(dg_ref)
            own, sends, _ = _chip_scatter_plan(h_refs, s_ref, q_refs, sq_ref, *sems)
            for cp in own + sends:
                cp.start()

        dhn = (jnp.dot(dp_ref[:, :E_COLS], we_ref[...], preferred_element_type=F32)
               + jnp.dot(dp_ref[:, E_COLS:], w_ref[...], preferred_element_type=F32)
               + jnp.dot(df_ref[...], wf_ref[...], preferred_element_type=F32))
        h0 = jnp.where(i == 0, mp_ref[...], x_ref[...])
        r = lax.rsqrt(jnp.mean(h0 * h0, axis=-1, keepdims=True) + RMS_EPS)
        xhat = h0 * r
        dg_ref[...] += jnp.sum(dhn * xhat, axis=0, keepdims=True)
        dxhat = dhn * g_ref[...]
        dh0 = dh_ref[...] + r * (dxhat - xhat * jnp.mean(dxhat * xhat, axis=-1, keepdims=True))
        gx_ref[...] = dh0

        @pl.when(i == 0)
        def _():
            g0_ref[...] = dh0

        @pl.when(i == nb - 1)
        def _():
            own, sends, recvs = _chip_scatter_plan(h_refs, s_ref, q_refs, sq_ref, *sems)
            for cp in recvs:
                cp.wait_recv()
            for cp in sends:
                cp.wait_send()
            for cp in own:
                cp.wait()

    const = lambda shape: pl.BlockSpec(shape, lambda i: (0, 0))
    res = pl.pallas_call(
        body, name="input_bwd", grid=(nb,),
        in_specs=[pl.BlockSpec((ROW_TILE, MAIN_COLS), lambda i: (i, 0)), pl.BlockSpec((ROW_TILE, F_COLS), lambda i: (i, 0)),
                  const((E_COLS, D_MODEL)), const((QKV_COLS, D_MODEL)), const((F_COLS, D_MODEL)),
                  _tokens_spec(), const((ROW_TILE, D_MODEL)),
                  pl.BlockSpec((ROW_TILE, D_MODEL), lambda i: (i, 0)), const((1, D_MODEL))] + [HBM] * (n + 1),
        out_specs=[_tokens_spec(), const((ROW_TILE, D_MODEL)), const((1, D_MODEL))] + [HBM] * (n + 1),
        out_shape=[jax.ShapeDtypeStruct(x2.shape, F32), jax.ShapeDtypeStruct((ROW_TILE, D_MODEL), F32),
                   jax.ShapeDtypeStruct((1, D_MODEL), F32)]
        + [jax.ShapeDtypeStruct(h.shape, h.dtype) for h in hs] + [jax.ShapeDtypeStruct((N_CHIPS,) + sc.shape, sc.dtype)],
        scratch_shapes=[pltpu.SemaphoreType.DMA((3 * (n + 1),)), pltpu.SemaphoreType.DMA((3 * (n + 1),)),
                        pltpu.SemaphoreType.DMA((n + 1,))],
        compiler_params=_params(("arbitrary",), vmem=56 * 1024 * 1024),
    )(dproj, df, wt_e, wt_qkv, wt_f, x2, metapad, dh1, g1, *_in_hbm(*hs, sc))
    return res[:3], res[3:3 + n], res[3 + n]


def _adamw(w, g, m, v, name):
    rows, cols = w.shape
    if rows % 8 == 0:
        tr, tc = _row_tile8(rows), cols
    else:
        tr, tc = rows, (2 * LANES if cols % (2 * LANES) == 0 and rows > 8 else cols)

    def body(w_ref, g_ref, m_ref, v_ref, d_ref, mo_ref, vo_ref):
        g_ = g_ref[...]
        m_new = ADAM_B1 * m_ref[...] + (1.0 - ADAM_B1) * g_
        v_new = ADAM_B2 * v_ref[...] + (1.0 - ADAM_B2) * (g_ * g_)
        m_hat = m_new / (1.0 - ADAM_B1 ** ADAM_STEP)
        v_hat = v_new / (1.0 - ADAM_B2 ** ADAM_STEP)
        d_ref[...] = -ADAM_LR * (m_hat / (jnp.sqrt(v_hat) + ADAM_EPS) + ADAM_WD * w_ref[...])
        mo_ref[...] = m_new
        vo_ref[...] = v_new

    blk = pl.BlockSpec((tr, tc), lambda i, j: (i, j))
    return pl.pallas_call(
        body, name=name, grid=(rows // tr, cols // tc),
        in_specs=[blk] * 4, out_specs=[blk] * 3,
        out_shape=[jax.ShapeDtypeStruct(w.shape, F32)] * 3,
        compiler_params=_params(("parallel", "parallel")),
    )(w, g, m, v)


def _row_tile8(rows):
    best = rows
    for t in range(8, 257, 8):
        if rows % t == 0:
            best = t
    return best


def kernel(x, meta_tokens, norm_g, w_in, b_forget, pool_w, pool_scale, w_up_pool, w_up_attn, w_out, final_norm_g, loss_target, m_meta_tokens, m_norm_g, m_w_in, m_b_forget, m_pool_w, m_pool_scale, m_w_up_pool, m_w_up_attn, m_w_out, m_final_norm_g, v_meta_tokens, v_norm_g, v_w_in, v_b_forget, v_pool_w, v_pool_scale, v_w_up_pool, v_w_up_attn, v_w_out, v_final_norm_g):
    seq = x.shape[1]
    assert seq % ROW_TILE == 0 and x.shape[0] == 1
    lp = seq + ROW_TILE
    nb = lp // ROW_TILE
    lk = -(-lp // KV_TILE_BWD) * KV_TILE_BWD
    core = jnp.reshape(lax.axis_index("c"), (1,)).astype(jnp.int32)
    x2 = x[0]
    target = loss_target[0]
    sh_d = D_MODEL // N_CHIPS

    w_in_t = jnp.transpose(w_in[0])
    gf = final_norm_g.reshape(1, D_MODEL)
    bfg = jnp.pad(b_forget, ((0, 0), (0, F_COLS - N_HEADS)))
    pw_b = pool_w[0].astype(BF16)
    hn, (wg_in, meta_g) = _rmsnorm_fwd(x2, norm_g, lk, [w_in_t.astype(BF16), meta_tokens], [1, 0])
    def w_rows(a, b):
        per = wg_in.shape[1]
        parts = [wg_in[j, max(a - j * per, 0):min(b - j * per, per)] for j in range(N_CHIPS) if a < (j + 1) * per and b > j * per]
        return parts[0] if len(parts) == 1 else jnp.concatenate(parts, axis=0)

    n_in = N_CHIPS * wg_in.shape[1]
    wt_e = jnp.concatenate([w_rows(REF_U, REF_Q), w_rows(REF_ZA, REF_F), w_rows(REF_GP, n_in)], axis=0)
    wt_qkv = w_rows(REF_Q, REF_ZA)
    wt_f = jnp.pad(w_rows(REF_F, REF_GP), ((0, F_COLS - N_HEADS), (0, 0)))
    meta_full = jnp.transpose(meta_g, (1, 0, 2)).reshape(N_META, D_MODEL)
    metapad = jnp.pad(meta_full, ((PAD_ROWS, 0), (0, 0)))

    tm = _row_tile(lp, 1100)
    e, (wg_up_p, wg_up_a, wg_out) = _mm_nt(
        hn, wt_e, BF16, "in_proj_gates", lp, E_COLS, tm, 512,
        gather=([w_up_pool[0].astype(BF16), w_up_attn[0].astype(BF16), w_out[0].astype(BF16)], [0, 0, 0]))
    wup_p = jnp.transpose(wg_up_p, (1, 0, 2)).reshape(POOL_WIDTH, D_MODEL)
    wup_a = jnp.transpose(wg_up_a, (1, 0, 2)).reshape(ATTN_WIDTH, D_MODEL)
    wout = wg_out.reshape(D_MODEL, D_MODEL)
    qkv = _mm_nt(hn, wt_qkv, BF16, "in_proj_qkv", lk, QKV_COLS, _row_tile(lk, 1300), 512, scale_first=HEAD_DIM ** -0.5)
    f = _mm_nt(hn, wt_f, F32, "in_proj_forget", lp, F_COLS, tm, F_COLS)
    c = _gates_fwd(f, bfg)
    c_t = jnp.transpose(c[:, :N_HEADS])
    bias = c_t[None, :, :] - jnp.transpose(c_t[:, ::ROW_TILE])[:, :, None]
    bias = jnp.where(jnp.arange(lp) < PAD_ROWS, -NEG, bias)
    bias = jnp.pad(bias, ((0, 0), (0, 0), (0, lk - lp)))
    by_kv = lambda t: jnp.transpose(bias.reshape(nb, N_HEADS // 2, 2, lk // t, t), (0, 1, 3, 2, 4))
    bias, bias_bwd = by_kv(KV_TILE), by_kv(KV_TILE_BWD)
    y_pool = _pool_fwd(e, pw_b, pool_scale)
    o, lse = _attn_fwd(qkv, bias, nb)
    merged, y_attn = _merge_fwd(y_pool, o, e, wup_p, wup_a)
    dh1, loss_part, dgf = _head_fwd_bwd(merged, wout, x2, metapad, gf, target)

    dap, daa, dgate, dza, do, delta, dy_pool = _merge_bwd(dh1, wout, y_pool, y_attn, wup_p, wup_a, e, o)
    dpc, dzp, dscale, dpw = _pool_bwd_local(e, dy_pool, pw_b, pool_scale)
    du = _pool_bwd_window(dpc)
    dq, dk, dv, dc4, dcq = _attn_bwd(qkv, do, lse, delta, bias_bwd, nb)
    dc = jnp.transpose(dc4, (1, 3, 0, 2)).reshape(lk, N_HEADS)[:lp]
    df, db = _gates_bwd(jnp.pad(dc, ((0, 0), (0, F_COLS - N_HEADS))), dcq, f, bfg)
    dproj = jnp.concatenate([du, dzp, dza, dgate, (dq * HEAD_DIM ** -0.5).astype(BF16), dk[:lp], dv[:lp]], axis=1)
    tk = _row_tile(lp, 1100)
    dw_out = _mm_tn(merged, dh1, "grad_w_out", lp, 512, D_MODEL, tk)
    dw_up_p = _mm_tn(y_pool, dap, "grad_w_up_pool", lp, 512, D_MODEL, tk, chunks=N_CHIPS)
    dw_up_a = _mm_tn(y_attn, daa, "grad_w_up_attn", lp, 512, D_MODEL, tk, chunks=N_CHIPS)
    dwt_f = _mm_tn(df, hn, "grad_w_in_forget", lp, F_COLS, D_MODEL, tk)

    def pad8(a):
        return jnp.pad(a, ((0, (-a.shape[0]) % 8), (0, 0)))

    small_parts = [pad8(a) for a in (dgf.reshape(-1, LANES), dscale.reshape(-1, LANES), db, dpw.reshape(-1, LANES),
                                     loss_part)]
    small = jnp.concatenate(small_parts, axis=0)
    soffs = [0]
    for p in small_parts:
        soffs.append(soffs[-1] + p.shape[0])

    gs_rows = [dw_up_p, dw_up_a, dw_out.reshape(N_CHIPS, sh_d, D_MODEL)]
    half = MAIN_COLS // 2
    dwt_a, (*rb_rows, rb_f, sr) = _mm_tn(dproj, hn, "grad_w_in_a", lp, half // 2, D_MODEL, tk, m=half, m_off=0,
                                         swap=(gs_rows + [dwt_f, small], [1, 1, 1, 1, None]))
    dwt_b, (rb_a,) = _mm_tn(dproj, hn, "grad_w_in_b", lp, half // 2, D_MODEL, tk, m=half, m_off=2, swap=([dwt_a], [1]))
    (rb_b,) = _sibling_exchange([dwt_b], [1])
    *hs_rows, sc = _add_sibling_half(gs_rows, rb_rows, [1, 1, 1], small, sr, core)
    h_a, h_b, h_f = _add_halves_2d([dwt_a, dwt_b, dwt_f], [rb_a, rb_b, rb_f], core)
    h_in = jnp.concatenate([h_a[E_U:E_ZA], h_b[D_MODEL:], h_a[E_ZA:E_GP], h_f[:N_HEADS], h_a[E_GP:], h_b[:D_MODEL]], axis=0)
    hs = [h_in.reshape(N_CHIPS, -1, D_MODEL // 2)] + hs_rows
    grad_axes = [2, 1, 1, 1]
    (grad_x, g_block0, dg1), qs, sq = _input_bwd(dproj, df, wt_e, wt_qkv, wt_f, x2, metapad, dh1, norm_g, hs, sc)
    g_w_in_t, g_w_up_p, g_w_up_a, g_w_out, st = _reduce_allgather(qs, grad_axes, sq)
    late = _small_allreduce(jnp.concatenate([pad8(dg1.reshape(-1, LANES)), g_block0[PAD_ROWS:].reshape(-1, LANES)], axis=0))
    n_norm = D_MODEL // LANES
    g_norm = late[:n_norm].reshape(1, D_MODEL)
    chip = 2 * lax.axis_index("x") + lax.axis_index("y")
    g_meta = lax.dynamic_slice_in_dim(late[n_norm:].reshape(N_META, D_MODEL), chip * sh_d, sh_d, axis=1)

    spiece = lambda k, rows: st[soffs[k]:soffs[k] + rows]
    g_final = spiece(0, D_MODEL // LANES).reshape(1, D_MODEL)
    g_scale = spiece(1, POOL_WIDTH // LANES).reshape(1, POOL_WIDTH)
    g_bf = spiece(2, 1)
    g_pw = spiece(3, POOL_WIDTH)
    loss = st[soffs[4], 0]

    def pad_lanes(a):
        return jnp.pad(a, ((0, 0), (0, F_COLS - N_HEADS)))

    upd = [
        ("meta_tokens", meta_tokens, g_meta, m_meta_tokens, v_meta_tokens),
        ("norm_g", norm_g, g_norm, m_norm_g, v_norm_g),
        ("w_in", w_in_t, g_w_in_t, jnp.transpose(m_w_in[0]), jnp.transpose(v_w_in[0])),
        ("b_forget", pad_lanes(b_forget), g_bf, pad_lanes(m_b_forget), pad_lanes(v_b_forget)),
        ("pool_w", pool_w.reshape(-1, LANES), g_pw, m_pool_w.reshape(-1, LANES), v_pool_w.reshape(-1, LANES)),
        ("pool_scale", pool_scale, g_scale, m_pool_scale, v_pool_scale),
        ("w_up_pool", w_up_pool[0], g_w_up_p, m_w_up_pool[0], v_w_up_pool[0]),
        ("w_up_attn", w_up_attn[0], g_w_up_a, m_w_up_attn[0], v_w_up_attn[0]),
        ("w_out", w_out[0], g_w_out, m_w_out[0], v_w_out[0]),
        ("final_norm_g", gf, g_final, m_final_norm_g.reshape(1, D_MODEL), v_final_norm_g.reshape(1, D_MODEL)),
    ]
    shapes = [meta_tokens.shape, norm_g.shape, w_in.shape, b_forget.shape, pool_w.shape, pool_scale.shape,
              w_up_pool.shape, w_up_attn.shape, w_out.shape, final_norm_g.shape]
    grads, deltas, new_ms, new_vs = [], [], [], []
    for (name, w_, g_, m_in, v_in), shp in zip(upd, shapes):
        d_, mn_, vn_ = _adamw(w_, g_, m_in, v_in, "adamw_" + name)
        res = (g_, d_, mn_, vn_)
        if name == "b_forget":
            res = tuple(a[:, :N_HEADS] for a in res)
        if name == "w_in":
            res = tuple(jnp.transpose(a) for a in res)
        for lst, a in zip((grads, deltas, new_ms, new_vs), res):
            lst.append(a.reshape(shp))

    return (loss, grad_x.reshape(x.shape), *grads, *deltas, *new_ms, *new_vs)
```

```python
import jax
import jax.numpy as jnp
from jax import lax
from jax.experimental import pallas as pl
from jax.experimental.pallas import tpu as pltpu

F32 = jnp.float32
BF16 = jnp.bfloat16
MESH = pl.DeviceIdType.MESH
HIGHEST = lax.Precision.HIGHEST
HBM = pl.BlockSpec(memory_space=pltpu.HBM)

D_MODEL = 1024
N_META = 16
POOL_WIDTH = 512
POOL_GROUP = 128
POOL_WINDOWS = (2, 4, 8, 16)
N_HEADS = 8
HEAD_DIM = 64
ATTN_WIDTH = 512
RMS_EPS = 1e-6
N_CHIPS = 4

ADAM_LR = 0.001
ADAM_B1 = 0.9
ADAM_B2 = 0.999
ADAM_EPS = 1e-08
ADAM_WD = 0.01
ADAM_STEP = 10

LANES = 128
ROW_TILE = 256
KV_TILE = 512
KV_TILE_BWD = 1024
PAD_ROWS = ROW_TILE - N_META
NEG = -1e30

E_U, E_ZP, E_ZA, E_GP, E_GA, E_COLS = 0, 512, 1024, 1536, 2560, 3584
QKV_COLS = 3 * ATTN_WIDTH
MAIN_COLS = E_COLS + QKV_COLS
F_COLS = LANES
REF_U, REF_Q, REF_ZA, REF_F, REF_GP = 0, 1024, 2560, 3072, 3080
VMEM_LIMIT = 48 * 1024 * 1024


def _params(sem=None, vmem=VMEM_LIMIT):
    return pltpu.CompilerParams(dimension_semantics=sem, vmem_limit_bytes=vmem)


def _in_hbm(*arrays):
    return [pltpu.with_memory_space_constraint(a, pltpu.HBM) for a in arrays]


def _row_tile(n, target):
    best = 16
    for t in range(16, target + 1, 16):
        if n % t == 0:
            best = t
    return best


def _sigmoid(x):
    return 1.0 / (1.0 + jnp.exp(-x))


def _half(ref, axis, which):
    n = ref.shape[axis] // 2
    idx = [slice(None)] * len(ref.shape)
    idx[axis] = pl.ds(pl.multiple_of(which * n, n), n)
    return ref.at[tuple(idx)]


def _half_shape(shape, axis):
    s = list(shape)
    s[axis] //= 2
    return tuple(s)


def _gather_start(ins, outs, axes, send, recv, fsend, frecv, local):
    x, y, c = lax.axis_index("x"), lax.axis_index("y"), lax.axis_index("c")
    me = 2 * x + y
    for t in range(len(ins)):
        pltpu.make_async_copy(ins[t], outs[t].at[me], local.at[t]).start()
        for k, chip in enumerate([(1 - x, y), (x, 1 - y), (1 - x, 1 - y)]):
            pltpu.make_async_remote_copy(
                src_ref=_half(ins[t], axes[t], c), dst_ref=_half(outs[t].at[me], axes[t], c),
                send_sem=send.at[3 * t + k], recv_sem=recv.at[3 * t + k], device_id=(*chip, c), device_id_type=MESH).start()


def _gather_finish(ins, outs, axes, send, recv, fsend, frecv, local):
    x, y, c = lax.axis_index("x"), lax.axis_index("y"), lax.axis_index("c")
    me = 2 * x + y
    sibling = (x, y, 1 - c)
    chips = [(1 - x, y), (x, 1 - y), (1 - x, 1 - y)]
    n = len(ins)

    def over_ici(t, k, chip, dst_slot):
        return pltpu.make_async_remote_copy(
            src_ref=_half(ins[t], axes[t], c), dst_ref=_half(outs[t].at[dst_slot], axes[t], c),
            send_sem=send.at[3 * t + k], recv_sem=recv.at[3 * t + k], device_id=(*chip, c), device_id_type=MESH)

    def to_sibling(t, k, slot, which):
        return pltpu.make_async_remote_copy(
            src_ref=_half(outs[t].at[slot], axes[t], which), dst_ref=_half(outs[t].at[slot], axes[t], which),
            send_sem=fsend.at[3 * t + k], recv_sem=frecv.at[3 * t + k], device_id=sibling, device_id_type=MESH)

    forwards = []
    for t in range(n):
        for k, (px, py) in enumerate(chips):
            over_ici(t, k, (px, py), 2 * px + py).wait_recv()
            fw = to_sibling(t, k, 2 * px + py, c)
            fw.start()
            forwards.append(fw)
    for t in range(n):
        for k, (px, py) in enumerate(chips):
            to_sibling(t, k, 2 * px + py, 1 - c).wait_recv()
    for t in range(n):
        for k, chip in enumerate(chips):
            over_ici(t, k, chip, me).wait_send()
    for fw in forwards:
        fw.wait_send()
    for t in range(n):
        pltpu.make_async_copy(ins[t], outs[t].at[me], local.at[t]).wait()


def _gather_sems(n):
    return [pltpu.SemaphoreType.DMA((3 * n,)), pltpu.SemaphoreType.DMA((3 * n,)), pltpu.SemaphoreType.DMA((3 * n,)),
            pltpu.SemaphoreType.DMA((3 * n,)), pltpu.SemaphoreType.DMA((n,))]


def _gathered_shapes(shards):
    return [jax.ShapeDtypeStruct((N_CHIPS,) + s.shape, s.dtype) for s in shards]


def _swap_copies(srcs, dsts, axes, send, recv):
    x, y, c = lax.axis_index("x"), lax.axis_index("y"), lax.axis_index("c")
    return [pltpu.make_async_remote_copy(
        src_ref=srcs[t] if axes[t] is None else _half(srcs[t], axes[t], 1 - c), dst_ref=dsts[t],
        send_sem=send.at[t], recv_sem=recv.at[t], device_id=(x, y, 1 - c), device_id_type=MESH) for t in range(len(srcs))]


def _swap_shapes(srcs, axes):
    return [jax.ShapeDtypeStruct(g.shape if a is None else _half_shape(g.shape, a), g.dtype) for g, a in zip(srcs, axes)]


def _sibling_exchange(gs, axes):
    n = len(gs)

    def body(*refs):
        cps = _swap_copies(refs[:n], refs[n:2 * n], axes, *refs[2 * n:])
        for cp in cps:
            cp.start()
        for cp in cps:
            cp.wait()

    return pl.pallas_call(
        body, name="grad_sibling_exchange",
        in_specs=[HBM] * n, out_specs=[HBM] * n, out_shape=_swap_shapes(gs, axes),
        scratch_shapes=[pltpu.SemaphoreType.DMA((n,)), pltpu.SemaphoreType.DMA((n,))],
    )(*_in_hbm(*gs))


def _add_halves_2d(parts, rbs, core):
    n = len(parts)
    rows, w = rbs[0].shape
    tr = 512
    last_rows = rbs[-1].shape[0]

    def body(core_ref, *refs):
        p_refs, r_refs, h_refs = refs[:n], refs[n:2 * n], refs[2 * n:]
        for t in range(n - 1):
            h_refs[t][...] = (p_refs[t][...] + r_refs[t][...]).astype(BF16)

        @pl.when(pl.program_id(0) == 0)
        def _():
            h_refs[n - 1][...] = (p_refs[n - 1][...] + r_refs[n - 1][...]).astype(BF16)

    mine = lambda r: pl.BlockSpec((r, w), lambda i, cr: (i, cr[0]))
    tile = lambda r: pl.BlockSpec((r, w), lambda i, cr: (i, 0))
    mine_small = pl.BlockSpec((last_rows, w), lambda i, cr: (0, cr[0]))
    tile_small = pl.BlockSpec((last_rows, w), lambda i, cr: (0, 0))
    return pl.pallas_call(
        body, name="grad_add_sibling_w_in",
        grid_spec=pltpu.PrefetchScalarGridSpec(
            num_scalar_prefetch=1, grid=(rows // tr,),
            in_specs=[mine(tr)] * (n - 1) + [mine_small] + [tile(tr)] * (n - 1) + [tile_small],
            out_specs=[tile(tr)] * (n - 1) + [tile_small]),
        out_shape=[jax.ShapeDtypeStruct(r.shape, BF16) for r in rbs],
        compiler_params=_params(("arbitrary",)),
    )(core, *parts, *rbs)


def _add_sibling_half(gs, rbs, axes, small, sr, core):
    n = len(gs)

    def body(core_ref, *refs):
        g_refs, rb_refs = refs[:n], refs[n:2 * n]
        s_ref, sr_ref = refs[2 * n], refs[2 * n + 1]
        h_refs, sc_ref = refs[2 * n + 2:3 * n + 2], refs[3 * n + 2]
        for t in range(n):
            h_refs[t][...] = (g_refs[t][...] + rb_refs[t][...]).astype(BF16)
        sc_ref[...] = s_ref[...] + sr_ref[...]

    def mine(rb, axis):
        blk = (None,) + rb.shape[1:]
        if axis == 2:
            return pl.BlockSpec(blk, lambda j, cr: (j, 0, cr[0]))
        return pl.BlockSpec(blk, lambda j, cr: (j, cr[0], 0))

    chunk = lambda rb: pl.BlockSpec((None,) + rb.shape[1:], lambda j, cr: (j, 0, 0))
    whole = pl.BlockSpec(small.shape, lambda j, cr: (0, 0))
    return pl.pallas_call(
        body, name="grad_add_sibling",
        grid_spec=pltpu.PrefetchScalarGridSpec(
            num_scalar_prefetch=1, grid=(N_CHIPS,),
            in_specs=[mine(rb, a) for rb, a in zip(rbs, axes)] + [chunk(rb) for rb in rbs] + [whole, whole],
            out_specs=[chunk(rb) for rb in rbs] + [whole]),
        out_shape=[jax.ShapeDtypeStruct(rb.shape, BF16) for rb in rbs] + [jax.ShapeDtypeStruct(small.shape, F32)],
        compiler_params=_params(("arbitrary",)),
    )(core, *gs, *rbs, small, sr)


def _chip_scatter_plan(h_refs, s_ref, q_refs, sq_ref, send, recv, local):
    n = len(h_refs)
    x, y, c = lax.axis_index("x"), lax.axis_index("y"), lax.axis_index("c")
    me = 2 * x + y
    chips = [(1 - x, y), (x, 1 - y), (1 - x, 1 - y)]

    def copy(t, k, chip, src_slot, dst_slot):
        src = h_refs[t].at[src_slot] if t < n else s_ref
        dst = (q_refs[t] if t < n else sq_ref).at[dst_slot]
        return pltpu.make_async_remote_copy(src_ref=src, dst_ref=dst, send_sem=send.at[3 * t + k],
                                            recv_sem=recv.at[3 * t + k], device_id=(*chip, c), device_id_type=MESH)

    own = [pltpu.make_async_copy(h_refs[t].at[me], q_refs[t].at[me], local.at[t]) for t in range(n)]
    own.append(pltpu.make_async_copy(s_ref, sq_ref.at[me], local.at[n]))
    sends = [copy(t, k, (px, py), 2 * px + py, me) for t in range(n + 1) for k, (px, py) in enumerate(chips)]
    recvs = [copy(t, k, (px, py), me, 2 * px + py) for t in range(n + 1) for k, (px, py) in enumerate(chips)]
    return own, sends, recvs


def _small_allreduce(buf):
    def body(b_ref, o_ref, sib_buf, chip_buf, send, recv):
        x, y, c = lax.axis_index("x"), lax.axis_index("y"), lax.axis_index("c")
        me = 2 * x + y
        chips = [(1 - x, y), (x, 1 - y), (1 - x, 1 - y)]
        swap = pltpu.make_async_remote_copy(src_ref=b_ref, dst_ref=sib_buf, send_sem=send.at[0], recv_sem=recv.at[0],
                                            device_id=(x, y, 1 - c), device_id_type=MESH)
        swap.start()
        swap.wait()
        chip_buf[me] = b_ref[...] + sib_buf[...]

        def copy(k, chip, slot):
            return pltpu.make_async_remote_copy(src_ref=chip_buf.at[slot], dst_ref=chip_buf.at[slot], send_sem=send.at[1 + k],
                                                recv_sem=recv.at[1 + k], device_id=(*chip, c), device_id_type=MESH)

        sends = [copy(k, chip, me) for k, chip in enumerate(chips)]
        for cp in sends:
            cp.start()
        for k, (px, py) in enumerate(chips):
            copy(k, (px, py), 2 * px + py).wait_recv()
        for cp in sends:
            cp.wait_send()
        o_ref[...] = ((chip_buf[0] + chip_buf[1]) + chip_buf[2]) + chip_buf[3]

    vmem = pl.BlockSpec(memory_space=pltpu.VMEM)
    return pl.pallas_call(
        body, name="small_allreduce", in_specs=[vmem], out_specs=vmem,
        out_shape=jax.ShapeDtypeStruct(buf.shape, F32),
        scratch_shapes=[pltpu.VMEM(buf.shape, F32), pltpu.VMEM((N_CHIPS,) + buf.shape, F32),
                        pltpu.SemaphoreType.DMA((4,)), pltpu.SemaphoreType.DMA((4,))],
        compiler_params=_params(),
    )(buf)


def _reduce_allgather(qs, axes, sq):
    n = len(qs)
    shard_shapes = [tuple(d * 2 if i == a - 1 else d for i, d in enumerate(q.shape[1:])) for q, a in zip(qs, axes)]

    def body(*refs):
        q_refs, sq_ref = refs[:n], refs[n]
        o_refs, st_ref = refs[n + 1:2 * n + 1], refs[2 * n + 1]
        send, recv = refs[2 * n + 2:]
        x, y, c = lax.axis_index("x"), lax.axis_index("y"), lax.axis_index("c")

        def swap(t, which):
            return pltpu.make_async_remote_copy(
                src_ref=_half(o_refs[t], axes[t] - 1, which), dst_ref=_half(o_refs[t], axes[t] - 1, which),
                send_sem=send.at[t], recv_sem=recv.at[t], device_id=(x, y, 1 - c), device_id_type=MESH)

        sent = []
        for t in range(n):
            q = q_refs[t]
            total = ((q[0].astype(F32) + q[1].astype(F32)) + q[2].astype(F32)) + q[3].astype(F32)
            _half(o_refs[t], axes[t] - 1, c)[...] = total
            cp = swap(t, c)
            cp.start()
            sent.append(cp)
        st_ref[...] = ((sq_ref[0] + sq_ref[1]) + sq_ref[2]) + sq_ref[3]
        for t in range(n):
            swap(t, 1 - c).wait_recv()
        for cp in sent:
            cp.wait_send()

    vmem = pl.BlockSpec(memory_space=pltpu.VMEM)
    return pl.pallas_call(
        body, name="grad_reduce_allgather",
        in_specs=[vmem] * (n + 1), out_specs=[vmem] * (n + 1),
        out_shape=[jax.ShapeDtypeStruct(s, F32) for s in shard_shapes] + [jax.ShapeDtypeStruct(sq.shape[1:], F32)],
        scratch_shapes=[pltpu.SemaphoreType.DMA((n,)), pltpu.SemaphoreType.DMA((n,))],
        compiler_params=_params(),
    )(*qs, sq)


def _dot_nt(a, b):
    return lax.dot_general(a, b, (((1,), (1,)), ((), ())), preferred_element_type=F32)


def _dot_tn(a, b):
    return lax.dot_general(a, b, (((0,), (0,)), ((), ())), preferred_element_type=F32)


def _mm_nt(a, bt, out_dtype, name, m, n, tm, tn, row_block=0, scale_first=None, gather=None):
    k = a.shape[1]
    shards, axes = gather if gather is not None else ([], [])
    ng = len(shards)
    grid = (m // tm, n // tn)

    def body(a_ref, b_ref, *rest):
        ins, o_ref, outs, sems = rest[:ng], rest[ng], rest[ng + 1:2 * ng + 1], rest[2 * ng + 1:]
        first = (pl.program_id(0) == 0) & (pl.program_id(1) == 0)
        last = (pl.program_id(0) == grid[0] - 1) & (pl.program_id(1) == grid[1] - 1)
        if ng:
            @pl.when(first)
            def _():
                _gather_start(ins, outs, axes, *sems)

        r = _dot_nt(a_ref[...], b_ref[...])
        if scale_first is not None:
            r = r * jnp.where(pl.program_id(1) == 0, scale_first, 1.0)
        o_ref[...] = r.astype(out_dtype)

        if ng:
            @pl.when(last)
            def _():
                _gather_finish(ins, outs, axes, *sems)

    res = pl.pallas_call(
        body, name=name, grid=grid,
        in_specs=[pl.BlockSpec((tm, k), lambda i, j: (i, 0)), pl.BlockSpec((tn, k), lambda i, j: (row_block + j, 0))]
        + [HBM] * ng,
        out_specs=[pl.BlockSpec((tm, tn), lambda i, j: (i, j))] + [HBM] * ng,
        out_shape=[jax.ShapeDtypeStruct((m, n), out_dtype)] + _gathered_shapes(shards),
        scratch_shapes=_gather_sems(ng) if ng else [],
        compiler_params=_params(("arbitrary", "arbitrary") if ng else ("parallel", "parallel")),
    )(a, bt, *_in_hbm(*shards))
    return (res[0], res[1:]) if ng else res[0]


def _mm_tn(a, b, name, k, tm, tn, tk, chunks=1, m=None, m_off=0, swap=None):
    m = a.shape[1] if m is None else m
    n = b.shape[1]
    cw = n // chunks
    srcs, axes = swap if swap is not None else ([], [])
    ns = len(srcs)
    grid = (m // tm, n // tn, k // tk)

    def body(a_ref, b_ref, *rest):
        s_refs, o_ref, d_refs, sems = rest[:ns], rest[ns], rest[ns + 1:2 * ns + 1], rest[2 * ns + 1:]
        ids = [pl.program_id(d) for d in range(3)]
        if ns:
            @pl.when((ids[0] == 0) & (ids[1] == 0) & (ids[2] == 0))
            def _():
                for cp in _swap_copies(s_refs, d_refs, axes, *sems):
                    cp.start()

        @pl.when(ids[2] == 0)
        def _():
            o_ref[...] = jnp.zeros_like(o_ref)
        r = _dot_tn(a_ref[...].astype(BF16), b_ref[...].astype(BF16))
        if chunks > 1:
            for c in range(chunks):
                o_ref[c] += r[:, c * cw:(c + 1) * cw]
        else:
            o_ref[...] += r

        if ns:
            @pl.when((ids[0] == grid[0] - 1) & (ids[1] == grid[1] - 1) & (ids[2] == grid[2] - 1))
            def _():
                for cp in _swap_copies(s_refs, d_refs, axes, *sems):
                    cp.wait()

    if chunks > 1:
        assert tn == n
        out_spec = pl.BlockSpec((chunks, tm, cw), lambda i, j, kk: (0, i, 0))
        out_shape = jax.ShapeDtypeStruct((chunks, m, cw), F32)
    else:
        out_spec = pl.BlockSpec((tm, tn), lambda i, j, kk: (i, j))
        out_shape = jax.ShapeDtypeStruct((m, n), F32)
    res = pl.pallas_call(
        body, name=name, grid=grid,
        in_specs=[pl.BlockSpec((tk, tm), lambda i, j, kk: (kk, m_off + i)), pl.BlockSpec((tk, tn), lambda i, j, kk: (kk, j))]
        + [HBM] * ns,
        out_specs=[out_spec] + [HBM] * ns, out_shape=[out_shape] + _swap_shapes(srcs, axes),
        scratch_shapes=[pltpu.SemaphoreType.DMA((ns,)), pltpu.SemaphoreType.DMA((ns,))] if ns else [],
        compiler_params=_params(("arbitrary",) * 3 if ns else ("parallel", "parallel", "arbitrary")),
    )(a, b, *_in_hbm(*srcs))
    return (res[0], res[1:]) if ns else res[0]


def _tokens_spec():
    return pl.BlockSpec((ROW_TILE, D_MODEL), lambda i: (jnp.maximum(i - 1, 0), 0))


def _rmsnorm_fwd(x2, g1, lk, shards, axes):
    nb = x2.shape[0] // ROW_TILE + 1
    nblk = lk // ROW_TILE
    ng = len(shards)
    meta_cols = shards[-1].shape[1]

    def body(x_ref, g_ref, *rest):
        ins, hn_ref, outs = rest[:ng], rest[ng], rest[ng + 1:2 * ng + 1]
        sems, meta_buf, meta_sem = rest[2 * ng + 1:2 * ng + 6], rest[2 * ng + 6], rest[2 * ng + 7]
        s = pl.program_id(0)
        blk = (s + 1) % nblk

        @pl.when(s == 0)
        def _():
            _gather_start(ins, outs, axes, *sems)

        def normed(h):
            r = lax.rsqrt(jnp.mean(h * h, axis=-1, keepdims=True) + RMS_EPS)
            return ((h * r) * g_ref[...]).astype(BF16)

        @pl.when(s < nblk - 1)
        def _():
            hn_ref[...] = normed(jnp.where(blk >= nb, 0.0, x_ref[...]))

        @pl.when(s == nblk - 1)
        def _():
            _gather_finish(ins, outs, axes, *sems)
            fetch = pltpu.make_async_copy(outs[-1], meta_buf, meta_sem.at[0])
            fetch.start()
            fetch.wait()
            meta = jnp.concatenate([meta_buf[j] for j in range(N_CHIPS)], axis=1)
            hn_ref[...] = normed(jnp.concatenate([jnp.zeros((PAD_ROWS, D_MODEL), F32), meta], axis=0))

    res = pl.pallas_call(
        body, name="rmsnorm_fwd", grid=(nblk,),
        in_specs=[pl.BlockSpec((ROW_TILE, D_MODEL), lambda s: (jnp.clip((s + 1) % nblk - 1, 0, nb - 2), 0)),
                  pl.BlockSpec((1, D_MODEL), lambda s: (0, 0))] + [HBM] * ng,
        out_specs=[pl.BlockSpec((ROW_TILE, D_MODEL), lambda s: ((s + 1) % nblk, 0))] + [HBM] * ng,
        out_shape=[jax.ShapeDtypeStruct((lk, D_MODEL), BF16)] + _gathered_shapes(shards),
        scratch_shapes=_gather_sems(ng) + [pltpu.VMEM((N_CHIPS, N_META, meta_cols), F32), pltpu.SemaphoreType.DMA((1,))],
        compiler_params=_params(("arbitrary",)),
    )(x2, g1, *_in_hbm(*shards))
    return res[0], res[1:]


def _valid_gate_mask(i):
    row = i * ROW_TILE + lax.broadcasted_iota(jnp.int32, (ROW_TILE, F_COLS), 0)
    col = lax.broadcasted_iota(jnp.int32, (ROW_TILE, F_COLS), 1)
    return (row >= PAD_ROWS) & (col < N_HEADS)


def _gates_fwd(f, bfg):
    nb = f.shape[0] // ROW_TILE

    def body(f_ref, b_ref, c_ref, carry):
        i = pl.program_id(0)

        @pl.when(i == 0)
        def _():
            carry[...] = jnp.zeros_like(carry)

        logit = f_ref[...] + b_ref[...]
        lf = jnp.minimum(logit, 0.0) - jnp.log1p(jnp.exp(-jnp.abs(logit)))
        lf = jnp.where(_valid_gate_mask(i), lf, 0.0)
        r_i = lax.broadcasted_iota(jnp.int32, (ROW_TILE, ROW_TILE), 0)
        c_i = lax.broadcasted_iota(jnp.int32, (ROW_TILE, ROW_TILE), 1)
        tri = (c_i <= r_i).astype(F32)
        c_ref[...] = jnp.dot(tri, lf, precision=HIGHEST, preferred_element_type=F32) + carry[...]
        carry[...] = carry[...] + jnp.sum(lf, axis=0, keepdims=True)

    return pl.pallas_call(
        body, name="gates_fwd", grid=(nb,),
        in_specs=[pl.BlockSpec((ROW_TILE, F_COLS), lambda i: (i, 0)), pl.BlockSpec((1, F_COLS), lambda i: (0, 0))],
        out_specs=pl.BlockSpec((ROW_TILE, F_COLS), lambda i: (i, 0)),
        out_shape=jax.ShapeDtypeStruct(f.shape, F32),
        scratch_shapes=[pltpu.VMEM((1, F_COLS), F32)],
        compiler_params=_params(("arbitrary",)),
    )(f, bfg)


def _pool_counts(i):
    row = i * ROW_TILE + lax.broadcasted_iota(jnp.int32, (ROW_TILE, 1), 0)
    return jnp.maximum(row - PAD_ROWS, 0)


def _trailing_sums(xc, levels):
    acc = xc
    for lv in range(levels):
        acc = acc + pltpu.roll(acc, 1 << lv, 0)
    return acc


def _leading_sums(xc, levels):
    n = xc.shape[0]
    acc = xc
    for lv in range(levels):
        acc = acc + pltpu.roll(acc, n - (1 << lv), 0)
    return acc


def _pool_p(u_cur, u_prev, i):
    pos = _pool_counts(i)
    ps, invs = [], []
    for g, w in enumerate(POOL_WINDOWS):
        sl = slice(g * POOL_GROUP, (g + 1) * POOL_GROUP)
        cur = u_cur[:, sl]
        xc = jnp.concatenate([u_prev[:, sl], cur], axis=0)
        win = _trailing_sums(xc, g + 1)[ROW_TILE:, :]
        inv = 1.0 / jnp.minimum(pos + 1, w).astype(F32)
        ps.append(win * inv - cur)
        invs.append(inv)
    return ps, invs


def _pool_fwd(e, pw, scale):
    nb = e.shape[0] // ROW_TILE

    def body(uc_ref, up_ref, z_ref, pw_ref, sc_ref, y_ref):
        i = pl.program_id(0)
        u_cur = uc_ref[...].astype(F32)
        u_prev = jnp.where(i == 0, 0.0, up_ref[...].astype(F32))
        ps, _ = _pool_p(u_cur, u_prev, i)
        z = z_ref[...].astype(F32)
        gate = z * _sigmoid(z)
        for g in range(len(POOL_WINDOWS)):
            sl = slice(g * POOL_GROUP, (g + 1) * POOL_GROUP)
            yraw = jnp.dot(ps[g].astype(BF16), pw_ref[g], preferred_element_type=F32)
            y_ref[:, sl] = ((yraw * sc_ref[:, sl]) * gate[:, sl]).astype(BF16)

    blk = (ROW_TILE, POOL_WIDTH)
    return pl.pallas_call(
        body, name="pool_fwd", grid=(nb,),
        in_specs=[pl.BlockSpec(blk, lambda i: (i, 0)), pl.BlockSpec(blk, lambda i: (jnp.maximum(i - 1, 0), 0)),
                  pl.BlockSpec(blk, lambda i: (i, 1)),
                  pl.BlockSpec((len(POOL_WINDOWS), POOL_GROUP, POOL_GROUP), lambda i: (0, 0, 0)),
                  pl.BlockSpec((1, POOL_WIDTH), lambda i: (0, 0))],
        out_specs=pl.BlockSpec(blk, lambda i: (i, 0)),
        out_shape=jax.ShapeDtypeStruct((e.shape[0], POOL_WIDTH), BF16),
        compiler_params=_params(("parallel",)),
    )(e, e, e, pw, scale)


def _stack_heads(a):
    first = lax.broadcasted_iota(jnp.int32, a.shape, 1) < HEAD_DIM
    zero = jnp.zeros_like(a)
    return jnp.concatenate([jnp.where(first, a, zero), jnp.where(first, zero, a)], axis=0)


def _unstack_heads(a):
    rows = a.shape[0] // 2
    first = lax.broadcasted_iota(jnp.int32, (rows, LANES), 1) < HEAD_DIM
    return jnp.where(first, a[:rows], a[rows:])


def _causal(i, jj, stacked, kv_tile):
    r = lax.broadcasted_iota(jnp.int32, (stacked * ROW_TILE, kv_tile), 0)
    if stacked == 2:
        r = jnp.where(r >= ROW_TILE, r - ROW_TILE, r)
    kidx = jj * kv_tile + lax.broadcasted_iota(jnp.int32, (stacked * ROW_TILE, kv_tile), 1)
    return kidx <= i * ROW_TILE + r


def _stack_rows(b):
    n = b.shape[1]
    return jnp.concatenate([jnp.broadcast_to(b[0:1], (ROW_TILE, n)), jnp.broadcast_to(b[1:2], (ROW_TILE, n))], axis=0)


def _attn_fwd(qkv, bias, nb):
    lk = qkv.shape[0]
    lp = nb * ROW_TILE
    nkb = lk // KV_TILE
    n_pairs = N_HEADS // 2

    def body(q_ref, k_ref, v_ref, b_ref, o_ref, lse_ref):
        i = pl.program_id(1)
        qs = _stack_heads(q_ref[...])
        last = (i * ROW_TILE) // KV_TILE

        def block(jj, carry, masked):
            m, l, acc = carry
            rows = pl.ds(pl.multiple_of(jj * KV_TILE, KV_TILE), KV_TILE)
            s = _dot_nt(qs, k_ref[rows, :]) - _stack_rows(b_ref[0, 0, jj])
            if masked:
                s = jnp.where(_causal(i, jj, 2, KV_TILE), s, NEG)
            m_new = jnp.maximum(m, jnp.max(s, axis=1, keepdims=True))
            alpha = jnp.exp(m - m_new)
            p = jnp.exp(s - m_new)
            l = alpha * l + jnp.sum(p, axis=1, keepdims=True)
            acc = alpha * acc + jnp.dot(p.astype(BF16), v_ref[rows, :], preferred_element_type=F32)
            return m_new, l, acc

        init = (jnp.full((2 * ROW_TILE, 1), NEG, F32), jnp.zeros((2 * ROW_TILE, 1), F32),
                jnp.zeros((2 * ROW_TILE, LANES), F32))
        carry = lax.fori_loop(0, last, lambda jj, c: block(jj, c, False), init)
        m, l, acc = block(last, carry, True)
        o_ref[...] = _unstack_heads(acc / l)
        lse_ref[...] = _unstack_heads(jnp.broadcast_to(m + jnp.log(l), (2 * ROW_TILE, LANES)))

    return pl.pallas_call(
        body, name="attn_fwd", grid=(n_pairs, nb),
        in_specs=[pl.BlockSpec((ROW_TILE, LANES), lambda hp, i: (i, hp)),
                  pl.BlockSpec((lk, LANES), lambda hp, i: (0, n_pairs + hp)),
                  pl.BlockSpec((lk, LANES), lambda hp, i: (0, 2 * n_pairs + hp)),
                  pl.BlockSpec((1, 1, nkb, 2, KV_TILE), lambda hp, i: (i, hp, 0, 0, 0))],
        out_specs=[pl.BlockSpec((ROW_TILE, LANES), lambda hp, i: (i, hp)),
                   pl.BlockSpec((ROW_TILE, LANES), lambda hp, i: (i, hp))],
        out_shape=[jax.ShapeDtypeStruct((lp, ATTN_WIDTH), F32), jax.ShapeDtypeStruct((lp, ATTN_WIDTH), F32)],
        compiler_params=_params(("parallel", "parallel")),
    )(qkv, qkv, qkv, bias)


def _merge_fwd(y_pool, o, e, wup_p, wup_a):
    lp = o.shape[0]
    nb = lp // ROW_TILE

    def body(yp_ref, o_ref, e_ref, wp_ref, wa_ref, mg_ref, ya_ref):
        za = e_ref[:, E_ZA:E_GP].astype(F32)
        ya = (o_ref[...] * (za * _sigmoid(za))).astype(BF16)
        ya_ref[...] = ya
        a_pool = jnp.dot(yp_ref[...], wp_ref[...], preferred_element_type=F32)
        a_attn = jnp.dot(ya, wa_ref[...], preferred_element_type=F32)
        mg_ref[...] = (_sigmoid(e_ref[:, E_GP:E_GA].astype(F32)) * a_pool
                       + _sigmoid(e_ref[:, E_GA:E_COLS].astype(F32)) * a_attn).astype(BF16)

    return pl.pallas_call(
        body, name="merge_fwd", grid=(nb,),
        in_specs=[pl.BlockSpec((ROW_TILE, POOL_WIDTH), lambda i: (i, 0)),
                  pl.BlockSpec((ROW_TILE, ATTN_WIDTH), lambda i: (i, 0)),
                  pl.BlockSpec((ROW_TILE, E_COLS), lambda i: (i, 0)),
                  pl.BlockSpec((POOL_WIDTH, D_MODEL), lambda i: (0, 0)),
                  pl.BlockSpec((ATTN_WIDTH, D_MODEL), lambda i: (0, 0))],
        out_specs=[pl.BlockSpec((ROW_TILE, D_MODEL), lambda i: (i, 0)),
                   pl.BlockSpec((ROW_TILE, ATTN_WIDTH), lambda i: (i, 0))],
        out_shape=[jax.ShapeDtypeStruct((lp, D_MODEL), BF16), jax.ShapeDtypeStruct((lp, ATTN_WIDTH), BF16)],
        compiler_params=_params(("parallel",)),
    )(y_pool, o, e, wup_p, wup_a)


def _head_fwd_bwd(merged, w_out, x2, metapad, gf, target):
    lp = merged.shape[0]
    nb = lp // ROW_TILE

    def body(mg_ref, w_ref, x_ref, mp_ref, g_ref, t_ref, dh_ref, loss_ref, dg_ref):
        i = pl.program_id(0)

        @pl.when(i == 0)
        def _():
            loss_ref[...] = jnp.zeros_like(loss_ref)
            dg_ref[...] = jnp.zeros_like(dg_ref)

        h0 = jnp.where(i == 0, mp_ref[...], x_ref[...])
        h1 = h0 + jnp.dot(mg_ref[...], w_ref[...], preferred_element_type=F32)
        r = lax.rsqrt(jnp.mean(h1 * h1, axis=-1, keepdims=True) + RMS_EPS)
        xhat = h1 * r
        g = g_ref[...]
        err = jnp.where(i == 0, 0.0, xhat * g - t_ref[...])
        loss_ref[...] += 0.5 * jnp.sum(jnp.mean(err * err, axis=-1, keepdims=True))
        dy = err / D_MODEL
        dg_ref[...] += jnp.sum(dy * xhat, axis=0, keepdims=True)
        dxhat = dy * g
        dh_ref[...] = r * (dxhat - xhat * jnp.mean(dxhat * xhat, axis=-1, keepdims=True))

    return pl.pallas_call(
        body, name="head_fwd_bwd", grid=(nb,),
        in_specs=[pl.BlockSpec((ROW_TILE, D_MODEL), lambda i: (i, 0)),
                  pl.BlockSpec((D_MODEL, D_MODEL), lambda i: (0, 0)),
                  _tokens_spec(), pl.BlockSpec((ROW_TILE, D_MODEL), lambda i: (0, 0)),
                  pl.BlockSpec((1, D_MODEL), lambda i: (0, 0)), _tokens_spec()],
        out_specs=[pl.BlockSpec((ROW_TILE, D_MODEL), lambda i: (i, 0)),
                   pl.BlockSpec((1, LANES), lambda i: (0, 0)), pl.BlockSpec((1, D_MODEL), lambda i: (0, 0))],
        out_shape=[jax.ShapeDtypeStruct((lp, D_MODEL), F32), jax.ShapeDtypeStruct((1, LANES), F32),
                   jax.ShapeDtypeStruct((1, D_MODEL), F32)],
        compiler_params=_params(("arbitrary",)),
    )(merged, w_out, x2, metapad, gf, target)


def _per_head_rowsum(t):
    head = lax.broadcasted_iota(jnp.int32, t.shape, 1) // HEAD_DIM
    out = jnp.zeros_like(t)
    for h in range(N_HEADS):
        sel = head == h
        out = jnp.where(sel, jnp.sum(jnp.where(sel, t, 0.0), axis=1, keepdims=True), out)
    return out


def _merge_bwd(dh1, w_out, y_pool, y_attn, wup_p, wup_a, e, o):
    lp = o.shape[0]
    nb = lp // ROW_TILE

    def body(dh_ref, wo_ref, yp_ref, ya_ref, wp_ref, wa_ref, e_ref, o_ref,
             dap_ref, daa_ref, dg_ref, dza_ref, do_ref, delta_ref, dyp_ref):
        dmerged = _dot_nt(dh_ref[...].astype(BF16), wo_ref[...])
        a_pool = jnp.dot(yp_ref[...], wp_ref[...], preferred_element_type=F32)
        a_attn = jnp.dot(ya_ref[...], wa_ref[...], preferred_element_type=F32)
        sp = _sigmoid(e_ref[:, E_GP:E_GA].astype(F32))
        sa = _sigmoid(e_ref[:, E_GA:E_COLS].astype(F32))
        dap = (dmerged * sp).astype(BF16)
        daa = (dmerged * sa).astype(BF16)
        dap_ref[...] = dap
        daa_ref[...] = daa
        dg_ref[:, :D_MODEL] = (dmerged * a_pool * (sp * (1.0 - sp))).astype(BF16)
        dg_ref[:, D_MODEL:] = (dmerged * a_attn * (sa * (1.0 - sa))).astype(BF16)
        dyp_ref[...] = _dot_nt(dap, wp_ref[...])
        dya = _dot_nt(daa, wa_ref[...])
        za = e_ref[:, E_ZA:E_GP].astype(F32)
        sz = _sigmoid(za)
        o = o_ref[...]
        do = dya * (za * sz)
        do_ref[...] = do.astype(BF16)
        dza_ref[...] = (dya * o * (sz * (1.0 + za * (1.0 - sz)))).astype(BF16)
        delta_ref[...] = _per_head_rowsum(do * o)

    row = lambda w: pl.BlockSpec((ROW_TILE, w), lambda i: (i, 0))
    full = lambda a: pl.BlockSpec(a.shape, lambda i: (0, 0))
    return pl.pallas_call(
        body, name="merge_bwd", grid=(nb,),
        in_specs=[row(D_MODEL), full(w_out), row(POOL_WIDTH), row(ATTN_WIDTH), full(wup_p), full(wup_a),
                  row(E_COLS), row(ATTN_WIDTH)],
        out_specs=[row(D_MODEL), row(D_MODEL), row(2 * D_MODEL), row(ATTN_WIDTH), row(ATTN_WIDTH),
                   row(ATTN_WIDTH), row(POOL_WIDTH)],
        out_shape=[jax.ShapeDtypeStruct((lp, D_MODEL), BF16), jax.ShapeDtypeStruct((lp, D_MODEL), BF16),
                   jax.ShapeDtypeStruct((lp, 2 * D_MODEL), BF16), jax.ShapeDtypeStruct((lp, ATTN_WIDTH), BF16),
                   jax.ShapeDtypeStruct((lp, ATTN_WIDTH), BF16), jax.ShapeDtypeStruct((lp, ATTN_WIDTH), F32),
                   jax.ShapeDtypeStruct((lp, POOL_WIDTH), F32)],
        compiler_params=_params(("parallel",)),
    )(dh1, w_out, y_pool, y_attn, wup_p, wup_a, e, o)


def _pool_bwd_local(e, dy_pool, pw, scale):
    lp = e.shape[0]
    nb = lp // ROW_TILE
    ng = len(POOL_WINDOWS)

    def body(uc_ref, up_ref, z_ref, dy_ref, pw_ref, sc_ref, dpc_ref, dz_ref, dsc_ref, dpw_ref):
        i = pl.program_id(0)

        @pl.when(i == 0)
        def _():
            dsc_ref[...] = jnp.zeros_like(dsc_ref)
            dpw_ref[...] = jnp.zeros_like(dpw_ref)

        u_cur = uc_ref[...].astype(F32)
        u_prev = jnp.where(i == 0, 0.0, up_ref[...].astype(F32))
        ps, invs = _pool_p(u_cur, u_prev, i)
        z = z_ref[...].astype(F32)
        sz = _sigmoid(z)
        dy = dy_ref[...]
        dypre = dy * (z * sz)
        dsilu = sz * (1.0 + z * (1.0 - sz))
        for g in range(ng):
            sl = slice(g * POOL_GROUP, (g + 1) * POOL_GROUP)
            pb = ps[g].astype(BF16)
            w = pw_ref[g]
            yraw = jnp.dot(pb, w, preferred_element_type=F32)
            sc = sc_ref[:, sl]
            dz_ref[:, sl] = (dy[:, sl] * (yraw * sc) * dsilu[:, sl]).astype(BF16)
            dsc_ref[:, sl] += jnp.sum(dypre[:, sl] * yraw, axis=0, keepdims=True)
            dyraw = (dypre[:, sl] * sc).astype(BF16)
            dpw_ref[g] += _dot_tn(pb, dyraw)
            dpc_ref[:, sl] = _dot_nt(dyraw, w) * invs[g]

    blk = (ROW_TILE, POOL_WIDTH)
    return pl.pallas_call(
        body, name="pool_bwd_local", grid=(nb,),
        in_specs=[pl.BlockSpec(blk, lambda i: (i, 0)), pl.BlockSpec(blk, lambda i: (jnp.maximum(i - 1, 0), 0)),
                  pl.BlockSpec(blk, lambda i: (i, 1)), pl.BlockSpec(blk, lambda i: (i, 0)),
                  pl.BlockSpec((ng, POOL_GROUP, POOL_GROUP), lambda i: (0, 0, 0)),
                  pl.BlockSpec((1, POOL_WIDTH), lambda i: (0, 0))],
        out_specs=[pl.BlockSpec(blk, lambda i: (i, 0)), pl.BlockSpec(blk, lambda i: (i, 0)),
                   pl.BlockSpec((1, POOL_WIDTH), lambda i: (0, 0)),
                   pl.BlockSpec((ng, POOL_GROUP, POOL_GROUP), lambda i: (0, 0, 0))],
        out_shape=[jax.ShapeDtypeStruct((lp, POOL_WIDTH), F32), jax.ShapeDtypeStruct((lp, POOL_WIDTH), BF16),
                   jax.ShapeDtypeStruct((1, POOL_WIDTH), F32),
                   jax.ShapeDtypeStruct((ng, POOL_GROUP, POOL_GROUP), F32)],
        compiler_params=_params(("arbitrary",)),
    )(e, e, e, dy_pool, pw, scale)


def _pool_bwd_window(dpc):
    lp = dpc.shape[0]
    nb = lp // ROW_TILE

    def body(cur_ref, nxt_ref, du_ref):
        i = pl.program_id(0)
        cur = cur_ref[...]
        nxt = jnp.where(i == nb - 1, 0.0, nxt_ref[...])
        pos = _pool_counts(i)
        for g, w in enumerate(POOL_WINDOWS):
            sl = slice(g * POOL_GROUP, (g + 1) * POOL_GROUP)
            xc = jnp.concatenate([cur[:, sl], nxt[:, sl]], axis=0)
            win = _leading_sums(xc, g + 1)[:ROW_TILE, :]
            dp = cur[:, sl] * jnp.minimum(pos + 1, w).astype(F32)
            du_ref[:, sl] = (win - dp).astype(BF16)

    blk = (ROW_TILE, POOL_WIDTH)
    return pl.pallas_call(
        body, name="pool_bwd_window", grid=(nb,),
        in_specs=[pl.BlockSpec(blk, lambda i: (i, 0)), pl.BlockSpec(blk, lambda i: (jnp.minimum(i + 1, nb - 1), 0))],
        out_specs=pl.BlockSpec(blk, lambda i: (i, 0)),
        out_shape=jax.ShapeDtypeStruct((lp, POOL_WIDTH), BF16),
        compiler_params=_params(("parallel",)),
    )(dpc, dpc)


def _attn_bwd(qkv, do, lse, delta, bias, nb):
    lk = qkv.shape[0]
    lp = nb * ROW_TILE
    nkb = lk // KV_TILE_BWD
    n_pairs = N_HEADS // 2
    per_kv = KV_TILE_BWD // ROW_TILE

    def body(q_ref, k_ref, v_ref, do_ref, lse_ref, dl_ref, b_ref, dq_ref, dk_ref, dv_ref, dc_ref, dcq_ref):
        jj = pl.program_id(1)

        @pl.when(jj == 0)
        def _():
            dq_ref[...] = jnp.zeros_like(dq_ref)
            dcq_ref[...] = jnp.zeros_like(dcq_ref)

        kb, vb = k_ref[...], v_ref[...]

        def block(i, carry, masked):
            dk_acc, dv_acc, dc_acc = carry
            rows = pl.ds(pl.multiple_of(i * ROW_TILE, ROW_TILE), ROW_TILE)
            qs = _stack_heads(q_ref[rows, :])
            dos = _stack_heads(do_ref[rows, :])
            lse_i, dl_i = lse_ref[rows, :], dl_ref[rows, :]
            s = _dot_nt(qs, kb)
            dp = _dot_nt(dos, vb)
            if masked:
                valid = _causal(i, jj, 1, KV_TILE_BWD)
            ps, dss, dcs, rowsums = [], [], [], []
            for hd in range(2):
                half = slice(hd * ROW_TILE, (hd + 1) * ROW_TILE)
                col = slice(hd * HEAD_DIM, hd * HEAD_DIM + 1)
                sh = s[half] - b_ref[i, 0, 0, hd:hd + 1, :]
                if masked:
                    sh = jnp.where(valid, sh, NEG)
                p = jnp.exp(sh - lse_i[:, col])
                ds = p * (dp[half] - dl_i[:, col])
                ps.append(p.astype(BF16))
                dss.append(ds.astype(BF16))
                dcs.append(jnp.sum(ds, axis=0, keepdims=True))
                rowsums.append(jnp.sum(ds, axis=1, keepdims=True))
            dsb = jnp.concatenate(dss, axis=0)
            dv_acc = dv_acc + _dot_tn(jnp.concatenate(ps, axis=0), dos)
            dk_acc = dk_acc + _dot_tn(dsb, qs)
            dc_acc = dc_acc - jnp.concatenate(dcs, axis=0)
            dq_ref[rows, :] += _unstack_heads(jnp.dot(dsb, kb, preferred_element_type=F32))
            dcq_ref[rows, :] += _unstack_heads(jnp.broadcast_to(jnp.concatenate(rowsums, axis=0), (2 * ROW_TILE, LANES)))
            return dk_acc, dv_acc, dc_acc

        init = (jnp.zeros((KV_TILE_BWD, LANES), F32), jnp.zeros((KV_TILE_BWD, LANES), F32),
                jnp.zeros((2, KV_TILE_BWD), F32))
        first_q = per_kv * jj
        diag_end = jnp.minimum(first_q + per_kv, nb)
        carry = lax.fori_loop(first_q, diag_end, lambda i, c: block(i, c, True), init)
        dk, dv, dc = lax.fori_loop(diag_end, nb, lambda i, c: block(i, c, False), carry)
        dk_ref[...] = dk.astype(BF16)
        dv_ref[...] = dv.astype(BF16)
        dc_ref[0, 0] = dc

    whole = lambda rows: pl.BlockSpec((rows, LANES), lambda hp, jj: (0, hp))
    kv_blk = lambda off: pl.BlockSpec((KV_TILE_BWD, LANES), lambda hp, jj: (jj, off + hp))
    return pl.pallas_call(
        body, name="attn_bwd", grid=(n_pairs, nkb),
        in_specs=[whole(lk), kv_blk(n_pairs), kv_blk(2 * n_pairs), whole(lp), whole(lp), whole(lp),
                  pl.BlockSpec((nb, 1, 1, 2, KV_TILE_BWD), lambda hp, jj: (0, hp, jj, 0, 0))],
        out_specs=[whole(lp), kv_blk(0), kv_blk(0),
                   pl.BlockSpec((1, 1, 2, KV_TILE_BWD), lambda hp, jj: (hp, jj, 0, 0)), whole(lp)],
        out_shape=[jax.ShapeDtypeStruct((lp, ATTN_WIDTH), F32), jax.ShapeDtypeStruct((lk, ATTN_WIDTH), BF16),
                   jax.ShapeDtypeStruct((lk, ATTN_WIDTH), BF16),
                   jax.ShapeDtypeStruct((n_pairs, nkb, 2, KV_TILE_BWD), F32),
                   jax.ShapeDtypeStruct((lp, ATTN_WIDTH), F32)],
        compiler_params=_params(("parallel", "arbitrary")),
    )(qkv, qkv, qkv, do, lse, delta, bias)


def _gates_bwd(dc, dcq, f, bfg):
    nb = f.shape[0] // ROW_TILE

    def body(dc_ref, dcq_ref, f_ref, b_ref, df_ref, db_ref, carry):
        step = pl.program_id(0)
        i = nb - 1 - step

        @pl.when(step == 0)
        def _():
            carry[...] = jnp.zeros_like(carry)
            db_ref[...] = jnp.zeros_like(db_ref)

        dcb = dc_ref[...]
        lane = lax.broadcasted_iota(jnp.int32, (ROW_TILE, F_COLS), 1)
        for h in range(N_HEADS):
            dcb = dcb + jnp.where(lane == h, dcq_ref[:, HEAD_DIM * h:HEAD_DIM * h + 1], 0.0)
        r_i = lax.broadcasted_iota(jnp.int32, (ROW_TILE, ROW_TILE), 0)
        c_i = lax.broadcasted_iota(jnp.int32, (ROW_TILE, ROW_TILE), 1)
        upper = (c_i >= r_i).astype(F32)
        dlf = jnp.dot(upper, dcb, precision=HIGHEST, preferred_element_type=F32) + carry[...]
        carry[...] = carry[...] + jnp.sum(dcb, axis=0, keepdims=True)
        logit = f_ref[...] + b_ref[...]
        dlogit = jnp.where(_valid_gate_mask(i), dlf * _sigmoid(-logit), 0.0)
        df_ref[...] = dlogit.astype(BF16)
        db_ref[...] += jnp.sum(dlogit, axis=0, keepdims=True)

    blk = pl.BlockSpec((ROW_TILE, F_COLS), lambda s: (nb - 1 - s, 0))
    wide = pl.BlockSpec((ROW_TILE, ATTN_WIDTH), lambda s: (nb - 1 - s, 0))
    one = pl.BlockSpec((1, F_COLS), lambda s: (0, 0))
    return pl.pallas_call(
        body, name="gates_bwd", grid=(nb,),
        in_specs=[blk, wide, blk, one], out_specs=[blk, one],
        out_shape=[jax.ShapeDtypeStruct(f.shape, BF16), jax.ShapeDtypeStruct((1, F_COLS), F32)],
        scratch_shapes=[pltpu.VMEM((1, F_COLS), F32)],
        compiler_params=_params(("arbitrary",)),
    )(dc, dcq, f, bfg)


def _input_bwd(dproj, df, wt_e, wt_qkv, wt_f, x2, metapad, dh1, g1, hs, sc):
    nb = x2.shape[0] // ROW_TILE + 1
    n = len(hs)

    def body(dp_ref, df_ref, we_ref, w_ref, wf_ref, x_ref, mp_ref, dh_ref, g_ref, *rest):
        h_refs, s_ref = rest[:n], rest[n]
        gx_ref, g0_ref, dg_ref = rest[n + 1:n + 4]
        q_refs, sq_ref = rest[n + 4:2 * n + 4], rest[2 * n + 4]
        sems = rest[2 * n + 5:]
        i = pl.program_id(0)

        @pl.when(i == 0)
        def _():
            dg_ref[...] = jnp.zeros_like(dg_ref)
            own, sends, _ = _chip_scatter_plan(h_refs, s_ref, q_refs, sq_ref, *sems)
            for cp in own + sends:
                cp.start()

        dhn = (jnp.dot(dp_ref[:, :E_COLS], we_ref[...], preferred_element_type=F32)
               + jnp.dot(dp_ref[:, E_COLS:], w_ref[...], preferred_element_type=F32)
               + jnp.dot(df_ref[...], wf_ref[...], preferred_element_type=F32))
        h0 = jnp.where(i == 0, mp_ref[...], x_ref[...])
        r = lax.rsqrt(jnp.mean(h0 * h0, axis=-1, keepdims=True) + RMS_EPS)
        xhat = h0 * r
        dg_ref[...] += jnp.sum(dhn * xhat, axis=0, keepdims=True)
        dxhat = dhn * g_ref[...]
        dh0 = dh_ref[...] + r * (dxhat - xhat * jnp.mean(dxhat * xhat, axis=-1, keepdims=True))
        gx_ref[...] = dh0

        @pl.when(i == 0)
        def _():
            g0_ref[...] = dh0

        @pl.when(i == nb - 1)
        def _():
            own, sends, recvs = _chip_scatter_plan(h_refs, s_ref, q_refs, sq_ref, *sems)
            for cp in recvs:
                cp.wait_recv()
            for cp in sends:
                cp.wait_send()
            for cp in own:
                cp.wait()

    const = lambda shape: pl.BlockSpec(shape, lambda i: (0, 0))
    res = pl.pallas_call(
        body, name="input_bwd", grid=(nb,),
        in_specs=[pl.BlockSpec((ROW_TILE, MAIN_COLS), lambda i: (i, 0)), pl.BlockSpec((ROW_TILE, F_COLS), lambda i: (i, 0)),
                  const((E_COLS, D_MODEL)), const((QKV_COLS, D_MODEL)), const((F_COLS, D_MODEL)),
                  _tokens_spec(), const((ROW_TILE, D_MODEL)),
                  pl.BlockSpec((ROW_TILE, D_MODEL), lambda i: (i, 0)), const((1, D_MODEL))] + [HBM] * (n + 1),
        out_specs=[_tokens_spec(), const((ROW_TILE, D_MODEL)), const((1, D_MODEL))] + [HBM] * (n + 1),
        out_shape=[jax.ShapeDtypeStruct(x2.shape, F32), jax.ShapeDtypeStruct((ROW_TILE, D_MODEL), F32),
                   jax.ShapeDtypeStruct((1, D_MODEL), F32)]
        + [jax.ShapeDtypeStruct(h.shape, h.dtype) for h in hs] + [jax.ShapeDtypeStruct((N_CHIPS,) + sc.shape, sc.dtype)],
        scratch_shapes=[pltpu.SemaphoreType.DMA((3 * (n + 1),)), pltpu.SemaphoreType.DMA((3 * (n + 1),)),
                        pltpu.SemaphoreType.DMA((n + 1,))],
        compiler_params=_params(("arbitrary",), vmem=56 * 1024 * 1024),
    )(dproj, df, wt_e, wt_qkv, wt_f, x2, metapad, dh1, g1, *_in_hbm(*hs, sc))
    return res[:3], res[3:3 + n], res[3 + n]


def _adamw(w, g, m, v, name):
    rows, cols = w.shape
    if rows % 8 == 0:
        tr, tc = _row_tile8(rows), cols
    else:
        tr, tc = rows, (2 * LANES if cols % (2 * LANES) == 0 and rows > 8 else cols)

    def body(w_ref, g_ref, m_ref, v_ref, d_ref, mo_ref, vo_ref):
        g_ = g_ref[...]
        m_new = ADAM_B1 * m_ref[...] + (1.0 - ADAM_B1) * g_
        v_new = ADAM_B2 * v_ref[...] + (1.0 - ADAM_B2) * (g_ * g_)
        m_hat = m_new / (1.0 - ADAM_B1 ** ADAM_STEP)
        v_hat = v_new / (1.0 - ADAM_B2 ** ADAM_STEP)
        d_ref[...] = -ADAM_LR * (m_hat / (jnp.sqrt(v_hat) + ADAM_EPS) + ADAM_WD * w_ref[...])
        mo_ref[...] = m_new
        vo_ref[...] = v_new

    blk = pl.BlockSpec((tr, tc), lambda i, j: (i, j))
    return pl.pallas_call(
        body, name=name, grid=(rows // tr, cols // tc),
        in_specs=[blk] * 4, out_specs=[blk] * 3,
        out_shape=[jax.ShapeDtypeStruct(w.shape, F32)] * 3,
        compiler_params=_params(("parallel", "parallel")),
    )(w, g, m, v)


def _row_tile8(rows):
    best = rows
    for t in range(8, 257, 8):
        if rows % t == 0:
            best = t
    return best


def kernel(x, meta_tokens, norm_g, w_in, b_forget, pool_w, pool_scale, w_up_pool, w_up_attn, w_out, final_norm_g, loss_target, m_meta_tokens, m_norm_g, m_w_in, m_b_forget, m_pool_w, m_pool_scale, m_w_up_pool, m_w_up_attn, m_w_out, m_final_norm_g, v_meta_tokens, v_norm_g, v_w_in, v_b_forget, v_pool_w, v_pool_scale, v_w_up_pool, v_w_up_attn, v_w_out, v_final_norm_g):
    seq = x.shape[1]
    assert seq % ROW_TILE == 0 and x.shape[0] == 1
    lp = seq + ROW_TILE
    nb = lp // ROW_TILE
    lk = -(-lp // KV_TILE_BWD) * KV_TILE_BWD
    core = jnp.reshape(lax.axis_index("c"), (1,)).astype(jnp.int32)
    x2 = x[0]
    target = loss_target[0]
    sh_d = D_MODEL // N_CHIPS

    w_in_t = jnp.transpose(w_in[0])
    gf = final_norm_g.reshape(1, D_MODEL)
    bfg = jnp.pad(b_forget, ((0, 0), (0, F_COLS - N_HEADS)))
    pw_b = pool_w[0].astype(BF16)
    hn, (wg_in, meta_g) = _rmsnorm_fwd(x2, norm_g, lk, [w_in_t.astype(BF16), meta_tokens], [1, 0])
    wt = wg_in.reshape(-1, D_MODEL)
    wt_e = jnp.concatenate([wt[REF_U:REF_Q], wt[REF_ZA:REF_F], wt[REF_GP:]], axis=0)
    wt_qkv = wt[REF_Q:REF_ZA]
    wt_f = jnp.pad(wt[REF_F:REF_GP], ((0, F_COLS - N_HEADS), (0, 0)))
    meta_full = jnp.transpose(meta_g, (1, 0, 2)).reshape(N_META, D_MODEL)
    metapad = jnp.pad(meta_full, ((PAD_ROWS, 0), (0, 0)))

    tm = _row_tile(lp, 1100)
    e, (wg_up_p, wg_up_a, wg_out) = _mm_nt(
        hn, wt_e, BF16, "in_proj_gates", lp, E_COLS, tm, 512,
        gather=([w_up_pool[0].astype(BF16), w_up_attn[0].astype(BF16), w_out[0].astype(BF16)], [0, 0, 0]))
    wup_p = jnp.transpose(wg_up_p, (1, 0, 2)).reshape(POOL_WIDTH, D_MODEL)
    wup_a = jnp.transpose(wg_up_a, (1, 0, 2)).reshape(ATTN_WIDTH, D_MODEL)
    wout = wg_out.reshape(D_MODEL, D_MODEL)
    qkv = _mm_nt(hn, wt_qkv, BF16, "in_proj_qkv", lk, QKV_COLS, _row_tile(lk, 1300), 512, scale_first=HEAD_DIM ** -0.5)
    f = _mm_nt(hn, wt_f, F32, "in_proj_forget", lp, F_COLS, tm, F_COLS)
    c = _gates_fwd(f, bfg)
    c_t = jnp.transpose(c[:, :N_HEADS])
    bias = c_t[None, :, :] - jnp.transpose(c_t[:, ::ROW_TILE])[:, :, None]
    bias = jnp.where(jnp.arange(lp) < PAD_ROWS, -NEG, bias)
    bias = jnp.pad(bias, ((0, 0), (0, 0), (0, lk - lp)))
    by_kv = lambda t: jnp.transpose(bias.reshape(nb, N_HEADS // 2, 2, lk // t, t), (0, 1, 3, 2, 4))
    bias, bias_bwd = by_kv(KV_TILE), by_kv(KV_TILE_BWD)
    y_pool = _pool_fwd(e, pw_b, pool_scale)
    o, lse = _attn_fwd(qkv, bias, nb)
    merged, y_attn = _merge_fwd(y_pool, o, e, wup_p, wup_a)
    dh1, loss_part, dgf = _head_fwd_bwd(merged, wout, x2, metapad, gf, target)

    dap, daa, dgate, dza, do, delta, dy_pool = _merge_bwd(dh1, wout, y_pool, y_attn, wup_p, wup_a, e, o)
    dpc, dzp, dscale, dpw = _pool_bwd_local(e, dy_pool, pw_b, pool_scale)
    du = _pool_bwd_window(dpc)
    dq, dk, dv, dc4, dcq = _attn_bwd(qkv, do, lse, delta, bias_bwd, nb)
    dc = jnp.transpose(dc4, (1, 3, 0, 2)).reshape(lk, N_HEADS)[:lp]
    df, db = _gates_bwd(jnp.pad(dc, ((0, 0), (0, F_COLS - N_HEADS))), dcq, f, bfg)
    dproj = jnp.concatenate([du, dzp, dza, dgate, (dq * HEAD_DIM ** -0.5).astype(BF16), dk[:lp], dv[:lp]], axis=1)
    tk = _row_tile(lp, 1100)
    dw_out = _mm_tn(merged, dh1, "grad_w_out", lp, 512, D_MODEL, tk)
    dw_up_p = _mm_tn(y_pool, dap, "grad_w_up_pool", lp, 512, D_MODEL, tk, chunks=N_CHIPS)
    dw_up_a = _mm_tn(y_attn, daa, "grad_w_up_attn", lp, 512, D_MODEL, tk, chunks=N_CHIPS)
    dwt_f = _mm_tn(df, hn, "grad_w_in_forget", lp, F_COLS, D_MODEL, tk)

    def pad8(a):
        return jnp.pad(a, ((0, (-a.shape[0]) % 8), (0, 0)))

    small_parts = [pad8(a) for a in (dgf.reshape(-1, LANES), dscale.reshape(-1, LANES), db, dpw.reshape(-1, LANES),
                                     loss_part)]
    small = jnp.concatenate(small_parts, axis=0)
    soffs = [0]
    for p in small_parts:
        soffs.append(soffs[-1] + p.shape[0])

    gs_rows = [dw_up_p, dw_up_a, dw_out.reshape(N_CHIPS, sh_d, D_MODEL)]
    half = MAIN_COLS // 2
    dwt_a, (*rb_rows, rb_f, sr) = _mm_tn(dproj, hn, "grad_w_in_a", lp, half // 2, D_MODEL, tk, m=half, m_off=0,
                                         swap=(gs_rows + [dwt_f, small], [1, 1, 1, 1, None]))
    dwt_b, (rb_a,) = _mm_tn(dproj, hn, "grad_w_in_b", lp, half // 2, D_MODEL, tk, m=half, m_off=2, swap=([dwt_a], [1]))
    (rb_b,) = _sibling_exchange([dwt_b], [1])
    *hs_rows, sc = _add_sibling_half(gs_rows, rb_rows, [1, 1, 1], small, sr, core)
    h_a, h_b, h_f = _add_halves_2d([dwt_a, dwt_b, dwt_f], [rb_a, rb_b, rb_f], core)
    h_in = jnp.concatenate([h_a[E_U:E_ZA], h_b[D_MODEL:], h_a[E_ZA:E_GP], h_f[:N_HEADS], h_a[E_GP:], h_b[:D_MODEL]], axis=0)
    hs = [h_in.reshape(N_CHIPS, -1, D_MODEL // 2)] + hs_rows
    grad_axes = [2, 1, 1, 1]
    (grad_x, g_block0, dg1), qs, sq = _input_bwd(dproj, df, wt_e, wt_qkv, wt_f, x2, metapad, dh1, norm_g, hs, sc)
    g_w_in_t, g_w_up_p, g_w_up_a, g_w_out, st = _reduce_allgather(qs, grad_axes, sq)
    late = _small_allreduce(jnp.concatenate([pad8(dg1.reshape(-1, LANES)), g_block0[PAD_ROWS:].reshape(-1, LANES)], axis=0))
    n_norm = D_MODEL // LANES
    g_norm = late[:n_norm].reshape(1, D_MODEL)
    chip = 2 * lax.axis_index("x") + lax.axis_index("y")
    g_meta = lax.dynamic_slice_in_dim(late[n_norm:].reshape(N_META, D_MODEL), chip * sh_d, sh_d, axis=1)

    spiece = lambda k, rows: st[soffs[k]:soffs[k] + rows]
    g_final = spiece(0, D_MODEL // LANES).reshape(1, D_MODEL)
    g_scale = spiece(1, POOL_WIDTH // LANES).reshape(1, POOL_WIDTH)
    g_bf = spiece(2, 1)
    g_pw = spiece(3, POOL_WIDTH)
    loss = st[soffs[4], 0]

    def pad_lanes(a):
        return jnp.pad(a, ((0, 0), (0, F_COLS - N_HEADS)))

    upd = [
        ("meta_tokens", meta_tokens, g_meta, m_meta_tokens, v_meta_tokens),
        ("norm_g", norm_g, g_norm, m_norm_g, v_norm_g),
        ("w_in", w_in_t, g_w_in_t, jnp.transpose(m_w_in[0]), jnp.transpose(v_w_in[0])),
        ("b_forget", pad_lanes(b_forget), g_bf, pad_lanes(m_b_forget), pad_lanes(v_b_forget)),
        ("pool_w", pool_w.reshape(-1, LANES), g_pw, m_pool_w.reshape(-1, LANES), v_pool_w.reshape(-1, LANES)),
        ("pool_scale", pool_scale, g_scale, m_pool_scale, v_pool_scale),
        ("w_up_pool", w_up_pool[0], g_w_up_p, m_w_up_pool[0], v_w_up_pool[0]),
        ("w_up_attn", w_up_attn[0], g_w_up_a, m_w_up_attn[0], v_w_up_attn[0]),
        ("w_out", w_out[0], g_w_out, m_w_out[0], v_w_out[0]),
        ("final_norm_g", gf, g_final, m_final_norm_g.reshape(1, D_MODEL), v_final_norm_g.reshape(1, D_MODEL)),
    ]
    shapes = [meta_tokens.shape, norm_g.shape, w_in.shape, b_forget.shape, pool_w.shape, pool_scale.shape,
              w_up_pool.shape, w_up_attn.shape, w_out.shape, final_norm_g.shape]
    grads, deltas, new_ms, new_vs = [], [], [], []
    for (name, w_, g_, m_in, v_in), shp in zip(upd, shapes):
        d_, mn_, vn_ = _adamw(w_, g_, m_in, v_in, "adamw_" + name)
        res = (g_, d_, mn_, vn_)
        if name == "b_forget":
            res = tuple(a[:, :N_HEADS] for a in res)
        if name == "w_in":
            res = tuple(jnp.transpose(a) for a in res)
        for lst, a in zip((grads, deltas, new_ms, new_vs), res):
            lst.append(a.reshape(shp))

    return (loss, grad_x.reshape(x.shape), *grads, *deltas, *new_ms, *new_vs)
```

```python
import jax
import jax.numpy as jnp
from jax import lax
from jax.experimental import pallas as pl
from jax.experimental.pallas import tpu as pltpu

F32 = jnp.float32
BF16 = jnp.bfloat16
MESH = pl.DeviceIdType.MESH
HIGHEST = lax.Precision.HIGHEST
HBM = pl.BlockSpec(memory_space=pltpu.HBM)

D_MODEL = 1024
N_META = 16
POOL_WIDTH = 512
POOL_GROUP = 128
POOL_WINDOWS = (2, 4, 8, 16)
N_HEADS = 8
HEAD_DIM = 64
ATTN_WIDTH = 512
RMS_EPS = 1e-6
N_CHIPS = 4

ADAM_LR = 0.001
ADAM_B1 = 0.9
ADAM_B2 = 0.999
ADAM_EPS = 1e-08
ADAM_WD = 0.01
ADAM_STEP = 10

LANES = 128
ROW_TILE = 256
KV_TILE = 512
KV_TILE_BWD = 1024
PAD_ROWS = ROW_TILE - N_META
NEG = -1e30

E_U, E_ZP, E_ZA, E_GP, E_GA, E_COLS = 0, 512, 1024, 1536, 2560, 3584
QKV_COLS = 3 * ATTN_WIDTH
MAIN_COLS = E_COLS + QKV_COLS
F_COLS = LANES
REF_U, REF_Q, REF_ZA, REF_F, REF_GP = 0, 1024, 2560, 3072, 3080
VMEM_LIMIT = 48 * 1024 * 1024


def _params(sem=None, vmem=VMEM_LIMIT):
    return pltpu.CompilerParams(dimension_semantics=sem, vmem_limit_bytes=vmem)


def _in_hbm(*arrays):
    return [pltpu.with_memory_space_constraint(a, pltpu.HBM) for a in arrays]


def _row_tile(n, target):
    best = 16
    for t in range(16, target + 1, 16):
        if n % t == 0:
            best = t
    return best


def _sigmoid(x):
    return 1.0 / (1.0 + jnp.exp(-x))


def _half(ref, axis, which):
    n = ref.shape[axis] // 2
    idx = [slice(None)] * len(ref.shape)
    idx[axis] = pl.ds(pl.multiple_of(which * n, n), n)
    return ref.at[tuple(idx)]


def _half_shape(shape, axis):
    s = list(shape)
    s[axis] //= 2
    return tuple(s)


def _gather_start(ins, outs, axes, send, recv, fsend, frecv, local):
    x, y, c = lax.axis_index("x"), lax.axis_index("y"), lax.axis_index("c")
    me = 2 * x + y
    for t in range(len(ins)):
        pltpu.make_async_copy(ins[t], outs[t].at[me], local.at[t]).start()
        for k, chip in enumerate([(1 - x, y), (x, 1 - y), (1 - x, 1 - y)]):
            pltpu.make_async_remote_copy(
                src_ref=_half(ins[t], axes[t], c), dst_ref=_half(outs[t].at[me], axes[t], c),
                send_sem=send.at[3 * t + k], recv_sem=recv.at[3 * t + k], device_id=(*chip, c), device_id_type=MESH).start()


def _gather_finish(ins, outs, axes, send, recv, fsend, frecv, local):
    x, y, c = lax.axis_index("x"), lax.axis_index("y"), lax.axis_index("c")
    me = 2 * x + y
    sibling = (x, y, 1 - c)
    chips = [(1 - x, y), (x, 1 - y), (1 - x, 1 - y)]
    n = len(ins)

    def over_ici(t, k, chip, dst_slot):
        return pltpu.make_async_remote_copy(
            src_ref=_half(ins[t], axes[t], c), dst_ref=_half(outs[t].at[dst_slot], axes[t], c),
            send_sem=send.at[3 * t + k], recv_sem=recv.at[3 * t + k], device_id=(*chip, c), device_id_type=MESH)

    def to_sibling(t, k, slot, which):
        return pltpu.make_async_remote_copy(
            src_ref=_half(outs[t].at[slot], axes[t], which), dst_ref=_half(outs[t].at[slot], axes[t], which),
            send_sem=fsend.at[3 * t + k], recv_sem=frecv.at[3 * t + k], device_id=sibling, device_id_type=MESH)

    forwards = []
    for t in range(n):
        for k, (px, py) in enumerate(chips):
            over_ici(t, k, (px, py), 2 * px + py).wait_recv()
            fw = to_sibling(t, k, 2 * px + py, c)
            fw.start()
            forwards.append(fw)
    for t in range(n):
        for k, (px, py) in enumerate(chips):
            to_sibling(t, k, 2 * px + py, 1 - c).wait_recv()
    for t in range(n):
        for k, chip in enumerate(chips):
            over_ici(t, k, chip, me).wait_send()
    for fw in forwards:
        fw.wait_send()
    for t in range(n):
        pltpu.make_async_copy(ins[t], outs[t].at[me], local.at[t]).wait()


def _gather_sems(n):
    return [pltpu.SemaphoreType.DMA((3 * n,)), pltpu.SemaphoreType.DMA((3 * n,)), pltpu.SemaphoreType.DMA((3 * n,)),
            pltpu.SemaphoreType.DMA((3 * n,)), pltpu.SemaphoreType.DMA((n,))]


def _gathered_shapes(shards):
    return [jax.ShapeDtypeStruct((N_CHIPS,) + s.shape, s.dtype) for s in shards]


def _swap_copies(srcs, dsts, axes, send, recv):
    x, y, c = lax.axis_index("x"), lax.axis_index("y"), lax.axis_index("c")
    return [pltpu.make_async_remote_copy(
        src_ref=srcs[t] if axes[t] is None else _half(srcs[t], axes[t], 1 - c), dst_ref=dsts[t],
        send_sem=send.at[t], recv_sem=recv.at[t], device_id=(x, y, 1 - c), device_id_type=MESH) for t in range(len(srcs))]


def _swap_shapes(srcs, axes):
    return [jax.ShapeDtypeStruct(g.shape if a is None else _half_shape(g.shape, a), g.dtype) for g, a in zip(srcs, axes)]


def _sibling_exchange(gs, axes):
    n = len(gs)

    def body(*refs):
        cps = _swap_copies(refs[:n], refs[n:2 * n], axes, *refs[2 * n:])
        for cp in cps:
            cp.start()
        for cp in cps:
            cp.wait()

    return pl.pallas_call(
        body, name="grad_sibling_exchange",
        in_specs=[HBM] * n, out_specs=[HBM] * n, out_shape=_swap_shapes(gs, axes),
        scratch_shapes=[pltpu.SemaphoreType.DMA((n,)), pltpu.SemaphoreType.DMA((n,))],
    )(*_in_hbm(*gs))


def _add_halves_2d(parts, rbs, core):
    n = len(parts)
    rows, w = rbs[0].shape
    tr = 512
    last_rows = rbs[-1].shape[0]

    def body(core_ref, *refs):
        p_refs, r_refs, h_refs = refs[:n], refs[n:2 * n], refs[2 * n:]
        for t in range(n - 1):
            h_refs[t][...] = (p_refs[t][...] + r_refs[t][...]).astype(BF16)

        @pl.when(pl.program_id(0) == 0)
        def _():
            h_refs[n - 1][...] = (p_refs[n - 1][...] + r_refs[n - 1][...]).astype(BF16)

    mine = lambda r: pl.BlockSpec((r, w), lambda i, cr: (i, cr[0]))
    tile = lambda r: pl.BlockSpec((r, w), lambda i, cr: (i, 0))
    mine_small = pl.BlockSpec((last_rows, w), lambda i, cr: (0, cr[0]))
    tile_small = pl.BlockSpec((last_rows, w), lambda i, cr: (0, 0))
    return pl.pallas_call(
        body, name="grad_add_sibling_w_in",
        grid_spec=pltpu.PrefetchScalarGridSpec(
            num_scalar_prefetch=1, grid=(rows // tr,),
            in_specs=[mine(tr)] * (n - 1) + [mine_small] + [tile(tr)] * (n - 1) + [tile_small],
            out_specs=[tile(tr)] * (n - 1) + [tile_small]),
        out_shape=[jax.ShapeDtypeStruct(r.shape, BF16) for r in rbs],
        compiler_params=_params(("arbitrary",)),
    )(core, *parts, *rbs)


def _add_sibling_half(gs, rbs, axes, small, sr, core):
    n = len(gs)

    def body(core_ref, *refs):
        g_refs, rb_refs = refs[:n], refs[n:2 * n]
        s_ref, sr_ref = refs[2 * n], refs[2 * n + 1]
        h_refs, sc_ref = refs[2 * n + 2:3 * n + 2], refs[3 * n + 2]
        for t in range(n):
            h_refs[t][...] = (g_refs[t][...] + rb_refs[t][...]).astype(BF16)
        sc_ref[...] = s_ref[...] + sr_ref[...]

    def mine(rb, axis):
        blk = (None,) + rb.shape[1:]
        if axis == 2:
            return pl.BlockSpec(blk, lambda j, cr: (j, 0, cr[0]))
        return pl.BlockSpec(blk, lambda j, cr: (j, cr[0], 0))

    chunk = lambda rb: pl.BlockSpec((None,) + rb.shape[1:], lambda j, cr: (j, 0, 0))
    whole = pl.BlockSpec(small.shape, lambda j, cr: (0, 0))
    return pl.pallas_call(
        body, name="grad_add_sibling",
        grid_spec=pltpu.PrefetchScalarGridSpec(
            num_scalar_prefetch=1, grid=(N_CHIPS,),
            in_specs=[mine(rb, a) for rb, a in zip(rbs, axes)] + [chunk(rb) for rb in rbs] + [whole, whole],
            out_specs=[chunk(rb) for rb in rbs] + [whole]),
        out_shape=[jax.ShapeDtypeStruct(rb.shape, BF16) for rb in rbs] + [jax.ShapeDtypeStruct(small.shape, F32)],
        compiler_params=_params(("arbitrary",)),
    )(core, *gs, *rbs, small, sr)


def _chip_scatter_plan(h_refs, s_ref, q_refs, sq_ref, send, recv, local):
    n = len(h_refs)
    x, y, c = lax.axis_index("x"), lax.axis_index("y"), lax.axis_index("c")
    me = 2 * x + y
    chips = [(1 - x, y), (x, 1 - y), (1 - x, 1 - y)]

    def copy(t, k, chip, src_slot, dst_slot):
        src = h_refs[t].at[src_slot] if t < n else s_ref
        dst = (q_refs[t] if t < n else sq_ref).at[dst_slot]
        return pltpu.make_async_remote_copy(src_ref=src, dst_ref=dst, send_sem=send.at[3 * t + k],
                                            recv_sem=recv.at[3 * t + k], device_id=(*chip, c), device_id_type=MESH)

    own = [pltpu.make_async_copy(h_refs[t].at[me], q_refs[t].at[me], local.at[t]) for t in range(n)]
    own.append(pltpu.make_async_copy(s_ref, sq_ref.at[me], local.at[n]))
    sends = [copy(t, k, (px, py), 2 * px + py, me) for t in range(n + 1) for k, (px, py) in enumerate(chips)]
    recvs = [copy(t, k, (px, py), me, 2 * px + py) for t in range(n + 1) for k, (px, py) in enumerate(chips)]
    return own, sends, recvs


def _small_allreduce(buf):
    def body(b_ref, o_ref, sib_buf, chip_buf, send, recv):
        x, y, c = lax.axis_index("x"), lax.axis_index("y"), lax.axis_index("c")
        me = 2 * x + y
        chips = [(1 - x, y), (x, 1 - y), (1 - x, 1 - y)]
        swap = pltpu.make_async_remote_copy(src_ref=b_ref, dst_ref=sib_buf, send_sem=send.at[0], recv_sem=recv.at[0],
                                            device_id=(x, y, 1 - c), device_id_type=MESH)
        swap.start()
        swap.wait()
        chip_buf[me] = b_ref[...] + sib_buf[...]

        def copy(k, chip, slot):
            return pltpu.make_async_remote_copy(src_ref=chip_buf.at[slot], dst_ref=chip_buf.at[slot], send_sem=send.at[1 + k],
                                                recv_sem=recv.at[1 + k], device_id=(*chip, c), device_id_type=MESH)

        sends = [copy(k, chip, me) for k, chip in enumerate(chips)]
        for cp in sends:
            cp.start()
        for k, (px, py) in enumerate(chips):
            copy(k, (px, py), 2 * px + py).wait_recv()
        for cp in sends:
            cp.wait_send()
        o_ref[...] = ((chip_buf[0] + chip_buf[1]) + chip_buf[2]) + chip_buf[3]

    vmem = pl.BlockSpec(memory_space=pltpu.VMEM)
    return pl.pallas_call(
        body, name="small_allreduce", in_specs=[vmem], out_specs=vmem,
        out_shape=jax.ShapeDtypeStruct(buf.shape, F32),
        scratch_shapes=[pltpu.VMEM(buf.shape, F32), pltpu.VMEM((N_CHIPS,) + buf.shape, F32),
                        pltpu.SemaphoreType.DMA((4,)), pltpu.SemaphoreType.DMA((4,))],
        compiler_params=_params(),
    )(buf)


def _reduce_allgather(qs, axes, sq):
    n = len(qs)
    shard_shapes = [tuple(d * 2 if i == a - 1 else d for i, d in enumerate(q.shape[1:])) for q, a in zip(qs, axes)]

    def body(*refs):
        q_refs, sq_ref = refs[:n], refs[n]
        o_refs, st_ref = refs[n + 1:2 * n + 1], refs[2 * n + 1]
        send, recv = refs[2 * n + 2:]
        x, y, c = lax.axis_index("x"), lax.axis_index("y"), lax.axis_index("c")

        def swap(t, which):
            return pltpu.make_async_remote_copy(
                src_ref=_half(o_refs[t], axes[t] - 1, which), dst_ref=_half(o_refs[t], axes[t] - 1, which),
                send_sem=send.at[t], recv_sem=recv.at[t], device_id=(x, y, 1 - c), device_id_type=MESH)

        sent = []
        for t in range(n):
            q = q_refs[t]
            total = ((q[0].astype(F32) + q[1].astype(F32)) + q[2].astype(F32)) + q[3].astype(F32)
            _half(o_refs[t], axes[t] - 1, c)[...] = total
            cp = swap(t, c)
            cp.start()
            sent.append(cp)
        st_ref[...] = ((sq_ref[0] + sq_ref[1]) + sq_ref[2]) + sq_ref[3]
        for t in range(n):
            swap(t, 1 - c).wait_recv()
        for cp in sent:
            cp.wait_send()

    vmem = pl.BlockSpec(memory_space=pltpu.VMEM)
    return pl.pallas_call(
        body, name="grad_reduce_allgather",
        in_specs=[vmem] * (n + 1), out_specs=[vmem] * (n + 1),
        out_shape=[jax.ShapeDtypeStruct(s, F32) for s in shard_shapes] + [jax.ShapeDtypeStruct(sq.shape[1:], F32)],
        scratch_shapes=[pltpu.SemaphoreType.DMA((n,)), pltpu.SemaphoreType.DMA((n,))],
        compiler_params=_params(),
    )(*qs, sq)


def _dot_nt(a, b):
    return lax.dot_general(a, b, (((1,), (1,)), ((), ())), preferred_element_type=F32)


def _dot_tn(a, b):
    return lax.dot_general(a, b, (((0,), (0,)), ((), ())), preferred_element_type=F32)


def _mm_nt(a, bt, out_dtype, name, m, n, tm, tn, row_block=0, scale_first=None, gather=None):
    k = a.shape[1]
    shards, axes = gather if gather is not None else ([], [])
    ng = len(shards)
    grid = (m // tm, n // tn)

    def body(a_ref, b_ref, *rest):
        ins, o_ref, outs, sems = rest[:ng], rest[ng], rest[ng + 1:2 * ng + 1], rest[2 * ng + 1:]
        first = (pl.program_id(0) == 0) & (pl.program_id(1) == 0)
        last = (pl.program_id(0) == grid[0] - 1) & (pl.program_id(1) == grid[1] - 1)
        if ng:
            @pl.when(first)
            def _():
                _gather_start(ins, outs, axes, *sems)

        r = _dot_nt(a_ref[...], b_ref[...])
        if scale_first is not None:
            r = r * jnp.where(pl.program_id(1) == 0, scale_first, 1.0)
        o_ref[...] = r.astype(out_dtype)

        if ng:
            @pl.when(last)
            def _():
                _gather_finish(ins, outs, axes, *sems)

    res = pl.pallas_call(
        body, name=name, grid=grid,
        in_specs=[pl.BlockSpec((tm, k), lambda i, j: (i, 0)), pl.BlockSpec((tn, k), lambda i, j: (row_block + j, 0))]
        + [HBM] * ng,
        out_specs=[pl.BlockSpec((tm, tn), lambda i, j: (i, j))] + [HBM] * ng,
        out_shape=[jax.ShapeDtypeStruct((m, n), out_dtype)] + _gathered_shapes(shards),
        scratch_shapes=_gather_sems(ng) if ng else [],
        compiler_params=_params(("arbitrary", "arbitrary") if ng else ("parallel", "parallel")),
    )(a, bt, *_in_hbm(*shards))
    return (res[0], res[1:]) if ng else res[0]


def _mm_tn(a, b, name, k, tm, tn, tk, chunks=1, m=None, m_off=0, swap=None):
    m = a.shape[1] if m is None else m
    n = b.shape[1]
    cw = n // chunks
    srcs, axes = swap if swap is not None else ([], [])
    ns = len(srcs)
    grid = (m // tm, n // tn, k // tk)

    def body(a_ref, b_ref, *rest):
        s_refs, o_ref, d_refs, sems = rest[:ns], rest[ns], rest[ns + 1:2 * ns + 1], rest[2 * ns + 1:]
        ids = [pl.program_id(d) for d in range(3)]
        if ns:
            @pl.when((ids[0] == 0) & (ids[1] == 0) & (ids[2] == 0))
            def _():
                for cp in _swap_copies(s_refs, d_refs, axes, *sems):
                    cp.start()

        @pl.when(ids[2] == 0)
        def _():
            o_ref[...] = jnp.zeros_like(o_ref)
        r = _dot_tn(a_ref[...].astype(BF16), b_ref[...].astype(BF16))
        if chunks > 1:
            for c in range(chunks):
                o_ref[c] += r[:, c * cw:(c + 1) * cw]
        else:
            o_ref[...] += r

        if ns:
            @pl.when((ids[0] == grid[0] - 1) & (ids[1] == grid[1] - 1) & (ids[2] == grid[2] - 1))
            def _():
                for cp in _swap_copies(s_refs, d_refs, axes, *sems):
                    cp.wait()

    if chunks > 1:
        assert tn == n
        out_spec = pl.BlockSpec((chunks, tm, cw), lambda i, j, kk: (0, i, 0))
        out_shape = jax.ShapeDtypeStruct((chunks, m, cw), F32)
    else:
        out_spec = pl.BlockSpec((tm, tn), lambda i, j, kk: (i, j))
        out_shape = jax.ShapeDtypeStruct((m, n), F32)
    res = pl.pallas_call(
        body, name=name, grid=grid,
        in_specs=[pl.BlockSpec((tk, tm), lambda i, j, kk: (kk, m_off + i)), pl.BlockSpec((tk, tn), lambda i, j, kk: (kk, j))]
        + [HBM] * ns,
        out_specs=[out_spec] + [HBM] * ns, out_shape=[out_shape] + _swap_shapes(srcs, axes),
        scratch_shapes=[pltpu.SemaphoreType.DMA((ns,)), pltpu.SemaphoreType.DMA((ns,))] if ns else [],
        compiler_params=_params(("arbitrary",) * 3 if ns else ("parallel", "parallel", "arbitrary")),
    )(a, b, *_in_hbm(*srcs))
    return (res[0], res[1:]) if ns else res[0]


def _tokens_spec():
    return pl.BlockSpec((ROW_TILE, D_MODEL), lambda i: (jnp.maximum(i - 1, 0), 0))


def _rmsnorm_fwd(x2, g1, lk, shards, axes):
    nb = x2.shape[0] // ROW_TILE + 1
    nblk = lk // ROW_TILE
    ng = len(shards)
    meta_cols = shards[-1].shape[1]

    def body(x_ref, g_ref, *rest):
        ins, hn_ref, outs = rest[:ng], rest[ng], rest[ng + 1:2 * ng + 1]
        sems, meta_buf, meta_sem = rest[2 * ng + 1:2 * ng + 6], rest[2 * ng + 6], rest[2 * ng + 7]
        s = pl.program_id(0)
        blk = (s + 1) % nblk

        @pl.when(s == 0)
        def _():
            _gather_start(ins, outs, axes, *sems)

        def normed(h):
            r = lax.rsqrt(jnp.mean(h * h, axis=-1, keepdims=True) + RMS_EPS)
            return ((h * r) * g_ref[...]).astype(BF16)

        @pl.when(s < nblk - 1)
        def _():
            hn_ref[...] = normed(jnp.where(blk >= nb, 0.0, x_ref[...]))

        @pl.when(s == nblk - 1)
        def _():
            _gather_finish(ins, outs, axes, *sems)
            fetch = pltpu.make_async_copy(outs[-1], meta_buf, meta_sem.at[0])
            fetch.start()
            fetch.wait()
            meta = jnp.concatenate([meta_buf[j] for j in range(N_CHIPS)], axis=1)
            hn_ref[...] = normed(jnp.concatenate([jnp.zeros((PAD_ROWS, D_MODEL), F32), meta], axis=0))

    res = pl.pallas_call(
        body, name="rmsnorm_fwd", grid=(nblk,),
        in_specs=[pl.BlockSpec((ROW_TILE, D_MODEL), lambda s: (jnp.clip((s + 1) % nblk - 1, 0, nb - 2), 0)),
                  pl.BlockSpec((1, D_MODEL), lambda s: (0, 0))] + [HBM] * ng,
        out_specs=[pl.BlockSpec((ROW_TILE, D_MODEL), lambda s: ((s + 1) % nblk, 0))] + [HBM] * ng,
        out_shape=[jax.ShapeDtypeStruct((lk, D_MODEL), BF16)] + _gathered_shapes(shards),
        scratch_shapes=_gather_sems(ng) + [pltpu.VMEM((N_CHIPS, N_META, meta_cols), F32), pltpu.SemaphoreType.DMA((1,))],
        compiler_params=_params(("arbitrary",)),
    )(x2, g1, *_in_hbm(*shards))
    return res[0], res[1:]


def _valid_gate_mask(i):
    row = i * ROW_TILE + lax.broadcasted_iota(jnp.int32, (ROW_TILE, F_COLS), 0)
    col = lax.broadcasted_iota(jnp.int32, (ROW_TILE, F_COLS), 1)
    return (row >= PAD_ROWS) & (col < N_HEADS)


def _gates_fwd(f, bfg):
    nb = f.shape[0] // ROW_TILE

    def body(f_ref, b_ref, c_ref, carry):
        i = pl.program_id(0)

        @pl.when(i == 0)
        def _():
            carry[...] = jnp.zeros_like(carry)

        logit = f_ref[...] + b_ref[...]
        lf = jnp.minimum(logit, 0.0) - jnp.log1p(jnp.exp(-jnp.abs(logit)))
        lf = jnp.where(_valid_gate_mask(i), lf, 0.0)
        r_i = lax.broadcasted_iota(jnp.int32, (ROW_TILE, ROW_TILE), 0)
        c_i = lax.broadcasted_iota(jnp.int32, (ROW_TILE, ROW_TILE), 1)
        tri = (c_i <= r_i).astype(F32)
        c_ref[...] = jnp.dot(tri, lf, precision=HIGHEST, preferred_element_type=F32) + carry[...]
        carry[...] = carry[...] + jnp.sum(lf, axis=0, keepdims=True)

    return pl.pallas_call(
        body, name="gates_fwd", grid=(nb,),
        in_specs=[pl.BlockSpec((ROW_TILE, F_COLS), lambda i: (i, 0)), pl.BlockSpec((1, F_COLS), lambda i: (0, 0))],
        out_specs=pl.BlockSpec((ROW_TILE, F_COLS), lambda i: (i, 0)),
        out_shape=jax.ShapeDtypeStruct(f.shape, F32),
        scratch_shapes=[pltpu.VMEM((1, F_COLS), F32)],
        compiler_params=_params(("arbitrary",)),
    )(f, bfg)


def _pool_counts(i):
    row = i * ROW_TILE + lax.broadcasted_iota(jnp.int32, (ROW_TILE, 1), 0)
    return jnp.maximum(row - PAD_ROWS, 0)


def _trailing_sums(xc, levels):
    acc = xc
    for lv in range(levels):
        acc = acc + pltpu.roll(acc, 1 << lv, 0)
    return acc


def _leading_sums(xc, levels):
    n = xc.shape[0]
    acc = xc
    for lv in range(levels):
        acc = acc + pltpu.roll(acc, n - (1 << lv), 0)
    return acc


def _pool_p(u_cur, u_prev, i):
    pos = _pool_counts(i)
    ps, invs = [], []
    for g, w in enumerate(POOL_WINDOWS):
        sl = slice(g * POOL_GROUP, (g + 1) * POOL_GROUP)
        cur = u_cur[:, sl]
        xc = jnp.concatenate([u_prev[:, sl], cur], axis=0)
        win = _trailing_sums(xc, g + 1)[ROW_TILE:, :]
        inv = 1.0 / jnp.minimum(pos + 1, w).astype(F32)
        ps.append(win * inv - cur)
        invs.append(inv)
    return ps, invs


def _pool_fwd(e, pw, scale):
    nb = e.shape[0] // ROW_TILE

    def body(uc_ref, up_ref, z_ref, pw_ref, sc_ref, y_ref):
        i = pl.program_id(0)
        u_cur = uc_ref[...].astype(F32)
        u_prev = jnp.where(i == 0, 0.0, up_ref[...].astype(F32))
        ps, _ = _pool_p(u_cur, u_prev, i)
        z = z_ref[...].astype(F32)
        gate = z * _sigmoid(z)
        for g in range(len(POOL_WINDOWS)):
            sl = slice(g * POOL_GROUP, (g + 1) * POOL_GROUP)
            yraw = jnp.dot(ps[g].astype(BF16), pw_ref[g], preferred_element_type=F32)
            y_ref[:, sl] = ((yraw * sc_ref[:, sl]) * gate[:, sl]).astype(BF16)

    blk = (ROW_TILE, POOL_WIDTH)
    return pl.pallas_call(
        body, name="pool_fwd", grid=(nb,),
        in_specs=[pl.BlockSpec(blk, lambda i: (i, 0)), pl.BlockSpec(blk, lambda i: (jnp.maximum(i - 1, 0), 0)),
                  pl.BlockSpec(blk, lambda i: (i, 1)),
                  pl.BlockSpec((len(POOL_WINDOWS), POOL_GROUP, POOL_GROUP), lambda i: (0, 0, 0)),
                  pl.BlockSpec((1, POOL_WIDTH), lambda i: (0, 0))],
        out_specs=pl.BlockSpec(blk, lambda i: (i, 0)),
        out_shape=jax.ShapeDtypeStruct((e.shape[0], POOL_WIDTH), BF16),
        compiler_params=_params(("parallel",)),
    )(e, e, e, pw, scale)


def _stack_heads(a):
    first = lax.broadcasted_iota(jnp.int32, a.shape, 1) < HEAD_DIM
    zero = jnp.zeros_like(a)
    return jnp.concatenate([jnp.where(first, a, zero), jnp.where(first, zero, a)], axis=0)


def _unstack_heads(a):
    rows = a.shape[0] // 2
    first = lax.broadcasted_iota(jnp.int32, (rows, LANES), 1) < HEAD_DIM
    return jnp.where(first, a[:rows], a[rows:])


def _causal(i, jj, stacked, kv_tile):
    r = lax.broadcasted_iota(jnp.int32, (stacked * ROW_TILE, kv_tile), 0)
    if stacked == 2:
        r = jnp.where(r >= ROW_TILE, r - ROW_TILE, r)
    kidx = jj * kv_tile + lax.broadcasted_iota(jnp.int32, (stacked * ROW_TILE, kv_tile), 1)
    return kidx <= i * ROW_TILE + r


def _stack_rows(b):
    n = b.shape[1]
    return jnp.concatenate([jnp.broadcast_to(b[0:1], (ROW_TILE, n)), jnp.broadcast_to(b[1:2], (ROW_TILE, n))], axis=0)


def _attn_fwd(qkv, bias, nb):
    lk = qkv.shape[0]
    lp = nb * ROW_TILE
    nkb = lk // KV_TILE
    n_pairs = N_HEADS // 2

    def body(q_ref, k_ref, v_ref, b_ref, o_ref, lse_ref):
        i = pl.program_id(1)
        qs = _stack_heads(q_ref[...])
        last = (i * ROW_TILE) // KV_TILE

        def block(jj, carry, masked):
            m, l, acc = carry
            rows = pl.ds(pl.multiple_of(jj * KV_TILE, KV_TILE), KV_TILE)
            s = _dot_nt(qs, k_ref[rows, :]) - _stack_rows(b_ref[0, 0, jj])
            if masked:
                s = jnp.where(_causal(i, jj, 2, KV_TILE), s, NEG)
            m_new = jnp.maximum(m, jnp.max(s, axis=1, keepdims=True))
            alpha = jnp.exp(m - m_new)
            p = jnp.exp(s - m_new)
            l = alpha * l + jnp.sum(p, axis=1, keepdims=True)
            acc = alpha * acc + jnp.dot(p.astype(BF16), v_ref[rows, :], preferred_element_type=F32)
            return m_new, l, acc

        init = (jnp.full((2 * ROW_TILE, 1), NEG, F32), jnp.zeros((2 * ROW_TILE, 1), F32),
                jnp.zeros((2 * ROW_TILE, LANES), F32))
        carry = lax.fori_loop(0, last, lambda jj, c: block(jj, c, False), init)
        m, l, acc = block(last, carry, True)
        o_ref[...] = _unstack_heads(acc / l)
        lse_ref[...] = _unstack_heads(jnp.broadcast_to(m + jnp.log(l), (2 * ROW_TILE, LANES)))

    return pl.pallas_call(
        body, name="attn_fwd", grid=(n_pairs, nb),
        in_specs=[pl.BlockSpec((ROW_TILE, LANES), lambda hp, i: (i, hp)),
                  pl.BlockSpec((lk, LANES), lambda hp, i: (0, n_pairs + hp)),
                  pl.BlockSpec((lk, LANES), lambda hp, i: (0, 2 * n_pairs + hp)),
                  pl.BlockSpec((1, 1, nkb, 2, KV_TILE), lambda hp, i: (i, hp, 0, 0, 0))],
        out_specs=[pl.BlockSpec((ROW_TILE, LANES), lambda hp, i: (i, hp)),
                   pl.BlockSpec((ROW_TILE, LANES), lambda hp, i: (i, hp))],
        out_shape=[jax.ShapeDtypeStruct((lp, ATTN_WIDTH), F32), jax.ShapeDtypeStruct((lp, ATTN_WIDTH), F32)],
        compiler_params=_params(("parallel", "parallel")),
    )(qkv, qkv, qkv, bias)


def _merge_fwd(y_pool, o, e, wup_p, wup_a):
    lp = o.shape[0]
    nb = lp // ROW_TILE

    def body(yp_ref, o_ref, e_ref, wp_ref, wa_ref, mg_ref, ya_ref):
        za = e_ref[:, E_ZA:E_GP].astype(F32)
        ya = (o_ref[...] * (za * _sigmoid(za))).astype(BF16)
        ya_ref[...] = ya
        a_pool = jnp.dot(yp_ref[...], wp_ref[...], preferred_element_type=F32)
        a_attn = jnp.dot(ya, wa_ref[...], preferred_element_type=F32)
        mg_ref[...] = (_sigmoid(e_ref[:, E_GP:E_GA].astype(F32)) * a_pool
                       + _sigmoid(e_ref[:, E_GA:E_COLS].astype(F32)) * a_attn).astype(BF16)

    return pl.pallas_call(
        body, name="merge_fwd", grid=(nb,),
        in_specs=[pl.BlockSpec((ROW_TILE, POOL_WIDTH), lambda i: (i, 0)),
                  pl.BlockSpec((ROW_TILE, ATTN_WIDTH), lambda i: (i, 0)),
                  pl.BlockSpec((ROW_TILE, E_COLS), lambda i: (i, 0)),
                  pl.BlockSpec((POOL_WIDTH, D_MODEL), lambda i: (0, 0)),
                  pl.BlockSpec((ATTN_WIDTH, D_MODEL), lambda i: (0, 0))],
        out_specs=[pl.BlockSpec((ROW_TILE, D_MODEL), lambda i: (i, 0)),
                   pl.BlockSpec((ROW_TILE, ATTN_WIDTH), lambda i: (i, 0))],
        out_shape=[jax.ShapeDtypeStruct((lp, D_MODEL), BF16), jax.ShapeDtypeStruct((lp, ATTN_WIDTH), BF16)],
        compiler_params=_params(("parallel",)),
    )(y_pool, o, e, wup_p, wup_a)


def _head_fwd_bwd(merged, w_out, x2, metapad, gf, target):
    lp = merged.shape[0]
    nb = lp // ROW_TILE

    def body(mg_ref, w_ref, x_ref, mp_ref, g_ref, t_ref, dh_ref, loss_ref, dg_ref):
        i = pl.program_id(0)

        @pl.when(i == 0)
        def _():
            loss_ref[...] = jnp.zeros_like(loss_ref)
            dg_ref[...] = jnp.zeros_like(dg_ref)

        h0 = jnp.where(i == 0, mp_ref[...], x_ref[...])
        h1 = h0 + jnp.dot(mg_ref[...], w_ref[...], preferred_element_type=F32)
        r = lax.rsqrt(jnp.mean(h1 * h1, axis=-1, keepdims=True) + RMS_EPS)
        xhat = h1 * r
        g = g_ref[...]
        err = jnp.where(i == 0, 0.0, xhat * g - t_ref[...])
        loss_ref[...] += 0.5 * jnp.sum(jnp.mean(err * err, axis=-1, keepdims=True))
        dy = err / D_MODEL
        dg_ref[...] += jnp.sum(dy * xhat, axis=0, keepdims=True)
        dxhat = dy * g
        dh_ref[...] = r * (dxhat - xhat * jnp.mean(dxhat * xhat, axis=-1, keepdims=True))

    return pl.pallas_call(
        body, name="head_fwd_bwd", grid=(nb,),
        in_specs=[pl.BlockSpec((ROW_TILE, D_MODEL), lambda i: (i, 0)),
                  pl.BlockSpec((D_MODEL, D_MODEL), lambda i: (0, 0)),
                  _tokens_spec(), pl.BlockSpec((ROW_TILE, D_MODEL), lambda i: (0, 0)),
                  pl.BlockSpec((1, D_MODEL), lambda i: (0, 0)), _tokens_spec()],
        out_specs=[pl.BlockSpec((ROW_TILE, D_MODEL), lambda i: (i, 0)),
                   pl.BlockSpec((1, LANES), lambda i: (0, 0)), pl.BlockSpec((1, D_MODEL), lambda i: (0, 0))],
        out_shape=[jax.ShapeDtypeStruct((lp, D_MODEL), F32), jax.ShapeDtypeStruct((1, LANES), F32),
                   jax.ShapeDtypeStruct((1, D_MODEL), F32)],
        compiler_params=_params(("arbitrary",)),
    )(merged, w_out, x2, metapad, gf, target)


def _per_head_rowsum(t):
    head = lax.broadcasted_iota(jnp.int32, t.shape, 1) // HEAD_DIM
    out = jnp.zeros_like(t)
    for h in range(N_HEADS):
        sel = head == h
        out = jnp.where(sel, jnp.sum(jnp.where(sel, t, 0.0), axis=1, keepdims=True), out)
    return out


def _merge_bwd(dh1, w_out, y_pool, y_attn, wup_p, wup_a, e, o):
    lp = o.shape[0]
    nb = lp // ROW_TILE

    def body(dh_ref, wo_ref, yp_ref, ya_ref, wp_ref, wa_ref, e_ref, o_ref,
             dap_ref, daa_ref, dg_ref, dza_ref, do_ref, delta_ref, dyp_ref):
        dmerged = _dot_nt(dh_ref[...].astype(BF16), wo_ref[...])
        a_pool = jnp.dot(yp_ref[...], wp_ref[...], preferred_element_type=F32)
        a_attn = jnp.dot(ya_ref[...], wa_ref[...], preferred_element_type=F32)
        sp = _sigmoid(e_ref[:, E_GP:E_GA].astype(F32))
        sa = _sigmoid(e_ref[:, E_GA:E_COLS].astype(F32))
        dap = (dmerged * sp).astype(BF16)
        daa = (dmerged * sa).astype(BF16)
        dap_ref[...] = dap
        daa_ref[...] = daa
        dg_ref[:, :D_MODEL] = (dmerged * a_pool * (sp * (1.0 - sp))).astype(BF16)
        dg_ref[:, D_MODEL:] = (dmerged * a_attn * (sa * (1.0 - sa))).astype(BF16)
        dyp_ref[...] = _dot_nt(dap, wp_ref[...])
        dya = _dot_nt(daa, wa_ref[...])
        za = e_ref[:, E_ZA:E_GP].astype(F32)
        sz = _sigmoid(za)
        o = o_ref[...]
        do = dya * (za * sz)
        do_ref[...] = do.astype(BF16)
        dza_ref[...] = (dya * o * (sz * (1.0 + za * (1.0 - sz)))).astype(BF16)
        delta_ref[...] = _per_head_rowsum(do * o)

    row = lambda w: pl.BlockSpec((ROW_TILE, w), lambda i: (i, 0))
    full = lambda a: pl.BlockSpec(a.shape, lambda i: (0, 0))
    return pl.pallas_call(
        body, name="merge_bwd", grid=(nb,),
        in_specs=[row(D_MODEL), full(w_out), row(POOL_WIDTH), row(ATTN_WIDTH), full(wup_p), full(wup_a),
                  row(E_COLS), row(ATTN_WIDTH)],
        out_specs=[row(D_MODEL), row(D_MODEL), row(2 * D_MODEL), row(ATTN_WIDTH), row(ATTN_WIDTH),
                   row(ATTN_WIDTH), row(POOL_WIDTH)],
        out_shape=[jax.ShapeDtypeStruct((lp, D_MODEL), BF16), jax.ShapeDtypeStruct((lp, D_MODEL), BF16),
                   jax.ShapeDtypeStruct((lp, 2 * D_MODEL), BF16), jax.ShapeDtypeStruct((lp, ATTN_WIDTH), BF16),
                   jax.ShapeDtypeStruct((lp, ATTN_WIDTH), BF16), jax.ShapeDtypeStruct((lp, ATTN_WIDTH), F32),
                   jax.ShapeDtypeStruct((lp, POOL_WIDTH), F32)],
        compiler_params=_params(("parallel",)),
    )(dh1, w_out, y_pool, y_attn, wup_p, wup_a, e, o)


def _pool_bwd_local(e, dy_pool, pw, scale):
    lp = e.shape[0]
    nb = lp // ROW_TILE
    ng = len(POOL_WINDOWS)

    def body(uc_ref, up_ref, z_ref, dy_ref, pw_ref, sc_ref, dpc_ref, dz_ref, dsc_ref, dpw_ref):
        i = pl.program_id(0)

        @pl.when(i == 0)
        def _():
            dsc_ref[...] = jnp.zeros_like(dsc_ref)
            dpw_ref[...] = jnp.zeros_like(dpw_ref)

        u_cur = uc_ref[...].astype(F32)
        u_prev = jnp.where(i == 0, 0.0, up_ref[...].astype(F32))
        ps, invs = _pool_p(u_cur, u_prev, i)
        z = z_ref[...].astype(F32)
        sz = _sigmoid(z)
        dy = dy_ref[...]
        dypre = dy * (z * sz)
        dsilu = sz * (1.0 + z * (1.0 - sz))
        for g in range(ng):
            sl = slice(g * POOL_GROUP, (g + 1) * POOL_GROUP)
            pb = ps[g].astype(BF16)
            w = pw_ref[g]
            yraw = jnp.dot(pb, w, preferred_element_type=F32)
            sc = sc_ref[:, sl]
            dz_ref[:, sl] = (dy[:, sl] * (yraw * sc) * dsilu[:, sl]).astype(BF16)
            dsc_ref[:, sl] += jnp.sum(dypre[:, sl] * yraw, axis=0, keepdims=True)
            dyraw = (dypre[:, sl] * sc).astype(BF16)
            dpw_ref[g] += _dot_tn(pb, dyraw)
            dpc_ref[:, sl] = _dot_nt(dyraw, w) * invs[g]

    blk = (ROW_TILE, POOL_WIDTH)
    return pl.pallas_call(
        body, name="pool_bwd_local", grid=(nb,),
        in_specs=[pl.BlockSpec(blk, lambda i: (i, 0)), pl.BlockSpec(blk, lambda i: (jnp.maximum(i - 1, 0), 0)),
                  pl.BlockSpec(blk, lambda i: (i, 1)), pl.BlockSpec(blk, lambda i: (i, 0)),
                  pl.BlockSpec((ng, POOL_GROUP, POOL_GROUP), lambda i: (0, 0, 0)),
                  pl.BlockSpec((1, POOL_WIDTH), lambda i: (0, 0))],
        out_specs=[pl.BlockSpec(blk, lambda i: (i, 0)), pl.BlockSpec(blk, lambda i: (i, 0)),
                   pl.BlockSpec((1, POOL_WIDTH), lambda i: (0, 0)),
                   pl.BlockSpec((ng, POOL_GROUP, POOL_GROUP), lambda i: (0, 0, 0))],
        out_shape=[jax.ShapeDtypeStruct((lp, POOL_WIDTH), F32), jax.ShapeDtypeStruct((lp, POOL_WIDTH), BF16),
                   jax.ShapeDtypeStruct((1, POOL_WIDTH), F32),
                   jax.ShapeDtypeStruct((ng, POOL_GROUP, POOL_GROUP), F32)],
        compiler_params=_params(("arbitrary",)),
    )(e, e, e, dy_pool, pw, scale)


def _pool_bwd_window(dpc):
    lp = dpc.shape[0]
    nb = lp // ROW_TILE

    def body(cur_ref, nxt_ref, du_ref):
        i = pl.program_id(0)
        cur = cur_ref[...]
        nxt = jnp.where(i == nb - 1, 0.0, nxt_ref[...])
        pos = _pool_counts(i)
        for g, w in enumerate(POOL_WINDOWS):
            sl = slice(g * POOL_GROUP, (g + 1) * POOL_GROUP)
            xc = jnp.concatenate([cur[:, sl], nxt[:, sl]], axis=0)
            win = _leading_sums(xc, g + 1)[:ROW_TILE, :]
            dp = cur[:, sl] * jnp.minimum(pos + 1, w).astype(F32)
            du_ref[:, sl] = (win - dp).astype(BF16)

    blk = (ROW_TILE, POOL_WIDTH)
    return pl.pallas_call(
        body, name="pool_bwd_window", grid=(nb,),
        in_specs=[pl.BlockSpec(blk, lambda i: (i, 0)), pl.BlockSpec(blk, lambda i: (jnp.minimum(i + 1, nb - 1), 0))],
        out_specs=pl.BlockSpec(blk, lambda i: (i, 0)),
        out_shape=jax.ShapeDtypeStruct((lp, POOL_WIDTH), BF16),
        compiler_params=_params(("parallel",)),
    )(dpc, dpc)


def _attn_bwd(qkv, do, lse, delta, bias, nb):
    lk = qkv.shape[0]
    lp = nb * ROW_TILE
    nkb = lk // KV_TILE_BWD
    n_pairs = N_HEADS // 2
    per_kv = KV_TILE_BWD // ROW_TILE

    def body(q_ref, k_ref, v_ref, do_ref, lse_ref, dl_ref, b_ref, dq_ref, dk_ref, dv_ref, dc_ref, dcq_ref):
        jj = pl.program_id(1)

        @pl.when(jj == 0)
        def _():
            dq_ref[...] = jnp.zeros_like(dq_ref)
            dcq_ref[...] = jnp.zeros_like(dcq_ref)

        kb, vb = k_ref[...], v_ref[...]

        def block(i, carry, masked):
            dk_acc, dv_acc, dc_acc = carry
            rows = pl.ds(pl.multiple_of(i * ROW_TILE, ROW_TILE), ROW_TILE)
            qs = _stack_heads(q_ref[rows, :])
            dos = _stack_heads(do_ref[rows, :])
            lse_i, dl_i = lse_ref[rows, :], dl_ref[rows, :]
            s = _dot_nt(qs, kb)
            dp = _dot_nt(dos, vb)
            if masked:
                valid = _causal(i, jj, 1, KV_TILE_BWD)
            ps, dss, dcs, rowsums = [], [], [], []
            for hd in range(2):
                half = slice(hd * ROW_TILE, (hd + 1) * ROW_TILE)
                col = slice(hd * HEAD_DIM, hd * HEAD_DIM + 1)
                sh = s[half] - b_ref[i, 0, 0, hd:hd + 1, :]
                if masked:
                    sh = jnp.where(valid, sh, NEG)
                p = jnp.exp(sh - lse_i[:, col])
                ds = p * (dp[half] - dl_i[:, col])
                ps.append(p.astype(BF16))
                dss.append(ds.astype(BF16))
                dcs.append(jnp.sum(ds, axis=0, keepdims=True))
                rowsums.append(jnp.sum(ds, axis=1, keepdims=True))
            dsb = jnp.concatenate(dss, axis=0)
            dv_acc = dv_acc + _dot_tn(jnp.concatenate(ps, axis=0), dos)
            dk_acc = dk_acc + _dot_tn(dsb, qs)
            dc_acc = dc_acc - jnp.concatenate(dcs, axis=0)
            dq_ref[rows, :] += _unstack_heads(jnp.dot(dsb, kb, preferred_element_type=F32))
            dcq_ref[rows, :] += _unstack_heads(jnp.broadcast_to(jnp.concatenate(rowsums, axis=0), (2 * ROW_TILE, LANES)))
            return dk_acc, dv_acc, dc_acc

        init = (jnp.zeros((KV_TILE_BWD, LANES), F32), jnp.zeros((KV_TILE_BWD, LANES), F32),
                jnp.zeros((2, KV_TILE_BWD), F32))
        first_q = per_kv * jj
        diag_end = jnp.minimum(first_q + per_kv, nb)
        carry = lax.fori_loop(first_q, diag_end, lambda i, c: block(i, c, True), init)
        dk, dv, dc = lax.fori_loop(diag_end, nb, lambda i, c: block(i, c, False), carry)
        dk_ref[...] = dk.astype(BF16)
        dv_ref[...] = dv.astype(BF16)
        dc_ref[0, 0] = dc

    whole = lambda rows: pl.BlockSpec((rows, LANES), lambda hp, jj: (0, hp))
    kv_blk = lambda off: pl.BlockSpec((KV_TILE_BWD, LANES), lambda hp, jj: (jj, off + hp))
    return pl.pallas_call(
        body, name="attn_bwd", grid=(n_pairs, nkb),
        in_specs=[whole(lk), kv_blk(n_pairs), kv_blk(2 * n_pairs), whole(lp), whole(lp), whole(lp),
                  pl.BlockSpec((nb, 1, 1, 2, KV_TILE_BWD), lambda hp, jj: (0, hp, jj, 0, 0))],
        out_specs=[whole(lp), kv_blk(0), kv_blk(0),
                   pl.BlockSpec((1, 1, 2, KV_TILE_BWD), lambda hp, jj: (hp, jj, 0, 0)), whole(lp)],
        out_shape=[jax.ShapeDtypeStruct((lp, ATTN_WIDTH), F32), jax.ShapeDtypeStruct((lk, ATTN_WIDTH), BF16),
                   jax.ShapeDtypeStruct((lk, ATTN_WIDTH), BF16),
                   jax.ShapeDtypeStruct((n_pairs, nkb, 2, KV_TILE_BWD), F32),
                   jax.ShapeDtypeStruct((lp, ATTN_WIDTH), F32)],
        compiler_params=_params(("parallel", "arbitrary")),
    )(qkv, qkv, qkv, do, lse, delta, bias)


def _gates_bwd(dc, dcq, f, bfg):
    nb = f.shape[0] // ROW_TILE

    def body(dc_ref, dcq_ref, f_ref, b_ref, df_ref, db_ref, carry):
        step = pl.program_id(0)
        i = nb - 1 - step

        @pl.when(step == 0)
        def _():
            carry[...] = jnp.zeros_like(carry)
            db_ref[...] = jnp.zeros_like(db_ref)

        dcb = dc_ref[...]
        lane = lax.broadcasted_iota(jnp.int32, (ROW_TILE, F_COLS), 1)
        for h in range(N_HEADS):
            dcb = dcb + jnp.where(lane == h, dcq_ref[:, HEAD_DIM * h:HEAD_DIM * h + 1], 0.0)
        r_i = lax.broadcasted_iota(jnp.int32, (ROW_TILE, ROW_TILE), 0)
        c_i = lax.broadcasted_iota(jnp.int32, (ROW_TILE, ROW_TILE), 1)
        upper = (c_i >= r_i).astype(F32)
        dlf = jnp.dot(upper, dcb, precision=HIGHEST, preferred_element_type=F32) + carry[...]
        carry[...] = carry[...] + jnp.sum(dcb, axis=0, keepdims=True)
        logit = f_ref[...] + b_ref[...]
        dlogit = jnp.where(_valid_gate_mask(i), dlf * _sigmoid(-logit), 0.0)
        df_ref[...] = dlogit.astype(BF16)
        db_ref[...] += jnp.sum(dlogit, axis=0, keepdims=True)

    blk = pl.BlockSpec((ROW_TILE, F_COLS), lambda s: (nb - 1 - s, 0))
    wide = pl.BlockSpec((ROW_TILE, ATTN_WIDTH), lambda s: (nb - 1 - s, 0))
    one = pl.BlockSpec((1, F_COLS), lambda s: (0, 0))
    return pl.pallas_call(
        body, name="gates_bwd", grid=(nb,),
        in_specs=[blk, wide, blk, one], out_specs=[blk, one],
        out_shape=[jax.ShapeDtypeStruct(f.shape, BF16), jax.ShapeDtypeStruct((1, F_COLS), F32)],
        scratch_shapes=[pltpu.VMEM((1, F_COLS), F32)],
        compiler_params=_params(("arbitrary",)),
    )(dc, dcq, f, bfg)


def _input_bwd(dproj, df, wt_e, wt_qkv, wt_f, x2, metapad, dh1, g1, hs, sc):
    nb = x2.shape[0] // ROW_TILE + 1
    n = len(hs)

    def body(dp_ref, df_ref, we_ref, w_ref, wf_ref, x_ref, mp_ref, dh_ref, g_ref, *rest):
        h_refs, s_ref = rest[:n], rest[n]
        gx_ref, g0_ref, dg_ref = rest[n + 1:n + 4]
        q_refs, sq_ref = rest[n + 4:2 * n + 4], rest[2 * n + 4]
        sems = rest[2 * n + 5:]
        i = pl.program_id(0)

        @pl.when(i == 0)
        def _():
            dg_ref[...] = jnp.zeros_like(dg_ref)
            own, sends, _ = _chip_scatter_plan(h_refs, s_ref, q_refs, sq_ref, *sems)
            for cp in own + sends:
                cp.start()

        dhn = (jnp.dot(dp_ref[:, :E_COLS], we_ref[...], preferred_element_type=F32)
               + jnp.dot(dp_ref[:, E_COLS:], w_ref[...], preferred_element_type=F32)
               + jnp.dot(df_ref[...], wf_ref[...], preferred_element_type=F32))
        h0 = jnp.where(i == 0, mp_ref[...], x_ref[...])
        r = lax.rsqrt(jnp.mean(h0 * h0, axis=-1, keepdims=True) + RMS_EPS)
        xhat = h0 * r
        dg_ref[...] += jnp.sum(dhn * xhat, axis=0, keepdims=True)
        dxhat = dhn * g_ref[...]
        dh0 = dh_ref[...] + r * (dxhat - xhat * jnp.mean(dxhat * xhat, axis=-1, keepdims=True))
        gx_ref[...] = dh0

        @pl.when(i == 0)
        def _():
            g0_ref[...] = dh0

        @pl.when(i == nb - 1)
        def _():
            own, sends, recvs = _chip_scatter_plan(h_refs, s_ref, q_refs, sq_ref, *sems)
            for cp in recvs:
                cp.wait_recv()
            for cp in sends:
                cp.wait_send()
            for cp in own:
                cp.wait()

    const = lambda shape: pl.BlockSpec(shape, lambda i: (0, 0))
    res = pl.pallas_call(
        body, name="input_bwd", grid=(nb,),
        in_specs=[pl.BlockSpec((ROW_TILE, MAIN_COLS), lambda i: (i, 0)), pl.BlockSpec((ROW_TILE, F_COLS), lambda i: (i, 0)),
                  const((E_COLS, D_MODEL)), const((QKV_COLS, D_MODEL)), const((F_COLS, D_MODEL)),
                  _tokens_spec(), const((ROW_TILE, D_MODEL)),
                  pl.BlockSpec((ROW_TILE, D_MODEL), lambda i: (i, 0)), const((1, D_MODEL))] + [HBM] * (n + 1),
        out_specs=[_tokens_spec(), const((ROW_TILE, D_MODEL)), const((1, D_MODEL))] + [HBM] * (n + 1),
        out_shape=[jax.ShapeDtypeStruct(x2.shape, F32), jax.ShapeDtypeStruct((ROW_TILE, D_MODEL), F32),
                   jax.ShapeDtypeStruct((1, D_MODEL), F32)]
        + [jax.ShapeDtypeStruct(h.shape, h.dtype) for h in hs] + [jax.ShapeDtypeStruct((N_CHIPS,) + sc.shape, sc.dtype)],
        scratch_shapes=[pltpu.SemaphoreType.DMA((3 * (n + 1),)), pltpu.SemaphoreType.DMA((3 * (n + 1),)),
                        pltpu.SemaphoreType.DMA((n + 1,))],
        compiler_params=_params(("arbitrary",), vmem=56 * 1024 * 1024),
    )(dproj, df, wt_e, wt_qkv, wt_f, x2, metapad, dh1, g1, *_in_hbm(*hs, sc))
    return res[:3], res[3:3 + n], res[3 + n]


def _adamw(w, g, m, v, name):
    rows, cols = w.shape
    if rows % 8 == 0:
        tr, tc = _row_tile8(rows), cols
    else:
        tr, tc = rows, (2 * LANES if cols % (2 * LANES) == 0 and rows > 8 else cols)

    def body(w_ref, g_ref, m_ref, v_ref, d_ref, mo_ref, vo_ref):
        g_ = g_ref[...]
        m_new = ADAM_B1 * m_ref[...] + (1.0 - ADAM_B1) * g_
        v_new = ADAM_B2 * v_ref[...] + (1.0 - ADAM_B2) * (g_ * g_)
        m_hat = m_new / (1.0 - ADAM_B1 ** ADAM_STEP)
        v_hat = v_new / (1.0 - ADAM_B2 ** ADAM_STEP)
        d_ref[...] = -ADAM_LR * (m_hat / (jnp.sqrt(v_hat) + ADAM_EPS) + ADAM_WD * w_ref[...])
        mo_ref[...] = m_new
        vo_ref[...] = v_new

    blk = pl.BlockSpec((tr, tc), lambda i, j: (i, j))
    return pl.pallas_call(
        body, name=name, grid=(rows // tr, cols // tc),
        in_specs=[blk] * 4, out_specs=[blk] * 3,
        out_shape=[jax.ShapeDtypeStruct(w.shape, F32)] * 3,
        compiler_params=_params(("parallel", "parallel")),
    )(w, g, m, v)


def _adamw_native(w3, g3, m3, v3, name):
    rows = w3.shape[0]
    tr = rows // 2
    blk = pl.BlockSpec((tr,) + w3.shape[1:], lambda i: (i, 0, 0))
    shape = jax.ShapeDtypeStruct(w3.shape, F32)

    def moments(g_ref, m_ref, v_ref, mo_ref, vo_ref):
        g_ = g_ref[...]
        mo_ref[...] = ADAM_B1 * m_ref[...] + (1.0 - ADAM_B1) * g_
        vo_ref[...] = ADAM_B2 * v_ref[...] + (1.0 - ADAM_B2) * (g_ * g_)

    new_m, new_v = pl.pallas_call(
        moments, name=name + "_moments", grid=(2,), in_specs=[blk] * 3, out_specs=[blk] * 2, out_shape=[shape] * 2,
        compiler_params=_params(("parallel",)),
    )(g3, m3, v3)

    def delta(w_ref, m_ref, v_ref, d_ref):
        m_hat = m_ref[...] / (1.0 - ADAM_B1 ** ADAM_STEP)
        v_hat = v_ref[...] / (1.0 - ADAM_B2 ** ADAM_STEP)
        d_ref[...] = -ADAM_LR * (m_hat / (jnp.sqrt(v_hat) + ADAM_EPS) + ADAM_WD * w_ref[...])

    d = pl.pallas_call(
        delta, name=name + "_delta", grid=(2,), in_specs=[blk] * 3, out_specs=blk, out_shape=shape,
        compiler_params=_params(("parallel",)),
    )(w3, new_m, new_v)
    return d, new_m, new_v


def _row_tile8(rows):
    best = rows
    for t in range(8, 257, 8):
        if rows % t == 0:
            best = t
    return best


def kernel(x, meta_tokens, norm_g, w_in, b_forget, pool_w, pool_scale, w_up_pool, w_up_attn, w_out, final_norm_g, loss_target, m_meta_tokens, m_norm_g, m_w_in, m_b_forget, m_pool_w, m_pool_scale, m_w_up_pool, m_w_up_attn, m_w_out, m_final_norm_g, v_meta_tokens, v_norm_g, v_w_in, v_b_forget, v_pool_w, v_pool_scale, v_w_up_pool, v_w_up_attn, v_w_out, v_final_norm_g):
    seq = x.shape[1]
    assert seq % ROW_TILE == 0 and x.shape[0] == 1
    lp = seq + ROW_TILE
    nb = lp // ROW_TILE
    lk = -(-lp // KV_TILE_BWD) * KV_TILE_BWD
    core = jnp.reshape(lax.axis_index("c"), (1,)).astype(jnp.int32)
    x2 = x[0]
    target = loss_target[0]
    sh_d = D_MODEL // N_CHIPS

    w_in_t = jnp.transpose(w_in[0])
    gf = final_norm_g.reshape(1, D_MODEL)
    bfg = jnp.pad(b_forget, ((0, 0), (0, F_COLS - N_HEADS)))
    pw_b = pool_w[0].astype(BF16)
    hn, (wg_in, meta_g) = _rmsnorm_fwd(x2, norm_g, lk, [w_in_t.astype(BF16), meta_tokens], [1, 0])
    wt = wg_in.reshape(-1, D_MODEL)
    wt_e = jnp.concatenate([wt[REF_U:REF_Q], wt[REF_ZA:REF_F], wt[REF_GP:]], axis=0)
    wt_qkv = wt[REF_Q:REF_ZA]
    wt_f = jnp.pad(wt[REF_F:REF_GP], ((0, F_COLS - N_HEADS), (0, 0)))
    meta_full = jnp.transpose(meta_g, (1, 0, 2)).reshape(N_META, D_MODEL)
    metapad = jnp.pad(meta_full, ((PAD_ROWS, 0), (0, 0)))

    tm = _row_tile(lp, 1100)
    e, (wg_up_p, wg_up_a, wg_out) = _mm_nt(
        hn, wt_e, BF16, "in_proj_gates", lp, E_COLS, tm, 512,
        gather=([w_up_pool[0].astype(BF16), w_up_attn[0].astype(BF16), w_out[0].astype(BF16)], [0, 0, 0]))
    wup_p = jnp.transpose(wg_up_p, (1, 0, 2)).reshape(POOL_WIDTH, D_MODEL)
    wup_a = jnp.transpose(wg_up_a, (1, 0, 2)).reshape(ATTN_WIDTH, D_MODEL)
    wout = wg_out.reshape(D_MODEL, D_MODEL)
    qkv = _mm_nt(hn, wt_qkv, BF16, "in_proj_qkv", lk, QKV_COLS, _row_tile(lk, 1300), 512, scale_first=HEAD_DIM ** -0.5)
    f = _mm_nt(hn, wt_f, F32, "in_proj_forget", lp, F_COLS, tm, F_COLS)
    c = _gates_fwd(f, bfg)
    c_t = jnp.transpose(c[:, :N_HEADS])
    bias = c_t[None, :, :] - jnp.transpose(c_t[:, ::ROW_TILE])[:, :, None]
    bias = jnp.where(jnp.arange(lp) < PAD_ROWS, -NEG, bias)
    bias = jnp.pad(bias, ((0, 0), (0, 0), (0, lk - lp)))
    by_kv = lambda t: jnp.transpose(bias.reshape(nb, N_HEADS // 2, 2, lk // t, t), (0, 1, 3, 2, 4))
    bias, bias_bwd = by_kv(KV_TILE), by_kv(KV_TILE_BWD)
    y_pool = _pool_fwd(e, pw_b, pool_scale)
    o, lse = _attn_fwd(qkv, bias, nb)
    merged, y_attn = _merge_fwd(y_pool, o, e, wup_p, wup_a)
    dh1, loss_part, dgf = _head_fwd_bwd(merged, wout, x2, metapad, gf, target)

    dap, daa, dgate, dza, do, delta, dy_pool = _merge_bwd(dh1, wout, y_pool, y_attn, wup_p, wup_a, e, o)
    dpc, dzp, dscale, dpw = _pool_bwd_local(e, dy_pool, pw_b, pool_scale)
    du = _pool_bwd_window(dpc)
    dq, dk, dv, dc4, dcq = _attn_bwd(qkv, do, lse, delta, bias_bwd, nb)
    dc = jnp.transpose(dc4, (1, 3, 0, 2)).reshape(lk, N_HEADS)[:lp]
    df, db = _gates_bwd(jnp.pad(dc, ((0, 0), (0, F_COLS - N_HEADS))), dcq, f, bfg)
    dproj = jnp.concatenate([du, dzp, dza, dgate, (dq * HEAD_DIM ** -0.5).astype(BF16), dk[:lp], dv[:lp]], axis=1)
    tk = _row_tile(lp, 1100)
    dw_out = _mm_tn(merged, dh1, "grad_w_out", lp, 512, D_MODEL, tk)
    dw_up_p = _mm_tn(y_pool, dap, "grad_w_up_pool", lp, 512, D_MODEL, tk, chunks=N_CHIPS)
    dw_up_a = _mm_tn(y_attn, daa, "grad_w_up_attn", lp, 512, D_MODEL, tk, chunks=N_CHIPS)
    dwt_f = _mm_tn(df, hn, "grad_w_in_forget", lp, F_COLS, D_MODEL, tk)

    def pad8(a):
        return jnp.pad(a, ((0, (-a.shape[0]) % 8), (0, 0)))

    small_parts = [pad8(a) for a in (dgf.reshape(-1, LANES), dscale.reshape(-1, LANES), db, dpw.reshape(-1, LANES),
                                     loss_part)]
    small = jnp.concatenate(small_parts, axis=0)
    soffs = [0]
    for p in small_parts:
        soffs.append(soffs[-1] + p.shape[0])

    gs_rows = [dw_up_p, dw_up_a, dw_out.reshape(N_CHIPS, sh_d, D_MODEL)]
    half = MAIN_COLS // 2
    dwt_a, (*rb_rows, rb_f, sr) = _mm_tn(dproj, hn, "grad_w_in_a", lp, half // 2, D_MODEL, tk, m=half, m_off=0,
                                         swap=(gs_rows + [dwt_f, small], [1, 1, 1, 1, None]))
    dwt_b, (rb_a,) = _mm_tn(dproj, hn, "grad_w_in_b", lp, half // 2, D_MODEL, tk, m=half, m_off=2, swap=([dwt_a], [1]))
    (rb_b,) = _sibling_exchange([dwt_b], [1])
    *hs_rows, sc = _add_sibling_half(gs_rows, rb_rows, [1, 1, 1], small, sr, core)
    h_a, h_b, h_f = _add_halves_2d([dwt_a, dwt_b, dwt_f], [rb_a, rb_b, rb_f], core)
    h_in = jnp.concatenate([h_a[E_U:E_ZA], h_b[D_MODEL:], h_a[E_ZA:E_GP], h_f[:N_HEADS], h_a[E_GP:], h_b[:D_MODEL]], axis=0)
    hs = [h_in.reshape(N_CHIPS, -1, D_MODEL // 2)] + hs_rows
    grad_axes = [2, 1, 1, 1]
    (grad_x, g_block0, dg1), qs, sq = _input_bwd(dproj, df, wt_e, wt_qkv, wt_f, x2, metapad, dh1, norm_g, hs, sc)
    g_w_in_t, g_w_up_p, g_w_up_a, g_w_out, st = _reduce_allgather(qs, grad_axes, sq)
    late = _small_allreduce(jnp.concatenate([pad8(dg1.reshape(-1, LANES)), g_block0[PAD_ROWS:].reshape(-1, LANES)], axis=0))
    n_norm = D_MODEL // LANES
    g_norm = late[:n_norm].reshape(1, D_MODEL)
    chip = 2 * lax.axis_index("x") + lax.axis_index("y")
    g_meta = lax.dynamic_slice_in_dim(late[n_norm:].reshape(N_META, D_MODEL), chip * sh_d, sh_d, axis=1)

    spiece = lambda k, rows: st[soffs[k]:soffs[k] + rows]
    g_final = spiece(0, D_MODEL // LANES).reshape(1, D_MODEL)
    g_scale = spiece(1, POOL_WIDTH // LANES).reshape(1, POOL_WIDTH)
    g_bf = spiece(2, 1)
    g_pw = spiece(3, POOL_WIDTH)
    loss = st[soffs[4], 0]

    def pad_lanes(a):
        return jnp.pad(a, ((0, 0), (0, F_COLS - N_HEADS)))

    to_rows = lambda a: jnp.transpose(a, (2, 0, 1))
    from_rows = lambda a: jnp.transpose(a, (1, 2, 0))
    g_w_in_rows = g_w_in_t.reshape(-1, 1, D_MODEL)
    w_in_res = (g_w_in_rows,) + _adamw_native(to_rows(w_in), g_w_in_rows, to_rows(m_w_in), to_rows(v_w_in), "adamw_w_in")

    upd = [
        ("meta_tokens", meta_tokens, g_meta, m_meta_tokens, v_meta_tokens),
        ("norm_g", norm_g, g_norm, m_norm_g, v_norm_g),
        ("w_in", None, None, None, None),
        ("b_forget", pad_lanes(b_forget), g_bf, pad_lanes(m_b_forget), pad_lanes(v_b_forget)),
        ("pool_w", pool_w.reshape(-1, LANES), g_pw, m_pool_w.reshape(-1, LANES), v_pool_w.reshape(-1, LANES)),
        ("pool_scale", pool_scale, g_scale, m_pool_scale, v_pool_scale),
        ("w_up_pool", w_up_pool[0], g_w_up_p, m_w_up_pool[0], v_w_up_pool[0]),
        ("w_up_attn", w_up_attn[0], g_w_up_a, m_w_up_attn[0], v_w_up_attn[0]),
        ("w_out", w_out[0], g_w_out, m_w_out[0], v_w_out[0]),
        ("final_norm_g", gf, g_final, m_final_norm_g.reshape(1, D_MODEL), v_final_norm_g.reshape(1, D_MODEL)),
    ]
    shapes = [meta_tokens.shape, norm_g.shape, w_in.shape, b_forget.shape, pool_w.shape, pool_scale.shape,
              w_up_pool.shape, w_up_attn.shape, w_out.shape, final_norm_g.shape]
    grads, deltas, new_ms, new_vs = [], [], [], []
    for (name, w_, g_, m_in, v_in), shp in zip(upd, shapes):
        if name == "w_in":
            res = tuple(from_rows(a) for a in w_in_res)
        else:
            d_, mn_, vn_ = _adamw(w_, g_, m_in, v_in, "adamw_" + name)
            res = (g_, d_, mn_, vn_)
        if name == "b_forget":
            res = tuple(a[:, :N_HEADS] for a in res)
        for lst, a in zip((grads, deltas, new_ms, new_vs), res):
            lst.append(a.reshape(shp))

    return (loss, grad_x.reshape(x.shape), *grads, *deltas, *new_ms, *new_vs)
```

```python
import jax
import jax.numpy as jnp
from jax import lax
from jax.experimental import pallas as pl
from jax.experimental.pallas import tpu as pltpu

F32 = jnp.float32
BF16 = jnp.bfloat16
MESH = pl.DeviceIdType.MESH
HIGHEST = lax.Precision.HIGHEST
HBM = pl.BlockSpec(memory_space=pltpu.HBM)

D_MODEL = 1024
N_META = 16
POOL_WIDTH = 512
POOL_GROUP = 128
POOL_WINDOWS = (2, 4, 8, 16)
N_HEADS = 8
HEAD_DIM = 64
ATTN_WIDTH = 512
RMS_EPS = 1e-6
N_CHIPS = 4

ADAM_LR = 0.001
ADAM_B1 = 0.9
ADAM_B2 = 0.999
ADAM_EPS = 1e-08
ADAM_WD = 0.01
ADAM_STEP = 10

LANES = 128
ROW_TILE = 256
KV_TILE = 512
KV_TILE_BWD = 1024
PAD_ROWS = ROW_TILE - N_META
NEG = -1e30

E_U, E_ZP, E_ZA, E_GP, E_GA, E_COLS = 0, 512, 1024, 1536, 2560, 3584
QKV_COLS = 3 * ATTN_WIDTH
MAIN_COLS = E_COLS + QKV_COLS
F_COLS = LANES
REF_U, REF_Q, REF_ZA, REF_F, REF_GP = 0, 1024, 2560, 3072, 3080
VMEM_LIMIT = 48 * 1024 * 1024


def _params(sem=None, vmem=VMEM_LIMIT):
    return pltpu.CompilerParams(dimension_semantics=sem, vmem_limit_bytes=vmem)


def _in_hbm(*arrays):
    return [pltpu.with_memory_space_constraint(a, pltpu.HBM) for a in arrays]


def _row_tile(n, target):
    best = 16
    for t in range(16, target + 1, 16):
        if n % t == 0:
            best = t
    return best


def _sigmoid(x):
    return 1.0 / (1.0 + jnp.exp(-x))


def _half(ref, axis, which):
    n = ref.shape[axis] // 2
    idx = [slice(None)] * len(ref.shape)
    idx[axis] = pl.ds(pl.multiple_of(which * n, n), n)
    return ref.at[tuple(idx)]


def _half_shape(shape, axis):
    s = list(shape)
    s[axis] //= 2
    return tuple(s)


def _gather_start(ins, outs, axes, send, recv, fsend, frecv, local):
    x, y, c = lax.axis_index("x"), lax.axis_index("y"), lax.axis_index("c")
    me = 2 * x + y
    for t in range(len(ins)):
        pltpu.make_async_copy(ins[t], outs[t].at[me], local.at[t]).start()
        for k, chip in enumerate([(1 - x, y), (x, 1 - y), (1 - x, 1 - y)]):
            pltpu.make_async_remote_copy(
                src_ref=_half(ins[t], axes[t], c), dst_ref=_half(outs[t].at[me], axes[t], c),
                send_sem=send.at[3 * t + k], recv_sem=recv.at[3 * t + k], device_id=(*chip, c), device_id_type=MESH).start()


def _gather_finish(ins, outs, axes, send, recv, fsend, frecv, local):
    x, y, c = lax.axis_index("x"), lax.axis_index("y"), lax.axis_index("c")
    me = 2 * x + y
    sibling = (x, y, 1 - c)
    chips = [(1 - x, y), (x, 1 - y), (1 - x, 1 - y)]
    n = len(ins)

    def over_ici(t, k, chip, dst_slot):
        return pltpu.make_async_remote_copy(
            src_ref=_half(ins[t], axes[t], c), dst_ref=_half(outs[t].at[dst_slot], axes[t], c),
            send_sem=send.at[3 * t + k], recv_sem=recv.at[3 * t + k], device_id=(*chip, c), device_id_type=MESH)

    def to_sibling(t, k, slot, which):
        return pltpu.make_async_remote_copy(
            src_ref=_half(outs[t].at[slot], axes[t], which), dst_ref=_half(outs[t].at[slot], axes[t], which),
            send_sem=fsend.at[3 * t + k], recv_sem=frecv.at[3 * t + k], device_id=sibling, device_id_type=MESH)

    forwards = []
    for t in range(n):
        for k, (px, py) in enumerate(chips):
            over_ici(t, k, (px, py), 2 * px + py).wait_recv()
            fw = to_sibling(t, k, 2 * px + py, c)
            fw.start()
            forwards.append(fw)
    for t in range(n):
        for k, (px, py) in enumerate(chips):
            to_sibling(t, k, 2 * px + py, 1 - c).wait_recv()
    for t in range(n):
        for k, chip in enumerate(chips):
            over_ici(t, k, chip, me).wait_send()
    for fw in forwards:
        fw.wait_send()
    for t in range(n):
        pltpu.make_async_copy(ins[t], outs[t].at[me], local.at[t]).wait()


def _gather_sems(n):
    return [pltpu.SemaphoreType.DMA((3 * n,)), pltpu.SemaphoreType.DMA((3 * n,)), pltpu.SemaphoreType.DMA((3 * n,)),
            pltpu.SemaphoreType.DMA((3 * n,)), pltpu.SemaphoreType.DMA((n,))]


def _gathered_shapes(shards):
    return [jax.ShapeDtypeStruct((N_CHIPS,) + s.shape, s.dtype) for s in shards]


def _swap_copies(srcs, dsts, axes, send, recv):
    x, y, c = lax.axis_index("x"), lax.axis_index("y"), lax.axis_index("c")
    return [pltpu.make_async_remote_copy(
        src_ref=srcs[t] if axes[t] is None else _half(srcs[t], axes[t], 1 - c), dst_ref=dsts[t],
        send_sem=send.at[t], recv_sem=recv.at[t], device_id=(x, y, 1 - c), device_id_type=MESH) for t in range(len(srcs))]


def _swap_shapes(srcs, axes):
    return [jax.ShapeDtypeStruct(g.shape if a is None else _half_shape(g.shape, a), g.dtype) for g, a in zip(srcs, axes)]


def _sibling_exchange(gs, axes):
    n = len(gs)

    def body(*refs):
        cps = _swap_copies(refs[:n], refs[n:2 * n], axes, *refs[2 * n:])
        for cp in cps:
            cp.start()
        for cp in cps:
            cp.wait()

    return pl.pallas_call(
        body, name="grad_sibling_exchange",
        in_specs=[HBM] * n, out_specs=[HBM] * n, out_shape=_swap_shapes(gs, axes),
        scratch_shapes=[pltpu.SemaphoreType.DMA((n,)), pltpu.SemaphoreType.DMA((n,))],
    )(*_in_hbm(*gs))


def _add_halves_2d(parts, rbs, core):
    n = len(parts)
    rows, w = rbs[0].shape
    tr = 512
    last_rows = rbs[-1].shape[0]

    def body(core_ref, *refs):
        p_refs, r_refs, h_refs = refs[:n], refs[n:2 * n], refs[2 * n:]
        for t in range(n - 1):
            h_refs[t][...] = (p_refs[t][...] + r_refs[t][...]).astype(BF16)

        @pl.when(pl.program_id(0) == 0)
        def _():
            h_refs[n - 1][...] = (p_refs[n - 1][...] + r_refs[n - 1][...]).astype(BF16)

    mine = lambda r: pl.BlockSpec((r, w), lambda i, cr: (i, cr[0]))
    tile = lambda r: pl.BlockSpec((r, w), lambda i, cr: (i, 0))
    mine_small = pl.BlockSpec((last_rows, w), lambda i, cr: (0, cr[0]))
    tile_small = pl.BlockSpec((last_rows, w), lambda i, cr: (0, 0))
    return pl.pallas_call(
        body, name="grad_add_sibling_w_in",
        grid_spec=pltpu.PrefetchScalarGridSpec(
            num_scalar_prefetch=1, grid=(rows // tr,),
            in_specs=[mine(tr)] * (n - 1) + [mine_small] + [tile(tr)] * (n - 1) + [tile_small],
            out_specs=[tile(tr)] * (n - 1) + [tile_small]),
        out_shape=[jax.ShapeDtypeStruct(r.shape, BF16) for r in rbs],
        compiler_params=_params(("arbitrary",)),
    )(core, *parts, *rbs)


def _add_sibling_half(gs, rbs, axes, small, sr, core):
    n = len(gs)

    def body(core_ref, *refs):
        g_refs, rb_refs = refs[:n], refs[n:2 * n]
        s_ref, sr_ref = refs[2 * n], refs[2 * n + 1]
        h_refs, sc_ref = refs[2 * n + 2:3 * n + 2], refs[3 * n + 2]
        for t in range(n):
            h_refs[t][...] = (g_refs[t][...] + rb_refs[t][...]).astype(BF16)
        sc_ref[...] = s_ref[...] + sr_ref[...]

    def mine(rb, axis):
        blk = (None,) + rb.shape[1:]
        if axis == 2:
            return pl.BlockSpec(blk, lambda j, cr: (j, 0, cr[0]))
        return pl.BlockSpec(blk, lambda j, cr: (j, cr[0], 0))

    chunk = lambda rb: pl.BlockSpec((None,) + rb.shape[1:], lambda j, cr: (j, 0, 0))
    whole = pl.BlockSpec(small.shape, lambda j, cr: (0, 0))
    return pl.pallas_call(
        body, name="grad_add_sibling",
        grid_spec=pltpu.PrefetchScalarGridSpec(
            num_scalar_prefetch=1, grid=(N_CHIPS,),
            in_specs=[mine(rb, a) for rb, a in zip(rbs, axes)] + [chunk(rb) for rb in rbs] + [whole, whole],
            out_specs=[chunk(rb) for rb in rbs] + [whole]),
        out_shape=[jax.ShapeDtypeStruct(rb.shape, BF16) for rb in rbs] + [jax.ShapeDtypeStruct(small.shape, F32)],
        compiler_params=_params(("arbitrary",)),
    )(core, *gs, *rbs, small, sr)


def _chip_scatter_plan(h_refs, s_ref, q_refs, sq_ref, send, recv, local):
    n = len(h_refs)
    x, y, c = lax.axis_index("x"), lax.axis_index("y"), lax.axis_index("c")
    me = 2 * x + y
    chips = [(1 - x, y), (x, 1 - y), (1 - x, 1 - y)]

    def copy(t, k, chip, src_slot, dst_slot):
        src = h_refs[t].at[src_slot] if t < n else s_ref
        dst = (q_refs[t] if t < n else sq_ref).at[dst_slot]
        return pltpu.make_async_remote_copy(src_ref=src, dst_ref=dst, send_sem=send.at[3 * t + k],
                                            recv_sem=recv.at[3 * t + k], device_id=(*chip, c), device_id_type=MESH)

    own = [pltpu.make_async_copy(h_refs[t].at[me], q_refs[t].at[me], local.at[t]) for t in range(n)]
    own.append(pltpu.make_async_copy(s_ref, sq_ref.at[me], local.at[n]))
    sends = [copy(t, k, (px, py), 2 * px + py, me) for t in range(n + 1) for k, (px, py) in enumerate(chips)]
    recvs = [copy(t, k, (px, py), me, 2 * px + py) for t in range(n + 1) for k, (px, py) in enumerate(chips)]
    return own, sends, recvs


def _small_allreduce(buf):
    def body(b_ref, o_ref, sib_buf, chip_buf, send, recv):
        x, y, c = lax.axis_index("x"), lax.axis_index("y"), lax.axis_index("c")
        me = 2 * x + y
        chips = [(1 - x, y), (x, 1 - y), (1 - x, 1 - y)]
        swap = pltpu.make_async_remote_copy(src_ref=b_ref, dst_ref=sib_buf, send_sem=send.at[0], recv_sem=recv.at[0],
                                            device_id=(x, y, 1 - c), device_id_type=MESH)
        swap.start()
        swap.wait()
        chip_buf[me] = b_ref[...] + sib_buf[...]

        def copy(k, chip, slot):
            return pltpu.make_async_remote_copy(src_ref=chip_buf.at[slot], dst_ref=chip_buf.at[slot], send_sem=send.at[1 + k],
                                                recv_sem=recv.at[1 + k], device_id=(*chip, c), device_id_type=MESH)

        sends = [copy(k, chip, me) for k, chip in enumerate(chips)]
        for cp in sends:
            cp.start()
        for k, (px, py) in enumerate(chips):
            copy(k, (px, py), 2 * px + py).wait_recv()
        for cp in sends:
            cp.wait_send()
        o_ref[...] = ((chip_buf[0] + chip_buf[1]) + chip_buf[2]) + chip_buf[3]

    vmem = pl.BlockSpec(memory_space=pltpu.VMEM)
    return pl.pallas_call(
        body, name="small_allreduce", in_specs=[vmem], out_specs=vmem,
        out_shape=jax.ShapeDtypeStruct(buf.shape, F32),
        scratch_shapes=[pltpu.VMEM(buf.shape, F32), pltpu.VMEM((N_CHIPS,) + buf.shape, F32),
                        pltpu.SemaphoreType.DMA((4,)), pltpu.SemaphoreType.DMA((4,))],
        compiler_params=_params(),
    )(buf)


def _reduce_allgather(qs, axes, sq, as_rows):
    n = len(qs)
    shard_shapes, half_axes = [], []
    for q, a, rows_form in zip(qs, axes, as_rows):
        shape = [d * 2 if i == a - 1 else d for i, d in enumerate(q.shape[1:])]
        if rows_form:
            assert a == 2
            shape = [shape[0], 1, shape[1]]
        shard_shapes.append(tuple(shape))
        half_axes.append(2 if rows_form else a - 1)

    def body(*refs):
        q_refs, sq_ref = refs[:n], refs[n]
        o_refs, st_ref = refs[n + 1:2 * n + 1], refs[2 * n + 1]
        send, recv = refs[2 * n + 2:]
        x, y, c = lax.axis_index("x"), lax.axis_index("y"), lax.axis_index("c")

        def swap(t, which):
            return pltpu.make_async_remote_copy(
                src_ref=_half(o_refs[t], half_axes[t], which), dst_ref=_half(o_refs[t], half_axes[t], which),
                send_sem=send.at[t], recv_sem=recv.at[t], device_id=(x, y, 1 - c), device_id_type=MESH)

        sent = []
        for t in range(n):
            q = q_refs[t]
            total = ((q[0].astype(F32) + q[1].astype(F32)) + q[2].astype(F32)) + q[3].astype(F32)
            if as_rows[t]:
                total = total.reshape(total.shape[0], 1, total.shape[1])
            _half(o_refs[t], half_axes[t], c)[...] = total
            cp = swap(t, c)
            cp.start()
            sent.append(cp)
        st_ref[...] = ((sq_ref[0] + sq_ref[1]) + sq_ref[2]) + sq_ref[3]
        for t in range(n):
            swap(t, 1 - c).wait_recv()
        for cp in sent:
            cp.wait_send()

    vmem = pl.BlockSpec(memory_space=pltpu.VMEM)
    return pl.pallas_call(
        body, name="grad_reduce_allgather",
        in_specs=[vmem] * (n + 1), out_specs=[vmem] * (n + 1),
        out_shape=[jax.ShapeDtypeStruct(s, F32) for s in shard_shapes] + [jax.ShapeDtypeStruct(sq.shape[1:], F32)],
        scratch_shapes=[pltpu.SemaphoreType.DMA((n,)), pltpu.SemaphoreType.DMA((n,))],
        compiler_params=_params(),
    )(*qs, sq)


def _dot_nt(a, b):
    return lax.dot_general(a, b, (((1,), (1,)), ((), ())), preferred_element_type=F32)


def _dot_tn(a, b):
    return lax.dot_general(a, b, (((0,), (0,)), ((), ())), preferred_element_type=F32)


def _mm_nt(a, bt, out_dtype, name, m, n, tm, tn, row_block=0, scale_first=None, gather=None):
    k = a.shape[1]
    shards, axes = gather if gather is not None else ([], [])
    ng = len(shards)
    grid = (m // tm, n // tn)

    def body(a_ref, b_ref, *rest):
        ins, o_ref, outs, sems = rest[:ng], rest[ng], rest[ng + 1:2 * ng + 1], rest[2 * ng + 1:]
        first = (pl.program_id(0) == 0) & (pl.program_id(1) == 0)
        last = (pl.program_id(0) == grid[0] - 1) & (pl.program_id(1) == grid[1] - 1)
        if ng:
            @pl.when(first)
            def _():
                _gather_start(ins, outs, axes, *sems)

        r = _dot_nt(a_ref[...], b_ref[...])
        if scale_first is not None:
            r = r * jnp.where(pl.program_id(1) == 0, scale_first, 1.0)
        o_ref[...] = r.astype(out_dtype)

        if ng:
            @pl.when(last)
            def _():
                _gather_finish(ins, outs, axes, *sems)

    res = pl.pallas_call(
        body, name=name, grid=grid,
        in_specs=[pl.BlockSpec((tm, k), lambda i, j: (i, 0)), pl.BlockSpec((tn, k), lambda i, j: (row_block + j, 0))]
        + [HBM] * ng,
        out_specs=[pl.BlockSpec((tm, tn), lambda i, j: (i, j))] + [HBM] * ng,
        out_shape=[jax.ShapeDtypeStruct((m, n), out_dtype)] + _gathered_shapes(shards),
        scratch_shapes=_gather_sems(ng) if ng else [],
        compiler_params=_params(("arbitrary", "arbitrary") if ng else ("parallel", "parallel")),
    )(a, bt, *_in_hbm(*shards))
    return (res[0], res[1:]) if ng else res[0]


def _mm_tn(a, b, name, k, tm, tn, tk, chunks=1, m=None, m_off=0, swap=None):
    m = a.shape[1] if m is None else m
    n = b.shape[1]
    cw = n // chunks
    srcs, axes = swap if swap is not None else ([], [])
    ns = len(srcs)
    grid = (m // tm, n // tn, k // tk)

    def body(a_ref, b_ref, *rest):
        s_refs, o_ref, d_refs, sems = rest[:ns], rest[ns], rest[ns + 1:2 * ns + 1], rest[2 * ns + 1:]
        ids = [pl.program_id(d) for d in range(3)]
        if ns:
            @pl.when((ids[0] == 0) & (ids[1] == 0) & (ids[2] == 0))
            def _():
                for cp in _swap_copies(s_refs, d_refs, axes, *sems):
                    cp.start()

        @pl.when(ids[2] == 0)
        def _():
            o_ref[...] = jnp.zeros_like(o_ref)
        r = _dot_tn(a_ref[...].astype(BF16), b_ref[...].astype(BF16))
        if chunks > 1:
            for c in range(chunks):
                o_ref[c] += r[:, c * cw:(c + 1) * cw]
        else:
            o_ref[...] += r

        if ns:
            @pl.when((ids[0] == grid[0] - 1) & (ids[1] == grid[1] - 1) & (ids[2] == grid[2] - 1))
            def _():
                for cp in _swap_copies(s_refs, d_refs, axes, *sems):
                    cp.wait()

    if chunks > 1:
        assert tn == n
        out_spec = pl.BlockSpec((chunks, tm, cw), lambda i, j, kk: (0, i, 0))
        out_shape = jax.ShapeDtypeStruct((chunks, m, cw), F32)
    else:
        out_spec = pl.BlockSpec((tm, tn), lambda i, j, kk: (i, j))
        out_shape = jax.ShapeDtypeStruct((m, n), F32)
    res = pl.pallas_call(
        body, name=name, grid=grid,
        in_specs=[pl.BlockSpec((tk, tm), lambda i, j, kk: (kk, m_off + i)), pl.BlockSpec((tk, tn), lambda i, j, kk: (kk, j))]
        + [HBM] * ns,
        out_specs=[out_spec] + [HBM] * ns, out_shape=[out_shape] + _swap_shapes(srcs, axes),
        scratch_shapes=[pltpu.SemaphoreType.DMA((ns,)), pltpu.SemaphoreType.DMA((ns,))] if ns else [],
        compiler_params=_params(("arbitrary",) * 3 if ns else ("parallel", "parallel", "arbitrary")),
    )(a, b, *_in_hbm(*srcs))
    return (res[0], res[1:]) if ns else res[0]


def _tokens_spec():
    return pl.BlockSpec((ROW_TILE, D_MODEL), lambda i: (jnp.maximum(i - 1, 0), 0))


def _rmsnorm_fwd(x2, g1, lk, shards, axes):
    nb = x2.shape[0] // ROW_TILE + 1
    nblk = lk // ROW_TILE
    ng = len(shards)
    meta_cols = shards[-1].shape[1]

    def body(x_ref, g_ref, *rest):
        ins, hn_ref, outs = rest[:ng], rest[ng], rest[ng + 1:2 * ng + 1]
        sems, meta_buf, meta_sem = rest[2 * ng + 1:2 * ng + 6], rest[2 * ng + 6], rest[2 * ng + 7]
        s = pl.program_id(0)
        blk = (s + 1) % nblk

        @pl.when(s == 0)
        def _():
            _gather_start(ins, outs, axes, *sems)

        def normed(h):
            r = lax.rsqrt(jnp.mean(h * h, axis=-1, keepdims=True) + RMS_EPS)
            return ((h * r) * g_ref[...]).astype(BF16)

        @pl.when(s < nblk - 1)
        def _():
            hn_ref[...] = normed(jnp.where(blk >= nb, 0.0, x_ref[...]))

        @pl.when(s == nblk - 1)
        def _():
            _gather_finish(ins, outs, axes, *sems)
            fetch = pltpu.make_async_copy(outs[-1], meta_buf, meta_sem.at[0])
            fetch.start()
            fetch.wait()
            meta = jnp.concatenate([meta_buf[j] for j in range(N_CHIPS)], axis=1)
            hn_ref[...] = normed(jnp.concatenate([jnp.zeros((PAD_ROWS, D_MODEL), F32), meta], axis=0))

    res = pl.pallas_call(
        body, name="rmsnorm_fwd", grid=(nblk,),
        in_specs=[pl.BlockSpec((ROW_TILE, D_MODEL), lambda s: (jnp.clip((s + 1) % nblk - 1, 0, nb - 2), 0)),
                  pl.BlockSpec((1, D_MODEL), lambda s: (0, 0))] + [HBM] * ng,
        out_specs=[pl.BlockSpec((ROW_TILE, D_MODEL), lambda s: ((s + 1) % nblk, 0))] + [HBM] * ng,
        out_shape=[jax.ShapeDtypeStruct((lk, D_MODEL), BF16)] + _gathered_shapes(shards),
        scratch_shapes=_gather_sems(ng) + [pltpu.VMEM((N_CHIPS, N_META, meta_cols), F32), pltpu.SemaphoreType.DMA((1,))],
        compiler_params=_params(("arbitrary",)),
    )(x2, g1, *_in_hbm(*shards))
    return res[0], res[1:]


def _valid_gate_mask(i):
    row = i * ROW_TILE + lax.broadcasted_iota(jnp.int32, (ROW_TILE, F_COLS), 0)
    col = lax.broadcasted_iota(jnp.int32, (ROW_TILE, F_COLS), 1)
    return (row >= PAD_ROWS) & (col < N_HEADS)


def _gates_fwd(f, bfg):
    nb = f.shape[0] // ROW_TILE

    def body(f_ref, b_ref, c_ref, carry):
        i = pl.program_id(0)

        @pl.when(i == 0)
        def _():
            carry[...] = jnp.zeros_like(carry)

        logit = f_ref[...] + b_ref[...]
        lf = jnp.minimum(logit, 0.0) - jnp.log1p(jnp.exp(-jnp.abs(logit)))
        lf = jnp.where(_valid_gate_mask(i), lf, 0.0)
        r_i = lax.broadcasted_iota(jnp.int32, (ROW_TILE, ROW_TILE), 0)
        c_i = lax.broadcasted_iota(jnp.int32, (ROW_TILE, ROW_TILE), 1)
        tri = (c_i <= r_i).astype(F32)
        c_ref[...] = jnp.dot(tri, lf, precision=HIGHEST, preferred_element_type=F32) + carry[...]
        carry[...] = carry[...] + jnp.sum(lf, axis=0, keepdims=True)

    return pl.pallas_call(
        body, name="gates_fwd", grid=(nb,),
        in_specs=[pl.BlockSpec((ROW_TILE, F_COLS), lambda i: (i, 0)), pl.BlockSpec((1, F_COLS), lambda i: (0, 0))],
        out_specs=pl.BlockSpec((ROW_TILE, F_COLS), lambda i: (i, 0)),
        out_shape=jax.ShapeDtypeStruct(f.shape, F32),
        scratch_shapes=[pltpu.VMEM((1, F_COLS), F32)],
        compiler_params=_params(("arbitrary",)),
    )(f, bfg)


def _pool_counts(i):
    row = i * ROW_TILE + lax.broadcasted_iota(jnp.int32, (ROW_TILE, 1), 0)
    return jnp.maximum(row - PAD_ROWS, 0)


def _trailing_sums(xc, levels):
    acc = xc
    for lv in range(levels):
        acc = acc + pltpu.roll(acc, 1 << lv, 0)
    return acc


def _leading_sums(xc, levels):
    n = xc.shape[0]
    acc = xc
    for lv in range(levels):
        acc = acc + pltpu.roll(acc, n - (1 << lv), 0)
    return acc


def _pool_p(u_cur, u_prev, i):
    pos = _pool_counts(i)
    ps, invs = [], []
    for g, w in enumerate(POOL_WINDOWS):
        sl = slice(g * POOL_GROUP, (g + 1) * POOL_GROUP)
        cur = u_cur[:, sl]
        xc = jnp.concatenate([u_prev[:, sl], cur], axis=0)
        win = _trailing_sums(xc, g + 1)[ROW_TILE:, :]
        inv = 1.0 / jnp.minimum(pos + 1, w).astype(F32)
        ps.append(win * inv - cur)
        invs.append(inv)
    return ps, invs


def _pool_fwd(e, pw, scale):
    nb = e.shape[0] // ROW_TILE

    def body(uc_ref, up_ref, z_ref, pw_ref, sc_ref, y_ref):
        i = pl.program_id(0)
        u_cur = uc_ref[...].astype(F32)
        u_prev = jnp.where(i == 0, 0.0, up_ref[...].astype(F32))
        ps, _ = _pool_p(u_cur, u_prev, i)
        z = z_ref[...].astype(F32)
        gate = z * _sigmoid(z)
        for g in range(len(POOL_WINDOWS)):
            sl = slice(g * POOL_GROUP, (g + 1) * POOL_GROUP)
            yraw = jnp.dot(ps[g].astype(BF16), pw_ref[g], preferred_element_type=F32)
            y_ref[:, sl] = ((yraw * sc_ref[:, sl]) * gate[:, sl]).astype(BF16)

    blk = (ROW_TILE, POOL_WIDTH)
    return pl.pallas_call(
        body, name="pool_fwd", grid=(nb,),
        in_specs=[pl.BlockSpec(blk, lambda i: (i, 0)), pl.BlockSpec(blk, lambda i: (jnp.maximum(i - 1, 0), 0)),
                  pl.BlockSpec(blk, lambda i: (i, 1)),
                  pl.BlockSpec((len(POOL_WINDOWS), POOL_GROUP, POOL_GROUP), lambda i: (0, 0, 0)),
                  pl.BlockSpec((1, POOL_WIDTH), lambda i: (0, 0))],
        out_specs=pl.BlockSpec(blk, lambda i: (i, 0)),
        out_shape=jax.ShapeDtypeStruct((e.shape[0], POOL_WIDTH), BF16),
        compiler_params=_params(("parallel",)),
    )(e, e, e, pw, scale)


def _stack_heads(a):
    first = lax.broadcasted_iota(jnp.int32, a.shape, 1) < HEAD_DIM
    zero = jnp.zeros_like(a)
    return jnp.concatenate([jnp.where(first, a, zero), jnp.where(first, zero, a)], axis=0)


def _unstack_heads(a):
    rows = a.shape[0] // 2
    first = lax.broadcasted_iota(jnp.int32, (rows, LANES), 1) < HEAD_DIM
    return jnp.where(first, a[:rows], a[rows:])


def _causal(i, jj, stacked, kv_tile):
    r = lax.broadcasted_iota(jnp.int32, (stacked * ROW_TILE, kv_tile), 0)
    if stacked == 2:
        r = jnp.where(r >= ROW_TILE, r - ROW_TILE, r)
    kidx = jj * kv_tile + lax.broadcasted_iota(jnp.int32, (stacked * ROW_TILE, kv_tile), 1)
    return kidx <= i * ROW_TILE + r


def _stack_rows(b):
    n = b.shape[1]
    return jnp.concatenate([jnp.broadcast_to(b[0:1], (ROW_TILE, n)), jnp.broadcast_to(b[1:2], (ROW_TILE, n))], axis=0)


def _attn_fwd(qkv, bias, nb):
    lk = qkv.shape[0]
    lp = nb * ROW_TILE
    nkb = lk // KV_TILE
    n_pairs = N_HEADS // 2

    def body(q_ref, k_ref, v_ref, b_ref, o_ref, lse_ref):
        i = pl.program_id(1)
        qs = _stack_heads(q_ref[...])
        last = (i * ROW_TILE) // KV_TILE

        def block(jj, carry, masked):
            m, l, acc = carry
            rows = pl.ds(pl.multiple_of(jj * KV_TILE, KV_TILE), KV_TILE)
            s = _dot_nt(qs, k_ref[rows, :]) - _stack_rows(b_ref[0, 0, jj])
            if masked:
                s = jnp.where(_causal(i, jj, 2, KV_TILE), s, NEG)
            m_new = jnp.maximum(m, jnp.max(s, axis=1, keepdims=True))
            alpha = jnp.exp(m - m_new)
            p = jnp.exp(s - m_new)
            l = alpha * l + jnp.sum(p, axis=1, keepdims=True)
            acc = alpha * acc + jnp.dot(p.astype(BF16), v_ref[rows, :], preferred_element_type=F32)
            return m_new, l, acc

        init = (jnp.full((2 * ROW_TILE, 1), NEG, F32), jnp.zeros((2 * ROW_TILE, 1), F32),
                jnp.zeros((2 * ROW_TILE, LANES), F32))
        carry = lax.fori_loop(0, last, lambda jj, c: block(jj, c, False), init)
        m, l, acc = block(last, carry, True)
        o_ref[...] = _unstack_heads(acc / l)
        lse_ref[...] = _unstack_heads(jnp.broadcast_to(m + jnp.log(l), (2 * ROW_TILE, LANES)))

    return pl.pallas_call(
        body, name="attn_fwd", grid=(n_pairs, nb),
        in_specs=[pl.BlockSpec((ROW_TILE, LANES), lambda hp, i: (i, hp)),
                  pl.BlockSpec((lk, LANES), lambda hp, i: (0, n_pairs + hp)),
                  pl.BlockSpec((lk, LANES), lambda hp, i: (0, 2 * n_pairs + hp)),
                  pl.BlockSpec((1, 1, nkb, 2, KV_TILE), lambda hp, i: (i, hp, 0, 0, 0))],
        out_specs=[pl.BlockSpec((ROW_TILE, LANES), lambda hp, i: (i, hp)),
                   pl.BlockSpec((ROW_TILE, LANES), lambda hp, i: (i, hp))],
        out_shape=[jax.ShapeDtypeStruct((lp, ATTN_WIDTH), F32), jax.ShapeDtypeStruct((lp, ATTN_WIDTH), F32)],
        compiler_params=_params(("parallel", "parallel")),
    )(qkv, qkv, qkv, bias)


def _merge_fwd(y_pool, o, e, wup_p, wup_a):
    lp = o.shape[0]
    nb = lp // ROW_TILE

    def body(yp_ref, o_ref, e_ref, wp_ref, wa_ref, mg_ref, ya_ref):
        za = e_ref[:, E_ZA:E_GP].astype(F32)
        ya = (o_ref[...] * (za * _sigmoid(za))).astype(BF16)
        ya_ref[...] = ya
        a_pool = jnp.dot(yp_ref[...], wp_ref[...], preferred_element_type=F32)
        a_attn = jnp.dot(ya, wa_ref[...], preferred_element_type=F32)
        mg_ref[...] = (_sigmoid(e_ref[:, E_GP:E_GA].astype(F32)) * a_pool
                       + _sigmoid(e_ref[:, E_GA:E_COLS].astype(F32)) * a_attn).astype(BF16)

    return pl.pallas_call(
        body, name="merge_fwd", grid=(nb,),
        in_specs=[pl.BlockSpec((ROW_TILE, POOL_WIDTH), lambda i: (i, 0)),
                  pl.BlockSpec((ROW_TILE, ATTN_WIDTH), lambda i: (i, 0)),
                  pl.BlockSpec((ROW_TILE, E_COLS), lambda i: (i, 0)),
                  pl.BlockSpec((POOL_WIDTH, D_MODEL), lambda i: (0, 0)),
                  pl.BlockSpec((ATTN_WIDTH, D_MODEL), lambda i: (0, 0))],
        out_specs=[pl.BlockSpec((ROW_TILE, D_MODEL), lambda i: (i, 0)),
                   pl.BlockSpec((ROW_TILE, ATTN_WIDTH), lambda i: (i, 0))],
        out_shape=[jax.ShapeDtypeStruct((lp, D_MODEL), BF16), jax.ShapeDtypeStruct((lp, ATTN_WIDTH), BF16)],
        compiler_params=_params(("parallel",)),
    )(y_pool, o, e, wup_p, wup_a)


def _head_fwd_bwd(merged, w_out, x2, metapad, gf, target):
    lp = merged.shape[0]
    nb = lp // ROW_TILE

    def body(mg_ref, w_ref, x_ref, mp_ref, g_ref, t_ref, dh_ref, loss_ref, dg_ref):
        i = pl.program_id(0)

        @pl.when(i == 0)
        def _():
            loss_ref[...] = jnp.zeros_like(loss_ref)
            dg_ref[...] = jnp.zeros_like(dg_ref)

        h0 = jnp.where(i == 0, mp_ref[...], x_ref[...])
        h1 = h0 + jnp.dot(mg_ref[...], w_ref[...], preferred_element_type=F32)
        r = lax.rsqrt(jnp.mean(h1 * h1, axis=-1, keepdims=True) + RMS_EPS)
        xhat = h1 * r
        g = g_ref[...]
        err = jnp.where(i == 0, 0.0, xhat * g - t_ref[...])
        loss_ref[...] += 0.5 * jnp.sum(jnp.mean(err * err, axis=-1, keepdims=True))
        dy = err / D_MODEL
        dg_ref[...] += jnp.sum(dy * xhat, axis=0, keepdims=True)
        dxhat = dy * g
        dh_ref[...] = r * (dxhat - xhat * jnp.mean(dxhat * xhat, axis=-1, keepdims=True))

    return pl.pallas_call(
        body, name="head_fwd_bwd", grid=(nb,),
        in_specs=[pl.BlockSpec((ROW_TILE, D_MODEL), lambda i: (i, 0)),
                  pl.BlockSpec((D_MODEL, D_MODEL), lambda i: (0, 0)),
                  _tokens_spec(), pl.BlockSpec((ROW_TILE, D_MODEL), lambda i: (0, 0)),
                  pl.BlockSpec((1, D_MODEL), lambda i: (0, 0)), _tokens_spec()],
        out_specs=[pl.BlockSpec((ROW_TILE, D_MODEL), lambda i: (i, 0)),
                   pl.BlockSpec((1, LANES), lambda i: (0, 0)), pl.BlockSpec((1, D_MODEL), lambda i: (0, 0))],
        out_shape=[jax.ShapeDtypeStruct((lp, D_MODEL), F32), jax.ShapeDtypeStruct((1, LANES), F32),
                   jax.ShapeDtypeStruct((1, D_MODEL), F32)],
        compiler_params=_params(("arbitrary",)),
    )(merged, w_out, x2, metapad, gf, target)


def _per_head_rowsum(t):
    head = lax.broadcasted_iota(jnp.int32, t.shape, 1) // HEAD_DIM
    out = jnp.zeros_like(t)
    for h in range(N_HEADS):
        sel = head == h
        out = jnp.where(sel, jnp.sum(jnp.where(sel, t, 0.0), axis=1, keepdims=True), out)
    return out


def _merge_bwd(dh1, w_out, y_pool, y_attn, wup_p, wup_a, e, o):
    lp = o.shape[0]
    nb = lp // ROW_TILE

    def body(dh_ref, wo_ref, yp_ref, ya_ref, wp_ref, wa_ref, e_ref, o_ref,
             dap_ref, daa_ref, dg_ref, dza_ref, do_ref, delta_ref, dyp_ref):
        dmerged = _dot_nt(dh_ref[...].astype(BF16), wo_ref[...])
        a_pool = jnp.dot(yp_ref[...], wp_ref[...], preferred_element_type=F32)
        a_attn = jnp.dot(ya_ref[...], wa_ref[...], preferred_element_type=F32)
        sp = _sigmoid(e_ref[:, E_GP:E_GA].astype(F32))
        sa = _sigmoid(e_ref[:, E_GA:E_COLS].astype(F32))
        dap = (dmerged * sp).astype(BF16)
        daa = (dmerged * sa).astype(BF16)
        dap_ref[...] = dap
        daa_ref[...] = daa
        dg_ref[:, :D_MODEL] = (dmerged * a_pool * (sp * (1.0 - sp))).astype(BF16)
        dg_ref[:, D_MODEL:] = (dmerged * a_attn * (sa * (1.0 - sa))).astype(BF16)
        dyp_ref[...] = _dot_nt(dap, wp_ref[...])
        dya = _dot_nt(daa, wa_ref[...])
        za = e_ref[:, E_ZA:E_GP].astype(F32)
        sz = _sigmoid(za)
        o = o_ref[...]
        do = dya * (za * sz)
        do_ref[...] = do.astype(BF16)
        dza_ref[...] = (dya * o * (sz * (1.0 + za * (1.0 - sz)))).astype(BF16)
        delta_ref[...] = _per_head_rowsum(do * o)

    row = lambda w: pl.BlockSpec((ROW_TILE, w), lambda i: (i, 0))
    full = lambda a: pl.BlockSpec(a.shape, lambda i: (0, 0))
    return pl.pallas_call(
        body, name="merge_bwd", grid=(nb,),
        in_specs=[row(D_MODEL), full(w_out), row(POOL_WIDTH), row(ATTN_WIDTH), full(wup_p), full(wup_a),
                  row(E_COLS), row(ATTN_WIDTH)],
        out_specs=[row(D_MODEL), row(D_MODEL), row(2 * D_MODEL), row(ATTN_WIDTH), row(ATTN_WIDTH),
                   row(ATTN_WIDTH), row(POOL_WIDTH)],
        out_shape=[jax.ShapeDtypeStruct((lp, D_MODEL), BF16), jax.ShapeDtypeStruct((lp, D_MODEL), BF16),
                   jax.ShapeDtypeStruct((lp, 2 * D_MODEL), BF16), jax.ShapeDtypeStruct((lp, ATTN_WIDTH), BF16),
                   jax.ShapeDtypeStruct((lp, ATTN_WIDTH), BF16), jax.ShapeDtypeStruct((lp, ATTN_WIDTH), F32),
                   jax.ShapeDtypeStruct((lp, POOL_WIDTH), F32)],
        compiler_params=_params(("parallel",)),
    )(dh1, w_out, y_pool, y_attn, wup_p, wup_a, e, o)


def _pool_bwd_local(e, dy_pool, pw, scale):
    lp = e.shape[0]
    nb = lp // ROW_TILE
    ng = len(POOL_WINDOWS)

    def body(uc_ref, up_ref, z_ref, dy_ref, pw_ref, sc_ref, dpc_ref, dz_ref, dsc_ref, dpw_ref):
        i = pl.program_id(0)

        @pl.when(i == 0)
        def _():
            dsc_ref[...] = jnp.zeros_like(dsc_ref)
            dpw_ref[...] = jnp.zeros_like(dpw_ref)

        u_cur = uc_ref[...].astype(F32)
        u_prev = jnp.where(i == 0, 0.0, up_ref[...].astype(F32))
        ps, invs = _pool_p(u_cur, u_prev, i)
        z = z_ref[...].astype(F32)
        sz = _sigmoid(z)
        dy = dy_ref[...]
        dypre = dy * (z * sz)
        dsilu = sz * (1.0 + z * (1.0 - sz))
        for g in range(ng):
            sl = slice(g * POOL_GROUP, (g + 1) * POOL_GROUP)
            pb = ps[g].astype(BF16)
            w = pw_ref[g]
            yraw = jnp.dot(pb, w, preferred_element_type=F32)
            sc = sc_ref[:, sl]
            dz_ref[:, sl] = (dy[:, sl] * (yraw * sc) * dsilu[:, sl]).astype(BF16)
            dsc_ref[:, sl] += jnp.sum(dypre[:, sl] * yraw, axis=0, keepdims=True)
            dyraw = (dypre[:, sl] * sc).astype(BF16)
            dpw_ref[g] += _dot_tn(pb, dyraw)
            dpc_ref[:, sl] = _dot_nt(dyraw, w) * invs[g]

    blk = (ROW_TILE, POOL_WIDTH)
    return pl.pallas_call(
        body, name="pool_bwd_local", grid=(nb,),
        in_specs=[pl.BlockSpec(blk, lambda i: (i, 0)), pl.BlockSpec(blk, lambda i: (jnp.maximum(i - 1, 0), 0)),
                  pl.BlockSpec(blk, lambda i: (i, 1)), pl.BlockSpec(blk, lambda i: (i, 0)),
                  pl.BlockSpec((ng, POOL_GROUP, POOL_GROUP), lambda i: (0, 0, 0)),
                  pl.BlockSpec((1, POOL_WIDTH), lambda i: (0, 0))],
        out_specs=[pl.BlockSpec(blk, lambda i: (i, 0)), pl.BlockSpec(blk, lambda i: (i, 0)),
                   pl.BlockSpec((1, POOL_WIDTH), lambda i: (0, 0)),
                   pl.BlockSpec((ng, POOL_GROUP, POOL_GROUP), lambda i: (0, 0, 0))],
        out_shape=[jax.ShapeDtypeStruct((lp, POOL_WIDTH), F32), jax.ShapeDtypeStruct((lp, POOL_WIDTH), BF16),
                   jax.ShapeDtypeStruct((1, POOL_WIDTH), F32),
                   jax.ShapeDtypeStruct((ng, POOL_GROUP, POOL_GROUP), F32)],
        compiler_params=_params(("arbitrary",)),
    )(e, e, e, dy_pool, pw, scale)


def _pool_bwd_window(dpc):
    lp = dpc.shape[0]
    nb = lp // ROW_TILE

    def body(cur_ref, nxt_ref, du_ref):
        i = pl.program_id(0)
        cur = cur_ref[...]
        nxt = jnp.where(i == nb - 1, 0.0, nxt_ref[...])
        pos = _pool_counts(i)
        for g, w in enumerate(POOL_WINDOWS):
            sl = slice(g * POOL_GROUP, (g + 1) * POOL_GROUP)
            xc = jnp.concatenate([cur[:, sl], nxt[:, sl]], axis=0)
            win = _leading_sums(xc, g + 1)[:ROW_TILE, :]
            dp = cur[:, sl] * jnp.minimum(pos + 1, w).astype(F32)
            du_ref[:, sl] = (win - dp).astype(BF16)

    blk = (ROW_TILE, POOL_WIDTH)
    return pl.pallas_call(
        body, name="pool_bwd_window", grid=(nb,),
        in_specs=[pl.BlockSpec(blk, lambda i: (i, 0)), pl.BlockSpec(blk, lambda i: (jnp.minimum(i + 1, nb - 1), 0))],
        out_specs=pl.BlockSpec(blk, lambda i: (i, 0)),
        out_shape=jax.ShapeDtypeStruct((lp, POOL_WIDTH), BF16),
        compiler_params=_params(("parallel",)),
    )(dpc, dpc)


def _attn_bwd(qkv, do, lse, delta, bias, nb):
    lk = qkv.shape[0]
    lp = nb * ROW_TILE
    nkb = lk // KV_TILE_BWD
    n_pairs = N_HEADS // 2
    per_kv = KV_TILE_BWD // ROW_TILE

    def body(q_ref, k_ref, v_ref, do_ref, lse_ref, dl_ref, b_ref, dq_ref, dk_ref, dv_ref, dc_ref, dcq_ref):
        jj = pl.program_id(1)

        @pl.when(jj == 0)
        def _():
            dq_ref[...] = jnp.zeros_like(dq_ref)
            dcq_ref[...] = jnp.zeros_like(dcq_ref)

        kb, vb = k_ref[...], v_ref[...]

        def block(i, carry, masked):
            dk_acc, dv_acc, dc_acc = carry
            rows = pl.ds(pl.multiple_of(i * ROW_TILE, ROW_TILE), ROW_TILE)
            qs = _stack_heads(q_ref[rows, :])
            dos = _stack_heads(do_ref[rows, :])
            lse_i, dl_i = lse_ref[rows, :], dl_ref[rows, :]
            s = _dot_nt(qs, kb)
            dp = _dot_nt(dos, vb)
            if masked:
                valid = _causal(i, jj, 1, KV_TILE_BWD)
            ps, dss, dcs, rowsums = [], [], [], []
            for hd in range(2):
                half = slice(hd * ROW_TILE, (hd + 1) * ROW_TILE)
                col = slice(hd * HEAD_DIM, hd * HEAD_DIM + 1)
                sh = s[half] - b_ref[i, 0, 0, hd:hd + 1, :]
                if masked:
                    sh = jnp.where(valid, sh, NEG)
                p = jnp.exp(sh - lse_i[:, col])
                ds = p * (dp[half] - dl_i[:, col])
                ps.append(p.astype(BF16))
                dss.append(ds.astype(BF16))
                dcs.append(jnp.sum(ds, axis=0, keepdims=True))
                rowsums.append(jnp.sum(ds, axis=1, keepdims=True))
            dsb = jnp.concatenate(dss, axis=0)
            dv_acc = dv_acc + _dot_tn(jnp.concatenate(ps, axis=0), dos)
            dk_acc = dk_acc + _dot_tn(dsb, qs)
            dc_acc = dc_acc - jnp.concatenate(dcs, axis=0)
            dq_ref[rows, :] += _unstack_heads(jnp.dot(dsb, kb, preferred_element_type=F32))
            dcq_ref[rows, :] += _unstack_heads(jnp.broadcast_to(jnp.concatenate(rowsums, axis=0), (2 * ROW_TILE, LANES)))
            return dk_acc, dv_acc, dc_acc

        init = (jnp.zeros((KV_TILE_BWD, LANES), F32), jnp.zeros((KV_TILE_BWD, LANES), F32),
                jnp.zeros((2, KV_TILE_BWD), F32))
        first_q = per_kv * jj
        diag_end = jnp.minimum(first_q + per_kv, nb)
        carry = lax.fori_loop(first_q, diag_end, lambda i, c: block(i, c, True), init)
        dk, dv, dc = lax.fori_loop(diag_end, nb, lambda i, c: block(i, c, False), carry)
        dk_ref[...] = dk.astype(BF16)
        dv_ref[...] = dv.astype(BF16)
        dc_ref[0, 0] = dc

    whole = lambda rows: pl.BlockSpec((rows, LANES), lambda hp, jj: (0, hp))
    kv_blk = lambda off: pl.BlockSpec((KV_TILE_BWD, LANES), lambda hp, jj: (jj, off + hp))
    return pl.pallas_call(
        body, name="attn_bwd", grid=(n_pairs, nkb),
        in_specs=[whole(lk), kv_blk(n_pairs), kv_blk(2 * n_pairs), whole(lp), whole(lp), whole(lp),
                  pl.BlockSpec((nb, 1, 1, 2, KV_TILE_BWD), lambda hp, jj: (0, hp, jj, 0, 0))],
        out_specs=[whole(lp), kv_blk(0), kv_blk(0),
                   pl.BlockSpec((1, 1, 2, KV_TILE_BWD), lambda hp, jj: (hp, jj, 0, 0)), whole(lp)],
        out_shape=[jax.ShapeDtypeStruct((lp, ATTN_WIDTH), F32), jax.ShapeDtypeStruct((lk, ATTN_WIDTH), BF16),
                   jax.ShapeDtypeStruct((lk, ATTN_WIDTH), BF16),
                   jax.ShapeDtypeStruct((n_pairs, nkb, 2, KV_TILE_BWD), F32),
                   jax.ShapeDtypeStruct((lp, ATTN_WIDTH), F32)],
        compiler_params=_params(("parallel", "arbitrary")),
    )(qkv, qkv, qkv, do, lse, delta, bias)


def _gates_bwd(dc, dcq, f, bfg):
    nb = f.shape[0] // ROW_TILE

    def body(dc_ref, dcq_ref, f_ref, b_ref, df_ref, db_ref, carry):
        step = pl.program_id(0)
        i = nb - 1 - step

        @pl.when(step == 0)
        def _():
            carry[...] = jnp.zeros_like(carry)
            db_ref[...] = jnp.zeros_like(db_ref)

        dcb = dc_ref[...]
        lane = lax.broadcasted_iota(jnp.int32, (ROW_TILE, F_COLS), 1)
        for h in range(N_HEADS):
            dcb = dcb + jnp.where(lane == h, dcq_ref[:, HEAD_DIM * h:HEAD_DIM * h + 1], 0.0)
        r_i = lax.broadcasted_iota(jnp.int32, (ROW_TILE, ROW_TILE), 0)
        c_i = lax.broadcasted_iota(jnp.int32, (ROW_TILE, ROW_TILE), 1)
        upper = (c_i >= r_i).astype(F32)
        dlf = jnp.dot(upper, dcb, precision=HIGHEST, preferred_element_type=F32) + carry[...]
        carry[...] = carry[...] + jnp.sum(dcb, axis=0, keepdims=True)
        logit = f_ref[...] + b_ref[...]
        dlogit = jnp.where(_valid_gate_mask(i), dlf * _sigmoid(-logit), 0.0)
        df_ref[...] = dlogit.astype(BF16)
        db_ref[...] += jnp.sum(dlogit, axis=0, keepdims=True)

    blk = pl.BlockSpec((ROW_TILE, F_COLS), lambda s: (nb - 1 - s, 0))
    wide = pl.BlockSpec((ROW_TILE, ATTN_WIDTH), lambda s: (nb - 1 - s, 0))
    one = pl.BlockSpec((1, F_COLS), lambda s: (0, 0))
    return pl.pallas_call(
        body, name="gates_bwd", grid=(nb,),
        in_specs=[blk, wide, blk, one], out_specs=[blk, one],
        out_shape=[jax.ShapeDtypeStruct(f.shape, BF16), jax.ShapeDtypeStruct((1, F_COLS), F32)],
        scratch_shapes=[pltpu.VMEM((1, F_COLS), F32)],
        compiler_params=_params(("arbitrary",)),
    )(dc, dcq, f, bfg)


def _input_bwd(dproj, df, wt_e, wt_qkv, wt_f, x2, metapad, dh1, g1, hs, sc):
    nb = x2.shape[0] // ROW_TILE + 1
    n = len(hs)

    def body(dp_ref, df_ref, we_ref, w_ref, wf_ref, x_ref, mp_ref, dh_ref, g_ref, *rest):
        h_refs, s_ref = rest[:n], rest[n]
        gx_ref, g0_ref, dg_ref = rest[n + 1:n + 4]
        q_refs, sq_ref = rest[n + 4:2 * n + 4], rest[2 * n + 4]
        sems = rest[2 * n + 5:]
        i = pl.program_id(0)

        @pl.when(i == 0)
        def _():
            dg_ref[...] = jnp.zeros_like(dg_ref)
            own, sends, _ = _chip_scatter_plan(h_refs, s_ref, q_refs, sq_ref, *sems)
            for cp in own + sends:
                cp.start()

        dhn = (jnp.dot(dp_ref[:, :E_COLS], we_ref[...], preferred_element_type=F32)
               + jnp.dot(dp_ref[:, E_COLS:], w_ref[...], preferred_element_type=F32)
               + jnp.dot(df_ref[...], wf_ref[...], preferred_element_type=F32))
        h0 = jnp.where(i == 0, mp_ref[...], x_ref[...])
        r = lax.rsqrt(jnp.mean(h0 * h0, axis=-1, keepdims=True) + RMS_EPS)
        xhat = h0 * r
        dg_ref[...] += jnp.sum(dhn * xhat, axis=0, keepdims=True)
        dxhat = dhn * g_ref[...]
        dh0 = dh_ref[...] + r * (dxhat - xhat * jnp.mean(dxhat * xhat, axis=-1, keepdims=True))
        gx_ref[...] = dh0

        @pl.when(i == 0)
        def _():
            g0_ref[...] = dh0

        @pl.when(i == nb - 1)
        def _():
            own, sends, recvs = _chip_scatter_plan(h_refs, s_ref, q_refs, sq_ref, *sems)
            for cp in recvs:
                cp.wait_recv()
            for cp in sends:
                cp.wait_send()
            for cp in own:
                cp.wait()

    const = lambda shape: pl.BlockSpec(shape, lambda i: (0, 0))
    res = pl.pallas_call(
        body, name="input_bwd", grid=(nb,),
        in_specs=[pl.BlockSpec((ROW_TILE, MAIN_COLS), lambda i: (i, 0)), pl.BlockSpec((ROW_TILE, F_COLS), lambda i: (i, 0)),
                  const((E_COLS, D_MODEL)), const((QKV_COLS, D_MODEL)), const((F_COLS, D_MODEL)),
                  _tokens_spec(), const((ROW_TILE, D_MODEL)),
                  pl.BlockSpec((ROW_TILE, D_MODEL), lambda i: (i, 0)), const((1, D_MODEL))] + [HBM] * (n + 1),
        out_specs=[_tokens_spec(), const((ROW_TILE, D_MODEL)), const((1, D_MODEL))] + [HBM] * (n + 1),
        out_shape=[jax.ShapeDtypeStruct(x2.shape, F32), jax.ShapeDtypeStruct((ROW_TILE, D_MODEL), F32),
                   jax.ShapeDtypeStruct((1, D_MODEL), F32)]
        + [jax.ShapeDtypeStruct(h.shape, h.dtype) for h in hs] + [jax.ShapeDtypeStruct((N_CHIPS,) + sc.shape, sc.dtype)],
        scratch_shapes=[pltpu.SemaphoreType.DMA((3 * (n + 1),)), pltpu.SemaphoreType.DMA((3 * (n + 1),)),
                        pltpu.SemaphoreType.DMA((n + 1,))],
        compiler_params=_params(("arbitrary",), vmem=56 * 1024 * 1024),
    )(dproj, df, wt_e, wt_qkv, wt_f, x2, metapad, dh1, g1, *_in_hbm(*hs, sc))
    return res[:3], res[3:3 + n], res[3 + n]


def _adamw(w, g, m, v, name):
    rows, cols = w.shape
    if rows % 8 == 0:
        tr, tc = _row_tile8(rows), cols
    else:
        tr, tc = rows, (2 * LANES if cols % (2 * LANES) == 0 and rows > 8 else cols)

    def body(w_ref, g_ref, m_ref, v_ref, d_ref, mo_ref, vo_ref):
        g_ = g_ref[...]
        m_new = ADAM_B1 * m_ref[...] + (1.0 - ADAM_B1) * g_
        v_new = ADAM_B2 * v_ref[...] + (1.0 - ADAM_B2) * (g_ * g_)
        m_hat = m_new / (1.0 - ADAM_B1 ** ADAM_STEP)
        v_hat = v_new / (1.0 - ADAM_B2 ** ADAM_STEP)
        d_ref[...] = -ADAM_LR * (m_hat / (jnp.sqrt(v_hat) + ADAM_EPS) + ADAM_WD * w_ref[...])
        mo_ref[...] = m_new
        vo_ref[...] = v_new

    blk = pl.BlockSpec((tr, tc), lambda i, j: (i, j))
    return pl.pallas_call(
        body, name=name, grid=(rows // tr, cols // tc),
        in_specs=[blk] * 4, out_specs=[blk] * 3,
        out_shape=[jax.ShapeDtypeStruct(w.shape, F32)] * 3,
        compiler_params=_params(("parallel", "parallel")),
    )(w, g, m, v)


def _adamw_native(w3, g3, m3, v3, name):
    rows = w3.shape[0]
    tr = rows // 2
    blk = pl.BlockSpec((tr,) + w3.shape[1:], lambda i: (i, 0, 0))
    shape = jax.ShapeDtypeStruct(w3.shape, F32)

    def moments(g_ref, m_ref, v_ref, mo_ref, vo_ref):
        g_ = g_ref[...]
        mo_ref[...] = ADAM_B1 * m_ref[...] + (1.0 - ADAM_B1) * g_
        vo_ref[...] = ADAM_B2 * v_ref[...] + (1.0 - ADAM_B2) * (g_ * g_)

    new_m, new_v = pl.pallas_call(
        moments, name=name + "_moments", grid=(2,), in_specs=[blk] * 3, out_specs=[blk] * 2, out_shape=[shape] * 2,
        compiler_params=_params(("parallel",)),
    )(g3, m3, v3)

    def delta(w_ref, m_ref, v_ref, d_ref):
        m_hat = m_ref[...] / (1.0 - ADAM_B1 ** ADAM_STEP)
        v_hat = v_ref[...] / (1.0 - ADAM_B2 ** ADAM_STEP)
        d_ref[...] = -ADAM_LR * (m_hat / (jnp.sqrt(v_hat) + ADAM_EPS) + ADAM_WD * w_ref[...])

    d = pl.pallas_call(
        delta, name=name + "_delta", grid=(2,), in_specs=[blk] * 3, out_specs=blk, out_shape=shape,
        compiler_params=_params(("parallel",)),
    )(w3, new_m, new_v)
    return d, new_m, new_v


def _row_tile8(rows):
    best = rows
    for t in range(8, 257, 8):
        if rows % t == 0:
            best = t
    return best


def kernel(x, meta_tokens, norm_g, w_in, b_forget, pool_w, pool_scale, w_up_pool, w_up_attn, w_out, final_norm_g, loss_target, m_meta_tokens, m_norm_g, m_w_in, m_b_forget, m_pool_w, m_pool_scale, m_w_up_pool, m_w_up_attn, m_w_out, m_final_norm_g, v_meta_tokens, v_norm_g, v_w_in, v_b_forget, v_pool_w, v_pool_scale, v_w_up_pool, v_w_up_attn, v_w_out, v_final_norm_g):
    seq = x.shape[1]
    assert seq % ROW_TILE == 0 and x.shape[0] == 1
    lp = seq + ROW_TILE
    nb = lp // ROW_TILE
    lk = -(-lp // KV_TILE_BWD) * KV_TILE_BWD
    core = jnp.reshape(lax.axis_index("c"), (1,)).astype(jnp.int32)
    x2 = x[0]
    target = loss_target[0]
    sh_d = D_MODEL // N_CHIPS

    to_rows = lambda a: jnp.transpose(a, (2, 0, 1))
    from_rows = lambda a: jnp.transpose(a, (1, 2, 0))
    gf = final_norm_g.reshape(1, D_MODEL)
    bfg = jnp.pad(b_forget, ((0, 0), (0, F_COLS - N_HEADS)))
    pw_b = pool_w[0].astype(BF16)
    hn, (wg_in, meta_g) = _rmsnorm_fwd(x2, norm_g, lk, [to_rows(w_in).astype(BF16), meta_tokens], [2, 0])
    wt = wg_in.reshape(-1, 1, D_MODEL)[:, 0]
    wt_e = jnp.concatenate([wt[REF_U:REF_Q], wt[REF_ZA:REF_F], wt[REF_GP:]], axis=0)
    wt_qkv = wt[REF_Q:REF_ZA]
    wt_f = jnp.pad(wt[REF_F:REF_GP], ((0, F_COLS - N_HEADS), (0, 0)))
    meta_full = jnp.transpose(meta_g, (1, 0, 2)).reshape(N_META, D_MODEL)
    metapad = jnp.pad(meta_full, ((PAD_ROWS, 0), (0, 0)))

    tm = _row_tile(lp, 1100)
    e, (wg_up_p, wg_up_a, wg_out) = _mm_nt(
        hn, wt_e, BF16, "in_proj_gates", lp, E_COLS, tm, 512,
        gather=([w_up_pool[0].astype(BF16), w_up_attn[0].astype(BF16), w_out[0].astype(BF16)], [0, 0, 0]))
    wup_p = jnp.transpose(wg_up_p, (1, 0, 2)).reshape(POOL_WIDTH, D_MODEL)
    wup_a = jnp.transpose(wg_up_a, (1, 0, 2)).reshape(ATTN_WIDTH, D_MODEL)
    wout = wg_out.reshape(D_MODEL, D_MODEL)
    qkv = _mm_nt(hn, wt_qkv, BF16, "in_proj_qkv", lk, QKV_COLS, _row_tile(lk, 1300), 512, scale_first=HEAD_DIM ** -0.5)
    f = _mm_nt(hn, wt_f, F32, "in_proj_forget", lp, F_COLS, tm, F_COLS)
    c = _gates_fwd(f, bfg)
    c_t = jnp.transpose(c[:, :N_HEADS])
    bias = c_t[None, :, :] - jnp.transpose(c_t[:, ::ROW_TILE])[:, :, None]
    bias = jnp.where(jnp.arange(lp) < PAD_ROWS, -NEG, bias)
    bias = jnp.pad(bias, ((0, 0), (0, 0), (0, lk - lp)))
    by_kv = lambda t: jnp.transpose(bias.reshape(nb, N_HEADS // 2, 2, lk // t, t), (0, 1, 3, 2, 4))
    bias, bias_bwd = by_kv(KV_TILE), by_kv(KV_TILE_BWD)
    y_pool = _pool_fwd(e, pw_b, pool_scale)
    o, lse = _attn_fwd(qkv, bias, nb)
    merged, y_attn = _merge_fwd(y_pool, o, e, wup_p, wup_a)
    dh1, loss_part, dgf = _head_fwd_bwd(merged, wout, x2, metapad, gf, target)

    dap, daa, dgate, dza, do, delta, dy_pool = _merge_bwd(dh1, wout, y_pool, y_attn, wup_p, wup_a, e, o)
    dpc, dzp, dscale, dpw = _pool_bwd_local(e, dy_pool, pw_b, pool_scale)
    du = _pool_bwd_window(dpc)
    dq, dk, dv, dc4, dcq = _attn_bwd(qkv, do, lse, delta, bias_bwd, nb)
    dc = jnp.transpose(dc4, (1, 3, 0, 2)).reshape(lk, N_HEADS)[:lp]
    df, db = _gates_bwd(jnp.pad(dc, ((0, 0), (0, F_COLS - N_HEADS))), dcq, f, bfg)
    dproj = jnp.concatenate([du, dzp, dza, dgate, (dq * HEAD_DIM ** -0.5).astype(BF16), dk[:lp], dv[:lp]], axis=1)
    tk = _row_tile(lp, 1100)
    dw_out = _mm_tn(merged, dh1, "grad_w_out", lp, 512, D_MODEL, tk)
    dw_up_p = _mm_tn(y_pool, dap, "grad_w_up_pool", lp, 512, D_MODEL, tk, chunks=N_CHIPS)
    dw_up_a = _mm_tn(y_attn, daa, "grad_w_up_attn", lp, 512, D_MODEL, tk, chunks=N_CHIPS)
    dwt_f = _mm_tn(df, hn, "grad_w_in_forget", lp, F_COLS, D_MODEL, tk)

    def pad8(a):
        return jnp.pad(a, ((0, (-a.shape[0]) % 8), (0, 0)))

    small_parts = [pad8(a) for a in (dgf.reshape(-1, LANES), dscale.reshape(-1, LANES), db, dpw.reshape(-1, LANES),
                                     loss_part)]
    small = jnp.concatenate(small_parts, axis=0)
    soffs = [0]
    for p in small_parts:
        soffs.append(soffs[-1] + p.shape[0])

    gs_rows = [dw_up_p, dw_up_a, dw_out.reshape(N_CHIPS, sh_d, D_MODEL)]
    half = MAIN_COLS // 2
    dwt_a, (*rb_rows, rb_f, sr) = _mm_tn(dproj, hn, "grad_w_in_a", lp, half // 2, D_MODEL, tk, m=half, m_off=0,
                                         swap=(gs_rows + [dwt_f, small], [1, 1, 1, 1, None]))
    dwt_b, (rb_a,) = _mm_tn(dproj, hn, "grad_w_in_b", lp, half // 2, D_MODEL, tk, m=half, m_off=2, swap=([dwt_a], [1]))
    (rb_b,) = _sibling_exchange([dwt_b], [1])
    *hs_rows, sc = _add_sibling_half(gs_rows, rb_rows, [1, 1, 1], small, sr, core)
    h_a, h_b, h_f = _add_halves_2d([dwt_a, dwt_b, dwt_f], [rb_a, rb_b, rb_f], core)
    h_in = jnp.concatenate([h_a[E_U:E_ZA], h_b[D_MODEL:], h_a[E_ZA:E_GP], h_f[:N_HEADS], h_a[E_GP:], h_b[:D_MODEL]], axis=0)
    hs = [h_in.reshape(N_CHIPS, -1, D_MODEL // 2)] + hs_rows
    grad_axes = [2, 1, 1, 1]
    (grad_x, g_block0, dg1), qs, sq = _input_bwd(dproj, df, wt_e, wt_qkv, wt_f, x2, metapad, dh1, norm_g, hs, sc)
    g_w_in_rows, g_w_up_p, g_w_up_a, g_w_out, st = _reduce_allgather(qs, grad_axes, sq, [True, False, False, False])
    late = _small_allreduce(jnp.concatenate([pad8(dg1.reshape(-1, LANES)), g_block0[PAD_ROWS:].reshape(-1, LANES)], axis=0))
    n_norm = D_MODEL // LANES
    g_norm = late[:n_norm].reshape(1, D_MODEL)
    chip = 2 * lax.axis_index("x") + lax.axis_index("y")
    g_meta = lax.dynamic_slice_in_dim(late[n_norm:].reshape(N_META, D_MODEL), chip * sh_d, sh_d, axis=1)

    spiece = lambda k, rows: st[soffs[k]:soffs[k] + rows]
    g_final = spiece(0, D_MODEL // LANES).reshape(1, D_MODEL)
    g_scale = spiece(1, POOL_WIDTH // LANES).reshape(1, POOL_WIDTH)
    g_bf = spiece(2, 1)
    g_pw = spiece(3, POOL_WIDTH)
    loss = st[soffs[4], 0]

    def pad_lanes(a):
        return jnp.pad(a, ((0, 0), (0, F_COLS - N_HEADS)))

    w_in_res = (g_w_in_rows,) + _adamw_native(to_rows(w_in), g_w_in_rows, to_rows(m_w_in), to_rows(v_w_in), "adamw_w_in")

    upd = [
        ("meta_tokens", meta_tokens, g_meta, m_meta_tokens, v_meta_tokens),
        ("norm_g", norm_g, g_norm, m_norm_g, v_norm_g),
        ("w_in", None, None, None, None),
        ("b_forget", pad_lanes(b_forget), g_bf, pad_lanes(m_b_forget), pad_lanes(v_b_forget)),
        ("pool_w", pool_w.reshape(-1, LANES), g_pw, m_pool_w.reshape(-1, LANES), v_pool_w.reshape(-1, LANES)),
        ("pool_scale", pool_scale, g_scale, m_pool_scale, v_pool_scale),
        ("w_up_pool", w_up_pool[0], g_w_up_p, m_w_up_pool[0], v_w_up_pool[0]),
        ("w_up_attn", w_up_attn[0], g_w_up_a, m_w_up_attn[0], v_w_up_attn[0]),
        ("w_out", w_out[0], g_w_out, m_w_out[0], v_w_out[0]),
        ("final_norm_g", gf, g_final, m_final_norm_g.reshape(1, D_MODEL), v_final_norm_g.reshape(1, D_MODEL)),
    ]
    shapes = [meta_tokens.shape, norm_g.shape, w_in.shape, b_forget.shape, pool_w.shape, pool_scale.shape,
              w_up_pool.shape, w_up_attn.shape, w_out.shape, final_norm_g.shape]
    grads, deltas, new_ms, new_vs = [], [], [], []
    for (name, w_, g_, m_in, v_in), shp in zip(upd, shapes):
        if name == "w_in":
            res = tuple(from_rows(a) for a in w_in_res)
        else:
            d_, mn_, vn_ = _adamw(w_, g_, m_in, v_in, "adamw_" + name)
            res = (g_, d_, mn_, vn_)
        if name == "b_forget":
            res = tuple(a[:, :N_HEADS] for a in res)
        for lst, a in zip((grads, deltas, new_ms, new_vs), res):
            lst.append(a.reshape(shp))

    return (loss, grad_x.reshape(x.shape), *grads, *deltas, *new_ms, *new_vs)
```

```python
import jax
import jax.numpy as jnp
from jax import lax
from jax.experimental import pallas as pl
from jax.experimental.pallas import tpu as pltpu

F32 = jnp.float32
BF16 = jnp.bfloat16
MESH = pl.DeviceIdType.MESH
HIGHEST = lax.Precision.HIGHEST
HBM = pl.BlockSpec(memory_space=pltpu.HBM)

D_MODEL = 1024
N_META = 16
POOL_WIDTH = 512
POOL_GROUP = 128
POOL_WINDOWS = (2, 4, 8, 16)
N_HEADS = 8
HEAD_DIM = 64
ATTN_WIDTH = 512
RMS_EPS = 1e-6
N_CHIPS = 4

ADAM_LR = 0.001
ADAM_B1 = 0.9
ADAM_B2 = 0.999
ADAM_EPS = 1e-08
ADAM_WD = 0.01
ADAM_STEP = 10

LANES = 128
ROW_TILE = 256
KV_TILE = 512
KV_TILE_BWD = 1024
PAD_ROWS = ROW_TILE - N_META
NEG = -1e30

E_U, E_ZP, E_ZA, E_GP, E_GA, E_COLS = 0, 512, 1024, 1536, 2560, 3584
QKV_COLS = 3 * ATTN_WIDTH
MAIN_COLS = E_COLS + QKV_COLS
F_COLS = LANES
REF_U, REF_Q, REF_ZA, REF_F, REF_GP = 0, 1024, 2560, 3072, 3080
VMEM_LIMIT = 48 * 1024 * 1024


def _params(sem=None, vmem=VMEM_LIMIT):
    return pltpu.CompilerParams(dimension_semantics=sem, vmem_limit_bytes=vmem)


def _in_hbm(*arrays):
    return [pltpu.with_memory_space_constraint(a, pltpu.HBM) for a in arrays]


def _row_tile(n, target):
    best = 16
    for t in range(16, target + 1, 16):
        if n % t == 0:
            best = t
    return best


def _sigmoid(x):
    return 1.0 / (1.0 + jnp.exp(-x))


def _half(ref, axis, which):
    n = ref.shape[axis] // 2
    idx = [slice(None)] * len(ref.shape)
    idx[axis] = pl.ds(pl.multiple_of(which * n, n), n)
    return ref.at[tuple(idx)]


def _half_shape(shape, axis):
    s = list(shape)
    s[axis] //= 2
    return tuple(s)


def _gather_start(ins, outs, axes, send, recv, fsend, frecv, local):
    x, y, c = lax.axis_index("x"), lax.axis_index("y"), lax.axis_index("c")
    me = 2 * x + y
    for t in range(len(ins)):
        pltpu.make_async_copy(ins[t], outs[t].at[me], local.at[t]).start()
        for k, chip in enumerate([(1 - x, y), (x, 1 - y), (1 - x, 1 - y)]):
            pltpu.make_async_remote_copy(
                src_ref=_half(ins[t], axes[t], c), dst_ref=_half(outs[t].at[me], axes[t], c),
                send_sem=send.at[3 * t + k], recv_sem=recv.at[3 * t + k], device_id=(*chip, c), device_id_type=MESH).start()


def _gather_finish(ins, outs, axes, send, recv, fsend, frecv, local):
    x, y, c = lax.axis_index("x"), lax.axis_index("y"), lax.axis_index("c")
    me = 2 * x + y
    sibling = (x, y, 1 - c)
    chips = [(1 - x, y), (x, 1 - y), (1 - x, 1 - y)]
    n = len(ins)

    def over_ici(t, k, chip, dst_slot):
        return pltpu.make_async_remote_copy(
            src_ref=_half(ins[t], axes[t], c), dst_ref=_half(outs[t].at[dst_slot], axes[t], c),
            send_sem=send.at[3 * t + k], recv_sem=recv.at[3 * t + k], device_id=(*chip, c), device_id_type=MESH)

    def to_sibling(t, k, slot, which):
        return pltpu.make_async_remote_copy(
            src_ref=_half(outs[t].at[slot], axes[t], which), dst_ref=_half(outs[t].at[slot], axes[t], which),
            send_sem=fsend.at[3 * t + k], recv_sem=frecv.at[3 * t + k], device_id=sibling, device_id_type=MESH)

    forwards = []
    for t in range(n):
        for k, (px, py) in enumerate(chips):
            over_ici(t, k, (px, py), 2 * px + py).wait_recv()
            fw = to_sibling(t, k, 2 * px + py, c)
            fw.start()
            forwards.append(fw)
    for t in range(n):
        for k, (px, py) in enumerate(chips):
            to_sibling(t, k, 2 * px + py, 1 - c).wait_recv()
    for t in range(n):
        for k, chip in enumerate(chips):
            over_ici(t, k, chip, me).wait_send()
    for fw in forwards:
        fw.wait_send()
    for t in range(n):
        pltpu.make_async_copy(ins[t], outs[t].at[me], local.at[t]).wait()


def _gather_sems(n):
    return [pltpu.SemaphoreType.DMA((3 * n,)), pltpu.SemaphoreType.DMA((3 * n,)), pltpu.SemaphoreType.DMA((3 * n,)),
            pltpu.SemaphoreType.DMA((3 * n,)), pltpu.SemaphoreType.DMA((n,))]


def _gathered_shapes(shards):
    return [jax.ShapeDtypeStruct((N_CHIPS,) + s.shape, s.dtype) for s in shards]


def _swap_copies(srcs, dsts, axes, send, recv):
    x, y, c = lax.axis_index("x"), lax.axis_index("y"), lax.axis_index("c")
    return [pltpu.make_async_remote_copy(
        src_ref=srcs[t] if axes[t] is None else _half(srcs[t], axes[t], 1 - c), dst_ref=dsts[t],
        send_sem=send.at[t], recv_sem=recv.at[t], device_id=(x, y, 1 - c), device_id_type=MESH) for t in range(len(srcs))]


def _swap_shapes(srcs, axes):
    return [jax.ShapeDtypeStruct(g.shape if a is None else _half_shape(g.shape, a), g.dtype) for g, a in zip(srcs, axes)]


def _sibling_exchange(gs, axes):
    n = len(gs)

    def body(*refs):
        cps = _swap_copies(refs[:n], refs[n:2 * n], axes, *refs[2 * n:])
        for cp in cps:
            cp.start()
        for cp in cps:
            cp.wait()

    return pl.pallas_call(
        body, name="grad_sibling_exchange",
        in_specs=[HBM] * n, out_specs=[HBM] * n, out_shape=_swap_shapes(gs, axes),
        scratch_shapes=[pltpu.SemaphoreType.DMA((n,)), pltpu.SemaphoreType.DMA((n,))],
    )(*_in_hbm(*gs))


def _add_halves_2d(parts, rbs, core):
    n = len(parts)
    rows, w = rbs[0].shape
    tr = 512
    last_rows = rbs[-1].shape[0]

    def body(core_ref, *refs):
        p_refs, r_refs, h_refs = refs[:n], refs[n:2 * n], refs[2 * n:]
        for t in range(n - 1):
            h_refs[t][...] = (p_refs[t][...] + r_refs[t][...]).astype(BF16)

        @pl.when(pl.program_id(0) == 0)
        def _():
            h_refs[n - 1][...] = (p_refs[n - 1][...] + r_refs[n - 1][...]).astype(BF16)

    mine = lambda r: pl.BlockSpec((r, w), lambda i, cr: (i, cr[0]))
    tile = lambda r: pl.BlockSpec((r, w), lambda i, cr: (i, 0))
    mine_small = pl.BlockSpec((last_rows, w), lambda i, cr: (0, cr[0]))
    tile_small = pl.BlockSpec((last_rows, w), lambda i, cr: (0, 0))
    return pl.pallas_call(
        body, name="grad_add_sibling_w_in",
        grid_spec=pltpu.PrefetchScalarGridSpec(
            num_scalar_prefetch=1, grid=(rows // tr,),
            in_specs=[mine(tr)] * (n - 1) + [mine_small] + [tile(tr)] * (n - 1) + [tile_small],
            out_specs=[tile(tr)] * (n - 1) + [tile_small]),
        out_shape=[jax.ShapeDtypeStruct(r.shape, BF16) for r in rbs],
        compiler_params=_params(("arbitrary",)),
    )(core, *parts, *rbs)


def _add_sibling_half(gs, rbs, axes, small, sr, core):
    n = len(gs)

    def body(core_ref, *refs):
        g_refs, rb_refs = refs[:n], refs[n:2 * n]
        s_ref, sr_ref = refs[2 * n], refs[2 * n + 1]
        h_refs, sc_ref = refs[2 * n + 2:3 * n + 2], refs[3 * n + 2]
        for t in range(n):
            h_refs[t][...] = (g_refs[t][...] + rb_refs[t][...]).astype(BF16)
        sc_ref[...] = s_ref[...] + sr_ref[...]

    def mine(rb, axis):
        blk = (None,) + rb.shape[1:]
        if axis == 2:
            return pl.BlockSpec(blk, lambda j, cr: (j, 0, cr[0]))
        return pl.BlockSpec(blk, lambda j, cr: (j, cr[0], 0))

    chunk = lambda rb: pl.BlockSpec((None,) + rb.shape[1:], lambda j, cr: (j, 0, 0))
    whole = pl.BlockSpec(small.shape, lambda j, cr: (0, 0))
    return pl.pallas_call(
        body, name="grad_add_sibling",
        grid_spec=pltpu.PrefetchScalarGridSpec(
            num_scalar_prefetch=1, grid=(N_CHIPS,),
            in_specs=[mine(rb, a) for rb, a in zip(rbs, axes)] + [chunk(rb) for rb in rbs] + [whole, whole],
            out_specs=[chunk(rb) for rb in rbs] + [whole]),
        out_shape=[jax.ShapeDtypeStruct(rb.shape, BF16) for rb in rbs] + [jax.ShapeDtypeStruct(small.shape, F32)],
        compiler_params=_params(("arbitrary",)),
    )(core, *gs, *rbs, small, sr)


def _chip_scatter_plan(h_refs, s_ref, q_refs, sq_ref, send, recv, local):
    n = len(h_refs)
    x, y, c = lax.axis_index("x"), lax.axis_index("y"), lax.axis_index("c")
    me = 2 * x + y
    chips = [(1 - x, y), (x, 1 - y), (1 - x, 1 - y)]

    def copy(t, k, chip, src_slot, dst_slot):
        src = h_refs[t].at[src_slot] if t < n else s_ref
        dst = (q_refs[t] if t < n else sq_ref).at[dst_slot]
        return pltpu.make_async_remote_copy(src_ref=src, dst_ref=dst, send_sem=send.at[3 * t + k],
                                            recv_sem=recv.at[3 * t + k], device_id=(*chip, c), device_id_type=MESH)

    own = [pltpu.make_async_copy(h_refs[t].at[me], q_refs[t].at[me], local.at[t]) for t in range(n)]
    own.append(pltpu.make_async_copy(s_ref, sq_ref.at[me], local.at[n]))
    sends = [copy(t, k, (px, py), 2 * px + py, me) for t in range(n + 1) for k, (px, py) in enumerate(chips)]
    recvs = [copy(t, k, (px, py), me, 2 * px + py) for t in range(n + 1) for k, (px, py) in enumerate(chips)]
    return own, sends, recvs


def _small_allreduce(buf):
    def body(b_ref, o_ref, sib_buf, chip_buf, send, recv):
        x, y, c = lax.axis_index("x"), lax.axis_index("y"), lax.axis_index("c")
        me = 2 * x + y
        chips = [(1 - x, y), (x, 1 - y), (1 - x, 1 - y)]
        swap = pltpu.make_async_remote_copy(src_ref=b_ref, dst_ref=sib_buf, send_sem=send.at[0], recv_sem=recv.at[0],
                                            device_id=(x, y, 1 - c), device_id_type=MESH)
        swap.start()
        swap.wait()
        chip_buf[me] = b_ref[...] + sib_buf[...]

        def copy(k, chip, slot):
            return pltpu.make_async_remote_copy(src_ref=chip_buf.at[slot], dst_ref=chip_buf.at[slot], send_sem=send.at[1 + k],
                                                recv_sem=recv.at[1 + k], device_id=(*chip, c), device_id_type=MESH)

        sends = [copy(k, chip, me) for k, chip in enumerate(chips)]
        for cp in sends:
            cp.start()
        for k, (px, py) in enumerate(chips):
            copy(k, (px, py), 2 * px + py).wait_recv()
        for cp in sends:
            cp.wait_send()
        o_ref[...] = ((chip_buf[0] + chip_buf[1]) + chip_buf[2]) + chip_buf[3]

    vmem = pl.BlockSpec(memory_space=pltpu.VMEM)
    return pl.pallas_call(
        body, name="small_allreduce", in_specs=[vmem], out_specs=vmem,
        out_shape=jax.ShapeDtypeStruct(buf.shape, F32),
        scratch_shapes=[pltpu.VMEM(buf.shape, F32), pltpu.VMEM((N_CHIPS,) + buf.shape, F32),
                        pltpu.SemaphoreType.DMA((4,)), pltpu.SemaphoreType.DMA((4,))],
        compiler_params=_params(),
    )(buf)


def _reduce_allgather(qs, axes, sq, as_rows):
    n = len(qs)
    shard_shapes, half_axes = [], []
    for q, a, rows_form in zip(qs, axes, as_rows):
        shape = [d * 2 if i == a - 1 else d for i, d in enumerate(q.shape[1:])]
        if rows_form:
            assert a == 2
            shape = [shape[0], 1, shape[1]]
        shard_shapes.append(tuple(shape))
        half_axes.append(2 if rows_form else a - 1)

    def body(*refs):
        q_refs, sq_ref = refs[:n], refs[n]
        o_refs, st_ref = refs[n + 1:2 * n + 1], refs[2 * n + 1]
        send, recv = refs[2 * n + 2:]
        x, y, c = lax.axis_index("x"), lax.axis_index("y"), lax.axis_index("c")

        def swap(t, which):
            return pltpu.make_async_remote_copy(
                src_ref=_half(o_refs[t], half_axes[t], which), dst_ref=_half(o_refs[t], half_axes[t], which),
                send_sem=send.at[t], recv_sem=recv.at[t], device_id=(x, y, 1 - c), device_id_type=MESH)

        sent = []
        for t in range(n):
            q = q_refs[t]
            total = ((q[0].astype(F32) + q[1].astype(F32)) + q[2].astype(F32)) + q[3].astype(F32)
            if as_rows[t]:
                total = total.reshape(total.shape[0], 1, total.shape[1])
            _half(o_refs[t], half_axes[t], c)[...] = total
            cp = swap(t, c)
            cp.start()
            sent.append(cp)
        st_ref[...] = ((sq_ref[0] + sq_ref[1]) + sq_ref[2]) + sq_ref[3]
        for t in range(n):
            swap(t, 1 - c).wait_recv()
        for cp in sent:
            cp.wait_send()

    vmem = pl.BlockSpec(memory_space=pltpu.VMEM)
    return pl.pallas_call(
        body, name="grad_reduce_allgather",
        in_specs=[vmem] * (n + 1), out_specs=[vmem] * (n + 1),
        out_shape=[jax.ShapeDtypeStruct(s, F32) for s in shard_shapes] + [jax.ShapeDtypeStruct(sq.shape[1:], F32)],
        scratch_shapes=[pltpu.SemaphoreType.DMA((n,)), pltpu.SemaphoreType.DMA((n,))],
        compiler_params=_params(),
    )(*qs, sq)


def _dot_nt(a, b):
    return lax.dot_general(a, b, (((1,), (1,)), ((), ())), preferred_element_type=F32)


def _dot_tn(a, b):
    return lax.dot_general(a, b, (((0,), (0,)), ((), ())), preferred_element_type=F32)


def _mm_nt(a, bt, out_dtype, name, m, n, tm, tn, row_block=0, scale_first=None, gather=None):
    k = a.shape[1]
    shards, axes = gather if gather is not None else ([], [])
    ng = len(shards)
    grid = (m // tm, n // tn)

    def body(a_ref, b_ref, *rest):
        ins, o_ref, outs, sems = rest[:ng], rest[ng], rest[ng + 1:2 * ng + 1], rest[2 * ng + 1:]
        first = (pl.program_id(0) == 0) & (pl.program_id(1) == 0)
        last = (pl.program_id(0) == grid[0] - 1) & (pl.program_id(1) == grid[1] - 1)
        if ng:
            @pl.when(first)
            def _():
                _gather_start(ins, outs, axes, *sems)

        r = _dot_nt(a_ref[...], b_ref[...])
        if scale_first is not None:
            r = r * jnp.where(pl.program_id(1) == 0, scale_first, 1.0)
        o_ref[...] = r.astype(out_dtype)

        if ng:
            @pl.when(last)
            def _():
                _gather_finish(ins, outs, axes, *sems)

    res = pl.pallas_call(
        body, name=name, grid=grid,
        in_specs=[pl.BlockSpec((tm, k), lambda i, j: (i, 0)), pl.BlockSpec((tn, k), lambda i, j: (row_block + j, 0))]
        + [HBM] * ng,
        out_specs=[pl.BlockSpec((tm, tn), lambda i, j: (i, j))] + [HBM] * ng,
        out_shape=[jax.ShapeDtypeStruct((m, n), out_dtype)] + _gathered_shapes(shards),
        scratch_shapes=_gather_sems(ng) if ng else [],
        compiler_params=_params(("arbitrary", "arbitrary") if ng else ("parallel", "parallel")),
    )(a, bt, *_in_hbm(*shards))
    return (res[0], res[1:]) if ng else res[0]


def _mm_tn(a, b, name, k, tm, tn, tk, chunks=1, m=None, m_off=0, swap=None):
    m = a.shape[1] if m is None else m
    n = b.shape[1]
    cw = n // chunks
    srcs, axes = swap if swap is not None else ([], [])
    ns = len(srcs)
    grid = (m // tm, n // tn, k // tk)

    def body(a_ref, b_ref, *rest):
        s_refs, o_ref, d_refs, sems = rest[:ns], rest[ns], rest[ns + 1:2 * ns + 1], rest[2 * ns + 1:]
        ids = [pl.program_id(d) for d in range(3)]
        if ns:
            @pl.when((ids[0] == 0) & (ids[1] == 0) & (ids[2] == 0))
            def _():
                for cp in _swap_copies(s_refs, d_refs, axes, *sems):
                    cp.start()

        @pl.when(ids[2] == 0)
        def _():
            o_ref[...] = jnp.zeros_like(o_ref)
        r = _dot_tn(a_ref[...].astype(BF16), b_ref[...].astype(BF16))
        if chunks > 1:
            for c in range(chunks):
                o_ref[c] += r[:, c * cw:(c + 1) * cw]
        else:
            o_ref[...] += r

        if ns:
            @pl.when((ids[0] == grid[0] - 1) & (ids[1] == grid[1] - 1) & (ids[2] == grid[2] - 1))
            def _():
                for cp in _swap_copies(s_refs, d_refs, axes, *sems):
                    cp.wait()

    if chunks > 1:
        assert tn == n
        out_spec = pl.BlockSpec((chunks, tm, cw), lambda i, j, kk: (0, i, 0))
        out_shape = jax.ShapeDtypeStruct((chunks, m, cw), F32)
    else:
        out_spec = pl.BlockSpec((tm, tn), lambda i, j, kk: (i, j))
        out_shape = jax.ShapeDtypeStruct((m, n), F32)
    res = pl.pallas_call(
        body, name=name, grid=grid,
        in_specs=[pl.BlockSpec((tk, tm), lambda i, j, kk: (kk, m_off + i)), pl.BlockSpec((tk, tn), lambda i, j, kk: (kk, j))]
        + [HBM] * ns,
        out_specs=[out_spec] + [HBM] * ns, out_shape=[out_shape] + _swap_shapes(srcs, axes),
        scratch_shapes=[pltpu.SemaphoreType.DMA((ns,)), pltpu.SemaphoreType.DMA((ns,))] if ns else [],
        compiler_params=_params(("arbitrary",) * 3 if ns else ("parallel", "parallel", "arbitrary")),
    )(a, b, *_in_hbm(*srcs))
    return (res[0], res[1:]) if ns else res[0]


def _tokens_spec():
    return pl.BlockSpec((ROW_TILE, D_MODEL), lambda i: (jnp.maximum(i - 1, 0), 0))


def _rmsnorm_fwd(x2, g1, lk, shards, axes):
    nb = x2.shape[0] // ROW_TILE + 1
    nblk = lk // ROW_TILE
    ng = len(shards)
    meta_cols = shards[-1].shape[1]

    def body(x_ref, g_ref, *rest):
        ins, hn_ref, outs = rest[:ng], rest[ng], rest[ng + 1:2 * ng + 1]
        sems, meta_buf, meta_sem = rest[2 * ng + 1:2 * ng + 6], rest[2 * ng + 6], rest[2 * ng + 7]
        s = pl.program_id(0)
        blk = (s + 1) % nblk

        @pl.when(s == 0)
        def _():
            _gather_start(ins, outs, axes, *sems)

        def normed(h):
            r = lax.rsqrt(jnp.mean(h * h, axis=-1, keepdims=True) + RMS_EPS)
            return ((h * r) * g_ref[...]).astype(BF16)

        @pl.when(s < nblk - 1)
        def _():
            hn_ref[...] = normed(jnp.where(blk >= nb, 0.0, x_ref[...]))

        @pl.when(s == nblk - 1)
        def _():
            _gather_finish(ins, outs, axes, *sems)
            fetch = pltpu.make_async_copy(outs[-1], meta_buf, meta_sem.at[0])
            fetch.start()
            fetch.wait()
            meta = jnp.concatenate([meta_buf[j] for j in range(N_CHIPS)], axis=1)
            hn_ref[...] = normed(jnp.concatenate([jnp.zeros((PAD_ROWS, D_MODEL), F32), meta], axis=0))

    res = pl.pallas_call(
        body, name="rmsnorm_fwd", grid=(nblk,),
        in_specs=[pl.BlockSpec((ROW_TILE, D_MODEL), lambda s: (jnp.clip((s + 1) % nblk - 1, 0, nb - 2), 0)),
                  pl.BlockSpec((1, D_MODEL), lambda s: (0, 0))] + [HBM] * ng,
        out_specs=[pl.BlockSpec((ROW_TILE, D_MODEL), lambda s: ((s + 1) % nblk, 0))] + [HBM] * ng,
        out_shape=[jax.ShapeDtypeStruct((lk, D_MODEL), BF16)] + _gathered_shapes(shards),
        scratch_shapes=_gather_sems(ng) + [pltpu.VMEM((N_CHIPS, N_META, meta_cols), F32), pltpu.SemaphoreType.DMA((1,))],
        compiler_params=_params(("arbitrary",)),
    )(x2, g1, *_in_hbm(*shards))
    return res[0], res[1:]


def _valid_gate_mask(i):
    row = i * ROW_TILE + lax.broadcasted_iota(jnp.int32, (ROW_TILE, F_COLS), 0)
    col = lax.broadcasted_iota(jnp.int32, (ROW_TILE, F_COLS), 1)
    return (row >= PAD_ROWS) & (col < N_HEADS)


def _gates_fwd(f, bfg):
    nb = f.shape[0] // ROW_TILE

    def body(f_ref, b_ref, c_ref, carry):
        i = pl.program_id(0)

        @pl.when(i == 0)
        def _():
            carry[...] = jnp.zeros_like(carry)

        logit = f_ref[...] + b_ref[...]
        lf = jnp.minimum(logit, 0.0) - jnp.log1p(jnp.exp(-jnp.abs(logit)))
        lf = jnp.where(_valid_gate_mask(i), lf, 0.0)
        r_i = lax.broadcasted_iota(jnp.int32, (ROW_TILE, ROW_TILE), 0)
        c_i = lax.broadcasted_iota(jnp.int32, (ROW_TILE, ROW_TILE), 1)
        tri = (c_i <= r_i).astype(F32)
        c_ref[...] = jnp.dot(tri, lf, precision=HIGHEST, preferred_element_type=F32) + carry[...]
        carry[...] = carry[...] + jnp.sum(lf, axis=0, keepdims=True)

    return pl.pallas_call(
        body, name="gates_fwd", grid=(nb,),
        in_specs=[pl.BlockSpec((ROW_TILE, F_COLS), lambda i: (i, 0)), pl.BlockSpec((1, F_COLS), lambda i: (0, 0))],
        out_specs=pl.BlockSpec((ROW_TILE, F_COLS), lambda i: (i, 0)),
        out_shape=jax.ShapeDtypeStruct(f.shape, F32),
        scratch_shapes=[pltpu.VMEM((1, F_COLS), F32)],
        compiler_params=_params(("arbitrary",)),
    )(f, bfg)


def _pool_counts(i):
    row = i * ROW_TILE + lax.broadcasted_iota(jnp.int32, (ROW_TILE, 1), 0)
    return jnp.maximum(row - PAD_ROWS, 0)


def _trailing_sums(xc, levels):
    acc = xc
    for lv in range(levels):
        acc = acc + pltpu.roll(acc, 1 << lv, 0)
    return acc


def _leading_sums(xc, levels):
    n = xc.shape[0]
    acc = xc
    for lv in range(levels):
        acc = acc + pltpu.roll(acc, n - (1 << lv), 0)
    return acc


def _pool_p(u_cur, u_prev, i):
    pos = _pool_counts(i)
    ps, invs = [], []
    for g, w in enumerate(POOL_WINDOWS):
        sl = slice(g * POOL_GROUP, (g + 1) * POOL_GROUP)
        cur = u_cur[:, sl]
        xc = jnp.concatenate([u_prev[:, sl], cur], axis=0)
        win = _trailing_sums(xc, g + 1)[ROW_TILE:, :]
        inv = 1.0 / jnp.minimum(pos + 1, w).astype(F32)
        ps.append(win * inv - cur)
        invs.append(inv)
    return ps, invs


def _pool_fwd(e, pw, scale):
    nb = e.shape[0] // ROW_TILE

    def body(uc_ref, up_ref, z_ref, pw_ref, sc_ref, y_ref):
        i = pl.program_id(0)
        u_cur = uc_ref[...].astype(F32)
        u_prev = jnp.where(i == 0, 0.0, up_ref[...].astype(F32))
        ps, _ = _pool_p(u_cur, u_prev, i)
        z = z_ref[...].astype(F32)
        gate = z * _sigmoid(z)
        for g in range(len(POOL_WINDOWS)):
            sl = slice(g * POOL_GROUP, (g + 1) * POOL_GROUP)
            yraw = jnp.dot(ps[g].astype(BF16), pw_ref[g], preferred_element_type=F32)
            y_ref[:, sl] = ((yraw * sc_ref[:, sl]) * gate[:, sl]).astype(BF16)

    blk = (ROW_TILE, POOL_WIDTH)
    return pl.pallas_call(
        body, name="pool_fwd", grid=(nb,),
        in_specs=[pl.BlockSpec(blk, lambda i: (i, 0)), pl.BlockSpec(blk, lambda i: (jnp.maximum(i - 1, 0), 0)),
                  pl.BlockSpec(blk, lambda i: (i, 1)),
                  pl.BlockSpec((len(POOL_WINDOWS), POOL_GROUP, POOL_GROUP), lambda i: (0, 0, 0)),
                  pl.BlockSpec((1, POOL_WIDTH), lambda i: (0, 0))],
        out_specs=pl.BlockSpec(blk, lambda i: (i, 0)),
        out_shape=jax.ShapeDtypeStruct((e.shape[0], POOL_WIDTH), BF16),
        compiler_params=_params(("parallel",)),
    )(e, e, e, pw, scale)


def _stack_heads(a):
    first = lax.broadcasted_iota(jnp.int32, a.shape, 1) < HEAD_DIM
    zero = jnp.zeros_like(a)
    return jnp.concatenate([jnp.where(first, a, zero), jnp.where(first, zero, a)], axis=0)


def _unstack_heads(a):
    rows = a.shape[0] // 2
    first = lax.broadcasted_iota(jnp.int32, (rows, LANES), 1) < HEAD_DIM
    return jnp.where(first, a[:rows], a[rows:])


def _causal(i, jj, stacked, kv_tile):
    r = lax.broadcasted_iota(jnp.int32, (stacked * ROW_TILE, kv_tile), 0)
    if stacked == 2:
        r = jnp.where(r >= ROW_TILE, r - ROW_TILE, r)
    kidx = jj * kv_tile + lax.broadcasted_iota(jnp.int32, (stacked * ROW_TILE, kv_tile), 1)
    return kidx <= i * ROW_TILE + r


def _stack_rows(b):
    n = b.shape[1]
    return jnp.concatenate([jnp.broadcast_to(b[0:1], (ROW_TILE, n)), jnp.broadcast_to(b[1:2], (ROW_TILE, n))], axis=0)


def _attn_fwd(qkv, bias, nb):
    lk = qkv.shape[0]
    lp = nb * ROW_TILE
    nkb = lk // KV_TILE
    n_pairs = N_HEADS // 2

    def body(q_ref, k_ref, v_ref, b_ref, o_ref, lse_ref):
        i = pl.program_id(1)
        qs = _stack_heads(q_ref[...])
        last = (i * ROW_TILE) // KV_TILE

        def block(jj, carry, masked):
            m, l, acc = carry
            rows = pl.ds(pl.multiple_of(jj * KV_TILE, KV_TILE), KV_TILE)
            s = _dot_nt(qs, k_ref[rows, :]) - _stack_rows(b_ref[0, 0, jj])
            if masked:
                s = jnp.where(_causal(i, jj, 2, KV_TILE), s, NEG)
            m_new = jnp.maximum(m, jnp.max(s, axis=1, keepdims=True))
            alpha = jnp.exp(m - m_new)
            p = jnp.exp(s - m_new)
            l = alpha * l + jnp.sum(p, axis=1, keepdims=True)
            acc = alpha * acc + jnp.dot(p.astype(BF16), v_ref[rows, :], preferred_element_type=F32)
            return m_new, l, acc

        init = (jnp.full((2 * ROW_TILE, 1), NEG, F32), jnp.zeros((2 * ROW_TILE, 1), F32),
                jnp.zeros((2 * ROW_TILE, LANES), F32))
        carry = lax.fori_loop(0, last, lambda jj, c: block(jj, c, False), init)
        m, l, acc = block(last, carry, True)
        o_ref[...] = _unstack_heads(acc / l)
        lse_ref[...] = _unstack_heads(jnp.broadcast_to(m + jnp.log(l), (2 * ROW_TILE, LANES)))

    return pl.pallas_call(
        body, name="attn_fwd", grid=(n_pairs, nb),
        in_specs=[pl.BlockSpec((ROW_TILE, LANES), lambda hp, i: (i, hp)),
                  pl.BlockSpec((lk, LANES), lambda hp, i: (0, n_pairs + hp)),
                  pl.BlockSpec((lk, LANES), lambda hp, i: (0, 2 * n_pairs + hp)),
                  pl.BlockSpec((1, 1, nkb, 2, KV_TILE), lambda hp, i: (i, hp, 0, 0, 0))],
        out_specs=[pl.BlockSpec((ROW_TILE, LANES), lambda hp, i: (i, hp)),
                   pl.BlockSpec((ROW_TILE, LANES), lambda hp, i: (i, hp))],
        out_shape=[jax.ShapeDtypeStruct((lp, ATTN_WIDTH), F32), jax.ShapeDtypeStruct((lp, ATTN_WIDTH), F32)],
        compiler_params=_params(("parallel", "parallel")),
    )(qkv, qkv, qkv, bias)


def _merge_fwd(y_pool, o, e, wup_p, wup_a):
    lp = o.shape[0]
    nb = lp // ROW_TILE

    def body(yp_ref, o_ref, e_ref, wp_ref, wa_ref, mg_ref, ya_ref):
        za = e_ref[:, E_ZA:E_GP].astype(F32)
        ya = (o_ref[...] * (za * _sigmoid(za))).astype(BF16)
        ya_ref[...] = ya
        a_pool = jnp.dot(yp_ref[...], wp_ref[...], preferred_element_type=F32)
        a_attn = jnp.dot(ya, wa_ref[...], preferred_element_type=F32)
        mg_ref[...] = (_sigmoid(e_ref[:, E_GP:E_GA].astype(F32)) * a_pool
                       + _sigmoid(e_ref[:, E_GA:E_COLS].astype(F32)) * a_attn).astype(BF16)

    return pl.pallas_call(
        body, name="merge_fwd", grid=(nb,),
        in_specs=[pl.BlockSpec((ROW_TILE, POOL_WIDTH), lambda i: (i, 0)),
                  pl.BlockSpec((ROW_TILE, ATTN_WIDTH), lambda i: (i, 0)),
                  pl.BlockSpec((ROW_TILE, E_COLS), lambda i: (i, 0)),
                  pl.BlockSpec((POOL_WIDTH, D_MODEL), lambda i: (0, 0)),
                  pl.BlockSpec((ATTN_WIDTH, D_MODEL), lambda i: (0, 0))],
        out_specs=[pl.BlockSpec((ROW_TILE, D_MODEL), lambda i: (i, 0)),
                   pl.BlockSpec((ROW_TILE, ATTN_WIDTH), lambda i: (i, 0))],
        out_shape=[jax.ShapeDtypeStruct((lp, D_MODEL), BF16), jax.ShapeDtypeStruct((lp, ATTN_WIDTH), BF16)],
        compiler_params=_params(("parallel",)),
    )(y_pool, o, e, wup_p, wup_a)


def _head_fwd_bwd(merged, w_out, x2, metapad, gf, target):
    lp = merged.shape[0]
    nb = lp // ROW_TILE

    def body(mg_ref, w_ref, x_ref, mp_ref, g_ref, t_ref, dh_ref, loss_ref, dg_ref):
        i = pl.program_id(0)

        @pl.when(i == 0)
        def _():
            loss_ref[...] = jnp.zeros_like(loss_ref)
            dg_ref[...] = jnp.zeros_like(dg_ref)

        h0 = jnp.where(i == 0, mp_ref[...], x_ref[...])
        h1 = h0 + jnp.dot(mg_ref[...], w_ref[...], preferred_element_type=F32)
        r = lax.rsqrt(jnp.mean(h1 * h1, axis=-1, keepdims=True) + RMS_EPS)
        xhat = h1 * r
        g = g_ref[...]
        err = jnp.where(i == 0, 0.0, xhat * g - t_ref[...])
        loss_ref[...] += 0.5 * jnp.sum(jnp.mean(err * err, axis=-1, keepdims=True))
        dy = err / D_MODEL
        dg_ref[...] += jnp.sum(dy * xhat, axis=0, keepdims=True)
        dxhat = dy * g
        dh_ref[...] = r * (dxhat - xhat * jnp.mean(dxhat * xhat, axis=-1, keepdims=True))

    return pl.pallas_call(
        body, name="head_fwd_bwd", grid=(nb,),
        in_specs=[pl.BlockSpec((ROW_TILE, D_MODEL), lambda i: (i, 0)),
                  pl.BlockSpec((D_MODEL, D_MODEL), lambda i: (0, 0)),
                  _tokens_spec(), pl.BlockSpec((ROW_TILE, D_MODEL), lambda i: (0, 0)),
                  pl.BlockSpec((1, D_MODEL), lambda i: (0, 0)), _tokens_spec()],
        out_specs=[pl.BlockSpec((ROW_TILE, D_MODEL), lambda i: (i, 0)),
                   pl.BlockSpec((1, LANES), lambda i: (0, 0)), pl.BlockSpec((1, D_MODEL), lambda i: (0, 0))],
        out_shape=[jax.ShapeDtypeStruct((lp, D_MODEL), F32), jax.ShapeDtypeStruct((1, LANES), F32),
                   jax.ShapeDtypeStruct((1, D_MODEL), F32)],
        compiler_params=_params(("arbitrary",)),
    )(merged, w_out, x2, metapad, gf, target)


def _per_head_rowsum(t):
    head = lax.broadcasted_iota(jnp.int32, t.shape, 1) // HEAD_DIM
    out = jnp.zeros_like(t)
    for h in range(N_HEADS):
        sel = head == h
        out = jnp.where(sel, jnp.sum(jnp.where(sel, t, 0.0), axis=1, keepdims=True), out)
    return out


def _merge_bwd(dh1, w_out, y_pool, y_attn, wup_p, wup_a, e, o):
    lp = o.shape[0]
    nb = lp // ROW_TILE

    def body(dh_ref, wo_ref, yp_ref, ya_ref, wp_ref, wa_ref, e_ref, o_ref,
             dap_ref, daa_ref, dg_ref, dza_ref, do_ref, delta_ref, dyp_ref):
        dmerged = _dot_nt(dh_ref[...].astype(BF16), wo_ref[...])
        a_pool = jnp.dot(yp_ref[...], wp_ref[...], preferred_element_type=F32)
        a_attn = jnp.dot(ya_ref[...], wa_ref[...], preferred_element_type=F32)
        sp = _sigmoid(e_ref[:, E_GP:E_GA].astype(F32))
        sa = _sigmoid(e_ref[:, E_GA:E_COLS].astype(F32))
        dap = (dmerged * sp).astype(BF16)
        daa = (dmerged * sa).astype(BF16)
        dap_ref[...] = dap
        daa_ref[...] = daa
        dg_ref[:, :D_MODEL] = (dmerged * a_pool * (sp * (1.0 - sp))).astype(BF16)
        dg_ref[:, D_MODEL:] = (dmerged * a_attn * (sa * (1.0 - sa))).astype(BF16)
        dyp_ref[...] = _dot_nt(dap, wp_ref[...])
        dya = _dot_nt(daa, wa_ref[...])
        za = e_ref[:, E_ZA:E_GP].astype(F32)
        sz = _sigmoid(za)
        o = o_ref[...]
        do = dya * (za * sz)
        do_ref[...] = do.astype(BF16)
        dza_ref[...] = (dya * o * (sz * (1.0 + za * (1.0 - sz)))).astype(BF16)
        delta_ref[...] = _per_head_rowsum(do * o)

    row = lambda w: pl.BlockSpec((ROW_TILE, w), lambda i: (i, 0))
    full = lambda a: pl.BlockSpec(a.shape, lambda i: (0, 0))
    return pl.pallas_call(
        body, name="merge_bwd", grid=(nb,),
        in_specs=[row(D_MODEL), full(w_out), row(POOL_WIDTH), row(ATTN_WIDTH), full(wup_p), full(wup_a),
                  row(E_COLS), row(ATTN_WIDTH)],
        out_specs=[row(D_MODEL), row(D_MODEL), row(2 * D_MODEL), row(ATTN_WIDTH), row(ATTN_WIDTH),
                   row(ATTN_WIDTH), row(POOL_WIDTH)],
        out_shape=[jax.ShapeDtypeStruct((lp, D_MODEL), BF16), jax.ShapeDtypeStruct((lp, D_MODEL), BF16),
                   jax.ShapeDtypeStruct((lp, 2 * D_MODEL), BF16), jax.ShapeDtypeStruct((lp, ATTN_WIDTH), BF16),
                   jax.ShapeDtypeStruct((lp, ATTN_WIDTH), BF16), jax.ShapeDtypeStruct((lp, ATTN_WIDTH), F32),
                   jax.ShapeDtypeStruct((lp, POOL_WIDTH), F32)],
        compiler_params=_params(("parallel",)),
    )(dh1, w_out, y_pool, y_attn, wup_p, wup_a, e, o)


def _pool_bwd_local(e, dy_pool, pw, scale):
    lp = e.shape[0]
    nb = lp // ROW_TILE
    ng = len(POOL_WINDOWS)

    def body(uc_ref, up_ref, z_ref, dy_ref, pw_ref, sc_ref, dpc_ref, dz_ref, dsc_ref, dpw_ref):
        i = pl.program_id(0)

        @pl.when(i == 0)
        def _():
            dsc_ref[...] = jnp.zeros_like(dsc_ref)
            dpw_ref[...] = jnp.zeros_like(dpw_ref)

        u_cur = uc_ref[...].astype(F32)
        u_prev = jnp.where(i == 0, 0.0, up_ref[...].astype(F32))
        ps, invs = _pool_p(u_cur, u_prev, i)
        z = z_ref[...].astype(F32)
        sz = _sigmoid(z)
        dy = dy_ref[...]
        dypre = dy * (z * sz)
        dsilu = sz * (1.0 + z * (1.0 - sz))
        for g in range(ng):
            sl = slice(g * POOL_GROUP, (g + 1) * POOL_GROUP)
            pb = ps[g].astype(BF16)
            w = pw_ref[g]
            yraw = jnp.dot(pb, w, preferred_element_type=F32)
            sc = sc_ref[:, sl]
            dz_ref[:, sl] = (dy[:, sl] * (yraw * sc) * dsilu[:, sl]).astype(BF16)
            dsc_ref[:, sl] += jnp.sum(dypre[:, sl] * yraw, axis=0, keepdims=True)
            dyraw = (dypre[:, sl] * sc).astype(BF16)
            dpw_ref[g] += _dot_tn(pb, dyraw)
            dpc_ref[:, sl] = _dot_nt(dyraw, w) * invs[g]

    blk = (ROW_TILE, POOL_WIDTH)
    return pl.pallas_call(
        body, name="pool_bwd_local", grid=(nb,),
        in_specs=[pl.BlockSpec(blk, lambda i: (i, 0)), pl.BlockSpec(blk, lambda i: (jnp.maximum(i - 1, 0), 0)),
                  pl.BlockSpec(blk, lambda i: (i, 1)), pl.BlockSpec(blk, lambda i: (i, 0)),
                  pl.BlockSpec((ng, POOL_GROUP, POOL_GROUP), lambda i: (0, 0, 0)),
                  pl.BlockSpec((1, POOL_WIDTH), lambda i: (0, 0))],
        out_specs=[pl.BlockSpec(blk, lambda i: (i, 0)), pl.BlockSpec(blk, lambda i: (i, 0)),
                   pl.BlockSpec((1, POOL_WIDTH), lambda i: (0, 0)),
                   pl.BlockSpec((ng, POOL_GROUP, POOL_GROUP), lambda i: (0, 0, 0))],
        out_shape=[jax.ShapeDtypeStruct((lp, POOL_WIDTH), F32), jax.ShapeDtypeStruct((lp, POOL_WIDTH), BF16),
                   jax.ShapeDtypeStruct((1, POOL_WIDTH), F32),
                   jax.ShapeDtypeStruct((ng, POOL_GROUP, POOL_GROUP), F32)],
        compiler_params=_params(("arbitrary",)),
    )(e, e, e, dy_pool, pw, scale)


def _pool_bwd_window(dpc):
    lp = dpc.shape[0]
    nb = lp // ROW_TILE

    def body(cur_ref, nxt_ref, du_ref):
        i = pl.program_id(0)
        cur = cur_ref[...]
        nxt = jnp.where(i == nb - 1, 0.0, nxt_ref[...])
        pos = _pool_counts(i)
        for g, w in enumerate(POOL_WINDOWS):
            sl = slice(g * POOL_GROUP, (g + 1) * POOL_GROUP)
            xc = jnp.concatenate([cur[:, sl], nxt[:, sl]], axis=0)
            win = _leading_sums(xc, g + 1)[:ROW_TILE, :]
            dp = cur[:, sl] * jnp.minimum(pos + 1, w).astype(F32)
            du_ref[:, sl] = (win - dp).astype(BF16)

    blk = (ROW_TILE, POOL_WIDTH)
    return pl.pallas_call(
        body, name="pool_bwd_window", grid=(nb,),
        in_specs=[pl.BlockSpec(blk, lambda i: (i, 0)), pl.BlockSpec(blk, lambda i: (jnp.minimum(i + 1, nb - 1), 0))],
        out_specs=pl.BlockSpec(blk, lambda i: (i, 0)),
        out_shape=jax.ShapeDtypeStruct((lp, POOL_WIDTH), BF16),
        compiler_params=_params(("parallel",)),
    )(dpc, dpc)


def _attn_bwd(qkv, do, lse, delta, bias, nb):
    lk = qkv.shape[0]
    lp = nb * ROW_TILE
    nkb = lk // KV_TILE_BWD
    n_pairs = N_HEADS // 2
    per_kv = KV_TILE_BWD // ROW_TILE

    def body(q_ref, k_ref, v_ref, do_ref, lse_ref, dl_ref, b_ref, dq_ref, dk_ref, dv_ref, dc_ref, dcq_ref):
        jj = pl.program_id(1)

        @pl.when(jj == 0)
        def _():
            dq_ref[...] = jnp.zeros_like(dq_ref)
            dcq_ref[...] = jnp.zeros_like(dcq_ref)

        kb, vb = k_ref[...], v_ref[...]

        def block(i, carry, masked):
            dk_acc, dv_acc, dc_acc = carry
            rows = pl.ds(pl.multiple_of(i * ROW_TILE, ROW_TILE), ROW_TILE)
            qs = _stack_heads(q_ref[rows, :])
            dos = _stack_heads(do_ref[rows, :])
            lse_i, dl_i = lse_ref[rows, :], dl_ref[rows, :]
            s = _dot_nt(qs, kb)
            dp = _dot_nt(dos, vb)
            if masked:
                valid = _causal(i, jj, 1, KV_TILE_BWD)
            ps, dss, dcs, rowsums = [], [], [], []
            for hd in range(2):
                half = slice(hd * ROW_TILE, (hd + 1) * ROW_TILE)
                col = slice(hd * HEAD_DIM, hd * HEAD_DIM + 1)
                sh = s[half] - b_ref[i, 0, 0, hd:hd + 1, :]
                if masked:
                    sh = jnp.where(valid, sh, NEG)
                p = jnp.exp(sh - lse_i[:, col])
                ds = p * (dp[half] - dl_i[:, col])
                ps.append(p.astype(BF16))
                dss.append(ds.astype(BF16))
                dcs.append(jnp.sum(ds, axis=0, keepdims=True))
                rowsums.append(jnp.sum(ds, axis=1, keepdims=True))
            dsb = jnp.concatenate(dss, axis=0)
            dv_acc = dv_acc + _dot_tn(jnp.concatenate(ps, axis=0), dos)
            dk_acc = dk_acc + _dot_tn(dsb, qs)
            dc_acc = dc_acc - jnp.concatenate(dcs, axis=0)
            dq_ref[rows, :] += _unstack_heads(jnp.dot(dsb, kb, preferred_element_type=F32))
            dcq_ref[rows, :] += _unstack_heads(jnp.broadcast_to(jnp.concatenate(rowsums, axis=0), (2 * ROW_TILE, LANES)))
            return dk_acc, dv_acc, dc_acc

        init = (jnp.zeros((KV_TILE_BWD, LANES), F32), jnp.zeros((KV_TILE_BWD, LANES), F32),
                jnp.zeros((2, KV_TILE_BWD), F32))
        first_q = per_kv * jj
        diag_end = jnp.minimum(first_q + per_kv, nb)
        carry = lax.fori_loop(first_q, diag_end, lambda i, c: block(i, c, True), init)
        dk, dv, dc = lax.fori_loop(diag_end, nb, lambda i, c: block(i, c, False), carry)
        dk_ref[...] = dk.astype(BF16)
        dv_ref[...] = dv.astype(BF16)
        dc_ref[0, 0] = dc

    whole = lambda rows: pl.BlockSpec((rows, LANES), lambda hp, jj: (0, hp))
    kv_blk = lambda off: pl.BlockSpec((KV_TILE_BWD, LANES), lambda hp, jj: (jj, off + hp))
    return pl.pallas_call(
        body, name="attn_bwd", grid=(n_pairs, nkb),
        in_specs=[whole(lk), kv_blk(n_pairs), kv_blk(2 * n_pairs), whole(lp), whole(lp), whole(lp),
                  pl.BlockSpec((nb, 1, 1, 2, KV_TILE_BWD), lambda hp, jj: (0, hp, jj, 0, 0))],
        out_specs=[whole(lp), kv_blk(0), kv_blk(0),
                   pl.BlockSpec((1, 1, 2, KV_TILE_BWD), lambda hp, jj: (hp, jj, 0, 0)), whole(lp)],
        out_shape=[jax.ShapeDtypeStruct((lp, ATTN_WIDTH), F32), jax.ShapeDtypeStruct((lk, ATTN_WIDTH), BF16),
                   jax.ShapeDtypeStruct((lk, ATTN_WIDTH), BF16),
                   jax.ShapeDtypeStruct((n_pairs, nkb, 2, KV_TILE_BWD), F32),
                   jax.ShapeDtypeStruct((lp, ATTN_WIDTH), F32)],
        compiler_params=_params(("parallel", "arbitrary")),
    )(qkv, qkv, qkv, do, lse, delta, bias)


def _gates_bwd(dc, dcq, f, bfg):
    nb = f.shape[0] // ROW_TILE

    def body(dc_ref, dcq_ref, f_ref, b_ref, df_ref, db_ref, carry):
        step = pl.program_id(0)
        i = nb - 1 - step

        @pl.when(step == 0)
        def _():
            carry[...] = jnp.zeros_like(carry)
            db_ref[...] = jnp.zeros_like(db_ref)

        dcb = dc_ref[...]
        lane = lax.broadcasted_iota(jnp.int32, (ROW_TILE, F_COLS), 1)
        for h in range(N_HEADS):
            dcb = dcb + jnp.where(lane == h, dcq_ref[:, HEAD_DIM * h:HEAD_DIM * h + 1], 0.0)
        r_i = lax.broadcasted_iota(jnp.int32, (ROW_TILE, ROW_TILE), 0)
        c_i = lax.broadcasted_iota(jnp.int32, (ROW_TILE, ROW_TILE), 1)
        upper = (c_i >= r_i).astype(F32)
        dlf = jnp.dot(upper, dcb, precision=HIGHEST, preferred_element_type=F32) + carry[...]
        carry[...] = carry[...] + jnp.sum(dcb, axis=0, keepdims=True)
        logit = f_ref[...] + b_ref[...]
        dlogit = jnp.where(_valid_gate_mask(i), dlf * _sigmoid(-logit), 0.0)
        df_ref[...] = dlogit.astype(BF16)
        db_ref[...] += jnp.sum(dlogit, axis=0, keepdims=True)

    blk = pl.BlockSpec((ROW_TILE, F_COLS), lambda s: (nb - 1 - s, 0))
    wide = pl.BlockSpec((ROW_TILE, ATTN_WIDTH), lambda s: (nb - 1 - s, 0))
    one = pl.BlockSpec((1, F_COLS), lambda s: (0, 0))
    return pl.pallas_call(
        body, name="gates_bwd", grid=(nb,),
        in_specs=[blk, wide, blk, one], out_specs=[blk, one],
        out_shape=[jax.ShapeDtypeStruct(f.shape, BF16), jax.ShapeDtypeStruct((1, F_COLS), F32)],
        scratch_shapes=[pltpu.VMEM((1, F_COLS), F32)],
        compiler_params=_params(("arbitrary",)),
    )(dc, dcq, f, bfg)


def _input_bwd(dproj, df, wt_e, wt_qkv, wt_f, x2, metapad, dh1, g1, hs, sc):
    nb = x2.shape[0] // ROW_TILE + 1
    n = len(hs)

    def body(dp_ref, df_ref, we_ref, w_ref, wf_ref, x_ref, mp_ref, dh_ref, g_ref, *rest):
        h_refs, s_ref = rest[:n], rest[n]
        gx_ref, g0_ref, dg_ref = rest[n + 1:n + 4]
        q_refs, sq_ref = rest[n + 4:2 * n + 4], rest[2 * n + 4]
        sems = rest[2 * n + 5:]
        i = pl.program_id(0)

        @pl.when(i == 0)
        def _():
            dg_ref[...] = jnp.zeros_like(dg_ref)
            own, sends, _ = _chip_scatter_plan(h_refs, s_ref, q_refs, sq_ref, *sems)
            for cp in own + sends:
                cp.start()

        dhn = (jnp.dot(dp_ref[:, :E_COLS], we_ref[...], preferred_element_type=F32)
               + jnp.dot(dp_ref[:, E_COLS:], w_ref[...], preferred_element_type=F32)
               + jnp.dot(df_ref[...], wf_ref[...], preferred_element_type=F32))
        h0 = jnp.where(i == 0, mp_ref[...], x_ref[...])
        r = lax.rsqrt(jnp.mean(h0 * h0, axis=-1, keepdims=True) + RMS_EPS)
        xhat = h0 * r
        dg_ref[...] += jnp.sum(dhn * xhat, axis=0, keepdims=True)
        dxhat = dhn * g_ref[...]
        dh0 = dh_ref[...] + r * (dxhat - xhat * jnp.mean(dxhat * xhat, axis=-1, keepdims=True))
        gx_ref[...] = dh0

        @pl.when(i == 0)
        def _():
            g0_ref[...] = dh0

        @pl.when(i == nb - 1)
        def _():
            own, sends, recvs = _chip_scatter_plan(h_refs, s_ref, q_refs, sq_ref, *sems)
            for cp in recvs:
                cp.wait_recv()
            for cp in sends:
                cp.wait_send()
            for cp in own:
                cp.wait()

    const = lambda shape: pl.BlockSpec(shape, lambda i: (0, 0))
    res = pl.pallas_call(
        body, name="input_bwd", grid=(nb,),
        in_specs=[pl.BlockSpec((ROW_TILE, MAIN_COLS), lambda i: (i, 0)), pl.BlockSpec((ROW_TILE, F_COLS), lambda i: (i, 0)),
                  const((E_COLS, D_MODEL)), const((QKV_COLS, D_MODEL)), const((F_COLS, D_MODEL)),
                  _tokens_spec(), const((ROW_TILE, D_MODEL)),
                  pl.BlockSpec((ROW_TILE, D_MODEL), lambda i: (i, 0)), const((1, D_MODEL))] + [HBM] * (n + 1),
        out_specs=[_tokens_spec(), const((ROW_TILE, D_MODEL)), const((1, D_MODEL))] + [HBM] * (n + 1),
        out_shape=[jax.ShapeDtypeStruct(x2.shape, F32), jax.ShapeDtypeStruct((ROW_TILE, D_MODEL), F32),
                   jax.ShapeDtypeStruct((1, D_MODEL), F32)]
        + [jax.ShapeDtypeStruct(h.shape, h.dtype) for h in hs] + [jax.ShapeDtypeStruct((N_CHIPS,) + sc.shape, sc.dtype)],
        scratch_shapes=[pltpu.SemaphoreType.DMA((3 * (n + 1),)), pltpu.SemaphoreType.DMA((3 * (n + 1),)),
                        pltpu.SemaphoreType.DMA((n + 1,))],
        compiler_params=_params(("arbitrary",), vmem=56 * 1024 * 1024),
    )(dproj, df, wt_e, wt_qkv, wt_f, x2, metapad, dh1, g1, *_in_hbm(*hs, sc))
    return res[:3], res[3:3 + n], res[3 + n]


def _adamw(w, g, m, v, name):
    rows, cols = w.shape
    if rows % 8 == 0:
        tr, tc = _row_tile8(rows), cols
    else:
        tr, tc = rows, (2 * LANES if cols % (2 * LANES) == 0 and rows > 8 else cols)

    def body(w_ref, g_ref, m_ref, v_ref, d_ref, mo_ref, vo_ref):
        g_ = g_ref[...]
        m_new = ADAM_B1 * m_ref[...] + (1.0 - ADAM_B1) * g_
        v_new = ADAM_B2 * v_ref[...] + (1.0 - ADAM_B2) * (g_ * g_)
        m_hat = m_new / (1.0 - ADAM_B1 ** ADAM_STEP)
        v_hat = v_new / (1.0 - ADAM_B2 ** ADAM_STEP)
        d_ref[...] = -ADAM_LR * (m_hat / (jnp.sqrt(v_hat) + ADAM_EPS) + ADAM_WD * w_ref[...])
        mo_ref[...] = m_new
        vo_ref[...] = v_new

    blk = pl.BlockSpec((tr, tc), lambda i, j: (i, j))
    return pl.pallas_call(
        body, name=name, grid=(rows // tr, cols // tc),
        in_specs=[blk] * 4, out_specs=[blk] * 3,
        out_shape=[jax.ShapeDtypeStruct(w.shape, F32)] * 3,
        compiler_params=_params(("parallel", "parallel")),
    )(w, g, m, v)


def _adamw_native(w3, g3, m3, v3, name):
    rows = w3.shape[0]
    tr = rows // 2
    blk = pl.BlockSpec((tr,) + w3.shape[1:], lambda i: (i, 0, 0))
    shape = jax.ShapeDtypeStruct(w3.shape, F32)

    def moments(g_ref, m_ref, v_ref, mo_ref, vo_ref):
        g_ = g_ref[...]
        mo_ref[...] = ADAM_B1 * m_ref[...] + (1.0 - ADAM_B1) * g_
        vo_ref[...] = ADAM_B2 * v_ref[...] + (1.0 - ADAM_B2) * (g_ * g_)

    new_m, new_v = pl.pallas_call(
        moments, name=name + "_moments", grid=(2,), in_specs=[blk] * 3, out_specs=[blk] * 2, out_shape=[shape] * 2,
        compiler_params=_params(("parallel",)),
    )(g3, m3, v3)

    def delta(w_ref, m_ref, v_ref, d_ref):
        m_hat = m_ref[...] / (1.0 - ADAM_B1 ** ADAM_STEP)
        v_hat = v_ref[...] / (1.0 - ADAM_B2 ** ADAM_STEP)
        d_ref[...] = -ADAM_LR * (m_hat / (jnp.sqrt(v_hat) + ADAM_EPS) + ADAM_WD * w_ref[...])

    d = pl.pallas_call(
        delta, name=name + "_delta", grid=(2,), in_specs=[blk] * 3, out_specs=blk, out_shape=shape,
        compiler_params=_params(("parallel",)),
    )(w3, new_m, new_v)
    return d, new_m, new_v


def _row_tile8(rows):
    best = rows
    for t in range(8, 257, 8):
        if rows % t == 0:
            best = t
    return best


def kernel(x, meta_tokens, norm_g, w_in, b_forget, pool_w, pool_scale, w_up_pool, w_up_attn, w_out, final_norm_g, loss_target, m_meta_tokens, m_norm_g, m_w_in, m_b_forget, m_pool_w, m_pool_scale, m_w_up_pool, m_w_up_attn, m_w_out, m_final_norm_g, v_meta_tokens, v_norm_g, v_w_in, v_b_forget, v_pool_w, v_pool_scale, v_w_up_pool, v_w_up_attn, v_w_out, v_final_norm_g):
    seq = x.shape[1]
    assert seq % ROW_TILE == 0 and x.shape[0] == 1
    lp = seq + ROW_TILE
    nb = lp // ROW_TILE
    lk = -(-lp // KV_TILE_BWD) * KV_TILE_BWD
    core = jnp.reshape(lax.axis_index("c"), (1,)).astype(jnp.int32)
    x2 = x[0]
    target = loss_target[0]
    sh_d = D_MODEL // N_CHIPS

    to_rows = lambda a: jnp.transpose(a, (2, 0, 1))
    from_rows = lambda a: jnp.transpose(a, (1, 2, 0))
    gf = final_norm_g.reshape(1, D_MODEL)
    bfg = jnp.pad(b_forget, ((0, 0), (0, F_COLS - N_HEADS)))
    pw_b = pool_w[0].astype(BF16)
    hn, (wg_in, meta_g) = _rmsnorm_fwd(x2, norm_g, lk, [jnp.transpose(w_in[0]).astype(BF16), meta_tokens], [1, 0])
    wt = wg_in.reshape(-1, D_MODEL)
    wt_e = jnp.concatenate([wt[REF_U:REF_Q], wt[REF_ZA:REF_F], wt[REF_GP:]], axis=0)
    wt_qkv = wt[REF_Q:REF_ZA]
    wt_f = jnp.pad(wt[REF_F:REF_GP], ((0, F_COLS - N_HEADS), (0, 0)))
    meta_full = jnp.transpose(meta_g, (1, 0, 2)).reshape(N_META, D_MODEL)
    metapad = jnp.pad(meta_full, ((PAD_ROWS, 0), (0, 0)))

    tm = _row_tile(lp, 1100)
    e, (wg_up_p, wg_up_a, wg_out) = _mm_nt(
        hn, wt_e, BF16, "in_proj_gates", lp, E_COLS, tm, 512,
        gather=([w_up_pool[0].astype(BF16), w_up_attn[0].astype(BF16), w_out[0].astype(BF16)], [0, 0, 0]))
    wup_p = jnp.transpose(wg_up_p, (1, 0, 2)).reshape(POOL_WIDTH, D_MODEL)
    wup_a = jnp.transpose(wg_up_a, (1, 0, 2)).reshape(ATTN_WIDTH, D_MODEL)
    wout = wg_out.reshape(D_MODEL, D_MODEL)
    qkv = _mm_nt(hn, wt_qkv, BF16, "in_proj_qkv", lk, QKV_COLS, _row_tile(lk, 1300), 512, scale_first=HEAD_DIM ** -0.5)
    f = _mm_nt(hn, wt_f, F32, "in_proj_forget", lp, F_COLS, tm, F_COLS)
    c = _gates_fwd(f, bfg)
    c_t = jnp.transpose(c[:, :N_HEADS])
    bias = c_t[None, :, :] - jnp.transpose(c_t[:, ::ROW_TILE])[:, :, None]
    bias = jnp.where(jnp.arange(lp) < PAD_ROWS, -NEG, bias)
    bias = jnp.pad(bias, ((0, 0), (0, 0), (0, lk - lp)))
    by_kv = lambda t: jnp.transpose(bias.reshape(nb, N_HEADS // 2, 2, lk // t, t), (0, 1, 3, 2, 4))
    bias, bias_bwd = by_kv(KV_TILE), by_kv(KV_TILE_BWD)
    y_pool = _pool_fwd(e, pw_b, pool_scale)
    o, lse = _attn_fwd(qkv, bias, nb)
    merged, y_attn = _merge_fwd(y_pool, o, e, wup_p, wup_a)
    dh1, loss_part, dgf = _head_fwd_bwd(merged, wout, x2, metapad, gf, target)

    dap, daa, dgate, dza, do, delta, dy_pool = _merge_bwd(dh1, wout, y_pool, y_attn, wup_p, wup_a, e, o)
    dpc, dzp, dscale, dpw = _pool_bwd_local(e, dy_pool, pw_b, pool_scale)
    du = _pool_bwd_window(dpc)
    dq, dk, dv, dc4, dcq = _attn_bwd(qkv, do, lse, delta, bias_bwd, nb)
    dc = jnp.transpose(dc4, (1, 3, 0, 2)).reshape(lk, N_HEADS)[:lp]
    df, db = _gates_bwd(jnp.pad(dc, ((0, 0), (0, F_COLS - N_HEADS))), dcq, f, bfg)
    dproj = jnp.concatenate([du, dzp, dza, dgate, (dq * HEAD_DIM ** -0.5).astype(BF16), dk[:lp], dv[:lp]], axis=1)
    tk = _row_tile(lp, 1100)
    dw_out = _mm_tn(merged, dh1, "grad_w_out", lp, 512, D_MODEL, tk)
    dw_up_p = _mm_tn(y_pool, dap, "grad_w_up_pool", lp, 512, D_MODEL, tk, chunks=N_CHIPS)
    dw_up_a = _mm_tn(y_attn, daa, "grad_w_up_attn", lp, 512, D_MODEL, tk, chunks=N_CHIPS)
    dwt_f = _mm_tn(df, hn, "grad_w_in_forget", lp, F_COLS, D_MODEL, tk)

    def pad8(a):
        return jnp.pad(a, ((0, (-a.shape[0]) % 8), (0, 0)))

    small_parts = [pad8(a) for a in (dgf.reshape(-1, LANES), dscale.reshape(-1, LANES), db, dpw.reshape(-1, LANES),
                                     loss_part)]
    small = jnp.concatenate(small_parts, axis=0)
    soffs = [0]
    for p in small_parts:
        soffs.append(soffs[-1] + p.shape[0])

    gs_rows = [dw_up_p, dw_up_a, dw_out.reshape(N_CHIPS, sh_d, D_MODEL)]
    half = MAIN_COLS // 2
    dwt_a, (*rb_rows, rb_f, sr) = _mm_tn(dproj, hn, "grad_w_in_a", lp, half // 2, D_MODEL, tk, m=half, m_off=0,
                                         swap=(gs_rows + [dwt_f, small], [1, 1, 1, 1, None]))
    dwt_b, (rb_a,) = _mm_tn(dproj, hn, "grad_w_in_b", lp, half // 2, D_MODEL, tk, m=half, m_off=2, swap=([dwt_a], [1]))
    (rb_b,) = _sibling_exchange([dwt_b], [1])
    *hs_rows, sc = _add_sibling_half(gs_rows, rb_rows, [1, 1, 1], small, sr, core)
    h_a, h_b, h_f = _add_halves_2d([dwt_a, dwt_b, dwt_f], [rb_a, rb_b, rb_f], core)
    h_in = jnp.concatenate([h_a[E_U:E_ZA], h_b[D_MODEL:], h_a[E_ZA:E_GP], h_f[:N_HEADS], h_a[E_GP:], h_b[:D_MODEL]], axis=0)
    hs = [h_in.reshape(N_CHIPS, -1, D_MODEL // 2)] + hs_rows
    grad_axes = [2, 1, 1, 1]
    (grad_x, g_block0, dg1), qs, sq = _input_bwd(dproj, df, wt_e, wt_qkv, wt_f, x2, metapad, dh1, norm_g, hs, sc)
    g_w_in_rows, g_w_up_p, g_w_up_a, g_w_out, st = _reduce_allgather(qs, grad_axes, sq, [True, False, False, False])
    late = _small_allreduce(jnp.concatenate([pad8(dg1.reshape(-1, LANES)), g_block0[PAD_ROWS:].reshape(-1, LANES)], axis=0))
    n_norm = D_MODEL // LANES
    g_norm = late[:n_norm].reshape(1, D_MODEL)
    chip = 2 * lax.axis_index("x") + lax.axis_index("y")
    g_meta = lax.dynamic_slice_in_dim(late[n_norm:].reshape(N_META, D_MODEL), chip * sh_d, sh_d, axis=1)

    spiece = lambda k, rows: st[soffs[k]:soffs[k] + rows]
    g_final = spiece(0, D_MODEL // LANES).reshape(1, D_MODEL)
    g_scale = spiece(1, POOL_WIDTH // LANES).reshape(1, POOL_WIDTH)
    g_bf = spiece(2, 1)
    g_pw = spiece(3, POOL_WIDTH)
    loss = st[soffs[4], 0]

    def pad_lanes(a):
        return jnp.pad(a, ((0, 0), (0, F_COLS - N_HEADS)))

    w_in_res = (g_w_in_rows,) + _adamw_native(to_rows(w_in), g_w_in_rows, to_rows(m_w_in), to_rows(v_w_in), "adamw_w_in")

    upd = [
        ("meta_tokens", meta_tokens, g_meta, m_meta_tokens, v_meta_tokens),
        ("norm_g", norm_g, g_norm, m_norm_g, v_norm_g),
        ("w_in", None, None, None, None),
        ("b_forget", pad_lanes(b_forget), g_bf, pad_lanes(m_b_forget), pad_lanes(v_b_forget)),
        ("pool_w", pool_w.reshape(-1, LANES), g_pw, m_pool_w.reshape(-1, LANES), v_pool_w.reshape(-1, LANES)),
        ("pool_scale", pool_scale, g_scale, m_pool_scale, v_pool_scale),
        ("w_up_pool", w_up_pool[0], g_w_up_p, m_w_up_pool[0], v_w_up_pool[0]),
        ("w_up_attn", w_up_attn[0], g_w_up_a, m_w_up_attn[0], v_w_up_attn[0]),
        ("w_out", w_out[0], g_w_out, m_w_out[0], v_w_out[0]),
        ("final_norm_g", gf, g_final, m_final_norm_g.reshape(1, D_MODEL), v_final_norm_g.reshape(1, D_MODEL)),
    ]
    shapes = [meta_tokens.shape, norm_g.shape, w_in.shape, b_forget.shape, pool_w.shape, pool_scale.shape,
              w_up_pool.shape, w_up_attn.shape, w_out.shape, final_norm_g.shape]
    grads, deltas, new_ms, new_vs = [], [], [], []
    for (name, w_, g_, m_in, v_in), shp in zip(upd, shapes):
        if name == "w_in":
            res = tuple(from_rows(a) for a in w_in_res)
        else:
            d_, mn_, vn_ = _adamw(w_, g_, m_in, v_in, "adamw_" + name)
            res = (g_, d_, mn_, vn_)
        if name == "b_forget":
            res = tuple(a[:, :N_HEADS] for a in res)
        for lst, a in zip((grads, deltas, new_ms, new_vs), res):
            lst.append(a.reshape(shp))

    return (loss, grad_x.reshape(x.shape), *grads, *deltas, *new_ms, *new_vs)
```

```python
import jax
import jax.numpy as jnp
from jax import lax
from jax.experimental import pallas as pl
from jax.experimental.pallas import tpu as pltpu

F32 = jnp.float32
BF16 = jnp.bfloat16
MESH = pl.DeviceIdType.MESH
HIGHEST = lax.Precision.HIGHEST
HBM = pl.BlockSpec(memory_space=pltpu.HBM)

D_MODEL = 1024
N_META = 16
POOL_WIDTH = 512
POOL_GROUP = 128
POOL_WINDOWS = (2, 4, 8, 16)
N_HEADS = 8
HEAD_DIM = 64
ATTN_WIDTH = 512
RMS_EPS = 1e-6
N_CHIPS = 4

ADAM_LR = 0.001
ADAM_B1 = 0.9
ADAM_B2 = 0.999
ADAM_EPS = 1e-08
ADAM_WD = 0.01
ADAM_STEP = 10

LANES = 128
ROW_TILE = 256
KV_TILE = 512
KV_TILE_BWD = 1024
PAD_ROWS = ROW_TILE - N_META
NEG = -1e30

E_U, E_ZP, E_ZA, E_GP, E_GA, E_COLS = 0, 512, 1024, 1536, 2560, 3584
QKV_COLS = 3 * ATTN_WIDTH
MAIN_COLS = E_COLS + QKV_COLS
F_COLS = LANES
REF_U, REF_Q, REF_ZA, REF_F, REF_GP = 0, 1024, 2560, 3072, 3080
VMEM_LIMIT = 48 * 1024 * 1024


def _params(sem=None, vmem=VMEM_LIMIT):
    return pltpu.CompilerParams(dimension_semantics=sem, vmem_limit_bytes=vmem)


def _in_hbm(*arrays):
    return [pltpu.with_memory_space_constraint(a, pltpu.HBM) for a in arrays]


def _row_tile(n, target):
    best = 16
    for t in range(16, target + 1, 16):
        if n % t == 0:
            best = t
    return best


def _sigmoid(x):
    return 1.0 / (1.0 + jnp.exp(-x))


def _half(ref, axis, which):
    n = ref.shape[axis] // 2
    idx = [slice(None)] * len(ref.shape)
    idx[axis] = pl.ds(pl.multiple_of(which * n, n), n)
    return ref.at[tuple(idx)]


def _half_shape(shape, axis):
    s = list(shape)
    s[axis] //= 2
    return tuple(s)


def _gather_start(ins, outs, axes, send, recv, fsend, frecv, local):
    x, y, c = lax.axis_index("x"), lax.axis_index("y"), lax.axis_index("c")
    me = 2 * x + y
    for t in range(len(ins)):
        pltpu.make_async_copy(ins[t], outs[t].at[me], local.at[t]).start()
        for k, chip in enumerate([(1 - x, y), (x, 1 - y), (1 - x, 1 - y)]):
            pltpu.make_async_remote_copy(
                src_ref=_half(ins[t], axes[t], c), dst_ref=_half(outs[t].at[me], axes[t], c),
                send_sem=send.at[3 * t + k], recv_sem=recv.at[3 * t + k], device_id=(*chip, c), device_id_type=MESH).start()


def _gather_finish(ins, outs, axes, send, recv, fsend, frecv, local):
    x, y, c = lax.axis_index("x"), lax.axis_index("y"), lax.axis_index("c")
    me = 2 * x + y
    sibling = (x, y, 1 - c)
    chips = [(1 - x, y), (x, 1 - y), (1 - x, 1 - y)]
    n = len(ins)

    def over_ici(t, k, chip, dst_slot):
        return pltpu.make_async_remote_copy(
            src_ref=_half(ins[t], axes[t], c), dst_ref=_half(outs[t].at[dst_slot], axes[t], c),
            send_sem=send.at[3 * t + k], recv_sem=recv.at[3 * t + k], device_id=(*chip, c), device_id_type=MESH)

    def to_sibling(t, k, slot, which):
        return pltpu.make_async_remote_copy(
            src_ref=_half(outs[t].at[slot], axes[t], which), dst_ref=_half(outs[t].at[slot], axes[t], which),
            send_sem=fsend.at[3 * t + k], recv_sem=frecv.at[3 * t + k], device_id=sibling, device_id_type=MESH)

    forwards = []
    for t in range(n):
        for k, (px, py) in enumerate(chips):
            over_ici(t, k, (px, py), 2 * px + py).wait_recv()
            fw = to_sibling(t, k, 2 * px + py, c)
            fw.start()
            forwards.append(fw)
    for t in range(n):
        for k, (px, py) in enumerate(chips):
            to_sibling(t, k, 2 * px + py, 1 - c).wait_recv()
    for t in range(n):
        for k, chip in enumerate(chips):
            over_ici(t, k, chip, me).wait_send()
    for fw in forwards:
        fw.wait_send()
    for t in range(n):
        pltpu.make_async_copy(ins[t], outs[t].at[me], local.at[t]).wait()


def _gather_sems(n):
    return [pltpu.SemaphoreType.DMA((3 * n,)), pltpu.SemaphoreType.DMA((3 * n,)), pltpu.SemaphoreType.DMA((3 * n,)),
            pltpu.SemaphoreType.DMA((3 * n,)), pltpu.SemaphoreType.DMA((n,))]


def _gathered_shapes(shards):
    return [jax.ShapeDtypeStruct((N_CHIPS,) + s.shape, s.dtype) for s in shards]


def _swap_copies(srcs, dsts, axes, send, recv):
    x, y, c = lax.axis_index("x"), lax.axis_index("y"), lax.axis_index("c")
    return [pltpu.make_async_remote_copy(
        src_ref=srcs[t] if axes[t] is None else _half(srcs[t], axes[t], 1 - c), dst_ref=dsts[t],
        send_sem=send.at[t], recv_sem=recv.at[t], device_id=(x, y, 1 - c), device_id_type=MESH) for t in range(len(srcs))]


def _swap_shapes(srcs, axes):
    return [jax.ShapeDtypeStruct(g.shape if a is None else _half_shape(g.shape, a), g.dtype) for g, a in zip(srcs, axes)]


def _sibling_exchange(gs, axes):
    n = len(gs)

    def body(*refs):
        cps = _swap_copies(refs[:n], refs[n:2 * n], axes, *refs[2 * n:])
        for cp in cps:
            cp.start()
        for cp in cps:
            cp.wait()

    return pl.pallas_call(
        body, name="grad_sibling_exchange",
        in_specs=[HBM] * n, out_specs=[HBM] * n, out_shape=_swap_shapes(gs, axes),
        scratch_shapes=[pltpu.SemaphoreType.DMA((n,)), pltpu.SemaphoreType.DMA((n,))],
    )(*_in_hbm(*gs))


def _add_halves_2d(parts, rbs, core):
    n = len(parts)
    rows, w = rbs[0].shape
    tr = 512
    last_rows = rbs[-1].shape[0]

    def body(core_ref, *refs):
        p_refs, r_refs, h_refs = refs[:n], refs[n:2 * n], refs[2 * n:]
        for t in range(n - 1):
            h_refs[t][...] = (p_refs[t][...] + r_refs[t][...]).astype(BF16)

        @pl.when(pl.program_id(0) == 0)
        def _():
            h_refs[n - 1][...] = (p_refs[n - 1][...] + r_refs[n - 1][...]).astype(BF16)

    mine = lambda r: pl.BlockSpec((r, w), lambda i, cr: (i, cr[0]))
    tile = lambda r: pl.BlockSpec((r, w), lambda i, cr: (i, 0))
    mine_small = pl.BlockSpec((last_rows, w), lambda i, cr: (0, cr[0]))
    tile_small = pl.BlockSpec((last_rows, w), lambda i, cr: (0, 0))
    return pl.pallas_call(
        body, name="grad_add_sibling_w_in",
        grid_spec=pltpu.PrefetchScalarGridSpec(
            num_scalar_prefetch=1, grid=(rows // tr,),
            in_specs=[mine(tr)] * (n - 1) + [mine_small] + [tile(tr)] * (n - 1) + [tile_small],
            out_specs=[tile(tr)] * (n - 1) + [tile_small]),
        out_shape=[jax.ShapeDtypeStruct(r.shape, BF16) for r in rbs],
        compiler_params=_params(("arbitrary",)),
    )(core, *parts, *rbs)


def _add_sibling_half(gs, rbs, axes, small, sr, core):
    n = len(gs)

    def body(core_ref, *refs):
        g_refs, rb_refs = refs[:n], refs[n:2 * n]
        s_ref, sr_ref = refs[2 * n], refs[2 * n + 1]
        h_refs, sc_ref = refs[2 * n + 2:3 * n + 2], refs[3 * n + 2]
        for t in range(n):
            h_refs[t][...] = (g_refs[t][...] + rb_refs[t][...]).astype(BF16)
        sc_ref[...] = s_ref[...] + sr_ref[...]

    def mine(rb, axis):
        blk = (None,) + rb.shape[1:]
        if axis == 2:
            return pl.BlockSpec(blk, lambda j, cr: (j, 0, cr[0]))
        return pl.BlockSpec(blk, lambda j, cr: (j, cr[0], 0))

    chunk = lambda rb: pl.BlockSpec((None,) + rb.shape[1:], lambda j, cr: (j, 0, 0))
    whole = pl.BlockSpec(small.shape, lambda j, cr: (0, 0))
    return pl.pallas_call(
        body, name="grad_add_sibling",
        grid_spec=pltpu.PrefetchScalarGridSpec(
            num_scalar_prefetch=1, grid=(N_CHIPS,),
            in_specs=[mine(rb, a) for rb, a in zip(rbs, axes)] + [chunk(rb) for rb in rbs] + [whole, whole],
            out_specs=[chunk(rb) for rb in rbs] + [whole]),
        out_shape=[jax.ShapeDtypeStruct(rb.shape, BF16) for rb in rbs] + [jax.ShapeDtypeStruct(small.shape, F32)],
        compiler_params=_params(("arbitrary",)),
    )(core, *gs, *rbs, small, sr)


def _chip_scatter_plan(h_refs, s_ref, q_refs, sq_ref, send, recv, local):
    n = len(h_refs)
    x, y, c = lax.axis_index("x"), lax.axis_index("y"), lax.axis_index("c")
    me = 2 * x + y
    chips = [(1 - x, y), (x, 1 - y), (1 - x, 1 - y)]

    def copy(t, k, chip, src_slot, dst_slot):
        src = h_refs[t].at[src_slot] if t < n else s_ref
        dst = (q_refs[t] if t < n else sq_ref).at[dst_slot]
        return pltpu.make_async_remote_copy(src_ref=src, dst_ref=dst, send_sem=send.at[3 * t + k],
                                            recv_sem=recv.at[3 * t + k], device_id=(*chip, c), device_id_type=MESH)

    own = [pltpu.make_async_copy(h_refs[t].at[me], q_refs[t].at[me], local.at[t]) for t in range(n)]
    own.append(pltpu.make_async_copy(s_ref, sq_ref.at[me], local.at[n]))
    sends = [copy(t, k, (px, py), 2 * px + py, me) for t in range(n + 1) for k, (px, py) in enumerate(chips)]
    recvs = [copy(t, k, (px, py), me, 2 * px + py) for t in range(n + 1) for k, (px, py) in enumerate(chips)]
    return own, sends, recvs


def _small_allreduce(buf):
    def body(b_ref, o_ref, sib_buf, chip_buf, send, recv):
        x, y, c = lax.axis_index("x"), lax.axis_index("y"), lax.axis_index("c")
        me = 2 * x + y
        chips = [(1 - x, y), (x, 1 - y), (1 - x, 1 - y)]
        swap = pltpu.make_async_remote_copy(src_ref=b_ref, dst_ref=sib_buf, send_sem=send.at[0], recv_sem=recv.at[0],
                                            device_id=(x, y, 1 - c), device_id_type=MESH)
        swap.start()
        swap.wait()
        chip_buf[me] = b_ref[...] + sib_buf[...]

        def copy(k, chip, slot):
            return pltpu.make_async_remote_copy(src_ref=chip_buf.at[slot], dst_ref=chip_buf.at[slot], send_sem=send.at[1 + k],
                                                recv_sem=recv.at[1 + k], device_id=(*chip, c), device_id_type=MESH)

        sends = [copy(k, chip, me) for k, chip in enumerate(chips)]
        for cp in sends:
            cp.start()
        for k, (px, py) in enumerate(chips):
            copy(k, (px, py), 2 * px + py).wait_recv()
        for cp in sends:
            cp.wait_send()
        o_ref[...] = ((chip_buf[0] + chip_buf[1]) + chip_buf[2]) + chip_buf[3]

    vmem = pl.BlockSpec(memory_space=pltpu.VMEM)
    return pl.pallas_call(
        body, name="small_allreduce", in_specs=[vmem], out_specs=vmem,
        out_shape=jax.ShapeDtypeStruct(buf.shape, F32),
        scratch_shapes=[pltpu.VMEM(buf.shape, F32), pltpu.VMEM((N_CHIPS,) + buf.shape, F32),
                        pltpu.SemaphoreType.DMA((4,)), pltpu.SemaphoreType.DMA((4,))],
        compiler_params=_params(),
    )(buf)


def _reduce_allgather(qs, axes, sq, as_rows):
    n = len(qs)
    shard_shapes, half_axes = [], []
    for q, a, rows_form in zip(qs, axes, as_rows):
        shape = [d * 2 if i == a - 1 else d for i, d in enumerate(q.shape[1:])]
        if rows_form:
            assert a == 2
            shape = [shape[0], 1, shape[1]]
        shard_shapes.append(tuple(shape))
        half_axes.append(2 if rows_form else a - 1)

    def body(*refs):
        q_refs, sq_ref = refs[:n], refs[n]
        o_refs, st_ref = refs[n + 1:2 * n + 1], refs[2 * n + 1]
        send, recv = refs[2 * n + 2:]
        x, y, c = lax.axis_index("x"), lax.axis_index("y"), lax.axis_index("c")

        def swap(t, which):
            return pltpu.make_async_remote_copy(
                src_ref=_half(o_refs[t], half_axes[t], which), dst_ref=_half(o_refs[t], half_axes[t], which),
                send_sem=send.at[t], recv_sem=recv.at[t], device_id=(x, y, 1 - c), device_id_type=MESH)

        sent = []
        for t in range(n):
            q = q_refs[t]
            total = ((q[0].astype(F32) + q[1].astype(F32)) + q[2].astype(F32)) + q[3].astype(F32)
            if as_rows[t]:
                total = total.reshape(total.shape[0], 1, total.shape[1])
            _half(o_refs[t], half_axes[t], c)[...] = total
            cp = swap(t, c)
            cp.start()
            sent.append(cp)
        st_ref[...] = ((sq_ref[0] + sq_ref[1]) + sq_ref[2]) + sq_ref[3]
        for t in range(n):
            swap(t, 1 - c).wait_recv()
        for cp in sent:
            cp.wait_send()

    vmem = pl.BlockSpec(memory_space=pltpu.VMEM)
    return pl.pallas_call(
        body, name="grad_reduce_allgather",
        in_specs=[vmem] * (n + 1), out_specs=[vmem] * (n + 1),
        out_shape=[jax.ShapeDtypeStruct(s, F32) for s in shard_shapes] + [jax.ShapeDtypeStruct(sq.shape[1:], F32)],
        scratch_shapes=[pltpu.SemaphoreType.DMA((n,)), pltpu.SemaphoreType.DMA((n,))],
        compiler_params=_params(),
    )(*qs, sq)


def _dot_nt(a, b):
    return lax.dot_general(a, b, (((1,), (1,)), ((), ())), preferred_element_type=F32)


def _dot_tn(a, b):
    return lax.dot_general(a, b, (((0,), (0,)), ((), ())), preferred_element_type=F32)


def _mm_nt(a, bt, out_dtype, name, m, n, tm, tn, row_block=0, scale_first=None, gather=None):
    k = a.shape[1]
    shards, axes = gather if gather is not None else ([], [])
    ng = len(shards)
    grid = (m // tm, n // tn)

    def body(a_ref, b_ref, *rest):
        ins, o_ref, outs, sems = rest[:ng], rest[ng], rest[ng + 1:2 * ng + 1], rest[2 * ng + 1:]
        first = (pl.program_id(0) == 0) & (pl.program_id(1) == 0)
        last = (pl.program_id(0) == grid[0] - 1) & (pl.program_id(1) == grid[1] - 1)
        if ng:
            @pl.when(first)
            def _():
                _gather_start(ins, outs, axes, *sems)

        r = _dot_nt(a_ref[...], b_ref[...])
        if scale_first is not None:
            r = r * jnp.where(pl.program_id(1) == 0, scale_first, 1.0)
        o_ref[...] = r.astype(out_dtype)

        if ng:
            @pl.when(last)
            def _():
                _gather_finish(ins, outs, axes, *sems)

    res = pl.pallas_call(
        body, name=name, grid=grid,
        in_specs=[pl.BlockSpec((tm, k), lambda i, j: (i, 0)), pl.BlockSpec((tn, k), lambda i, j: (row_block + j, 0))]
        + [HBM] * ng,
        out_specs=[pl.BlockSpec((tm, tn), lambda i, j: (i, j))] + [HBM] * ng,
        out_shape=[jax.ShapeDtypeStruct((m, n), out_dtype)] + _gathered_shapes(shards),
        scratch_shapes=_gather_sems(ng) if ng else [],
        compiler_params=_params(("arbitrary", "arbitrary") if ng else ("parallel", "parallel")),
    )(a, bt, *_in_hbm(*shards))
    return (res[0], res[1:]) if ng else res[0]


def _mm_tn(a, b, name, k, tm, tn, tk, chunks=1, m=None, m_off=0, swap=None):
    m = a.shape[1] if m is None else m
    n = b.shape[1]
    cw = n // chunks
    srcs, axes = swap if swap is not None else ([], [])
    ns = len(srcs)
    grid = (m // tm, n // tn, k // tk)

    def body(a_ref, b_ref, *rest):
        s_refs, o_ref, d_refs, sems = rest[:ns], rest[ns], rest[ns + 1:2 * ns + 1], rest[2 * ns + 1:]
        ids = [pl.program_id(d) for d in range(3)]
        if ns:
            @pl.when((ids[0] == 0) & (ids[1] == 0) & (ids[2] == 0))
            def _():
                for cp in _swap_copies(s_refs, d_refs, axes, *sems):
                    cp.start()

        @pl.when(ids[2] == 0)
        def _():
            o_ref[...] = jnp.zeros_like(o_ref)
        r = _dot_tn(a_ref[...].astype(BF16), b_ref[...].astype(BF16))
        if chunks > 1:
            for c in range(chunks):
                o_ref[c] += r[:, c * cw:(c + 1) * cw]
        else:
            o_ref[...] += r

        if ns:
            @pl.when((ids[0] == grid[0] - 1) & (ids[1] == grid[1] - 1) & (ids[2] == grid[2] - 1))
            def _():
                for cp in _swap_copies(s_refs, d_refs, axes, *sems):
                    cp.wait()

    if chunks > 1:
        assert tn == n
        out_spec = pl.BlockSpec((chunks, tm, cw), lambda i, j, kk: (0, i, 0))
        out_shape = jax.ShapeDtypeStruct((chunks, m, cw), F32)
    else:
        out_spec = pl.BlockSpec((tm, tn), lambda i, j, kk: (i, j))
        out_shape = jax.ShapeDtypeStruct((m, n), F32)
    res = pl.pallas_call(
        body, name=name, grid=grid,
        in_specs=[pl.BlockSpec((tk, tm), lambda i, j, kk: (kk, m_off + i)), pl.BlockSpec((tk, tn), lambda i, j, kk: (kk, j))]
        + [HBM] * ns,
        out_specs=[out_spec] + [HBM] * ns, out_shape=[out_shape] + _swap_shapes(srcs, axes),
        scratch_shapes=[pltpu.SemaphoreType.DMA((ns,)), pltpu.SemaphoreType.DMA((ns,))] if ns else [],
        compiler_params=_params(("arbitrary",) * 3 if ns else ("parallel", "parallel", "arbitrary")),
    )(a, b, *_in_hbm(*srcs))
    return (res[0], res[1:]) if ns else res[0]


def _tokens_spec():
    return pl.BlockSpec((ROW_TILE, D_MODEL), lambda i: (jnp.maximum(i - 1, 0), 0))


def _rmsnorm_fwd(x2, g1, lk, shards, axes):
    nb = x2.shape[0] // ROW_TILE + 1
    nblk = lk // ROW_TILE
    ng = len(shards)
    meta_cols = shards[-1].shape[1]

    def body(x_ref, g_ref, *rest):
        ins, hn_ref, outs = rest[:ng], rest[ng], rest[ng + 1:2 * ng + 1]
        sems, meta_buf, meta_sem = rest[2 * ng + 1:2 * ng + 6], rest[2 * ng + 6], rest[2 * ng + 7]
        s = pl.program_id(0)
        blk = (s + 1) % nblk

        @pl.when(s == 0)
        def _():
            _gather_start(ins, outs, axes, *sems)

        def normed(h):
            r = lax.rsqrt(jnp.mean(h * h, axis=-1, keepdims=True) + RMS_EPS)
            return ((h * r) * g_ref[...]).astype(BF16)

        @pl.when(s < nblk - 1)
        def _():
            hn_ref[...] = normed(jnp.where(blk >= nb, 0.0, x_ref[...]))

        @pl.when(s == nblk - 1)
        def _():
            _gather_finish(ins, outs, axes, *sems)
            fetch = pltpu.make_async_copy(outs[-1], meta_buf, meta_sem.at[0])
            fetch.start()
            fetch.wait()
            meta = jnp.concatenate([meta_buf[j] for j in range(N_CHIPS)], axis=1)
            hn_ref[...] = normed(jnp.concatenate([jnp.zeros((PAD_ROWS, D_MODEL), F32), meta], axis=0))

    res = pl.pallas_call(
        body, name="rmsnorm_fwd", grid=(nblk,),
        in_specs=[pl.BlockSpec((ROW_TILE, D_MODEL), lambda s: (jnp.clip((s + 1) % nblk - 1, 0, nb - 2), 0)),
                  pl.BlockSpec((1, D_MODEL), lambda s: (0, 0))] + [HBM] * ng,
        out_specs=[pl.BlockSpec((ROW_TILE, D_MODEL), lambda s: ((s + 1) % nblk, 0))] + [HBM] * ng,
        out_shape=[jax.ShapeDtypeStruct((lk, D_MODEL), BF16)] + _gathered_shapes(shards),
        scratch_shapes=_gather_sems(ng) + [pltpu.VMEM((N_CHIPS, N_META, meta_cols), F32), pltpu.SemaphoreType.DMA((1,))],
        compiler_params=_params(("arbitrary",)),
    )(x2, g1, *_in_hbm(*shards))
    return res[0], res[1:]


def _valid_gate_mask(i):
    row = i * ROW_TILE + lax.broadcasted_iota(jnp.int32, (ROW_TILE, F_COLS), 0)
    col = lax.broadcasted_iota(jnp.int32, (ROW_TILE, F_COLS), 1)
    return (row >= PAD_ROWS) & (col < N_HEADS)


def _gates_fwd(f, bfg):
    nb = f.shape[0] // ROW_TILE

    def body(f_ref, b_ref, c_ref, carry):
        i = pl.program_id(0)

        @pl.when(i == 0)
        def _():
            carry[...] = jnp.zeros_like(carry)

        logit = f_ref[...] + b_ref[...]
        lf = jnp.minimum(logit, 0.0) - jnp.log1p(jnp.exp(-jnp.abs(logit)))
        lf = jnp.where(_valid_gate_mask(i), lf, 0.0)
        r_i = lax.broadcasted_iota(jnp.int32, (ROW_TILE, ROW_TILE), 0)
        c_i = lax.broadcasted_iota(jnp.int32, (ROW_TILE, ROW_TILE), 1)
        tri = (c_i <= r_i).astype(F32)
        c_ref[...] = jnp.dot(tri, lf, precision=HIGHEST, preferred_element_type=F32) + carry[...]
        carry[...] = carry[...] + jnp.sum(lf, axis=0, keepdims=True)

    return pl.pallas_call(
        body, name="gates_fwd", grid=(nb,),
        in_specs=[pl.BlockSpec((ROW_TILE, F_COLS), lambda i: (i, 0)), pl.BlockSpec((1, F_COLS), lambda i: (0, 0))],
        out_specs=pl.BlockSpec((ROW_TILE, F_COLS), lambda i: (i, 0)),
        out_shape=jax.ShapeDtypeStruct(f.shape, F32),
        scratch_shapes=[pltpu.VMEM((1, F_COLS), F32)],
        compiler_params=_params(("arbitrary",)),
    )(f, bfg)


def _pool_counts(i):
    row = i * ROW_TILE + lax.broadcasted_iota(jnp.int32, (ROW_TILE, 1), 0)
    return jnp.maximum(row - PAD_ROWS, 0)


def _trailing_sums(xc, levels):
    acc = xc
    for lv in range(levels):
        acc = acc + pltpu.roll(acc, 1 << lv, 0)
    return acc


def _leading_sums(xc, levels):
    n = xc.shape[0]
    acc = xc
    for lv in range(levels):
        acc = acc + pltpu.roll(acc, n - (1 << lv), 0)
    return acc


def _pool_p(u_cur, u_prev, i):
    pos = _pool_counts(i)
    ps, invs = [], []
    for g, w in enumerate(POOL_WINDOWS):
        sl = slice(g * POOL_GROUP, (g + 1) * POOL_GROUP)
        cur = u_cur[:, sl]
        xc = jnp.concatenate([u_prev[:, sl], cur], axis=0)
        win = _trailing_sums(xc, g + 1)[ROW_TILE:, :]
        inv = 1.0 / jnp.minimum(pos + 1, w).astype(F32)
        ps.append(win * inv - cur)
        invs.append(inv)
    return ps, invs


def _pool_fwd(e, pw, scale):
    nb = e.shape[0] // ROW_TILE

    def body(uc_ref, up_ref, z_ref, pw_ref, sc_ref, y_ref):
        i = pl.program_id(0)
        u_cur = uc_ref[...].astype(F32)
        u_prev = jnp.where(i == 0, 0.0, up_ref[...].astype(F32))
        ps, _ = _pool_p(u_cur, u_prev, i)
        z = z_ref[...].astype(F32)
        gate = z * _sigmoid(z)
        for g in range(len(POOL_WINDOWS)):
            sl = slice(g * POOL_GROUP, (g + 1) * POOL_GROUP)
            yraw = jnp.dot(ps[g].astype(BF16), pw_ref[g], preferred_element_type=F32)
            y_ref[:, sl] = ((yraw * sc_ref[:, sl]) * gate[:, sl]).astype(BF16)

    blk = (ROW_TILE, POOL_WIDTH)
    return pl.pallas_call(
        body, name="pool_fwd", grid=(nb,),
        in_specs=[pl.BlockSpec(blk, lambda i: (i, 0)), pl.BlockSpec(blk, lambda i: (jnp.maximum(i - 1, 0), 0)),
                  pl.BlockSpec(blk, lambda i: (i, 1)),
                  pl.BlockSpec((len(POOL_WINDOWS), POOL_GROUP, POOL_GROUP), lambda i: (0, 0, 0)),
                  pl.BlockSpec((1, POOL_WIDTH), lambda i: (0, 0))],
        out_specs=pl.BlockSpec(blk, lambda i: (i, 0)),
        out_shape=jax.ShapeDtypeStruct((e.shape[0], POOL_WIDTH), BF16),
        compiler_params=_params(("parallel",)),
    )(e, e, e, pw, scale)


def _stack_heads(a):
    first = lax.broadcasted_iota(jnp.int32, a.shape, 1) < HEAD_DIM
    zero = jnp.zeros_like(a)
    return jnp.concatenate([jnp.where(first, a, zero), jnp.where(first, zero, a)], axis=0)


def _unstack_heads(a):
    rows = a.shape[0] // 2
    first = lax.broadcasted_iota(jnp.int32, (rows, LANES), 1) < HEAD_DIM
    return jnp.where(first, a[:rows], a[rows:])


def _causal(i, jj, stacked, kv_tile):
    r = lax.broadcasted_iota(jnp.int32, (stacked * ROW_TILE, kv_tile), 0)
    if stacked == 2:
        r = jnp.where(r >= ROW_TILE, r - ROW_TILE, r)
    kidx = jj * kv_tile + lax.broadcasted_iota(jnp.int32, (stacked * ROW_TILE, kv_tile), 1)
    return kidx <= i * ROW_TILE + r


def _stack_rows(b):
    n = b.shape[1]
    return jnp.concatenate([jnp.broadcast_to(b[0:1], (ROW_TILE, n)), jnp.broadcast_to(b[1:2], (ROW_TILE, n))], axis=0)


def _attn_fwd(qkv, bias, nb):
    lk = qkv.shape[0]
    lp = nb * ROW_TILE
    nkb = lk // KV_TILE
    n_pairs = N_HEADS // 2

    def body(q_ref, k_ref, v_ref, b_ref, o_ref, lse_ref):
        i = pl.program_id(1)
        qs = _stack_heads(q_ref[...])
        last = (i * ROW_TILE) // KV_TILE

        def block(jj, carry, masked):
            m, l, acc = carry
            rows = pl.ds(pl.multiple_of(jj * KV_TILE, KV_TILE), KV_TILE)
            s = _dot_nt(qs, k_ref[rows, :]) - _stack_rows(b_ref[0, 0, jj])
            if masked:
                s = jnp.where(_causal(i, jj, 2, KV_TILE), s, NEG)
            m_new = jnp.maximum(m, jnp.max(s, axis=1, keepdims=True))
            alpha = jnp.exp(m - m_new)
            p = jnp.exp(s - m_new)
            l = alpha * l + jnp.sum(p, axis=1, keepdims=True)
            acc = alpha * acc + jnp.dot(p.astype(BF16), v_ref[rows, :], preferred_element_type=F32)
            return m_new, l, acc

        init = (jnp.full((2 * ROW_TILE, 1), NEG, F32), jnp.zeros((2 * ROW_TILE, 1), F32),
                jnp.zeros((2 * ROW_TILE, LANES), F32))
        carry = lax.fori_loop(0, last, lambda jj, c: block(jj, c, False), init)
        m, l, acc = block(last, carry, True)
        o_ref[...] = _unstack_heads(acc / l)
        lse_ref[...] = _unstack_heads(jnp.broadcast_to(m + jnp.log(l), (2 * ROW_TILE, LANES)))

    return pl.pallas_call(
        body, name="attn_fwd", grid=(n_pairs, nb),
        in_specs=[pl.BlockSpec((ROW_TILE, LANES), lambda hp, i: (i, hp)),
                  pl.BlockSpec((lk, LANES), lambda hp, i: (0, n_pairs + hp)),
                  pl.BlockSpec((lk, LANES), lambda hp, i: (0, 2 * n_pairs + hp)),
                  pl.BlockSpec((1, 1, nkb, 2, KV_TILE), lambda hp, i: (i, hp, 0, 0, 0))],
        out_specs=[pl.BlockSpec((ROW_TILE, LANES), lambda hp, i: (i, hp)),
                   pl.BlockSpec((ROW_TILE, LANES), lambda hp, i: (i, hp))],
        out_shape=[jax.ShapeDtypeStruct((lp, ATTN_WIDTH), F32), jax.ShapeDtypeStruct((lp, ATTN_WIDTH), F32)],
        compiler_params=_params(("parallel", "parallel")),
    )(qkv, qkv, qkv, bias)


def _merge_fwd(y_pool, o, e, wup_p, wup_a):
    lp = o.shape[0]
    nb = lp // ROW_TILE

    def body(yp_ref, o_ref, e_ref, wp_ref, wa_ref, mg_ref, ya_ref):
        za = e_ref[:, E_ZA:E_GP].astype(F32)
        ya = (o_ref[...] * (za * _sigmoid(za))).astype(BF16)
        ya_ref[...] = ya
        a_pool = jnp.dot(yp_ref[...], wp_ref[...], preferred_element_type=F32)
        a_attn = jnp.dot(ya, wa_ref[...], preferred_element_type=F32)
        mg_ref[...] = (_sigmoid(e_ref[:, E_GP:E_GA].astype(F32)) * a_pool
                       + _sigmoid(e_ref[:, E_GA:E_COLS].astype(F32)) * a_attn).astype(BF16)

    return pl.pallas_call(
        body, name="merge_fwd", grid=(nb,),
        in_specs=[pl.BlockSpec((ROW_TILE, POOL_WIDTH), lambda i: (i, 0)),
                  pl.BlockSpec((ROW_TILE, ATTN_WIDTH), lambda i: (i, 0)),
                  pl.BlockSpec((ROW_TILE, E_COLS), lambda i: (i, 0)),
                  pl.BlockSpec((POOL_WIDTH, D_MODEL), lambda i: (0, 0)),
                  pl.BlockSpec((ATTN_WIDTH, D_MODEL), lambda i: (0, 0))],
        out_specs=[pl.BlockSpec((ROW_TILE, D_MODEL), lambda i: (i, 0)),
                   pl.BlockSpec((ROW_TILE, ATTN_WIDTH), lambda i: (i, 0))],
        out_shape=[jax.ShapeDtypeStruct((lp, D_MODEL), BF16), jax.ShapeDtypeStruct((lp, ATTN_WIDTH), BF16)],
        compiler_params=_params(("parallel",)),
    )(y_pool, o, e, wup_p, wup_a)


def _head_fwd_bwd(merged, w_out, x2, metapad, gf, target):
    lp = merged.shape[0]
    nb = lp // ROW_TILE

    def body(mg_ref, w_ref, x_ref, mp_ref, g_ref, t_ref, dh_ref, loss_ref, dg_ref):
        i = pl.program_id(0)

        @pl.when(i == 0)
        def _():
            loss_ref[...] = jnp.zeros_like(loss_ref)
            dg_ref[...] = jnp.zeros_like(dg_ref)

        h0 = jnp.where(i == 0, mp_ref[...], x_ref[...])
        h1 = h0 + jnp.dot(mg_ref[...], w_ref[...], preferred_element_type=F32)
        r = lax.rsqrt(jnp.mean(h1 * h1, axis=-1, keepdims=True) + RMS_EPS)
        xhat = h1 * r
        g = g_ref[...]
        err = jnp.where(i == 0, 0.0, xhat * g - t_ref[...])
        loss_ref[...] += 0.5 * jnp.sum(jnp.mean(err * err, axis=-1, keepdims=True))
        dy = err / D_MODEL
        dg_ref[...] += jnp.sum(dy * xhat, axis=0, keepdims=True)
        dxhat = dy * g
        dh_ref[...] = r * (dxhat - xhat * jnp.mean(dxhat * xhat, axis=-1, keepdims=True))

    return pl.pallas_call(
        body, name="head_fwd_bwd", grid=(nb,),
        in_specs=[pl.BlockSpec((ROW_TILE, D_MODEL), lambda i: (i, 0)),
                  pl.BlockSpec((D_MODEL, D_MODEL), lambda i: (0, 0)),
                  _tokens_spec(), pl.BlockSpec((ROW_TILE, D_MODEL), lambda i: (0, 0)),
                  pl.BlockSpec((1, D_MODEL), lambda i: (0, 0)), _tokens_spec()],
        out_specs=[pl.BlockSpec((ROW_TILE, D_MODEL), lambda i: (i, 0)),
                   pl.BlockSpec((1, LANES), lambda i: (0, 0)), pl.BlockSpec((1, D_MODEL), lambda i: (0, 0))],
        out_shape=[jax.ShapeDtypeStruct((lp, D_MODEL), F32), jax.ShapeDtypeStruct((1, LANES), F32),
                   jax.ShapeDtypeStruct((1, D_MODEL), F32)],
        compiler_params=_params(("arbitrary",)),
    )(merged, w_out, x2, metapad, gf, target)


def _per_head_rowsum(t):
    head = lax.broadcasted_iota(jnp.int32, t.shape, 1) // HEAD_DIM
    out = jnp.zeros_like(t)
    for h in range(N_HEADS):
        sel = head == h
        out = jnp.where(sel, jnp.sum(jnp.where(sel, t, 0.0), axis=1, keepdims=True), out)
    return out


def _merge_bwd(dh1, w_out, y_pool, y_attn, wup_p, wup_a, e, o):
    lp = o.shape[0]
    nb = lp // ROW_TILE

    def body(dh_ref, wo_ref, yp_ref, ya_ref, wp_ref, wa_ref, e_ref, o_ref,
             dap_ref, daa_ref, dg_ref, dza_ref, do_ref, delta_ref, dyp_ref):
        dmerged = _dot_nt(dh_ref[...].astype(BF16), wo_ref[...])
        a_pool = jnp.dot(yp_ref[...], wp_ref[...], preferred_element_type=F32)
        a_attn = jnp.dot(ya_ref[...], wa_ref[...], preferred_element_type=F32)
        sp = _sigmoid(e_ref[:, E_GP:E_GA].astype(F32))
        sa = _sigmoid(e_ref[:, E_GA:E_COLS].astype(F32))
        dap = (dmerged * sp).astype(BF16)
        daa = (dmerged * sa).astype(BF16)
        dap_ref[...] = dap
        daa_ref[...] = daa
        dg_ref[:, :D_MODEL] = (dmerged * a_pool * (sp * (1.0 - sp))).astype(BF16)
        dg_ref[:, D_MODEL:] = (dmerged * a_attn * (sa * (1.0 - sa))).astype(BF16)
        dyp_ref[...] = _dot_nt(dap, wp_ref[...])
        dya = _dot_nt(daa, wa_ref[...])
        za = e_ref[:, E_ZA:E_GP].astype(F32)
        sz = _sigmoid(za)
        o = o_ref[...]
        do = dya * (za * sz)
        do_ref[...] = do.astype(BF16)
        dza_ref[...] = (dya * o * (sz * (1.0 + za * (1.0 - sz)))).astype(BF16)
        delta_ref[...] = _per_head_rowsum(do * o)

    row = lambda w: pl.BlockSpec((ROW_TILE, w), lambda i: (i, 0))
    full = lambda a: pl.BlockSpec(a.shape, lambda i: (0, 0))
    return pl.pallas_call(
        body, name="merge_bwd", grid=(nb,),
        in_specs=[row(D_MODEL), full(w_out), row(POOL_WIDTH), row(ATTN_WIDTH), full(wup_p), full(wup_a),
                  row(E_COLS), row(ATTN_WIDTH)],
        out_specs=[row(D_MODEL), row(D_MODEL), row(2 * D_MODEL), row(ATTN_WIDTH), row(ATTN_WIDTH),
                   row(ATTN_WIDTH), row(POOL_WIDTH)],
        out_shape=[jax.ShapeDtypeStruct((lp, D_MODEL), BF16), jax.ShapeDtypeStruct((lp, D_MODEL), BF16),
                   jax.ShapeDtypeStruct((lp, 2 * D_MODEL), BF16), jax.ShapeDtypeStruct((lp, ATTN_WIDTH), BF16),
                   jax.ShapeDtypeStruct((lp, ATTN_WIDTH), BF16), jax.ShapeDtypeStruct((lp, ATTN_WIDTH), F32),
                   jax.ShapeDtypeStruct((lp, POOL_WIDTH), F32)],
        compiler_params=_params(("parallel",)),
    )(dh1, w_out, y_pool, y_attn, wup_p, wup_a, e, o)


def _pool_bwd_local(e, dy_pool, pw, scale):
    lp = e.shape[0]
    nb = lp // ROW_TILE
    ng = len(POOL_WINDOWS)

    def body(uc_ref, up_ref, z_ref, dy_ref, pw_ref, sc_ref, dpc_ref, dz_ref, dsc_ref, dpw_ref):
        i = pl.program_id(0)

        @pl.when(i == 0)
        def _():
            dsc_ref[...] = jnp.zeros_like(dsc_ref)
            dpw_ref[...] = jnp.zeros_like(dpw_ref)

        u_cur = uc_ref[...].astype(F32)
        u_prev = jnp.where(i == 0, 0.0, up_ref[...].astype(F32))
        ps, invs = _pool_p(u_cur, u_prev, i)
        z = z_ref[...].astype(F32)
        sz = _sigmoid(z)
        dy = dy_ref[...]
        dypre = dy * (z * sz)
        dsilu = sz * (1.0 + z * (1.0 - sz))
        for g in range(ng):
            sl = slice(g * POOL_GROUP, (g + 1) * POOL_GROUP)
            pb = ps[g].astype(BF16)
            w = pw_ref[g]
            yraw = jnp.dot(pb, w, preferred_element_type=F32)
            sc = sc_ref[:, sl]
            dz_ref[:, sl] = (dy[:, sl] * (yraw * sc) * dsilu[:, sl]).astype(BF16)
            dsc_ref[:, sl] += jnp.sum(dypre[:, sl] * yraw, axis=0, keepdims=True)
            dyraw = (dypre[:, sl] * sc).astype(BF16)
            dpw_ref[g] += _dot_tn(pb, dyraw)
            dpc_ref[:, sl] = _dot_nt(dyraw, w) * invs[g]

    blk = (ROW_TILE, POOL_WIDTH)
    return pl.pallas_call(
        body, name="pool_bwd_local", grid=(nb,),
        in_specs=[pl.BlockSpec(blk, lambda i: (i, 0)), pl.BlockSpec(blk, lambda i: (jnp.maximum(i - 1, 0), 0)),
                  pl.BlockSpec(blk, lambda i: (i, 1)), pl.BlockSpec(blk, lambda i: (i, 0)),
                  pl.BlockSpec((ng, POOL_GROUP, POOL_GROUP), lambda i: (0, 0, 0)),
                  pl.BlockSpec((1, POOL_WIDTH), lambda i: (0, 0))],
        out_specs=[pl.BlockSpec(blk, lambda i: (i, 0)), pl.BlockSpec(blk, lambda i: (i, 0)),
                   pl.BlockSpec((1, POOL_WIDTH), lambda i: (0, 0)),
                   pl.BlockSpec((ng, POOL_GROUP, POOL_GROUP), lambda i: (0, 0, 0))],
        out_shape=[jax.ShapeDtypeStruct((lp, POOL_WIDTH), F32), jax.ShapeDtypeStruct((lp, POOL_WIDTH), BF16),
                   jax.ShapeDtypeStruct((1, POOL_WIDTH), F32),
                   jax.ShapeDtypeStruct((ng, POOL_GROUP, POOL_GROUP), F32)],
        compiler_params=_params(("arbitrary",)),
    )(e, e, e, dy_pool, pw, scale)


def _pool_bwd_window(dpc):
    lp = dpc.shape[0]
    nb = lp // ROW_TILE

    def body(cur_ref, nxt_ref, du_ref):
        i = pl.program_id(0)
        cur = cur_ref[...]
        nxt = jnp.where(i == nb - 1, 0.0, nxt_ref[...])
        pos = _pool_counts(i)
        for g, w in enumerate(POOL_WINDOWS):
            sl = slice(g * POOL_GROUP, (g + 1) * POOL_GROUP)
            xc = jnp.concatenate([cur[:, sl], nxt[:, sl]], axis=0)
            win = _leading_sums(xc, g + 1)[:ROW_TILE, :]
            dp = cur[:, sl] * jnp.minimum(pos + 1, w).astype(F32)
            du_ref[:, sl] = (win - dp).astype(BF16)

    blk = (ROW_TILE, POOL_WIDTH)
    return pl.pallas_call(
        body, name="pool_bwd_window", grid=(nb,),
        in_specs=[pl.BlockSpec(blk, lambda i: (i, 0)), pl.BlockSpec(blk, lambda i: (jnp.minimum(i + 1, nb - 1), 0))],
        out_specs=pl.BlockSpec(blk, lambda i: (i, 0)),
        out_shape=jax.ShapeDtypeStruct((lp, POOL_WIDTH), BF16),
        compiler_params=_params(("parallel",)),
    )(dpc, dpc)


def _attn_bwd(qkv, do, lse, delta, bias, nb):
    lk = qkv.shape[0]
    lp = nb * ROW_TILE
    nkb = lk // KV_TILE_BWD
    n_pairs = N_HEADS // 2
    per_kv = KV_TILE_BWD // ROW_TILE

    def body(q_ref, k_ref, v_ref, do_ref, lse_ref, dl_ref, b_ref, dq_ref, dk_ref, dv_ref, dc_ref, dcq_ref,
             dk_acc, dv_acc, dc_acc):
        jj = pl.program_id(1)

        @pl.when(jj == 0)
        def _():
            dq_ref[...] = jnp.zeros_like(dq_ref)
            dcq_ref[...] = jnp.zeros_like(dcq_ref)

        dk_acc[...] = jnp.zeros_like(dk_acc)
        dv_acc[...] = jnp.zeros_like(dv_acc)
        dc_acc[...] = jnp.zeros_like(dc_acc)

        def block(i, n_keys, masked):
            kb, vb = k_ref[:n_keys, :], v_ref[:n_keys, :]
            rows = pl.ds(pl.multiple_of(i * ROW_TILE, ROW_TILE), ROW_TILE)
            qs = _stack_heads(q_ref[rows, :])
            dos = _stack_heads(do_ref[rows, :])
            lse_i, dl_i = lse_ref[rows, :], dl_ref[rows, :]
            s = _dot_nt(qs, kb)
            dp = _dot_nt(dos, vb)
            if masked:
                valid = _causal(i, jj, 1, KV_TILE_BWD)[:, :n_keys]
            ps, dss, dcs, rowsums = [], [], [], []
            for hd in range(2):
                half = slice(hd * ROW_TILE, (hd + 1) * ROW_TILE)
                col = slice(hd * HEAD_DIM, hd * HEAD_DIM + 1)
                sh = s[half] - b_ref[i, 0, 0, hd:hd + 1, :n_keys]
                if masked:
                    sh = jnp.where(valid, sh, NEG)
                p = jnp.exp(sh - lse_i[:, col])
                ds = p * (dp[half] - dl_i[:, col])
                ps.append(p.astype(BF16))
                dss.append(ds.astype(BF16))
                dcs.append(jnp.sum(ds, axis=0, keepdims=True))
                rowsums.append(jnp.sum(ds, axis=1, keepdims=True))
            dsb = jnp.concatenate(dss, axis=0)
            dv_acc[:n_keys, :] += _dot_tn(jnp.concatenate(ps, axis=0), dos)
            dk_acc[:n_keys, :] += _dot_tn(dsb, qs)
            dc_acc[:, :n_keys] -= jnp.concatenate(dcs, axis=0)
            dq_ref[rows, :] += _unstack_heads(jnp.dot(dsb, kb, preferred_element_type=F32))
            dcq_ref[rows, :] += _unstack_heads(jnp.broadcast_to(jnp.concatenate(rowsums, axis=0), (2 * ROW_TILE, LANES)))

        first_q = per_kv * jj
        for r in range(per_kv):
            @pl.when(first_q + r < nb)
            def _():
                block(first_q + r, (r + 1) * ROW_TILE, True)

        def rest(i, carry):
            block(i, KV_TILE_BWD, False)
            return carry

        lax.fori_loop(jnp.minimum(first_q + per_kv, nb), nb, rest, 0)
        dk_ref[...] = dk_acc[...].astype(BF16)
        dv_ref[...] = dv_acc[...].astype(BF16)
        dc_ref[0, 0] = dc_acc[...]

    whole = lambda rows: pl.BlockSpec((rows, LANES), lambda hp, jj: (0, hp))
    kv_blk = lambda off: pl.BlockSpec((KV_TILE_BWD, LANES), lambda hp, jj: (jj, off + hp))
    return pl.pallas_call(
        body, name="attn_bwd", grid=(n_pairs, nkb),
        in_specs=[whole(lk), kv_blk(n_pairs), kv_blk(2 * n_pairs), whole(lp), whole(lp), whole(lp),
                  pl.BlockSpec((nb, 1, 1, 2, KV_TILE_BWD), lambda hp, jj: (0, hp, jj, 0, 0))],
        out_specs=[whole(lp), kv_blk(0), kv_blk(0),
                   pl.BlockSpec((1, 1, 2, KV_TILE_BWD), lambda hp, jj: (hp, jj, 0, 0)), whole(lp)],
        out_shape=[jax.ShapeDtypeStruct((lp, ATTN_WIDTH), F32), jax.ShapeDtypeStruct((lk, ATTN_WIDTH), BF16),
                   jax.ShapeDtypeStruct((lk, ATTN_WIDTH), BF16),
                   jax.ShapeDtypeStruct((n_pairs, nkb, 2, KV_TILE_BWD), F32),
                   jax.ShapeDtypeStruct((lp, ATTN_WIDTH), F32)],
        scratch_shapes=[pltpu.VMEM((KV_TILE_BWD, LANES), F32), pltpu.VMEM((KV_TILE_BWD, LANES), F32),
                        pltpu.VMEM((2, KV_TILE_BWD), F32)],
        compiler_params=_params(("parallel", "arbitrary")),
    )(qkv, qkv, qkv, do, lse, delta, bias)


def _gates_bwd(dc, dcq, f, bfg):
    nb = f.shape[0] // ROW_TILE

    def body(dc_ref, dcq_ref, f_ref, b_ref, df_ref, db_ref, carry):
        step = pl.program_id(0)
        i = nb - 1 - step

        @pl.when(step == 0)
        def _():
            carry[...] = jnp.zeros_like(carry)
            db_ref[...] = jnp.zeros_like(db_ref)

        dcb = dc_ref[...]
        lane = lax.broadcasted_iota(jnp.int32, (ROW_TILE, F_COLS), 1)
        for h in range(N_HEADS):
            dcb = dcb + jnp.where(lane == h, dcq_ref[:, HEAD_DIM * h:HEAD_DIM * h + 1], 0.0)
        r_i = lax.broadcasted_iota(jnp.int32, (ROW_TILE, ROW_TILE), 0)
        c_i = lax.broadcasted_iota(jnp.int32, (ROW_TILE, ROW_TILE), 1)
        upper = (c_i >= r_i).astype(F32)
        dlf = jnp.dot(upper, dcb, precision=HIGHEST, preferred_element_type=F32) + carry[...]
        carry[...] = carry[...] + jnp.sum(dcb, axis=0, keepdims=True)
        logit = f_ref[...] + b_ref[...]
        dlogit = jnp.where(_valid_gate_mask(i), dlf * _sigmoid(-logit), 0.0)
        df_ref[...] = dlogit.astype(BF16)
        db_ref[...] += jnp.sum(dlogit, axis=0, keepdims=True)

    blk = pl.BlockSpec((ROW_TILE, F_COLS), lambda s: (nb - 1 - s, 0))
    wide = pl.BlockSpec((ROW_TILE, ATTN_WIDTH), lambda s: (nb - 1 - s, 0))
    one = pl.BlockSpec((1, F_COLS), lambda s: (0, 0))
    return pl.pallas_call(
        body, name="gates_bwd", grid=(nb,),
        in_specs=[blk, wide, blk, one], out_specs=[blk, one],
        out_shape=[jax.ShapeDtypeStruct(f.shape, BF16), jax.ShapeDtypeStruct((1, F_COLS), F32)],
        scratch_shapes=[pltpu.VMEM((1, F_COLS), F32)],
        compiler_params=_params(("arbitrary",)),
    )(dc, dcq, f, bfg)


def _input_bwd(dproj, df, wt_e, wt_qkv, wt_f, x2, metapad, dh1, g1, hs, sc):
    nb = x2.shape[0] // ROW_TILE + 1
    n = len(hs)

    def body(dp_ref, df_ref, we_ref, w_ref, wf_ref, x_ref, mp_ref, dh_ref, g_ref, *rest):
        h_refs, s_ref = rest[:n], rest[n]
        gx_ref, g0_ref, dg_ref = rest[n + 1:n + 4]
        q_refs, sq_ref = rest[n + 4:2 * n + 4], rest[2 * n + 4]
        sems = rest[2 * n + 5:]
        i = pl.program_id(0)

        @pl.when(i == 0)
        def _():
            dg_ref[...] = jnp.zeros_like(dg_ref)
            own, sends, _ = _chip_scatter_plan(h_refs, s_ref, q_refs, sq_ref, *sems)
            for cp in own + sends:
                cp.start()

        dhn = (jnp.dot(dp_ref[:, :E_COLS], we_ref[...], preferred_element_type=F32)
               + jnp.dot(dp_ref[:, E_COLS:], w_ref[...], preferred_element_type=F32)
               + jnp.dot(df_ref[...], wf_ref[...], preferred_element_type=F32))
        h0 = jnp.where(i == 0, mp_ref[...], x_ref[...])
        r = lax.rsqrt(jnp.mean(h0 * h0, axis=-1, keepdims=True) + RMS_EPS)
        xhat = h0 * r
        dg_ref[...] += jnp.sum(dhn * xhat, axis=0, keepdims=True)
        dxhat = dhn * g_ref[...]
        dh0 = dh_ref[...] + r * (dxhat - xhat * jnp.mean(dxhat * xhat, axis=-1, keepdims=True))
        gx_ref[...] = dh0

        @pl.when(i == 0)
        def _():
            g0_ref[...] = dh0

        @pl.when(i == nb - 1)
        def _():
            own, sends, recvs = _chip_scatter_plan(h_refs, s_ref, q_refs, sq_ref, *sems)
            for cp in recvs:
                cp.wait_recv()
            for cp in sends:
                cp.wait_send()
            for cp in own:
                cp.wait()

    const = lambda shape: pl.BlockSpec(shape, lambda i: (0, 0))
    res = pl.pallas_call(
        body, name="input_bwd", grid=(nb,),
        in_specs=[pl.BlockSpec((ROW_TILE, MAIN_COLS), lambda i: (i, 0)), pl.BlockSpec((ROW_TILE, F_COLS), lambda i: (i, 0)),
                  const((E_COLS, D_MODEL)), const((QKV_COLS, D_MODEL)), const((F_COLS, D_MODEL)),
                  _tokens_spec(), const((ROW_TILE, D_MODEL)),
                  pl.BlockSpec((ROW_TILE, D_MODEL), lambda i: (i, 0)), const((1, D_MODEL))] + [HBM] * (n + 1),
        out_specs=[_tokens_spec(), const((ROW_TILE, D_MODEL)), const((1, D_MODEL))] + [HBM] * (n + 1),
        out_shape=[jax.ShapeDtypeStruct(x2.shape, F32), jax.ShapeDtypeStruct((ROW_TILE, D_MODEL), F32),
                   jax.ShapeDtypeStruct((1, D_MODEL), F32)]
        + [jax.ShapeDtypeStruct(h.shape, h.dtype) for h in hs] + [jax.ShapeDtypeStruct((N_CHIPS,) + sc.shape, sc.dtype)],
        scratch_shapes=[pltpu.SemaphoreType.DMA((3 * (n + 1),)), pltpu.SemaphoreType.DMA((3 * (n + 1),)),
                        pltpu.SemaphoreType.DMA((n + 1,))],
        compiler_params=_params(("arbitrary",), vmem=56 * 1024 * 1024),
    )(dproj, df, wt_e, wt_qkv, wt_f, x2, metapad, dh1, g1, *_in_hbm(*hs, sc))
    return res[:3], res[3:3 + n], res[3 + n]


def _adamw(w, g, m, v, name):
    rows, cols = w.shape
    if rows % 8 == 0:
        tr, tc = _row_tile8(rows), cols
    else:
        tr, tc = rows, (2 * LANES if cols % (2 * LANES) == 0 and rows > 8 else cols)

    def body(w_ref, g_ref, m_ref, v_ref, d_ref, mo_ref, vo_ref):
        g_ = g_ref[...]
        m_new = ADAM_B1 * m_ref[...] + (1.0 - ADAM_B1) * g_
        v_new = ADAM_B2 * v_ref[...] + (1.0 - ADAM_B2) * (g_ * g_)
        m_hat = m_new / (1.0 - ADAM_B1 ** ADAM_STEP)
        v_hat = v_new / (1.0 - ADAM_B2 ** ADAM_STEP)
        d_ref[...] = -ADAM_LR * (m_hat / (jnp.sqrt(v_hat) + ADAM_EPS) + ADAM_WD * w_ref[...])
        mo_ref[...] = m_new
        vo_ref[...] = v_new

    blk = pl.BlockSpec((tr, tc), lambda i, j: (i, j))
    return pl.pallas_call(
        body, name=name, grid=(rows // tr, cols // tc),
        in_specs=[blk] * 4, out_specs=[blk] * 3,
        out_shape=[jax.ShapeDtypeStruct(w.shape, F32)] * 3,
        compiler_params=_params(("parallel", "parallel")),
    )(w, g, m, v)


def _adamw_native(w3, g3, m3, v3, name):
    rows = w3.shape[0]
    tr = rows // 2
    blk = pl.BlockSpec((tr,) + w3.shape[1:], lambda i: (i, 0, 0))
    shape = jax.ShapeDtypeStruct(w3.shape, F32)

    def moments(g_ref, m_ref, v_ref, mo_ref, vo_ref):
        g_ = g_ref[...]
        mo_ref[...] = ADAM_B1 * m_ref[...] + (1.0 - ADAM_B1) * g_
        vo_ref[...] = ADAM_B2 * v_ref[...] + (1.0 - ADAM_B2) * (g_ * g_)

    new_m, new_v = pl.pallas_call(
        moments, name=name + "_moments", grid=(2,), in_specs=[blk] * 3, out_specs=[blk] * 2, out_shape=[shape] * 2,
        compiler_params=_params(("parallel",)),
    )(g3, m3, v3)

    def delta(w_ref, m_ref, v_ref, d_ref):
        m_hat = m_ref[...] / (1.0 - ADAM_B1 ** ADAM_STEP)
        v_hat = v_ref[...] / (1.0 - ADAM_B2 ** ADAM_STEP)
        d_ref[...] = -ADAM_LR * (m_hat / (jnp.sqrt(v_hat) + ADAM_EPS) + ADAM_WD * w_ref[...])

    d = pl.pallas_call(
        delta, name=name + "_delta", grid=(2,), in_specs=[blk] * 3, out_specs=blk, out_shape=shape,
        compiler_params=_params(("parallel",)),
    )(w3, new_m, new_v)
    return d, new_m, new_v


def _row_tile8(rows):
    best = rows
    for t in range(8, 257, 8):
        if rows % t == 0:
            best = t
    return best


def kernel(x, meta_tokens, norm_g, w_in, b_forget, pool_w, pool_scale, w_up_pool, w_up_attn, w_out, final_norm_g, loss_target, m_meta_tokens, m_norm_g, m_w_in, m_b_forget, m_pool_w, m_pool_scale, m_w_up_pool, m_w_up_attn, m_w_out, m_final_norm_g, v_meta_tokens, v_norm_g, v_w_in, v_b_forget, v_pool_w, v_pool_scale, v_w_up_pool, v_w_up_attn, v_w_out, v_final_norm_g):
    seq = x.shape[1]
    assert seq % ROW_TILE == 0 and x.shape[0] == 1
    lp = seq + ROW_TILE
    nb = lp // ROW_TILE
    lk = -(-lp // KV_TILE_BWD) * KV_TILE_BWD
    core = jnp.reshape(lax.axis_index("c"), (1,)).astype(jnp.int32)
    x2 = x[0]
    target = loss_target[0]
    sh_d = D_MODEL // N_CHIPS

    to_rows = lambda a: jnp.transpose(a, (2, 0, 1))
    from_rows = lambda a: jnp.transpose(a, (1, 2, 0))
    gf = final_norm_g.reshape(1, D_MODEL)
    bfg = jnp.pad(b_forget, ((0, 0), (0, F_COLS - N_HEADS)))
    pw_b = pool_w[0].astype(BF16)
    hn, (wg_in, meta_g) = _rmsnorm_fwd(x2, norm_g, lk, [jnp.transpose(w_in[0]).astype(BF16), meta_tokens], [1, 0])
    wt = wg_in.reshape(-1, D_MODEL)
    wt_e = jnp.concatenate([wt[REF_U:REF_Q], wt[REF_ZA:REF_F], wt[REF_GP:]], axis=0)
    wt_qkv = wt[REF_Q:REF_ZA]
    wt_f = jnp.pad(wt[REF_F:REF_GP], ((0, F_COLS - N_HEADS), (0, 0)))
    meta_full = jnp.transpose(meta_g, (1, 0, 2)).reshape(N_META, D_MODEL)
    metapad = jnp.pad(meta_full, ((PAD_ROWS, 0), (0, 0)))

    tm = _row_tile(lp, 1100)
    e, (wg_up_p, wg_up_a, wg_out) = _mm_nt(
        hn, wt_e, BF16, "in_proj_gates", lp, E_COLS, tm, 512,
        gather=([w_up_pool[0].astype(BF16), w_up_attn[0].astype(BF16), w_out[0].astype(BF16)], [0, 0, 0]))
    wup_p = jnp.transpose(wg_up_p, (1, 0, 2)).reshape(POOL_WIDTH, D_MODEL)
    wup_a = jnp.transpose(wg_up_a, (1, 0, 2)).reshape(ATTN_WIDTH, D_MODEL)
    wout = wg_out.reshape(D_MODEL, D_MODEL)
    qkv = _mm_nt(hn, wt_qkv, BF16, "in_proj_qkv", lk, QKV_COLS, _row_tile(lk, 1300), 512, scale_first=HEAD_DIM ** -0.5)
    f = _mm_nt(hn, wt_f, F32, "in_proj_forget", lp, F_COLS, tm, F_COLS)
    c = _gates_fwd(f, bfg)
    c_t = jnp.transpose(c[:, :N_HEADS])
    bias = c_t[None, :, :] - jnp.transpose(c_t[:, ::ROW_TILE])[:, :, None]
    bias = jnp.where(jnp.arange(lp) < PAD_ROWS, -NEG, bias)
    bias = jnp.pad(bias, ((0, 0), (0, 0), (0, lk - lp)))
    by_kv = lambda t: jnp.transpose(bias.reshape(nb, N_HEADS // 2, 2, lk // t, t), (0, 1, 3, 2, 4))
    bias, bias_bwd = by_kv(KV_TILE), by_kv(KV_TILE_BWD)
    y_pool = _pool_fwd(e, pw_b, pool_scale)
    o, lse = _attn_fwd(qkv, bias, nb)
    merged, y_attn = _merge_fwd(y_pool, o, e, wup_p, wup_a)
    dh1, loss_part, dgf = _head_fwd_bwd(merged, wout, x2, metapad, gf, target)

    dap, daa, dgate, dza, do, delta, dy_pool = _merge_bwd(dh1, wout, y_pool, y_attn, wup_p, wup_a, e, o)
    dpc, dzp, dscale, dpw = _pool_bwd_local(e, dy_pool, pw_b, pool_scale)
    du = _pool_bwd_window(dpc)
    dq, dk, dv, dc4, dcq = _attn_bwd(qkv, do, lse, delta, bias_bwd, nb)
    dc = jnp.transpose(dc4, (1, 3, 0, 2)).reshape(lk, N_HEADS)[:lp]
    df, db = _gates_bwd(jnp.pad(dc, ((0, 0), (0, F_COLS - N_HEADS))), dcq, f, bfg)
    dproj = jnp.concatenate([du, dzp, dza, dgate, (dq * HEAD_DIM ** -0.5).astype(BF16), dk[:lp], dv[:lp]], axis=1)
    tk = _row_tile(lp, 1100)
    dw_out = _mm_tn(merged, dh1, "grad_w_out", lp, 512, D_MODEL, tk)
    dw_up_p = _mm_tn(y_pool, dap, "grad_w_up_pool", lp, 512, D_MODEL, tk, chunks=N_CHIPS)
    dw_up_a = _mm_tn(y_attn, daa, "grad_w_up_attn", lp, 512, D_MODEL, tk, chunks=N_CHIPS)
    dwt_f = _mm_tn(df, hn, "grad_w_in_forget", lp, F_COLS, D_MODEL, tk)

    def pad8(a):
        return jnp.pad(a, ((0, (-a.shape[0]) % 8), (0, 0)))

    small_parts = [pad8(a) for a in (dgf.reshape(-1, LANES), dscale.reshape(-1, LANES), db, dpw.reshape(-1, LANES),
                                     loss_part)]
    small = jnp.concatenate(small_parts, axis=0)
    soffs = [0]
    for p in small_parts:
        soffs.append(soffs[-1] + p.shape[0])

    gs_rows = [dw_up_p, dw_up_a, dw_out.reshape(N_CHIPS, sh_d, D_MODEL)]
    half = MAIN_COLS // 2
    dwt_a, (*rb_rows, rb_f, sr) = _mm_tn(dproj, hn, "grad_w_in_a", lp, half // 2, D_MODEL, tk, m=half, m_off=0,
                                         swap=(gs_rows + [dwt_f, small], [1, 1, 1, 1, None]))
    dwt_b, (rb_a,) = _mm_tn(dproj, hn, "grad_w_in_b", lp, half // 2, D_MODEL, tk, m=half, m_off=2, swap=([dwt_a], [1]))
    (rb_b,) = _sibling_exchange([dwt_b], [1])
    *hs_rows, sc = _add_sibling_half(gs_rows, rb_rows, [1, 1, 1], small, sr, core)
    h_a, h_b, h_f = _add_halves_2d([dwt_a, dwt_b, dwt_f], [rb_a, rb_b, rb_f], core)
    h_in = jnp.concatenate([h_a[E_U:E_ZA], h_b[D_MODEL:], h_a[E_ZA:E_GP], h_f[:N_HEADS], h_a[E_GP:], h_b[:D_MODEL]], axis=0)
    hs = [h_in.reshape(N_CHIPS, -1, D_MODEL // 2)] + hs_rows
    grad_axes = [2, 1, 1, 1]
    (grad_x, g_block0, dg1), qs, sq = _input_bwd(dproj, df, wt_e, wt_qkv, wt_f, x2, metapad, dh1, norm_g, hs, sc)
    g_w_in_rows, g_w_up_p, g_w_up_a, g_w_out, st = _reduce_allgather(qs, grad_axes, sq, [True, False, False, False])
    late = _small_allreduce(jnp.concatenate([pad8(dg1.reshape(-1, LANES)), g_block0[PAD_ROWS:].reshape(-1, LANES)], axis=0))
    n_norm = D_MODEL // LANES
    g_norm = late[:n_norm].reshape(1, D_MODEL)
    chip = 2 * lax.axis_index("x") + lax.axis_index("y")
    g_meta = lax.dynamic_slice_in_dim(late[n_norm:].reshape(N_META, D_MODEL), chip * sh_d, sh_d, axis=1)

    spiece = lambda k, rows: st[soffs[k]:soffs[k] + rows]
    g_final = spiece(0, D_MODEL // LANES).reshape(1, D_MODEL)
    g_scale = spiece(1, POOL_WIDTH // LANES).reshape(1, POOL_WIDTH)
    g_bf = spiece(2, 1)
    g_pw = spiece(3, POOL_WIDTH)
    loss = st[soffs[4], 0]

    def pad_lanes(a):
        return jnp.pad(a, ((0, 0), (0, F_COLS - N_HEADS)))

    w_in_res = (g_w_in_rows,) + _adamw_native(to_rows(w_in), g_w_in_rows, to_rows(m_w_in), to_rows(v_w_in), "adamw_w_in")

    upd = [
        ("meta_tokens", meta_tokens, g_meta, m_meta_tokens, v_meta_tokens),
        ("norm_g", norm_g, g_norm, m_norm_g, v_norm_g),
        ("w_in", None, None, None, None),
        ("b_forget", pad_lanes(b_forget), g_bf, pad_lanes(m_b_forget), pad_lanes(v_b_forget)),
        ("pool_w", pool_w.reshape(-1, LANES), g_pw, m_pool_w.reshape(-1, LANES), v_pool_w.reshape(-1, LANES)),
        ("pool_scale", pool_scale, g_scale, m_pool_scale, v_pool_scale),
        ("w_up_pool", w_up_pool[0], g_w_up_p, m_w_up_pool[0], v_w_up_pool[0]),
        ("w_up_attn", w_up_attn[0], g_w_up_a, m_w_up_attn[0], v_w_up_attn[0]),
        ("w_out", w_out[0], g_w_out, m_w_out[0], v_w_out[0]),
        ("final_norm_g", gf, g_final, m_final_norm_g.reshape(1, D_MODEL), v_final_norm_g.reshape(1, D_MODEL)),
    ]
    shapes = [meta_tokens.shape, norm_g.shape, w_in.shape, b_forget.shape, pool_w.shape, pool_scale.shape,
              w_up_pool.shape, w_up_attn.shape, w_out.shape, final_norm_g.shape]
    grads, deltas, new_ms, new_vs = [], [], [], []
    for (name, w_, g_, m_in, v_in), shp in zip(upd, shapes):
        if name == "w_in":
            res = tuple(from_rows(a) for a in w_in_res)
        else:
            d_, mn_, vn_ = _adamw(w_, g_, m_in, v_in, "adamw_" + name)
            res = (g_, d_, mn_, vn_)
        if name == "b_forget":
            res = tuple(a[:, :N_HEADS] for a in res)
        for lst, a in zip((grads, deltas, new_ms, new_vs), res):
            lst.append(a.reshape(shp))

    return (loss, grad_x.reshape(x.shape), *grads, *deltas, *new_ms, *new_vs)
```

```python
import jax
import jax.numpy as jnp
from jax import lax
from jax.experimental import pallas as pl
from jax.experimental.pallas import tpu as pltpu

F32 = jnp.float32
BF16 = jnp.bfloat16
MESH = pl.DeviceIdType.MESH
HIGHEST = lax.Precision.HIGHEST
HBM = pl.BlockSpec(memory_space=pltpu.HBM)

D_MODEL = 1024
N_META = 16
POOL_WIDTH = 512
POOL_GROUP = 128
POOL_WINDOWS = (2, 4, 8, 16)
N_HEADS = 8
HEAD_DIM = 64
ATTN_WIDTH = 512
RMS_EPS = 1e-6
N_CHIPS = 4

ADAM_LR = 0.001
ADAM_B1 = 0.9
ADAM_B2 = 0.999
ADAM_EPS = 1e-08
ADAM_WD = 0.01
ADAM_STEP = 10

LANES = 128
ROW_TILE = 256
KV_TILE = 1024
KV_TILE_BWD = 1024
PAD_ROWS = ROW_TILE - N_META
NEG = -1e30

E_U, E_ZP, E_ZA, E_GP, E_GA, E_COLS = 0, 512, 1024, 1536, 2560, 3584
QKV_COLS = 3 * ATTN_WIDTH
MAIN_COLS = E_COLS + QKV_COLS
F_COLS = LANES
REF_U, REF_Q, REF_ZA, REF_F, REF_GP = 0, 1024, 2560, 3072, 3080
VMEM_LIMIT = 48 * 1024 * 1024


def _params(sem=None, vmem=VMEM_LIMIT):
    return pltpu.CompilerParams(dimension_semantics=sem, vmem_limit_bytes=vmem)


def _in_hbm(*arrays):
    return [pltpu.with_memory_space_constraint(a, pltpu.HBM) for a in arrays]


def _row_tile(n, target):
    best = 16
    for t in range(16, target + 1, 16):
        if n % t == 0:
            best = t
    return best


def _sigmoid(x):
    return 1.0 / (1.0 + jnp.exp(-x))


def _half(ref, axis, which):
    n = ref.shape[axis] // 2
    idx = [slice(None)] * len(ref.shape)
    idx[axis] = pl.ds(pl.multiple_of(which * n, n), n)
    return ref.at[tuple(idx)]


def _half_shape(shape, axis):
    s = list(shape)
    s[axis] //= 2
    return tuple(s)


def _gather_start(ins, outs, axes, send, recv, fsend, frecv, local):
    x, y, c = lax.axis_index("x"), lax.axis_index("y"), lax.axis_index("c")
    me = 2 * x + y
    for t in range(len(ins)):
        pltpu.make_async_copy(ins[t], outs[t].at[me], local.at[t]).start()
        for k, chip in enumerate([(1 - x, y), (x, 1 - y), (1 - x, 1 - y)]):
            pltpu.make_async_remote_copy(
                src_ref=_half(ins[t], axes[t], c), dst_ref=_half(outs[t].at[me], axes[t], c),
                send_sem=send.at[3 * t + k], recv_sem=recv.at[3 * t + k], device_id=(*chip, c), device_id_type=MESH).start()


def _gather_finish(ins, outs, axes, send, recv, fsend, frecv, local):
    x, y, c = lax.axis_index("x"), lax.axis_index("y"), lax.axis_index("c")
    me = 2 * x + y
    sibling = (x, y, 1 - c)
    chips = [(1 - x, y), (x, 1 - y), (1 - x, 1 - y)]
    n = len(ins)

    def over_ici(t, k, chip, dst_slot):
        return pltpu.make_async_remote_copy(
            src_ref=_half(ins[t], axes[t], c), dst_ref=_half(outs[t].at[dst_slot], axes[t], c),
            send_sem=send.at[3 * t + k], recv_sem=recv.at[3 * t + k], device_id=(*chip, c), device_id_type=MESH)

    def to_sibling(t, k, slot, which):
        return pltpu.make_async_remote_copy(
            src_ref=_half(outs[t].at[slot], axes[t], which), dst_ref=_half(outs[t].at[slot], axes[t], which),
            send_sem=fsend.at[3 * t + k], recv_sem=frecv.at[3 * t + k], device_id=sibling, device_id_type=MESH)

    forwards = []
    for t in range(n):
        for k, (px, py) in enumerate(chips):
            over_ici(t, k, (px, py), 2 * px + py).wait_recv()
            fw = to_sibling(t, k, 2 * px + py, c)
            fw.start()
            forwards.append(fw)
    for t in range(n):
        for k, (px, py) in enumerate(chips):
            to_sibling(t, k, 2 * px + py, 1 - c).wait_recv()
    for t in range(n):
        for k, chip in enumerate(chips):
            over_ici(t, k, chip, me).wait_send()
    for fw in forwards:
        fw.wait_send()
    for t in range(n):
        pltpu.make_async_copy(ins[t], outs[t].at[me], local.at[t]).wait()


def _gather_sems(n):
    return [pltpu.SemaphoreType.DMA((3 * n,)), pltpu.SemaphoreType.DMA((3 * n,)), pltpu.SemaphoreType.DMA((3 * n,)),
            pltpu.SemaphoreType.DMA((3 * n,)), pltpu.SemaphoreType.DMA((n,))]


def _gathered_shapes(shards):
    return [jax.ShapeDtypeStruct((N_CHIPS,) + s.shape, s.dtype) for s in shards]


def _swap_copies(srcs, dsts, axes, send, recv):
    x, y, c = lax.axis_index("x"), lax.axis_index("y"), lax.axis_index("c")
    return [pltpu.make_async_remote_copy(
        src_ref=srcs[t] if axes[t] is None else _half(srcs[t], axes[t], 1 - c), dst_ref=dsts[t],
        send_sem=send.at[t], recv_sem=recv.at[t], device_id=(x, y, 1 - c), device_id_type=MESH) for t in range(len(srcs))]


def _swap_shapes(srcs, axes):
    return [jax.ShapeDtypeStruct(g.shape if a is None else _half_shape(g.shape, a), g.dtype) for g, a in zip(srcs, axes)]


def _sibling_exchange(gs, axes):
    n = len(gs)

    def body(*refs):
        cps = _swap_copies(refs[:n], refs[n:2 * n], axes, *refs[2 * n:])
        for cp in cps:
            cp.start()
        for cp in cps:
            cp.wait()

    return pl.pallas_call(
        body, name="grad_sibling_exchange",
        in_specs=[HBM] * n, out_specs=[HBM] * n, out_shape=_swap_shapes(gs, axes),
        scratch_shapes=[pltpu.SemaphoreType.DMA((n,)), pltpu.SemaphoreType.DMA((n,))],
    )(*_in_hbm(*gs))


def _add_halves_2d(parts, rbs, core):
    n = len(parts)
    rows, w = rbs[0].shape
    tr = 512
    last_rows = rbs[-1].shape[0]

    def body(core_ref, *refs):
        p_refs, r_refs, h_refs = refs[:n], refs[n:2 * n], refs[2 * n:]
        for t in range(n - 1):
            h_refs[t][...] = (p_refs[t][...] + r_refs[t][...]).astype(BF16)

        @pl.when(pl.program_id(0) == 0)
        def _():
            h_refs[n - 1][...] = (p_refs[n - 1][...] + r_refs[n - 1][...]).astype(BF16)

    mine = lambda r: pl.BlockSpec((r, w), lambda i, cr: (i, cr[0]))
    tile = lambda r: pl.BlockSpec((r, w), lambda i, cr: (i, 0))
    mine_small = pl.BlockSpec((last_rows, w), lambda i, cr: (0, cr[0]))
    tile_small = pl.BlockSpec((last_rows, w), lambda i, cr: (0, 0))
    return pl.pallas_call(
        body, name="grad_add_sibling_w_in",
        grid_spec=pltpu.PrefetchScalarGridSpec(
            num_scalar_prefetch=1, grid=(rows // tr,),
            in_specs=[mine(tr)] * (n - 1) + [mine_small] + [tile(tr)] * (n - 1) + [tile_small],
            out_specs=[tile(tr)] * (n - 1) + [tile_small]),
        out_shape=[jax.ShapeDtypeStruct(r.shape, BF16) for r in rbs],
        compiler_params=_params(("arbitrary",)),
    )(core, *parts, *rbs)


def _add_sibling_half(gs, rbs, axes, small, sr, core):
    n = len(gs)

    def body(core_ref, *refs):
        g_refs, rb_refs = refs[:n], refs[n:2 * n]
        s_ref, sr_ref = refs[2 * n], refs[2 * n + 1]
        h_refs, sc_ref = refs[2 * n + 2:3 * n + 2], refs[3 * n + 2]
        for t in range(n):
            h_refs[t][...] = (g_refs[t][...] + rb_refs[t][...]).astype(BF16)
        sc_ref[...] = s_ref[...] + sr_ref[...]

    def mine(rb, axis):
        blk = (None,) + rb.shape[1:]
        if axis == 2:
            return pl.BlockSpec(blk, lambda j, cr: (j, 0, cr[0]))
        return pl.BlockSpec(blk, lambda j, cr: (j, cr[0], 0))

    chunk = lambda rb: pl.BlockSpec((None,) + rb.shape[1:], lambda j, cr: (j, 0, 0))
    whole = pl.BlockSpec(small.shape, lambda j, cr: (0, 0))
    return pl.pallas_call(
        body, name="grad_add_sibling",
        grid_spec=pltpu.PrefetchScalarGridSpec(
            num_scalar_prefetch=1, grid=(N_CHIPS,),
            in_specs=[mine(rb, a) for rb, a in zip(rbs, axes)] + [chunk(rb) for rb in rbs] + [whole, whole],
            out_specs=[chunk(rb) for rb in rbs] + [whole]),
        out_shape=[jax.ShapeDtypeStruct(rb.shape, BF16) for rb in rbs] + [jax.ShapeDtypeStruct(small.shape, F32)],
        compiler_params=_params(("arbitrary",)),
    )(core, *gs, *rbs, small, sr)


def _chip_scatter_plan(h_refs, s_ref, q_refs, sq_ref, send, recv, local):
    n = len(h_refs)
    x, y, c = lax.axis_index("x"), lax.axis_index("y"), lax.axis_index("c")
    me = 2 * x + y
    chips = [(1 - x, y), (x, 1 - y), (1 - x, 1 - y)]

    def copy(t, k, chip, src_slot, dst_slot):
        src = h_refs[t].at[src_slot] if t < n else s_ref
        dst = (q_refs[t] if t < n else sq_ref).at[dst_slot]
        return pltpu.make_async_remote_copy(src_ref=src, dst_ref=dst, send_sem=send.at[3 * t + k],
                                            recv_sem=recv.at[3 * t + k], device_id=(*chip, c), device_id_type=MESH)

    own = [pltpu.make_async_copy(h_refs[t].at[me], q_refs[t].at[me], local.at[t]) for t in range(n)]
    own.append(pltpu.make_async_copy(s_ref, sq_ref.at[me], local.at[n]))
    sends = [copy(t, k, (px, py), 2 * px + py, me) for t in range(n + 1) for k, (px, py) in enumerate(chips)]
    recvs = [copy(t, k, (px, py), me, 2 * px + py) for t in range(n + 1) for k, (px, py) in enumerate(chips)]
    return own, sends, recvs


def _small_allreduce(buf):
    def body(b_ref, o_ref, sib_buf, chip_buf, send, recv):
        x, y, c = lax.axis_index("x"), lax.axis_index("y"), lax.axis_index("c")
        me = 2 * x + y
        chips = [(1 - x, y), (x, 1 - y), (1 - x, 1 - y)]
        swap = pltpu.make_async_remote_copy(src_ref=b_ref, dst_ref=sib_buf, send_sem=send.at[0], recv_sem=recv.at[0],
                                            device_id=(x, y, 1 - c), device_id_type=MESH)
        swap.start()
        swap.wait()
        chip_buf[me] = b_ref[...] + sib_buf[...]

        def copy(k, chip, slot):
            return pltpu.make_async_remote_copy(src_ref=chip_buf.at[slot], dst_ref=chip_buf.at[slot], send_sem=send.at[1 + k],
                                                recv_sem=recv.at[1 + k], device_id=(*chip, c), device_id_type=MESH)

        sends = [copy(k, chip, me) for k, chip in enumerate(chips)]
        for cp in sends:
            cp.start()
        for k, (px, py) in enumerate(chips):
            copy(k, (px, py), 2 * px + py).wait_recv()
        for cp in sends:
            cp.wait_send()
        o_ref[...] = ((chip_buf[0] + chip_buf[1]) + chip_buf[2]) + chip_buf[3]

    vmem = pl.BlockSpec(memory_space=pltpu.VMEM)
    return pl.pallas_call(
        body, name="small_allreduce", in_specs=[vmem], out_specs=vmem,
        out_shape=jax.ShapeDtypeStruct(buf.shape, F32),
        scratch_shapes=[pltpu.VMEM(buf.shape, F32), pltpu.VMEM((N_CHIPS,) + buf.shape, F32),
                        pltpu.SemaphoreType.DMA((4,)), pltpu.SemaphoreType.DMA((4,))],
        compiler_params=_params(),
    )(buf)


def _reduce_allgather(qs, axes, sq, as_rows):
    n = len(qs)
    shard_shapes, half_axes = [], []
    for q, a, rows_form in zip(qs, axes, as_rows):
        shape = [d * 2 if i == a - 1 else d for i, d in enumerate(q.shape[1:])]
        if rows_form:
            assert a == 2
            shape = [shape[0], 1, shape[1]]
        shard_shapes.append(tuple(shape))
        half_axes.append(2 if rows_form else a - 1)

    def body(*refs):
        q_refs, sq_ref = refs[:n], refs[n]
        o_refs, st_ref = refs[n + 1:2 * n + 1], refs[2 * n + 1]
        send, recv = refs[2 * n + 2:]
        x, y, c = lax.axis_index("x"), lax.axis_index("y"), lax.axis_index("c")

        def swap(t, which):
            return pltpu.make_async_remote_copy(
                src_ref=_half(o_refs[t], half_axes[t], which), dst_ref=_half(o_refs[t], half_axes[t], which),
                send_sem=send.at[t], recv_sem=recv.at[t], device_id=(x, y, 1 - c), device_id_type=MESH)

        sent = []
        for t in range(n):
            q = q_refs[t]
            total = ((q[0].astype(F32) + q[1].astype(F32)) + q[2].astype(F32)) + q[3].astype(F32)
            if as_rows[t]:
                total = total.reshape(total.shape[0], 1, total.shape[1])
            _half(o_refs[t], half_axes[t], c)[...] = total
            cp = swap(t, c)
            cp.start()
            sent.append(cp)
        st_ref[...] = ((sq_ref[0] + sq_ref[1]) + sq_ref[2]) + sq_ref[3]
        for t in range(n):
            swap(t, 1 - c).wait_recv()
        for cp in sent:
            cp.wait_send()

    vmem = pl.BlockSpec(memory_space=pltpu.VMEM)
    return pl.pallas_call(
        body, name="grad_reduce_allgather",
        in_specs=[vmem] * (n + 1), out_specs=[vmem] * (n + 1),
        out_shape=[jax.ShapeDtypeStruct(s, F32) for s in shard_shapes] + [jax.ShapeDtypeStruct(sq.shape[1:], F32)],
        scratch_shapes=[pltpu.SemaphoreType.DMA((n,)), pltpu.SemaphoreType.DMA((n,))],
        compiler_params=_params(),
    )(*qs, sq)


def _dot_nt(a, b):
    return lax.dot_general(a, b, (((1,), (1,)), ((), ())), preferred_element_type=F32)


def _dot_tn(a, b):
    return lax.dot_general(a, b, (((0,), (0,)), ((), ())), preferred_element_type=F32)


def _mm_nt(a, bt, out_dtype, name, m, n, tm, tn, row_block=0, scale_first=None, gather=None):
    k = a.shape[1]
    shards, axes = gather if gather is not None else ([], [])
    ng = len(shards)
    grid = (m // tm, n // tn)

    def body(a_ref, b_ref, *rest):
        ins, o_ref, outs, sems = rest[:ng], rest[ng], rest[ng + 1:2 * ng + 1], rest[2 * ng + 1:]
        first = (pl.program_id(0) == 0) & (pl.program_id(1) == 0)
        last = (pl.program_id(0) == grid[0] - 1) & (pl.program_id(1) == grid[1] - 1)
        if ng:
            @pl.when(first)
            def _():
                _gather_start(ins, outs, axes, *sems)

        r = _dot_nt(a_ref[...], b_ref[...])
        if scale_first is not None:
            r = r * jnp.where(pl.program_id(1) == 0, scale_first, 1.0)
        o_ref[...] = r.astype(out_dtype)

        if ng:
            @pl.when(last)
            def _():
                _gather_finish(ins, outs, axes, *sems)

    res = pl.pallas_call(
        body, name=name, grid=grid,
        in_specs=[pl.BlockSpec((tm, k), lambda i, j: (i, 0)), pl.BlockSpec((tn, k), lambda i, j: (row_block + j, 0))]
        + [HBM] * ng,
        out_specs=[pl.BlockSpec((tm, tn), lambda i, j: (i, j))] + [HBM] * ng,
        out_shape=[jax.ShapeDtypeStruct((m, n), out_dtype)] + _gathered_shapes(shards),
        scratch_shapes=_gather_sems(ng) if ng else [],
        compiler_params=_params(("arbitrary", "arbitrary") if ng else ("parallel", "parallel")),
    )(a, bt, *_in_hbm(*shards))
    return (res[0], res[1:]) if ng else res[0]


def _mm_tn(a, b, name, k, tm, tn, tk, chunks=1, m=None, m_off=0, swap=None):
    m = a.shape[1] if m is None else m
    n = b.shape[1]
    cw = n // chunks
    srcs, axes = swap if swap is not None else ([], [])
    ns = len(srcs)
    grid = (m // tm, n // tn, k // tk)

    def body(a_ref, b_ref, *rest):
        s_refs, o_ref, d_refs, sems = rest[:ns], rest[ns], rest[ns + 1:2 * ns + 1], rest[2 * ns + 1:]
        ids = [pl.program_id(d) for d in range(3)]
        if ns:
            @pl.when((ids[0] == 0) & (ids[1] == 0) & (ids[2] == 0))
            def _():
                for cp in _swap_copies(s_refs, d_refs, axes, *sems):
                    cp.start()

        @pl.when(ids[2] == 0)
        def _():
            o_ref[...] = jnp.zeros_like(o_ref)
        r = _dot_tn(a_ref[...].astype(BF16), b_ref[...].astype(BF16))
        if chunks > 1:
            for c in range(chunks):
                o_ref[c] += r[:, c * cw:(c + 1) * cw]
        else:
            o_ref[...] += r

        if ns:
            @pl.when((ids[0] == grid[0] - 1) & (ids[1] == grid[1] - 1) & (ids[2] == grid[2] - 1))
            def _():
                for cp in _swap_copies(s_refs, d_refs, axes, *sems):
                    cp.wait()

    if chunks > 1:
        assert tn == n
        out_spec = pl.BlockSpec((chunks, tm, cw), lambda i, j, kk: (0, i, 0))
        out_shape = jax.ShapeDtypeStruct((chunks, m, cw), F32)
    else:
        out_spec = pl.BlockSpec((tm, tn), lambda i, j, kk: (i, j))
        out_shape = jax.ShapeDtypeStruct((m, n), F32)
    res = pl.pallas_call(
        body, name=name, grid=grid,
        in_specs=[pl.BlockSpec((tk, tm), lambda i, j, kk: (kk, m_off + i)), pl.BlockSpec((tk, tn), lambda i, j, kk: (kk, j))]
        + [HBM] * ns,
        out_specs=[out_spec] + [HBM] * ns, out_shape=[out_shape] + _swap_shapes(srcs, axes),
        scratch_shapes=[pltpu.SemaphoreType.DMA((ns,)), pltpu.SemaphoreType.DMA((ns,))] if ns else [],
        compiler_params=_params(("arbitrary",) * 3 if ns else ("parallel", "parallel", "arbitrary")),
    )(a, b, *_in_hbm(*srcs))
    return (res[0], res[1:]) if ns else res[0]


def _tokens_spec():
    return pl.BlockSpec((ROW_TILE, D_MODEL), lambda i: (jnp.maximum(i - 1, 0), 0))


def _rmsnorm_fwd(x2, g1, lk, shards, axes):
    nb = x2.shape[0] // ROW_TILE + 1
    nblk = lk // ROW_TILE
    ng = len(shards)
    meta_cols = shards[-1].shape[1]

    def body(x_ref, g_ref, *rest):
        ins, hn_ref, outs = rest[:ng], rest[ng], rest[ng + 1:2 * ng + 1]
        sems, meta_buf, meta_sem = rest[2 * ng + 1:2 * ng + 6], rest[2 * ng + 6], rest[2 * ng + 7]
        s = pl.program_id(0)
        blk = (s + 1) % nblk

        @pl.when(s == 0)
        def _():
            _gather_start(ins, outs, axes, *sems)

        def normed(h):
            r = lax.rsqrt(jnp.mean(h * h, axis=-1, keepdims=True) + RMS_EPS)
            return ((h * r) * g_ref[...]).astype(BF16)

        @pl.when(s < nblk - 1)
        def _():
            hn_ref[...] = normed(jnp.where(blk >= nb, 0.0, x_ref[...]))

        @pl.when(s == nblk - 1)
        def _():
            _gather_finish(ins, outs, axes, *sems)
            fetch = pltpu.make_async_copy(outs[-1], meta_buf, meta_sem.at[0])
            fetch.start()
            fetch.wait()
            meta = jnp.concatenate([meta_buf[j] for j in range(N_CHIPS)], axis=1)
            hn_ref[...] = normed(jnp.concatenate([jnp.zeros((PAD_ROWS, D_MODEL), F32), meta], axis=0))

    res = pl.pallas_call(
        body, name="rmsnorm_fwd", grid=(nblk,),
        in_specs=[pl.BlockSpec((ROW_TILE, D_MODEL), lambda s: (jnp.clip((s + 1) % nblk - 1, 0, nb - 2), 0)),
                  pl.BlockSpec((1, D_MODEL), lambda s: (0, 0))] + [HBM] * ng,
        out_specs=[pl.BlockSpec((ROW_TILE, D_MODEL), lambda s: ((s + 1) % nblk, 0))] + [HBM] * ng,
        out_shape=[jax.ShapeDtypeStruct((lk, D_MODEL), BF16)] + _gathered_shapes(shards),
        scratch_shapes=_gather_sems(ng) + [pltpu.VMEM((N_CHIPS, N_META, meta_cols), F32), pltpu.SemaphoreType.DMA((1,))],
        compiler_params=_params(("arbitrary",)),
    )(x2, g1, *_in_hbm(*shards))
    return res[0], res[1:]


def _valid_gate_mask(i):
    row = i * ROW_TILE + lax.broadcasted_iota(jnp.int32, (ROW_TILE, F_COLS), 0)
    col = lax.broadcasted_iota(jnp.int32, (ROW_TILE, F_COLS), 1)
    return (row >= PAD_ROWS) & (col < N_HEADS)


def _gates_fwd(f, bfg):
    nb = f.shape[0] // ROW_TILE

    def body(f_ref, b_ref, c_ref, carry):
        i = pl.program_id(0)

        @pl.when(i == 0)
        def _():
            carry[...] = jnp.zeros_like(carry)

        logit = f_ref[...] + b_ref[...]
        lf = jnp.minimum(logit, 0.0) - jnp.log1p(jnp.exp(-jnp.abs(logit)))
        lf = jnp.where(_valid_gate_mask(i), lf, 0.0)
        r_i = lax.broadcasted_iota(jnp.int32, (ROW_TILE, ROW_TILE), 0)
        c_i = lax.broadcasted_iota(jnp.int32, (ROW_TILE, ROW_TILE), 1)
        tri = (c_i <= r_i).astype(F32)
        c_ref[...] = jnp.dot(tri, lf, precision=HIGHEST, preferred_element_type=F32) + carry[...]
        carry[...] = carry[...] + jnp.sum(lf, axis=0, keepdims=True)

    return pl.pallas_call(
        body, name="gates_fwd", grid=(nb,),
        in_specs=[pl.BlockSpec((ROW_TILE, F_COLS), lambda i: (i, 0)), pl.BlockSpec((1, F_COLS), lambda i: (0, 0))],
        out_specs=pl.BlockSpec((ROW_TILE, F_COLS), lambda i: (i, 0)),
        out_shape=jax.ShapeDtypeStruct(f.shape, F32),
        scratch_shapes=[pltpu.VMEM((1, F_COLS), F32)],
        compiler_params=_params(("arbitrary",)),
    )(f, bfg)


def _pool_counts(i):
    row = i * ROW_TILE + lax.broadcasted_iota(jnp.int32, (ROW_TILE, 1), 0)
    return jnp.maximum(row - PAD_ROWS, 0)


def _trailing_sums(xc, levels):
    acc = xc
    for lv in range(levels):
        acc = acc + pltpu.roll(acc, 1 << lv, 0)
    return acc


def _leading_sums(xc, levels):
    n = xc.shape[0]
    acc = xc
    for lv in range(levels):
        acc = acc + pltpu.roll(acc, n - (1 << lv), 0)
    return acc


def _pool_p(u_cur, u_prev, i):
    pos = _pool_counts(i)
    ps, invs = [], []
    for g, w in enumerate(POOL_WINDOWS):
        sl = slice(g * POOL_GROUP, (g + 1) * POOL_GROUP)
        cur = u_cur[:, sl]
        xc = jnp.concatenate([u_prev[:, sl], cur], axis=0)
        win = _trailing_sums(xc, g + 1)[ROW_TILE:, :]
        inv = 1.0 / jnp.minimum(pos + 1, w).astype(F32)
        ps.append(win * inv - cur)
        invs.append(inv)
    return ps, invs


def _pool_fwd(e, pw, scale):
    nb = e.shape[0] // ROW_TILE

    def body(uc_ref, up_ref, z_ref, pw_ref, sc_ref, y_ref):
        i = pl.program_id(0)
        u_cur = uc_ref[...].astype(F32)
        u_prev = jnp.where(i == 0, 0.0, up_ref[...].astype(F32))
        ps, _ = _pool_p(u_cur, u_prev, i)
        z = z_ref[...].astype(F32)
        gate = z * _sigmoid(z)
        for g in range(len(POOL_WINDOWS)):
            sl = slice(g * POOL_GROUP, (g + 1) * POOL_GROUP)
            yraw = jnp.dot(ps[g].astype(BF16), pw_ref[g], preferred_element_type=F32)
            y_ref[:, sl] = ((yraw * sc_ref[:, sl]) * gate[:, sl]).astype(BF16)

    blk = (ROW_TILE, POOL_WIDTH)
    return pl.pallas_call(
        body, name="pool_fwd", grid=(nb,),
        in_specs=[pl.BlockSpec(blk, lambda i: (i, 0)), pl.BlockSpec(blk, lambda i: (jnp.maximum(i - 1, 0), 0)),
                  pl.BlockSpec(blk, lambda i: (i, 1)),
                  pl.BlockSpec((len(POOL_WINDOWS), POOL_GROUP, POOL_GROUP), lambda i: (0, 0, 0)),
                  pl.BlockSpec((1, POOL_WIDTH), lambda i: (0, 0))],
        out_specs=pl.BlockSpec(blk, lambda i: (i, 0)),
        out_shape=jax.ShapeDtypeStruct((e.shape[0], POOL_WIDTH), BF16),
        compiler_params=_params(("parallel",)),
    )(e, e, e, pw, scale)


def _stack_heads(a):
    first = lax.broadcasted_iota(jnp.int32, a.shape, 1) < HEAD_DIM
    zero = jnp.zeros_like(a)
    return jnp.concatenate([jnp.where(first, a, zero), jnp.where(first, zero, a)], axis=0)


def _unstack_heads(a):
    rows = a.shape[0] // 2
    first = lax.broadcasted_iota(jnp.int32, (rows, LANES), 1) < HEAD_DIM
    return jnp.where(first, a[:rows], a[rows:])


def _causal(i, jj, stacked, kv_tile):
    r = lax.broadcasted_iota(jnp.int32, (stacked * ROW_TILE, kv_tile), 0)
    if stacked == 2:
        r = jnp.where(r >= ROW_TILE, r - ROW_TILE, r)
    kidx = jj * kv_tile + lax.broadcasted_iota(jnp.int32, (stacked * ROW_TILE, kv_tile), 1)
    return kidx <= i * ROW_TILE + r


def _stack_rows(b):
    n = b.shape[1]
    return jnp.concatenate([jnp.broadcast_to(b[0:1], (ROW_TILE, n)), jnp.broadcast_to(b[1:2], (ROW_TILE, n))], axis=0)


def _attn_fwd(qkv, bias, nb):
    lk = qkv.shape[0]
    lp = nb * ROW_TILE
    nkb = lk // KV_TILE
    n_pairs = N_HEADS // 2

    def body(q_ref, k_ref, v_ref, b_ref, o_ref, lse_ref):
        i = pl.program_id(1)
        qs = _stack_heads(q_ref[...])
        last = (i * ROW_TILE) // KV_TILE

        def block(jj, carry, masked, n_keys=KV_TILE):
            m, l, acc = carry
            rows = pl.ds(pl.multiple_of(jj * KV_TILE, KV_TILE), n_keys)
            s = _dot_nt(qs, k_ref[rows, :]) - _stack_rows(b_ref[0, 0, jj][:, :n_keys])
            if masked:
                s = jnp.where(_causal(i, jj, 2, KV_TILE)[:, :n_keys], s, NEG)
            m_new = jnp.maximum(m, jnp.max(s, axis=1, keepdims=True))
            alpha = jnp.exp(m - m_new)
            p = jnp.exp(s - m_new)
            l = alpha * l + jnp.sum(p, axis=1, keepdims=True)
            acc = alpha * acc + jnp.dot(p.astype(BF16), v_ref[rows, :], preferred_element_type=F32)
            return m_new, l, acc

        init = (jnp.full((2 * ROW_TILE, 1), NEG, F32), jnp.zeros((2 * ROW_TILE, 1), F32),
                jnp.zeros((2 * ROW_TILE, LANES), F32))
        carry = lax.fori_loop(0, last, lambda jj, c: block(jj, c, False), init)
        ends = [lambda c, n=(r + 1) * ROW_TILE: block(last, c, True, n) for r in range(KV_TILE // ROW_TILE)]
        m, l, acc = lax.switch(i - last * (KV_TILE // ROW_TILE), ends, carry)
        o_ref[...] = _unstack_heads(acc / l)
        lse_ref[...] = _unstack_heads(jnp.broadcast_to(m + jnp.log(l), (2 * ROW_TILE, LANES)))

    return pl.pallas_call(
        body, name="attn_fwd", grid=(n_pairs, nb),
        in_specs=[pl.BlockSpec((ROW_TILE, LANES), lambda hp, i: (i, hp)),
                  pl.BlockSpec((lk, LANES), lambda hp, i: (0, n_pairs + hp)),
                  pl.BlockSpec((lk, LANES), lambda hp, i: (0, 2 * n_pairs + hp)),
                  pl.BlockSpec((1, 1, nkb, 2, KV_TILE), lambda hp, i: (i, hp, 0, 0, 0))],
        out_specs=[pl.BlockSpec((ROW_TILE, LANES), lambda hp, i: (i, hp)),
                   pl.BlockSpec((ROW_TILE, LANES), lambda hp, i: (i, hp))],
        out_shape=[jax.ShapeDtypeStruct((lp, ATTN_WIDTH), F32), jax.ShapeDtypeStruct((lp, ATTN_WIDTH), F32)],
        compiler_params=_params(("parallel", "parallel")),
    )(qkv, qkv, qkv, bias)


def _merge_fwd(y_pool, o, e, wup_p, wup_a):
    lp = o.shape[0]
    nb = lp // ROW_TILE

    def body(yp_ref, o_ref, e_ref, wp_ref, wa_ref, mg_ref, ya_ref):
        za = e_ref[:, E_ZA:E_GP].astype(F32)
        ya = (o_ref[...] * (za * _sigmoid(za))).astype(BF16)
        ya_ref[...] = ya
        a_pool = jnp.dot(yp_ref[...], wp_ref[...], preferred_element_type=F32)
        a_attn = jnp.dot(ya, wa_ref[...], preferred_element_type=F32)
        mg_ref[...] = (_sigmoid(e_ref[:, E_GP:E_GA].astype(F32)) * a_pool
                       + _sigmoid(e_ref[:, E_GA:E_COLS].astype(F32)) * a_attn).astype(BF16)

    return pl.pallas_call(
        body, name="merge_fwd", grid=(nb,),
        in_specs=[pl.BlockSpec((ROW_TILE, POOL_WIDTH), lambda i: (i, 0)),
                  pl.BlockSpec((ROW_TILE, ATTN_WIDTH), lambda i: (i, 0)),
                  pl.BlockSpec((ROW_TILE, E_COLS), lambda i: (i, 0)),
                  pl.BlockSpec((POOL_WIDTH, D_MODEL), lambda i: (0, 0)),
                  pl.BlockSpec((ATTN_WIDTH, D_MODEL), lambda i: (0, 0))],
        out_specs=[pl.BlockSpec((ROW_TILE, D_MODEL), lambda i: (i, 0)),
                   pl.BlockSpec((ROW_TILE, ATTN_WIDTH), lambda i: (i, 0))],
        out_shape=[jax.ShapeDtypeStruct((lp, D_MODEL), BF16), jax.ShapeDtypeStruct((lp, ATTN_WIDTH), BF16)],
        compiler_params=_params(("parallel",)),
    )(y_pool, o, e, wup_p, wup_a)


def _head_fwd_bwd(merged, w_out, x2, metapad, gf, target):
    lp = merged.shape[0]
    nb = lp // ROW_TILE

    def body(mg_ref, w_ref, x_ref, mp_ref, g_ref, t_ref, dh_ref, loss_ref, dg_ref):
        i = pl.program_id(0)

        @pl.when(i == 0)
        def _():
            loss_ref[...] = jnp.zeros_like(loss_ref)
            dg_ref[...] = jnp.zeros_like(dg_ref)

        h0 = jnp.where(i == 0, mp_ref[...], x_ref[...])
        h1 = h0 + jnp.dot(mg_ref[...], w_ref[...], preferred_element_type=F32)
        r = lax.rsqrt(jnp.mean(h1 * h1, axis=-1, keepdims=True) + RMS_EPS)
        xhat = h1 * r
        g = g_ref[...]
        err = jnp.where(i == 0, 0.0, xhat * g - t_ref[...])
        loss_ref[...] += 0.5 * jnp.sum(jnp.mean(err * err, axis=-1, keepdims=True))
        dy = err / D_MODEL
        dg_ref[...] += jnp.sum(dy * xhat, axis=0, keepdims=True)
        dxhat = dy * g
        dh_ref[...] = r * (dxhat - xhat * jnp.mean(dxhat * xhat, axis=-1, keepdims=True))

    return pl.pallas_call(
        body, name="head_fwd_bwd", grid=(nb,),
        in_specs=[pl.BlockSpec((ROW_TILE, D_MODEL), lambda i: (i, 0)),
                  pl.BlockSpec((D_MODEL, D_MODEL), lambda i: (0, 0)),
                  _tokens_spec(), pl.BlockSpec((ROW_TILE, D_MODEL), lambda i: (0, 0)),
                  pl.BlockSpec((1, D_MODEL), lambda i: (0, 0)), _tokens_spec()],
        out_specs=[pl.BlockSpec((ROW_TILE, D_MODEL), lambda i: (i, 0)),
                   pl.BlockSpec((1, LANES), lambda i: (0, 0)), pl.BlockSpec((1, D_MODEL), lambda i: (0, 0))],
        out_shape=[jax.ShapeDtypeStruct((lp, D_MODEL), F32), jax.ShapeDtypeStruct((1, LANES), F32),
                   jax.ShapeDtypeStruct((1, D_MODEL), F32)],
        compiler_params=_params(("arbitrary",)),
    )(merged, w_out, x2, metapad, gf, target)


def _per_head_rowsum(t):
    head = lax.broadcasted_iota(jnp.int32, t.shape, 1) // HEAD_DIM
    out = jnp.zeros_like(t)
    for h in range(N_HEADS):
        sel = head == h
        out = jnp.where(sel, jnp.sum(jnp.where(sel, t, 0.0), axis=1, keepdims=True), out)
    return out


def _merge_bwd(dh1, w_out, y_pool, y_attn, wup_p, wup_a, e, o):
    lp = o.shape[0]
    nb = lp // ROW_TILE

    def body(dh_ref, wo_ref, yp_ref, ya_ref, wp_ref, wa_ref, e_ref, o_ref,
             dap_ref, daa_ref, dg_ref, dza_ref, do_ref, delta_ref, dyp_ref):
        dmerged = _dot_nt(dh_ref[...].astype(BF16), wo_ref[...])
        a_pool = jnp.dot(yp_ref[...], wp_ref[...], preferred_element_type=F32)
        a_attn = jnp.dot(ya_ref[...], wa_ref[...], preferred_element_type=F32)
        sp = _sigmoid(e_ref[:, E_GP:E_GA].astype(F32))
        sa = _sigmoid(e_ref[:, E_GA:E_COLS].astype(F32))
        dap = (dmerged * sp).astype(BF16)
        daa = (dmerged * sa).astype(BF16)
        dap_ref[...] = dap
        daa_ref[...] = daa
        dg_ref[:, :D_MODEL] = (dmerged * a_pool * (sp * (1.0 - sp))).astype(BF16)
        dg_ref[:, D_MODEL:] = (dmerged * a_attn * (sa * (1.0 - sa))).astype(BF16)
        dyp_ref[...] = _dot_nt(dap, wp_ref[...])
        dya = _dot_nt(daa, wa_ref[...])
        za = e_ref[:, E_ZA:E_GP].astype(F32)
        sz = _sigmoid(za)
        o = o_ref[...]
        do = dya * (za * sz)
        do_ref[...] = do.astype(BF16)
        dza_ref[...] = (dya * o * (sz * (1.0 + za * (1.0 - sz)))).astype(BF16)
        delta_ref[...] = _per_head_rowsum(do * o)

    row = lambda w: pl.BlockSpec((ROW_TILE, w), lambda i: (i, 0))
    full = lambda a: pl.BlockSpec(a.shape, lambda i: (0, 0))
    return pl.pallas_call(
        body, name="merge_bwd", grid=(nb,),
        in_specs=[row(D_MODEL), full(w_out), row(POOL_WIDTH), row(ATTN_WIDTH), full(wup_p), full(wup_a),
                  row(E_COLS), row(ATTN_WIDTH)],
        out_specs=[row(D_MODEL), row(D_MODEL), row(2 * D_MODEL), row(ATTN_WIDTH), row(ATTN_WIDTH),
                   row(ATTN_WIDTH), row(POOL_WIDTH)],
        out_shape=[jax.ShapeDtypeStruct((lp, D_MODEL), BF16), jax.ShapeDtypeStruct((lp, D_MODEL), BF16),
                   jax.ShapeDtypeStruct((lp, 2 * D_MODEL), BF16), jax.ShapeDtypeStruct((lp, ATTN_WIDTH), BF16),
                   jax.ShapeDtypeStruct((lp, ATTN_WIDTH), BF16), jax.ShapeDtypeStruct((lp, ATTN_WIDTH), F32),
                   jax.ShapeDtypeStruct((lp, POOL_WIDTH), F32)],
        compiler_params=_params(("parallel",)),
    )(dh1, w_out, y_pool, y_attn, wup_p, wup_a, e, o)


def _pool_bwd_local(e, dy_pool, pw, scale):
    lp = e.shape[0]
    nb = lp // ROW_TILE
    ng = len(POOL_WINDOWS)

    def body(uc_ref, up_ref, z_ref, dy_ref, pw_ref, sc_ref, dpc_ref, dz_ref, dsc_ref, dpw_ref):
        i = pl.program_id(0)

        @pl.when(i == 0)
        def _():
            dsc_ref[...] = jnp.zeros_like(dsc_ref)
            dpw_ref[...] = jnp.zeros_like(dpw_ref)

        u_cur = uc_ref[...].astype(F32)
        u_prev = jnp.where(i == 0, 0.0, up_ref[...].astype(F32))
        ps, invs = _pool_p(u_cur, u_prev, i)
        z = z_ref[...].astype(F32)
        sz = _sigmoid(z)
        dy = dy_ref[...]
        dypre = dy * (z * sz)
        dsilu = sz * (1.0 + z * (1.0 - sz))
        for g in range(ng):
            sl = slice(g * POOL_GROUP, (g + 1) * POOL_GROUP)
            pb = ps[g].astype(BF16)
            w = pw_ref[g]
            yraw = jnp.dot(pb, w, preferred_element_type=F32)
            sc = sc_ref[:, sl]
            dz_ref[:, sl] = (dy[:, sl] * (yraw * sc) * dsilu[:, sl]).astype(BF16)
            dsc_ref[:, sl] += jnp.sum(dypre[:, sl] * yraw, axis=0, keepdims=True)
            dyraw = (dypre[:, sl] * sc).astype(BF16)
            dpw_ref[g] += _dot_tn(pb, dyraw)
            dpc_ref[:, sl] = _dot_nt(dyraw, w) * invs[g]

    blk = (ROW_TILE, POOL_WIDTH)
    return pl.pallas_call(
        body, name="pool_bwd_local", grid=(nb,),
        in_specs=[pl.BlockSpec(blk, lambda i: (i, 0)), pl.BlockSpec(blk, lambda i: (jnp.maximum(i - 1, 0), 0)),
                  pl.BlockSpec(blk, lambda i: (i, 1)), pl.BlockSpec(blk, lambda i: (i, 0)),
                  pl.BlockSpec((ng, POOL_GROUP, POOL_GROUP), lambda i: (0, 0, 0)),
                  pl.BlockSpec((1, POOL_WIDTH), lambda i: (0, 0))],
        out_specs=[pl.BlockSpec(blk, lambda i: (i, 0)), pl.BlockSpec(blk, lambda i: (i, 0)),
                   pl.BlockSpec((1, POOL_WIDTH), lambda i: (0, 0)),
                   pl.BlockSpec((ng, POOL_GROUP, POOL_GROUP), lambda i: (0, 0, 0))],
        out_shape=[jax.ShapeDtypeStruct((lp, POOL_WIDTH), F32), jax.ShapeDtypeStruct((lp, POOL_WIDTH), BF16),
                   jax.ShapeDtypeStruct((1, POOL_WIDTH), F32),
                   jax.ShapeDtypeStruct((ng, POOL_GROUP, POOL_GROUP), F32)],
        compiler_params=_params(("arbitrary",)),
    )(e, e, e, dy_pool, pw, scale)


def _pool_bwd_window(dpc):
    lp = dpc.shape[0]
    nb = lp // ROW_TILE

    def body(cur_ref, nxt_ref, du_ref):
        i = pl.program_id(0)
        cur = cur_ref[...]
        nxt = jnp.where(i == nb - 1, 0.0, nxt_ref[...])
        pos = _pool_counts(i)
        for g, w in enumerate(POOL_WINDOWS):
            sl = slice(g * POOL_GROUP, (g + 1) * POOL_GROUP)
            xc = jnp.concatenate([cur[:, sl], nxt[:, sl]], axis=0)
            win = _leading_sums(xc, g + 1)[:ROW_TILE, :]
            dp = cur[:, sl] * jnp.minimum(pos + 1, w).astype(F32)
            du_ref[:, sl] = (win - dp).astype(BF16)

    blk = (ROW_TILE, POOL_WIDTH)
    return pl.pallas_call(
        body, name="pool_bwd_window", grid=(nb,),
        in_specs=[pl.BlockSpec(blk, lambda i: (i, 0)), pl.BlockSpec(blk, lambda i: (jnp.minimum(i + 1, nb - 1), 0))],
        out_specs=pl.BlockSpec(blk, lambda i: (i, 0)),
        out_shape=jax.ShapeDtypeStruct((lp, POOL_WIDTH), BF16),
        compiler_params=_params(("parallel",)),
    )(dpc, dpc)


def _attn_bwd(qkv, do, lse, delta, bias, nb):
    lk = qkv.shape[0]
    lp = nb * ROW_TILE
    nkb = lk // KV_TILE_BWD
    n_pairs = N_HEADS // 2
    per_kv = KV_TILE_BWD // ROW_TILE

    def body(q_ref, k_ref, v_ref, do_ref, lse_ref, dl_ref, b_ref, dq_ref, dk_ref, dv_ref, dc_ref, dcq_ref,
             dk_acc, dv_acc, dc_acc):
        jj = pl.program_id(1)

        @pl.when(jj == 0)
        def _():
            dq_ref[...] = jnp.zeros_like(dq_ref)
            dcq_ref[...] = jnp.zeros_like(dcq_ref)

        dk_acc[...] = jnp.zeros_like(dk_acc)
        dv_acc[...] = jnp.zeros_like(dv_acc)
        dc_acc[...] = jnp.zeros_like(dc_acc)

        def block(i, n_keys, masked):
            kb, vb = k_ref[:n_keys, :], v_ref[:n_keys, :]
            rows = pl.ds(pl.multiple_of(i * ROW_TILE, ROW_TILE), ROW_TILE)
            qs = _stack_heads(q_ref[rows, :])
            dos = _stack_heads(do_ref[rows, :])
            lse_i, dl_i = lse_ref[rows, :], dl_ref[rows, :]
            s = _dot_nt(qs, kb)
            dp = _dot_nt(dos, vb)
            if masked:
                valid = _causal(i, jj, 1, KV_TILE_BWD)[:, :n_keys]
            ps, dss, dcs, rowsums = [], [], [], []
            for hd in range(2):
                half = slice(hd * ROW_TILE, (hd + 1) * ROW_TILE)
                col = slice(hd * HEAD_DIM, hd * HEAD_DIM + 1)
                sh = s[half] - b_ref[i, 0, 0, hd:hd + 1, :n_keys]
                if masked:
                    sh = jnp.where(valid, sh, NEG)
                p = jnp.exp(sh - lse_i[:, col])
                ds = p * (dp[half] - dl_i[:, col])
                ps.append(p.astype(BF16))
                dss.append(ds.astype(BF16))
                dcs.append(jnp.sum(ds, axis=0, keepdims=True))
                rowsums.append(jnp.sum(ds, axis=1, keepdims=True))
            dsb = jnp.concatenate(dss, axis=0)
            dv_acc[:n_keys, :] += _dot_tn(jnp.concatenate(ps, axis=0), dos)
            dk_acc[:n_keys, :] += _dot_tn(dsb, qs)
            dc_acc[:, :n_keys] -= jnp.concatenate(dcs, axis=0)
            dq_ref[rows, :] += _unstack_heads(jnp.dot(dsb, kb, preferred_element_type=F32))
            dcq_ref[rows, :] += _unstack_heads(jnp.broadcast_to(jnp.concatenate(rowsums, axis=0), (2 * ROW_TILE, LANES)))

        first_q = per_kv * jj
        for r in range(per_kv):
            @pl.when(first_q + r < nb)
            def _():
                block(first_q + r, (r + 1) * ROW_TILE, True)

        def rest(i, carry):
            block(i, KV_TILE_BWD, False)
            return carry

        lax.fori_loop(jnp.minimum(first_q + per_kv, nb), nb, rest, 0)
        dk_ref[...] = dk_acc[...].astype(BF16)
        dv_ref[...] = dv_acc[...].astype(BF16)
        dc_ref[0, 0] = dc_acc[...]

    whole = lambda rows: pl.BlockSpec((rows, LANES), lambda hp, jj: (0, hp))
    kv_blk = lambda off: pl.BlockSpec((KV_TILE_BWD, LANES), lambda hp, jj: (jj, off + hp))
    return pl.pallas_call(
        body, name="attn_bwd", grid=(n_pairs, nkb),
        in_specs=[whole(lk), kv_blk(n_pairs), kv_blk(2 * n_pairs), whole(lp), whole(lp), whole(lp),
                  pl.BlockSpec((nb, 1, 1, 2, KV_TILE_BWD), lambda hp, jj: (0, hp, jj, 0, 0))],
        out_specs=[whole(lp), kv_blk(0), kv_blk(0),
                   pl.BlockSpec((1, 1, 2, KV_TILE_BWD), lambda hp, jj: (hp, jj, 0, 0)), whole(lp)],
        out_shape=[jax.ShapeDtypeStruct((lp, ATTN_WIDTH), F32), jax.ShapeDtypeStruct((lk, ATTN_WIDTH), BF16),
                   jax.ShapeDtypeStruct((lk, ATTN_WIDTH), BF16),
                   jax.ShapeDtypeStruct((n_pairs, nkb, 2, KV_TILE_BWD), F32),
                   jax.ShapeDtypeStruct((lp, ATTN_WIDTH), F32)],
        scratch_shapes=[pltpu.VMEM((KV_TILE_BWD, LANES), F32), pltpu.VMEM((KV_TILE_BWD, LANES), F32),
                        pltpu.VMEM((2, KV_TILE_BWD), F32)],
        compiler_params=_params(("parallel", "arbitrary")),
    )(qkv, qkv, qkv, do, lse, delta, bias)


def _gates_bwd(dc, dcq, f, bfg):
    nb = f.shape[0] // ROW_TILE

    def body(dc_ref, dcq_ref, f_ref, b_ref, df_ref, db_ref, carry):
        step = pl.program_id(0)
        i = nb - 1 - step

        @pl.when(step == 0)
        def _():
            carry[...] = jnp.zeros_like(carry)
            db_ref[...] = jnp.zeros_like(db_ref)

        dcb = dc_ref[...]
        lane = lax.broadcasted_iota(jnp.int32, (ROW_TILE, F_COLS), 1)
        for h in range(N_HEADS):
            dcb = dcb + jnp.where(lane == h, dcq_ref[:, HEAD_DIM * h:HEAD_DIM * h + 1], 0.0)
        r_i = lax.broadcasted_iota(jnp.int32, (ROW_TILE, ROW_TILE), 0)
        c_i = lax.broadcasted_iota(jnp.int32, (ROW_TILE, ROW_TILE), 1)
        upper = (c_i >= r_i).astype(F32)
        dlf = jnp.dot(upper, dcb, precision=HIGHEST, preferred_element_type=F32) + carry[...]
        carry[...] = carry[...] + jnp.sum(dcb, axis=0, keepdims=True)
        logit = f_ref[...] + b_ref[...]
        dlogit = jnp.where(_valid_gate_mask(i), dlf * _sigmoid(-logit), 0.0)
        df_ref[...] = dlogit.astype(BF16)
        db_ref[...] += jnp.sum(dlogit, axis=0, keepdims=True)

    blk = pl.BlockSpec((ROW_TILE, F_COLS), lambda s: (nb - 1 - s, 0))
    wide = pl.BlockSpec((ROW_TILE, ATTN_WIDTH), lambda s: (nb - 1 - s, 0))
    one = pl.BlockSpec((1, F_COLS), lambda s: (0, 0))
    return pl.pallas_call(
        body, name="gates_bwd", grid=(nb,),
        in_specs=[blk, wide, blk, one], out_specs=[blk, one],
        out_shape=[jax.ShapeDtypeStruct(f.shape, BF16), jax.ShapeDtypeStruct((1, F_COLS), F32)],
        scratch_shapes=[pltpu.VMEM((1, F_COLS), F32)],
        compiler_params=_params(("arbitrary",)),
    )(dc, dcq, f, bfg)


def _input_bwd(dproj, df, wt_e, wt_qkv, wt_f, x2, metapad, dh1, g1, hs, sc):
    nb = x2.shape[0] // ROW_TILE + 1
    n = len(hs)

    def body(dp_ref, df_ref, we_ref, w_ref, wf_ref, x_ref, mp_ref, dh_ref, g_ref, *rest):
        h_refs, s_ref = rest[:n], rest[n]
        gx_ref, g0_ref, dg_ref = rest[n + 1:n + 4]
        q_refs, sq_ref = rest[n + 4:2 * n + 4], rest[2 * n + 4]
        sems = rest[2 * n + 5:]
        i = pl.program_id(0)

        @pl.when(i == 0)
        def _():
            dg_ref[...] = jnp.zeros_like(dg_ref)
            own, sends, _ = _chip_scatter_plan(h_refs, s_ref, q_refs, sq_ref, *sems)
            for cp in own + sends:
                cp.start()

        dhn = (jnp.dot(dp_ref[:, :E_COLS], we_ref[...], preferred_element_type=F32)
               + jnp.dot(dp_ref[:, E_COLS:], w_ref[...], preferred_element_type=F32)
               + jnp.dot(df_ref[...], wf_ref[...], preferred_element_type=F32))
        h0 = jnp.where(i == 0, mp_ref[...], x_ref[...])
        r = lax.rsqrt(jnp.mean(h0 * h0, axis=-1, keepdims=True) + RMS_EPS)
        xhat = h0 * r
        dg_ref[...] += jnp.sum(dhn * xhat, axis=0, keepdims=True)
        dxhat = dhn * g_ref[...]
        dh0 = dh_ref[...] + r * (dxhat - xhat * jnp.mean(dxhat * xhat, axis=-1, keepdims=True))
        gx_ref[...] = dh0

        @pl.when(i == 0)
        def _():
            g0_ref[...] = dh0

        @pl.when(i == nb - 1)
        def _():
            own, sends, recvs = _chip_scatter_plan(h_refs, s_ref, q_refs, sq_ref, *sems)
            for cp in recvs:
                cp.wait_recv()
            for cp in sends:
                cp.wait_send()
            for cp in own:
                cp.wait()

    const = lambda shape: pl.BlockSpec(shape, lambda i: (0, 0))
    res = pl.pallas_call(
        body, name="input_bwd", grid=(nb,),
        in_specs=[pl.BlockSpec((ROW_TILE, MAIN_COLS), lambda i: (i, 0)), pl.BlockSpec((ROW_TILE, F_COLS), lambda i: (i, 0)),
                  const((E_COLS, D_MODEL)), const((QKV_COLS, D_MODEL)), const((F_COLS, D_MODEL)),
                  _tokens_spec(), const((ROW_TILE, D_MODEL)),
                  pl.BlockSpec((ROW_TILE, D_MODEL), lambda i: (i, 0)), const((1, D_MODEL))] + [HBM] * (n + 1),
        out_specs=[_tokens_spec(), const((ROW_TILE, D_MODEL)), const((1, D_MODEL))] + [HBM] * (n + 1),
        out_shape=[jax.ShapeDtypeStruct(x2.shape, F32), jax.ShapeDtypeStruct((ROW_TILE, D_MODEL), F32),
                   jax.ShapeDtypeStruct((1, D_MODEL), F32)]
        + [jax.ShapeDtypeStruct(h.shape, h.dtype) for h in hs] + [jax.ShapeDtypeStruct((N_CHIPS,) + sc.shape, sc.dtype)],
        scratch_shapes=[pltpu.SemaphoreType.DMA((3 * (n + 1),)), pltpu.SemaphoreType.DMA((3 * (n + 1),)),
                        pltpu.SemaphoreType.DMA((n + 1,))],
        compiler_params=_params(("arbitrary",), vmem=56 * 1024 * 1024),
    )(dproj, df, wt_e, wt_qkv, wt_f, x2, metapad, dh1, g1, *_in_hbm(*hs, sc))
    return res[:3], res[3:3 + n], res[3 + n]


def _adamw(w, g, m, v, name):
    rows, cols = w.shape
    if rows % 8 == 0:
        tr, tc = _row_tile8(rows), cols
    else:
        tr, tc = rows, (2 * LANES if cols % (2 * LANES) == 0 and rows > 8 else cols)

    def body(w_ref, g_ref, m_ref, v_ref, d_ref, mo_ref, vo_ref):
        g_ = g_ref[...]
        m_new = ADAM_B1 * m_ref[...] + (1.0 - ADAM_B1) * g_
        v_new = ADAM_B2 * v_ref[...] + (1.0 - ADAM_B2) * (g_ * g_)
        m_hat = m_new / (1.0 - ADAM_B1 ** ADAM_STEP)
        v_hat = v_new / (1.0 - ADAM_B2 ** ADAM_STEP)
        d_ref[...] = -ADAM_LR * (m_hat / (jnp.sqrt(v_hat) + ADAM_EPS) + ADAM_WD * w_ref[...])
        mo_ref[...] = m_new
        vo_ref[...] = v_new

    blk = pl.BlockSpec((tr, tc), lambda i, j: (i, j))
    return pl.pallas_call(
        body, name=name, grid=(rows // tr, cols // tc),
        in_specs=[blk] * 4, out_specs=[blk] * 3,
        out_shape=[jax.ShapeDtypeStruct(w.shape, F32)] * 3,
        compiler_params=_params(("parallel", "parallel")),
    )(w, g, m, v)


def _adamw_native(w3, g3, m3, v3, name):
    rows = w3.shape[0]
    tr = rows // 2
    blk = pl.BlockSpec((tr,) + w3.shape[1:], lambda i: (i, 0, 0))
    shape = jax.ShapeDtypeStruct(w3.shape, F32)

    def moments(g_ref, m_ref, v_ref, mo_ref, vo_ref):
        g_ = g_ref[...]
        mo_ref[...] = ADAM_B1 * m_ref[...] + (1.0 - ADAM_B1) * g_
        vo_ref[...] = ADAM_B2 * v_ref[...] + (1.0 - ADAM_B2) * (g_ * g_)

    new_m, new_v = pl.pallas_call(
        moments, name=name + "_moments", grid=(2,), in_specs=[blk] * 3, out_specs=[blk] * 2, out_shape=[shape] * 2,
        compiler_params=_params(("parallel",)),
    )(g3, m3, v3)

    def delta(w_ref, m_ref, v_ref, d_ref):
        m_hat = m_ref[...] / (1.0 - ADAM_B1 ** ADAM_STEP)
        v_hat = v_ref[...] / (1.0 - ADAM_B2 ** ADAM_STEP)
        d_ref[...] = -ADAM_LR * (m_hat / (jnp.sqrt(v_hat) + ADAM_EPS) + ADAM_WD * w_ref[...])

    d = pl.pallas_call(
        delta, name=name + "_delta", grid=(2,), in_specs=[blk] * 3, out_specs=blk, out_shape=shape,
        compiler_params=_params(("parallel",)),
    )(w3, new_m, new_v)
    return d, new_m, new_v


def _row_tile8(rows):
    best = rows
    for t in range(8, 257, 8):
        if rows % t == 0:
            best = t
    return best


def kernel(x, meta_tokens, norm_g, w_in, b_forget, pool_w, pool_scale, w_up_pool, w_up_attn, w_out, final_norm_g, loss_target, m_meta_tokens, m_norm_g, m_w_in, m_b_forget, m_pool_w, m_pool_scale, m_w_up_pool, m_w_up_attn, m_w_out, m_final_norm_g, v_meta_tokens, v_norm_g, v_w_in, v_b_forget, v_pool_w, v_pool_scale, v_w_up_pool, v_w_up_attn, v_w_out, v_final_norm_g):
    seq = x.shape[1]
    assert seq % ROW_TILE == 0 and x.shape[0] == 1
    lp = seq + ROW_TILE
    nb = lp // ROW_TILE
    lk = -(-lp // KV_TILE_BWD) * KV_TILE_BWD
    core = jnp.reshape(lax.axis_index("c"), (1,)).astype(jnp.int32)
    x2 = x[0]
    target = loss_target[0]
    sh_d = D_MODEL // N_CHIPS

    to_rows = lambda a: jnp.transpose(a, (2, 0, 1))
    from_rows = lambda a: jnp.transpose(a, (1, 2, 0))
    gf = final_norm_g.reshape(1, D_MODEL)
    bfg = jnp.pad(b_forget, ((0, 0), (0, F_COLS - N_HEADS)))
    pw_b = pool_w[0].astype(BF16)
    hn, (wg_in, meta_g) = _rmsnorm_fwd(x2, norm_g, lk, [jnp.transpose(w_in[0]).astype(BF16), meta_tokens], [1, 0])
    wt = wg_in.reshape(-1, D_MODEL)
    wt_e = jnp.concatenate([wt[REF_U:REF_Q], wt[REF_ZA:REF_F], wt[REF_GP:]], axis=0)
    wt_qkv = wt[REF_Q:REF_ZA]
    wt_f = jnp.pad(wt[REF_F:REF_GP], ((0, F_COLS - N_HEADS), (0, 0)))
    meta_full = jnp.transpose(meta_g, (1, 0, 2)).reshape(N_META, D_MODEL)
    metapad = jnp.pad(meta_full, ((PAD_ROWS, 0), (0, 0)))

    tm = _row_tile(lp, 1100)
    e, (wg_up_p, wg_up_a, wg_out) = _mm_nt(
        hn, wt_e, BF16, "in_proj_gates", lp, E_COLS, tm, 512,
        gather=([w_up_pool[0].astype(BF16), w_up_attn[0].astype(BF16), w_out[0].astype(BF16)], [0, 0, 0]))
    wup_p = jnp.transpose(wg_up_p, (1, 0, 2)).reshape(POOL_WIDTH, D_MODEL)
    wup_a = jnp.transpose(wg_up_a, (1, 0, 2)).reshape(ATTN_WIDTH, D_MODEL)
    wout = wg_out.reshape(D_MODEL, D_MODEL)
    qkv = _mm_nt(hn, wt_qkv, BF16, "in_proj_qkv", lk, QKV_COLS, _row_tile(lk, 1300), 512, scale_first=HEAD_DIM ** -0.5)
    f = _mm_nt(hn, wt_f, F32, "in_proj_forget", lp, F_COLS, tm, F_COLS)
    c = _gates_fwd(f, bfg)
    c_t = jnp.transpose(c[:, :N_HEADS])
    bias = c_t[None, :, :] - jnp.transpose(c_t[:, ::ROW_TILE])[:, :, None]
    bias = jnp.where(jnp.arange(lp) < PAD_ROWS, -NEG, bias)
    bias = jnp.pad(bias, ((0, 0), (0, 0), (0, lk - lp)))
    by_kv = lambda t: jnp.transpose(bias.reshape(nb, N_HEADS // 2, 2, lk // t, t), (0, 1, 3, 2, 4))
    bias, bias_bwd = by_kv(KV_TILE), by_kv(KV_TILE_BWD)
    y_pool = _pool_fwd(e, pw_b, pool_scale)
    o, lse = _attn_fwd(qkv, bias, nb)
    merged, y_attn = _merge_fwd(y_pool, o, e, wup_p, wup_a)
    dh1, loss_part, dgf = _head_fwd_bwd(merged, wout, x2, metapad, gf, target)

    dap, daa, dgate, dza, do, delta, dy_pool = _merge_bwd(dh1, wout, y_pool, y_attn, wup_p, wup_a, e, o)
    dpc, dzp, dscale, dpw = _pool_bwd_local(e, dy_pool, pw_b, pool_scale)
    du = _pool_bwd_window(dpc)
    dq, dk, dv, dc4, dcq = _attn_bwd(qkv, do, lse, delta, bias_bwd, nb)
    dc = jnp.transpose(dc4, (1, 3, 0, 2)).reshape(lk, N_HEADS)[:lp]
    df, db = _gates_bwd(jnp.pad(dc, ((0, 0), (0, F_COLS - N_HEADS))), dcq, f, bfg)
    dproj = jnp.concatenate([du, dzp, dza, dgate, (dq * HEAD_DIM ** -0.5).astype(BF16), dk[:lp], dv[:lp]], axis=1)
    tk = _row_tile(lp, 1100)
    dw_out = _mm_tn(merged, dh1, "grad_w_out", lp, 512, D_MODEL, tk)
    dw_up_p = _mm_tn(y_pool, dap, "grad_w_up_pool", lp, 512, D_MODEL, tk, chunks=N_CHIPS)
    dw_up_a = _mm_tn(y_attn, daa, "grad_w_up_attn", lp, 512, D_MODEL, tk, chunks=N_CHIPS)
    dwt_f = _mm_tn(df, hn, "grad_w_in_forget", lp, F_COLS, D_MODEL, tk)

    def pad8(a):
        return jnp.pad(a, ((0, (-a.shape[0]) % 8), (0, 0)))

    small_parts = [pad8(a) for a in (dgf.reshape(-1, LANES), dscale.reshape(-1, LANES), db, dpw.reshape(-1, LANES),
                                     loss_part)]
    small = jnp.concatenate(small_parts, axis=0)
    soffs = [0]
    for p in small_parts:
        soffs.append(soffs[-1] + p.shape[0])

    gs_rows = [dw_up_p, dw_up_a, dw_out.reshape(N_CHIPS, sh_d, D_MODEL)]
    half = MAIN_COLS // 2
    dwt_a, (*rb_rows, rb_f, sr) = _mm_tn(dproj, hn, "grad_w_in_a", lp, half // 2, D_MODEL, tk, m=half, m_off=0,
                                         swap=(gs_rows + [dwt_f, small], [1, 1, 1, 1, None]))
    dwt_b, (rb_a,) = _mm_tn(dproj, hn, "grad_w_in_b", lp, half // 2, D_MODEL, tk, m=half, m_off=2, swap=([dwt_a], [1]))
    (rb_b,) = _sibling_exchange([dwt_b], [1])
    *hs_rows, sc = _add_sibling_half(gs_rows, rb_rows, [1, 1, 1], small, sr, core)
    h_a, h_b, h_f = _add_halves_2d([dwt_a, dwt_b, dwt_f], [rb_a, rb_b, rb_f], core)
    h_in = jnp.concatenate([h_a[E_U:E_ZA], h_b[D_MODEL:], h_a[E_ZA:E_GP], h_f[:N_HEADS], h_a[E_GP:], h_b[:D_MODEL]], axis=0)
    hs = [h_in.reshape(N_CHIPS, -1, D_MODEL // 2)] + hs_rows
    grad_axes = [2, 1, 1, 1]
    (grad_x, g_block0, dg1), qs, sq = _input_bwd(dproj, df, wt_e, wt_qkv, wt_f, x2, metapad, dh1, norm_g, hs, sc)
    g_w_in_rows, g_w_up_p, g_w_up_a, g_w_out, st = _reduce_allgather(qs, grad_axes, sq, [True, False, False, False])
    late = _small_allreduce(jnp.concatenate([pad8(dg1.reshape(-1, LANES)), g_block0[PAD_ROWS:].reshape(-1, LANES)], axis=0))
    n_norm = D_MODEL // LANES
    g_norm = late[:n_norm].reshape(1, D_MODEL)
    chip = 2 * lax.axis_index("x") + lax.axis_index("y")
    g_meta = lax.dynamic_slice_in_dim(late[n_norm:].reshape(N_META, D_MODEL), chip * sh_d, sh_d, axis=1)

    spiece = lambda k, rows: st[soffs[k]:soffs[k] + rows]
    g_final = spiece(0, D_MODEL // LANES).reshape(1, D_MODEL)
    g_scale = spiece(1, POOL_WIDTH // LANES).reshape(1, POOL_WIDTH)
    g_bf = spiece(2, 1)
    g_pw = spiece(3, POOL_WIDTH)
    loss = st[soffs[4], 0]

    def pad_lanes(a):
        return jnp.pad(a, ((0, 0), (0, F_COLS - N_HEADS)))

    w_in_res = (g_w_in_rows,) + _adamw_native(to_rows(w_in), g_w_in_rows, to_rows(m_w_in), to_rows(v_w_in), "adamw_w_in")

    upd = [
        ("meta_tokens", meta_tokens, g_meta, m_meta_tokens, v_meta_tokens),
        ("norm_g", norm_g, g_norm, m_norm_g, v_norm_g),
        ("w_in", None, None, None, None),
        ("b_forget", pad_lanes(b_forget), g_bf, pad_lanes(m_b_forget), pad_lanes(v_b_forget)),
        ("pool_w", pool_w.reshape(-1, LANES), g_pw, m_pool_w.reshape(-1, LANES), v_pool_w.reshape(-1, LANES)),
        ("pool_scale", pool_scale, g_scale, m_pool_scale, v_pool_scale),
        ("w_up_pool", w_up_pool[0], g_w_up_p, m_w_up_pool[0], v_w_up_pool[0]),
        ("w_up_attn", w_up_attn[0], g_w_up_a, m_w_up_attn[0], v_w_up_attn[0]),
        ("w_out", w_out[0], g_w_out, m_w_out[0], v_w_out[0]),
        ("final_norm_g", gf, g_final, m_final_norm_g.reshape(1, D_MODEL), v_final_norm_g.reshape(1, D_MODEL)),
    ]
    shapes = [meta_tokens.shape, norm_g.shape, w_in.shape, b_forget.shape, pool_w.shape, pool_scale.shape,
              w_up_pool.shape, w_up_attn.shape, w_out.shape, final_norm_g.shape]
    grads, deltas, new_ms, new_vs = [], [], [], []
    for (name, w_, g_, m_in, v_in), shp in zip(upd, shapes):
        if name == "w_in":
            res = tuple(from_rows(a) for a in w_in_res)
        else:
            d_, mn_, vn_ = _adamw(w_, g_, m_in, v_in, "adamw_" + name)
            res = (g_, d_, mn_, vn_)
        if name == "b_forget":
            res = tuple(a[:, :N_HEADS] for a in res)
        for lst, a in zip((grads, deltas, new_ms, new_vs), res):
            lst.append(a.reshape(shp))

    return (loss, grad_x.reshape(x.shape), *grads, *deltas, *new_ms, *new_vs)
```

```python
import jax
import jax.numpy as jnp
from jax import lax
from jax.experimental import pallas as pl
from jax.experimental.pallas import tpu as pltpu

F32 = jnp.float32
BF16 = jnp.bfloat16
MESH = pl.DeviceIdType.MESH
HIGHEST = lax.Precision.HIGHEST
HBM = pl.BlockSpec(memory_space=pltpu.HBM)

D_MODEL = 1024
N_META = 16
POOL_WIDTH = 512
POOL_GROUP = 128
POOL_WINDOWS = (2, 4, 8, 16)
N_HEADS = 8
HEAD_DIM = 64
ATTN_WIDTH = 512
RMS_EPS = 1e-6
N_CHIPS = 4

ADAM_LR = 0.001
ADAM_B1 = 0.9
ADAM_B2 = 0.999
ADAM_EPS = 1e-08
ADAM_WD = 0.01
ADAM_STEP = 10

LANES = 128
ROW_TILE = 256
KV_TILE = 1024
KV_TILE_BWD = 1024
PAD_ROWS = ROW_TILE - N_META
NEG = -1e30

E_U, E_ZP, E_ZA, E_GP, E_GA, E_COLS = 0, 512, 1024, 1536, 2560, 3584
QKV_COLS = 3 * ATTN_WIDTH
MAIN_COLS = E_COLS + QKV_COLS
F_COLS = LANES
REF_U, REF_Q, REF_ZA, REF_F, REF_GP = 0, 1024, 2560, 3072, 3080
VMEM_LIMIT = 48 * 1024 * 1024


def _params(sem=None, vmem=VMEM_LIMIT):
    return pltpu.CompilerParams(dimension_semantics=sem, vmem_limit_bytes=vmem)


def _in_hbm(*arrays):
    return [pltpu.with_memory_space_constraint(a, pltpu.HBM) for a in arrays]


def _row_tile(n, target):
    best = 16
    for t in range(16, target + 1, 16):
        if n % t == 0:
            best = t
    return best


def _sigmoid(x):
    return 1.0 / (1.0 + jnp.exp(-x))


def _half(ref, axis, which):
    n = ref.shape[axis] // 2
    idx = [slice(None)] * len(ref.shape)
    idx[axis] = pl.ds(pl.multiple_of(which * n, n), n)
    return ref.at[tuple(idx)]


def _half_shape(shape, axis):
    s = list(shape)
    s[axis] //= 2
    return tuple(s)


def _gather_start(ins, outs, axes, send, recv, fsend, frecv, local):
    x, y, c = lax.axis_index("x"), lax.axis_index("y"), lax.axis_index("c")
    me = 2 * x + y
    for t in range(len(ins)):
        pltpu.make_async_copy(ins[t], outs[t].at[me], local.at[t]).start()
        for k, chip in enumerate([(1 - x, y), (x, 1 - y), (1 - x, 1 - y)]):
            pltpu.make_async_remote_copy(
                src_ref=_half(ins[t], axes[t], c), dst_ref=_half(outs[t].at[me], axes[t], c),
                send_sem=send.at[3 * t + k], recv_sem=recv.at[3 * t + k], device_id=(*chip, c), device_id_type=MESH).start()


def _gather_finish(ins, outs, axes, send, recv, fsend, frecv, local):
    x, y, c = lax.axis_index("x"), lax.axis_index("y"), lax.axis_index("c")
    me = 2 * x + y
    sibling = (x, y, 1 - c)
    chips = [(1 - x, y), (x, 1 - y), (1 - x, 1 - y)]
    n = len(ins)

    def over_ici(t, k, chip, dst_slot):
        return pltpu.make_async_remote_copy(
            src_ref=_half(ins[t], axes[t], c), dst_ref=_half(outs[t].at[dst_slot], axes[t], c),
            send_sem=send.at[3 * t + k], recv_sem=recv.at[3 * t + k], device_id=(*chip, c), device_id_type=MESH)

    def to_sibling(t, k, slot, which):
        return pltpu.make_async_remote_copy(
            src_ref=_half(outs[t].at[slot], axes[t], which), dst_ref=_half(outs[t].at[slot], axes[t], which),
            send_sem=fsend.at[3 * t + k], recv_sem=frecv.at[3 * t + k], device_id=sibling, device_id_type=MESH)

    forwards = []
    for t in range(n):
        for k, (px, py) in enumerate(chips):
            over_ici(t, k, (px, py), 2 * px + py).wait_recv()
            fw = to_sibling(t, k, 2 * px + py, c)
            fw.start()
            forwards.append(fw)
    for t in range(n):
        for k, (px, py) in enumerate(chips):
            to_sibling(t, k, 2 * px + py, 1 - c).wait_recv()
    for t in range(n):
        for k, chip in enumerate(chips):
            over_ici(t, k, chip, me).wait_send()
    for fw in forwards:
        fw.wait_send()
    for t in range(n):
        pltpu.make_async_copy(ins[t], outs[t].at[me], local.at[t]).wait()


def _gather_sems(n):
    return [pltpu.SemaphoreType.DMA((3 * n,)), pltpu.SemaphoreType.DMA((3 * n,)), pltpu.SemaphoreType.DMA((3 * n,)),
            pltpu.SemaphoreType.DMA((3 * n,)), pltpu.SemaphoreType.DMA((n,))]


def _gathered_shapes(shards):
    return [jax.ShapeDtypeStruct((N_CHIPS,) + s.shape, s.dtype) for s in shards]


def _swap_copies(srcs, dsts, axes, send, recv):
    x, y, c = lax.axis_index("x"), lax.axis_index("y"), lax.axis_index("c")
    return [pltpu.make_async_remote_copy(
        src_ref=srcs[t] if axes[t] is None else _half(srcs[t], axes[t], 1 - c), dst_ref=dsts[t],
        send_sem=send.at[t], recv_sem=recv.at[t], device_id=(x, y, 1 - c), device_id_type=MESH) for t in range(len(srcs))]


def _swap_shapes(srcs, axes):
    return [jax.ShapeDtypeStruct(g.shape if a is None else _half_shape(g.shape, a), g.dtype) for g, a in zip(srcs, axes)]


def _sibling_exchange(gs, axes):
    n = len(gs)

    def body(*refs):
        cps = _swap_copies(refs[:n], refs[n:2 * n], axes, *refs[2 * n:])
        for cp in cps:
            cp.start()
        for cp in cps:
            cp.wait()

    return pl.pallas_call(
        body, name="grad_sibling_exchange",
        in_specs=[HBM] * n, out_specs=[HBM] * n, out_shape=_swap_shapes(gs, axes),
        scratch_shapes=[pltpu.SemaphoreType.DMA((n,)), pltpu.SemaphoreType.DMA((n,))],
    )(*_in_hbm(*gs))


def _add_halves_2d(parts, rbs, core):
    n = len(parts)
    rows, w = rbs[0].shape
    tr = 512
    last_rows = rbs[-1].shape[0]

    def body(core_ref, *refs):
        p_refs, r_refs, h_refs = refs[:n], refs[n:2 * n], refs[2 * n:]
        for t in range(n - 1):
            h_refs[t][...] = (p_refs[t][...] + r_refs[t][...]).astype(BF16)

        @pl.when(pl.program_id(0) == 0)
        def _():
            h_refs[n - 1][...] = (p_refs[n - 1][...] + r_refs[n - 1][...]).astype(BF16)

    mine = lambda r: pl.BlockSpec((r, w), lambda i, cr: (i, cr[0]))
    tile = lambda r: pl.BlockSpec((r, w), lambda i, cr: (i, 0))
    mine_small = pl.BlockSpec((last_rows, w), lambda i, cr: (0, cr[0]))
    tile_small = pl.BlockSpec((last_rows, w), lambda i, cr: (0, 0))
    return pl.pallas_call(
        body, name="grad_add_sibling_w_in",
        grid_spec=pltpu.PrefetchScalarGridSpec(
            num_scalar_prefetch=1, grid=(rows // tr,),
            in_specs=[mine(tr)] * (n - 1) + [mine_small] + [tile(tr)] * (n - 1) + [tile_small],
            out_specs=[tile(tr)] * (n - 1) + [tile_small]),
        out_shape=[jax.ShapeDtypeStruct(r.shape, BF16) for r in rbs],
        compiler_params=_params(("arbitrary",)),
    )(core, *parts, *rbs)


def _add_sibling_half(gs, rbs, axes, small, sr, core):
    n = len(gs)

    def body(core_ref, *refs):
        g_refs, rb_refs = refs[:n], refs[n:2 * n]
        s_ref, sr_ref = refs[2 * n], refs[2 * n + 1]
        h_refs, sc_ref = refs[2 * n + 2:3 * n + 2], refs[3 * n + 2]
        for t in range(n):
            h_refs[t][...] = (g_refs[t][...] + rb_refs[t][...]).astype(BF16)
        sc_ref[...] = s_ref[...] + sr_ref[...]

    def mine(rb, axis):
        blk = (None,) + rb.shape[1:]
        if axis == 2:
            return pl.BlockSpec(blk, lambda j, cr: (j, 0, cr[0]))
        return pl.BlockSpec(blk, lambda j, cr: (j, cr[0], 0))

    chunk = lambda rb: pl.BlockSpec((None,) + rb.shape[1:], lambda j, cr: (j, 0, 0))
    whole = pl.BlockSpec(small.shape, lambda j, cr: (0, 0))
    return pl.pallas_call(
        body, name="grad_add_sibling",
        grid_spec=pltpu.PrefetchScalarGridSpec(
            num_scalar_prefetch=1, grid=(N_CHIPS,),
            in_specs=[mine(rb, a) for rb, a in zip(rbs, axes)] + [chunk(rb) for rb in rbs] + [whole, whole],
            out_specs=[chunk(rb) for rb in rbs] + [whole]),
        out_shape=[jax.ShapeDtypeStruct(rb.shape, BF16) for rb in rbs] + [jax.ShapeDtypeStruct(small.shape, F32)],
        compiler_params=_params(("arbitrary",)),
    )(core, *gs, *rbs, small, sr)


def _chip_scatter_plan(h_refs, s_ref, q_refs, sq_ref, send, recv, local):
    n = len(h_refs)
    x, y, c = lax.axis_index("x"), lax.axis_index("y"), lax.axis_index("c")
    me = 2 * x + y
    chips = [(1 - x, y), (x, 1 - y), (1 - x, 1 - y)]

    def copy(t, k, chip, src_slot, dst_slot):
        src = h_refs[t].at[src_slot] if t < n else s_ref
        dst = (q_refs[t] if t < n else sq_ref).at[dst_slot]
        return pltpu.make_async_remote_copy(src_ref=src, dst_ref=dst, send_sem=send.at[3 * t + k],
                                            recv_sem=recv.at[3 * t + k], device_id=(*chip, c), device_id_type=MESH)

    own = [pltpu.make_async_copy(h_refs[t].at[me], q_refs[t].at[me], local.at[t]) for t in range(n)]
    own.append(pltpu.make_async_copy(s_ref, sq_ref.at[me], local.at[n]))
    sends = [copy(t, k, (px, py), 2 * px + py, me) for t in range(n + 1) for k, (px, py) in enumerate(chips)]
    recvs = [copy(t, k, (px, py), me, 2 * px + py) for t in range(n + 1) for k, (px, py) in enumerate(chips)]
    return own, sends, recvs


def _small_allreduce(buf):
    def body(b_ref, o_ref, sib_buf, chip_buf, send, recv):
        x, y, c = lax.axis_index("x"), lax.axis_index("y"), lax.axis_index("c")
        me = 2 * x + y
        chips = [(1 - x, y), (x, 1 - y), (1 - x, 1 - y)]
        swap = pltpu.make_async_remote_copy(src_ref=b_ref, dst_ref=sib_buf, send_sem=send.at[0], recv_sem=recv.at[0],
                                            device_id=(x, y, 1 - c), device_id_type=MESH)
        swap.start()
        swap.wait()
        chip_buf[me] = b_ref[...] + sib_buf[...]

        def copy(k, chip, slot):
            return pltpu.make_async_remote_copy(src_ref=chip_buf.at[slot], dst_ref=chip_buf.at[slot], send_sem=send.at[1 + k],
                                                recv_sem=recv.at[1 + k], device_id=(*chip, c), device_id_type=MESH)

        sends = [copy(k, chip, me) for k, chip in enumerate(chips)]
        for cp in sends:
            cp.start()
        for k, (px, py) in enumerate(chips):
            copy(k, (px, py), 2 * px + py).wait_recv()
        for cp in sends:
            cp.wait_send()
        o_ref[...] = ((chip_buf[0] + chip_buf[1]) + chip_buf[2]) + chip_buf[3]

    vmem = pl.BlockSpec(memory_space=pltpu.VMEM)
    return pl.pallas_call(
        body, name="small_allreduce", in_specs=[vmem], out_specs=vmem,
        out_shape=jax.ShapeDtypeStruct(buf.shape, F32),
        scratch_shapes=[pltpu.VMEM(buf.shape, F32), pltpu.VMEM((N_CHIPS,) + buf.shape, F32),
                        pltpu.SemaphoreType.DMA((4,)), pltpu.SemaphoreType.DMA((4,))],
        compiler_params=_params(),
    )(buf)


def _reduce_allgather(qs, axes, sq, as_rows):
    n = len(qs)
    shard_shapes, half_axes = [], []
    for q, a, rows_form in zip(qs, axes, as_rows):
        shape = [d * 2 if i == a - 1 else d for i, d in enumerate(q.shape[1:])]
        if rows_form:
            assert a == 2
            shape = [shape[0], 1, shape[1]]
        shard_shapes.append(tuple(shape))
        half_axes.append(2 if rows_form else a - 1)

    def body(*refs):
        q_refs, sq_ref = refs[:n], refs[n]
        o_refs, st_ref = refs[n + 1:2 * n + 1], refs[2 * n + 1]
        send, recv = refs[2 * n + 2:]
        x, y, c = lax.axis_index("x"), lax.axis_index("y"), lax.axis_index("c")

        def swap(t, which):
            return pltpu.make_async_remote_copy(
                src_ref=_half(o_refs[t], half_axes[t], which), dst_ref=_half(o_refs[t], half_axes[t], which),
                send_sem=send.at[t], recv_sem=recv.at[t], device_id=(x, y, 1 - c), device_id_type=MESH)

        sent = []
        for t in range(n):
            q = q_refs[t]
            total = ((q[0].astype(F32) + q[1].astype(F32)) + q[2].astype(F32)) + q[3].astype(F32)
            if as_rows[t]:
                total = total.reshape(total.shape[0], 1, total.shape[1])
            _half(o_refs[t], half_axes[t], c)[...] = total
            cp = swap(t, c)
            cp.start()
            sent.append(cp)
        st_ref[...] = ((sq_ref[0] + sq_ref[1]) + sq_ref[2]) + sq_ref[3]
        for t in range(n):
            swap(t, 1 - c).wait_recv()
        for cp in sent:
            cp.wait_send()

    vmem = pl.BlockSpec(memory_space=pltpu.VMEM)
    return pl.pallas_call(
        body, name="grad_reduce_allgather",
        in_specs=[vmem] * (n + 1), out_specs=[vmem] * (n + 1),
        out_shape=[jax.ShapeDtypeStruct(s, F32) for s in shard_shapes] + [jax.ShapeDtypeStruct(sq.shape[1:], F32)],
        scratch_shapes=[pltpu.SemaphoreType.DMA((n,)), pltpu.SemaphoreType.DMA((n,))],
        compiler_params=_params(),
    )(*qs, sq)


def _dot_nt(a, b):
    return lax.dot_general(a, b, (((1,), (1,)), ((), ())), preferred_element_type=F32)


def _dot_tn(a, b):
    return lax.dot_general(a, b, (((0,), (0,)), ((), ())), preferred_element_type=F32)


def _mm_nt(a, bt, out_dtype, name, m, n, tm, tn, row_block=0, scale_first=None, gather=None):
    k = a.shape[1]
    shards, axes = gather if gather is not None else ([], [])
    ng = len(shards)
    grid = (m // tm, n // tn)

    def body(a_ref, b_ref, *rest):
        ins, o_ref, outs, sems = rest[:ng], rest[ng], rest[ng + 1:2 * ng + 1], rest[2 * ng + 1:]
        first = (pl.program_id(0) == 0) & (pl.program_id(1) == 0)
        last = (pl.program_id(0) == grid[0] - 1) & (pl.program_id(1) == grid[1] - 1)
        if ng:
            @pl.when(first)
            def _():
                _gather_start(ins, outs, axes, *sems)

        r = _dot_nt(a_ref[...], b_ref[...])
        if scale_first is not None:
            r = r * jnp.where(pl.program_id(1) == 0, scale_first, 1.0)
        o_ref[...] = r.astype(out_dtype)

        if ng:
            @pl.when(last)
            def _():
                _gather_finish(ins, outs, axes, *sems)

    res = pl.pallas_call(
        body, name=name, grid=grid,
        in_specs=[pl.BlockSpec((tm, k), lambda i, j: (i, 0)), pl.BlockSpec((tn, k), lambda i, j: (row_block + j, 0))]
        + [HBM] * ng,
        out_specs=[pl.BlockSpec((tm, tn), lambda i, j: (i, j))] + [HBM] * ng,
        out_shape=[jax.ShapeDtypeStruct((m, n), out_dtype)] + _gathered_shapes(shards),
        scratch_shapes=_gather_sems(ng) if ng else [],
        compiler_params=_params(("arbitrary", "arbitrary") if ng else ("parallel", "parallel")),
    )(a, bt, *_in_hbm(*shards))
    return (res[0], res[1:]) if ng else res[0]


def _mm_tn(a, b, name, k, tm, tn, tk, chunks=1, m=None, m_off=0, swap=None):
    m = a.shape[1] if m is None else m
    n = b.shape[1]
    cw = n // chunks
    srcs, axes = swap if swap is not None else ([], [])
    ns = len(srcs)
    grid = (m // tm, n // tn, k // tk)

    def body(a_ref, b_ref, *rest):
        s_refs, o_ref, d_refs, sems = rest[:ns], rest[ns], rest[ns + 1:2 * ns + 1], rest[2 * ns + 1:]
        ids = [pl.program_id(d) for d in range(3)]
        if ns:
            @pl.when((ids[0] == 0) & (ids[1] == 0) & (ids[2] == 0))
            def _():
                for cp in _swap_copies(s_refs, d_refs, axes, *sems):
                    cp.start()

        @pl.when(ids[2] == 0)
        def _():
            o_ref[...] = jnp.zeros_like(o_ref)
        r = _dot_tn(a_ref[...].astype(BF16), b_ref[...].astype(BF16))
        if chunks > 1:
            for c in range(chunks):
                o_ref[c] += r[:, c * cw:(c + 1) * cw]
        else:
            o_ref[...] += r

        if ns:
            @pl.when((ids[0] == grid[0] - 1) & (ids[1] == grid[1] - 1) & (ids[2] == grid[2] - 1))
            def _():
                for cp in _swap_copies(s_refs, d_refs, axes, *sems):
                    cp.wait()

    if chunks > 1:
        assert tn == n
        out_spec = pl.BlockSpec((chunks, tm, cw), lambda i, j, kk: (0, i, 0))
        out_shape = jax.ShapeDtypeStruct((chunks, m, cw), F32)
    else:
        out_spec = pl.BlockSpec((tm, tn), lambda i, j, kk: (i, j))
        out_shape = jax.ShapeDtypeStruct((m, n), F32)
    res = pl.pallas_call(
        body, name=name, grid=grid,
        in_specs=[pl.BlockSpec((tk, tm), lambda i, j, kk: (kk, m_off + i)), pl.BlockSpec((tk, tn), lambda i, j, kk: (kk, j))]
        + [HBM] * ns,
        out_specs=[out_spec] + [HBM] * ns, out_shape=[out_shape] + _swap_shapes(srcs, axes),
        scratch_shapes=[pltpu.SemaphoreType.DMA((ns,)), pltpu.SemaphoreType.DMA((ns,))] if ns else [],
        compiler_params=_params(("arbitrary",) * 3 if ns else ("parallel", "parallel", "arbitrary")),
    )(a, b, *_in_hbm(*srcs))
    return (res[0], res[1:]) if ns else res[0]


def _tokens_spec():
    return pl.BlockSpec((ROW_TILE, D_MODEL), lambda i: (jnp.maximum(i - 1, 0), 0))


def _rmsnorm_fwd(x2, g1, lk, shards, axes):
    nb = x2.shape[0] // ROW_TILE + 1
    nblk = lk // ROW_TILE
    ng = len(shards)
    meta_cols = shards[-1].shape[1]

    def body(x_ref, g_ref, *rest):
        ins, hn_ref, outs = rest[:ng], rest[ng], rest[ng + 1:2 * ng + 1]
        sems, meta_buf, meta_sem = rest[2 * ng + 1:2 * ng + 6], rest[2 * ng + 6], rest[2 * ng + 7]
        s = pl.program_id(0)
        blk = (s + 1) % nblk

        @pl.when(s == 0)
        def _():
            _gather_start(ins, outs, axes, *sems)

        def normed(h):
            r = lax.rsqrt(jnp.mean(h * h, axis=-1, keepdims=True) + RMS_EPS)
            return ((h * r) * g_ref[...]).astype(BF16)

        @pl.when(s < nblk - 1)
        def _():
            hn_ref[...] = normed(jnp.where(blk >= nb, 0.0, x_ref[...]))

        @pl.when(s == nblk - 1)
        def _():
            _gather_finish(ins, outs, axes, *sems)
            fetch = pltpu.make_async_copy(outs[-1], meta_buf, meta_sem.at[0])
            fetch.start()
            fetch.wait()
            meta = jnp.concatenate([meta_buf[j] for j in range(N_CHIPS)], axis=1)
            hn_ref[...] = normed(jnp.concatenate([jnp.zeros((PAD_ROWS, D_MODEL), F32), meta], axis=0))

    res = pl.pallas_call(
        body, name="rmsnorm_fwd", grid=(nblk,),
        in_specs=[pl.BlockSpec((ROW_TILE, D_MODEL), lambda s: (jnp.clip((s + 1) % nblk - 1, 0, nb - 2), 0)),
                  pl.BlockSpec((1, D_MODEL), lambda s: (0, 0))] + [HBM] * ng,
        out_specs=[pl.BlockSpec((ROW_TILE, D_MODEL), lambda s: ((s + 1) % nblk, 0))] + [HBM] * ng,
        out_shape=[jax.ShapeDtypeStruct((lk, D_MODEL), BF16)] + _gathered_shapes(shards),
        scratch_shapes=_gather_sems(ng) + [pltpu.VMEM((N_CHIPS, N_META, meta_cols), F32), pltpu.SemaphoreType.DMA((1,))],
        compiler_params=_params(("arbitrary",)),
    )(x2, g1, *_in_hbm(*shards))
    return res[0], res[1:]


def _valid_gate_mask(i):
    row = i * ROW_TILE + lax.broadcasted_iota(jnp.int32, (ROW_TILE, F_COLS), 0)
    col = lax.broadcasted_iota(jnp.int32, (ROW_TILE, F_COLS), 1)
    return (row >= PAD_ROWS) & (col < N_HEADS)


def _gates_fwd(f, bfg):
    nb = f.shape[0] // ROW_TILE

    def body(f_ref, b_ref, c_ref, carry):
        i = pl.program_id(0)

        @pl.when(i == 0)
        def _():
            carry[...] = jnp.zeros_like(carry)

        logit = f_ref[...] + b_ref[...]
        lf = jnp.minimum(logit, 0.0) - jnp.log1p(jnp.exp(-jnp.abs(logit)))
        lf = jnp.where(_valid_gate_mask(i), lf, 0.0)
        r_i = lax.broadcasted_iota(jnp.int32, (ROW_TILE, ROW_TILE), 0)
        c_i = lax.broadcasted_iota(jnp.int32, (ROW_TILE, ROW_TILE), 1)
        tri = (c_i <= r_i).astype(F32)
        c_ref[...] = jnp.dot(tri, lf, precision=HIGHEST, preferred_element_type=F32) + carry[...]
        carry[...] = carry[...] + jnp.sum(lf, axis=0, keepdims=True)

    return pl.pallas_call(
        body, name="gates_fwd", grid=(nb,),
        in_specs=[pl.BlockSpec((ROW_TILE, F_COLS), lambda i: (i, 0)), pl.BlockSpec((1, F_COLS), lambda i: (0, 0))],
        out_specs=pl.BlockSpec((ROW_TILE, F_COLS), lambda i: (i, 0)),
        out_shape=jax.ShapeDtypeStruct(f.shape, F32),
        scratch_shapes=[pltpu.VMEM((1, F_COLS), F32)],
        compiler_params=_params(("arbitrary",)),
    )(f, bfg)


def _pool_counts(i):
    row = i * ROW_TILE + lax.broadcasted_iota(jnp.int32, (ROW_TILE, 1), 0)
    return jnp.maximum(row - PAD_ROWS, 0)


def _trailing_sums(xc, levels):
    acc = xc
    for lv in range(levels):
        acc = acc + pltpu.roll(acc, 1 << lv, 0)
    return acc


def _leading_sums(xc, levels):
    n = xc.shape[0]
    acc = xc
    for lv in range(levels):
        acc = acc + pltpu.roll(acc, n - (1 << lv), 0)
    return acc


def _pool_p(u_cur, u_prev, i):
    pos = _pool_counts(i)
    ps, invs = [], []
    for g, w in enumerate(POOL_WINDOWS):
        sl = slice(g * POOL_GROUP, (g + 1) * POOL_GROUP)
        cur = u_cur[:, sl]
        xc = jnp.concatenate([u_prev[:, sl], cur], axis=0)
        win = _trailing_sums(xc, g + 1)[ROW_TILE:, :]
        inv = 1.0 / jnp.minimum(pos + 1, w).astype(F32)
        ps.append(win * inv - cur)
        invs.append(inv)
    return ps, invs


def _pool_fwd(e, pw, scale):
    nb = e.shape[0] // ROW_TILE

    def body(uc_ref, up_ref, z_ref, pw_ref, sc_ref, y_ref):
        i = pl.program_id(0)
        u_cur = uc_ref[...].astype(F32)
        u_prev = jnp.where(i == 0, 0.0, up_ref[...].astype(F32))
        ps, _ = _pool_p(u_cur, u_prev, i)
        z = z_ref[...].astype(F32)
        gate = z * _sigmoid(z)
        for g in range(len(POOL_WINDOWS)):
            sl = slice(g * POOL_GROUP, (g + 1) * POOL_GROUP)
            yraw = jnp.dot(ps[g].astype(BF16), pw_ref[g], preferred_element_type=F32)
            y_ref[:, sl] = ((yraw * sc_ref[:, sl]) * gate[:, sl]).astype(BF16)

    blk = (ROW_TILE, POOL_WIDTH)
    return pl.pallas_call(
        body, name="pool_fwd", grid=(nb,),
        in_specs=[pl.BlockSpec(blk, lambda i: (i, 0)), pl.BlockSpec(blk, lambda i: (jnp.maximum(i - 1, 0), 0)),
                  pl.BlockSpec(blk, lambda i: (i, 1)),
                  pl.BlockSpec((len(POOL_WINDOWS), POOL_GROUP, POOL_GROUP), lambda i: (0, 0, 0)),
                  pl.BlockSpec((1, POOL_WIDTH), lambda i: (0, 0))],
        out_specs=pl.BlockSpec(blk, lambda i: (i, 0)),
        out_shape=jax.ShapeDtypeStruct((e.shape[0], POOL_WIDTH), BF16),
        compiler_params=_params(("parallel",)),
    )(e, e, e, pw, scale)


def _stack_heads(a):
    first = lax.broadcasted_iota(jnp.int32, a.shape, 1) < HEAD_DIM
    zero = jnp.zeros_like(a)
    return jnp.concatenate([jnp.where(first, a, zero), jnp.where(first, zero, a)], axis=0)


def _unstack_heads(a):
    rows = a.shape[0] // 2
    first = lax.broadcasted_iota(jnp.int32, (rows, LANES), 1) < HEAD_DIM
    return jnp.where(first, a[:rows], a[rows:])


def _causal(i, jj, stacked, kv_tile):
    r = lax.broadcasted_iota(jnp.int32, (stacked * ROW_TILE, kv_tile), 0)
    if stacked == 2:
        r = jnp.where(r >= ROW_TILE, r - ROW_TILE, r)
    kidx = jj * kv_tile + lax.broadcasted_iota(jnp.int32, (stacked * ROW_TILE, kv_tile), 1)
    return kidx <= i * ROW_TILE + r


def _stack_rows(b):
    n = b.shape[1]
    return jnp.concatenate([jnp.broadcast_to(b[0:1], (ROW_TILE, n)), jnp.broadcast_to(b[1:2], (ROW_TILE, n))], axis=0)


def _attn_fwd(qkv, bias, nb):
    lk = qkv.shape[0]
    lp = nb * ROW_TILE
    nkb = lk // KV_TILE
    n_pairs = N_HEADS // 2

    def body(q_ref, k_ref, v_ref, b_ref, o_ref, lse_ref):
        i = pl.program_id(1)
        qs = _stack_heads(q_ref[...])
        last = (i * ROW_TILE) // KV_TILE

        def block(jj, carry, masked, n_keys=KV_TILE):
            m, l, acc = carry
            rows = pl.ds(pl.multiple_of(jj * KV_TILE, KV_TILE), n_keys)
            s = _dot_nt(qs, k_ref[rows, :]) - _stack_rows(b_ref[0, 0, jj][:, :n_keys])
            if masked:
                s = jnp.where(_causal(i, jj, 2, KV_TILE)[:, :n_keys], s, NEG)
            m_new = jnp.maximum(m, jnp.max(s, axis=1, keepdims=True))
            alpha = jnp.exp(m - m_new)
            p = jnp.exp(s - m_new)
            l = alpha * l + jnp.sum(p, axis=1, keepdims=True)
            acc = alpha * acc + jnp.dot(p.astype(BF16), v_ref[rows, :], preferred_element_type=F32)
            return m_new, l, acc

        init = (jnp.full((2 * ROW_TILE, 1), NEG, F32), jnp.zeros((2 * ROW_TILE, 1), F32),
                jnp.zeros((2 * ROW_TILE, LANES), F32))
        carry = lax.fori_loop(0, last, lambda jj, c: block(jj, c, False), init)
        ends = [lambda c, n=(r + 1) * ROW_TILE: block(last, c, True, n) for r in range(KV_TILE // ROW_TILE)]
        m, l, acc = lax.switch(i - last * (KV_TILE // ROW_TILE), ends, carry)
        o_ref[...] = _unstack_heads(acc / l)
        lse_ref[...] = _unstack_heads(jnp.broadcast_to(m + jnp.log(l), (2 * ROW_TILE, LANES)))

    return pl.pallas_call(
        body, name="attn_fwd", grid=(n_pairs, nb),
        in_specs=[pl.BlockSpec((ROW_TILE, LANES), lambda hp, i: (i, hp)),
                  pl.BlockSpec((lk, LANES), lambda hp, i: (0, n_pairs + hp)),
                  pl.BlockSpec((lk, LANES), lambda hp, i: (0, 2 * n_pairs + hp)),
                  pl.BlockSpec((1, 1, nkb, 2, KV_TILE), lambda hp, i: (i, hp, 0, 0, 0))],
        out_specs=[pl.BlockSpec((ROW_TILE, LANES), lambda hp, i: (i, hp)),
                   pl.BlockSpec((ROW_TILE, LANES), lambda hp, i: (i, hp))],
        out_shape=[jax.ShapeDtypeStruct((lp, ATTN_WIDTH), F32), jax.ShapeDtypeStruct((lp, ATTN_WIDTH), F32)],
        compiler_params=_params(("parallel", "parallel")),
    )(qkv, qkv, qkv, bias)


def _merge_fwd(y_pool, o, e, wup_p, wup_a):
    lp = o.shape[0]
    nb = lp // ROW_TILE

    def body(yp_ref, o_ref, e_ref, wp_ref, wa_ref, mg_ref, ya_ref):
        za = e_ref[:, E_ZA:E_GP].astype(F32)
        ya = (o_ref[...] * (za * _sigmoid(za))).astype(BF16)
        ya_ref[...] = ya
        a_pool = jnp.dot(yp_ref[...], wp_ref[...], preferred_element_type=F32)
        a_attn = jnp.dot(ya, wa_ref[...], preferred_element_type=F32)
        mg_ref[...] = (_sigmoid(e_ref[:, E_GP:E_GA].astype(F32)) * a_pool
                       + _sigmoid(e_ref[:, E_GA:E_COLS].astype(F32)) * a_attn).astype(BF16)

    return pl.pallas_call(
        body, name="merge_fwd", grid=(nb,),
        in_specs=[pl.BlockSpec((ROW_TILE, POOL_WIDTH), lambda i: (i, 0)),
                  pl.BlockSpec((ROW_TILE, ATTN_WIDTH), lambda i: (i, 0)),
                  pl.BlockSpec((ROW_TILE, E_COLS), lambda i: (i, 0)),
                  pl.BlockSpec((POOL_WIDTH, D_MODEL), lambda i: (0, 0)),
                  pl.BlockSpec((ATTN_WIDTH, D_MODEL), lambda i: (0, 0))],
        out_specs=[pl.BlockSpec((ROW_TILE, D_MODEL), lambda i: (i, 0)),
                   pl.BlockSpec((ROW_TILE, ATTN_WIDTH), lambda i: (i, 0))],
        out_shape=[jax.ShapeDtypeStruct((lp, D_MODEL), BF16), jax.ShapeDtypeStruct((lp, ATTN_WIDTH), BF16)],
        compiler_params=_params(("parallel",)),
    )(y_pool, o, e, wup_p, wup_a)


def _head_fwd_bwd(merged, w_out, x2, metapad, gf, target):
    lp = merged.shape[0]
    nb = lp // ROW_TILE

    def body(mg_ref, w_ref, x_ref, mp_ref, g_ref, t_ref, dh_ref, loss_ref, dg_ref):
        i = pl.program_id(0)

        @pl.when(i == 0)
        def _():
            loss_ref[...] = jnp.zeros_like(loss_ref)
            dg_ref[...] = jnp.zeros_like(dg_ref)

        h0 = jnp.where(i == 0, mp_ref[...], x_ref[...])
        h1 = h0 + jnp.dot(mg_ref[...], w_ref[...], preferred_element_type=F32)
        r = lax.rsqrt(jnp.mean(h1 * h1, axis=-1, keepdims=True) + RMS_EPS)
        xhat = h1 * r
        g = g_ref[...]
        err = jnp.where(i == 0, 0.0, xhat * g - t_ref[...])
        loss_ref[...] += 0.5 * jnp.sum(jnp.mean(err * err, axis=-1, keepdims=True))
        dy = err / D_MODEL
        dg_ref[...] += jnp.sum(dy * xhat, axis=0, keepdims=True)
        dxhat = dy * g
        dh_ref[...] = r * (dxhat - xhat * jnp.mean(dxhat * xhat, axis=-1, keepdims=True))

    return pl.pallas_call(
        body, name="head_fwd_bwd", grid=(nb,),
        in_specs=[pl.BlockSpec((ROW_TILE, D_MODEL), lambda i: (i, 0)),
                  pl.BlockSpec((D_MODEL, D_MODEL), lambda i: (0, 0)),
                  _tokens_spec(), pl.BlockSpec((ROW_TILE, D_MODEL), lambda i: (0, 0)),
                  pl.BlockSpec((1, D_MODEL), lambda i: (0, 0)), _tokens_spec()],
        out_specs=[pl.BlockSpec((ROW_TILE, D_MODEL), lambda i: (i, 0)),
                   pl.BlockSpec((1, LANES), lambda i: (0, 0)), pl.BlockSpec((1, D_MODEL), lambda i: (0, 0))],
        out_shape=[jax.ShapeDtypeStruct((lp, D_MODEL), F32), jax.ShapeDtypeStruct((1, LANES), F32),
                   jax.ShapeDtypeStruct((1, D_MODEL), F32)],
        compiler_params=_params(("arbitrary",)),
    )(merged, w_out, x2, metapad, gf, target)


def _per_head_rowsum(t):
    head = lax.broadcasted_iota(jnp.int32, t.shape, 1) // HEAD_DIM
    out = jnp.zeros_like(t)
    for h in range(N_HEADS):
        sel = head == h
        out = jnp.where(sel, jnp.sum(jnp.where(sel, t, 0.0), axis=1, keepdims=True), out)
    return out


def _merge_bwd(dh1, w_out, y_pool, y_attn, wup_p, wup_a, e, o):
    lp = o.shape[0]
    nb = lp // ROW_TILE

    def body(dh_ref, wo_ref, yp_ref, ya_ref, wp_ref, wa_ref, e_ref, o_ref,
             dap_ref, daa_ref, dg_ref, dza_ref, do_ref, delta_ref, dyp_ref):
        dmerged = _dot_nt(dh_ref[...].astype(BF16), wo_ref[...])
        a_pool = jnp.dot(yp_ref[...], wp_ref[...], preferred_element_type=F32)
        a_attn = jnp.dot(ya_ref[...], wa_ref[...], preferred_element_type=F32)
        sp = _sigmoid(e_ref[:, E_GP:E_GA].astype(F32))
        sa = _sigmoid(e_ref[:, E_GA:E_COLS].astype(F32))
        dap = (dmerged * sp).astype(BF16)
        daa = (dmerged * sa).astype(BF16)
        dap_ref[...] = dap
        daa_ref[...] = daa
        dg_ref[:, :D_MODEL] = (dmerged * a_pool * (sp * (1.0 - sp))).astype(BF16)
        dg_ref[:, D_MODEL:] = (dmerged * a_attn * (sa * (1.0 - sa))).astype(BF16)
        dyp_ref[...] = _dot_nt(dap, wp_ref[...])
        dya = _dot_nt(daa, wa_ref[...])
        za = e_ref[:, E_ZA:E_GP].astype(F32)
        sz = _sigmoid(za)
        o = o_ref[...]
        do = dya * (za * sz)
        do_ref[...] = do.astype(BF16)
        dza_ref[...] = (dya * o * (sz * (1.0 + za * (1.0 - sz)))).astype(BF16)
        delta_ref[...] = _per_head_rowsum(do * o)

    row = lambda w: pl.BlockSpec((ROW_TILE, w), lambda i: (i, 0))
    full = lambda a: pl.BlockSpec(a.shape, lambda i: (0, 0))
    return pl.pallas_call(
        body, name="merge_bwd", grid=(nb,),
        in_specs=[row(D_MODEL), full(w_out), row(POOL_WIDTH), row(ATTN_WIDTH), full(wup_p), full(wup_a),
                  row(E_COLS), row(ATTN_WIDTH)],
        out_specs=[row(D_MODEL), row(D_MODEL), row(2 * D_MODEL), row(ATTN_WIDTH), row(ATTN_WIDTH),
                   row(ATTN_WIDTH), row(POOL_WIDTH)],
        out_shape=[jax.ShapeDtypeStruct((lp, D_MODEL), BF16), jax.ShapeDtypeStruct((lp, D_MODEL), BF16),
                   jax.ShapeDtypeStruct((lp, 2 * D_MODEL), BF16), jax.ShapeDtypeStruct((lp, ATTN_WIDTH), BF16),
                   jax.ShapeDtypeStruct((lp, ATTN_WIDTH), BF16), jax.ShapeDtypeStruct((lp, ATTN_WIDTH), F32),
                   jax.ShapeDtypeStruct((lp, POOL_WIDTH), F32)],
        compiler_params=_params(("parallel",)),
    )(dh1, w_out, y_pool, y_attn, wup_p, wup_a, e, o)


def _pool_bwd_local(e, dy_pool, pw, scale):
    lp = e.shape[0]
    nb = lp // ROW_TILE
    ng = len(POOL_WINDOWS)

    def body(uc_ref, up_ref, z_ref, dy_ref, pw_ref, sc_ref, dpc_ref, dz_ref, dsc_ref, dpw_ref):
        i = pl.program_id(0)

        @pl.when(i == 0)
        def _():
            dsc_ref[...] = jnp.zeros_like(dsc_ref)
            dpw_ref[...] = jnp.zeros_like(dpw_ref)

        u_cur = uc_ref[...].astype(F32)
        u_prev = jnp.where(i == 0, 0.0, up_ref[...].astype(F32))
        ps, invs = _pool_p(u_cur, u_prev, i)
        z = z_ref[...].astype(F32)
        sz = _sigmoid(z)
        dy = dy_ref[...]
        dypre = dy * (z * sz)
        dsilu = sz * (1.0 + z * (1.0 - sz))
        for g in range(ng):
            sl = slice(g * POOL_GROUP, (g + 1) * POOL_GROUP)
            pb = ps[g].astype(BF16)
            w = pw_ref[g]
            yraw = jnp.dot(pb, w, preferred_element_type=F32)
            sc = sc_ref[:, sl]
            dz_ref[:, sl] = (dy[:, sl] * (yraw * sc) * dsilu[:, sl]).astype(BF16)
            dsc_ref[:, sl] += jnp.sum(dypre[:, sl] * yraw, axis=0, keepdims=True)
            dyraw = (dypre[:, sl] * sc).astype(BF16)
            dpw_ref[g] += _dot_tn(pb, dyraw)
            dpc_ref[:, sl] = _dot_nt(dyraw, w) * invs[g]

    blk = (ROW_TILE, POOL_WIDTH)
    return pl.pallas_call(
        body, name="pool_bwd_local", grid=(nb,),
        in_specs=[pl.BlockSpec(blk, lambda i: (i, 0)), pl.BlockSpec(blk, lambda i: (jnp.maximum(i - 1, 0), 0)),
                  pl.BlockSpec(blk, lambda i: (i, 1)), pl.BlockSpec(blk, lambda i: (i, 0)),
                  pl.BlockSpec((ng, POOL_GROUP, POOL_GROUP), lambda i: (0, 0, 0)),
                  pl.BlockSpec((1, POOL_WIDTH), lambda i: (0, 0))],
        out_specs=[pl.BlockSpec(blk, lambda i: (i, 0)), pl.BlockSpec(blk, lambda i: (i, 0)),
                   pl.BlockSpec((1, POOL_WIDTH), lambda i: (0, 0)),
                   pl.BlockSpec((ng, POOL_GROUP, POOL_GROUP), lambda i: (0, 0, 0))],
        out_shape=[jax.ShapeDtypeStruct((lp, POOL_WIDTH), F32), jax.ShapeDtypeStruct((lp, POOL_WIDTH), BF16),
                   jax.ShapeDtypeStruct((1, POOL_WIDTH), F32),
                   jax.ShapeDtypeStruct((ng, POOL_GROUP, POOL_GROUP), F32)],
        compiler_params=_params(("arbitrary",)),
    )(e, e, e, dy_pool, pw, scale)


def _pool_bwd_window(dpc):
    lp = dpc.shape[0]
    nb = lp // ROW_TILE

    def body(cur_ref, nxt_ref, du_ref):
        i = pl.program_id(0)
        cur = cur_ref[...]
        nxt = jnp.where(i == nb - 1, 0.0, nxt_ref[...])
        pos = _pool_counts(i)
        for g, w in enumerate(POOL_WINDOWS):
            sl = slice(g * POOL_GROUP, (g + 1) * POOL_GROUP)
            xc = jnp.concatenate([cur[:, sl], nxt[:, sl]], axis=0)
            win = _leading_sums(xc, g + 1)[:ROW_TILE, :]
            dp = cur[:, sl] * jnp.minimum(pos + 1, w).astype(F32)
            du_ref[:, sl] = (win - dp).astype(BF16)

    blk = (ROW_TILE, POOL_WIDTH)
    return pl.pallas_call(
        body, name="pool_bwd_window", grid=(nb,),
        in_specs=[pl.BlockSpec(blk, lambda i: (i, 0)), pl.BlockSpec(blk, lambda i: (jnp.minimum(i + 1, nb - 1), 0))],
        out_specs=pl.BlockSpec(blk, lambda i: (i, 0)),
        out_shape=jax.ShapeDtypeStruct((lp, POOL_WIDTH), BF16),
        compiler_params=_params(("parallel",)),
    )(dpc, dpc)


def _attn_bwd(qkv, do, lse, delta, bias, nb):
    lk = qkv.shape[0]
    lp = nb * ROW_TILE
    nkb = lk // KV_TILE_BWD
    n_pairs = N_HEADS // 2
    per_kv = KV_TILE_BWD // ROW_TILE

    def body(q_ref, k_ref, v_ref, do_ref, lse_ref, dl_ref, b_ref, dq_ref, dk_ref, dv_ref, dc_ref, dcq_ref,
             dk_acc, dv_acc, dc_acc):
        jj = pl.program_id(1)

        @pl.when(jj == 0)
        def _():
            dq_ref[...] = jnp.zeros_like(dq_ref)
            dcq_ref[...] = jnp.zeros_like(dcq_ref)

        dk_acc[...] = jnp.zeros_like(dk_acc)
        dv_acc[...] = jnp.zeros_like(dv_acc)
        dc_acc[...] = jnp.zeros_like(dc_acc)

        def block(i, n_keys, masked):
            kb, vb = k_ref[:n_keys, :], v_ref[:n_keys, :]
            rows = pl.ds(pl.multiple_of(i * ROW_TILE, ROW_TILE), ROW_TILE)
            qs = _stack_heads(q_ref[rows, :])
            dos = _stack_heads(do_ref[rows, :])
            lse_i, dl_i = lse_ref[rows, :], dl_ref[rows, :]
            s = _dot_nt(qs, kb)
            dp = _dot_nt(dos, vb)
            if masked:
                valid = _causal(i, jj, 1, KV_TILE_BWD)[:, :n_keys]
            ps, dss, dcs, rowsums = [], [], [], []
            for hd in range(2):
                half = slice(hd * ROW_TILE, (hd + 1) * ROW_TILE)
                col = slice(hd * HEAD_DIM, hd * HEAD_DIM + 1)
                sh = s[half] - b_ref[i, 0, 0, hd:hd + 1, :n_keys]
                if masked:
                    sh = jnp.where(valid, sh, NEG)
                p = jnp.exp(sh - lse_i[:, col])
                ds = p * (dp[half] - dl_i[:, col])
                ps.append(p.astype(BF16))
                dss.append(ds.astype(BF16))
                dcs.append(jnp.sum(ds, axis=0, keepdims=True))
                rowsums.append(jnp.sum(ds, axis=1, keepdims=True))
            dsb = jnp.concatenate(dss, axis=0)
            dv_acc[:n_keys, :] += _dot_tn(jnp.concatenate(ps, axis=0), dos)
            dk_acc[:n_keys, :] += _dot_tn(dsb, qs)
            dc_acc[:, :n_keys] -= jnp.concatenate(dcs, axis=0)
            dq_ref[rows, :] += _unstack_heads(jnp.dot(dsb, kb, preferred_element_type=F32))
            dcq_ref[rows, :] += _unstack_heads(jnp.broadcast_to(jnp.concatenate(rowsums, axis=0), (2 * ROW_TILE, LANES)))

        first_q = per_kv * jj
        for r in range(per_kv):
            @pl.when(first_q + r < nb)
            def _():
                block(first_q + r, (r + 1) * ROW_TILE, True)

        def rest(i, carry):
            block(i, KV_TILE_BWD, False)
            return carry

        lax.fori_loop(jnp.minimum(first_q + per_kv, nb), nb, rest, 0)
        dk_ref[...] = dk_acc[...].astype(BF16)
        dv_ref[...] = dv_acc[...].astype(BF16)
        dc_ref[0, 0] = dc_acc[...]

    whole = lambda rows: pl.BlockSpec((rows, LANES), lambda hp, jj: (0, hp))
    kv_blk = lambda off: pl.BlockSpec((KV_TILE_BWD, LANES), lambda hp, jj: (jj, off + hp))
    return pl.pallas_call(
        body, name="attn_bwd", grid=(n_pairs, nkb),
        in_specs=[whole(lk), kv_blk(n_pairs), kv_blk(2 * n_pairs), whole(lp), whole(lp), whole(lp),
                  pl.BlockSpec((nb, 1, 1, 2, KV_TILE_BWD), lambda hp, jj: (0, hp, jj, 0, 0))],
        out_specs=[whole(lp), kv_blk(0), kv_blk(0),
                   pl.BlockSpec((1, 1, 2, KV_TILE_BWD), lambda hp, jj: (hp, jj, 0, 0)), whole(lp)],
        out_shape=[jax.ShapeDtypeStruct((lp, ATTN_WIDTH), F32), jax.ShapeDtypeStruct((lk, ATTN_WIDTH), BF16),
                   jax.ShapeDtypeStruct((lk, ATTN_WIDTH), BF16),
                   jax.ShapeDtypeStruct((n_pairs, nkb, 2, KV_TILE_BWD), F32),
                   jax.ShapeDtypeStruct((lp, ATTN_WIDTH), F32)],
        scratch_shapes=[pltpu.VMEM((KV_TILE_BWD, LANES), F32), pltpu.VMEM((KV_TILE_BWD, LANES), F32),
                        pltpu.VMEM((2, KV_TILE_BWD), F32)],
        compiler_params=_params(("parallel", "arbitrary")),
    )(qkv, qkv, qkv, do, lse, delta, bias)


def _gates_bwd(dc, dcq, f, bfg):
    nb = f.shape[0] // ROW_TILE

    def body(dc_ref, dcq_ref, f_ref, b_ref, df_ref, db_ref, carry):
        step = pl.program_id(0)
        i = nb - 1 - step

        @pl.when(step == 0)
        def _():
            carry[...] = jnp.zeros_like(carry)
            db_ref[...] = jnp.zeros_like(db_ref)

        dcb = dc_ref[...]
        lane = lax.broadcasted_iota(jnp.int32, (ROW_TILE, F_COLS), 1)
        for h in range(N_HEADS):
            dcb = dcb + jnp.where(lane == h, dcq_ref[:, HEAD_DIM * h:HEAD_DIM * h + 1], 0.0)
        r_i = lax.broadcasted_iota(jnp.int32, (ROW_TILE, ROW_TILE), 0)
        c_i = lax.broadcasted_iota(jnp.int32, (ROW_TILE, ROW_TILE), 1)
        upper = (c_i >= r_i).astype(F32)
        dlf = jnp.dot(upper, dcb, precision=HIGHEST, preferred_element_type=F32) + carry[...]
        carry[...] = carry[...] + jnp.sum(dcb, axis=0, keepdims=True)
        logit = f_ref[...] + b_ref[...]
        dlogit = jnp.where(_valid_gate_mask(i), dlf * _sigmoid(-logit), 0.0)
        df_ref[...] = dlogit.astype(BF16)
        db_ref[...] += jnp.sum(dlogit, axis=0, keepdims=True)

    blk = pl.BlockSpec((ROW_TILE, F_COLS), lambda s: (nb - 1 - s, 0))
    wide = pl.BlockSpec((ROW_TILE, ATTN_WIDTH), lambda s: (nb - 1 - s, 0))
    one = pl.BlockSpec((1, F_COLS), lambda s: (0, 0))
    return pl.pallas_call(
        body, name="gates_bwd", grid=(nb,),
        in_specs=[blk, wide, blk, one], out_specs=[blk, one],
        out_shape=[jax.ShapeDtypeStruct(f.shape, BF16), jax.ShapeDtypeStruct((1, F_COLS), F32)],
        scratch_shapes=[pltpu.VMEM((1, F_COLS), F32)],
        compiler_params=_params(("arbitrary",)),
    )(dc, dcq, f, bfg)


def _input_bwd(dproj, df, wt_e, wt_qkv, wt_f, x2, metapad, dh1, g1, hs, sc):
    nb = x2.shape[0] // ROW_TILE + 1
    n = len(hs)

    def body(dp_ref, df_ref, we_ref, w_ref, wf_ref, x_ref, mp_ref, dh_ref, g_ref, *rest):
        h_refs, s_ref = rest[:n], rest[n]
        gx_ref, g0_ref, dg_ref = rest[n + 1:n + 4]
        q_refs, sq_ref = rest[n + 4:2 * n + 4], rest[2 * n + 4]
        sems = rest[2 * n + 5:]
        i = pl.program_id(0)

        @pl.when(i == 0)
        def _():
            dg_ref[...] = jnp.zeros_like(dg_ref)
            own, sends, _ = _chip_scatter_plan(h_refs, s_ref, q_refs, sq_ref, *sems)
            for cp in own + sends:
                cp.start()

        dhn = (jnp.dot(dp_ref[:, :E_COLS], we_ref[...], preferred_element_type=F32)
               + jnp.dot(dp_ref[:, E_COLS:], w_ref[...], preferred_element_type=F32)
               + jnp.dot(df_ref[...], wf_ref[...], preferred_element_type=F32))
        h0 = jnp.where(i == 0, mp_ref[...], x_ref[...])
        r = lax.rsqrt(jnp.mean(h0 * h0, axis=-1, keepdims=True) + RMS_EPS)
        xhat = h0 * r
        dg_ref[...] += jnp.sum(dhn * xhat, axis=0, keepdims=True)
        dxhat = dhn * g_ref[...]
        dh0 = dh_ref[...] + r * (dxhat - xhat * jnp.mean(dxhat * xhat, axis=-1, keepdims=True))
        gx_ref[...] = dh0

        @pl.when(i == 0)
        def _():
            g0_ref[...] = dh0

        @pl.when(i == nb - 1)
        def _():
            own, sends, recvs = _chip_scatter_plan(h_refs, s_ref, q_refs, sq_ref, *sems)
            for cp in recvs:
                cp.wait_recv()
            for cp in sends:
                cp.wait_send()
            for cp in own:
                cp.wait()

    const = lambda shape: pl.BlockSpec(shape, lambda i: (0, 0))
    res = pl.pallas_call(
        body, name="input_bwd", grid=(nb,),
        in_specs=[pl.BlockSpec((ROW_TILE, MAIN_COLS), lambda i: (i, 0)), pl.BlockSpec((ROW_TILE, F_COLS), lambda i: (i, 0)),
                  const((E_COLS, D_MODEL)), const((QKV_COLS, D_MODEL)), const((F_COLS, D_MODEL)),
                  _tokens_spec(), const((ROW_TILE, D_MODEL)),
                  pl.BlockSpec((ROW_TILE, D_MODEL), lambda i: (i, 0)), const((1, D_MODEL))] + [HBM] * (n + 1),
        out_specs=[_tokens_spec(), const((ROW_TILE, D_MODEL)), const((1, D_MODEL))] + [HBM] * (n + 1),
        out_shape=[jax.ShapeDtypeStruct(x2.shape, F32), jax.ShapeDtypeStruct((ROW_TILE, D_MODEL), F32),
                   jax.ShapeDtypeStruct((1, D_MODEL), F32)]
        + [jax.ShapeDtypeStruct(h.shape, h.dtype) for h in hs] + [jax.ShapeDtypeStruct((N_CHIPS,) + sc.shape, sc.dtype)],
        scratch_shapes=[pltpu.SemaphoreType.DMA((3 * (n + 1),)), pltpu.SemaphoreType.DMA((3 * (n + 1),)),
                        pltpu.SemaphoreType.DMA((n + 1,))],
        compiler_params=_params(("arbitrary",), vmem=56 * 1024 * 1024),
    )(dproj, df, wt_e, wt_qkv, wt_f, x2, metapad, dh1, g1, *_in_hbm(*hs, sc))
    return res[:3], res[3:3 + n], res[3 + n]


def _adamw(w, g, m, v, name):
    rows, cols = w.shape
    if rows % 8 == 0:
        tr, tc = _row_tile8(rows), cols
    else:
        tr, tc = rows, (2 * LANES if cols % (2 * LANES) == 0 and rows > 8 else cols)

    def body(w_ref, g_ref, m_ref, v_ref, d_ref, mo_ref, vo_ref):
        g_ = g_ref[...]
        m_new = ADAM_B1 * m_ref[...] + (1.0 - ADAM_B1) * g_
        v_new = ADAM_B2 * v_ref[...] + (1.0 - ADAM_B2) * (g_ * g_)
        m_hat = m_new / (1.0 - ADAM_B1 ** ADAM_STEP)
        v_hat = v_new / (1.0 - ADAM_B2 ** ADAM_STEP)
        d_ref[...] = -ADAM_LR * (m_hat / (jnp.sqrt(v_hat) + ADAM_EPS) + ADAM_WD * w_ref[...])
        mo_ref[...] = m_new
        vo_ref[...] = v_new

    blk = pl.BlockSpec((tr, tc), lambda i, j: (i, j))
    return pl.pallas_call(
        body, name=name, grid=(rows // tr, cols // tc),
        in_specs=[blk] * 4, out_specs=[blk] * 3,
        out_shape=[jax.ShapeDtypeStruct(w.shape, F32)] * 3,
        compiler_params=_params(("parallel", "parallel")),
    )(w, g, m, v)


def _adamw_native(w3, g3, m3, v3, name):
    rows = w3.shape[0]
    tr = rows // 2
    blk = pl.BlockSpec((tr,) + w3.shape[1:], lambda i: (i, 0, 0))
    shape = jax.ShapeDtypeStruct(w3.shape, F32)

    def moments(g_ref, m_ref, v_ref, mo_ref, vo_ref):
        g_ = g_ref[...]
        mo_ref[...] = ADAM_B1 * m_ref[...] + (1.0 - ADAM_B1) * g_
        vo_ref[...] = ADAM_B2 * v_ref[...] + (1.0 - ADAM_B2) * (g_ * g_)

    new_m, new_v = pl.pallas_call(
        moments, name=name + "_moments", grid=(2,), in_specs=[blk] * 3, out_specs=[blk] * 2, out_shape=[shape] * 2,
        compiler_params=_params(("parallel",)),
    )(g3, m3, v3)

    def delta(w_ref, m_ref, v_ref, d_ref):
        m_hat = m_ref[...] / (1.0 - ADAM_B1 ** ADAM_STEP)
        v_hat = v_ref[...] / (1.0 - ADAM_B2 ** ADAM_STEP)
        d_ref[...] = -ADAM_LR * (m_hat / (jnp.sqrt(v_hat) + ADAM_EPS) + ADAM_WD * w_ref[...])

    d = pl.pallas_call(
        delta, name=name + "_delta", grid=(2,), in_specs=[blk] * 3, out_specs=blk, out_shape=shape,
        compiler_params=_params(("parallel",)),
    )(w3, new_m, new_v)
    return d, new_m, new_v


def _row_tile8(rows):
    best = rows
    for t in range(8, 257, 8):
        if rows % t == 0:
            best = t
    return best


def kernel(x, meta_tokens, norm_g, w_in, b_forget, pool_w, pool_scale, w_up_pool, w_up_attn, w_out, final_norm_g, loss_target, m_meta_tokens, m_norm_g, m_w_in, m_b_forget, m_pool_w, m_pool_scale, m_w_up_pool, m_w_up_attn, m_w_out, m_final_norm_g, v_meta_tokens, v_norm_g, v_w_in, v_b_forget, v_pool_w, v_pool_scale, v_w_up_pool, v_w_up_attn, v_w_out, v_final_norm_g):
    seq = x.shape[1]
    assert seq % ROW_TILE == 0 and x.shape[0] == 1
    lp = seq + ROW_TILE
    nb = lp // ROW_TILE
    lk = -(-lp // KV_TILE_BWD) * KV_TILE_BWD
    core = jnp.reshape(lax.axis_index("c"), (1,)).astype(jnp.int32)
    x2 = x[0]
    target = loss_target[0]
    sh_d = D_MODEL // N_CHIPS

    to_rows = lambda a: jnp.transpose(a, (2, 0, 1))
    from_rows = lambda a: jnp.transpose(a, (1, 2, 0))
    gf = final_norm_g.reshape(1, D_MODEL)
    bfg = jnp.pad(b_forget, ((0, 0), (0, F_COLS - N_HEADS)))
    pw_b = pool_w[0].astype(BF16)
    hn, (wg_in, meta_g) = _rmsnorm_fwd(x2, norm_g, lk, [jnp.transpose(w_in[0]).astype(BF16), meta_tokens], [1, 0])
    wt = wg_in.reshape(-1, D_MODEL)
    wt_e = jnp.concatenate([wt[REF_U:REF_Q], wt[REF_ZA:REF_F], wt[REF_GP:]], axis=0)
    wt_qkv = wt[REF_Q:REF_ZA]
    wt_f = jnp.pad(wt[REF_F:REF_GP], ((0, F_COLS - N_HEADS), (0, 0)))
    meta_full = jnp.transpose(meta_g, (1, 0, 2)).reshape(N_META, D_MODEL)
    metapad = jnp.pad(meta_full, ((PAD_ROWS, 0), (0, 0)))

    tm = _row_tile(lp, 2200)
    e, (wg_up_p, wg_up_a, wg_out) = _mm_nt(
        hn, wt_e, BF16, "in_proj_gates", lp, E_COLS, tm, 512,
        gather=([w_up_pool[0].astype(BF16), w_up_attn[0].astype(BF16), w_out[0].astype(BF16)], [0, 0, 0]))
    wup_p = jnp.transpose(wg_up_p, (1, 0, 2)).reshape(POOL_WIDTH, D_MODEL)
    wup_a = jnp.transpose(wg_up_a, (1, 0, 2)).reshape(ATTN_WIDTH, D_MODEL)
    wout = wg_out.reshape(D_MODEL, D_MODEL)
    qkv = _mm_nt(hn, wt_qkv, BF16, "in_proj_qkv", lk, QKV_COLS, _row_tile(lk, 2600), 512, scale_first=HEAD_DIM ** -0.5)
    f = _mm_nt(hn, wt_f, F32, "in_proj_forget", lp, F_COLS, tm, F_COLS)
    c = _gates_fwd(f, bfg)
    c_t = jnp.transpose(c[:, :N_HEADS])
    bias = c_t[None, :, :] - jnp.transpose(c_t[:, ::ROW_TILE])[:, :, None]
    bias = jnp.where(jnp.arange(lp) < PAD_ROWS, -NEG, bias)
    bias = jnp.pad(bias, ((0, 0), (0, 0), (0, lk - lp)))
    by_kv = lambda t: jnp.transpose(bias.reshape(nb, N_HEADS // 2, 2, lk // t, t), (0, 1, 3, 2, 4))
    bias, bias_bwd = by_kv(KV_TILE), by_kv(KV_TILE_BWD)
    y_pool = _pool_fwd(e, pw_b, pool_scale)
    o, lse = _attn_fwd(qkv, bias, nb)
    merged, y_attn = _merge_fwd(y_pool, o, e, wup_p, wup_a)
    dh1, loss_part, dgf = _head_fwd_bwd(merged, wout, x2, metapad, gf, target)

    dap, daa, dgate, dza, do, delta, dy_pool = _merge_bwd(dh1, wout, y_pool, y_attn, wup_p, wup_a, e, o)
    dpc, dzp, dscale, dpw = _pool_bwd_local(e, dy_pool, pw_b, pool_scale)
    du = _pool_bwd_window(dpc)
    dq, dk, dv, dc4, dcq = _attn_bwd(qkv, do, lse, delta, bias_bwd, nb)
    dc = jnp.transpose(dc4, (1, 3, 0, 2)).reshape(lk, N_HEADS)[:lp]
    df, db = _gates_bwd(jnp.pad(dc, ((0, 0), (0, F_COLS - N_HEADS))), dcq, f, bfg)
    dproj = jnp.concatenate([du, dzp, dza, dgate, (dq * HEAD_DIM ** -0.5).astype(BF16), dk[:lp], dv[:lp]], axis=1)
    tk = _row_tile(lp, 1100)
    dw_out = _mm_tn(merged, dh1, "grad_w_out", lp, 512, D_MODEL, tk)
    dw_up_p = _mm_tn(y_pool, dap, "grad_w_up_pool", lp, 512, D_MODEL, lp, chunks=N_CHIPS)
    dw_up_a = _mm_tn(y_attn, daa, "grad_w_up_attn", lp, 512, D_MODEL, lp, chunks=N_CHIPS)
    dwt_f = _mm_tn(df, hn, "grad_w_in_forget", lp, F_COLS, D_MODEL, tk)

    def pad8(a):
        return jnp.pad(a, ((0, (-a.shape[0]) % 8), (0, 0)))

    small_parts = [pad8(a) for a in (dgf.reshape(-1, LANES), dscale.reshape(-1, LANES), db, dpw.reshape(-1, LANES),
                                     loss_part)]
    small = jnp.concatenate(small_parts, axis=0)
    soffs = [0]
    for p in small_parts:
        soffs.append(soffs[-1] + p.shape[0])

    gs_rows = [dw_up_p, dw_up_a, dw_out.reshape(N_CHIPS, sh_d, D_MODEL)]
    half = MAIN_COLS // 2
    dwt_a, (*rb_rows, rb_f, sr) = _mm_tn(dproj, hn, "grad_w_in_a", lp, half // 4, D_MODEL, lp, m=half, m_off=0,
                                         swap=(gs_rows + [dwt_f, small], [1, 1, 1, 1, None]))
    dwt_b, (rb_a,) = _mm_tn(dproj, hn, "grad_w_in_b", lp, half // 4, D_MODEL, lp, m=half, m_off=4, swap=([dwt_a], [1]))
    (rb_b,) = _sibling_exchange([dwt_b], [1])
    *hs_rows, sc = _add_sibling_half(gs_rows, rb_rows, [1, 1, 1], small, sr, core)
    h_a, h_b, h_f = _add_halves_2d([dwt_a, dwt_b, dwt_f], [rb_a, rb_b, rb_f], core)
    h_in = jnp.concatenate([h_a[E_U:E_ZA], h_b[D_MODEL:], h_a[E_ZA:E_GP], h_f[:N_HEADS], h_a[E_GP:], h_b[:D_MODEL]], axis=0)
    hs = [h_in.reshape(N_CHIPS, -1, D_MODEL // 2)] + hs_rows
    grad_axes = [2, 1, 1, 1]
    (grad_x, g_block0, dg1), qs, sq = _input_bwd(dproj, df, wt_e, wt_qkv, wt_f, x2, metapad, dh1, norm_g, hs, sc)
    g_w_in_rows, g_w_up_p, g_w_up_a, g_w_out, st = _reduce_allgather(qs, grad_axes, sq, [True, False, False, False])
    late = _small_allreduce(jnp.concatenate([pad8(dg1.reshape(-1, LANES)), g_block0[PAD_ROWS:].reshape(-1, LANES)], axis=0))
    n_norm = D_MODEL // LANES
    g_norm = late[:n_norm].reshape(1, D_MODEL)
    chip = 2 * lax.axis_index("x") + lax.axis_index("y")
    g_meta = lax.dynamic_slice_in_dim(late[n_norm:].reshape(N_META, D_MODEL), chip * sh_d, sh_d, axis=1)

    spiece = lambda k, rows: st[soffs[k]:soffs[k] + rows]
    g_final = spiece(0, D_MODEL // LANES).reshape(1, D_MODEL)
    g_scale = spiece(1, POOL_WIDTH // LANES).reshape(1, POOL_WIDTH)
    g_bf = spiece(2, 1)
    g_pw = spiece(3, POOL_WIDTH)
    loss = st[soffs[4], 0]

    def pad_lanes(a):
        return jnp.pad(a, ((0, 0), (0, F_COLS - N_HEADS)))

    w_in_res = (g_w_in_rows,) + _adamw_native(to_rows(w_in), g_w_in_rows, to_rows(m_w_in), to_rows(v_w_in), "adamw_w_in")

    upd = [
        ("meta_tokens", meta_tokens, g_meta, m_meta_tokens, v_meta_tokens),
        ("norm_g", norm_g, g_norm, m_norm_g, v_norm_g),
        ("w_in", None, None, None, None),
        ("b_forget", pad_lanes(b_forget), g_bf, pad_lanes(m_b_forget), pad_lanes(v_b_forget)),
        ("pool_w", pool_w.reshape(-1, LANES), g_pw, m_pool_w.reshape(-1, LANES), v_pool_w.reshape(-1, LANES)),
        ("pool_scale", pool_scale, g_scale, m_pool_scale, v_pool_scale),
        ("w_up_pool", w_up_pool[0], g_w_up_p, m_w_up_pool[0], v_w_up_pool[0]),
        ("w_up_attn", w_up_attn[0], g_w_up_a, m_w_up_attn[0], v_w_up_attn[0]),
        ("w_out", w_out[0], g_w_out, m_w_out[0], v_w_out[0]),
        ("final_norm_g", gf, g_final, m_final_norm_g.reshape(1, D_MODEL), v_final_norm_g.reshape(1, D_MODEL)),
    ]
    shapes = [meta_tokens.shape, norm_g.shape, w_in.shape, b_forget.shape, pool_w.shape, pool_scale.shape,
              w_up_pool.shape, w_up_attn.shape, w_out.shape, final_norm_g.shape]
    grads, deltas, new_ms, new_vs = [], [], [], []
    for (name, w_, g_, m_in, v_in), shp in zip(upd, shapes):
        if name == "w_in":
            res = tuple(from_rows(a) for a in w_in_res)
        else:
            d_, mn_, vn_ = _adamw(w_, g_, m_in, v_in, "adamw_" + name)
            res = (g_, d_, mn_, vn_)
        if name == "b_forget":
            res = tuple(a[:, :N_HEADS] for a in res)
        for lst, a in zip((grads, deltas, new_ms, new_vs), res):
            lst.append(a.reshape(shp))

    return (loss, grad_x.reshape(x.shape), *grads, *deltas, *new_ms, *new_vs)
```

```python
import jax
import jax.numpy as jnp
from jax import lax
from jax.experimental import pallas as pl
from jax.experimental.pallas import tpu as pltpu

F32 = jnp.float32
BF16 = jnp.bfloat16
MESH = pl.DeviceIdType.MESH
HIGHEST = lax.Precision.HIGHEST
HBM = pl.BlockSpec(memory_space=pltpu.HBM)

D_MODEL = 1024
N_META = 16
POOL_WIDTH = 512
POOL_GROUP = 128
POOL_WINDOWS = (2, 4, 8, 16)
N_HEADS = 8
HEAD_DIM = 64
ATTN_WIDTH = 512
RMS_EPS = 1e-6
N_CHIPS = 4

ADAM_LR = 0.001
ADAM_B1 = 0.9
ADAM_B2 = 0.999
ADAM_EPS = 1e-08
ADAM_WD = 0.01
ADAM_STEP = 10

LANES = 128
ROW_TILE = 256
KV_TILE = 1024
KV_TILE_BWD = 1024
PAD_ROWS = ROW_TILE - N_META
NEG = -1e30

E_U, E_ZP, E_ZA, E_GP, E_GA, E_COLS = 0, 512, 1024, 1536, 2560, 3584
QKV_COLS = 3 * ATTN_WIDTH
MAIN_COLS = E_COLS + QKV_COLS
F_COLS = LANES
REF_U, REF_Q, REF_ZA, REF_F, REF_GP = 0, 1024, 2560, 3072, 3080
VMEM_LIMIT = 48 * 1024 * 1024


def _params(sem=None, vmem=VMEM_LIMIT):
    return pltpu.CompilerParams(dimension_semantics=sem, vmem_limit_bytes=vmem)


def _in_hbm(*arrays):
    return [pltpu.with_memory_space_constraint(a, pltpu.HBM) for a in arrays]


def _row_tile(n, target):
    best = 16
    for t in range(16, target + 1, 16):
        if n % t == 0:
            best = t
    return best


def _sigmoid(x):
    return 1.0 / (1.0 + jnp.exp(-x))


def _half(ref, axis, which):
    n = ref.shape[axis] // 2
    idx = [slice(None)] * len(ref.shape)
    idx[axis] = pl.ds(pl.multiple_of(which * n, n), n)
    return ref.at[tuple(idx)]


def _half_shape(shape, axis):
    s = list(shape)
    s[axis] //= 2
    return tuple(s)


def _part(ref, axis, start, size):
    idx = [slice(None)] * len(ref.shape)
    idx[axis] = pl.ds(pl.multiple_of(start, size), size)
    return ref.at[tuple(idx)]


def _relayed(shard, axis):
    return shard.shape[axis] % (4 * (16 if shard.dtype == BF16 else 8) if axis == 0 else 4 * LANES) == 0


def _gather_start(ins, outs, axes, send, recv, fsend, frecv, xsend, xrecv, local):
    x, y, c = lax.axis_index("x"), lax.axis_index("y"), lax.axis_index("c")
    me = 2 * x + y
    for t in range(len(ins)):
        pltpu.make_async_copy(ins[t], outs[t].at[me], local.at[t]).start()
        chips = [(1 - x, y), (x, 1 - y)] + ([] if _relayed(ins[t], axes[t]) else [(1 - x, 1 - y)])
        for k, chip in enumerate(chips):
            pltpu.make_async_remote_copy(
                src_ref=_half(ins[t], axes[t], c), dst_ref=_half(outs[t].at[me], axes[t], c),
                send_sem=send.at[3 * t + k], recv_sem=recv.at[3 * t + k], device_id=(*chip, c), device_id_type=MESH).start()


def _gather_finish(ins, outs, axes, send, recv, fsend, frecv, xsend, xrecv, local):
    x, y, c = lax.axis_index("x"), lax.axis_index("y"), lax.axis_index("c")
    me = 2 * x + y
    sibling = (x, y, 1 - c)
    chips = [(1 - x, y), (x, 1 - y), (1 - x, 1 - y)]
    diag = 2 * (1 - x) + (1 - y)
    n = len(ins)

    def over_ici(t, k, chip, dst_slot):
        return pltpu.make_async_remote_copy(
            src_ref=_half(ins[t], axes[t], c), dst_ref=_half(outs[t].at[dst_slot], axes[t], c),
            send_sem=send.at[3 * t + k], recv_sem=recv.at[3 * t + k], device_id=(*chip, c), device_id_type=MESH)

    def to_sibling(t, k, slot, which):
        return pltpu.make_async_remote_copy(
            src_ref=_half(outs[t].at[slot], axes[t], which), dst_ref=_half(outs[t].at[slot], axes[t], which),
            send_sem=fsend.at[3 * t + k], recv_sem=frecv.at[3 * t + k], device_id=sibling, device_id_type=MESH)

    def quarter(t, slot, which, q):
        size = ins[t].shape[axes[t]] // 4
        return _part(outs[t].at[slot], axes[t], (2 * which + q) * size, size)

    def relay(t, q, slot, chip):
        return pltpu.make_async_remote_copy(
            src_ref=quarter(t, slot, c, q), dst_ref=quarter(t, slot, c, q), send_sem=xsend.at[4 * t + q],
            recv_sem=xrecv.at[4 * t + q], device_id=(*chip, c), device_id_type=MESH)

    def quarter_to_sibling(t, q, which):
        return pltpu.make_async_remote_copy(
            src_ref=quarter(t, diag, which, q), dst_ref=quarter(t, diag, which, q), send_sem=xsend.at[4 * t + 2 + q],
            recv_sem=xrecv.at[4 * t + 2 + q], device_id=sibling, device_id_type=MESH)

    sent = []
    for t in range(n):
        relayed = _relayed(ins[t], axes[t])
        for k, (px, py) in enumerate(chips[:2] if relayed else chips):
            slot = 2 * px + py
            over_ici(t, k, (px, py), slot).wait_recv()
            sent.append(to_sibling(t, k, slot, c))
            if relayed:
                sent.append(relay(t, k, slot, chips[1 - k]))
            for cp in sent[-(2 if relayed else 1):]:
                cp.start()
    for t in range(n):
        relayed = _relayed(ins[t], axes[t])
        if relayed:
            for q in range(2):
                relay(t, q, diag, chips[1 - q]).wait_recv()
                sent.append(quarter_to_sibling(t, q, c))
                sent[-1].start()
        for k, (px, py) in enumerate(chips[:2] if relayed else chips):
            to_sibling(t, k, 2 * px + py, 1 - c).wait_recv()
        if relayed:
            for q in range(2):
                quarter_to_sibling(t, q, 1 - c).wait_recv()
    for t in range(n):
        for k, chip in enumerate(chips[:2] if _relayed(ins[t], axes[t]) else chips):
            over_ici(t, k, chip, me).wait_send()
    for cp in sent:
        cp.wait_send()
    for t in range(n):
        pltpu.make_async_copy(ins[t], outs[t].at[me], local.at[t]).wait()


def _gather_sems(n):
    return [pltpu.SemaphoreType.DMA((3 * n,)), pltpu.SemaphoreType.DMA((3 * n,)), pltpu.SemaphoreType.DMA((3 * n,)),
            pltpu.SemaphoreType.DMA((3 * n,)), pltpu.SemaphoreType.DMA((4 * n,)), pltpu.SemaphoreType.DMA((4 * n,)),
            pltpu.SemaphoreType.DMA((n,))]


def _gathered_shapes(shards):
    return [jax.ShapeDtypeStruct((N_CHIPS,) + s.shape, s.dtype) for s in shards]


def _swap_copies(srcs, dsts, axes, send, recv):
    x, y, c = lax.axis_index("x"), lax.axis_index("y"), lax.axis_index("c")
    return [pltpu.make_async_remote_copy(
        src_ref=srcs[t] if axes[t] is None else _half(srcs[t], axes[t], 1 - c), dst_ref=dsts[t],
        send_sem=send.at[t], recv_sem=recv.at[t], device_id=(x, y, 1 - c), device_id_type=MESH) for t in range(len(srcs))]


def _swap_shapes(srcs, axes):
    return [jax.ShapeDtypeStruct(g.shape if a is None else _half_shape(g.shape, a), g.dtype) for g, a in zip(srcs, axes)]


def _sibling_exchange(gs, axes):
    n = len(gs)

    def body(*refs):
        cps = _swap_copies(refs[:n], refs[n:2 * n], axes, *refs[2 * n:])
        for cp in cps:
            cp.start()
        for cp in cps:
            cp.wait()

    return pl.pallas_call(
        body, name="grad_sibling_exchange",
        in_specs=[HBM] * n, out_specs=[HBM] * n, out_shape=_swap_shapes(gs, axes),
        scratch_shapes=[pltpu.SemaphoreType.DMA((n,)), pltpu.SemaphoreType.DMA((n,))],
    )(*_in_hbm(*gs))


def _add_halves_2d(parts, rbs, core):
    n = len(parts)
    rows, w = rbs[0].shape
    tr = 512
    last_rows = rbs[-1].shape[0]

    def body(core_ref, *refs):
        p_refs, r_refs, h_refs = refs[:n], refs[n:2 * n], refs[2 * n:]
        for t in range(n - 1):
            h_refs[t][...] = (p_refs[t][...] + r_refs[t][...]).astype(BF16)

        @pl.when(pl.program_id(0) == 0)
        def _():
            h_refs[n - 1][...] = (p_refs[n - 1][...] + r_refs[n - 1][...]).astype(BF16)

    mine = lambda r: pl.BlockSpec((r, w), lambda i, cr: (i, cr[0]))
    tile = lambda r: pl.BlockSpec((r, w), lambda i, cr: (i, 0))
    mine_small = pl.BlockSpec((last_rows, w), lambda i, cr: (0, cr[0]))
    tile_small = pl.BlockSpec((last_rows, w), lambda i, cr: (0, 0))
    return pl.pallas_call(
        body, name="grad_add_sibling_w_in",
        grid_spec=pltpu.PrefetchScalarGridSpec(
            num_scalar_prefetch=1, grid=(rows // tr,),
            in_specs=[mine(tr)] * (n - 1) + [mine_small] + [tile(tr)] * (n - 1) + [tile_small],
            out_specs=[tile(tr)] * (n - 1) + [tile_small]),
        out_shape=[jax.ShapeDtypeStruct(r.shape, BF16) for r in rbs],
        compiler_params=_params(("arbitrary",)),
    )(core, *parts, *rbs)


def _add_sibling_half(gs, rbs, axes, small, sr, core):
    n = len(gs)

    def body(core_ref, *refs):
        g_refs, rb_refs = refs[:n], refs[n:2 * n]
        s_ref, sr_ref = refs[2 * n], refs[2 * n + 1]
        h_refs, sc_ref = refs[2 * n + 2:3 * n + 2], refs[3 * n + 2]
        for t in range(n):
            h_refs[t][...] = (g_refs[t][...] + rb_refs[t][...]).astype(BF16)
        sc_ref[...] = s_ref[...] + sr_ref[...]

    def mine(rb, axis):
        blk = (None,) + rb.shape[1:]
        if axis == 2:
            return pl.BlockSpec(blk, lambda j, cr: (j, 0, cr[0]))
        return pl.BlockSpec(blk, lambda j, cr: (j, cr[0], 0))

    chunk = lambda rb: pl.BlockSpec((None,) + rb.shape[1:], lambda j, cr: (j, 0, 0))
    whole = pl.BlockSpec(small.shape, lambda j, cr: (0, 0))
    return pl.pallas_call(
        body, name="grad_add_sibling",
        grid_spec=pltpu.PrefetchScalarGridSpec(
            num_scalar_prefetch=1, grid=(N_CHIPS,),
            in_specs=[mine(rb, a) for rb, a in zip(rbs, axes)] + [chunk(rb) for rb in rbs] + [whole, whole],
            out_specs=[chunk(rb) for rb in rbs] + [whole]),
        out_shape=[jax.ShapeDtypeStruct(rb.shape, BF16) for rb in rbs] + [jax.ShapeDtypeStruct(small.shape, F32)],
        compiler_params=_params(("arbitrary",)),
    )(core, *gs, *rbs, small, sr)


def _chip_scatter_plan(h_refs, s_ref, q_refs, sq_ref, send, recv, local):
    n = len(h_refs)
    x, y, c = lax.axis_index("x"), lax.axis_index("y"), lax.axis_index("c")
    me = 2 * x + y
    chips = [(1 - x, y), (x, 1 - y), (1 - x, 1 - y)]

    def copy(t, k, chip, src_slot, dst_slot):
        src = h_refs[t].at[src_slot] if t < n else s_ref
        dst = (q_refs[t] if t < n else sq_ref).at[dst_slot]
        return pltpu.make_async_remote_copy(src_ref=src, dst_ref=dst, send_sem=send.at[3 * t + k],
                                            recv_sem=recv.at[3 * t + k], device_id=(*chip, c), device_id_type=MESH)

    own = [pltpu.make_async_copy(h_refs[t].at[me], q_refs[t].at[me], local.at[t]) for t in range(n)]
    own.append(pltpu.make_async_copy(s_ref, sq_ref.at[me], local.at[n]))
    sends = [copy(t, k, (px, py), 2 * px + py, me) for t in range(n + 1) for k, (px, py) in enumerate(chips)]
    recvs = [copy(t, k, (px, py), me, 2 * px + py) for t in range(n + 1) for k, (px, py) in enumerate(chips)]
    return own, sends, recvs


def _small_allreduce(buf):
    def body(b_ref, o_ref, sib_buf, chip_buf, send, recv):
        x, y, c = lax.axis_index("x"), lax.axis_index("y"), lax.axis_index("c")
        me = 2 * x + y
        chips = [(1 - x, y), (x, 1 - y), (1 - x, 1 - y)]
        swap = pltpu.make_async_remote_copy(src_ref=b_ref, dst_ref=sib_buf, send_sem=send.at[0], recv_sem=recv.at[0],
                                            device_id=(x, y, 1 - c), device_id_type=MESH)
        swap.start()
        swap.wait()
        chip_buf[me] = b_ref[...] + sib_buf[...]

        def copy(k, chip, slot):
            return pltpu.make_async_remote_copy(src_ref=chip_buf.at[slot], dst_ref=chip_buf.at[slot], send_sem=send.at[1 + k],
                                                recv_sem=recv.at[1 + k], device_id=(*chip, c), device_id_type=MESH)

        sends = [copy(k, chip, me) for k, chip in enumerate(chips)]
        for cp in sends:
            cp.start()
        for k, (px, py) in enumerate(chips):
            copy(k, (px, py), 2 * px + py).wait_recv()
        for cp in sends:
            cp.wait_send()
        o_ref[...] = ((chip_buf[0] + chip_buf[1]) + chip_buf[2]) + chip_buf[3]

    vmem = pl.BlockSpec(memory_space=pltpu.VMEM)
    return pl.pallas_call(
        body, name="small_allreduce", in_specs=[vmem], out_specs=vmem,
        out_shape=jax.ShapeDtypeStruct(buf.shape, F32),
        scratch_shapes=[pltpu.VMEM(buf.shape, F32), pltpu.VMEM((N_CHIPS,) + buf.shape, F32),
                        pltpu.SemaphoreType.DMA((4,)), pltpu.SemaphoreType.DMA((4,))],
        compiler_params=_params(),
    )(buf)


def _reduce_allgather(qs, axes, sq, as_rows):
    n = len(qs)
    shard_shapes, half_axes = [], []
    for q, a, rows_form in zip(qs, axes, as_rows):
        shape = [d * 2 if i == a - 1 else d for i, d in enumerate(q.shape[1:])]
        if rows_form:
            assert a == 2
            shape = [shape[0], 1, shape[1]]
        shard_shapes.append(tuple(shape))
        half_axes.append(2 if rows_form else a - 1)

    def body(*refs):
        q_refs, sq_ref = refs[:n], refs[n]
        o_refs, st_ref = refs[n + 1:2 * n + 1], refs[2 * n + 1]
        send, recv = refs[2 * n + 2:]
        x, y, c = lax.axis_index("x"), lax.axis_index("y"), lax.axis_index("c")

        def swap(t, which):
            return pltpu.make_async_remote_copy(
                src_ref=_half(o_refs[t], half_axes[t], which), dst_ref=_half(o_refs[t], half_axes[t], which),
                send_sem=send.at[t], recv_sem=recv.at[t], device_id=(x, y, 1 - c), device_id_type=MESH)

        sent = []
        for t in range(n):
            q = q_refs[t]
            total = ((q[0].astype(F32) + q[1].astype(F32)) + q[2].astype(F32)) + q[3].astype(F32)
            if as_rows[t]:
                total = total.reshape(total.shape[0], 1, total.shape[1])
            _half(o_refs[t], half_axes[t], c)[...] = total
            cp = swap(t, c)
            cp.start()
            sent.append(cp)
        st_ref[...] = ((sq_ref[0] + sq_ref[1]) + sq_ref[2]) + sq_ref[3]
        for t in range(n):
            swap(t, 1 - c).wait_recv()
        for cp in sent:
            cp.wait_send()

    vmem = pl.BlockSpec(memory_space=pltpu.VMEM)
    return pl.pallas_call(
        body, name="grad_reduce_allgather",
        in_specs=[vmem] * (n + 1), out_specs=[vmem] * (n + 1),
        out_shape=[jax.ShapeDtypeStruct(s, F32) for s in shard_shapes] + [jax.ShapeDtypeStruct(sq.shape[1:], F32)],
        scratch_shapes=[pltpu.SemaphoreType.DMA((n,)), pltpu.SemaphoreType.DMA((n,))],
        compiler_params=_params(),
    )(*qs, sq)


def _dot_nt(a, b):
    return lax.dot_general(a, b, (((1,), (1,)), ((), ())), preferred_element_type=F32)


def _dot_tn(a, b):
    return lax.dot_general(a, b, (((0,), (0,)), ((), ())), preferred_element_type=F32)


def _mm_nt(a, bt, out_dtype, name, m, n, tm, tn, row_block=0, scale_first=None, gather=None):
    k = a.shape[1]
    shards, axes = gather if gather is not None else ([], [])
    ng = len(shards)
    grid = (m // tm, n // tn)

    def body(a_ref, b_ref, *rest):
        ins, o_ref, outs, sems = rest[:ng], rest[ng], rest[ng + 1:2 * ng + 1], rest[2 * ng + 1:]
        first = (pl.program_id(0) == 0) & (pl.program_id(1) == 0)
        last = (pl.program_id(0) == grid[0] - 1) & (pl.program_id(1) == grid[1] - 1)
        if ng:
            @pl.when(first)
            def _():
                _gather_start(ins, outs, axes, *sems)

        r = _dot_nt(a_ref[...], b_ref[...])
        if scale_first is not None:
            r = r * jnp.where(pl.program_id(1) == 0, scale_first, 1.0)
        o_ref[...] = r.astype(out_dtype)

        if ng:
            @pl.when(last)
            def _():
                _gather_finish(ins, outs, axes, *sems)

    res = pl.pallas_call(
        body, name=name, grid=grid,
        in_specs=[pl.BlockSpec((tm, k), lambda i, j: (i, 0)), pl.BlockSpec((tn, k), lambda i, j: (row_block + j, 0))]
        + [HBM] * ng,
        out_specs=[pl.BlockSpec((tm, tn), lambda i, j: (i, j))] + [HBM] * ng,
        out_shape=[jax.ShapeDtypeStruct((m, n), out_dtype)] + _gathered_shapes(shards),
        scratch_shapes=_gather_sems(ng) if ng else [],
        compiler_params=_params(("arbitrary", "arbitrary") if ng else ("parallel", "parallel")),
    )(a, bt, *_in_hbm(*shards))
    return (res[0], res[1:]) if ng else res[0]


def _mm_tn(a, b, name, k, tm, tn, tk, chunks=1, m=None, m_off=0, swap=None):
    m = a.shape[1] if m is None else m
    n = b.shape[1]
    cw = n // chunks
    srcs, axes = swap if swap is not None else ([], [])
    ns = len(srcs)
    grid = (m // tm, n // tn, k // tk)

    def body(a_ref, b_ref, *rest):
        s_refs, o_ref, d_refs, sems = rest[:ns], rest[ns], rest[ns + 1:2 * ns + 1], rest[2 * ns + 1:]
        ids = [pl.program_id(d) for d in range(3)]
        if ns:
            @pl.when((ids[0] == 0) & (ids[1] == 0) & (ids[2] == 0))
            def _():
                for cp in _swap_copies(s_refs, d_refs, axes, *sems):
                    cp.start()

        @pl.when(ids[2] == 0)
        def _():
            o_ref[...] = jnp.zeros_like(o_ref)
        r = _dot_tn(a_ref[...].astype(BF16), b_ref[...].astype(BF16))
        if chunks > 1:
            for c in range(chunks):
                o_ref[c] += r[:, c * cw:(c + 1) * cw]
        else:
            o_ref[...] += r

        if ns:
            @pl.when((ids[0] == grid[0] - 1) & (ids[1] == grid[1] - 1) & (ids[2] == grid[2] - 1))
            def _():
                for cp in _swap_copies(s_refs, d_refs, axes, *sems):
                    cp.wait()

    if chunks > 1:
        assert tn == n
        out_spec = pl.BlockSpec((chunks, tm, cw), lambda i, j, kk: (0, i, 0))
        out_shape = jax.ShapeDtypeStruct((chunks, m, cw), F32)
    else:
        out_spec = pl.BlockSpec((tm, tn), lambda i, j, kk: (i, j))
        out_shape = jax.ShapeDtypeStruct((m, n), F32)
    res = pl.pallas_call(
        body, name=name, grid=grid,
        in_specs=[pl.BlockSpec((tk, tm), lambda i, j, kk: (kk, m_off + i)), pl.BlockSpec((tk, tn), lambda i, j, kk: (kk, j))]
        + [HBM] * ns,
        out_specs=[out_spec] + [HBM] * ns, out_shape=[out_shape] + _swap_shapes(srcs, axes),
        scratch_shapes=[pltpu.SemaphoreType.DMA((ns,)), pltpu.SemaphoreType.DMA((ns,))] if ns else [],
        compiler_params=_params(("arbitrary",) * 3 if ns else ("parallel", "parallel", "arbitrary")),
    )(a, b, *_in_hbm(*srcs))
    return (res[0], res[1:]) if ns else res[0]


def _tokens_spec():
    return pl.BlockSpec((ROW_TILE, D_MODEL), lambda i: (jnp.maximum(i - 1, 0), 0))


def _rmsnorm_fwd(x2, g1, lk, shards, axes):
    nb = x2.shape[0] // ROW_TILE + 1
    nblk = lk // ROW_TILE
    ng = len(shards)
    meta_cols = shards[-1].shape[1]

    def body(x_ref, g_ref, *rest):
        ins, hn_ref, outs = rest[:ng], rest[ng], rest[ng + 1:2 * ng + 1]
        sems, meta_buf, meta_sem = rest[2 * ng + 1:2 * ng + 8], rest[2 * ng + 8], rest[2 * ng + 9]
        s = pl.program_id(0)
        blk = (s + 1) % nblk

        @pl.when(s == 0)
        def _():
            _gather_start(ins, outs, axes, *sems)

        def normed(h):
            r = lax.rsqrt(jnp.mean(h * h, axis=-1, keepdims=True) + RMS_EPS)
            return ((h * r) * g_ref[...]).astype(BF16)

        @pl.when(s < nblk - 1)
        def _():
            hn_ref[...] = normed(jnp.where(blk >= nb, 0.0, x_ref[...]))

        @pl.when(s == nblk - 1)
        def _():
            _gather_finish(ins, outs, axes, *sems)
            fetch = pltpu.make_async_copy(outs[-1], meta_buf, meta_sem.at[0])
            fetch.start()
            fetch.wait()
            meta = jnp.concatenate([meta_buf[j] for j in range(N_CHIPS)], axis=1)
            hn_ref[...] = normed(jnp.concatenate([jnp.zeros((PAD_ROWS, D_MODEL), F32), meta], axis=0))

    res = pl.pallas_call(
        body, name="rmsnorm_fwd", grid=(nblk,),
        in_specs=[pl.BlockSpec((ROW_TILE, D_MODEL), lambda s: (jnp.clip((s + 1) % nblk - 1, 0, nb - 2), 0)),
                  pl.BlockSpec((1, D_MODEL), lambda s: (0, 0))] + [HBM] * ng,
        out_specs=[pl.BlockSpec((ROW_TILE, D_MODEL), lambda s: ((s + 1) % nblk, 0))] + [HBM] * ng,
        out_shape=[jax.ShapeDtypeStruct((lk, D_MODEL), BF16)] + _gathered_shapes(shards),
        scratch_shapes=_gather_sems(ng) + [pltpu.VMEM((N_CHIPS, N_META, meta_cols), F32), pltpu.SemaphoreType.DMA((1,))],
        compiler_params=_params(("arbitrary",)),
    )(x2, g1, *_in_hbm(*shards))
    return res[0], res[1:]


def _valid_gate_mask(i):
    row = i * ROW_TILE + lax.broadcasted_iota(jnp.int32, (ROW_TILE, F_COLS), 0)
    col = lax.broadcasted_iota(jnp.int32, (ROW_TILE, F_COLS), 1)
    return (row >= PAD_ROWS) & (col < N_HEADS)


def _gates_fwd(f, bfg):
    nb = f.shape[0] // ROW_TILE

    def body(f_ref, b_ref, c_ref, carry):
        i = pl.program_id(0)

        @pl.when(i == 0)
        def _():
            carry[...] = jnp.zeros_like(carry)

        logit = f_ref[...] + b_ref[...]
        lf = jnp.minimum(logit, 0.0) - jnp.log1p(jnp.exp(-jnp.abs(logit)))
        lf = jnp.where(_valid_gate_mask(i), lf, 0.0)
        r_i = lax.broadcasted_iota(jnp.int32, (ROW_TILE, ROW_TILE), 0)
        c_i = lax.broadcasted_iota(jnp.int32, (ROW_TILE, ROW_TILE), 1)
        tri = (c_i <= r_i).astype(F32)
        c_ref[...] = jnp.dot(tri, lf, precision=HIGHEST, preferred_element_type=F32) + carry[...]
        carry[...] = carry[...] + jnp.sum(lf, axis=0, keepdims=True)

    return pl.pallas_call(
        body, name="gates_fwd", grid=(nb,),
        in_specs=[pl.BlockSpec((ROW_TILE, F_COLS), lambda i: (i, 0)), pl.BlockSpec((1, F_COLS), lambda i: (0, 0))],
        out_specs=pl.BlockSpec((ROW_TILE, F_COLS), lambda i: (i, 0)),
        out_shape=jax.ShapeDtypeStruct(f.shape, F32),
        scratch_shapes=[pltpu.VMEM((1, F_COLS), F32)],
        compiler_params=_params(("arbitrary",)),
    )(f, bfg)


def _pool_counts(i):
    row = i * ROW_TILE + lax.broadcasted_iota(jnp.int32, (ROW_TILE, 1), 0)
    return jnp.maximum(row - PAD_ROWS, 0)


def _trailing_sums(xc, levels):
    acc = xc
    for lv in range(levels):
        acc = acc + pltpu.roll(acc, 1 << lv, 0)
    return acc


def _leading_sums(xc, levels):
    n = xc.shape[0]
    acc = xc
    for lv in range(levels):
        acc = acc + pltpu.roll(acc, n - (1 << lv), 0)
    return acc


def _pool_p(u_cur, u_prev, i):
    pos = _pool_counts(i)
    ps, invs = [], []
    for g, w in enumerate(POOL_WINDOWS):
        sl = slice(g * POOL_GROUP, (g + 1) * POOL_GROUP)
        cur = u_cur[:, sl]
        xc = jnp.concatenate([u_prev[:, sl], cur], axis=0)
        win = _trailing_sums(xc, g + 1)[ROW_TILE:, :]
        inv = 1.0 / jnp.minimum(pos + 1, w).astype(F32)
        ps.append(win * inv - cur)
        invs.append(inv)
    return ps, invs


def _pool_fwd(e, pw, scale):
    nb = e.shape[0] // ROW_TILE

    def body(uc_ref, up_ref, z_ref, pw_ref, sc_ref, y_ref):
        i = pl.program_id(0)
        u_cur = uc_ref[...].astype(F32)
        u_prev = jnp.where(i == 0, 0.0, up_ref[...].astype(F32))
        ps, _ = _pool_p(u_cur, u_prev, i)
        z = z_ref[...].astype(F32)
        gate = z * _sigmoid(z)
        for g in range(len(POOL_WINDOWS)):
            sl = slice(g * POOL_GROUP, (g + 1) * POOL_GROUP)
            yraw = jnp.dot(ps[g].astype(BF16), pw_ref[g], preferred_element_type=F32)
            y_ref[:, sl] = ((yraw * sc_ref[:, sl]) * gate[:, sl]).astype(BF16)

    blk = (ROW_TILE, POOL_WIDTH)
    return pl.pallas_call(
        body, name="pool_fwd", grid=(nb,),
        in_specs=[pl.BlockSpec(blk, lambda i: (i, 0)), pl.BlockSpec(blk, lambda i: (jnp.maximum(i - 1, 0), 0)),
                  pl.BlockSpec(blk, lambda i: (i, 1)),
                  pl.BlockSpec((len(POOL_WINDOWS), POOL_GROUP, POOL_GROUP), lambda i: (0, 0, 0)),
                  pl.BlockSpec((1, POOL_WIDTH), lambda i: (0, 0))],
        out_specs=pl.BlockSpec(blk, lambda i: (i, 0)),
        out_shape=jax.ShapeDtypeStruct((e.shape[0], POOL_WIDTH), BF16),
        compiler_params=_params(("parallel",)),
    )(e, e, e, pw, scale)


def _stack_heads(a):
    first = lax.broadcasted_iota(jnp.int32, a.shape, 1) < HEAD_DIM
    zero = jnp.zeros_like(a)
    return jnp.concatenate([jnp.where(first, a, zero), jnp.where(first, zero, a)], axis=0)


def _unstack_heads(a):
    rows = a.shape[0] // 2
    first = lax.broadcasted_iota(jnp.int32, (rows, LANES), 1) < HEAD_DIM
    return jnp.where(first, a[:rows], a[rows:])


def _causal(i, jj, stacked, kv_tile):
    r = lax.broadcasted_iota(jnp.int32, (stacked * ROW_TILE, kv_tile), 0)
    if stacked == 2:
        r = jnp.where(r >= ROW_TILE, r - ROW_TILE, r)
    kidx = jj * kv_tile + lax.broadcasted_iota(jnp.int32, (stacked * ROW_TILE, kv_tile), 1)
    return kidx <= i * ROW_TILE + r


def _stack_rows(b):
    n = b.shape[1]
    return jnp.concatenate([jnp.broadcast_to(b[0:1], (ROW_TILE, n)), jnp.broadcast_to(b[1:2], (ROW_TILE, n))], axis=0)


def _attn_fwd(qkv, bias, nb):
    lk = qkv.shape[0]
    lp = nb * ROW_TILE
    nkb = lk // KV_TILE
    n_pairs = N_HEADS // 2

    def body(q_ref, k_ref, v_ref, b_ref, o_ref, lse_ref):
        i = pl.program_id(1)
        qs = _stack_heads(q_ref[...])
        last = (i * ROW_TILE) // KV_TILE

        def block(jj, carry, masked, n_keys=KV_TILE):
            m, l, acc = carry
            rows = pl.ds(pl.multiple_of(jj * KV_TILE, KV_TILE), n_keys)
            s = _dot_nt(qs, k_ref[rows, :]) - _stack_rows(b_ref[0, 0, jj][:, :n_keys])
            if masked:
                s = jnp.where(_causal(i, jj, 2, KV_TILE)[:, :n_keys], s, NEG)
            m_new = jnp.maximum(m, jnp.max(s, axis=1, keepdims=True))
            alpha = jnp.exp(m - m_new)
            p = jnp.exp(s - m_new)
            l = alpha * l + jnp.sum(p, axis=1, keepdims=True)
            acc = alpha * acc + jnp.dot(p.astype(BF16), v_ref[rows, :], preferred_element_type=F32)
            return m_new, l, acc

        init = (jnp.full((2 * ROW_TILE, 1), NEG, F32), jnp.zeros((2 * ROW_TILE, 1), F32),
                jnp.zeros((2 * ROW_TILE, LANES), F32))
        carry = lax.fori_loop(0, last, lambda jj, c: block(jj, c, False), init)
        ends = [lambda c, n=(r + 1) * ROW_TILE: block(last, c, True, n) for r in range(KV_TILE // ROW_TILE)]
        m, l, acc = lax.switch(i - last * (KV_TILE // ROW_TILE), ends, carry)
        o_ref[...] = _unstack_heads(acc / l)
        lse_ref[...] = _unstack_heads(jnp.broadcast_to(m + jnp.log(l), (2 * ROW_TILE, LANES)))

    return pl.pallas_call(
        body, name="attn_fwd", grid=(n_pairs, nb),
        in_specs=[pl.BlockSpec((ROW_TILE, LANES), lambda hp, i: (i, hp)),
                  pl.BlockSpec((lk, LANES), lambda hp, i: (0, n_pairs + hp)),
                  pl.BlockSpec((lk, LANES), lambda hp, i: (0, 2 * n_pairs + hp)),
                  pl.BlockSpec((1, 1, nkb, 2, KV_TILE), lambda hp, i: (i, hp, 0, 0, 0))],
        out_specs=[pl.BlockSpec((ROW_TILE, LANES), lambda hp, i: (i, hp)),
                   pl.BlockSpec((ROW_TILE, LANES), lambda hp, i: (i, hp))],
        out_shape=[jax.ShapeDtypeStruct((lp, ATTN_WIDTH), F32), jax.ShapeDtypeStruct((lp, ATTN_WIDTH), F32)],
        compiler_params=_params(("parallel", "parallel")),
    )(qkv, qkv, qkv, bias)


def _merge_fwd(y_pool, o, e, wup_p, wup_a):
    lp = o.shape[0]
    nb = lp // ROW_TILE

    def body(yp_ref, o_ref, e_ref, wp_ref, wa_ref, mg_ref, ya_ref):
        za = e_ref[:, E_ZA:E_GP].astype(F32)
        ya = (o_ref[...] * (za * _sigmoid(za))).astype(BF16)
        ya_ref[...] = ya
        a_pool = jnp.dot(yp_ref[...], wp_ref[...], preferred_element_type=F32)
        a_attn = jnp.dot(ya, wa_ref[...], preferred_element_type=F32)
        mg_ref[...] = (_sigmoid(e_ref[:, E_GP:E_GA].astype(F32)) * a_pool
                       + _sigmoid(e_ref[:, E_GA:E_COLS].astype(F32)) * a_attn).astype(BF16)

    return pl.pallas_call(
        body, name="merge_fwd", grid=(nb,),
        in_specs=[pl.BlockSpec((ROW_TILE, POOL_WIDTH), lambda i: (i, 0)),
                  pl.BlockSpec((ROW_TILE, ATTN_WIDTH), lambda i: (i, 0)),
                  pl.BlockSpec((ROW_TILE, E_COLS), lambda i: (i, 0)),
                  pl.BlockSpec((POOL_WIDTH, D_MODEL), lambda i: (0, 0)),
                  pl.BlockSpec((ATTN_WIDTH, D_MODEL), lambda i: (0, 0))],
        out_specs=[pl.BlockSpec((ROW_TILE, D_MODEL), lambda i: (i, 0)),
                   pl.BlockSpec((ROW_TILE, ATTN_WIDTH), lambda i: (i, 0))],
        out_shape=[jax.ShapeDtypeStruct((lp, D_MODEL), BF16), jax.ShapeDtypeStruct((lp, ATTN_WIDTH), BF16)],
        compiler_params=_params(("parallel",)),
    )(y_pool, o, e, wup_p, wup_a)


def _head_fwd_bwd(merged, w_out, x2, metapad, gf, target):
    lp = merged.shape[0]
    nb = lp // ROW_TILE

    def body(mg_ref, w_ref, x_ref, mp_ref, g_ref, t_ref, dh_ref, loss_ref, dg_ref):
        i = pl.program_id(0)

        @pl.when(i == 0)
        def _():
            loss_ref[...] = jnp.zeros_like(loss_ref)
            dg_ref[...] = jnp.zeros_like(dg_ref)

        h0 = jnp.where(i == 0, mp_ref[...], x_ref[...])
        h1 = h0 + jnp.dot(mg_ref[...], w_ref[...], preferred_element_type=F32)
        r = lax.rsqrt(jnp.mean(h1 * h1, axis=-1, keepdims=True) + RMS_EPS)
        xhat = h1 * r
        g = g_ref[...]
        err = jnp.where(i == 0, 0.0, xhat * g - t_ref[...])
        loss_ref[...] += 0.5 * jnp.sum(jnp.mean(err * err, axis=-1, keepdims=True))
        dy = err / D_MODEL
        dg_ref[...] += jnp.sum(dy * xhat, axis=0, keepdims=True)
        dxhat = dy * g
        dh_ref[...] = r * (dxhat - xhat * jnp.mean(dxhat * xhat, axis=-1, keepdims=True))

    return pl.pallas_call(
        body, name="head_fwd_bwd", grid=(nb,),
        in_specs=[pl.BlockSpec((ROW_TILE, D_MODEL), lambda i: (i, 0)),
                  pl.BlockSpec((D_MODEL, D_MODEL), lambda i: (0, 0)),
                  _tokens_spec(), pl.BlockSpec((ROW_TILE, D_MODEL), lambda i: (0, 0)),
                  pl.BlockSpec((1, D_MODEL), lambda i: (0, 0)), _tokens_spec()],
        out_specs=[pl.BlockSpec((ROW_TILE, D_MODEL), lambda i: (i, 0)),
                   pl.BlockSpec((1, LANES), lambda i: (0, 0)), pl.BlockSpec((1, D_MODEL), lambda i: (0, 0))],
        out_shape=[jax.ShapeDtypeStruct((lp, D_MODEL), F32), jax.ShapeDtypeStruct((1, LANES), F32),
                   jax.ShapeDtypeStruct((1, D_MODEL), F32)],
        compiler_params=_params(("arbitrary",)),
    )(merged, w_out, x2, metapad, gf, target)


def _per_head_rowsum(t):
    head = lax.broadcasted_iota(jnp.int32, t.shape, 1) // HEAD_DIM
    out = jnp.zeros_like(t)
    for h in range(N_HEADS):
        sel = head == h
        out = jnp.where(sel, jnp.sum(jnp.where(sel, t, 0.0), axis=1, keepdims=True), out)
    return out


def _merge_bwd(dh1, w_out, y_pool, y_attn, wup_p, wup_a, e, o):
    lp = o.shape[0]
    nb = lp // ROW_TILE

    def body(dh_ref, wo_ref, yp_ref, ya_ref, wp_ref, wa_ref, e_ref, o_ref,
             dap_ref, daa_ref, dg_ref, dza_ref, do_ref, delta_ref, dyp_ref):
        dmerged = _dot_nt(dh_ref[...].astype(BF16), wo_ref[...])
        a_pool = jnp.dot(yp_ref[...], wp_ref[...], preferred_element_type=F32)
        a_attn = jnp.dot(ya_ref[...], wa_ref[...], preferred_element_type=F32)
        sp = _sigmoid(e_ref[:, E_GP:E_GA].astype(F32))
        sa = _sigmoid(e_ref[:, E_GA:E_COLS].astype(F32))
        dap = (dmerged * sp).astype(BF16)
        daa = (dmerged * sa).astype(BF16)
        dap_ref[...] = dap
        daa_ref[...] = daa
        dg_ref[:, :D_MODEL] = (dmerged * a_pool * (sp * (1.0 - sp))).astype(BF16)
        dg_ref[:, D_MODEL:] = (dmerged * a_attn * (sa * (1.0 - sa))).astype(BF16)
        dyp_ref[...] = _dot_nt(dap, wp_ref[...])
        dya = _dot_nt(daa, wa_ref[...])
        za = e_ref[:, E_ZA:E_GP].astype(F32)
        sz = _sigmoid(za)
        o = o_ref[...]
        do = dya * (za * sz)
        do_ref[...] = do.astype(BF16)
        dza_ref[...] = (dya * o * (sz * (1.0 + za * (1.0 - sz)))).astype(BF16)
        delta_ref[...] = _per_head_rowsum(do * o)

    row = lambda w: pl.BlockSpec((ROW_TILE, w), lambda i: (i, 0))
    full = lambda a: pl.BlockSpec(a.shape, lambda i: (0, 0))
    return pl.pallas_call(
        body, name="merge_bwd", grid=(nb,),
        in_specs=[row(D_MODEL), full(w_out), row(POOL_WIDTH), row(ATTN_WIDTH), full(wup_p), full(wup_a),
                  row(E_COLS), row(ATTN_WIDTH)],
        out_specs=[row(D_MODEL), row(D_MODEL), row(2 * D_MODEL), row(ATTN_WIDTH), row(ATTN_WIDTH),
                   row(ATTN_WIDTH), row(POOL_WIDTH)],
        out_shape=[jax.ShapeDtypeStruct((lp, D_MODEL), BF16), jax.ShapeDtypeStruct((lp, D_MODEL), BF16),
                   jax.ShapeDtypeStruct((lp, 2 * D_MODEL), BF16), jax.ShapeDtypeStruct((lp, ATTN_WIDTH), BF16),
                   jax.ShapeDtypeStruct((lp, ATTN_WIDTH), BF16), jax.ShapeDtypeStruct((lp, ATTN_WIDTH), F32),
                   jax.ShapeDtypeStruct((lp, POOL_WIDTH), F32)],
        compiler_params=_params(("parallel",)),
    )(dh1, w_out, y_pool, y_attn, wup_p, wup_a, e, o)


def _pool_bwd_local(e, dy_pool, pw, scale):
    lp = e.shape[0]
    nb = lp // ROW_TILE
    ng = len(POOL_WINDOWS)

    def body(uc_ref, up_ref, z_ref, dy_ref, pw_ref, sc_ref, dpc_ref, dz_ref, dsc_ref, dpw_ref):
        i = pl.program_id(0)

        @pl.when(i == 0)
        def _():
            dsc_ref[...] = jnp.zeros_like(dsc_ref)
            dpw_ref[...] = jnp.zeros_like(dpw_ref)

        u_cur = uc_ref[...].astype(F32)
        u_prev = jnp.where(i == 0, 0.0, up_ref[...].astype(F32))
        ps, invs = _pool_p(u_cur, u_prev, i)
        z = z_ref[...].astype(F32)
        sz = _sigmoid(z)
        dy = dy_ref[...]
        dypre = dy * (z * sz)
        dsilu = sz * (1.0 + z * (1.0 - sz))
        for g in range(ng):
            sl = slice(g * POOL_GROUP, (g + 1) * POOL_GROUP)
            pb = ps[g].astype(BF16)
            w = pw_ref[g]
            yraw = jnp.dot(pb, w, preferred_element_type=F32)
            sc = sc_ref[:, sl]
            dz_ref[:, sl] = (dy[:, sl] * (yraw * sc) * dsilu[:, sl]).astype(BF16)
            dsc_ref[:, sl] += jnp.sum(dypre[:, sl] * yraw, axis=0, keepdims=True)
            dyraw = (dypre[:, sl] * sc).astype(BF16)
            dpw_ref[g] += _dot_tn(pb, dyraw)
            dpc_ref[:, sl] = _dot_nt(dyraw, w) * invs[g]

    blk = (ROW_TILE, POOL_WIDTH)
    return pl.pallas_call(
        body, name="pool_bwd_local", grid=(nb,),
        in_specs=[pl.BlockSpec(blk, lambda i: (i, 0)), pl.BlockSpec(blk, lambda i: (jnp.maximum(i - 1, 0), 0)),
                  pl.BlockSpec(blk, lambda i: (i, 1)), pl.BlockSpec(blk, lambda i: (i, 0)),
                  pl.BlockSpec((ng, POOL_GROUP, POOL_GROUP), lambda i: (0, 0, 0)),
                  pl.BlockSpec((1, POOL_WIDTH), lambda i: (0, 0))],
        out_specs=[pl.BlockSpec(blk, lambda i: (i, 0)), pl.BlockSpec(blk, lambda i: (i, 0)),
                   pl.BlockSpec((1, POOL_WIDTH), lambda i: (0, 0)),
                   pl.BlockSpec((ng, POOL_GROUP, POOL_GROUP), lambda i: (0, 0, 0))],
        out_shape=[jax.ShapeDtypeStruct((lp, POOL_WIDTH), F32), jax.ShapeDtypeStruct((lp, POOL_WIDTH), BF16),
                   jax.ShapeDtypeStruct((1, POOL_WIDTH), F32),
                   jax.ShapeDtypeStruct((ng, POOL_GROUP, POOL_GROUP), F32)],
        compiler_params=_params(("arbitrary",)),
    )(e, e, e, dy_pool, pw, scale)


def _pool_bwd_window(dpc):
    lp = dpc.shape[0]
    nb = lp // ROW_TILE

    def body(cur_ref, nxt_ref, du_ref):
        i = pl.program_id(0)
        cur = cur_ref[...]
        nxt = jnp.where(i == nb - 1, 0.0, nxt_ref[...])
        pos = _pool_counts(i)
        for g, w in enumerate(POOL_WINDOWS):
            sl = slice(g * POOL_GROUP, (g + 1) * POOL_GROUP)
            xc = jnp.concatenate([cur[:, sl], nxt[:, sl]], axis=0)
            win = _leading_sums(xc, g + 1)[:ROW_TILE, :]
            dp = cur[:, sl] * jnp.minimum(pos + 1, w).astype(F32)
            du_ref[:, sl] = (win - dp).astype(BF16)

    blk = (ROW_TILE, POOL_WIDTH)
    return pl.pallas_call(
        body, name="pool_bwd_window", grid=(nb,),
        in_specs=[pl.BlockSpec(blk, lambda i: (i, 0)), pl.BlockSpec(blk, lambda i: (jnp.minimum(i + 1, nb - 1), 0))],
        out_specs=pl.BlockSpec(blk, lambda i: (i, 0)),
        out_shape=jax.ShapeDtypeStruct((lp, POOL_WIDTH), BF16),
        compiler_params=_params(("parallel",)),
    )(dpc, dpc)


def _attn_bwd(qkv, do, lse, delta, bias, nb):
    lk = qkv.shape[0]
    lp = nb * ROW_TILE
    nkb = lk // KV_TILE_BWD
    n_pairs = N_HEADS // 2
    per_kv = KV_TILE_BWD // ROW_TILE

    def body(q_ref, k_ref, v_ref, do_ref, lse_ref, dl_ref, b_ref, dq_ref, dk_ref, dv_ref, dc_ref, dcq_ref,
             dk_acc, dv_acc, dc_acc):
        jj = pl.program_id(1)

        @pl.when(jj == 0)
        def _():
            dq_ref[...] = jnp.zeros_like(dq_ref)
            dcq_ref[...] = jnp.zeros_like(dcq_ref)

        dk_acc[...] = jnp.zeros_like(dk_acc)
        dv_acc[...] = jnp.zeros_like(dv_acc)
        dc_acc[...] = jnp.zeros_like(dc_acc)

        def block(i, n_keys, masked):
            kb, vb = k_ref[:n_keys, :], v_ref[:n_keys, :]
            rows = pl.ds(pl.multiple_of(i * ROW_TILE, ROW_TILE), ROW_TILE)
            qs = _stack_heads(q_ref[rows, :])
            dos = _stack_heads(do_ref[rows, :])
            lse_i, dl_i = lse_ref[rows, :], dl_ref[rows, :]
            s = _dot_nt(qs, kb)
            dp = _dot_nt(dos, vb)
            if masked:
                valid = _causal(i, jj, 1, KV_TILE_BWD)[:, :n_keys]
            ps, dss, dcs, rowsums = [], [], [], []
            for hd in range(2):
                half = slice(hd * ROW_TILE, (hd + 1) * ROW_TILE)
                col = slice(hd * HEAD_DIM, hd * HEAD_DIM + 1)
                sh = s[half] - b_ref[i, 0, 0, hd:hd + 1, :n_keys]
                if masked:
                    sh = jnp.where(valid, sh, NEG)
                p = jnp.exp(sh - lse_i[:, col])
                ds = p * (dp[half] - dl_i[:, col])
                ps.append(p.astype(BF16))
                dss.append(ds.astype(BF16))
                dcs.append(jnp.sum(ds, axis=0, keepdims=True))
                rowsums.append(jnp.sum(ds, axis=1, keepdims=True))
            dsb = jnp.concatenate(dss, axis=0)
            dv_acc[:n_keys, :] += _dot_tn(jnp.concatenate(ps, axis=0), dos)
            dk_acc[:n_keys, :] += _dot_tn(dsb, qs)
            dc_acc[:, :n_keys] -= jnp.concatenate(dcs, axis=0)
            dq_ref[rows, :] += _unstack_heads(jnp.dot(dsb, kb, preferred_element_type=F32))
            dcq_ref[rows, :] += _unstack_heads(jnp.broadcast_to(jnp.concatenate(rowsums, axis=0), (2 * ROW_TILE, LANES)))

        first_q = per_kv * jj
        for r in range(per_kv):
            @pl.when(first_q + r < nb)
            def _():
                block(first_q + r, (r + 1) * ROW_TILE, True)

        def rest(i, carry):
            block(i, KV_TILE_BWD, False)
            return carry

        lax.fori_loop(jnp.minimum(first_q + per_kv, nb), nb, rest, 0)
        dk_ref[...] = dk_acc[...].astype(BF16)
        dv_ref[...] = dv_acc[...].astype(BF16)
        dc_ref[0, 0] = dc_acc[...]

    whole = lambda rows: pl.BlockSpec((rows, LANES), lambda hp, jj: (0, hp))
    kv_blk = lambda off: pl.BlockSpec((KV_TILE_BWD, LANES), lambda hp, jj: (jj, off + hp))
    return pl.pallas_call(
        body, name="attn_bwd", grid=(n_pairs, nkb),
        in_specs=[whole(lk), kv_blk(n_pairs), kv_blk(2 * n_pairs), whole(lp), whole(lp), whole(lp),
                  pl.BlockSpec((nb, 1, 1, 2, KV_TILE_BWD), lambda hp, jj: (0, hp, jj, 0, 0))],
        out_specs=[whole(lp), kv_blk(0), kv_blk(0),
                   pl.BlockSpec((1, 1, 2, KV_TILE_BWD), lambda hp, jj: (hp, jj, 0, 0)), whole(lp)],
        out_shape=[jax.ShapeDtypeStruct((lp, ATTN_WIDTH), F32), jax.ShapeDtypeStruct((lk, ATTN_WIDTH), BF16),
                   jax.ShapeDtypeStruct((lk, ATTN_WIDTH), BF16),
                   jax.ShapeDtypeStruct((n_pairs, nkb, 2, KV_TILE_BWD), F32),
                   jax.ShapeDtypeStruct((lp, ATTN_WIDTH), F32)],
        scratch_shapes=[pltpu.VMEM((KV_TILE_BWD, LANES), F32), pltpu.VMEM((KV_TILE_BWD, LANES), F32),
                        pltpu.VMEM((2, KV_TILE_BWD), F32)],
        compiler_params=_params(("parallel", "arbitrary")),
    )(qkv, qkv, qkv, do, lse, delta, bias)


def _gates_bwd(dc, dcq, f, bfg):
    nb = f.shape[0] // ROW_TILE

    def body(dc_ref, dcq_ref, f_ref, b_ref, df_ref, db_ref, carry):
        step = pl.program_id(0)
        i = nb - 1 - step

        @pl.when(step == 0)
        def _():
            carry[...] = jnp.zeros_like(carry)
            db_ref[...] = jnp.zeros_like(db_ref)

        dcb = dc_ref[...]
        lane = lax.broadcasted_iota(jnp.int32, (ROW_TILE, F_COLS), 1)
        for h in range(N_HEADS):
            dcb = dcb + jnp.where(lane == h, dcq_ref[:, HEAD_DIM * h:HEAD_DIM * h + 1], 0.0)
        r_i = lax.broadcasted_iota(jnp.int32, (ROW_TILE, ROW_TILE), 0)
        c_i = lax.broadcasted_iota(jnp.int32, (ROW_TILE, ROW_TILE), 1)
        upper = (c_i >= r_i).astype(F32)
        dlf = jnp.dot(upper, dcb, precision=HIGHEST, preferred_element_type=F32) + carry[...]
        carry[...] = carry[...] + jnp.sum(dcb, axis=0, keepdims=True)
        logit = f_ref[...] + b_ref[...]
        dlogit = jnp.where(_valid_gate_mask(i), dlf * _sigmoid(-logit), 0.0)
        df_ref[...] = dlogit.astype(BF16)
        db_ref[...] += jnp.sum(dlogit, axis=0, keepdims=True)

    blk = pl.BlockSpec((ROW_TILE, F_COLS), lambda s: (nb - 1 - s, 0))
    wide = pl.BlockSpec((ROW_TILE, ATTN_WIDTH), lambda s: (nb - 1 - s, 0))
    one = pl.BlockSpec((1, F_COLS), lambda s: (0, 0))
    return pl.pallas_call(
        body, name="gates_bwd", grid=(nb,),
        in_specs=[blk, wide, blk, one], out_specs=[blk, one],
        out_shape=[jax.ShapeDtypeStruct(f.shape, BF16), jax.ShapeDtypeStruct((1, F_COLS), F32)],
        scratch_shapes=[pltpu.VMEM((1, F_COLS), F32)],
        compiler_params=_params(("arbitrary",)),
    )(dc, dcq, f, bfg)


def _input_bwd(dproj, df, wt_e, wt_qkv, wt_f, x2, metapad, dh1, g1, hs, sc):
    nb = x2.shape[0] // ROW_TILE + 1
    n = len(hs)

    def body(dp_ref, df_ref, we_ref, w_ref, wf_ref, x_ref, mp_ref, dh_ref, g_ref, *rest):
        h_refs, s_ref = rest[:n], rest[n]
        gx_ref, g0_ref, dg_ref = rest[n + 1:n + 4]
        q_refs, sq_ref = rest[n + 4:2 * n + 4], rest[2 * n + 4]
        sems = rest[2 * n + 5:]
        i = pl.program_id(0)

        @pl.when(i == 0)
        def _():
            dg_ref[...] = jnp.zeros_like(dg_ref)
            own, sends, _ = _chip_scatter_plan(h_refs, s_ref, q_refs, sq_ref, *sems)
            for cp in own + sends:
                cp.start()

        dhn = (jnp.dot(dp_ref[:, :E_COLS], we_ref[...], preferred_element_type=F32)
               + jnp.dot(dp_ref[:, E_COLS:], w_ref[...], preferred_element_type=F32)
               + jnp.dot(df_ref[...], wf_ref[...], preferred_element_type=F32))
        h0 = jnp.where(i == 0, mp_ref[...], x_ref[...])
        r = lax.rsqrt(jnp.mean(h0 * h0, axis=-1, keepdims=True) + RMS_EPS)
        xhat = h0 * r
        dg_ref[...] += jnp.sum(dhn * xhat, axis=0, keepdims=True)
        dxhat = dhn * g_ref[...]
        dh0 = dh_ref[...] + r * (dxhat - xhat * jnp.mean(dxhat * xhat, axis=-1, keepdims=True))
        gx_ref[...] = dh0

        @pl.when(i == 0)
        def _():
            g0_ref[...] = dh0

        @pl.when(i == nb - 1)
        def _():
            own, sends, recvs = _chip_scatter_plan(h_refs, s_ref, q_refs, sq_ref, *sems)
            for cp in recvs:
                cp.wait_recv()
            for cp in sends:
                cp.wait_send()
            for cp in own:
                cp.wait()

    const = lambda shape: pl.BlockSpec(shape, lambda i: (0, 0))
    res = pl.pallas_call(
        body, name="input_bwd", grid=(nb,),
        in_specs=[pl.BlockSpec((ROW_TILE, MAIN_COLS), lambda i: (i, 0)), pl.BlockSpec((ROW_TILE, F_COLS), lambda i: (i, 0)),
                  const((E_COLS, D_MODEL)), const((QKV_COLS, D_MODEL)), const((F_COLS, D_MODEL)),
                  _tokens_spec(), const((ROW_TILE, D_MODEL)),
                  pl.BlockSpec((ROW_TILE, D_MODEL), lambda i: (i, 0)), const((1, D_MODEL))] + [HBM] * (n + 1),
        out_specs=[_tokens_spec(), const((ROW_TILE, D_MODEL)), const((1, D_MODEL))] + [HBM] * (n + 1),
        out_shape=[jax.ShapeDtypeStruct(x2.shape, F32), jax.ShapeDtypeStruct((ROW_TILE, D_MODEL), F32),
                   jax.ShapeDtypeStruct((1, D_MODEL), F32)]
        + [jax.ShapeDtypeStruct(h.shape, h.dtype) for h in hs] + [jax.ShapeDtypeStruct((N_CHIPS,) + sc.shape, sc.dtype)],
        scratch_shapes=[pltpu.SemaphoreType.DMA((3 * (n + 1),)), pltpu.SemaphoreType.DMA((3 * (n + 1),)),
                        pltpu.SemaphoreType.DMA((n + 1,))],
        compiler_params=_params(("arbitrary",), vmem=56 * 1024 * 1024),
    )(dproj, df, wt_e, wt_qkv, wt_f, x2, metapad, dh1, g1, *_in_hbm(*hs, sc))
    return res[:3], res[3:3 + n], res[3 + n]


def _adamw(w, g, m, v, name):
    rows, cols = w.shape
    if rows % 8 == 0:
        tr, tc = _row_tile8(rows), cols
    else:
        tr, tc = rows, (2 * LANES if cols % (2 * LANES) == 0 and rows > 8 else cols)

    def body(w_ref, g_ref, m_ref, v_ref, d_ref, mo_ref, vo_ref):
        g_ = g_ref[...]
        m_new = ADAM_B1 * m_ref[...] + (1.0 - ADAM_B1) * g_
        v_new = ADAM_B2 * v_ref[...] + (1.0 - ADAM_B2) * (g_ * g_)
        m_hat = m_new / (1.0 - ADAM_B1 ** ADAM_STEP)
        v_hat = v_new / (1.0 - ADAM_B2 ** ADAM_STEP)
        d_ref[...] = -ADAM_LR * (m_hat / (jnp.sqrt(v_hat) + ADAM_EPS) + ADAM_WD * w_ref[...])
        mo_ref[...] = m_new
        vo_ref[...] = v_new

    blk = pl.BlockSpec((tr, tc), lambda i, j: (i, j))
    return pl.pallas_call(
        body, name=name, grid=(rows // tr, cols // tc),
        in_specs=[blk] * 4, out_specs=[blk] * 3,
        out_shape=[jax.ShapeDtypeStruct(w.shape, F32)] * 3,
        compiler_params=_params(("parallel", "parallel")),
    )(w, g, m, v)


def _adamw_native(w3, g3, m3, v3, name):
    rows = w3.shape[0]
    tr = rows // 2
    blk = pl.BlockSpec((tr,) + w3.shape[1:], lambda i: (i, 0, 0))
    shape = jax.ShapeDtypeStruct(w3.shape, F32)

    def moments(g_ref, m_ref, v_ref, mo_ref, vo_ref):
        g_ = g_ref[...]
        mo_ref[...] = ADAM_B1 * m_ref[...] + (1.0 - ADAM_B1) * g_
        vo_ref[...] = ADAM_B2 * v_ref[...] + (1.0 - ADAM_B2) * (g_ * g_)

    new_m, new_v = pl.pallas_call(
        moments, name=name + "_moments", grid=(2,), in_specs=[blk] * 3, out_specs=[blk] * 2, out_shape=[shape] * 2,
        compiler_params=_params(("parallel",)),
    )(g3, m3, v3)

    def delta(w_ref, m_ref, v_ref, d_ref):
        m_hat = m_ref[...] / (1.0 - ADAM_B1 ** ADAM_STEP)
        v_hat = v_ref[...] / (1.0 - ADAM_B2 ** ADAM_STEP)
        d_ref[...] = -ADAM_LR * (m_hat / (jnp.sqrt(v_hat) + ADAM_EPS) + ADAM_WD * w_ref[...])

    d = pl.pallas_call(
        delta, name=name + "_delta", grid=(2,), in_specs=[blk] * 3, out_specs=blk, out_shape=shape,
        compiler_params=_params(("parallel",)),
    )(w3, new_m, new_v)
    return d, new_m, new_v


def _row_tile8(rows):
    best = rows
    for t in range(8, 257, 8):
        if rows % t == 0:
            best = t
    return best


def kernel(x, meta_tokens, norm_g, w_in, b_forget, pool_w, pool_scale, w_up_pool, w_up_attn, w_out, final_norm_g, loss_target, m_meta_tokens, m_norm_g, m_w_in, m_b_forget, m_pool_w, m_pool_scale, m_w_up_pool, m_w_up_attn, m_w_out, m_final_norm_g, v_meta_tokens, v_norm_g, v_w_in, v_b_forget, v_pool_w, v_pool_scale, v_w_up_pool, v_w_up_attn, v_w_out, v_final_norm_g):
    seq = x.shape[1]
    assert seq % ROW_TILE == 0 and x.shape[0] == 1
    lp = seq + ROW_TILE
    nb = lp // ROW_TILE
    lk = -(-lp // KV_TILE_BWD) * KV_TILE_BWD
    core = jnp.reshape(lax.axis_index("c"), (1,)).astype(jnp.int32)
    x2 = x[0]
    target = loss_target[0]
    sh_d = D_MODEL // N_CHIPS

    to_rows = lambda a: jnp.transpose(a, (2, 0, 1))
    from_rows = lambda a: jnp.transpose(a, (1, 2, 0))
    gf = final_norm_g.reshape(1, D_MODEL)
    bfg = jnp.pad(b_forget, ((0, 0), (0, F_COLS - N_HEADS)))
    pw_b = pool_w[0].astype(BF16)
    hn, (wg_in, meta_g) = _rmsnorm_fwd(x2, norm_g, lk, [jnp.transpose(w_in[0]).astype(BF16), meta_tokens], [1, 0])
    wt = wg_in.reshape(-1, D_MODEL)
    wt_e = jnp.concatenate([wt[REF_U:REF_Q], wt[REF_ZA:REF_F], wt[REF_GP:]], axis=0)
    wt_qkv = wt[REF_Q:REF_ZA]
    wt_f = jnp.pad(wt[REF_F:REF_GP], ((0, F_COLS - N_HEADS), (0, 0)))
    meta_full = jnp.transpose(meta_g, (1, 0, 2)).reshape(N_META, D_MODEL)
    metapad = jnp.pad(meta_full, ((PAD_ROWS, 0), (0, 0)))

    tm = _row_tile(lp, 2200)
    e, (wg_up_p, wg_up_a, wg_out) = _mm_nt(
        hn, wt_e, BF16, "in_proj_gates", lp, E_COLS, tm, 512,
        gather=([w_up_pool[0].astype(BF16), w_up_attn[0].astype(BF16), w_out[0].astype(BF16)], [0, 0, 0]))
    wup_p = jnp.transpose(wg_up_p, (1, 0, 2)).reshape(POOL_WIDTH, D_MODEL)
    wup_a = jnp.transpose(wg_up_a, (1, 0, 2)).reshape(ATTN_WIDTH, D_MODEL)
    wout = wg_out.reshape(D_MODEL, D_MODEL)
    qkv = _mm_nt(hn, wt_qkv, BF16, "in_proj_qkv", lk, QKV_COLS, _row_tile(lk, 2600), 512, scale_first=HEAD_DIM ** -0.5)
    f = _mm_nt(hn, wt_f, F32, "in_proj_forget", lp, F_COLS, tm, F_COLS)
    c = _gates_fwd(f, bfg)
    c_t = jnp.transpose(c[:, :N_HEADS])
    bias = c_t[None, :, :] - jnp.transpose(c_t[:, ::ROW_TILE])[:, :, None]
    bias = jnp.where(jnp.arange(lp) < PAD_ROWS, -NEG, bias)
    bias = jnp.pad(bias, ((0, 0), (0, 0), (0, lk - lp)))
    by_kv = lambda t: jnp.transpose(bias.reshape(nb, N_HEADS // 2, 2, lk // t, t), (0, 1, 3, 2, 4))
    bias, bias_bwd = by_kv(KV_TILE), by_kv(KV_TILE_BWD)
    y_pool = _pool_fwd(e, pw_b, pool_scale)
    o, lse = _attn_fwd(qkv, bias, nb)
    merged, y_attn = _merge_fwd(y_pool, o, e, wup_p, wup_a)
    dh1, loss_part, dgf = _head_fwd_bwd(merged, wout, x2, metapad, gf, target)

    dap, daa, dgate, dza, do, delta, dy_pool = _merge_bwd(dh1, wout, y_pool, y_attn, wup_p, wup_a, e, o)
    dpc, dzp, dscale, dpw = _pool_bwd_local(e, dy_pool, pw_b, pool_scale)
    du = _pool_bwd_window(dpc)
    dq, dk, dv, dc4, dcq = _attn_bwd(qkv, do, lse, delta, bias_bwd, nb)
    dc = jnp.transpose(dc4, (1, 3, 0, 2)).reshape(lk, N_HEADS)[:lp]
    df, db = _gates_bwd(jnp.pad(dc, ((0, 0), (0, F_COLS - N_HEADS))), dcq, f, bfg)
    dproj = jnp.concatenate([du, dzp, dza, dgate, (dq * HEAD_DIM ** -0.5).astype(BF16), dk[:lp], dv[:lp]], axis=1)
    tk = _row_tile(lp, 1100)
    dw_out = _mm_tn(merged, dh1, "grad_w_out", lp, 512, D_MODEL, tk)
    dw_up_p = _mm_tn(y_pool, dap, "grad_w_up_pool", lp, 512, D_MODEL, lp, chunks=N_CHIPS)
    dw_up_a = _mm_tn(y_attn, daa, "grad_w_up_attn", lp, 512, D_MODEL, lp, chunks=N_CHIPS)
    dwt_f = _mm_tn(df, hn, "grad_w_in_forget", lp, F_COLS, D_MODEL, tk)

    def pad8(a):
        return jnp.pad(a, ((0, (-a.shape[0]) % 8), (0, 0)))

    small_parts = [pad8(a) for a in (dgf.reshape(-1, LANES), dscale.reshape(-1, LANES), db, dpw.reshape(-1, LANES),
                                     loss_part)]
    small = jnp.concatenate(small_parts, axis=0)
    soffs = [0]
    for p in small_parts:
        soffs.append(soffs[-1] + p.shape[0])

    gs_rows = [dw_up_p, dw_up_a, dw_out.reshape(N_CHIPS, sh_d, D_MODEL)]
    half = MAIN_COLS // 2
    dwt_a, (*rb_rows, rb_f, sr) = _mm_tn(dproj, hn, "grad_w_in_a", lp, half // 4, D_MODEL, lp, m=half, m_off=0,
                                         swap=(gs_rows + [dwt_f, small], [1, 1, 1, 1, None]))
    dwt_b, (rb_a,) = _mm_tn(dproj, hn, "grad_w_in_b", lp, half // 4, D_MODEL, lp, m=half, m_off=4, swap=([dwt_a], [1]))
    (rb_b,) = _sibling_exchange([dwt_b], [1])
    *hs_rows, sc = _add_sibling_half(gs_rows, rb_rows, [1, 1, 1], small, sr, core)
    h_a, h_b, h_f = _add_halves_2d([dwt_a, dwt_b, dwt_f], [rb_a, rb_b, rb_f], core)
    h_in = jnp.concatenate([h_a[E_U:E_ZA], h_b[D_MODEL:], h_a[E_ZA:E_GP], h_f[:N_HEADS], h_a[E_GP:], h_b[:D_MODEL]], axis=0)
    hs = [h_in.reshape(N_CHIPS, -1, D_MODEL // 2)] + hs_rows
    grad_axes = [2, 1, 1, 1]
    (grad_x, g_block0, dg1), qs, sq = _input_bwd(dproj, df, wt_e, wt_qkv, wt_f, x2, metapad, dh1, norm_g, hs, sc)
    g_w_in_rows, g_w_up_p, g_w_up_a, g_w_out, st = _reduce_allgather(qs, grad_axes, sq, [True, False, False, False])
    late = _small_allreduce(jnp.concatenate([pad8(dg1.reshape(-1, LANES)), g_block0[PAD_ROWS:].reshape(-1, LANES)], axis=0))
    n_norm = D_MODEL // LANES
    g_norm = late[:n_norm].reshape(1, D_MODEL)
    chip = 2 * lax.axis_index("x") + lax.axis_index("y")
    g_meta = lax.dynamic_slice_in_dim(late[n_norm:].reshape(N_META, D_MODEL), chip * sh_d, sh_d, axis=1)

    spiece = lambda k, rows: st[soffs[k]:soffs[k] + rows]
    g_final = spiece(0, D_MODEL // LANES).reshape(1, D_MODEL)
    g_scale = spiece(1, POOL_WIDTH // LANES).reshape(1, POOL_WIDTH)
    g_bf = spiece(2, 1)
    g_pw = spiece(3, POOL_WIDTH)
    loss = st[soffs[4], 0]

    def pad_lanes(a):
        return jnp.pad(a, ((0, 0), (0, F_COLS - N_HEADS)))

    w_in_res = (g_w_in_rows,) + _adamw_native(to_rows(w_in), g_w_in_rows, to_rows(m_w_in), to_rows(v_w_in), "adamw_w_in")

    upd = [
        ("meta_tokens", meta_tokens, g_meta, m_meta_tokens, v_meta_tokens),
        ("norm_g", norm_g, g_norm, m_norm_g, v_norm_g),
        ("w_in", None, None, None, None),
        ("b_forget", pad_lanes(b_forget), g_bf, pad_lanes(m_b_forget), pad_lanes(v_b_forget)),
        ("pool_w", pool_w.reshape(-1, LANES), g_pw, m_pool_w.reshape(-1, LANES), v_pool_w.reshape(-1, LANES)),
        ("pool_scale", pool_scale, g_scale, m_pool_scale, v_pool_scale),
        ("w_up_pool", w_up_pool[0], g_w_up_p, m_w_up_pool[0], v_w_up_pool[0]),
        ("w_up_attn", w_up_attn[0], g_w_up_a, m_w_up_attn[0], v_w_up_attn[0]),
        ("w_out", w_out[0], g_w_out, m_w_out[0], v_w_out[0]),
        ("final_norm_g", gf, g_final, m_final_norm_g.reshape(1, D_MODEL), v_final_norm_g.reshape(1, D_MODEL)),
    ]
    shapes = [meta_tokens.shape, norm_g.shape, w_in.shape, b_forget.shape, pool_w.shape, pool_scale.shape,
              w_up_pool.shape, w_up_attn.shape, w_out.shape, final_norm_g.shape]
    grads, deltas, new_ms, new_vs = [], [], [], []
    for (name, w_, g_, m_in, v_in), shp in zip(upd, shapes):
        if name == "w_in":
            res = tuple(from_rows(a) for a in w_in_res)
        else:
            d_, mn_, vn_ = _adamw(w_, g_, m_in, v_in, "adamw_" + name)
            res = (g_, d_, mn_, vn_)
        if name == "b_forget":
            res = tuple(a[:, :N_HEADS] for a in res)
        for lst, a in zip((grads, deltas, new_ms, new_vs), res):
            lst.append(a.reshape(shp))

    return (loss, grad_x.reshape(x.shape), *grads, *deltas, *new_ms, *new_vs)
```

```python
import jax
import jax.numpy as jnp
from jax import lax
from jax.experimental import pallas as pl
from jax.experimental.pallas import tpu as pltpu

F32 = jnp.float32
BF16 = jnp.bfloat16
MESH = pl.DeviceIdType.MESH
HIGHEST = lax.Precision.HIGHEST
HBM = pl.BlockSpec(memory_space=pltpu.HBM)

D_MODEL = 1024
N_META = 16
POOL_WIDTH = 512
POOL_GROUP = 128
POOL_WINDOWS = (2, 4, 8, 16)
N_HEADS = 8
HEAD_DIM = 64
ATTN_WIDTH = 512
RMS_EPS = 1e-6
N_CHIPS = 4

ADAM_LR = 0.001
ADAM_B1 = 0.9
ADAM_B2 = 0.999
ADAM_EPS = 1e-08
ADAM_WD = 0.01
ADAM_STEP = 10

LANES = 128
ROW_TILE = 256
KV_TILE = 1024
KV_TILE_BWD = 1024
PAD_ROWS = ROW_TILE - N_META
NEG = -1e30

E_U, E_ZP, E_ZA, E_GP, E_GA, E_COLS = 0, 512, 1024, 1536, 2560, 3584
QKV_COLS = 3 * ATTN_WIDTH
MAIN_COLS = E_COLS + QKV_COLS
F_COLS = LANES
REF_U, REF_Q, REF_ZA, REF_F, REF_GP = 0, 1024, 2560, 3072, 3080
VMEM_LIMIT = 48 * 1024 * 1024


def _params(sem=None, vmem=VMEM_LIMIT):
    return pltpu.CompilerParams(dimension_semantics=sem, vmem_limit_bytes=vmem)


def _in_hbm(*arrays):
    return [pltpu.with_memory_space_constraint(a, pltpu.HBM) for a in arrays]


def _row_tile(n, target):
    best = 16
    for t in range(16, target + 1, 16):
        if n % t == 0:
            best = t
    return best


def _sigmoid(x):
    return 1.0 / (1.0 + jnp.exp(-x))


def _half(ref, axis, which):
    n = ref.shape[axis] // 2
    idx = [slice(None)] * len(ref.shape)
    idx[axis] = pl.ds(pl.multiple_of(which * n, n), n)
    return ref.at[tuple(idx)]


def _half_shape(shape, axis):
    s = list(shape)
    s[axis] //= 2
    return tuple(s)


def _gather_start(ins, outs, axes, send, recv, fsend, frecv, local):
    x, y, c = lax.axis_index("x"), lax.axis_index("y"), lax.axis_index("c")
    me = 2 * x + y
    for t in range(len(ins)):
        pltpu.make_async_copy(ins[t], outs[t].at[me], local.at[t]).start()
        for k, chip in enumerate([(1 - x, y), (x, 1 - y), (1 - x, 1 - y)]):
            pltpu.make_async_remote_copy(
                src_ref=_half(ins[t], axes[t], c), dst_ref=_half(outs[t].at[me], axes[t], c),
                send_sem=send.at[3 * t + k], recv_sem=recv.at[3 * t + k], device_id=(*chip, c), device_id_type=MESH).start()


def _gather_finish(ins, outs, axes, send, recv, fsend, frecv, local):
    x, y, c = lax.axis_index("x"), lax.axis_index("y"), lax.axis_index("c")
    me = 2 * x + y
    sibling = (x, y, 1 - c)
    chips = [(1 - x, y), (x, 1 - y), (1 - x, 1 - y)]
    n = len(ins)

    def over_ici(t, k, chip, dst_slot):
        return pltpu.make_async_remote_copy(
            src_ref=_half(ins[t], axes[t], c), dst_ref=_half(outs[t].at[dst_slot], axes[t], c),
            send_sem=send.at[3 * t + k], recv_sem=recv.at[3 * t + k], device_id=(*chip, c), device_id_type=MESH)

    def to_sibling(t, k, slot, which):
        return pltpu.make_async_remote_copy(
            src_ref=_half(outs[t].at[slot], axes[t], which), dst_ref=_half(outs[t].at[slot], axes[t], which),
            send_sem=fsend.at[3 * t + k], recv_sem=frecv.at[3 * t + k], device_id=sibling, device_id_type=MESH)

    forwards = []
    for t in range(n):
        for k, (px, py) in enumerate(chips):
            over_ici(t, k, (px, py), 2 * px + py).wait_recv()
            fw = to_sibling(t, k, 2 * px + py, c)
            fw.start()
            forwards.append(fw)
    for t in range(n):
        for k, (px, py) in enumerate(chips):
            to_sibling(t, k, 2 * px + py, 1 - c).wait_recv()
    for t in range(n):
        for k, chip in enumerate(chips):
            over_ici(t, k, chip, me).wait_send()
    for fw in forwards:
        fw.wait_send()
    for t in range(n):
        pltpu.make_async_copy(ins[t], outs[t].at[me], local.at[t]).wait()


def _gather_sems(n):
    return [pltpu.SemaphoreType.DMA((3 * n,)), pltpu.SemaphoreType.DMA((3 * n,)), pltpu.SemaphoreType.DMA((3 * n,)),
            pltpu.SemaphoreType.DMA((3 * n,)), pltpu.SemaphoreType.DMA((n,))]


def _gathered_shapes(shards):
    return [jax.ShapeDtypeStruct((N_CHIPS,) + s.shape, s.dtype) for s in shards]


def _swap_copies(srcs, dsts, axes, send, recv):
    x, y, c = lax.axis_index("x"), lax.axis_index("y"), lax.axis_index("c")
    return [pltpu.make_async_remote_copy(
        src_ref=srcs[t] if axes[t] is None else _half(srcs[t], axes[t], 1 - c), dst_ref=dsts[t],
        send_sem=send.at[t], recv_sem=recv.at[t], device_id=(x, y, 1 - c), device_id_type=MESH) for t in range(len(srcs))]


def _swap_shapes(srcs, axes):
    return [jax.ShapeDtypeStruct(g.shape if a is None else _half_shape(g.shape, a), g.dtype) for g, a in zip(srcs, axes)]


def _sibling_exchange(gs, axes):
    n = len(gs)

    def body(*refs):
        cps = _swap_copies(refs[:n], refs[n:2 * n], axes, *refs[2 * n:])
        for cp in cps:
            cp.start()
        for cp in cps:
            cp.wait()

    return pl.pallas_call(
        body, name="grad_sibling_exchange",
        in_specs=[HBM] * n, out_specs=[HBM] * n, out_shape=_swap_shapes(gs, axes),
        scratch_shapes=[pltpu.SemaphoreType.DMA((n,)), pltpu.SemaphoreType.DMA((n,))],
    )(*_in_hbm(*gs))


def _add_halves_2d(parts, rbs, core):
    n = len(parts)
    rows, w = rbs[0].shape
    tr = 512
    last_rows = rbs[-1].shape[0]

    def body(core_ref, *refs):
        p_refs, r_refs, h_refs = refs[:n], refs[n:2 * n], refs[2 * n:]
        for t in range(n - 1):
            h_refs[t][...] = (p_refs[t][...] + r_refs[t][...]).astype(BF16)

        @pl.when(pl.program_id(0) == 0)
        def _():
            h_refs[n - 1][...] = (p_refs[n - 1][...] + r_refs[n - 1][...]).astype(BF16)

    mine = lambda r: pl.BlockSpec((r, w), lambda i, cr: (i, cr[0]))
    tile = lambda r: pl.BlockSpec((r, w), lambda i, cr: (i, 0))
    mine_small = pl.BlockSpec((last_rows, w), lambda i, cr: (0, cr[0]))
    tile_small = pl.BlockSpec((last_rows, w), lambda i, cr: (0, 0))
    return pl.pallas_call(
        body, name="grad_add_sibling_w_in",
        grid_spec=pltpu.PrefetchScalarGridSpec(
            num_scalar_prefetch=1, grid=(rows // tr,),
            in_specs=[mine(tr)] * (n - 1) + [mine_small] + [tile(tr)] * (n - 1) + [tile_small],
            out_specs=[tile(tr)] * (n - 1) + [tile_small]),
        out_shape=[jax.ShapeDtypeStruct(r.shape, BF16) for r in rbs],
        compiler_params=_params(("arbitrary",)),
    )(core, *parts, *rbs)


def _add_sibling_half(gs, rbs, axes, small, sr, core):
    n = len(gs)

    def body(core_ref, *refs):
        g_refs, rb_refs = refs[:n], refs[n:2 * n]
        s_ref, sr_ref = refs[2 * n], refs[2 * n + 1]
        h_refs, sc_ref = refs[2 * n + 2:3 * n + 2], refs[3 * n + 2]
        for t in range(n):
            h_refs[t][...] = (g_refs[t][...] + rb_refs[t][...]).astype(BF16)
        sc_ref[...] = s_ref[...] + sr_ref[...]

    def mine(rb, axis):
        blk = (None,) + rb.shape[1:]
        if axis == 2:
            return pl.BlockSpec(blk, lambda j, cr: (j, 0, cr[0]))
        return pl.BlockSpec(blk, lambda j, cr: (j, cr[0], 0))

    chunk = lambda rb: pl.BlockSpec((None,) + rb.shape[1:], lambda j, cr: (j, 0, 0))
    whole = pl.BlockSpec(small.shape, lambda j, cr: (0, 0))
    return pl.pallas_call(
        body, name="grad_add_sibling",
        grid_spec=pltpu.PrefetchScalarGridSpec(
            num_scalar_prefetch=1, grid=(N_CHIPS,),
            in_specs=[mine(rb, a) for rb, a in zip(rbs, axes)] + [chunk(rb) for rb in rbs] + [whole, whole],
            out_specs=[chunk(rb) for rb in rbs] + [whole]),
        out_shape=[jax.ShapeDtypeStruct(rb.shape, BF16) for rb in rbs] + [jax.ShapeDtypeStruct(small.shape, F32)],
        compiler_params=_params(("arbitrary",)),
    )(core, *gs, *rbs, small, sr)


def _chip_scatter_plan(h_refs, s_ref, q_refs, sq_ref, send, recv, local):
    n = len(h_refs)
    x, y, c = lax.axis_index("x"), lax.axis_index("y"), lax.axis_index("c")
    me = 2 * x + y
    chips = [(1 - x, y), (x, 1 - y), (1 - x, 1 - y)]

    def copy(t, k, chip, src_slot, dst_slot):
        src = h_refs[t].at[src_slot] if t < n else s_ref
        dst = (q_refs[t] if t < n else sq_ref).at[dst_slot]
        return pltpu.make_async_remote_copy(src_ref=src, dst_ref=dst, send_sem=send.at[3 * t + k],
                                            recv_sem=recv.at[3 * t + k], device_id=(*chip, c), device_id_type=MESH)

    own = [pltpu.make_async_copy(h_refs[t].at[me], q_refs[t].at[me], local.at[t]) for t in range(n)]
    own.append(pltpu.make_async_copy(s_ref, sq_ref.at[me], local.at[n]))
    sends = [copy(t, k, (px, py), 2 * px + py, me) for t in range(n + 1) for k, (px, py) in enumerate(chips)]
    recvs = [copy(t, k, (px, py), me, 2 * px + py) for t in range(n + 1) for k, (px, py) in enumerate(chips)]
    return own, sends, recvs


def _small_allreduce(buf):
    def body(b_ref, o_ref, sib_buf, chip_buf, send, recv):
        x, y, c = lax.axis_index("x"), lax.axis_index("y"), lax.axis_index("c")
        me = 2 * x + y
        chips = [(1 - x, y), (x, 1 - y), (1 - x, 1 - y)]
        swap = pltpu.make_async_remote_copy(src_ref=b_ref, dst_ref=sib_buf, send_sem=send.at[0], recv_sem=recv.at[0],
                                            device_id=(x, y, 1 - c), device_id_type=MESH)
        swap.start()
        swap.wait()
        chip_buf[me] = b_ref[...] + sib_buf[...]

        def copy(k, chip, slot):
            return pltpu.make_async_remote_copy(src_ref=chip_buf.at[slot], dst_ref=chip_buf.at[slot], send_sem=send.at[1 + k],
                                                recv_sem=recv.at[1 + k], device_id=(*chip, c), device_id_type=MESH)

        sends = [copy(k, chip, me) for k, chip in enumerate(chips)]
        for cp in sends:
            cp.start()
        for k, (px, py) in enumerate(chips):
            copy(k, (px, py), 2 * px + py).wait_recv()
        for cp in sends:
            cp.wait_send()
        o_ref[...] = ((chip_buf[0] + chip_buf[1]) + chip_buf[2]) + chip_buf[3]

    vmem = pl.BlockSpec(memory_space=pltpu.VMEM)
    return pl.pallas_call(
        body, name="small_allreduce", in_specs=[vmem], out_specs=vmem,
        out_shape=jax.ShapeDtypeStruct(buf.shape, F32),
        scratch_shapes=[pltpu.VMEM(buf.shape, F32), pltpu.VMEM((N_CHIPS,) + buf.shape, F32),
                        pltpu.SemaphoreType.DMA((4,)), pltpu.SemaphoreType.DMA((4,))],
        compiler_params=_params(),
    )(buf)


def _reduce_allgather(qs, axes, sq, as_rows):
    n = len(qs)
    shard_shapes, half_axes = [], []
    for q, a, rows_form in zip(qs, axes, as_rows):
        shape = [d * 2 if i == a - 1 else d for i, d in enumerate(q.shape[1:])]
        if rows_form:
            assert a == 2
            shape = [shape[0], 1, shape[1]]
        shard_shapes.append(tuple(shape))
        half_axes.append(2 if rows_form else a - 1)

    def body(*refs):
        q_refs, sq_ref = refs[:n], refs[n]
        o_refs, st_ref = refs[n + 1:2 * n + 1], refs[2 * n + 1]
        send, recv = refs[2 * n + 2:]
        x, y, c = lax.axis_index("x"), lax.axis_index("y"), lax.axis_index("c")

        def swap(t, which):
            return pltpu.make_async_remote_copy(
                src_ref=_half(o_refs[t], half_axes[t], which), dst_ref=_half(o_refs[t], half_axes[t], which),
                send_sem=send.at[t], recv_sem=recv.at[t], device_id=(x, y, 1 - c), device_id_type=MESH)

        sent = []
        for t in range(n):
            q = q_refs[t]
            total = ((q[0].astype(F32) + q[1].astype(F32)) + q[2].astype(F32)) + q[3].astype(F32)
            if as_rows[t]:
                total = total.reshape(total.shape[0], 1, total.shape[1])
            _half(o_refs[t], half_axes[t], c)[...] = total
            cp = swap(t, c)
            cp.start()
            sent.append(cp)
        st_ref[...] = ((sq_ref[0] + sq_ref[1]) + sq_ref[2]) + sq_ref[3]
        for t in range(n):
            swap(t, 1 - c).wait_recv()
        for cp in sent:
            cp.wait_send()

    vmem = pl.BlockSpec(memory_space=pltpu.VMEM)
    return pl.pallas_call(
        body, name="grad_reduce_allgather",
        in_specs=[vmem] * (n + 1), out_specs=[vmem] * (n + 1),
        out_shape=[jax.ShapeDtypeStruct(s, F32) for s in shard_shapes] + [jax.ShapeDtypeStruct(sq.shape[1:], F32)],
        scratch_shapes=[pltpu.SemaphoreType.DMA((n,)), pltpu.SemaphoreType.DMA((n,))],
        compiler_params=_params(),
    )(*qs, sq)


def _dot_nt(a, b):
    return lax.dot_general(a, b, (((1,), (1,)), ((), ())), preferred_element_type=F32)


def _dot_tn(a, b):
    return lax.dot_general(a, b, (((0,), (0,)), ((), ())), preferred_element_type=F32)


def _mm_nt(a, bt, out_dtype, name, m, n, tm, tn, row_block=0, scale_first=None, gather=None):
    k = a.shape[1]
    shards, axes = gather if gather is not None else ([], [])
    ng = len(shards)
    grid = (m // tm, n // tn)

    def body(a_ref, b_ref, *rest):
        ins, o_ref, outs, sems = rest[:ng], rest[ng], rest[ng + 1:2 * ng + 1], rest[2 * ng + 1:]
        first = (pl.program_id(0) == 0) & (pl.program_id(1) == 0)
        last = (pl.program_id(0) == grid[0] - 1) & (pl.program_id(1) == grid[1] - 1)
        if ng:
            @pl.when(first)
            def _():
                _gather_start(ins, outs, axes, *sems)

        r = _dot_nt(a_ref[...], b_ref[...])
        if scale_first is not None:
            r = r * jnp.where(pl.program_id(1) == 0, scale_first, 1.0)
        o_ref[...] = r.astype(out_dtype)

        if ng:
            @pl.when(last)
            def _():
                _gather_finish(ins, outs, axes, *sems)

    res = pl.pallas_call(
        body, name=name, grid=grid,
        in_specs=[pl.BlockSpec((tm, k), lambda i, j: (i, 0)), pl.BlockSpec((tn, k), lambda i, j: (row_block + j, 0))]
        + [HBM] * ng,
        out_specs=[pl.BlockSpec((tm, tn), lambda i, j: (i, j))] + [HBM] * ng,
        out_shape=[jax.ShapeDtypeStruct((m, n), out_dtype)] + _gathered_shapes(shards),
        scratch_shapes=_gather_sems(ng) if ng else [],
        compiler_params=_params(("arbitrary", "arbitrary") if ng else ("parallel", "parallel")),
    )(a, bt, *_in_hbm(*shards))
    return (res[0], res[1:]) if ng else res[0]


def _mm_tn(a, b, name, k, tm, tn, tk, chunks=1, m=None, m_off=0, swap=None):
    m = a.shape[1] if m is None else m
    n = b.shape[1]
    cw = n // chunks
    srcs, axes = swap if swap is not None else ([], [])
    ns = len(srcs)
    grid = (m // tm, n // tn, k // tk)

    def body(a_ref, b_ref, *rest):
        s_refs, o_ref, d_refs, sems = rest[:ns], rest[ns], rest[ns + 1:2 * ns + 1], rest[2 * ns + 1:]
        ids = [pl.program_id(d) for d in range(3)]
        if ns:
            @pl.when((ids[0] == 0) & (ids[1] == 0) & (ids[2] == 0))
            def _():
                for cp in _swap_copies(s_refs, d_refs, axes, *sems):
                    cp.start()

        @pl.when(ids[2] == 0)
        def _():
            o_ref[...] = jnp.zeros_like(o_ref)
        r = _dot_tn(a_ref[...].astype(BF16), b_ref[...].astype(BF16))
        if chunks > 1:
            for c in range(chunks):
                o_ref[c] += r[:, c * cw:(c + 1) * cw]
        else:
            o_ref[...] += r

        if ns:
            @pl.when((ids[0] == grid[0] - 1) & (ids[1] == grid[1] - 1) & (ids[2] == grid[2] - 1))
            def _():
                for cp in _swap_copies(s_refs, d_refs, axes, *sems):
                    cp.wait()

    if chunks > 1:
        assert tn == n
        out_spec = pl.BlockSpec((chunks, tm, cw), lambda i, j, kk: (0, i, 0))
        out_shape = jax.ShapeDtypeStruct((chunks, m, cw), F32)
    else:
        out_spec = pl.BlockSpec((tm, tn), lambda i, j, kk: (i, j))
        out_shape = jax.ShapeDtypeStruct((m, n), F32)
    res = pl.pallas_call(
        body, name=name, grid=grid,
        in_specs=[pl.BlockSpec((tk, tm), lambda i, j, kk: (kk, m_off + i)), pl.BlockSpec((tk, tn), lambda i, j, kk: (kk, j))]
        + [HBM] * ns,
        out_specs=[out_spec] + [HBM] * ns, out_shape=[out_shape] + _swap_shapes(srcs, axes),
        scratch_shapes=[pltpu.SemaphoreType.DMA((ns,)), pltpu.SemaphoreType.DMA((ns,))] if ns else [],
        compiler_params=_params(("arbitrary",) * 3 if ns else ("parallel", "parallel", "arbitrary")),
    )(a, b, *_in_hbm(*srcs))
    return (res[0], res[1:]) if ns else res[0]


def _tokens_spec():
    return pl.BlockSpec((ROW_TILE, D_MODEL), lambda i: (jnp.maximum(i - 1, 0), 0))


def _rmsnorm_fwd(x2, g1, lk, shards, axes):
    nb = x2.shape[0] // ROW_TILE + 1
    nblk = lk // ROW_TILE
    ng = len(shards)
    meta_cols = shards[-1].shape[1]

    def body(x_ref, g_ref, *rest):
        ins, hn_ref, outs = rest[:ng], rest[ng], rest[ng + 1:2 * ng + 1]
        sems, meta_buf, meta_sem = rest[2 * ng + 1:2 * ng + 6], rest[2 * ng + 6], rest[2 * ng + 7]
        s = pl.program_id(0)
        blk = (s + 1) % nblk

        @pl.when(s == 0)
        def _():
            _gather_start(ins, outs, axes, *sems)

        def normed(h):
            r = lax.rsqrt(jnp.mean(h * h, axis=-1, keepdims=True) + RMS_EPS)
            return ((h * r) * g_ref[...]).astype(BF16)

        @pl.when(s < nblk - 1)
        def _():
            hn_ref[...] = normed(jnp.where(blk >= nb, 0.0, x_ref[...]))

        @pl.when(s == nblk - 1)
        def _():
            _gather_finish(ins, outs, axes, *sems)
            fetch = pltpu.make_async_copy(outs[-1], meta_buf, meta_sem.at[0])
            fetch.start()
            fetch.wait()
            meta = jnp.concatenate([meta_buf[j] for j in range(N_CHIPS)], axis=1)
            hn_ref[...] = normed(jnp.concatenate([jnp.zeros((PAD_ROWS, D_MODEL), F32), meta], axis=0))

    res = pl.pallas_call(
        body, name="rmsnorm_fwd", grid=(nblk,),
        in_specs=[pl.BlockSpec((ROW_TILE, D_MODEL), lambda s: (jnp.clip((s + 1) % nblk - 1, 0, nb - 2), 0)),
                  pl.BlockSpec((1, D_MODEL), lambda s: (0, 0))] + [HBM] * ng,
        out_specs=[pl.BlockSpec((ROW_TILE, D_MODEL), lambda s: ((s + 1) % nblk, 0))] + [HBM] * ng,
        out_shape=[jax.ShapeDtypeStruct((lk, D_MODEL), BF16)] + _gathered_shapes(shards),
        scratch_shapes=_gather_sems(ng) + [pltpu.VMEM((N_CHIPS, N_META, meta_cols), F32), pltpu.SemaphoreType.DMA((1,))],
        compiler_params=_params(("arbitrary",)),
    )(x2, g1, *_in_hbm(*shards))
    return res[0], res[1:]


def _valid_gate_mask(i):
    row = i * ROW_TILE + lax.broadcasted_iota(jnp.int32, (ROW_TILE, F_COLS), 0)
    col = lax.broadcasted_iota(jnp.int32, (ROW_TILE, F_COLS), 1)
    return (row >= PAD_ROWS) & (col < N_HEADS)


def _gates_fwd(f, bfg):
    nb = f.shape[0] // ROW_TILE

    def body(f_ref, b_ref, c_ref, carry):
        i = pl.program_id(0)

        @pl.when(i == 0)
        def _():
            carry[...] = jnp.zeros_like(carry)

        logit = f_ref[...] + b_ref[...]
        lf = jnp.minimum(logit, 0.0) - jnp.log1p(jnp.exp(-jnp.abs(logit)))
        lf = jnp.where(_valid_gate_mask(i), lf, 0.0)
        r_i = lax.broadcasted_iota(jnp.int32, (ROW_TILE, ROW_TILE), 0)
        c_i = lax.broadcasted_iota(jnp.int32, (ROW_TILE, ROW_TILE), 1)
        tri = (c_i <= r_i).astype(F32)
        c_ref[...] = jnp.dot(tri, lf, precision=HIGHEST, preferred_element_type=F32) + carry[...]
        carry[...] = carry[...] + jnp.sum(lf, axis=0, keepdims=True)

    return pl.pallas_call(
        body, name="gates_fwd", grid=(nb,),
        in_specs=[pl.BlockSpec((ROW_TILE, F_COLS), lambda i: (i, 0)), pl.BlockSpec((1, F_COLS), lambda i: (0, 0))],
        out_specs=pl.BlockSpec((ROW_TILE, F_COLS), lambda i: (i, 0)),
        out_shape=jax.ShapeDtypeStruct(f.shape, F32),
        scratch_shapes=[pltpu.VMEM((1, F_COLS), F32)],
        compiler_params=_params(("arbitrary",)),
    )(f, bfg)


def _pool_counts(i):
    row = i * ROW_TILE + lax.broadcasted_iota(jnp.int32, (ROW_TILE, 1), 0)
    return jnp.maximum(row - PAD_ROWS, 0)


def _trailing_sums(xc, levels):
    acc = xc
    for lv in range(levels):
        acc = acc + pltpu.roll(acc, 1 << lv, 0)
    return acc


def _leading_sums(xc, levels):
    n = xc.shape[0]
    acc = xc
    for lv in range(levels):
        acc = acc + pltpu.roll(acc, n - (1 << lv), 0)
    return acc


def _pool_p(u_cur, u_prev, i):
    pos = _pool_counts(i)
    ps, invs = [], []
    for g, w in enumerate(POOL_WINDOWS):
        sl = slice(g * POOL_GROUP, (g + 1) * POOL_GROUP)
        cur = u_cur[:, sl]
        xc = jnp.concatenate([u_prev[:, sl], cur], axis=0)
        win = _trailing_sums(xc, g + 1)[ROW_TILE:, :]
        inv = 1.0 / jnp.minimum(pos + 1, w).astype(F32)
        ps.append(win * inv - cur)
        invs.append(inv)
    return ps, invs


def _pool_fwd(e, pw, scale):
    nb = e.shape[0] // ROW_TILE

    def body(uc_ref, up_ref, z_ref, pw_ref, sc_ref, y_ref):
        i = pl.program_id(0)
        u_cur = uc_ref[...].astype(F32)
        u_prev = jnp.where(i == 0, 0.0, up_ref[...].astype(F32))
        ps, _ = _pool_p(u_cur, u_prev, i)
        z = z_ref[...].astype(F32)
        gate = z * _sigmoid(z)
        for g in range(len(POOL_WINDOWS)):
            sl = slice(g * POOL_GROUP, (g + 1) * POOL_GROUP)
            yraw = jnp.dot(ps[g].astype(BF16), pw_ref[g], preferred_element_type=F32)
            y_ref[:, sl] = ((yraw * sc_ref[:, sl]) * gate[:, sl]).astype(BF16)

    blk = (ROW_TILE, POOL_WIDTH)
    return pl.pallas_call(
        body, name="pool_fwd", grid=(nb,),
        in_specs=[pl.BlockSpec(blk, lambda i: (i, 0)), pl.BlockSpec(blk, lambda i: (jnp.maximum(i - 1, 0), 0)),
                  pl.BlockSpec(blk, lambda i: (i, 1)),
                  pl.BlockSpec((len(POOL_WINDOWS), POOL_GROUP, POOL_GROUP), lambda i: (0, 0, 0)),
                  pl.BlockSpec((1, POOL_WIDTH), lambda i: (0, 0))],
        out_specs=pl.BlockSpec(blk, lambda i: (i, 0)),
        out_shape=jax.ShapeDtypeStruct((e.shape[0], POOL_WIDTH), BF16),
        compiler_params=_params(("parallel",)),
    )(e, e, e, pw, scale)


def _stack_heads(a):
    first = lax.broadcasted_iota(jnp.int32, a.shape, 1) < HEAD_DIM
    zero = jnp.zeros_like(a)
    return jnp.concatenate([jnp.where(first, a, zero), jnp.where(first, zero, a)], axis=0)


def _unstack_heads(a):
    rows = a.shape[0] // 2
    first = lax.broadcasted_iota(jnp.int32, (rows, LANES), 1) < HEAD_DIM
    return jnp.where(first, a[:rows], a[rows:])


def _causal(i, jj, stacked, kv_tile):
    r = lax.broadcasted_iota(jnp.int32, (stacked * ROW_TILE, kv_tile), 0)
    if stacked == 2:
        r = jnp.where(r >= ROW_TILE, r - ROW_TILE, r)
    kidx = jj * kv_tile + lax.broadcasted_iota(jnp.int32, (stacked * ROW_TILE, kv_tile), 1)
    return kidx <= i * ROW_TILE + r


def _stack_rows(b):
    n = b.shape[1]
    return jnp.concatenate([jnp.broadcast_to(b[0:1], (ROW_TILE, n)), jnp.broadcast_to(b[1:2], (ROW_TILE, n))], axis=0)


def _attn_fwd(qkv, bias, nb):
    lk = qkv.shape[0]
    lp = nb * ROW_TILE
    nkb = lk // KV_TILE
    n_pairs = N_HEADS // 2

    def body(q_ref, k_ref, v_ref, b_ref, o_ref, lse_ref):
        i = pl.program_id(1)
        qs = _stack_heads(q_ref[...])
        last = (i * ROW_TILE) // KV_TILE

        def block(jj, carry, masked, n_keys=KV_TILE):
            m, l, acc = carry
            rows = pl.ds(pl.multiple_of(jj * KV_TILE, KV_TILE), n_keys)
            s = _dot_nt(qs, k_ref[rows, :]) - _stack_rows(b_ref[0, 0, jj][:, :n_keys])
            if masked:
                s = jnp.where(_causal(i, jj, 2, KV_TILE)[:, :n_keys], s, NEG)
            m_new = jnp.maximum(m, jnp.max(s, axis=1, keepdims=True))
            alpha = jnp.exp(m - m_new)
            p = jnp.exp(s - m_new)
            l = alpha * l + jnp.sum(p, axis=1, keepdims=True)
            acc = alpha * acc + jnp.dot(p.astype(BF16), v_ref[rows, :], preferred_element_type=F32)
            return m_new, l, acc

        init = (jnp.full((2 * ROW_TILE, 1), NEG, F32), jnp.zeros((2 * ROW_TILE, 1), F32),
                jnp.zeros((2 * ROW_TILE, LANES), F32))
        carry = lax.fori_loop(0, last, lambda jj, c: block(jj, c, False), init)
        ends = [lambda c, n=(r + 1) * ROW_TILE: block(last, c, True, n) for r in range(KV_TILE // ROW_TILE)]
        m, l, acc = lax.switch(i - last * (KV_TILE // ROW_TILE), ends, carry)
        o_ref[...] = _unstack_heads(acc / l)
        lse_ref[...] = _unstack_heads(jnp.broadcast_to(m + jnp.log(l), (2 * ROW_TILE, LANES)))

    return pl.pallas_call(
        body, name="attn_fwd", grid=(n_pairs, nb),
        in_specs=[pl.BlockSpec((ROW_TILE, LANES), lambda hp, i: (i, hp)),
                  pl.BlockSpec((lk, LANES), lambda hp, i: (0, n_pairs + hp)),
                  pl.BlockSpec((lk, LANES), lambda hp, i: (0, 2 * n_pairs + hp)),
                  pl.BlockSpec((1, 1, nkb, 2, KV_TILE), lambda hp, i: (i, hp, 0, 0, 0))],
        out_specs=[pl.BlockSpec((ROW_TILE, LANES), lambda hp, i: (i, hp)),
                   pl.BlockSpec((ROW_TILE, LANES), lambda hp, i: (i, hp))],
        out_shape=[jax.ShapeDtypeStruct((lp, ATTN_WIDTH), F32), jax.ShapeDtypeStruct((lp, ATTN_WIDTH), F32)],
        compiler_params=_params(("parallel", "parallel")),
    )(qkv, qkv, qkv, bias)


def _merge_fwd(y_pool, o, e, wup_p, wup_a):
    lp = o.shape[0]
    nb = lp // ROW_TILE

    def body(yp_ref, o_ref, e_ref, wp_ref, wa_ref, mg_ref, ya_ref):
        za = e_ref[:, E_ZA:E_GP].astype(F32)
        ya = (o_ref[...] * (za * _sigmoid(za))).astype(BF16)
        ya_ref[...] = ya
        a_pool = jnp.dot(yp_ref[...], wp_ref[...], preferred_element_type=F32)
        a_attn = jnp.dot(ya, wa_ref[...], preferred_element_type=F32)
        mg_ref[...] = (_sigmoid(e_ref[:, E_GP:E_GA].astype(F32)) * a_pool
                       + _sigmoid(e_ref[:, E_GA:E_COLS].astype(F32)) * a_attn).astype(BF16)

    return pl.pallas_call(
        body, name="merge_fwd", grid=(nb,),
        in_specs=[pl.BlockSpec((ROW_TILE, POOL_WIDTH), lambda i: (i, 0)),
                  pl.BlockSpec((ROW_TILE, ATTN_WIDTH), lambda i: (i, 0)),
                  pl.BlockSpec((ROW_TILE, E_COLS), lambda i: (i, 0)),
                  pl.BlockSpec((POOL_WIDTH, D_MODEL), lambda i: (0, 0)),
                  pl.BlockSpec((ATTN_WIDTH, D_MODEL), lambda i: (0, 0))],
        out_specs=[pl.BlockSpec((ROW_TILE, D_MODEL), lambda i: (i, 0)),
                   pl.BlockSpec((ROW_TILE, ATTN_WIDTH), lambda i: (i, 0))],
        out_shape=[jax.ShapeDtypeStruct((lp, D_MODEL), BF16), jax.ShapeDtypeStruct((lp, ATTN_WIDTH), BF16)],
        compiler_params=_params(("parallel",)),
    )(y_pool, o, e, wup_p, wup_a)


def _head_fwd_bwd(merged, w_out, x2, metapad, gf, target):
    lp = merged.shape[0]
    nb = lp // ROW_TILE

    def body(mg_ref, w_ref, x_ref, mp_ref, g_ref, t_ref, dh_ref, loss_ref, dg_ref):
        i = pl.program_id(0)

        @pl.when(i == 0)
        def _():
            loss_ref[...] = jnp.zeros_like(loss_ref)
            dg_ref[...] = jnp.zeros_like(dg_ref)

        h0 = jnp.where(i == 0, mp_ref[...], x_ref[...])
        h1 = h0 + jnp.dot(mg_ref[...], w_ref[...], preferred_element_type=F32)
        r = lax.rsqrt(jnp.mean(h1 * h1, axis=-1, keepdims=True) + RMS_EPS)
        xhat = h1 * r
        g = g_ref[...]
        err = jnp.where(i == 0, 0.0, xhat * g - t_ref[...])
        loss_ref[...] += 0.5 * jnp.sum(jnp.mean(err * err, axis=-1, keepdims=True))
        dy = err / D_MODEL
        dg_ref[...] += jnp.sum(dy * xhat, axis=0, keepdims=True)
        dxhat = dy * g
        dh_ref[...] = r * (dxhat - xhat * jnp.mean(dxhat * xhat, axis=-1, keepdims=True))

    return pl.pallas_call(
        body, name="head_fwd_bwd", grid=(nb,),
        in_specs=[pl.BlockSpec((ROW_TILE, D_MODEL), lambda i: (i, 0)),
                  pl.BlockSpec((D_MODEL, D_MODEL), lambda i: (0, 0)),
                  _tokens_spec(), pl.BlockSpec((ROW_TILE, D_MODEL), lambda i: (0, 0)),
                  pl.BlockSpec((1, D_MODEL), lambda i: (0, 0)), _tokens_spec()],
        out_specs=[pl.BlockSpec((ROW_TILE, D_MODEL), lambda i: (i, 0)),
                   pl.BlockSpec((1, LANES), lambda i: (0, 0)), pl.BlockSpec((1, D_MODEL), lambda i: (0, 0))],
        out_shape=[jax.ShapeDtypeStruct((lp, D_MODEL), F32), jax.ShapeDtypeStruct((1, LANES), F32),
                   jax.ShapeDtypeStruct((1, D_MODEL), F32)],
        compiler_params=_params(("arbitrary",)),
    )(merged, w_out, x2, metapad, gf, target)


def _per_head_rowsum(t):
    head = lax.broadcasted_iota(jnp.int32, t.shape, 1) // HEAD_DIM
    out = jnp.zeros_like(t)
    for h in range(N_HEADS):
        sel = head == h
        out = jnp.where(sel, jnp.sum(jnp.where(sel, t, 0.0), axis=1, keepdims=True), out)
    return out


def _merge_bwd(dh1, w_out, y_pool, y_attn, wup_p, wup_a, e, o):
    lp = o.shape[0]
    nb = lp // ROW_TILE

    def body(dh_ref, wo_ref, yp_ref, ya_ref, wp_ref, wa_ref, e_ref, o_ref,
             dap_ref, daa_ref, dg_ref, do_ref, delta_ref, dyp_ref):
        dmerged = _dot_nt(dh_ref[...].astype(BF16), wo_ref[...])
        a_pool = jnp.dot(yp_ref[...], wp_ref[...], preferred_element_type=F32)
        a_attn = jnp.dot(ya_ref[...], wa_ref[...], preferred_element_type=F32)
        sp = _sigmoid(e_ref[:, E_GP:E_GA].astype(F32))
        sa = _sigmoid(e_ref[:, E_GA:E_COLS].astype(F32))
        dap = (dmerged * sp).astype(BF16)
        daa = (dmerged * sa).astype(BF16)
        dap_ref[...] = dap
        daa_ref[...] = daa
        dg_ref[:, ATTN_WIDTH:ATTN_WIDTH + D_MODEL] = (dmerged * a_pool * (sp * (1.0 - sp))).astype(BF16)
        dg_ref[:, ATTN_WIDTH + D_MODEL:] = (dmerged * a_attn * (sa * (1.0 - sa))).astype(BF16)
        dyp_ref[...] = _dot_nt(dap, wp_ref[...])
        dya = _dot_nt(daa, wa_ref[...])
        za = e_ref[:, E_ZA:E_GP].astype(F32)
        sz = _sigmoid(za)
        o = o_ref[...]
        do = dya * (za * sz)
        do_ref[...] = do.astype(BF16)
        dg_ref[:, :ATTN_WIDTH] = (dya * o * (sz * (1.0 + za * (1.0 - sz)))).astype(BF16)
        delta_ref[...] = _per_head_rowsum(do * o)

    row = lambda w: pl.BlockSpec((ROW_TILE, w), lambda i: (i, 0))
    full = lambda a: pl.BlockSpec(a.shape, lambda i: (0, 0))
    return pl.pallas_call(
        body, name="merge_bwd", grid=(nb,),
        in_specs=[row(D_MODEL), full(w_out), row(POOL_WIDTH), row(ATTN_WIDTH), full(wup_p), full(wup_a),
                  row(E_COLS), row(ATTN_WIDTH)],
        out_specs=[row(D_MODEL), row(D_MODEL), pl.BlockSpec((ROW_TILE, MAIN_COLS // 2), lambda i: (i, 1)),
                   row(ATTN_WIDTH), row(ATTN_WIDTH), row(POOL_WIDTH)],
        out_shape=[jax.ShapeDtypeStruct((lp, D_MODEL), BF16), jax.ShapeDtypeStruct((lp, D_MODEL), BF16),
                   jax.ShapeDtypeStruct((lp, MAIN_COLS), BF16),
                   jax.ShapeDtypeStruct((lp, ATTN_WIDTH), BF16), jax.ShapeDtypeStruct((lp, ATTN_WIDTH), F32),
                   jax.ShapeDtypeStruct((lp, POOL_WIDTH), F32)],
        compiler_params=_params(("parallel",)),
    )(dh1, w_out, y_pool, y_attn, wup_p, wup_a, e, o)


def _pool_bwd_local(e, dy_pool, pw, scale):
    lp = e.shape[0]
    nb = lp // ROW_TILE
    ng = len(POOL_WINDOWS)

    def body(uc_ref, up_ref, z_ref, dy_ref, pw_ref, sc_ref, dpc_ref, dz_ref, dsc_ref, dpw_ref):
        i = pl.program_id(0)

        @pl.when(i == 0)
        def _():
            dsc_ref[...] = jnp.zeros_like(dsc_ref)
            dpw_ref[...] = jnp.zeros_like(dpw_ref)

        u_cur = uc_ref[...].astype(F32)
        u_prev = jnp.where(i == 0, 0.0, up_ref[...].astype(F32))
        ps, invs = _pool_p(u_cur, u_prev, i)
        z = z_ref[...].astype(F32)
        sz = _sigmoid(z)
        dy = dy_ref[...]
        dypre = dy * (z * sz)
        dsilu = sz * (1.0 + z * (1.0 - sz))
        for g in range(ng):
            sl = slice(g * POOL_GROUP, (g + 1) * POOL_GROUP)
            pb = ps[g].astype(BF16)
            w = pw_ref[g]
            yraw = jnp.dot(pb, w, preferred_element_type=F32)
            sc = sc_ref[:, sl]
            dz_ref[:, sl] = (dy[:, sl] * (yraw * sc) * dsilu[:, sl]).astype(BF16)
            dsc_ref[:, sl] += jnp.sum(dypre[:, sl] * yraw, axis=0, keepdims=True)
            dyraw = (dypre[:, sl] * sc).astype(BF16)
            dpw_ref[g] += _dot_tn(pb, dyraw)
            dpc_ref[:, sl] = _dot_nt(dyraw, w) * invs[g]

    blk = (ROW_TILE, POOL_WIDTH)
    return pl.pallas_call(
        body, name="pool_bwd_local", grid=(nb,),
        in_specs=[pl.BlockSpec(blk, lambda i: (i, 0)), pl.BlockSpec(blk, lambda i: (jnp.maximum(i - 1, 0), 0)),
                  pl.BlockSpec(blk, lambda i: (i, 1)), pl.BlockSpec(blk, lambda i: (i, 0)),
                  pl.BlockSpec((ng, POOL_GROUP, POOL_GROUP), lambda i: (0, 0, 0)),
                  pl.BlockSpec((1, POOL_WIDTH), lambda i: (0, 0))],
        out_specs=[pl.BlockSpec(blk, lambda i: (i, 0)), pl.BlockSpec(blk, lambda i: (i, 0)),
                   pl.BlockSpec((1, POOL_WIDTH), lambda i: (0, 0)),
                   pl.BlockSpec((ng, POOL_GROUP, POOL_GROUP), lambda i: (0, 0, 0))],
        out_shape=[jax.ShapeDtypeStruct((lp, POOL_WIDTH), F32), jax.ShapeDtypeStruct((lp, POOL_WIDTH), BF16),
                   jax.ShapeDtypeStruct((1, POOL_WIDTH), F32),
                   jax.ShapeDtypeStruct((ng, POOL_GROUP, POOL_GROUP), F32)],
        compiler_params=_params(("arbitrary",)),
    )(e, e, e, dy_pool, pw, scale)


def _pool_bwd_window(dpc):
    lp = dpc.shape[0]
    nb = lp // ROW_TILE

    def body(cur_ref, nxt_ref, du_ref):
        i = pl.program_id(0)
        cur = cur_ref[...]
        nxt = jnp.where(i == nb - 1, 0.0, nxt_ref[...])
        pos = _pool_counts(i)
        for g, w in enumerate(POOL_WINDOWS):
            sl = slice(g * POOL_GROUP, (g + 1) * POOL_GROUP)
            xc = jnp.concatenate([cur[:, sl], nxt[:, sl]], axis=0)
            win = _leading_sums(xc, g + 1)[:ROW_TILE, :]
            dp = cur[:, sl] * jnp.minimum(pos + 1, w).astype(F32)
            du_ref[:, sl] = (win - dp).astype(BF16)

    blk = (ROW_TILE, POOL_WIDTH)
    return pl.pallas_call(
        body, name="pool_bwd_window", grid=(nb,),
        in_specs=[pl.BlockSpec(blk, lambda i: (i, 0)), pl.BlockSpec(blk, lambda i: (jnp.minimum(i + 1, nb - 1), 0))],
        out_specs=pl.BlockSpec(blk, lambda i: (i, 0)),
        out_shape=jax.ShapeDtypeStruct((lp, POOL_WIDTH), BF16),
        compiler_params=_params(("parallel",)),
    )(dpc, dpc)


def _attn_bwd(qkv, do, lse, delta, bias, nb):
    lk = qkv.shape[0]
    lp = nb * ROW_TILE
    nkb = lk // KV_TILE_BWD
    n_pairs = N_HEADS // 2
    per_kv = KV_TILE_BWD // ROW_TILE

    def body(q_ref, k_ref, v_ref, do_ref, lse_ref, dl_ref, b_ref, dq_ref, dk_ref, dv_ref, dc_ref, dcq_ref,
             dk_acc, dv_acc, dc_acc):
        jj = pl.program_id(1)

        @pl.when(jj == 0)
        def _():
            dq_ref[...] = jnp.zeros_like(dq_ref)
            dcq_ref[...] = jnp.zeros_like(dcq_ref)

        dk_acc[...] = jnp.zeros_like(dk_acc)
        dv_acc[...] = jnp.zeros_like(dv_acc)
        dc_acc[...] = jnp.zeros_like(dc_acc)

        def block(i, n_keys, masked):
            kb, vb = k_ref[:n_keys, :], v_ref[:n_keys, :]
            rows = pl.ds(pl.multiple_of(i * ROW_TILE, ROW_TILE), ROW_TILE)
            qs = _stack_heads(q_ref[rows, :])
            dos = _stack_heads(do_ref[rows, :])
            lse_i, dl_i = lse_ref[rows, :], dl_ref[rows, :]
            s = _dot_nt(qs, kb)
            dp = _dot_nt(dos, vb)
            if masked:
                valid = _causal(i, jj, 1, KV_TILE_BWD)[:, :n_keys]
            ps, dss, dcs, rowsums = [], [], [], []
            for hd in range(2):
                half = slice(hd * ROW_TILE, (hd + 1) * ROW_TILE)
                col = slice(hd * HEAD_DIM, hd * HEAD_DIM + 1)
                sh = s[half] - b_ref[i, 0, 0, hd:hd + 1, :n_keys]
                if masked:
                    sh = jnp.where(valid, sh, NEG)
                p = jnp.exp(sh - lse_i[:, col])
                ds = p * (dp[half] - dl_i[:, col])
                ps.append(p.astype(BF16))
                dss.append(ds.astype(BF16))
                dcs.append(jnp.sum(ds, axis=0, keepdims=True))
                rowsums.append(jnp.sum(ds, axis=1, keepdims=True))
            dsb = jnp.concatenate(dss, axis=0)
            dv_acc[:n_keys, :] += _dot_tn(jnp.concatenate(ps, axis=0), dos)
            dk_acc[:n_keys, :] += _dot_tn(dsb, qs)
            dc_acc[:, :n_keys] -= jnp.concatenate(dcs, axis=0)
            dq_ref[rows, :] += _unstack_heads(jnp.dot(dsb, kb, preferred_element_type=F32))
            dcq_ref[rows, :] += _unstack_heads(jnp.broadcast_to(jnp.concatenate(rowsums, axis=0), (2 * ROW_TILE, LANES)))

        first_q = per_kv * jj
        for r in range(per_kv):
            @pl.when(first_q + r < nb)
            def _():
                block(first_q + r, (r + 1) * ROW_TILE, True)

        def rest(i, carry):
            block(i, KV_TILE_BWD, False)
            return carry

        lax.fori_loop(jnp.minimum(first_q + per_kv, nb), nb, rest, 0)
        dk_ref[...] = dk_acc[...].astype(BF16)
        dv_ref[...] = dv_acc[...].astype(BF16)
        dc_ref[0, 0] = dc_acc[...]

    whole = lambda rows: pl.BlockSpec((rows, LANES), lambda hp, jj: (0, hp))
    kv_blk = lambda off: pl.BlockSpec((KV_TILE_BWD, LANES), lambda hp, jj: (jj, off + hp))
    return pl.pallas_call(
        body, name="attn_bwd", grid=(n_pairs, nkb),
        in_specs=[whole(lk), kv_blk(n_pairs), kv_blk(2 * n_pairs), whole(lp), whole(lp), whole(lp),
                  pl.BlockSpec((nb, 1, 1, 2, KV_TILE_BWD), lambda hp, jj: (0, hp, jj, 0, 0))],
        out_specs=[whole(lp), kv_blk(0), kv_blk(0),
                   pl.BlockSpec((1, 1, 2, KV_TILE_BWD), lambda hp, jj: (hp, jj, 0, 0)), whole(lp)],
        out_shape=[jax.ShapeDtypeStruct((lp, ATTN_WIDTH), F32), jax.ShapeDtypeStruct((lk, ATTN_WIDTH), BF16),
                   jax.ShapeDtypeStruct((lk, ATTN_WIDTH), BF16),
                   jax.ShapeDtypeStruct((n_pairs, nkb, 2, KV_TILE_BWD), F32),
                   jax.ShapeDtypeStruct((lp, ATTN_WIDTH), F32)],
        scratch_shapes=[pltpu.VMEM((KV_TILE_BWD, LANES), F32), pltpu.VMEM((KV_TILE_BWD, LANES), F32),
                        pltpu.VMEM((2, KV_TILE_BWD), F32)],
        compiler_params=_params(("parallel", "arbitrary")),
    )(qkv, qkv, qkv, do, lse, delta, bias)


def _gates_bwd(dc, dcq, f, bfg):
    nb = f.shape[0] // ROW_TILE

    def body(dc_ref, dcq_ref, f_ref, b_ref, df_ref, db_ref, carry):
        step = pl.program_id(0)
        i = nb - 1 - step

        @pl.when(step == 0)
        def _():
            carry[...] = jnp.zeros_like(carry)
            db_ref[...] = jnp.zeros_like(db_ref)

        dcb = dc_ref[...]
        lane = lax.broadcasted_iota(jnp.int32, (ROW_TILE, F_COLS), 1)
        for h in range(N_HEADS):
            dcb = dcb + jnp.where(lane == h, dcq_ref[:, HEAD_DIM * h:HEAD_DIM * h + 1], 0.0)
        r_i = lax.broadcasted_iota(jnp.int32, (ROW_TILE, ROW_TILE), 0)
        c_i = lax.broadcasted_iota(jnp.int32, (ROW_TILE, ROW_TILE), 1)
        upper = (c_i >= r_i).astype(F32)
        dlf = jnp.dot(upper, dcb, precision=HIGHEST, preferred_element_type=F32) + carry[...]
        carry[...] = carry[...] + jnp.sum(dcb, axis=0, keepdims=True)
        logit = f_ref[...] + b_ref[...]
        dlogit = jnp.where(_valid_gate_mask(i), dlf * _sigmoid(-logit), 0.0)
        df_ref[...] = dlogit.astype(BF16)
        db_ref[...] += jnp.sum(dlogit, axis=0, keepdims=True)

    blk = pl.BlockSpec((ROW_TILE, F_COLS), lambda s: (nb - 1 - s, 0))
    wide = pl.BlockSpec((ROW_TILE, ATTN_WIDTH), lambda s: (nb - 1 - s, 0))
    one = pl.BlockSpec((1, F_COLS), lambda s: (0, 0))
    return pl.pallas_call(
        body, name="gates_bwd", grid=(nb,),
        in_specs=[blk, wide, blk, one], out_specs=[blk, one],
        out_shape=[jax.ShapeDtypeStruct(f.shape, BF16), jax.ShapeDtypeStruct((1, F_COLS), F32)],
        scratch_shapes=[pltpu.VMEM((1, F_COLS), F32)],
        compiler_params=_params(("arbitrary",)),
    )(dc, dcq, f, bfg)


def _input_bwd(dproj, df, wt_e, wt_qkv, wt_f, x2, metapad, dh1, g1, hs, sc):
    nb = x2.shape[0] // ROW_TILE + 1
    n = len(hs)

    def body(dp_ref, df_ref, we_ref, w_ref, wf_ref, x_ref, mp_ref, dh_ref, g_ref, *rest):
        h_refs, s_ref = rest[:n], rest[n]
        gx_ref, g0_ref, dg_ref = rest[n + 1:n + 4]
        q_refs, sq_ref = rest[n + 4:2 * n + 4], rest[2 * n + 4]
        sems = rest[2 * n + 5:]
        i = pl.program_id(0)

        @pl.when(i == 0)
        def _():
            dg_ref[...] = jnp.zeros_like(dg_ref)
            own, sends, _ = _chip_scatter_plan(h_refs, s_ref, q_refs, sq_ref, *sems)
            for cp in own + sends:
                cp.start()

        dhn = (jnp.dot(dp_ref[:, :E_ZA], we_ref[:E_ZA, :], preferred_element_type=F32)
               + jnp.dot(dp_ref[:, E_ZA:MAIN_COLS // 2], w_ref[...], preferred_element_type=F32)
               + jnp.dot(dp_ref[:, MAIN_COLS // 2:], we_ref[E_ZA:, :], preferred_element_type=F32)
               + jnp.dot(df_ref[...], wf_ref[...], preferred_element_type=F32))
        h0 = jnp.where(i == 0, mp_ref[...], x_ref[...])
        r = lax.rsqrt(jnp.mean(h0 * h0, axis=-1, keepdims=True) + RMS_EPS)
        xhat = h0 * r
        dg_ref[...] += jnp.sum(dhn * xhat, axis=0, keepdims=True)
        dxhat = dhn * g_ref[...]
        dh0 = dh_ref[...] + r * (dxhat - xhat * jnp.mean(dxhat * xhat, axis=-1, keepdims=True))
        gx_ref[...] = dh0

        @pl.when(i == 0)
        def _():
            g0_ref[...] = dh0

        @pl.when(i == nb - 1)
        def _():
            own, sends, recvs = _chip_scatter_plan(h_refs, s_ref, q_refs, sq_ref, *sems)
            for cp in recvs:
                cp.wait_recv()
            for cp in sends:
                cp.wait_send()
            for cp in own:
                cp.wait()

    const = lambda shape: pl.BlockSpec(shape, lambda i: (0, 0))
    res = pl.pallas_call(
        body, name="input_bwd", grid=(nb,),
        in_specs=[pl.BlockSpec((ROW_TILE, MAIN_COLS), lambda i: (i, 0)), pl.BlockSpec((ROW_TILE, F_COLS), lambda i: (i, 0)),
                  const((E_COLS, D_MODEL)), const((QKV_COLS, D_MODEL)), const((F_COLS, D_MODEL)),
                  _tokens_spec(), const((ROW_TILE, D_MODEL)),
                  pl.BlockSpec((ROW_TILE, D_MODEL), lambda i: (i, 0)), const((1, D_MODEL))] + [HBM] * (n + 1),
        out_specs=[_tokens_spec(), const((ROW_TILE, D_MODEL)), const((1, D_MODEL))] + [HBM] * (n + 1),
        out_shape=[jax.ShapeDtypeStruct(x2.shape, F32), jax.ShapeDtypeStruct((ROW_TILE, D_MODEL), F32),
                   jax.ShapeDtypeStruct((1, D_MODEL), F32)]
        + [jax.ShapeDtypeStruct(h.shape, h.dtype) for h in hs] + [jax.ShapeDtypeStruct((N_CHIPS,) + sc.shape, sc.dtype)],
        scratch_shapes=[pltpu.SemaphoreType.DMA((3 * (n + 1),)), pltpu.SemaphoreType.DMA((3 * (n + 1),)),
                        pltpu.SemaphoreType.DMA((n + 1,))],
        compiler_params=_params(("arbitrary",), vmem=56 * 1024 * 1024),
    )(dproj, df, wt_e, wt_qkv, wt_f, x2, metapad, dh1, g1, *_in_hbm(*hs, sc))
    return res[:3], res[3:3 + n], res[3 + n]


def _adamw(w, g, m, v, name):
    rows, cols = w.shape
    if rows % 8 == 0:
        tr, tc = _row_tile8(rows), cols
    else:
        tr, tc = rows, (2 * LANES if cols % (2 * LANES) == 0 and rows > 8 else cols)

    def body(w_ref, g_ref, m_ref, v_ref, d_ref, mo_ref, vo_ref):
        g_ = g_ref[...]
        m_new = ADAM_B1 * m_ref[...] + (1.0 - ADAM_B1) * g_
        v_new = ADAM_B2 * v_ref[...] + (1.0 - ADAM_B2) * (g_ * g_)
        m_hat = m_new / (1.0 - ADAM_B1 ** ADAM_STEP)
        v_hat = v_new / (1.0 - ADAM_B2 ** ADAM_STEP)
        d_ref[...] = -ADAM_LR * (m_hat / (jnp.sqrt(v_hat) + ADAM_EPS) + ADAM_WD * w_ref[...])
        mo_ref[...] = m_new
        vo_ref[...] = v_new

    blk = pl.BlockSpec((tr, tc), lambda i, j: (i, j))
    return pl.pallas_call(
        body, name=name, grid=(rows // tr, cols // tc),
        in_specs=[blk] * 4, out_specs=[blk] * 3,
        out_shape=[jax.ShapeDtypeStruct(w.shape, F32)] * 3,
        compiler_params=_params(("parallel", "parallel")),
    )(w, g, m, v)


def _adamw_native(w3, g3, m3, v3, name):
    rows = w3.shape[0]
    tr = rows // 2
    blk = pl.BlockSpec((tr,) + w3.shape[1:], lambda i: (i, 0, 0))
    shape = jax.ShapeDtypeStruct(w3.shape, F32)

    def moments(g_ref, m_ref, v_ref, mo_ref, vo_ref):
        g_ = g_ref[...]
        mo_ref[...] = ADAM_B1 * m_ref[...] + (1.0 - ADAM_B1) * g_
        vo_ref[...] = ADAM_B2 * v_ref[...] + (1.0 - ADAM_B2) * (g_ * g_)

    new_m, new_v = pl.pallas_call(
        moments, name=name + "_moments", grid=(2,), in_specs=[blk] * 3, out_specs=[blk] * 2, out_shape=[shape] * 2,
        compiler_params=_params(("parallel",)),
    )(g3, m3, v3)

    def delta(w_ref, m_ref, v_ref, d_ref):
        m_hat = m_ref[...] / (1.0 - ADAM_B1 ** ADAM_STEP)
        v_hat = v_ref[...] / (1.0 - ADAM_B2 ** ADAM_STEP)
        d_ref[...] = -ADAM_LR * (m_hat / (jnp.sqrt(v_hat) + ADAM_EPS) + ADAM_WD * w_ref[...])

    d = pl.pallas_call(
        delta, name=name + "_delta", grid=(2,), in_specs=[blk] * 3, out_specs=blk, out_shape=shape,
        compiler_params=_params(("parallel",)),
    )(w3, new_m, new_v)
    return d, new_m, new_v


def _row_tile8(rows):
    best = rows
    for t in range(8, 257, 8):
        if rows % t == 0:
            best = t
    return best


def kernel(x, meta_tokens, norm_g, w_in, b_forget, pool_w, pool_scale, w_up_pool, w_up_attn, w_out, final_norm_g, loss_target, m_meta_tokens, m_norm_g, m_w_in, m_b_forget, m_pool_w, m_pool_scale, m_w_up_pool, m_w_up_attn, m_w_out, m_final_norm_g, v_meta_tokens, v_norm_g, v_w_in, v_b_forget, v_pool_w, v_pool_scale, v_w_up_pool, v_w_up_attn, v_w_out, v_final_norm_g):
    seq = x.shape[1]
    assert seq % ROW_TILE == 0 and x.shape[0] == 1
    lp = seq + ROW_TILE
    nb = lp // ROW_TILE
    lk = -(-lp // KV_TILE_BWD) * KV_TILE_BWD
    core = jnp.reshape(lax.axis_index("c"), (1,)).astype(jnp.int32)
    x2 = x[0]
    target = loss_target[0]
    sh_d = D_MODEL // N_CHIPS

    to_rows = lambda a: jnp.transpose(a, (2, 0, 1))
    from_rows = lambda a: jnp.transpose(a, (1, 2, 0))
    gf = final_norm_g.reshape(1, D_MODEL)
    bfg = jnp.pad(b_forget, ((0, 0), (0, F_COLS - N_HEADS)))
    pw_b = pool_w[0].astype(BF16)
    hn, (wg_in, meta_g) = _rmsnorm_fwd(x2, norm_g, lk, [jnp.transpose(w_in[0]).astype(BF16), meta_tokens], [1, 0])
    wt = wg_in.reshape(-1, D_MODEL)
    wt_e = jnp.concatenate([wt[REF_U:REF_Q], wt[REF_ZA:REF_F], wt[REF_GP:]], axis=0)
    wt_qkv = wt[REF_Q:REF_ZA]
    wt_f = jnp.pad(wt[REF_F:REF_GP], ((0, F_COLS - N_HEADS), (0, 0)))
    meta_full = jnp.transpose(meta_g, (1, 0, 2)).reshape(N_META, D_MODEL)
    metapad = jnp.pad(meta_full, ((PAD_ROWS, 0), (0, 0)))

    tm = _row_tile(lp, 2200)
    e, (wg_up_p, wg_up_a, wg_out) = _mm_nt(
        hn, wt_e, BF16, "in_proj_gates", lp, E_COLS, tm, 512,
        gather=([w_up_pool[0].astype(BF16), w_up_attn[0].astype(BF16), w_out[0].astype(BF16)], [0, 0, 0]))
    wup_p = jnp.transpose(wg_up_p, (1, 0, 2)).reshape(POOL_WIDTH, D_MODEL)
    wup_a = jnp.transpose(wg_up_a, (1, 0, 2)).reshape(ATTN_WIDTH, D_MODEL)
    wout = wg_out.reshape(D_MODEL, D_MODEL)
    qkv = _mm_nt(hn, wt_qkv, BF16, "in_proj_qkv", lk, QKV_COLS, _row_tile(lk, 2600), 512, scale_first=HEAD_DIM ** -0.5)
    f = _mm_nt(hn, wt_f, F32, "in_proj_forget", lp, F_COLS, tm, F_COLS)
    c = _gates_fwd(f, bfg)
    c_t = jnp.transpose(c[:, :N_HEADS])
    bias = c_t[None, :, :] - jnp.transpose(c_t[:, ::ROW_TILE])[:, :, None]
    bias = jnp.where(jnp.arange(lp) < PAD_ROWS, -NEG, bias)
    bias = jnp.pad(bias, ((0, 0), (0, 0), (0, lk - lp)))
    by_kv = lambda t: jnp.transpose(bias.reshape(nb, N_HEADS // 2, 2, lk // t, t), (0, 1, 3, 2, 4))
    bias, bias_bwd = by_kv(KV_TILE), by_kv(KV_TILE_BWD)
    y_pool = _pool_fwd(e, pw_b, pool_scale)
    o, lse = _attn_fwd(qkv, bias, nb)
    merged, y_attn = _merge_fwd(y_pool, o, e, wup_p, wup_a)
    dh1, loss_part, dgf = _head_fwd_bwd(merged, wout, x2, metapad, gf, target)

    dap, daa, dproj_half, do, delta, dy_pool = _merge_bwd(dh1, wout, y_pool, y_attn, wup_p, wup_a, e, o)
    dpc, dzp, dscale, dpw = _pool_bwd_local(e, dy_pool, pw_b, pool_scale)
    du = _pool_bwd_window(dpc)
    dq, dk, dv, dc4, dcq = _attn_bwd(qkv, do, lse, delta, bias_bwd, nb)
    dc = jnp.transpose(dc4, (1, 3, 0, 2)).reshape(lk, N_HEADS)[:lp]
    df, db = _gates_bwd(jnp.pad(dc, ((0, 0), (0, F_COLS - N_HEADS))), dcq, f, bfg)
    dproj = lax.dynamic_update_slice(
        dproj_half, jnp.concatenate([du, dzp, (dq * HEAD_DIM ** -0.5).astype(BF16), dk[:lp], dv[:lp]], axis=1), (0, 0))
    tk = _row_tile(lp, 1100)
    dw_out = _mm_tn(merged, dh1, "grad_w_out", lp, 512, D_MODEL, tk)
    dw_up_p = _mm_tn(y_pool, dap, "grad_w_up_pool", lp, 512, D_MODEL, lp, chunks=N_CHIPS)
    dw_up_a = _mm_tn(y_attn, daa, "grad_w_up_attn", lp, 512, D_MODEL, lp, chunks=N_CHIPS)
    dwt_f = _mm_tn(df, hn, "grad_w_in_forget", lp, F_COLS, D_MODEL, tk)

    def pad8(a):
        return jnp.pad(a, ((0, (-a.shape[0]) % 8), (0, 0)))

    small_parts = [pad8(a) for a in (dgf.reshape(-1, LANES), dscale.reshape(-1, LANES), db, dpw.reshape(-1, LANES),
                                     loss_part)]
    small = jnp.concatenate(small_parts, axis=0)
    soffs = [0]
    for p in small_parts:
        soffs.append(soffs[-1] + p.shape[0])

    gs_rows = [dw_up_p, dw_up_a, dw_out.reshape(N_CHIPS, sh_d, D_MODEL)]
    half = MAIN_COLS // 2
    dwt_a, (*rb_rows, rb_f, sr) = _mm_tn(dproj, hn, "grad_w_in_a", lp, half // 4, D_MODEL, lp, m=half, m_off=0,
                                         swap=(gs_rows + [dwt_f, small], [1, 1, 1, 1, None]))
    dwt_b, (rb_a,) = _mm_tn(dproj, hn, "grad_w_in_b", lp, half // 4, D_MODEL, lp, m=half, m_off=4, swap=([dwt_a], [1]))
    (rb_b,) = _sibling_exchange([dwt_b], [1])
    *hs_rows, sc = _add_sibling_half(gs_rows, rb_rows, [1, 1, 1], small, sr, core)
    h_a, h_b, h_f = _add_halves_2d([dwt_a, dwt_b, dwt_f], [rb_a, rb_b, rb_f], core)
    h_in = jnp.concatenate([h_a, h_b[:ATTN_WIDTH], h_f[:N_HEADS], h_b[ATTN_WIDTH:]], axis=0)
    hs = [h_in.reshape(N_CHIPS, -1, D_MODEL // 2)] + hs_rows
    grad_axes = [2, 1, 1, 1]
    (grad_x, g_block0, dg1), qs, sq = _input_bwd(dproj, df, wt_e, wt_qkv, wt_f, x2, metapad, dh1, norm_g, hs, sc)
    g_w_in_rows, g_w_up_p, g_w_up_a, g_w_out, st = _reduce_allgather(qs, grad_axes, sq, [True, False, False, False])
    late = _small_allreduce(jnp.concatenate([pad8(dg1.reshape(-1, LANES)), g_block0[PAD_ROWS:].reshape(-1, LANES)], axis=0))
    n_norm = D_MODEL // LANES
    g_norm = late[:n_norm].reshape(1, D_MODEL)
    chip = 2 * lax.axis_index("x") + lax.axis_index("y")
    g_meta = lax.dynamic_slice_in_dim(late[n_norm:].reshape(N_META, D_MODEL), chip * sh_d, sh_d, axis=1)

    spiece = lambda k, rows: st[soffs[k]:soffs[k] + rows]
    g_final = spiece(0, D_MODEL // LANES).reshape(1, D_MODEL)
    g_scale = spiece(1, POOL_WIDTH // LANES).reshape(1, POOL_WIDTH)
    g_bf = spiece(2, 1)
    g_pw = spiece(3, POOL_WIDTH)
    loss = st[soffs[4], 0]

    def pad_lanes(a):
        return jnp.pad(a, ((0, 0), (0, F_COLS - N_HEADS)))

    w_in_res = (g_w_in_rows,) + _adamw_native(to_rows(w_in), g_w_in_rows, to_rows(m_w_in), to_rows(v_w_in), "adamw_w_in")

    upd = [
        ("meta_tokens", meta_tokens, g_meta, m_meta_tokens, v_meta_tokens),
        ("norm_g", norm_g, g_norm, m_norm_g, v_norm_g),
        ("w_in", None, None, None, None),
        ("b_forget", pad_lanes(b_forget), g_bf, pad_lanes(m_b_forget), pad_lanes(v_b_forget)),
        ("pool_w", pool_w.reshape(-1, LANES), g_pw, m_pool_w.reshape(-1, LANES), v_pool_w.reshape(-1, LANES)),
        ("pool_scale", pool_scale, g_scale, m_pool_scale, v_pool_scale),
        ("w_up_pool", w_up_pool[0], g_w_up_p, m_w_up_pool[0], v_w_up_pool[0]),
        ("w_up_attn", w_up_attn[0], g_w_up_a, m_w_up_attn[0], v_w_up_attn[0]),
        ("w_out", w_out[0], g_w_out, m_w_out[0], v_w_out[0]),
        ("final_norm_g", gf, g_final, m_final_norm_g.reshape(1, D_MODEL), v_final_norm_g.reshape(1, D_MODEL)),
    ]
    shapes = [meta_tokens.shape, norm_g.shape, w_in.shape, b_forget.shape, pool_w.shape, pool_scale.shape,
              w_up_pool.shape, w_up_attn.shape, w_out.shape, final_norm_g.shape]
    grads, deltas, new_ms, new_vs = [], [], [], []
    for (name, w_, g_, m_in, v_in), shp in zip(upd, shapes):
        if name == "w_in":
            res = tuple(from_rows(a) for a in w_in_res)
        else:
            d_, mn_, vn_ = _adamw(w_, g_, m_in, v_in, "adamw_" + name)
            res = (g_, d_, mn_, vn_)
        if name == "b_forget":
            res = tuple(a[:, :N_HEADS] for a in res)
        for lst, a in zip((grads, deltas, new_ms, new_vs), res):
            lst.append(a.reshape(shp))

    return (loss, grad_x.reshape(x.shape), *grads, *deltas, *new_ms, *new_vs)
```

```python
import jax
import jax.numpy as jnp
from jax import lax
from jax.experimental import pallas as pl
from jax.experimental.pallas import tpu as pltpu

F32 = jnp.float32
BF16 = jnp.bfloat16
MESH = pl.DeviceIdType.MESH
HIGHEST = lax.Precision.HIGHEST
HBM = pl.BlockSpec(memory_space=pltpu.HBM)

D_MODEL = 1024
N_META = 16
POOL_WIDTH = 512
POOL_GROUP = 128
POOL_WINDOWS = (2, 4, 8, 16)
N_HEADS = 8
HEAD_DIM = 64
ATTN_WIDTH = 512
RMS_EPS = 1e-6
N_CHIPS = 4

ADAM_LR = 0.001
ADAM_B1 = 0.9
ADAM_B2 = 0.999
ADAM_EPS = 1e-08
ADAM_WD = 0.01
ADAM_STEP = 10

LANES = 128
ROW_TILE = 256
KV_TILE = 1024
KV_TILE_BWD = 1024
PAD_ROWS = ROW_TILE - N_META
NEG = -1e30

E_U, E_ZP, E_ZA, E_GP, E_GA, E_COLS = 0, 512, 1024, 1536, 2560, 3584
QKV_COLS = 3 * ATTN_WIDTH
MAIN_COLS = E_COLS + QKV_COLS
F_COLS = LANES
REF_U, REF_Q, REF_ZA, REF_F, REF_GP = 0, 1024, 2560, 3072, 3080
VMEM_LIMIT = 48 * 1024 * 1024


def _params(sem=None, vmem=VMEM_LIMIT):
    return pltpu.CompilerParams(dimension_semantics=sem, vmem_limit_bytes=vmem)


def _in_hbm(*arrays):
    return [pltpu.with_memory_space_constraint(a, pltpu.HBM) for a in arrays]


def _row_tile(n, target):
    best = 16
    for t in range(16, target + 1, 16):
        if n % t == 0:
            best = t
    return best


def _sigmoid(x):
    return 1.0 / (1.0 + jnp.exp(-x))


def _half(ref, axis, which):
    n = ref.shape[axis] // 2
    idx = [slice(None)] * len(ref.shape)
    idx[axis] = pl.ds(pl.multiple_of(which * n, n), n)
    return ref.at[tuple(idx)]


def _half_shape(shape, axis):
    s = list(shape)
    s[axis] //= 2
    return tuple(s)


def _gather_start(ins, outs, axes, send, recv, fsend, frecv, local):
    x, y, c = lax.axis_index("x"), lax.axis_index("y"), lax.axis_index("c")
    me = 2 * x + y
    for t in range(len(ins)):
        pltpu.make_async_copy(ins[t], outs[t].at[me], local.at[t]).start()
        for k, chip in enumerate([(1 - x, y), (x, 1 - y), (1 - x, 1 - y)]):
            pltpu.make_async_remote_copy(
                src_ref=_half(ins[t], axes[t], c), dst_ref=_half(outs[t].at[me], axes[t], c),
                send_sem=send.at[3 * t + k], recv_sem=recv.at[3 * t + k], device_id=(*chip, c), device_id_type=MESH).start()


def _gather_finish(ins, outs, axes, send, recv, fsend, frecv, local):
    x, y, c = lax.axis_index("x"), lax.axis_index("y"), lax.axis_index("c")
    me = 2 * x + y
    sibling = (x, y, 1 - c)
    chips = [(1 - x, y), (x, 1 - y), (1 - x, 1 - y)]
    n = len(ins)

    def over_ici(t, k, chip, dst_slot):
        return pltpu.make_async_remote_copy(
            src_ref=_half(ins[t], axes[t], c), dst_ref=_half(outs[t].at[dst_slot], axes[t], c),
            send_sem=send.at[3 * t + k], recv_sem=recv.at[3 * t + k], device_id=(*chip, c), device_id_type=MESH)

    def to_sibling(t, k, slot, which):
        return pltpu.make_async_remote_copy(
            src_ref=_half(outs[t].at[slot], axes[t], which), dst_ref=_half(outs[t].at[slot], axes[t], which),
            send_sem=fsend.at[3 * t + k], recv_sem=frecv.at[3 * t + k], device_id=sibling, device_id_type=MESH)

    forwards = []
    for t in range(n):
        for k, (px, py) in enumerate(chips):
            over_ici(t, k, (px, py), 2 * px + py).wait_recv()
            fw = to_sibling(t, k, 2 * px + py, c)
            fw.start()
            forwards.append(fw)
    for t in range(n):
        for k, (px, py) in enumerate(chips):
            to_sibling(t, k, 2 * px + py, 1 - c).wait_recv()
    for t in range(n):
        for k, chip in enumerate(chips):
            over_ici(t, k, chip, me).wait_send()
    for fw in forwards:
        fw.wait_send()
    for t in range(n):
        pltpu.make_async_copy(ins[t], outs[t].at[me], local.at[t]).wait()


def _gather_sems(n):
    return [pltpu.SemaphoreType.DMA((3 * n,)), pltpu.SemaphoreType.DMA((3 * n,)), pltpu.SemaphoreType.DMA((3 * n,)),
            pltpu.SemaphoreType.DMA((3 * n,)), pltpu.SemaphoreType.DMA((n,))]


def _gathered_shapes(shards):
    return [jax.ShapeDtypeStruct((N_CHIPS,) + s.shape, s.dtype) for s in shards]


def _swap_copies(srcs, dsts, axes, send, recv):
    x, y, c = lax.axis_index("x"), lax.axis_index("y"), lax.axis_index("c")
    return [pltpu.make_async_remote_copy(
        src_ref=srcs[t] if axes[t] is None else _half(srcs[t], axes[t], 1 - c), dst_ref=dsts[t],
        send_sem=send.at[t], recv_sem=recv.at[t], device_id=(x, y, 1 - c), device_id_type=MESH) for t in range(len(srcs))]


def _swap_shapes(srcs, axes):
    return [jax.ShapeDtypeStruct(g.shape if a is None else _half_shape(g.shape, a), g.dtype) for g, a in zip(srcs, axes)]


def _sibling_exchange(gs, axes):
    n = len(gs)

    def body(*refs):
        cps = _swap_copies(refs[:n], refs[n:2 * n], axes, *refs[2 * n:])
        for cp in cps:
            cp.start()
        for cp in cps:
            cp.wait()

    return pl.pallas_call(
        body, name="grad_sibling_exchange",
        in_specs=[HBM] * n, out_specs=[HBM] * n, out_shape=_swap_shapes(gs, axes),
        scratch_shapes=[pltpu.SemaphoreType.DMA((n,)), pltpu.SemaphoreType.DMA((n,))],
    )(*_in_hbm(*gs))


def _add_halves_2d(parts, rbs, core):
    n = len(parts)
    rows, w = rbs[0].shape
    tr = 512
    last_rows = rbs[-1].shape[0]

    def body(core_ref, *refs):
        p_refs, r_refs, h_refs = refs[:n], refs[n:2 * n], refs[2 * n:]
        for t in range(n - 1):
            h_refs[t][...] = (p_refs[t][...] + r_refs[t][...]).astype(BF16)

        @pl.when(pl.program_id(0) == 0)
        def _():
            h_refs[n - 1][...] = (p_refs[n - 1][...] + r_refs[n - 1][...]).astype(BF16)

    mine = lambda r: pl.BlockSpec((r, w), lambda i, cr: (i, cr[0]))
    tile = lambda r: pl.BlockSpec((r, w), lambda i, cr: (i, 0))
    mine_small = pl.BlockSpec((last_rows, w), lambda i, cr: (0, cr[0]))
    tile_small = pl.BlockSpec((last_rows, w), lambda i, cr: (0, 0))
    return pl.pallas_call(
        body, name="grad_add_sibling_w_in",
        grid_spec=pltpu.PrefetchScalarGridSpec(
            num_scalar_prefetch=1, grid=(rows // tr,),
            in_specs=[mine(tr)] * (n - 1) + [mine_small] + [tile(tr)] * (n - 1) + [tile_small],
            out_specs=[tile(tr)] * (n - 1) + [tile_small]),
        out_shape=[jax.ShapeDtypeStruct(r.shape, BF16) for r in rbs],
        compiler_params=_params(("arbitrary",)),
    )(core, *parts, *rbs)


def _add_sibling_half(gs, rbs, axes, small, sr, core):
    n = len(gs)

    def body(core_ref, *refs):
        g_refs, rb_refs = refs[:n], refs[n:2 * n]
        s_ref, sr_ref = refs[2 * n], refs[2 * n + 1]
        h_refs, sc_ref = refs[2 * n + 2:3 * n + 2], refs[3 * n + 2]
        for t in range(n):
            h_refs[t][...] = (g_refs[t][...] + rb_refs[t][...]).astype(BF16)
        sc_ref[...] = s_ref[...] + sr_ref[...]

    def mine(rb, axis):
        blk = (None,) + rb.shape[1:]
        if axis == 2:
            return pl.BlockSpec(blk, lambda j, cr: (j, 0, cr[0]))
        return pl.BlockSpec(blk, lambda j, cr: (j, cr[0], 0))

    chunk = lambda rb: pl.BlockSpec((None,) + rb.shape[1:], lambda j, cr: (j, 0, 0))
    whole = pl.BlockSpec(small.shape, lambda j, cr: (0, 0))
    return pl.pallas_call(
        body, name="grad_add_sibling",
        grid_spec=pltpu.PrefetchScalarGridSpec(
            num_scalar_prefetch=1, grid=(N_CHIPS,),
            in_specs=[mine(rb, a) for rb, a in zip(rbs, axes)] + [chunk(rb) for rb in rbs] + [whole, whole],
            out_specs=[chunk(rb) for rb in rbs] + [whole]),
        out_shape=[jax.ShapeDtypeStruct(rb.shape, BF16) for rb in rbs] + [jax.ShapeDtypeStruct(small.shape, F32)],
        compiler_params=_params(("arbitrary",)),
    )(core, *gs, *rbs, small, sr)


def _chip_scatter_plan(h_refs, s_ref, q_refs, sq_ref, send, recv, local):
    n = len(h_refs)
    x, y, c = lax.axis_index("x"), lax.axis_index("y"), lax.axis_index("c")
    me = 2 * x + y
    chips = [(1 - x, y), (x, 1 - y), (1 - x, 1 - y)]

    def copy(t, k, chip, src_slot, dst_slot):
        src = h_refs[t].at[src_slot] if t < n else s_ref
        dst = (q_refs[t] if t < n else sq_ref).at[dst_slot]
        return pltpu.make_async_remote_copy(src_ref=src, dst_ref=dst, send_sem=send.at[3 * t + k],
                                            recv_sem=recv.at[3 * t + k], device_id=(*chip, c), device_id_type=MESH)

    own = [pltpu.make_async_copy(h_refs[t].at[me], q_refs[t].at[me], local.at[t]) for t in range(n)]
    own.append(pltpu.make_async_copy(s_ref, sq_ref.at[me], local.at[n]))
    sends = [copy(t, k, (px, py), 2 * px + py, me) for t in range(n + 1) for k, (px, py) in enumerate(chips)]
    recvs = [copy(t, k, (px, py), me, 2 * px + py) for t in range(n + 1) for k, (px, py) in enumerate(chips)]
    return own, sends, recvs


def _small_allreduce(buf):
    def body(b_ref, o_ref, sib_buf, chip_buf, send, recv):
        x, y, c = lax.axis_index("x"), lax.axis_index("y"), lax.axis_index("c")
        me = 2 * x + y
        chips = [(1 - x, y), (x, 1 - y), (1 - x, 1 - y)]
        swap = pltpu.make_async_remote_copy(src_ref=b_ref, dst_ref=sib_buf, send_sem=send.at[0], recv_sem=recv.at[0],
                                            device_id=(x, y, 1 - c), device_id_type=MESH)
        swap.start()
        swap.wait()
        chip_buf[me] = b_ref[...] + sib_buf[...]

        def copy(k, chip, slot):
            return pltpu.make_async_remote_copy(src_ref=chip_buf.at[slot], dst_ref=chip_buf.at[slot], send_sem=send.at[1 + k],
                                                recv_sem=recv.at[1 + k], device_id=(*chip, c), device_id_type=MESH)

        sends = [copy(k, chip, me) for k, chip in enumerate(chips)]
        for cp in sends:
            cp.start()
        for k, (px, py) in enumerate(chips):
            copy(k, (px, py), 2 * px + py).wait_recv()
        for cp in sends:
            cp.wait_send()
        o_ref[...] = ((chip_buf[0] + chip_buf[1]) + chip_buf[2]) + chip_buf[3]

    vmem = pl.BlockSpec(memory_space=pltpu.VMEM)
    return pl.pallas_call(
        body, name="small_allreduce", in_specs=[vmem], out_specs=vmem,
        out_shape=jax.ShapeDtypeStruct(buf.shape, F32),
        scratch_shapes=[pltpu.VMEM(buf.shape, F32), pltpu.VMEM((N_CHIPS,) + buf.shape, F32),
                        pltpu.SemaphoreType.DMA((4,)), pltpu.SemaphoreType.DMA((4,))],
        compiler_params=_params(),
    )(buf)


def _reduce_allgather(qs, axes, sq, as_rows):
    n = len(qs)
    shard_shapes, half_axes = [], []
    for q, a, rows_form in zip(qs, axes, as_rows):
        shape = [d * 2 if i == a - 1 else d for i, d in enumerate(q.shape[1:])]
        if rows_form:
            assert a == 2
            shape = [shape[0], 1, shape[1]]
        shard_shapes.append(tuple(shape))
        half_axes.append(2 if rows_form else a - 1)

    def body(*refs):
        q_refs, sq_ref = refs[:n], refs[n]
        o_refs, st_ref = refs[n + 1:2 * n + 1], refs[2 * n + 1]
        send, recv = refs[2 * n + 2:]
        x, y, c = lax.axis_index("x"), lax.axis_index("y"), lax.axis_index("c")

        def swap(t, which):
            return pltpu.make_async_remote_copy(
                src_ref=_half(o_refs[t], half_axes[t], which), dst_ref=_half(o_refs[t], half_axes[t], which),
                send_sem=send.at[t], recv_sem=recv.at[t], device_id=(x, y, 1 - c), device_id_type=MESH)

        sent = []
        for t in range(n):
            q = q_refs[t]
            total = ((q[0].astype(F32) + q[1].astype(F32)) + q[2].astype(F32)) + q[3].astype(F32)
            if as_rows[t]:
                total = total.reshape(total.shape[0], 1, total.shape[1])
            _half(o_refs[t], half_axes[t], c)[...] = total
            cp = swap(t, c)
            cp.start()
            sent.append(cp)
        st_ref[...] = ((sq_ref[0] + sq_ref[1]) + sq_ref[2]) + sq_ref[3]
        for t in range(n):
            swap(t, 1 - c).wait_recv()
        for cp in sent:
            cp.wait_send()

    vmem = pl.BlockSpec(memory_space=pltpu.VMEM)
    return pl.pallas_call(
        body, name="grad_reduce_allgather",
        in_specs=[vmem] * (n + 1), out_specs=[vmem] * (n + 1),
        out_shape=[jax.ShapeDtypeStruct(s, F32) for s in shard_shapes] + [jax.ShapeDtypeStruct(sq.shape[1:], F32)],
        scratch_shapes=[pltpu.SemaphoreType.DMA((n,)), pltpu.SemaphoreType.DMA((n,))],
        compiler_params=_params(),
    )(*qs, sq)


def _dot_nt(a, b):
    return lax.dot_general(a, b, (((1,), (1,)), ((), ())), preferred_element_type=F32)


def _dot_tn(a, b):
    return lax.dot_general(a, b, (((0,), (0,)), ((), ())), preferred_element_type=F32)


def _mm_nt(a, bt, out_dtype, name, m, n, tm, tn, row_block=0, scale_first=None, gather=None):
    k = a.shape[1]
    shards, axes = gather if gather is not None else ([], [])
    ng = len(shards)
    grid = (m // tm, n // tn)

    def body(a_ref, b_ref, *rest):
        ins, o_ref, outs, sems = rest[:ng], rest[ng], rest[ng + 1:2 * ng + 1], rest[2 * ng + 1:]
        first = (pl.program_id(0) == 0) & (pl.program_id(1) == 0)
        last = (pl.program_id(0) == grid[0] - 1) & (pl.program_id(1) == grid[1] - 1)
        if ng:
            @pl.when(first)
            def _():
                _gather_start(ins, outs, axes, *sems)

        r = _dot_nt(a_ref[...], b_ref[...])
        if scale_first is not None:
            r = r * jnp.where(pl.program_id(1) == 0, scale_first, 1.0)
        o_ref[...] = r.astype(out_dtype)

        if ng:
            @pl.when(last)
            def _():
                _gather_finish(ins, outs, axes, *sems)

    res = pl.pallas_call(
        body, name=name, grid=grid,
        in_specs=[pl.BlockSpec((tm, k), lambda i, j: (i, 0)), pl.BlockSpec((tn, k), lambda i, j: (row_block + j, 0))]
        + [HBM] * ng,
        out_specs=[pl.BlockSpec((tm, tn), lambda i, j: (i, j))] + [HBM] * ng,
        out_shape=[jax.ShapeDtypeStruct((m, n), out_dtype)] + _gathered_shapes(shards),
        scratch_shapes=_gather_sems(ng) if ng else [],
        compiler_params=_params(("arbitrary", "arbitrary") if ng else ("parallel", "parallel")),
    )(a, bt, *_in_hbm(*shards))
    return (res[0], res[1:]) if ng else res[0]


def _mm_tn(a, b, name, k, tm, tn, tk, chunks=1, m=None, m_off=0, swap=None):
    m = a.shape[1] if m is None else m
    n = b.shape[1]
    cw = n // chunks
    srcs, axes = swap if swap is not None else ([], [])
    ns = len(srcs)
    grid = (m // tm, n // tn, k // tk)

    def body(a_ref, b_ref, *rest):
        s_refs, o_ref, d_refs, sems = rest[:ns], rest[ns], rest[ns + 1:2 * ns + 1], rest[2 * ns + 1:]
        ids = [pl.program_id(d) for d in range(3)]
        if ns:
            @pl.when((ids[0] == 0) & (ids[1] == 0) & (ids[2] == 0))
            def _():
                for cp in _swap_copies(s_refs, d_refs, axes, *sems):
                    cp.start()

        @pl.when(ids[2] == 0)
        def _():
            o_ref[...] = jnp.zeros_like(o_ref)
        r = _dot_tn(a_ref[...].astype(BF16), b_ref[...].astype(BF16))
        if chunks > 1:
            for c in range(chunks):
                o_ref[c] += r[:, c * cw:(c + 1) * cw]
        else:
            o_ref[...] += r

        if ns:
            @pl.when((ids[0] == grid[0] - 1) & (ids[1] == grid[1] - 1) & (ids[2] == grid[2] - 1))
            def _():
                for cp in _swap_copies(s_refs, d_refs, axes, *sems):
                    cp.wait()

    if chunks > 1:
        assert tn == n
        out_spec = pl.BlockSpec((chunks, tm, cw), lambda i, j, kk: (0, i, 0))
        out_shape = jax.ShapeDtypeStruct((chunks, m, cw), F32)
    else:
        out_spec = pl.BlockSpec((tm, tn), lambda i, j, kk: (i, j))
        out_shape = jax.ShapeDtypeStruct((m, n), F32)
    res = pl.pallas_call(
        body, name=name, grid=grid,
        in_specs=[pl.BlockSpec((tk, tm), lambda i, j, kk: (kk, m_off + i)), pl.BlockSpec((tk, tn), lambda i, j, kk: (kk, j))]
        + [HBM] * ns,
        out_specs=[out_spec] + [HBM] * ns, out_shape=[out_shape] + _swap_shapes(srcs, axes),
        scratch_shapes=[pltpu.SemaphoreType.DMA((ns,)), pltpu.SemaphoreType.DMA((ns,))] if ns else [],
        compiler_params=_params(("arbitrary",) * 3 if ns else ("parallel", "parallel", "arbitrary")),
    )(a, b, *_in_hbm(*srcs))
    return (res[0], res[1:]) if ns else res[0]


def _tokens_spec():
    return pl.BlockSpec((ROW_TILE, D_MODEL), lambda i: (jnp.maximum(i - 1, 0), 0))


def _rmsnorm_fwd(x2, g1, lk, shards, axes):
    nb = x2.shape[0] // ROW_TILE + 1
    nblk = lk // ROW_TILE
    ng = len(shards)
    meta_cols = shards[-1].shape[1]

    def body(x_ref, g_ref, *rest):
        ins, hn_ref, outs = rest[:ng], rest[ng], rest[ng + 1:2 * ng + 1]
        sems, meta_buf, meta_sem = rest[2 * ng + 1:2 * ng + 6], rest[2 * ng + 6], rest[2 * ng + 7]
        s = pl.program_id(0)
        blk = (s + 1) % nblk

        @pl.when(s == 0)
        def _():
            _gather_start(ins, outs, axes, *sems)

        def normed(h):
            r = lax.rsqrt(jnp.mean(h * h, axis=-1, keepdims=True) + RMS_EPS)
            return ((h * r) * g_ref[...]).astype(BF16)

        @pl.when(s < nblk - 1)
        def _():
            hn_ref[...] = normed(jnp.where(blk >= nb, 0.0, x_ref[...]))

        @pl.when(s == nblk - 1)
        def _():
            _gather_finish(ins, outs, axes, *sems)
            fetch = pltpu.make_async_copy(outs[-1], meta_buf, meta_sem.at[0])
            fetch.start()
            fetch.wait()
            meta = jnp.concatenate([meta_buf[j] for j in range(N_CHIPS)], axis=1)
            hn_ref[...] = normed(jnp.concatenate([jnp.zeros((PAD_ROWS, D_MODEL), F32), meta], axis=0))

    res = pl.pallas_call(
        body, name="rmsnorm_fwd", grid=(nblk,),
        in_specs=[pl.BlockSpec((ROW_TILE, D_MODEL), lambda s: (jnp.clip((s + 1) % nblk - 1, 0, nb - 2), 0)),
                  pl.BlockSpec((1, D_MODEL), lambda s: (0, 0))] + [HBM] * ng,
        out_specs=[pl.BlockSpec((ROW_TILE, D_MODEL), lambda s: ((s + 1) % nblk, 0))] + [HBM] * ng,
        out_shape=[jax.ShapeDtypeStruct((lk, D_MODEL), BF16)] + _gathered_shapes(shards),
        scratch_shapes=_gather_sems(ng) + [pltpu.VMEM((N_CHIPS, N_META, meta_cols), F32), pltpu.SemaphoreType.DMA((1,))],
        compiler_params=_params(("arbitrary",)),
    )(x2, g1, *_in_hbm(*shards))
    return res[0], res[1:]


def _valid_gate_mask(i):
    row = i * ROW_TILE + lax.broadcasted_iota(jnp.int32, (ROW_TILE, F_COLS), 0)
    col = lax.broadcasted_iota(jnp.int32, (ROW_TILE, F_COLS), 1)
    return (row >= PAD_ROWS) & (col < N_HEADS)


def _gates_fwd(f, bfg):
    nb = f.shape[0] // ROW_TILE

    def body(f_ref, b_ref, c_ref, carry):
        i = pl.program_id(0)

        @pl.when(i == 0)
        def _():
            carry[...] = jnp.zeros_like(carry)

        logit = f_ref[...] + b_ref[...]
        lf = jnp.minimum(logit, 0.0) - jnp.log1p(jnp.exp(-jnp.abs(logit)))
        lf = jnp.where(_valid_gate_mask(i), lf, 0.0)
        r_i = lax.broadcasted_iota(jnp.int32, (ROW_TILE, ROW_TILE), 0)
        c_i = lax.broadcasted_iota(jnp.int32, (ROW_TILE, ROW_TILE), 1)
        tri = (c_i <= r_i).astype(F32)
        c_ref[...] = jnp.dot(tri, lf, precision=HIGHEST, preferred_element_type=F32) + carry[...]
        carry[...] = carry[...] + jnp.sum(lf, axis=0, keepdims=True)

    return pl.pallas_call(
        body, name="gates_fwd", grid=(nb,),
        in_specs=[pl.BlockSpec((ROW_TILE, F_COLS), lambda i: (i, 0)), pl.BlockSpec((1, F_COLS), lambda i: (0, 0))],
        out_specs=pl.BlockSpec((ROW_TILE, F_COLS), lambda i: (i, 0)),
        out_shape=jax.ShapeDtypeStruct(f.shape, F32),
        scratch_shapes=[pltpu.VMEM((1, F_COLS), F32)],
        compiler_params=_params(("arbitrary",)),
    )(f, bfg)


def _pool_counts(i):
    row = i * ROW_TILE + lax.broadcasted_iota(jnp.int32, (ROW_TILE, 1), 0)
    return jnp.maximum(row - PAD_ROWS, 0)


def _trailing_sums(xc, levels):
    acc = xc
    for lv in range(levels):
        acc = acc + pltpu.roll(acc, 1 << lv, 0)
    return acc


def _leading_sums(xc, levels):
    n = xc.shape[0]
    acc = xc
    for lv in range(levels):
        acc = acc + pltpu.roll(acc, n - (1 << lv), 0)
    return acc


def _pool_p(u_cur, u_prev, i):
    pos = _pool_counts(i)
    ps, invs = [], []
    for g, w in enumerate(POOL_WINDOWS):
        sl = slice(g * POOL_GROUP, (g + 1) * POOL_GROUP)
        cur = u_cur[:, sl]
        xc = jnp.concatenate([u_prev[:, sl], cur], axis=0)
        win = _trailing_sums(xc, g + 1)[ROW_TILE:, :]
        inv = 1.0 / jnp.minimum(pos + 1, w).astype(F32)
        ps.append(win * inv - cur)
        invs.append(inv)
    return ps, invs


def _pool_fwd(e, pw, scale):
    nb = e.shape[0] // ROW_TILE

    def body(uc_ref, up_ref, z_ref, pw_ref, sc_ref, y_ref):
        i = pl.program_id(0)
        u_cur = uc_ref[...].astype(F32)
        u_prev = jnp.where(i == 0, 0.0, up_ref[...].astype(F32))
        ps, _ = _pool_p(u_cur, u_prev, i)
        z = z_ref[...].astype(F32)
        gate = z * _sigmoid(z)
        for g in range(len(POOL_WINDOWS)):
            sl = slice(g * POOL_GROUP, (g + 1) * POOL_GROUP)
            yraw = jnp.dot(ps[g].astype(BF16), pw_ref[g], preferred_element_type=F32)
            y_ref[:, sl] = ((yraw * sc_ref[:, sl]) * gate[:, sl]).astype(BF16)

    blk = (ROW_TILE, POOL_WIDTH)
    return pl.pallas_call(
        body, name="pool_fwd", grid=(nb,),
        in_specs=[pl.BlockSpec(blk, lambda i: (i, 0)), pl.BlockSpec(blk, lambda i: (jnp.maximum(i - 1, 0), 0)),
                  pl.BlockSpec(blk, lambda i: (i, 1)),
                  pl.BlockSpec((len(POOL_WINDOWS), POOL_GROUP, POOL_GROUP), lambda i: (0, 0, 0)),
                  pl.BlockSpec((1, POOL_WIDTH), lambda i: (0, 0))],
        out_specs=pl.BlockSpec(blk, lambda i: (i, 0)),
        out_shape=jax.ShapeDtypeStruct((e.shape[0], POOL_WIDTH), BF16),
        compiler_params=_params(("parallel",)),
    )(e, e, e, pw, scale)


def _stack_heads(a):
    first = lax.broadcasted_iota(jnp.int32, a.shape, 1) < HEAD_DIM
    zero = jnp.zeros_like(a)
    return jnp.concatenate([jnp.where(first, a, zero), jnp.where(first, zero, a)], axis=0)


def _unstack_heads(a):
    rows = a.shape[0] // 2
    first = lax.broadcasted_iota(jnp.int32, (rows, LANES), 1) < HEAD_DIM
    return jnp.where(first, a[:rows], a[rows:])


def _causal(i, jj, stacked, kv_tile):
    r = lax.broadcasted_iota(jnp.int32, (stacked * ROW_TILE, kv_tile), 0)
    if stacked == 2:
        r = jnp.where(r >= ROW_TILE, r - ROW_TILE, r)
    kidx = jj * kv_tile + lax.broadcasted_iota(jnp.int32, (stacked * ROW_TILE, kv_tile), 1)
    return kidx <= i * ROW_TILE + r


def _stack_rows(b):
    n = b.shape[1]
    return jnp.concatenate([jnp.broadcast_to(b[0:1], (ROW_TILE, n)), jnp.broadcast_to(b[1:2], (ROW_TILE, n))], axis=0)


def _attn_fwd(qkv, bias, nb):
    lk = qkv.shape[0]
    lp = nb * ROW_TILE
    nkb = lk // KV_TILE
    n_pairs = N_HEADS // 2

    def body(q_ref, k_ref, v_ref, b_ref, o_ref, lse_ref):
        i = pl.program_id(1)
        qs = _stack_heads(q_ref[...])
        last = (i * ROW_TILE) // KV_TILE

        def block(jj, carry, masked, n_keys=KV_TILE):
            m, l, acc = carry
            rows = pl.ds(pl.multiple_of(jj * KV_TILE, KV_TILE), n_keys)
            s = _dot_nt(qs, k_ref[rows, :]) - _stack_rows(b_ref[0, 0, jj][:, :n_keys])
            if masked:
                s = jnp.where(_causal(i, jj, 2, KV_TILE)[:, :n_keys], s, NEG)
            m_new = jnp.maximum(m, jnp.max(s, axis=1, keepdims=True))
            alpha = jnp.exp(m - m_new)
            p = jnp.exp(s - m_new)
            l = alpha * l + jnp.sum(p, axis=1, keepdims=True)
            acc = alpha * acc + jnp.dot(p.astype(BF16), v_ref[rows, :], preferred_element_type=F32)
            return m_new, l, acc

        init = (jnp.full((2 * ROW_TILE, 1), NEG, F32), jnp.zeros((2 * ROW_TILE, 1), F32),
                jnp.zeros((2 * ROW_TILE, LANES), F32))
        carry = lax.fori_loop(0, last, lambda jj, c: block(jj, c, False), init)
        ends = [lambda c, n=(r + 1) * ROW_TILE: block(last, c, True, n) for r in range(KV_TILE // ROW_TILE)]
        m, l, acc = lax.switch(i - last * (KV_TILE // ROW_TILE), ends, carry)
        o_ref[...] = _unstack_heads(acc / l)
        lse_ref[...] = _unstack_heads(jnp.broadcast_to(m + jnp.log(l), (2 * ROW_TILE, LANES)))

    return pl.pallas_call(
        body, name="attn_fwd", grid=(n_pairs, nb),
        in_specs=[pl.BlockSpec((ROW_TILE, LANES), lambda hp, i: (i, hp)),
                  pl.BlockSpec((lk, LANES), lambda hp, i: (0, n_pairs + hp)),
                  pl.BlockSpec((lk, LANES), lambda hp, i: (0, 2 * n_pairs + hp)),
                  pl.BlockSpec((1, 1, nkb, 2, KV_TILE), lambda hp, i: (i, hp, 0, 0, 0))],
        out_specs=[pl.BlockSpec((ROW_TILE, LANES), lambda hp, i: (i, hp)),
                   pl.BlockSpec((ROW_TILE, LANES), lambda hp, i: (i, hp))],
        out_shape=[jax.ShapeDtypeStruct((lp, ATTN_WIDTH), F32), jax.ShapeDtypeStruct((lp, ATTN_WIDTH), F32)],
        compiler_params=_params(("parallel", "parallel")),
    )(qkv, qkv, qkv, bias)


def _merge_fwd(y_pool, o, e, wup_p, wup_a):
    lp = o.shape[0]
    nb = lp // ROW_TILE

    def body(yp_ref, o_ref, e_ref, wp_ref, wa_ref, mg_ref, ya_ref):
        za = e_ref[:, E_ZA:E_GP].astype(F32)
        ya = (o_ref[...] * (za * _sigmoid(za))).astype(BF16)
        ya_ref[...] = ya
        a_pool = jnp.dot(yp_ref[...], wp_ref[...], preferred_element_type=F32)
        a_attn = jnp.dot(ya, wa_ref[...], preferred_element_type=F32)
        mg_ref[...] = (_sigmoid(e_ref[:, E_GP:E_GA].astype(F32)) * a_pool
                       + _sigmoid(e_ref[:, E_GA:E_COLS].astype(F32)) * a_attn).astype(BF16)

    return pl.pallas_call(
        body, name="merge_fwd", grid=(nb,),
        in_specs=[pl.BlockSpec((ROW_TILE, POOL_WIDTH), lambda i: (i, 0)),
                  pl.BlockSpec((ROW_TILE, ATTN_WIDTH), lambda i: (i, 0)),
                  pl.BlockSpec((ROW_TILE, E_COLS), lambda i: (i, 0)),
                  pl.BlockSpec((POOL_WIDTH, D_MODEL), lambda i: (0, 0)),
                  pl.BlockSpec((ATTN_WIDTH, D_MODEL), lambda i: (0, 0))],
        out_specs=[pl.BlockSpec((ROW_TILE, D_MODEL), lambda i: (i, 0)),
                   pl.BlockSpec((ROW_TILE, ATTN_WIDTH), lambda i: (i, 0))],
        out_shape=[jax.ShapeDtypeStruct((lp, D_MODEL), BF16), jax.ShapeDtypeStruct((lp, ATTN_WIDTH), BF16)],
        compiler_params=_params(("parallel",)),
    )(y_pool, o, e, wup_p, wup_a)


def _head_fwd_bwd(merged, w_out, x2, metapad, gf, target):
    lp = merged.shape[0]
    nb = lp // ROW_TILE

    def body(mg_ref, w_ref, x_ref, mp_ref, g_ref, t_ref, dh_ref, loss_ref, dg_ref):
        i = pl.program_id(0)

        @pl.when(i == 0)
        def _():
            loss_ref[...] = jnp.zeros_like(loss_ref)
            dg_ref[...] = jnp.zeros_like(dg_ref)

        h0 = jnp.where(i == 0, mp_ref[...], x_ref[...])
        h1 = h0 + jnp.dot(mg_ref[...], w_ref[...], preferred_element_type=F32)
        r = lax.rsqrt(jnp.mean(h1 * h1, axis=-1, keepdims=True) + RMS_EPS)
        xhat = h1 * r
        g = g_ref[...]
        err = jnp.where(i == 0, 0.0, xhat * g - t_ref[...])
        loss_ref[...] += 0.5 * jnp.sum(jnp.mean(err * err, axis=-1, keepdims=True))
        dy = err / D_MODEL
        dg_ref[...] += jnp.sum(dy * xhat, axis=0, keepdims=True)
        dxhat = dy * g
        dh_ref[...] = r * (dxhat - xhat * jnp.mean(dxhat * xhat, axis=-1, keepdims=True))

    return pl.pallas_call(
        body, name="head_fwd_bwd", grid=(nb,),
        in_specs=[pl.BlockSpec((ROW_TILE, D_MODEL), lambda i: (i, 0)),
                  pl.BlockSpec((D_MODEL, D_MODEL), lambda i: (0, 0)),
                  _tokens_spec(), pl.BlockSpec((ROW_TILE, D_MODEL), lambda i: (0, 0)),
                  pl.BlockSpec((1, D_MODEL), lambda i: (0, 0)), _tokens_spec()],
        out_specs=[pl.BlockSpec((ROW_TILE, D_MODEL), lambda i: (i, 0)),
                   pl.BlockSpec((1, LANES), lambda i: (0, 0)), pl.BlockSpec((1, D_MODEL), lambda i: (0, 0))],
        out_shape=[jax.ShapeDtypeStruct((lp, D_MODEL), F32), jax.ShapeDtypeStruct((1, LANES), F32),
                   jax.ShapeDtypeStruct((1, D_MODEL), F32)],
        compiler_params=_params(("arbitrary",)),
    )(merged, w_out, x2, metapad, gf, target)


def _per_head_rowsum(t):
    head = lax.broadcasted_iota(jnp.int32, t.shape, 1) // HEAD_DIM
    out = jnp.zeros_like(t)
    for h in range(N_HEADS):
        sel = head == h
        out = jnp.where(sel, jnp.sum(jnp.where(sel, t, 0.0), axis=1, keepdims=True), out)
    return out


def _merge_bwd(dh1, w_out, y_pool, y_attn, wup_p, wup_a, e, o):
    lp = o.shape[0]
    nb = lp // ROW_TILE

    def body(dh_ref, wo_ref, yp_ref, ya_ref, wp_ref, wa_ref, e_ref, o_ref,
             dap_ref, daa_ref, dg_ref, do_ref, delta_ref, dyp_ref):
        dmerged = _dot_nt(dh_ref[...].astype(BF16), wo_ref[...])
        a_pool = jnp.dot(yp_ref[...], wp_ref[...], preferred_element_type=F32)
        a_attn = jnp.dot(ya_ref[...], wa_ref[...], preferred_element_type=F32)
        sp = _sigmoid(e_ref[:, E_GP:E_GA].astype(F32))
        sa = _sigmoid(e_ref[:, E_GA:E_COLS].astype(F32))
        dap = (dmerged * sp).astype(BF16)
        daa = (dmerged * sa).astype(BF16)
        dap_ref[...] = dap
        daa_ref[...] = daa
        dg_ref[:, ATTN_WIDTH:ATTN_WIDTH + D_MODEL] = (dmerged * a_pool * (sp * (1.0 - sp))).astype(BF16)
        dg_ref[:, ATTN_WIDTH + D_MODEL:] = (dmerged * a_attn * (sa * (1.0 - sa))).astype(BF16)
        dyp_ref[...] = _dot_nt(dap, wp_ref[...])
        dya = _dot_nt(daa, wa_ref[...])
        za = e_ref[:, E_ZA:E_GP].astype(F32)
        sz = _sigmoid(za)
        o = o_ref[...]
        do = dya * (za * sz)
        do_ref[...] = do.astype(BF16)
        dg_ref[:, :ATTN_WIDTH] = (dya * o * (sz * (1.0 + za * (1.0 - sz)))).astype(BF16)
        delta_ref[...] = _per_head_rowsum(do * o)

    row = lambda w: pl.BlockSpec((ROW_TILE, w), lambda i: (i, 0))
    full = lambda a: pl.BlockSpec(a.shape, lambda i: (0, 0))
    return pl.pallas_call(
        body, name="merge_bwd", grid=(nb,),
        in_specs=[row(D_MODEL), full(w_out), row(POOL_WIDTH), row(ATTN_WIDTH), full(wup_p), full(wup_a),
                  row(E_COLS), row(ATTN_WIDTH)],
        out_specs=[row(D_MODEL), row(D_MODEL), pl.BlockSpec((ROW_TILE, MAIN_COLS // 2), lambda i: (i, 1)),
                   row(ATTN_WIDTH), row(ATTN_WIDTH), row(POOL_WIDTH)],
        out_shape=[jax.ShapeDtypeStruct((lp, D_MODEL), BF16), jax.ShapeDtypeStruct((lp, D_MODEL), BF16),
                   jax.ShapeDtypeStruct((lp, MAIN_COLS), BF16),
                   jax.ShapeDtypeStruct((lp, ATTN_WIDTH), BF16), jax.ShapeDtypeStruct((lp, ATTN_WIDTH), F32),
                   jax.ShapeDtypeStruct((lp, POOL_WIDTH), F32)],
        compiler_params=_params(("parallel",)),
    )(dh1, w_out, y_pool, y_attn, wup_p, wup_a, e, o)


def _fill_dproj(dproj_half, du, dzp, dq, dk, dv):
    lp = dproj_half.shape[0]
    nb = lp // ROW_TILE

    def body(base_ref, du_ref, dzp_ref, dq_ref, dk_ref, dv_ref, o_ref):
        parts = [du_ref[...], dzp_ref[...], (dq_ref[...] * HEAD_DIM ** -0.5).astype(BF16), dk_ref[...], dv_ref[...]]
        for t, p in enumerate(parts):
            o_ref[:, t * ATTN_WIDTH:(t + 1) * ATTN_WIDTH] = p

    blk = pl.BlockSpec((ROW_TILE, ATTN_WIDTH), lambda i: (i, 0))
    return pl.pallas_call(
        body, name="fill_dproj", grid=(nb,),
        in_specs=[pl.BlockSpec(memory_space=pl.ANY)] + [blk] * 5,
        out_specs=pl.BlockSpec((ROW_TILE, MAIN_COLS // 2), lambda i: (i, 0)),
        out_shape=jax.ShapeDtypeStruct(dproj_half.shape, BF16),
        input_output_aliases={0: 0},
        compiler_params=_params(("parallel",)),
    )(dproj_half, du, dzp, dq, dk, dv)


def _pool_bwd_local(e, dy_pool, pw, scale):
    lp = e.shape[0]
    nb = lp // ROW_TILE
    ng = len(POOL_WINDOWS)

    def body(uc_ref, up_ref, z_ref, dy_ref, pw_ref, sc_ref, dpc_ref, dz_ref, dsc_ref, dpw_ref):
        i = pl.program_id(0)

        @pl.when(i == 0)
        def _():
            dsc_ref[...] = jnp.zeros_like(dsc_ref)
            dpw_ref[...] = jnp.zeros_like(dpw_ref)

        u_cur = uc_ref[...].astype(F32)
        u_prev = jnp.where(i == 0, 0.0, up_ref[...].astype(F32))
        ps, invs = _pool_p(u_cur, u_prev, i)
        z = z_ref[...].astype(F32)
        sz = _sigmoid(z)
        dy = dy_ref[...]
        dypre = dy * (z * sz)
        dsilu = sz * (1.0 + z * (1.0 - sz))
        for g in range(ng):
            sl = slice(g * POOL_GROUP, (g + 1) * POOL_GROUP)
            pb = ps[g].astype(BF16)
            w = pw_ref[g]
            yraw = jnp.dot(pb, w, preferred_element_type=F32)
            sc = sc_ref[:, sl]
            dz_ref[:, sl] = (dy[:, sl] * (yraw * sc) * dsilu[:, sl]).astype(BF16)
            dsc_ref[:, sl] += jnp.sum(dypre[:, sl] * yraw, axis=0, keepdims=True)
            dyraw = (dypre[:, sl] * sc).astype(BF16)
            dpw_ref[g] += _dot_tn(pb, dyraw)
            dpc_ref[:, sl] = _dot_nt(dyraw, w) * invs[g]

    blk = (ROW_TILE, POOL_WIDTH)
    return pl.pallas_call(
        body, name="pool_bwd_local", grid=(nb,),
        in_specs=[pl.BlockSpec(blk, lambda i: (i, 0)), pl.BlockSpec(blk, lambda i: (jnp.maximum(i - 1, 0), 0)),
                  pl.BlockSpec(blk, lambda i: (i, 1)), pl.BlockSpec(blk, lambda i: (i, 0)),
                  pl.BlockSpec((ng, POOL_GROUP, POOL_GROUP), lambda i: (0, 0, 0)),
                  pl.BlockSpec((1, POOL_WIDTH), lambda i: (0, 0))],
        out_specs=[pl.BlockSpec(blk, lambda i: (i, 0)), pl.BlockSpec(blk, lambda i: (i, 0)),
                   pl.BlockSpec((1, POOL_WIDTH), lambda i: (0, 0)),
                   pl.BlockSpec((ng, POOL_GROUP, POOL_GROUP), lambda i: (0, 0, 0))],
        out_shape=[jax.ShapeDtypeStruct((lp, POOL_WIDTH), F32), jax.ShapeDtypeStruct((lp, POOL_WIDTH), BF16),
                   jax.ShapeDtypeStruct((1, POOL_WIDTH), F32),
                   jax.ShapeDtypeStruct((ng, POOL_GROUP, POOL_GROUP), F32)],
        compiler_params=_params(("arbitrary",)),
    )(e, e, e, dy_pool, pw, scale)


def _pool_bwd_window(dpc):
    lp = dpc.shape[0]
    nb = lp // ROW_TILE

    def body(cur_ref, nxt_ref, du_ref):
        i = pl.program_id(0)
        cur = cur_ref[...]
        nxt = jnp.where(i == nb - 1, 0.0, nxt_ref[...])
        pos = _pool_counts(i)
        for g, w in enumerate(POOL_WINDOWS):
            sl = slice(g * POOL_GROUP, (g + 1) * POOL_GROUP)
            xc = jnp.concatenate([cur[:, sl], nxt[:, sl]], axis=0)
            win = _leading_sums(xc, g + 1)[:ROW_TILE, :]
            dp = cur[:, sl] * jnp.minimum(pos + 1, w).astype(F32)
            du_ref[:, sl] = (win - dp).astype(BF16)

    blk = (ROW_TILE, POOL_WIDTH)
    return pl.pallas_call(
        body, name="pool_bwd_window", grid=(nb,),
        in_specs=[pl.BlockSpec(blk, lambda i: (i, 0)), pl.BlockSpec(blk, lambda i: (jnp.minimum(i + 1, nb - 1), 0))],
        out_specs=pl.BlockSpec(blk, lambda i: (i, 0)),
        out_shape=jax.ShapeDtypeStruct((lp, POOL_WIDTH), BF16),
        compiler_params=_params(("parallel",)),
    )(dpc, dpc)


def _attn_bwd(qkv, do, lse, delta, bias, nb):
    lk = qkv.shape[0]
    lp = nb * ROW_TILE
    nkb = lk // KV_TILE_BWD
    n_pairs = N_HEADS // 2
    per_kv = KV_TILE_BWD // ROW_TILE

    def body(q_ref, k_ref, v_ref, do_ref, lse_ref, dl_ref, b_ref, dq_ref, dk_ref, dv_ref, dc_ref, dcq_ref,
             dk_acc, dv_acc, dc_acc):
        jj = pl.program_id(1)

        @pl.when(jj == 0)
        def _():
            dq_ref[...] = jnp.zeros_like(dq_ref)
            dcq_ref[...] = jnp.zeros_like(dcq_ref)

        dk_acc[...] = jnp.zeros_like(dk_acc)
        dv_acc[...] = jnp.zeros_like(dv_acc)
        dc_acc[...] = jnp.zeros_like(dc_acc)

        def block(i, n_keys, masked):
            kb, vb = k_ref[:n_keys, :], v_ref[:n_keys, :]
            rows = pl.ds(pl.multiple_of(i * ROW_TILE, ROW_TILE), ROW_TILE)
            qs = _stack_heads(q_ref[rows, :])
            dos = _stack_heads(do_ref[rows, :])
            lse_i, dl_i = lse_ref[rows, :], dl_ref[rows, :]
            s = _dot_nt(qs, kb)
            dp = _dot_nt(dos, vb)
            if masked:
                valid = _causal(i, jj, 1, KV_TILE_BWD)[:, :n_keys]
            ps, dss, dcs, rowsums = [], [], [], []
            for hd in range(2):
                half = slice(hd * ROW_TILE, (hd + 1) * ROW_TILE)
                col = slice(hd * HEAD_DIM, hd * HEAD_DIM + 1)
                sh = s[half] - b_ref[i, 0, 0, hd:hd + 1, :n_keys]
                if masked:
                    sh = jnp.where(valid, sh, NEG)
                p = jnp.exp(sh - lse_i[:, col])
                ds = p * (dp[half] - dl_i[:, col])
                ps.append(p.astype(BF16))
                dss.append(ds.astype(BF16))
                dcs.append(jnp.sum(ds, axis=0, keepdims=True))
                rowsums.append(jnp.sum(ds, axis=1, keepdims=True))
            dsb = jnp.concatenate(dss, axis=0)
            dv_acc[:n_keys, :] += _dot_tn(jnp.concatenate(ps, axis=0), dos)
            dk_acc[:n_keys, :] += _dot_tn(dsb, qs)
            dc_acc[:, :n_keys] -= jnp.concatenate(dcs, axis=0)
            dq_ref[rows, :] += _unstack_heads(jnp.dot(dsb, kb, preferred_element_type=F32))
            dcq_ref[rows, :] += _unstack_heads(jnp.broadcast_to(jnp.concatenate(rowsums, axis=0), (2 * ROW_TILE, LANES)))

        first_q = per_kv * jj
        for r in range(per_kv):
            @pl.when(first_q + r < nb)
            def _():
                block(first_q + r, (r + 1) * ROW_TILE, True)

        def rest(i, carry):
            block(i, KV_TILE_BWD, False)
            return carry

        lax.fori_loop(jnp.minimum(first_q + per_kv, nb), nb, rest, 0)
        dk_ref[...] = dk_acc[...].astype(BF16)
        dv_ref[...] = dv_acc[...].astype(BF16)
        dc_ref[0, 0] = dc_acc[...]

    whole = lambda rows: pl.BlockSpec((rows, LANES), lambda hp, jj: (0, hp))
    kv_blk = lambda off: pl.BlockSpec((KV_TILE_BWD, LANES), lambda hp, jj: (jj, off + hp))
    return pl.pallas_call(
        body, name="attn_bwd", grid=(n_pairs, nkb),
        in_specs=[whole(lk), kv_blk(n_pairs), kv_blk(2 * n_pairs), whole(lp), whole(lp), whole(lp),
                  pl.BlockSpec((nb, 1, 1, 2, KV_TILE_BWD), lambda hp, jj: (0, hp, jj, 0, 0))],
        out_specs=[whole(lp), kv_blk(0), kv_blk(0),
                   pl.BlockSpec((1, 1, 2, KV_TILE_BWD), lambda hp, jj: (hp, jj, 0, 0)), whole(lp)],
        out_shape=[jax.ShapeDtypeStruct((lp, ATTN_WIDTH), F32), jax.ShapeDtypeStruct((lk, ATTN_WIDTH), BF16),
                   jax.ShapeDtypeStruct((lk, ATTN_WIDTH), BF16),
                   jax.ShapeDtypeStruct((n_pairs, nkb, 2, KV_TILE_BWD), F32),
                   jax.ShapeDtypeStruct((lp, ATTN_WIDTH), F32)],
        scratch_shapes=[pltpu.VMEM((KV_TILE_BWD, LANES), F32), pltpu.VMEM((KV_TILE_BWD, LANES), F32),
                        pltpu.VMEM((2, KV_TILE_BWD), F32)],
        compiler_params=_params(("parallel", "arbitrary")),
    )(qkv, qkv, qkv, do, lse, delta, bias)


def _gates_bwd(dc, dcq, f, bfg):
    nb = f.shape[0] // ROW_TILE

    def body(dc_ref, dcq_ref, f_ref, b_ref, df_ref, db_ref, carry):
        step = pl.program_id(0)
        i = nb - 1 - step

        @pl.when(step == 0)
        def _():
            carry[...] = jnp.zeros_like(carry)
            db_ref[...] = jnp.zeros_like(db_ref)

        dcb = dc_ref[...]
        lane = lax.broadcasted_iota(jnp.int32, (ROW_TILE, F_COLS), 1)
        for h in range(N_HEADS):
            dcb = dcb + jnp.where(lane == h, dcq_ref[:, HEAD_DIM * h:HEAD_DIM * h + 1], 0.0)
        r_i = lax.broadcasted_iota(jnp.int32, (ROW_TILE, ROW_TILE), 0)
        c_i = lax.broadcasted_iota(jnp.int32, (ROW_TILE, ROW_TILE), 1)
        upper = (c_i >= r_i).astype(F32)
        dlf = jnp.dot(upper, dcb, precision=HIGHEST, preferred_element_type=F32) + carry[...]
        carry[...] = carry[...] + jnp.sum(dcb, axis=0, keepdims=True)
        logit = f_ref[...] + b_ref[...]
        dlogit = jnp.where(_valid_gate_mask(i), dlf * _sigmoid(-logit), 0.0)
        df_ref[...] = dlogit.astype(BF16)
        db_ref[...] += jnp.sum(dlogit, axis=0, keepdims=True)

    blk = pl.BlockSpec((ROW_TILE, F_COLS), lambda s: (nb - 1 - s, 0))
    wide = pl.BlockSpec((ROW_TILE, ATTN_WIDTH), lambda s: (nb - 1 - s, 0))
    one = pl.BlockSpec((1, F_COLS), lambda s: (0, 0))
    return pl.pallas_call(
        body, name="gates_bwd", grid=(nb,),
        in_specs=[blk, wide, blk, one], out_specs=[blk, one],
        out_shape=[jax.ShapeDtypeStruct(f.shape, BF16), jax.ShapeDtypeStruct((1, F_COLS), F32)],
        scratch_shapes=[pltpu.VMEM((1, F_COLS), F32)],
        compiler_params=_params(("arbitrary",)),
    )(dc, dcq, f, bfg)


def _input_bwd(dproj, df, wt_e, wt_qkv, wt_f, x2, metapad, dh1, g1, hs, sc):
    nb = x2.shape[0] // ROW_TILE + 1
    n = len(hs)

    def body(dp_ref, df_ref, we_ref, w_ref, wf_ref, x_ref, mp_ref, dh_ref, g_ref, *rest):
        h_refs, s_ref = rest[:n], rest[n]
        gx_ref, g0_ref, dg_ref = rest[n + 1:n + 4]
        q_refs, sq_ref = rest[n + 4:2 * n + 4], rest[2 * n + 4]
        sems = rest[2 * n + 5:]
        i = pl.program_id(0)

        @pl.when(i == 0)
        def _():
            dg_ref[...] = jnp.zeros_like(dg_ref)
            own, sends, _ = _chip_scatter_plan(h_refs, s_ref, q_refs, sq_ref, *sems)
            for cp in own + sends:
                cp.start()

        dhn = (jnp.dot(dp_ref[:, :E_ZA], we_ref[:E_ZA, :], preferred_element_type=F32)
               + jnp.dot(dp_ref[:, E_ZA:MAIN_COLS // 2], w_ref[...], preferred_element_type=F32)
               + jnp.dot(dp_ref[:, MAIN_COLS // 2:], we_ref[E_ZA:, :], preferred_element_type=F32)
               + jnp.dot(df_ref[...], wf_ref[...], preferred_element_type=F32))
        h0 = jnp.where(i == 0, mp_ref[...], x_ref[...])
        r = lax.rsqrt(jnp.mean(h0 * h0, axis=-1, keepdims=True) + RMS_EPS)
        xhat = h0 * r
        dg_ref[...] += jnp.sum(dhn * xhat, axis=0, keepdims=True)
        dxhat = dhn * g_ref[...]
        dh0 = dh_ref[...] + r * (dxhat - xhat * jnp.mean(dxhat * xhat, axis=-1, keepdims=True))
        gx_ref[...] = dh0

        @pl.when(i == 0)
        def _():
            g0_ref[...] = dh0

        @pl.when(i == nb - 1)
        def _():
            own, sends, recvs = _chip_scatter_plan(h_refs, s_ref, q_refs, sq_ref, *sems)
            for cp in recvs:
                cp.wait_recv()
            for cp in sends:
                cp.wait_send()
            for cp in own:
                cp.wait()

    const = lambda shape: pl.BlockSpec(shape, lambda i: (0, 0))
    res = pl.pallas_call(
        body, name="input_bwd", grid=(nb,),
        in_specs=[pl.BlockSpec((ROW_TILE, MAIN_COLS), lambda i: (i, 0)), pl.BlockSpec((ROW_TILE, F_COLS), lambda i: (i, 0)),
                  const((E_COLS, D_MODEL)), const((QKV_COLS, D_MODEL)), const((F_COLS, D_MODEL)),
                  _tokens_spec(), const((ROW_TILE, D_MODEL)),
                  pl.BlockSpec((ROW_TILE, D_MODEL), lambda i: (i, 0)), const((1, D_MODEL))] + [HBM] * (n + 1),
        out_specs=[_tokens_spec(), const((ROW_TILE, D_MODEL)), const((1, D_MODEL))] + [HBM] * (n + 1),
        out_shape=[jax.ShapeDtypeStruct(x2.shape, F32), jax.ShapeDtypeStruct((ROW_TILE, D_MODEL), F32),
                   jax.ShapeDtypeStruct((1, D_MODEL), F32)]
        + [jax.ShapeDtypeStruct(h.shape, h.dtype) for h in hs] + [jax.ShapeDtypeStruct((N_CHIPS,) + sc.shape, sc.dtype)],
        scratch_shapes=[pltpu.SemaphoreType.DMA((3 * (n + 1),)), pltpu.SemaphoreType.DMA((3 * (n + 1),)),
                        pltpu.SemaphoreType.DMA((n + 1,))],
        compiler_params=_params(("arbitrary",), vmem=56 * 1024 * 1024),
    )(dproj, df, wt_e, wt_qkv, wt_f, x2, metapad, dh1, g1, *_in_hbm(*hs, sc))
    return res[:3], res[3:3 + n], res[3 + n]


def _adamw(w, g, m, v, name):
    rows, cols = w.shape
    if rows % 8 == 0:
        tr, tc = _row_tile8(rows), cols
    else:
        tr, tc = rows, (2 * LANES if cols % (2 * LANES) == 0 and rows > 8 else cols)

    def body(w_ref, g_ref, m_ref, v_ref, d_ref, mo_ref, vo_ref):
        g_ = g_ref[...]
        m_new = ADAM_B1 * m_ref[...] + (1.0 - ADAM_B1) * g_
        v_new = ADAM_B2 * v_ref[...] + (1.0 - ADAM_B2) * (g_ * g_)
        m_hat = m_new / (1.0 - ADAM_B1 ** ADAM_STEP)
        v_hat = v_new / (1.0 - ADAM_B2 ** ADAM_STEP)
        d_ref[...] = -ADAM_LR * (m_hat / (jnp.sqrt(v_hat) + ADAM_EPS) + ADAM_WD * w_ref[...])
        mo_ref[...] = m_new
        vo_ref[...] = v_new

    blk = pl.BlockSpec((tr, tc), lambda i, j: (i, j))
    return pl.pallas_call(
        body, name=name, grid=(rows // tr, cols // tc),
        in_specs=[blk] * 4, out_specs=[blk] * 3,
        out_shape=[jax.ShapeDtypeStruct(w.shape, F32)] * 3,
        compiler_params=_params(("parallel", "parallel")),
    )(w, g, m, v)


def _adamw_native(w3, g3, m3, v3, name):
    rows = w3.shape[0]
    tr = rows // 2
    blk = pl.BlockSpec((tr,) + w3.shape[1:], lambda i: (i, 0, 0))
    shape = jax.ShapeDtypeStruct(w3.shape, F32)

    def moments(g_ref, m_ref, v_ref, mo_ref, vo_ref):
        g_ = g_ref[...]
        mo_ref[...] = ADAM_B1 * m_ref[...] + (1.0 - ADAM_B1) * g_
        vo_ref[...] = ADAM_B2 * v_ref[...] + (1.0 - ADAM_B2) * (g_ * g_)

    new_m, new_v = pl.pallas_call(
        moments, name=name + "_moments", grid=(2,), in_specs=[blk] * 3, out_specs=[blk] * 2, out_shape=[shape] * 2,
        compiler_params=_params(("parallel",)),
    )(g3, m3, v3)

    def delta(w_ref, m_ref, v_ref, d_ref):
        m_hat = m_ref[...] / (1.0 - ADAM_B1 ** ADAM_STEP)
        v_hat = v_ref[...] / (1.0 - ADAM_B2 ** ADAM_STEP)
        d_ref[...] = -ADAM_LR * (m_hat / (jnp.sqrt(v_hat) + ADAM_EPS) + ADAM_WD * w_ref[...])

    d = pl.pallas_call(
        delta, name=name + "_delta", grid=(2,), in_specs=[blk] * 3, out_specs=blk, out_shape=shape,
        compiler_params=_params(("parallel",)),
    )(w3, new_m, new_v)
    return d, new_m, new_v


def _row_tile8(rows):
    best = rows
    for t in range(8, 257, 8):
        if rows % t == 0:
            best = t
    return best


def kernel(x, meta_tokens, norm_g, w_in, b_forget, pool_w, pool_scale, w_up_pool, w_up_attn, w_out, final_norm_g, loss_target, m_meta_tokens, m_norm_g, m_w_in, m_b_forget, m_pool_w, m_pool_scale, m_w_up_pool, m_w_up_attn, m_w_out, m_final_norm_g, v_meta_tokens, v_norm_g, v_w_in, v_b_forget, v_pool_w, v_pool_scale, v_w_up_pool, v_w_up_attn, v_w_out, v_final_norm_g):
    seq = x.shape[1]
    assert seq % ROW_TILE == 0 and x.shape[0] == 1
    lp = seq + ROW_TILE
    nb = lp // ROW_TILE
    lk = -(-lp // KV_TILE_BWD) * KV_TILE_BWD
    core = jnp.reshape(lax.axis_index("c"), (1,)).astype(jnp.int32)
    x2 = x[0]
    target = loss_target[0]
    sh_d = D_MODEL // N_CHIPS

    to_rows = lambda a: jnp.transpose(a, (2, 0, 1))
    from_rows = lambda a: jnp.transpose(a, (1, 2, 0))
    gf = final_norm_g.reshape(1, D_MODEL)
    bfg = jnp.pad(b_forget, ((0, 0), (0, F_COLS - N_HEADS)))
    pw_b = pool_w[0].astype(BF16)
    hn, (wg_in, meta_g) = _rmsnorm_fwd(x2, norm_g, lk, [jnp.transpose(w_in[0]).astype(BF16), meta_tokens], [1, 0])
    wt = wg_in.reshape(-1, D_MODEL)
    wt_e = jnp.concatenate([wt[REF_U:REF_Q], wt[REF_ZA:REF_F], wt[REF_GP:]], axis=0)
    wt_qkv = wt[REF_Q:REF_ZA]
    wt_f = jnp.pad(wt[REF_F:REF_GP], ((0, F_COLS - N_HEADS), (0, 0)))
    meta_full = jnp.transpose(meta_g, (1, 0, 2)).reshape(N_META, D_MODEL)
    metapad = jnp.pad(meta_full, ((PAD_ROWS, 0), (0, 0)))

    tm = _row_tile(lp, 2200)
    e, (wg_up_p, wg_up_a, wg_out) = _mm_nt(
        hn, wt_e, BF16, "in_proj_gates", lp, E_COLS, tm, 512,
        gather=([w_up_pool[0].astype(BF16), w_up_attn[0].astype(BF16), w_out[0].astype(BF16)], [0, 0, 0]))
    wup_p = jnp.transpose(wg_up_p, (1, 0, 2)).reshape(POOL_WIDTH, D_MODEL)
    wup_a = jnp.transpose(wg_up_a, (1, 0, 2)).reshape(ATTN_WIDTH, D_MODEL)
    wout = wg_out.reshape(D_MODEL, D_MODEL)
    qkv = _mm_nt(hn, wt_qkv, BF16, "in_proj_qkv", lk, QKV_COLS, _row_tile(lk, 2600), 512, scale_first=HEAD_DIM ** -0.5)
    f = _mm_nt(hn, wt_f, F32, "in_proj_forget", lp, F_COLS, tm, F_COLS)
    c = _gates_fwd(f, bfg)
    c_t = jnp.transpose(c[:, :N_HEADS])
    bias = c_t[None, :, :] - jnp.transpose(c_t[:, ::ROW_TILE])[:, :, None]
    bias = jnp.where(jnp.arange(lp) < PAD_ROWS, -NEG, bias)
    bias = jnp.pad(bias, ((0, 0), (0, 0), (0, lk - lp)))
    by_kv = lambda t: jnp.transpose(bias.reshape(nb, N_HEADS // 2, 2, lk // t, t), (0, 1, 3, 2, 4))
    bias, bias_bwd = by_kv(KV_TILE), by_kv(KV_TILE_BWD)
    y_pool = _pool_fwd(e, pw_b, pool_scale)
    o, lse = _attn_fwd(qkv, bias, nb)
    merged, y_attn = _merge_fwd(y_pool, o, e, wup_p, wup_a)
    dh1, loss_part, dgf = _head_fwd_bwd(merged, wout, x2, metapad, gf, target)

    dap, daa, dproj_half, do, delta, dy_pool = _merge_bwd(dh1, wout, y_pool, y_attn, wup_p, wup_a, e, o)
    dpc, dzp, dscale, dpw = _pool_bwd_local(e, dy_pool, pw_b, pool_scale)
    du = _pool_bwd_window(dpc)
    dq, dk, dv, dc4, dcq = _attn_bwd(qkv, do, lse, delta, bias_bwd, nb)
    dc = jnp.transpose(dc4, (1, 3, 0, 2)).reshape(lk, N_HEADS)[:lp]
    df, db = _gates_bwd(jnp.pad(dc, ((0, 0), (0, F_COLS - N_HEADS))), dcq, f, bfg)
    dproj = _fill_dproj(dproj_half, du, dzp, dq, dk, dv)
    tk = _row_tile(lp, 1100)
    dw_out = _mm_tn(merged, dh1, "grad_w_out", lp, 512, D_MODEL, tk)
    dw_up_p = _mm_tn(y_pool, dap, "grad_w_up_pool", lp, 512, D_MODEL, lp, chunks=N_CHIPS)
    dw_up_a = _mm_tn(y_attn, daa, "grad_w_up_attn", lp, 512, D_MODEL, lp, chunks=N_CHIPS)
    dwt_f = _mm_tn(df, hn, "grad_w_in_forget", lp, F_COLS, D_MODEL, tk)

    def pad8(a):
        return jnp.pad(a, ((0, (-a.shape[0]) % 8), (0, 0)))

    small_parts = [pad8(a) for a in (dgf.reshape(-1, LANES), dscale.reshape(-1, LANES), db, dpw.reshape(-1, LANES),
                                     loss_part)]
    small = jnp.concatenate(small_parts, axis=0)
    soffs = [0]
    for p in small_parts:
        soffs.append(soffs[-1] + p.shape[0])

    gs_rows = [dw_up_p, dw_up_a, dw_out.reshape(N_CHIPS, sh_d, D_MODEL)]
    half = MAIN_COLS // 2
    dwt_a, (*rb_rows, rb_f, sr) = _mm_tn(dproj, hn, "grad_w_in_a", lp, half // 4, D_MODEL, lp, m=half, m_off=0,
                                         swap=(gs_rows + [dwt_f, small], [1, 1, 1, 1, None]))
    dwt_b, (rb_a,) = _mm_tn(dproj, hn, "grad_w_in_b", lp, half // 4, D_MODEL, lp, m=half, m_off=4, swap=([dwt_a], [1]))
    (rb_b,) = _sibling_exchange([dwt_b], [1])
    *hs_rows, sc = _add_sibling_half(gs_rows, rb_rows, [1, 1, 1], small, sr, core)
    h_a, h_b, h_f = _add_halves_2d([dwt_a, dwt_b, dwt_f], [rb_a, rb_b, rb_f], core)
    h_in = jnp.concatenate([h_a, h_b[:ATTN_WIDTH], h_f[:N_HEADS], h_b[ATTN_WIDTH:]], axis=0)
    hs = [h_in.reshape(N_CHIPS, -1, D_MODEL // 2)] + hs_rows
    grad_axes = [2, 1, 1, 1]
    (grad_x, g_block0, dg1), qs, sq = _input_bwd(dproj, df, wt_e, wt_qkv, wt_f, x2, metapad, dh1, norm_g, hs, sc)
    g_w_in_rows, g_w_up_p, g_w_up_a, g_w_out, st = _reduce_allgather(qs, grad_axes, sq, [True, False, False, False])
    late = _small_allreduce(jnp.concatenate([pad8(dg1.reshape(-1, LANES)), g_block0[PAD_ROWS:].reshape(-1, LANES)], axis=0))
    n_norm = D_MODEL // LANES
    g_norm = late[:n_norm].reshape(1, D_MODEL)
    chip = 2 * lax.axis_index("x") + lax.axis_index("y")
    g_meta = lax.dynamic_slice_in_dim(late[n_norm:].reshape(N_META, D_MODEL), chip * sh_d, sh_d, axis=1)

    spiece = lambda k, rows: st[soffs[k]:soffs[k] + rows]
    g_final = spiece(0, D_MODEL // LANES).reshape(1, D_MODEL)
    g_scale = spiece(1, POOL_WIDTH // LANES).reshape(1, POOL_WIDTH)
    g_bf = spiece(2, 1)
    g_pw = spiece(3, POOL_WIDTH)
    loss = st[soffs[4], 0]

    def pad_lanes(a):
        return jnp.pad(a, ((0, 0), (0, F_COLS - N_HEADS)))

    w_in_res = (g_w_in_rows,) + _adamw_native(to_rows(w_in), g_w_in_rows, to_rows(m_w_in), to_rows(v_w_in), "adamw_w_in")

    upd = [
        ("meta_tokens", meta_tokens, g_meta, m_meta_tokens, v_meta_tokens),
        ("norm_g", norm_g, g_norm, m_norm_g, v_norm_g),
        ("w_in", None, None, None, None),
        ("b_forget", pad_lanes(b_forget), g_bf, pad_lanes(m_b_forget), pad_lanes(v_b_forget)),
        ("pool_w", pool_w.reshape(-1, LANES), g_pw, m_pool_w.reshape(-1, LANES), v_pool_w.reshape(-1, LANES)),
        ("pool_scale", pool_scale, g_scale, m_pool_scale, v_pool_scale),
        ("w_up_pool", w_up_pool[0], g_w_up_p, m_w_up_pool[0], v_w_up_pool[0]),
        ("w_up_attn", w_up_attn[0], g_w_up_a, m_w_up_attn[0], v_w_up_attn[0]),
        ("w_out", w_out[0], g_w_out, m_w_out[0], v_w_out[0]),
        ("final_norm_g", gf, g_final, m_final_norm_g.reshape(1, D_MODEL), v_final_norm_g.reshape(1, D_MODEL)),
    ]
    shapes = [meta_tokens.shape, norm_g.shape, w_in.shape, b_forget.shape, pool_w.shape, pool_scale.shape,
              w_up_pool.shape, w_up_attn.shape, w_out.shape, final_norm_g.shape]
    grads, deltas, new_ms, new_vs = [], [], [], []
    for (name, w_, g_, m_in, v_in), shp in zip(upd, shapes):
        if name == "w_in":
            res = tuple(from_rows(a) for a in w_in_res)
        else:
            d_, mn_, vn_ = _adamw(w_, g_, m_in, v_in, "adamw_" + name)
            res = (g_, d_, mn_, vn_)
        if name == "b_forget":
            res = tuple(a[:, :N_HEADS] for a in res)
        for lst, a in zip((grads, deltas, new_ms, new_vs), res):
            lst.append(a.reshape(shp))

    return (loss, grad_x.reshape(x.shape), *grads, *deltas, *new_ms, *new_vs)
```

```python
import jax
import jax.numpy as jnp
from jax import lax
from jax.experimental import pallas as pl
from jax.experimental.pallas import tpu as pltpu

F32 = jnp.float32
BF16 = jnp.bfloat16
MESH = pl.DeviceIdType.MESH
HIGHEST = lax.Precision.HIGHEST
HBM = pl.BlockSpec(memory_space=pltpu.HBM)

D_MODEL = 1024
N_META = 16
POOL_WIDTH = 512
POOL_GROUP = 128
POOL_WINDOWS = (2, 4, 8, 16)
N_HEADS = 8
HEAD_DIM = 64
ATTN_WIDTH = 512
RMS_EPS = 1e-6
N_CHIPS = 4

ADAM_LR = 0.001
ADAM_B1 = 0.9
ADAM_B2 = 0.999
ADAM_EPS = 1e-08
ADAM_WD = 0.01
ADAM_STEP = 10

LANES = 128
ROW_TILE = 256
KV_TILE = 1024
KV_TILE_BWD = 1024
PAD_ROWS = ROW_TILE - N_META
NEG = -1e30

E_U, E_ZP, E_ZA, E_GP, E_GA, E_COLS = 0, 512, 1024, 1536, 2560, 3584
QKV_COLS = 3 * ATTN_WIDTH
MAIN_COLS = E_COLS + QKV_COLS
F_COLS = LANES
REF_U, REF_Q, REF_ZA, REF_F, REF_GP = 0, 1024, 2560, 3072, 3080
VMEM_LIMIT = 48 * 1024 * 1024


def _params(sem=None, vmem=VMEM_LIMIT):
    return pltpu.CompilerParams(dimension_semantics=sem, vmem_limit_bytes=vmem)


def _in_hbm(*arrays):
    return [pltpu.with_memory_space_constraint(a, pltpu.HBM) for a in arrays]


def _row_tile(n, target):
    best = 16
    for t in range(16, target + 1, 16):
        if n % t == 0:
            best = t
    return best


def _sigmoid(x):
    return 1.0 / (1.0 + jnp.exp(-x))


def _half(ref, axis, which):
    n = ref.shape[axis] // 2
    idx = [slice(None)] * len(ref.shape)
    idx[axis] = pl.ds(pl.multiple_of(which * n, n), n)
    return ref.at[tuple(idx)]


def _half_shape(shape, axis):
    s = list(shape)
    s[axis] //= 2
    return tuple(s)


def _gather_start(ins, outs, axes, send, recv, fsend, frecv, local):
    x, y, c = lax.axis_index("x"), lax.axis_index("y"), lax.axis_index("c")
    me = 2 * x + y
    for t in range(len(ins)):
        pltpu.make_async_copy(ins[t], outs[t].at[me], local.at[t]).start()
        for k, chip in enumerate([(1 - x, y), (x, 1 - y), (1 - x, 1 - y)]):
            pltpu.make_async_remote_copy(
                src_ref=_half(ins[t], axes[t], c), dst_ref=_half(outs[t].at[me], axes[t], c),
                send_sem=send.at[3 * t + k], recv_sem=recv.at[3 * t + k], device_id=(*chip, c), device_id_type=MESH).start()


def _gather_finish(ins, outs, axes, send, recv, fsend, frecv, local):
    x, y, c = lax.axis_index("x"), lax.axis_index("y"), lax.axis_index("c")
    me = 2 * x + y
    sibling = (x, y, 1 - c)
    chips = [(1 - x, y), (x, 1 - y), (1 - x, 1 - y)]
    n = len(ins)

    def over_ici(t, k, chip, dst_slot):
        return pltpu.make_async_remote_copy(
            src_ref=_half(ins[t], axes[t], c), dst_ref=_half(outs[t].at[dst_slot], axes[t], c),
            send_sem=send.at[3 * t + k], recv_sem=recv.at[3 * t + k], device_id=(*chip, c), device_id_type=MESH)

    def to_sibling(t, k, slot, which):
        return pltpu.make_async_remote_copy(
            src_ref=_half(outs[t].at[slot], axes[t], which), dst_ref=_half(outs[t].at[slot], axes[t], which),
            send_sem=fsend.at[3 * t + k], recv_sem=frecv.at[3 * t + k], device_id=sibling, device_id_type=MESH)

    forwards = []
    for t in range(n):
        for k, (px, py) in enumerate(chips):
            over_ici(t, k, (px, py), 2 * px + py).wait_recv()
            fw = to_sibling(t, k, 2 * px + py, c)
            fw.start()
            forwards.append(fw)
    for t in range(n):
        for k, (px, py) in enumerate(chips):
            to_sibling(t, k, 2 * px + py, 1 - c).wait_recv()
    for t in range(n):
        for k, chip in enumerate(chips):
            over_ici(t, k, chip, me).wait_send()
    for fw in forwards:
        fw.wait_send()
    for t in range(n):
        pltpu.make_async_copy(ins[t], outs[t].at[me], local.at[t]).wait()


def _gather_sems(n):
    return [pltpu.SemaphoreType.DMA((3 * n,)), pltpu.SemaphoreType.DMA((3 * n,)), pltpu.SemaphoreType.DMA((3 * n,)),
            pltpu.SemaphoreType.DMA((3 * n,)), pltpu.SemaphoreType.DMA((n,))]


def _gathered_shapes(shards):
    return [jax.ShapeDtypeStruct((N_CHIPS,) + s.shape, s.dtype) for s in shards]


def _swap_copies(srcs, dsts, axes, send, recv):
    x, y, c = lax.axis_index("x"), lax.axis_index("y"), lax.axis_index("c")
    return [pltpu.make_async_remote_copy(
        src_ref=srcs[t] if axes[t] is None else _half(srcs[t], axes[t], 1 - c), dst_ref=dsts[t],
        send_sem=send.at[t], recv_sem=recv.at[t], device_id=(x, y, 1 - c), device_id_type=MESH) for t in range(len(srcs))]


def _swap_shapes(srcs, axes):
    return [jax.ShapeDtypeStruct(g.shape if a is None else _half_shape(g.shape, a), g.dtype) for g, a in zip(srcs, axes)]


def _sibling_exchange(gs, axes):
    n = len(gs)

    def body(*refs):
        cps = _swap_copies(refs[:n], refs[n:2 * n], axes, *refs[2 * n:])
        for cp in cps:
            cp.start()
        for cp in cps:
            cp.wait()

    return pl.pallas_call(
        body, name="grad_sibling_exchange",
        in_specs=[HBM] * n, out_specs=[HBM] * n, out_shape=_swap_shapes(gs, axes),
        scratch_shapes=[pltpu.SemaphoreType.DMA((n,)), pltpu.SemaphoreType.DMA((n,))],
    )(*_in_hbm(*gs))


def _add_halves_2d(parts, rbs, core):
    n = len(parts)
    rows, w = rbs[0].shape
    tr = 512
    last_rows = rbs[-1].shape[0]

    def body(core_ref, *refs):
        p_refs, r_refs, h_refs = refs[:n], refs[n:2 * n], refs[2 * n:]
        for t in range(n - 1):
            h_refs[t][...] = (p_refs[t][...] + r_refs[t][...]).astype(BF16)

        @pl.when(pl.program_id(0) == 0)
        def _():
            h_refs[n - 1][...] = (p_refs[n - 1][...] + r_refs[n - 1][...]).astype(BF16)

    mine = lambda r: pl.BlockSpec((r, w), lambda i, cr: (i, cr[0]))
    tile = lambda r: pl.BlockSpec((r, w), lambda i, cr: (i, 0))
    mine_small = pl.BlockSpec((last_rows, w), lambda i, cr: (0, cr[0]))
    tile_small = pl.BlockSpec((last_rows, w), lambda i, cr: (0, 0))
    return pl.pallas_call(
        body, name="grad_add_sibling_w_in",
        grid_spec=pltpu.PrefetchScalarGridSpec(
            num_scalar_prefetch=1, grid=(rows // tr,),
            in_specs=[mine(tr)] * (n - 1) + [mine_small] + [tile(tr)] * (n - 1) + [tile_small],
            out_specs=[tile(tr)] * (n - 1) + [tile_small]),
        out_shape=[jax.ShapeDtypeStruct(r.shape, BF16) for r in rbs],
        compiler_params=_params(("arbitrary",)),
    )(core, *parts, *rbs)


def _add_sibling_half(gs, rbs, axes, small, sr, core):
    n = len(gs)

    def body(core_ref, *refs):
        g_refs, rb_refs = refs[:n], refs[n:2 * n]
        s_ref, sr_ref = refs[2 * n], refs[2 * n + 1]
        h_refs, sc_ref = refs[2 * n + 2:3 * n + 2], refs[3 * n + 2]
        for t in range(n):
            h_refs[t][...] = (g_refs[t][...] + rb_refs[t][...]).astype(BF16)
        sc_ref[...] = s_ref[...] + sr_ref[...]

    def mine(rb, axis):
        blk = (None,) + rb.shape[1:]
        if axis == 2:
            return pl.BlockSpec(blk, lambda j, cr: (j, 0, cr[0]))
        return pl.BlockSpec(blk, lambda j, cr: (j, cr[0], 0))

    chunk = lambda rb: pl.BlockSpec((None,) + rb.shape[1:], lambda j, cr: (j, 0, 0))
    whole = pl.BlockSpec(small.shape, lambda j, cr: (0, 0))
    return pl.pallas_call(
        body, name="grad_add_sibling",
        grid_spec=pltpu.PrefetchScalarGridSpec(
            num_scalar_prefetch=1, grid=(N_CHIPS,),
            in_specs=[mine(rb, a) for rb, a in zip(rbs, axes)] + [chunk(rb) for rb in rbs] + [whole, whole],
            out_specs=[chunk(rb) for rb in rbs] + [whole]),
        out_shape=[jax.ShapeDtypeStruct(rb.shape, BF16) for rb in rbs] + [jax.ShapeDtypeStruct(small.shape, F32)],
        compiler_params=_params(("arbitrary",)),
    )(core, *gs, *rbs, small, sr)


def _chip_scatter_plan(h_refs, s_ref, q_refs, sq_ref, send, recv, local):
    n = len(h_refs)
    nt = n + (s_ref is not None)
    x, y, c = lax.axis_index("x"), lax.axis_index("y"), lax.axis_index("c")
    me = 2 * x + y
    chips = [(1 - x, y), (x, 1 - y), (1 - x, 1 - y)]

    def copy(t, k, chip, src_slot, dst_slot):
        src = h_refs[t].at[src_slot] if t < n else s_ref
        dst = (q_refs[t] if t < n else sq_ref).at[dst_slot]
        return pltpu.make_async_remote_copy(src_ref=src, dst_ref=dst, send_sem=send.at[3 * t + k],
                                            recv_sem=recv.at[3 * t + k], device_id=(*chip, c), device_id_type=MESH)

    own = [pltpu.make_async_copy(h_refs[t].at[me], q_refs[t].at[me], local.at[t]) for t in range(n)]
    if s_ref is not None:
        own.append(pltpu.make_async_copy(s_ref, sq_ref.at[me], local.at[n]))
    sends = [copy(t, k, (px, py), 2 * px + py, me) for t in range(nt) for k, (px, py) in enumerate(chips)]
    recvs = [copy(t, k, (px, py), me, 2 * px + py) for t in range(nt) for k, (px, py) in enumerate(chips)]
    return own, sends, recvs


def _scatter_operands(hs, sc):
    small = [] if sc is None else [sc]
    nt = len(hs) + len(small)
    shapes = [jax.ShapeDtypeStruct(h.shape, h.dtype) for h in hs] + [
        jax.ShapeDtypeStruct((N_CHIPS,) + a.shape, a.dtype) for a in small]
    sems = [pltpu.SemaphoreType.DMA((3 * nt,)), pltpu.SemaphoreType.DMA((3 * nt,)), pltpu.SemaphoreType.DMA((nt,))]
    return list(hs) + small, shapes, sems


def _small_allreduce(buf):
    def body(b_ref, o_ref, sib_buf, chip_buf, send, recv):
        x, y, c = lax.axis_index("x"), lax.axis_index("y"), lax.axis_index("c")
        me = 2 * x + y
        chips = [(1 - x, y), (x, 1 - y), (1 - x, 1 - y)]
        swap = pltpu.make_async_remote_copy(src_ref=b_ref, dst_ref=sib_buf, send_sem=send.at[0], recv_sem=recv.at[0],
                                            device_id=(x, y, 1 - c), device_id_type=MESH)
        swap.start()
        swap.wait()
        chip_buf[me] = b_ref[...] + sib_buf[...]

        def copy(k, chip, slot):
            return pltpu.make_async_remote_copy(src_ref=chip_buf.at[slot], dst_ref=chip_buf.at[slot], send_sem=send.at[1 + k],
                                                recv_sem=recv.at[1 + k], device_id=(*chip, c), device_id_type=MESH)

        sends = [copy(k, chip, me) for k, chip in enumerate(chips)]
        for cp in sends:
            cp.start()
        for k, (px, py) in enumerate(chips):
            copy(k, (px, py), 2 * px + py).wait_recv()
        for cp in sends:
            cp.wait_send()
        o_ref[...] = ((chip_buf[0] + chip_buf[1]) + chip_buf[2]) + chip_buf[3]

    vmem = pl.BlockSpec(memory_space=pltpu.VMEM)
    return pl.pallas_call(
        body, name="small_allreduce", in_specs=[vmem], out_specs=vmem,
        out_shape=jax.ShapeDtypeStruct(buf.shape, F32),
        scratch_shapes=[pltpu.VMEM(buf.shape, F32), pltpu.VMEM((N_CHIPS,) + buf.shape, F32),
                        pltpu.SemaphoreType.DMA((4,)), pltpu.SemaphoreType.DMA((4,))],
        compiler_params=_params(),
    )(buf)


def _reduce_allgather(qs, axes, sq, as_rows):
    n = len(qs)
    shard_shapes, half_axes = [], []
    for q, a, rows_form in zip(qs, axes, as_rows):
        shape = [d * 2 if i == a - 1 else d for i, d in enumerate(q.shape[1:])]
        if rows_form:
            assert a == 2
            shape = [shape[0], 1, shape[1]]
        shard_shapes.append(tuple(shape))
        half_axes.append(2 if rows_form else a - 1)

    def body(*refs):
        q_refs, sq_ref = refs[:n], refs[n]
        o_refs, st_ref = refs[n + 1:2 * n + 1], refs[2 * n + 1]
        send, recv = refs[2 * n + 2:]
        x, y, c = lax.axis_index("x"), lax.axis_index("y"), lax.axis_index("c")

        def swap(t, which):
            return pltpu.make_async_remote_copy(
                src_ref=_half(o_refs[t], half_axes[t], which), dst_ref=_half(o_refs[t], half_axes[t], which),
                send_sem=send.at[t], recv_sem=recv.at[t], device_id=(x, y, 1 - c), device_id_type=MESH)

        sent = []
        for t in range(n):
            q = q_refs[t]
            total = ((q[0].astype(F32) + q[1].astype(F32)) + q[2].astype(F32)) + q[3].astype(F32)
            if as_rows[t]:
                total = total.reshape(total.shape[0], 1, total.shape[1])
            _half(o_refs[t], half_axes[t], c)[...] = total
            cp = swap(t, c)
            cp.start()
            sent.append(cp)
        st_ref[...] = ((sq_ref[0] + sq_ref[1]) + sq_ref[2]) + sq_ref[3]
        for t in range(n):
            swap(t, 1 - c).wait_recv()
        for cp in sent:
            cp.wait_send()

    vmem = pl.BlockSpec(memory_space=pltpu.VMEM)
    return pl.pallas_call(
        body, name="grad_reduce_allgather",
        in_specs=[vmem] * (n + 1), out_specs=[vmem] * (n + 1),
        out_shape=[jax.ShapeDtypeStruct(s, F32) for s in shard_shapes] + [jax.ShapeDtypeStruct(sq.shape[1:], F32)],
        scratch_shapes=[pltpu.SemaphoreType.DMA((n,)), pltpu.SemaphoreType.DMA((n,))],
        compiler_params=_params(),
    )(*qs, sq)


def _dot_nt(a, b):
    return lax.dot_general(a, b, (((1,), (1,)), ((), ())), preferred_element_type=F32)


def _dot_tn(a, b):
    return lax.dot_general(a, b, (((0,), (0,)), ((), ())), preferred_element_type=F32)


def _mm_nt(a, bt, out_dtype, name, m, n, tm, tn, row_block=0, scale_first=None, gather=None):
    k = a.shape[1]
    shards, axes = gather if gather is not None else ([], [])
    ng = len(shards)
    grid = (m // tm, n // tn)

    def body(a_ref, b_ref, *rest):
        ins, o_ref, outs, sems = rest[:ng], rest[ng], rest[ng + 1:2 * ng + 1], rest[2 * ng + 1:]
        first = (pl.program_id(0) == 0) & (pl.program_id(1) == 0)
        last = (pl.program_id(0) == grid[0] - 1) & (pl.program_id(1) == grid[1] - 1)
        if ng:
            @pl.when(first)
            def _():
                _gather_start(ins, outs, axes, *sems)

        r = _dot_nt(a_ref[...], b_ref[...])
        if scale_first is not None:
            r = r * jnp.where(pl.program_id(1) == 0, scale_first, 1.0)
        o_ref[...] = r.astype(out_dtype)

        if ng:
            @pl.when(last)
            def _():
                _gather_finish(ins, outs, axes, *sems)

    res = pl.pallas_call(
        body, name=name, grid=grid,
        in_specs=[pl.BlockSpec((tm, k), lambda i, j: (i, 0)), pl.BlockSpec((tn, k), lambda i, j: (row_block + j, 0))]
        + [HBM] * ng,
        out_specs=[pl.BlockSpec((tm, tn), lambda i, j: (i, j))] + [HBM] * ng,
        out_shape=[jax.ShapeDtypeStruct((m, n), out_dtype)] + _gathered_shapes(shards),
        scratch_shapes=_gather_sems(ng) if ng else [],
        compiler_params=_params(("arbitrary", "arbitrary") if ng else ("parallel", "parallel")),
    )(a, bt, *_in_hbm(*shards))
    return (res[0], res[1:]) if ng else res[0]


def _mm_tn(a, b, name, k, tm, tn, tk, chunks=1, m=None, m_off=0, swap=None, scatter=None):
    m = a.shape[1] if m is None else m
    n = b.shape[1]
    cw = n // chunks
    srcs, axes = swap if swap is not None else ([], [])
    ns = len(srcs)
    hs, sc = scatter if scatter is not None else ([], None)
    sc_in, sc_shapes, sc_sems = _scatter_operands(hs, sc) if scatter is not None else ([], [], [])
    nh, nq = len(hs), len(sc_in)
    grid = (m // tm, n // tn, k // tk)

    def body(a_ref, b_ref, *rest):
        s_refs, h_refs = rest[:ns], rest[ns:ns + nq]
        o_ref = rest[ns + nq]
        d_refs, q_refs = rest[ns + nq + 1:2 * ns + nq + 1], rest[2 * ns + nq + 1:2 * ns + 2 * nq + 1]
        sems = rest[2 * ns + 2 * nq + 1:]
        swap_sems, scatter_sems = (sems[:2], sems[2:]) if ns else ((), sems)
        ids = [pl.program_id(d) for d in range(3)]

        def scatter_plan():
            small_in = h_refs[nh] if sc is not None else None
            small_out = q_refs[nh] if sc is not None else None
            return _chip_scatter_plan(h_refs[:nh], small_in, q_refs[:nh], small_out, *scatter_sems)

        if ns or nq:
            @pl.when((ids[0] == 0) & (ids[1] == 0) & (ids[2] == 0))
            def _():
                if ns:
                    for cp in _swap_copies(s_refs, d_refs, axes, *swap_sems):
                        cp.start()
                if nq:
                    own, sends, _ = scatter_plan()
                    for cp in own + sends:
                        cp.start()

        @pl.when(ids[2] == 0)
        def _():
            o_ref[...] = jnp.zeros_like(o_ref)
        r = _dot_tn(a_ref[...].astype(BF16), b_ref[...].astype(BF16))
        if chunks > 1:
            for c in range(chunks):
                o_ref[c] += r[:, c * cw:(c + 1) * cw]
        else:
            o_ref[...] += r

        if ns or nq:
            @pl.when((ids[0] == grid[0] - 1) & (ids[1] == grid[1] - 1) & (ids[2] == grid[2] - 1))
            def _():
                if ns:
                    for cp in _swap_copies(s_refs, d_refs, axes, *swap_sems):
                        cp.wait()
                if nq:
                    own, sends, recvs = scatter_plan()
                    for cp in recvs:
                        cp.wait_recv()
                    for cp in sends:
                        cp.wait_send()
                    for cp in own:
                        cp.wait()

    if chunks > 1:
        assert tn == n
        out_spec = pl.BlockSpec((chunks, tm, cw), lambda i, j, kk: (0, i, 0))
        out_shape = jax.ShapeDtypeStruct((chunks, m, cw), F32)
    else:
        out_spec = pl.BlockSpec((tm, tn), lambda i, j, kk: (i, j))
        out_shape = jax.ShapeDtypeStruct((m, n), F32)
    riders = ns + nq
    res = pl.pallas_call(
        body, name=name, grid=grid,
        in_specs=[pl.BlockSpec((tk, tm), lambda i, j, kk: (kk, m_off + i)), pl.BlockSpec((tk, tn), lambda i, j, kk: (kk, j))]
        + [HBM] * riders,
        out_specs=[out_spec] + [HBM] * riders, out_shape=[out_shape] + _swap_shapes(srcs, axes) + sc_shapes,
        scratch_shapes=([pltpu.SemaphoreType.DMA((ns,)), pltpu.SemaphoreType.DMA((ns,))] if ns else []) + sc_sems,
        compiler_params=_params(("arbitrary",) * 3 if riders else ("parallel", "parallel", "arbitrary")),
    )(a, b, *_in_hbm(*srcs, *sc_in))
    if not riders:
        return res[0]
    out = [res[0]]
    if ns:
        out.append(res[1:1 + ns])
    if nq:
        out.append((res[1 + ns:1 + ns + nh], res[1 + ns + nh] if sc is not None else None))
    return tuple(out)


def _tokens_spec():
    return pl.BlockSpec((ROW_TILE, D_MODEL), lambda i: (jnp.maximum(i - 1, 0), 0))


def _rmsnorm_fwd(x2, g1, lk, shards, axes):
    nb = x2.shape[0] // ROW_TILE + 1
    nblk = lk // ROW_TILE
    ng = len(shards)
    meta_cols = shards[-1].shape[1]

    def body(x_ref, g_ref, *rest):
        ins, hn_ref, outs = rest[:ng], rest[ng], rest[ng + 1:2 * ng + 1]
        sems, meta_buf, meta_sem = rest[2 * ng + 1:2 * ng + 6], rest[2 * ng + 6], rest[2 * ng + 7]
        s = pl.program_id(0)
        blk = (s + 1) % nblk

        @pl.when(s == 0)
        def _():
            _gather_start(ins, outs, axes, *sems)

        def normed(h):
            r = lax.rsqrt(jnp.mean(h * h, axis=-1, keepdims=True) + RMS_EPS)
            return ((h * r) * g_ref[...]).astype(BF16)

        @pl.when(s < nblk - 1)
        def _():
            hn_ref[...] = normed(jnp.where(blk >= nb, 0.0, x_ref[...]))

        @pl.when(s == nblk - 1)
        def _():
            _gather_finish(ins, outs, axes, *sems)
            fetch = pltpu.make_async_copy(outs[-1], meta_buf, meta_sem.at[0])
            fetch.start()
            fetch.wait()
            meta = jnp.concatenate([meta_buf[j] for j in range(N_CHIPS)], axis=1)
            hn_ref[...] = normed(jnp.concatenate([jnp.zeros((PAD_ROWS, D_MODEL), F32), meta], axis=0))

    res = pl.pallas_call(
        body, name="rmsnorm_fwd", grid=(nblk,),
        in_specs=[pl.BlockSpec((ROW_TILE, D_MODEL), lambda s: (jnp.clip((s + 1) % nblk - 1, 0, nb - 2), 0)),
                  pl.BlockSpec((1, D_MODEL), lambda s: (0, 0))] + [HBM] * ng,
        out_specs=[pl.BlockSpec((ROW_TILE, D_MODEL), lambda s: ((s + 1) % nblk, 0))] + [HBM] * ng,
        out_shape=[jax.ShapeDtypeStruct((lk, D_MODEL), BF16)] + _gathered_shapes(shards),
        scratch_shapes=_gather_sems(ng) + [pltpu.VMEM((N_CHIPS, N_META, meta_cols), F32), pltpu.SemaphoreType.DMA((1,))],
        compiler_params=_params(("arbitrary",)),
    )(x2, g1, *_in_hbm(*shards))
    return res[0], res[1:]


def _valid_gate_mask(i):
    row = i * ROW_TILE + lax.broadcasted_iota(jnp.int32, (ROW_TILE, F_COLS), 0)
    col = lax.broadcasted_iota(jnp.int32, (ROW_TILE, F_COLS), 1)
    return (row >= PAD_ROWS) & (col < N_HEADS)


def _gates_fwd(f, bfg):
    nb = f.shape[0] // ROW_TILE

    def body(f_ref, b_ref, c_ref, carry):
        i = pl.program_id(0)

        @pl.when(i == 0)
        def _():
            carry[...] = jnp.zeros_like(carry)

        logit = f_ref[...] + b_ref[...]
        lf = jnp.minimum(logit, 0.0) - jnp.log1p(jnp.exp(-jnp.abs(logit)))
        lf = jnp.where(_valid_gate_mask(i), lf, 0.0)
        r_i = lax.broadcasted_iota(jnp.int32, (ROW_TILE, ROW_TILE), 0)
        c_i = lax.broadcasted_iota(jnp.int32, (ROW_TILE, ROW_TILE), 1)
        tri = (c_i <= r_i).astype(F32)
        c_ref[...] = jnp.dot(tri, lf, precision=HIGHEST, preferred_element_type=F32) + carry[...]
        carry[...] = carry[...] + jnp.sum(lf, axis=0, keepdims=True)

    return pl.pallas_call(
        body, name="gates_fwd", grid=(nb,),
        in_specs=[pl.BlockSpec((ROW_TILE, F_COLS), lambda i: (i, 0)), pl.BlockSpec((1, F_COLS), lambda i: (0, 0))],
        out_specs=pl.BlockSpec((ROW_TILE, F_COLS), lambda i: (i, 0)),
        out_shape=jax.ShapeDtypeStruct(f.shape, F32),
        scratch_shapes=[pltpu.VMEM((1, F_COLS), F32)],
        compiler_params=_params(("arbitrary",)),
    )(f, bfg)


def _pool_counts(i):
    row = i * ROW_TILE + lax.broadcasted_iota(jnp.int32, (ROW_TILE, 1), 0)
    return jnp.maximum(row - PAD_ROWS, 0)


def _trailing_sums(xc, levels):
    acc = xc
    for lv in range(levels):
        acc = acc + pltpu.roll(acc, 1 << lv, 0)
    return acc


def _leading_sums(xc, levels):
    n = xc.shape[0]
    acc = xc
    for lv in range(levels):
        acc = acc + pltpu.roll(acc, n - (1 << lv), 0)
    return acc


def _pool_p(u_cur, u_prev, i):
    pos = _pool_counts(i)
    ps, invs = [], []
    for g, w in enumerate(POOL_WINDOWS):
        sl = slice(g * POOL_GROUP, (g + 1) * POOL_GROUP)
        cur = u_cur[:, sl]
        xc = jnp.concatenate([u_prev[:, sl], cur], axis=0)
        win = _trailing_sums(xc, g + 1)[ROW_TILE:, :]
        inv = 1.0 / jnp.minimum(pos + 1, w).astype(F32)
        ps.append(win * inv - cur)
        invs.append(inv)
    return ps, invs


def _pool_fwd(e, pw, scale):
    nb = e.shape[0] // ROW_TILE

    def body(uc_ref, up_ref, z_ref, pw_ref, sc_ref, y_ref):
        i = pl.program_id(0)
        u_cur = uc_ref[...].astype(F32)
        u_prev = jnp.where(i == 0, 0.0, up_ref[...].astype(F32))
        ps, _ = _pool_p(u_cur, u_prev, i)
        z = z_ref[...].astype(F32)
        gate = z * _sigmoid(z)
        for g in range(len(POOL_WINDOWS)):
            sl = slice(g * POOL_GROUP, (g + 1) * POOL_GROUP)
            yraw = jnp.dot(ps[g].astype(BF16), pw_ref[g], preferred_element_type=F32)
            y_ref[:, sl] = ((yraw * sc_ref[:, sl]) * gate[:, sl]).astype(BF16)

    blk = (ROW_TILE, POOL_WIDTH)
    return pl.pallas_call(
        body, name="pool_fwd", grid=(nb,),
        in_specs=[pl.BlockSpec(blk, lambda i: (i, 0)), pl.BlockSpec(blk, lambda i: (jnp.maximum(i - 1, 0), 0)),
                  pl.BlockSpec(blk, lambda i: (i, 1)),
                  pl.BlockSpec((len(POOL_WINDOWS), POOL_GROUP, POOL_GROUP), lambda i: (0, 0, 0)),
                  pl.BlockSpec((1, POOL_WIDTH), lambda i: (0, 0))],
        out_specs=pl.BlockSpec(blk, lambda i: (i, 0)),
        out_shape=jax.ShapeDtypeStruct((e.shape[0], POOL_WIDTH), BF16),
        compiler_params=_params(("parallel",)),
    )(e, e, e, pw, scale)


def _stack_heads(a):
    first = lax.broadcasted_iota(jnp.int32, a.shape, 1) < HEAD_DIM
    zero = jnp.zeros_like(a)
    return jnp.concatenate([jnp.where(first, a, zero), jnp.where(first, zero, a)], axis=0)


def _unstack_heads(a):
    rows = a.shape[0] // 2
    first = lax.broadcasted_iota(jnp.int32, (rows, LANES), 1) < HEAD_DIM
    return jnp.where(first, a[:rows], a[rows:])


def _causal(i, jj, stacked, kv_tile):
    r = lax.broadcasted_iota(jnp.int32, (stacked * ROW_TILE, kv_tile), 0)
    if stacked == 2:
        r = jnp.where(r >= ROW_TILE, r - ROW_TILE, r)
    kidx = jj * kv_tile + lax.broadcasted_iota(jnp.int32, (stacked * ROW_TILE, kv_tile), 1)
    return kidx <= i * ROW_TILE + r


def _stack_rows(b):
    n = b.shape[1]
    return jnp.concatenate([jnp.broadcast_to(b[0:1], (ROW_TILE, n)), jnp.broadcast_to(b[1:2], (ROW_TILE, n))], axis=0)


def _attn_fwd(qkv, bias, nb):
    lk = qkv.shape[0]
    lp = nb * ROW_TILE
    nkb = lk // KV_TILE
    n_pairs = N_HEADS // 2

    def body(q_ref, k_ref, v_ref, b_ref, o_ref, lse_ref):
        i = pl.program_id(1)
        qs = _stack_heads(q_ref[...])
        last = (i * ROW_TILE) // KV_TILE

        def block(jj, carry, masked, n_keys=KV_TILE):
            m, l, acc = carry
            rows = pl.ds(pl.multiple_of(jj * KV_TILE, KV_TILE), n_keys)
            s = _dot_nt(qs, k_ref[rows, :]) - _stack_rows(b_ref[0, 0, jj][:, :n_keys])
            if masked:
                s = jnp.where(_causal(i, jj, 2, KV_TILE)[:, :n_keys], s, NEG)
            m_new = jnp.maximum(m, jnp.max(s, axis=1, keepdims=True))
            alpha = jnp.exp(m - m_new)
            p = jnp.exp(s - m_new)
            l = alpha * l + jnp.sum(p, axis=1, keepdims=True)
            acc = alpha * acc + jnp.dot(p.astype(BF16), v_ref[rows, :], preferred_element_type=F32)
            return m_new, l, acc

        init = (jnp.full((2 * ROW_TILE, 1), NEG, F32), jnp.zeros((2 * ROW_TILE, 1), F32),
                jnp.zeros((2 * ROW_TILE, LANES), F32))
        carry = lax.fori_loop(0, last, lambda jj, c: block(jj, c, False), init)
        ends = [lambda c, n=(r + 1) * ROW_TILE: block(last, c, True, n) for r in range(KV_TILE // ROW_TILE)]
        m, l, acc = lax.switch(i - last * (KV_TILE // ROW_TILE), ends, carry)
        o_ref[...] = _unstack_heads(acc / l)
        lse_ref[...] = _unstack_heads(jnp.broadcast_to(m + jnp.log(l), (2 * ROW_TILE, LANES)))

    return pl.pallas_call(
        body, name="attn_fwd", grid=(n_pairs, nb),
        in_specs=[pl.BlockSpec((ROW_TILE, LANES), lambda hp, i: (i, hp)),
                  pl.BlockSpec((lk, LANES), lambda hp, i: (0, n_pairs + hp)),
                  pl.BlockSpec((lk, LANES), lambda hp, i: (0, 2 * n_pairs + hp)),
                  pl.BlockSpec((1, 1, nkb, 2, KV_TILE), lambda hp, i: (i, hp, 0, 0, 0))],
        out_specs=[pl.BlockSpec((ROW_TILE, LANES), lambda hp, i: (i, hp)),
                   pl.BlockSpec((ROW_TILE, LANES), lambda hp, i: (i, hp))],
        out_shape=[jax.ShapeDtypeStruct((lp, ATTN_WIDTH), F32), jax.ShapeDtypeStruct((lp, ATTN_WIDTH), F32)],
        compiler_params=_params(("parallel", "parallel")),
    )(qkv, qkv, qkv, bias)


def _merge_fwd(y_pool, o, e, wup_p, wup_a):
    lp = o.shape[0]
    nb = lp // ROW_TILE

    def body(yp_ref, o_ref, e_ref, wp_ref, wa_ref, mg_ref, ya_ref):
        za = e_ref[:, E_ZA:E_GP].astype(F32)
        ya = (o_ref[...] * (za * _sigmoid(za))).astype(BF16)
        ya_ref[...] = ya
        a_pool = jnp.dot(yp_ref[...], wp_ref[...], preferred_element_type=F32)
        a_attn = jnp.dot(ya, wa_ref[...], preferred_element_type=F32)
        mg_ref[...] = (_sigmoid(e_ref[:, E_GP:E_GA].astype(F32)) * a_pool
                       + _sigmoid(e_ref[:, E_GA:E_COLS].astype(F32)) * a_attn).astype(BF16)

    return pl.pallas_call(
        body, name="merge_fwd", grid=(nb,),
        in_specs=[pl.BlockSpec((ROW_TILE, POOL_WIDTH), lambda i: (i, 0)),
                  pl.BlockSpec((ROW_TILE, ATTN_WIDTH), lambda i: (i, 0)),
                  pl.BlockSpec((ROW_TILE, E_COLS), lambda i: (i, 0)),
                  pl.BlockSpec((POOL_WIDTH, D_MODEL), lambda i: (0, 0)),
                  pl.BlockSpec((ATTN_WIDTH, D_MODEL), lambda i: (0, 0))],
        out_specs=[pl.BlockSpec((ROW_TILE, D_MODEL), lambda i: (i, 0)),
                   pl.BlockSpec((ROW_TILE, ATTN_WIDTH), lambda i: (i, 0))],
        out_shape=[jax.ShapeDtypeStruct((lp, D_MODEL), BF16), jax.ShapeDtypeStruct((lp, ATTN_WIDTH), BF16)],
        compiler_params=_params(("parallel",)),
    )(y_pool, o, e, wup_p, wup_a)


def _head_fwd_bwd(merged, w_out, x2, metapad, gf, target):
    lp = merged.shape[0]
    nb = lp // ROW_TILE

    def body(mg_ref, w_ref, x_ref, mp_ref, g_ref, t_ref, dh_ref, loss_ref, dg_ref):
        i = pl.program_id(0)

        @pl.when(i == 0)
        def _():
            loss_ref[...] = jnp.zeros_like(loss_ref)
            dg_ref[...] = jnp.zeros_like(dg_ref)

        h0 = jnp.where(i == 0, mp_ref[...], x_ref[...])
        h1 = h0 + jnp.dot(mg_ref[...], w_ref[...], preferred_element_type=F32)
        r = lax.rsqrt(jnp.mean(h1 * h1, axis=-1, keepdims=True) + RMS_EPS)
        xhat = h1 * r
        g = g_ref[...]
        err = jnp.where(i == 0, 0.0, xhat * g - t_ref[...])
        loss_ref[...] += 0.5 * jnp.sum(jnp.mean(err * err, axis=-1, keepdims=True))
        dy = err / D_MODEL
        dg_ref[...] += jnp.sum(dy * xhat, axis=0, keepdims=True)
        dxhat = dy * g
        dh_ref[...] = r * (dxhat - xhat * jnp.mean(dxhat * xhat, axis=-1, keepdims=True))

    return pl.pallas_call(
        body, name="head_fwd_bwd", grid=(nb,),
        in_specs=[pl.BlockSpec((ROW_TILE, D_MODEL), lambda i: (i, 0)),
                  pl.BlockSpec((D_MODEL, D_MODEL), lambda i: (0, 0)),
                  _tokens_spec(), pl.BlockSpec((ROW_TILE, D_MODEL), lambda i: (0, 0)),
                  pl.BlockSpec((1, D_MODEL), lambda i: (0, 0)), _tokens_spec()],
        out_specs=[pl.BlockSpec((ROW_TILE, D_MODEL), lambda i: (i, 0)),
                   pl.BlockSpec((1, LANES), lambda i: (0, 0)), pl.BlockSpec((1, D_MODEL), lambda i: (0, 0))],
        out_shape=[jax.ShapeDtypeStruct((lp, D_MODEL), F32), jax.ShapeDtypeStruct((1, LANES), F32),
                   jax.ShapeDtypeStruct((1, D_MODEL), F32)],
        compiler_params=_params(("arbitrary",)),
    )(merged, w_out, x2, metapad, gf, target)


def _per_head_rowsum(t):
    head = lax.broadcasted_iota(jnp.int32, t.shape, 1) // HEAD_DIM
    out = jnp.zeros_like(t)
    for h in range(N_HEADS):
        sel = head == h
        out = jnp.where(sel, jnp.sum(jnp.where(sel, t, 0.0), axis=1, keepdims=True), out)
    return out


def _merge_bwd(dh1, w_out, y_pool, y_attn, wup_p, wup_a, e, o):
    lp = o.shape[0]
    nb = lp // ROW_TILE

    def body(dh_ref, wo_ref, yp_ref, ya_ref, wp_ref, wa_ref, e_ref, o_ref,
             dap_ref, daa_ref, dg_ref, do_ref, delta_ref, dyp_ref):
        dmerged = _dot_nt(dh_ref[...].astype(BF16), wo_ref[...])
        a_pool = jnp.dot(yp_ref[...], wp_ref[...], preferred_element_type=F32)
        a_attn = jnp.dot(ya_ref[...], wa_ref[...], preferred_element_type=F32)
        sp = _sigmoid(e_ref[:, E_GP:E_GA].astype(F32))
        sa = _sigmoid(e_ref[:, E_GA:E_COLS].astype(F32))
        dap = (dmerged * sp).astype(BF16)
        daa = (dmerged * sa).astype(BF16)
        dap_ref[...] = dap
        daa_ref[...] = daa
        dg_ref[:, ATTN_WIDTH:ATTN_WIDTH + D_MODEL] = (dmerged * a_pool * (sp * (1.0 - sp))).astype(BF16)
        dg_ref[:, ATTN_WIDTH + D_MODEL:] = (dmerged * a_attn * (sa * (1.0 - sa))).astype(BF16)
        dyp_ref[...] = _dot_nt(dap, wp_ref[...])
        dya = _dot_nt(daa, wa_ref[...])
        za = e_ref[:, E_ZA:E_GP].astype(F32)
        sz = _sigmoid(za)
        o = o_ref[...]
        do = dya * (za * sz)
        do_ref[...] = do.astype(BF16)
        dg_ref[:, :ATTN_WIDTH] = (dya * o * (sz * (1.0 + za * (1.0 - sz)))).astype(BF16)
        delta_ref[...] = _per_head_rowsum(do * o)

    row = lambda w: pl.BlockSpec((ROW_TILE, w), lambda i: (i, 0))
    full = lambda a: pl.BlockSpec(a.shape, lambda i: (0, 0))
    return pl.pallas_call(
        body, name="merge_bwd", grid=(nb,),
        in_specs=[row(D_MODEL), full(w_out), row(POOL_WIDTH), row(ATTN_WIDTH), full(wup_p), full(wup_a),
                  row(E_COLS), row(ATTN_WIDTH)],
        out_specs=[row(D_MODEL), row(D_MODEL), pl.BlockSpec((ROW_TILE, MAIN_COLS // 2), lambda i: (i, 1)),
                   row(ATTN_WIDTH), row(ATTN_WIDTH), row(POOL_WIDTH)],
        out_shape=[jax.ShapeDtypeStruct((lp, D_MODEL), BF16), jax.ShapeDtypeStruct((lp, D_MODEL), BF16),
                   jax.ShapeDtypeStruct((lp, MAIN_COLS), BF16),
                   jax.ShapeDtypeStruct((lp, ATTN_WIDTH), BF16), jax.ShapeDtypeStruct((lp, ATTN_WIDTH), F32),
                   jax.ShapeDtypeStruct((lp, POOL_WIDTH), F32)],
        compiler_params=_params(("parallel",)),
    )(dh1, w_out, y_pool, y_attn, wup_p, wup_a, e, o)


def _fill_dproj(dproj_half, du, dzp, dq, dk, dv):
    lp = dproj_half.shape[0]
    nb = lp // ROW_TILE

    def body(base_ref, du_ref, dzp_ref, dq_ref, dk_ref, dv_ref, o_ref):
        parts = [du_ref[...], dzp_ref[...], (dq_ref[...] * HEAD_DIM ** -0.5).astype(BF16), dk_ref[...], dv_ref[...]]
        for t, p in enumerate(parts):
            o_ref[:, t * ATTN_WIDTH:(t + 1) * ATTN_WIDTH] = p

    blk = pl.BlockSpec((ROW_TILE, ATTN_WIDTH), lambda i: (i, 0))
    return pl.pallas_call(
        body, name="fill_dproj", grid=(nb,),
        in_specs=[pl.BlockSpec(memory_space=pl.ANY)] + [blk] * 5,
        out_specs=pl.BlockSpec((ROW_TILE, MAIN_COLS // 2), lambda i: (i, 0)),
        out_shape=jax.ShapeDtypeStruct(dproj_half.shape, BF16),
        input_output_aliases={0: 0},
        compiler_params=_params(("parallel",)),
    )(dproj_half, du, dzp, dq, dk, dv)


def _pool_bwd_local(e, dy_pool, pw, scale):
    lp = e.shape[0]
    nb = lp // ROW_TILE
    ng = len(POOL_WINDOWS)

    def body(uc_ref, up_ref, z_ref, dy_ref, pw_ref, sc_ref, dpc_ref, dz_ref, dsc_ref, dpw_ref):
        i = pl.program_id(0)

        @pl.when(i == 0)
        def _():
            dsc_ref[...] = jnp.zeros_like(dsc_ref)
            dpw_ref[...] = jnp.zeros_like(dpw_ref)

        u_cur = uc_ref[...].astype(F32)
        u_prev = jnp.where(i == 0, 0.0, up_ref[...].astype(F32))
        ps, invs = _pool_p(u_cur, u_prev, i)
        z = z_ref[...].astype(F32)
        sz = _sigmoid(z)
        dy = dy_ref[...]
        dypre = dy * (z * sz)
        dsilu = sz * (1.0 + z * (1.0 - sz))
        for g in range(ng):
            sl = slice(g * POOL_GROUP, (g + 1) * POOL_GROUP)
            pb = ps[g].astype(BF16)
            w = pw_ref[g]
            yraw = jnp.dot(pb, w, preferred_element_type=F32)
            sc = sc_ref[:, sl]
            dz_ref[:, sl] = (dy[:, sl] * (yraw * sc) * dsilu[:, sl]).astype(BF16)
            dsc_ref[:, sl] += jnp.sum(dypre[:, sl] * yraw, axis=0, keepdims=True)
            dyraw = (dypre[:, sl] * sc).astype(BF16)
            dpw_ref[g] += _dot_tn(pb, dyraw)
            dpc_ref[:, sl] = _dot_nt(dyraw, w) * invs[g]

    blk = (ROW_TILE, POOL_WIDTH)
    return pl.pallas_call(
        body, name="pool_bwd_local", grid=(nb,),
        in_specs=[pl.BlockSpec(blk, lambda i: (i, 0)), pl.BlockSpec(blk, lambda i: (jnp.maximum(i - 1, 0), 0)),
                  pl.BlockSpec(blk, lambda i: (i, 1)), pl.BlockSpec(blk, lambda i: (i, 0)),
                  pl.BlockSpec((ng, POOL_GROUP, POOL_GROUP), lambda i: (0, 0, 0)),
                  pl.BlockSpec((1, POOL_WIDTH), lambda i: (0, 0))],
        out_specs=[pl.BlockSpec(blk, lambda i: (i, 0)), pl.BlockSpec(blk, lambda i: (i, 0)),
                   pl.BlockSpec((1, POOL_WIDTH), lambda i: (0, 0)),
                   pl.BlockSpec((ng, POOL_GROUP, POOL_GROUP), lambda i: (0, 0, 0))],
        out_shape=[jax.ShapeDtypeStruct((lp, POOL_WIDTH), F32), jax.ShapeDtypeStruct((lp, POOL_WIDTH), BF16),
                   jax.ShapeDtypeStruct((1, POOL_WIDTH), F32),
                   jax.ShapeDtypeStruct((ng, POOL_GROUP, POOL_GROUP), F32)],
        compiler_params=_params(("arbitrary",)),
    )(e, e, e, dy_pool, pw, scale)


def _pool_bwd_window(dpc):
    lp = dpc.shape[0]
    nb = lp // ROW_TILE

    def body(cur_ref, nxt_ref, du_ref):
        i = pl.program_id(0)
        cur = cur_ref[...]
        nxt = jnp.where(i == nb - 1, 0.0, nxt_ref[...])
        pos = _pool_counts(i)
        for g, w in enumerate(POOL_WINDOWS):
            sl = slice(g * POOL_GROUP, (g + 1) * POOL_GROUP)
            xc = jnp.concatenate([cur[:, sl], nxt[:, sl]], axis=0)
            win = _leading_sums(xc, g + 1)[:ROW_TILE, :]
            dp = cur[:, sl] * jnp.minimum(pos + 1, w).astype(F32)
            du_ref[:, sl] = (win - dp).astype(BF16)

    blk = (ROW_TILE, POOL_WIDTH)
    return pl.pallas_call(
        body, name="pool_bwd_window", grid=(nb,),
        in_specs=[pl.BlockSpec(blk, lambda i: (i, 0)), pl.BlockSpec(blk, lambda i: (jnp.minimum(i + 1, nb - 1), 0))],
        out_specs=pl.BlockSpec(blk, lambda i: (i, 0)),
        out_shape=jax.ShapeDtypeStruct((lp, POOL_WIDTH), BF16),
        compiler_params=_params(("parallel",)),
    )(dpc, dpc)


def _attn_bwd(qkv, do, lse, delta, bias, nb):
    lk = qkv.shape[0]
    lp = nb * ROW_TILE
    nkb = lk // KV_TILE_BWD
    n_pairs = N_HEADS // 2
    per_kv = KV_TILE_BWD // ROW_TILE

    def body(q_ref, k_ref, v_ref, do_ref, lse_ref, dl_ref, b_ref, dq_ref, dk_ref, dv_ref, dc_ref, dcq_ref,
             dk_acc, dv_acc, dc_acc):
        jj = pl.program_id(1)

        @pl.when(jj == 0)
        def _():
            dq_ref[...] = jnp.zeros_like(dq_ref)
            dcq_ref[...] = jnp.zeros_like(dcq_ref)

        dk_acc[...] = jnp.zeros_like(dk_acc)
        dv_acc[...] = jnp.zeros_like(dv_acc)
        dc_acc[...] = jnp.zeros_like(dc_acc)

        def block(i, n_keys, masked):
            kb, vb = k_ref[:n_keys, :], v_ref[:n_keys, :]
            rows = pl.ds(pl.multiple_of(i * ROW_TILE, ROW_TILE), ROW_TILE)
            qs = _stack_heads(q_ref[rows, :])
            dos = _stack_heads(do_ref[rows, :])
            lse_i, dl_i = lse_ref[rows, :], dl_ref[rows, :]
            s = _dot_nt(qs, kb)
            dp = _dot_nt(dos, vb)
            if masked:
                valid = _causal(i, jj, 1, KV_TILE_BWD)[:, :n_keys]
            ps, dss, dcs, rowsums = [], [], [], []
            for hd in range(2):
                half = slice(hd * ROW_TILE, (hd + 1) * ROW_TILE)
                col = slice(hd * HEAD_DIM, hd * HEAD_DIM + 1)
                sh = s[half] - b_ref[i, 0, 0, hd:hd + 1, :n_keys]
                if masked:
                    sh = jnp.where(valid, sh, NEG)
                p = jnp.exp(sh - lse_i[:, col])
                ds = p * (dp[half] - dl_i[:, col])
                ps.append(p.astype(BF16))
                dss.append(ds.astype(BF16))
                dcs.append(jnp.sum(ds, axis=0, keepdims=True))
                rowsums.append(jnp.sum(ds, axis=1, keepdims=True))
            dsb = jnp.concatenate(dss, axis=0)
            dv_acc[:n_keys, :] += _dot_tn(jnp.concatenate(ps, axis=0), dos)
            dk_acc[:n_keys, :] += _dot_tn(dsb, qs)
            dc_acc[:, :n_keys] -= jnp.concatenate(dcs, axis=0)
            dq_ref[rows, :] += _unstack_heads(jnp.dot(dsb, kb, preferred_element_type=F32))
            dcq_ref[rows, :] += _unstack_heads(jnp.broadcast_to(jnp.concatenate(rowsums, axis=0), (2 * ROW_TILE, LANES)))

        first_q = per_kv * jj
        for r in range(per_kv):
            @pl.when(first_q + r < nb)
            def _():
                block(first_q + r, (r + 1) * ROW_TILE, True)

        def rest(i, carry):
            block(i, KV_TILE_BWD, False)
            return carry

        lax.fori_loop(jnp.minimum(first_q + per_kv, nb), nb, rest, 0)
        dk_ref[...] = dk_acc[...].astype(BF16)
        dv_ref[...] = dv_acc[...].astype(BF16)
        dc_ref[0, 0] = dc_acc[...]

    whole = lambda rows: pl.BlockSpec((rows, LANES), lambda hp, jj: (0, hp))
    kv_blk = lambda off: pl.BlockSpec((KV_TILE_BWD, LANES), lambda hp, jj: (jj, off + hp))
    return pl.pallas_call(
        body, name="attn_bwd", grid=(n_pairs, nkb),
        in_specs=[whole(lk), kv_blk(n_pairs), kv_blk(2 * n_pairs), whole(lp), whole(lp), whole(lp),
                  pl.BlockSpec((nb, 1, 1, 2, KV_TILE_BWD), lambda hp, jj: (0, hp, jj, 0, 0))],
        out_specs=[whole(lp), kv_blk(0), kv_blk(0),
                   pl.BlockSpec((1, 1, 2, KV_TILE_BWD), lambda hp, jj: (hp, jj, 0, 0)), whole(lp)],
        out_shape=[jax.ShapeDtypeStruct((lp, ATTN_WIDTH), F32), jax.ShapeDtypeStruct((lk, ATTN_WIDTH), BF16),
                   jax.ShapeDtypeStruct((lk, ATTN_WIDTH), BF16),
                   jax.ShapeDtypeStruct((n_pairs, nkb, 2, KV_TILE_BWD), F32),
                   jax.ShapeDtypeStruct((lp, ATTN_WIDTH), F32)],
        scratch_shapes=[pltpu.VMEM((KV_TILE_BWD, LANES), F32), pltpu.VMEM((KV_TILE_BWD, LANES), F32),
                        pltpu.VMEM((2, KV_TILE_BWD), F32)],
        compiler_params=_params(("parallel", "arbitrary")),
    )(qkv, qkv, qkv, do, lse, delta, bias)


def _gates_bwd(dc, dcq, f, bfg):
    nb = f.shape[0] // ROW_TILE

    def body(dc_ref, dcq_ref, f_ref, b_ref, df_ref, db_ref, carry):
        step = pl.program_id(0)
        i = nb - 1 - step

        @pl.when(step == 0)
        def _():
            carry[...] = jnp.zeros_like(carry)
            db_ref[...] = jnp.zeros_like(db_ref)

        dcb = dc_ref[...]
        lane = lax.broadcasted_iota(jnp.int32, (ROW_TILE, F_COLS), 1)
        for h in range(N_HEADS):
            dcb = dcb + jnp.where(lane == h, dcq_ref[:, HEAD_DIM * h:HEAD_DIM * h + 1], 0.0)
        r_i = lax.broadcasted_iota(jnp.int32, (ROW_TILE, ROW_TILE), 0)
        c_i = lax.broadcasted_iota(jnp.int32, (ROW_TILE, ROW_TILE), 1)
        upper = (c_i >= r_i).astype(F32)
        dlf = jnp.dot(upper, dcb, precision=HIGHEST, preferred_element_type=F32) + carry[...]
        carry[...] = carry[...] + jnp.sum(dcb, axis=0, keepdims=True)
        logit = f_ref[...] + b_ref[...]
        dlogit = jnp.where(_valid_gate_mask(i), dlf * _sigmoid(-logit), 0.0)
        df_ref[...] = dlogit.astype(BF16)
        db_ref[...] += jnp.sum(dlogit, axis=0, keepdims=True)

    blk = pl.BlockSpec((ROW_TILE, F_COLS), lambda s: (nb - 1 - s, 0))
    wide = pl.BlockSpec((ROW_TILE, ATTN_WIDTH), lambda s: (nb - 1 - s, 0))
    one = pl.BlockSpec((1, F_COLS), lambda s: (0, 0))
    return pl.pallas_call(
        body, name="gates_bwd", grid=(nb,),
        in_specs=[blk, wide, blk, one], out_specs=[blk, one],
        out_shape=[jax.ShapeDtypeStruct(f.shape, BF16), jax.ShapeDtypeStruct((1, F_COLS), F32)],
        scratch_shapes=[pltpu.VMEM((1, F_COLS), F32)],
        compiler_params=_params(("arbitrary",)),
    )(dc, dcq, f, bfg)


def _input_bwd(dproj, df, wt_e, wt_qkv, wt_f, x2, metapad, dh1, g1, hs, sc=None):
    nb = x2.shape[0] // ROW_TILE + 1
    n = len(hs)
    sc_in, sc_shapes, sc_sems = _scatter_operands(hs, sc)
    nq = len(sc_in)

    def body(dp_ref, df_ref, we_ref, w_ref, wf_ref, x_ref, mp_ref, dh_ref, g_ref, *rest):
        h_refs, s_ref = rest[:n], (rest[n] if sc is not None else None)
        gx_ref, g0_ref, dg_ref = rest[nq:nq + 3]
        q_refs, sq_ref = rest[nq + 3:nq + 3 + n], (rest[nq + 3 + n] if sc is not None else None)
        sems = rest[2 * nq + 3:]
        i = pl.program_id(0)

        @pl.when(i == 0)
        def _():
            dg_ref[...] = jnp.zeros_like(dg_ref)
            own, sends, _ = _chip_scatter_plan(h_refs, s_ref, q_refs, sq_ref, *sems)
            for cp in own + sends:
                cp.start()

        dhn = (jnp.dot(dp_ref[:, :E_ZA], we_ref[:E_ZA, :], preferred_element_type=F32)
               + jnp.dot(dp_ref[:, E_ZA:MAIN_COLS // 2], w_ref[...], preferred_element_type=F32)
               + jnp.dot(dp_ref[:, MAIN_COLS // 2:], we_ref[E_ZA:, :], preferred_element_type=F32)
               + jnp.dot(df_ref[...], wf_ref[...], preferred_element_type=F32))
        h0 = jnp.where(i == 0, mp_ref[...], x_ref[...])
        r = lax.rsqrt(jnp.mean(h0 * h0, axis=-1, keepdims=True) + RMS_EPS)
        xhat = h0 * r
        dg_ref[...] += jnp.sum(dhn * xhat, axis=0, keepdims=True)
        dxhat = dhn * g_ref[...]
        dh0 = dh_ref[...] + r * (dxhat - xhat * jnp.mean(dxhat * xhat, axis=-1, keepdims=True))
        gx_ref[...] = dh0

        @pl.when(i == 0)
        def _():
            g0_ref[...] = dh0

        @pl.when(i == nb - 1)
        def _():
            own, sends, recvs = _chip_scatter_plan(h_refs, s_ref, q_refs, sq_ref, *sems)
            for cp in recvs:
                cp.wait_recv()
            for cp in sends:
                cp.wait_send()
            for cp in own:
                cp.wait()

    const = lambda shape: pl.BlockSpec(shape, lambda i: (0, 0))
    res = pl.pallas_call(
        body, name="input_bwd", grid=(nb,),
        in_specs=[pl.BlockSpec((ROW_TILE, MAIN_COLS), lambda i: (i, 0)), pl.BlockSpec((ROW_TILE, F_COLS), lambda i: (i, 0)),
                  const((E_COLS, D_MODEL)), const((QKV_COLS, D_MODEL)), const((F_COLS, D_MODEL)),
                  _tokens_spec(), const((ROW_TILE, D_MODEL)),
                  pl.BlockSpec((ROW_TILE, D_MODEL), lambda i: (i, 0)), const((1, D_MODEL))] + [HBM] * nq,
        out_specs=[_tokens_spec(), const((ROW_TILE, D_MODEL)), const((1, D_MODEL))] + [HBM] * nq,
        out_shape=[jax.ShapeDtypeStruct(x2.shape, F32), jax.ShapeDtypeStruct((ROW_TILE, D_MODEL), F32),
                   jax.ShapeDtypeStruct((1, D_MODEL), F32)] + sc_shapes,
        scratch_shapes=sc_sems,
        compiler_params=_params(("arbitrary",), vmem=56 * 1024 * 1024),
    )(dproj, df, wt_e, wt_qkv, wt_f, x2, metapad, dh1, g1, *_in_hbm(*sc_in))
    return res[:3], res[3:3 + n], (res[3 + n] if sc is not None else None)


def _adamw(w, g, m, v, name):
    rows, cols = w.shape
    if rows % 8 == 0:
        tr, tc = _row_tile8(rows), cols
    else:
        tr, tc = rows, (2 * LANES if cols % (2 * LANES) == 0 and rows > 8 else cols)

    def body(w_ref, g_ref, m_ref, v_ref, d_ref, mo_ref, vo_ref):
        g_ = g_ref[...]
        m_new = ADAM_B1 * m_ref[...] + (1.0 - ADAM_B1) * g_
        v_new = ADAM_B2 * v_ref[...] + (1.0 - ADAM_B2) * (g_ * g_)
        m_hat = m_new / (1.0 - ADAM_B1 ** ADAM_STEP)
        v_hat = v_new / (1.0 - ADAM_B2 ** ADAM_STEP)
        d_ref[...] = -ADAM_LR * (m_hat / (jnp.sqrt(v_hat) + ADAM_EPS) + ADAM_WD * w_ref[...])
        mo_ref[...] = m_new
        vo_ref[...] = v_new

    blk = pl.BlockSpec((tr, tc), lambda i, j: (i, j))
    return pl.pallas_call(
        body, name=name, grid=(rows // tr, cols // tc),
        in_specs=[blk] * 4, out_specs=[blk] * 3,
        out_shape=[jax.ShapeDtypeStruct(w.shape, F32)] * 3,
        compiler_params=_params(("parallel", "parallel")),
    )(w, g, m, v)


def _adamw_native(w3, g3, m3, v3, name):
    rows = w3.shape[0]
    tr = rows // 2
    blk = pl.BlockSpec((tr,) + w3.shape[1:], lambda i: (i, 0, 0))
    shape = jax.ShapeDtypeStruct(w3.shape, F32)

    def moments(g_ref, m_ref, v_ref, mo_ref, vo_ref):
        g_ = g_ref[...]
        mo_ref[...] = ADAM_B1 * m_ref[...] + (1.0 - ADAM_B1) * g_
        vo_ref[...] = ADAM_B2 * v_ref[...] + (1.0 - ADAM_B2) * (g_ * g_)

    new_m, new_v = pl.pallas_call(
        moments, name=name + "_moments", grid=(2,), in_specs=[blk] * 3, out_specs=[blk] * 2, out_shape=[shape] * 2,
        compiler_params=_params(("parallel",)),
    )(g3, m3, v3)

    def delta(w_ref, m_ref, v_ref, d_ref):
        m_hat = m_ref[...] / (1.0 - ADAM_B1 ** ADAM_STEP)
        v_hat = v_ref[...] / (1.0 - ADAM_B2 ** ADAM_STEP)
        d_ref[...] = -ADAM_LR * (m_hat / (jnp.sqrt(v_hat) + ADAM_EPS) + ADAM_WD * w_ref[...])

    d = pl.pallas_call(
        delta, name=name + "_delta", grid=(2,), in_specs=[blk] * 3, out_specs=blk, out_shape=shape,
        compiler_params=_params(("parallel",)),
    )(w3, new_m, new_v)
    return d, new_m, new_v


def _row_tile8(rows):
    best = rows
    for t in range(8, 257, 8):
        if rows % t == 0:
            best = t
    return best


def kernel(x, meta_tokens, norm_g, w_in, b_forget, pool_w, pool_scale, w_up_pool, w_up_attn, w_out, final_norm_g, loss_target, m_meta_tokens, m_norm_g, m_w_in, m_b_forget, m_pool_w, m_pool_scale, m_w_up_pool, m_w_up_attn, m_w_out, m_final_norm_g, v_meta_tokens, v_norm_g, v_w_in, v_b_forget, v_pool_w, v_pool_scale, v_w_up_pool, v_w_up_attn, v_w_out, v_final_norm_g):
    seq = x.shape[1]
    assert seq % ROW_TILE == 0 and x.shape[0] == 1
    lp = seq + ROW_TILE
    nb = lp // ROW_TILE
    lk = -(-lp // KV_TILE_BWD) * KV_TILE_BWD
    core = jnp.reshape(lax.axis_index("c"), (1,)).astype(jnp.int32)
    x2 = x[0]
    target = loss_target[0]
    sh_d = D_MODEL // N_CHIPS

    to_rows = lambda a: jnp.transpose(a, (2, 0, 1))
    from_rows = lambda a: jnp.transpose(a, (1, 2, 0))
    gf = final_norm_g.reshape(1, D_MODEL)
    bfg = jnp.pad(b_forget, ((0, 0), (0, F_COLS - N_HEADS)))
    pw_b = pool_w[0].astype(BF16)
    hn, (wg_in, meta_g) = _rmsnorm_fwd(x2, norm_g, lk, [jnp.transpose(w_in[0]).astype(BF16), meta_tokens], [1, 0])
    wt = wg_in.reshape(-1, D_MODEL)
    wt_e = jnp.concatenate([wt[REF_U:REF_Q], wt[REF_ZA:REF_F], wt[REF_GP:]], axis=0)
    wt_qkv = wt[REF_Q:REF_ZA]
    wt_f = jnp.pad(wt[REF_F:REF_GP], ((0, F_COLS - N_HEADS), (0, 0)))
    meta_full = jnp.transpose(meta_g, (1, 0, 2)).reshape(N_META, D_MODEL)
    metapad = jnp.pad(meta_full, ((PAD_ROWS, 0), (0, 0)))

    tm = _row_tile(lp, 2200)
    e, (wg_up_p, wg_up_a, wg_out) = _mm_nt(
        hn, wt_e, BF16, "in_proj_gates", lp, E_COLS, tm, 512,
        gather=([w_up_pool[0].astype(BF16), w_up_attn[0].astype(BF16), w_out[0].astype(BF16)], [0, 0, 0]))
    wup_p = jnp.transpose(wg_up_p, (1, 0, 2)).reshape(POOL_WIDTH, D_MODEL)
    wup_a = jnp.transpose(wg_up_a, (1, 0, 2)).reshape(ATTN_WIDTH, D_MODEL)
    wout = wg_out.reshape(D_MODEL, D_MODEL)
    qkv = _mm_nt(hn, wt_qkv, BF16, "in_proj_qkv", lk, QKV_COLS, _row_tile(lk, 2600), 512, scale_first=HEAD_DIM ** -0.5)
    f = _mm_nt(hn, wt_f, F32, "in_proj_forget", lp, F_COLS, tm, F_COLS)
    c = _gates_fwd(f, bfg)
    c_t = jnp.transpose(c[:, :N_HEADS])
    bias = c_t[None, :, :] - jnp.transpose(c_t[:, ::ROW_TILE])[:, :, None]
    bias = jnp.where(jnp.arange(lp) < PAD_ROWS, -NEG, bias)
    bias = jnp.pad(bias, ((0, 0), (0, 0), (0, lk - lp)))
    by_kv = lambda t: jnp.transpose(bias.reshape(nb, N_HEADS // 2, 2, lk // t, t), (0, 1, 3, 2, 4))
    bias, bias_bwd = by_kv(KV_TILE), by_kv(KV_TILE_BWD)
    y_pool = _pool_fwd(e, pw_b, pool_scale)
    o, lse = _attn_fwd(qkv, bias, nb)
    merged, y_attn = _merge_fwd(y_pool, o, e, wup_p, wup_a)
    dh1, loss_part, dgf = _head_fwd_bwd(merged, wout, x2, metapad, gf, target)

    dap, daa, dproj_half, do, delta, dy_pool = _merge_bwd(dh1, wout, y_pool, y_attn, wup_p, wup_a, e, o)
    dpc, dzp, dscale, dpw = _pool_bwd_local(e, dy_pool, pw_b, pool_scale)
    du = _pool_bwd_window(dpc)
    dq, dk, dv, dc4, dcq = _attn_bwd(qkv, do, lse, delta, bias_bwd, nb)
    dc = jnp.transpose(dc4, (1, 3, 0, 2)).reshape(lk, N_HEADS)[:lp]
    df, db = _gates_bwd(jnp.pad(dc, ((0, 0), (0, F_COLS - N_HEADS))), dcq, f, bfg)
    dproj = _fill_dproj(dproj_half, du, dzp, dq, dk, dv)
    tk = _row_tile(lp, 1100)
    dw_out = _mm_tn(merged, dh1, "grad_w_out", lp, 512, D_MODEL, tk)
    dw_up_p = _mm_tn(y_pool, dap, "grad_w_up_pool", lp, 512, D_MODEL, lp, chunks=N_CHIPS)
    dw_up_a = _mm_tn(y_attn, daa, "grad_w_up_attn", lp, 512, D_MODEL, lp, chunks=N_CHIPS)
    dwt_f = _mm_tn(df, hn, "grad_w_in_forget", lp, F_COLS, D_MODEL, tk)

    def pad8(a):
        return jnp.pad(a, ((0, (-a.shape[0]) % 8), (0, 0)))

    small_parts = [pad8(a) for a in (dgf.reshape(-1, LANES), dscale.reshape(-1, LANES), db, dpw.reshape(-1, LANES),
                                     loss_part)]
    small = jnp.concatenate(small_parts, axis=0)
    soffs = [0]
    for p in small_parts:
        soffs.append(soffs[-1] + p.shape[0])

    gs_rows = [dw_up_p, dw_up_a, dw_out.reshape(N_CHIPS, sh_d, D_MODEL)]
    half = MAIN_COLS // 2
    dwt_a, (*rb_rows, rb_f, sr) = _mm_tn(dproj, hn, "grad_w_in_a", lp, half // 4, D_MODEL, lp, m=half, m_off=0,
                                         swap=(gs_rows + [dwt_f, small], [1, 1, 1, 1, None]))
    *hs_rows, sc = _add_sibling_half(gs_rows, rb_rows, [1, 1, 1], small, sr, core)
    dwt_b, (rb_a,), (qs_rows, sq) = _mm_tn(dproj, hn, "grad_w_in_b", lp, half // 4, D_MODEL, lp, m=half, m_off=4,
                                           swap=([dwt_a], [1]), scatter=(hs_rows, sc))
    (rb_b,) = _sibling_exchange([dwt_b], [1])
    h_a, h_b, h_f = _add_halves_2d([dwt_a, dwt_b, dwt_f], [rb_a, rb_b, rb_f], core)
    h_in = jnp.concatenate([h_a, h_b[:ATTN_WIDTH], h_f[:N_HEADS], h_b[ATTN_WIDTH:]], axis=0)
    grad_axes = [2, 1, 1, 1]
    (grad_x, g_block0, dg1), (q_in,), _ = _input_bwd(dproj, df, wt_e, wt_qkv, wt_f, x2, metapad, dh1, norm_g,
                                                     [h_in.reshape(N_CHIPS, -1, D_MODEL // 2)])
    g_w_in_rows, g_w_up_p, g_w_up_a, g_w_out, st = _reduce_allgather([q_in] + list(qs_rows), grad_axes, sq,
                                                                       [True, False, False, False])
    late = _small_allreduce(jnp.concatenate([pad8(dg1.reshape(-1, LANES)), g_block0[PAD_ROWS:].reshape(-1, LANES)], axis=0))
    n_norm = D_MODEL // LANES
    g_norm = late[:n_norm].reshape(1, D_MODEL)
    chip = 2 * lax.axis_index("x") + lax.axis_index("y")
    g_meta = lax.dynamic_slice_in_dim(late[n_norm:].reshape(N_META, D_MODEL), chip * sh_d, sh_d, axis=1)

    spiece = lambda k, rows: st[soffs[k]:soffs[k] + rows]
    g_final = spiece(0, D_MODEL // LANES).reshape(1, D_MODEL)
    g_scale = spiece(1, POOL_WIDTH // LANES).reshape(1, POOL_WIDTH)
    g_bf = spiece(2, 1)
    g_pw = spiece(3, POOL_WIDTH)
    loss = st[soffs[4], 0]

    def pad_lanes(a):
        return jnp.pad(a, ((0, 0), (0, F_COLS - N_HEADS)))

    w_in_res = (g_w_in_rows,) + _adamw_native(to_rows(w_in), g_w_in_rows, to_rows(m_w_in), to_rows(v_w_in), "adamw_w_in")

    upd = [
        ("meta_tokens", meta_tokens, g_meta, m_meta_tokens, v_meta_tokens),
        ("norm_g", norm_g, g_norm, m_norm_g, v_norm_g),
        ("w_in", None, None, None, None),
        ("b_forget", pad_lanes(b_forget), g_bf, pad_lanes(m_b_forget), pad_lanes(v_b_forget)),
        ("pool_w", pool_w.reshape(-1, LANES), g_pw, m_pool_w.reshape(-1, LANES), v_pool_w.reshape(-1, LANES)),
        ("pool_scale", pool_scale, g_scale, m_pool_scale, v_pool_scale),
        ("w_up_pool", w_up_pool[0], g_w_up_p, m_w_up_pool[0], v_w_up_pool[0]),
        ("w_up_attn", w_up_attn[0], g_w_up_a, m_w_up_attn[0], v_w_up_attn[0]),
        ("w_out", w_out[0], g_w_out, m_w_out[0], v_w_out[0]),
        ("final_norm_g", gf, g_final, m_final_norm_g.reshape(1, D_MODEL), v_final_norm_g.reshape(1, D_MODEL)),
    ]
    shapes = [meta_tokens.shape, norm_g.shape, w_in.shape, b_forget.shape, pool_w.shape, pool_scale.shape,
              w_up_pool.shape, w_up_attn.shape, w_out.shape, final_norm_g.shape]
    grads, deltas, new_ms, new_vs = [], [], [], []
    for (name, w_, g_, m_in, v_in), shp in zip(upd, shapes):
        if name == "w_in":
            res = tuple(from_rows(a) for a in w_in_res)
        else:
            d_, mn_, vn_ = _adamw(w_, g_, m_in, v_in, "adamw_" + name)
            res = (g_, d_, mn_, vn_)
        if name == "b_forget":
            res = tuple(a[:, :N_HEADS] for a in res)
        for lst, a in zip((grads, deltas, new_ms, new_vs), res):
            lst.append(a.reshape(shp))

    return (loss, grad_x.reshape(x.shape), *grads, *deltas, *new_ms, *new_vs)
```

```python
import jax
import jax.numpy as jnp
from jax import lax
from jax.experimental import pallas as pl
from jax.experimental.pallas import tpu as pltpu

F32 = jnp.float32
BF16 = jnp.bfloat16
MESH = pl.DeviceIdType.MESH
HIGHEST = lax.Precision.HIGHEST
HBM = pl.BlockSpec(memory_space=pltpu.HBM)

D_MODEL = 1024
N_META = 16
POOL_WIDTH = 512
POOL_GROUP = 128
POOL_WINDOWS = (2, 4, 8, 16)
N_HEADS = 8
HEAD_DIM = 64
ATTN_WIDTH = 512
RMS_EPS = 1e-6
N_CHIPS = 4

ADAM_LR = 0.001
ADAM_B1 = 0.9
ADAM_B2 = 0.999
ADAM_EPS = 1e-08
ADAM_WD = 0.01
ADAM_STEP = 10

LANES = 128
ROW_TILE = 256
KV_TILE = 1024
KV_TILE_BWD = 1024
PAD_ROWS = ROW_TILE - N_META
NEG = -1e30

E_U, E_ZP, E_ZA, E_GP, E_GA, E_COLS = 0, 512, 1024, 1536, 2560, 3584
QKV_COLS = 3 * ATTN_WIDTH
MAIN_COLS = E_COLS + QKV_COLS
F_COLS = LANES
REF_U, REF_Q, REF_ZA, REF_F, REF_GP = 0, 1024, 2560, 3072, 3080
VMEM_LIMIT = 48 * 1024 * 1024


def _params(sem=None, vmem=VMEM_LIMIT):
    return pltpu.CompilerParams(dimension_semantics=sem, vmem_limit_bytes=vmem)


def _in_hbm(*arrays):
    return [pltpu.with_memory_space_constraint(a, pltpu.HBM) for a in arrays]


def _row_tile(n, target):
    best = 16
    for t in range(16, target + 1, 16):
        if n % t == 0:
            best = t
    return best


def _sigmoid(x):
    return 1.0 / (1.0 + jnp.exp(-x))


def _half(ref, axis, which):
    n = ref.shape[axis] // 2
    idx = [slice(None)] * len(ref.shape)
    idx[axis] = pl.ds(pl.multiple_of(which * n, n), n)
    return ref.at[tuple(idx)]


def _half_shape(shape, axis):
    s = list(shape)
    s[axis] //= 2
    return tuple(s)


def _gather_start(ins, outs, axes, send, recv, fsend, frecv, local):
    x, y, c = lax.axis_index("x"), lax.axis_index("y"), lax.axis_index("c")
    me = 2 * x + y
    for t in range(len(ins)):
        pltpu.make_async_copy(ins[t], outs[t].at[me], local.at[t]).start()
        for k, chip in enumerate([(1 - x, y), (x, 1 - y), (1 - x, 1 - y)]):
            pltpu.make_async_remote_copy(
                src_ref=_half(ins[t], axes[t], c), dst_ref=_half(outs[t].at[me], axes[t], c),
                send_sem=send.at[3 * t + k], recv_sem=recv.at[3 * t + k], device_id=(*chip, c), device_id_type=MESH).start()


def _gather_finish(ins, outs, axes, send, recv, fsend, frecv, local):
    x, y, c = lax.axis_index("x"), lax.axis_index("y"), lax.axis_index("c")
    me = 2 * x + y
    sibling = (x, y, 1 - c)
    chips = [(1 - x, y), (x, 1 - y), (1 - x, 1 - y)]
    n = len(ins)

    def over_ici(t, k, chip, dst_slot):
        return pltpu.make_async_remote_copy(
            src_ref=_half(ins[t], axes[t], c), dst_ref=_half(outs[t].at[dst_slot], axes[t], c),
            send_sem=send.at[3 * t + k], recv_sem=recv.at[3 * t + k], device_id=(*chip, c), device_id_type=MESH)

    def to_sibling(t, k, slot, which):
        return pltpu.make_async_remote_copy(
            src_ref=_half(outs[t].at[slot], axes[t], which), dst_ref=_half(outs[t].at[slot], axes[t], which),
            send_sem=fsend.at[3 * t + k], recv_sem=frecv.at[3 * t + k], device_id=sibling, device_id_type=MESH)

    forwards = []
    for t in range(n):
        for k, (px, py) in enumerate(chips):
            over_ici(t, k, (px, py), 2 * px + py).wait_recv()
            fw = to_sibling(t, k, 2 * px + py, c)
            fw.start()
            forwards.append(fw)
    for t in range(n):
        for k, (px, py) in enumerate(chips):
            to_sibling(t, k, 2 * px + py, 1 - c).wait_recv()
    for t in range(n):
        for k, chip in enumerate(chips):
            over_ici(t, k, chip, me).wait_send()
    for fw in forwards:
        fw.wait_send()
    for t in range(n):
        pltpu.make_async_copy(ins[t], outs[t].at[me], local.at[t]).wait()


def _gather_sems(n):
    return [pltpu.SemaphoreType.DMA((3 * n,)), pltpu.SemaphoreType.DMA((3 * n,)), pltpu.SemaphoreType.DMA((3 * n,)),
            pltpu.SemaphoreType.DMA((3 * n,)), pltpu.SemaphoreType.DMA((n,))]


def _gathered_shapes(shards):
    return [jax.ShapeDtypeStruct((N_CHIPS,) + s.shape, s.dtype) for s in shards]


def _swap_copies(srcs, dsts, axes, send, recv):
    x, y, c = lax.axis_index("x"), lax.axis_index("y"), lax.axis_index("c")
    return [pltpu.make_async_remote_copy(
        src_ref=srcs[t] if axes[t] is None else _half(srcs[t], axes[t], 1 - c), dst_ref=dsts[t],
        send_sem=send.at[t], recv_sem=recv.at[t], device_id=(x, y, 1 - c), device_id_type=MESH) for t in range(len(srcs))]


def _swap_shapes(srcs, axes):
    return [jax.ShapeDtypeStruct(g.shape if a is None else _half_shape(g.shape, a), g.dtype) for g, a in zip(srcs, axes)]


def _sibling_exchange(gs, axes):
    n = len(gs)

    def body(*refs):
        cps = _swap_copies(refs[:n], refs[n:2 * n], axes, *refs[2 * n:])
        for cp in cps:
            cp.start()
        for cp in cps:
            cp.wait()

    return pl.pallas_call(
        body, name="grad_sibling_exchange",
        in_specs=[HBM] * n, out_specs=[HBM] * n, out_shape=_swap_shapes(gs, axes),
        scratch_shapes=[pltpu.SemaphoreType.DMA((n,)), pltpu.SemaphoreType.DMA((n,))],
    )(*_in_hbm(*gs))


def _add_halves_2d(parts, rbs, small, sr, core):
    n = len(parts)
    rows, w = rbs[0].shape
    tr = 512
    last_rows = rbs[-1].shape[0]

    def body(core_ref, *refs):
        p_refs, r_refs = refs[:n], refs[n:2 * n]
        s_ref, sr_ref = refs[2 * n], refs[2 * n + 1]
        h_refs, sc_ref = refs[2 * n + 2:3 * n + 2], refs[3 * n + 2]
        for t in range(n - 1):
            h_refs[t][...] = (p_refs[t][...] + r_refs[t][...]).astype(BF16)

        @pl.when(pl.program_id(0) == 0)
        def _():
            h_refs[n - 1][...] = (p_refs[n - 1][...] + r_refs[n - 1][...]).astype(BF16)
            sc_ref[...] = s_ref[...] + sr_ref[...]

    mine = lambda r: pl.BlockSpec((r, w), lambda i, cr: (i, cr[0]))
    tile = lambda r: pl.BlockSpec((r, w), lambda i, cr: (i, 0))
    mine_small = pl.BlockSpec((last_rows, w), lambda i, cr: (0, cr[0]))
    tile_small = pl.BlockSpec((last_rows, w), lambda i, cr: (0, 0))
    whole = pl.BlockSpec(small.shape, lambda i, cr: (0, 0))
    return pl.pallas_call(
        body, name="grad_add_sibling_w_in",
        grid_spec=pltpu.PrefetchScalarGridSpec(
            num_scalar_prefetch=1, grid=(rows // tr,),
            in_specs=[mine(tr)] * (n - 1) + [mine_small] + [tile(tr)] * (n - 1) + [tile_small, whole, whole],
            out_specs=[tile(tr)] * (n - 1) + [tile_small, whole]),
        out_shape=[jax.ShapeDtypeStruct(r.shape, BF16) for r in rbs] + [jax.ShapeDtypeStruct(small.shape, F32)],
        compiler_params=_params(("arbitrary",)),
    )(core, *parts, *rbs, small, sr)


def _add_sibling_half(gs, rbs, axes, core):
    n = len(gs)

    def body(core_ref, *refs):
        g_refs, rb_refs, h_refs = refs[:n], refs[n:2 * n], refs[2 * n:]
        for t in range(n):
            h_refs[t][...] = (g_refs[t][...] + rb_refs[t][...]).astype(BF16)

    def mine(rb, axis):
        blk = (None,) + rb.shape[1:]
        if axis == 2:
            return pl.BlockSpec(blk, lambda j, cr: (j, 0, cr[0]))
        return pl.BlockSpec(blk, lambda j, cr: (j, cr[0], 0))

    chunk = lambda rb: pl.BlockSpec((None,) + rb.shape[1:], lambda j, cr: (j, 0, 0))
    return pl.pallas_call(
        body, name="grad_add_sibling",
        grid_spec=pltpu.PrefetchScalarGridSpec(
            num_scalar_prefetch=1, grid=(N_CHIPS,),
            in_specs=[mine(rb, a) for rb, a in zip(rbs, axes)] + [chunk(rb) for rb in rbs],
            out_specs=[chunk(rb) for rb in rbs]),
        out_shape=[jax.ShapeDtypeStruct(rb.shape, BF16) for rb in rbs],
        compiler_params=_params(("arbitrary",)),
    )(core, *gs, *rbs)


def _chip_scatter_plan(h_refs, s_ref, q_refs, sq_ref, send, recv, local):
    n = len(h_refs)
    nt = n + (s_ref is not None)
    x, y, c = lax.axis_index("x"), lax.axis_index("y"), lax.axis_index("c")
    me = 2 * x + y
    chips = [(1 - x, y), (x, 1 - y), (1 - x, 1 - y)]

    def copy(t, k, chip, src_slot, dst_slot):
        src = h_refs[t].at[src_slot] if t < n else s_ref
        dst = (q_refs[t] if t < n else sq_ref).at[dst_slot]
        return pltpu.make_async_remote_copy(src_ref=src, dst_ref=dst, send_sem=send.at[3 * t + k],
                                            recv_sem=recv.at[3 * t + k], device_id=(*chip, c), device_id_type=MESH)

    own = [pltpu.make_async_copy(h_refs[t].at[me], q_refs[t].at[me], local.at[t]) for t in range(n)]
    if s_ref is not None:
        own.append(pltpu.make_async_copy(s_ref, sq_ref.at[me], local.at[n]))
    sends = [copy(t, k, (px, py), 2 * px + py, me) for t in range(nt) for k, (px, py) in enumerate(chips)]
    recvs = [copy(t, k, (px, py), me, 2 * px + py) for t in range(nt) for k, (px, py) in enumerate(chips)]
    return own, sends, recvs


def _scatter_operands(hs, sc):
    small = [] if sc is None else [sc]
    nt = len(hs) + len(small)
    shapes = [jax.ShapeDtypeStruct(h.shape, h.dtype) for h in hs] + [
        jax.ShapeDtypeStruct((N_CHIPS,) + a.shape, a.dtype) for a in small]
    sems = [pltpu.SemaphoreType.DMA((3 * nt,)), pltpu.SemaphoreType.DMA((3 * nt,)), pltpu.SemaphoreType.DMA((nt,))]
    return list(hs) + small, shapes, sems


def _small_allreduce(buf):
    def body(b_ref, o_ref, sib_buf, chip_buf, send, recv):
        x, y, c = lax.axis_index("x"), lax.axis_index("y"), lax.axis_index("c")
        me = 2 * x + y
        chips = [(1 - x, y), (x, 1 - y), (1 - x, 1 - y)]
        swap = pltpu.make_async_remote_copy(src_ref=b_ref, dst_ref=sib_buf, send_sem=send.at[0], recv_sem=recv.at[0],
                                            device_id=(x, y, 1 - c), device_id_type=MESH)
        swap.start()
        swap.wait()
        chip_buf[me] = b_ref[...] + sib_buf[...]

        def copy(k, chip, slot):
            return pltpu.make_async_remote_copy(src_ref=chip_buf.at[slot], dst_ref=chip_buf.at[slot], send_sem=send.at[1 + k],
                                                recv_sem=recv.at[1 + k], device_id=(*chip, c), device_id_type=MESH)

        sends = [copy(k, chip, me) for k, chip in enumerate(chips)]
        for cp in sends:
            cp.start()
        for k, (px, py) in enumerate(chips):
            copy(k, (px, py), 2 * px + py).wait_recv()
        for cp in sends:
            cp.wait_send()
        o_ref[...] = ((chip_buf[0] + chip_buf[1]) + chip_buf[2]) + chip_buf[3]

    vmem = pl.BlockSpec(memory_space=pltpu.VMEM)
    return pl.pallas_call(
        body, name="small_allreduce", in_specs=[vmem], out_specs=vmem,
        out_shape=jax.ShapeDtypeStruct(buf.shape, F32),
        scratch_shapes=[pltpu.VMEM(buf.shape, F32), pltpu.VMEM((N_CHIPS,) + buf.shape, F32),
                        pltpu.SemaphoreType.DMA((4,)), pltpu.SemaphoreType.DMA((4,))],
        compiler_params=_params(),
    )(buf)


def _reduce_allgather(qs, axes, sq, as_rows):
    n = len(qs)
    shard_shapes, half_axes = [], []
    for q, a, rows_form in zip(qs, axes, as_rows):
        shape = [d * 2 if i == a - 1 else d for i, d in enumerate(q.shape[1:])]
        if rows_form:
            assert a == 2
            shape = [shape[0], 1, shape[1]]
        shard_shapes.append(tuple(shape))
        half_axes.append(2 if rows_form else a - 1)

    def body(*refs):
        q_refs, sq_ref = refs[:n], refs[n]
        o_refs, st_ref = refs[n + 1:2 * n + 1], refs[2 * n + 1]
        send, recv = refs[2 * n + 2:]
        x, y, c = lax.axis_index("x"), lax.axis_index("y"), lax.axis_index("c")

        def swap(t, which):
            return pltpu.make_async_remote_copy(
                src_ref=_half(o_refs[t], half_axes[t], which), dst_ref=_half(o_refs[t], half_axes[t], which),
                send_sem=send.at[t], recv_sem=recv.at[t], device_id=(x, y, 1 - c), device_id_type=MESH)

        sent = []
        for t in range(n):
            q = q_refs[t]
            total = ((q[0].astype(F32) + q[1].astype(F32)) + q[2].astype(F32)) + q[3].astype(F32)
            if as_rows[t]:
                total = total.reshape(total.shape[0], 1, total.shape[1])
            _half(o_refs[t], half_axes[t], c)[...] = total
            cp = swap(t, c)
            cp.start()
            sent.append(cp)
        st_ref[...] = ((sq_ref[0] + sq_ref[1]) + sq_ref[2]) + sq_ref[3]
        for t in range(n):
            swap(t, 1 - c).wait_recv()
        for cp in sent:
            cp.wait_send()

    vmem = pl.BlockSpec(memory_space=pltpu.VMEM)
    return pl.pallas_call(
        body, name="grad_reduce_allgather",
        in_specs=[vmem] * (n + 1), out_specs=[vmem] * (n + 1),
        out_shape=[jax.ShapeDtypeStruct(s, F32) for s in shard_shapes] + [jax.ShapeDtypeStruct(sq.shape[1:], F32)],
        scratch_shapes=[pltpu.SemaphoreType.DMA((n,)), pltpu.SemaphoreType.DMA((n,))],
        compiler_params=_params(),
    )(*qs, sq)


def _dot_nt(a, b):
    return lax.dot_general(a, b, (((1,), (1,)), ((), ())), preferred_element_type=F32)


def _dot_tn(a, b):
    return lax.dot_general(a, b, (((0,), (0,)), ((), ())), preferred_element_type=F32)


def _mm_nt(a, bt, out_dtype, name, m, n, tm, tn, row_block=0, scale_first=None, gather=None):
    k = a.shape[1]
    shards, axes = gather if gather is not None else ([], [])
    ng = len(shards)
    grid = (m // tm, n // tn)

    def body(a_ref, b_ref, *rest):
        ins, o_ref, outs, sems = rest[:ng], rest[ng], rest[ng + 1:2 * ng + 1], rest[2 * ng + 1:]
        first = (pl.program_id(0) == 0) & (pl.program_id(1) == 0)
        last = (pl.program_id(0) == grid[0] - 1) & (pl.program_id(1) == grid[1] - 1)
        if ng:
            @pl.when(first)
            def _():
                _gather_start(ins, outs, axes, *sems)

        r = _dot_nt(a_ref[...], b_ref[...])
        if scale_first is not None:
            r = r * jnp.where(pl.program_id(1) == 0, scale_first, 1.0)
        o_ref[...] = r.astype(out_dtype)

        if ng:
            @pl.when(last)
            def _():
                _gather_finish(ins, outs, axes, *sems)

    res = pl.pallas_call(
        body, name=name, grid=grid,
        in_specs=[pl.BlockSpec((tm, k), lambda i, j: (i, 0)), pl.BlockSpec((tn, k), lambda i, j: (row_block + j, 0))]
        + [HBM] * ng,
        out_specs=[pl.BlockSpec((tm, tn), lambda i, j: (i, j))] + [HBM] * ng,
        out_shape=[jax.ShapeDtypeStruct((m, n), out_dtype)] + _gathered_shapes(shards),
        scratch_shapes=_gather_sems(ng) if ng else [],
        compiler_params=_params(("arbitrary", "arbitrary") if ng else ("parallel", "parallel")),
    )(a, bt, *_in_hbm(*shards))
    return (res[0], res[1:]) if ng else res[0]


def _mm_tn(a, b, name, k, tm, tn, tk, chunks=1, m=None, m_off=0, swap=None, scatter=None):
    m = a.shape[1] if m is None else m
    n = b.shape[1]
    cw = n // chunks
    srcs, axes = swap if swap is not None else ([], [])
    ns = len(srcs)
    hs, sc = scatter if scatter is not None else ([], None)
    sc_in, sc_shapes, sc_sems = _scatter_operands(hs, sc) if scatter is not None else ([], [], [])
    nh, nq = len(hs), len(sc_in)
    grid = (m // tm, n // tn, k // tk)

    def body(a_ref, b_ref, *rest):
        s_refs, h_refs = rest[:ns], rest[ns:ns + nq]
        o_ref = rest[ns + nq]
        d_refs, q_refs = rest[ns + nq + 1:2 * ns + nq + 1], rest[2 * ns + nq + 1:2 * ns + 2 * nq + 1]
        sems = rest[2 * ns + 2 * nq + 1:]
        swap_sems, scatter_sems = (sems[:2], sems[2:]) if ns else ((), sems)
        ids = [pl.program_id(d) for d in range(3)]

        def scatter_plan():
            small_in = h_refs[nh] if sc is not None else None
            small_out = q_refs[nh] if sc is not None else None
            return _chip_scatter_plan(h_refs[:nh], small_in, q_refs[:nh], small_out, *scatter_sems)

        if ns or nq:
            @pl.when((ids[0] == 0) & (ids[1] == 0) & (ids[2] == 0))
            def _():
                if ns:
                    for cp in _swap_copies(s_refs, d_refs, axes, *swap_sems):
                        cp.start()
                if nq:
                    own, sends, _ = scatter_plan()
                    for cp in own + sends:
                        cp.start()

        @pl.when(ids[2] == 0)
        def _():
            o_ref[...] = jnp.zeros_like(o_ref)
        r = _dot_tn(a_ref[...].astype(BF16), b_ref[...].astype(BF16))
        if chunks > 1:
            for c in range(chunks):
                o_ref[c] += r[:, c * cw:(c + 1) * cw]
        else:
            o_ref[...] += r

        if ns or nq:
            @pl.when((ids[0] == grid[0] - 1) & (ids[1] == grid[1] - 1) & (ids[2] == grid[2] - 1))
            def _():
                if ns:
                    for cp in _swap_copies(s_refs, d_refs, axes, *swap_sems):
                        cp.wait()
                if nq:
                    own, sends, recvs = scatter_plan()
                    for cp in recvs:
                        cp.wait_recv()
                    for cp in sends:
                        cp.wait_send()
                    for cp in own:
                        cp.wait()

    if chunks > 1:
        assert tn == n
        out_spec = pl.BlockSpec((chunks, tm, cw), lambda i, j, kk: (0, i, 0))
        out_shape = jax.ShapeDtypeStruct((chunks, m, cw), F32)
    else:
        out_spec = pl.BlockSpec((tm, tn), lambda i, j, kk: (i, j))
        out_shape = jax.ShapeDtypeStruct((m, n), F32)
    riders = ns + nq
    res = pl.pallas_call(
        body, name=name, grid=grid,
        in_specs=[pl.BlockSpec((tk, tm), lambda i, j, kk: (kk, m_off + i)), pl.BlockSpec((tk, tn), lambda i, j, kk: (kk, j))]
        + [HBM] * riders,
        out_specs=[out_spec] + [HBM] * riders, out_shape=[out_shape] + _swap_shapes(srcs, axes) + sc_shapes,
        scratch_shapes=([pltpu.SemaphoreType.DMA((ns,)), pltpu.SemaphoreType.DMA((ns,))] if ns else []) + sc_sems,
        compiler_params=_params(("arbitrary",) * 3 if riders else ("parallel", "parallel", "arbitrary")),
    )(a, b, *_in_hbm(*srcs, *sc_in))
    if not riders:
        return res[0]
    out = [res[0]]
    if ns:
        out.append(res[1:1 + ns])
    if nq:
        out.append((res[1 + ns:1 + ns + nh], res[1 + ns + nh] if sc is not None else None))
    return tuple(out)


def _tokens_spec():
    return pl.BlockSpec((ROW_TILE, D_MODEL), lambda i: (jnp.maximum(i - 1, 0), 0))


def _rmsnorm_fwd(x2, g1, lk, shards, axes):
    nb = x2.shape[0] // ROW_TILE + 1
    nblk = lk // ROW_TILE
    ng = len(shards)
    meta_cols = shards[-1].shape[1]

    def body(x_ref, g_ref, *rest):
        ins, hn_ref, outs = rest[:ng], rest[ng], rest[ng + 1:2 * ng + 1]
        sems, meta_buf, meta_sem = rest[2 * ng + 1:2 * ng + 6], rest[2 * ng + 6], rest[2 * ng + 7]
        s = pl.program_id(0)
        blk = (s + 1) % nblk

        @pl.when(s == 0)
        def _():
            _gather_start(ins, outs, axes, *sems)

        def normed(h):
            r = lax.rsqrt(jnp.mean(h * h, axis=-1, keepdims=True) + RMS_EPS)
            return ((h * r) * g_ref[...]).astype(BF16)

        @pl.when(s < nblk - 1)
        def _():
            hn_ref[...] = normed(jnp.where(blk >= nb, 0.0, x_ref[...]))

        @pl.when(s == nblk - 1)
        def _():
            _gather_finish(ins, outs, axes, *sems)
            fetch = pltpu.make_async_copy(outs[-1], meta_buf, meta_sem.at[0])
            fetch.start()
            fetch.wait()
            meta = jnp.concatenate([meta_buf[j] for j in range(N_CHIPS)], axis=1)
            hn_ref[...] = normed(jnp.concatenate([jnp.zeros((PAD_ROWS, D_MODEL), F32), meta], axis=0))

    res = pl.pallas_call(
        body, name="rmsnorm_fwd", grid=(nblk,),
        in_specs=[pl.BlockSpec((ROW_TILE, D_MODEL), lambda s: (jnp.clip((s + 1) % nblk - 1, 0, nb - 2), 0)),
                  pl.BlockSpec((1, D_MODEL), lambda s: (0, 0))] + [HBM] * ng,
        out_specs=[pl.BlockSpec((ROW_TILE, D_MODEL), lambda s: ((s + 1) % nblk, 0))] + [HBM] * ng,
        out_shape=[jax.ShapeDtypeStruct((lk, D_MODEL), BF16)] + _gathered_shapes(shards),
        scratch_shapes=_gather_sems(ng) + [pltpu.VMEM((N_CHIPS, N_META, meta_cols), F32), pltpu.SemaphoreType.DMA((1,))],
        compiler_params=_params(("arbitrary",)),
    )(x2, g1, *_in_hbm(*shards))
    return res[0], res[1:]


def _valid_gate_mask(i):
    row = i * ROW_TILE + lax.broadcasted_iota(jnp.int32, (ROW_TILE, F_COLS), 0)
    col = lax.broadcasted_iota(jnp.int32, (ROW_TILE, F_COLS), 1)
    return (row >= PAD_ROWS) & (col < N_HEADS)


def _gates_fwd(f, bfg):
    nb = f.shape[0] // ROW_TILE

    def body(f_ref, b_ref, c_ref, carry):
        i = pl.program_id(0)

        @pl.when(i == 0)
        def _():
            carry[...] = jnp.zeros_like(carry)

        logit = f_ref[...] + b_ref[...]
        lf = jnp.minimum(logit, 0.0) - jnp.log1p(jnp.exp(-jnp.abs(logit)))
        lf = jnp.where(_valid_gate_mask(i), lf, 0.0)
        r_i = lax.broadcasted_iota(jnp.int32, (ROW_TILE, ROW_TILE), 0)
        c_i = lax.broadcasted_iota(jnp.int32, (ROW_TILE, ROW_TILE), 1)
        tri = (c_i <= r_i).astype(F32)
        c_ref[...] = jnp.dot(tri, lf, precision=HIGHEST, preferred_element_type=F32) + carry[...]
        carry[...] = carry[...] + jnp.sum(lf, axis=0, keepdims=True)

    return pl.pallas_call(
        body, name="gates_fwd", grid=(nb,),
        in_specs=[pl.BlockSpec((ROW_TILE, F_COLS), lambda i: (i, 0)), pl.BlockSpec((1, F_COLS), lambda i: (0, 0))],
        out_specs=pl.BlockSpec((ROW_TILE, F_COLS), lambda i: (i, 0)),
        out_shape=jax.ShapeDtypeStruct(f.shape, F32),
        scratch_shapes=[pltpu.VMEM((1, F_COLS), F32)],
        compiler_params=_params(("arbitrary",)),
    )(f, bfg)


def _pool_counts(i):
    row = i * ROW_TILE + lax.broadcasted_iota(jnp.int32, (ROW_TILE, 1), 0)
    return jnp.maximum(row - PAD_ROWS, 0)


def _trailing_sums(xc, levels):
    acc = xc
    for lv in range(levels):
        acc = acc + pltpu.roll(acc, 1 << lv, 0)
    return acc


def _leading_sums(xc, levels):
    n = xc.shape[0]
    acc = xc
    for lv in range(levels):
        acc = acc + pltpu.roll(acc, n - (1 << lv), 0)
    return acc


def _pool_p(u_cur, u_prev, i):
    pos = _pool_counts(i)
    ps, invs = [], []
    for g, w in enumerate(POOL_WINDOWS):
        sl = slice(g * POOL_GROUP, (g + 1) * POOL_GROUP)
        cur = u_cur[:, sl]
        xc = jnp.concatenate([u_prev[:, sl], cur], axis=0)
        win = _trailing_sums(xc, g + 1)[ROW_TILE:, :]
        inv = 1.0 / jnp.minimum(pos + 1, w).astype(F32)
        ps.append(win * inv - cur)
        invs.append(inv)
    return ps, invs


def _pool_fwd(e, pw, scale):
    nb = e.shape[0] // ROW_TILE

    def body(uc_ref, up_ref, z_ref, pw_ref, sc_ref, y_ref):
        i = pl.program_id(0)
        u_cur = uc_ref[...].astype(F32)
        u_prev = jnp.where(i == 0, 0.0, up_ref[...].astype(F32))
        ps, _ = _pool_p(u_cur, u_prev, i)
        z = z_ref[...].astype(F32)
        gate = z * _sigmoid(z)
        for g in range(len(POOL_WINDOWS)):
            sl = slice(g * POOL_GROUP, (g + 1) * POOL_GROUP)
            yraw = jnp.dot(ps[g].astype(BF16), pw_ref[g], preferred_element_type=F32)
            y_ref[:, sl] = ((yraw * sc_ref[:, sl]) * gate[:, sl]).astype(BF16)

    blk = (ROW_TILE, POOL_WIDTH)
    return pl.pallas_call(
        body, name="pool_fwd", grid=(nb,),
        in_specs=[pl.BlockSpec(blk, lambda i: (i, 0)), pl.BlockSpec(blk, lambda i: (jnp.maximum(i - 1, 0), 0)),
                  pl.BlockSpec(blk, lambda i: (i, 1)),
                  pl.BlockSpec((len(POOL_WINDOWS), POOL_GROUP, POOL_GROUP), lambda i: (0, 0, 0)),
                  pl.BlockSpec((1, POOL_WIDTH), lambda i: (0, 0))],
        out_specs=pl.BlockSpec(blk, lambda i: (i, 0)),
        out_shape=jax.ShapeDtypeStruct((e.shape[0], POOL_WIDTH), BF16),
        compiler_params=_params(("parallel",)),
    )(e, e, e, pw, scale)


def _stack_heads(a):
    first = lax.broadcasted_iota(jnp.int32, a.shape, 1) < HEAD_DIM
    zero = jnp.zeros_like(a)
    return jnp.concatenate([jnp.where(first, a, zero), jnp.where(first, zero, a)], axis=0)


def _unstack_heads(a):
    rows = a.shape[0] // 2
    first = lax.broadcasted_iota(jnp.int32, (rows, LANES), 1) < HEAD_DIM
    return jnp.where(first, a[:rows], a[rows:])


def _causal(i, jj, stacked, kv_tile):
    r = lax.broadcasted_iota(jnp.int32, (stacked * ROW_TILE, kv_tile), 0)
    if stacked == 2:
        r = jnp.where(r >= ROW_TILE, r - ROW_TILE, r)
    kidx = jj * kv_tile + lax.broadcasted_iota(jnp.int32, (stacked * ROW_TILE, kv_tile), 1)
    return kidx <= i * ROW_TILE + r


def _stack_rows(b):
    n = b.shape[1]
    return jnp.concatenate([jnp.broadcast_to(b[0:1], (ROW_TILE, n)), jnp.broadcast_to(b[1:2], (ROW_TILE, n))], axis=0)


def _attn_fwd(qkv, bias, nb):
    lk = qkv.shape[0]
    lp = nb * ROW_TILE
    nkb = lk // KV_TILE
    n_pairs = N_HEADS // 2

    def body(q_ref, k_ref, v_ref, b_ref, o_ref, lse_ref):
        i = pl.program_id(1)
        qs = _stack_heads(q_ref[...])
        last = (i * ROW_TILE) // KV_TILE

        def block(jj, carry, masked, n_keys=KV_TILE):
            m, l, acc = carry
            rows = pl.ds(pl.multiple_of(jj * KV_TILE, KV_TILE), n_keys)
            s = _dot_nt(qs, k_ref[rows, :]) - _stack_rows(b_ref[0, 0, jj][:, :n_keys])
            if masked:
                s = jnp.where(_causal(i, jj, 2, KV_TILE)[:, :n_keys], s, NEG)
            m_new = jnp.maximum(m, jnp.max(s, axis=1, keepdims=True))
            alpha = jnp.exp(m - m_new)
            p = jnp.exp(s - m_new)
            l = alpha * l + jnp.sum(p, axis=1, keepdims=True)
            acc = alpha * acc + jnp.dot(p.astype(BF16), v_ref[rows, :], preferred_element_type=F32)
            return m_new, l, acc

        init = (jnp.full((2 * ROW_TILE, 1), NEG, F32), jnp.zeros((2 * ROW_TILE, 1), F32),
                jnp.zeros((2 * ROW_TILE, LANES), F32))
        carry = lax.fori_loop(0, last, lambda jj, c: block(jj, c, False), init)
        ends = [lambda c, n=(r + 1) * ROW_TILE: block(last, c, True, n) for r in range(KV_TILE // ROW_TILE)]
        m, l, acc = lax.switch(i - last * (KV_TILE // ROW_TILE), ends, carry)
        o_ref[...] = _unstack_heads(acc / l)
        lse_ref[...] = _unstack_heads(jnp.broadcast_to(m + jnp.log(l), (2 * ROW_TILE, LANES)))

    return pl.pallas_call(
        body, name="attn_fwd", grid=(n_pairs, nb),
        in_specs=[pl.BlockSpec((ROW_TILE, LANES), lambda hp, i: (i, hp)),
                  pl.BlockSpec((lk, LANES), lambda hp, i: (0, n_pairs + hp)),
                  pl.BlockSpec((lk, LANES), lambda hp, i: (0, 2 * n_pairs + hp)),
                  pl.BlockSpec((1, 1, nkb, 2, KV_TILE), lambda hp, i: (i, hp, 0, 0, 0))],
        out_specs=[pl.BlockSpec((ROW_TILE, LANES), lambda hp, i: (i, hp)),
                   pl.BlockSpec((ROW_TILE, LANES), lambda hp, i: (i, hp))],
        out_shape=[jax.ShapeDtypeStruct((lp, ATTN_WIDTH), F32), jax.ShapeDtypeStruct((lp, ATTN_WIDTH), F32)],
        compiler_params=_params(("parallel", "parallel")),
    )(qkv, qkv, qkv, bias)


def _merge_fwd(y_pool, o, e, wup_p, wup_a):
    lp = o.shape[0]
    nb = lp // ROW_TILE

    def body(yp_ref, o_ref, e_ref, wp_ref, wa_ref, mg_ref, ya_ref):
        za = e_ref[:, E_ZA:E_GP].astype(F32)
        ya = (o_ref[...] * (za * _sigmoid(za))).astype(BF16)
        ya_ref[...] = ya
        a_pool = jnp.dot(yp_ref[...], wp_ref[...], preferred_element_type=F32)
        a_attn = jnp.dot(ya, wa_ref[...], preferred_element_type=F32)
        mg_ref[...] = (_sigmoid(e_ref[:, E_GP:E_GA].astype(F32)) * a_pool
                       + _sigmoid(e_ref[:, E_GA:E_COLS].astype(F32)) * a_attn).astype(BF16)

    return pl.pallas_call(
        body, name="merge_fwd", grid=(nb,),
        in_specs=[pl.BlockSpec((ROW_TILE, POOL_WIDTH), lambda i: (i, 0)),
                  pl.BlockSpec((ROW_TILE, ATTN_WIDTH), lambda i: (i, 0)),
                  pl.BlockSpec((ROW_TILE, E_COLS), lambda i: (i, 0)),
                  pl.BlockSpec((POOL_WIDTH, D_MODEL), lambda i: (0, 0)),
                  pl.BlockSpec((ATTN_WIDTH, D_MODEL), lambda i: (0, 0))],
        out_specs=[pl.BlockSpec((ROW_TILE, D_MODEL), lambda i: (i, 0)),
                   pl.BlockSpec((ROW_TILE, ATTN_WIDTH), lambda i: (i, 0))],
        out_shape=[jax.ShapeDtypeStruct((lp, D_MODEL), BF16), jax.ShapeDtypeStruct((lp, ATTN_WIDTH), BF16)],
        compiler_params=_params(("parallel",)),
    )(y_pool, o, e, wup_p, wup_a)


def _head_fwd_bwd(merged, w_out, x2, metapad, gf, target):
    lp = merged.shape[0]
    nb = lp // ROW_TILE

    def body(mg_ref, w_ref, x_ref, mp_ref, g_ref, t_ref, dh_ref, loss_ref, dg_ref):
        i = pl.program_id(0)

        @pl.when(i == 0)
        def _():
            loss_ref[...] = jnp.zeros_like(loss_ref)
            dg_ref[...] = jnp.zeros_like(dg_ref)

        h0 = jnp.where(i == 0, mp_ref[...], x_ref[...])
        h1 = h0 + jnp.dot(mg_ref[...], w_ref[...], preferred_element_type=F32)
        r = lax.rsqrt(jnp.mean(h1 * h1, axis=-1, keepdims=True) + RMS_EPS)
        xhat = h1 * r
        g = g_ref[...]
        err = jnp.where(i == 0, 0.0, xhat * g - t_ref[...])
        loss_ref[...] += 0.5 * jnp.sum(jnp.mean(err * err, axis=-1, keepdims=True))
        dy = err / D_MODEL
        dg_ref[...] += jnp.sum(dy * xhat, axis=0, keepdims=True)
        dxhat = dy * g
        dh_ref[...] = r * (dxhat - xhat * jnp.mean(dxhat * xhat, axis=-1, keepdims=True))

    return pl.pallas_call(
        body, name="head_fwd_bwd", grid=(nb,),
        in_specs=[pl.BlockSpec((ROW_TILE, D_MODEL), lambda i: (i, 0)),
                  pl.BlockSpec((D_MODEL, D_MODEL), lambda i: (0, 0)),
                  _tokens_spec(), pl.BlockSpec((ROW_TILE, D_MODEL), lambda i: (0, 0)),
                  pl.BlockSpec((1, D_MODEL), lambda i: (0, 0)), _tokens_spec()],
        out_specs=[pl.BlockSpec((ROW_TILE, D_MODEL), lambda i: (i, 0)),
                   pl.BlockSpec((1, LANES), lambda i: (0, 0)), pl.BlockSpec((1, D_MODEL), lambda i: (0, 0))],
        out_shape=[jax.ShapeDtypeStruct((lp, D_MODEL), F32), jax.ShapeDtypeStruct((1, LANES), F32),
                   jax.ShapeDtypeStruct((1, D_MODEL), F32)],
        compiler_params=_params(("arbitrary",)),
    )(merged, w_out, x2, metapad, gf, target)


def _per_head_rowsum(t):
    head = lax.broadcasted_iota(jnp.int32, t.shape, 1) // HEAD_DIM
    out = jnp.zeros_like(t)
    for h in range(N_HEADS):
        sel = head == h
        out = jnp.where(sel, jnp.sum(jnp.where(sel, t, 0.0), axis=1, keepdims=True), out)
    return out


def _merge_bwd(dh1, w_out, y_pool, y_attn, wup_p, wup_a, e, o):
    lp = o.shape[0]
    nb = lp // ROW_TILE

    def body(dh_ref, wo_ref, yp_ref, ya_ref, wp_ref, wa_ref, e_ref, o_ref,
             dap_ref, daa_ref, dg_ref, do_ref, delta_ref, dyp_ref):
        dmerged = _dot_nt(dh_ref[...].astype(BF16), wo_ref[...])
        a_pool = jnp.dot(yp_ref[...], wp_ref[...], preferred_element_type=F32)
        a_attn = jnp.dot(ya_ref[...], wa_ref[...], preferred_element_type=F32)
        sp = _sigmoid(e_ref[:, E_GP:E_GA].astype(F32))
        sa = _sigmoid(e_ref[:, E_GA:E_COLS].astype(F32))
        dap = (dmerged * sp).astype(BF16)
        daa = (dmerged * sa).astype(BF16)
        dap_ref[...] = dap
        daa_ref[...] = daa
        dg_ref[:, ATTN_WIDTH:ATTN_WIDTH + D_MODEL] = (dmerged * a_pool * (sp * (1.0 - sp))).astype(BF16)
        dg_ref[:, ATTN_WIDTH + D_MODEL:] = (dmerged * a_attn * (sa * (1.0 - sa))).astype(BF16)
        dyp_ref[...] = _dot_nt(dap, wp_ref[...])
        dya = _dot_nt(daa, wa_ref[...])
        za = e_ref[:, E_ZA:E_GP].astype(F32)
        sz = _sigmoid(za)
        o = o_ref[...]
        do = dya * (za * sz)
        do_ref[...] = do.astype(BF16)
        dg_ref[:, :ATTN_WIDTH] = (dya * o * (sz * (1.0 + za * (1.0 - sz)))).astype(BF16)
        delta_ref[...] = _per_head_rowsum(do * o)

    row = lambda w: pl.BlockSpec((ROW_TILE, w), lambda i: (i, 0))
    full = lambda a: pl.BlockSpec(a.shape, lambda i: (0, 0))
    return pl.pallas_call(
        body, name="merge_bwd", grid=(nb,),
        in_specs=[row(D_MODEL), full(w_out), row(POOL_WIDTH), row(ATTN_WIDTH), full(wup_p), full(wup_a),
                  row(E_COLS), row(ATTN_WIDTH)],
        out_specs=[row(D_MODEL), row(D_MODEL), pl.BlockSpec((ROW_TILE, MAIN_COLS // 2), lambda i: (i, 1)),
                   row(ATTN_WIDTH), row(ATTN_WIDTH), row(POOL_WIDTH)],
        out_shape=[jax.ShapeDtypeStruct((lp, D_MODEL), BF16), jax.ShapeDtypeStruct((lp, D_MODEL), BF16),
                   jax.ShapeDtypeStruct((lp, MAIN_COLS), BF16),
                   jax.ShapeDtypeStruct((lp, ATTN_WIDTH), BF16), jax.ShapeDtypeStruct((lp, ATTN_WIDTH), F32),
                   jax.ShapeDtypeStruct((lp, POOL_WIDTH), F32)],
        compiler_params=_params(("parallel",)),
    )(dh1, w_out, y_pool, y_attn, wup_p, wup_a, e, o)


def _fill_dproj(dproj_half, du, dzp, dq, dk, dv):
    lp = dproj_half.shape[0]
    nb = lp // ROW_TILE

    def body(base_ref, du_ref, dzp_ref, dq_ref, dk_ref, dv_ref, o_ref):
        parts = [du_ref[...], dzp_ref[...], (dq_ref[...] * HEAD_DIM ** -0.5).astype(BF16), dk_ref[...], dv_ref[...]]
        for t, p in enumerate(parts):
            o_ref[:, t * ATTN_WIDTH:(t + 1) * ATTN_WIDTH] = p

    blk = pl.BlockSpec((ROW_TILE, ATTN_WIDTH), lambda i: (i, 0))
    return pl.pallas_call(
        body, name="fill_dproj", grid=(nb,),
        in_specs=[pl.BlockSpec(memory_space=pl.ANY)] + [blk] * 5,
        out_specs=pl.BlockSpec((ROW_TILE, MAIN_COLS // 2), lambda i: (i, 0)),
        out_shape=jax.ShapeDtypeStruct(dproj_half.shape, BF16),
        input_output_aliases={0: 0},
        compiler_params=_params(("parallel",)),
    )(dproj_half, du, dzp, dq, dk, dv)


def _pool_bwd_local(e, dy_pool, pw, scale):
    lp = e.shape[0]
    nb = lp // ROW_TILE
    ng = len(POOL_WINDOWS)

    def body(uc_ref, up_ref, z_ref, dy_ref, pw_ref, sc_ref, dpc_ref, dz_ref, dsc_ref, dpw_ref):
        i = pl.program_id(0)

        @pl.when(i == 0)
        def _():
            dsc_ref[...] = jnp.zeros_like(dsc_ref)
            dpw_ref[...] = jnp.zeros_like(dpw_ref)

        u_cur = uc_ref[...].astype(F32)
        u_prev = jnp.where(i == 0, 0.0, up_ref[...].astype(F32))
        ps, invs = _pool_p(u_cur, u_prev, i)
        z = z_ref[...].astype(F32)
        sz = _sigmoid(z)
        dy = dy_ref[...]
        dypre = dy * (z * sz)
        dsilu = sz * (1.0 + z * (1.0 - sz))
        for g in range(ng):
            sl = slice(g * POOL_GROUP, (g + 1) * POOL_GROUP)
            pb = ps[g].astype(BF16)
            w = pw_ref[g]
            yraw = jnp.dot(pb, w, preferred_element_type=F32)
            sc = sc_ref[:, sl]
            dz_ref[:, sl] = (dy[:, sl] * (yraw * sc) * dsilu[:, sl]).astype(BF16)
            dsc_ref[:, sl] += jnp.sum(dypre[:, sl] * yraw, axis=0, keepdims=True)
            dyraw = (dypre[:, sl] * sc).astype(BF16)
            dpw_ref[g] += _dot_tn(pb, dyraw)
            dpc_ref[:, sl] = _dot_nt(dyraw, w) * invs[g]

    blk = (ROW_TILE, POOL_WIDTH)
    return pl.pallas_call(
        body, name="pool_bwd_local", grid=(nb,),
        in_specs=[pl.BlockSpec(blk, lambda i: (i, 0)), pl.BlockSpec(blk, lambda i: (jnp.maximum(i - 1, 0), 0)),
                  pl.BlockSpec(blk, lambda i: (i, 1)), pl.BlockSpec(blk, lambda i: (i, 0)),
                  pl.BlockSpec((ng, POOL_GROUP, POOL_GROUP), lambda i: (0, 0, 0)),
                  pl.BlockSpec((1, POOL_WIDTH), lambda i: (0, 0))],
        out_specs=[pl.BlockSpec(blk, lambda i: (i, 0)), pl.BlockSpec(blk, lambda i: (i, 0)),
                   pl.BlockSpec((1, POOL_WIDTH), lambda i: (0, 0)),
                   pl.BlockSpec((ng, POOL_GROUP, POOL_GROUP), lambda i: (0, 0, 0))],
        out_shape=[jax.ShapeDtypeStruct((lp, POOL_WIDTH), F32), jax.ShapeDtypeStruct((lp, POOL_WIDTH), BF16),
                   jax.ShapeDtypeStruct((1, POOL_WIDTH), F32),
                   jax.ShapeDtypeStruct((ng, POOL_GROUP, POOL_GROUP), F32)],
        compiler_params=_params(("arbitrary",)),
    )(e, e, e, dy_pool, pw, scale)


def _pool_bwd_window(dpc, swap):
    lp = dpc.shape[0]
    nb = lp // ROW_TILE
    srcs, axes = swap
    ns = len(srcs)

    def body(cur_ref, nxt_ref, *rest):
        s_refs, du_ref, d_refs, sems = rest[:ns], rest[ns], rest[ns + 1:2 * ns + 1], rest[2 * ns + 1:]
        i = pl.program_id(0)

        @pl.when(i == 0)
        def _():
            for cp in _swap_copies(s_refs, d_refs, axes, *sems):
                cp.start()

        cur = cur_ref[...]
        nxt = jnp.where(i == nb - 1, 0.0, nxt_ref[...])
        pos = _pool_counts(i)
        for g, w in enumerate(POOL_WINDOWS):
            sl = slice(g * POOL_GROUP, (g + 1) * POOL_GROUP)
            xc = jnp.concatenate([cur[:, sl], nxt[:, sl]], axis=0)
            win = _leading_sums(xc, g + 1)[:ROW_TILE, :]
            dp = cur[:, sl] * jnp.minimum(pos + 1, w).astype(F32)
            du_ref[:, sl] = (win - dp).astype(BF16)

        @pl.when(i == nb - 1)
        def _():
            for cp in _swap_copies(s_refs, d_refs, axes, *sems):
                cp.wait()

    blk = (ROW_TILE, POOL_WIDTH)
    res = pl.pallas_call(
        body, name="pool_bwd_window", grid=(nb,),
        in_specs=[pl.BlockSpec(blk, lambda i: (i, 0)), pl.BlockSpec(blk, lambda i: (jnp.minimum(i + 1, nb - 1), 0))]
        + [HBM] * ns,
        out_specs=[pl.BlockSpec(blk, lambda i: (i, 0))] + [HBM] * ns,
        out_shape=[jax.ShapeDtypeStruct((lp, POOL_WIDTH), BF16)] + _swap_shapes(srcs, axes),
        scratch_shapes=[pltpu.SemaphoreType.DMA((ns,)), pltpu.SemaphoreType.DMA((ns,))],
        compiler_params=_params(("arbitrary",)),
    )(dpc, dpc, *_in_hbm(*srcs))
    return res[0], res[1:]


def _attn_bwd(qkv, do, lse, delta, bias, nb, hs):
    lk = qkv.shape[0]
    lp = nb * ROW_TILE
    nkb = lk // KV_TILE_BWD
    n_pairs = N_HEADS // 2
    per_kv = KV_TILE_BWD // ROW_TILE
    sc_in, sc_shapes, sc_sems = _scatter_operands(hs, None)
    nh = len(hs)

    def body(q_ref, k_ref, v_ref, do_ref, lse_ref, dl_ref, b_ref, *rest):
        h_refs = rest[:nh]
        dq_ref, dk_ref, dv_ref, dc_ref, dcq_ref = rest[nh:nh + 5]
        q_refs = rest[nh + 5:2 * nh + 5]
        dk_acc, dv_acc, dc_acc = rest[2 * nh + 5:2 * nh + 8]
        sems = rest[2 * nh + 8:]
        jj = pl.program_id(1)

        @pl.when((pl.program_id(0) == 0) & (jj == 0))
        def _():
            own, sends, _ = _chip_scatter_plan(h_refs, None, q_refs, None, *sems)
            for cp in own + sends:
                cp.start()

        @pl.when(jj == 0)
        def _():
            dq_ref[...] = jnp.zeros_like(dq_ref)
            dcq_ref[...] = jnp.zeros_like(dcq_ref)

        dk_acc[...] = jnp.zeros_like(dk_acc)
        dv_acc[...] = jnp.zeros_like(dv_acc)
        dc_acc[...] = jnp.zeros_like(dc_acc)

        def block(i, n_keys, masked):
            kb, vb = k_ref[:n_keys, :], v_ref[:n_keys, :]
            rows = pl.ds(pl.multiple_of(i * ROW_TILE, ROW_TILE), ROW_TILE)
            qs = _stack_heads(q_ref[rows, :])
            dos = _stack_heads(do_ref[rows, :])
            lse_i, dl_i = lse_ref[rows, :], dl_ref[rows, :]
            s = _dot_nt(qs, kb)
            dp = _dot_nt(dos, vb)
            if masked:
                valid = _causal(i, jj, 1, KV_TILE_BWD)[:, :n_keys]
            ps, dss, dcs, rowsums = [], [], [], []
            for hd in range(2):
                half = slice(hd * ROW_TILE, (hd + 1) * ROW_TILE)
                col = slice(hd * HEAD_DIM, hd * HEAD_DIM + 1)
                sh = s[half] - b_ref[i, 0, 0, hd:hd + 1, :n_keys]
                if masked:
                    sh = jnp.where(valid, sh, NEG)
                p = jnp.exp(sh - lse_i[:, col])
                ds = p * (dp[half] - dl_i[:, col])
                ps.append(p.astype(BF16))
                dss.append(ds.astype(BF16))
                dcs.append(jnp.sum(ds, axis=0, keepdims=True))
                rowsums.append(jnp.sum(ds, axis=1, keepdims=True))
            dsb = jnp.concatenate(dss, axis=0)
            dv_acc[:n_keys, :] += _dot_tn(jnp.concatenate(ps, axis=0), dos)
            dk_acc[:n_keys, :] += _dot_tn(dsb, qs)
            dc_acc[:, :n_keys] -= jnp.concatenate(dcs, axis=0)
            dq_ref[rows, :] += _unstack_heads(jnp.dot(dsb, kb, preferred_element_type=F32))
            dcq_ref[rows, :] += _unstack_heads(jnp.broadcast_to(jnp.concatenate(rowsums, axis=0), (2 * ROW_TILE, LANES)))

        first_q = per_kv * jj
        for r in range(per_kv):
            @pl.when(first_q + r < nb)
            def _():
                block(first_q + r, (r + 1) * ROW_TILE, True)

        def rest(i, carry):
            block(i, KV_TILE_BWD, False)
            return carry

        lax.fori_loop(jnp.minimum(first_q + per_kv, nb), nb, rest, 0)
        dk_ref[...] = dk_acc[...].astype(BF16)
        dv_ref[...] = dv_acc[...].astype(BF16)
        dc_ref[0, 0] = dc_acc[...]

        @pl.when((pl.program_id(0) == n_pairs - 1) & (jj == nkb - 1))
        def _():
            own, sends, recvs = _chip_scatter_plan(h_refs, None, q_refs, None, *sems)
            for cp in recvs:
                cp.wait_recv()
            for cp in sends:
                cp.wait_send()
            for cp in own:
                cp.wait()

    whole = lambda rows: pl.BlockSpec((rows, LANES), lambda hp, jj: (0, hp))
    kv_blk = lambda off: pl.BlockSpec((KV_TILE_BWD, LANES), lambda hp, jj: (jj, off + hp))
    res = pl.pallas_call(
        body, name="attn_bwd", grid=(n_pairs, nkb),
        in_specs=[whole(lk), kv_blk(n_pairs), kv_blk(2 * n_pairs), whole(lp), whole(lp), whole(lp),
                  pl.BlockSpec((nb, 1, 1, 2, KV_TILE_BWD), lambda hp, jj: (0, hp, jj, 0, 0))] + [HBM] * nh,
        out_specs=[whole(lp), kv_blk(0), kv_blk(0),
                   pl.BlockSpec((1, 1, 2, KV_TILE_BWD), lambda hp, jj: (hp, jj, 0, 0)), whole(lp)] + [HBM] * nh,
        out_shape=[jax.ShapeDtypeStruct((lp, ATTN_WIDTH), F32), jax.ShapeDtypeStruct((lk, ATTN_WIDTH), BF16),
                   jax.ShapeDtypeStruct((lk, ATTN_WIDTH), BF16),
                   jax.ShapeDtypeStruct((n_pairs, nkb, 2, KV_TILE_BWD), F32),
                   jax.ShapeDtypeStruct((lp, ATTN_WIDTH), F32)] + sc_shapes,
        scratch_shapes=[pltpu.VMEM((KV_TILE_BWD, LANES), F32), pltpu.VMEM((KV_TILE_BWD, LANES), F32),
                        pltpu.VMEM((2, KV_TILE_BWD), F32)] + sc_sems,
        compiler_params=_params(("arbitrary", "arbitrary")),
    )(qkv, qkv, qkv, do, lse, delta, bias, *_in_hbm(*sc_in))
    return res[:5], res[5:]


def _gates_bwd(dc, dcq, f, bfg):
    nb = f.shape[0] // ROW_TILE

    def body(dc_ref, dcq_ref, f_ref, b_ref, df_ref, db_ref, carry):
        step = pl.program_id(0)
        i = nb - 1 - step

        @pl.when(step == 0)
        def _():
            carry[...] = jnp.zeros_like(carry)
            db_ref[...] = jnp.zeros_like(db_ref)

        dcb = dc_ref[...]
        lane = lax.broadcasted_iota(jnp.int32, (ROW_TILE, F_COLS), 1)
        for h in range(N_HEADS):
            dcb = dcb + jnp.where(lane == h, dcq_ref[:, HEAD_DIM * h:HEAD_DIM * h + 1], 0.0)
        r_i = lax.broadcasted_iota(jnp.int32, (ROW_TILE, ROW_TILE), 0)
        c_i = lax.broadcasted_iota(jnp.int32, (ROW_TILE, ROW_TILE), 1)
        upper = (c_i >= r_i).astype(F32)
        dlf = jnp.dot(upper, dcb, precision=HIGHEST, preferred_element_type=F32) + carry[...]
        carry[...] = carry[...] + jnp.sum(dcb, axis=0, keepdims=True)
        logit = f_ref[...] + b_ref[...]
        dlogit = jnp.where(_valid_gate_mask(i), dlf * _sigmoid(-logit), 0.0)
        df_ref[...] = dlogit.astype(BF16)
        db_ref[...] += jnp.sum(dlogit, axis=0, keepdims=True)

    blk = pl.BlockSpec((ROW_TILE, F_COLS), lambda s: (nb - 1 - s, 0))
    wide = pl.BlockSpec((ROW_TILE, ATTN_WIDTH), lambda s: (nb - 1 - s, 0))
    one = pl.BlockSpec((1, F_COLS), lambda s: (0, 0))
    return pl.pallas_call(
        body, name="gates_bwd", grid=(nb,),
        in_specs=[blk, wide, blk, one], out_specs=[blk, one],
        out_shape=[jax.ShapeDtypeStruct(f.shape, BF16), jax.ShapeDtypeStruct((1, F_COLS), F32)],
        scratch_shapes=[pltpu.VMEM((1, F_COLS), F32)],
        compiler_params=_params(("arbitrary",)),
    )(dc, dcq, f, bfg)


def _input_bwd(dproj, df, wt_e, wt_qkv, wt_f, x2, metapad, dh1, g1, hs, sc=None):
    nb = x2.shape[0] // ROW_TILE + 1
    n = len(hs)
    sc_in, sc_shapes, sc_sems = _scatter_operands(hs, sc)
    nq = len(sc_in)

    def body(dp_ref, df_ref, we_ref, w_ref, wf_ref, x_ref, mp_ref, dh_ref, g_ref, *rest):
        h_refs, s_ref = rest[:n], (rest[n] if sc is not None else None)
        gx_ref, g0_ref, dg_ref = rest[nq:nq + 3]
        q_refs, sq_ref = rest[nq + 3:nq + 3 + n], (rest[nq + 3 + n] if sc is not None else None)
        sems = rest[2 * nq + 3:]
        i = pl.program_id(0)

        @pl.when(i == 0)
        def _():
            dg_ref[...] = jnp.zeros_like(dg_ref)
            own, sends, _ = _chip_scatter_plan(h_refs, s_ref, q_refs, sq_ref, *sems)
            for cp in own + sends:
                cp.start()

        dhn = (jnp.dot(dp_ref[:, :E_ZA], we_ref[:E_ZA, :], preferred_element_type=F32)
               + jnp.dot(dp_ref[:, E_ZA:MAIN_COLS // 2], w_ref[...], preferred_element_type=F32)
               + jnp.dot(dp_ref[:, MAIN_COLS // 2:], we_ref[E_ZA:, :], preferred_element_type=F32)
               + jnp.dot(df_ref[...], wf_ref[...], preferred_element_type=F32))
        h0 = jnp.where(i == 0, mp_ref[...], x_ref[...])
        r = lax.rsqrt(jnp.mean(h0 * h0, axis=-1, keepdims=True) + RMS_EPS)
        xhat = h0 * r
        dg_ref[...] += jnp.sum(dhn * xhat, axis=0, keepdims=True)
        dxhat = dhn * g_ref[...]
        dh0 = dh_ref[...] + r * (dxhat - xhat * jnp.mean(dxhat * xhat, axis=-1, keepdims=True))
        gx_ref[...] = dh0

        @pl.when(i == 0)
        def _():
            g0_ref[...] = dh0

        @pl.when(i == nb - 1)
        def _():
            own, sends, recvs = _chip_scatter_plan(h_refs, s_ref, q_refs, sq_ref, *sems)
            for cp in recvs:
                cp.wait_recv()
            for cp in sends:
                cp.wait_send()
            for cp in own:
                cp.wait()

    const = lambda shape: pl.BlockSpec(shape, lambda i: (0, 0))
    res = pl.pallas_call(
        body, name="input_bwd", grid=(nb,),
        in_specs=[pl.BlockSpec((ROW_TILE, MAIN_COLS), lambda i: (i, 0)), pl.BlockSpec((ROW_TILE, F_COLS), lambda i: (i, 0)),
                  const((E_COLS, D_MODEL)), const((QKV_COLS, D_MODEL)), const((F_COLS, D_MODEL)),
                  _tokens_spec(), const((ROW_TILE, D_MODEL)),
                  pl.BlockSpec((ROW_TILE, D_MODEL), lambda i: (i, 0)), const((1, D_MODEL))] + [HBM] * nq,
        out_specs=[_tokens_spec(), const((ROW_TILE, D_MODEL)), const((1, D_MODEL))] + [HBM] * nq,
        out_shape=[jax.ShapeDtypeStruct(x2.shape, F32), jax.ShapeDtypeStruct((ROW_TILE, D_MODEL), F32),
                   jax.ShapeDtypeStruct((1, D_MODEL), F32)] + sc_shapes,
        scratch_shapes=sc_sems,
        compiler_params=_params(("arbitrary",), vmem=56 * 1024 * 1024),
    )(dproj, df, wt_e, wt_qkv, wt_f, x2, metapad, dh1, g1, *_in_hbm(*sc_in))
    return res[:3], res[3:3 + n], (res[3 + n] if sc is not None else None)


def _adamw(w, g, m, v, name):
    rows, cols = w.shape
    if rows % 8 == 0:
        tr, tc = _row_tile8(rows), cols
    else:
        tr, tc = rows, (2 * LANES if cols % (2 * LANES) == 0 and rows > 8 else cols)

    def body(w_ref, g_ref, m_ref, v_ref, d_ref, mo_ref, vo_ref):
        g_ = g_ref[...]
        m_new = ADAM_B1 * m_ref[...] + (1.0 - ADAM_B1) * g_
        v_new = ADAM_B2 * v_ref[...] + (1.0 - ADAM_B2) * (g_ * g_)
        m_hat = m_new / (1.0 - ADAM_B1 ** ADAM_STEP)
        v_hat = v_new / (1.0 - ADAM_B2 ** ADAM_STEP)
        d_ref[...] = -ADAM_LR * (m_hat / (jnp.sqrt(v_hat) + ADAM_EPS) + ADAM_WD * w_ref[...])
        mo_ref[...] = m_new
        vo_ref[...] = v_new

    blk = pl.BlockSpec((tr, tc), lambda i, j: (i, j))
    return pl.pallas_call(
        body, name=name, grid=(rows // tr, cols // tc),
        in_specs=[blk] * 4, out_specs=[blk] * 3,
        out_shape=[jax.ShapeDtypeStruct(w.shape, F32)] * 3,
        compiler_params=_params(("parallel", "parallel")),
    )(w, g, m, v)


def _adamw_native(w3, g3, m3, v3, name):
    rows = w3.shape[0]
    tr = rows // 2
    blk = pl.BlockSpec((tr,) + w3.shape[1:], lambda i: (i, 0, 0))
    shape = jax.ShapeDtypeStruct(w3.shape, F32)

    def moments(g_ref, m_ref, v_ref, mo_ref, vo_ref):
        g_ = g_ref[...]
        mo_ref[...] = ADAM_B1 * m_ref[...] + (1.0 - ADAM_B1) * g_
        vo_ref[...] = ADAM_B2 * v_ref[...] + (1.0 - ADAM_B2) * (g_ * g_)

    new_m, new_v = pl.pallas_call(
        moments, name=name + "_moments", grid=(2,), in_specs=[blk] * 3, out_specs=[blk] * 2, out_shape=[shape] * 2,
        compiler_params=_params(("parallel",)),
    )(g3, m3, v3)

    def delta(w_ref, m_ref, v_ref, d_ref):
        m_hat = m_ref[...] / (1.0 - ADAM_B1 ** ADAM_STEP)
        v_hat = v_ref[...] / (1.0 - ADAM_B2 ** ADAM_STEP)
        d_ref[...] = -ADAM_LR * (m_hat / (jnp.sqrt(v_hat) + ADAM_EPS) + ADAM_WD * w_ref[...])

    d = pl.pallas_call(
        delta, name=name + "_delta", grid=(2,), in_specs=[blk] * 3, out_specs=blk, out_shape=shape,
        compiler_params=_params(("parallel",)),
    )(w3, new_m, new_v)
    return d, new_m, new_v


def _row_tile8(rows):
    best = rows
    for t in range(8, 257, 8):
        if rows % t == 0:
            best = t
    return best


def kernel(x, meta_tokens, norm_g, w_in, b_forget, pool_w, pool_scale, w_up_pool, w_up_attn, w_out, final_norm_g, loss_target, m_meta_tokens, m_norm_g, m_w_in, m_b_forget, m_pool_w, m_pool_scale, m_w_up_pool, m_w_up_attn, m_w_out, m_final_norm_g, v_meta_tokens, v_norm_g, v_w_in, v_b_forget, v_pool_w, v_pool_scale, v_w_up_pool, v_w_up_attn, v_w_out, v_final_norm_g):
    seq = x.shape[1]
    assert seq % ROW_TILE == 0 and x.shape[0] == 1
    lp = seq + ROW_TILE
    nb = lp // ROW_TILE
    lk = -(-lp // KV_TILE_BWD) * KV_TILE_BWD
    core = jnp.reshape(lax.axis_index("c"), (1,)).astype(jnp.int32)
    x2 = x[0]
    target = loss_target[0]
    sh_d = D_MODEL // N_CHIPS

    to_rows = lambda a: jnp.transpose(a, (2, 0, 1))
    from_rows = lambda a: jnp.transpose(a, (1, 2, 0))
    gf = final_norm_g.reshape(1, D_MODEL)
    bfg = jnp.pad(b_forget, ((0, 0), (0, F_COLS - N_HEADS)))
    pw_b = pool_w[0].astype(BF16)
    hn, (wg_in, meta_g) = _rmsnorm_fwd(x2, norm_g, lk, [jnp.transpose(w_in[0]).astype(BF16), meta_tokens], [1, 0])
    wt = wg_in.reshape(-1, D_MODEL)
    wt_e = jnp.concatenate([wt[REF_U:REF_Q], wt[REF_ZA:REF_F], wt[REF_GP:]], axis=0)
    wt_qkv = wt[REF_Q:REF_ZA]
    wt_f = jnp.pad(wt[REF_F:REF_GP], ((0, F_COLS - N_HEADS), (0, 0)))
    meta_full = jnp.transpose(meta_g, (1, 0, 2)).reshape(N_META, D_MODEL)
    metapad = jnp.pad(meta_full, ((PAD_ROWS, 0), (0, 0)))

    tm = _row_tile(lp, 2200)
    e, (wg_up_p, wg_up_a, wg_out) = _mm_nt(
        hn, wt_e, BF16, "in_proj_gates", lp, E_COLS, tm, 512,
        gather=([w_up_pool[0].astype(BF16), w_up_attn[0].astype(BF16), w_out[0].astype(BF16)], [0, 0, 0]))
    wup_p = jnp.transpose(wg_up_p, (1, 0, 2)).reshape(POOL_WIDTH, D_MODEL)
    wup_a = jnp.transpose(wg_up_a, (1, 0, 2)).reshape(ATTN_WIDTH, D_MODEL)
    wout = wg_out.reshape(D_MODEL, D_MODEL)
    qkv = _mm_nt(hn, wt_qkv, BF16, "in_proj_qkv", lk, QKV_COLS, _row_tile(lk, 2600), 512, scale_first=HEAD_DIM ** -0.5)
    f = _mm_nt(hn, wt_f, F32, "in_proj_forget", lp, F_COLS, tm, F_COLS)
    c = _gates_fwd(f, bfg)
    c_t = jnp.transpose(c[:, :N_HEADS])
    bias = c_t[None, :, :] - jnp.transpose(c_t[:, ::ROW_TILE])[:, :, None]
    bias = jnp.where(jnp.arange(lp) < PAD_ROWS, -NEG, bias)
    bias = jnp.pad(bias, ((0, 0), (0, 0), (0, lk - lp)))
    by_kv = lambda t: jnp.transpose(bias.reshape(nb, N_HEADS // 2, 2, lk // t, t), (0, 1, 3, 2, 4))
    bias, bias_bwd = by_kv(KV_TILE), by_kv(KV_TILE_BWD)
    y_pool = _pool_fwd(e, pw_b, pool_scale)
    o, lse = _attn_fwd(qkv, bias, nb)
    merged, y_attn = _merge_fwd(y_pool, o, e, wup_p, wup_a)
    dh1, loss_part, dgf = _head_fwd_bwd(merged, wout, x2, metapad, gf, target)

    dap, daa, dproj_half, do, delta, dy_pool = _merge_bwd(dh1, wout, y_pool, y_attn, wup_p, wup_a, e, o)
    tk = _row_tile(lp, 1100)
    dw_out = _mm_tn(merged, dh1, "grad_w_out", lp, 512, D_MODEL, tk)
    dw_up_p = _mm_tn(y_pool, dap, "grad_w_up_pool", lp, 512, D_MODEL, lp, chunks=N_CHIPS)
    dw_up_a = _mm_tn(y_attn, daa, "grad_w_up_attn", lp, 512, D_MODEL, lp, chunks=N_CHIPS)
    gs_rows = [dw_up_p, dw_up_a, dw_out.reshape(N_CHIPS, sh_d, D_MODEL)]
    dpc, dzp, dscale, dpw = _pool_bwd_local(e, dy_pool, pw_b, pool_scale)
    du, rb_rows = _pool_bwd_window(dpc, (gs_rows, [1, 1, 1]))
    hs_rows = _add_sibling_half(gs_rows, rb_rows, [1, 1, 1], core)
    (dq, dk, dv, dc4, dcq), qs_rows = _attn_bwd(qkv, do, lse, delta, bias_bwd, nb, hs_rows)
    dc = jnp.transpose(dc4, (1, 3, 0, 2)).reshape(lk, N_HEADS)[:lp]
    df, db = _gates_bwd(jnp.pad(dc, ((0, 0), (0, F_COLS - N_HEADS))), dcq, f, bfg)
    dproj = _fill_dproj(dproj_half, du, dzp, dq, dk, dv)
    dwt_f = _mm_tn(df, hn, "grad_w_in_forget", lp, F_COLS, D_MODEL, tk)

    def pad8(a):
        return jnp.pad(a, ((0, (-a.shape[0]) % 8), (0, 0)))

    small_parts = [pad8(a) for a in (dgf.reshape(-1, LANES), dscale.reshape(-1, LANES), db, dpw.reshape(-1, LANES),
                                     loss_part)]
    small = jnp.concatenate(small_parts, axis=0)
    soffs = [0]
    for p_ in small_parts:
        soffs.append(soffs[-1] + p_.shape[0])

    half = MAIN_COLS // 2
    dwt_a, (rb_f, sr) = _mm_tn(dproj, hn, "grad_w_in_a", lp, half // 4, D_MODEL, lp, m=half, m_off=0,
                               swap=([dwt_f, small], [1, None]))
    dwt_b, (rb_a,) = _mm_tn(dproj, hn, "grad_w_in_b", lp, half // 4, D_MODEL, lp, m=half, m_off=4, swap=([dwt_a], [1]))
    (rb_b,) = _sibling_exchange([dwt_b], [1])
    h_a, h_b, h_f, sc = _add_halves_2d([dwt_a, dwt_b, dwt_f], [rb_a, rb_b, rb_f], small, sr, core)
    h_in = jnp.concatenate([h_a, h_b[:ATTN_WIDTH], h_f[:N_HEADS], h_b[ATTN_WIDTH:]], axis=0)
    grad_axes = [2, 1, 1, 1]
    (grad_x, g_block0, dg1), (q_in,), sq = _input_bwd(dproj, df, wt_e, wt_qkv, wt_f, x2, metapad, dh1, norm_g,
                                                      [h_in.reshape(N_CHIPS, -1, D_MODEL // 2)], sc)
    g_w_in_rows, g_w_up_p, g_w_up_a, g_w_out, st = _reduce_allgather([q_in] + list(qs_rows), grad_axes, sq,
                                                                       [True, False, False, False])
    late = _small_allreduce(jnp.concatenate([pad8(dg1.reshape(-1, LANES)), g_block0[PAD_ROWS:].reshape(-1, LANES)], axis=0))
    n_norm = D_MODEL // LANES
    g_norm = late[:n_norm].reshape(1, D_MODEL)
    chip = 2 * lax.axis_index("x") + lax.axis_index("y")
    g_meta = lax.dynamic_slice_in_dim(late[n_norm:].reshape(N_META, D_MODEL), chip * sh_d, sh_d, axis=1)

    spiece = lambda k, rows: st[soffs[k]:soffs[k] + rows]
    g_final = spiece(0, D_MODEL // LANES).reshape(1, D_MODEL)
    g_scale = spiece(1, POOL_WIDTH // LANES).reshape(1, POOL_WIDTH)
    g_bf = spiece(2, 1)
    g_pw = spiece(3, POOL_WIDTH)
    loss = st[soffs[4], 0]

    def pad_lanes(a):
        return jnp.pad(a, ((0, 0), (0, F_COLS - N_HEADS)))

    w_in_res = (g_w_in_rows,) + _adamw_native(to_rows(w_in), g_w_in_rows, to_rows(m_w_in), to_rows(v_w_in), "adamw_w_in")

    upd = [
        ("meta_tokens", meta_tokens, g_meta, m_meta_tokens, v_meta_tokens),
        ("norm_g", norm_g, g_norm, m_norm_g, v_norm_g),
        ("w_in", None, None, None, None),
        ("b_forget", pad_lanes(b_forget), g_bf, pad_lanes(m_b_forget), pad_lanes(v_b_forget)),
        ("pool_w", pool_w.reshape(-1, LANES), g_pw, m_pool_w.reshape(-1, LANES), v_pool_w.reshape(-1, LANES)),
        ("pool_scale", pool_scale, g_scale, m_pool_scale, v_pool_scale),
        ("w_up_pool", w_up_pool[0], g_w_up_p, m_w_up_pool[0], v_w_up_pool[0]),
        ("w_up_attn", w_up_attn[0], g_w_up_a, m_w_up_attn[0], v_w_up_attn[0]),
        ("w_out", w_out[0], g_w_out, m_w_out[0], v_w_out[0]),
        ("final_norm_g", gf, g_final, m_final_norm_g.reshape(1, D_MODEL), v_final_norm_g.reshape(1, D_MODEL)),
    ]
    shapes = [meta_tokens.shape, norm_g.shape, w_in.shape, b_forget.shape, pool_w.shape, pool_scale.shape,
              w_up_pool.shape, w_up_attn.shape, w_out.shape, final_norm_g.shape]
    grads, deltas, new_ms, new_vs = [], [], [], []
    for (name, w_, g_, m_in, v_in), shp in zip(upd, shapes):
        if name == "w_in":
            res = tuple(from_rows(a) for a in w_in_res)
        else:
            d_, mn_, vn_ = _adamw(w_, g_, m_in, v_in, "adamw_" + name)
            res = (g_, d_, mn_, vn_)
        if name == "b_forget":
            res = tuple(a[:, :N_HEADS] for a in res)
        for lst, a in zip((grads, deltas, new_ms, new_vs), res):
            lst.append(a.reshape(shp))

    return (loss, grad_x.reshape(x.shape), *grads, *deltas, *new_ms, *new_vs)
```

```python
import jax
import jax.numpy as jnp
from jax import lax
from jax.experimental import pallas as pl
from jax.experimental.pallas import tpu as pltpu

F32 = jnp.float32
BF16 = jnp.bfloat16
MESH = pl.DeviceIdType.MESH
HIGHEST = lax.Precision.HIGHEST
HBM = pl.BlockSpec(memory_space=pltpu.HBM)

D_MODEL = 1024
N_META = 16
POOL_WIDTH = 512
POOL_GROUP = 128
POOL_WINDOWS = (2, 4, 8, 16)
N_HEADS = 8
HEAD_DIM = 64
ATTN_WIDTH = 512
RMS_EPS = 1e-6
N_CHIPS = 4

ADAM_LR = 0.001
ADAM_B1 = 0.9
ADAM_B2 = 0.999
ADAM_EPS = 1e-08
ADAM_WD = 0.01
ADAM_STEP = 10

LANES = 128
ROW_TILE = 256
KV_TILE = 1024
KV_TILE_BWD = 1024
PAD_ROWS = ROW_TILE - N_META
NEG = -1e30

E_U, E_ZP, E_ZA, E_GP, E_GA, E_COLS = 0, 512, 1024, 1536, 2560, 3584
QKV_COLS = 3 * ATTN_WIDTH
MAIN_COLS = E_COLS + QKV_COLS
F_COLS = LANES
REF_U, REF_Q, REF_ZA, REF_F, REF_GP = 0, 1024, 2560, 3072, 3080
VMEM_LIMIT = 48 * 1024 * 1024


def _params(sem=None, vmem=VMEM_LIMIT):
    return pltpu.CompilerParams(dimension_semantics=sem, vmem_limit_bytes=vmem)


def _in_hbm(*arrays):
    return [pltpu.with_memory_space_constraint(a, pltpu.HBM) for a in arrays]


def _row_tile(n, target):
    best = 16
    for t in range(16, target + 1, 16):
        if n % t == 0:
            best = t
    return best


def _sigmoid(x):
    return 1.0 / (1.0 + jnp.exp(-x))


def _half(ref, axis, which):
    n = ref.shape[axis] // 2
    idx = [slice(None)] * len(ref.shape)
    idx[axis] = pl.ds(pl.multiple_of(which * n, n), n)
    return ref.at[tuple(idx)]


def _half_shape(shape, axis):
    s = list(shape)
    s[axis] //= 2
    return tuple(s)


def _gather_start(ins, outs, axes, send, recv, fsend, frecv, local):
    x, y, c = lax.axis_index("x"), lax.axis_index("y"), lax.axis_index("c")
    me = 2 * x + y
    for t in range(len(ins)):
        pltpu.make_async_copy(ins[t], outs[t].at[me], local.at[t]).start()
        for k, chip in enumerate([(1 - x, y), (x, 1 - y), (1 - x, 1 - y)]):
            pltpu.make_async_remote_copy(
                src_ref=_half(ins[t], axes[t], c), dst_ref=_half(outs[t].at[me], axes[t], c),
                send_sem=send.at[3 * t + k], recv_sem=recv.at[3 * t + k], device_id=(*chip, c), device_id_type=MESH).start()


def _gather_finish(ins, outs, axes, send, recv, fsend, frecv, local):
    x, y, c = lax.axis_index("x"), lax.axis_index("y"), lax.axis_index("c")
    me = 2 * x + y
    sibling = (x, y, 1 - c)
    chips = [(1 - x, y), (x, 1 - y), (1 - x, 1 - y)]
    n = len(ins)

    def over_ici(t, k, chip, dst_slot):
        return pltpu.make_async_remote_copy(
            src_ref=_half(ins[t], axes[t], c), dst_ref=_half(outs[t].at[dst_slot], axes[t], c),
            send_sem=send.at[3 * t + k], recv_sem=recv.at[3 * t + k], device_id=(*chip, c), device_id_type=MESH)

    def to_sibling(t, k, slot, which):
        return pltpu.make_async_remote_copy(
            src_ref=_half(outs[t].at[slot], axes[t], which), dst_ref=_half(outs[t].at[slot], axes[t], which),
            send_sem=fsend.at[3 * t + k], recv_sem=frecv.at[3 * t + k], device_id=sibling, device_id_type=MESH)

    forwards = []
    for t in range(n):
        for k, (px, py) in enumerate(chips):
            over_ici(t, k, (px, py), 2 * px + py).wait_recv()
            fw = to_sibling(t, k, 2 * px + py, c)
            fw.start()
            forwards.append(fw)
    for t in range(n):
        for k, (px, py) in enumerate(chips):
            to_sibling(t, k, 2 * px + py, 1 - c).wait_recv()
    for t in range(n):
        for k, chip in enumerate(chips):
            over_ici(t, k, chip, me).wait_send()
    for fw in forwards:
        fw.wait_send()
    for t in range(n):
        pltpu.make_async_copy(ins[t], outs[t].at[me], local.at[t]).wait()


def _gather_sems(n):
    return [pltpu.SemaphoreType.DMA((3 * n,)), pltpu.SemaphoreType.DMA((3 * n,)), pltpu.SemaphoreType.DMA((3 * n,)),
            pltpu.SemaphoreType.DMA((3 * n,)), pltpu.SemaphoreType.DMA((n,))]


def _gathered_shapes(shards):
    return [jax.ShapeDtypeStruct((N_CHIPS,) + s.shape, s.dtype) for s in shards]


def _swap_copies(srcs, dsts, axes, send, recv):
    x, y, c = lax.axis_index("x"), lax.axis_index("y"), lax.axis_index("c")
    return [pltpu.make_async_remote_copy(
        src_ref=srcs[t] if axes[t] is None else _half(srcs[t], axes[t], 1 - c), dst_ref=dsts[t],
        send_sem=send.at[t], recv_sem=recv.at[t], device_id=(x, y, 1 - c), device_id_type=MESH) for t in range(len(srcs))]


def _swap_shapes(srcs, axes):
    return [jax.ShapeDtypeStruct(g.shape if a is None else _half_shape(g.shape, a), g.dtype) for g, a in zip(srcs, axes)]


def _sibling_exchange(gs, axes):
    n = len(gs)

    def body(*refs):
        cps = _swap_copies(refs[:n], refs[n:2 * n], axes, *refs[2 * n:])
        for cp in cps:
            cp.start()
        for cp in cps:
            cp.wait()

    return pl.pallas_call(
        body, name="grad_sibling_exchange",
        in_specs=[HBM] * n, out_specs=[HBM] * n, out_shape=_swap_shapes(gs, axes),
        scratch_shapes=[pltpu.SemaphoreType.DMA((n,)), pltpu.SemaphoreType.DMA((n,))],
    )(*_in_hbm(*gs))


def _add_halves_2d(parts, rbs, core):
    n = len(parts)
    rows, w = rbs[0].shape
    tr = 512
    last_rows = rbs[-1].shape[0]

    def body(core_ref, *refs):
        p_refs, r_refs, h_refs = refs[:n], refs[n:2 * n], refs[2 * n:]
        for t in range(n - 1):
            h_refs[t][...] = (p_refs[t][...] + r_refs[t][...]).astype(BF16)

        @pl.when(pl.program_id(0) == 0)
        def _():
            h_refs[n - 1][...] = (p_refs[n - 1][...] + r_refs[n - 1][...]).astype(BF16)

    mine = lambda r: pl.BlockSpec((r, w), lambda i, cr: (i, cr[0]))
    tile = lambda r: pl.BlockSpec((r, w), lambda i, cr: (i, 0))
    mine_small = pl.BlockSpec((last_rows, w), lambda i, cr: (0, cr[0]))
    tile_small = pl.BlockSpec((last_rows, w), lambda i, cr: (0, 0))
    return pl.pallas_call(
        body, name="grad_add_sibling_w_in",
        grid_spec=pltpu.PrefetchScalarGridSpec(
            num_scalar_prefetch=1, grid=(rows // tr,),
            in_specs=[mine(tr)] * (n - 1) + [mine_small] + [tile(tr)] * (n - 1) + [tile_small],
            out_specs=[tile(tr)] * (n - 1) + [tile_small]),
        out_shape=[jax.ShapeDtypeStruct(r.shape, BF16) for r in rbs],
        compiler_params=_params(("arbitrary",)),
    )(core, *parts, *rbs)


def _add_sibling_half(gs, rbs, axes, small, sr, core):
    n = len(gs)

    def body(core_ref, *refs):
        g_refs, rb_refs = refs[:n], refs[n:2 * n]
        s_ref, sr_ref = refs[2 * n], refs[2 * n + 1]
        h_refs, sc_ref = refs[2 * n + 2:3 * n + 2], refs[3 * n + 2]
        for t in range(n):
            h_refs[t][...] = (g_refs[t][...] + rb_refs[t][...]).astype(BF16)
        sc_ref[...] = s_ref[...] + sr_ref[...]

    def mine(rb, axis):
        blk = (None,) + rb.shape[1:]
        if axis == 2:
            return pl.BlockSpec(blk, lambda j, cr: (j, 0, cr[0]))
        return pl.BlockSpec(blk, lambda j, cr: (j, cr[0], 0))

    chunk = lambda rb: pl.BlockSpec((None,) + rb.shape[1:], lambda j, cr: (j, 0, 0))
    whole = pl.BlockSpec(small.shape, lambda j, cr: (0, 0))
    return pl.pallas_call(
        body, name="grad_add_sibling",
        grid_spec=pltpu.PrefetchScalarGridSpec(
            num_scalar_prefetch=1, grid=(N_CHIPS,),
            in_specs=[mine(rb, a) for rb, a in zip(rbs, axes)] + [chunk(rb) for rb in rbs] + [whole, whole],
            out_specs=[chunk(rb) for rb in rbs] + [whole]),
        out_shape=[jax.ShapeDtypeStruct(rb.shape, BF16) for rb in rbs] + [jax.ShapeDtypeStruct(small.shape, F32)],
        compiler_params=_params(("arbitrary",)),
    )(core, *gs, *rbs, small, sr)


def _chip_scatter_plan(h_refs, s_ref, q_refs, sq_ref, send, recv, local):
    n = len(h_refs)
    nt = n + (s_ref is not None)
    x, y, c = lax.axis_index("x"), lax.axis_index("y"), lax.axis_index("c")
    me = 2 * x + y
    chips = [(1 - x, y), (x, 1 - y), (1 - x, 1 - y)]

    def copy(t, k, chip, src_slot, dst_slot):
        src = h_refs[t].at[src_slot] if t < n else s_ref
        dst = (q_refs[t] if t < n else sq_ref).at[dst_slot]
        return pltpu.make_async_remote_copy(src_ref=src, dst_ref=dst, send_sem=send.at[3 * t + k],
                                            recv_sem=recv.at[3 * t + k], device_id=(*chip, c), device_id_type=MESH)

    own = [pltpu.make_async_copy(h_refs[t].at[me], q_refs[t].at[me], local.at[t]) for t in range(n)]
    if s_ref is not None:
        own.append(pltpu.make_async_copy(s_ref, sq_ref.at[me], local.at[n]))
    sends = [copy(t, k, (px, py), 2 * px + py, me) for t in range(nt) for k, (px, py) in enumerate(chips)]
    recvs = [copy(t, k, (px, py), me, 2 * px + py) for t in range(nt) for k, (px, py) in enumerate(chips)]
    return own, sends, recvs


def _scatter_operands(hs, sc):
    small = [] if sc is None else [sc]
    nt = len(hs) + len(small)
    shapes = [jax.ShapeDtypeStruct(h.shape, h.dtype) for h in hs] + [
        jax.ShapeDtypeStruct((N_CHIPS,) + a.shape, a.dtype) for a in small]
    sems = [pltpu.SemaphoreType.DMA((3 * nt,)), pltpu.SemaphoreType.DMA((3 * nt,)), pltpu.SemaphoreType.DMA((nt,))]
    return list(hs) + small, shapes, sems


def _small_allreduce(buf):
    def body(b_ref, o_ref, sib_buf, chip_buf, send, recv):
        x, y, c = lax.axis_index("x"), lax.axis_index("y"), lax.axis_index("c")
        me = 2 * x + y
        chips = [(1 - x, y), (x, 1 - y), (1 - x, 1 - y)]
        swap = pltpu.make_async_remote_copy(src_ref=b_ref, dst_ref=sib_buf, send_sem=send.at[0], recv_sem=recv.at[0],
                                            device_id=(x, y, 1 - c), device_id_type=MESH)
        swap.start()
        swap.wait()
        chip_buf[me] = b_ref[...] + sib_buf[...]

        def copy(k, chip, slot):
            return pltpu.make_async_remote_copy(src_ref=chip_buf.at[slot], dst_ref=chip_buf.at[slot], send_sem=send.at[1 + k],
                                                recv_sem=recv.at[1 + k], device_id=(*chip, c), device_id_type=MESH)

        sends = [copy(k, chip, me) for k, chip in enumerate(chips)]
        for cp in sends:
            cp.start()
        for k, (px, py) in enumerate(chips):
            copy(k, (px, py), 2 * px + py).wait_recv()
        for cp in sends:
            cp.wait_send()
        o_ref[...] = ((chip_buf[0] + chip_buf[1]) + chip_buf[2]) + chip_buf[3]

    vmem = pl.BlockSpec(memory_space=pltpu.VMEM)
    return pl.pallas_call(
        body, name="small_allreduce", in_specs=[vmem], out_specs=vmem,
        out_shape=jax.ShapeDtypeStruct(buf.shape, F32),
        scratch_shapes=[pltpu.VMEM(buf.shape, F32), pltpu.VMEM((N_CHIPS,) + buf.shape, F32),
                        pltpu.SemaphoreType.DMA((4,)), pltpu.SemaphoreType.DMA((4,))],
        compiler_params=_params(),
    )(buf)


def _reduce_allgather(qs, axes, sq, as_rows):
    n = len(qs)
    shard_shapes, half_axes = [], []
    for q, a, rows_form in zip(qs, axes, as_rows):
        shape = [d * 2 if i == a - 1 else d for i, d in enumerate(q.shape[1:])]
        if rows_form:
            assert a == 2
            shape = [shape[0], 1, shape[1]]
        shard_shapes.append(tuple(shape))
        half_axes.append(2 if rows_form else a - 1)

    def body(*refs):
        q_refs, sq_ref = refs[:n], refs[n]
        o_refs, st_ref = refs[n + 1:2 * n + 1], refs[2 * n + 1]
        send, recv = refs[2 * n + 2:]
        x, y, c = lax.axis_index("x"), lax.axis_index("y"), lax.axis_index("c")

        def swap(t, which):
            return pltpu.make_async_remote_copy(
                src_ref=_half(o_refs[t], half_axes[t], which), dst_ref=_half(o_refs[t], half_axes[t], which),
                send_sem=send.at[t], recv_sem=recv.at[t], device_id=(x, y, 1 - c), device_id_type=MESH)

        sent = []
        for t in range(n):
            q = q_refs[t]
            total = ((q[0].astype(F32) + q[1].astype(F32)) + q[2].astype(F32)) + q[3].astype(F32)
            if as_rows[t]:
                total = total.reshape(total.shape[0], 1, total.shape[1])
            _half(o_refs[t], half_axes[t], c)[...] = total
            cp = swap(t, c)
            cp.start()
            sent.append(cp)
        st_ref[...] = ((sq_ref[0] + sq_ref[1]) + sq_ref[2]) + sq_ref[3]
        for t in range(n):
            swap(t, 1 - c).wait_recv()
        for cp in sent:
            cp.wait_send()

    vmem = pl.BlockSpec(memory_space=pltpu.VMEM)
    return pl.pallas_call(
        body, name="grad_reduce_allgather",
        in_specs=[vmem] * (n + 1), out_specs=[vmem] * (n + 1),
        out_shape=[jax.ShapeDtypeStruct(s, F32) for s in shard_shapes] + [jax.ShapeDtypeStruct(sq.shape[1:], F32)],
        scratch_shapes=[pltpu.SemaphoreType.DMA((n,)), pltpu.SemaphoreType.DMA((n,))],
        compiler_params=_params(),
    )(*qs, sq)


def _dot_nt(a, b):
    return lax.dot_general(a, b, (((1,), (1,)), ((), ())), preferred_element_type=F32)


def _dot_tn(a, b):
    return lax.dot_general(a, b, (((0,), (0,)), ((), ())), preferred_element_type=F32)


def _mm_nt(a, bt, out_dtype, name, m, n, tm, tn, row_block=0, scale_first=None, gather=None):
    k = a.shape[1]
    shards, axes = gather if gather is not None else ([], [])
    ng = len(shards)
    grid = (m // tm, n // tn)

    def body(a_ref, b_ref, *rest):
        ins, o_ref, outs, sems = rest[:ng], rest[ng], rest[ng + 1:2 * ng + 1], rest[2 * ng + 1:]
        first = (pl.program_id(0) == 0) & (pl.program_id(1) == 0)
        last = (pl.program_id(0) == grid[0] - 1) & (pl.program_id(1) == grid[1] - 1)
        if ng:
            @pl.when(first)
            def _():
                _gather_start(ins, outs, axes, *sems)

        r = _dot_nt(a_ref[...], b_ref[...])
        if scale_first is not None:
            r = r * jnp.where(pl.program_id(1) == 0, scale_first, 1.0)
        o_ref[...] = r.astype(out_dtype)

        if ng:
            @pl.when(last)
            def _():
                _gather_finish(ins, outs, axes, *sems)

    res = pl.pallas_call(
        body, name=name, grid=grid,
        in_specs=[pl.BlockSpec((tm, k), lambda i, j: (i, 0)), pl.BlockSpec((tn, k), lambda i, j: (row_block + j, 0))]
        + [HBM] * ng,
        out_specs=[pl.BlockSpec((tm, tn), lambda i, j: (i, j))] + [HBM] * ng,
        out_shape=[jax.ShapeDtypeStruct((m, n), out_dtype)] + _gathered_shapes(shards),
        scratch_shapes=_gather_sems(ng) if ng else [],
        compiler_params=_params(("arbitrary", "arbitrary") if ng else ("parallel", "parallel")),
    )(a, bt, *_in_hbm(*shards))
    return (res[0], res[1:]) if ng else res[0]


def _mm_tn(a, b, name, k, tm, tn, tk, chunks=1, m=None, m_off=0, swap=None, scatter=None):
    m = a.shape[1] if m is None else m
    n = b.shape[1]
    cw = n // chunks
    srcs, axes = swap if swap is not None else ([], [])
    ns = len(srcs)
    hs, sc = scatter if scatter is not None else ([], None)
    sc_in, sc_shapes, sc_sems = _scatter_operands(hs, sc) if scatter is not None else ([], [], [])
    nh, nq = len(hs), len(sc_in)
    grid = (m // tm, n // tn, k // tk)

    def body(a_ref, b_ref, *rest):
        s_refs, h_refs = rest[:ns], rest[ns:ns + nq]
        o_ref = rest[ns + nq]
        d_refs, q_refs = rest[ns + nq + 1:2 * ns + nq + 1], rest[2 * ns + nq + 1:2 * ns + 2 * nq + 1]
        sems = rest[2 * ns + 2 * nq + 1:]
        swap_sems, scatter_sems = (sems[:2], sems[2:]) if ns else ((), sems)
        ids = [pl.program_id(d) for d in range(3)]

        def scatter_plan():
            small_in = h_refs[nh] if sc is not None else None
            small_out = q_refs[nh] if sc is not None else None
            return _chip_scatter_plan(h_refs[:nh], small_in, q_refs[:nh], small_out, *scatter_sems)

        if ns or nq:
            @pl.when((ids[0] == 0) & (ids[1] == 0) & (ids[2] == 0))
            def _():
                if ns:
                    for cp in _swap_copies(s_refs, d_refs, axes, *swap_sems):
                        cp.start()
                if nq:
                    own, sends, _ = scatter_plan()
                    for cp in own + sends:
                        cp.start()

        @pl.when(ids[2] == 0)
        def _():
            o_ref[...] = jnp.zeros_like(o_ref)
        r = _dot_tn(a_ref[...].astype(BF16), b_ref[...].astype(BF16))
        if chunks > 1:
            for c in range(chunks):
                o_ref[c] += r[:, c * cw:(c + 1) * cw]
        else:
            o_ref[...] += r

        if ns or nq:
            @pl.when((ids[0] == grid[0] - 1) & (ids[1] == grid[1] - 1) & (ids[2] == grid[2] - 1))
            def _():
                if ns:
                    for cp in _swap_copies(s_refs, d_refs, axes, *swap_sems):
                        cp.wait()
                if nq:
                    own, sends, recvs = scatter_plan()
                    for cp in recvs:
                        cp.wait_recv()
                    for cp in sends:
                        cp.wait_send()
                    for cp in own:
                        cp.wait()

    if chunks > 1:
        assert tn == n
        out_spec = pl.BlockSpec((chunks, tm, cw), lambda i, j, kk: (0, i, 0))
        out_shape = jax.ShapeDtypeStruct((chunks, m, cw), F32)
    else:
        out_spec = pl.BlockSpec((tm, tn), lambda i, j, kk: (i, j))
        out_shape = jax.ShapeDtypeStruct((m, n), F32)
    riders = ns + nq
    res = pl.pallas_call(
        body, name=name, grid=grid,
        in_specs=[pl.BlockSpec((tk, tm), lambda i, j, kk: (kk, m_off + i)), pl.BlockSpec((tk, tn), lambda i, j, kk: (kk, j))]
        + [HBM] * riders,
        out_specs=[out_spec] + [HBM] * riders, out_shape=[out_shape] + _swap_shapes(srcs, axes) + sc_shapes,
        scratch_shapes=([pltpu.SemaphoreType.DMA((ns,)), pltpu.SemaphoreType.DMA((ns,))] if ns else []) + sc_sems,
        compiler_params=_params(("arbitrary",) * 3 if riders else ("parallel", "parallel", "arbitrary")),
    )(a, b, *_in_hbm(*srcs, *sc_in))
    if not riders:
        return res[0]
    out = [res[0]]
    if ns:
        out.append(res[1:1 + ns])
    if nq:
        out.append((res[1 + ns:1 + ns + nh], res[1 + ns + nh] if sc is not None else None))
    return tuple(out)


def _tokens_spec():
    return pl.BlockSpec((ROW_TILE, D_MODEL), lambda i: (jnp.maximum(i - 1, 0), 0))


def _rmsnorm_fwd(x2, g1, lk, shards, axes):
    nb = x2.shape[0] // ROW_TILE + 1
    nblk = lk // ROW_TILE
    ng = len(shards)
    meta_cols = shards[-1].shape[1]

    def body(x_ref, g_ref, *rest):
        ins, hn_ref, outs = rest[:ng], rest[ng], rest[ng + 1:2 * ng + 1]
        sems, meta_buf, meta_sem = rest[2 * ng + 1:2 * ng + 6], rest[2 * ng + 6], rest[2 * ng + 7]
        s = pl.program_id(0)
        blk = (s + 1) % nblk

        @pl.when(s == 0)
        def _():
            _gather_start(ins, outs, axes, *sems)

        def normed(h):
            r = lax.rsqrt(jnp.mean(h * h, axis=-1, keepdims=True) + RMS_EPS)
            return ((h * r) * g_ref[...]).astype(BF16)

        @pl.when(s < nblk - 1)
        def _():
            hn_ref[...] = normed(jnp.where(blk >= nb, 0.0, x_ref[...]))

        @pl.when(s == nblk - 1)
        def _():
            _gather_finish(ins, outs, axes, *sems)
            fetch = pltpu.make_async_copy(outs[-1], meta_buf, meta_sem.at[0])
            fetch.start()
            fetch.wait()
            meta = jnp.concatenate([meta_buf[j] for j in range(N_CHIPS)], axis=1)
            hn_ref[...] = normed(jnp.concatenate([jnp.zeros((PAD_ROWS, D_MODEL), F32), meta], axis=0))

    res = pl.pallas_call(
        body, name="rmsnorm_fwd", grid=(nblk,),
        in_specs=[pl.BlockSpec((ROW_TILE, D_MODEL), lambda s: (jnp.clip((s + 1) % nblk - 1, 0, nb - 2), 0)),
                  pl.BlockSpec((1, D_MODEL), lambda s: (0, 0))] + [HBM] * ng,
        out_specs=[pl.BlockSpec((ROW_TILE, D_MODEL), lambda s: ((s + 1) % nblk, 0))] + [HBM] * ng,
        out_shape=[jax.ShapeDtypeStruct((lk, D_MODEL), BF16)] + _gathered_shapes(shards),
        scratch_shapes=_gather_sems(ng) + [pltpu.VMEM((N_CHIPS, N_META, meta_cols), F32), pltpu.SemaphoreType.DMA((1,))],
        compiler_params=_params(("arbitrary",)),
    )(x2, g1, *_in_hbm(*shards))
    return res[0], res[1:]


def _valid_gate_mask(i):
    row = i * ROW_TILE + lax.broadcasted_iota(jnp.int32, (ROW_TILE, F_COLS), 0)
    col = lax.broadcasted_iota(jnp.int32, (ROW_TILE, F_COLS), 1)
    return (row >= PAD_ROWS) & (col < N_HEADS)


def _gates_fwd(f, bfg):
    nb = f.shape[0] // ROW_TILE

    def body(f_ref, b_ref, c_ref, carry):
        i = pl.program_id(0)

        @pl.when(i == 0)
        def _():
            carry[...] = jnp.zeros_like(carry)

        logit = f_ref[...] + b_ref[...]
        lf = jnp.minimum(logit, 0.0) - jnp.log1p(jnp.exp(-jnp.abs(logit)))
        lf = jnp.where(_valid_gate_mask(i), lf, 0.0)
        r_i = lax.broadcasted_iota(jnp.int32, (ROW_TILE, ROW_TILE), 0)
        c_i = lax.broadcasted_iota(jnp.int32, (ROW_TILE, ROW_TILE), 1)
        tri = (c_i <= r_i).astype(F32)
        c_ref[...] = jnp.dot(tri, lf, precision=HIGHEST, preferred_element_type=F32) + carry[...]
        carry[...] = carry[...] + jnp.sum(lf, axis=0, keepdims=True)

    return pl.pallas_call(
        body, name="gates_fwd", grid=(nb,),
        in_specs=[pl.BlockSpec((ROW_TILE, F_COLS), lambda i: (i, 0)), pl.BlockSpec((1, F_COLS), lambda i: (0, 0))],
        out_specs=pl.BlockSpec((ROW_TILE, F_COLS), lambda i: (i, 0)),
        out_shape=jax.ShapeDtypeStruct(f.shape, F32),
        scratch_shapes=[pltpu.VMEM((1, F_COLS), F32)],
        compiler_params=_params(("arbitrary",)),
    )(f, bfg)


def _pool_counts(i):
    row = i * ROW_TILE + lax.broadcasted_iota(jnp.int32, (ROW_TILE, 1), 0)
    return jnp.maximum(row - PAD_ROWS, 0)


def _trailing_sums(xc, levels):
    acc = xc
    for lv in range(levels):
        acc = acc + pltpu.roll(acc, 1 << lv, 0)
    return acc


def _leading_sums(xc, levels):
    n = xc.shape[0]
    acc = xc
    for lv in range(levels):
        acc = acc + pltpu.roll(acc, n - (1 << lv), 0)
    return acc


def _pool_p(u_cur, u_prev, i):
    pos = _pool_counts(i)
    ps, invs = [], []
    for g, w in enumerate(POOL_WINDOWS):
        sl = slice(g * POOL_GROUP, (g + 1) * POOL_GROUP)
        cur = u_cur[:, sl]
        xc = jnp.concatenate([u_prev[:, sl], cur], axis=0)
        win = _trailing_sums(xc, g + 1)[ROW_TILE:, :]
        inv = 1.0 / jnp.minimum(pos + 1, w).astype(F32)
        ps.append(win * inv - cur)
        invs.append(inv)
    return ps, invs


def _pool_fwd(e, pw, scale):
    nb = e.shape[0] // ROW_TILE

    def body(uc_ref, up_ref, z_ref, pw_ref, sc_ref, y_ref):
        i = pl.program_id(0)
        u_cur = uc_ref[...].astype(F32)
        u_prev = jnp.where(i == 0, 0.0, up_ref[...].astype(F32))
        ps, _ = _pool_p(u_cur, u_prev, i)
        z = z_ref[...].astype(F32)
        gate = z * _sigmoid(z)
        for g in range(len(POOL_WINDOWS)):
            sl = slice(g * POOL_GROUP, (g + 1) * POOL_GROUP)
            yraw = jnp.dot(ps[g].astype(BF16), pw_ref[g], preferred_element_type=F32)
            y_ref[:, sl] = ((yraw * sc_ref[:, sl]) * gate[:, sl]).astype(BF16)

    blk = (ROW_TILE, POOL_WIDTH)
    return pl.pallas_call(
        body, name="pool_fwd", grid=(nb,),
        in_specs=[pl.BlockSpec(blk, lambda i: (i, 0)), pl.BlockSpec(blk, lambda i: (jnp.maximum(i - 1, 0), 0)),
                  pl.BlockSpec(blk, lambda i: (i, 1)),
                  pl.BlockSpec((len(POOL_WINDOWS), POOL_GROUP, POOL_GROUP), lambda i: (0, 0, 0)),
                  pl.BlockSpec((1, POOL_WIDTH), lambda i: (0, 0))],
        out_specs=pl.BlockSpec(blk, lambda i: (i, 0)),
        out_shape=jax.ShapeDtypeStruct((e.shape[0], POOL_WIDTH), BF16),
        compiler_params=_params(("parallel",)),
    )(e, e, e, pw, scale)


def _stack_heads(a):
    first = lax.broadcasted_iota(jnp.int32, a.shape, 1) < HEAD_DIM
    zero = jnp.zeros_like(a)
    return jnp.concatenate([jnp.where(first, a, zero), jnp.where(first, zero, a)], axis=0)


def _unstack_heads(a):
    rows = a.shape[0] // 2
    first = lax.broadcasted_iota(jnp.int32, (rows, LANES), 1) < HEAD_DIM
    return jnp.where(first, a[:rows], a[rows:])


def _causal(i, jj, stacked, kv_tile):
    r = lax.broadcasted_iota(jnp.int32, (stacked * ROW_TILE, kv_tile), 0)
    if stacked == 2:
        r = jnp.where(r >= ROW_TILE, r - ROW_TILE, r)
    kidx = jj * kv_tile + lax.broadcasted_iota(jnp.int32, (stacked * ROW_TILE, kv_tile), 1)
    return kidx <= i * ROW_TILE + r


def _stack_rows(b):
    n = b.shape[1]
    return jnp.concatenate([jnp.broadcast_to(b[0:1], (ROW_TILE, n)), jnp.broadcast_to(b[1:2], (ROW_TILE, n))], axis=0)


def _attn_fwd(qkv, bias, nb):
    lk = qkv.shape[0]
    lp = nb * ROW_TILE
    nkb = lk // KV_TILE
    n_pairs = N_HEADS // 2

    def body(q_ref, k_ref, v_ref, b_ref, o_ref, lse_ref):
        i = pl.program_id(1)
        qs = _stack_heads(q_ref[...])
        last = (i * ROW_TILE) // KV_TILE

        def block(jj, carry, masked, n_keys=KV_TILE):
            m, l, acc = carry
            rows = pl.ds(pl.multiple_of(jj * KV_TILE, KV_TILE), n_keys)
            s = _dot_nt(qs, k_ref[rows, :]) - _stack_rows(b_ref[0, 0, jj][:, :n_keys])
            if masked:
                s = jnp.where(_causal(i, jj, 2, KV_TILE)[:, :n_keys], s, NEG)
            m_new = jnp.maximum(m, jnp.max(s, axis=1, keepdims=True))
            alpha = jnp.exp(m - m_new)
            p = jnp.exp(s - m_new)
            l = alpha * l + jnp.sum(p, axis=1, keepdims=True)
            acc = alpha * acc + jnp.dot(p.astype(BF16), v_ref[rows, :], preferred_element_type=F32)
            return m_new, l, acc

        init = (jnp.full((2 * ROW_TILE, 1), NEG, F32), jnp.zeros((2 * ROW_TILE, 1), F32),
                jnp.zeros((2 * ROW_TILE, LANES), F32))
        def finish(carry):
            m, l, acc = carry
            o_ref[...] = _unstack_heads(acc / l)
            lse_ref[...] = _unstack_heads(jnp.broadcast_to(m + jnp.log(l), (2 * ROW_TILE, LANES)))

        carry = lax.fori_loop(0, last, lambda jj, c: block(jj, c, False), init)
        ends = [lambda c, n=(r + 1) * ROW_TILE: finish(block(last, c, True, n)) for r in range(KV_TILE // ROW_TILE)]
        lax.switch(i - last * (KV_TILE // ROW_TILE), ends, carry)

    return pl.pallas_call(
        body, name="attn_fwd", grid=(n_pairs, nb),
        in_specs=[pl.BlockSpec((ROW_TILE, LANES), lambda hp, i: (i, hp)),
                  pl.BlockSpec((lk, LANES), lambda hp, i: (0, n_pairs + hp)),
                  pl.BlockSpec((lk, LANES), lambda hp, i: (0, 2 * n_pairs + hp)),
                  pl.BlockSpec((1, 1, nkb, 2, KV_TILE), lambda hp, i: (i, hp, 0, 0, 0))],
        out_specs=[pl.BlockSpec((ROW_TILE, LANES), lambda hp, i: (i, hp)),
                   pl.BlockSpec((ROW_TILE, LANES), lambda hp, i: (i, hp))],
        out_shape=[jax.ShapeDtypeStruct((lp, ATTN_WIDTH), F32), jax.ShapeDtypeStruct((lp, ATTN_WIDTH), F32)],
        compiler_params=_params(("parallel", "parallel")),
    )(qkv, qkv, qkv, bias)


def _merge_fwd(y_pool, o, e, wup_p, wup_a):
    lp = o.shape[0]
    nb = lp // ROW_TILE

    def body(yp_ref, o_ref, e_ref, wp_ref, wa_ref, mg_ref, ya_ref):
        za = e_ref[:, E_ZA:E_GP].astype(F32)
        ya = (o_ref[...] * (za * _sigmoid(za))).astype(BF16)
        ya_ref[...] = ya
        a_pool = jnp.dot(yp_ref[...], wp_ref[...], preferred_element_type=F32)
        a_attn = jnp.dot(ya, wa_ref[...], preferred_element_type=F32)
        mg_ref[...] = (_sigmoid(e_ref[:, E_GP:E_GA].astype(F32)) * a_pool
                       + _sigmoid(e_ref[:, E_GA:E_COLS].astype(F32)) * a_attn).astype(BF16)

    return pl.pallas_call(
        body, name="merge_fwd", grid=(nb,),
        in_specs=[pl.BlockSpec((ROW_TILE, POOL_WIDTH), lambda i: (i, 0)),
                  pl.BlockSpec((ROW_TILE, ATTN_WIDTH), lambda i: (i, 0)),
                  pl.BlockSpec((ROW_TILE, E_COLS), lambda i: (i, 0)),
                  pl.BlockSpec((POOL_WIDTH, D_MODEL), lambda i: (0, 0)),
                  pl.BlockSpec((ATTN_WIDTH, D_MODEL), lambda i: (0, 0))],
        out_specs=[pl.BlockSpec((ROW_TILE, D_MODEL), lambda i: (i, 0)),
                   pl.BlockSpec((ROW_TILE, ATTN_WIDTH), lambda i: (i, 0))],
        out_shape=[jax.ShapeDtypeStruct((lp, D_MODEL), BF16), jax.ShapeDtypeStruct((lp, ATTN_WIDTH), BF16)],
        compiler_params=_params(("parallel",)),
    )(y_pool, o, e, wup_p, wup_a)


def _head_fwd_bwd(merged, w_out, x2, metapad, gf, target):
    lp = merged.shape[0]
    nb = lp // ROW_TILE

    def body(mg_ref, w_ref, x_ref, mp_ref, g_ref, t_ref, dh_ref, loss_ref, dg_ref):
        i = pl.program_id(0)

        @pl.when(i == 0)
        def _():
            loss_ref[...] = jnp.zeros_like(loss_ref)
            dg_ref[...] = jnp.zeros_like(dg_ref)

        h0 = jnp.where(i == 0, mp_ref[...], x_ref[...])
        h1 = h0 + jnp.dot(mg_ref[...], w_ref[...], preferred_element_type=F32)
        r = lax.rsqrt(jnp.mean(h1 * h1, axis=-1, keepdims=True) + RMS_EPS)
        xhat = h1 * r
        g = g_ref[...]
        err = jnp.where(i == 0, 0.0, xhat * g - t_ref[...])
        loss_ref[...] += 0.5 * jnp.sum(jnp.mean(err * err, axis=-1, keepdims=True))
        dy = err / D_MODEL
        dg_ref[...] += jnp.sum(dy * xhat, axis=0, keepdims=True)
        dxhat = dy * g
        dh_ref[...] = r * (dxhat - xhat * jnp.mean(dxhat * xhat, axis=-1, keepdims=True))

    return pl.pallas_call(
        body, name="head_fwd_bwd", grid=(nb,),
        in_specs=[pl.BlockSpec((ROW_TILE, D_MODEL), lambda i: (i, 0)),
                  pl.BlockSpec((D_MODEL, D_MODEL), lambda i: (0, 0)),
                  _tokens_spec(), pl.BlockSpec((ROW_TILE, D_MODEL), lambda i: (0, 0)),
                  pl.BlockSpec((1, D_MODEL), lambda i: (0, 0)), _tokens_spec()],
        out_specs=[pl.BlockSpec((ROW_TILE, D_MODEL), lambda i: (i, 0)),
                   pl.BlockSpec((1, LANES), lambda i: (0, 0)), pl.BlockSpec((1, D_MODEL), lambda i: (0, 0))],
        out_shape=[jax.ShapeDtypeStruct((lp, D_MODEL), F32), jax.ShapeDtypeStruct((1, LANES), F32),
                   jax.ShapeDtypeStruct((1, D_MODEL), F32)],
        compiler_params=_params(("arbitrary",)),
    )(merged, w_out, x2, metapad, gf, target)


def _per_head_rowsum(t):
    head = lax.broadcasted_iota(jnp.int32, t.shape, 1) // HEAD_DIM
    out = jnp.zeros_like(t)
    for h in range(N_HEADS):
        sel = head == h
        out = jnp.where(sel, jnp.sum(jnp.where(sel, t, 0.0), axis=1, keepdims=True), out)
    return out


def _merge_bwd(dh1, w_out, y_pool, y_attn, wup_p, wup_a, e, o):
    lp = o.shape[0]
    nb = lp // ROW_TILE

    def body(dh_ref, wo_ref, yp_ref, ya_ref, wp_ref, wa_ref, e_ref, o_ref,
             dap_ref, daa_ref, dg_ref, do_ref, delta_ref, dyp_ref):
        dmerged = _dot_nt(dh_ref[...].astype(BF16), wo_ref[...])
        a_pool = jnp.dot(yp_ref[...], wp_ref[...], preferred_element_type=F32)
        a_attn = jnp.dot(ya_ref[...], wa_ref[...], preferred_element_type=F32)
        sp = _sigmoid(e_ref[:, E_GP:E_GA].astype(F32))
        sa = _sigmoid(e_ref[:, E_GA:E_COLS].astype(F32))
        dap = (dmerged * sp).astype(BF16)
        daa = (dmerged * sa).astype(BF16)
        dap_ref[...] = dap
        daa_ref[...] = daa
        dg_ref[:, ATTN_WIDTH:ATTN_WIDTH + D_MODEL] = (dmerged * a_pool * (sp * (1.0 - sp))).astype(BF16)
        dg_ref[:, ATTN_WIDTH + D_MODEL:] = (dmerged * a_attn * (sa * (1.0 - sa))).astype(BF16)
        dyp_ref[...] = _dot_nt(dap, wp_ref[...])
        dya = _dot_nt(daa, wa_ref[...])
        za = e_ref[:, E_ZA:E_GP].astype(F32)
        sz = _sigmoid(za)
        o = o_ref[...]
        do = dya * (za * sz)
        do_ref[...] = do.astype(BF16)
        dg_ref[:, :ATTN_WIDTH] = (dya * o * (sz * (1.0 + za * (1.0 - sz)))).astype(BF16)
        delta_ref[...] = _per_head_rowsum(do * o)

    row = lambda w: pl.BlockSpec((ROW_TILE, w), lambda i: (i, 0))
    full = lambda a: pl.BlockSpec(a.shape, lambda i: (0, 0))
    return pl.pallas_call(
        body, name="merge_bwd", grid=(nb,),
        in_specs=[row(D_MODEL), full(w_out), row(POOL_WIDTH), row(ATTN_WIDTH), full(wup_p), full(wup_a),
                  row(E_COLS), row(ATTN_WIDTH)],
        out_specs=[row(D_MODEL), row(D_MODEL), pl.BlockSpec((ROW_TILE, MAIN_COLS // 2), lambda i: (i, 1)),
                   row(ATTN_WIDTH), row(ATTN_WIDTH), row(POOL_WIDTH)],
        out_shape=[jax.ShapeDtypeStruct((lp, D_MODEL), BF16), jax.ShapeDtypeStruct((lp, D_MODEL), BF16),
                   jax.ShapeDtypeStruct((lp, MAIN_COLS), BF16),
                   jax.ShapeDtypeStruct((lp, ATTN_WIDTH), BF16), jax.ShapeDtypeStruct((lp, ATTN_WIDTH), F32),
                   jax.ShapeDtypeStruct((lp, POOL_WIDTH), F32)],
        compiler_params=_params(("parallel",)),
    )(dh1, w_out, y_pool, y_attn, wup_p, wup_a, e, o)


def _fill_dproj(dproj_half, du, dzp, dq, dk, dv):
    lp = dproj_half.shape[0]
    nb = lp // ROW_TILE

    def body(base_ref, du_ref, dzp_ref, dq_ref, dk_ref, dv_ref, o_ref):
        parts = [du_ref[...], dzp_ref[...], (dq_ref[...] * HEAD_DIM ** -0.5).astype(BF16), dk_ref[...], dv_ref[...]]
        for t, p in enumerate(parts):
            o_ref[:, t * ATTN_WIDTH:(t + 1) * ATTN_WIDTH] = p

    blk = pl.BlockSpec((ROW_TILE, ATTN_WIDTH), lambda i: (i, 0))
    return pl.pallas_call(
        body, name="fill_dproj", grid=(nb,),
        in_specs=[pl.BlockSpec(memory_space=pl.ANY)] + [blk] * 5,
        out_specs=pl.BlockSpec((ROW_TILE, MAIN_COLS // 2), lambda i: (i, 0)),
        out_shape=jax.ShapeDtypeStruct(dproj_half.shape, BF16),
        input_output_aliases={0: 0},
        compiler_params=_params(("parallel",)),
    )(dproj_half, du, dzp, dq, dk, dv)


def _pool_bwd_local(e, dy_pool, pw, scale):
    lp = e.shape[0]
    nb = lp // ROW_TILE
    ng = len(POOL_WINDOWS)

    def body(uc_ref, up_ref, z_ref, dy_ref, pw_ref, sc_ref, dpc_ref, dz_ref, dsc_ref, dpw_ref):
        i = pl.program_id(0)

        @pl.when(i == 0)
        def _():
            dsc_ref[...] = jnp.zeros_like(dsc_ref)
            dpw_ref[...] = jnp.zeros_like(dpw_ref)

        u_cur = uc_ref[...].astype(F32)
        u_prev = jnp.where(i == 0, 0.0, up_ref[...].astype(F32))
        ps, invs = _pool_p(u_cur, u_prev, i)
        z = z_ref[...].astype(F32)
        sz = _sigmoid(z)
        dy = dy_ref[...]
        dypre = dy * (z * sz)
        dsilu = sz * (1.0 + z * (1.0 - sz))
        for g in range(ng):
            sl = slice(g * POOL_GROUP, (g + 1) * POOL_GROUP)
            pb = ps[g].astype(BF16)
            w = pw_ref[g]
            yraw = jnp.dot(pb, w, preferred_element_type=F32)
            sc = sc_ref[:, sl]
            dz_ref[:, sl] = (dy[:, sl] * (yraw * sc) * dsilu[:, sl]).astype(BF16)
            dsc_ref[:, sl] += jnp.sum(dypre[:, sl] * yraw, axis=0, keepdims=True)
            dyraw = (dypre[:, sl] * sc).astype(BF16)
            dpw_ref[g] += _dot_tn(pb, dyraw)
            dpc_ref[:, sl] = _dot_nt(dyraw, w) * invs[g]

    blk = (ROW_TILE, POOL_WIDTH)
    return pl.pallas_call(
        body, name="pool_bwd_local", grid=(nb,),
        in_specs=[pl.BlockSpec(blk, lambda i: (i, 0)), pl.BlockSpec(blk, lambda i: (jnp.maximum(i - 1, 0), 0)),
                  pl.BlockSpec(blk, lambda i: (i, 1)), pl.BlockSpec(blk, lambda i: (i, 0)),
                  pl.BlockSpec((ng, POOL_GROUP, POOL_GROUP), lambda i: (0, 0, 0)),
                  pl.BlockSpec((1, POOL_WIDTH), lambda i: (0, 0))],
        out_specs=[pl.BlockSpec(blk, lambda i: (i, 0)), pl.BlockSpec(blk, lambda i: (i, 0)),
                   pl.BlockSpec((1, POOL_WIDTH), lambda i: (0, 0)),
                   pl.BlockSpec((ng, POOL_GROUP, POOL_GROUP), lambda i: (0, 0, 0))],
        out_shape=[jax.ShapeDtypeStruct((lp, POOL_WIDTH), F32), jax.ShapeDtypeStruct((lp, POOL_WIDTH), BF16),
                   jax.ShapeDtypeStruct((1, POOL_WIDTH), F32),
                   jax.ShapeDtypeStruct((ng, POOL_GROUP, POOL_GROUP), F32)],
        compiler_params=_params(("arbitrary",)),
    )(e, e, e, dy_pool, pw, scale)


def _pool_bwd_window(dpc):
    lp = dpc.shape[0]
    nb = lp // ROW_TILE

    def body(cur_ref, nxt_ref, du_ref):
        i = pl.program_id(0)
        cur = cur_ref[...]
        nxt = jnp.where(i == nb - 1, 0.0, nxt_ref[...])
        pos = _pool_counts(i)
        for g, w in enumerate(POOL_WINDOWS):
            sl = slice(g * POOL_GROUP, (g + 1) * POOL_GROUP)
            xc = jnp.concatenate([cur[:, sl], nxt[:, sl]], axis=0)
            win = _leading_sums(xc, g + 1)[:ROW_TILE, :]
            dp = cur[:, sl] * jnp.minimum(pos + 1, w).astype(F32)
            du_ref[:, sl] = (win - dp).astype(BF16)

    blk = (ROW_TILE, POOL_WIDTH)
    return pl.pallas_call(
        body, name="pool_bwd_window", grid=(nb,),
        in_specs=[pl.BlockSpec(blk, lambda i: (i, 0)), pl.BlockSpec(blk, lambda i: (jnp.minimum(i + 1, nb - 1), 0))],
        out_specs=pl.BlockSpec(blk, lambda i: (i, 0)),
        out_shape=jax.ShapeDtypeStruct((lp, POOL_WIDTH), BF16),
        compiler_params=_params(("parallel",)),
    )(dpc, dpc)


def _attn_bwd(qkv, do, lse, delta, bias, nb):
    lk = qkv.shape[0]
    lp = nb * ROW_TILE
    nkb = lk // KV_TILE_BWD
    n_pairs = N_HEADS // 2
    per_kv = KV_TILE_BWD // ROW_TILE

    def body(q_ref, k_ref, v_ref, do_ref, lse_ref, dl_ref, b_ref, dq_ref, dk_ref, dv_ref, dc_ref, dcq_ref,
             dk_acc, dv_acc, dc_acc):
        jj = pl.program_id(1)

        @pl.when(jj == 0)
        def _():
            dq_ref[...] = jnp.zeros_like(dq_ref)
            dcq_ref[...] = jnp.zeros_like(dcq_ref)

        dk_acc[...] = jnp.zeros_like(dk_acc)
        dv_acc[...] = jnp.zeros_like(dv_acc)
        dc_acc[...] = jnp.zeros_like(dc_acc)

        def block(i, n_keys, masked):
            kb, vb = k_ref[:n_keys, :], v_ref[:n_keys, :]
            rows = pl.ds(pl.multiple_of(i * ROW_TILE, ROW_TILE), ROW_TILE)
            qs = _stack_heads(q_ref[rows, :])
            dos = _stack_heads(do_ref[rows, :])
            lse_i, dl_i = lse_ref[rows, :], dl_ref[rows, :]
            s = _dot_nt(qs, kb)
            dp = _dot_nt(dos, vb)
            if masked:
                valid = _causal(i, jj, 1, KV_TILE_BWD)[:, :n_keys]
            ps, dss, dcs, rowsums = [], [], [], []
            for hd in range(2):
                half = slice(hd * ROW_TILE, (hd + 1) * ROW_TILE)
                col = slice(hd * HEAD_DIM, hd * HEAD_DIM + 1)
                sh = s[half] - b_ref[i, 0, 0, hd:hd + 1, :n_keys]
                if masked:
                    sh = jnp.where(valid, sh, NEG)
                p = jnp.exp(sh - lse_i[:, col])
                ds = p * (dp[half] - dl_i[:, col])
                ps.append(p.astype(BF16))
                dss.append(ds.astype(BF16))
                dcs.append(jnp.sum(ds, axis=0, keepdims=True))
                rowsums.append(jnp.sum(ds, axis=1, keepdims=True))
            dsb = jnp.concatenate(dss, axis=0)
            dv_acc[:n_keys, :] += _dot_tn(jnp.concatenate(ps, axis=0), dos)
            dk_acc[:n_keys, :] += _dot_tn(dsb, qs)
            dc_acc[:, :n_keys] -= jnp.concatenate(dcs, axis=0)
            dq_ref[rows, :] += _unstack_heads(jnp.dot(dsb, kb, preferred_element_type=F32))
            dcq_ref[rows, :] += _unstack_heads(jnp.broadcast_to(jnp.concatenate(rowsums, axis=0), (2 * ROW_TILE, LANES)))

        first_q = per_kv * jj
        for r in range(per_kv):
            @pl.when(first_q + r < nb)
            def _():
                block(first_q + r, (r + 1) * ROW_TILE, True)

        def rest(i, carry):
            block(i, KV_TILE_BWD, False)
            return carry

        lax.fori_loop(jnp.minimum(first_q + per_kv, nb), nb, rest, 0)
        dk_ref[...] = dk_acc[...].astype(BF16)
        dv_ref[...] = dv_acc[...].astype(BF16)
        dc_ref[0, 0] = dc_acc[...]

    whole = lambda rows: pl.BlockSpec((rows, LANES), lambda hp, jj: (0, hp))
    kv_blk = lambda off: pl.BlockSpec((KV_TILE_BWD, LANES), lambda hp, jj: (jj, off + hp))
    return pl.pallas_call(
        body, name="attn_bwd", grid=(n_pairs, nkb),
        in_specs=[whole(lk), kv_blk(n_pairs), kv_blk(2 * n_pairs), whole(lp), whole(lp), whole(lp),
                  pl.BlockSpec((nb, 1, 1, 2, KV_TILE_BWD), lambda hp, jj: (0, hp, jj, 0, 0))],
        out_specs=[whole(lp), kv_blk(0), kv_blk(0),
                   pl.BlockSpec((1, 1, 2, KV_TILE_BWD), lambda hp, jj: (hp, jj, 0, 0)), whole(lp)],
        out_shape=[jax.ShapeDtypeStruct((lp, ATTN_WIDTH), F32), jax.ShapeDtypeStruct((lk, ATTN_WIDTH), BF16),
                   jax.ShapeDtypeStruct((lk, ATTN_WIDTH), BF16),
                   jax.ShapeDtypeStruct((n_pairs, nkb, 2, KV_TILE_BWD), F32),
                   jax.ShapeDtypeStruct((lp, ATTN_WIDTH), F32)],
        scratch_shapes=[pltpu.VMEM((KV_TILE_BWD, LANES), F32), pltpu.VMEM((KV_TILE_BWD, LANES), F32),
                        pltpu.VMEM((2, KV_TILE_BWD), F32)],
        compiler_params=_params(("parallel", "arbitrary")),
    )(qkv, qkv, qkv, do, lse, delta, bias)


def _gates_bwd(dc, dcq, f, bfg):
    nb = f.shape[0] // ROW_TILE

    def body(dc_ref, dcq_ref, f_ref, b_ref, df_ref, db_ref, carry):
        step = pl.program_id(0)
        i = nb - 1 - step

        @pl.when(step == 0)
        def _():
            carry[...] = jnp.zeros_like(carry)
            db_ref[...] = jnp.zeros_like(db_ref)

        dcb = dc_ref[...]
        lane = lax.broadcasted_iota(jnp.int32, (ROW_TILE, F_COLS), 1)
        for h in range(N_HEADS):
            dcb = dcb + jnp.where(lane == h, dcq_ref[:, HEAD_DIM * h:HEAD_DIM * h + 1], 0.0)
        r_i = lax.broadcasted_iota(jnp.int32, (ROW_TILE, ROW_TILE), 0)
        c_i = lax.broadcasted_iota(jnp.int32, (ROW_TILE, ROW_TILE), 1)
        upper = (c_i >= r_i).astype(F32)
        dlf = jnp.dot(upper, dcb, precision=HIGHEST, preferred_element_type=F32) + carry[...]
        carry[...] = carry[...] + jnp.sum(dcb, axis=0, keepdims=True)
        logit = f_ref[...] + b_ref[...]
        dlogit = jnp.where(_valid_gate_mask(i), dlf * _sigmoid(-logit), 0.0)
        df_ref[...] = dlogit.astype(BF16)
        db_ref[...] += jnp.sum(dlogit, axis=0, keepdims=True)

    blk = pl.BlockSpec((ROW_TILE, F_COLS), lambda s: (nb - 1 - s, 0))
    wide = pl.BlockSpec((ROW_TILE, ATTN_WIDTH), lambda s: (nb - 1 - s, 0))
    one = pl.BlockSpec((1, F_COLS), lambda s: (0, 0))
    return pl.pallas_call(
        body, name="gates_bwd", grid=(nb,),
        in_specs=[blk, wide, blk, one], out_specs=[blk, one],
        out_shape=[jax.ShapeDtypeStruct(f.shape, BF16), jax.ShapeDtypeStruct((1, F_COLS), F32)],
        scratch_shapes=[pltpu.VMEM((1, F_COLS), F32)],
        compiler_params=_params(("arbitrary",)),
    )(dc, dcq, f, bfg)


def _input_bwd(dproj, df, wt_e, wt_qkv, wt_f, x2, metapad, dh1, g1, hs, sc=None):
    nb = x2.shape[0] // ROW_TILE + 1
    n = len(hs)
    sc_in, sc_shapes, sc_sems = _scatter_operands(hs, sc)
    nq = len(sc_in)

    def body(dp_ref, df_ref, we_ref, w_ref, wf_ref, x_ref, mp_ref, dh_ref, g_ref, *rest):
        h_refs, s_ref = rest[:n], (rest[n] if sc is not None else None)
        gx_ref, g0_ref, dg_ref = rest[nq:nq + 3]
        q_refs, sq_ref = rest[nq + 3:nq + 3 + n], (rest[nq + 3 + n] if sc is not None else None)
        sems = rest[2 * nq + 3:]
        i = pl.program_id(0)

        @pl.when(i == 0)
        def _():
            dg_ref[...] = jnp.zeros_like(dg_ref)
            own, sends, _ = _chip_scatter_plan(h_refs, s_ref, q_refs, sq_ref, *sems)
            for cp in own + sends:
                cp.start()

        dhn = (jnp.dot(dp_ref[:, :E_ZA], we_ref[:E_ZA, :], preferred_element_type=F32)
               + jnp.dot(dp_ref[:, E_ZA:MAIN_COLS // 2], w_ref[...], preferred_element_type=F32)
               + jnp.dot(dp_ref[:, MAIN_COLS // 2:], we_ref[E_ZA:, :], preferred_element_type=F32)
               + jnp.dot(df_ref[...], wf_ref[...], preferred_element_type=F32))
        h0 = jnp.where(i == 0, mp_ref[...], x_ref[...])
        r = lax.rsqrt(jnp.mean(h0 * h0, axis=-1, keepdims=True) + RMS_EPS)
        xhat = h0 * r
        dg_ref[...] += jnp.sum(dhn * xhat, axis=0, keepdims=True)
        dxhat = dhn * g_ref[...]
        dh0 = dh_ref[...] + r * (dxhat - xhat * jnp.mean(dxhat * xhat, axis=-1, keepdims=True))
        gx_ref[...] = dh0

        @pl.when(i == 0)
        def _():
            g0_ref[...] = dh0

        @pl.when(i == nb - 1)
        def _():
            own, sends, recvs = _chip_scatter_plan(h_refs, s_ref, q_refs, sq_ref, *sems)
            for cp in recvs:
                cp.wait_recv()
            for cp in sends:
                cp.wait_send()
            for cp in own:
                cp.wait()

    const = lambda shape: pl.BlockSpec(shape, lambda i: (0, 0))
    res = pl.pallas_call(
        body, name="input_bwd", grid=(nb,),
        in_specs=[pl.BlockSpec((ROW_TILE, MAIN_COLS), lambda i: (i, 0)), pl.BlockSpec((ROW_TILE, F_COLS), lambda i: (i, 0)),
                  const((E_COLS, D_MODEL)), const((QKV_COLS, D_MODEL)), const((F_COLS, D_MODEL)),
                  _tokens_spec(), const((ROW_TILE, D_MODEL)),
                  pl.BlockSpec((ROW_TILE, D_MODEL), lambda i: (i, 0)), const((1, D_MODEL))] + [HBM] * nq,
        out_specs=[_tokens_spec(), const((ROW_TILE, D_MODEL)), const((1, D_MODEL))] + [HBM] * nq,
        out_shape=[jax.ShapeDtypeStruct(x2.shape, F32), jax.ShapeDtypeStruct((ROW_TILE, D_MODEL), F32),
                   jax.ShapeDtypeStruct((1, D_MODEL), F32)] + sc_shapes,
        scratch_shapes=sc_sems,
        compiler_params=_params(("arbitrary",), vmem=56 * 1024 * 1024),
    )(dproj, df, wt_e, wt_qkv, wt_f, x2, metapad, dh1, g1, *_in_hbm(*sc_in))
    return res[:3], res[3:3 + n], (res[3 + n] if sc is not None else None)


def _adamw(w, g, m, v, name):
    rows, cols = w.shape
    if rows % 8 == 0:
        tr, tc = _row_tile8(rows), cols
    else:
        tr, tc = rows, (2 * LANES if cols % (2 * LANES) == 0 and rows > 8 else cols)

    def body(w_ref, g_ref, m_ref, v_ref, d_ref, mo_ref, vo_ref):
        g_ = g_ref[...]
        m_new = ADAM_B1 * m_ref[...] + (1.0 - ADAM_B1) * g_
        v_new = ADAM_B2 * v_ref[...] + (1.0 - ADAM_B2) * (g_ * g_)
        m_hat = m_new / (1.0 - ADAM_B1 ** ADAM_STEP)
        v_hat = v_new / (1.0 - ADAM_B2 ** ADAM_STEP)
        d_ref[...] = -ADAM_LR * (m_hat / (jnp.sqrt(v_hat) + ADAM_EPS) + ADAM_WD * w_ref[...])
        mo_ref[...] = m_new
        vo_ref[...] = v_new

    blk = pl.BlockSpec((tr, tc), lambda i, j: (i, j))
    return pl.pallas_call(
        body, name=name, grid=(rows // tr, cols // tc),
        in_specs=[blk] * 4, out_specs=[blk] * 3,
        out_shape=[jax.ShapeDtypeStruct(w.shape, F32)] * 3,
        compiler_params=_params(("parallel", "parallel")),
    )(w, g, m, v)


def _adamw_native(w3, g3, m3, v3, name):
    rows = w3.shape[0]
    tr = rows // 2
    blk = pl.BlockSpec((tr,) + w3.shape[1:], lambda i: (i, 0, 0))
    shape = jax.ShapeDtypeStruct(w3.shape, F32)

    def moments(g_ref, m_ref, v_ref, mo_ref, vo_ref):
        g_ = g_ref[...]
        mo_ref[...] = ADAM_B1 * m_ref[...] + (1.0 - ADAM_B1) * g_
        vo_ref[...] = ADAM_B2 * v_ref[...] + (1.0 - ADAM_B2) * (g_ * g_)

    new_m, new_v = pl.pallas_call(
        moments, name=name + "_moments", grid=(2,), in_specs=[blk] * 3, out_specs=[blk] * 2, out_shape=[shape] * 2,
        compiler_params=_params(("parallel",)),
    )(g3, m3, v3)

    def delta(w_ref, m_ref, v_ref, d_ref):
        m_hat = m_ref[...] / (1.0 - ADAM_B1 ** ADAM_STEP)
        v_hat = v_ref[...] / (1.0 - ADAM_B2 ** ADAM_STEP)
        d_ref[...] = -ADAM_LR * (m_hat / (jnp.sqrt(v_hat) + ADAM_EPS) + ADAM_WD * w_ref[...])

    d = pl.pallas_call(
        delta, name=name + "_delta", grid=(2,), in_specs=[blk] * 3, out_specs=blk, out_shape=shape,
        compiler_params=_params(("parallel",)),
    )(w3, new_m, new_v)
    return d, new_m, new_v


def _row_tile8(rows):
    best = rows
    for t in range(8, 257, 8):
        if rows % t == 0:
            best = t
    return best


def kernel(x, meta_tokens, norm_g, w_in, b_forget, pool_w, pool_scale, w_up_pool, w_up_attn, w_out, final_norm_g, loss_target, m_meta_tokens, m_norm_g, m_w_in, m_b_forget, m_pool_w, m_pool_scale, m_w_up_pool, m_w_up_attn, m_w_out, m_final_norm_g, v_meta_tokens, v_norm_g, v_w_in, v_b_forget, v_pool_w, v_pool_scale, v_w_up_pool, v_w_up_attn, v_w_out, v_final_norm_g):
    seq = x.shape[1]
    assert seq % ROW_TILE == 0 and x.shape[0] == 1
    lp = seq + ROW_TILE
    nb = lp // ROW_TILE
    lk = -(-lp // KV_TILE_BWD) * KV_TILE_BWD
    core = jnp.reshape(lax.axis_index("c"), (1,)).astype(jnp.int32)
    x2 = x[0]
    target = loss_target[0]
    sh_d = D_MODEL // N_CHIPS

    to_rows = lambda a: jnp.transpose(a, (2, 0, 1))
    from_rows = lambda a: jnp.transpose(a, (1, 2, 0))
    gf = final_norm_g.reshape(1, D_MODEL)
    bfg = jnp.pad(b_forget, ((0, 0), (0, F_COLS - N_HEADS)))
    pw_b = pool_w[0].astype(BF16)
    hn, (wg_in, meta_g) = _rmsnorm_fwd(x2, norm_g, lk, [jnp.transpose(w_in[0]).astype(BF16), meta_tokens], [1, 0])
    wt = wg_in.reshape(-1, D_MODEL)
    wt_e = jnp.concatenate([wt[REF_U:REF_Q], wt[REF_ZA:REF_F], wt[REF_GP:]], axis=0)
    wt_qkv = wt[REF_Q:REF_ZA]
    wt_f = jnp.pad(wt[REF_F:REF_GP], ((0, F_COLS - N_HEADS), (0, 0)))
    meta_full = jnp.transpose(meta_g, (1, 0, 2)).reshape(N_META, D_MODEL)
    metapad = jnp.pad(meta_full, ((PAD_ROWS, 0), (0, 0)))

    tm = _row_tile(lp, 2200)
    e, (wg_up_p, wg_up_a, wg_out) = _mm_nt(
        hn, wt_e, BF16, "in_proj_gates", lp, E_COLS, tm, 512,
        gather=([w_up_pool[0].astype(BF16), w_up_attn[0].astype(BF16), w_out[0].astype(BF16)], [0, 0, 0]))
    wup_p = jnp.transpose(wg_up_p, (1, 0, 2)).reshape(POOL_WIDTH, D_MODEL)
    wup_a = jnp.transpose(wg_up_a, (1, 0, 2)).reshape(ATTN_WIDTH, D_MODEL)
    wout = wg_out.reshape(D_MODEL, D_MODEL)
    qkv = _mm_nt(hn, wt_qkv, BF16, "in_proj_qkv", lk, QKV_COLS, _row_tile(lk, 2600), 512, scale_first=HEAD_DIM ** -0.5)
    f = _mm_nt(hn, wt_f, F32, "in_proj_forget", lp, F_COLS, tm, F_COLS)
    c = _gates_fwd(f, bfg)
    c_t = jnp.transpose(c[:, :N_HEADS])
    bias = c_t[None, :, :] - jnp.transpose(c_t[:, ::ROW_TILE])[:, :, None]
    bias = jnp.where(jnp.arange(lp) < PAD_ROWS, -NEG, bias)
    bias = jnp.pad(bias, ((0, 0), (0, 0), (0, lk - lp)))
    by_kv = lambda t: jnp.transpose(bias.reshape(nb, N_HEADS // 2, 2, lk // t, t), (0, 1, 3, 2, 4))
    bias, bias_bwd = by_kv(KV_TILE), by_kv(KV_TILE_BWD)
    y_pool = _pool_fwd(e, pw_b, pool_scale)
    o, lse = _attn_fwd(qkv, bias, nb)
    merged, y_attn = _merge_fwd(y_pool, o, e, wup_p, wup_a)
    dh1, loss_part, dgf = _head_fwd_bwd(merged, wout, x2, metapad, gf, target)

    dap, daa, dproj_half, do, delta, dy_pool = _merge_bwd(dh1, wout, y_pool, y_attn, wup_p, wup_a, e, o)
    dpc, dzp, dscale, dpw = _pool_bwd_local(e, dy_pool, pw_b, pool_scale)
    du = _pool_bwd_window(dpc)
    dq, dk, dv, dc4, dcq = _attn_bwd(qkv, do, lse, delta, bias_bwd, nb)
    dc = jnp.transpose(dc4, (1, 3, 0, 2)).reshape(lk, N_HEADS)[:lp]
    df, db = _gates_bwd(jnp.pad(dc, ((0, 0), (0, F_COLS - N_HEADS))), dcq, f, bfg)
    dproj = _fill_dproj(dproj_half, du, dzp, dq, dk, dv)
    tk = _row_tile(lp, 1100)
    dw_out = _mm_tn(merged, dh1, "grad_w_out", lp, 512, D_MODEL, tk)
    dw_up_p = _mm_tn(y_pool, dap, "grad_w_up_pool", lp, 512, D_MODEL, lp, chunks=N_CHIPS)
    dw_up_a = _mm_tn(y_attn, daa, "grad_w_up_attn", lp, 512, D_MODEL, lp, chunks=N_CHIPS)
    dwt_f = _mm_tn(df, hn, "grad_w_in_forget", lp, F_COLS, D_MODEL, tk)

    def pad8(a):
        return jnp.pad(a, ((0, (-a.shape[0]) % 8), (0, 0)))

    small_parts = [pad8(a) for a in (dgf.reshape(-1, LANES), dscale.reshape(-1, LANES), db, dpw.reshape(-1, LANES),
                                     loss_part)]
    small = jnp.concatenate(small_parts, axis=0)
    soffs = [0]
    for p in small_parts:
        soffs.append(soffs[-1] + p.shape[0])

    gs_rows = [dw_up_p, dw_up_a, dw_out.reshape(N_CHIPS, sh_d, D_MODEL)]
    half = MAIN_COLS // 2
    dwt_a, (*rb_rows, rb_f, sr) = _mm_tn(dproj, hn, "grad_w_in_a", lp, half // 4, D_MODEL, lp, m=half, m_off=0,
                                         swap=(gs_rows + [dwt_f, small], [1, 1, 1, 1, None]))
    *hs_rows, sc = _add_sibling_half(gs_rows, rb_rows, [1, 1, 1], small, sr, core)
    dwt_b, (rb_a,), (qs_rows, sq) = _mm_tn(dproj, hn, "grad_w_in_b", lp, half // 4, D_MODEL, lp, m=half, m_off=4,
                                           swap=([dwt_a], [1]), scatter=(hs_rows, sc))
    (rb_b,) = _sibling_exchange([dwt_b], [1])
    h_a, h_b, h_f = _add_halves_2d([dwt_a, dwt_b, dwt_f], [rb_a, rb_b, rb_f], core)
    h_in = jnp.concatenate([h_a, h_b[:ATTN_WIDTH], h_f[:N_HEADS], h_b[ATTN_WIDTH:]], axis=0)
    grad_axes = [2, 1, 1, 1]
    (grad_x, g_block0, dg1), (q_in,), _ = _input_bwd(dproj, df, wt_e, wt_qkv, wt_f, x2, metapad, dh1, norm_g,
                                                     [h_in.reshape(N_CHIPS, -1, D_MODEL // 2)])
    g_w_in_rows, g_w_up_p, g_w_up_a, g_w_out, st = _reduce_allgather([q_in] + list(qs_rows), grad_axes, sq,
                                                                       [True, False, False, False])
    late = _small_allreduce(jnp.concatenate([pad8(dg1.reshape(-1, LANES)), g_block0[PAD_ROWS:].reshape(-1, LANES)], axis=0))
    n_norm = D_MODEL // LANES
    g_norm = late[:n_norm].reshape(1, D_MODEL)
    chip = 2 * lax.axis_index("x") + lax.axis_index("y")
    g_meta = lax.dynamic_slice_in_dim(late[n_norm:].reshape(N_META, D_MODEL), chip * sh_d, sh_d, axis=1)

    spiece = lambda k, rows: st[soffs[k]:soffs[k] + rows]
    g_final = spiece(0, D_MODEL // LANES).reshape(1, D_MODEL)
    g_scale = spiece(1, POOL_WIDTH // LANES).reshape(1, POOL_WIDTH)
    g_bf = spiece(2, 1)
    g_pw = spiece(3, POOL_WIDTH)
    loss = st[soffs[4], 0]

    def pad_lanes(a):
        return jnp.pad(a, ((0, 0), (0, F_COLS - N_HEADS)))

    w_in_res = (g_w_in_rows,) + _adamw_native(to_rows(w_in), g_w_in_rows, to_rows(m_w_in), to_rows(v_w_in), "adamw_w_in")

    upd = [
        ("meta_tokens", meta_tokens, g_meta, m_meta_tokens, v_meta_tokens),
        ("norm_g", norm_g, g_norm, m_norm_g, v_norm_g),
        ("w_in", None, None, None, None),
        ("b_forget", pad_lanes(b_forget), g_bf, pad_lanes(m_b_forget), pad_lanes(v_b_forget)),
        ("pool_w", pool_w.reshape(-1, LANES), g_pw, m_pool_w.reshape(-1, LANES), v_pool_w.reshape(-1, LANES)),
        ("pool_scale", pool_scale, g_scale, m_pool_scale, v_pool_scale),
        ("w_up_pool", w_up_pool[0], g_w_up_p, m_w_up_pool[0], v_w_up_pool[0]),
        ("w_up_attn", w_up_attn[0], g_w_up_a, m_w_up_attn[0], v_w_up_attn[0]),
        ("w_out", w_out[0], g_w_out, m_w_out[0], v_w_out[0]),
        ("final_norm_g", gf, g_final, m_final_norm_g.reshape(1, D_MODEL), v_final_norm_g.reshape(1, D_MODEL)),
    ]
    shapes = [meta_tokens.shape, norm_g.shape, w_in.shape, b_forget.shape, pool_w.shape, pool_scale.shape,
              w_up_pool.shape, w_up_attn.shape, w_out.shape, final_norm_g.shape]
    grads, deltas, new_ms, new_vs = [], [], [], []
    for (name, w_, g_, m_in, v_in), shp in zip(upd, shapes):
        if name == "w_in":
            res = tuple(from_rows(a) for a in w_in_res)
        else:
            d_, mn_, vn_ = _adamw(w_, g_, m_in, v_in, "adamw_" + name)
            res = (g_, d_, mn_, vn_)
        if name == "b_forget":
            res = tuple(a[:, :N_HEADS] for a in res)
        for lst, a in zip((grads, deltas, new_ms, new_vs), res):
            lst.append(a.reshape(shp))

    return (loss, grad_x.reshape(x.shape), *grads, *deltas, *new_ms, *new_vs)
```

```python
import jax
import jax.numpy as jnp
from jax import lax
from jax.experimental import pallas as pl
from jax.experimental.pallas import tpu as pltpu

F32 = jnp.float32
BF16 = jnp.bfloat16
MESH = pl.DeviceIdType.MESH
HIGHEST = lax.Precision.HIGHEST
HBM = pl.BlockSpec(memory_space=pltpu.HBM)

D_MODEL = 1024
N_META = 16
POOL_WIDTH = 512
POOL_GROUP = 128
POOL_WINDOWS = (2, 4, 8, 16)
N_HEADS = 8
HEAD_DIM = 64
ATTN_WIDTH = 512
RMS_EPS = 1e-6
N_CHIPS = 4

ADAM_LR = 0.001
ADAM_B1 = 0.9
ADAM_B2 = 0.999
ADAM_EPS = 1e-08
ADAM_WD = 0.01
ADAM_STEP = 10

LANES = 128
ROW_TILE = 256
KV_TILE = 1024
KV_TILE_BWD = 1024
PAD_ROWS = ROW_TILE - N_META
NEG = -1e30

E_U, E_ZP, E_ZA, E_GP, E_GA, E_COLS = 0, 512, 1024, 1536, 2560, 3584
QKV_COLS = 3 * ATTN_WIDTH
MAIN_COLS = E_COLS + QKV_COLS
F_COLS = LANES
REF_U, REF_Q, REF_ZA, REF_F, REF_GP = 0, 1024, 2560, 3072, 3080
VMEM_LIMIT = 48 * 1024 * 1024


def _params(sem=None, vmem=VMEM_LIMIT):
    return pltpu.CompilerParams(dimension_semantics=sem, vmem_limit_bytes=vmem)


def _in_hbm(*arrays):
    return [pltpu.with_memory_space_constraint(a, pltpu.HBM) for a in arrays]


def _row_tile(n, target):
    best = 16
    for t in range(16, target + 1, 16):
        if n % t == 0:
            best = t
    return best


def _sigmoid(x):
    return 1.0 / (1.0 + jnp.exp(-x))


def _half(ref, axis, which):
    n = ref.shape[axis] // 2
    idx = [slice(None)] * len(ref.shape)
    idx[axis] = pl.ds(pl.multiple_of(which * n, n), n)
    return ref.at[tuple(idx)]


def _half_shape(shape, axis):
    s = list(shape)
    s[axis] //= 2
    return tuple(s)


def _gather_start(ins, outs, axes, send, recv, fsend, frecv, local):
    x, y, c = lax.axis_index("x"), lax.axis_index("y"), lax.axis_index("c")
    me = 2 * x + y
    for t in range(len(ins)):
        pltpu.make_async_copy(ins[t], outs[t].at[me], local.at[t]).start()
        for k, chip in enumerate([(1 - x, y), (x, 1 - y), (1 - x, 1 - y)]):
            pltpu.make_async_remote_copy(
                src_ref=_half(ins[t], axes[t], c), dst_ref=_half(outs[t].at[me], axes[t], c),
                send_sem=send.at[3 * t + k], recv_sem=recv.at[3 * t + k], device_id=(*chip, c), device_id_type=MESH).start()


def _gather_finish(ins, outs, axes, send, recv, fsend, frecv, local):
    x, y, c = lax.axis_index("x"), lax.axis_index("y"), lax.axis_index("c")
    me = 2 * x + y
    sibling = (x, y, 1 - c)
    chips = [(1 - x, y), (x, 1 - y), (1 - x, 1 - y)]
    n = len(ins)

    def over_ici(t, k, chip, dst_slot):
        return pltpu.make_async_remote_copy(
            src_ref=_half(ins[t], axes[t], c), dst_ref=_half(outs[t].at[dst_slot], axes[t], c),
            send_sem=send.at[3 * t + k], recv_sem=recv.at[3 * t + k], device_id=(*chip, c), device_id_type=MESH)

    def to_sibling(t, k, slot, which):
        return pltpu.make_async_remote_copy(
            src_ref=_half(outs[t].at[slot], axes[t], which), dst_ref=_half(outs[t].at[slot], axes[t], which),
            send_sem=fsend.at[3 * t + k], recv_sem=frecv.at[3 * t + k], device_id=sibling, device_id_type=MESH)

    forwards = []
    for t in range(n):
        for k, (px, py) in enumerate(chips):
            over_ici(t, k, (px, py), 2 * px + py).wait_recv()
            fw = to_sibling(t, k, 2 * px + py, c)
            fw.start()
            forwards.append(fw)
    for t in range(n):
        for k, (px, py) in enumerate(chips):
            to_sibling(t, k, 2 * px + py, 1 - c).wait_recv()
    for t in range(n):
        for k, chip in enumerate(chips):
            over_ici(t, k, chip, me).wait_send()
    for fw in forwards:
        fw.wait_send()
    for t in range(n):
        pltpu.make_async_copy(ins[t], outs[t].at[me], local.at[t]).wait()


def _gather_sems(n):
    return [pltpu.SemaphoreType.DMA((3 * n,)), pltpu.SemaphoreType.DMA((3 * n,)), pltpu.SemaphoreType.DMA((3 * n,)),
            pltpu.SemaphoreType.DMA((3 * n,)), pltpu.SemaphoreType.DMA((n,))]


def _gathered_shapes(shards):
    return [jax.ShapeDtypeStruct((N_CHIPS,) + s.shape, s.dtype) for s in shards]


def _swap_copies(srcs, dsts, axes, send, recv):
    x, y, c = lax.axis_index("x"), lax.axis_index("y"), lax.axis_index("c")
    return [pltpu.make_async_remote_copy(
        src_ref=srcs[t] if axes[t] is None else _half(srcs[t], axes[t], 1 - c), dst_ref=dsts[t],
        send_sem=send.at[t], recv_sem=recv.at[t], device_id=(x, y, 1 - c), device_id_type=MESH) for t in range(len(srcs))]


def _swap_shapes(srcs, axes):
    return [jax.ShapeDtypeStruct(g.shape if a is None else _half_shape(g.shape, a), g.dtype) for g, a in zip(srcs, axes)]


def _sibling_exchange(gs, axes):
    n = len(gs)

    def body(*refs):
        cps = _swap_copies(refs[:n], refs[n:2 * n], axes, *refs[2 * n:])
        for cp in cps:
            cp.start()
        for cp in cps:
            cp.wait()

    return pl.pallas_call(
        body, name="grad_sibling_exchange",
        in_specs=[HBM] * n, out_specs=[HBM] * n, out_shape=_swap_shapes(gs, axes),
        scratch_shapes=[pltpu.SemaphoreType.DMA((n,)), pltpu.SemaphoreType.DMA((n,))],
    )(*_in_hbm(*gs))


def _add_halves_2d(parts, rbs, core):
    n = len(parts)
    rows, w = rbs[0].shape
    tr = 512
    last_rows = rbs[-1].shape[0]

    def body(core_ref, *refs):
        p_refs, r_refs, h_refs = refs[:n], refs[n:2 * n], refs[2 * n:]
        for t in range(n - 1):
            h_refs[t][...] = (p_refs[t][...] + r_refs[t][...]).astype(BF16)

        @pl.when(pl.program_id(0) == 0)
        def _():
            h_refs[n - 1][...] = (p_refs[n - 1][...] + r_refs[n - 1][...]).astype(BF16)

    mine = lambda r: pl.BlockSpec((r, w), lambda i, cr: (i, cr[0]))
    tile = lambda r: pl.BlockSpec((r, w), lambda i, cr: (i, 0))
    mine_small = pl.BlockSpec((last_rows, w), lambda i, cr: (0, cr[0]))
    tile_small = pl.BlockSpec((last_rows, w), lambda i, cr: (0, 0))
    return pl.pallas_call(
        body, name="grad_add_sibling_w_in",
        grid_spec=pltpu.PrefetchScalarGridSpec(
            num_scalar_prefetch=1, grid=(rows // tr,),
            in_specs=[mine(tr)] * (n - 1) + [mine_small] + [tile(tr)] * (n - 1) + [tile_small],
            out_specs=[tile(tr)] * (n - 1) + [tile_small]),
        out_shape=[jax.ShapeDtypeStruct(r.shape, BF16) for r in rbs],
        compiler_params=_params(("arbitrary",)),
    )(core, *parts, *rbs)


def _add_sibling_half(gs, rbs, axes, small, sr, core):
    n = len(gs)

    def body(core_ref, *refs):
        g_refs, rb_refs = refs[:n], refs[n:2 * n]
        s_ref, sr_ref = refs[2 * n], refs[2 * n + 1]
        h_refs, sc_ref = refs[2 * n + 2:3 * n + 2], refs[3 * n + 2]
        for t in range(n):
            h_refs[t][...] = (g_refs[t][...] + rb_refs[t][...]).astype(BF16)
        sc_ref[...] = s_ref[...] + sr_ref[...]

    def mine(rb, axis):
        blk = (None,) + rb.shape[1:]
        if axis == 2:
            return pl.BlockSpec(blk, lambda j, cr: (j, 0, cr[0]))
        return pl.BlockSpec(blk, lambda j, cr: (j, cr[0], 0))

    chunk = lambda rb: pl.BlockSpec((None,) + rb.shape[1:], lambda j, cr: (j, 0, 0))
    whole = pl.BlockSpec(small.shape, lambda j, cr: (0, 0))
    return pl.pallas_call(
        body, name="grad_add_sibling",
        grid_spec=pltpu.PrefetchScalarGridSpec(
            num_scalar_prefetch=1, grid=(N_CHIPS,),
            in_specs=[mine(rb, a) for rb, a in zip(rbs, axes)] + [chunk(rb) for rb in rbs] + [whole, whole],
            out_specs=[chunk(rb) for rb in rbs] + [whole]),
        out_shape=[jax.ShapeDtypeStruct(rb.shape, BF16) for rb in rbs] + [jax.ShapeDtypeStruct(small.shape, F32)],
        compiler_params=_params(("arbitrary",)),
    )(core, *gs, *rbs, small, sr)


def _chip_scatter_plan(h_refs, s_ref, q_refs, sq_ref, send, recv, local):
    n = len(h_refs)
    nt = n + (s_ref is not None)
    x, y, c = lax.axis_index("x"), lax.axis_index("y"), lax.axis_index("c")
    me = 2 * x + y
    chips = [(1 - x, y), (x, 1 - y), (1 - x, 1 - y)]

    def copy(t, k, chip, src_slot, dst_slot):
        src = h_refs[t].at[src_slot] if t < n else s_ref
        dst = (q_refs[t] if t < n else sq_ref).at[dst_slot]
        return pltpu.make_async_remote_copy(src_ref=src, dst_ref=dst, send_sem=send.at[3 * t + k],
                                            recv_sem=recv.at[3 * t + k], device_id=(*chip, c), device_id_type=MESH)

    own = [pltpu.make_async_copy(h_refs[t].at[me], q_refs[t].at[me], local.at[t]) for t in range(n)]
    if s_ref is not None:
        own.append(pltpu.make_async_copy(s_ref, sq_ref.at[me], local.at[n]))
    sends = [copy(t, k, (px, py), 2 * px + py, me) for t in range(nt) for k, (px, py) in enumerate(chips)]
    recvs = [copy(t, k, (px, py), me, 2 * px + py) for t in range(nt) for k, (px, py) in enumerate(chips)]
    return own, sends, recvs


def _scatter_operands(hs, sc):
    small = [] if sc is None else [sc]
    nt = len(hs) + len(small)
    shapes = [jax.ShapeDtypeStruct(h.shape, h.dtype) for h in hs] + [
        jax.ShapeDtypeStruct((N_CHIPS,) + a.shape, a.dtype) for a in small]
    sems = [pltpu.SemaphoreType.DMA((3 * nt,)), pltpu.SemaphoreType.DMA((3 * nt,)), pltpu.SemaphoreType.DMA((nt,))]
    return list(hs) + small, shapes, sems


def _small_allreduce(buf):
    def body(b_ref, o_ref, sib_buf, chip_buf, send, recv):
        x, y, c = lax.axis_index("x"), lax.axis_index("y"), lax.axis_index("c")
        me = 2 * x + y
        chips = [(1 - x, y), (x, 1 - y), (1 - x, 1 - y)]
        swap = pltpu.make_async_remote_copy(src_ref=b_ref, dst_ref=sib_buf, send_sem=send.at[0], recv_sem=recv.at[0],
                                            device_id=(x, y, 1 - c), device_id_type=MESH)
        swap.start()
        swap.wait()
        chip_buf[me] = b_ref[...] + sib_buf[...]

        def copy(k, chip, slot):
            return pltpu.make_async_remote_copy(src_ref=chip_buf.at[slot], dst_ref=chip_buf.at[slot], send_sem=send.at[1 + k],
                                                recv_sem=recv.at[1 + k], device_id=(*chip, c), device_id_type=MESH)

        sends = [copy(k, chip, me) for k, chip in enumerate(chips)]
        for cp in sends:
            cp.start()
        for k, (px, py) in enumerate(chips):
            copy(k, (px, py), 2 * px + py).wait_recv()
        for cp in sends:
            cp.wait_send()
        o_ref[...] = ((chip_buf[0] + chip_buf[1]) + chip_buf[2]) + chip_buf[3]

    vmem = pl.BlockSpec(memory_space=pltpu.VMEM)
    return pl.pallas_call(
        body, name="small_allreduce", in_specs=[vmem], out_specs=vmem,
        out_shape=jax.ShapeDtypeStruct(buf.shape, F32),
        scratch_shapes=[pltpu.VMEM(buf.shape, F32), pltpu.VMEM((N_CHIPS,) + buf.shape, F32),
                        pltpu.SemaphoreType.DMA((4,)), pltpu.SemaphoreType.DMA((4,))],
        compiler_params=_params(),
    )(buf)


def _reduce_allgather(qs, axes, sq, as_rows):
    n = len(qs)
    shard_shapes, half_axes = [], []
    for q, a, rows_form in zip(qs, axes, as_rows):
        shape = [d * 2 if i == a - 1 else d for i, d in enumerate(q.shape[1:])]
        if rows_form:
            assert a == 2
            shape = [shape[0], 1, shape[1]]
        shard_shapes.append(tuple(shape))
        half_axes.append(2 if rows_form else a - 1)

    def body(*refs):
        q_refs, sq_ref = refs[:n], refs[n]
        o_refs, st_ref = refs[n + 1:2 * n + 1], refs[2 * n + 1]
        send, recv = refs[2 * n + 2:]
        x, y, c = lax.axis_index("x"), lax.axis_index("y"), lax.axis_index("c")

        def swap(t, which):
            return pltpu.make_async_remote_copy(
                src_ref=_half(o_refs[t], half_axes[t], which), dst_ref=_half(o_refs[t], half_axes[t], which),
                send_sem=send.at[t], recv_sem=recv.at[t], device_id=(x, y, 1 - c), device_id_type=MESH)

        sent = []
        for t in range(n):
            q = q_refs[t]
            total = ((q[0].astype(F32) + q[1].astype(F32)) + q[2].astype(F32)) + q[3].astype(F32)
            if as_rows[t]:
                total = total.reshape(total.shape[0], 1, total.shape[1])
            _half(o_refs[t], half_axes[t], c)[...] = total
            cp = swap(t, c)
            cp.start()
            sent.append(cp)
        st_ref[...] = ((sq_ref[0] + sq_ref[1]) + sq_ref[2]) + sq_ref[3]
        for t in range(n):
            swap(t, 1 - c).wait_recv()
        for cp in sent:
            cp.wait_send()

    vmem = pl.BlockSpec(memory_space=pltpu.VMEM)
    return pl.pallas_call(
        body, name="grad_reduce_allgather",
        in_specs=[vmem] * (n + 1), out_specs=[vmem] * (n + 1),
        out_shape=[jax.ShapeDtypeStruct(s, F32) for s in shard_shapes] + [jax.ShapeDtypeStruct(sq.shape[1:], F32)],
        scratch_shapes=[pltpu.SemaphoreType.DMA((n,)), pltpu.SemaphoreType.DMA((n,))],
        compiler_params=_params(),
    )(*qs, sq)


def _dot_nt(a, b):
    return lax.dot_general(a, b, (((1,), (1,)), ((), ())), preferred_element_type=F32)


def _dot_tn(a, b):
    return lax.dot_general(a, b, (((0,), (0,)), ((), ())), preferred_element_type=F32)


def _mm_nt(a, bt, out_dtype, name, m, n, tm, tn, row_block=0, scale_first=None, gather=None):
    k = a.shape[1]
    shards, axes = gather if gather is not None else ([], [])
    ng = len(shards)
    grid = (m // tm, n // tn)

    def body(a_ref, b_ref, *rest):
        ins, o_ref, outs, sems = rest[:ng], rest[ng], rest[ng + 1:2 * ng + 1], rest[2 * ng + 1:]
        first = (pl.program_id(0) == 0) & (pl.program_id(1) == 0)
        last = (pl.program_id(0) == grid[0] - 1) & (pl.program_id(1) == grid[1] - 1)
        if ng:
            @pl.when(first)
            def _():
                _gather_start(ins, outs, axes, *sems)

        r = _dot_nt(a_ref[...], b_ref[...])
        if scale_first is not None:
            r = r * jnp.where(pl.program_id(1) == 0, scale_first, 1.0)
        o_ref[...] = r.astype(out_dtype)

        if ng:
            @pl.when(last)
            def _():
                _gather_finish(ins, outs, axes, *sems)

    res = pl.pallas_call(
        body, name=name, grid=grid,
        in_specs=[pl.BlockSpec((tm, k), lambda i, j: (i, 0)), pl.BlockSpec((tn, k), lambda i, j: (row_block + j, 0))]
        + [HBM] * ng,
        out_specs=[pl.BlockSpec((tm, tn), lambda i, j: (i, j))] + [HBM] * ng,
        out_shape=[jax.ShapeDtypeStruct((m, n), out_dtype)] + _gathered_shapes(shards),
        scratch_shapes=_gather_sems(ng) if ng else [],
        compiler_params=_params(("arbitrary", "arbitrary") if ng else ("parallel", "parallel")),
    )(a, bt, *_in_hbm(*shards))
    return (res[0], res[1:]) if ng else res[0]


def _mm_tn(a, b, name, k, tm, tn, tk, chunks=1, m=None, m_off=0, swap=None, scatter=None):
    m = a.shape[1] if m is None else m
    n = b.shape[1]
    cw = n // chunks
    srcs, axes = swap if swap is not None else ([], [])
    ns = len(srcs)
    hs, sc = scatter if scatter is not None else ([], None)
    sc_in, sc_shapes, sc_sems = _scatter_operands(hs, sc) if scatter is not None else ([], [], [])
    nh, nq = len(hs), len(sc_in)
    grid = (m // tm, n // tn, k // tk)

    def body(a_ref, b_ref, *rest):
        s_refs, h_refs = rest[:ns], rest[ns:ns + nq]
        o_ref = rest[ns + nq]
        d_refs, q_refs = rest[ns + nq + 1:2 * ns + nq + 1], rest[2 * ns + nq + 1:2 * ns + 2 * nq + 1]
        sems = rest[2 * ns + 2 * nq + 1:]
        swap_sems, scatter_sems = (sems[:2], sems[2:]) if ns else ((), sems)
        ids = [pl.program_id(d) for d in range(3)]

        def scatter_plan():
            small_in = h_refs[nh] if sc is not None else None
            small_out = q_refs[nh] if sc is not None else None
            return _chip_scatter_plan(h_refs[:nh], small_in, q_refs[:nh], small_out, *scatter_sems)

        if ns or nq:
            @pl.when((ids[0] == 0) & (ids[1] == 0) & (ids[2] == 0))
            def _():
                if ns:
                    for cp in _swap_copies(s_refs, d_refs, axes, *swap_sems):
                        cp.start()
                if nq:
                    own, sends, _ = scatter_plan()
                    for cp in own + sends:
                        cp.start()

        @pl.when(ids[2] == 0)
        def _():
            o_ref[...] = jnp.zeros_like(o_ref)
        r = _dot_tn(a_ref[...].astype(BF16), b_ref[...].astype(BF16))
        if chunks > 1:
            for c in range(chunks):
                o_ref[c] += r[:, c * cw:(c + 1) * cw]
        else:
            o_ref[...] += r

        if ns or nq:
            @pl.when((ids[0] == grid[0] - 1) & (ids[1] == grid[1] - 1) & (ids[2] == grid[2] - 1))
            def _():
                if ns:
                    for cp in _swap_copies(s_refs, d_refs, axes, *swap_sems):
                        cp.wait()
                if nq:
                    own, sends, recvs = scatter_plan()
                    for cp in recvs:
                        cp.wait_recv()
                    for cp in sends:
                        cp.wait_send()
                    for cp in own:
                        cp.wait()

    if chunks > 1:
        assert tn == n
        out_spec = pl.BlockSpec((chunks, tm, cw), lambda i, j, kk: (0, i, 0))
        out_shape = jax.ShapeDtypeStruct((chunks, m, cw), F32)
    else:
        out_spec = pl.BlockSpec((tm, tn), lambda i, j, kk: (i, j))
        out_shape = jax.ShapeDtypeStruct((m, n), F32)
    riders = ns + nq
    res = pl.pallas_call(
        body, name=name, grid=grid,
        in_specs=[pl.BlockSpec((tk, tm), lambda i, j, kk: (kk, m_off + i)), pl.BlockSpec((tk, tn), lambda i, j, kk: (kk, j))]
        + [HBM] * riders,
        out_specs=[out_spec] + [HBM] * riders, out_shape=[out_shape] + _swap_shapes(srcs, axes) + sc_shapes,
        scratch_shapes=([pltpu.SemaphoreType.DMA((ns,)), pltpu.SemaphoreType.DMA((ns,))] if ns else []) + sc_sems,
        compiler_params=_params(("arbitrary",) * 3 if riders else ("parallel", "parallel", "arbitrary")),
    )(a, b, *_in_hbm(*srcs, *sc_in))
    if not riders:
        return res[0]
    out = [res[0]]
    if ns:
        out.append(res[1:1 + ns])
    if nq:
        out.append((res[1 + ns:1 + ns + nh], res[1 + ns + nh] if sc is not None else None))
    return tuple(out)


def _tokens_spec():
    return pl.BlockSpec((ROW_TILE, D_MODEL), lambda i: (jnp.maximum(i - 1, 0), 0))


def _rmsnorm_fwd(x2, g1, lk, shards, axes):
    nb = x2.shape[0] // ROW_TILE + 1
    nblk = lk // ROW_TILE
    ng = len(shards)
    meta_cols = shards[-1].shape[1]

    def body(x_ref, g_ref, *rest):
        ins, hn_ref, outs = rest[:ng], rest[ng], rest[ng + 1:2 * ng + 1]
        sems, meta_buf, meta_sem = rest[2 * ng + 1:2 * ng + 6], rest[2 * ng + 6], rest[2 * ng + 7]
        s = pl.program_id(0)
        blk = (s + 1) % nblk

        @pl.when(s == 0)
        def _():
            _gather_start(ins, outs, axes, *sems)

        def normed(h):
            r = lax.rsqrt(jnp.mean(h * h, axis=-1, keepdims=True) + RMS_EPS)
            return ((h * r) * g_ref[...]).astype(BF16)

        @pl.when(s < nblk - 1)
        def _():
            hn_ref[...] = normed(jnp.where(blk >= nb, 0.0, x_ref[...]))

        @pl.when(s == nblk - 1)
        def _():
            _gather_finish(ins, outs, axes, *sems)
            fetch = pltpu.make_async_copy(outs[-1], meta_buf, meta_sem.at[0])
            fetch.start()
            fetch.wait()
            meta = jnp.concatenate([meta_buf[j] for j in range(N_CHIPS)], axis=1)
            hn_ref[...] = normed(jnp.concatenate([jnp.zeros((PAD_ROWS, D_MODEL), F32), meta], axis=0))

    res = pl.pallas_call(
        body, name="rmsnorm_fwd", grid=(nblk,),
        in_specs=[pl.BlockSpec((ROW_TILE, D_MODEL), lambda s: (jnp.clip((s + 1) % nblk - 1, 0, nb - 2), 0)),
                  pl.BlockSpec((1, D_MODEL), lambda s: (0, 0))] + [HBM] * ng,
        out_specs=[pl.BlockSpec((ROW_TILE, D_MODEL), lambda s: ((s + 1) % nblk, 0))] + [HBM] * ng,
        out_shape=[jax.ShapeDtypeStruct((lk, D_MODEL), BF16)] + _gathered_shapes(shards),
        scratch_shapes=_gather_sems(ng) + [pltpu.VMEM((N_CHIPS, N_META, meta_cols), F32), pltpu.SemaphoreType.DMA((1,))],
        compiler_params=_params(("arbitrary",)),
    )(x2, g1, *_in_hbm(*shards))
    return res[0], res[1:]


def _valid_gate_mask(i):
    row = i * ROW_TILE + lax.broadcasted_iota(jnp.int32, (ROW_TILE, F_COLS), 0)
    col = lax.broadcasted_iota(jnp.int32, (ROW_TILE, F_COLS), 1)
    return (row >= PAD_ROWS) & (col < N_HEADS)


def _gates_fwd(f, bfg):
    nb = f.shape[0] // ROW_TILE

    def body(f_ref, b_ref, c_ref, carry):
        i = pl.program_id(0)

        @pl.when(i == 0)
        def _():
            carry[...] = jnp.zeros_like(carry)

        logit = f_ref[...] + b_ref[...]
        lf = jnp.minimum(logit, 0.0) - jnp.log1p(jnp.exp(-jnp.abs(logit)))
        lf = jnp.where(_valid_gate_mask(i), lf, 0.0)
        r_i = lax.broadcasted_iota(jnp.int32, (ROW_TILE, ROW_TILE), 0)
        c_i = lax.broadcasted_iota(jnp.int32, (ROW_TILE, ROW_TILE), 1)
        tri = (c_i <= r_i).astype(F32)
        c_ref[...] = jnp.dot(tri, lf, precision=HIGHEST, preferred_element_type=F32) + carry[...]
        carry[...] = carry[...] + jnp.sum(lf, axis=0, keepdims=True)

    return pl.pallas_call(
        body, name="gates_fwd", grid=(nb,),
        in_specs=[pl.BlockSpec((ROW_TILE, F_COLS), lambda i: (i, 0)), pl.BlockSpec((1, F_COLS), lambda i: (0, 0))],
        out_specs=pl.BlockSpec((ROW_TILE, F_COLS), lambda i: (i, 0)),
        out_shape=jax.ShapeDtypeStruct(f.shape, F32),
        scratch_shapes=[pltpu.VMEM((1, F_COLS), F32)],
        compiler_params=_params(("arbitrary",)),
    )(f, bfg)


def _pool_counts(i):
    row = i * ROW_TILE + lax.broadcasted_iota(jnp.int32, (ROW_TILE, 1), 0)
    return jnp.maximum(row - PAD_ROWS, 0)


def _trailing_sums(xc, levels):
    acc = xc
    for lv in range(levels):
        acc = acc + pltpu.roll(acc, 1 << lv, 0)
    return acc


def _leading_sums(xc, levels):
    n = xc.shape[0]
    acc = xc
    for lv in range(levels):
        acc = acc + pltpu.roll(acc, n - (1 << lv), 0)
    return acc


def _pool_p(u_cur, u_prev, i):
    pos = _pool_counts(i)
    ps, invs = [], []
    for g, w in enumerate(POOL_WINDOWS):
        sl = slice(g * POOL_GROUP, (g + 1) * POOL_GROUP)
        cur = u_cur[:, sl]
        xc = jnp.concatenate([u_prev[:, sl], cur], axis=0)
        win = _trailing_sums(xc, g + 1)[ROW_TILE:, :]
        inv = 1.0 / jnp.minimum(pos + 1, w).astype(F32)
        ps.append(win * inv - cur)
        invs.append(inv)
    return ps, invs


def _pool_fwd(e, pw, scale):
    nb = e.shape[0] // ROW_TILE

    def body(uc_ref, up_ref, z_ref, pw_ref, sc_ref, y_ref):
        i = pl.program_id(0)
        u_cur = uc_ref[...].astype(F32)
        u_prev = jnp.where(i == 0, 0.0, up_ref[...].astype(F32))
        ps, _ = _pool_p(u_cur, u_prev, i)
        z = z_ref[...].astype(F32)
        gate = z * _sigmoid(z)
        for g in range(len(POOL_WINDOWS)):
            sl = slice(g * POOL_GROUP, (g + 1) * POOL_GROUP)
            yraw = jnp.dot(ps[g].astype(BF16), pw_ref[g], preferred_element_type=F32)
            y_ref[:, sl] = ((yraw * sc_ref[:, sl]) * gate[:, sl]).astype(BF16)

    blk = (ROW_TILE, POOL_WIDTH)
    return pl.pallas_call(
        body, name="pool_fwd", grid=(nb,),
        in_specs=[pl.BlockSpec(blk, lambda i: (i, 0)), pl.BlockSpec(blk, lambda i: (jnp.maximum(i - 1, 0), 0)),
                  pl.BlockSpec(blk, lambda i: (i, 1)),
                  pl.BlockSpec((len(POOL_WINDOWS), POOL_GROUP, POOL_GROUP), lambda i: (0, 0, 0)),
                  pl.BlockSpec((1, POOL_WIDTH), lambda i: (0, 0))],
        out_specs=pl.BlockSpec(blk, lambda i: (i, 0)),
        out_shape=jax.ShapeDtypeStruct((e.shape[0], POOL_WIDTH), BF16),
        compiler_params=_params(("parallel",)),
    )(e, e, e, pw, scale)


def _stack_heads(a):
    first = lax.broadcasted_iota(jnp.int32, a.shape, 1) < HEAD_DIM
    zero = jnp.zeros_like(a)
    return jnp.concatenate([jnp.where(first, a, zero), jnp.where(first, zero, a)], axis=0)


def _unstack_heads(a):
    rows = a.shape[0] // 2
    first = lax.broadcasted_iota(jnp.int32, (rows, LANES), 1) < HEAD_DIM
    return jnp.where(first, a[:rows], a[rows:])


def _causal(i, jj, stacked, kv_tile):
    r = lax.broadcasted_iota(jnp.int32, (stacked * ROW_TILE, kv_tile), 0)
    if stacked == 2:
        r = jnp.where(r >= ROW_TILE, r - ROW_TILE, r)
    kidx = jj * kv_tile + lax.broadcasted_iota(jnp.int32, (stacked * ROW_TILE, kv_tile), 1)
    return kidx <= i * ROW_TILE + r


def _stack_rows(b):
    n = b.shape[1]
    return jnp.concatenate([jnp.broadcast_to(b[0:1], (ROW_TILE, n)), jnp.broadcast_to(b[1:2], (ROW_TILE, n))], axis=0)


def _attn_fwd(qkv, bias, nb):
    lk = qkv.shape[0]
    lp = nb * ROW_TILE
    nkb = lk // KV_TILE
    n_pairs = N_HEADS // 2

    def body(q_ref, k_ref, v_ref, b_ref, o_ref, lse_ref):
        i = pl.program_id(1)
        qs = _stack_heads(q_ref[...])
        last = (i * ROW_TILE) // KV_TILE

        def block(jj, carry, masked, n_keys=KV_TILE):
            m, l, acc = carry
            rows = pl.ds(pl.multiple_of(jj * KV_TILE, KV_TILE), n_keys)
            s = _dot_nt(qs, k_ref[rows, :]) - _stack_rows(b_ref[0, 0, :, rows])
            if masked:
                s = jnp.where(_causal(i, jj, 2, KV_TILE)[:, :n_keys], s, NEG)
            m_new = jnp.maximum(m, jnp.max(s, axis=1, keepdims=True))
            alpha = jnp.exp(m - m_new)
            p = jnp.exp(s - m_new)
            l = alpha * l + jnp.sum(p, axis=1, keepdims=True)
            acc = alpha * acc + jnp.dot(p.astype(BF16), v_ref[rows, :], preferred_element_type=F32)
            return m_new, l, acc

        init = (jnp.full((2 * ROW_TILE, 1), NEG, F32), jnp.zeros((2 * ROW_TILE, 1), F32),
                jnp.zeros((2 * ROW_TILE, LANES), F32))
        def finish(carry):
            m, l, acc = carry
            o_ref[...] = _unstack_heads(acc / l)
            lse_ref[...] = _unstack_heads(jnp.broadcast_to(m + jnp.log(l), (2 * ROW_TILE, LANES)))

        carry = lax.fori_loop(0, last, lambda jj, c: block(jj, c, False), init)
        ends = [lambda c, n=(r + 1) * ROW_TILE: finish(block(last, c, True, n)) for r in range(KV_TILE // ROW_TILE)]
        lax.switch(i - last * (KV_TILE // ROW_TILE), ends, carry)

    return pl.pallas_call(
        body, name="attn_fwd", grid=(n_pairs, nb),
        in_specs=[pl.BlockSpec((ROW_TILE, LANES), lambda hp, i: (i, hp)),
                  pl.BlockSpec((lk, LANES), lambda hp, i: (0, n_pairs + hp)),
                  pl.BlockSpec((lk, LANES), lambda hp, i: (0, 2 * n_pairs + hp)),
                  pl.BlockSpec((1, 1, 2, lk), lambda hp, i: (i, hp, 0, 0))],
        out_specs=[pl.BlockSpec((ROW_TILE, LANES), lambda hp, i: (i, hp)),
                   pl.BlockSpec((ROW_TILE, LANES), lambda hp, i: (i, hp))],
        out_shape=[jax.ShapeDtypeStruct((lp, ATTN_WIDTH), F32), jax.ShapeDtypeStruct((lp, ATTN_WIDTH), F32)],
        compiler_params=_params(("parallel", "parallel")),
    )(qkv, qkv, qkv, bias)


def _merge_fwd(y_pool, o, e, wup_p, wup_a):
    lp = o.shape[0]
    nb = lp // ROW_TILE

    def body(yp_ref, o_ref, e_ref, wp_ref, wa_ref, mg_ref, ya_ref):
        za = e_ref[:, E_ZA:E_GP].astype(F32)
        ya = (o_ref[...] * (za * _sigmoid(za))).astype(BF16)
        ya_ref[...] = ya
        a_pool = jnp.dot(yp_ref[...], wp_ref[...], preferred_element_type=F32)
        a_attn = jnp.dot(ya, wa_ref[...], preferred_element_type=F32)
        mg_ref[...] = (_sigmoid(e_ref[:, E_GP:E_GA].astype(F32)) * a_pool
                       + _sigmoid(e_ref[:, E_GA:E_COLS].astype(F32)) * a_attn).astype(BF16)

    return pl.pallas_call(
        body, name="merge_fwd", grid=(nb,),
        in_specs=[pl.BlockSpec((ROW_TILE, POOL_WIDTH), lambda i: (i, 0)),
                  pl.BlockSpec((ROW_TILE, ATTN_WIDTH), lambda i: (i, 0)),
                  pl.BlockSpec((ROW_TILE, E_COLS), lambda i: (i, 0)),
                  pl.BlockSpec((POOL_WIDTH, D_MODEL), lambda i: (0, 0)),
                  pl.BlockSpec((ATTN_WIDTH, D_MODEL), lambda i: (0, 0))],
        out_specs=[pl.BlockSpec((ROW_TILE, D_MODEL), lambda i: (i, 0)),
                   pl.BlockSpec((ROW_TILE, ATTN_WIDTH), lambda i: (i, 0))],
        out_shape=[jax.ShapeDtypeStruct((lp, D_MODEL), BF16), jax.ShapeDtypeStruct((lp, ATTN_WIDTH), BF16)],
        compiler_params=_params(("parallel",)),
    )(y_pool, o, e, wup_p, wup_a)


def _head_fwd_bwd(merged, w_out, x2, metapad, gf, target):
    lp = merged.shape[0]
    nb = lp // ROW_TILE

    def body(mg_ref, w_ref, x_ref, mp_ref, g_ref, t_ref, dh_ref, loss_ref, dg_ref):
        i = pl.program_id(0)

        @pl.when(i == 0)
        def _():
            loss_ref[...] = jnp.zeros_like(loss_ref)
            dg_ref[...] = jnp.zeros_like(dg_ref)

        h0 = jnp.where(i == 0, mp_ref[...], x_ref[...])
        h1 = h0 + jnp.dot(mg_ref[...], w_ref[...], preferred_element_type=F32)
        r = lax.rsqrt(jnp.mean(h1 * h1, axis=-1, keepdims=True) + RMS_EPS)
        xhat = h1 * r
        g = g_ref[...]
        err = jnp.where(i == 0, 0.0, xhat * g - t_ref[...])
        loss_ref[...] += 0.5 * jnp.sum(jnp.mean(err * err, axis=-1, keepdims=True))
        dy = err / D_MODEL
        dg_ref[...] += jnp.sum(dy * xhat, axis=0, keepdims=True)
        dxhat = dy * g
        dh_ref[...] = r * (dxhat - xhat * jnp.mean(dxhat * xhat, axis=-1, keepdims=True))

    return pl.pallas_call(
        body, name="head_fwd_bwd", grid=(nb,),
        in_specs=[pl.BlockSpec((ROW_TILE, D_MODEL), lambda i: (i, 0)),
                  pl.BlockSpec((D_MODEL, D_MODEL), lambda i: (0, 0)),
                  _tokens_spec(), pl.BlockSpec((ROW_TILE, D_MODEL), lambda i: (0, 0)),
                  pl.BlockSpec((1, D_MODEL), lambda i: (0, 0)), _tokens_spec()],
        out_specs=[pl.BlockSpec((ROW_TILE, D_MODEL), lambda i: (i, 0)),
                   pl.BlockSpec((1, LANES), lambda i: (0, 0)), pl.BlockSpec((1, D_MODEL), lambda i: (0, 0))],
        out_shape=[jax.ShapeDtypeStruct((lp, D_MODEL), F32), jax.ShapeDtypeStruct((1, LANES), F32),
                   jax.ShapeDtypeStruct((1, D_MODEL), F32)],
        compiler_params=_params(("arbitrary",)),
    )(merged, w_out, x2, metapad, gf, target)


def _per_head_rowsum(t):
    head = lax.broadcasted_iota(jnp.int32, t.shape, 1) // HEAD_DIM
    out = jnp.zeros_like(t)
    for h in range(N_HEADS):
        sel = head == h
        out = jnp.where(sel, jnp.sum(jnp.where(sel, t, 0.0), axis=1, keepdims=True), out)
    return out


def _merge_bwd(dh1, w_out, y_pool, y_attn, wup_p, wup_a, e, o):
    lp = o.shape[0]
    nb = lp // ROW_TILE

    def body(dh_ref, wo_ref, yp_ref, ya_ref, wp_ref, wa_ref, e_ref, o_ref,
             dap_ref, daa_ref, dg_ref, do_ref, delta_ref, dyp_ref):
        dmerged = _dot_nt(dh_ref[...].astype(BF16), wo_ref[...])
        a_pool = jnp.dot(yp_ref[...], wp_ref[...], preferred_element_type=F32)
        a_attn = jnp.dot(ya_ref[...], wa_ref[...], preferred_element_type=F32)
        sp = _sigmoid(e_ref[:, E_GP:E_GA].astype(F32))
        sa = _sigmoid(e_ref[:, E_GA:E_COLS].astype(F32))
        dap = (dmerged * sp).astype(BF16)
        daa = (dmerged * sa).astype(BF16)
        dap_ref[...] = dap
        daa_ref[...] = daa
        dg_ref[:, ATTN_WIDTH:ATTN_WIDTH + D_MODEL] = (dmerged * a_pool * (sp * (1.0 - sp))).astype(BF16)
        dg_ref[:, ATTN_WIDTH + D_MODEL:] = (dmerged * a_attn * (sa * (1.0 - sa))).astype(BF16)
        dyp_ref[...] = _dot_nt(dap, wp_ref[...])
        dya = _dot_nt(daa, wa_ref[...])
        za = e_ref[:, E_ZA:E_GP].astype(F32)
        sz = _sigmoid(za)
        o = o_ref[...]
        do = dya * (za * sz)
        do_ref[...] = do.astype(BF16)
        dg_ref[:, :ATTN_WIDTH] = (dya * o * (sz * (1.0 + za * (1.0 - sz)))).astype(BF16)
        delta_ref[...] = _per_head_rowsum(do * o)

    row = lambda w: pl.BlockSpec((ROW_TILE, w), lambda i: (i, 0))
    full = lambda a: pl.BlockSpec(a.shape, lambda i: (0, 0))
    return pl.pallas_call(
        body, name="merge_bwd", grid=(nb,),
        in_specs=[row(D_MODEL), full(w_out), row(POOL_WIDTH), row(ATTN_WIDTH), full(wup_p), full(wup_a),
                  row(E_COLS), row(ATTN_WIDTH)],
        out_specs=[row(D_MODEL), row(D_MODEL), pl.BlockSpec((ROW_TILE, MAIN_COLS // 2), lambda i: (i, 1)),
                   row(ATTN_WIDTH), row(ATTN_WIDTH), row(POOL_WIDTH)],
        out_shape=[jax.ShapeDtypeStruct((lp, D_MODEL), BF16), jax.ShapeDtypeStruct((lp, D_MODEL), BF16),
                   jax.ShapeDtypeStruct((lp, MAIN_COLS), BF16),
                   jax.ShapeDtypeStruct((lp, ATTN_WIDTH), BF16), jax.ShapeDtypeStruct((lp, ATTN_WIDTH), F32),
                   jax.ShapeDtypeStruct((lp, POOL_WIDTH), F32)],
        compiler_params=_params(("parallel",)),
    )(dh1, w_out, y_pool, y_attn, wup_p, wup_a, e, o)


def _fill_dproj(dproj_half, du, dzp, dq, dk, dv):
    lp = dproj_half.shape[0]
    nb = lp // ROW_TILE

    def body(base_ref, du_ref, dzp_ref, dq_ref, dk_ref, dv_ref, o_ref):
        parts = [du_ref[...], dzp_ref[...], (dq_ref[...] * HEAD_DIM ** -0.5).astype(BF16), dk_ref[...], dv_ref[...]]
        for t, p in enumerate(parts):
            o_ref[:, t * ATTN_WIDTH:(t + 1) * ATTN_WIDTH] = p

    blk = pl.BlockSpec((ROW_TILE, ATTN_WIDTH), lambda i: (i, 0))
    return pl.pallas_call(
        body, name="fill_dproj", grid=(nb,),
        in_specs=[pl.BlockSpec(memory_space=pl.ANY)] + [blk] * 5,
        out_specs=pl.BlockSpec((ROW_TILE, MAIN_COLS // 2), lambda i: (i, 0)),
        out_shape=jax.ShapeDtypeStruct(dproj_half.shape, BF16),
        input_output_aliases={0: 0},
        compiler_params=_params(("parallel",)),
    )(dproj_half, du, dzp, dq, dk, dv)


def _pool_bwd_local(e, dy_pool, pw, scale):
    lp = e.shape[0]
    nb = lp // ROW_TILE
    ng = len(POOL_WINDOWS)

    def body(uc_ref, up_ref, z_ref, dy_ref, pw_ref, sc_ref, dpc_ref, dz_ref, dsc_ref, dpw_ref):
        i = pl.program_id(0)

        @pl.when(i == 0)
        def _():
            dsc_ref[...] = jnp.zeros_like(dsc_ref)
            dpw_ref[...] = jnp.zeros_like(dpw_ref)

        u_cur = uc_ref[...].astype(F32)
        u_prev = jnp.where(i == 0, 0.0, up_ref[...].astype(F32))
        ps, invs = _pool_p(u_cur, u_prev, i)
        z = z_ref[...].astype(F32)
        sz = _sigmoid(z)
        dy = dy_ref[...]
        dypre = dy * (z * sz)
        dsilu = sz * (1.0 + z * (1.0 - sz))
        for g in range(ng):
            sl = slice(g * POOL_GROUP, (g + 1) * POOL_GROUP)
            pb = ps[g].astype(BF16)
            w = pw_ref[g]
            yraw = jnp.dot(pb, w, preferred_element_type=F32)
            sc = sc_ref[:, sl]
            dz_ref[:, sl] = (dy[:, sl] * (yraw * sc) * dsilu[:, sl]).astype(BF16)
            dsc_ref[:, sl] += jnp.sum(dypre[:, sl] * yraw, axis=0, keepdims=True)
            dyraw = (dypre[:, sl] * sc).astype(BF16)
            dpw_ref[g] += _dot_tn(pb, dyraw)
            dpc_ref[:, sl] = _dot_nt(dyraw, w) * invs[g]

    blk = (ROW_TILE, POOL_WIDTH)
    return pl.pallas_call(
        body, name="pool_bwd_local", grid=(nb,),
        in_specs=[pl.BlockSpec(blk, lambda i: (i, 0)), pl.BlockSpec(blk, lambda i: (jnp.maximum(i - 1, 0), 0)),
                  pl.BlockSpec(blk, lambda i: (i, 1)), pl.BlockSpec(blk, lambda i: (i, 0)),
                  pl.BlockSpec((ng, POOL_GROUP, POOL_GROUP), lambda i: (0, 0, 0)),
                  pl.BlockSpec((1, POOL_WIDTH), lambda i: (0, 0))],
        out_specs=[pl.BlockSpec(blk, lambda i: (i, 0)), pl.BlockSpec(blk, lambda i: (i, 0)),
                   pl.BlockSpec((1, POOL_WIDTH), lambda i: (0, 0)),
                   pl.BlockSpec((ng, POOL_GROUP, POOL_GROUP), lambda i: (0, 0, 0))],
        out_shape=[jax.ShapeDtypeStruct((lp, POOL_WIDTH), F32), jax.ShapeDtypeStruct((lp, POOL_WIDTH), BF16),
                   jax.ShapeDtypeStruct((1, POOL_WIDTH), F32),
                   jax.ShapeDtypeStruct((ng, POOL_GROUP, POOL_GROUP), F32)],
        compiler_params=_params(("arbitrary",)),
    )(e, e, e, dy_pool, pw, scale)


def _pool_bwd_window(dpc):
    lp = dpc.shape[0]
    nb = lp // ROW_TILE

    def body(cur_ref, nxt_ref, du_ref):
        i = pl.program_id(0)
        cur = cur_ref[...]
        nxt = jnp.where(i == nb - 1, 0.0, nxt_ref[...])
        pos = _pool_counts(i)
        for g, w in enumerate(POOL_WINDOWS):
            sl = slice(g * POOL_GROUP, (g + 1) * POOL_GROUP)
            xc = jnp.concatenate([cur[:, sl], nxt[:, sl]], axis=0)
            win = _leading_sums(xc, g + 1)[:ROW_TILE, :]
            dp = cur[:, sl] * jnp.minimum(pos + 1, w).astype(F32)
            du_ref[:, sl] = (win - dp).astype(BF16)

    blk = (ROW_TILE, POOL_WIDTH)
    return pl.pallas_call(
        body, name="pool_bwd_window", grid=(nb,),
        in_specs=[pl.BlockSpec(blk, lambda i: (i, 0)), pl.BlockSpec(blk, lambda i: (jnp.minimum(i + 1, nb - 1), 0))],
        out_specs=pl.BlockSpec(blk, lambda i: (i, 0)),
        out_shape=jax.ShapeDtypeStruct((lp, POOL_WIDTH), BF16),
        compiler_params=_params(("parallel",)),
    )(dpc, dpc)


def _attn_bwd(qkv, do, lse, delta, bias, nb):
    lk = qkv.shape[0]
    lp = nb * ROW_TILE
    nkb = lk // KV_TILE_BWD
    n_pairs = N_HEADS // 2
    per_kv = KV_TILE_BWD // ROW_TILE

    def body(q_ref, k_ref, v_ref, do_ref, lse_ref, dl_ref, b_ref, dq_ref, dk_ref, dv_ref, dc_ref, dcq_ref,
             dk_acc, dv_acc, dc_acc):
        jj = pl.program_id(1)

        @pl.when(jj == 0)
        def _():
            dq_ref[...] = jnp.zeros_like(dq_ref)
            dcq_ref[...] = jnp.zeros_like(dcq_ref)

        dk_acc[...] = jnp.zeros_like(dk_acc)
        dv_acc[...] = jnp.zeros_like(dv_acc)
        dc_acc[...] = jnp.zeros_like(dc_acc)

        def block(i, n_keys, masked):
            kb, vb = k_ref[:n_keys, :], v_ref[:n_keys, :]
            rows = pl.ds(pl.multiple_of(i * ROW_TILE, ROW_TILE), ROW_TILE)
            qs = _stack_heads(q_ref[rows, :])
            dos = _stack_heads(do_ref[rows, :])
            lse_i, dl_i = lse_ref[rows, :], dl_ref[rows, :]
            s = _dot_nt(qs, kb)
            dp = _dot_nt(dos, vb)
            if masked:
                valid = _causal(i, jj, 1, KV_TILE_BWD)[:, :n_keys]
            ps, dss, dcs, rowsums = [], [], [], []
            for hd in range(2):
                half = slice(hd * ROW_TILE, (hd + 1) * ROW_TILE)
                col = slice(hd * HEAD_DIM, hd * HEAD_DIM + 1)
                sh = s[half] - b_ref[i, 0, hd:hd + 1, :n_keys]
                if masked:
                    sh = jnp.where(valid, sh, NEG)
                p = jnp.exp(sh - lse_i[:, col])
                ds = p * (dp[half] - dl_i[:, col])
                ps.append(p.astype(BF16))
                dss.append(ds.astype(BF16))
                dcs.append(jnp.sum(ds, axis=0, keepdims=True))
                rowsums.append(jnp.sum(ds, axis=1, keepdims=True))
            dsb = jnp.concatenate(dss, axis=0)
            dv_acc[:n_keys, :] += _dot_tn(jnp.concatenate(ps, axis=0), dos)
            dk_acc[:n_keys, :] += _dot_tn(dsb, qs)
            dc_acc[:, :n_keys] -= jnp.concatenate(dcs, axis=0)
            dq_ref[rows, :] += _unstack_heads(jnp.dot(dsb, kb, preferred_element_type=F32))
            dcq_ref[rows, :] += _unstack_heads(jnp.broadcast_to(jnp.concatenate(rowsums, axis=0), (2 * ROW_TILE, LANES)))

        first_q = per_kv * jj
        for r in range(per_kv):
            @pl.when(first_q + r < nb)
            def _():
                block(first_q + r, (r + 1) * ROW_TILE, True)

        def rest(i, carry):
            block(i, KV_TILE_BWD, False)
            return carry

        lax.fori_loop(jnp.minimum(first_q + per_kv, nb), nb, rest, 0)
        dk_ref[...] = dk_acc[...].astype(BF16)
        dv_ref[...] = dv_acc[...].astype(BF16)
        dc_ref[0, 0] = dc_acc[...]

    whole = lambda rows: pl.BlockSpec((rows, LANES), lambda hp, jj: (0, hp))
    kv_blk = lambda off: pl.BlockSpec((KV_TILE_BWD, LANES), lambda hp, jj: (jj, off + hp))
    return pl.pallas_call(
        body, name="attn_bwd", grid=(n_pairs, nkb),
        in_specs=[whole(lk), kv_blk(n_pairs), kv_blk(2 * n_pairs), whole(lp), whole(lp), whole(lp),
                  pl.BlockSpec((nb, 1, 2, KV_TILE_BWD), lambda hp, jj: (0, hp, 0, jj))],
        out_specs=[whole(lp), kv_blk(0), kv_blk(0),
                   pl.BlockSpec((1, 1, 2, KV_TILE_BWD), lambda hp, jj: (hp, jj, 0, 0)), whole(lp)],
        out_shape=[jax.ShapeDtypeStruct((lp, ATTN_WIDTH), F32), jax.ShapeDtypeStruct((lk, ATTN_WIDTH), BF16),
                   jax.ShapeDtypeStruct((lk, ATTN_WIDTH), BF16),
                   jax.ShapeDtypeStruct((n_pairs, nkb, 2, KV_TILE_BWD), F32),
                   jax.ShapeDtypeStruct((lp, ATTN_WIDTH), F32)],
        scratch_shapes=[pltpu.VMEM((KV_TILE_BWD, LANES), F32), pltpu.VMEM((KV_TILE_BWD, LANES), F32),
                        pltpu.VMEM((2, KV_TILE_BWD), F32)],
        compiler_params=_params(("parallel", "arbitrary")),
    )(qkv, qkv, qkv, do, lse, delta, bias)


def _gates_bwd(dc, dcq, f, bfg):
    nb = f.shape[0] // ROW_TILE

    def body(dc_ref, dcq_ref, f_ref, b_ref, df_ref, db_ref, carry):
        step = pl.program_id(0)
        i = nb - 1 - step

        @pl.when(step == 0)
        def _():
            carry[...] = jnp.zeros_like(carry)
            db_ref[...] = jnp.zeros_like(db_ref)

        dcb = dc_ref[...]
        lane = lax.broadcasted_iota(jnp.int32, (ROW_TILE, F_COLS), 1)
        for h in range(N_HEADS):
            dcb = dcb + jnp.where(lane == h, dcq_ref[:, HEAD_DIM * h:HEAD_DIM * h + 1], 0.0)
        r_i = lax.broadcasted_iota(jnp.int32, (ROW_TILE, ROW_TILE), 0)
        c_i = lax.broadcasted_iota(jnp.int32, (ROW_TILE, ROW_TILE), 1)
        upper = (c_i >= r_i).astype(F32)
        dlf = jnp.dot(upper, dcb, precision=HIGHEST, preferred_element_type=F32) + carry[...]
        carry[...] = carry[...] + jnp.sum(dcb, axis=0, keepdims=True)
        logit = f_ref[...] + b_ref[...]
        dlogit = jnp.where(_valid_gate_mask(i), dlf * _sigmoid(-logit), 0.0)
        df_ref[...] = dlogit.astype(BF16)
        db_ref[...] += jnp.sum(dlogit, axis=0, keepdims=True)

    blk = pl.BlockSpec((ROW_TILE, F_COLS), lambda s: (nb - 1 - s, 0))
    wide = pl.BlockSpec((ROW_TILE, ATTN_WIDTH), lambda s: (nb - 1 - s, 0))
    one = pl.BlockSpec((1, F_COLS), lambda s: (0, 0))
    return pl.pallas_call(
        body, name="gates_bwd", grid=(nb,),
        in_specs=[blk, wide, blk, one], out_specs=[blk, one],
        out_shape=[jax.ShapeDtypeStruct(f.shape, BF16), jax.ShapeDtypeStruct((1, F_COLS), F32)],
        scratch_shapes=[pltpu.VMEM((1, F_COLS), F32)],
        compiler_params=_params(("arbitrary",)),
    )(dc, dcq, f, bfg)


def _input_bwd(dproj, df, wt_e, wt_qkv, wt_f, x2, metapad, dh1, g1, hs, sc=None):
    nb = x2.shape[0] // ROW_TILE + 1
    n = len(hs)
    sc_in, sc_shapes, sc_sems = _scatter_operands(hs, sc)
    nq = len(sc_in)

    def body(dp_ref, df_ref, we_ref, w_ref, wf_ref, x_ref, mp_ref, dh_ref, g_ref, *rest):
        h_refs, s_ref = rest[:n], (rest[n] if sc is not None else None)
        gx_ref, g0_ref, dg_ref = rest[nq:nq + 3]
        q_refs, sq_ref = rest[nq + 3:nq + 3 + n], (rest[nq + 3 + n] if sc is not None else None)
        sems = rest[2 * nq + 3:]
        i = pl.program_id(0)

        @pl.when(i == 0)
        def _():
            dg_ref[...] = jnp.zeros_like(dg_ref)
            own, sends, _ = _chip_scatter_plan(h_refs, s_ref, q_refs, sq_ref, *sems)
            for cp in own + sends:
                cp.start()

        dhn = (jnp.dot(dp_ref[:, :E_ZA], we_ref[:E_ZA, :], preferred_element_type=F32)
               + jnp.dot(dp_ref[:, E_ZA:MAIN_COLS // 2], w_ref[...], preferred_element_type=F32)
               + jnp.dot(dp_ref[:, MAIN_COLS // 2:], we_ref[E_ZA:, :], preferred_element_type=F32)
               + jnp.dot(df_ref[...], wf_ref[...], preferred_element_type=F32))
        h0 = jnp.where(i == 0, mp_ref[...], x_ref[...])
        r = lax.rsqrt(jnp.mean(h0 * h0, axis=-1, keepdims=True) + RMS_EPS)
        xhat = h0 * r
        dg_ref[...] += jnp.sum(dhn * xhat, axis=0, keepdims=True)
        dxhat = dhn * g_ref[...]
        dh0 = dh_ref[...] + r * (dxhat - xhat * jnp.mean(dxhat * xhat, axis=-1, keepdims=True))
        gx_ref[...] = dh0

        @pl.when(i == 0)
        def _():
            g0_ref[...] = dh0

        @pl.when(i == nb - 1)
        def _():
            own, sends, recvs = _chip_scatter_plan(h_refs, s_ref, q_refs, sq_ref, *sems)
            for cp in recvs:
                cp.wait_recv()
            for cp in sends:
                cp.wait_send()
            for cp in own:
                cp.wait()

    const = lambda shape: pl.BlockSpec(shape, lambda i: (0, 0))
    res = pl.pallas_call(
        body, name="input_bwd", grid=(nb,),
        in_specs=[pl.BlockSpec((ROW_TILE, MAIN_COLS), lambda i: (i, 0)), pl.BlockSpec((ROW_TILE, F_COLS), lambda i: (i, 0)),
                  const((E_COLS, D_MODEL)), const((QKV_COLS, D_MODEL)), const((F_COLS, D_MODEL)),
                  _tokens_spec(), const((ROW_TILE, D_MODEL)),
                  pl.BlockSpec((ROW_TILE, D_MODEL), lambda i: (i, 0)), const((1, D_MODEL))] + [HBM] * nq,
        out_specs=[_tokens_spec(), const((ROW_TILE, D_MODEL)), const((1, D_MODEL))] + [HBM] * nq,
        out_shape=[jax.ShapeDtypeStruct(x2.shape, F32), jax.ShapeDtypeStruct((ROW_TILE, D_MODEL), F32),
                   jax.ShapeDtypeStruct((1, D_MODEL), F32)] + sc_shapes,
        scratch_shapes=sc_sems,
        compiler_params=_params(("arbitrary",), vmem=56 * 1024 * 1024),
    )(dproj, df, wt_e, wt_qkv, wt_f, x2, metapad, dh1, g1, *_in_hbm(*sc_in))
    return res[:3], res[3:3 + n], (res[3 + n] if sc is not None else None)


def _adamw(w, g, m, v, name):
    rows, cols = w.shape
    if rows % 8 == 0:
        tr, tc = _row_tile8(rows), cols
    else:
        tr, tc = rows, (2 * LANES if cols % (2 * LANES) == 0 and rows > 8 else cols)

    def body(w_ref, g_ref, m_ref, v_ref, d_ref, mo_ref, vo_ref):
        g_ = g_ref[...]
        m_new = ADAM_B1 * m_ref[...] + (1.0 - ADAM_B1) * g_
        v_new = ADAM_B2 * v_ref[...] + (1.0 - ADAM_B2) * (g_ * g_)
        m_hat = m_new / (1.0 - ADAM_B1 ** ADAM_STEP)
        v_hat = v_new / (1.0 - ADAM_B2 ** ADAM_STEP)
        d_ref[...] = -ADAM_LR * (m_hat / (jnp.sqrt(v_hat) + ADAM_EPS) + ADAM_WD * w_ref[...])
        mo_ref[...] = m_new
        vo_ref[...] = v_new

    blk = pl.BlockSpec((tr, tc), lambda i, j: (i, j))
    return pl.pallas_call(
        body, name=name, grid=(rows // tr, cols // tc),
        in_specs=[blk] * 4, out_specs=[blk] * 3,
        out_shape=[jax.ShapeDtypeStruct(w.shape, F32)] * 3,
        compiler_params=_params(("parallel", "parallel")),
    )(w, g, m, v)


def _adamw_native(w3, g3, m3, v3, name):
    rows = w3.shape[0]
    tr = rows // 2
    blk = pl.BlockSpec((tr,) + w3.shape[1:], lambda i: (i, 0, 0))
    shape = jax.ShapeDtypeStruct(w3.shape, F32)

    def moments(g_ref, m_ref, v_ref, mo_ref, vo_ref):
        g_ = g_ref[...]
        mo_ref[...] = ADAM_B1 * m_ref[...] + (1.0 - ADAM_B1) * g_
        vo_ref[...] = ADAM_B2 * v_ref[...] + (1.0 - ADAM_B2) * (g_ * g_)

    new_m, new_v = pl.pallas_call(
        moments, name=name + "_moments", grid=(2,), in_specs=[blk] * 3, out_specs=[blk] * 2, out_shape=[shape] * 2,
        compiler_params=_params(("parallel",)),
    )(g3, m3, v3)

    def delta(w_ref, m_ref, v_ref, d_ref):
        m_hat = m_ref[...] / (1.0 - ADAM_B1 ** ADAM_STEP)
        v_hat = v_ref[...] / (1.0 - ADAM_B2 ** ADAM_STEP)
        d_ref[...] = -ADAM_LR * (m_hat / (jnp.sqrt(v_hat) + ADAM_EPS) + ADAM_WD * w_ref[...])

    d = pl.pallas_call(
        delta, name=name + "_delta", grid=(2,), in_specs=[blk] * 3, out_specs=blk, out_shape=shape,
        compiler_params=_params(("parallel",)),
    )(w3, new_m, new_v)
    return d, new_m, new_v


def _row_tile8(rows):
    best = rows
    for t in range(8, 257, 8):
        if rows % t == 0:
            best = t
    return best


def kernel(x, meta_tokens, norm_g, w_in, b_forget, pool_w, pool_scale, w_up_pool, w_up_attn, w_out, final_norm_g, loss_target, m_meta_tokens, m_norm_g, m_w_in, m_b_forget, m_pool_w, m_pool_scale, m_w_up_pool, m_w_up_attn, m_w_out, m_final_norm_g, v_meta_tokens, v_norm_g, v_w_in, v_b_forget, v_pool_w, v_pool_scale, v_w_up_pool, v_w_up_attn, v_w_out, v_final_norm_g):
    seq = x.shape[1]
    assert seq % ROW_TILE == 0 and x.shape[0] == 1
    lp = seq + ROW_TILE
    nb = lp // ROW_TILE
    lk = -(-lp // KV_TILE_BWD) * KV_TILE_BWD
    core = jnp.reshape(lax.axis_index("c"), (1,)).astype(jnp.int32)
    x2 = x[0]
    target = loss_target[0]
    sh_d = D_MODEL // N_CHIPS

    to_rows = lambda a: jnp.transpose(a, (2, 0, 1))
    from_rows = lambda a: jnp.transpose(a, (1, 2, 0))
    gf = final_norm_g.reshape(1, D_MODEL)
    bfg = jnp.pad(b_forget, ((0, 0), (0, F_COLS - N_HEADS)))
    pw_b = pool_w[0].astype(BF16)
    hn, (wg_in, meta_g) = _rmsnorm_fwd(x2, norm_g, lk, [jnp.transpose(w_in[0]).astype(BF16), meta_tokens], [1, 0])
    wt = wg_in.reshape(-1, D_MODEL)
    wt_e = jnp.concatenate([wt[REF_U:REF_Q], wt[REF_ZA:REF_F], wt[REF_GP:]], axis=0)
    wt_qkv = wt[REF_Q:REF_ZA]
    wt_f = jnp.pad(wt[REF_F:REF_GP], ((0, F_COLS - N_HEADS), (0, 0)))
    meta_full = jnp.transpose(meta_g, (1, 0, 2)).reshape(N_META, D_MODEL)
    metapad = jnp.pad(meta_full, ((PAD_ROWS, 0), (0, 0)))

    tm = _row_tile(lp, 2200)
    e, (wg_up_p, wg_up_a, wg_out) = _mm_nt(
        hn, wt_e, BF16, "in_proj_gates", lp, E_COLS, tm, 512,
        gather=([w_up_pool[0].astype(BF16), w_up_attn[0].astype(BF16), w_out[0].astype(BF16)], [0, 0, 0]))
    wup_p = jnp.transpose(wg_up_p, (1, 0, 2)).reshape(POOL_WIDTH, D_MODEL)
    wup_a = jnp.transpose(wg_up_a, (1, 0, 2)).reshape(ATTN_WIDTH, D_MODEL)
    wout = wg_out.reshape(D_MODEL, D_MODEL)
    qkv = _mm_nt(hn, wt_qkv, BF16, "in_proj_qkv", lk, QKV_COLS, _row_tile(lk, 2600), 512, scale_first=HEAD_DIM ** -0.5)
    f = _mm_nt(hn, wt_f, F32, "in_proj_forget", lp, F_COLS, tm, F_COLS)
    c = _gates_fwd(f, bfg)
    c_t = jnp.transpose(c[:, :N_HEADS])
    c_first = jnp.transpose(c_t[:, ::ROW_TILE]).reshape(nb, N_HEADS // 2, 2, 1)
    c_keys = jnp.pad(c_t, ((0, 0), (0, lk - lp))).reshape(1, N_HEADS // 2, 2, lk)
    bias = jnp.where(jnp.arange(lk) < PAD_ROWS, -NEG, c_keys - c_first)
    y_pool = _pool_fwd(e, pw_b, pool_scale)
    o, lse = _attn_fwd(qkv, bias, nb)
    merged, y_attn = _merge_fwd(y_pool, o, e, wup_p, wup_a)
    dh1, loss_part, dgf = _head_fwd_bwd(merged, wout, x2, metapad, gf, target)

    dap, daa, dproj_half, do, delta, dy_pool = _merge_bwd(dh1, wout, y_pool, y_attn, wup_p, wup_a, e, o)
    dpc, dzp, dscale, dpw = _pool_bwd_local(e, dy_pool, pw_b, pool_scale)
    du = _pool_bwd_window(dpc)
    dq, dk, dv, dc4, dcq = _attn_bwd(qkv, do, lse, delta, bias, nb)
    dc = jnp.transpose(dc4, (1, 3, 0, 2)).reshape(lk, N_HEADS)[:lp]
    df, db = _gates_bwd(jnp.pad(dc, ((0, 0), (0, F_COLS - N_HEADS))), dcq, f, bfg)
    dproj = _fill_dproj(dproj_half, du, dzp, dq, dk, dv)
    tk = _row_tile(lp, 1100)
    dw_out = _mm_tn(merged, dh1, "grad_w_out", lp, 512, D_MODEL, tk)
    dw_up_p = _mm_tn(y_pool, dap, "grad_w_up_pool", lp, 512, D_MODEL, lp, chunks=N_CHIPS)
    dw_up_a = _mm_tn(y_attn, daa, "grad_w_up_attn", lp, 512, D_MODEL, lp, chunks=N_CHIPS)
    dwt_f = _mm_tn(df, hn, "grad_w_in_forget", lp, F_COLS, D_MODEL, tk)

    def pad8(a):
        return jnp.pad(a, ((0, (-a.shape[0]) % 8), (0, 0)))

    small_parts = [pad8(a) for a in (dgf.reshape(-1, LANES), dscale.reshape(-1, LANES), db, dpw.reshape(-1, LANES),
                                     loss_part)]
    small = jnp.concatenate(small_parts, axis=0)
    soffs = [0]
    for p in small_parts:
        soffs.append(soffs[-1] + p.shape[0])

    gs_rows = [dw_up_p, dw_up_a, dw_out.reshape(N_CHIPS, sh_d, D_MODEL)]
    half = MAIN_COLS // 2
    dwt_a, (*rb_rows, rb_f, sr) = _mm_tn(dproj, hn, "grad_w_in_a", lp, half // 4, D_MODEL, lp, m=half, m_off=0,
                                         swap=(gs_rows + [dwt_f, small], [1, 1, 1, 1, None]))
    *hs_rows, sc = _add_sibling_half(gs_rows, rb_rows, [1, 1, 1], small, sr, core)
    dwt_b, (rb_a,), (qs_rows, sq) = _mm_tn(dproj, hn, "grad_w_in_b", lp, half // 4, D_MODEL, lp, m=half, m_off=4,
                                           swap=([dwt_a], [1]), scatter=(hs_rows, sc))
    (rb_b,) = _sibling_exchange([dwt_b], [1])
    h_a, h_b, h_f = _add_halves_2d([dwt_a, dwt_b, dwt_f], [rb_a, rb_b, rb_f], core)
    h_in = jnp.concatenate([h_a, h_b[:ATTN_WIDTH], h_f[:N_HEADS], h_b[ATTN_WIDTH:]], axis=0)
    grad_axes = [2, 1, 1, 1]
    (grad_x, g_block0, dg1), (q_in,), _ = _input_bwd(dproj, df, wt_e, wt_qkv, wt_f, x2, metapad, dh1, norm_g,
                                                     [h_in.reshape(N_CHIPS, -1, D_MODEL // 2)])
    g_w_in_rows, g_w_up_p, g_w_up_a, g_w_out, st = _reduce_allgather([q_in] + list(qs_rows), grad_axes, sq,
                                                                       [True, False, False, False])
    late = _small_allreduce(jnp.concatenate([pad8(dg1.reshape(-1, LANES)), g_block0[PAD_ROWS:].reshape(-1, LANES)], axis=0))
    n_norm = D_MODEL // LANES
    g_norm = late[:n_norm].reshape(1, D_MODEL)
    chip = 2 * lax.axis_index("x") + lax.axis_index("y")
    g_meta = lax.dynamic_slice_in_dim(late[n_norm:].reshape(N_META, D_MODEL), chip * sh_d, sh_d, axis=1)

    spiece = lambda k, rows: st[soffs[k]:soffs[k] + rows]
    g_final = spiece(0, D_MODEL // LANES).reshape(1, D_MODEL)
    g_scale = spiece(1, POOL_WIDTH // LANES).reshape(1, POOL_WIDTH)
    g_bf = spiece(2, 1)
    g_pw = spiece(3, POOL_WIDTH)
    loss = st[soffs[4], 0]

    def pad_lanes(a):
        return jnp.pad(a, ((0, 0), (0, F_COLS - N_HEADS)))

    w_in_res = (g_w_in_rows,) + _adamw_native(to_rows(w_in), g_w_in_rows, to_rows(m_w_in), to_rows(v_w_in), "adamw_w_in")

    upd = [
        ("meta_tokens", meta_tokens, g_meta, m_meta_tokens, v_meta_tokens),
        ("norm_g", norm_g, g_norm, m_norm_g, v_norm_g),
        ("w_in", None, None, None, None),
        ("b_forget", pad_lanes(b_forget), g_bf, pad_lanes(m_b_forget), pad_lanes(v_b_forget)),
        ("pool_w", pool_w.reshape(-1, LANES), g_pw, m_pool_w.reshape(-1, LANES), v_pool_w.reshape(-1, LANES)),
        ("pool_scale", pool_scale, g_scale, m_pool_scale, v_pool_scale),
        ("w_up_pool", w_up_pool[0], g_w_up_p, m_w_up_pool[0], v_w_up_pool[0]),
        ("w_up_attn", w_up_attn[0], g_w_up_a, m_w_up_attn[0], v_w_up_attn[0]),
        ("w_out", w_out[0], g_w_out, m_w_out[0], v_w_out[0]),
        ("final_norm_g", gf, g_final, m_final_norm_g.reshape(1, D_MODEL), v_final_norm_g.reshape(1, D_MODEL)),
    ]
    shapes = [meta_tokens.shape, norm_g.shape, w_in.shape, b_forget.shape, pool_w.shape, pool_scale.shape,
              w_up_pool.shape, w_up_attn.shape, w_out.shape, final_norm_g.shape]
    grads, deltas, new_ms, new_vs = [], [], [], []
    for (name, w_, g_, m_in, v_in), shp in zip(upd, shapes):
        if name == "w_in":
            res = tuple(from_rows(a) for a in w_in_res)
        else:
            d_, mn_, vn_ = _adamw(w_, g_, m_in, v_in, "adamw_" + name)
            res = (g_, d_, mn_, vn_)
        if name == "b_forget":
            res = tuple(a[:, :N_HEADS] for a in res)
        for lst, a in zip((grads, deltas, new_ms, new_vs), res):
            lst.append(a.reshape(shp))

    return (loss, grad_x.reshape(x.shape), *grads, *deltas, *new_ms, *new_vs)
```

```python
import jax
import jax.numpy as jnp
from jax import lax
from jax.experimental import pallas as pl
from jax.experimental.pallas import tpu as pltpu

F32 = jnp.float32
BF16 = jnp.bfloat16
MESH = pl.DeviceIdType.MESH
HIGHEST = lax.Precision.HIGHEST
HBM = pl.BlockSpec(memory_space=pltpu.HBM)

D_MODEL = 1024
N_META = 16
POOL_WIDTH = 512
POOL_GROUP = 128
POOL_WINDOWS = (2, 4, 8, 16)
N_HEADS = 8
HEAD_DIM = 64
ATTN_WIDTH = 512
RMS_EPS = 1e-6
N_CHIPS = 4

ADAM_LR = 0.001
ADAM_B1 = 0.9
ADAM_B2 = 0.999
ADAM_EPS = 1e-08
ADAM_WD = 0.01
ADAM_STEP = 10

LANES = 128
ROW_TILE = 256
KV_TILE = 1024
KV_TILE_BWD = 1024
PAD_ROWS = ROW_TILE - N_META
NEG = -1e30

E_U, E_ZP, E_ZA, E_GP, E_GA, E_COLS = 0, 512, 1024, 1536, 2560, 3584
QKV_COLS = 3 * ATTN_WIDTH
MAIN_COLS = E_COLS + QKV_COLS
F_COLS = LANES
REF_U, REF_Q, REF_ZA, REF_F, REF_GP = 0, 1024, 2560, 3072, 3080
VMEM_LIMIT = 48 * 1024 * 1024


def _params(sem=None, vmem=VMEM_LIMIT):
    return pltpu.CompilerParams(dimension_semantics=sem, vmem_limit_bytes=vmem)


def _in_hbm(*arrays):
    return [pltpu.with_memory_space_constraint(a, pltpu.HBM) for a in arrays]


def _row_tile(n, target):
    best = 16
    for t in range(16, target + 1, 16):
        if n % t == 0:
            best = t
    return best


def _sigmoid(x):
    return 1.0 / (1.0 + jnp.exp(-x))


def _half(ref, axis, which):
    n = ref.shape[axis] // 2
    idx = [slice(None)] * len(ref.shape)
    idx[axis] = pl.ds(pl.multiple_of(which * n, n), n)
    return ref.at[tuple(idx)]


def _half_shape(shape, axis):
    s = list(shape)
    s[axis] //= 2
    return tuple(s)


def _gather_start(ins, outs, axes, send, recv, fsend, frecv, local):
    x, y, c = lax.axis_index("x"), lax.axis_index("y"), lax.axis_index("c")
    me = 2 * x + y
    for t in range(len(ins)):
        pltpu.make_async_copy(ins[t], outs[t].at[me], local.at[t]).start()
        for k, chip in enumerate([(1 - x, y), (x, 1 - y), (1 - x, 1 - y)]):
            pltpu.make_async_remote_copy(
                src_ref=_half(ins[t], axes[t], c), dst_ref=_half(outs[t].at[me], axes[t], c),
                send_sem=send.at[3 * t + k], recv_sem=recv.at[3 * t + k], device_id=(*chip, c), device_id_type=MESH).start()


def _gather_finish(ins, outs, axes, send, recv, fsend, frecv, local):
    x, y, c = lax.axis_index("x"), lax.axis_index("y"), lax.axis_index("c")
    me = 2 * x + y
    sibling = (x, y, 1 - c)
    chips = [(1 - x, y), (x, 1 - y), (1 - x, 1 - y)]
    n = len(ins)

    def over_ici(t, k, chip, dst_slot):
        return pltpu.make_async_remote_copy(
            src_ref=_half(ins[t], axes[t], c), dst_ref=_half(outs[t].at[dst_slot], axes[t], c),
            send_sem=send.at[3 * t + k], recv_sem=recv.at[3 * t + k], device_id=(*chip, c), device_id_type=MESH)

    def to_sibling(t, k, slot, which):
        return pltpu.make_async_remote_copy(
            src_ref=_half(outs[t].at[slot], axes[t], which), dst_ref=_half(outs[t].at[slot], axes[t], which),
            send_sem=fsend.at[3 * t + k], recv_sem=frecv.at[3 * t + k], device_id=sibling, device_id_type=MESH)

    forwards = []
    for t in range(n):
        for k, (px, py) in enumerate(chips):
            over_ici(t, k, (px, py), 2 * px + py).wait_recv()
            fw = to_sibling(t, k, 2 * px + py, c)
            fw.start()
            forwards.append(fw)
    for t in range(n):
        for k, (px, py) in enumerate(chips):
            to_sibling(t, k, 2 * px + py, 1 - c).wait_recv()
    for t in range(n):
        for k, chip in enumerate(chips):
            over_ici(t, k, chip, me).wait_send()
    for fw in forwards:
        fw.wait_send()
    for t in range(n):
        pltpu.make_async_copy(ins[t], outs[t].at[me], local.at[t]).wait()


def _gather_sems(n):
    return [pltpu.SemaphoreType.DMA((3 * n,)), pltpu.SemaphoreType.DMA((3 * n,)), pltpu.SemaphoreType.DMA((3 * n,)),
            pltpu.SemaphoreType.DMA((3 * n,)), pltpu.SemaphoreType.DMA((n,))]


def _gathered_shapes(shards):
    return [jax.ShapeDtypeStruct((N_CHIPS,) + s.shape, s.dtype) for s in shards]


def _swap_copies(srcs, dsts, axes, send, recv):
    x, y, c = lax.axis_index("x"), lax.axis_index("y"), lax.axis_index("c")
    return [pltpu.make_async_remote_copy(
        src_ref=srcs[t] if axes[t] is None else _half(srcs[t], axes[t], 1 - c), dst_ref=dsts[t],
        send_sem=send.at[t], recv_sem=recv.at[t], device_id=(x, y, 1 - c), device_id_type=MESH) for t in range(len(srcs))]


def _swap_shapes(srcs, axes):
    return [jax.ShapeDtypeStruct(g.shape if a is None else _half_shape(g.shape, a), g.dtype) for g, a in zip(srcs, axes)]


def _sibling_exchange(gs, axes):
    n = len(gs)

    def body(*refs):
        cps = _swap_copies(refs[:n], refs[n:2 * n], axes, *refs[2 * n:])
        for cp in cps:
            cp.start()
        for cp in cps:
            cp.wait()

    return pl.pallas_call(
        body, name="grad_sibling_exchange",
        in_specs=[HBM] * n, out_specs=[HBM] * n, out_shape=_swap_shapes(gs, axes),
        scratch_shapes=[pltpu.SemaphoreType.DMA((n,)), pltpu.SemaphoreType.DMA((n,))],
    )(*_in_hbm(*gs))


def _add_halves_w_in(parts, rbs, core):
    na, nb_rows, w = rbs[0].shape[0], rbs[1].shape[0], rbs[0].shape[1]
    per = (na + nb_rows + N_HEADS) // N_CHIPS
    pieces = [[] for _ in range(N_CHIPS)]
    r = 0
    for src, lo, hi in ((0, 0, na), (1, 0, ATTN_WIDTH), (2, 0, N_HEADS), (1, ATTN_WIDTH, nb_rows)):
        while lo < hi:
            c, at = divmod(r, per)
            n = min(hi - lo, per - at)
            pieces[c].append((src, lo, n, at))
            lo, r = lo + n, r + n

    def body(core_ref, a_ref, b_ref, f_ref, ra_ref, rb_ref, rf_ref, o_ref, am, bm, fm, ar, br, fr, rows_buf, sems):
        cols = pl.ds(pl.multiple_of(core_ref[0] * w, w), w)
        few = pl.ds(0, N_HEADS)
        loads = [(pltpu.make_async_copy(a_ref.at[:, cols], am, sems.at[0]), pltpu.make_async_copy(ra_ref, ar, sems.at[1])),
                 (pltpu.make_async_copy(b_ref.at[:, cols], bm, sems.at[2]), pltpu.make_async_copy(rb_ref, br, sems.at[3])),
                 (pltpu.make_async_copy(f_ref.at[few, cols], fm, sems.at[4]),
                  pltpu.make_async_copy(rf_ref.at[few, :], fr, sems.at[5]))]
        for pair in loads:
            for cp in pair:
                cp.start()
        bufs = [(am, ar), (bm, br), (fm, fr)]
        there = set()
        for c in range(N_CHIPS):
            for src, lo, n, at in pieces[c]:
                if src not in there:
                    for cp in loads[src]:
                        cp.wait()
                    there.add(src)
                mine, other = bufs[src]
                rows_buf[at:at + n, :] = mine[lo:lo + n, :] + other[lo:lo + n, :]
            o_ref[c] = rows_buf[0:per, :].astype(BF16)

    any_space = pl.BlockSpec(memory_space=pl.ANY)
    return pl.pallas_call(
        body, name="grad_add_sibling_w_in",
        grid_spec=pltpu.PrefetchScalarGridSpec(
            num_scalar_prefetch=1, grid=(1,),
            in_specs=[any_space] * 6,
            out_specs=pl.BlockSpec((N_CHIPS, per, w), lambda i, cr: (0, 0, 0)),
            scratch_shapes=[pltpu.VMEM((na, w), F32), pltpu.VMEM((nb_rows, w), F32), pltpu.VMEM((N_HEADS, w), F32),
                            pltpu.VMEM((na, w), F32), pltpu.VMEM((nb_rows, w), F32), pltpu.VMEM((N_HEADS, w), F32),
                            pltpu.VMEM((per + (-per) % 8, w), F32), pltpu.SemaphoreType.DMA((6,))]),
        out_shape=jax.ShapeDtypeStruct((N_CHIPS, per, w), BF16),
        compiler_params=_params(("arbitrary",)),
    )(core, *parts, *rbs)


def _add_sibling_half(gs, rbs, axes, small, sr, core):
    n = len(gs)

    def body(core_ref, *refs):
        g_refs, rb_refs = refs[:n], refs[n:2 * n]
        s_ref, sr_ref = refs[2 * n], refs[2 * n + 1]
        h_refs, sc_ref = refs[2 * n + 2:3 * n + 2], refs[3 * n + 2]
        for t in range(n):
            h_refs[t][...] = (g_refs[t][...] + rb_refs[t][...]).astype(BF16)
        sc_ref[...] = s_ref[...] + sr_ref[...]

    def mine(rb, axis):
        blk = (None,) + rb.shape[1:]
        if axis == 2:
            return pl.BlockSpec(blk, lambda j, cr: (j, 0, cr[0]))
        return pl.BlockSpec(blk, lambda j, cr: (j, cr[0], 0))

    chunk = lambda rb: pl.BlockSpec((None,) + rb.shape[1:], lambda j, cr: (j, 0, 0))
    whole = pl.BlockSpec(small.shape, lambda j, cr: (0, 0))
    return pl.pallas_call(
        body, name="grad_add_sibling",
        grid_spec=pltpu.PrefetchScalarGridSpec(
            num_scalar_prefetch=1, grid=(N_CHIPS,),
            in_specs=[mine(rb, a) for rb, a in zip(rbs, axes)] + [chunk(rb) for rb in rbs] + [whole, whole],
            out_specs=[chunk(rb) for rb in rbs] + [whole]),
        out_shape=[jax.ShapeDtypeStruct(rb.shape, BF16) for rb in rbs] + [jax.ShapeDtypeStruct(small.shape, F32)],
        compiler_params=_params(("arbitrary",)),
    )(core, *gs, *rbs, small, sr)


def _chip_scatter_plan(h_refs, s_ref, q_refs, sq_ref, send, recv, local):
    n = len(h_refs)
    nt = n + (s_ref is not None)
    x, y, c = lax.axis_index("x"), lax.axis_index("y"), lax.axis_index("c")
    me = 2 * x + y
    chips = [(1 - x, y), (x, 1 - y), (1 - x, 1 - y)]

    def copy(t, k, chip, src_slot, dst_slot):
        src = h_refs[t].at[src_slot] if t < n else s_ref
        dst = (q_refs[t] if t < n else sq_ref).at[dst_slot]
        return pltpu.make_async_remote_copy(src_ref=src, dst_ref=dst, send_sem=send.at[3 * t + k],
                                            recv_sem=recv.at[3 * t + k], device_id=(*chip, c), device_id_type=MESH)

    own = [pltpu.make_async_copy(h_refs[t].at[me], q_refs[t].at[me], local.at[t]) for t in range(n)]
    if s_ref is not None:
        own.append(pltpu.make_async_copy(s_ref, sq_ref.at[me], local.at[n]))
    sends = [copy(t, k, (px, py), 2 * px + py, me) for t in range(nt) for k, (px, py) in enumerate(chips)]
    recvs = [copy(t, k, (px, py), me, 2 * px + py) for t in range(nt) for k, (px, py) in enumerate(chips)]
    return own, sends, recvs


def _scatter_operands(hs, sc):
    small = [] if sc is None else [sc]
    nt = len(hs) + len(small)
    shapes = [jax.ShapeDtypeStruct(h.shape, h.dtype) for h in hs] + [
        jax.ShapeDtypeStruct((N_CHIPS,) + a.shape, a.dtype) for a in small]
    sems = [pltpu.SemaphoreType.DMA((3 * nt,)), pltpu.SemaphoreType.DMA((3 * nt,)), pltpu.SemaphoreType.DMA((nt,))]
    return list(hs) + small, shapes, sems


def _small_allreduce(buf):
    def body(b_ref, o_ref, sib_buf, chip_buf, send, recv):
        x, y, c = lax.axis_index("x"), lax.axis_index("y"), lax.axis_index("c")
        me = 2 * x + y
        chips = [(1 - x, y), (x, 1 - y), (1 - x, 1 - y)]
        swap = pltpu.make_async_remote_copy(src_ref=b_ref, dst_ref=sib_buf, send_sem=send.at[0], recv_sem=recv.at[0],
                                            device_id=(x, y, 1 - c), device_id_type=MESH)
        swap.start()
        swap.wait()
        chip_buf[me] = b_ref[...] + sib_buf[...]

        def copy(k, chip, slot):
            return pltpu.make_async_remote_copy(src_ref=chip_buf.at[slot], dst_ref=chip_buf.at[slot], send_sem=send.at[1 + k],
                                                recv_sem=recv.at[1 + k], device_id=(*chip, c), device_id_type=MESH)

        sends = [copy(k, chip, me) for k, chip in enumerate(chips)]
        for cp in sends:
            cp.start()
        for k, (px, py) in enumerate(chips):
            copy(k, (px, py), 2 * px + py).wait_recv()
        for cp in sends:
            cp.wait_send()
        o_ref[...] = ((chip_buf[0] + chip_buf[1]) + chip_buf[2]) + chip_buf[3]

    vmem = pl.BlockSpec(memory_space=pltpu.VMEM)
    return pl.pallas_call(
        body, name="small_allreduce", in_specs=[vmem], out_specs=vmem,
        out_shape=jax.ShapeDtypeStruct(buf.shape, F32),
        scratch_shapes=[pltpu.VMEM(buf.shape, F32), pltpu.VMEM((N_CHIPS,) + buf.shape, F32),
                        pltpu.SemaphoreType.DMA((4,)), pltpu.SemaphoreType.DMA((4,))],
        compiler_params=_params(),
    )(buf)


def _reduce_allgather(qs, axes, sq, as_rows):
    n = len(qs)
    shard_shapes, half_axes = [], []
    for q, a, rows_form in zip(qs, axes, as_rows):
        shape = [d * 2 if i == a - 1 else d for i, d in enumerate(q.shape[1:])]
        if rows_form:
            assert a == 2
            shape = [shape[0], 1, shape[1]]
        shard_shapes.append(tuple(shape))
        half_axes.append(2 if rows_form else a - 1)

    def body(*refs):
        q_refs, sq_ref = refs[:n], refs[n]
        o_refs, st_ref = refs[n + 1:2 * n + 1], refs[2 * n + 1]
        send, recv = refs[2 * n + 2:]
        x, y, c = lax.axis_index("x"), lax.axis_index("y"), lax.axis_index("c")

        def swap(t, which):
            return pltpu.make_async_remote_copy(
                src_ref=_half(o_refs[t], half_axes[t], which), dst_ref=_half(o_refs[t], half_axes[t], which),
                send_sem=send.at[t], recv_sem=recv.at[t], device_id=(x, y, 1 - c), device_id_type=MESH)

        sent = []
        for t in range(n):
            q = q_refs[t]
            total = ((q[0].astype(F32) + q[1].astype(F32)) + q[2].astype(F32)) + q[3].astype(F32)
            if as_rows[t]:
                total = total.reshape(total.shape[0], 1, total.shape[1])
            _half(o_refs[t], half_axes[t], c)[...] = total
            cp = swap(t, c)
            cp.start()
            sent.append(cp)
        st_ref[...] = ((sq_ref[0] + sq_ref[1]) + sq_ref[2]) + sq_ref[3]
        for t in range(n):
            swap(t, 1 - c).wait_recv()
        for cp in sent:
            cp.wait_send()

    vmem = pl.BlockSpec(memory_space=pltpu.VMEM)
    return pl.pallas_call(
        body, name="grad_reduce_allgather",
        in_specs=[vmem] * (n + 1), out_specs=[vmem] * (n + 1),
        out_shape=[jax.ShapeDtypeStruct(s, F32) for s in shard_shapes] + [jax.ShapeDtypeStruct(sq.shape[1:], F32)],
        scratch_shapes=[pltpu.SemaphoreType.DMA((n,)), pltpu.SemaphoreType.DMA((n,))],
        compiler_params=_params(),
    )(*qs, sq)


def _dot_nt(a, b):
    return lax.dot_general(a, b, (((1,), (1,)), ((), ())), preferred_element_type=F32)


def _dot_tn(a, b):
    return lax.dot_general(a, b, (((0,), (0,)), ((), ())), preferred_element_type=F32)


def _mm_nt(a, bt, out_dtype, name, m, n, tm, tn, row_block=0, scale_first=None, gather=None):
    k = a.shape[1]
    shards, axes = gather if gather is not None else ([], [])
    ng = len(shards)
    grid = (m // tm, n // tn)

    def body(a_ref, b_ref, *rest):
        ins, o_ref, outs, sems = rest[:ng], rest[ng], rest[ng + 1:2 * ng + 1], rest[2 * ng + 1:]
        first = (pl.program_id(0) == 0) & (pl.program_id(1) == 0)
        last = (pl.program_id(0) == grid[0] - 1) & (pl.program_id(1) == grid[1] - 1)
        if ng:
            @pl.when(first)
            def _():
                _gather_start(ins, outs, axes, *sems)

        r = _dot_nt(a_ref[...], b_ref[...])
        if scale_first is not None:
            r = r * jnp.where(pl.program_id(1) == 0, scale_first, 1.0)
        o_ref[...] = r.astype(out_dtype)

        if ng:
            @pl.when(last)
            def _():
                _gather_finish(ins, outs, axes, *sems)

    res = pl.pallas_call(
        body, name=name, grid=grid,
        in_specs=[pl.BlockSpec((tm, k), lambda i, j: (i, 0)), pl.BlockSpec((tn, k), lambda i, j: (row_block + j, 0))]
        + [HBM] * ng,
        out_specs=[pl.BlockSpec((tm, tn), lambda i, j: (i, j))] + [HBM] * ng,
        out_shape=[jax.ShapeDtypeStruct((m, n), out_dtype)] + _gathered_shapes(shards),
        scratch_shapes=_gather_sems(ng) if ng else [],
        compiler_params=_params(("arbitrary", "arbitrary") if ng else ("parallel", "parallel")),
    )(a, bt, *_in_hbm(*shards))
    return (res[0], res[1:]) if ng else res[0]


def _mm_tn(a, b, name, k, tm, tn, tk, chunks=1, m=None, m_off=0, swap=None, scatter=None):
    m = a.shape[1] if m is None else m
    n = b.shape[1]
    cw = n // chunks
    srcs, axes = swap if swap is not None else ([], [])
    ns = len(srcs)
    hs, sc = scatter if scatter is not None else ([], None)
    sc_in, sc_shapes, sc_sems = _scatter_operands(hs, sc) if scatter is not None else ([], [], [])
    nh, nq = len(hs), len(sc_in)
    grid = (m // tm, n // tn, k // tk)

    def body(a_ref, b_ref, *rest):
        s_refs, h_refs = rest[:ns], rest[ns:ns + nq]
        o_ref = rest[ns + nq]
        d_refs, q_refs = rest[ns + nq + 1:2 * ns + nq + 1], rest[2 * ns + nq + 1:2 * ns + 2 * nq + 1]
        sems = rest[2 * ns + 2 * nq + 1:]
        swap_sems, scatter_sems = (sems[:2], sems[2:]) if ns else ((), sems)
        ids = [pl.program_id(d) for d in range(3)]

        def scatter_plan():
            small_in = h_refs[nh] if sc is not None else None
            small_out = q_refs[nh] if sc is not None else None
            return _chip_scatter_plan(h_refs[:nh], small_in, q_refs[:nh], small_out, *scatter_sems)

        if ns or nq:
            @pl.when((ids[0] == 0) & (ids[1] == 0) & (ids[2] == 0))
            def _():
                if ns:
                    for cp in _swap_copies(s_refs, d_refs, axes, *swap_sems):
                        cp.start()
                if nq:
                    own, sends, _ = scatter_plan()
                    for cp in own + sends:
                        cp.start()

        @pl.when(ids[2] == 0)
        def _():
            o_ref[...] = jnp.zeros_like(o_ref)
        r = _dot_tn(a_ref[...].astype(BF16), b_ref[...].astype(BF16))
        if chunks > 1:
            for c in range(chunks):
                o_ref[c] += r[:, c * cw:(c + 1) * cw]
        else:
            o_ref[...] += r

        if ns or nq:
            @pl.when((ids[0] == grid[0] - 1) & (ids[1] == grid[1] - 1) & (ids[2] == grid[2] - 1))
            def _():
                if ns:
                    for cp in _swap_copies(s_refs, d_refs, axes, *swap_sems):
                        cp.wait()
                if nq:
                    own, sends, recvs = scatter_plan()
                    for cp in recvs:
                        cp.wait_recv()
                    for cp in sends:
                        cp.wait_send()
                    for cp in own:
                        cp.wait()

    if chunks > 1:
        assert tn == n
        out_spec = pl.BlockSpec((chunks, tm, cw), lambda i, j, kk: (0, i, 0))
        out_shape = jax.ShapeDtypeStruct((chunks, m, cw), F32)
    else:
        out_spec = pl.BlockSpec((tm, tn), lambda i, j, kk: (i, j))
        out_shape = jax.ShapeDtypeStruct((m, n), F32)
    riders = ns + nq
    res = pl.pallas_call(
        body, name=name, grid=grid,
        in_specs=[pl.BlockSpec((tk, tm), lambda i, j, kk: (kk, m_off + i)), pl.BlockSpec((tk, tn), lambda i, j, kk: (kk, j))]
        + [HBM] * riders,
        out_specs=[out_spec] + [HBM] * riders, out_shape=[out_shape] + _swap_shapes(srcs, axes) + sc_shapes,
        scratch_shapes=([pltpu.SemaphoreType.DMA((ns,)), pltpu.SemaphoreType.DMA((ns,))] if ns else []) + sc_sems,
        compiler_params=_params(("arbitrary",) * 3 if riders else ("parallel", "parallel", "arbitrary")),
    )(a, b, *_in_hbm(*srcs, *sc_in))
    if not riders:
        return res[0]
    out = [res[0]]
    if ns:
        out.append(res[1:1 + ns])
    if nq:
        out.append((res[1 + ns:1 + ns + nh], res[1 + ns + nh] if sc is not None else None))
    return tuple(out)


def _tokens_spec():
    return pl.BlockSpec((ROW_TILE, D_MODEL), lambda i: (jnp.maximum(i - 1, 0), 0))


def _rmsnorm_fwd(x2, g1, lk, shards, axes):
    nb = x2.shape[0] // ROW_TILE + 1
    nblk = lk // ROW_TILE
    ng = len(shards)
    meta_cols = shards[-1].shape[1]

    def body(x_ref, g_ref, *rest):
        ins, hn_ref, outs = rest[:ng], rest[ng], rest[ng + 1:2 * ng + 1]
        sems, meta_buf, meta_sem = rest[2 * ng + 1:2 * ng + 6], rest[2 * ng + 6], rest[2 * ng + 7]
        s = pl.program_id(0)
        blk = (s + 1) % nblk

        @pl.when(s == 0)
        def _():
            _gather_start(ins, outs, axes, *sems)

        def normed(h):
            r = lax.rsqrt(jnp.mean(h * h, axis=-1, keepdims=True) + RMS_EPS)
            return ((h * r) * g_ref[...]).astype(BF16)

        @pl.when(s < nblk - 1)
        def _():
            hn_ref[...] = normed(jnp.where(blk >= nb, 0.0, x_ref[...]))

        @pl.when(s == nblk - 1)
        def _():
            _gather_finish(ins, outs, axes, *sems)
            fetch = pltpu.make_async_copy(outs[-1], meta_buf, meta_sem.at[0])
            fetch.start()
            fetch.wait()
            meta = jnp.concatenate([meta_buf[j] for j in range(N_CHIPS)], axis=1)
            hn_ref[...] = normed(jnp.concatenate([jnp.zeros((PAD_ROWS, D_MODEL), F32), meta], axis=0))

    res = pl.pallas_call(
        body, name="rmsnorm_fwd", grid=(nblk,),
        in_specs=[pl.BlockSpec((ROW_TILE, D_MODEL), lambda s: (jnp.clip((s + 1) % nblk - 1, 0, nb - 2), 0)),
                  pl.BlockSpec((1, D_MODEL), lambda s: (0, 0))] + [HBM] * ng,
        out_specs=[pl.BlockSpec((ROW_TILE, D_MODEL), lambda s: ((s + 1) % nblk, 0))] + [HBM] * ng,
        out_shape=[jax.ShapeDtypeStruct((lk, D_MODEL), BF16)] + _gathered_shapes(shards),
        scratch_shapes=_gather_sems(ng) + [pltpu.VMEM((N_CHIPS, N_META, meta_cols), F32), pltpu.SemaphoreType.DMA((1,))],
        compiler_params=_params(("arbitrary",)),
    )(x2, g1, *_in_hbm(*shards))
    return res[0], res[1:]


def _valid_gate_mask(i):
    row = i * ROW_TILE + lax.broadcasted_iota(jnp.int32, (ROW_TILE, F_COLS), 0)
    col = lax.broadcasted_iota(jnp.int32, (ROW_TILE, F_COLS), 1)
    return (row >= PAD_ROWS) & (col < N_HEADS)


def _gates_fwd(f, bfg):
    nb = f.shape[0] // ROW_TILE

    def body(f_ref, b_ref, c_ref, carry):
        i = pl.program_id(0)

        @pl.when(i == 0)
        def _():
            carry[...] = jnp.zeros_like(carry)

        logit = f_ref[...] + b_ref[...]
        lf = jnp.minimum(logit, 0.0) - jnp.log1p(jnp.exp(-jnp.abs(logit)))
        lf = jnp.where(_valid_gate_mask(i), lf, 0.0)
        r_i = lax.broadcasted_iota(jnp.int32, (ROW_TILE, ROW_TILE), 0)
        c_i = lax.broadcasted_iota(jnp.int32, (ROW_TILE, ROW_TILE), 1)
        tri = (c_i <= r_i).astype(F32)
        c_ref[...] = jnp.dot(tri, lf, precision=HIGHEST, preferred_element_type=F32) + carry[...]
        carry[...] = carry[...] + jnp.sum(lf, axis=0, keepdims=True)

    return pl.pallas_call(
        body, name="gates_fwd", grid=(nb,),
        in_specs=[pl.BlockSpec((ROW_TILE, F_COLS), lambda i: (i, 0)), pl.BlockSpec((1, F_COLS), lambda i: (0, 0))],
        out_specs=pl.BlockSpec((ROW_TILE, F_COLS), lambda i: (i, 0)),
        out_shape=jax.ShapeDtypeStruct(f.shape, F32),
        scratch_shapes=[pltpu.VMEM((1, F_COLS), F32)],
        compiler_params=_params(("arbitrary",)),
    )(f, bfg)


def _pool_counts(i):
    row = i * ROW_TILE + lax.broadcasted_iota(jnp.int32, (ROW_TILE, 1), 0)
    return jnp.maximum(row - PAD_ROWS, 0)


def _trailing_sums(xc, levels):
    acc = xc
    for lv in range(levels):
        acc = acc + pltpu.roll(acc, 1 << lv, 0)
    return acc


def _leading_sums(xc, levels):
    n = xc.shape[0]
    acc = xc
    for lv in range(levels):
        acc = acc + pltpu.roll(acc, n - (1 << lv), 0)
    return acc


def _pool_p(u_cur, u_prev, i):
    pos = _pool_counts(i)
    ps, invs = [], []
    for g, w in enumerate(POOL_WINDOWS):
        sl = slice(g * POOL_GROUP, (g + 1) * POOL_GROUP)
        cur = u_cur[:, sl]
        xc = jnp.concatenate([u_prev[:, sl], cur], axis=0)
        win = _trailing_sums(xc, g + 1)[ROW_TILE:, :]
        inv = 1.0 / jnp.minimum(pos + 1, w).astype(F32)
        ps.append(win * inv - cur)
        invs.append(inv)
    return ps, invs


def _pool_fwd(e, pw, scale):
    nb = e.shape[0] // ROW_TILE

    def body(uc_ref, up_ref, z_ref, pw_ref, sc_ref, y_ref):
        i = pl.program_id(0)
        u_cur = uc_ref[...].astype(F32)
        u_prev = jnp.where(i == 0, 0.0, up_ref[...].astype(F32))
        ps, _ = _pool_p(u_cur, u_prev, i)
        z = z_ref[...].astype(F32)
        gate = z * _sigmoid(z)
        for g in range(len(POOL_WINDOWS)):
            sl = slice(g * POOL_GROUP, (g + 1) * POOL_GROUP)
            yraw = jnp.dot(ps[g].astype(BF16), pw_ref[g], preferred_element_type=F32)
            y_ref[:, sl] = ((yraw * sc_ref[:, sl]) * gate[:, sl]).astype(BF16)

    blk = (ROW_TILE, POOL_WIDTH)
    return pl.pallas_call(
        body, name="pool_fwd", grid=(nb,),
        in_specs=[pl.BlockSpec(blk, lambda i: (i, 0)), pl.BlockSpec(blk, lambda i: (jnp.maximum(i - 1, 0), 0)),
                  pl.BlockSpec(blk, lambda i: (i, 1)),
                  pl.BlockSpec((len(POOL_WINDOWS), POOL_GROUP, POOL_GROUP), lambda i: (0, 0, 0)),
                  pl.BlockSpec((1, POOL_WIDTH), lambda i: (0, 0))],
        out_specs=pl.BlockSpec(blk, lambda i: (i, 0)),
        out_shape=jax.ShapeDtypeStruct((e.shape[0], POOL_WIDTH), BF16),
        compiler_params=_params(("parallel",)),
    )(e, e, e, pw, scale)


def _stack_heads(a):
    first = lax.broadcasted_iota(jnp.int32, a.shape, 1) < HEAD_DIM
    zero = jnp.zeros_like(a)
    return jnp.concatenate([jnp.where(first, a, zero), jnp.where(first, zero, a)], axis=0)


def _unstack_heads(a):
    rows = a.shape[0] // 2
    first = lax.broadcasted_iota(jnp.int32, (rows, LANES), 1) < HEAD_DIM
    return jnp.where(first, a[:rows], a[rows:])


def _causal(i, jj, stacked, kv_tile):
    r = lax.broadcasted_iota(jnp.int32, (stacked * ROW_TILE, kv_tile), 0)
    if stacked == 2:
        r = jnp.where(r >= ROW_TILE, r - ROW_TILE, r)
    kidx = jj * kv_tile + lax.broadcasted_iota(jnp.int32, (stacked * ROW_TILE, kv_tile), 1)
    return kidx <= i * ROW_TILE + r


def _stack_rows(b):
    n = b.shape[1]
    return jnp.concatenate([jnp.broadcast_to(b[0:1], (ROW_TILE, n)), jnp.broadcast_to(b[1:2], (ROW_TILE, n))], axis=0)


def _attn_fwd(qkv, bias, nb):
    lk = qkv.shape[0]
    lp = nb * ROW_TILE
    nkb = lk // KV_TILE
    n_pairs = N_HEADS // 2

    def body(q_ref, k_ref, v_ref, b_ref, o_ref, lse_ref):
        i = pl.program_id(1)
        qs = _stack_heads(q_ref[...])
        last = (i * ROW_TILE) // KV_TILE

        def block(jj, carry, masked, n_keys=KV_TILE):
            m, l, acc = carry
            rows = pl.ds(pl.multiple_of(jj * KV_TILE, KV_TILE), n_keys)
            s = _dot_nt(qs, k_ref[rows, :]) - _stack_rows(b_ref[0, 0, :, rows])
            if masked:
                s = jnp.where(_causal(i, jj, 2, KV_TILE)[:, :n_keys], s, NEG)
            m_new = jnp.maximum(m, jnp.max(s, axis=1, keepdims=True))
            alpha = jnp.exp(m - m_new)
            p = jnp.exp(s - m_new)
            l = alpha * l + jnp.sum(p, axis=1, keepdims=True)
            acc = alpha * acc + jnp.dot(p.astype(BF16), v_ref[rows, :], preferred_element_type=F32)
            return m_new, l, acc

        init = (jnp.full((2 * ROW_TILE, 1), NEG, F32), jnp.zeros((2 * ROW_TILE, 1), F32),
                jnp.zeros((2 * ROW_TILE, LANES), F32))
        def finish(carry):
            m, l, acc = carry
            o_ref[...] = _unstack_heads(acc / l)
            lse_ref[...] = _unstack_heads(jnp.broadcast_to(m + jnp.log(l), (2 * ROW_TILE, LANES)))

        carry = lax.fori_loop(0, last, lambda jj, c: block(jj, c, False), init)
        ends = [lambda c, n=(r + 1) * ROW_TILE: finish(block(last, c, True, n)) for r in range(KV_TILE // ROW_TILE)]
        lax.switch(i - last * (KV_TILE // ROW_TILE), ends, carry)

    return pl.pallas_call(
        body, name="attn_fwd", grid=(n_pairs, nb),
        in_specs=[pl.BlockSpec((ROW_TILE, LANES), lambda hp, i: (i, hp)),
                  pl.BlockSpec((lk, LANES), lambda hp, i: (0, n_pairs + hp)),
                  pl.BlockSpec((lk, LANES), lambda hp, i: (0, 2 * n_pairs + hp)),
                  pl.BlockSpec((1, 1, 2, lk), lambda hp, i: (i, hp, 0, 0))],
        out_specs=[pl.BlockSpec((ROW_TILE, LANES), lambda hp, i: (i, hp)),
                   pl.BlockSpec((ROW_TILE, LANES), lambda hp, i: (i, hp))],
        out_shape=[jax.ShapeDtypeStruct((lp, ATTN_WIDTH), F32), jax.ShapeDtypeStruct((lp, ATTN_WIDTH), F32)],
        compiler_params=_params(("parallel", "parallel")),
    )(qkv, qkv, qkv, bias)


def _merge_fwd(y_pool, o, e, wup_p, wup_a):
    lp = o.shape[0]
    nb = lp // ROW_TILE

    def body(yp_ref, o_ref, e_ref, wp_ref, wa_ref, mg_ref, ya_ref):
        za = e_ref[:, E_ZA:E_GP].astype(F32)
        ya = (o_ref[...] * (za * _sigmoid(za))).astype(BF16)
        ya_ref[...] = ya
        a_pool = jnp.dot(yp_ref[...], wp_ref[...], preferred_element_type=F32)
        a_attn = jnp.dot(ya, wa_ref[...], preferred_element_type=F32)
        mg_ref[...] = (_sigmoid(e_ref[:, E_GP:E_GA].astype(F32)) * a_pool
                       + _sigmoid(e_ref[:, E_GA:E_COLS].astype(F32)) * a_attn).astype(BF16)

    return pl.pallas_call(
        body, name="merge_fwd", grid=(nb,),
        in_specs=[pl.BlockSpec((ROW_TILE, POOL_WIDTH), lambda i: (i, 0)),
                  pl.BlockSpec((ROW_TILE, ATTN_WIDTH), lambda i: (i, 0)),
                  pl.BlockSpec((ROW_TILE, E_COLS), lambda i: (i, 0)),
                  pl.BlockSpec((POOL_WIDTH, D_MODEL), lambda i: (0, 0)),
                  pl.BlockSpec((ATTN_WIDTH, D_MODEL), lambda i: (0, 0))],
        out_specs=[pl.BlockSpec((ROW_TILE, D_MODEL), lambda i: (i, 0)),
                   pl.BlockSpec((ROW_TILE, ATTN_WIDTH), lambda i: (i, 0))],
        out_shape=[jax.ShapeDtypeStruct((lp, D_MODEL), BF16), jax.ShapeDtypeStruct((lp, ATTN_WIDTH), BF16)],
        compiler_params=_params(("parallel",)),
    )(y_pool, o, e, wup_p, wup_a)


def _head_fwd_bwd(merged, w_out, x2, metapad, gf, target):
    lp = merged.shape[0]
    nb = lp // ROW_TILE

    def body(mg_ref, w_ref, x_ref, mp_ref, g_ref, t_ref, dh_ref, loss_ref, dg_ref):
        i = pl.program_id(0)

        @pl.when(i == 0)
        def _():
            loss_ref[...] = jnp.zeros_like(loss_ref)
            dg_ref[...] = jnp.zeros_like(dg_ref)

        h0 = jnp.where(i == 0, mp_ref[...], x_ref[...])
        h1 = h0 + jnp.dot(mg_ref[...], w_ref[...], preferred_element_type=F32)
        r = lax.rsqrt(jnp.mean(h1 * h1, axis=-1, keepdims=True) + RMS_EPS)
        xhat = h1 * r
        g = g_ref[...]
        err = jnp.where(i == 0, 0.0, xhat * g - t_ref[...])
        loss_ref[...] += 0.5 * jnp.sum(jnp.mean(err * err, axis=-1, keepdims=True))
        dy = err / D_MODEL
        dg_ref[...] += jnp.sum(dy * xhat, axis=0, keepdims=True)
        dxhat = dy * g
        dh_ref[...] = r * (dxhat - xhat * jnp.mean(dxhat * xhat, axis=-1, keepdims=True))

    return pl.pallas_call(
        body, name="head_fwd_bwd", grid=(nb,),
        in_specs=[pl.BlockSpec((ROW_TILE, D_MODEL), lambda i: (i, 0)),
                  pl.BlockSpec((D_MODEL, D_MODEL), lambda i: (0, 0)),
                  _tokens_spec(), pl.BlockSpec((ROW_TILE, D_MODEL), lambda i: (0, 0)),
                  pl.BlockSpec((1, D_MODEL), lambda i: (0, 0)), _tokens_spec()],
        out_specs=[pl.BlockSpec((ROW_TILE, D_MODEL), lambda i: (i, 0)),
                   pl.BlockSpec((1, LANES), lambda i: (0, 0)), pl.BlockSpec((1, D_MODEL), lambda i: (0, 0))],
        out_shape=[jax.ShapeDtypeStruct((lp, D_MODEL), F32), jax.ShapeDtypeStruct((1, LANES), F32),
                   jax.ShapeDtypeStruct((1, D_MODEL), F32)],
        compiler_params=_params(("arbitrary",)),
    )(merged, w_out, x2, metapad, gf, target)


def _per_head_rowsum(t):
    head = lax.broadcasted_iota(jnp.int32, t.shape, 1) // HEAD_DIM
    out = jnp.zeros_like(t)
    for h in range(N_HEADS):
        sel = head == h
        out = jnp.where(sel, jnp.sum(jnp.where(sel, t, 0.0), axis=1, keepdims=True), out)
    return out


def _merge_bwd(dh1, w_out, y_pool, y_attn, wup_p, wup_a, e, o):
    lp = o.shape[0]
    nb = lp // ROW_TILE

    def body(dh_ref, wo_ref, yp_ref, ya_ref, wp_ref, wa_ref, e_ref, o_ref,
             dap_ref, daa_ref, dg_ref, do_ref, delta_ref, dyp_ref):
        dmerged = _dot_nt(dh_ref[...].astype(BF16), wo_ref[...])
        a_pool = jnp.dot(yp_ref[...], wp_ref[...], preferred_element_type=F32)
        a_attn = jnp.dot(ya_ref[...], wa_ref[...], preferred_element_type=F32)
        sp = _sigmoid(e_ref[:, E_GP:E_GA].astype(F32))
        sa = _sigmoid(e_ref[:, E_GA:E_COLS].astype(F32))
        dap = (dmerged * sp).astype(BF16)
        daa = (dmerged * sa).astype(BF16)
        dap_ref[...] = dap
        daa_ref[...] = daa
        dg_ref[:, ATTN_WIDTH:ATTN_WIDTH + D_MODEL] = (dmerged * a_pool * (sp * (1.0 - sp))).astype(BF16)
        dg_ref[:, ATTN_WIDTH + D_MODEL:] = (dmerged * a_attn * (sa * (1.0 - sa))).astype(BF16)
        dyp_ref[...] = _dot_nt(dap, wp_ref[...])
        dya = _dot_nt(daa, wa_ref[...])
        za = e_ref[:, E_ZA:E_GP].astype(F32)
        sz = _sigmoid(za)
        o = o_ref[...]
        do = dya * (za * sz)
        do_ref[...] = do.astype(BF16)
        dg_ref[:, :ATTN_WIDTH] = (dya * o * (sz * (1.0 + za * (1.0 - sz)))).astype(BF16)
        delta_ref[...] = _per_head_rowsum(do * o)

    row = lambda w: pl.BlockSpec((ROW_TILE, w), lambda i: (i, 0))
    full = lambda a: pl.BlockSpec(a.shape, lambda i: (0, 0))
    return pl.pallas_call(
        body, name="merge_bwd", grid=(nb,),
        in_specs=[row(D_MODEL), full(w_out), row(POOL_WIDTH), row(ATTN_WIDTH), full(wup_p), full(wup_a),
                  row(E_COLS), row(ATTN_WIDTH)],
        out_specs=[row(D_MODEL), row(D_MODEL), pl.BlockSpec((ROW_TILE, MAIN_COLS // 2), lambda i: (i, 1)),
                   row(ATTN_WIDTH), row(ATTN_WIDTH), row(POOL_WIDTH)],
        out_shape=[jax.ShapeDtypeStruct((lp, D_MODEL), BF16), jax.ShapeDtypeStruct((lp, D_MODEL), BF16),
                   jax.ShapeDtypeStruct((lp, MAIN_COLS), BF16),
                   jax.ShapeDtypeStruct((lp, ATTN_WIDTH), BF16), jax.ShapeDtypeStruct((lp, ATTN_WIDTH), F32),
                   jax.ShapeDtypeStruct((lp, POOL_WIDTH), F32)],
        compiler_params=_params(("parallel",)),
    )(dh1, w_out, y_pool, y_attn, wup_p, wup_a, e, o)


def _fill_dproj(dproj_half, du, dzp, dq, dk, dv):
    lp = dproj_half.shape[0]
    nb = lp // ROW_TILE

    def body(base_ref, du_ref, dzp_ref, dq_ref, dk_ref, dv_ref, o_ref):
        parts = [du_ref[...], dzp_ref[...], (dq_ref[...] * HEAD_DIM ** -0.5).astype(BF16), dk_ref[...], dv_ref[...]]
        for t, p in enumerate(parts):
            o_ref[:, t * ATTN_WIDTH:(t + 1) * ATTN_WIDTH] = p

    blk = pl.BlockSpec((ROW_TILE, ATTN_WIDTH), lambda i: (i, 0))
    return pl.pallas_call(
        body, name="fill_dproj", grid=(nb,),
        in_specs=[pl.BlockSpec(memory_space=pl.ANY)] + [blk] * 5,
        out_specs=pl.BlockSpec((ROW_TILE, MAIN_COLS // 2), lambda i: (i, 0)),
        out_shape=jax.ShapeDtypeStruct(dproj_half.shape, BF16),
        input_output_aliases={0: 0},
        compiler_params=_params(("parallel",)),
    )(dproj_half, du, dzp, dq, dk, dv)


def _pool_bwd_local(e, dy_pool, pw, scale):
    lp = e.shape[0]
    nb = lp // ROW_TILE
    ng = len(POOL_WINDOWS)

    def body(uc_ref, up_ref, z_ref, dy_ref, pw_ref, sc_ref, dpc_ref, dz_ref, dsc_ref, dpw_ref):
        i = pl.program_id(0)

        @pl.when(i == 0)
        def _():
            dsc_ref[...] = jnp.zeros_like(dsc_ref)
            dpw_ref[...] = jnp.zeros_like(dpw_ref)

        u_cur = uc_ref[...].astype(F32)
        u_prev = jnp.where(i == 0, 0.0, up_ref[...].astype(F32))
        ps, invs = _pool_p(u_cur, u_prev, i)
        z = z_ref[...].astype(F32)
        sz = _sigmoid(z)
        dy = dy_ref[...]
        dypre = dy * (z * sz)
        dsilu = sz * (1.0 + z * (1.0 - sz))
        for g in range(ng):
            sl = slice(g * POOL_GROUP, (g + 1) * POOL_GROUP)
            pb = ps[g].astype(BF16)
            w = pw_ref[g]
            yraw = jnp.dot(pb, w, preferred_element_type=F32)
            sc = sc_ref[:, sl]
            dz_ref[:, sl] = (dy[:, sl] * (yraw * sc) * dsilu[:, sl]).astype(BF16)
            dsc_ref[:, sl] += jnp.sum(dypre[:, sl] * yraw, axis=0, keepdims=True)
            dyraw = (dypre[:, sl] * sc).astype(BF16)
            dpw_ref[g] += _dot_tn(pb, dyraw)
            dpc_ref[:, sl] = _dot_nt(dyraw, w) * invs[g]

    blk = (ROW_TILE, POOL_WIDTH)
    return pl.pallas_call(
        body, name="pool_bwd_local", grid=(nb,),
        in_specs=[pl.BlockSpec(blk, lambda i: (i, 0)), pl.BlockSpec(blk, lambda i: (jnp.maximum(i - 1, 0), 0)),
                  pl.BlockSpec(blk, lambda i: (i, 1)), pl.BlockSpec(blk, lambda i: (i, 0)),
                  pl.BlockSpec((ng, POOL_GROUP, POOL_GROUP), lambda i: (0, 0, 0)),
                  pl.BlockSpec((1, POOL_WIDTH), lambda i: (0, 0))],
        out_specs=[pl.BlockSpec(blk, lambda i: (i, 0)), pl.BlockSpec(blk, lambda i: (i, 0)),
                   pl.BlockSpec((1, POOL_WIDTH), lambda i: (0, 0)),
                   pl.BlockSpec((ng, POOL_GROUP, POOL_GROUP), lambda i: (0, 0, 0))],
        out_shape=[jax.ShapeDtypeStruct((lp, POOL_WIDTH), F32), jax.ShapeDtypeStruct((lp, POOL_WIDTH), BF16),
                   jax.ShapeDtypeStruct((1, POOL_WIDTH), F32),
                   jax.ShapeDtypeStruct((ng, POOL_GROUP, POOL_GROUP), F32)],
        compiler_params=_params(("arbitrary",)),
    )(e, e, e, dy_pool, pw, scale)


def _pool_bwd_window(dpc):
    lp = dpc.shape[0]
    nb = lp // ROW_TILE

    def body(cur_ref, nxt_ref, du_ref):
        i = pl.program_id(0)
        cur = cur_ref[...]
        nxt = jnp.where(i == nb - 1, 0.0, nxt_ref[...])
        pos = _pool_counts(i)
        for g, w in enumerate(POOL_WINDOWS):
            sl = slice(g * POOL_GROUP, (g + 1) * POOL_GROUP)
            xc = jnp.concatenate([cur[:, sl], nxt[:, sl]], axis=0)
            win = _leading_sums(xc, g + 1)[:ROW_TILE, :]
            dp = cur[:, sl] * jnp.minimum(pos + 1, w).astype(F32)
            du_ref[:, sl] = (win - dp).astype(BF16)

    blk = (ROW_TILE, POOL_WIDTH)
    return pl.pallas_call(
        body, name="pool_bwd_window", grid=(nb,),
        in_specs=[pl.BlockSpec(blk, lambda i: (i, 0)), pl.BlockSpec(blk, lambda i: (jnp.minimum(i + 1, nb - 1), 0))],
        out_specs=pl.BlockSpec(blk, lambda i: (i, 0)),
        out_shape=jax.ShapeDtypeStruct((lp, POOL_WIDTH), BF16),
        compiler_params=_params(("parallel",)),
    )(dpc, dpc)


def _attn_bwd(qkv, do, lse, delta, bias, nb):
    lk = qkv.shape[0]
    lp = nb * ROW_TILE
    nkb = lk // KV_TILE_BWD
    n_pairs = N_HEADS // 2
    per_kv = KV_TILE_BWD // ROW_TILE

    def body(q_ref, k_ref, v_ref, do_ref, lse_ref, dl_ref, b_ref, dq_ref, dk_ref, dv_ref, dc_ref, dcq_ref,
             dk_acc, dv_acc, dc_acc):
        jj = pl.program_id(1)

        @pl.when(jj == 0)
        def _():
            dq_ref[...] = jnp.zeros_like(dq_ref)
            dcq_ref[...] = jnp.zeros_like(dcq_ref)

        dk_acc[...] = jnp.zeros_like(dk_acc)
        dv_acc[...] = jnp.zeros_like(dv_acc)
        dc_acc[...] = jnp.zeros_like(dc_acc)

        def block(i, n_keys, masked):
            kb, vb = k_ref[:n_keys, :], v_ref[:n_keys, :]
            rows = pl.ds(pl.multiple_of(i * ROW_TILE, ROW_TILE), ROW_TILE)
            qs = _stack_heads(q_ref[rows, :])
            dos = _stack_heads(do_ref[rows, :])
            lse_i, dl_i = lse_ref[rows, :], dl_ref[rows, :]
            s = _dot_nt(qs, kb)
            dp = _dot_nt(dos, vb)
            if masked:
                valid = _causal(i, jj, 1, KV_TILE_BWD)[:, :n_keys]
            ps, dss, dcs, rowsums = [], [], [], []
            for hd in range(2):
                half = slice(hd * ROW_TILE, (hd + 1) * ROW_TILE)
                col = slice(hd * HEAD_DIM, hd * HEAD_DIM + 1)
                sh = s[half] - b_ref[i, 0, hd:hd + 1, :n_keys]
                if masked:
                    sh = jnp.where(valid, sh, NEG)
                p = jnp.exp(sh - lse_i[:, col])
                ds = p * (dp[half] - dl_i[:, col])
                ps.append(p.astype(BF16))
                dss.append(ds.astype(BF16))
                dcs.append(jnp.sum(ds, axis=0, keepdims=True))
                rowsums.append(jnp.sum(ds, axis=1, keepdims=True))
            dsb = jnp.concatenate(dss, axis=0)
            dv_acc[:n_keys, :] += _dot_tn(jnp.concatenate(ps, axis=0), dos)
            dk_acc[:n_keys, :] += _dot_tn(dsb, qs)
            dc_acc[:, :n_keys] -= jnp.concatenate(dcs, axis=0)
            dq_ref[rows, :] += _unstack_heads(jnp.dot(dsb, kb, preferred_element_type=F32))
            dcq_ref[rows, :] += _unstack_heads(jnp.broadcast_to(jnp.concatenate(rowsums, axis=0), (2 * ROW_TILE, LANES)))

        first_q = per_kv * jj
        for r in range(per_kv):
            @pl.when(first_q + r < nb)
            def _():
                block(first_q + r, (r + 1) * ROW_TILE, True)

        def rest(i, carry):
            block(i, KV_TILE_BWD, False)
            return carry

        lax.fori_loop(jnp.minimum(first_q + per_kv, nb), nb, rest, 0)
        dk_ref[...] = dk_acc[...].astype(BF16)
        dv_ref[...] = dv_acc[...].astype(BF16)
        dc_ref[0, 0] = dc_acc[...]

    whole = lambda rows: pl.BlockSpec((rows, LANES), lambda hp, jj: (0, hp))
    kv_blk = lambda off: pl.BlockSpec((KV_TILE_BWD, LANES), lambda hp, jj: (jj, off + hp))
    return pl.pallas_call(
        body, name="attn_bwd", grid=(n_pairs, nkb),
        in_specs=[whole(lk), kv_blk(n_pairs), kv_blk(2 * n_pairs), whole(lp), whole(lp), whole(lp),
                  pl.BlockSpec((nb, 1, 2, KV_TILE_BWD), lambda hp, jj: (0, hp, 0, jj))],
        out_specs=[whole(lp), kv_blk(0), kv_blk(0),
                   pl.BlockSpec((1, 1, 2, KV_TILE_BWD), lambda hp, jj: (hp, jj, 0, 0)), whole(lp)],
        out_shape=[jax.ShapeDtypeStruct((lp, ATTN_WIDTH), F32), jax.ShapeDtypeStruct((lk, ATTN_WIDTH), BF16),
                   jax.ShapeDtypeStruct((lk, ATTN_WIDTH), BF16),
                   jax.ShapeDtypeStruct((n_pairs, nkb, 2, KV_TILE_BWD), F32),
                   jax.ShapeDtypeStruct((lp, ATTN_WIDTH), F32)],
        scratch_shapes=[pltpu.VMEM((KV_TILE_BWD, LANES), F32), pltpu.VMEM((KV_TILE_BWD, LANES), F32),
                        pltpu.VMEM((2, KV_TILE_BWD), F32)],
        compiler_params=_params(("parallel", "arbitrary")),
    )(qkv, qkv, qkv, do, lse, delta, bias)


def _gates_bwd(dc, dcq, f, bfg):
    nb = f.shape[0] // ROW_TILE

    def body(dc_ref, dcq_ref, f_ref, b_ref, df_ref, db_ref, carry):
        step = pl.program_id(0)
        i = nb - 1 - step

        @pl.when(step == 0)
        def _():
            carry[...] = jnp.zeros_like(carry)
            db_ref[...] = jnp.zeros_like(db_ref)

        dcb = dc_ref[...]
        lane = lax.broadcasted_iota(jnp.int32, (ROW_TILE, F_COLS), 1)
        for h in range(N_HEADS):
            dcb = dcb + jnp.where(lane == h, dcq_ref[:, HEAD_DIM * h:HEAD_DIM * h + 1], 0.0)
        r_i = lax.broadcasted_iota(jnp.int32, (ROW_TILE, ROW_TILE), 0)
        c_i = lax.broadcasted_iota(jnp.int32, (ROW_TILE, ROW_TILE), 1)
        upper = (c_i >= r_i).astype(F32)
        dlf = jnp.dot(upper, dcb, precision=HIGHEST, preferred_element_type=F32) + carry[...]
        carry[...] = carry[...] + jnp.sum(dcb, axis=0, keepdims=True)
        logit = f_ref[...] + b_ref[...]
        dlogit = jnp.where(_valid_gate_mask(i), dlf * _sigmoid(-logit), 0.0)
        df_ref[...] = dlogit.astype(BF16)
        db_ref[...] += jnp.sum(dlogit, axis=0, keepdims=True)

    blk = pl.BlockSpec((ROW_TILE, F_COLS), lambda s: (nb - 1 - s, 0))
    wide = pl.BlockSpec((ROW_TILE, ATTN_WIDTH), lambda s: (nb - 1 - s, 0))
    one = pl.BlockSpec((1, F_COLS), lambda s: (0, 0))
    return pl.pallas_call(
        body, name="gates_bwd", grid=(nb,),
        in_specs=[blk, wide, blk, one], out_specs=[blk, one],
        out_shape=[jax.ShapeDtypeStruct(f.shape, BF16), jax.ShapeDtypeStruct((1, F_COLS), F32)],
        scratch_shapes=[pltpu.VMEM((1, F_COLS), F32)],
        compiler_params=_params(("arbitrary",)),
    )(dc, dcq, f, bfg)


def _input_bwd(dproj, df, wt_e, wt_qkv, wt_f, x2, metapad, dh1, g1, hs, sc=None):
    nb = x2.shape[0] // ROW_TILE + 1
    n = len(hs)
    sc_in, sc_shapes, sc_sems = _scatter_operands(hs, sc)
    nq = len(sc_in)

    def body(dp_ref, df_ref, we_ref, w_ref, wf_ref, x_ref, mp_ref, dh_ref, g_ref, *rest):
        h_refs, s_ref = rest[:n], (rest[n] if sc is not None else None)
        gx_ref, g0_ref, dg_ref = rest[nq:nq + 3]
        q_refs, sq_ref = rest[nq + 3:nq + 3 + n], (rest[nq + 3 + n] if sc is not None else None)
        sems = rest[2 * nq + 3:]
        i = pl.program_id(0)

        @pl.when(i == 0)
        def _():
            dg_ref[...] = jnp.zeros_like(dg_ref)
            own, sends, _ = _chip_scatter_plan(h_refs, s_ref, q_refs, sq_ref, *sems)
            for cp in own + sends:
                cp.start()

        dhn = (jnp.dot(dp_ref[:, :E_ZA], we_ref[:E_ZA, :], preferred_element_type=F32)
               + jnp.dot(dp_ref[:, E_ZA:MAIN_COLS // 2], w_ref[...], preferred_element_type=F32)
               + jnp.dot(dp_ref[:, MAIN_COLS // 2:], we_ref[E_ZA:, :], preferred_element_type=F32)
               + jnp.dot(df_ref[...], wf_ref[...], preferred_element_type=F32))
        h0 = jnp.where(i == 0, mp_ref[...], x_ref[...])
        r = lax.rsqrt(jnp.mean(h0 * h0, axis=-1, keepdims=True) + RMS_EPS)
        xhat = h0 * r
        dg_ref[...] += jnp.sum(dhn * xhat, axis=0, keepdims=True)
        dxhat = dhn * g_ref[...]
        dh0 = dh_ref[...] + r * (dxhat - xhat * jnp.mean(dxhat * xhat, axis=-1, keepdims=True))
        gx_ref[...] = dh0

        @pl.when(i == 0)
        def _():
            g0_ref[...] = dh0

        @pl.when(i == nb - 1)
        def _():
            own, sends, recvs = _chip_scatter_plan(h_refs, s_ref, q_refs, sq_ref, *sems)
            for cp in recvs:
                cp.wait_recv()
            for cp in sends:
                cp.wait_send()
            for cp in own:
                cp.wait()

    const = lambda shape: pl.BlockSpec(shape, lambda i: (0, 0))
    res = pl.pallas_call(
        body, name="input_bwd", grid=(nb,),
        in_specs=[pl.BlockSpec((ROW_TILE, MAIN_COLS), lambda i: (i, 0)), pl.BlockSpec((ROW_TILE, F_COLS), lambda i: (i, 0)),
                  const((E_COLS, D_MODEL)), const((QKV_COLS, D_MODEL)), const((F_COLS, D_MODEL)),
                  _tokens_spec(), const((ROW_TILE, D_MODEL)),
                  pl.BlockSpec((ROW_TILE, D_MODEL), lambda i: (i, 0)), const((1, D_MODEL))] + [HBM] * nq,
        out_specs=[_tokens_spec(), const((ROW_TILE, D_MODEL)), const((1, D_MODEL))] + [HBM] * nq,
        out_shape=[jax.ShapeDtypeStruct(x2.shape, F32), jax.ShapeDtypeStruct((ROW_TILE, D_MODEL), F32),
                   jax.ShapeDtypeStruct((1, D_MODEL), F32)] + sc_shapes,
        scratch_shapes=sc_sems,
        compiler_params=_params(("arbitrary",), vmem=56 * 1024 * 1024),
    )(dproj, df, wt_e, wt_qkv, wt_f, x2, metapad, dh1, g1, *_in_hbm(*sc_in))
    return res[:3], res[3:3 + n], (res[3 + n] if sc is not None else None)


def _adamw(w, g, m, v, name):
    rows, cols = w.shape
    if rows % 8 == 0:
        tr, tc = _row_tile8(rows), cols
    else:
        tr, tc = rows, (2 * LANES if cols % (2 * LANES) == 0 and rows > 8 else cols)

    def body(w_ref, g_ref, m_ref, v_ref, d_ref, mo_ref, vo_ref):
        g_ = g_ref[...]
        m_new = ADAM_B1 * m_ref[...] + (1.0 - ADAM_B1) * g_
        v_new = ADAM_B2 * v_ref[...] + (1.0 - ADAM_B2) * (g_ * g_)
        m_hat = m_new / (1.0 - ADAM_B1 ** ADAM_STEP)
        v_hat = v_new / (1.0 - ADAM_B2 ** ADAM_STEP)
        d_ref[...] = -ADAM_LR * (m_hat / (jnp.sqrt(v_hat) + ADAM_EPS) + ADAM_WD * w_ref[...])
        mo_ref[...] = m_new
        vo_ref[...] = v_new

    blk = pl.BlockSpec((tr, tc), lambda i, j: (i, j))
    return pl.pallas_call(
        body, name=name, grid=(rows // tr, cols // tc),
        in_specs=[blk] * 4, out_specs=[blk] * 3,
        out_shape=[jax.ShapeDtypeStruct(w.shape, F32)] * 3,
        compiler_params=_params(("parallel", "parallel")),
    )(w, g, m, v)


def _adamw_native(w3, g3, m3, v3, name):
    rows = w3.shape[0]
    tr = rows // 2
    blk = pl.BlockSpec((tr,) + w3.shape[1:], lambda i: (i, 0, 0))
    shape = jax.ShapeDtypeStruct(w3.shape, F32)

    def moments(g_ref, m_ref, v_ref, mo_ref, vo_ref):
        g_ = g_ref[...]
        mo_ref[...] = ADAM_B1 * m_ref[...] + (1.0 - ADAM_B1) * g_
        vo_ref[...] = ADAM_B2 * v_ref[...] + (1.0 - ADAM_B2) * (g_ * g_)

    new_m, new_v = pl.pallas_call(
        moments, name=name + "_moments", grid=(2,), in_specs=[blk] * 3, out_specs=[blk] * 2, out_shape=[shape] * 2,
        compiler_params=_params(("parallel",)),
    )(g3, m3, v3)

    def delta(w_ref, m_ref, v_ref, d_ref):
        m_hat = m_ref[...] / (1.0 - ADAM_B1 ** ADAM_STEP)
        v_hat = v_ref[...] / (1.0 - ADAM_B2 ** ADAM_STEP)
        d_ref[...] = -ADAM_LR * (m_hat / (jnp.sqrt(v_hat) + ADAM_EPS) + ADAM_WD * w_ref[...])

    d = pl.pallas_call(
        delta, name=name + "_delta", grid=(2,), in_specs=[blk] * 3, out_specs=blk, out_shape=shape,
        compiler_params=_params(("parallel",)),
    )(w3, new_m, new_v)
    return d, new_m, new_v


def _row_tile8(rows):
    best = rows
    for t in range(8, 257, 8):
        if rows % t == 0:
            best = t
    return best


def kernel(x, meta_tokens, norm_g, w_in, b_forget, pool_w, pool_scale, w_up_pool, w_up_attn, w_out, final_norm_g, loss_target, m_meta_tokens, m_norm_g, m_w_in, m_b_forget, m_pool_w, m_pool_scale, m_w_up_pool, m_w_up_attn, m_w_out, m_final_norm_g, v_meta_tokens, v_norm_g, v_w_in, v_b_forget, v_pool_w, v_pool_scale, v_w_up_pool, v_w_up_attn, v_w_out, v_final_norm_g):
    seq = x.shape[1]
    assert seq % ROW_TILE == 0 and x.shape[0] == 1
    lp = seq + ROW_TILE
    nb = lp // ROW_TILE
    lk = -(-lp // KV_TILE_BWD) * KV_TILE_BWD
    core = jnp.reshape(lax.axis_index("c"), (1,)).astype(jnp.int32)
    x2 = x[0]
    target = loss_target[0]
    sh_d = D_MODEL // N_CHIPS

    to_rows = lambda a: jnp.transpose(a, (2, 0, 1))
    from_rows = lambda a: jnp.transpose(a, (1, 2, 0))
    gf = final_norm_g.reshape(1, D_MODEL)
    bfg = jnp.pad(b_forget, ((0, 0), (0, F_COLS - N_HEADS)))
    pw_b = pool_w[0].astype(BF16)
    hn, (wg_in, meta_g) = _rmsnorm_fwd(x2, norm_g, lk, [jnp.transpose(w_in[0]).astype(BF16), meta_tokens], [1, 0])
    wt = wg_in.reshape(-1, D_MODEL)
    wt_e = jnp.concatenate([wt[REF_U:REF_Q], wt[REF_ZA:REF_F], wt[REF_GP:]], axis=0)
    wt_qkv = wt[REF_Q:REF_ZA]
    wt_f = jnp.pad(wt[REF_F:REF_GP], ((0, F_COLS - N_HEADS), (0, 0)))
    meta_full = jnp.transpose(meta_g, (1, 0, 2)).reshape(N_META, D_MODEL)
    metapad = jnp.pad(meta_full, ((PAD_ROWS, 0), (0, 0)))

    tm = _row_tile(lp, 2200)
    e, (wg_up_p, wg_up_a, wg_out) = _mm_nt(
        hn, wt_e, BF16, "in_proj_gates", lp, E_COLS, tm, 512,
        gather=([w_up_pool[0].astype(BF16), w_up_attn[0].astype(BF16), w_out[0].astype(BF16)], [0, 0, 0]))
    wup_p = jnp.transpose(wg_up_p, (1, 0, 2)).reshape(POOL_WIDTH, D_MODEL)
    wup_a = jnp.transpose(wg_up_a, (1, 0, 2)).reshape(ATTN_WIDTH, D_MODEL)
    wout = wg_out.reshape(D_MODEL, D_MODEL)
    qkv = _mm_nt(hn, wt_qkv, BF16, "in_proj_qkv", lk, QKV_COLS, _row_tile(lk, 2600), 512, scale_first=HEAD_DIM ** -0.5)
    f = _mm_nt(hn, wt_f, F32, "in_proj_forget", lp, F_COLS, tm, F_COLS)
    c = _gates_fwd(f, bfg)
    c_t = jnp.transpose(c[:, :N_HEADS])
    c_first = jnp.transpose(c_t[:, ::ROW_TILE]).reshape(nb, N_HEADS // 2, 2, 1)
    c_keys = jnp.pad(c_t, ((0, 0), (0, lk - lp))).reshape(1, N_HEADS // 2, 2, lk)
    bias = jnp.where(jnp.arange(lk) < PAD_ROWS, -NEG, c_keys - c_first)
    y_pool = _pool_fwd(e, pw_b, pool_scale)
    o, lse = _attn_fwd(qkv, bias, nb)
    merged, y_attn = _merge_fwd(y_pool, o, e, wup_p, wup_a)
    dh1, loss_part, dgf = _head_fwd_bwd(merged, wout, x2, metapad, gf, target)

    dap, daa, dproj_half, do, delta, dy_pool = _merge_bwd(dh1, wout, y_pool, y_attn, wup_p, wup_a, e, o)
    dpc, dzp, dscale, dpw = _pool_bwd_local(e, dy_pool, pw_b, pool_scale)
    du = _pool_bwd_window(dpc)
    dq, dk, dv, dc4, dcq = _attn_bwd(qkv, do, lse, delta, bias, nb)
    dc = jnp.transpose(dc4, (1, 3, 0, 2)).reshape(lk, N_HEADS)[:lp]
    df, db = _gates_bwd(jnp.pad(dc, ((0, 0), (0, F_COLS - N_HEADS))), dcq, f, bfg)
    dproj = _fill_dproj(dproj_half, du, dzp, dq, dk, dv)
    tk = _row_tile(lp, 1100)
    dw_out = _mm_tn(merged, dh1, "grad_w_out", lp, 512, D_MODEL, tk)
    dw_up_p = _mm_tn(y_pool, dap, "grad_w_up_pool", lp, 512, D_MODEL, lp, chunks=N_CHIPS)
    dw_up_a = _mm_tn(y_attn, daa, "grad_w_up_attn", lp, 512, D_MODEL, lp, chunks=N_CHIPS)
    dwt_f = _mm_tn(df, hn, "grad_w_in_forget", lp, F_COLS, D_MODEL, tk)

    def pad8(a):
        return jnp.pad(a, ((0, (-a.shape[0]) % 8), (0, 0)))

    small_parts = [pad8(a) for a in (dgf.reshape(-1, LANES), dscale.reshape(-1, LANES), db, dpw.reshape(-1, LANES),
                                     loss_part)]
    small = jnp.concatenate(small_parts, axis=0)
    soffs = [0]
    for p in small_parts:
        soffs.append(soffs[-1] + p.shape[0])

    gs_rows = [dw_up_p, dw_up_a, dw_out.reshape(N_CHIPS, sh_d, D_MODEL)]
    half = MAIN_COLS // 2
    dwt_a, (*rb_rows, rb_f, sr) = _mm_tn(dproj, hn, "grad_w_in_a", lp, half // 4, D_MODEL, lp, m=half, m_off=0,
                                         swap=(gs_rows + [dwt_f, small], [1, 1, 1, 1, None]))
    *hs_rows, sc = _add_sibling_half(gs_rows, rb_rows, [1, 1, 1], small, sr, core)
    dwt_b, (rb_a,), (qs_rows, sq) = _mm_tn(dproj, hn, "grad_w_in_b", lp, half // 4, D_MODEL, lp, m=half, m_off=4,
                                           swap=([dwt_a], [1]), scatter=(hs_rows, sc))
    (rb_b,) = _sibling_exchange([dwt_b], [1])
    h_in = _add_halves_w_in([dwt_a, dwt_b, dwt_f], [rb_a, rb_b, rb_f], core)
    grad_axes = [2, 1, 1, 1]
    (grad_x, g_block0, dg1), (q_in,), _ = _input_bwd(dproj, df, wt_e, wt_qkv, wt_f, x2, metapad, dh1, norm_g, [h_in])
    g_w_in_rows, g_w_up_p, g_w_up_a, g_w_out, st = _reduce_allgather([q_in] + list(qs_rows), grad_axes, sq,
                                                                       [True, False, False, False])
    late = _small_allreduce(jnp.concatenate([pad8(dg1.reshape(-1, LANES)), g_block0[PAD_ROWS:].reshape(-1, LANES)], axis=0))
    n_norm = D_MODEL // LANES
    g_norm = late[:n_norm].reshape(1, D_MODEL)
    chip = 2 * lax.axis_index("x") + lax.axis_index("y")
    g_meta = lax.dynamic_slice_in_dim(late[n_norm:].reshape(N_META, D_MODEL), chip * sh_d, sh_d, axis=1)

    spiece = lambda k, rows: st[soffs[k]:soffs[k] + rows]
    g_final = spiece(0, D_MODEL // LANES).reshape(1, D_MODEL)
    g_scale = spiece(1, POOL_WIDTH // LANES).reshape(1, POOL_WIDTH)
    g_bf = spiece(2, 1)
    g_pw = spiece(3, POOL_WIDTH)
    loss = st[soffs[4], 0]

    def pad_lanes(a):
        return jnp.pad(a, ((0, 0), (0, F_COLS - N_HEADS)))

    w_in_res = (g_w_in_rows,) + _adamw_native(to_rows(w_in), g_w_in_rows, to_rows(m_w_in), to_rows(v_w_in), "adamw_w_in")

    upd = [
        ("meta_tokens", meta_tokens, g_meta, m_meta_tokens, v_meta_tokens),
        ("norm_g", norm_g, g_norm, m_norm_g, v_norm_g),
        ("w_in", None, None, None, None),
        ("b_forget", pad_lanes(b_forget), g_bf, pad_lanes(m_b_forget), pad_lanes(v_b_forget)),
        ("pool_w", pool_w.reshape(-1, LANES), g_pw, m_pool_w.reshape(-1, LANES), v_pool_w.reshape(-1, LANES)),
        ("pool_scale", pool_scale, g_scale, m_pool_scale, v_pool_scale),
        ("w_up_pool", w_up_pool[0], g_w_up_p, m_w_up_pool[0], v_w_up_pool[0]),
        ("w_up_attn", w_up_attn[0], g_w_up_a, m_w_up_attn[0], v_w_up_attn[0]),
        ("w_out", w_out[0], g_w_out, m_w_out[0], v_w_out[0]),
        ("final_norm_g", gf, g_final, m_final_norm_g.reshape(1, D_MODEL), v_final_norm_g.reshape(1, D_MODEL)),
    ]
    shapes = [meta_tokens.shape, norm_g.shape, w_in.shape, b_forget.shape, pool_w.shape, pool_scale.shape,
              w_up_pool.shape, w_up_attn.shape, w_out.shape, final_norm_g.shape]
    grads, deltas, new_ms, new_vs = [], [], [], []
    for (name, w_, g_, m_in, v_in), shp in zip(upd, shapes):
        if name == "w_in":
            res = tuple(from_rows(a) for a in w_in_res)
        else:
            d_, mn_, vn_ = _adamw(w_, g_, m_in, v_in, "adamw_" + name)
            res = (g_, d_, mn_, vn_)
        if name == "b_forget":
            res = tuple(a[:, :N_HEADS] for a in res)
        for lst, a in zip((grads, deltas, new_ms, new_vs), res):
            lst.append(a.reshape(shp))

    return (loss, grad_x.reshape(x.shape), *grads, *deltas, *new_ms, *new_vs)
```

```python
import jax
import jax.numpy as jnp
from jax import lax
from jax.experimental import pallas as pl
from jax.experimental.pallas import tpu as pltpu

F32 = jnp.float32
BF16 = jnp.bfloat16
MESH = pl.DeviceIdType.MESH
HIGHEST = lax.Precision.HIGHEST
HBM = pl.BlockSpec(memory_space=pltpu.HBM)

D_MODEL = 1024
N_META = 16
POOL_WIDTH = 512
POOL_GROUP = 128
POOL_WINDOWS = (2, 4, 8, 16)
N_HEADS = 8
HEAD_DIM = 64
ATTN_WIDTH = 512
RMS_EPS = 1e-6
N_CHIPS = 4

ADAM_LR = 0.001
ADAM_B1 = 0.9
ADAM_B2 = 0.999
ADAM_EPS = 1e-08
ADAM_WD = 0.01
ADAM_STEP = 10

LANES = 128
ROW_TILE = 256
KV_TILE = 1024
KV_TILE_BWD = 1024
PAD_ROWS = ROW_TILE - N_META
NEG = -1e30

E_U, E_ZP, E_ZA, E_GP, E_GA, E_COLS = 0, 512, 1024, 1536, 2560, 3584
QKV_COLS = 3 * ATTN_WIDTH
MAIN_COLS = E_COLS + QKV_COLS
F_COLS = LANES
REF_U, REF_Q, REF_ZA, REF_F, REF_GP = 0, 1024, 2560, 3072, 3080
VMEM_LIMIT = 48 * 1024 * 1024


def _params(sem=None, vmem=VMEM_LIMIT):
    return pltpu.CompilerParams(dimension_semantics=sem, vmem_limit_bytes=vmem)


def _in_hbm(*arrays):
    return [pltpu.with_memory_space_constraint(a, pltpu.HBM) for a in arrays]


def _pallas_call(body, **kw):
    call = pl.pallas_call(body, **kw)
    n_scalar = kw["grid_spec"].num_scalar_prefetch if "grid_spec" in kw else 0
    return lambda *args: call(*args[:n_scalar], *_in_hbm(*args[n_scalar:]))


def _row_tile(n, target):
    best = 16
    for t in range(16, target + 1, 16):
        if n % t == 0:
            best = t
    return best


def _sigmoid(x):
    return 1.0 / (1.0 + jnp.exp(-x))


def _half(ref, axis, which):
    n = ref.shape[axis] // 2
    idx = [slice(None)] * len(ref.shape)
    idx[axis] = pl.ds(pl.multiple_of(which * n, n), n)
    return ref.at[tuple(idx)]


def _half_shape(shape, axis):
    s = list(shape)
    s[axis] //= 2
    return tuple(s)


def _gather_start(ins, outs, axes, send, recv, fsend, frecv, local):
    x, y, c = lax.axis_index("x"), lax.axis_index("y"), lax.axis_index("c")
    me = 2 * x + y
    for t in range(len(ins)):
        pltpu.make_async_copy(ins[t], outs[t].at[me], local.at[t]).start()
        for k, chip in enumerate([(1 - x, y), (x, 1 - y), (1 - x, 1 - y)]):
            pltpu.make_async_remote_copy(
                src_ref=_half(ins[t], axes[t], c), dst_ref=_half(outs[t].at[me], axes[t], c),
                send_sem=send.at[3 * t + k], recv_sem=recv.at[3 * t + k], device_id=(*chip, c), device_id_type=MESH).start()


def _gather_finish(ins, outs, axes, send, recv, fsend, frecv, local):
    x, y, c = lax.axis_index("x"), lax.axis_index("y"), lax.axis_index("c")
    me = 2 * x + y
    sibling = (x, y, 1 - c)
    chips = [(1 - x, y), (x, 1 - y), (1 - x, 1 - y)]
    n = len(ins)

    def over_ici(t, k, chip, dst_slot):
        return pltpu.make_async_remote_copy(
            src_ref=_half(ins[t], axes[t], c), dst_ref=_half(outs[t].at[dst_slot], axes[t], c),
            send_sem=send.at[3 * t + k], recv_sem=recv.at[3 * t + k], device_id=(*chip, c), device_id_type=MESH)

    def to_sibling(t, k, slot, which):
        return pltpu.make_async_remote_copy(
            src_ref=_half(outs[t].at[slot], axes[t], which), dst_ref=_half(outs[t].at[slot], axes[t], which),
            send_sem=fsend.at[3 * t + k], recv_sem=frecv.at[3 * t + k], device_id=sibling, device_id_type=MESH)

    forwards = []
    for t in range(n):
        for k, (px, py) in enumerate(chips):
            over_ici(t, k, (px, py), 2 * px + py).wait_recv()
            fw = to_sibling(t, k, 2 * px + py, c)
            fw.start()
            forwards.append(fw)
    for t in range(n):
        for k, (px, py) in enumerate(chips):
            to_sibling(t, k, 2 * px + py, 1 - c).wait_recv()
    for t in range(n):
        for k, chip in enumerate(chips):
            over_ici(t, k, chip, me).wait_send()
    for fw in forwards:
        fw.wait_send()
    for t in range(n):
        pltpu.make_async_copy(ins[t], outs[t].at[me], local.at[t]).wait()


def _gather_sems(n):
    return [pltpu.SemaphoreType.DMA((3 * n,)), pltpu.SemaphoreType.DMA((3 * n,)), pltpu.SemaphoreType.DMA((3 * n,)),
            pltpu.SemaphoreType.DMA((3 * n,)), pltpu.SemaphoreType.DMA((n,))]


def _gathered_shapes(shards):
    return [jax.ShapeDtypeStruct((N_CHIPS,) + s.shape, s.dtype) for s in shards]


def _swap_copies(srcs, dsts, axes, send, recv):
    x, y, c = lax.axis_index("x"), lax.axis_index("y"), lax.axis_index("c")
    return [pltpu.make_async_remote_copy(
        src_ref=srcs[t] if axes[t] is None else _half(srcs[t], axes[t], 1 - c), dst_ref=dsts[t],
        send_sem=send.at[t], recv_sem=recv.at[t], device_id=(x, y, 1 - c), device_id_type=MESH) for t in range(len(srcs))]


def _swap_shapes(srcs, axes):
    return [jax.ShapeDtypeStruct(g.shape if a is None else _half_shape(g.shape, a), g.dtype) for g, a in zip(srcs, axes)]


def _sibling_exchange(gs, axes):
    n = len(gs)

    def body(*refs):
        cps = _swap_copies(refs[:n], refs[n:2 * n], axes, *refs[2 * n:])
        for cp in cps:
            cp.start()
        for cp in cps:
            cp.wait()

    return _pallas_call(
        body, name="grad_sibling_exchange",
        in_specs=[HBM] * n, out_specs=[HBM] * n, out_shape=_swap_shapes(gs, axes),
        scratch_shapes=[pltpu.SemaphoreType.DMA((n,)), pltpu.SemaphoreType.DMA((n,))],
    )(*_in_hbm(*gs))


def _add_halves_w_in(parts, rbs, core):
    na, nb_rows, w = rbs[0].shape[0], rbs[1].shape[0], rbs[0].shape[1]
    per = (na + nb_rows + N_HEADS) // N_CHIPS
    pieces = [[] for _ in range(N_CHIPS)]
    r = 0
    for src, lo, hi in ((0, 0, na), (1, 0, ATTN_WIDTH), (2, 0, N_HEADS), (1, ATTN_WIDTH, nb_rows)):
        while lo < hi:
            c, at = divmod(r, per)
            n = min(hi - lo, per - at)
            pieces[c].append((src, lo, n, at))
            lo, r = lo + n, r + n

    def body(core_ref, a_ref, b_ref, f_ref, ra_ref, rb_ref, rf_ref, o_ref, am, bm, fm, ar, br, fr, rows_buf, sems):
        cols = pl.ds(pl.multiple_of(core_ref[0] * w, w), w)
        few = pl.ds(0, N_HEADS)
        loads = [(pltpu.make_async_copy(a_ref.at[:, cols], am, sems.at[0]), pltpu.make_async_copy(ra_ref, ar, sems.at[1])),
                 (pltpu.make_async_copy(b_ref.at[:, cols], bm, sems.at[2]), pltpu.make_async_copy(rb_ref, br, sems.at[3])),
                 (pltpu.make_async_copy(f_ref.at[few, cols], fm, sems.at[4]),
                  pltpu.make_async_copy(rf_ref.at[few, :], fr, sems.at[5]))]
        for pair in loads:
            for cp in pair:
                cp.start()
        bufs = [(am, ar), (bm, br), (fm, fr)]
        there = set()
        for c in range(N_CHIPS):
            for src, lo, n, at in pieces[c]:
                if src not in there:
                    for cp in loads[src]:
                        cp.wait()
                    there.add(src)
                mine, other = bufs[src]
                rows_buf[at:at + n, :] = mine[lo:lo + n, :] + other[lo:lo + n, :]
            o_ref[c] = rows_buf[0:per, :].astype(BF16)

    any_space = pl.BlockSpec(memory_space=pl.ANY)
    return _pallas_call(
        body, name="grad_add_sibling_w_in",
        grid_spec=pltpu.PrefetchScalarGridSpec(
            num_scalar_prefetch=1, grid=(1,),
            in_specs=[any_space] * 6,
            out_specs=pl.BlockSpec((N_CHIPS, per, w), lambda i, cr: (0, 0, 0)),
            scratch_shapes=[pltpu.VMEM((na, w), F32), pltpu.VMEM((nb_rows, w), F32), pltpu.VMEM((N_HEADS, w), F32),
                            pltpu.VMEM((na, w), F32), pltpu.VMEM((nb_rows, w), F32), pltpu.VMEM((N_HEADS, w), F32),
                            pltpu.VMEM((per + (-per) % 8, w), F32), pltpu.SemaphoreType.DMA((6,))]),
        out_shape=jax.ShapeDtypeStruct((N_CHIPS, per, w), BF16),
        compiler_params=_params(("arbitrary",)),
    )(core, *parts, *rbs)


def _add_sibling_half(gs, rbs, axes, small, sr, core):
    n = len(gs)

    def body(core_ref, *refs):
        g_refs, rb_refs = refs[:n], refs[n:2 * n]
        s_ref, sr_ref = refs[2 * n], refs[2 * n + 1]
        h_refs, sc_ref = refs[2 * n + 2:3 * n + 2], refs[3 * n + 2]
        for t in range(n):
            h_refs[t][...] = (g_refs[t][...] + rb_refs[t][...]).astype(BF16)
        sc_ref[...] = s_ref[...] + sr_ref[...]

    def mine(rb, axis):
        blk = (None,) + rb.shape[1:]
        if axis == 2:
            return pl.BlockSpec(blk, lambda j, cr: (j, 0, cr[0]))
        return pl.BlockSpec(blk, lambda j, cr: (j, cr[0], 0))

    chunk = lambda rb: pl.BlockSpec((None,) + rb.shape[1:], lambda j, cr: (j, 0, 0))
    whole = pl.BlockSpec(small.shape, lambda j, cr: (0, 0))
    return _pallas_call(
        body, name="grad_add_sibling",
        grid_spec=pltpu.PrefetchScalarGridSpec(
            num_scalar_prefetch=1, grid=(N_CHIPS,),
            in_specs=[mine(rb, a) for rb, a in zip(rbs, axes)] + [chunk(rb) for rb in rbs] + [whole, whole],
            out_specs=[chunk(rb) for rb in rbs] + [whole]),
        out_shape=[jax.ShapeDtypeStruct(rb.shape, BF16) for rb in rbs] + [jax.ShapeDtypeStruct(small.shape, F32)],
        compiler_params=_params(("arbitrary",)),
    )(core, *gs, *rbs, small, sr)


def _chip_scatter_plan(h_refs, s_ref, q_refs, sq_ref, send, recv, local):
    n = len(h_refs)
    nt = n + (s_ref is not None)
    x, y, c = lax.axis_index("x"), lax.axis_index("y"), lax.axis_index("c")
    me = 2 * x + y
    chips = [(1 - x, y), (x, 1 - y), (1 - x, 1 - y)]

    def copy(t, k, chip, src_slot, dst_slot):
        src = h_refs[t].at[src_slot] if t < n else s_ref
        dst = (q_refs[t] if t < n else sq_ref).at[dst_slot]
        return pltpu.make_async_remote_copy(src_ref=src, dst_ref=dst, send_sem=send.at[3 * t + k],
                                            recv_sem=recv.at[3 * t + k], device_id=(*chip, c), device_id_type=MESH)

    own = [pltpu.make_async_copy(h_refs[t].at[me], q_refs[t].at[me], local.at[t]) for t in range(n)]
    if s_ref is not None:
        own.append(pltpu.make_async_copy(s_ref, sq_ref.at[me], local.at[n]))
    sends = [copy(t, k, (px, py), 2 * px + py, me) for t in range(nt) for k, (px, py) in enumerate(chips)]
    recvs = [copy(t, k, (px, py), me, 2 * px + py) for t in range(nt) for k, (px, py) in enumerate(chips)]
    return own, sends, recvs


def _scatter_operands(hs, sc):
    small = [] if sc is None else [sc]
    nt = len(hs) + len(small)
    shapes = [jax.ShapeDtypeStruct(h.shape, h.dtype) for h in hs] + [
        jax.ShapeDtypeStruct((N_CHIPS,) + a.shape, a.dtype) for a in small]
    sems = [pltpu.SemaphoreType.DMA((3 * nt,)), pltpu.SemaphoreType.DMA((3 * nt,)), pltpu.SemaphoreType.DMA((nt,))]
    return list(hs) + small, shapes, sems


def _small_allreduce(buf):
    def body(b_ref, o_ref, sib_buf, chip_buf, send, recv):
        x, y, c = lax.axis_index("x"), lax.axis_index("y"), lax.axis_index("c")
        me = 2 * x + y
        chips = [(1 - x, y), (x, 1 - y), (1 - x, 1 - y)]
        swap = pltpu.make_async_remote_copy(src_ref=b_ref, dst_ref=sib_buf, send_sem=send.at[0], recv_sem=recv.at[0],
                                            device_id=(x, y, 1 - c), device_id_type=MESH)
        swap.start()
        swap.wait()
        chip_buf[me] = b_ref[...] + sib_buf[...]

        def copy(k, chip, slot):
            return pltpu.make_async_remote_copy(src_ref=chip_buf.at[slot], dst_ref=chip_buf.at[slot], send_sem=send.at[1 + k],
                                                recv_sem=recv.at[1 + k], device_id=(*chip, c), device_id_type=MESH)

        sends = [copy(k, chip, me) for k, chip in enumerate(chips)]
        for cp in sends:
            cp.start()
        for k, (px, py) in enumerate(chips):
            copy(k, (px, py), 2 * px + py).wait_recv()
        for cp in sends:
            cp.wait_send()
        o_ref[...] = ((chip_buf[0] + chip_buf[1]) + chip_buf[2]) + chip_buf[3]

    vmem = pl.BlockSpec(memory_space=pltpu.VMEM)
    return _pallas_call(
        body, name="small_allreduce", in_specs=[vmem], out_specs=vmem,
        out_shape=jax.ShapeDtypeStruct(buf.shape, F32),
        scratch_shapes=[pltpu.VMEM(buf.shape, F32), pltpu.VMEM((N_CHIPS,) + buf.shape, F32),
                        pltpu.SemaphoreType.DMA((4,)), pltpu.SemaphoreType.DMA((4,))],
        compiler_params=_params(),
    )(buf)


def _reduce_allgather(qs, axes, sq, as_rows):
    n = len(qs)
    shard_shapes, half_axes = [], []
    for q, a, rows_form in zip(qs, axes, as_rows):
        shape = [d * 2 if i == a - 1 else d for i, d in enumerate(q.shape[1:])]
        if rows_form:
            assert a == 2
            shape = [shape[0], 1, shape[1]]
        shard_shapes.append(tuple(shape))
        half_axes.append(2 if rows_form else a - 1)

    def body(*refs):
        q_refs, sq_ref = refs[:n], refs[n]
        o_refs, st_ref = refs[n + 1:2 * n + 1], refs[2 * n + 1]
        send, recv = refs[2 * n + 2:]
        x, y, c = lax.axis_index("x"), lax.axis_index("y"), lax.axis_index("c")

        def swap(t, which):
            return pltpu.make_async_remote_copy(
                src_ref=_half(o_refs[t], half_axes[t], which), dst_ref=_half(o_refs[t], half_axes[t], which),
                send_sem=send.at[t], recv_sem=recv.at[t], device_id=(x, y, 1 - c), device_id_type=MESH)

        sent = []
        for t in range(n):
            q = q_refs[t]
            total = ((q[0].astype(F32) + q[1].astype(F32)) + q[2].astype(F32)) + q[3].astype(F32)
            if as_rows[t]:
                total = total.reshape(total.shape[0], 1, total.shape[1])
            _half(o_refs[t], half_axes[t], c)[...] = total
            cp = swap(t, c)
            cp.start()
            sent.append(cp)
        st_ref[...] = ((sq_ref[0] + sq_ref[1]) + sq_ref[2]) + sq_ref[3]
        for t in range(n):
            swap(t, 1 - c).wait_recv()
        for cp in sent:
            cp.wait_send()

    vmem = pl.BlockSpec(memory_space=pltpu.VMEM)
    return _pallas_call(
        body, name="grad_reduce_allgather",
        in_specs=[vmem] * (n + 1), out_specs=[vmem] * (n + 1),
        out_shape=[jax.ShapeDtypeStruct(s, F32) for s in shard_shapes] + [jax.ShapeDtypeStruct(sq.shape[1:], F32)],
        scratch_shapes=[pltpu.SemaphoreType.DMA((n,)), pltpu.SemaphoreType.DMA((n,))],
        compiler_params=_params(),
    )(*qs, sq)


def _dot_nt(a, b):
    return lax.dot_general(a, b, (((1,), (1,)), ((), ())), preferred_element_type=F32)


def _dot_tn(a, b):
    return lax.dot_general(a, b, (((0,), (0,)), ((), ())), preferred_element_type=F32)


def _mm_nt(a, bt, out_dtype, name, m, n, tm, tn, row_block=0, scale_first=None, gather=None):
    k = a.shape[1]
    shards, axes = gather if gather is not None else ([], [])
    ng = len(shards)
    grid = (m // tm, n // tn)

    def body(a_ref, b_ref, *rest):
        ins, o_ref, outs, sems = rest[:ng], rest[ng], rest[ng + 1:2 * ng + 1], rest[2 * ng + 1:]
        first = (pl.program_id(0) == 0) & (pl.program_id(1) == 0)
        last = (pl.program_id(0) == grid[0] - 1) & (pl.program_id(1) == grid[1] - 1)
        if ng:
            @pl.when(first)
            def _():
                _gather_start(ins, outs, axes, *sems)

        r = _dot_nt(a_ref[...], b_ref[...])
        if scale_first is not None:
            r = r * jnp.where(pl.program_id(1) == 0, scale_first, 1.0)
        o_ref[...] = r.astype(out_dtype)

        if ng:
            @pl.when(last)
            def _():
                _gather_finish(ins, outs, axes, *sems)

    res = _pallas_call(
        body, name=name, grid=grid,
        in_specs=[pl.BlockSpec((tm, k), lambda i, j: (i, 0)), pl.BlockSpec((tn, k), lambda i, j: (row_block + j, 0))]
        + [HBM] * ng,
        out_specs=[pl.BlockSpec((tm, tn), lambda i, j: (i, j))] + [HBM] * ng,
        out_shape=[jax.ShapeDtypeStruct((m, n), out_dtype)] + _gathered_shapes(shards),
        scratch_shapes=_gather_sems(ng) if ng else [],
        compiler_params=_params(("arbitrary", "arbitrary") if ng else ("parallel", "parallel")),
    )(a, bt, *_in_hbm(*shards))
    return (res[0], res[1:]) if ng else res[0]


def _mm_tn(a, b, name, k, tm, tn, tk, chunks=1, m=None, m_off=0, swap=None, scatter=None):
    m = a.shape[1] if m is None else m
    n = b.shape[1]
    cw = n // chunks
    srcs, axes = swap if swap is not None else ([], [])
    ns = len(srcs)
    hs, sc = scatter if scatter is not None else ([], None)
    sc_in, sc_shapes, sc_sems = _scatter_operands(hs, sc) if scatter is not None else ([], [], [])
    nh, nq = len(hs), len(sc_in)
    grid = (m // tm, n // tn, k // tk)

    def body(a_ref, b_ref, *rest):
        s_refs, h_refs = rest[:ns], rest[ns:ns + nq]
        o_ref = rest[ns + nq]
        d_refs, q_refs = rest[ns + nq + 1:2 * ns + nq + 1], rest[2 * ns + nq + 1:2 * ns + 2 * nq + 1]
        sems = rest[2 * ns + 2 * nq + 1:]
        swap_sems, scatter_sems = (sems[:2], sems[2:]) if ns else ((), sems)
        ids = [pl.program_id(d) for d in range(3)]

        def scatter_plan():
            small_in = h_refs[nh] if sc is not None else None
            small_out = q_refs[nh] if sc is not None else None
            return _chip_scatter_plan(h_refs[:nh], small_in, q_refs[:nh], small_out, *scatter_sems)

        if ns or nq:
            @pl.when((ids[0] == 0) & (ids[1] == 0) & (ids[2] == 0))
            def _():
                if ns:
                    for cp in _swap_copies(s_refs, d_refs, axes, *swap_sems):
                        cp.start()
                if nq:
                    own, sends, _ = scatter_plan()
                    for cp in own + sends:
                        cp.start()

        @pl.when(ids[2] == 0)
        def _():
            o_ref[...] = jnp.zeros_like(o_ref)
        r = _dot_tn(a_ref[...].astype(BF16), b_ref[...].astype(BF16))
        if chunks > 1:
            for c in range(chunks):
                o_ref[c] += r[:, c * cw:(c + 1) * cw]
        else:
            o_ref[...] += r

        if ns or nq:
            @pl.when((ids[0] == grid[0] - 1) & (ids[1] == grid[1] - 1) & (ids[2] == grid[2] - 1))
            def _():
                if ns:
                    for cp in _swap_copies(s_refs, d_refs, axes, *swap_sems):
                        cp.wait()
                if nq:
                    own, sends, recvs = scatter_plan()
                    for cp in recvs:
                        cp.wait_recv()
                    for cp in sends:
                        cp.wait_send()
                    for cp in own:
                        cp.wait()

    if chunks > 1:
        assert tn == n
        out_spec = pl.BlockSpec((chunks, tm, cw), lambda i, j, kk: (0, i, 0))
        out_shape = jax.ShapeDtypeStruct((chunks, m, cw), F32)
    else:
        out_spec = pl.BlockSpec((tm, tn), lambda i, j, kk: (i, j))
        out_shape = jax.ShapeDtypeStruct((m, n), F32)
    riders = ns + nq
    res = _pallas_call(
        body, name=name, grid=grid,
        in_specs=[pl.BlockSpec((tk, tm), lambda i, j, kk: (kk, m_off + i)), pl.BlockSpec((tk, tn), lambda i, j, kk: (kk, j))]
        + [HBM] * riders,
        out_specs=[out_spec] + [HBM] * riders, out_shape=[out_shape] + _swap_shapes(srcs, axes) + sc_shapes,
        scratch_shapes=([pltpu.SemaphoreType.DMA((ns,)), pltpu.SemaphoreType.DMA((ns,))] if ns else []) + sc_sems,
        compiler_params=_params(("arbitrary",) * 3 if riders else ("parallel", "parallel", "arbitrary")),
    )(a, b, *_in_hbm(*srcs, *sc_in))
    if not riders:
        return res[0]
    out = [res[0]]
    if ns:
        out.append(res[1:1 + ns])
    if nq:
        out.append((res[1 + ns:1 + ns + nh], res[1 + ns + nh] if sc is not None else None))
    return tuple(out)


def _tokens_spec():
    return pl.BlockSpec((ROW_TILE, D_MODEL), lambda i: (jnp.maximum(i - 1, 0), 0))


def _rmsnorm_fwd(x2, g1, lk, shards, axes):
    nb = x2.shape[0] // ROW_TILE + 1
    nblk = lk // ROW_TILE
    ng = len(shards)
    meta_cols = shards[-1].shape[1]

    def body(x_ref, g_ref, *rest):
        ins, hn_ref, outs = rest[:ng], rest[ng], rest[ng + 1:2 * ng + 1]
        sems, meta_buf, meta_sem = rest[2 * ng + 1:2 * ng + 6], rest[2 * ng + 6], rest[2 * ng + 7]
        s = pl.program_id(0)
        blk = (s + 1) % nblk

        @pl.when(s == 0)
        def _():
            _gather_start(ins, outs, axes, *sems)

        def normed(h):
            r = lax.rsqrt(jnp.mean(h * h, axis=-1, keepdims=True) + RMS_EPS)
            return ((h * r) * g_ref[...]).astype(BF16)

        @pl.when(s < nblk - 1)
        def _():
            hn_ref[...] = normed(jnp.where(blk >= nb, 0.0, x_ref[...]))

        @pl.when(s == nblk - 1)
        def _():
            _gather_finish(ins, outs, axes, *sems)
            fetch = pltpu.make_async_copy(outs[-1], meta_buf, meta_sem.at[0])
            fetch.start()
            fetch.wait()
            meta = jnp.concatenate([meta_buf[j] for j in range(N_CHIPS)], axis=1)
            hn_ref[...] = normed(jnp.concatenate([jnp.zeros((PAD_ROWS, D_MODEL), F32), meta], axis=0))

    res = _pallas_call(
        body, name="rmsnorm_fwd", grid=(nblk,),
        in_specs=[pl.BlockSpec((ROW_TILE, D_MODEL), lambda s: (jnp.clip((s + 1) % nblk - 1, 0, nb - 2), 0)),
                  pl.BlockSpec((1, D_MODEL), lambda s: (0, 0))] + [HBM] * ng,
        out_specs=[pl.BlockSpec((ROW_TILE, D_MODEL), lambda s: ((s + 1) % nblk, 0))] + [HBM] * ng,
        out_shape=[jax.ShapeDtypeStruct((lk, D_MODEL), BF16)] + _gathered_shapes(shards),
        scratch_shapes=_gather_sems(ng) + [pltpu.VMEM((N_CHIPS, N_META, meta_cols), F32), pltpu.SemaphoreType.DMA((1,))],
        compiler_params=_params(("arbitrary",)),
    )(x2, g1, *_in_hbm(*shards))
    return res[0], res[1:]


def _valid_gate_mask(i):
    row = i * ROW_TILE + lax.broadcasted_iota(jnp.int32, (ROW_TILE, F_COLS), 0)
    col = lax.broadcasted_iota(jnp.int32, (ROW_TILE, F_COLS), 1)
    return (row >= PAD_ROWS) & (col < N_HEADS)


def _gates_fwd(f, bfg):
    nb = f.shape[0] // ROW_TILE

    def body(f_ref, b_ref, c_ref, carry):
        i = pl.program_id(0)

        @pl.when(i == 0)
        def _():
            carry[...] = jnp.zeros_like(carry)

        logit = f_ref[...] + b_ref[...]
        lf = jnp.minimum(logit, 0.0) - jnp.log1p(jnp.exp(-jnp.abs(logit)))
        lf = jnp.where(_valid_gate_mask(i), lf, 0.0)
        r_i = lax.broadcasted_iota(jnp.int32, (ROW_TILE, ROW_TILE), 0)
        c_i = lax.broadcasted_iota(jnp.int32, (ROW_TILE, ROW_TILE), 1)
        tri = (c_i <= r_i).astype(F32)
        c_ref[...] = jnp.dot(tri, lf, precision=HIGHEST, preferred_element_type=F32) + carry[...]
        carry[...] = carry[...] + jnp.sum(lf, axis=0, keepdims=True)

    return _pallas_call(
        body, name="gates_fwd", grid=(nb,),
        in_specs=[pl.BlockSpec((ROW_TILE, F_COLS), lambda i: (i, 0)), pl.BlockSpec((1, F_COLS), lambda i: (0, 0))],
        out_specs=pl.BlockSpec((ROW_TILE, F_COLS), lambda i: (i, 0)),
        out_shape=jax.ShapeDtypeStruct(f.shape, F32),
        scratch_shapes=[pltpu.VMEM((1, F_COLS), F32)],
        compiler_params=_params(("arbitrary",)),
    )(f, bfg)


def _pool_counts(i):
    row = i * ROW_TILE + lax.broadcasted_iota(jnp.int32, (ROW_TILE, 1), 0)
    return jnp.maximum(row - PAD_ROWS, 0)


def _trailing_sums(xc, levels):
    acc = xc
    for lv in range(levels):
        acc = acc + pltpu.roll(acc, 1 << lv, 0)
    return acc


def _leading_sums(xc, levels):
    n = xc.shape[0]
    acc = xc
    for lv in range(levels):
        acc = acc + pltpu.roll(acc, n - (1 << lv), 0)
    return acc


def _pool_p(u_cur, u_prev, i):
    pos = _pool_counts(i)
    ps, invs = [], []
    for g, w in enumerate(POOL_WINDOWS):
        sl = slice(g * POOL_GROUP, (g + 1) * POOL_GROUP)
        cur = u_cur[:, sl]
        xc = jnp.concatenate([u_prev[:, sl], cur], axis=0)
        win = _trailing_sums(xc, g + 1)[ROW_TILE:, :]
        inv = 1.0 / jnp.minimum(pos + 1, w).astype(F32)
        ps.append(win * inv - cur)
        invs.append(inv)
    return ps, invs


def _pool_fwd(e, pw, scale):
    nb = e.shape[0] // ROW_TILE

    def body(uc_ref, up_ref, z_ref, pw_ref, sc_ref, y_ref):
        i = pl.program_id(0)
        u_cur = uc_ref[...].astype(F32)
        u_prev = jnp.where(i == 0, 0.0, up_ref[...].astype(F32))
        ps, _ = _pool_p(u_cur, u_prev, i)
        z = z_ref[...].astype(F32)
        gate = z * _sigmoid(z)
        for g in range(len(POOL_WINDOWS)):
            sl = slice(g * POOL_GROUP, (g + 1) * POOL_GROUP)
            yraw = jnp.dot(ps[g].astype(BF16), pw_ref[g], preferred_element_type=F32)
            y_ref[:, sl] = ((yraw * sc_ref[:, sl]) * gate[:, sl]).astype(BF16)

    blk = (ROW_TILE, POOL_WIDTH)
    return _pallas_call(
        body, name="pool_fwd", grid=(nb,),
        in_specs=[pl.BlockSpec(blk, lambda i: (i, 0)), pl.BlockSpec(blk, lambda i: (jnp.maximum(i - 1, 0), 0)),
                  pl.BlockSpec(blk, lambda i: (i, 1)),
                  pl.BlockSpec((len(POOL_WINDOWS), POOL_GROUP, POOL_GROUP), lambda i: (0, 0, 0)),
                  pl.BlockSpec((1, POOL_WIDTH), lambda i: (0, 0))],
        out_specs=pl.BlockSpec(blk, lambda i: (i, 0)),
        out_shape=jax.ShapeDtypeStruct((e.shape[0], POOL_WIDTH), BF16),
        compiler_params=_params(("parallel",)),
    )(e, e, e, pw, scale)


def _stack_heads(a):
    first = lax.broadcasted_iota(jnp.int32, a.shape, 1) < HEAD_DIM
    zero = jnp.zeros_like(a)
    return jnp.concatenate([jnp.where(first, a, zero), jnp.where(first, zero, a)], axis=0)


def _unstack_heads(a):
    rows = a.shape[0] // 2
    first = lax.broadcasted_iota(jnp.int32, (rows, LANES), 1) < HEAD_DIM
    return jnp.where(first, a[:rows], a[rows:])


def _causal(i, jj, stacked, kv_tile):
    r = lax.broadcasted_iota(jnp.int32, (stacked * ROW_TILE, kv_tile), 0)
    if stacked == 2:
        r = jnp.where(r >= ROW_TILE, r - ROW_TILE, r)
    kidx = jj * kv_tile + lax.broadcasted_iota(jnp.int32, (stacked * ROW_TILE, kv_tile), 1)
    return kidx <= i * ROW_TILE + r


def _stack_rows(b):
    n = b.shape[1]
    return jnp.concatenate([jnp.broadcast_to(b[0:1], (ROW_TILE, n)), jnp.broadcast_to(b[1:2], (ROW_TILE, n))], axis=0)


def _attn_fwd(qkv, bias, nb):
    lk = qkv.shape[0]
    lp = nb * ROW_TILE
    nkb = lk // KV_TILE
    n_pairs = N_HEADS // 2

    def body(q_ref, k_ref, v_ref, b_ref, o_ref, lse_ref):
        i = pl.program_id(1)
        qs = _stack_heads(q_ref[...])
        last = (i * ROW_TILE) // KV_TILE

        def block(jj, carry, masked, n_keys=KV_TILE):
            m, l, acc = carry
            rows = pl.ds(pl.multiple_of(jj * KV_TILE, KV_TILE), n_keys)
            s = _dot_nt(qs, k_ref[rows, :]) - _stack_rows(b_ref[0, 0, :, rows])
            if masked:
                s = jnp.where(_causal(i, jj, 2, KV_TILE)[:, :n_keys], s, NEG)
            m_new = jnp.maximum(m, jnp.max(s, axis=1, keepdims=True))
            alpha = jnp.exp(m - m_new)
            p = jnp.exp(s - m_new)
            l = alpha * l + jnp.sum(p, axis=1, keepdims=True)
            acc = alpha * acc + jnp.dot(p.astype(BF16), v_ref[rows, :], preferred_element_type=F32)
            return m_new, l, acc

        init = (jnp.full((2 * ROW_TILE, 1), NEG, F32), jnp.zeros((2 * ROW_TILE, 1), F32),
                jnp.zeros((2 * ROW_TILE, LANES), F32))
        def finish(carry):
            m, l, acc = carry
            o_ref[...] = _unstack_heads(acc / l)
            lse_ref[...] = _unstack_heads(jnp.broadcast_to(m + jnp.log(l), (2 * ROW_TILE, LANES)))

        carry = lax.fori_loop(0, last, lambda jj, c: block(jj, c, False), init)
        ends = [lambda c, n=(r + 1) * ROW_TILE: finish(block(last, c, True, n)) for r in range(KV_TILE // ROW_TILE)]
        lax.switch(i - last * (KV_TILE // ROW_TILE), ends, carry)

    return _pallas_call(
        body, name="attn_fwd", grid=(n_pairs, nb),
        in_specs=[pl.BlockSpec((ROW_TILE, LANES), lambda hp, i: (i, hp)),
                  pl.BlockSpec((lk, LANES), lambda hp, i: (0, n_pairs + hp)),
                  pl.BlockSpec((lk, LANES), lambda hp, i: (0, 2 * n_pairs + hp)),
                  pl.BlockSpec((1, 1, 2, lk), lambda hp, i: (i, hp, 0, 0))],
        out_specs=[pl.BlockSpec((ROW_TILE, LANES), lambda hp, i: (i, hp)),
                   pl.BlockSpec((ROW_TILE, LANES), lambda hp, i: (i, hp))],
        out_shape=[jax.ShapeDtypeStruct((lp, ATTN_WIDTH), F32), jax.ShapeDtypeStruct((lp, ATTN_WIDTH), F32)],
        compiler_params=_params(("parallel", "parallel")),
    )(qkv, qkv, qkv, bias)


def _merge_fwd(y_pool, o, e, wup_p, wup_a):
    lp = o.shape[0]
    nb = lp // ROW_TILE

    def body(yp_ref, o_ref, e_ref, wp_ref, wa_ref, mg_ref, ya_ref):
        za = e_ref[:, E_ZA:E_GP].astype(F32)
        ya = (o_ref[...] * (za * _sigmoid(za))).astype(BF16)
        ya_ref[...] = ya
        a_pool = jnp.dot(yp_ref[...], wp_ref[...], preferred_element_type=F32)
        a_attn = jnp.dot(ya, wa_ref[...], preferred_element_type=F32)
        mg_ref[...] = (_sigmoid(e_ref[:, E_GP:E_GA].astype(F32)) * a_pool
                       + _sigmoid(e_ref[:, E_GA:E_COLS].astype(F32)) * a_attn).astype(BF16)

    return _pallas_call(
        body, name="merge_fwd", grid=(nb,),
        in_specs=[pl.BlockSpec((ROW_TILE, POOL_WIDTH), lambda i: (i, 0)),
                  pl.BlockSpec((ROW_TILE, ATTN_WIDTH), lambda i: (i, 0)),
                  pl.BlockSpec((ROW_TILE, E_COLS), lambda i: (i, 0)),
                  pl.BlockSpec((POOL_WIDTH, D_MODEL), lambda i: (0, 0)),
                  pl.BlockSpec((ATTN_WIDTH, D_MODEL), lambda i: (0, 0))],
        out_specs=[pl.BlockSpec((ROW_TILE, D_MODEL), lambda i: (i, 0)),
                   pl.BlockSpec((ROW_TILE, ATTN_WIDTH), lambda i: (i, 0))],
        out_shape=[jax.ShapeDtypeStruct((lp, D_MODEL), BF16), jax.ShapeDtypeStruct((lp, ATTN_WIDTH), BF16)],
        compiler_params=_params(("parallel",)),
    )(y_pool, o, e, wup_p, wup_a)


def _head_fwd_bwd(merged, w_out, x2, metapad, gf, target):
    lp = merged.shape[0]
    nb = lp // ROW_TILE

    def body(mg_ref, w_ref, x_ref, mp_ref, g_ref, t_ref, dh_ref, loss_ref, dg_ref):
        i = pl.program_id(0)

        @pl.when(i == 0)
        def _():
            loss_ref[...] = jnp.zeros_like(loss_ref)
            dg_ref[...] = jnp.zeros_like(dg_ref)

        h0 = jnp.where(i == 0, mp_ref[...], x_ref[...])
        h1 = h0 + jnp.dot(mg_ref[...], w_ref[...], preferred_element_type=F32)
        r = lax.rsqrt(jnp.mean(h1 * h1, axis=-1, keepdims=True) + RMS_EPS)
        xhat = h1 * r
        g = g_ref[...]
        err = jnp.where(i == 0, 0.0, xhat * g - t_ref[...])
        loss_ref[...] += 0.5 * jnp.sum(jnp.mean(err * err, axis=-1, keepdims=True))
        dy = err / D_MODEL
        dg_ref[...] += jnp.sum(dy * xhat, axis=0, keepdims=True)
        dxhat = dy * g
        dh_ref[...] = r * (dxhat - xhat * jnp.mean(dxhat * xhat, axis=-1, keepdims=True))

    return _pallas_call(
        body, name="head_fwd_bwd", grid=(nb,),
        in_specs=[pl.BlockSpec((ROW_TILE, D_MODEL), lambda i: (i, 0)),
                  pl.BlockSpec((D_MODEL, D_MODEL), lambda i: (0, 0)),
                  _tokens_spec(), pl.BlockSpec((ROW_TILE, D_MODEL), lambda i: (0, 0)),
                  pl.BlockSpec((1, D_MODEL), lambda i: (0, 0)), _tokens_spec()],
        out_specs=[pl.BlockSpec((ROW_TILE, D_MODEL), lambda i: (i, 0)),
                   pl.BlockSpec((1, LANES), lambda i: (0, 0)), pl.BlockSpec((1, D_MODEL), lambda i: (0, 0))],
        out_shape=[jax.ShapeDtypeStruct((lp, D_MODEL), F32), jax.ShapeDtypeStruct((1, LANES), F32),
                   jax.ShapeDtypeStruct((1, D_MODEL), F32)],
        compiler_params=_params(("arbitrary",)),
    )(merged, w_out, x2, metapad, gf, target)


def _per_head_rowsum(t):
    head = lax.broadcasted_iota(jnp.int32, t.shape, 1) // HEAD_DIM
    out = jnp.zeros_like(t)
    for h in range(N_HEADS):
        sel = head == h
        out = jnp.where(sel, jnp.sum(jnp.where(sel, t, 0.0), axis=1, keepdims=True), out)
    return out


def _merge_bwd(dh1, w_out, y_pool, y_attn, wup_p, wup_a, e, o):
    lp = o.shape[0]
    nb = lp // ROW_TILE

    def body(dh_ref, wo_ref, yp_ref, ya_ref, wp_ref, wa_ref, e_ref, o_ref,
             dap_ref, daa_ref, dg_ref, do_ref, delta_ref, dyp_ref):
        dmerged = _dot_nt(dh_ref[...].astype(BF16), wo_ref[...])
        a_pool = jnp.dot(yp_ref[...], wp_ref[...], preferred_element_type=F32)
        a_attn = jnp.dot(ya_ref[...], wa_ref[...], preferred_element_type=F32)
        sp = _sigmoid(e_ref[:, E_GP:E_GA].astype(F32))
        sa = _sigmoid(e_ref[:, E_GA:E_COLS].astype(F32))
        dap = (dmerged * sp).astype(BF16)
        daa = (dmerged * sa).astype(BF16)
        dap_ref[...] = dap
        daa_ref[...] = daa
        dg_ref[:, ATTN_WIDTH:ATTN_WIDTH + D_MODEL] = (dmerged * a_pool * (sp * (1.0 - sp))).astype(BF16)
        dg_ref[:, ATTN_WIDTH + D_MODEL:] = (dmerged * a_attn * (sa * (1.0 - sa))).astype(BF16)
        dyp_ref[...] = _dot_nt(dap, wp_ref[...])
        dya = _dot_nt(daa, wa_ref[...])
        za = e_ref[:, E_ZA:E_GP].astype(F32)
        sz = _sigmoid(za)
        o = o_ref[...]
        do = dya * (za * sz)
        do_ref[...] = do.astype(BF16)
        dg_ref[:, :ATTN_WIDTH] = (dya * o * (sz * (1.0 + za * (1.0 - sz)))).astype(BF16)
        delta_ref[...] = _per_head_rowsum(do * o)

    row = lambda w: pl.BlockSpec((ROW_TILE, w), lambda i: (i, 0))
    full = lambda a: pl.BlockSpec(a.shape, lambda i: (0, 0))
    return _pallas_call(
        body, name="merge_bwd", grid=(nb,),
        in_specs=[row(D_MODEL), full(w_out), row(POOL_WIDTH), row(ATTN_WIDTH), full(wup_p), full(wup_a),
                  row(E_COLS), row(ATTN_WIDTH)],
        out_specs=[row(D_MODEL), row(D_MODEL), pl.BlockSpec((ROW_TILE, MAIN_COLS // 2), lambda i: (i, 1)),
                   row(ATTN_WIDTH), row(ATTN_WIDTH), row(POOL_WIDTH)],
        out_shape=[jax.ShapeDtypeStruct((lp, D_MODEL), BF16), jax.ShapeDtypeStruct((lp, D_MODEL), BF16),
                   jax.ShapeDtypeStruct((lp, MAIN_COLS), BF16),
                   jax.ShapeDtypeStruct((lp, ATTN_WIDTH), BF16), jax.ShapeDtypeStruct((lp, ATTN_WIDTH), F32),
                   jax.ShapeDtypeStruct((lp, POOL_WIDTH), F32)],
        compiler_params=_params(("parallel",)),
    )(dh1, w_out, y_pool, y_attn, wup_p, wup_a, e, o)


def _fill_dproj(dproj_half, du, dzp, dq, dk, dv):
    lp = dproj_half.shape[0]
    nb = lp // ROW_TILE

    def body(base_ref, du_ref, dzp_ref, dq_ref, dk_ref, dv_ref, o_ref):
        parts = [du_ref[...], dzp_ref[...], (dq_ref[...] * HEAD_DIM ** -0.5).astype(BF16), dk_ref[...], dv_ref[...]]
        for t, p in enumerate(parts):
            o_ref[:, t * ATTN_WIDTH:(t + 1) * ATTN_WIDTH] = p

    blk = pl.BlockSpec((ROW_TILE, ATTN_WIDTH), lambda i: (i, 0))
    return _pallas_call(
        body, name="fill_dproj", grid=(nb,),
        in_specs=[pl.BlockSpec(memory_space=pl.ANY)] + [blk] * 5,
        out_specs=pl.BlockSpec((ROW_TILE, MAIN_COLS // 2), lambda i: (i, 0)),
        out_shape=jax.ShapeDtypeStruct(dproj_half.shape, BF16),
        input_output_aliases={0: 0},
        compiler_params=_params(("parallel",)),
    )(dproj_half, du, dzp, dq, dk, dv)


def _pool_bwd_local(e, dy_pool, pw, scale):
    lp = e.shape[0]
    nb = lp // ROW_TILE
    ng = len(POOL_WINDOWS)

    def body(uc_ref, up_ref, z_ref, dy_ref, pw_ref, sc_ref, dpc_ref, dz_ref, dsc_ref, dpw_ref):
        i = pl.program_id(0)

        @pl.when(i == 0)
        def _():
            dsc_ref[...] = jnp.zeros_like(dsc_ref)
            dpw_ref[...] = jnp.zeros_like(dpw_ref)

        u_cur = uc_ref[...].astype(F32)
        u_prev = jnp.where(i == 0, 0.0, up_ref[...].astype(F32))
        ps, invs = _pool_p(u_cur, u_prev, i)
        z = z_ref[...].astype(F32)
        sz = _sigmoid(z)
        dy = dy_ref[...]
        dypre = dy * (z * sz)
        dsilu = sz * (1.0 + z * (1.0 - sz))
        for g in range(ng):
            sl = slice(g * POOL_GROUP, (g + 1) * POOL_GROUP)
            pb = ps[g].astype(BF16)
            w = pw_ref[g]
            yraw = jnp.dot(pb, w, preferred_element_type=F32)
            sc = sc_ref[:, sl]
            dz_ref[:, sl] = (dy[:, sl] * (yraw * sc) * dsilu[:, sl]).astype(BF16)
            dsc_ref[:, sl] += jnp.sum(dypre[:, sl] * yraw, axis=0, keepdims=True)
            dyraw = (dypre[:, sl] * sc).astype(BF16)
            dpw_ref[g] += _dot_tn(pb, dyraw)
            dpc_ref[:, sl] = _dot_nt(dyraw, w) * invs[g]

    blk = (ROW_TILE, POOL_WIDTH)
    return _pallas_call(
        body, name="pool_bwd_local", grid=(nb,),
        in_specs=[pl.BlockSpec(blk, lambda i: (i, 0)), pl.BlockSpec(blk, lambda i: (jnp.maximum(i - 1, 0), 0)),
                  pl.BlockSpec(blk, lambda i: (i, 1)), pl.BlockSpec(blk, lambda i: (i, 0)),
                  pl.BlockSpec((ng, POOL_GROUP, POOL_GROUP), lambda i: (0, 0, 0)),
                  pl.BlockSpec((1, POOL_WIDTH), lambda i: (0, 0))],
        out_specs=[pl.BlockSpec(blk, lambda i: (i, 0)), pl.BlockSpec(blk, lambda i: (i, 0)),
                   pl.BlockSpec((1, POOL_WIDTH), lambda i: (0, 0)),
                   pl.BlockSpec((ng, POOL_GROUP, POOL_GROUP), lambda i: (0, 0, 0))],
        out_shape=[jax.ShapeDtypeStruct((lp, POOL_WIDTH), F32), jax.ShapeDtypeStruct((lp, POOL_WIDTH), BF16),
                   jax.ShapeDtypeStruct((1, POOL_WIDTH), F32),
                   jax.ShapeDtypeStruct((ng, POOL_GROUP, POOL_GROUP), F32)],
        compiler_params=_params(("arbitrary",)),
    )(e, e, e, dy_pool, pw, scale)


def _pool_bwd_window(dpc):
    lp = dpc.shape[0]
    nb = lp // ROW_TILE

    def body(cur_ref, nxt_ref, du_ref):
        i = pl.program_id(0)
        cur = cur_ref[...]
        nxt = jnp.where(i == nb - 1, 0.0, nxt_ref[...])
        pos = _pool_counts(i)
        for g, w in enumerate(POOL_WINDOWS):
            sl = slice(g * POOL_GROUP, (g + 1) * POOL_GROUP)
            xc = jnp.concatenate([cur[:, sl], nxt[:, sl]], axis=0)
            win = _leading_sums(xc, g + 1)[:ROW_TILE, :]
            dp = cur[:, sl] * jnp.minimum(pos + 1, w).astype(F32)
            du_ref[:, sl] = (win - dp).astype(BF16)

    blk = (ROW_TILE, POOL_WIDTH)
    return _pallas_call(
        body, name="pool_bwd_window", grid=(nb,),
        in_specs=[pl.BlockSpec(blk, lambda i: (i, 0)), pl.BlockSpec(blk, lambda i: (jnp.minimum(i + 1, nb - 1), 0))],
        out_specs=pl.BlockSpec(blk, lambda i: (i, 0)),
        out_shape=jax.ShapeDtypeStruct((lp, POOL_WIDTH), BF16),
        compiler_params=_params(("parallel",)),
    )(dpc, dpc)


def _attn_bwd(qkv, do, lse, delta, bias, nb):
    lk = qkv.shape[0]
    lp = nb * ROW_TILE
    nkb = lk // KV_TILE_BWD
    n_pairs = N_HEADS // 2
    per_kv = KV_TILE_BWD // ROW_TILE

    def body(q_ref, k_ref, v_ref, do_ref, lse_ref, dl_ref, b_ref, dq_ref, dk_ref, dv_ref, dc_ref, dcq_ref,
             dk_acc, dv_acc, dc_acc):
        jj = pl.program_id(1)

        @pl.when(jj == 0)
        def _():
            dq_ref[...] = jnp.zeros_like(dq_ref)
            dcq_ref[...] = jnp.zeros_like(dcq_ref)

        dk_acc[...] = jnp.zeros_like(dk_acc)
        dv_acc[...] = jnp.zeros_like(dv_acc)
        dc_acc[...] = jnp.zeros_like(dc_acc)

        def block(i, n_keys, masked):
            kb, vb = k_ref[:n_keys, :], v_ref[:n_keys, :]
            rows = pl.ds(pl.multiple_of(i * ROW_TILE, ROW_TILE), ROW_TILE)
            qs = _stack_heads(q_ref[rows, :])
            dos = _stack_heads(do_ref[rows, :])
            lse_i, dl_i = lse_ref[rows, :], dl_ref[rows, :]
            s = _dot_nt(qs, kb)
            dp = _dot_nt(dos, vb)
            if masked:
                valid = _causal(i, jj, 1, KV_TILE_BWD)[:, :n_keys]
            ps, dss, dcs, rowsums = [], [], [], []
            for hd in range(2):
                half = slice(hd * ROW_TILE, (hd + 1) * ROW_TILE)
                col = slice(hd * HEAD_DIM, hd * HEAD_DIM + 1)
                sh = s[half] - b_ref[i, 0, hd:hd + 1, :n_keys]
                if masked:
                    sh = jnp.where(valid, sh, NEG)
                p = jnp.exp(sh - lse_i[:, col])
                ds = p * (dp[half] - dl_i[:, col])
                ps.append(p.astype(BF16))
                dss.append(ds.astype(BF16))
                dcs.append(jnp.sum(ds, axis=0, keepdims=True))
                rowsums.append(jnp.sum(ds, axis=1, keepdims=True))
            dsb = jnp.concatenate(dss, axis=0)
            dv_acc[:n_keys, :] += _dot_tn(jnp.concatenate(ps, axis=0), dos)
            dk_acc[:n_keys, :] += _dot_tn(dsb, qs)
            dc_acc[:, :n_keys] -= jnp.concatenate(dcs, axis=0)
            dq_ref[rows, :] += _unstack_heads(jnp.dot(dsb, kb, preferred_element_type=F32))
            dcq_ref[rows, :] += _unstack_heads(jnp.broadcast_to(jnp.concatenate(rowsums, axis=0), (2 * ROW_TILE, LANES)))

        first_q = per_kv * jj
        for r in range(per_kv):
            @pl.when(first_q + r < nb)
            def _():
                block(first_q + r, (r + 1) * ROW_TILE, True)

        def rest(i, carry):
            block(i, KV_TILE_BWD, False)
            return carry

        lax.fori_loop(jnp.minimum(first_q + per_kv, nb), nb, rest, 0)
        dk_ref[...] = dk_acc[...].astype(BF16)
        dv_ref[...] = dv_acc[...].astype(BF16)
        dc_ref[0, 0] = dc_acc[...]

    whole = lambda rows: pl.BlockSpec((rows, LANES), lambda hp, jj: (0, hp))
    kv_blk = lambda off: pl.BlockSpec((KV_TILE_BWD, LANES), lambda hp, jj: (jj, off + hp))
    return _pallas_call(
        body, name="attn_bwd", grid=(n_pairs, nkb),
        in_specs=[whole(lk), kv_blk(n_pairs), kv_blk(2 * n_pairs), whole(lp), whole(lp), whole(lp),
                  pl.BlockSpec((nb, 1, 2, KV_TILE_BWD), lambda hp, jj: (0, hp, 0, jj))],
        out_specs=[whole(lp), kv_blk(0), kv_blk(0),
                   pl.BlockSpec((1, 1, 2, KV_TILE_BWD), lambda hp, jj: (hp, jj, 0, 0)), whole(lp)],
        out_shape=[jax.ShapeDtypeStruct((lp, ATTN_WIDTH), F32), jax.ShapeDtypeStruct((lk, ATTN_WIDTH), BF16),
                   jax.ShapeDtypeStruct((lk, ATTN_WIDTH), BF16),
                   jax.ShapeDtypeStruct((n_pairs, nkb, 2, KV_TILE_BWD), F32),
                   jax.ShapeDtypeStruct((lp, ATTN_WIDTH), F32)],
        scratch_shapes=[pltpu.VMEM((KV_TILE_BWD, LANES), F32), pltpu.VMEM((KV_TILE_BWD, LANES), F32),
                        pltpu.VMEM((2, KV_TILE_BWD), F32)],
        compiler_params=_params(("parallel", "arbitrary")),
    )(qkv, qkv, qkv, do, lse, delta, bias)


def _gates_bwd(dc, dcq, f, bfg):
    nb = f.shape[0] // ROW_TILE

    def body(dc_ref, dcq_ref, f_ref, b_ref, df_ref, db_ref, carry):
        step = pl.program_id(0)
        i = nb - 1 - step

        @pl.when(step == 0)
        def _():
            carry[...] = jnp.zeros_like(carry)
            db_ref[...] = jnp.zeros_like(db_ref)

        dcb = dc_ref[...]
        lane = lax.broadcasted_iota(jnp.int32, (ROW_TILE, F_COLS), 1)
        for h in range(N_HEADS):
            dcb = dcb + jnp.where(lane == h, dcq_ref[:, HEAD_DIM * h:HEAD_DIM * h + 1], 0.0)
        r_i = lax.broadcasted_iota(jnp.int32, (ROW_TILE, ROW_TILE), 0)
        c_i = lax.broadcasted_iota(jnp.int32, (ROW_TILE, ROW_TILE), 1)
        upper = (c_i >= r_i).astype(F32)
        dlf = jnp.dot(upper, dcb, precision=HIGHEST, preferred_element_type=F32) + carry[...]
        carry[...] = carry[...] + jnp.sum(dcb, axis=0, keepdims=True)
        logit = f_ref[...] + b_ref[...]
        dlogit = jnp.where(_valid_gate_mask(i), dlf * _sigmoid(-logit), 0.0)
        df_ref[...] = dlogit.astype(BF16)
        db_ref[...] += jnp.sum(dlogit, axis=0, keepdims=True)

    blk = pl.BlockSpec((ROW_TILE, F_COLS), lambda s: (nb - 1 - s, 0))
    wide = pl.BlockSpec((ROW_TILE, ATTN_WIDTH), lambda s: (nb - 1 - s, 0))
    one = pl.BlockSpec((1, F_COLS), lambda s: (0, 0))
    return _pallas_call(
        body, name="gates_bwd", grid=(nb,),
        in_specs=[blk, wide, blk, one], out_specs=[blk, one],
        out_shape=[jax.ShapeDtypeStruct(f.shape, BF16), jax.ShapeDtypeStruct((1, F_COLS), F32)],
        scratch_shapes=[pltpu.VMEM((1, F_COLS), F32)],
        compiler_params=_params(("arbitrary",)),
    )(dc, dcq, f, bfg)


def _input_bwd(dproj, df, wt_e, wt_qkv, wt_f, x2, metapad, dh1, g1, hs, sc=None):
    nb = x2.shape[0] // ROW_TILE + 1
    n = len(hs)
    sc_in, sc_shapes, sc_sems = _scatter_operands(hs, sc)
    nq = len(sc_in)

    def body(dp_ref, df_ref, we_ref, w_ref, wf_ref, x_ref, mp_ref, dh_ref, g_ref, *rest):
        h_refs, s_ref = rest[:n], (rest[n] if sc is not None else None)
        gx_ref, g0_ref, dg_ref = rest[nq:nq + 3]
        q_refs, sq_ref = rest[nq + 3:nq + 3 + n], (rest[nq + 3 + n] if sc is not None else None)
        sems = rest[2 * nq + 3:]
        i = pl.program_id(0)

        @pl.when(i == 0)
        def _():
            dg_ref[...] = jnp.zeros_like(dg_ref)
            own, sends, _ = _chip_scatter_plan(h_refs, s_ref, q_refs, sq_ref, *sems)
            for cp in own + sends:
                cp.start()

        dhn = (jnp.dot(dp_ref[:, :E_ZA], we_ref[:E_ZA, :], preferred_element_type=F32)
               + jnp.dot(dp_ref[:, E_ZA:MAIN_COLS // 2], w_ref[...], preferred_element_type=F32)
               + jnp.dot(dp_ref[:, MAIN_COLS // 2:], we_ref[E_ZA:, :], preferred_element_type=F32)
               + jnp.dot(df_ref[...], wf_ref[...], preferred_element_type=F32))
        h0 = jnp.where(i == 0, mp_ref[...], x_ref[...])
        r = lax.rsqrt(jnp.mean(h0 * h0, axis=-1, keepdims=True) + RMS_EPS)
        xhat = h0 * r
        dg_ref[...] += jnp.sum(dhn * xhat, axis=0, keepdims=True)
        dxhat = dhn * g_ref[...]
        dh0 = dh_ref[...] + r * (dxhat - xhat * jnp.mean(dxhat * xhat, axis=-1, keepdims=True))
        gx_ref[...] = dh0

        @pl.when(i == 0)
        def _():
            g0_ref[...] = dh0

        @pl.when(i == nb - 1)
        def _():
            own, sends, recvs = _chip_scatter_plan(h_refs, s_ref, q_refs, sq_ref, *sems)
            for cp in recvs:
                cp.wait_recv()
            for cp in sends:
                cp.wait_send()
            for cp in own:
                cp.wait()

    const = lambda shape: pl.BlockSpec(shape, lambda i: (0, 0))
    res = _pallas_call(
        body, name="input_bwd", grid=(nb,),
        in_specs=[pl.BlockSpec((ROW_TILE, MAIN_COLS), lambda i: (i, 0)), pl.BlockSpec((ROW_TILE, F_COLS), lambda i: (i, 0)),
                  const((E_COLS, D_MODEL)), const((QKV_COLS, D_MODEL)), const((F_COLS, D_MODEL)),
                  _tokens_spec(), const((ROW_TILE, D_MODEL)),
                  pl.BlockSpec((ROW_TILE, D_MODEL), lambda i: (i, 0)), const((1, D_MODEL))] + [HBM] * nq,
        out_specs=[_tokens_spec(), const((ROW_TILE, D_MODEL)), const((1, D_MODEL))] + [HBM] * nq,
        out_shape=[jax.ShapeDtypeStruct(x2.shape, F32), jax.ShapeDtypeStruct((ROW_TILE, D_MODEL), F32),
                   jax.ShapeDtypeStruct((1, D_MODEL), F32)] + sc_shapes,
        scratch_shapes=sc_sems,
        compiler_params=_params(("arbitrary",), vmem=56 * 1024 * 1024),
    )(dproj, df, wt_e, wt_qkv, wt_f, x2, metapad, dh1, g1, *_in_hbm(*sc_in))
    return res[:3], res[3:3 + n], (res[3 + n] if sc is not None else None)


def _adamw(w, g, m, v, name):
    rows, cols = w.shape
    if rows % 8 == 0:
        tr, tc = _row_tile8(rows), cols
    else:
        tr, tc = rows, (2 * LANES if cols % (2 * LANES) == 0 and rows > 8 else cols)

    def body(w_ref, g_ref, m_ref, v_ref, d_ref, mo_ref, vo_ref):
        g_ = g_ref[...]
        m_new = ADAM_B1 * m_ref[...] + (1.0 - ADAM_B1) * g_
        v_new = ADAM_B2 * v_ref[...] + (1.0 - ADAM_B2) * (g_ * g_)
        m_hat = m_new / (1.0 - ADAM_B1 ** ADAM_STEP)
        v_hat = v_new / (1.0 - ADAM_B2 ** ADAM_STEP)
        d_ref[...] = -ADAM_LR * (m_hat / (jnp.sqrt(v_hat) + ADAM_EPS) + ADAM_WD * w_ref[...])
        mo_ref[...] = m_new
        vo_ref[...] = v_new

    blk = pl.BlockSpec((tr, tc), lambda i, j: (i, j))
    return _pallas_call(
        body, name=name, grid=(rows // tr, cols // tc),
        in_specs=[blk] * 4, out_specs=[blk] * 3,
        out_shape=[jax.ShapeDtypeStruct(w.shape, F32)] * 3,
        compiler_params=_params(("parallel", "parallel")),
    )(w, g, m, v)


def _adamw_native(w3, g3, m3, v3, name):
    rows = w3.shape[0]
    tr = rows // 2
    blk = pl.BlockSpec((tr,) + w3.shape[1:], lambda i: (i, 0, 0))
    shape = jax.ShapeDtypeStruct(w3.shape, F32)

    def moments(g_ref, m_ref, v_ref, mo_ref, vo_ref):
        g_ = g_ref[...]
        mo_ref[...] = ADAM_B1 * m_ref[...] + (1.0 - ADAM_B1) * g_
        vo_ref[...] = ADAM_B2 * v_ref[...] + (1.0 - ADAM_B2) * (g_ * g_)

    new_m, new_v = _pallas_call(
        moments, name=name + "_moments", grid=(2,), in_specs=[blk] * 3, out_specs=[blk] * 2, out_shape=[shape] * 2,
        compiler_params=_params(("parallel",)),
    )(g3, m3, v3)

    def delta(w_ref, m_ref, v_ref, d_ref):
        m_hat = m_ref[...] / (1.0 - ADAM_B1 ** ADAM_STEP)
        v_hat = v_ref[...] / (1.0 - ADAM_B2 ** ADAM_STEP)
        d_ref[...] = -ADAM_LR * (m_hat / (jnp.sqrt(v_hat) + ADAM_EPS) + ADAM_WD * w_ref[...])

    d = _pallas_call(
        delta, name=name + "_delta", grid=(2,), in_specs=[blk] * 3, out_specs=blk, out_shape=shape,
        compiler_params=_params(("parallel",)),
    )(w3, new_m, new_v)
    return d, new_m, new_v


def _row_tile8(rows):
    best = rows
    for t in range(8, 257, 8):
        if rows % t == 0:
            best = t
    return best


def kernel(x, meta_tokens, norm_g, w_in, b_forget, pool_w, pool_scale, w_up_pool, w_up_attn, w_out, final_norm_g, loss_target, m_meta_tokens, m_norm_g, m_w_in, m_b_forget, m_pool_w, m_pool_scale, m_w_up_pool, m_w_up_attn, m_w_out, m_final_norm_g, v_meta_tokens, v_norm_g, v_w_in, v_b_forget, v_pool_w, v_pool_scale, v_w_up_pool, v_w_up_attn, v_w_out, v_final_norm_g):
    seq = x.shape[1]
    assert seq % ROW_TILE == 0 and x.shape[0] == 1
    lp = seq + ROW_TILE
    nb = lp // ROW_TILE
    lk = -(-lp // KV_TILE_BWD) * KV_TILE_BWD
    core = jnp.reshape(lax.axis_index("c"), (1,)).astype(jnp.int32)
    x2 = x[0]
    target = loss_target[0]
    sh_d = D_MODEL // N_CHIPS

    to_rows = lambda a: jnp.transpose(a, (2, 0, 1))
    from_rows = lambda a: jnp.transpose(a, (1, 2, 0))
    gf = final_norm_g.reshape(1, D_MODEL)
    bfg = jnp.pad(b_forget, ((0, 0), (0, F_COLS - N_HEADS)))
    pw_b = pool_w[0].astype(BF16)
    hn, (wg_in, meta_g) = _rmsnorm_fwd(x2, norm_g, lk, [jnp.transpose(w_in[0]).astype(BF16), meta_tokens], [1, 0])
    wt = wg_in.reshape(-1, D_MODEL)
    wt_e = jnp.concatenate([wt[REF_U:REF_Q], wt[REF_ZA:REF_F], wt[REF_GP:]], axis=0)
    wt_qkv = wt[REF_Q:REF_ZA]
    wt_f = jnp.pad(wt[REF_F:REF_GP], ((0, F_COLS - N_HEADS), (0, 0)))
    meta_full = jnp.transpose(meta_g, (1, 0, 2)).reshape(N_META, D_MODEL)
    metapad = jnp.pad(meta_full, ((PAD_ROWS, 0), (0, 0)))

    tm = _row_tile(lp, 2200)
    e, (wg_up_p, wg_up_a, wg_out) = _mm_nt(
        hn, wt_e, BF16, "in_proj_gates", lp, E_COLS, tm, 512,
        gather=([w_up_pool[0].astype(BF16), w_up_attn[0].astype(BF16), w_out[0].astype(BF16)], [0, 0, 0]))
    wup_p = jnp.transpose(wg_up_p, (1, 0, 2)).reshape(POOL_WIDTH, D_MODEL)
    wup_a = jnp.transpose(wg_up_a, (1, 0, 2)).reshape(ATTN_WIDTH, D_MODEL)
    wout = wg_out.reshape(D_MODEL, D_MODEL)
    qkv = _mm_nt(hn, wt_qkv, BF16, "in_proj_qkv", lk, QKV_COLS, _row_tile(lk, 2600), 512, scale_first=HEAD_DIM ** -0.5)
    f = _mm_nt(hn, wt_f, F32, "in_proj_forget", lp, F_COLS, tm, F_COLS)
    c = _gates_fwd(f, bfg)
    c_t = jnp.transpose(c[:, :N_HEADS])
    c_first = jnp.transpose(c_t[:, ::ROW_TILE]).reshape(nb, N_HEADS // 2, 2, 1)
    c_keys = jnp.pad(c_t, ((0, 0), (0, lk - lp))).reshape(1, N_HEADS // 2, 2, lk)
    bias = jnp.where(jnp.arange(lk) < PAD_ROWS, -NEG, c_keys - c_first)
    y_pool = _pool_fwd(e, pw_b, pool_scale)
    o, lse = _attn_fwd(qkv, bias, nb)
    merged, y_attn = _merge_fwd(y_pool, o, e, wup_p, wup_a)
    dh1, loss_part, dgf = _head_fwd_bwd(merged, wout, x2, metapad, gf, target)

    dap, daa, dproj_half, do, delta, dy_pool = _merge_bwd(dh1, wout, y_pool, y_attn, wup_p, wup_a, e, o)
    dpc, dzp, dscale, dpw = _pool_bwd_local(e, dy_pool, pw_b, pool_scale)
    du = _pool_bwd_window(dpc)
    dq, dk, dv, dc4, dcq = _attn_bwd(qkv, do, lse, delta, bias, nb)
    dc = jnp.transpose(dc4, (1, 3, 0, 2)).reshape(lk, N_HEADS)[:lp]
    df, db = _gates_bwd(jnp.pad(dc, ((0, 0), (0, F_COLS - N_HEADS))), dcq, f, bfg)
    dproj = _fill_dproj(dproj_half, du, dzp, dq, dk, dv)
    tk = _row_tile(lp, 1100)
    dw_out = _mm_tn(merged, dh1, "grad_w_out", lp, 512, D_MODEL, tk)
    dw_up_p = _mm_tn(y_pool, dap, "grad_w_up_pool", lp, 512, D_MODEL, lp, chunks=N_CHIPS)
    dw_up_a = _mm_tn(y_attn, daa, "grad_w_up_attn", lp, 512, D_MODEL, lp, chunks=N_CHIPS)
    dwt_f = _mm_tn(df, hn, "grad_w_in_forget", lp, F_COLS, D_MODEL, tk)

    def pad8(a):
        return jnp.pad(a, ((0, (-a.shape[0]) % 8), (0, 0)))

    small_parts = [pad8(a) for a in (dgf.reshape(-1, LANES), dscale.reshape(-1, LANES), db, dpw.reshape(-1, LANES),
                                     loss_part)]
    small = jnp.concatenate(small_parts, axis=0)
    soffs = [0]
    for p in small_parts:
        soffs.append(soffs[-1] + p.shape[0])

    gs_rows = [dw_up_p, dw_up_a, dw_out.reshape(N_CHIPS, sh_d, D_MODEL)]
    half = MAIN_COLS // 2
    dwt_a, (*rb_rows, rb_f, sr) = _mm_tn(dproj, hn, "grad_w_in_a", lp, half // 4, D_MODEL, lp, m=half, m_off=0,
                                         swap=(gs_rows + [dwt_f, small], [1, 1, 1, 1, None]))
    *hs_rows, sc = _add_sibling_half(gs_rows, rb_rows, [1, 1, 1], small, sr, core)
    dwt_b, (rb_a,), (qs_rows, sq) = _mm_tn(dproj, hn, "grad_w_in_b", lp, half // 4, D_MODEL, lp, m=half, m_off=4,
                                           swap=([dwt_a], [1]), scatter=(hs_rows, sc))
    (rb_b,) = _sibling_exchange([dwt_b], [1])
    h_in = _add_halves_w_in([dwt_a, dwt_b, dwt_f], [rb_a, rb_b, rb_f], core)
    grad_axes = [2, 1, 1, 1]
    (grad_x, g_block0, dg1), (q_in,), _ = _input_bwd(dproj, df, wt_e, wt_qkv, wt_f, x2, metapad, dh1, norm_g, [h_in])
    g_w_in_rows, g_w_up_p, g_w_up_a, g_w_out, st = _reduce_allgather([q_in] + list(qs_rows), grad_axes, sq,
                                                                       [True, False, False, False])
    late = _small_allreduce(jnp.concatenate([pad8(dg1.reshape(-1, LANES)), g_block0[PAD_ROWS:].reshape(-1, LANES)], axis=0))
    n_norm = D_MODEL // LANES
    g_norm = late[:n_norm].reshape(1, D_MODEL)
    chip = 2 * lax.axis_index("x") + lax.axis_index("y")
    g_meta = lax.dynamic_slice_in_dim(late[n_norm:].reshape(N_META, D_MODEL), chip * sh_d, sh_d, axis=1)

    spiece = lambda k, rows: st[soffs[k]:soffs[k] + rows]
    g_final = spiece(0, D_MODEL // LANES).reshape(1, D_MODEL)
    g_scale = spiece(1, POOL_WIDTH // LANES).reshape(1, POOL_WIDTH)
    g_bf = spiece(2, 1)
    g_pw = spiece(3, POOL_WIDTH)
    loss = st[soffs[4], 0]

    def pad_lanes(a):
        return jnp.pad(a, ((0, 0), (0, F_COLS - N_HEADS)))

    w_in_res = (g_w_in_rows,) + _adamw_native(to_rows(w_in), g_w_in_rows, to_rows(m_w_in), to_rows(v_w_in), "adamw_w_in")

    upd = [
        ("meta_tokens", meta_tokens, g_meta, m_meta_tokens, v_meta_tokens),
        ("norm_g", norm_g, g_norm, m_norm_g, v_norm_g),
        ("w_in", None, None, None, None),
        ("b_forget", pad_lanes(b_forget), g_bf, pad_lanes(m_b_forget), pad_lanes(v_b_forget)),
        ("pool_w", pool_w.reshape(-1, LANES), g_pw, m_pool_w.reshape(-1, LANES), v_pool_w.reshape(-1, LANES)),
        ("pool_scale", pool_scale, g_scale, m_pool_scale, v_pool_scale),
        ("w_up_pool", w_up_pool[0], g_w_up_p, m_w_up_pool[0], v_w_up_pool[0]),
        ("w_up_attn", w_up_attn[0], g_w_up_a, m_w_up_attn[0], v_w_up_attn[0]),
        ("w_out", w_out[0], g_w_out, m_w_out[0], v_w_out[0]),
        ("final_norm_g", gf, g_final, m_final_norm_g.reshape(1, D_MODEL), v_final_norm_g.reshape(1, D_MODEL)),
    ]
    shapes = [meta_tokens.shape, norm_g.shape, w_in.shape, b_forget.shape, pool_w.shape, pool_scale.shape,
              w_up_pool.shape, w_up_attn.shape, w_out.shape, final_norm_g.shape]
    grads, deltas, new_ms, new_vs = [], [], [], []
    for (name, w_, g_, m_in, v_in), shp in zip(upd, shapes):
        if name == "w_in":
            res = tuple(from_rows(a) for a in w_in_res)
        else:
            d_, mn_, vn_ = _adamw(w_, g_, m_in, v_in, "adamw_" + name)
            res = (g_, d_, mn_, vn_)
        if name == "b_forget":
            res = tuple(a[:, :N_HEADS] for a in res)
        for lst, a in zip((grads, deltas, new_ms, new_vs), res):
            lst.append(a.reshape(shp))

    return (loss, grad_x.reshape(x.shape), *grads, *deltas, *new_ms, *new_vs)
```

```python
import jax
import jax.numpy as jnp
from jax import lax
from jax.experimental import pallas as pl
from jax.experimental.pallas import tpu as pltpu

F32 = jnp.float32
BF16 = jnp.bfloat16
MESH = pl.DeviceIdType.MESH
HIGHEST = lax.Precision.HIGHEST
HBM = pl.BlockSpec(memory_space=pltpu.HBM)

D_MODEL = 1024
N_META = 16
POOL_WIDTH = 512
POOL_GROUP = 128
POOL_WINDOWS = (2, 4, 8, 16)
N_HEADS = 8
HEAD_DIM = 64
ATTN_WIDTH = 512
RMS_EPS = 1e-6
N_CHIPS = 4

ADAM_LR = 0.001
ADAM_B1 = 0.9
ADAM_B2 = 0.999
ADAM_EPS = 1e-08
ADAM_WD = 0.01
ADAM_STEP = 10

LANES = 128
ROW_TILE = 256
KV_TILE = 1024
KV_TILE_BWD = 1024
PAD_ROWS = ROW_TILE - N_META
NEG = -1e30

E_U, E_ZP, E_ZA, E_GP, E_GA, E_COLS = 0, 512, 1024, 1536, 2560, 3584
QKV_COLS = 3 * ATTN_WIDTH
MAIN_COLS = E_COLS + QKV_COLS
F_COLS = LANES
REF_U, REF_Q, REF_ZA, REF_F, REF_GP = 0, 1024, 2560, 3072, 3080
VMEM_LIMIT = 48 * 1024 * 1024


def _params(sem=None, vmem=VMEM_LIMIT):
    return pltpu.CompilerParams(dimension_semantics=sem, vmem_limit_bytes=vmem)


def _in_hbm(*arrays):
    return [pltpu.with_memory_space_constraint(a, pltpu.HBM) for a in arrays]


def _row_tile(n, target):
    best = 16
    for t in range(16, target + 1, 16):
        if n % t == 0:
            best = t
    return best


def _sigmoid(x):
    return 1.0 / (1.0 + jnp.exp(-x))


def _half(ref, axis, which):
    n = ref.shape[axis] // 2
    idx = [slice(None)] * len(ref.shape)
    idx[axis] = pl.ds(pl.multiple_of(which * n, n), n)
    return ref.at[tuple(idx)]


def _half_shape(shape, axis):
    s = list(shape)
    s[axis] //= 2
    return tuple(s)


def _gather_start(ins, outs, axes, send, recv, fsend, frecv, local):
    x, y, c = lax.axis_index("x"), lax.axis_index("y"), lax.axis_index("c")
    me = 2 * x + y
    for t in range(len(ins)):
        pltpu.make_async_copy(ins[t], outs[t].at[me], local.at[t]).start()
        for k, chip in enumerate([(1 - x, y), (x, 1 - y), (1 - x, 1 - y)]):
            pltpu.make_async_remote_copy(
                src_ref=_half(ins[t], axes[t], c), dst_ref=_half(outs[t].at[me], axes[t], c),
                send_sem=send.at[3 * t + k], recv_sem=recv.at[3 * t + k], device_id=(*chip, c), device_id_type=MESH).start()


def _gather_finish(ins, outs, axes, send, recv, fsend, frecv, local):
    x, y, c = lax.axis_index("x"), lax.axis_index("y"), lax.axis_index("c")
    me = 2 * x + y
    sibling = (x, y, 1 - c)
    chips = [(1 - x, y), (x, 1 - y), (1 - x, 1 - y)]
    n = len(ins)

    def over_ici(t, k, chip, dst_slot):
        return pltpu.make_async_remote_copy(
            src_ref=_half(ins[t], axes[t], c), dst_ref=_half(outs[t].at[dst_slot], axes[t], c),
            send_sem=send.at[3 * t + k], recv_sem=recv.at[3 * t + k], device_id=(*chip, c), device_id_type=MESH)

    def to_sibling(t, k, slot, which):
        return pltpu.make_async_remote_copy(
            src_ref=_half(outs[t].at[slot], axes[t], which), dst_ref=_half(outs[t].at[slot], axes[t], which),
            send_sem=fsend.at[3 * t + k], recv_sem=frecv.at[3 * t + k], device_id=sibling, device_id_type=MESH)

    forwards = []
    for t in range(n):
        for k, (px, py) in enumerate(chips):
            over_ici(t, k, (px, py), 2 * px + py).wait_recv()
            fw = to_sibling(t, k, 2 * px + py, c)
            fw.start()
            forwards.append(fw)
    for t in range(n):
        for k, (px, py) in enumerate(chips):
            to_sibling(t, k, 2 * px + py, 1 - c).wait_recv()
    for t in range(n):
        for k, chip in enumerate(chips):
            over_ici(t, k, chip, me).wait_send()
    for fw in forwards:
        fw.wait_send()
    for t in range(n):
        pltpu.make_async_copy(ins[t], outs[t].at[me], local.at[t]).wait()


def _gather_sems(n):
    return [pltpu.SemaphoreType.DMA((3 * n,)), pltpu.SemaphoreType.DMA((3 * n,)), pltpu.SemaphoreType.DMA((3 * n,)),
            pltpu.SemaphoreType.DMA((3 * n,)), pltpu.SemaphoreType.DMA((n,))]


def _gathered_shapes(shards):
    return [jax.ShapeDtypeStruct((N_CHIPS,) + s.shape, s.dtype) for s in shards]


def _swap_copies(srcs, dsts, axes, send, recv):
    x, y, c = lax.axis_index("x"), lax.axis_index("y"), lax.axis_index("c")
    return [pltpu.make_async_remote_copy(
        src_ref=srcs[t] if axes[t] is None else _half(srcs[t], axes[t], 1 - c), dst_ref=dsts[t],
        send_sem=send.at[t], recv_sem=recv.at[t], device_id=(x, y, 1 - c), device_id_type=MESH) for t in range(len(srcs))]


def _swap_shapes(srcs, axes):
    return [jax.ShapeDtypeStruct(g.shape if a is None else _half_shape(g.shape, a), g.dtype) for g, a in zip(srcs, axes)]


def _sibling_exchange(gs, axes):
    n = len(gs)

    def body(*refs):
        cps = _swap_copies(refs[:n], refs[n:2 * n], axes, *refs[2 * n:])
        for cp in cps:
            cp.start()
        for cp in cps:
            cp.wait()

    return pl.pallas_call(
        body, name="grad_sibling_exchange",
        in_specs=[HBM] * n, out_specs=[HBM] * n, out_shape=_swap_shapes(gs, axes),
        scratch_shapes=[pltpu.SemaphoreType.DMA((n,)), pltpu.SemaphoreType.DMA((n,))],
    )(*_in_hbm(*gs))


def _add_halves_w_in(parts, rbs, core):
    na, nb_rows, w = rbs[0].shape[0], rbs[1].shape[0], rbs[0].shape[1]
    per = (na + nb_rows + N_HEADS) // N_CHIPS
    pieces = [[] for _ in range(N_CHIPS)]
    r = 0
    for src, lo, hi in ((0, 0, na), (1, 0, ATTN_WIDTH), (2, 0, N_HEADS), (1, ATTN_WIDTH, nb_rows)):
        while lo < hi:
            c, at = divmod(r, per)
            n = min(hi - lo, per - at)
            pieces[c].append((src, lo, n, at))
            lo, r = lo + n, r + n

    def body(core_ref, a_ref, b_ref, f_ref, ra_ref, rb_ref, rf_ref, o_ref, am, bm, fm, ar, br, fr, rows_buf, sems):
        cols = pl.ds(pl.multiple_of(core_ref[0] * w, w), w)
        few = pl.ds(0, N_HEADS)
        loads = [(pltpu.make_async_copy(a_ref.at[:, cols], am, sems.at[0]), pltpu.make_async_copy(ra_ref, ar, sems.at[1])),
                 (pltpu.make_async_copy(b_ref.at[:, cols], bm, sems.at[2]), pltpu.make_async_copy(rb_ref, br, sems.at[3])),
                 (pltpu.make_async_copy(f_ref.at[few, cols], fm, sems.at[4]),
                  pltpu.make_async_copy(rf_ref.at[few, :], fr, sems.at[5]))]
        for pair in loads:
            for cp in pair:
                cp.start()
        bufs = [(am, ar), (bm, br), (fm, fr)]
        there = set()
        for c in range(N_CHIPS):
            for src, lo, n, at in pieces[c]:
                if src not in there:
                    for cp in loads[src]:
                        cp.wait()
                    there.add(src)
                mine, other = bufs[src]
                rows_buf[at:at + n, :] = mine[lo:lo + n, :] + other[lo:lo + n, :]
            o_ref[c] = rows_buf[0:per, :].astype(BF16)

    any_space = pl.BlockSpec(memory_space=pl.ANY)
    return pl.pallas_call(
        body, name="grad_add_sibling_w_in",
        grid_spec=pltpu.PrefetchScalarGridSpec(
            num_scalar_prefetch=1, grid=(1,),
            in_specs=[any_space] * 6,
            out_specs=pl.BlockSpec((N_CHIPS, per, w), lambda i, cr: (0, 0, 0)),
            scratch_shapes=[pltpu.VMEM((na, w), F32), pltpu.VMEM((nb_rows, w), F32), pltpu.VMEM((N_HEADS, w), F32),
                            pltpu.VMEM((na, w), F32), pltpu.VMEM((nb_rows, w), F32), pltpu.VMEM((N_HEADS, w), F32),
                            pltpu.VMEM((per + (-per) % 8, w), F32), pltpu.SemaphoreType.DMA((6,))]),
        out_shape=jax.ShapeDtypeStruct((N_CHIPS, per, w), BF16),
        compiler_params=_params(("arbitrary",)),
    )(core, *parts, *rbs)


def _weight_sections(wg):
    per, d = wg.shape[1], wg.shape[2]
    n_ref = N_CHIPS * per
    units = [(0, 0, REF_U, REF_Q), (1, 0, REF_Q, REF_ZA), (0, REF_Q - REF_U, REF_ZA, REF_F), (2, 0, REF_F, REF_GP),
             (0, REF_Q - REF_U + REF_F - REF_ZA, REF_GP, n_ref)]
    plan = []
    for out, first, lo, hi in sorted(units, key=lambda u: u[2]):
        pieces, at = [], 0
        while lo < hi:
            c, src = divmod(lo, per)
            n = min(hi - lo, per - src)
            pieces.append((c, src, n, at))
            lo, at = lo + n, at + n
        plan.append((out, first, at, pieces))
    stage_rows = max(rows for _, _, rows, _ in plan)

    def body(g_ref, e_ref, q_ref, f_ref, gb, stage, sems):
        loads = [pltpu.make_async_copy(g_ref.at[c], gb.at[c], sems.at[c]) for c in range(N_CHIPS)]
        for cp in loads:
            cp.start()
        outs = [e_ref, q_ref, f_ref]
        there = set()
        for out, first, rows, pieces in plan:
            for c, _, _, _ in pieces:
                if c not in there:
                    loads[c].wait()
                    there.add(c)
            if len(pieces) == 1 and pieces[0][1] % 16 == 0 and rows % 16 == 0:
                c, src, _, _ = pieces[0]
                outs[out][first:first + rows, :] = gb[c, src:src + rows, :]
                continue
            for c, src, n, at in pieces:
                stage[at:at + n, :] = gb[c, src:src + n, :].astype(F32)
            padded = rows if rows % 16 == 0 else outs[out].shape[0] - first
            if padded > rows:
                stage[rows:padded, :] = jnp.zeros((padded - rows, d), F32)
            outs[out][first:first + padded, :] = stage[0:padded, :].astype(BF16)

    return pl.pallas_call(
        body, name="weight_sections",
        in_specs=[pl.BlockSpec(memory_space=pl.ANY)],
        out_shape=[jax.ShapeDtypeStruct((E_COLS, d), BF16), jax.ShapeDtypeStruct((QKV_COLS, d), BF16),
                   jax.ShapeDtypeStruct((F_COLS, d), BF16)],
        scratch_shapes=[pltpu.VMEM(wg.shape, BF16), pltpu.VMEM((stage_rows, d), F32), pltpu.SemaphoreType.DMA((N_CHIPS,))],
        compiler_params=_params(),
    )(wg)


def _add_sibling_half(gs, rbs, axes, small, sr, core):
    n = len(gs)

    def body(core_ref, *refs):
        g_refs, rb_refs = refs[:n], refs[n:2 * n]
        s_ref, sr_ref = refs[2 * n], refs[2 * n + 1]
        h_refs, sc_ref = refs[2 * n + 2:3 * n + 2], refs[3 * n + 2]
        for t in range(n):
            h_refs[t][...] = (g_refs[t][...] + rb_refs[t][...]).astype(BF16)
        sc_ref[...] = s_ref[...] + sr_ref[...]

    def mine(rb, axis):
        blk = (None,) + rb.shape[1:]
        if axis == 2:
            return pl.BlockSpec(blk, lambda j, cr: (j, 0, cr[0]))
        return pl.BlockSpec(blk, lambda j, cr: (j, cr[0], 0))

    chunk = lambda rb: pl.BlockSpec((None,) + rb.shape[1:], lambda j, cr: (j, 0, 0))
    whole = pl.BlockSpec(small.shape, lambda j, cr: (0, 0))
    return pl.pallas_call(
        body, name="grad_add_sibling",
        grid_spec=pltpu.PrefetchScalarGridSpec(
            num_scalar_prefetch=1, grid=(N_CHIPS,),
            in_specs=[mine(rb, a) for rb, a in zip(rbs, axes)] + [chunk(rb) for rb in rbs] + [whole, whole],
            out_specs=[chunk(rb) for rb in rbs] + [whole]),
        out_shape=[jax.ShapeDtypeStruct(rb.shape, BF16) for rb in rbs] + [jax.ShapeDtypeStruct(small.shape, F32)],
        compiler_params=_params(("arbitrary",)),
    )(core, *gs, *rbs, small, sr)


def _chip_scatter_plan(h_refs, s_ref, q_refs, sq_ref, send, recv, local):
    n = len(h_refs)
    nt = n + (s_ref is not None)
    x, y, c = lax.axis_index("x"), lax.axis_index("y"), lax.axis_index("c")
    me = 2 * x + y
    chips = [(1 - x, y), (x, 1 - y), (1 - x, 1 - y)]

    def copy(t, k, chip, src_slot, dst_slot):
        src = h_refs[t].at[src_slot] if t < n else s_ref
        dst = (q_refs[t] if t < n else sq_ref).at[dst_slot]
        return pltpu.make_async_remote_copy(src_ref=src, dst_ref=dst, send_sem=send.at[3 * t + k],
                                            recv_sem=recv.at[3 * t + k], device_id=(*chip, c), device_id_type=MESH)

    own = [pltpu.make_async_copy(h_refs[t].at[me], q_refs[t].at[me], local.at[t]) for t in range(n)]
    if s_ref is not None:
        own.append(pltpu.make_async_copy(s_ref, sq_ref.at[me], local.at[n]))
    sends = [copy(t, k, (px, py), 2 * px + py, me) for t in range(nt) for k, (px, py) in enumerate(chips)]
    recvs = [copy(t, k, (px, py), me, 2 * px + py) for t in range(nt) for k, (px, py) in enumerate(chips)]
    return own, sends, recvs


def _scatter_operands(hs, sc):
    small = [] if sc is None else [sc]
    nt = len(hs) + len(small)
    shapes = [jax.ShapeDtypeStruct(h.shape, h.dtype) for h in hs] + [
        jax.ShapeDtypeStruct((N_CHIPS,) + a.shape, a.dtype) for a in small]
    sems = [pltpu.SemaphoreType.DMA((3 * nt,)), pltpu.SemaphoreType.DMA((3 * nt,)), pltpu.SemaphoreType.DMA((nt,))]
    return list(hs) + small, shapes, sems


def _small_allreduce(buf):
    def body(b_ref, o_ref, sib_buf, chip_buf, send, recv):
        x, y, c = lax.axis_index("x"), lax.axis_index("y"), lax.axis_index("c")
        me = 2 * x + y
        chips = [(1 - x, y), (x, 1 - y), (1 - x, 1 - y)]
        swap = pltpu.make_async_remote_copy(src_ref=b_ref, dst_ref=sib_buf, send_sem=send.at[0], recv_sem=recv.at[0],
                                            device_id=(x, y, 1 - c), device_id_type=MESH)
        swap.start()
        swap.wait()
        chip_buf[me] = b_ref[...] + sib_buf[...]

        def copy(k, chip, slot):
            return pltpu.make_async_remote_copy(src_ref=chip_buf.at[slot], dst_ref=chip_buf.at[slot], send_sem=send.at[1 + k],
                                                recv_sem=recv.at[1 + k], device_id=(*chip, c), device_id_type=MESH)

        sends = [copy(k, chip, me) for k, chip in enumerate(chips)]
        for cp in sends:
            cp.start()
        for k, (px, py) in enumerate(chips):
            copy(k, (px, py), 2 * px + py).wait_recv()
        for cp in sends:
            cp.wait_send()
        o_ref[...] = ((chip_buf[0] + chip_buf[1]) + chip_buf[2]) + chip_buf[3]

    vmem = pl.BlockSpec(memory_space=pltpu.VMEM)
    return pl.pallas_call(
        body, name="small_allreduce", in_specs=[vmem], out_specs=vmem,
        out_shape=jax.ShapeDtypeStruct(buf.shape, F32),
        scratch_shapes=[pltpu.VMEM(buf.shape, F32), pltpu.VMEM((N_CHIPS,) + buf.shape, F32),
                        pltpu.SemaphoreType.DMA((4,)), pltpu.SemaphoreType.DMA((4,))],
        compiler_params=_params(),
    )(buf)


def _reduce_allgather(qs, axes, sq, as_rows):
    n = len(qs)
    shard_shapes, half_axes = [], []
    for q, a, rows_form in zip(qs, axes, as_rows):
        shape = [d * 2 if i == a - 1 else d for i, d in enumerate(q.shape[1:])]
        if rows_form:
            assert a == 2
            shape = [shape[0], 1, shape[1]]
        shard_shapes.append(tuple(shape))
        half_axes.append(2 if rows_form else a - 1)

    def body(*refs):
        q_refs, sq_ref = refs[:n], refs[n]
        o_refs, st_ref = refs[n + 1:2 * n + 1], refs[2 * n + 1]
        send, recv = refs[2 * n + 2:]
        x, y, c = lax.axis_index("x"), lax.axis_index("y"), lax.axis_index("c")

        def swap(t, which):
            return pltpu.make_async_remote_copy(
                src_ref=_half(o_refs[t], half_axes[t], which), dst_ref=_half(o_refs[t], half_axes[t], which),
                send_sem=send.at[t], recv_sem=recv.at[t], device_id=(x, y, 1 - c), device_id_type=MESH)

        sent = []
        for t in range(n):
            q = q_refs[t]
            total = ((q[0].astype(F32) + q[1].astype(F32)) + q[2].astype(F32)) + q[3].astype(F32)
            if as_rows[t]:
                total = total.reshape(total.shape[0], 1, total.shape[1])
            _half(o_refs[t], half_axes[t], c)[...] = total
            cp = swap(t, c)
            cp.start()
            sent.append(cp)
        st_ref[...] = ((sq_ref[0] + sq_ref[1]) + sq_ref[2]) + sq_ref[3]
        for t in range(n):
            swap(t, 1 - c).wait_recv()
        for cp in sent:
            cp.wait_send()

    vmem = pl.BlockSpec(memory_space=pltpu.VMEM)
    return pl.pallas_call(
        body, name="grad_reduce_allgather",
        in_specs=[vmem] * (n + 1), out_specs=[vmem] * (n + 1),
        out_shape=[jax.ShapeDtypeStruct(s, F32) for s in shard_shapes] + [jax.ShapeDtypeStruct(sq.shape[1:], F32)],
        scratch_shapes=[pltpu.SemaphoreType.DMA((n,)), pltpu.SemaphoreType.DMA((n,))],
        compiler_params=_params(),
    )(*qs, sq)


def _dot_nt(a, b):
    return lax.dot_general(a, b, (((1,), (1,)), ((), ())), preferred_element_type=F32)


def _dot_tn(a, b):
    return lax.dot_general(a, b, (((0,), (0,)), ((), ())), preferred_element_type=F32)


def _mm_nt(a, bt, out_dtype, name, m, n, tm, tn, row_block=0, scale_first=None, gather=None):
    k = a.shape[1]
    shards, axes = gather if gather is not None else ([], [])
    ng = len(shards)
    grid = (m // tm, n // tn)

    def body(a_ref, b_ref, *rest):
        ins, o_ref, outs, sems = rest[:ng], rest[ng], rest[ng + 1:2 * ng + 1], rest[2 * ng + 1:]
        first = (pl.program_id(0) == 0) & (pl.program_id(1) == 0)
        last = (pl.program_id(0) == grid[0] - 1) & (pl.program_id(1) == grid[1] - 1)
        if ng:
            @pl.when(first)
            def _():
                _gather_start(ins, outs, axes, *sems)

        r = _dot_nt(a_ref[...], b_ref[...])
        if scale_first is not None:
            r = r * jnp.where(pl.program_id(1) == 0, scale_first, 1.0)
        o_ref[...] = r.astype(out_dtype)

        if ng:
            @pl.when(last)
            def _():
                _gather_finish(ins, outs, axes, *sems)

    res = pl.pallas_call(
        body, name=name, grid=grid,
        in_specs=[pl.BlockSpec((tm, k), lambda i, j: (i, 0)), pl.BlockSpec((tn, k), lambda i, j: (row_block + j, 0))]
        + [HBM] * ng,
        out_specs=[pl.BlockSpec((tm, tn), lambda i, j: (i, j))] + [HBM] * ng,
        out_shape=[jax.ShapeDtypeStruct((m, n), out_dtype)] + _gathered_shapes(shards),
        scratch_shapes=_gather_sems(ng) if ng else [],
        compiler_params=_params(("arbitrary", "arbitrary") if ng else ("parallel", "parallel")),
    )(a, bt, *_in_hbm(*shards))
    return (res[0], res[1:]) if ng else res[0]


def _mm_tn(a, b, name, k, tm, tn, tk, chunks=1, m=None, m_off=0, swap=None, scatter=None):
    m = a.shape[1] if m is None else m
    n = b.shape[1]
    cw = n // chunks
    srcs, axes = swap if swap is not None else ([], [])
    ns = len(srcs)
    hs, sc = scatter if scatter is not None else ([], None)
    sc_in, sc_shapes, sc_sems = _scatter_operands(hs, sc) if scatter is not None else ([], [], [])
    nh, nq = len(hs), len(sc_in)
    grid = (m // tm, n // tn, k // tk)

    def body(a_ref, b_ref, *rest):
        s_refs, h_refs = rest[:ns], rest[ns:ns + nq]
        o_ref = rest[ns + nq]
        d_refs, q_refs = rest[ns + nq + 1:2 * ns + nq + 1], rest[2 * ns + nq + 1:2 * ns + 2 * nq + 1]
        sems = rest[2 * ns + 2 * nq + 1:]
        swap_sems, scatter_sems = (sems[:2], sems[2:]) if ns else ((), sems)
        ids = [pl.program_id(d) for d in range(3)]

        def scatter_plan():
            small_in = h_refs[nh] if sc is not None else None
            small_out = q_refs[nh] if sc is not None else None
            return _chip_scatter_plan(h_refs[:nh], small_in, q_refs[:nh], small_out, *scatter_sems)

        if ns or nq:
            @pl.when((ids[0] == 0) & (ids[1] == 0) & (ids[2] == 0))
            def _():
                if ns:
                    for cp in _swap_copies(s_refs, d_refs, axes, *swap_sems):
                        cp.start()
                if nq:
                    own, sends, _ = scatter_plan()
                    for cp in own + sends:
                        cp.start()

        @pl.when(ids[2] == 0)
        def _():
            o_ref[...] = jnp.zeros_like(o_ref)
        r = _dot_tn(a_ref[...].astype(BF16), b_ref[...].astype(BF16))
        if chunks > 1:
            for c in range(chunks):
                o_ref[c] += r[:, c * cw:(c + 1) * cw]
        else:
            o_ref[...] += r

        if ns or nq:
            @pl.when((ids[0] == grid[0] - 1) & (ids[1] == grid[1] - 1) & (ids[2] == grid[2] - 1))
            def _():
                if ns:
                    for cp in _swap_copies(s_refs, d_refs, axes, *swap_sems):
                        cp.wait()
                if nq:
                    own, sends, recvs = scatter_plan()
                    for cp in recvs:
                        cp.wait_recv()
                    for cp in sends:
                        cp.wait_send()
                    for cp in own:
                        cp.wait()

    if chunks > 1:
        assert tn == n
        out_spec = pl.BlockSpec((chunks, tm, cw), lambda i, j, kk: (0, i, 0))
        out_shape = jax.ShapeDtypeStruct((chunks, m, cw), F32)
    else:
        out_spec = pl.BlockSpec((tm, tn), lambda i, j, kk: (i, j))
        out_shape = jax.ShapeDtypeStruct((m, n), F32)
    riders = ns + nq
    res = pl.pallas_call(
        body, name=name, grid=grid,
        in_specs=[pl.BlockSpec((tk, tm), lambda i, j, kk: (kk, m_off + i)), pl.BlockSpec((tk, tn), lambda i, j, kk: (kk, j))]
        + [HBM] * riders,
        out_specs=[out_spec] + [HBM] * riders, out_shape=[out_shape] + _swap_shapes(srcs, axes) + sc_shapes,
        scratch_shapes=([pltpu.SemaphoreType.DMA((ns,)), pltpu.SemaphoreType.DMA((ns,))] if ns else []) + sc_sems,
        compiler_params=_params(("arbitrary",) * 3 if riders else ("parallel", "parallel", "arbitrary")),
    )(a, b, *_in_hbm(*srcs, *sc_in))
    if not riders:
        return res[0]
    out = [res[0]]
    if ns:
        out.append(res[1:1 + ns])
    if nq:
        out.append((res[1 + ns:1 + ns + nh], res[1 + ns + nh] if sc is not None else None))
    return tuple(out)


def _tokens_spec():
    return pl.BlockSpec((ROW_TILE, D_MODEL), lambda i: (jnp.maximum(i - 1, 0), 0))


def _rmsnorm_fwd(x2, g1, lk, shards, axes):
    nb = x2.shape[0] // ROW_TILE + 1
    nblk = lk // ROW_TILE
    ng = len(shards)
    meta_cols = shards[-1].shape[1]

    def body(x_ref, g_ref, *rest):
        ins, hn_ref, outs = rest[:ng], rest[ng], rest[ng + 1:2 * ng + 1]
        sems, meta_buf, meta_sem = rest[2 * ng + 1:2 * ng + 6], rest[2 * ng + 6], rest[2 * ng + 7]
        s = pl.program_id(0)
        blk = (s + 1) % nblk

        @pl.when(s == 0)
        def _():
            _gather_start(ins, outs, axes, *sems)

        def normed(h):
            r = lax.rsqrt(jnp.mean(h * h, axis=-1, keepdims=True) + RMS_EPS)
            return ((h * r) * g_ref[...]).astype(BF16)

        @pl.when(s < nblk - 1)
        def _():
            hn_ref[...] = normed(jnp.where(blk >= nb, 0.0, x_ref[...]))

        @pl.when(s == nblk - 1)
        def _():
            _gather_finish(ins, outs, axes, *sems)
            fetch = pltpu.make_async_copy(outs[-1], meta_buf, meta_sem.at[0])
            fetch.start()
            fetch.wait()
            meta = jnp.concatenate([meta_buf[j] for j in range(N_CHIPS)], axis=1)
            hn_ref[...] = normed(jnp.concatenate([jnp.zeros((PAD_ROWS, D_MODEL), F32), meta], axis=0))

    res = pl.pallas_call(
        body, name="rmsnorm_fwd", grid=(nblk,),
        in_specs=[pl.BlockSpec((ROW_TILE, D_MODEL), lambda s: (jnp.clip((s + 1) % nblk - 1, 0, nb - 2), 0)),
                  pl.BlockSpec((1, D_MODEL), lambda s: (0, 0))] + [HBM] * ng,
        out_specs=[pl.BlockSpec((ROW_TILE, D_MODEL), lambda s: ((s + 1) % nblk, 0))] + [HBM] * ng,
        out_shape=[jax.ShapeDtypeStruct((lk, D_MODEL), BF16)] + _gathered_shapes(shards),
        scratch_shapes=_gather_sems(ng) + [pltpu.VMEM((N_CHIPS, N_META, meta_cols), F32), pltpu.SemaphoreType.DMA((1,))],
        compiler_params=_params(("arbitrary",)),
    )(x2, g1, *_in_hbm(*shards))
    return res[0], res[1:]


def _valid_gate_mask(i):
    row = i * ROW_TILE + lax.broadcasted_iota(jnp.int32, (ROW_TILE, F_COLS), 0)
    col = lax.broadcasted_iota(jnp.int32, (ROW_TILE, F_COLS), 1)
    return (row >= PAD_ROWS) & (col < N_HEADS)


def _gates_fwd(f, bfg):
    nb = f.shape[0] // ROW_TILE

    def body(f_ref, b_ref, c_ref, carry):
        i = pl.program_id(0)

        @pl.when(i == 0)
        def _():
            carry[...] = jnp.zeros_like(carry)

        logit = f_ref[...] + b_ref[...]
        lf = jnp.minimum(logit, 0.0) - jnp.log1p(jnp.exp(-jnp.abs(logit)))
        lf = jnp.where(_valid_gate_mask(i), lf, 0.0)
        r_i = lax.broadcasted_iota(jnp.int32, (ROW_TILE, ROW_TILE), 0)
        c_i = lax.broadcasted_iota(jnp.int32, (ROW_TILE, ROW_TILE), 1)
        tri = (c_i <= r_i).astype(F32)
        c_ref[...] = jnp.dot(tri, lf, precision=HIGHEST, preferred_element_type=F32) + carry[...]
        carry[...] = carry[...] + jnp.sum(lf, axis=0, keepdims=True)

    return pl.pallas_call(
        body, name="gates_fwd", grid=(nb,),
        in_specs=[pl.BlockSpec((ROW_TILE, F_COLS), lambda i: (i, 0)), pl.BlockSpec((1, F_COLS), lambda i: (0, 0))],
        out_specs=pl.BlockSpec((ROW_TILE, F_COLS), lambda i: (i, 0)),
        out_shape=jax.ShapeDtypeStruct(f.shape, F32),
        scratch_shapes=[pltpu.VMEM((1, F_COLS), F32)],
        compiler_params=_params(("arbitrary",)),
    )(f, bfg)


def _pool_counts(i):
    row = i * ROW_TILE + lax.broadcasted_iota(jnp.int32, (ROW_TILE, 1), 0)
    return jnp.maximum(row - PAD_ROWS, 0)


def _trailing_sums(xc, levels):
    acc = xc
    for lv in range(levels):
        acc = acc + pltpu.roll(acc, 1 << lv, 0)
    return acc


def _leading_sums(xc, levels):
    n = xc.shape[0]
    acc = xc
    for lv in range(levels):
        acc = acc + pltpu.roll(acc, n - (1 << lv), 0)
    return acc


def _pool_p(u_cur, u_prev, i):
    pos = _pool_counts(i)
    ps, invs = [], []
    for g, w in enumerate(POOL_WINDOWS):
        sl = slice(g * POOL_GROUP, (g + 1) * POOL_GROUP)
        cur = u_cur[:, sl]
        xc = jnp.concatenate([u_prev[:, sl], cur], axis=0)
        win = _trailing_sums(xc, g + 1)[ROW_TILE:, :]
        inv = 1.0 / jnp.minimum(pos + 1, w).astype(F32)
        ps.append(win * inv - cur)
        invs.append(inv)
    return ps, invs


def _pool_fwd(e, pw, scale):
    nb = e.shape[0] // ROW_TILE

    def body(uc_ref, up_ref, z_ref, pw_ref, sc_ref, y_ref):
        i = pl.program_id(0)
        u_cur = uc_ref[...].astype(F32)
        u_prev = jnp.where(i == 0, 0.0, up_ref[...].astype(F32))
        ps, _ = _pool_p(u_cur, u_prev, i)
        z = z_ref[...].astype(F32)
        gate = z * _sigmoid(z)
        for g in range(len(POOL_WINDOWS)):
            sl = slice(g * POOL_GROUP, (g + 1) * POOL_GROUP)
            yraw = jnp.dot(ps[g].astype(BF16), pw_ref[g], preferred_element_type=F32)
            y_ref[:, sl] = ((yraw * sc_ref[:, sl]) * gate[:, sl]).astype(BF16)

    blk = (ROW_TILE, POOL_WIDTH)
    return pl.pallas_call(
        body, name="pool_fwd", grid=(nb,),
        in_specs=[pl.BlockSpec(blk, lambda i: (i, 0)), pl.BlockSpec(blk, lambda i: (jnp.maximum(i - 1, 0), 0)),
                  pl.BlockSpec(blk, lambda i: (i, 1)),
                  pl.BlockSpec((len(POOL_WINDOWS), POOL_GROUP, POOL_GROUP), lambda i: (0, 0, 0)),
                  pl.BlockSpec((1, POOL_WIDTH), lambda i: (0, 0))],
        out_specs=pl.BlockSpec(blk, lambda i: (i, 0)),
        out_shape=jax.ShapeDtypeStruct((e.shape[0], POOL_WIDTH), BF16),
        compiler_params=_params(("parallel",)),
    )(e, e, e, pw, scale)


def _stack_heads(a):
    first = lax.broadcasted_iota(jnp.int32, a.shape, 1) < HEAD_DIM
    zero = jnp.zeros_like(a)
    return jnp.concatenate([jnp.where(first, a, zero), jnp.where(first, zero, a)], axis=0)


def _unstack_heads(a):
    rows = a.shape[0] // 2
    first = lax.broadcasted_iota(jnp.int32, (rows, LANES), 1) < HEAD_DIM
    return jnp.where(first, a[:rows], a[rows:])


def _causal(i, jj, stacked, kv_tile):
    r = lax.broadcasted_iota(jnp.int32, (stacked * ROW_TILE, kv_tile), 0)
    if stacked == 2:
        r = jnp.where(r >= ROW_TILE, r - ROW_TILE, r)
    kidx = jj * kv_tile + lax.broadcasted_iota(jnp.int32, (stacked * ROW_TILE, kv_tile), 1)
    return kidx <= i * ROW_TILE + r


def _stack_rows(b):
    n = b.shape[1]
    return jnp.concatenate([jnp.broadcast_to(b[0:1], (ROW_TILE, n)), jnp.broadcast_to(b[1:2], (ROW_TILE, n))], axis=0)


def _attn_fwd(qkv, bias, nb):
    lk = qkv.shape[0]
    lp = nb * ROW_TILE
    nkb = lk // KV_TILE
    n_pairs = N_HEADS // 2

    def body(q_ref, k_ref, v_ref, b_ref, o_ref, lse_ref):
        i = pl.program_id(1)
        qs = _stack_heads(q_ref[...])
        last = (i * ROW_TILE) // KV_TILE

        def block(jj, carry, masked, n_keys=KV_TILE):
            m, l, acc = carry
            rows = pl.ds(pl.multiple_of(jj * KV_TILE, KV_TILE), n_keys)
            s = _dot_nt(qs, k_ref[rows, :]) - _stack_rows(b_ref[0, 0, :, rows])
            if masked:
                s = jnp.where(_causal(i, jj, 2, KV_TILE)[:, :n_keys], s, NEG)
            m_new = jnp.maximum(m, jnp.max(s, axis=1, keepdims=True))
            alpha = jnp.exp(m - m_new)
            p = jnp.exp(s - m_new)
            l = alpha * l + jnp.sum(p, axis=1, keepdims=True)
            acc = alpha * acc + jnp.dot(p.astype(BF16), v_ref[rows, :], preferred_element_type=F32)
            return m_new, l, acc

        init = (jnp.full((2 * ROW_TILE, 1), NEG, F32), jnp.zeros((2 * ROW_TILE, 1), F32),
                jnp.zeros((2 * ROW_TILE, LANES), F32))
        def finish(carry):
            m, l, acc = carry
            o_ref[...] = _unstack_heads(acc / l)
            lse_ref[...] = _unstack_heads(jnp.broadcast_to(m + jnp.log(l), (2 * ROW_TILE, LANES)))

        carry = lax.fori_loop(0, last, lambda jj, c: block(jj, c, False), init)
        ends = [lambda c, n=(r + 1) * ROW_TILE: finish(block(last, c, True, n)) for r in range(KV_TILE // ROW_TILE)]
        lax.switch(i - last * (KV_TILE // ROW_TILE), ends, carry)

    return pl.pallas_call(
        body, name="attn_fwd", grid=(n_pairs, nb),
        in_specs=[pl.BlockSpec((ROW_TILE, LANES), lambda hp, i: (i, hp)),
                  pl.BlockSpec((lk, LANES), lambda hp, i: (0, n_pairs + hp)),
                  pl.BlockSpec((lk, LANES), lambda hp, i: (0, 2 * n_pairs + hp)),
                  pl.BlockSpec((1, 1, 2, lk), lambda hp, i: (i, hp, 0, 0))],
        out_specs=[pl.BlockSpec((ROW_TILE, LANES), lambda hp, i: (i, hp)),
                   pl.BlockSpec((ROW_TILE, LANES), lambda hp, i: (i, hp))],
        out_shape=[jax.ShapeDtypeStruct((lp, ATTN_WIDTH), F32), jax.ShapeDtypeStruct((lp, ATTN_WIDTH), F32)],
        compiler_params=_params(("parallel", "parallel")),
    )(qkv, qkv, qkv, bias)


def _merge_fwd(y_pool, o, e, wup_p, wup_a):
    lp = o.shape[0]
    nb = lp // ROW_TILE

    def body(yp_ref, o_ref, e_ref, wp_ref, wa_ref, mg_ref, ya_ref):
        za = e_ref[:, E_ZA:E_GP].astype(F32)
        ya = (o_ref[...] * (za * _sigmoid(za))).astype(BF16)
        ya_ref[...] = ya
        a_pool = jnp.dot(yp_ref[...], wp_ref[...], preferred_element_type=F32)
        a_attn = jnp.dot(ya, wa_ref[...], preferred_element_type=F32)
        mg_ref[...] = (_sigmoid(e_ref[:, E_GP:E_GA].astype(F32)) * a_pool
                       + _sigmoid(e_ref[:, E_GA:E_COLS].astype(F32)) * a_attn).astype(BF16)

    return pl.pallas_call(
        body, name="merge_fwd", grid=(nb,),
        in_specs=[pl.BlockSpec((ROW_TILE, POOL_WIDTH), lambda i: (i, 0)),
                  pl.BlockSpec((ROW_TILE, ATTN_WIDTH), lambda i: (i, 0)),
                  pl.BlockSpec((ROW_TILE, E_COLS), lambda i: (i, 0)),
                  pl.BlockSpec((POOL_WIDTH, D_MODEL), lambda i: (0, 0)),
                  pl.BlockSpec((ATTN_WIDTH, D_MODEL), lambda i: (0, 0))],
        out_specs=[pl.BlockSpec((ROW_TILE, D_MODEL), lambda i: (i, 0)),
                   pl.BlockSpec((ROW_TILE, ATTN_WIDTH), lambda i: (i, 0))],
        out_shape=[jax.ShapeDtypeStruct((lp, D_MODEL), BF16), jax.ShapeDtypeStruct((lp, ATTN_WIDTH), BF16)],
        compiler_params=_params(("parallel",)),
    )(y_pool, o, e, wup_p, wup_a)


def _head_fwd_bwd(merged, w_out, x2, metapad, gf, target):
    lp = merged.shape[0]
    nb = lp // ROW_TILE

    def body(mg_ref, w_ref, x_ref, mp_ref, g_ref, t_ref, dh_ref, loss_ref, dg_ref):
        i = pl.program_id(0)

        @pl.when(i == 0)
        def _():
            loss_ref[...] = jnp.zeros_like(loss_ref)
            dg_ref[...] = jnp.zeros_like(dg_ref)

        h0 = jnp.where(i == 0, mp_ref[...], x_ref[...])
        h1 = h0 + jnp.dot(mg_ref[...], w_ref[...], preferred_element_type=F32)
        r = lax.rsqrt(jnp.mean(h1 * h1, axis=-1, keepdims=True) + RMS_EPS)
        xhat = h1 * r
        g = g_ref[...]
        err = jnp.where(i == 0, 0.0, xhat * g - t_ref[...])
        loss_ref[...] += 0.5 * jnp.sum(jnp.mean(err * err, axis=-1, keepdims=True))
        dy = err / D_MODEL
        dg_ref[...] += jnp.sum(dy * xhat, axis=0, keepdims=True)
        dxhat = dy * g
        dh_ref[...] = r * (dxhat - xhat * jnp.mean(dxhat * xhat, axis=-1, keepdims=True))

    return pl.pallas_call(
        body, name="head_fwd_bwd", grid=(nb,),
        in_specs=[pl.BlockSpec((ROW_TILE, D_MODEL), lambda i: (i, 0)),
                  pl.BlockSpec((D_MODEL, D_MODEL), lambda i: (0, 0)),
                  _tokens_spec(), pl.BlockSpec((ROW_TILE, D_MODEL), lambda i: (0, 0)),
                  pl.BlockSpec((1, D_MODEL), lambda i: (0, 0)), _tokens_spec()],
        out_specs=[pl.BlockSpec((ROW_TILE, D_MODEL), lambda i: (i, 0)),
                   pl.BlockSpec((1, LANES), lambda i: (0, 0)), pl.BlockSpec((1, D_MODEL), lambda i: (0, 0))],
        out_shape=[jax.ShapeDtypeStruct((lp, D_MODEL), F32), jax.ShapeDtypeStruct((1, LANES), F32),
                   jax.ShapeDtypeStruct((1, D_MODEL), F32)],
        compiler_params=_params(("arbitrary",)),
    )(merged, w_out, x2, metapad, gf, target)


def _per_head_rowsum(t):
    head = lax.broadcasted_iota(jnp.int32, t.shape, 1) // HEAD_DIM
    out = jnp.zeros_like(t)
    for h in range(N_HEADS):
        sel = head == h
        out = jnp.where(sel, jnp.sum(jnp.where(sel, t, 0.0), axis=1, keepdims=True), out)
    return out


def _merge_bwd(dh1, w_out, y_pool, y_attn, wup_p, wup_a, e, o):
    lp = o.shape[0]
    nb = lp // ROW_TILE

    def body(dh_ref, wo_ref, yp_ref, ya_ref, wp_ref, wa_ref, e_ref, o_ref,
             dap_ref, daa_ref, dg_ref, do_ref, delta_ref, dyp_ref):
        dmerged = _dot_nt(dh_ref[...].astype(BF16), wo_ref[...])
        a_pool = jnp.dot(yp_ref[...], wp_ref[...], preferred_element_type=F32)
        a_attn = jnp.dot(ya_ref[...], wa_ref[...], preferred_element_type=F32)
        sp = _sigmoid(e_ref[:, E_GP:E_GA].astype(F32))
        sa = _sigmoid(e_ref[:, E_GA:E_COLS].astype(F32))
        dap = (dmerged * sp).astype(BF16)
        daa = (dmerged * sa).astype(BF16)
        dap_ref[...] = dap
        daa_ref[...] = daa
        dg_ref[:, ATTN_WIDTH:ATTN_WIDTH + D_MODEL] = (dmerged * a_pool * (sp * (1.0 - sp))).astype(BF16)
        dg_ref[:, ATTN_WIDTH + D_MODEL:] = (dmerged * a_attn * (sa * (1.0 - sa))).astype(BF16)
        dyp_ref[...] = _dot_nt(dap, wp_ref[...])
        dya = _dot_nt(daa, wa_ref[...])
        za = e_ref[:, E_ZA:E_GP].astype(F32)
        sz = _sigmoid(za)
        o = o_ref[...]
        do = dya * (za * sz)
        do_ref[...] = do.astype(BF16)
        dg_ref[:, :ATTN_WIDTH] = (dya * o * (sz * (1.0 + za * (1.0 - sz)))).astype(BF16)
        delta_ref[...] = _per_head_rowsum(do * o)

    row = lambda w: pl.BlockSpec((ROW_TILE, w), lambda i: (i, 0))
    full = lambda a: pl.BlockSpec(a.shape, lambda i: (0, 0))
    return pl.pallas_call(
        body, name="merge_bwd", grid=(nb,),
        in_specs=[row(D_MODEL), full(w_out), row(POOL_WIDTH), row(ATTN_WIDTH), full(wup_p), full(wup_a),
                  row(E_COLS), row(ATTN_WIDTH)],
        out_specs=[row(D_MODEL), row(D_MODEL), pl.BlockSpec((ROW_TILE, MAIN_COLS // 2), lambda i: (i, 1)),
                   row(ATTN_WIDTH), row(ATTN_WIDTH), row(POOL_WIDTH)],
        out_shape=[jax.ShapeDtypeStruct((lp, D_MODEL), BF16), jax.ShapeDtypeStruct((lp, D_MODEL), BF16),
                   jax.ShapeDtypeStruct((lp, MAIN_COLS), BF16),
                   jax.ShapeDtypeStruct((lp, ATTN_WIDTH), BF16), jax.ShapeDtypeStruct((lp, ATTN_WIDTH), F32),
                   jax.ShapeDtypeStruct((lp, POOL_WIDTH), F32)],
        compiler_params=_params(("parallel",)),
    )(dh1, w_out, y_pool, y_attn, wup_p, wup_a, e, o)


def _fill_dproj(dproj_half, du, dzp, dq, dk, dv):
    lp = dproj_half.shape[0]
    nb = lp // ROW_TILE

    def body(base_ref, du_ref, dzp_ref, dq_ref, dk_ref, dv_ref, o_ref):
        parts = [du_ref[...], dzp_ref[...], (dq_ref[...] * HEAD_DIM ** -0.5).astype(BF16), dk_ref[...], dv_ref[...]]
        for t, p in enumerate(parts):
            o_ref[:, t * ATTN_WIDTH:(t + 1) * ATTN_WIDTH] = p

    blk = pl.BlockSpec((ROW_TILE, ATTN_WIDTH), lambda i: (i, 0))
    return pl.pallas_call(
        body, name="fill_dproj", grid=(nb,),
        in_specs=[pl.BlockSpec(memory_space=pl.ANY)] + [blk] * 5,
        out_specs=pl.BlockSpec((ROW_TILE, MAIN_COLS // 2), lambda i: (i, 0)),
        out_shape=jax.ShapeDtypeStruct(dproj_half.shape, BF16),
        input_output_aliases={0: 0},
        compiler_params=_params(("parallel",)),
    )(dproj_half, du, dzp, dq, dk, dv)


def _pool_bwd_local(e, dy_pool, pw, scale):
    lp = e.shape[0]
    nb = lp // ROW_TILE
    ng = len(POOL_WINDOWS)

    def body(uc_ref, up_ref, z_ref, dy_ref, pw_ref, sc_ref, dpc_ref, dz_ref, dsc_ref, dpw_ref):
        i = pl.program_id(0)

        @pl.when(i == 0)
        def _():
            dsc_ref[...] = jnp.zeros_like(dsc_ref)
            dpw_ref[...] = jnp.zeros_like(dpw_ref)

        u_cur = uc_ref[...].astype(F32)
        u_prev = jnp.where(i == 0, 0.0, up_ref[...].astype(F32))
        ps, invs = _pool_p(u_cur, u_prev, i)
        z = z_ref[...].astype(F32)
        sz = _sigmoid(z)
        dy = dy_ref[...]
        dypre = dy * (z * sz)
        dsilu = sz * (1.0 + z * (1.0 - sz))
        for g in range(ng):
            sl = slice(g * POOL_GROUP, (g + 1) * POOL_GROUP)
            pb = ps[g].astype(BF16)
            w = pw_ref[g]
            yraw = jnp.dot(pb, w, preferred_element_type=F32)
            sc = sc_ref[:, sl]
            dz_ref[:, sl] = (dy[:, sl] * (yraw * sc) * dsilu[:, sl]).astype(BF16)
            dsc_ref[:, sl] += jnp.sum(dypre[:, sl] * yraw, axis=0, keepdims=True)
            dyraw = (dypre[:, sl] * sc).astype(BF16)
            dpw_ref[g] += _dot_tn(pb, dyraw)
            dpc_ref[:, sl] = _dot_nt(dyraw, w) * invs[g]

    blk = (ROW_TILE, POOL_WIDTH)
    return pl.pallas_call(
        body, name="pool_bwd_local", grid=(nb,),
        in_specs=[pl.BlockSpec(blk, lambda i: (i, 0)), pl.BlockSpec(blk, lambda i: (jnp.maximum(i - 1, 0), 0)),
                  pl.BlockSpec(blk, lambda i: (i, 1)), pl.BlockSpec(blk, lambda i: (i, 0)),
                  pl.BlockSpec((ng, POOL_GROUP, POOL_GROUP), lambda i: (0, 0, 0)),
                  pl.BlockSpec((1, POOL_WIDTH), lambda i: (0, 0))],
        out_specs=[pl.BlockSpec(blk, lambda i: (i, 0)), pl.BlockSpec(blk, lambda i: (i, 0)),
                   pl.BlockSpec((1, POOL_WIDTH), lambda i: (0, 0)),
                   pl.BlockSpec((ng, POOL_GROUP, POOL_GROUP), lambda i: (0, 0, 0))],
        out_shape=[jax.ShapeDtypeStruct((lp, POOL_WIDTH), F32), jax.ShapeDtypeStruct((lp, POOL_WIDTH), BF16),
                   jax.ShapeDtypeStruct((1, POOL_WIDTH), F32),
                   jax.ShapeDtypeStruct((ng, POOL_GROUP, POOL_GROUP), F32)],
        compiler_params=_params(("arbitrary",)),
    )(e, e, e, dy_pool, pw, scale)


def _pool_bwd_window(dpc):
    lp = dpc.shape[0]
    nb = lp // ROW_TILE

    def body(cur_ref, nxt_ref, du_ref):
        i = pl.program_id(0)
        cur = cur_ref[...]
        nxt = jnp.where(i == nb - 1, 0.0, nxt_ref[...])
        pos = _pool_counts(i)
        for g, w in enumerate(POOL_WINDOWS):
            sl = slice(g * POOL_GROUP, (g + 1) * POOL_GROUP)
            xc = jnp.concatenate([cur[:, sl], nxt[:, sl]], axis=0)
            win = _leading_sums(xc, g + 1)[:ROW_TILE, :]
            dp = cur[:, sl] * jnp.minimum(pos + 1, w).astype(F32)
            du_ref[:, sl] = (win - dp).astype(BF16)

    blk = (ROW_TILE, POOL_WIDTH)
    return pl.pallas_call(
        body, name="pool_bwd_window", grid=(nb,),
        in_specs=[pl.BlockSpec(blk, lambda i: (i, 0)), pl.BlockSpec(blk, lambda i: (jnp.minimum(i + 1, nb - 1), 0))],
        out_specs=pl.BlockSpec(blk, lambda i: (i, 0)),
        out_shape=jax.ShapeDtypeStruct((lp, POOL_WIDTH), BF16),
        compiler_params=_params(("parallel",)),
    )(dpc, dpc)


def _attn_bwd(qkv, do, lse, delta, bias, nb):
    lk = qkv.shape[0]
    lp = nb * ROW_TILE
    nkb = lk // KV_TILE_BWD
    n_pairs = N_HEADS // 2
    per_kv = KV_TILE_BWD // ROW_TILE

    def body(q_ref, k_ref, v_ref, do_ref, lse_ref, dl_ref, b_ref, dq_ref, dk_ref, dv_ref, dc_ref, dcq_ref,
             dk_acc, dv_acc, dc_acc):
        jj = pl.program_id(1)

        @pl.when(jj == 0)
        def _():
            dq_ref[...] = jnp.zeros_like(dq_ref)
            dcq_ref[...] = jnp.zeros_like(dcq_ref)

        dk_acc[...] = jnp.zeros_like(dk_acc)
        dv_acc[...] = jnp.zeros_like(dv_acc)
        dc_acc[...] = jnp.zeros_like(dc_acc)

        def block(i, n_keys, masked):
            kb, vb = k_ref[:n_keys, :], v_ref[:n_keys, :]
            rows = pl.ds(pl.multiple_of(i * ROW_TILE, ROW_TILE), ROW_TILE)
            qs = _stack_heads(q_ref[rows, :])
            dos = _stack_heads(do_ref[rows, :])
            lse_i, dl_i = lse_ref[rows, :], dl_ref[rows, :]
            s = _dot_nt(qs, kb)
            dp = _dot_nt(dos, vb)
            if masked:
                valid = _causal(i, jj, 1, KV_TILE_BWD)[:, :n_keys]
            ps, dss, dcs, rowsums = [], [], [], []
            for hd in range(2):
                half = slice(hd * ROW_TILE, (hd + 1) * ROW_TILE)
                col = slice(hd * HEAD_DIM, hd * HEAD_DIM + 1)
                sh = s[half] - b_ref[i, 0, hd:hd + 1, :n_keys]
                if masked:
                    sh = jnp.where(valid, sh, NEG)
                p = jnp.exp(sh - lse_i[:, col])
                ds = p * (dp[half] - dl_i[:, col])
                ps.append(p.astype(BF16))
                dss.append(ds.astype(BF16))
                dcs.append(jnp.sum(ds, axis=0, keepdims=True))
                rowsums.append(jnp.sum(ds, axis=1, keepdims=True))
            dsb = jnp.concatenate(dss, axis=0)
            dv_acc[:n_keys, :] += _dot_tn(jnp.concatenate(ps, axis=0), dos)
            dk_acc[:n_keys, :] += _dot_tn(dsb, qs)
            dc_acc[:, :n_keys] -= jnp.concatenate(dcs, axis=0)
            dq_ref[rows, :] += _unstack_heads(jnp.dot(dsb, kb, preferred_element_type=F32))
            dcq_ref[rows, :] += _unstack_heads(jnp.broadcast_to(jnp.concatenate(rowsums, axis=0), (2 * ROW_TILE, LANES)))

        first_q = per_kv * jj
        for r in range(per_kv):
            @pl.when(first_q + r < nb)
            def _():
                block(first_q + r, (r + 1) * ROW_TILE, True)

        def rest(i, carry):
            block(i, KV_TILE_BWD, False)
            return carry

        lax.fori_loop(jnp.minimum(first_q + per_kv, nb), nb, rest, 0)
        dk_ref[...] = dk_acc[...].astype(BF16)
        dv_ref[...] = dv_acc[...].astype(BF16)
        dc_ref[0, 0] = dc_acc[...]

    whole = lambda rows: pl.BlockSpec((rows, LANES), lambda hp, jj: (0, hp))
    kv_blk = lambda off: pl.BlockSpec((KV_TILE_BWD, LANES), lambda hp, jj: (jj, off + hp))
    return pl.pallas_call(
        body, name="attn_bwd", grid=(n_pairs, nkb),
        in_specs=[whole(lk), kv_blk(n_pairs), kv_blk(2 * n_pairs), whole(lp), whole(lp), whole(lp),
                  pl.BlockSpec((nb, 1, 2, KV_TILE_BWD), lambda hp, jj: (0, hp, 0, jj))],
        out_specs=[whole(lp), kv_blk(0), kv_blk(0),
                   pl.BlockSpec((1, 1, 2, KV_TILE_BWD), lambda hp, jj: (hp, jj, 0, 0)), whole(lp)],
        out_shape=[jax.ShapeDtypeStruct((lp, ATTN_WIDTH), F32), jax.ShapeDtypeStruct((lk, ATTN_WIDTH), BF16),
                   jax.ShapeDtypeStruct((lk, ATTN_WIDTH), BF16),
                   jax.ShapeDtypeStruct((n_pairs, nkb, 2, KV_TILE_BWD), F32),
                   jax.ShapeDtypeStruct((lp, ATTN_WIDTH), F32)],
        scratch_shapes=[pltpu.VMEM((KV_TILE_BWD, LANES), F32), pltpu.VMEM((KV_TILE_BWD, LANES), F32),
                        pltpu.VMEM((2, KV_TILE_BWD), F32)],
        compiler_params=_params(("parallel", "arbitrary")),
    )(qkv, qkv, qkv, do, lse, delta, bias)


def _gates_bwd(dc, dcq, f, bfg):
    nb = f.shape[0] // ROW_TILE

    def body(dc_ref, dcq_ref, f_ref, b_ref, df_ref, db_ref, carry):
        step = pl.program_id(0)
        i = nb - 1 - step

        @pl.when(step == 0)
        def _():
            carry[...] = jnp.zeros_like(carry)
            db_ref[...] = jnp.zeros_like(db_ref)

        dcb = dc_ref[...]
        lane = lax.broadcasted_iota(jnp.int32, (ROW_TILE, F_COLS), 1)
        for h in range(N_HEADS):
            dcb = dcb + jnp.where(lane == h, dcq_ref[:, HEAD_DIM * h:HEAD_DIM * h + 1], 0.0)
        r_i = lax.broadcasted_iota(jnp.int32, (ROW_TILE, ROW_TILE), 0)
        c_i = lax.broadcasted_iota(jnp.int32, (ROW_TILE, ROW_TILE), 1)
        upper = (c_i >= r_i).astype(F32)
        dlf = jnp.dot(upper, dcb, precision=HIGHEST, preferred_element_type=F32) + carry[...]
        carry[...] = carry[...] + jnp.sum(dcb, axis=0, keepdims=True)
        logit = f_ref[...] + b_ref[...]
        dlogit = jnp.where(_valid_gate_mask(i), dlf * _sigmoid(-logit), 0.0)
        df_ref[...] = dlogit.astype(BF16)
        db_ref[...] += jnp.sum(dlogit, axis=0, keepdims=True)

    blk = pl.BlockSpec((ROW_TILE, F_COLS), lambda s: (nb - 1 - s, 0))
    wide = pl.BlockSpec((ROW_TILE, ATTN_WIDTH), lambda s: (nb - 1 - s, 0))
    one = pl.BlockSpec((1, F_COLS), lambda s: (0, 0))
    return pl.pallas_call(
        body, name="gates_bwd", grid=(nb,),
        in_specs=[blk, wide, blk, one], out_specs=[blk, one],
        out_shape=[jax.ShapeDtypeStruct(f.shape, BF16), jax.ShapeDtypeStruct((1, F_COLS), F32)],
        scratch_shapes=[pltpu.VMEM((1, F_COLS), F32)],
        compiler_params=_params(("arbitrary",)),
    )(dc, dcq, f, bfg)


def _input_bwd(dproj, df, wt_e, wt_qkv, wt_f, x2, metapad, dh1, g1, hs, sc=None):
    nb = x2.shape[0] // ROW_TILE + 1
    n = len(hs)
    sc_in, sc_shapes, sc_sems = _scatter_operands(hs, sc)
    nq = len(sc_in)

    def body(dp_ref, df_ref, we_ref, w_ref, wf_ref, x_ref, mp_ref, dh_ref, g_ref, *rest):
        h_refs, s_ref = rest[:n], (rest[n] if sc is not None else None)
        gx_ref, g0_ref, dg_ref = rest[nq:nq + 3]
        q_refs, sq_ref = rest[nq + 3:nq + 3 + n], (rest[nq + 3 + n] if sc is not None else None)
        sems = rest[2 * nq + 3:]
        i = pl.program_id(0)

        @pl.when(i == 0)
        def _():
            dg_ref[...] = jnp.zeros_like(dg_ref)
            own, sends, _ = _chip_scatter_plan(h_refs, s_ref, q_refs, sq_ref, *sems)
            for cp in own + sends:
                cp.start()

        dhn = (jnp.dot(dp_ref[:, :E_ZA], we_ref[:E_ZA, :], preferred_element_type=F32)
               + jnp.dot(dp_ref[:, E_ZA:MAIN_COLS // 2], w_ref[...], preferred_element_type=F32)
               + jnp.dot(dp_ref[:, MAIN_COLS // 2:], we_ref[E_ZA:, :], preferred_element_type=F32)
               + jnp.dot(df_ref[...], wf_ref[...], preferred_element_type=F32))
        h0 = jnp.where(i == 0, mp_ref[...], x_ref[...])
        r = lax.rsqrt(jnp.mean(h0 * h0, axis=-1, keepdims=True) + RMS_EPS)
        xhat = h0 * r
        dg_ref[...] += jnp.sum(dhn * xhat, axis=0, keepdims=True)
        dxhat = dhn * g_ref[...]
        dh0 = dh_ref[...] + r * (dxhat - xhat * jnp.mean(dxhat * xhat, axis=-1, keepdims=True))
        gx_ref[...] = dh0

        @pl.when(i == 0)
        def _():
            g0_ref[...] = dh0

        @pl.when(i == nb - 1)
        def _():
            own, sends, recvs = _chip_scatter_plan(h_refs, s_ref, q_refs, sq_ref, *sems)
            for cp in recvs:
                cp.wait_recv()
            for cp in sends:
                cp.wait_send()
            for cp in own:
                cp.wait()

    const = lambda shape: pl.BlockSpec(shape, lambda i: (0, 0))
    res = pl.pallas_call(
        body, name="input_bwd", grid=(nb,),
        in_specs=[pl.BlockSpec((ROW_TILE, MAIN_COLS), lambda i: (i, 0)), pl.BlockSpec((ROW_TILE, F_COLS), lambda i: (i, 0)),
                  const((E_COLS, D_MODEL)), const((QKV_COLS, D_MODEL)), const((F_COLS, D_MODEL)),
                  _tokens_spec(), const((ROW_TILE, D_MODEL)),
                  pl.BlockSpec((ROW_TILE, D_MODEL), lambda i: (i, 0)), const((1, D_MODEL))] + [HBM] * nq,
        out_specs=[_tokens_spec(), const((ROW_TILE, D_MODEL)), const((1, D_MODEL))] + [HBM] * nq,
        out_shape=[jax.ShapeDtypeStruct(x2.shape, F32), jax.ShapeDtypeStruct((ROW_TILE, D_MODEL), F32),
                   jax.ShapeDtypeStruct((1, D_MODEL), F32)] + sc_shapes,
        scratch_shapes=sc_sems,
        compiler_params=_params(("arbitrary",), vmem=56 * 1024 * 1024),
    )(dproj, df, wt_e, wt_qkv, wt_f, x2, metapad, dh1, g1, *_in_hbm(*sc_in))
    return res[:3], res[3:3 + n], (res[3 + n] if sc is not None else None)


def _adamw(w, g, m, v, name):
    rows, cols = w.shape
    if rows % 8 == 0:
        tr, tc = _row_tile8(rows), cols
    else:
        tr, tc = rows, (2 * LANES if cols % (2 * LANES) == 0 and rows > 8 else cols)

    def body(w_ref, g_ref, m_ref, v_ref, d_ref, mo_ref, vo_ref):
        g_ = g_ref[...]
        m_new = ADAM_B1 * m_ref[...] + (1.0 - ADAM_B1) * g_
        v_new = ADAM_B2 * v_ref[...] + (1.0 - ADAM_B2) * (g_ * g_)
        m_hat = m_new / (1.0 - ADAM_B1 ** ADAM_STEP)
        v_hat = v_new / (1.0 - ADAM_B2 ** ADAM_STEP)
        d_ref[...] = -ADAM_LR * (m_hat / (jnp.sqrt(v_hat) + ADAM_EPS) + ADAM_WD * w_ref[...])
        mo_ref[...] = m_new
        vo_ref[...] = v_new

    blk = pl.BlockSpec((tr, tc), lambda i, j: (i, j))
    return pl.pallas_call(
        body, name=name, grid=(rows // tr, cols // tc),
        in_specs=[blk] * 4, out_specs=[blk] * 3,
        out_shape=[jax.ShapeDtypeStruct(w.shape, F32)] * 3,
        compiler_params=_params(("parallel", "parallel")),
    )(w, g, m, v)


def _adamw_native(w3, g3, m3, v3, name):
    rows = w3.shape[0]
    tr = rows // 2
    blk = pl.BlockSpec((tr,) + w3.shape[1:], lambda i: (i, 0, 0))
    shape = jax.ShapeDtypeStruct(w3.shape, F32)

    def moments(g_ref, m_ref, v_ref, mo_ref, vo_ref):
        g_ = g_ref[...]
        mo_ref[...] = ADAM_B1 * m_ref[...] + (1.0 - ADAM_B1) * g_
        vo_ref[...] = ADAM_B2 * v_ref[...] + (1.0 - ADAM_B2) * (g_ * g_)

    new_m, new_v = pl.pallas_call(
        moments, name=name + "_moments", grid=(2,), in_specs=[blk] * 3, out_specs=[blk] * 2, out_shape=[shape] * 2,
        compiler_params=_params(("parallel",)),
    )(g3, m3, v3)

    def delta(w_ref, m_ref, v_ref, d_ref):
        m_hat = m_ref[...] / (1.0 - ADAM_B1 ** ADAM_STEP)
        v_hat = v_ref[...] / (1.0 - ADAM_B2 ** ADAM_STEP)
        d_ref[...] = -ADAM_LR * (m_hat / (jnp.sqrt(v_hat) + ADAM_EPS) + ADAM_WD * w_ref[...])

    d = pl.pallas_call(
        delta, name=name + "_delta", grid=(2,), in_specs=[blk] * 3, out_specs=blk, out_shape=shape,
        compiler_params=_params(("parallel",)),
    )(w3, new_m, new_v)
    return d, new_m, new_v


def _row_tile8(rows):
    best = rows
    for t in range(8, 257, 8):
        if rows % t == 0:
            best = t
    return best


def kernel(x, meta_tokens, norm_g, w_in, b_forget, pool_w, pool_scale, w_up_pool, w_up_attn, w_out, final_norm_g, loss_target, m_meta_tokens, m_norm_g, m_w_in, m_b_forget, m_pool_w, m_pool_scale, m_w_up_pool, m_w_up_attn, m_w_out, m_final_norm_g, v_meta_tokens, v_norm_g, v_w_in, v_b_forget, v_pool_w, v_pool_scale, v_w_up_pool, v_w_up_attn, v_w_out, v_final_norm_g):
    seq = x.shape[1]
    assert seq % ROW_TILE == 0 and x.shape[0] == 1
    lp = seq + ROW_TILE
    nb = lp // ROW_TILE
    lk = -(-lp // KV_TILE_BWD) * KV_TILE_BWD
    core = jnp.reshape(lax.axis_index("c"), (1,)).astype(jnp.int32)
    x2 = x[0]
    target = loss_target[0]
    sh_d = D_MODEL // N_CHIPS

    to_rows = lambda a: jnp.transpose(a, (2, 0, 1))
    from_rows = lambda a: jnp.transpose(a, (1, 2, 0))
    gf = final_norm_g.reshape(1, D_MODEL)
    bfg = jnp.pad(b_forget, ((0, 0), (0, F_COLS - N_HEADS)))
    pw_b = pool_w[0].astype(BF16)
    hn, (wg_in, meta_g) = _rmsnorm_fwd(x2, norm_g, lk, [jnp.transpose(w_in[0]).astype(BF16), meta_tokens], [1, 0])
    wt_e, wt_qkv, wt_f = _weight_sections(wg_in)
    meta_full = jnp.transpose(meta_g, (1, 0, 2)).reshape(N_META, D_MODEL)
    metapad = jnp.pad(meta_full, ((PAD_ROWS, 0), (0, 0)))

    tm = _row_tile(lp, 2200)
    e, (wg_up_p, wg_up_a, wg_out) = _mm_nt(
        hn, wt_e, BF16, "in_proj_gates", lp, E_COLS, tm, 512,
        gather=([w_up_pool[0].astype(BF16), w_up_attn[0].astype(BF16), w_out[0].astype(BF16)], [0, 0, 0]))
    wup_p = jnp.transpose(wg_up_p, (1, 0, 2)).reshape(POOL_WIDTH, D_MODEL)
    wup_a = jnp.transpose(wg_up_a, (1, 0, 2)).reshape(ATTN_WIDTH, D_MODEL)
    wout = wg_out.reshape(D_MODEL, D_MODEL)
    qkv = _mm_nt(hn, wt_qkv, BF16, "in_proj_qkv", lk, QKV_COLS, _row_tile(lk, 2600), 512, scale_first=HEAD_DIM ** -0.5)
    f = _mm_nt(hn, wt_f, F32, "in_proj_forget", lp, F_COLS, tm, F_COLS)
    c = _gates_fwd(f, bfg)
    c_t = jnp.transpose(c[:, :N_HEADS])
    c_first = jnp.transpose(c_t[:, ::ROW_TILE]).reshape(nb, N_HEADS // 2, 2, 1)
    c_keys = jnp.pad(c_t, ((0, 0), (0, lk - lp))).reshape(1, N_HEADS // 2, 2, lk)
    bias = jnp.where(jnp.arange(lk) < PAD_ROWS, -NEG, c_keys - c_first)
    y_pool = _pool_fwd(e, pw_b, pool_scale)
    o, lse = _attn_fwd(qkv, bias, nb)
    merged, y_attn = _merge_fwd(y_pool, o, e, wup_p, wup_a)
    dh1, loss_part, dgf = _head_fwd_bwd(merged, wout, x2, metapad, gf, target)

    dap, daa, dproj_half, do, delta, dy_pool = _merge_bwd(dh1, wout, y_pool, y_attn, wup_p, wup_a, e, o)
    dpc, dzp, dscale, dpw = _pool_bwd_local(e, dy_pool, pw_b, pool_scale)
    du = _pool_bwd_window(dpc)
    dq, dk, dv, dc4, dcq = _attn_bwd(qkv, do, lse, delta, bias, nb)
    dc = jnp.transpose(dc4, (1, 3, 0, 2)).reshape(lk, N_HEADS)[:lp]
    df, db = _gates_bwd(jnp.pad(dc, ((0, 0), (0, F_COLS - N_HEADS))), dcq, f, bfg)
    dproj = _fill_dproj(dproj_half, du, dzp, dq, dk, dv)
    tk = _row_tile(lp, 1100)
    dw_out = _mm_tn(merged, dh1, "grad_w_out", lp, 512, D_MODEL, tk)
    dw_up_p = _mm_tn(y_pool, dap, "grad_w_up_pool", lp, 512, D_MODEL, lp, chunks=N_CHIPS)
    dw_up_a = _mm_tn(y_attn, daa, "grad_w_up_attn", lp, 512, D_MODEL, lp, chunks=N_CHIPS)
    dwt_f = _mm_tn(df, hn, "grad_w_in_forget", lp, F_COLS, D_MODEL, tk)

    def pad8(a):
        return jnp.pad(a, ((0, (-a.shape[0]) % 8), (0, 0)))

    small_parts = [pad8(a) for a in (dgf.reshape(-1, LANES), dscale.reshape(-1, LANES), db, dpw.reshape(-1, LANES),
                                     loss_part)]
    small = jnp.concatenate(small_parts, axis=0)
    soffs = [0]
    for p in small_parts:
        soffs.append(soffs[-1] + p.shape[0])

    gs_rows = [dw_up_p, dw_up_a, dw_out.reshape(N_CHIPS, sh_d, D_MODEL)]
    half = MAIN_COLS // 2
    dwt_a, (*rb_rows, rb_f, sr) = _mm_tn(dproj, hn, "grad_w_in_a", lp, half // 4, D_MODEL, lp, m=half, m_off=0,
                                         swap=(gs_rows + [dwt_f, small], [1, 1, 1, 1, None]))
    *hs_rows, sc = _add_sibling_half(gs_rows, rb_rows, [1, 1, 1], small, sr, core)
    dwt_b, (rb_a,), (qs_rows, sq) = _mm_tn(dproj, hn, "grad_w_in_b", lp, half // 4, D_MODEL, lp, m=half, m_off=4,
                                           swap=([dwt_a], [1]), scatter=(hs_rows, sc))
    (rb_b,) = _sibling_exchange([dwt_b], [1])
    h_in = _add_halves_w_in([dwt_a, dwt_b, dwt_f], [rb_a, rb_b, rb_f], core)
    grad_axes = [2, 1, 1, 1]
    (grad_x, g_block0, dg1), (q_in,), _ = _input_bwd(dproj, df, wt_e, wt_qkv, wt_f, x2, metapad, dh1, norm_g, [h_in])
    g_w_in_rows, g_w_up_p, g_w_up_a, g_w_out, st = _reduce_allgather([q_in] + list(qs_rows), grad_axes, sq,
                                                                       [True, False, False, False])
    late = _small_allreduce(jnp.concatenate([pad8(dg1.reshape(-1, LANES)), g_block0[PAD_ROWS:].reshape(-1, LANES)], axis=0))
    n_norm = D_MODEL // LANES
    g_norm = late[:n_norm].reshape(1, D_MODEL)
    chip = 2 * lax.axis_index("x") + lax.axis_index("y")
    g_meta = lax.dynamic_slice_in_dim(late[n_norm:].reshape(N_META, D_MODEL), chip * sh_d, sh_d, axis=1)

    spiece = lambda k, rows: st[soffs[k]:soffs[k] + rows]
    g_final = spiece(0, D_MODEL // LANES).reshape(1, D_MODEL)
    g_scale = spiece(1, POOL_WIDTH // LANES).reshape(1, POOL_WIDTH)
    g_bf = spiece(2, 1)
    g_pw = spiece(3, POOL_WIDTH)
    loss = st[soffs[4], 0]

    def pad_lanes(a):
        return jnp.pad(a, ((0, 0), (0, F_COLS - N_HEADS)))

    w_in_res = (g_w_in_rows,) + _adamw_native(to_rows(w_in), g_w_in_rows, to_rows(m_w_in), to_rows(v_w_in), "adamw_w_in")

    upd = [
        ("meta_tokens", meta_tokens, g_meta, m_meta_tokens, v_meta_tokens),
        ("norm_g", norm_g, g_norm, m_norm_g, v_norm_g),
        ("w_in", None, None, None, None),
        ("b_forget", pad_lanes(b_forget), g_bf, pad_lanes(m_b_forget), pad_lanes(v_b_forget)),
        ("pool_w", pool_w.reshape(-1, LANES), g_pw, m_pool_w.reshape(-1, LANES), v_pool_w.reshape(-1, LANES)),
        ("pool_scale", pool_scale, g_scale, m_pool_scale, v_pool_scale),
        ("w_up_pool", w_up_pool[0], g_w_up_p, m_w_up_pool[0], v_w_up_pool[0]),
        ("w_up_attn", w_up_attn[0], g_w_up_a, m_w_up_attn[0], v_w_up_attn[0]),
        ("w_out", w_out[0], g_w_out, m_w_out[0], v_w_out[0]),
        ("final_norm_g", gf, g_final, m_final_norm_g.reshape(1, D_MODEL), v_final_norm_g.reshape(1, D_MODEL)),
    ]
    shapes = [meta_tokens.shape, norm_g.shape, w_in.shape, b_forget.shape, pool_w.shape, pool_scale.shape,
              w_up_pool.shape, w_up_attn.shape, w_out.shape, final_norm_g.shape]
    grads, deltas, new_ms, new_vs = [], [], [], []
    for (name, w_, g_, m_in, v_in), shp in zip(upd, shapes):
        if name == "w_in":
            res = tuple(from_rows(a) for a in w_in_res)
        else:
            d_, mn_, vn_ = _adamw(w_, g_, m_in, v_in, "adamw_" + name)
            res = (g_, d_, mn_, vn_)
        if name == "b_forget":
            res = tuple(a[:, :N_HEADS] for a in res)
        for lst, a in zip((grads, deltas, new_ms, new_vs), res):
            lst.append(a.reshape(shp))

    return (loss, grad_x.reshape(x.shape), *grads, *deltas, *new_ms, *new_vs)
```

```python
import jax
import jax.numpy as jnp
from jax import lax
from jax.experimental import pallas as pl
from jax.experimental.pallas import tpu as pltpu

F32 = jnp.float32
BF16 = jnp.bfloat16
MESH = pl.DeviceIdType.MESH
HIGHEST = lax.Precision.HIGHEST
HBM = pl.BlockSpec(memory_space=pltpu.HBM)

D_MODEL = 1024
N_META = 16
POOL_WIDTH = 512
POOL_GROUP = 128
POOL_WINDOWS = (2, 4, 8, 16)
N_HEADS = 8
HEAD_DIM = 64
ATTN_WIDTH = 512
RMS_EPS = 1e-6
N_CHIPS = 4

ADAM_LR = 0.001
ADAM_B1 = 0.9
ADAM_B2 = 0.999
ADAM_EPS = 1e-08
ADAM_WD = 0.01
ADAM_STEP = 10

LANES = 128
ROW_TILE = 256
KV_TILE = 1024
KV_TILE_BWD = 1024
PAD_ROWS = ROW_TILE - N_META
NEG = -1e30

E_U, E_ZP, E_ZA, E_GP, E_GA, E_COLS = 0, 512, 1024, 1536, 2560, 3584
QKV_COLS = 3 * ATTN_WIDTH
MAIN_COLS = E_COLS + QKV_COLS
F_COLS = LANES
REF_U, REF_Q, REF_ZA, REF_F, REF_GP = 0, 1024, 2560, 3072, 3080
VMEM_LIMIT = 48 * 1024 * 1024


def _params(sem=None, vmem=VMEM_LIMIT):
    return pltpu.CompilerParams(dimension_semantics=sem, vmem_limit_bytes=vmem)


def _in_hbm(*arrays):
    return [pltpu.with_memory_space_constraint(a, pltpu.HBM) for a in arrays]


def _row_tile(n, target):
    best = 16
    for t in range(16, target + 1, 16):
        if n % t == 0:
            best = t
    return best


def _sigmoid(x):
    return 1.0 / (1.0 + jnp.exp(-x))


def _half(ref, axis, which):
    n = ref.shape[axis] // 2
    idx = [slice(None)] * len(ref.shape)
    idx[axis] = pl.ds(pl.multiple_of(which * n, n), n)
    return ref.at[tuple(idx)]


def _half_shape(shape, axis):
    s = list(shape)
    s[axis] //= 2
    return tuple(s)


def _gather_start(ins, outs, axes, send, recv, fsend, frecv, local):
    x, y, c = lax.axis_index("x"), lax.axis_index("y"), lax.axis_index("c")
    me = 2 * x + y
    for t in range(len(ins)):
        pltpu.make_async_copy(ins[t], outs[t].at[me], local.at[t]).start()
        for k, chip in enumerate([(1 - x, y), (x, 1 - y), (1 - x, 1 - y)]):
            pltpu.make_async_remote_copy(
                src_ref=_half(ins[t], axes[t], c), dst_ref=_half(outs[t].at[me], axes[t], c),
                send_sem=send.at[3 * t + k], recv_sem=recv.at[3 * t + k], device_id=(*chip, c), device_id_type=MESH).start()


def _gather_finish(ins, outs, axes, send, recv, fsend, frecv, local):
    x, y, c = lax.axis_index("x"), lax.axis_index("y"), lax.axis_index("c")
    me = 2 * x + y
    sibling = (x, y, 1 - c)
    chips = [(1 - x, y), (x, 1 - y), (1 - x, 1 - y)]
    n = len(ins)

    def over_ici(t, k, chip, dst_slot):
        return pltpu.make_async_remote_copy(
            src_ref=_half(ins[t], axes[t], c), dst_ref=_half(outs[t].at[dst_slot], axes[t], c),
            send_sem=send.at[3 * t + k], recv_sem=recv.at[3 * t + k], device_id=(*chip, c), device_id_type=MESH)

    def to_sibling(t, k, slot, which):
        return pltpu.make_async_remote_copy(
            src_ref=_half(outs[t].at[slot], axes[t], which), dst_ref=_half(outs[t].at[slot], axes[t], which),
            send_sem=fsend.at[3 * t + k], recv_sem=frecv.at[3 * t + k], device_id=sibling, device_id_type=MESH)

    forwards = []
    for t in range(n):
        for k, (px, py) in enumerate(chips):
            over_ici(t, k, (px, py), 2 * px + py).wait_recv()
            fw = to_sibling(t, k, 2 * px + py, c)
            fw.start()
            forwards.append(fw)
    for t in range(n):
        for k, (px, py) in enumerate(chips):
            to_sibling(t, k, 2 * px + py, 1 - c).wait_recv()
    for t in range(n):
        for k, chip in enumerate(chips):
            over_ici(t, k, chip, me).wait_send()
    for fw in forwards:
        fw.wait_send()
    for t in range(n):
        pltpu.make_async_copy(ins[t], outs[t].at[me], local.at[t]).wait()


def _gather_sems(n):
    return [pltpu.SemaphoreType.DMA((3 * n,)), pltpu.SemaphoreType.DMA((3 * n,)), pltpu.SemaphoreType.DMA((3 * n,)),
            pltpu.SemaphoreType.DMA((3 * n,)), pltpu.SemaphoreType.DMA((n,))]


def _gathered_shapes(shards):
    return [jax.ShapeDtypeStruct((N_CHIPS,) + s.shape, s.dtype) for s in shards]


def _swap_copies(srcs, dsts, axes, send, recv):
    x, y, c = lax.axis_index("x"), lax.axis_index("y"), lax.axis_index("c")
    return [pltpu.make_async_remote_copy(
        src_ref=srcs[t] if axes[t] is None else _half(srcs[t], axes[t], 1 - c), dst_ref=dsts[t],
        send_sem=send.at[t], recv_sem=recv.at[t], device_id=(x, y, 1 - c), device_id_type=MESH) for t in range(len(srcs))]


def _swap_shapes(srcs, axes):
    return [jax.ShapeDtypeStruct(g.shape if a is None else _half_shape(g.shape, a), g.dtype) for g, a in zip(srcs, axes)]


def _sibling_exchange(gs, axes):
    n = len(gs)

    def body(*refs):
        cps = _swap_copies(refs[:n], refs[n:2 * n], axes, *refs[2 * n:])
        for cp in cps:
            cp.start()
        for cp in cps:
            cp.wait()

    return pl.pallas_call(
        body, name="grad_sibling_exchange",
        in_specs=[HBM] * n, out_specs=[HBM] * n, out_shape=_swap_shapes(gs, axes),
        scratch_shapes=[pltpu.SemaphoreType.DMA((n,)), pltpu.SemaphoreType.DMA((n,))],
    )(*_in_hbm(*gs))


def _add_halves_w_in(parts, rbs, order, core):
    n_parts, w = len(parts), rbs[0].shape[1]
    per = sum(hi - lo for _, lo, hi in order) // N_CHIPS
    used = [max(hi for t, _, hi in order if t == s) for s in range(n_parts)]
    pieces = [[] for _ in range(N_CHIPS)]
    r = 0
    for src, lo, hi in order:
        while lo < hi:
            c, at = divmod(r, per)
            n = min(hi - lo, per - at)
            pieces[c].append((src, lo, n, at))
            lo, r = lo + n, r + n

    def body(core_ref, *refs):
        p_refs, r_refs, o_ref = refs[:n_parts], refs[n_parts:2 * n_parts], refs[2 * n_parts]
        mine, other = refs[2 * n_parts + 1:3 * n_parts + 1], refs[3 * n_parts + 1:4 * n_parts + 1]
        rows_buf, sems = refs[4 * n_parts + 1:]
        cols = pl.ds(pl.multiple_of(core_ref[0] * w, w), w)
        loads = [(pltpu.make_async_copy(p_refs[t].at[pl.ds(0, used[t]), cols], mine[t], sems.at[2 * t]),
                  pltpu.make_async_copy(r_refs[t].at[pl.ds(0, used[t]), :], other[t], sems.at[2 * t + 1]))
                 for t in range(n_parts)]
        for pair in loads:
            for cp in pair:
                cp.start()
        there = set()
        for c in range(N_CHIPS):
            for src, lo, n, at in pieces[c]:
                if src not in there:
                    for cp in loads[src]:
                        cp.wait()
                    there.add(src)
                rows_buf[at:at + n, :] = mine[src][lo:lo + n, :] + other[src][lo:lo + n, :]
            o_ref[c] = rows_buf[0:per, :].astype(BF16)

    any_space = pl.BlockSpec(memory_space=pl.ANY)
    return pl.pallas_call(
        body, name="grad_add_sibling_w_in",
        grid_spec=pltpu.PrefetchScalarGridSpec(
            num_scalar_prefetch=1, grid=(1,),
            in_specs=[any_space] * (2 * n_parts),
            out_specs=pl.BlockSpec((N_CHIPS, per, w), lambda i, cr: (0, 0, 0)),
            scratch_shapes=[pltpu.VMEM((rows, w), F32) for rows in used] * 2
            + [pltpu.VMEM((per + (-per) % 8, w), F32), pltpu.SemaphoreType.DMA((2 * n_parts,))]),
        out_shape=jax.ShapeDtypeStruct((N_CHIPS, per, w), BF16),
        compiler_params=_params(("arbitrary",)),
    )(core, *parts, *rbs)


def _weight_sections(wg):
    per, d = wg.shape[1], wg.shape[2]
    n_ref = N_CHIPS * per
    units = [(0, 0, REF_U, REF_Q), (1, 0, REF_Q, REF_ZA), (0, REF_Q - REF_U, REF_ZA, REF_F), (2, 0, REF_F, REF_GP),
             (0, REF_Q - REF_U + REF_F - REF_ZA, REF_GP, n_ref)]
    plan = []
    for out, first, lo, hi in sorted(units, key=lambda u: u[2]):
        pieces, at = [], 0
        while lo < hi:
            c, src = divmod(lo, per)
            n = min(hi - lo, per - src)
            pieces.append((c, src, n, at))
            lo, at = lo + n, at + n
        plan.append((out, first, at, pieces))
    stage_rows = max(rows for _, _, rows, _ in plan)

    def body(g_ref, e_ref, q_ref, f_ref, gb, stage, sems):
        loads = [pltpu.make_async_copy(g_ref.at[c], gb.at[c], sems.at[c]) for c in range(N_CHIPS)]
        for cp in loads:
            cp.start()
        outs = [e_ref, q_ref, f_ref]
        there = set()
        for out, first, rows, pieces in plan:
            for c, _, _, _ in pieces:
                if c not in there:
                    loads[c].wait()
                    there.add(c)
            if len(pieces) == 1 and pieces[0][1] % 16 == 0 and rows % 16 == 0:
                c, src, _, _ = pieces[0]
                outs[out][first:first + rows, :] = gb[c, src:src + rows, :]
                continue
            for c, src, n, at in pieces:
                stage[at:at + n, :] = gb[c, src:src + n, :].astype(F32)
            padded = rows if rows % 16 == 0 else outs[out].shape[0] - first
            if padded > rows:
                stage[rows:padded, :] = jnp.zeros((padded - rows, d), F32)
            outs[out][first:first + padded, :] = stage[0:padded, :].astype(BF16)

    return pl.pallas_call(
        body, name="weight_sections",
        in_specs=[pl.BlockSpec(memory_space=pl.ANY)],
        out_shape=[jax.ShapeDtypeStruct((E_COLS, d), BF16), jax.ShapeDtypeStruct((QKV_COLS, d), BF16),
                   jax.ShapeDtypeStruct((F_COLS, d), BF16)],
        scratch_shapes=[pltpu.VMEM(wg.shape, BF16), pltpu.VMEM((stage_rows, d), F32), pltpu.SemaphoreType.DMA((N_CHIPS,))],
        compiler_params=_params(),
    )(wg)


def _add_sibling_half(gs, rbs, axes, small, sr, core):
    n = len(gs)

    def body(core_ref, *refs):
        g_refs, rb_refs = refs[:n], refs[n:2 * n]
        s_ref, sr_ref = refs[2 * n], refs[2 * n + 1]
        h_refs, sc_ref = refs[2 * n + 2:3 * n + 2], refs[3 * n + 2]
        for t in range(n):
            h_refs[t][...] = (g_refs[t][...] + rb_refs[t][...]).astype(BF16)
        sc_ref[...] = s_ref[...] + sr_ref[...]

    def mine(rb, axis):
        blk = (None,) + rb.shape[1:]
        if axis == 2:
            return pl.BlockSpec(blk, lambda j, cr: (j, 0, cr[0]))
        return pl.BlockSpec(blk, lambda j, cr: (j, cr[0], 0))

    chunk = lambda rb: pl.BlockSpec((None,) + rb.shape[1:], lambda j, cr: (j, 0, 0))
    whole = pl.BlockSpec(small.shape, lambda j, cr: (0, 0))
    return pl.pallas_call(
        body, name="grad_add_sibling",
        grid_spec=pltpu.PrefetchScalarGridSpec(
            num_scalar_prefetch=1, grid=(N_CHIPS,),
            in_specs=[mine(rb, a) for rb, a in zip(rbs, axes)] + [chunk(rb) for rb in rbs] + [whole, whole],
            out_specs=[chunk(rb) for rb in rbs] + [whole]),
        out_shape=[jax.ShapeDtypeStruct(rb.shape, BF16) for rb in rbs] + [jax.ShapeDtypeStruct(small.shape, F32)],
        compiler_params=_params(("arbitrary",)),
    )(core, *gs, *rbs, small, sr)


def _chip_scatter_plan(h_refs, s_ref, q_refs, sq_ref, send, recv, local):
    n = len(h_refs)
    nt = n + (s_ref is not None)
    x, y, c = lax.axis_index("x"), lax.axis_index("y"), lax.axis_index("c")
    me = 2 * x + y
    chips = [(1 - x, y), (x, 1 - y), (1 - x, 1 - y)]

    def copy(t, k, chip, src_slot, dst_slot):
        src = h_refs[t].at[src_slot] if t < n else s_ref
        dst = (q_refs[t] if t < n else sq_ref).at[dst_slot]
        return pltpu.make_async_remote_copy(src_ref=src, dst_ref=dst, send_sem=send.at[3 * t + k],
                                            recv_sem=recv.at[3 * t + k], device_id=(*chip, c), device_id_type=MESH)

    own = [pltpu.make_async_copy(h_refs[t].at[me], q_refs[t].at[me], local.at[t]) for t in range(n)]
    if s_ref is not None:
        own.append(pltpu.make_async_copy(s_ref, sq_ref.at[me], local.at[n]))
    sends = [copy(t, k, (px, py), 2 * px + py, me) for t in range(nt) for k, (px, py) in enumerate(chips)]
    recvs = [copy(t, k, (px, py), me, 2 * px + py) for t in range(nt) for k, (px, py) in enumerate(chips)]
    return own, sends, recvs


def _scatter_operands(hs, sc):
    small = [] if sc is None else [sc]
    nt = len(hs) + len(small)
    shapes = [jax.ShapeDtypeStruct(h.shape, h.dtype) for h in hs] + [
        jax.ShapeDtypeStruct((N_CHIPS,) + a.shape, a.dtype) for a in small]
    sems = [pltpu.SemaphoreType.DMA((3 * nt,)), pltpu.SemaphoreType.DMA((3 * nt,)), pltpu.SemaphoreType.DMA((nt,))]
    return list(hs) + small, shapes, sems


def _small_allreduce(buf):
    def body(b_ref, o_ref, sib_buf, chip_buf, send, recv):
        x, y, c = lax.axis_index("x"), lax.axis_index("y"), lax.axis_index("c")
        me = 2 * x + y
        chips = [(1 - x, y), (x, 1 - y), (1 - x, 1 - y)]
        swap = pltpu.make_async_remote_copy(src_ref=b_ref, dst_ref=sib_buf, send_sem=send.at[0], recv_sem=recv.at[0],
                                            device_id=(x, y, 1 - c), device_id_type=MESH)
        swap.start()
        swap.wait()
        chip_buf[me] = b_ref[...] + sib_buf[...]

        def copy(k, chip, slot):
            return pltpu.make_async_remote_copy(src_ref=chip_buf.at[slot], dst_ref=chip_buf.at[slot], send_sem=send.at[1 + k],
                                                recv_sem=recv.at[1 + k], device_id=(*chip, c), device_id_type=MESH)

        sends = [copy(k, chip, me) for k, chip in enumerate(chips)]
        for cp in sends:
            cp.start()
        for k, (px, py) in enumerate(chips):
            copy(k, (px, py), 2 * px + py).wait_recv()
        for cp in sends:
            cp.wait_send()
        o_ref[...] = ((chip_buf[0] + chip_buf[1]) + chip_buf[2]) + chip_buf[3]

    vmem = pl.BlockSpec(memory_space=pltpu.VMEM)
    return pl.pallas_call(
        body, name="small_allreduce", in_specs=[vmem], out_specs=vmem,
        out_shape=jax.ShapeDtypeStruct(buf.shape, F32),
        scratch_shapes=[pltpu.VMEM(buf.shape, F32), pltpu.VMEM((N_CHIPS,) + buf.shape, F32),
                        pltpu.SemaphoreType.DMA((4,)), pltpu.SemaphoreType.DMA((4,))],
        compiler_params=_params(),
    )(buf)


def _reduce_allgather(qs, axes, sq, as_rows):
    n = len(qs)
    shard_shapes, half_axes = [], []
    for q, a, rows_form in zip(qs, axes, as_rows):
        shape = [d * 2 if i == a - 1 else d for i, d in enumerate(q.shape[1:])]
        if rows_form:
            assert a == 2
            shape = [shape[0], 1, shape[1]]
        shard_shapes.append(tuple(shape))
        half_axes.append(2 if rows_form else a - 1)

    def body(*refs):
        q_refs, sq_ref = refs[:n], refs[n]
        o_refs, st_ref = refs[n + 1:2 * n + 1], refs[2 * n + 1]
        send, recv = refs[2 * n + 2:]
        x, y, c = lax.axis_index("x"), lax.axis_index("y"), lax.axis_index("c")

        def swap(t, which):
            return pltpu.make_async_remote_copy(
                src_ref=_half(o_refs[t], half_axes[t], which), dst_ref=_half(o_refs[t], half_axes[t], which),
                send_sem=send.at[t], recv_sem=recv.at[t], device_id=(x, y, 1 - c), device_id_type=MESH)

        sent = []
        for t in range(n):
            q = q_refs[t]
            total = ((q[0].astype(F32) + q[1].astype(F32)) + q[2].astype(F32)) + q[3].astype(F32)
            if as_rows[t]:
                total = total.reshape(total.shape[0], 1, total.shape[1])
            _half(o_refs[t], half_axes[t], c)[...] = total
            cp = swap(t, c)
            cp.start()
            sent.append(cp)
        st_ref[...] = ((sq_ref[0] + sq_ref[1]) + sq_ref[2]) + sq_ref[3]
        for t in range(n):
            swap(t, 1 - c).wait_recv()
        for cp in sent:
            cp.wait_send()

    vmem = pl.BlockSpec(memory_space=pltpu.VMEM)
    return pl.pallas_call(
        body, name="grad_reduce_allgather",
        in_specs=[vmem] * (n + 1), out_specs=[vmem] * (n + 1),
        out_shape=[jax.ShapeDtypeStruct(s, F32) for s in shard_shapes] + [jax.ShapeDtypeStruct(sq.shape[1:], F32)],
        scratch_shapes=[pltpu.SemaphoreType.DMA((n,)), pltpu.SemaphoreType.DMA((n,))],
        compiler_params=_params(),
    )(*qs, sq)


def _dot_nt(a, b):
    return lax.dot_general(a, b, (((1,), (1,)), ((), ())), preferred_element_type=F32)


def _dot_tn(a, b):
    return lax.dot_general(a, b, (((0,), (0,)), ((), ())), preferred_element_type=F32)


def _mm_nt(a, bt, out_dtype, name, m, n, tm, tn, row_block=0, scale_first=None, gather=None):
    k = a.shape[1]
    shards, axes = gather if gather is not None else ([], [])
    ng = len(shards)
    grid = (m // tm, n // tn)

    def body(a_ref, b_ref, *rest):
        ins, o_ref, outs, sems = rest[:ng], rest[ng], rest[ng + 1:2 * ng + 1], rest[2 * ng + 1:]
        first = (pl.program_id(0) == 0) & (pl.program_id(1) == 0)
        last = (pl.program_id(0) == grid[0] - 1) & (pl.program_id(1) == grid[1] - 1)
        if ng:
            @pl.when(first)
            def _():
                _gather_start(ins, outs, axes, *sems)

        r = _dot_nt(a_ref[...], b_ref[...])
        if scale_first is not None:
            r = r * jnp.where(pl.program_id(1) == 0, scale_first, 1.0)
        o_ref[...] = r.astype(out_dtype)

        if ng:
            @pl.when(last)
            def _():
                _gather_finish(ins, outs, axes, *sems)

    res = pl.pallas_call(
        body, name=name, grid=grid,
        in_specs=[pl.BlockSpec((tm, k), lambda i, j: (i, 0)), pl.BlockSpec((tn, k), lambda i, j: (row_block + j, 0))]
        + [HBM] * ng,
        out_specs=[pl.BlockSpec((tm, tn), lambda i, j: (i, j))] + [HBM] * ng,
        out_shape=[jax.ShapeDtypeStruct((m, n), out_dtype)] + _gathered_shapes(shards),
        scratch_shapes=_gather_sems(ng) if ng else [],
        compiler_params=_params(("arbitrary", "arbitrary") if ng else ("parallel", "parallel")),
    )(a, bt, *_in_hbm(*shards))
    return (res[0], res[1:]) if ng else res[0]


def _mm_tn(a, b, name, k, tm, tn, tk, chunks=1, m=None, m_off=0, swap=None, scatter=None):
    m = a.shape[1] if m is None else m
    n = b.shape[1]
    cw = n // chunks
    srcs, axes = swap if swap is not None else ([], [])
    ns = len(srcs)
    hs, sc = scatter if scatter is not None else ([], None)
    sc_in, sc_shapes, sc_sems = _scatter_operands(hs, sc) if scatter is not None else ([], [], [])
    nh, nq = len(hs), len(sc_in)
    grid = (m // tm, n // tn, k // tk)

    def body(a_ref, b_ref, *rest):
        s_refs, h_refs = rest[:ns], rest[ns:ns + nq]
        o_ref = rest[ns + nq]
        d_refs, q_refs = rest[ns + nq + 1:2 * ns + nq + 1], rest[2 * ns + nq + 1:2 * ns + 2 * nq + 1]
        sems = rest[2 * ns + 2 * nq + 1:]
        swap_sems, scatter_sems = (sems[:2], sems[2:]) if ns else ((), sems)
        ids = [pl.program_id(d) for d in range(3)]

        def scatter_plan():
            small_in = h_refs[nh] if sc is not None else None
            small_out = q_refs[nh] if sc is not None else None
            return _chip_scatter_plan(h_refs[:nh], small_in, q_refs[:nh], small_out, *scatter_sems)

        if ns or nq:
            @pl.when((ids[0] == 0) & (ids[1] == 0) & (ids[2] == 0))
            def _():
                if ns:
                    for cp in _swap_copies(s_refs, d_refs, axes, *swap_sems):
                        cp.start()
                if nq:
                    own, sends, _ = scatter_plan()
                    for cp in own + sends:
                        cp.start()

        @pl.when(ids[2] == 0)
        def _():
            o_ref[...] = jnp.zeros_like(o_ref)
        r = _dot_tn(a_ref[...].astype(BF16), b_ref[...].astype(BF16))
        if chunks > 1:
            for c in range(chunks):
                o_ref[c] += r[:, c * cw:(c + 1) * cw]
        else:
            o_ref[...] += r

        if ns or nq:
            @pl.when((ids[0] == grid[0] - 1) & (ids[1] == grid[1] - 1) & (ids[2] == grid[2] - 1))
            def _():
                if ns:
                    for cp in _swap_copies(s_refs, d_refs, axes, *swap_sems):
                        cp.wait()
                if nq:
                    own, sends, recvs = scatter_plan()
                    for cp in recvs:
                        cp.wait_recv()
                    for cp in sends:
                        cp.wait_send()
                    for cp in own:
                        cp.wait()

    if chunks > 1:
        assert tn == n
        out_spec = pl.BlockSpec((chunks, tm, cw), lambda i, j, kk: (0, i, 0))
        out_shape = jax.ShapeDtypeStruct((chunks, m, cw), F32)
    else:
        out_spec = pl.BlockSpec((tm, tn), lambda i, j, kk: (i, j))
        out_shape = jax.ShapeDtypeStruct((m, n), F32)
    riders = ns + nq
    res = pl.pallas_call(
        body, name=name, grid=grid,
        in_specs=[pl.BlockSpec((tk, tm), lambda i, j, kk: (kk, m_off + i)), pl.BlockSpec((tk, tn), lambda i, j, kk: (kk, j))]
        + [HBM] * riders,
        out_specs=[out_spec] + [HBM] * riders, out_shape=[out_shape] + _swap_shapes(srcs, axes) + sc_shapes,
        scratch_shapes=([pltpu.SemaphoreType.DMA((ns,)), pltpu.SemaphoreType.DMA((ns,))] if ns else []) + sc_sems,
        compiler_params=_params(("arbitrary",) * 3 if riders else ("parallel", "parallel", "arbitrary")),
    )(a, b, *_in_hbm(*srcs, *sc_in))
    if not riders:
        return res[0]
    out = [res[0]]
    if ns:
        out.append(res[1:1 + ns])
    if nq:
        out.append((res[1 + ns:1 + ns + nh], res[1 + ns + nh] if sc is not None else None))
    return tuple(out)


def _tokens_spec():
    return pl.BlockSpec((ROW_TILE, D_MODEL), lambda i: (jnp.maximum(i - 1, 0), 0))


def _rmsnorm_fwd(x2, g1, lk, shards, axes):
    nb = x2.shape[0] // ROW_TILE + 1
    nblk = lk // ROW_TILE
    ng = len(shards)
    meta_cols = shards[-1].shape[1]

    def body(x_ref, g_ref, *rest):
        ins, hn_ref, outs = rest[:ng], rest[ng], rest[ng + 1:2 * ng + 1]
        sems, meta_buf, meta_sem = rest[2 * ng + 1:2 * ng + 6], rest[2 * ng + 6], rest[2 * ng + 7]
        s = pl.program_id(0)
        blk = (s + 1) % nblk

        @pl.when(s == 0)
        def _():
            _gather_start(ins, outs, axes, *sems)

        def normed(h):
            r = lax.rsqrt(jnp.mean(h * h, axis=-1, keepdims=True) + RMS_EPS)
            return ((h * r) * g_ref[...]).astype(BF16)

        @pl.when(s < nblk - 1)
        def _():
            hn_ref[...] = normed(jnp.where(blk >= nb, 0.0, x_ref[...]))

        @pl.when(s == nblk - 1)
        def _():
            _gather_finish(ins, outs, axes, *sems)
            fetch = pltpu.make_async_copy(outs[-1], meta_buf, meta_sem.at[0])
            fetch.start()
            fetch.wait()
            meta = jnp.concatenate([meta_buf[j] for j in range(N_CHIPS)], axis=1)
            hn_ref[...] = normed(jnp.concatenate([jnp.zeros((PAD_ROWS, D_MODEL), F32), meta], axis=0))

    res = pl.pallas_call(
        body, name="rmsnorm_fwd", grid=(nblk,),
        in_specs=[pl.BlockSpec((ROW_TILE, D_MODEL), lambda s: (jnp.clip((s + 1) % nblk - 1, 0, nb - 2), 0)),
                  pl.BlockSpec((1, D_MODEL), lambda s: (0, 0))] + [HBM] * ng,
        out_specs=[pl.BlockSpec((ROW_TILE, D_MODEL), lambda s: ((s + 1) % nblk, 0))] + [HBM] * ng,
        out_shape=[jax.ShapeDtypeStruct((lk, D_MODEL), BF16)] + _gathered_shapes(shards),
        scratch_shapes=_gather_sems(ng) + [pltpu.VMEM((N_CHIPS, N_META, meta_cols), F32), pltpu.SemaphoreType.DMA((1,))],
        compiler_params=_params(("arbitrary",)),
    )(x2, g1, *_in_hbm(*shards))
    return res[0], res[1:]


def _valid_gate_mask(i):
    row = i * ROW_TILE + lax.broadcasted_iota(jnp.int32, (ROW_TILE, F_COLS), 0)
    col = lax.broadcasted_iota(jnp.int32, (ROW_TILE, F_COLS), 1)
    return (row >= PAD_ROWS) & (col < N_HEADS)


def _gates_fwd(f, bfg):
    nb = f.shape[0] // ROW_TILE

    def body(f_ref, b_ref, c_ref, carry):
        i = pl.program_id(0)

        @pl.when(i == 0)
        def _():
            carry[...] = jnp.zeros_like(carry)

        logit = f_ref[...] + b_ref[...]
        lf = jnp.minimum(logit, 0.0) - jnp.log1p(jnp.exp(-jnp.abs(logit)))
        lf = jnp.where(_valid_gate_mask(i), lf, 0.0)
        r_i = lax.broadcasted_iota(jnp.int32, (ROW_TILE, ROW_TILE), 0)
        c_i = lax.broadcasted_iota(jnp.int32, (ROW_TILE, ROW_TILE), 1)
        tri = (c_i <= r_i).astype(F32)
        c_ref[...] = jnp.dot(tri, lf, precision=HIGHEST, preferred_element_type=F32) + carry[...]
        carry[...] = carry[...] + jnp.sum(lf, axis=0, keepdims=True)

    return pl.pallas_call(
        body, name="gates_fwd", grid=(nb,),
        in_specs=[pl.BlockSpec((ROW_TILE, F_COLS), lambda i: (i, 0)), pl.BlockSpec((1, F_COLS), lambda i: (0, 0))],
        out_specs=pl.BlockSpec((ROW_TILE, F_COLS), lambda i: (i, 0)),
        out_shape=jax.ShapeDtypeStruct(f.shape, F32),
        scratch_shapes=[pltpu.VMEM((1, F_COLS), F32)],
        compiler_params=_params(("arbitrary",)),
    )(f, bfg)


def _pool_counts(i):
    row = i * ROW_TILE + lax.broadcasted_iota(jnp.int32, (ROW_TILE, 1), 0)
    return jnp.maximum(row - PAD_ROWS, 0)


def _trailing_sums(xc, levels):
    acc = xc
    for lv in range(levels):
        acc = acc + pltpu.roll(acc, 1 << lv, 0)
    return acc


def _leading_sums(xc, levels):
    n = xc.shape[0]
    acc = xc
    for lv in range(levels):
        acc = acc + pltpu.roll(acc, n - (1 << lv), 0)
    return acc


def _pool_p(u_cur, u_prev, i):
    pos = _pool_counts(i)
    ps, invs = [], []
    for g, w in enumerate(POOL_WINDOWS):
        sl = slice(g * POOL_GROUP, (g + 1) * POOL_GROUP)
        cur = u_cur[:, sl]
        xc = jnp.concatenate([u_prev[:, sl], cur], axis=0)
        win = _trailing_sums(xc, g + 1)[ROW_TILE:, :]
        inv = 1.0 / jnp.minimum(pos + 1, w).astype(F32)
        ps.append(win * inv - cur)
        invs.append(inv)
    return ps, invs


def _pool_fwd(e, pw, scale):
    nb = e.shape[0] // ROW_TILE

    def body(uc_ref, up_ref, z_ref, pw_ref, sc_ref, y_ref):
        i = pl.program_id(0)
        u_cur = uc_ref[...].astype(F32)
        u_prev = jnp.where(i == 0, 0.0, up_ref[...].astype(F32))
        ps, _ = _pool_p(u_cur, u_prev, i)
        z = z_ref[...].astype(F32)
        gate = z * _sigmoid(z)
        for g in range(len(POOL_WINDOWS)):
            sl = slice(g * POOL_GROUP, (g + 1) * POOL_GROUP)
            yraw = jnp.dot(ps[g].astype(BF16), pw_ref[g], preferred_element_type=F32)
            y_ref[:, sl] = ((yraw * sc_ref[:, sl]) * gate[:, sl]).astype(BF16)

    blk = (ROW_TILE, POOL_WIDTH)
    return pl.pallas_call(
        body, name="pool_fwd", grid=(nb,),
        in_specs=[pl.BlockSpec(blk, lambda i: (i, 0)), pl.BlockSpec(blk, lambda i: (jnp.maximum(i - 1, 0), 0)),
                  pl.BlockSpec(blk, lambda i: (i, 1)),
                  pl.BlockSpec((len(POOL_WINDOWS), POOL_GROUP, POOL_GROUP), lambda i: (0, 0, 0)),
                  pl.BlockSpec((1, POOL_WIDTH), lambda i: (0, 0))],
        out_specs=pl.BlockSpec(blk, lambda i: (i, 0)),
        out_shape=jax.ShapeDtypeStruct((e.shape[0], POOL_WIDTH), BF16),
        compiler_params=_params(("parallel",)),
    )(e, e, e, pw, scale)


def _stack_heads(a):
    first = lax.broadcasted_iota(jnp.int32, a.shape, 1) < HEAD_DIM
    zero = jnp.zeros_like(a)
    return jnp.concatenate([jnp.where(first, a, zero), jnp.where(first, zero, a)], axis=0)


def _unstack_heads(a):
    rows = a.shape[0] // 2
    first = lax.broadcasted_iota(jnp.int32, (rows, LANES), 1) < HEAD_DIM
    return jnp.where(first, a[:rows], a[rows:])


def _causal(i, jj, stacked, kv_tile):
    r = lax.broadcasted_iota(jnp.int32, (stacked * ROW_TILE, kv_tile), 0)
    if stacked == 2:
        r = jnp.where(r >= ROW_TILE, r - ROW_TILE, r)
    kidx = jj * kv_tile + lax.broadcasted_iota(jnp.int32, (stacked * ROW_TILE, kv_tile), 1)
    return kidx <= i * ROW_TILE + r


def _stack_rows(b):
    n = b.shape[1]
    return jnp.concatenate([jnp.broadcast_to(b[0:1], (ROW_TILE, n)), jnp.broadcast_to(b[1:2], (ROW_TILE, n))], axis=0)


def _attn_fwd(qkv, bias, nb):
    lk = qkv.shape[0]
    lp = nb * ROW_TILE
    nkb = lk // KV_TILE
    n_pairs = N_HEADS // 2

    def body(q_ref, k_ref, v_ref, b_ref, o_ref, lse_ref):
        i = pl.program_id(1)
        qs = _stack_heads(q_ref[...])
        last = (i * ROW_TILE) // KV_TILE

        def block(jj, carry, masked, n_keys=KV_TILE):
            m, l, acc = carry
            rows = pl.ds(pl.multiple_of(jj * KV_TILE, KV_TILE), n_keys)
            s = _dot_nt(qs, k_ref[rows, :]) - _stack_rows(b_ref[0, 0, :, rows])
            if masked:
                s = jnp.where(_causal(i, jj, 2, KV_TILE)[:, :n_keys], s, NEG)
            m_new = jnp.maximum(m, jnp.max(s, axis=1, keepdims=True))
            alpha = jnp.exp(m - m_new)
            p = jnp.exp(s - m_new)
            l = alpha * l + jnp.sum(p, axis=1, keepdims=True)
            acc = alpha * acc + jnp.dot(p.astype(BF16), v_ref[rows, :], preferred_element_type=F32)
            return m_new, l, acc

        init = (jnp.full((2 * ROW_TILE, 1), NEG, F32), jnp.zeros((2 * ROW_TILE, 1), F32),
                jnp.zeros((2 * ROW_TILE, LANES), F32))
        def finish(carry):
            m, l, acc = carry
            o_ref[...] = _unstack_heads(acc / l)
            lse_ref[...] = _unstack_heads(jnp.broadcast_to(m + jnp.log(l), (2 * ROW_TILE, LANES)))

        carry = lax.fori_loop(0, last, lambda jj, c: block(jj, c, False), init)
        ends = [lambda c, n=(r + 1) * ROW_TILE: finish(block(last, c, True, n)) for r in range(KV_TILE // ROW_TILE)]
        lax.switch(i - last * (KV_TILE // ROW_TILE), ends, carry)

    return pl.pallas_call(
        body, name="attn_fwd", grid=(n_pairs, nb),
        in_specs=[pl.BlockSpec((ROW_TILE, LANES), lambda hp, i: (i, hp)),
                  pl.BlockSpec((lk, LANES), lambda hp, i: (0, n_pairs + hp)),
                  pl.BlockSpec((lk, LANES), lambda hp, i: (0, 2 * n_pairs + hp)),
                  pl.BlockSpec((1, 1, 2, lk), lambda hp, i: (i, hp, 0, 0))],
        out_specs=[pl.BlockSpec((ROW_TILE, LANES), lambda hp, i: (i, hp)),
                   pl.BlockSpec((ROW_TILE, LANES), lambda hp, i: (i, hp))],
        out_shape=[jax.ShapeDtypeStruct((lp, ATTN_WIDTH), F32), jax.ShapeDtypeStruct((lp, ATTN_WIDTH), F32)],
        compiler_params=_params(("parallel", "parallel")),
    )(qkv, qkv, qkv, bias)


def _merge_fwd(y_pool, o, e, wup_p, wup_a):
    lp = o.shape[0]
    nb = lp // ROW_TILE

    def body(yp_ref, o_ref, e_ref, wp_ref, wa_ref, mg_ref, ya_ref):
        za = e_ref[:, E_ZA:E_GP].astype(F32)
        ya = (o_ref[...] * (za * _sigmoid(za))).astype(BF16)
        ya_ref[...] = ya
        a_pool = jnp.dot(yp_ref[...], wp_ref[...], preferred_element_type=F32)
        a_attn = jnp.dot(ya, wa_ref[...], preferred_element_type=F32)
        mg_ref[...] = (_sigmoid(e_ref[:, E_GP:E_GA].astype(F32)) * a_pool
                       + _sigmoid(e_ref[:, E_GA:E_COLS].astype(F32)) * a_attn).astype(BF16)

    return pl.pallas_call(
        body, name="merge_fwd", grid=(nb,),
        in_specs=[pl.BlockSpec((ROW_TILE, POOL_WIDTH), lambda i: (i, 0)),
                  pl.BlockSpec((ROW_TILE, ATTN_WIDTH), lambda i: (i, 0)),
                  pl.BlockSpec((ROW_TILE, E_COLS), lambda i: (i, 0)),
                  pl.BlockSpec((POOL_WIDTH, D_MODEL), lambda i: (0, 0)),
                  pl.BlockSpec((ATTN_WIDTH, D_MODEL), lambda i: (0, 0))],
        out_specs=[pl.BlockSpec((ROW_TILE, D_MODEL), lambda i: (i, 0)),
                   pl.BlockSpec((ROW_TILE, ATTN_WIDTH), lambda i: (i, 0))],
        out_shape=[jax.ShapeDtypeStruct((lp, D_MODEL), BF16), jax.ShapeDtypeStruct((lp, ATTN_WIDTH), BF16)],
        compiler_params=_params(("parallel",)),
    )(y_pool, o, e, wup_p, wup_a)


def _head_fwd_bwd(merged, w_out, x2, metapad, gf, target):
    lp = merged.shape[0]
    nb = lp // ROW_TILE

    def body(mg_ref, w_ref, x_ref, mp_ref, g_ref, t_ref, dh_ref, loss_ref, dg_ref):
        i = pl.program_id(0)

        @pl.when(i == 0)
        def _():
            loss_ref[...] = jnp.zeros_like(loss_ref)
            dg_ref[...] = jnp.zeros_like(dg_ref)

        h0 = jnp.where(i == 0, mp_ref[...], x_ref[...])
        h1 = h0 + jnp.dot(mg_ref[...], w_ref[...], preferred_element_type=F32)
        r = lax.rsqrt(jnp.mean(h1 * h1, axis=-1, keepdims=True) + RMS_EPS)
        xhat = h1 * r
        g = g_ref[...]
        err = jnp.where(i == 0, 0.0, xhat * g - t_ref[...])
        loss_ref[...] += 0.5 * jnp.sum(jnp.mean(err * err, axis=-1, keepdims=True))
        dy = err / D_MODEL
        dg_ref[...] += jnp.sum(dy * xhat, axis=0, keepdims=True)
        dxhat = dy * g
        dh_ref[...] = r * (dxhat - xhat * jnp.mean(dxhat * xhat, axis=-1, keepdims=True))

    return pl.pallas_call(
        body, name="head_fwd_bwd", grid=(nb,),
        in_specs=[pl.BlockSpec((ROW_TILE, D_MODEL), lambda i: (i, 0)),
                  pl.BlockSpec((D_MODEL, D_MODEL), lambda i: (0, 0)),
                  _tokens_spec(), pl.BlockSpec((ROW_TILE, D_MODEL), lambda i: (0, 0)),
                  pl.BlockSpec((1, D_MODEL), lambda i: (0, 0)), _tokens_spec()],
        out_specs=[pl.BlockSpec((ROW_TILE, D_MODEL), lambda i: (i, 0)),
                   pl.BlockSpec((1, LANES), lambda i: (0, 0)), pl.BlockSpec((1, D_MODEL), lambda i: (0, 0))],
        out_shape=[jax.ShapeDtypeStruct((lp, D_MODEL), F32), jax.ShapeDtypeStruct((1, LANES), F32),
                   jax.ShapeDtypeStruct((1, D_MODEL), F32)],
        compiler_params=_params(("arbitrary",)),
    )(merged, w_out, x2, metapad, gf, target)


def _per_head_rowsum(t):
    head = lax.broadcasted_iota(jnp.int32, t.shape, 1) // HEAD_DIM
    out = jnp.zeros_like(t)
    for h in range(N_HEADS):
        sel = head == h
        out = jnp.where(sel, jnp.sum(jnp.where(sel, t, 0.0), axis=1, keepdims=True), out)
    return out


def _merge_bwd(dh1, w_out, y_pool, y_attn, wup_p, wup_a, e, o):
    lp = o.shape[0]
    nb = lp // ROW_TILE

    def body(dh_ref, wo_ref, yp_ref, ya_ref, wp_ref, wa_ref, e_ref, o_ref,
             dap_ref, daa_ref, dg_ref, do_ref, delta_ref, dyp_ref):
        dmerged = _dot_nt(dh_ref[...].astype(BF16), wo_ref[...])
        a_pool = jnp.dot(yp_ref[...], wp_ref[...], preferred_element_type=F32)
        a_attn = jnp.dot(ya_ref[...], wa_ref[...], preferred_element_type=F32)
        sp = _sigmoid(e_ref[:, E_GP:E_GA].astype(F32))
        sa = _sigmoid(e_ref[:, E_GA:E_COLS].astype(F32))
        dap = (dmerged * sp).astype(BF16)
        daa = (dmerged * sa).astype(BF16)
        dap_ref[...] = dap
        daa_ref[...] = daa
        dg_ref[:, ATTN_WIDTH:ATTN_WIDTH + D_MODEL] = (dmerged * a_pool * (sp * (1.0 - sp))).astype(BF16)
        dg_ref[:, ATTN_WIDTH + D_MODEL:] = (dmerged * a_attn * (sa * (1.0 - sa))).astype(BF16)
        dyp_ref[...] = _dot_nt(dap, wp_ref[...])
        dya = _dot_nt(daa, wa_ref[...])
        za = e_ref[:, E_ZA:E_GP].astype(F32)
        sz = _sigmoid(za)
        o = o_ref[...]
        do = dya * (za * sz)
        do_ref[...] = do.astype(BF16)
        dg_ref[:, :ATTN_WIDTH] = (dya * o * (sz * (1.0 + za * (1.0 - sz)))).astype(BF16)
        delta_ref[...] = _per_head_rowsum(do * o)

    row = lambda w: pl.BlockSpec((ROW_TILE, w), lambda i: (i, 0))
    full = lambda a: pl.BlockSpec(a.shape, lambda i: (0, 0))
    return pl.pallas_call(
        body, name="merge_bwd", grid=(nb,),
        in_specs=[row(D_MODEL), full(w_out), row(POOL_WIDTH), row(ATTN_WIDTH), full(wup_p), full(wup_a),
                  row(E_COLS), row(ATTN_WIDTH)],
        out_specs=[row(D_MODEL), row(D_MODEL), pl.BlockSpec((ROW_TILE, MAIN_COLS // 2), lambda i: (i, 1)),
                   row(ATTN_WIDTH), row(ATTN_WIDTH), row(POOL_WIDTH)],
        out_shape=[jax.ShapeDtypeStruct((lp, D_MODEL), BF16), jax.ShapeDtypeStruct((lp, D_MODEL), BF16),
                   jax.ShapeDtypeStruct((lp, MAIN_COLS), BF16),
                   jax.ShapeDtypeStruct((lp, ATTN_WIDTH), BF16), jax.ShapeDtypeStruct((lp, ATTN_WIDTH), F32),
                   jax.ShapeDtypeStruct((lp, POOL_WIDTH), F32)],
        compiler_params=_params(("parallel",)),
    )(dh1, w_out, y_pool, y_attn, wup_p, wup_a, e, o)


def _fill_dproj(dproj_half, du, dzp, dq, dk, dv):
    lp = dproj_half.shape[0]
    nb = lp // ROW_TILE

    def body(base_ref, du_ref, dzp_ref, dq_ref, dk_ref, dv_ref, o_ref):
        parts = [du_ref[...], dzp_ref[...], (dq_ref[...] * HEAD_DIM ** -0.5).astype(BF16), dk_ref[...], dv_ref[...]]
        for t, p in enumerate(parts):
            o_ref[:, t * ATTN_WIDTH:(t + 1) * ATTN_WIDTH] = p

    blk = pl.BlockSpec((ROW_TILE, ATTN_WIDTH), lambda i: (i, 0))
    return pl.pallas_call(
        body, name="fill_dproj", grid=(nb,),
        in_specs=[pl.BlockSpec(memory_space=pl.ANY)] + [blk] * 5,
        out_specs=pl.BlockSpec((ROW_TILE, MAIN_COLS // 2), lambda i: (i, 0)),
        out_shape=jax.ShapeDtypeStruct(dproj_half.shape, BF16),
        input_output_aliases={0: 0},
        compiler_params=_params(("parallel",)),
    )(dproj_half, du, dzp, dq, dk, dv)


def _pool_bwd_local(e, dy_pool, pw, scale):
    lp = e.shape[0]
    nb = lp // ROW_TILE
    ng = len(POOL_WINDOWS)

    def body(uc_ref, up_ref, z_ref, dy_ref, pw_ref, sc_ref, dpc_ref, dz_ref, dsc_ref, dpw_ref):
        i = pl.program_id(0)

        @pl.when(i == 0)
        def _():
            dsc_ref[...] = jnp.zeros_like(dsc_ref)
            dpw_ref[...] = jnp.zeros_like(dpw_ref)

        u_cur = uc_ref[...].astype(F32)
        u_prev = jnp.where(i == 0, 0.0, up_ref[...].astype(F32))
        ps, invs = _pool_p(u_cur, u_prev, i)
        z = z_ref[...].astype(F32)
        sz = _sigmoid(z)
        dy = dy_ref[...]
        dypre = dy * (z * sz)
        dsilu = sz * (1.0 + z * (1.0 - sz))
        for g in range(ng):
            sl = slice(g * POOL_GROUP, (g + 1) * POOL_GROUP)
            pb = ps[g].astype(BF16)
            w = pw_ref[g]
            yraw = jnp.dot(pb, w, preferred_element_type=F32)
            sc = sc_ref[:, sl]
            dz_ref[:, sl] = (dy[:, sl] * (yraw * sc) * dsilu[:, sl]).astype(BF16)
            dsc_ref[:, sl] += jnp.sum(dypre[:, sl] * yraw, axis=0, keepdims=True)
            dyraw = (dypre[:, sl] * sc).astype(BF16)
            dpw_ref[g] += _dot_tn(pb, dyraw)
            dpc_ref[:, sl] = _dot_nt(dyraw, w) * invs[g]

    blk = (ROW_TILE, POOL_WIDTH)
    return pl.pallas_call(
        body, name="pool_bwd_local", grid=(nb,),
        in_specs=[pl.BlockSpec(blk, lambda i: (i, 0)), pl.BlockSpec(blk, lambda i: (jnp.maximum(i - 1, 0), 0)),
                  pl.BlockSpec(blk, lambda i: (i, 1)), pl.BlockSpec(blk, lambda i: (i, 0)),
                  pl.BlockSpec((ng, POOL_GROUP, POOL_GROUP), lambda i: (0, 0, 0)),
                  pl.BlockSpec((1, POOL_WIDTH), lambda i: (0, 0))],
        out_specs=[pl.BlockSpec(blk, lambda i: (i, 0)), pl.BlockSpec(blk, lambda i: (i, 0)),
                   pl.BlockSpec((1, POOL_WIDTH), lambda i: (0, 0)),
                   pl.BlockSpec((ng, POOL_GROUP, POOL_GROUP), lambda i: (0, 0, 0))],
        out_shape=[jax.ShapeDtypeStruct((lp, POOL_WIDTH), F32), jax.ShapeDtypeStruct((lp, POOL_WIDTH), BF16),
                   jax.ShapeDtypeStruct((1, POOL_WIDTH), F32),
                   jax.ShapeDtypeStruct((ng, POOL_GROUP, POOL_GROUP), F32)],
        compiler_params=_params(("arbitrary",)),
    )(e, e, e, dy_pool, pw, scale)


def _pool_bwd_window(dpc):
    lp = dpc.shape[0]
    nb = lp // ROW_TILE

    def body(cur_ref, nxt_ref, du_ref):
        i = pl.program_id(0)
        cur = cur_ref[...]
        nxt = jnp.where(i == nb - 1, 0.0, nxt_ref[...])
        pos = _pool_counts(i)
        for g, w in enumerate(POOL_WINDOWS):
            sl = slice(g * POOL_GROUP, (g + 1) * POOL_GROUP)
            xc = jnp.concatenate([cur[:, sl], nxt[:, sl]], axis=0)
            win = _leading_sums(xc, g + 1)[:ROW_TILE, :]
            dp = cur[:, sl] * jnp.minimum(pos + 1, w).astype(F32)
            du_ref[:, sl] = (win - dp).astype(BF16)

    blk = (ROW_TILE, POOL_WIDTH)
    return pl.pallas_call(
        body, name="pool_bwd_window", grid=(nb,),
        in_specs=[pl.BlockSpec(blk, lambda i: (i, 0)), pl.BlockSpec(blk, lambda i: (jnp.minimum(i + 1, nb - 1), 0))],
        out_specs=pl.BlockSpec(blk, lambda i: (i, 0)),
        out_shape=jax.ShapeDtypeStruct((lp, POOL_WIDTH), BF16),
        compiler_params=_params(("parallel",)),
    )(dpc, dpc)


def _attn_bwd(qkv, do, lse, delta, bias, nb):
    lk = qkv.shape[0]
    lp = nb * ROW_TILE
    nkb = lk // KV_TILE_BWD
    n_pairs = N_HEADS // 2
    per_kv = KV_TILE_BWD // ROW_TILE

    def body(q_ref, k_ref, v_ref, do_ref, lse_ref, dl_ref, b_ref, dq_ref, dk_ref, dv_ref, dc_ref, dcq_ref,
             dk_acc, dv_acc, dc_acc):
        jj = pl.program_id(1)

        @pl.when(jj == 0)
        def _():
            dq_ref[...] = jnp.zeros_like(dq_ref)
            dcq_ref[...] = jnp.zeros_like(dcq_ref)

        dk_acc[...] = jnp.zeros_like(dk_acc)
        dv_acc[...] = jnp.zeros_like(dv_acc)
        dc_acc[...] = jnp.zeros_like(dc_acc)

        def block(i, n_keys, masked):
            kb, vb = k_ref[:n_keys, :], v_ref[:n_keys, :]
            rows = pl.ds(pl.multiple_of(i * ROW_TILE, ROW_TILE), ROW_TILE)
            qs = _stack_heads(q_ref[rows, :])
            dos = _stack_heads(do_ref[rows, :])
            lse_i, dl_i = lse_ref[rows, :], dl_ref[rows, :]
            s = _dot_nt(qs, kb)
            dp = _dot_nt(dos, vb)
            if masked:
                valid = _causal(i, jj, 1, KV_TILE_BWD)[:, :n_keys]
            ps, dss, dcs, rowsums = [], [], [], []
            for hd in range(2):
                half = slice(hd * ROW_TILE, (hd + 1) * ROW_TILE)
                col = slice(hd * HEAD_DIM, hd * HEAD_DIM + 1)
                sh = s[half] - b_ref[i, 0, hd:hd + 1, :n_keys]
                if masked:
                    sh = jnp.where(valid, sh, NEG)
                p = jnp.exp(sh - lse_i[:, col])
                ds = p * (dp[half] - dl_i[:, col])
                ps.append(p.astype(BF16))
                dss.append(ds.astype(BF16))
                dcs.append(jnp.sum(ds, axis=0, keepdims=True))
                rowsums.append(jnp.sum(ds, axis=1, keepdims=True))
            dsb = jnp.concatenate(dss, axis=0)
            dv_acc[:n_keys, :] += _dot_tn(jnp.concatenate(ps, axis=0), dos)
            dk_acc[:n_keys, :] += _dot_tn(dsb, qs)
            dc_acc[:, :n_keys] -= jnp.concatenate(dcs, axis=0)
            dq_ref[rows, :] += _unstack_heads(jnp.dot(dsb, kb, preferred_element_type=F32))
            dcq_ref[rows, :] += _unstack_heads(jnp.broadcast_to(jnp.concatenate(rowsums, axis=0), (2 * ROW_TILE, LANES)))

        first_q = per_kv * jj
        for r in range(per_kv):
            @pl.when(first_q + r < nb)
            def _():
                block(first_q + r, (r + 1) * ROW_TILE, True)

        def rest(i, carry):
            block(i, KV_TILE_BWD, False)
            return carry

        lax.fori_loop(jnp.minimum(first_q + per_kv, nb), nb, rest, 0)
        dk_ref[...] = dk_acc[...].astype(BF16)
        dv_ref[...] = dv_acc[...].astype(BF16)
        dc_ref[0, 0] = dc_acc[...]

    whole = lambda rows: pl.BlockSpec((rows, LANES), lambda hp, jj: (0, hp))
    kv_blk = lambda off: pl.BlockSpec((KV_TILE_BWD, LANES), lambda hp, jj: (jj, off + hp))
    return pl.pallas_call(
        body, name="attn_bwd", grid=(n_pairs, nkb),
        in_specs=[whole(lk), kv_blk(n_pairs), kv_blk(2 * n_pairs), whole(lp), whole(lp), whole(lp),
                  pl.BlockSpec((nb, 1, 2, KV_TILE_BWD), lambda hp, jj: (0, hp, 0, jj))],
        out_specs=[whole(lp), kv_blk(0), kv_blk(0),
                   pl.BlockSpec((1, 1, 2, KV_TILE_BWD), lambda hp, jj: (hp, jj, 0, 0)), whole(lp)],
        out_shape=[jax.ShapeDtypeStruct((lp, ATTN_WIDTH), F32), jax.ShapeDtypeStruct((lk, ATTN_WIDTH), BF16),
                   jax.ShapeDtypeStruct((lk, ATTN_WIDTH), BF16),
                   jax.ShapeDtypeStruct((n_pairs, nkb, 2, KV_TILE_BWD), F32),
                   jax.ShapeDtypeStruct((lp, ATTN_WIDTH), F32)],
        scratch_shapes=[pltpu.VMEM((KV_TILE_BWD, LANES), F32), pltpu.VMEM((KV_TILE_BWD, LANES), F32),
                        pltpu.VMEM((2, KV_TILE_BWD), F32)],
        compiler_params=_params(("parallel", "arbitrary")),
    )(qkv, qkv, qkv, do, lse, delta, bias)


def _gates_bwd(dc, dcq, f, bfg):
    nb = f.shape[0] // ROW_TILE

    def body(dc_ref, dcq_ref, f_ref, b_ref, df_ref, db_ref, carry):
        step = pl.program_id(0)
        i = nb - 1 - step

        @pl.when(step == 0)
        def _():
            carry[...] = jnp.zeros_like(carry)
            db_ref[...] = jnp.zeros_like(db_ref)

        dcb = dc_ref[...]
        lane = lax.broadcasted_iota(jnp.int32, (ROW_TILE, F_COLS), 1)
        for h in range(N_HEADS):
            dcb = dcb + jnp.where(lane == h, dcq_ref[:, HEAD_DIM * h:HEAD_DIM * h + 1], 0.0)
        r_i = lax.broadcasted_iota(jnp.int32, (ROW_TILE, ROW_TILE), 0)
        c_i = lax.broadcasted_iota(jnp.int32, (ROW_TILE, ROW_TILE), 1)
        upper = (c_i >= r_i).astype(F32)
        dlf = jnp.dot(upper, dcb, precision=HIGHEST, preferred_element_type=F32) + carry[...]
        carry[...] = carry[...] + jnp.sum(dcb, axis=0, keepdims=True)
        logit = f_ref[...] + b_ref[...]
        dlogit = jnp.where(_valid_gate_mask(i), dlf * _sigmoid(-logit), 0.0)
        df_ref[...] = dlogit.astype(BF16)
        db_ref[...] += jnp.sum(dlogit, axis=0, keepdims=True)

    blk = pl.BlockSpec((ROW_TILE, F_COLS), lambda s: (nb - 1 - s, 0))
    wide = pl.BlockSpec((ROW_TILE, ATTN_WIDTH), lambda s: (nb - 1 - s, 0))
    one = pl.BlockSpec((1, F_COLS), lambda s: (0, 0))
    return pl.pallas_call(
        body, name="gates_bwd", grid=(nb,),
        in_specs=[blk, wide, blk, one], out_specs=[blk, one],
        out_shape=[jax.ShapeDtypeStruct(f.shape, BF16), jax.ShapeDtypeStruct((1, F_COLS), F32)],
        scratch_shapes=[pltpu.VMEM((1, F_COLS), F32)],
        compiler_params=_params(("arbitrary",)),
    )(dc, dcq, f, bfg)


def _input_bwd(dproj, df, wt_e, wt_qkv, wt_f, x2, metapad, dh1, g1, hs, sc=None):
    nb = x2.shape[0] // ROW_TILE + 1
    n = len(hs)
    sc_in, sc_shapes, sc_sems = _scatter_operands(hs, sc)
    nq = len(sc_in)

    def body(dp_ref, df_ref, we_ref, w_ref, wf_ref, x_ref, mp_ref, dh_ref, g_ref, *rest):
        h_refs, s_ref = rest[:n], (rest[n] if sc is not None else None)
        gx_ref, g0_ref, dg_ref = rest[nq:nq + 3]
        q_refs, sq_ref = rest[nq + 3:nq + 3 + n], (rest[nq + 3 + n] if sc is not None else None)
        sems = rest[2 * nq + 3:]
        i = pl.program_id(0)

        @pl.when(i == 0)
        def _():
            dg_ref[...] = jnp.zeros_like(dg_ref)
            own, sends, _ = _chip_scatter_plan(h_refs, s_ref, q_refs, sq_ref, *sems)
            for cp in own + sends:
                cp.start()

        dhn = (jnp.dot(dp_ref[:, :E_ZA], we_ref[:E_ZA, :], preferred_element_type=F32)
               + jnp.dot(dp_ref[:, E_ZA:MAIN_COLS // 2], w_ref[...], preferred_element_type=F32)
               + jnp.dot(dp_ref[:, MAIN_COLS // 2:], we_ref[E_ZA:, :], preferred_element_type=F32)
               + jnp.dot(df_ref[...], wf_ref[...], preferred_element_type=F32))
        h0 = jnp.where(i == 0, mp_ref[...], x_ref[...])
        r = lax.rsqrt(jnp.mean(h0 * h0, axis=-1, keepdims=True) + RMS_EPS)
        xhat = h0 * r
        dg_ref[...] += jnp.sum(dhn * xhat, axis=0, keepdims=True)
        dxhat = dhn * g_ref[...]
        dh0 = dh_ref[...] + r * (dxhat - xhat * jnp.mean(dxhat * xhat, axis=-1, keepdims=True))
        gx_ref[...] = dh0

        @pl.when(i == 0)
        def _():
            g0_ref[...] = dh0

        @pl.when(i == nb - 1)
        def _():
            own, sends, recvs = _chip_scatter_plan(h_refs, s_ref, q_refs, sq_ref, *sems)
            for cp in recvs:
                cp.wait_recv()
            for cp in sends:
                cp.wait_send()
            for cp in own:
                cp.wait()

    const = lambda shape: pl.BlockSpec(shape, lambda i: (0, 0))
    res = pl.pallas_call(
        body, name="input_bwd", grid=(nb,),
        in_specs=[pl.BlockSpec((ROW_TILE, MAIN_COLS), lambda i: (i, 0)), pl.BlockSpec((ROW_TILE, F_COLS), lambda i: (i, 0)),
                  const((E_COLS, D_MODEL)), const((QKV_COLS, D_MODEL)), const((F_COLS, D_MODEL)),
                  _tokens_spec(), const((ROW_TILE, D_MODEL)),
                  pl.BlockSpec((ROW_TILE, D_MODEL), lambda i: (i, 0)), const((1, D_MODEL))] + [HBM] * nq,
        out_specs=[_tokens_spec(), const((ROW_TILE, D_MODEL)), const((1, D_MODEL))] + [HBM] * nq,
        out_shape=[jax.ShapeDtypeStruct(x2.shape, F32), jax.ShapeDtypeStruct((ROW_TILE, D_MODEL), F32),
                   jax.ShapeDtypeStruct((1, D_MODEL), F32)] + sc_shapes,
        scratch_shapes=sc_sems,
        compiler_params=_params(("arbitrary",), vmem=56 * 1024 * 1024),
    )(dproj, df, wt_e, wt_qkv, wt_f, x2, metapad, dh1, g1, *_in_hbm(*sc_in))
    return res[:3], res[3:3 + n], (res[3 + n] if sc is not None else None)


def _adamw(w, g, m, v, name):
    rows, cols = w.shape
    if rows % 8 == 0:
        tr, tc = _row_tile8(rows), cols
    else:
        tr, tc = rows, (2 * LANES if cols % (2 * LANES) == 0 and rows > 8 else cols)

    def body(w_ref, g_ref, m_ref, v_ref, d_ref, mo_ref, vo_ref):
        g_ = g_ref[...]
        m_new = ADAM_B1 * m_ref[...] + (1.0 - ADAM_B1) * g_
        v_new = ADAM_B2 * v_ref[...] + (1.0 - ADAM_B2) * (g_ * g_)
        m_hat = m_new / (1.0 - ADAM_B1 ** ADAM_STEP)
        v_hat = v_new / (1.0 - ADAM_B2 ** ADAM_STEP)
        d_ref[...] = -ADAM_LR * (m_hat / (jnp.sqrt(v_hat) + ADAM_EPS) + ADAM_WD * w_ref[...])
        mo_ref[...] = m_new
        vo_ref[...] = v_new

    blk = pl.BlockSpec((tr, tc), lambda i, j: (i, j))
    return pl.pallas_call(
        body, name=name, grid=(rows // tr, cols // tc),
        in_specs=[blk] * 4, out_specs=[blk] * 3,
        out_shape=[jax.ShapeDtypeStruct(w.shape, F32)] * 3,
        compiler_params=_params(("parallel", "parallel")),
    )(w, g, m, v)


def _adamw_native(w3, g3, m3, v3, name):
    rows = w3.shape[0]
    tr = rows // 2
    blk = pl.BlockSpec((tr,) + w3.shape[1:], lambda i: (i, 0, 0))
    shape = jax.ShapeDtypeStruct(w3.shape, F32)

    def moments(g_ref, m_ref, v_ref, mo_ref, vo_ref):
        g_ = g_ref[...]
        mo_ref[...] = ADAM_B1 * m_ref[...] + (1.0 - ADAM_B1) * g_
        vo_ref[...] = ADAM_B2 * v_ref[...] + (1.0 - ADAM_B2) * (g_ * g_)

    new_m, new_v = pl.pallas_call(
        moments, name=name + "_moments", grid=(2,), in_specs=[blk] * 3, out_specs=[blk] * 2, out_shape=[shape] * 2,
        compiler_params=_params(("parallel",)),
    )(g3, m3, v3)

    def delta(w_ref, m_ref, v_ref, d_ref):
        m_hat = m_ref[...] / (1.0 - ADAM_B1 ** ADAM_STEP)
        v_hat = v_ref[...] / (1.0 - ADAM_B2 ** ADAM_STEP)
        d_ref[...] = -ADAM_LR * (m_hat / (jnp.sqrt(v_hat) + ADAM_EPS) + ADAM_WD * w_ref[...])

    d = pl.pallas_call(
        delta, name=name + "_delta", grid=(2,), in_specs=[blk] * 3, out_specs=blk, out_shape=shape,
        compiler_params=_params(("parallel",)),
    )(w3, new_m, new_v)
    return d, new_m, new_v


def _row_tile8(rows):
    best = rows
    for t in range(8, 257, 8):
        if rows % t == 0:
            best = t
    return best


def kernel(x, meta_tokens, norm_g, w_in, b_forget, pool_w, pool_scale, w_up_pool, w_up_attn, w_out, final_norm_g, loss_target, m_meta_tokens, m_norm_g, m_w_in, m_b_forget, m_pool_w, m_pool_scale, m_w_up_pool, m_w_up_attn, m_w_out, m_final_norm_g, v_meta_tokens, v_norm_g, v_w_in, v_b_forget, v_pool_w, v_pool_scale, v_w_up_pool, v_w_up_attn, v_w_out, v_final_norm_g):
    seq = x.shape[1]
    assert seq % ROW_TILE == 0 and x.shape[0] == 1
    lp = seq + ROW_TILE
    nb = lp // ROW_TILE
    lk = -(-lp // KV_TILE_BWD) * KV_TILE_BWD
    core = jnp.reshape(lax.axis_index("c"), (1,)).astype(jnp.int32)
    x2 = x[0]
    target = loss_target[0]
    sh_d = D_MODEL // N_CHIPS

    to_rows = lambda a: jnp.transpose(a, (2, 0, 1))
    from_rows = lambda a: jnp.transpose(a, (1, 2, 0))
    gf = final_norm_g.reshape(1, D_MODEL)
    bfg = jnp.pad(b_forget, ((0, 0), (0, F_COLS - N_HEADS)))
    pw_b = pool_w[0].astype(BF16)
    hn, (wg_in, meta_g) = _rmsnorm_fwd(x2, norm_g, lk, [jnp.transpose(w_in[0]).astype(BF16), meta_tokens], [1, 0])
    wt_e, wt_qkv, wt_f = _weight_sections(wg_in)
    meta_full = jnp.transpose(meta_g, (1, 0, 2)).reshape(N_META, D_MODEL)
    metapad = jnp.pad(meta_full, ((PAD_ROWS, 0), (0, 0)))

    tm = _row_tile(lp, 2200)
    e, (wg_up_p, wg_up_a, wg_out) = _mm_nt(
        hn, wt_e, BF16, "in_proj_gates", lp, E_COLS, tm, 512,
        gather=([w_up_pool[0].astype(BF16), w_up_attn[0].astype(BF16), w_out[0].astype(BF16)], [0, 0, 0]))
    wup_p = jnp.transpose(wg_up_p, (1, 0, 2)).reshape(POOL_WIDTH, D_MODEL)
    wup_a = jnp.transpose(wg_up_a, (1, 0, 2)).reshape(ATTN_WIDTH, D_MODEL)
    wout = wg_out.reshape(D_MODEL, D_MODEL)
    qkv = _mm_nt(hn, wt_qkv, BF16, "in_proj_qkv", lk, QKV_COLS, _row_tile(lk, 2600), 512, scale_first=HEAD_DIM ** -0.5)
    f = _mm_nt(hn, wt_f, F32, "in_proj_forget", lp, F_COLS, tm, F_COLS)
    c = _gates_fwd(f, bfg)
    c_t = jnp.transpose(c[:, :N_HEADS])
    c_first = jnp.transpose(c_t[:, ::ROW_TILE]).reshape(nb, N_HEADS // 2, 2, 1)
    c_keys = jnp.pad(c_t, ((0, 0), (0, lk - lp))).reshape(1, N_HEADS // 2, 2, lk)
    bias = jnp.where(jnp.arange(lk) < PAD_ROWS, -NEG, c_keys - c_first)
    y_pool = _pool_fwd(e, pw_b, pool_scale)
    o, lse = _attn_fwd(qkv, bias, nb)
    merged, y_attn = _merge_fwd(y_pool, o, e, wup_p, wup_a)
    dh1, loss_part, dgf = _head_fwd_bwd(merged, wout, x2, metapad, gf, target)

    dap, daa, dproj_half, do, delta, dy_pool = _merge_bwd(dh1, wout, y_pool, y_attn, wup_p, wup_a, e, o)
    dpc, dzp, dscale, dpw = _pool_bwd_local(e, dy_pool, pw_b, pool_scale)
    du = _pool_bwd_window(dpc)
    dq, dk, dv, dc4, dcq = _attn_bwd(qkv, do, lse, delta, bias, nb)
    dc = jnp.transpose(dc4, (1, 3, 0, 2)).reshape(lk, N_HEADS)[:lp]
    df, db = _gates_bwd(jnp.pad(dc, ((0, 0), (0, F_COLS - N_HEADS))), dcq, f, bfg)
    dproj = _fill_dproj(dproj_half, du, dzp, dq, dk, dv)
    tk = _row_tile(lp, 1100)
    dw_out = _mm_tn(merged, dh1, "grad_w_out", lp, 512, D_MODEL, tk)
    dw_up_p = _mm_tn(y_pool, dap, "grad_w_up_pool", lp, 512, D_MODEL, lp, chunks=N_CHIPS)
    dw_up_a = _mm_tn(y_attn, daa, "grad_w_up_attn", lp, 512, D_MODEL, lp, chunks=N_CHIPS)
    dwt_f = _mm_tn(df, hn, "grad_w_in_forget", lp, F_COLS, D_MODEL, tk)

    def pad8(a):
        return jnp.pad(a, ((0, (-a.shape[0]) % 8), (0, 0)))

    small_parts = [pad8(a) for a in (dgf.reshape(-1, LANES), dscale.reshape(-1, LANES), db, dpw.reshape(-1, LANES),
                                     loss_part)]
    small = jnp.concatenate(small_parts, axis=0)
    soffs = [0]
    for p in small_parts:
        soffs.append(soffs[-1] + p.shape[0])

    gs_rows = [dw_up_p, dw_up_a, dw_out.reshape(N_CHIPS, sh_d, D_MODEL)]
    half, tm_w = MAIN_COLS // 2, MAIN_COLS // 8
    dwt_a, (*rb_rows, rb_f, sr) = _mm_tn(dproj, hn, "grad_w_in_a", lp, tm_w, D_MODEL, lp, m=half, m_off=0,
                                         swap=(gs_rows + [dwt_f, small], [1, 1, 1, 1, None]))
    *hs_rows, sc = _add_sibling_half(gs_rows, rb_rows, [1, 1, 1], small, sr, core)
    dwt_b, (rb_a,) = _mm_tn(dproj, hn, "grad_w_in_b", lp, tm_w, D_MODEL, lp, m=half // 2, m_off=4, swap=([dwt_a], [1]))
    dwt_c, (rb_b,), (qs_rows, sq) = _mm_tn(dproj, hn, "grad_w_in_c", lp, tm_w, D_MODEL, lp, m=half // 2, m_off=6,
                                           swap=([dwt_b], [1]), scatter=(hs_rows, sc))
    (rb_c,) = _sibling_exchange([dwt_c], [1])
    order = [(0, 0, half), (1, 0, ATTN_WIDTH), (3, 0, N_HEADS), (1, ATTN_WIDTH, half // 2), (2, 0, half // 2)]
    h_in = _add_halves_w_in([dwt_a, dwt_b, dwt_c, dwt_f], [rb_a, rb_b, rb_c, rb_f], order, core)
    grad_axes = [2, 1, 1, 1]
    (grad_x, g_block0, dg1), (q_in,), _ = _input_bwd(dproj, df, wt_e, wt_qkv, wt_f, x2, metapad, dh1, norm_g, [h_in])
    g_w_in_rows, g_w_up_p, g_w_up_a, g_w_out, st = _reduce_allgather([q_in] + list(qs_rows), grad_axes, sq,
                                                                       [True, False, False, False])
    late = _small_allreduce(jnp.concatenate([pad8(dg1.reshape(-1, LANES)), g_block0[PAD_ROWS:].reshape(-1, LANES)], axis=0))
    n_norm = D_MODEL // LANES
    g_norm = late[:n_norm].reshape(1, D_MODEL)
    chip = 2 * lax.axis_index("x") + lax.axis_index("y")
    g_meta = lax.dynamic_slice_in_dim(late[n_norm:].reshape(N_META, D_MODEL), chip * sh_d, sh_d, axis=1)

    spiece = lambda k, rows: st[soffs[k]:soffs[k] + rows]
    g_final = spiece(0, D_MODEL // LANES).reshape(1, D_MODEL)
    g_scale = spiece(1, POOL_WIDTH // LANES).reshape(1, POOL_WIDTH)
    g_bf = spiece(2, 1)
    g_pw = spiece(3, POOL_WIDTH)
    loss = st[soffs[4], 0]

    def pad_lanes(a):
        return jnp.pad(a, ((0, 0), (0, F_COLS - N_HEADS)))

    w_in_res = (g_w_in_rows,) + _adamw_native(to_rows(w_in), g_w_in_rows, to_rows(m_w_in), to_rows(v_w_in), "adamw_w_in")

    upd = [
        ("meta_tokens", meta_tokens, g_meta, m_meta_tokens, v_meta_tokens),
        ("norm_g", norm_g, g_norm, m_norm_g, v_norm_g),
        ("w_in", None, None, None, None),
        ("b_forget", pad_lanes(b_forget), g_bf, pad_lanes(m_b_forget), pad_lanes(v_b_forget)),
        ("pool_w", pool_w.reshape(-1, LANES), g_pw, m_pool_w.reshape(-1, LANES), v_pool_w.reshape(-1, LANES)),
        ("pool_scale", pool_scale, g_scale, m_pool_scale, v_pool_scale),
        ("w_up_pool", w_up_pool[0], g_w_up_p, m_w_up_pool[0], v_w_up_pool[0]),
        ("w_up_attn", w_up_attn[0], g_w_up_a, m_w_up_attn[0], v_w_up_attn[0]),
        ("w_out", w_out[0], g_w_out, m_w_out[0], v_w_out[0]),
        ("final_norm_g", gf, g_final, m_final_norm_g.reshape(1, D_MODEL), v_final_norm_g.reshape(1, D_MODEL)),
    ]
    shapes = [meta_tokens.shape, norm_g.shape, w_in.shape, b_forget.shape, pool_w.shape, pool_scale.shape,
              w_up_pool.shape, w_up_attn.shape, w_out.shape, final_norm_g.shape]
    grads, deltas, new_ms, new_vs = [], [], [], []
    for (name, w_, g_, m_in, v_in), shp in zip(upd, shapes):
        if name == "w_in":
            res = tuple(from_rows(a) for a in w_in_res)
        else:
            d_, mn_, vn_ = _adamw(w_, g_, m_in, v_in, "adamw_" + name)
            res = (g_, d_, mn_, vn_)
        if name == "b_forget":
            res = tuple(a[:, :N_HEADS] for a in res)
        for lst, a in zip((grads, deltas, new_ms, new_vs), res):
            lst.append(a.reshape(shp))

    return (loss, grad_x.reshape(x.shape), *grads, *deltas, *new_ms, *new_vs)
```

```python
import jax
import jax.numpy as jnp
from jax import lax
from jax.experimental import pallas as pl
from jax.experimental.pallas import tpu as pltpu

F32 = jnp.float32
BF16 = jnp.bfloat16
MESH = pl.DeviceIdType.MESH
HIGHEST = lax.Precision.HIGHEST
HBM = pl.BlockSpec(memory_space=pltpu.HBM)

D_MODEL = 1024
N_META = 16
POOL_WIDTH = 512
POOL_GROUP = 128
POOL_WINDOWS = (2, 4, 8, 16)
N_HEADS = 8
HEAD_DIM = 64
ATTN_WIDTH = 512
RMS_EPS = 1e-6
N_CHIPS = 4

ADAM_LR = 0.001
ADAM_B1 = 0.9
ADAM_B2 = 0.999
ADAM_EPS = 1e-08
ADAM_WD = 0.01
ADAM_STEP = 10

LANES = 128
ROW_TILE = 256
KV_TILE = 1024
KV_TILE_BWD = 1024
PAD_ROWS = ROW_TILE - N_META
NEG = -1e30

E_U, E_ZP, E_ZA, E_GP, E_GA, E_COLS = 0, 512, 1024, 1536, 2560, 3584
QKV_COLS = 3 * ATTN_WIDTH
MAIN_COLS = E_COLS + QKV_COLS
F_COLS = LANES
REF_U, REF_Q, REF_ZA, REF_F, REF_GP = 0, 1024, 2560, 3072, 3080
VMEM_LIMIT = 48 * 1024 * 1024


def _params(sem=None, vmem=VMEM_LIMIT):
    return pltpu.CompilerParams(dimension_semantics=sem, vmem_limit_bytes=vmem)


def _in_hbm(*arrays):
    return [pltpu.with_memory_space_constraint(a, pltpu.HBM) for a in arrays]


def _row_tile(n, target):
    best = 16
    for t in range(16, target + 1, 16):
        if n % t == 0:
            best = t
    return best


def _sigmoid(x):
    return 1.0 / (1.0 + jnp.exp(-x))


def _half(ref, axis, which):
    n = ref.shape[axis] // 2
    idx = [slice(None)] * len(ref.shape)
    idx[axis] = pl.ds(pl.multiple_of(which * n, n), n)
    return ref.at[tuple(idx)]


def _half_shape(shape, axis):
    s = list(shape)
    s[axis] //= 2
    return tuple(s)


def _gather_start(ins, outs, axes, send, recv, fsend, frecv, local):
    x, y, c = lax.axis_index("x"), lax.axis_index("y"), lax.axis_index("c")
    me = 2 * x + y
    for t in range(len(ins)):
        pltpu.make_async_copy(ins[t], outs[t].at[me], local.at[t]).start()
        for k, chip in enumerate([(1 - x, y), (x, 1 - y), (1 - x, 1 - y)]):
            pltpu.make_async_remote_copy(
                src_ref=_half(ins[t], axes[t], c), dst_ref=_half(outs[t].at[me], axes[t], c),
                send_sem=send.at[3 * t + k], recv_sem=recv.at[3 * t + k], device_id=(*chip, c), device_id_type=MESH).start()


def _gather_finish(ins, outs, axes, send, recv, fsend, frecv, local):
    x, y, c = lax.axis_index("x"), lax.axis_index("y"), lax.axis_index("c")
    me = 2 * x + y
    sibling = (x, y, 1 - c)
    chips = [(1 - x, y), (x, 1 - y), (1 - x, 1 - y)]
    n = len(ins)

    def over_ici(t, k, chip, dst_slot):
        return pltpu.make_async_remote_copy(
            src_ref=_half(ins[t], axes[t], c), dst_ref=_half(outs[t].at[dst_slot], axes[t], c),
            send_sem=send.at[3 * t + k], recv_sem=recv.at[3 * t + k], device_id=(*chip, c), device_id_type=MESH)

    def to_sibling(t, k, slot, which):
        return pltpu.make_async_remote_copy(
            src_ref=_half(outs[t].at[slot], axes[t], which), dst_ref=_half(outs[t].at[slot], axes[t], which),
            send_sem=fsend.at[3 * t + k], recv_sem=frecv.at[3 * t + k], device_id=sibling, device_id_type=MESH)

    forwards = []
    for t in range(n):
        for k, (px, py) in enumerate(chips):
            over_ici(t, k, (px, py), 2 * px + py).wait_recv()
            fw = to_sibling(t, k, 2 * px + py, c)
            fw.start()
            forwards.append(fw)
    for t in range(n):
        for k, (px, py) in enumerate(chips):
            to_sibling(t, k, 2 * px + py, 1 - c).wait_recv()
    for t in range(n):
        for k, chip in enumerate(chips):
            over_ici(t, k, chip, me).wait_send()
    for fw in forwards:
        fw.wait_send()
    for t in range(n):
        pltpu.make_async_copy(ins[t], outs[t].at[me], local.at[t]).wait()


def _gather_sems(n):
    return [pltpu.SemaphoreType.DMA((3 * n,)), pltpu.SemaphoreType.DMA((3 * n,)), pltpu.SemaphoreType.DMA((3 * n,)),
            pltpu.SemaphoreType.DMA((3 * n,)), pltpu.SemaphoreType.DMA((n,))]


def _gathered_shapes(shards):
    return [jax.ShapeDtypeStruct((N_CHIPS,) + s.shape, s.dtype) for s in shards]


def _swap_copies(srcs, dsts, axes, send, recv):
    x, y, c = lax.axis_index("x"), lax.axis_index("y"), lax.axis_index("c")
    return [pltpu.make_async_remote_copy(
        src_ref=srcs[t] if axes[t] is None else _half(srcs[t], axes[t], 1 - c), dst_ref=dsts[t],
        send_sem=send.at[t], recv_sem=recv.at[t], device_id=(x, y, 1 - c), device_id_type=MESH) for t in range(len(srcs))]


def _swap_shapes(srcs, axes):
    return [jax.ShapeDtypeStruct(g.shape if a is None else _half_shape(g.shape, a), g.dtype) for g, a in zip(srcs, axes)]


def _sibling_exchange(gs, axes):
    n = len(gs)

    def body(*refs):
        cps = _swap_copies(refs[:n], refs[n:2 * n], axes, *refs[2 * n:])
        for cp in cps:
            cp.start()
        for cp in cps:
            cp.wait()

    return pl.pallas_call(
        body, name="grad_sibling_exchange",
        in_specs=[HBM] * n, out_specs=[HBM] * n, out_shape=_swap_shapes(gs, axes),
        scratch_shapes=[pltpu.SemaphoreType.DMA((n,)), pltpu.SemaphoreType.DMA((n,))],
    )(*_in_hbm(*gs))


def _add_halves_w_in(parts, rbs, order, core):
    n_parts, w = len(parts), rbs[0].shape[1]
    per = sum(hi - lo for _, lo, hi in order) // N_CHIPS
    used = [max(hi for t, _, hi in order if t == s) for s in range(n_parts)]
    pieces = [[] for _ in range(N_CHIPS)]
    r = 0
    for src, lo, hi in order:
        while lo < hi:
            c, at = divmod(r, per)
            n = min(hi - lo, per - at)
            pieces[c].append((src, lo, n, at))
            lo, r = lo + n, r + n

    def body(core_ref, *refs):
        p_refs, r_refs, o_ref = refs[:n_parts], refs[n_parts:2 * n_parts], refs[2 * n_parts]
        mine, other = refs[2 * n_parts + 1:3 * n_parts + 1], refs[3 * n_parts + 1:4 * n_parts + 1]
        rows_buf, sems = refs[4 * n_parts + 1:]
        cols = pl.ds(pl.multiple_of(core_ref[0] * w, w), w)
        loads = [(pltpu.make_async_copy(p_refs[t].at[pl.ds(0, used[t]), cols], mine[t], sems.at[2 * t]),
                  pltpu.make_async_copy(r_refs[t].at[pl.ds(0, used[t]), :], other[t], sems.at[2 * t + 1]))
                 for t in range(n_parts)]
        for pair in loads:
            for cp in pair:
                cp.start()
        there = set()
        for c in range(N_CHIPS):
            for src, lo, n, at in pieces[c]:
                if src not in there:
                    for cp in loads[src]:
                        cp.wait()
                    there.add(src)
                rows_buf[at:at + n, :] = mine[src][lo:lo + n, :] + other[src][lo:lo + n, :]
            o_ref[c] = rows_buf[0:per, :].astype(BF16)

    any_space = pl.BlockSpec(memory_space=pl.ANY)
    return pl.pallas_call(
        body, name="grad_add_sibling_w_in",
        grid_spec=pltpu.PrefetchScalarGridSpec(
            num_scalar_prefetch=1, grid=(1,),
            in_specs=[any_space] * (2 * n_parts),
            out_specs=pl.BlockSpec((N_CHIPS, per, w), lambda i, cr: (0, 0, 0)),
            scratch_shapes=[pltpu.VMEM((rows, w), F32) for rows in used] * 2
            + [pltpu.VMEM((per + (-per) % 8, w), F32), pltpu.SemaphoreType.DMA((2 * n_parts,))]),
        out_shape=jax.ShapeDtypeStruct((N_CHIPS, per, w), BF16),
        compiler_params=_params(("arbitrary",)),
    )(core, *parts, *rbs)


def _weight_sections(wg):
    per, d = wg.shape[1], wg.shape[2]
    n_ref = N_CHIPS * per
    units = [(0, 0, REF_U, REF_Q), (1, 0, REF_Q, REF_ZA), (0, REF_Q - REF_U, REF_ZA, REF_F), (2, 0, REF_F, REF_GP),
             (0, REF_Q - REF_U + REF_F - REF_ZA, REF_GP, n_ref)]
    plan = []
    for out, first, lo, hi in sorted(units, key=lambda u: u[2]):
        pieces, at = [], 0
        while lo < hi:
            c, src = divmod(lo, per)
            n = min(hi - lo, per - src)
            pieces.append((c, src, n, at))
            lo, at = lo + n, at + n
        plan.append((out, first, at, pieces))
    stage_rows = max(rows for _, _, rows, _ in plan)

    def body(g_ref, e_ref, q_ref, f_ref, gb, stage, sems):
        loads = [pltpu.make_async_copy(g_ref.at[c], gb.at[c], sems.at[c]) for c in range(N_CHIPS)]
        for cp in loads:
            cp.start()
        outs = [e_ref, q_ref, f_ref]
        there = set()
        for out, first, rows, pieces in plan:
            for c, _, _, _ in pieces:
                if c not in there:
                    loads[c].wait()
                    there.add(c)
            if len(pieces) == 1 and pieces[0][1] % 16 == 0 and rows % 16 == 0:
                c, src, _, _ = pieces[0]
                outs[out][first:first + rows, :] = gb[c, src:src + rows, :]
                continue
            for c, src, n, at in pieces:
                stage[at:at + n, :] = gb[c, src:src + n, :].astype(F32)
            padded = rows if rows % 16 == 0 else outs[out].shape[0] - first
            if padded > rows:
                stage[rows:padded, :] = jnp.zeros((padded - rows, d), F32)
            outs[out][first:first + padded, :] = stage[0:padded, :].astype(BF16)

    return pl.pallas_call(
        body, name="weight_sections",
        in_specs=[pl.BlockSpec(memory_space=pl.ANY)],
        out_shape=[jax.ShapeDtypeStruct((E_COLS, d), BF16), jax.ShapeDtypeStruct((QKV_COLS, d), BF16),
                   jax.ShapeDtypeStruct((F_COLS, d), BF16)],
        scratch_shapes=[pltpu.VMEM(wg.shape, BF16), pltpu.VMEM((stage_rows, d), F32), pltpu.SemaphoreType.DMA((N_CHIPS,))],
        compiler_params=_params(),
    )(wg)


def _add_sibling_half(gs, rbs, axes, small, sr, core):
    n = len(gs)

    def body(core_ref, *refs):
        g_refs, rb_refs = refs[:n], refs[n:2 * n]
        s_ref, sr_ref = refs[2 * n], refs[2 * n + 1]
        h_refs, sc_ref = refs[2 * n + 2:3 * n + 2], refs[3 * n + 2]
        for t in range(n):
            h_refs[t][...] = (g_refs[t][...] + rb_refs[t][...]).astype(BF16)
        sc_ref[...] = s_ref[...] + sr_ref[...]

    def mine(rb, axis):
        blk = (None,) + rb.shape[1:]
        if axis == 2:
            return pl.BlockSpec(blk, lambda j, cr: (j, 0, cr[0]))
        return pl.BlockSpec(blk, lambda j, cr: (j, cr[0], 0))

    chunk = lambda rb: pl.BlockSpec((None,) + rb.shape[1:], lambda j, cr: (j, 0, 0))
    whole = pl.BlockSpec(small.shape, lambda j, cr: (0, 0))
    return pl.pallas_call(
        body, name="grad_add_sibling",
        grid_spec=pltpu.PrefetchScalarGridSpec(
            num_scalar_prefetch=1, grid=(N_CHIPS,),
            in_specs=[mine(rb, a) for rb, a in zip(rbs, axes)] + [chunk(rb) for rb in rbs] + [whole, whole],
            out_specs=[chunk(rb) for rb in rbs] + [whole]),
        out_shape=[jax.ShapeDtypeStruct(rb.shape, BF16) for rb in rbs] + [jax.ShapeDtypeStruct(small.shape, F32)],
        compiler_params=_params(("arbitrary",)),
    )(core, *gs, *rbs, small, sr)


def _chip_scatter_plan(h_refs, s_ref, q_refs, sq_ref, send, recv, local):
    n = len(h_refs)
    nt = n + (s_ref is not None)
    x, y, c = lax.axis_index("x"), lax.axis_index("y"), lax.axis_index("c")
    me = 2 * x + y
    chips = [(1 - x, y), (x, 1 - y), (1 - x, 1 - y)]

    def copy(t, k, chip, src_slot, dst_slot):
        src = h_refs[t].at[src_slot] if t < n else s_ref
        dst = (q_refs[t] if t < n else sq_ref).at[dst_slot]
        return pltpu.make_async_remote_copy(src_ref=src, dst_ref=dst, send_sem=send.at[3 * t + k],
                                            recv_sem=recv.at[3 * t + k], device_id=(*chip, c), device_id_type=MESH)

    own = [pltpu.make_async_copy(h_refs[t].at[me], q_refs[t].at[me], local.at[t]) for t in range(n)]
    if s_ref is not None:
        own.append(pltpu.make_async_copy(s_ref, sq_ref.at[me], local.at[n]))
    sends = [copy(t, k, (px, py), 2 * px + py, me) for t in range(nt) for k, (px, py) in enumerate(chips)]
    recvs = [copy(t, k, (px, py), me, 2 * px + py) for t in range(nt) for k, (px, py) in enumerate(chips)]
    return own, sends, recvs


def _scatter_operands(hs, sc):
    small = [] if sc is None else [sc]
    nt = len(hs) + len(small)
    shapes = [jax.ShapeDtypeStruct(h.shape, h.dtype) for h in hs] + [
        jax.ShapeDtypeStruct((N_CHIPS,) + a.shape, a.dtype) for a in small]
    sems = [pltpu.SemaphoreType.DMA((3 * nt,)), pltpu.SemaphoreType.DMA((3 * nt,)), pltpu.SemaphoreType.DMA((nt,))]
    return list(hs) + small, shapes, sems


def _small_allreduce(buf):
    def body(b_ref, o_ref, sib_buf, chip_buf, send, recv):
        x, y, c = lax.axis_index("x"), lax.axis_index("y"), lax.axis_index("c")
        me = 2 * x + y
        chips = [(1 - x, y), (x, 1 - y), (1 - x, 1 - y)]
        swap = pltpu.make_async_remote_copy(src_ref=b_ref, dst_ref=sib_buf, send_sem=send.at[0], recv_sem=recv.at[0],
                                            device_id=(x, y, 1 - c), device_id_type=MESH)
        swap.start()
        swap.wait()
        chip_buf[me] = b_ref[...] + sib_buf[...]

        def copy(k, chip, slot):
            return pltpu.make_async_remote_copy(src_ref=chip_buf.at[slot], dst_ref=chip_buf.at[slot], send_sem=send.at[1 + k],
                                                recv_sem=recv.at[1 + k], device_id=(*chip, c), device_id_type=MESH)

        sends = [copy(k, chip, me) for k, chip in enumerate(chips)]
        for cp in sends:
            cp.start()
        for k, (px, py) in enumerate(chips):
            copy(k, (px, py), 2 * px + py).wait_recv()
        for cp in sends:
            cp.wait_send()
        o_ref[...] = ((chip_buf[0] + chip_buf[1]) + chip_buf[2]) + chip_buf[3]

    vmem = pl.BlockSpec(memory_space=pltpu.VMEM)
    return pl.pallas_call(
        body, name="small_allreduce", in_specs=[vmem], out_specs=vmem,
        out_shape=jax.ShapeDtypeStruct(buf.shape, F32),
        scratch_shapes=[pltpu.VMEM(buf.shape, F32), pltpu.VMEM((N_CHIPS,) + buf.shape, F32),
                        pltpu.SemaphoreType.DMA((4,)), pltpu.SemaphoreType.DMA((4,))],
        compiler_params=_params(),
    )(buf)


def _reduce_allgather(qs, axes, sq, as_rows):
    n = len(qs)
    shard_shapes, half_axes = [], []
    for q, a, rows_form in zip(qs, axes, as_rows):
        shape = [d * 2 if i == a - 1 else d for i, d in enumerate(q.shape[1:])]
        if rows_form:
            assert a == 2
            shape = [shape[0], 1, shape[1]]
        shard_shapes.append(tuple(shape))
        half_axes.append(2 if rows_form else a - 1)

    def body(*refs):
        q_refs, sq_ref = refs[:n], refs[n]
        o_refs, st_ref = refs[n + 1:2 * n + 1], refs[2 * n + 1]
        send, recv = refs[2 * n + 2:]
        x, y, c = lax.axis_index("x"), lax.axis_index("y"), lax.axis_index("c")

        def swap(t, which):
            return pltpu.make_async_remote_copy(
                src_ref=_half(o_refs[t], half_axes[t], which), dst_ref=_half(o_refs[t], half_axes[t], which),
                send_sem=send.at[t], recv_sem=recv.at[t], device_id=(x, y, 1 - c), device_id_type=MESH)

        sent = []
        for t in range(n):
            q = q_refs[t]
            total = ((q[0].astype(F32) + q[1].astype(F32)) + q[2].astype(F32)) + q[3].astype(F32)
            if as_rows[t]:
                total = total.reshape(total.shape[0], 1, total.shape[1])
            _half(o_refs[t], half_axes[t], c)[...] = total
            cp = swap(t, c)
            cp.start()
            sent.append(cp)
        st_ref[...] = ((sq_ref[0] + sq_ref[1]) + sq_ref[2]) + sq_ref[3]
        for t in range(n):
            swap(t, 1 - c).wait_recv()
        for cp in sent:
            cp.wait_send()

    vmem = pl.BlockSpec(memory_space=pltpu.VMEM)
    return pl.pallas_call(
        body, name="grad_reduce_allgather",
        in_specs=[vmem] * (n + 1), out_specs=[vmem] * (n + 1),
        out_shape=[jax.ShapeDtypeStruct(s, F32) for s in shard_shapes] + [jax.ShapeDtypeStruct(sq.shape[1:], F32)],
        scratch_shapes=[pltpu.SemaphoreType.DMA((n,)), pltpu.SemaphoreType.DMA((n,))],
        compiler_params=_params(),
    )(*qs, sq)


def _dot_nt(a, b):
    return lax.dot_general(a, b, (((1,), (1,)), ((), ())), preferred_element_type=F32)


def _dot_tn(a, b):
    return lax.dot_general(a, b, (((0,), (0,)), ((), ())), preferred_element_type=F32)


def _mm_nt(a, bt, out_dtype, name, m, n, tm, tn, row_block=0, scale_first=None, gather=None):
    k = a.shape[1]
    shards, axes = gather if gather is not None else ([], [])
    ng = len(shards)
    grid = (m // tm, n // tn)

    def body(a_ref, b_ref, *rest):
        ins, o_ref, outs, sems = rest[:ng], rest[ng], rest[ng + 1:2 * ng + 1], rest[2 * ng + 1:]
        first = (pl.program_id(0) == 0) & (pl.program_id(1) == 0)
        last = (pl.program_id(0) == grid[0] - 1) & (pl.program_id(1) == grid[1] - 1)
        if ng:
            @pl.when(first)
            def _():
                _gather_start(ins, outs, axes, *sems)

        r = _dot_nt(a_ref[...], b_ref[...])
        if scale_first is not None:
            r = r * jnp.where(pl.program_id(1) == 0, scale_first, 1.0)
        o_ref[...] = r.astype(out_dtype)

        if ng:
            @pl.when(last)
            def _():
                _gather_finish(ins, outs, axes, *sems)

    res = pl.pallas_call(
        body, name=name, grid=grid,
        in_specs=[pl.BlockSpec((tm, k), lambda i, j: (i, 0)), pl.BlockSpec((tn, k), lambda i, j: (row_block + j, 0))]
        + [HBM] * ng,
        out_specs=[pl.BlockSpec((tm, tn), lambda i, j: (i, j))] + [HBM] * ng,
        out_shape=[jax.ShapeDtypeStruct((m, n), out_dtype)] + _gathered_shapes(shards),
        scratch_shapes=_gather_sems(ng) if ng else [],
        compiler_params=_params(("arbitrary", "arbitrary") if ng else ("parallel", "parallel")),
    )(a, bt, *_in_hbm(*shards))
    return (res[0], res[1:]) if ng else res[0]


def _mm_tn(a, b, name, k, tm, tn, tk, chunks=1, m=None, m_off=0, swap=None, scatter=None):
    m = a.shape[1] if m is None else m
    n = b.shape[1]
    cw = n // chunks
    srcs, axes = swap if swap is not None else ([], [])
    ns = len(srcs)
    hs, sc = scatter if scatter is not None else ([], None)
    sc_in, sc_shapes, sc_sems = _scatter_operands(hs, sc) if scatter is not None else ([], [], [])
    nh, nq = len(hs), len(sc_in)
    grid = (m // tm, n // tn, k // tk)

    def body(a_ref, b_ref, *rest):
        s_refs, h_refs = rest[:ns], rest[ns:ns + nq]
        o_ref = rest[ns + nq]
        d_refs, q_refs = rest[ns + nq + 1:2 * ns + nq + 1], rest[2 * ns + nq + 1:2 * ns + 2 * nq + 1]
        sems = rest[2 * ns + 2 * nq + 1:]
        swap_sems, scatter_sems = (sems[:2], sems[2:]) if ns else ((), sems)
        ids = [pl.program_id(d) for d in range(3)]

        def scatter_plan():
            small_in = h_refs[nh] if sc is not None else None
            small_out = q_refs[nh] if sc is not None else None
            return _chip_scatter_plan(h_refs[:nh], small_in, q_refs[:nh], small_out, *scatter_sems)

        if ns or nq:
            @pl.when((ids[0] == 0) & (ids[1] == 0) & (ids[2] == 0))
            def _():
                if ns:
                    for cp in _swap_copies(s_refs, d_refs, axes, *swap_sems):
                        cp.start()
                if nq:
                    own, sends, _ = scatter_plan()
                    for cp in own + sends:
                        cp.start()

        @pl.when(ids[2] == 0)
        def _():
            o_ref[...] = jnp.zeros_like(o_ref)
        r = _dot_tn(a_ref[...].astype(BF16), b_ref[...].astype(BF16))
        if chunks > 1:
            for c in range(chunks):
                o_ref[c] += r[:, c * cw:(c + 1) * cw]
        else:
            o_ref[...] += r

        if ns or nq:
            @pl.when((ids[0] == grid[0] - 1) & (ids[1] == grid[1] - 1) & (ids[2] == grid[2] - 1))
            def _():
                if ns:
                    for cp in _swap_copies(s_refs, d_refs, axes, *swap_sems):
                        cp.wait()
                if nq:
                    own, sends, recvs = scatter_plan()
                    for cp in recvs:
                        cp.wait_recv()
                    for cp in sends:
                        cp.wait_send()
                    for cp in own:
                        cp.wait()

    if chunks > 1:
        assert tn == n
        out_spec = pl.BlockSpec((chunks, tm, cw), lambda i, j, kk: (0, i, 0))
        out_shape = jax.ShapeDtypeStruct((chunks, m, cw), F32)
    else:
        out_spec = pl.BlockSpec((tm, tn), lambda i, j, kk: (i, j))
        out_shape = jax.ShapeDtypeStruct((m, n), F32)
    riders = ns + nq
    res = pl.pallas_call(
        body, name=name, grid=grid,
        in_specs=[pl.BlockSpec((tk, tm), lambda i, j, kk: (kk, m_off + i)), pl.BlockSpec((tk, tn), lambda i, j, kk: (kk, j))]
        + [HBM] * riders,
        out_specs=[out_spec] + [HBM] * riders, out_shape=[out_shape] + _swap_shapes(srcs, axes) + sc_shapes,
        scratch_shapes=([pltpu.SemaphoreType.DMA((ns,)), pltpu.SemaphoreType.DMA((ns,))] if ns else []) + sc_sems,
        compiler_params=_params(("arbitrary",) * 3 if riders else ("parallel", "parallel", "arbitrary")),
    )(a, b, *_in_hbm(*srcs, *sc_in))
    if not riders:
        return res[0]
    out = [res[0]]
    if ns:
        out.append(res[1:1 + ns])
    if nq:
        out.append((res[1 + ns:1 + ns + nh], res[1 + ns + nh] if sc is not None else None))
    return tuple(out)


def _tokens_spec():
    return pl.BlockSpec((ROW_TILE, D_MODEL), lambda i: (jnp.maximum(i - 1, 0), 0))


def _rmsnorm_fwd(x2, g1, lk, shards, axes):
    nb = x2.shape[0] // ROW_TILE + 1
    nblk = lk // ROW_TILE
    ng = len(shards)
    meta_cols = shards[-1].shape[1]

    def body(x_ref, g_ref, *rest):
        ins, hn_ref, outs = rest[:ng], rest[ng], rest[ng + 1:2 * ng + 1]
        sems, meta_buf, meta_sem = rest[2 * ng + 1:2 * ng + 6], rest[2 * ng + 6], rest[2 * ng + 7]
        s = pl.program_id(0)
        blk = (s + 1) % nblk

        @pl.when(s == 0)
        def _():
            _gather_start(ins, outs, axes, *sems)

        def normed(h):
            r = lax.rsqrt(jnp.mean(h * h, axis=-1, keepdims=True) + RMS_EPS)
            return ((h * r) * g_ref[...]).astype(BF16)

        @pl.when(s < nblk - 1)
        def _():
            hn_ref[...] = normed(jnp.where(blk >= nb, 0.0, x_ref[...]))

        @pl.when(s == nblk - 1)
        def _():
            _gather_finish(ins, outs, axes, *sems)
            fetch = pltpu.make_async_copy(outs[-1], meta_buf, meta_sem.at[0])
            fetch.start()
            fetch.wait()
            meta = jnp.concatenate([meta_buf[j] for j in range(N_CHIPS)], axis=1)
            hn_ref[...] = normed(jnp.concatenate([jnp.zeros((PAD_ROWS, D_MODEL), F32), meta], axis=0))

    res = pl.pallas_call(
        body, name="rmsnorm_fwd", grid=(nblk,),
        in_specs=[pl.BlockSpec((ROW_TILE, D_MODEL), lambda s: (jnp.clip((s + 1) % nblk - 1, 0, nb - 2), 0)),
                  pl.BlockSpec((1, D_MODEL), lambda s: (0, 0))] + [HBM] * ng,
        out_specs=[pl.BlockSpec((ROW_TILE, D_MODEL), lambda s: ((s + 1) % nblk, 0))] + [HBM] * ng,
        out_shape=[jax.ShapeDtypeStruct((lk, D_MODEL), BF16)] + _gathered_shapes(shards),
        scratch_shapes=_gather_sems(ng) + [pltpu.VMEM((N_CHIPS, N_META, meta_cols), F32), pltpu.SemaphoreType.DMA((1,))],
        compiler_params=_params(("arbitrary",)),
    )(x2, g1, *_in_hbm(*shards))
    return res[0], res[1:]


def _valid_gate_mask(i):
    row = i * ROW_TILE + lax.broadcasted_iota(jnp.int32, (ROW_TILE, F_COLS), 0)
    col = lax.broadcasted_iota(jnp.int32, (ROW_TILE, F_COLS), 1)
    return (row >= PAD_ROWS) & (col < N_HEADS)


def _gates_fwd(f, bfg):
    nb = f.shape[0] // ROW_TILE

    def body(f_ref, b_ref, c_ref, carry):
        i = pl.program_id(0)

        @pl.when(i == 0)
        def _():
            carry[...] = jnp.zeros_like(carry)

        logit = f_ref[...] + b_ref[...]
        lf = jnp.minimum(logit, 0.0) - jnp.log1p(jnp.exp(-jnp.abs(logit)))
        lf = jnp.where(_valid_gate_mask(i), lf, 0.0)
        r_i = lax.broadcasted_iota(jnp.int32, (ROW_TILE, ROW_TILE), 0)
        c_i = lax.broadcasted_iota(jnp.int32, (ROW_TILE, ROW_TILE), 1)
        tri = (c_i <= r_i).astype(F32)
        c_ref[...] = jnp.dot(tri, lf, precision=HIGHEST, preferred_element_type=F32) + carry[...]
        carry[...] = carry[...] + jnp.sum(lf, axis=0, keepdims=True)

    return pl.pallas_call(
        body, name="gates_fwd", grid=(nb,),
        in_specs=[pl.BlockSpec((ROW_TILE, F_COLS), lambda i: (i, 0)), pl.BlockSpec((1, F_COLS), lambda i: (0, 0))],
        out_specs=pl.BlockSpec((ROW_TILE, F_COLS), lambda i: (i, 0)),
        out_shape=jax.ShapeDtypeStruct(f.shape, F32),
        scratch_shapes=[pltpu.VMEM((1, F_COLS), F32)],
        compiler_params=_params(("arbitrary",)),
    )(f, bfg)


def _pool_counts(i):
    row = i * ROW_TILE + lax.broadcasted_iota(jnp.int32, (ROW_TILE, 1), 0)
    return jnp.maximum(row - PAD_ROWS, 0)


def _trailing_sums(xc, levels):
    acc = xc
    for lv in range(levels):
        acc = acc + pltpu.roll(acc, 1 << lv, 0)
    return acc


def _leading_sums(xc, levels):
    n = xc.shape[0]
    acc = xc
    for lv in range(levels):
        acc = acc + pltpu.roll(acc, n - (1 << lv), 0)
    return acc


def _pool_p(u_cur, u_prev, i):
    pos = _pool_counts(i)
    ps, invs = [], []
    for g, w in enumerate(POOL_WINDOWS):
        sl = slice(g * POOL_GROUP, (g + 1) * POOL_GROUP)
        cur = u_cur[:, sl]
        xc = jnp.concatenate([u_prev[:, sl], cur], axis=0)
        win = _trailing_sums(xc, g + 1)[ROW_TILE:, :]
        inv = 1.0 / jnp.minimum(pos + 1, w).astype(F32)
        ps.append(win * inv - cur)
        invs.append(inv)
    return ps, invs


def _pool_fwd(e, pw, scale):
    nb = e.shape[0] // ROW_TILE

    def body(uc_ref, up_ref, z_ref, pw_ref, sc_ref, y_ref):
        i = pl.program_id(0)
        u_cur = uc_ref[...].astype(F32)
        u_prev = jnp.where(i == 0, 0.0, up_ref[...].astype(F32))
        ps, _ = _pool_p(u_cur, u_prev, i)
        z = z_ref[...].astype(F32)
        gate = z * _sigmoid(z)
        for g in range(len(POOL_WINDOWS)):
            sl = slice(g * POOL_GROUP, (g + 1) * POOL_GROUP)
            yraw = jnp.dot(ps[g].astype(BF16), pw_ref[g], preferred_element_type=F32)
            y_ref[:, sl] = ((yraw * sc_ref[:, sl]) * gate[:, sl]).astype(BF16)

    blk = (ROW_TILE, POOL_WIDTH)
    return pl.pallas_call(
        body, name="pool_fwd", grid=(nb,),
        in_specs=[pl.BlockSpec(blk, lambda i: (i, 0)), pl.BlockSpec(blk, lambda i: (jnp.maximum(i - 1, 0), 0)),
                  pl.BlockSpec(blk, lambda i: (i, 1)),
                  pl.BlockSpec((len(POOL_WINDOWS), POOL_GROUP, POOL_GROUP), lambda i: (0, 0, 0)),
                  pl.BlockSpec((1, POOL_WIDTH), lambda i: (0, 0))],
        out_specs=pl.BlockSpec(blk, lambda i: (i, 0)),
        out_shape=jax.ShapeDtypeStruct((e.shape[0], POOL_WIDTH), BF16),
        compiler_params=_params(("parallel",)),
    )(e, e, e, pw, scale)


def _stack_heads(a):
    first = lax.broadcasted_iota(jnp.int32, a.shape, 1) < HEAD_DIM
    zero = jnp.zeros_like(a)
    return jnp.concatenate([jnp.where(first, a, zero), jnp.where(first, zero, a)], axis=0)


def _unstack_heads(a):
    rows = a.shape[0] // 2
    first = lax.broadcasted_iota(jnp.int32, (rows, LANES), 1) < HEAD_DIM
    return jnp.where(first, a[:rows], a[rows:])


def _causal(i, jj, stacked, kv_tile):
    r = lax.broadcasted_iota(jnp.int32, (stacked * ROW_TILE, kv_tile), 0)
    if stacked == 2:
        r = jnp.where(r >= ROW_TILE, r - ROW_TILE, r)
    kidx = jj * kv_tile + lax.broadcasted_iota(jnp.int32, (stacked * ROW_TILE, kv_tile), 1)
    return kidx <= i * ROW_TILE + r


def _stack_rows(b):
    n = b.shape[1]
    return jnp.concatenate([jnp.broadcast_to(b[0:1], (ROW_TILE, n)), jnp.broadcast_to(b[1:2], (ROW_TILE, n))], axis=0)


def _attn_fwd(qkv, bias, nb):
    lk = qkv.shape[0]
    lp = nb * ROW_TILE
    nkb = lk // KV_TILE
    n_pairs = N_HEADS // 2

    def body(q_ref, k_ref, v_ref, b_ref, o_ref, lse_ref):
        i = pl.program_id(1)
        qs = _stack_heads(q_ref[...])
        last = (i * ROW_TILE) // KV_TILE

        def block(jj, carry, masked, n_keys=KV_TILE):
            m, l, acc = carry
            rows = pl.ds(pl.multiple_of(jj * KV_TILE, KV_TILE), n_keys)
            s = _dot_nt(qs, k_ref[rows, :]) - _stack_rows(b_ref[0, 0, :, rows])
            if masked:
                s = jnp.where(_causal(i, jj, 2, KV_TILE)[:, :n_keys], s, NEG)
            m_new = jnp.maximum(m, jnp.max(s, axis=1, keepdims=True))
            alpha = jnp.exp(m - m_new)
            p = jnp.exp(s - m_new)
            l = alpha * l + jnp.sum(p, axis=1, keepdims=True)
            acc = alpha * acc + jnp.dot(p.astype(BF16), v_ref[rows, :], preferred_element_type=F32)
            return m_new, l, acc

        init = (jnp.full((2 * ROW_TILE, 1), NEG, F32), jnp.zeros((2 * ROW_TILE, 1), F32),
                jnp.zeros((2 * ROW_TILE, LANES), F32))
        def finish(carry):
            m, l, acc = carry
            o_ref[...] = _unstack_heads(acc / l)
            lse_ref[...] = _unstack_heads(jnp.broadcast_to(m + jnp.log(l), (2 * ROW_TILE, LANES)))

        carry = lax.fori_loop(0, last, lambda jj, c: block(jj, c, False), init)
        ends = [lambda c, n=(r + 1) * ROW_TILE: finish(block(last, c, True, n)) for r in range(KV_TILE // ROW_TILE)]
        lax.switch(i - last * (KV_TILE // ROW_TILE), ends, carry)

    return pl.pallas_call(
        body, name="attn_fwd", grid=(n_pairs, nb),
        in_specs=[pl.BlockSpec((ROW_TILE, LANES), lambda hp, i: (i, hp)),
                  pl.BlockSpec((lk, LANES), lambda hp, i: (0, n_pairs + hp)),
                  pl.BlockSpec((lk, LANES), lambda hp, i: (0, 2 * n_pairs + hp)),
                  pl.BlockSpec((1, 1, 2, lk), lambda hp, i: (i, hp, 0, 0))],
        out_specs=[pl.BlockSpec((ROW_TILE, LANES), lambda hp, i: (i, hp)),
                   pl.BlockSpec((ROW_TILE, LANES), lambda hp, i: (i, hp))],
        out_shape=[jax.ShapeDtypeStruct((lp, ATTN_WIDTH), F32), jax.ShapeDtypeStruct((lp, ATTN_WIDTH), F32)],
        compiler_params=_params(("parallel", "parallel")),
    )(qkv, qkv, qkv, bias)


def _merge_fwd(y_pool, o, e, wup_p, wup_a):
    lp = o.shape[0]
    nb = lp // ROW_TILE

    def body(yp_ref, o_ref, e_ref, wp_ref, wa_ref, mg_ref, ya_ref):
        za = e_ref[:, E_ZA:E_GP].astype(F32)
        ya = (o_ref[...] * (za * _sigmoid(za))).astype(BF16)
        ya_ref[...] = ya
        a_pool = jnp.dot(yp_ref[...], wp_ref[...], preferred_element_type=F32)
        a_attn = jnp.dot(ya, wa_ref[...], preferred_element_type=F32)
        mg_ref[...] = (_sigmoid(e_ref[:, E_GP:E_GA].astype(F32)) * a_pool
                       + _sigmoid(e_ref[:, E_GA:E_COLS].astype(F32)) * a_attn).astype(BF16)

    return pl.pallas_call(
        body, name="merge_fwd", grid=(nb,),
        in_specs=[pl.BlockSpec((ROW_TILE, POOL_WIDTH), lambda i: (i, 0)),
                  pl.BlockSpec((ROW_TILE, ATTN_WIDTH), lambda i: (i, 0)),
                  pl.BlockSpec((ROW_TILE, E_COLS), lambda i: (i, 0)),
                  pl.BlockSpec((POOL_WIDTH, D_MODEL), lambda i: (0, 0)),
                  pl.BlockSpec((ATTN_WIDTH, D_MODEL), lambda i: (0, 0))],
        out_specs=[pl.BlockSpec((ROW_TILE, D_MODEL), lambda i: (i, 0)),
                   pl.BlockSpec((ROW_TILE, ATTN_WIDTH), lambda i: (i, 0))],
        out_shape=[jax.ShapeDtypeStruct((lp, D_MODEL), BF16), jax.ShapeDtypeStruct((lp, ATTN_WIDTH), BF16)],
        compiler_params=_params(("parallel",)),
    )(y_pool, o, e, wup_p, wup_a)


def _head_fwd_bwd(merged, w_out, x2, metapad, gf, target):
    lp = merged.shape[0]
    nb = lp // ROW_TILE

    def body(mg_ref, w_ref, x_ref, mp_ref, g_ref, t_ref, dh_ref, loss_ref, dg_ref):
        i = pl.program_id(0)

        @pl.when(i == 0)
        def _():
            loss_ref[...] = jnp.zeros_like(loss_ref)
            dg_ref[...] = jnp.zeros_like(dg_ref)

        h0 = jnp.where(i == 0, mp_ref[...], x_ref[...])
        h1 = h0 + jnp.dot(mg_ref[...], w_ref[...], preferred_element_type=F32)
        r = lax.rsqrt(jnp.mean(h1 * h1, axis=-1, keepdims=True) + RMS_EPS)
        xhat = h1 * r
        g = g_ref[...]
        err = jnp.where(i == 0, 0.0, xhat * g - t_ref[...])
        loss_ref[...] += 0.5 * jnp.sum(jnp.mean(err * err, axis=-1, keepdims=True))
        dy = err / D_MODEL
        dg_ref[...] += jnp.sum(dy * xhat, axis=0, keepdims=True)
        dxhat = dy * g
        dh_ref[...] = r * (dxhat - xhat * jnp.mean(dxhat * xhat, axis=-1, keepdims=True))

    return pl.pallas_call(
        body, name="head_fwd_bwd", grid=(nb,),
        in_specs=[pl.BlockSpec((ROW_TILE, D_MODEL), lambda i: (i, 0)),
                  pl.BlockSpec((D_MODEL, D_MODEL), lambda i: (0, 0)),
                  _tokens_spec(), pl.BlockSpec((ROW_TILE, D_MODEL), lambda i: (0, 0)),
                  pl.BlockSpec((1, D_MODEL), lambda i: (0, 0)), _tokens_spec()],
        out_specs=[pl.BlockSpec((ROW_TILE, D_MODEL), lambda i: (i, 0)),
                   pl.BlockSpec((1, LANES), lambda i: (0, 0)), pl.BlockSpec((1, D_MODEL), lambda i: (0, 0))],
        out_shape=[jax.ShapeDtypeStruct((lp, D_MODEL), F32), jax.ShapeDtypeStruct((1, LANES), F32),
                   jax.ShapeDtypeStruct((1, D_MODEL), F32)],
        compiler_params=_params(("arbitrary",)),
    )(merged, w_out, x2, metapad, gf, target)


def _per_head_rowsum(t):
    head = lax.broadcasted_iota(jnp.int32, t.shape, 1) // HEAD_DIM
    out = jnp.zeros_like(t)
    for h in range(N_HEADS):
        sel = head == h
        out = jnp.where(sel, jnp.sum(jnp.where(sel, t, 0.0), axis=1, keepdims=True), out)
    return out


def _merge_bwd(dh1, w_out, y_pool, y_attn, wup_p, wup_a, e, o):
    lp = o.shape[0]
    nb = lp // ROW_TILE

    def body(dh_ref, wo_ref, yp_ref, ya_ref, wp_ref, wa_ref, e_ref, o_ref,
             dap_ref, daa_ref, dg_ref, do_ref, delta_ref, dyp_ref):
        dmerged = _dot_nt(dh_ref[...].astype(BF16), wo_ref[...])
        a_pool = jnp.dot(yp_ref[...], wp_ref[...], preferred_element_type=F32)
        a_attn = jnp.dot(ya_ref[...], wa_ref[...], preferred_element_type=F32)
        sp = _sigmoid(e_ref[:, E_GP:E_GA].astype(F32))
        sa = _sigmoid(e_ref[:, E_GA:E_COLS].astype(F32))
        dap = (dmerged * sp).astype(BF16)
        daa = (dmerged * sa).astype(BF16)
        dap_ref[...] = dap
        daa_ref[...] = daa
        dg_ref[:, ATTN_WIDTH:ATTN_WIDTH + D_MODEL] = (dmerged * a_pool * (sp * (1.0 - sp))).astype(BF16)
        dg_ref[:, ATTN_WIDTH + D_MODEL:] = (dmerged * a_attn * (sa * (1.0 - sa))).astype(BF16)
        dyp_ref[...] = _dot_nt(dap, wp_ref[...])
        dya = _dot_nt(daa, wa_ref[...])
        za = e_ref[:, E_ZA:E_GP].astype(F32)
        sz = _sigmoid(za)
        o = o_ref[...]
        do = dya * (za * sz)
        do_ref[...] = do.astype(BF16)
        dg_ref[:, :ATTN_WIDTH] = (dya * o * (sz * (1.0 + za * (1.0 - sz)))).astype(BF16)
        delta_ref[...] = _per_head_rowsum(do * o)

    row = lambda w: pl.BlockSpec((ROW_TILE, w), lambda i: (i, 0))
    full = lambda a: pl.BlockSpec(a.shape, lambda i: (0, 0))
    return pl.pallas_call(
        body, name="merge_bwd", grid=(nb,),
        in_specs=[row(D_MODEL), full(w_out), row(POOL_WIDTH), row(ATTN_WIDTH), full(wup_p), full(wup_a),
                  row(E_COLS), row(ATTN_WIDTH)],
        out_specs=[row(D_MODEL), row(D_MODEL), pl.BlockSpec((ROW_TILE, MAIN_COLS // 2), lambda i: (i, 1)),
                   row(ATTN_WIDTH), row(ATTN_WIDTH), row(POOL_WIDTH)],
        out_shape=[jax.ShapeDtypeStruct((lp, D_MODEL), BF16), jax.ShapeDtypeStruct((lp, D_MODEL), BF16),
                   jax.ShapeDtypeStruct((lp, MAIN_COLS), BF16),
                   jax.ShapeDtypeStruct((lp, ATTN_WIDTH), BF16), jax.ShapeDtypeStruct((lp, ATTN_WIDTH), F32),
                   jax.ShapeDtypeStruct((lp, POOL_WIDTH), F32)],
        compiler_params=_params(("parallel",)),
    )(dh1, w_out, y_pool, y_attn, wup_p, wup_a, e, o)


def _fill_dproj(dproj_half, du, dzp, dq, dk, dv):
    lp = dproj_half.shape[0]
    nb = lp // ROW_TILE

    def body(base_ref, du_ref, dzp_ref, dq_ref, dk_ref, dv_ref, o_ref):
        parts = [du_ref[...], dzp_ref[...], (dq_ref[...] * HEAD_DIM ** -0.5).astype(BF16), dk_ref[...], dv_ref[...]]
        for t, p in enumerate(parts):
            o_ref[:, t * ATTN_WIDTH:(t + 1) * ATTN_WIDTH] = p

    blk = pl.BlockSpec((ROW_TILE, ATTN_WIDTH), lambda i: (i, 0))
    return pl.pallas_call(
        body, name="fill_dproj", grid=(nb,),
        in_specs=[pl.BlockSpec(memory_space=pl.ANY)] + [blk] * 5,
        out_specs=pl.BlockSpec((ROW_TILE, MAIN_COLS // 2), lambda i: (i, 0)),
        out_shape=jax.ShapeDtypeStruct(dproj_half.shape, BF16),
        input_output_aliases={0: 0},
        compiler_params=_params(("parallel",)),
    )(dproj_half, du, dzp, dq, dk, dv)


def _pool_bwd_local(e, dy_pool, pw, scale):
    lp = e.shape[0]
    nb = lp // ROW_TILE
    ng = len(POOL_WINDOWS)

    def body(uc_ref, up_ref, z_ref, dy_ref, pw_ref, sc_ref, dpc_ref, dz_ref, dsc_ref, dpw_ref):
        i = pl.program_id(0)

        @pl.when(i == 0)
        def _():
            dsc_ref[...] = jnp.zeros_like(dsc_ref)
            dpw_ref[...] = jnp.zeros_like(dpw_ref)

        u_cur = uc_ref[...].astype(F32)
        u_prev = jnp.where(i == 0, 0.0, up_ref[...].astype(F32))
        ps, invs = _pool_p(u_cur, u_prev, i)
        z = z_ref[...].astype(F32)
        sz = _sigmoid(z)
        dy = dy_ref[...]
        dypre = dy * (z * sz)
        dsilu = sz * (1.0 + z * (1.0 - sz))
        for g in range(ng):
            sl = slice(g * POOL_GROUP, (g + 1) * POOL_GROUP)
            pb = ps[g].astype(BF16)
            w = pw_ref[g]
            yraw = jnp.dot(pb, w, preferred_element_type=F32)
            sc = sc_ref[:, sl]
            dz_ref[:, sl] = (dy[:, sl] * (yraw * sc) * dsilu[:, sl]).astype(BF16)
            dsc_ref[:, sl] += jnp.sum(dypre[:, sl] * yraw, axis=0, keepdims=True)
            dyraw = (dypre[:, sl] * sc).astype(BF16)
            dpw_ref[g] += _dot_tn(pb, dyraw)
            dpc_ref[:, sl] = _dot_nt(dyraw, w) * invs[g]

    blk = (ROW_TILE, POOL_WIDTH)
    return pl.pallas_call(
        body, name="pool_bwd_local", grid=(nb,),
        in_specs=[pl.BlockSpec(blk, lambda i: (i, 0)), pl.BlockSpec(blk, lambda i: (jnp.maximum(i - 1, 0), 0)),
                  pl.BlockSpec(blk, lambda i: (i, 1)), pl.BlockSpec(blk, lambda i: (i, 0)),
                  pl.BlockSpec((ng, POOL_GROUP, POOL_GROUP), lambda i: (0, 0, 0)),
                  pl.BlockSpec((1, POOL_WIDTH), lambda i: (0, 0))],
        out_specs=[pl.BlockSpec(blk, lambda i: (i, 0)), pl.BlockSpec(blk, lambda i: (i, 0)),
                   pl.BlockSpec((1, POOL_WIDTH), lambda i: (0, 0)),
                   pl.BlockSpec((ng, POOL_GROUP, POOL_GROUP), lambda i: (0, 0, 0))],
        out_shape=[jax.ShapeDtypeStruct((lp, POOL_WIDTH), F32), jax.ShapeDtypeStruct((lp, POOL_WIDTH), BF16),
                   jax.ShapeDtypeStruct((1, POOL_WIDTH), F32),
                   jax.ShapeDtypeStruct((ng, POOL_GROUP, POOL_GROUP), F32)],
        compiler_params=_params(("arbitrary",)),
    )(e, e, e, dy_pool, pw, scale)


def _pool_bwd_window(dpc):
    lp = dpc.shape[0]
    nb = lp // ROW_TILE

    def body(cur_ref, nxt_ref, du_ref):
        i = pl.program_id(0)
        cur = cur_ref[...]
        nxt = jnp.where(i == nb - 1, 0.0, nxt_ref[...])
        pos = _pool_counts(i)
        for g, w in enumerate(POOL_WINDOWS):
            sl = slice(g * POOL_GROUP, (g + 1) * POOL_GROUP)
            xc = jnp.concatenate([cur[:, sl], nxt[:, sl]], axis=0)
            win = _leading_sums(xc, g + 1)[:ROW_TILE, :]
            dp = cur[:, sl] * jnp.minimum(pos + 1, w).astype(F32)
            du_ref[:, sl] = (win - dp).astype(BF16)

    blk = (ROW_TILE, POOL_WIDTH)
    return pl.pallas_call(
        body, name="pool_bwd_window", grid=(nb,),
        in_specs=[pl.BlockSpec(blk, lambda i: (i, 0)), pl.BlockSpec(blk, lambda i: (jnp.minimum(i + 1, nb - 1), 0))],
        out_specs=pl.BlockSpec(blk, lambda i: (i, 0)),
        out_shape=jax.ShapeDtypeStruct((lp, POOL_WIDTH), BF16),
        compiler_params=_params(("parallel",)),
    )(dpc, dpc)


def _attn_bwd(qkv, do, lse, delta, bias, nb):
    lk = qkv.shape[0]
    lp = nb * ROW_TILE
    nkb = lk // KV_TILE_BWD
    n_pairs = N_HEADS // 2
    per_kv = KV_TILE_BWD // ROW_TILE

    def body(q_ref, k_ref, v_ref, do_ref, lse_ref, dl_ref, b_ref, dq_ref, dk_ref, dv_ref, dc_ref, dcq_ref,
             dk_acc, dv_acc, dc_acc):
        jj = pl.program_id(1)

        @pl.when(jj == 0)
        def _():
            dq_ref[...] = jnp.zeros_like(dq_ref)
            dcq_ref[...] = jnp.zeros_like(dcq_ref)

        dk_acc[...] = jnp.zeros_like(dk_acc)
        dv_acc[...] = jnp.zeros_like(dv_acc)
        dc_acc[...] = jnp.zeros_like(dc_acc)

        def block(i, n_keys, masked):
            kb, vb = k_ref[:n_keys, :], v_ref[:n_keys, :]
            rows = pl.ds(pl.multiple_of(i * ROW_TILE, ROW_TILE), ROW_TILE)
            qs = _stack_heads(q_ref[rows, :])
            dos = _stack_heads(do_ref[rows, :])
            lse_i, dl_i = lse_ref[rows, :], dl_ref[rows, :]
            s = _dot_nt(qs, kb)
            dp = _dot_nt(dos, vb)
            if masked:
                valid = _causal(i, jj, 1, KV_TILE_BWD)[:, :n_keys]
            ps, dss, dcs, rowsums = [], [], [], []
            for hd in range(2):
                half = slice(hd * ROW_TILE, (hd + 1) * ROW_TILE)
                col = slice(hd * HEAD_DIM, hd * HEAD_DIM + 1)
                sh = s[half] - b_ref[i, 0, hd:hd + 1, :n_keys]
                if masked:
                    sh = jnp.where(valid, sh, NEG)
                p = jnp.exp(sh - lse_i[:, col])
                ds = p * (dp[half] - dl_i[:, col])
                ps.append(p.astype(BF16))
                dss.append(ds.astype(BF16))
                dcs.append(jnp.sum(ds, axis=0, keepdims=True))
                rowsums.append(jnp.sum(ds, axis=1, keepdims=True))
            dsb = jnp.concatenate(dss, axis=0)
            dv_acc[:n_keys, :] += _dot_tn(jnp.concatenate(ps, axis=0), dos)
            dk_acc[:n_keys, :] += _dot_tn(dsb, qs)
            dc_acc[:, :n_keys] -= jnp.concatenate(dcs, axis=0)
            dq_ref[rows, :] += _unstack_heads(jnp.dot(dsb, kb, preferred_element_type=F32))
            dcq_ref[rows, :] += _unstack_heads(jnp.broadcast_to(jnp.concatenate(rowsums, axis=0), (2 * ROW_TILE, LANES)))

        first_q = per_kv * jj
        for r in range(per_kv):
            @pl.when(first_q + r < nb)
            def _():
                block(first_q + r, (r + 1) * ROW_TILE, True)

        def rest(i, carry):
            block(i, KV_TILE_BWD, False)
            return carry

        lax.fori_loop(jnp.minimum(first_q + per_kv, nb), nb, rest, 0)
        dk_ref[...] = dk_acc[...].astype(BF16)
        dv_ref[...] = dv_acc[...].astype(BF16)
        dc_ref[0, 0] = dc_acc[...]

    whole = lambda rows: pl.BlockSpec((rows, LANES), lambda hp, jj: (0, hp))
    kv_blk = lambda off: pl.BlockSpec((KV_TILE_BWD, LANES), lambda hp, jj: (jj, off + hp))
    return pl.pallas_call(
        body, name="attn_bwd", grid=(n_pairs, nkb),
        in_specs=[whole(lk), kv_blk(n_pairs), kv_blk(2 * n_pairs), whole(lp), whole(lp), whole(lp),
                  pl.BlockSpec((nb, 1, 2, KV_TILE_BWD), lambda hp, jj: (0, hp, 0, jj))],
        out_specs=[whole(lp), kv_blk(0), kv_blk(0),
                   pl.BlockSpec((1, 1, 2, KV_TILE_BWD), lambda hp, jj: (hp, jj, 0, 0)), whole(lp)],
        out_shape=[jax.ShapeDtypeStruct((lp, ATTN_WIDTH), F32), jax.ShapeDtypeStruct((lk, ATTN_WIDTH), BF16),
                   jax.ShapeDtypeStruct((lk, ATTN_WIDTH), BF16),
                   jax.ShapeDtypeStruct((n_pairs, nkb, 2, KV_TILE_BWD), F32),
                   jax.ShapeDtypeStruct((lp, ATTN_WIDTH), F32)],
        scratch_shapes=[pltpu.VMEM((KV_TILE_BWD, LANES), F32), pltpu.VMEM((KV_TILE_BWD, LANES), F32),
                        pltpu.VMEM((2, KV_TILE_BWD), F32)],
        compiler_params=_params(("parallel", "arbitrary")),
    )(qkv, qkv, qkv, do, lse, delta, bias)


def _gates_bwd(dc, dcq, f, bfg):
    nb = f.shape[0] // ROW_TILE

    def body(dc_ref, dcq_ref, f_ref, b_ref, df_ref, db_ref, carry):
        step = pl.program_id(0)
        i = nb - 1 - step

        @pl.when(step == 0)
        def _():
            carry[...] = jnp.zeros_like(carry)
            db_ref[...] = jnp.zeros_like(db_ref)

        dcb = dc_ref[...]
        lane = lax.broadcasted_iota(jnp.int32, (ROW_TILE, F_COLS), 1)
        for h in range(N_HEADS):
            dcb = dcb + jnp.where(lane == h, dcq_ref[:, HEAD_DIM * h:HEAD_DIM * h + 1], 0.0)
        r_i = lax.broadcasted_iota(jnp.int32, (ROW_TILE, ROW_TILE), 0)
        c_i = lax.broadcasted_iota(jnp.int32, (ROW_TILE, ROW_TILE), 1)
        upper = (c_i >= r_i).astype(F32)
        dlf = jnp.dot(upper, dcb, precision=HIGHEST, preferred_element_type=F32) + carry[...]
        carry[...] = carry[...] + jnp.sum(dcb, axis=0, keepdims=True)
        logit = f_ref[...] + b_ref[...]
        dlogit = jnp.where(_valid_gate_mask(i), dlf * _sigmoid(-logit), 0.0)
        df_ref[...] = dlogit.astype(BF16)
        db_ref[...] += jnp.sum(dlogit, axis=0, keepdims=True)

    blk = pl.BlockSpec((ROW_TILE, F_COLS), lambda s: (nb - 1 - s, 0))
    wide = pl.BlockSpec((ROW_TILE, ATTN_WIDTH), lambda s: (nb - 1 - s, 0))
    one = pl.BlockSpec((1, F_COLS), lambda s: (0, 0))
    return pl.pallas_call(
        body, name="gates_bwd", grid=(nb,),
        in_specs=[blk, wide, blk, one], out_specs=[blk, one],
        out_shape=[jax.ShapeDtypeStruct(f.shape, BF16), jax.ShapeDtypeStruct((1, F_COLS), F32)],
        scratch_shapes=[pltpu.VMEM((1, F_COLS), F32)],
        compiler_params=_params(("arbitrary",)),
    )(dc, dcq, f, bfg)


def _input_bwd(dproj, df, wt_e, wt_qkv, wt_f, x2, metapad, dh1, g1, hs, sc=None):
    nb = x2.shape[0] // ROW_TILE + 1
    n = len(hs)
    sc_in, sc_shapes, sc_sems = _scatter_operands(hs, sc)
    nq = len(sc_in)

    def body(dp_ref, df_ref, we_ref, w_ref, wf_ref, x_ref, mp_ref, dh_ref, g_ref, *rest):
        h_refs, s_ref = rest[:n], (rest[n] if sc is not None else None)
        gx_ref, g0_ref, dg_ref = rest[nq:nq + 3]
        q_refs, sq_ref = rest[nq + 3:nq + 3 + n], (rest[nq + 3 + n] if sc is not None else None)
        sems = rest[2 * nq + 3:]
        i = pl.program_id(0)

        @pl.when(i == 0)
        def _():
            dg_ref[...] = jnp.zeros_like(dg_ref)
            own, sends, _ = _chip_scatter_plan(h_refs, s_ref, q_refs, sq_ref, *sems)
            for cp in own + sends:
                cp.start()

        dhn = (jnp.dot(dp_ref[:, :E_ZA], we_ref[:E_ZA, :], preferred_element_type=F32)
               + jnp.dot(dp_ref[:, E_ZA:MAIN_COLS // 2], w_ref[...], preferred_element_type=F32)
               + jnp.dot(dp_ref[:, MAIN_COLS // 2:], we_ref[E_ZA:, :], preferred_element_type=F32)
               + jnp.dot(df_ref[...], wf_ref[...], preferred_element_type=F32))
        h0 = jnp.where(i == 0, mp_ref[...], x_ref[...])
        r = lax.rsqrt(jnp.mean(h0 * h0, axis=-1, keepdims=True) + RMS_EPS)
        xhat = h0 * r
        dg_ref[...] += jnp.sum(dhn * xhat, axis=0, keepdims=True)
        dxhat = dhn * g_ref[...]
        dh0 = dh_ref[...] + r * (dxhat - xhat * jnp.mean(dxhat * xhat, axis=-1, keepdims=True))
        gx_ref[...] = dh0

        @pl.when(i == 0)
        def _():
            g0_ref[...] = dh0

        @pl.when(i == nb - 1)
        def _():
            own, sends, recvs = _chip_scatter_plan(h_refs, s_ref, q_refs, sq_ref, *sems)
            for cp in recvs:
                cp.wait_recv()
            for cp in sends:
                cp.wait_send()
            for cp in own:
                cp.wait()

    const = lambda shape: pl.BlockSpec(shape, lambda i: (0, 0))
    res = pl.pallas_call(
        body, name="input_bwd", grid=(nb,),
        in_specs=[pl.BlockSpec((ROW_TILE, MAIN_COLS), lambda i: (i, 0)), pl.BlockSpec((ROW_TILE, F_COLS), lambda i: (i, 0)),
                  const((E_COLS, D_MODEL)), const((QKV_COLS, D_MODEL)), const((F_COLS, D_MODEL)),
                  _tokens_spec(), const((ROW_TILE, D_MODEL)),
                  pl.BlockSpec((ROW_TILE, D_MODEL), lambda i: (i, 0)), const((1, D_MODEL))] + [HBM] * nq,
        out_specs=[_tokens_spec(), const((ROW_TILE, D_MODEL)), const((1, D_MODEL))] + [HBM] * nq,
        out_shape=[jax.ShapeDtypeStruct(x2.shape, F32), jax.ShapeDtypeStruct((ROW_TILE, D_MODEL), F32),
                   jax.ShapeDtypeStruct((1, D_MODEL), F32)] + sc_shapes,
        scratch_shapes=sc_sems,
        compiler_params=_params(("arbitrary",), vmem=56 * 1024 * 1024),
    )(dproj, df, wt_e, wt_qkv, wt_f, x2, metapad, dh1, g1, *_in_hbm(*sc_in))
    return res[:3], res[3:3 + n], (res[3 + n] if sc is not None else None)


def _adamw(w, g, m, v, name):
    rows, cols = w.shape
    if rows % 8 == 0:
        tr, tc = _row_tile8(rows), cols
    else:
        tr, tc = rows, (2 * LANES if cols % (2 * LANES) == 0 and rows > 8 else cols)

    def body(w_ref, g_ref, m_ref, v_ref, d_ref, mo_ref, vo_ref):
        g_ = g_ref[...]
        m_new = ADAM_B1 * m_ref[...] + (1.0 - ADAM_B1) * g_
        v_new = ADAM_B2 * v_ref[...] + (1.0 - ADAM_B2) * (g_ * g_)
        m_hat = m_new / (1.0 - ADAM_B1 ** ADAM_STEP)
        v_hat = v_new / (1.0 - ADAM_B2 ** ADAM_STEP)
        d_ref[...] = -ADAM_LR * (m_hat / (jnp.sqrt(v_hat) + ADAM_EPS) + ADAM_WD * w_ref[...])
        mo_ref[...] = m_new
        vo_ref[...] = v_new

    blk = pl.BlockSpec((tr, tc), lambda i, j: (i, j))
    return pl.pallas_call(
        body, name=name, grid=(rows // tr, cols // tc),
        in_specs=[blk] * 4, out_specs=[blk] * 3,
        out_shape=[jax.ShapeDtypeStruct(w.shape, F32)] * 3,
        compiler_params=_params(("parallel", "parallel")),
    )(w, g, m, v)


def _adamw_native(w3, g3, m3, v3, name):
    rows = w3.shape[0]
    tr = rows // 2
    blk = pl.BlockSpec((tr,) + w3.shape[1:], lambda i: (i, 0, 0))
    shape = jax.ShapeDtypeStruct(w3.shape, F32)

    def moments(g_ref, m_ref, v_ref, mo_ref, vo_ref):
        g_ = g_ref[...]
        mo_ref[...] = ADAM_B1 * m_ref[...] + (1.0 - ADAM_B1) * g_
        vo_ref[...] = ADAM_B2 * v_ref[...] + (1.0 - ADAM_B2) * (g_ * g_)

    new_m, new_v = pl.pallas_call(
        moments, name=name + "_moments", grid=(2,), in_specs=[blk] * 3, out_specs=[blk] * 2, out_shape=[shape] * 2,
        compiler_params=_params(("parallel",)),
    )(g3, m3, v3)

    def delta(w_ref, m_ref, v_ref, d_ref):
        m_hat = m_ref[...] / (1.0 - ADAM_B1 ** ADAM_STEP)
        v_hat = v_ref[...] / (1.0 - ADAM_B2 ** ADAM_STEP)
        d_ref[...] = -ADAM_LR * (m_hat / (jnp.sqrt(v_hat) + ADAM_EPS) + ADAM_WD * w_ref[...])

    d = pl.pallas_call(
        delta, name=name + "_delta", grid=(2,), in_specs=[blk] * 3, out_specs=blk, out_shape=shape,
        compiler_params=_params(("parallel",)),
    )(w3, new_m, new_v)
    return d, new_m, new_v


def _row_tile8(rows):
    best = rows
    for t in range(8, 257, 8):
        if rows % t == 0:
            best = t
    return best


def kernel(x, meta_tokens, norm_g, w_in, b_forget, pool_w, pool_scale, w_up_pool, w_up_attn, w_out, final_norm_g, loss_target, m_meta_tokens, m_norm_g, m_w_in, m_b_forget, m_pool_w, m_pool_scale, m_w_up_pool, m_w_up_attn, m_w_out, m_final_norm_g, v_meta_tokens, v_norm_g, v_w_in, v_b_forget, v_pool_w, v_pool_scale, v_w_up_pool, v_w_up_attn, v_w_out, v_final_norm_g):
    seq = x.shape[1]
    assert seq % ROW_TILE == 0 and x.shape[0] == 1
    lp = seq + ROW_TILE
    nb = lp // ROW_TILE
    lk = -(-lp // KV_TILE_BWD) * KV_TILE_BWD
    core = jnp.reshape(lax.axis_index("c"), (1,)).astype(jnp.int32)
    x2 = x[0]
    target = loss_target[0]
    sh_d = D_MODEL // N_CHIPS

    to_rows = lambda a: jnp.transpose(a, (2, 0, 1))
    from_rows = lambda a: jnp.transpose(a, (1, 2, 0))
    gf = final_norm_g.reshape(1, D_MODEL)
    bfg = jnp.pad(b_forget, ((0, 0), (0, F_COLS - N_HEADS)))
    pw_b = pool_w[0].astype(BF16)
    hn, (wg_in, meta_g) = _rmsnorm_fwd(x2, norm_g, lk, [jnp.transpose(w_in[0]).astype(BF16), meta_tokens], [1, 0])
    wt_e, wt_qkv, wt_f = _weight_sections(wg_in)
    meta_full = jnp.transpose(meta_g, (1, 0, 2)).reshape(N_META, D_MODEL)
    metapad = jnp.pad(meta_full, ((PAD_ROWS, 0), (0, 0)))

    tm = _row_tile(lp, 2200)
    e, (wg_up_p, wg_up_a, wg_out) = _mm_nt(
        hn, wt_e, BF16, "in_proj_gates", lp, E_COLS, tm, 512,
        gather=([w_up_pool[0].astype(BF16), w_up_attn[0].astype(BF16), w_out[0].astype(BF16)], [0, 0, 0]))
    wup_p = jnp.transpose(wg_up_p, (1, 0, 2)).reshape(POOL_WIDTH, D_MODEL)
    wup_a = jnp.transpose(wg_up_a, (1, 0, 2)).reshape(ATTN_WIDTH, D_MODEL)
    wout = wg_out.reshape(D_MODEL, D_MODEL)
    qkv = _mm_nt(hn, wt_qkv, BF16, "in_proj_qkv", lk, QKV_COLS, _row_tile(lk, 2600), 512, scale_first=HEAD_DIM ** -0.5)
    f = _mm_nt(hn, wt_f, F32, "in_proj_forget", lp, F_COLS, tm, F_COLS)
    c = _gates_fwd(f, bfg)
    c_t = jnp.transpose(c[:, :N_HEADS])
    c_first = jnp.transpose(c_t[:, ::ROW_TILE]).reshape(nb, N_HEADS // 2, 2, 1)
    c_keys = jnp.pad(c_t, ((0, 0), (0, lk - lp))).reshape(1, N_HEADS // 2, 2, lk)
    bias = jnp.where(jnp.arange(lk) < PAD_ROWS, -NEG, c_keys - c_first)
    y_pool = _pool_fwd(e, pw_b, pool_scale)
    o, lse = _attn_fwd(qkv, bias, nb)
    merged, y_attn = _merge_fwd(y_pool, o, e, wup_p, wup_a)
    dh1, loss_part, dgf = _head_fwd_bwd(merged, wout, x2, metapad, gf, target)

    dap, daa, dproj_half, do, delta, dy_pool = _merge_bwd(dh1, wout, y_pool, y_attn, wup_p, wup_a, e, o)
    dpc, dzp, dscale, dpw = _pool_bwd_local(e, dy_pool, pw_b, pool_scale)
    du = _pool_bwd_window(dpc)
    dq, dk, dv, dc4, dcq = _attn_bwd(qkv, do, lse, delta, bias, nb)
    dc = jnp.transpose(dc4, (1, 3, 0, 2)).reshape(lk, N_HEADS)[:lp]
    df, db = _gates_bwd(jnp.pad(dc, ((0, 0), (0, F_COLS - N_HEADS))), dcq, f, bfg)
    dproj = _fill_dproj(dproj_half, du, dzp, dq, dk, dv)
    tk = _row_tile(lp, 1100)
    dw_out = _mm_tn(merged, dh1, "grad_w_out", lp, 512, D_MODEL, tk)
    dw_up_p = _mm_tn(y_pool, dap, "grad_w_up_pool", lp, 512, D_MODEL, lp, chunks=N_CHIPS)
    dw_up_a = _mm_tn(y_attn, daa, "grad_w_up_attn", lp, 512, D_MODEL, lp, chunks=N_CHIPS)

    def pad8(a):
        return jnp.pad(a, ((0, (-a.shape[0]) % 8), (0, 0)))

    small_parts = [pad8(a) for a in (dgf.reshape(-1, LANES), dscale.reshape(-1, LANES), db, dpw.reshape(-1, LANES),
                                     loss_part)]
    small = jnp.concatenate(small_parts, axis=0)
    soffs = [0]
    for p in small_parts:
        soffs.append(soffs[-1] + p.shape[0])

    gs_rows = [dw_up_p, dw_up_a, dw_out.reshape(N_CHIPS, sh_d, D_MODEL)]
    dwt_f, (*rb_rows, sr) = _mm_tn(df, hn, "grad_w_in_forget", lp, F_COLS, D_MODEL, tk,
                                   swap=(gs_rows + [small], [1, 1, 1, None]))
    *hs_rows, sc = _add_sibling_half(gs_rows, rb_rows, [1, 1, 1], small, sr, core)
    half = MAIN_COLS // 2
    dwt_a, (rb_f,), (qs_rows, sq) = _mm_tn(dproj, hn, "grad_w_in_a", lp, half // 4, D_MODEL, lp, m=half, m_off=0,
                                           swap=([dwt_f], [1]), scatter=(hs_rows, sc))
    dwt_b, (rb_a,) = _mm_tn(dproj, hn, "grad_w_in_b", lp, half // 4, D_MODEL, lp, m=half, m_off=4, swap=([dwt_a], [1]))
    (rb_b,) = _sibling_exchange([dwt_b], [1])
    order = [(0, 0, half), (1, 0, ATTN_WIDTH), (2, 0, N_HEADS), (1, ATTN_WIDTH, half)]
    h_in = _add_halves_w_in([dwt_a, dwt_b, dwt_f], [rb_a, rb_b, rb_f], order, core)
    grad_axes = [2, 1, 1, 1]
    (grad_x, g_block0, dg1), (q_in,), _ = _input_bwd(dproj, df, wt_e, wt_qkv, wt_f, x2, metapad, dh1, norm_g, [h_in])
    g_w_in_rows, g_w_up_p, g_w_up_a, g_w_out, st = _reduce_allgather([q_in] + list(qs_rows), grad_axes, sq,
                                                                       [True, False, False, False])
    late = _small_allreduce(jnp.concatenate([pad8(dg1.reshape(-1, LANES)), g_block0[PAD_ROWS:].reshape(-1, LANES)], axis=0))
    n_norm = D_MODEL // LANES
    g_norm = late[:n_norm].reshape(1, D_MODEL)
    chip = 2 * lax.axis_index("x") + lax.axis_index("y")
    g_meta = lax.dynamic_slice_in_dim(late[n_norm:].reshape(N_META, D_MODEL), chip * sh_d, sh_d, axis=1)

    spiece = lambda k, rows: st[soffs[k]:soffs[k] + rows]
    g_final = spiece(0, D_MODEL // LANES).reshape(1, D_MODEL)
    g_scale = spiece(1, POOL_WIDTH // LANES).reshape(1, POOL_WIDTH)
    g_bf = spiece(2, 1)
    g_pw = spiece(3, POOL_WIDTH)
    loss = st[soffs[4], 0]

    def pad_lanes(a):
        return jnp.pad(a, ((0, 0), (0, F_COLS - N_HEADS)))

    w_in_res = (g_w_in_rows,) + _adamw_native(to_rows(w_in), g_w_in_rows, to_rows(m_w_in), to_rows(v_w_in), "adamw_w_in")

    upd = [
        ("meta_tokens", meta_tokens, g_meta, m_meta_tokens, v_meta_tokens),
        ("norm_g", norm_g, g_norm, m_norm_g, v_norm_g),
        ("w_in", None, None, None, None),
        ("b_forget", pad_lanes(b_forget), g_bf, pad_lanes(m_b_forget), pad_lanes(v_b_forget)),
        ("pool_w", pool_w.reshape(-1, LANES), g_pw, m_pool_w.reshape(-1, LANES), v_pool_w.reshape(-1, LANES)),
        ("pool_scale", pool_scale, g_scale, m_pool_scale, v_pool_scale),
        ("w_up_pool", w_up_pool[0], g_w_up_p, m_w_up_pool[0], v_w_up_pool[0]),
        ("w_up_attn", w_up_attn[0], g_w_up_a, m_w_up_attn[0], v_w_up_attn[0]),
        ("w_out", w_out[0], g_w_out, m_w_out[0], v_w_out[0]),
        ("final_norm_g", gf, g_final, m_final_norm_g.reshape(1, D_MODEL), v_final_norm_g.reshape(1, D_MODEL)),
    ]
    shapes = [meta_tokens.shape, norm_g.shape, w_in.shape, b_forget.shape, pool_w.shape, pool_scale.shape,
              w_up_pool.shape, w_up_attn.shape, w_out.shape, final_norm_g.shape]
    grads, deltas, new_ms, new_vs = [], [], [], []
    for (name, w_, g_, m_in, v_in), shp in zip(upd, shapes):
        if name == "w_in":
            res = tuple(from_rows(a) for a in w_in_res)
        else:
            d_, mn_, vn_ = _adamw(w_, g_, m_in, v_in, "adamw_" + name)
            res = (g_, d_, mn_, vn_)
        if name == "b_forget":
            res = tuple(a[:, :N_HEADS] for a in res)
        for lst, a in zip((grads, deltas, new_ms, new_vs), res):
            lst.append(a.reshape(shp))

    return (loss, grad_x.reshape(x.shape), *grads, *deltas, *new_ms, *new_vs)
```

```python
import jax
import jax.numpy as jnp
from jax import lax
from jax.experimental import pallas as pl
from jax.experimental.pallas import tpu as pltpu

F32 = jnp.float32
BF16 = jnp.bfloat16
MESH = pl.DeviceIdType.MESH
HIGHEST = lax.Precision.HIGHEST
HBM = pl.BlockSpec(memory_space=pltpu.HBM)

D_MODEL = 1024
N_META = 16
POOL_WIDTH = 512
POOL_GROUP = 128
POOL_WINDOWS = (2, 4, 8, 16)
N_HEADS = 8
HEAD_DIM = 64
ATTN_WIDTH = 512
RMS_EPS = 1e-6
N_CHIPS = 4

ADAM_LR = 0.001
ADAM_B1 = 0.9
ADAM_B2 = 0.999
ADAM_EPS = 1e-08
ADAM_WD = 0.01
ADAM_STEP = 10

LANES = 128
ROW_TILE = 256
KV_TILE = 1024
KV_TILE_BWD = 1024
PAD_ROWS = ROW_TILE - N_META
NEG = -1e30

E_U, E_ZP, E_ZA, E_GP, E_GA, E_COLS = 0, 512, 1024, 1536, 2560, 3584
QKV_COLS = 3 * ATTN_WIDTH
MAIN_COLS = E_COLS + QKV_COLS
F_COLS = LANES
REF_U, REF_Q, REF_ZA, REF_F, REF_GP = 0, 1024, 2560, 3072, 3080
VMEM_LIMIT = 48 * 1024 * 1024


def _params(sem=None, vmem=VMEM_LIMIT):
    return pltpu.CompilerParams(dimension_semantics=sem, vmem_limit_bytes=vmem)


def _in_hbm(*arrays):
    return [pltpu.with_memory_space_constraint(a, pltpu.HBM) for a in arrays]


def _row_tile(n, target):
    best = 16
    for t in range(16, target + 1, 16):
        if n % t == 0:
            best = t
    return best


def _sigmoid(x):
    return 1.0 / (1.0 + jnp.exp(-x))


def _half(ref, axis, which):
    n = ref.shape[axis] // 2
    idx = [slice(None)] * len(ref.shape)
    idx[axis] = pl.ds(pl.multiple_of(which * n, n), n)
    return ref.at[tuple(idx)]


def _half_shape(shape, axis):
    s = list(shape)
    s[axis] //= 2
    return tuple(s)


def _gather_start(ins, outs, axes, send, recv, fsend, frecv, local):
    x, y, c = lax.axis_index("x"), lax.axis_index("y"), lax.axis_index("c")
    me = 2 * x + y
    for t in range(len(ins)):
        pltpu.make_async_copy(ins[t], outs[t].at[me], local.at[t]).start()
        for k, chip in enumerate([(1 - x, y), (x, 1 - y), (1 - x, 1 - y)]):
            pltpu.make_async_remote_copy(
                src_ref=_half(ins[t], axes[t], c), dst_ref=_half(outs[t].at[me], axes[t], c),
                send_sem=send.at[3 * t + k], recv_sem=recv.at[3 * t + k], device_id=(*chip, c), device_id_type=MESH).start()


def _gather_finish(ins, outs, axes, send, recv, fsend, frecv, local):
    x, y, c = lax.axis_index("x"), lax.axis_index("y"), lax.axis_index("c")
    me = 2 * x + y
    sibling = (x, y, 1 - c)
    chips = [(1 - x, y), (x, 1 - y), (1 - x, 1 - y)]
    n = len(ins)

    def over_ici(t, k, chip, dst_slot):
        return pltpu.make_async_remote_copy(
            src_ref=_half(ins[t], axes[t], c), dst_ref=_half(outs[t].at[dst_slot], axes[t], c),
            send_sem=send.at[3 * t + k], recv_sem=recv.at[3 * t + k], device_id=(*chip, c), device_id_type=MESH)

    def to_sibling(t, k, slot, which):
        return pltpu.make_async_remote_copy(
            src_ref=_half(outs[t].at[slot], axes[t], which), dst_ref=_half(outs[t].at[slot], axes[t], which),
            send_sem=fsend.at[3 * t + k], recv_sem=frecv.at[3 * t + k], device_id=sibling, device_id_type=MESH)

    forwards = []
    for t in range(n):
        for k, (px, py) in enumerate(chips):
            over_ici(t, k, (px, py), 2 * px + py).wait_recv()
            fw = to_sibling(t, k, 2 * px + py, c)
            fw.start()
            forwards.append(fw)
    for t in range(n):
        for k, (px, py) in enumerate(chips):
            to_sibling(t, k, 2 * px + py, 1 - c).wait_recv()
    for t in range(n):
        for k, chip in enumerate(chips):
            over_ici(t, k, chip, me).wait_send()
    for fw in forwards:
        fw.wait_send()
    for t in range(n):
        pltpu.make_async_copy(ins[t], outs[t].at[me], local.at[t]).wait()


def _gather_sems(n):
    return [pltpu.SemaphoreType.DMA((3 * n,)), pltpu.SemaphoreType.DMA((3 * n,)), pltpu.SemaphoreType.DMA((3 * n,)),
            pltpu.SemaphoreType.DMA((3 * n,)), pltpu.SemaphoreType.DMA((n,))]


def _gathered_shapes(shards):
    return [jax.ShapeDtypeStruct((N_CHIPS,) + s.shape, s.dtype) for s in shards]


def _swap_copies(srcs, dsts, axes, send, recv):
    x, y, c = lax.axis_index("x"), lax.axis_index("y"), lax.axis_index("c")
    return [pltpu.make_async_remote_copy(
        src_ref=srcs[t] if axes[t] is None else _half(srcs[t], axes[t], 1 - c), dst_ref=dsts[t],
        send_sem=send.at[t], recv_sem=recv.at[t], device_id=(x, y, 1 - c), device_id_type=MESH) for t in range(len(srcs))]


def _swap_shapes(srcs, axes):
    return [jax.ShapeDtypeStruct(g.shape if a is None else _half_shape(g.shape, a), g.dtype) for g, a in zip(srcs, axes)]


def _sibling_exchange(gs, axes):
    n = len(gs)

    def body(*refs):
        cps = _swap_copies(refs[:n], refs[n:2 * n], axes, *refs[2 * n:])
        for cp in cps:
            cp.start()
        for cp in cps:
            cp.wait()

    return pl.pallas_call(
        body, name="grad_sibling_exchange",
        in_specs=[HBM] * n, out_specs=[HBM] * n, out_shape=_swap_shapes(gs, axes),
        scratch_shapes=[pltpu.SemaphoreType.DMA((n,)), pltpu.SemaphoreType.DMA((n,))],
    )(*_in_hbm(*gs))


def _add_halves_w_in(parts, rbs, core):
    na, nb_rows, w = rbs[0].shape[0], rbs[1].shape[0], rbs[0].shape[1]
    per = (na + nb_rows + N_HEADS) // N_CHIPS
    pieces = [[] for _ in range(N_CHIPS)]
    r = 0
    for src, lo, hi in ((0, 0, na), (1, 0, ATTN_WIDTH), (2, 0, N_HEADS), (1, ATTN_WIDTH, nb_rows)):
        while lo < hi:
            c, at = divmod(r, per)
            n = min(hi - lo, per - at)
            pieces[c].append((src, lo, n, at))
            lo, r = lo + n, r + n

    def body(core_ref, a_ref, b_ref, f_ref, ra_ref, rb_ref, rf_ref, o_ref, am, bm, fm, ar, br, fr, rows_buf, sems):
        cols = pl.ds(pl.multiple_of(core_ref[0] * w, w), w)
        few = pl.ds(0, N_HEADS)
        loads = [(pltpu.make_async_copy(a_ref.at[:, cols], am, sems.at[0]), pltpu.make_async_copy(ra_ref, ar, sems.at[1])),
                 (pltpu.make_async_copy(b_ref.at[:, cols], bm, sems.at[2]), pltpu.make_async_copy(rb_ref, br, sems.at[3])),
                 (pltpu.make_async_copy(f_ref.at[few, cols], fm, sems.at[4]),
                  pltpu.make_async_copy(rf_ref.at[few, :], fr, sems.at[5]))]
        for pair in loads:
            for cp in pair:
                cp.start()
        bufs = [(am, ar), (bm, br), (fm, fr)]
        there = set()
        for c in range(N_CHIPS):
            for src, lo, n, at in pieces[c]:
                if src not in there:
                    for cp in loads[src]:
                        cp.wait()
                    there.add(src)
                mine, other = bufs[src]
                rows_buf[at:at + n, :] = mine[lo:lo + n, :] + other[lo:lo + n, :]
            o_ref[c] = rows_buf[0:per, :].astype(BF16)

    any_space = pl.BlockSpec(memory_space=pl.ANY)
    return pl.pallas_call(
        body, name="grad_add_sibling_w_in",
        grid_spec=pltpu.PrefetchScalarGridSpec(
            num_scalar_prefetch=1, grid=(1,),
            in_specs=[any_space] * 6,
            out_specs=pl.BlockSpec((N_CHIPS, per, w), lambda i, cr: (0, 0, 0)),
            scratch_shapes=[pltpu.VMEM((na, w), F32), pltpu.VMEM((nb_rows, w), F32), pltpu.VMEM((N_HEADS, w), F32),
                            pltpu.VMEM((na, w), F32), pltpu.VMEM((nb_rows, w), F32), pltpu.VMEM((N_HEADS, w), F32),
                            pltpu.VMEM((per + (-per) % 8, w), F32), pltpu.SemaphoreType.DMA((6,))]),
        out_shape=jax.ShapeDtypeStruct((N_CHIPS, per, w), BF16),
        compiler_params=_params(("arbitrary",)),
    )(core, *parts, *rbs)


def _weight_sections(wg):
    per, d = wg.shape[1], wg.shape[2]
    n_ref = N_CHIPS * per
    units = [(0, 0, REF_U, REF_Q), (1, 0, REF_Q, REF_ZA), (0, REF_Q - REF_U, REF_ZA, REF_F), (2, 0, REF_F, REF_GP),
             (0, REF_Q - REF_U + REF_F - REF_ZA, REF_GP, n_ref)]
    plan = []
    for out, first, lo, hi in sorted(units, key=lambda u: u[2]):
        pieces, at = [], 0
        while lo < hi:
            c, src = divmod(lo, per)
            n = min(hi - lo, per - src)
            pieces.append((c, src, n, at))
            lo, at = lo + n, at + n
        plan.append((out, first, at, pieces))
    stage_rows = max(rows for _, _, rows, _ in plan)

    def body(g_ref, e_ref, q_ref, f_ref, gb, stage, sems):
        loads = [pltpu.make_async_copy(g_ref.at[c], gb.at[c], sems.at[c]) for c in range(N_CHIPS)]
        for cp in loads:
            cp.start()
        outs = [e_ref, q_ref, f_ref]
        there = set()
        for out, first, rows, pieces in plan:
            for c, _, _, _ in pieces:
                if c not in there:
                    loads[c].wait()
                    there.add(c)
            if len(pieces) == 1 and pieces[0][1] % 16 == 0 and rows % 16 == 0:
                c, src, _, _ = pieces[0]
                outs[out][first:first + rows, :] = gb[c, src:src + rows, :]
                continue
            for c, src, n, at in pieces:
                stage[at:at + n, :] = gb[c, src:src + n, :].astype(F32)
            padded = rows if rows % 16 == 0 else outs[out].shape[0] - first
            if padded > rows:
                stage[rows:padded, :] = jnp.zeros((padded - rows, d), F32)
            outs[out][first:first + padded, :] = stage[0:padded, :].astype(BF16)

    return pl.pallas_call(
        body, name="weight_sections",
        in_specs=[pl.BlockSpec(memory_space=pl.ANY)],
        out_shape=[jax.ShapeDtypeStruct((E_COLS, d), BF16), jax.ShapeDtypeStruct((QKV_COLS, d), BF16),
                   jax.ShapeDtypeStruct((F_COLS, d), BF16)],
        scratch_shapes=[pltpu.VMEM(wg.shape, BF16), pltpu.VMEM((stage_rows, d), F32), pltpu.SemaphoreType.DMA((N_CHIPS,))],
        compiler_params=_params(),
    )(wg)


def _add_sibling_half(gs, rbs, axes, small, sr, core):
    n = len(gs)

    def body(core_ref, *refs):
        g_refs, rb_refs = refs[:n], refs[n:2 * n]
        s_ref, sr_ref = refs[2 * n], refs[2 * n + 1]
        h_refs, sc_ref = refs[2 * n + 2:3 * n + 2], refs[3 * n + 2]
        for t in range(n):
            h_refs[t][...] = (g_refs[t][...] + rb_refs[t][...]).astype(BF16)
        sc_ref[...] = s_ref[...] + sr_ref[...]

    def mine(rb, axis):
        blk = (None,) + rb.shape[1:]
        if axis == 2:
            return pl.BlockSpec(blk, lambda j, cr: (j, 0, cr[0]))
        return pl.BlockSpec(blk, lambda j, cr: (j, cr[0], 0))

    chunk = lambda rb: pl.BlockSpec((None,) + rb.shape[1:], lambda j, cr: (j, 0, 0))
    whole = pl.BlockSpec(small.shape, lambda j, cr: (0, 0))
    return pl.pallas_call(
        body, name="grad_add_sibling",
        grid_spec=pltpu.PrefetchScalarGridSpec(
            num_scalar_prefetch=1, grid=(N_CHIPS,),
            in_specs=[mine(rb, a) for rb, a in zip(rbs, axes)] + [chunk(rb) for rb in rbs] + [whole, whole],
            out_specs=[chunk(rb) for rb in rbs] + [whole]),
        out_shape=[jax.ShapeDtypeStruct(rb.shape, BF16) for rb in rbs] + [jax.ShapeDtypeStruct(small.shape, F32)],
        compiler_params=_params(("arbitrary",)),
    )(core, *gs, *rbs, small, sr)


def _chip_scatter_plan(h_refs, s_ref, q_refs, sq_ref, send, recv, local):
    n = len(h_refs)
    nt = n + (s_ref is not None)
    x, y, c = lax.axis_index("x"), lax.axis_index("y"), lax.axis_index("c")
    me = 2 * x + y
    chips = [(1 - x, y), (x, 1 - y), (1 - x, 1 - y)]

    def copy(t, k, chip, src_slot, dst_slot):
        src = h_refs[t].at[src_slot] if t < n else s_ref
        dst = (q_refs[t] if t < n else sq_ref).at[dst_slot]
        return pltpu.make_async_remote_copy(src_ref=src, dst_ref=dst, send_sem=send.at[3 * t + k],
                                            recv_sem=recv.at[3 * t + k], device_id=(*chip, c), device_id_type=MESH)

    own = [pltpu.make_async_copy(h_refs[t].at[me], q_refs[t].at[me], local.at[t]) for t in range(n)]
    if s_ref is not None:
        own.append(pltpu.make_async_copy(s_ref, sq_ref.at[me], local.at[n]))
    sends = [copy(t, k, (px, py), 2 * px + py, me) for t in range(nt) for k, (px, py) in enumerate(chips)]
    recvs = [copy(t, k, (px, py), me, 2 * px + py) for t in range(nt) for k, (px, py) in enumerate(chips)]
    return own, sends, recvs


def _scatter_operands(hs, sc):
    small = [] if sc is None else [sc]
    nt = len(hs) + len(small)
    shapes = [jax.ShapeDtypeStruct(h.shape, h.dtype) for h in hs] + [
        jax.ShapeDtypeStruct((N_CHIPS,) + a.shape, a.dtype) for a in small]
    sems = [pltpu.SemaphoreType.DMA((3 * nt,)), pltpu.SemaphoreType.DMA((3 * nt,)), pltpu.SemaphoreType.DMA((nt,))]
    return list(hs) + small, shapes, sems


def _reduce_allgather(qs, axes, sq, as_rows, late):
    n = len(qs)
    shard_shapes, half_axes = [], []
    for q, a, rows_form in zip(qs, axes, as_rows):
        shape = [d * 2 if i == a - 1 else d for i, d in enumerate(q.shape[1:])]
        if rows_form:
            assert a == 2
            shape = [shape[0], 1, shape[1]]
        shard_shapes.append(tuple(shape))
        half_axes.append(2 if rows_form else a - 1)

    def body(*refs):
        q_refs, sq_ref, late_ref = refs[:n], refs[n], refs[n + 1]
        o_refs, st_ref, late_sum_ref = refs[n + 2:2 * n + 2], refs[2 * n + 2], refs[2 * n + 3]
        sib_buf, chip_buf, send, recv, late_send, late_recv = refs[2 * n + 4:]
        x, y, c = lax.axis_index("x"), lax.axis_index("y"), lax.axis_index("c")
        me = 2 * x + y
        chips = [(1 - x, y), (x, 1 - y), (1 - x, 1 - y)]

        def swap(t, which):
            return pltpu.make_async_remote_copy(
                src_ref=_half(o_refs[t], half_axes[t], which), dst_ref=_half(o_refs[t], half_axes[t], which),
                send_sem=send.at[t], recv_sem=recv.at[t], device_id=(x, y, 1 - c), device_id_type=MESH)

        def late_copy(k, chip, slot):
            return pltpu.make_async_remote_copy(src_ref=chip_buf.at[slot], dst_ref=chip_buf.at[slot], send_sem=late_send.at[1 + k],
                                                recv_sem=late_recv.at[1 + k], device_id=(*chip, c), device_id_type=MESH)

        late_swap = pltpu.make_async_remote_copy(src_ref=late_ref, dst_ref=sib_buf, send_sem=late_send.at[0],
                                                 recv_sem=late_recv.at[0], device_id=(x, y, 1 - c), device_id_type=MESH)
        late_swap.start()
        sent = []
        for t in range(n):
            q = q_refs[t]
            total = ((q[0].astype(F32) + q[1].astype(F32)) + q[2].astype(F32)) + q[3].astype(F32)
            if as_rows[t]:
                total = total.reshape(total.shape[0], 1, total.shape[1])
            _half(o_refs[t], half_axes[t], c)[...] = total
            cp = swap(t, c)
            cp.start()
            sent.append(cp)
            if t == 0:
                late_swap.wait()
                chip_buf[me] = late_ref[...] + sib_buf[...]
                late_sends = [late_copy(k, chip, me) for k, chip in enumerate(chips)]
                for lc in late_sends:
                    lc.start()
        st_ref[...] = ((sq_ref[0] + sq_ref[1]) + sq_ref[2]) + sq_ref[3]
        for t in range(n):
            swap(t, 1 - c).wait_recv()
        for cp in sent:
            cp.wait_send()
        for k, (px, py) in enumerate(chips):
            late_copy(k, (px, py), 2 * px + py).wait_recv()
        for lc in late_sends:
            lc.wait_send()
        late_sum_ref[...] = ((chip_buf[0] + chip_buf[1]) + chip_buf[2]) + chip_buf[3]

    vmem = pl.BlockSpec(memory_space=pltpu.VMEM)
    return pl.pallas_call(
        body, name="grad_reduce_allgather",
        in_specs=[vmem] * (n + 2), out_specs=[vmem] * (n + 2),
        out_shape=[jax.ShapeDtypeStruct(s, F32) for s in shard_shapes] + [jax.ShapeDtypeStruct(sq.shape[1:], F32),
                                                                         jax.ShapeDtypeStruct(late.shape, F32)],
        scratch_shapes=[pltpu.VMEM(late.shape, F32), pltpu.VMEM((N_CHIPS,) + late.shape, F32),
                        pltpu.SemaphoreType.DMA((n,)), pltpu.SemaphoreType.DMA((n,)),
                        pltpu.SemaphoreType.DMA((4,)), pltpu.SemaphoreType.DMA((4,))],
        compiler_params=_params(),
    )(*qs, sq, late)


def _dot_nt(a, b):
    return lax.dot_general(a, b, (((1,), (1,)), ((), ())), preferred_element_type=F32)


def _dot_tn(a, b):
    return lax.dot_general(a, b, (((0,), (0,)), ((), ())), preferred_element_type=F32)


def _mm_nt(a, bt, out_dtype, name, m, n, tm, tn, row_block=0, scale_first=None, gather=None):
    k = a.shape[1]
    shards, axes = gather if gather is not None else ([], [])
    ng = len(shards)
    grid = (m // tm, n // tn)

    def body(a_ref, b_ref, *rest):
        ins, o_ref, outs, sems = rest[:ng], rest[ng], rest[ng + 1:2 * ng + 1], rest[2 * ng + 1:]
        first = (pl.program_id(0) == 0) & (pl.program_id(1) == 0)
        last = (pl.program_id(0) == grid[0] - 1) & (pl.program_id(1) == grid[1] - 1)
        if ng:
            @pl.when(first)
            def _():
                _gather_start(ins, outs, axes, *sems)

        r = _dot_nt(a_ref[...], b_ref[...])
        if scale_first is not None:
            r = r * jnp.where(pl.program_id(1) == 0, scale_first, 1.0)
        o_ref[...] = r.astype(out_dtype)

        if ng:
            @pl.when(last)
            def _():
                _gather_finish(ins, outs, axes, *sems)

    res = pl.pallas_call(
        body, name=name, grid=grid,
        in_specs=[pl.BlockSpec((tm, k), lambda i, j: (i, 0)), pl.BlockSpec((tn, k), lambda i, j: (row_block + j, 0))]
        + [HBM] * ng,
        out_specs=[pl.BlockSpec((tm, tn), lambda i, j: (i, j))] + [HBM] * ng,
        out_shape=[jax.ShapeDtypeStruct((m, n), out_dtype)] + _gathered_shapes(shards),
        scratch_shapes=_gather_sems(ng) if ng else [],
        compiler_params=_params(("arbitrary", "arbitrary") if ng else ("parallel", "parallel")),
    )(a, bt, *_in_hbm(*shards))
    return (res[0], res[1:]) if ng else res[0]


def _mm_tn(a, b, name, k, tm, tn, tk, chunks=1, m=None, m_off=0, swap=None, scatter=None):
    m = a.shape[1] if m is None else m
    n = b.shape[1]
    cw = n // chunks
    srcs, axes = swap if swap is not None else ([], [])
    ns = len(srcs)
    hs, sc = scatter if scatter is not None else ([], None)
    sc_in, sc_shapes, sc_sems = _scatter_operands(hs, sc) if scatter is not None else ([], [], [])
    nh, nq = len(hs), len(sc_in)
    grid = (m // tm, n // tn, k // tk)

    def body(a_ref, b_ref, *rest):
        s_refs, h_refs = rest[:ns], rest[ns:ns + nq]
        o_ref = rest[ns + nq]
        d_refs, q_refs = rest[ns + nq + 1:2 * ns + nq + 1], rest[2 * ns + nq + 1:2 * ns + 2 * nq + 1]
        sems = rest[2 * ns + 2 * nq + 1:]
        swap_sems, scatter_sems = (sems[:2], sems[2:]) if ns else ((), sems)
        ids = [pl.program_id(d) for d in range(3)]

        def scatter_plan():
            small_in = h_refs[nh] if sc is not None else None
            small_out = q_refs[nh] if sc is not None else None
            return _chip_scatter_plan(h_refs[:nh], small_in, q_refs[:nh], small_out, *scatter_sems)

        if ns or nq:
            @pl.when((ids[0] == 0) & (ids[1] == 0) & (ids[2] == 0))
            def _():
                if ns:
                    for cp in _swap_copies(s_refs, d_refs, axes, *swap_sems):
                        cp.start()
                if nq:
                    own, sends, _ = scatter_plan()
                    for cp in own + sends:
                        cp.start()

        @pl.when(ids[2] == 0)
        def _():
            o_ref[...] = jnp.zeros_like(o_ref)
        r = _dot_tn(a_ref[...].astype(BF16), b_ref[...].astype(BF16))
        if chunks > 1:
            for c in range(chunks):
                o_ref[c] += r[:, c * cw:(c + 1) * cw]
        else:
            o_ref[...] += r

        if ns or nq:
            @pl.when((ids[0] == grid[0] - 1) & (ids[1] == grid[1] - 1) & (ids[2] == grid[2] - 1))
            def _():
                if ns:
                    for cp in _swap_copies(s_refs, d_refs, axes, *swap_sems):
                        cp.wait()
                if nq:
                    own, sends, recvs = scatter_plan()
                    for cp in recvs:
                        cp.wait_recv()
                    for cp in sends:
                        cp.wait_send()
                    for cp in own:
                        cp.wait()

    if chunks > 1:
        assert tn == n
        out_spec = pl.BlockSpec((chunks, tm, cw), lambda i, j, kk: (0, i, 0))
        out_shape = jax.ShapeDtypeStruct((chunks, m, cw), F32)
    else:
        out_spec = pl.BlockSpec((tm, tn), lambda i, j, kk: (i, j))
        out_shape = jax.ShapeDtypeStruct((m, n), F32)
    riders = ns + nq
    res = pl.pallas_call(
        body, name=name, grid=grid,
        in_specs=[pl.BlockSpec((tk, tm), lambda i, j, kk: (kk, m_off + i)), pl.BlockSpec((tk, tn), lambda i, j, kk: (kk, j))]
        + [HBM] * riders,
        out_specs=[out_spec] + [HBM] * riders, out_shape=[out_shape] + _swap_shapes(srcs, axes) + sc_shapes,
        scratch_shapes=([pltpu.SemaphoreType.DMA((ns,)), pltpu.SemaphoreType.DMA((ns,))] if ns else []) + sc_sems,
        compiler_params=_params(("arbitrary",) * 3 if riders else ("parallel", "parallel", "arbitrary")),
    )(a, b, *_in_hbm(*srcs, *sc_in))
    if not riders:
        return res[0]
    out = [res[0]]
    if ns:
        out.append(res[1:1 + ns])
    if nq:
        out.append((res[1 + ns:1 + ns + nh], res[1 + ns + nh] if sc is not None else None))
    return tuple(out)


def _tokens_spec():
    return pl.BlockSpec((ROW_TILE, D_MODEL), lambda i: (jnp.maximum(i - 1, 0), 0))


def _rmsnorm_fwd(x2, g1, lk, shards, axes):
    nb = x2.shape[0] // ROW_TILE + 1
    nblk = lk // ROW_TILE
    ng = len(shards)
    meta_cols = shards[-1].shape[1]

    def body(x_ref, g_ref, *rest):
        ins, hn_ref, outs = rest[:ng], rest[ng], rest[ng + 1:2 * ng + 1]
        sems, meta_buf, meta_sem = rest[2 * ng + 1:2 * ng + 6], rest[2 * ng + 6], rest[2 * ng + 7]
        s = pl.program_id(0)
        blk = (s + 1) % nblk

        @pl.when(s == 0)
        def _():
            _gather_start(ins, outs, axes, *sems)

        def normed(h):
            r = lax.rsqrt(jnp.mean(h * h, axis=-1, keepdims=True) + RMS_EPS)
            return ((h * r) * g_ref[...]).astype(BF16)

        @pl.when(s < nblk - 1)
        def _():
            hn_ref[...] = normed(jnp.where(blk >= nb, 0.0, x_ref[...]))

        @pl.when(s == nblk - 1)
        def _():
            _gather_finish(ins, outs, axes, *sems)
            fetch = pltpu.make_async_copy(outs[-1], meta_buf, meta_sem.at[0])
            fetch.start()
            fetch.wait()
            meta = jnp.concatenate([meta_buf[j] for j in range(N_CHIPS)], axis=1)
            hn_ref[...] = normed(jnp.concatenate([jnp.zeros((PAD_ROWS, D_MODEL), F32), meta], axis=0))

    res = pl.pallas_call(
        body, name="rmsnorm_fwd", grid=(nblk,),
        in_specs=[pl.BlockSpec((ROW_TILE, D_MODEL), lambda s: (jnp.clip((s + 1) % nblk - 1, 0, nb - 2), 0)),
                  pl.BlockSpec((1, D_MODEL), lambda s: (0, 0))] + [HBM] * ng,
        out_specs=[pl.BlockSpec((ROW_TILE, D_MODEL), lambda s: ((s + 1) % nblk, 0))] + [HBM] * ng,
        out_shape=[jax.ShapeDtypeStruct((lk, D_MODEL), BF16)] + _gathered_shapes(shards),
        scratch_shapes=_gather_sems(ng) + [pltpu.VMEM((N_CHIPS, N_META, meta_cols), F32), pltpu.SemaphoreType.DMA((1,))],
        compiler_params=_params(("arbitrary",)),
    )(x2, g1, *_in_hbm(*shards))
    return res[0], res[1:]


def _valid_gate_mask(i):
    row = i * ROW_TILE + lax.broadcasted_iota(jnp.int32, (ROW_TILE, F_COLS), 0)
    col = lax.broadcasted_iota(jnp.int32, (ROW_TILE, F_COLS), 1)
    return (row >= PAD_ROWS) & (col < N_HEADS)


def _gates_fwd(f, bfg):
    nb = f.shape[0] // ROW_TILE

    def body(f_ref, b_ref, c_ref, carry):
        i = pl.program_id(0)

        @pl.when(i == 0)
        def _():
            carry[...] = jnp.zeros_like(carry)

        logit = f_ref[...] + b_ref[...]
        lf = jnp.minimum(logit, 0.0) - jnp.log1p(jnp.exp(-jnp.abs(logit)))
        lf = jnp.where(_valid_gate_mask(i), lf, 0.0)
        r_i = lax.broadcasted_iota(jnp.int32, (ROW_TILE, ROW_TILE), 0)
        c_i = lax.broadcasted_iota(jnp.int32, (ROW_TILE, ROW_TILE), 1)
        tri = (c_i <= r_i).astype(F32)
        c_ref[...] = jnp.dot(tri, lf, precision=HIGHEST, preferred_element_type=F32) + carry[...]
        carry[...] = carry[...] + jnp.sum(lf, axis=0, keepdims=True)

    return pl.pallas_call(
        body, name="gates_fwd", grid=(nb,),
        in_specs=[pl.BlockSpec((ROW_TILE, F_COLS), lambda i: (i, 0)), pl.BlockSpec((1, F_COLS), lambda i: (0, 0))],
        out_specs=pl.BlockSpec((ROW_TILE, F_COLS), lambda i: (i, 0)),
        out_shape=jax.ShapeDtypeStruct(f.shape, F32),
        scratch_shapes=[pltpu.VMEM((1, F_COLS), F32)],
        compiler_params=_params(("arbitrary",)),
    )(f, bfg)


def _pool_counts(i):
    row = i * ROW_TILE + lax.broadcasted_iota(jnp.int32, (ROW_TILE, 1), 0)
    return jnp.maximum(row - PAD_ROWS, 0)


def _trailing_sums(xc, levels):
    acc = xc
    for lv in range(levels):
        acc = acc + pltpu.roll(acc, 1 << lv, 0)
    return acc


def _leading_sums(xc, levels):
    n = xc.shape[0]
    acc = xc
    for lv in range(levels):
        acc = acc + pltpu.roll(acc, n - (1 << lv), 0)
    return acc


def _pool_p(u_cur, u_prev, i):
    pos = _pool_counts(i)
    ps, invs = [], []
    for g, w in enumerate(POOL_WINDOWS):
        sl = slice(g * POOL_GROUP, (g + 1) * POOL_GROUP)
        cur = u_cur[:, sl]
        xc = jnp.concatenate([u_prev[:, sl], cur], axis=0)
        win = _trailing_sums(xc, g + 1)[ROW_TILE:, :]
        inv = 1.0 / jnp.minimum(pos + 1, w).astype(F32)
        ps.append(win * inv - cur)
        invs.append(inv)
    return ps, invs


def _pool_fwd(e, pw, scale):
    nb = e.shape[0] // ROW_TILE

    def body(uc_ref, up_ref, z_ref, pw_ref, sc_ref, y_ref):
        i = pl.program_id(0)
        u_cur = uc_ref[...].astype(F32)
        u_prev = jnp.where(i == 0, 0.0, up_ref[...].astype(F32))
        ps, _ = _pool_p(u_cur, u_prev, i)
        z = z_ref[...].astype(F32)
        gate = z * _sigmoid(z)
        for g in range(len(POOL_WINDOWS)):
            sl = slice(g * POOL_GROUP, (g + 1) * POOL_GROUP)
            yraw = jnp.dot(ps[g].astype(BF16), pw_ref[g], preferred_element_type=F32)
            y_ref[:, sl] = ((yraw * sc_ref[:, sl]) * gate[:, sl]).astype(BF16)

    blk = (ROW_TILE, POOL_WIDTH)
    return pl.pallas_call(
        body, name="pool_fwd", grid=(nb,),
        in_specs=[pl.BlockSpec(blk, lambda i: (i, 0)), pl.BlockSpec(blk, lambda i: (jnp.maximum(i - 1, 0), 0)),
                  pl.BlockSpec(blk, lambda i: (i, 1)),
                  pl.BlockSpec((len(POOL_WINDOWS), POOL_GROUP, POOL_GROUP), lambda i: (0, 0, 0)),
                  pl.BlockSpec((1, POOL_WIDTH), lambda i: (0, 0))],
        out_specs=pl.BlockSpec(blk, lambda i: (i, 0)),
        out_shape=jax.ShapeDtypeStruct((e.shape[0], POOL_WIDTH), BF16),
        compiler_params=_params(("parallel",)),
    )(e, e, e, pw, scale)


def _stack_heads(a):
    first = lax.broadcasted_iota(jnp.int32, a.shape, 1) < HEAD_DIM
    zero = jnp.zeros_like(a)
    return jnp.concatenate([jnp.where(first, a, zero), jnp.where(first, zero, a)], axis=0)


def _unstack_heads(a):
    rows = a.shape[0] // 2
    first = lax.broadcasted_iota(jnp.int32, (rows, LANES), 1) < HEAD_DIM
    return jnp.where(first, a[:rows], a[rows:])


def _causal(i, jj, stacked, kv_tile):
    r = lax.broadcasted_iota(jnp.int32, (stacked * ROW_TILE, kv_tile), 0)
    if stacked == 2:
        r = jnp.where(r >= ROW_TILE, r - ROW_TILE, r)
    kidx = jj * kv_tile + lax.broadcasted_iota(jnp.int32, (stacked * ROW_TILE, kv_tile), 1)
    return kidx <= i * ROW_TILE + r


def _stack_rows(b):
    n = b.shape[1]
    return jnp.concatenate([jnp.broadcast_to(b[0:1], (ROW_TILE, n)), jnp.broadcast_to(b[1:2], (ROW_TILE, n))], axis=0)


def _attn_fwd(qkv, bias, nb):
    lk = qkv.shape[0]
    lp = nb * ROW_TILE
    nkb = lk // KV_TILE
    n_pairs = N_HEADS // 2

    def body(q_ref, k_ref, v_ref, b_ref, o_ref, lse_ref):
        i = pl.program_id(1)
        qs = _stack_heads(q_ref[...])
        last = (i * ROW_TILE) // KV_TILE

        def block(jj, carry, masked, n_keys=KV_TILE):
            m, l, acc = carry
            rows = pl.ds(pl.multiple_of(jj * KV_TILE, KV_TILE), n_keys)
            s = _dot_nt(qs, k_ref[rows, :]) - _stack_rows(b_ref[0, 0, :, rows])
            if masked:
                s = jnp.where(_causal(i, jj, 2, KV_TILE)[:, :n_keys], s, NEG)
            m_new = jnp.maximum(m, jnp.max(s, axis=1, keepdims=True))
            alpha = jnp.exp(m - m_new)
            p = jnp.exp(s - m_new)
            l = alpha * l + jnp.sum(p, axis=1, keepdims=True)
            acc = alpha * acc + jnp.dot(p.astype(BF16), v_ref[rows, :], preferred_element_type=F32)
            return m_new, l, acc

        init = (jnp.full((2 * ROW_TILE, 1), NEG, F32), jnp.zeros((2 * ROW_TILE, 1), F32),
                jnp.zeros((2 * ROW_TILE, LANES), F32))
        def finish(carry):
            m, l, acc = carry
            o_ref[...] = _unstack_heads(acc / l)
            lse_ref[...] = _unstack_heads(jnp.broadcast_to(m + jnp.log(l), (2 * ROW_TILE, LANES)))

        carry = lax.fori_loop(0, last, lambda jj, c: block(jj, c, False), init)
        ends = [lambda c, n=(r + 1) * ROW_TILE: finish(block(last, c, True, n)) for r in range(KV_TILE // ROW_TILE)]
        lax.switch(i - last * (KV_TILE // ROW_TILE), ends, carry)

    return pl.pallas_call(
        body, name="attn_fwd", grid=(n_pairs, nb),
        in_specs=[pl.BlockSpec((ROW_TILE, LANES), lambda hp, i: (i, hp)),
                  pl.BlockSpec((lk, LANES), lambda hp, i: (0, n_pairs + hp)),
                  pl.BlockSpec((lk, LANES), lambda hp, i: (0, 2 * n_pairs + hp)),
                  pl.BlockSpec((1, 1, 2, lk), lambda hp, i: (i, hp, 0, 0))],
        out_specs=[pl.BlockSpec((ROW_TILE, LANES), lambda hp, i: (i, hp)),
                   pl.BlockSpec((ROW_TILE, LANES), lambda hp, i: (i, hp))],
        out_shape=[jax.ShapeDtypeStruct((lp, ATTN_WIDTH), F32), jax.ShapeDtypeStruct((lp, ATTN_WIDTH), F32)],
        compiler_params=_params(("parallel", "parallel")),
    )(qkv, qkv, qkv, bias)


def _merge_fwd(y_pool, o, e, wup_p, wup_a):
    lp = o.shape[0]
    nb = lp // ROW_TILE

    def body(yp_ref, o_ref, e_ref, wp_ref, wa_ref, mg_ref, ya_ref):
        za = e_ref[:, E_ZA:E_GP].astype(F32)
        ya = (o_ref[...] * (za * _sigmoid(za))).astype(BF16)
        ya_ref[...] = ya
        a_pool = jnp.dot(yp_ref[...], wp_ref[...], preferred_element_type=F32)
        a_attn = jnp.dot(ya, wa_ref[...], preferred_element_type=F32)
        mg_ref[...] = (_sigmoid(e_ref[:, E_GP:E_GA].astype(F32)) * a_pool
                       + _sigmoid(e_ref[:, E_GA:E_COLS].astype(F32)) * a_attn).astype(BF16)

    return pl.pallas_call(
        body, name="merge_fwd", grid=(nb,),
        in_specs=[pl.BlockSpec((ROW_TILE, POOL_WIDTH), lambda i: (i, 0)),
                  pl.BlockSpec((ROW_TILE, ATTN_WIDTH), lambda i: (i, 0)),
                  pl.BlockSpec((ROW_TILE, E_COLS), lambda i: (i, 0)),
                  pl.BlockSpec((POOL_WIDTH, D_MODEL), lambda i: (0, 0)),
                  pl.BlockSpec((ATTN_WIDTH, D_MODEL), lambda i: (0, 0))],
        out_specs=[pl.BlockSpec((ROW_TILE, D_MODEL), lambda i: (i, 0)),
                   pl.BlockSpec((ROW_TILE, ATTN_WIDTH), lambda i: (i, 0))],
        out_shape=[jax.ShapeDtypeStruct((lp, D_MODEL), BF16), jax.ShapeDtypeStruct((lp, ATTN_WIDTH), BF16)],
        compiler_params=_params(("parallel",)),
    )(y_pool, o, e, wup_p, wup_a)


def _head_fwd_bwd(merged, w_out, x2, metapad, gf, target):
    lp = merged.shape[0]
    nb = lp // ROW_TILE

    def body(mg_ref, w_ref, x_ref, mp_ref, g_ref, t_ref, dh_ref, loss_ref, dg_ref):
        i = pl.program_id(0)

        @pl.when(i == 0)
        def _():
            loss_ref[...] = jnp.zeros_like(loss_ref)
            dg_ref[...] = jnp.zeros_like(dg_ref)

        h0 = jnp.where(i == 0, mp_ref[...], x_ref[...])
        h1 = h0 + jnp.dot(mg_ref[...], w_ref[...], preferred_element_type=F32)
        r = lax.rsqrt(jnp.mean(h1 * h1, axis=-1, keepdims=True) + RMS_EPS)
        xhat = h1 * r
        g = g_ref[...]
        err = jnp.where(i == 0, 0.0, xhat * g - t_ref[...])
        loss_ref[...] += 0.5 * jnp.sum(jnp.mean(err * err, axis=-1, keepdims=True))
        dy = err / D_MODEL
        dg_ref[...] += jnp.sum(dy * xhat, axis=0, keepdims=True)
        dxhat = dy * g
        dh_ref[...] = r * (dxhat - xhat * jnp.mean(dxhat * xhat, axis=-1, keepdims=True))

    return pl.pallas_call(
        body, name="head_fwd_bwd", grid=(nb,),
        in_specs=[pl.BlockSpec((ROW_TILE, D_MODEL), lambda i: (i, 0)),
                  pl.BlockSpec((D_MODEL, D_MODEL), lambda i: (0, 0)),
                  _tokens_spec(), pl.BlockSpec((ROW_TILE, D_MODEL), lambda i: (0, 0)),
                  pl.BlockSpec((1, D_MODEL), lambda i: (0, 0)), _tokens_spec()],
        out_specs=[pl.BlockSpec((ROW_TILE, D_MODEL), lambda i: (i, 0)),
                   pl.BlockSpec((1, LANES), lambda i: (0, 0)), pl.BlockSpec((1, D_MODEL), lambda i: (0, 0))],
        out_shape=[jax.ShapeDtypeStruct((lp, D_MODEL), F32), jax.ShapeDtypeStruct((1, LANES), F32),
                   jax.ShapeDtypeStruct((1, D_MODEL), F32)],
        compiler_params=_params(("arbitrary",)),
    )(merged, w_out, x2, metapad, gf, target)


def _per_head_rowsum(t):
    head = lax.broadcasted_iota(jnp.int32, t.shape, 1) // HEAD_DIM
    out = jnp.zeros_like(t)
    for h in range(N_HEADS):
        sel = head == h
        out = jnp.where(sel, jnp.sum(jnp.where(sel, t, 0.0), axis=1, keepdims=True), out)
    return out


def _merge_bwd(dh1, w_out, y_pool, y_attn, wup_p, wup_a, e, o):
    lp = o.shape[0]
    nb = lp // ROW_TILE

    def body(dh_ref, wo_ref, yp_ref, ya_ref, wp_ref, wa_ref, e_ref, o_ref,
             dap_ref, daa_ref, dg_ref, do_ref, delta_ref, dyp_ref):
        dmerged = _dot_nt(dh_ref[...].astype(BF16), wo_ref[...])
        a_pool = jnp.dot(yp_ref[...], wp_ref[...], preferred_element_type=F32)
        a_attn = jnp.dot(ya_ref[...], wa_ref[...], preferred_element_type=F32)
        sp = _sigmoid(e_ref[:, E_GP:E_GA].astype(F32))
        sa = _sigmoid(e_ref[:, E_GA:E_COLS].astype(F32))
        dap = (dmerged * sp).astype(BF16)
        daa = (dmerged * sa).astype(BF16)
        dap_ref[...] = dap
        daa_ref[...] = daa
        dg_ref[:, ATTN_WIDTH:ATTN_WIDTH + D_MODEL] = (dmerged * a_pool * (sp * (1.0 - sp))).astype(BF16)
        dg_ref[:, ATTN_WIDTH + D_MODEL:] = (dmerged * a_attn * (sa * (1.0 - sa))).astype(BF16)
        dyp_ref[...] = _dot_nt(dap, wp_ref[...])
        dya = _dot_nt(daa, wa_ref[...])
        za = e_ref[:, E_ZA:E_GP].astype(F32)
        sz = _sigmoid(za)
        o = o_ref[...]
        do = dya * (za * sz)
        do_ref[...] = do.astype(BF16)
        dg_ref[:, :ATTN_WIDTH] = (dya * o * (sz * (1.0 + za * (1.0 - sz)))).astype(BF16)
        delta_ref[...] = _per_head_rowsum(do * o)

    row = lambda w: pl.BlockSpec((ROW_TILE, w), lambda i: (i, 0))
    full = lambda a: pl.BlockSpec(a.shape, lambda i: (0, 0))
    return pl.pallas_call(
        body, name="merge_bwd", grid=(nb,),
        in_specs=[row(D_MODEL), full(w_out), row(POOL_WIDTH), row(ATTN_WIDTH), full(wup_p), full(wup_a),
                  row(E_COLS), row(ATTN_WIDTH)],
        out_specs=[row(D_MODEL), row(D_MODEL), pl.BlockSpec((ROW_TILE, MAIN_COLS // 2), lambda i: (i, 1)),
                   row(ATTN_WIDTH), row(ATTN_WIDTH), row(POOL_WIDTH)],
        out_shape=[jax.ShapeDtypeStruct((lp, D_MODEL), BF16), jax.ShapeDtypeStruct((lp, D_MODEL), BF16),
                   jax.ShapeDtypeStruct((lp, MAIN_COLS), BF16),
                   jax.ShapeDtypeStruct((lp, ATTN_WIDTH), BF16), jax.ShapeDtypeStruct((lp, ATTN_WIDTH), F32),
                   jax.ShapeDtypeStruct((lp, POOL_WIDTH), F32)],
        compiler_params=_params(("parallel",)),
    )(dh1, w_out, y_pool, y_attn, wup_p, wup_a, e, o)


def _fill_dproj(dproj_half, du, dzp, dq, dk, dv):
    lp = dproj_half.shape[0]
    nb = lp // ROW_TILE

    def body(base_ref, du_ref, dzp_ref, dq_ref, dk_ref, dv_ref, o_ref):
        parts = [du_ref[...], dzp_ref[...], (dq_ref[...] * HEAD_DIM ** -0.5).astype(BF16), dk_ref[...], dv_ref[...]]
        for t, p in enumerate(parts):
            o_ref[:, t * ATTN_WIDTH:(t + 1) * ATTN_WIDTH] = p

    blk = pl.BlockSpec((ROW_TILE, ATTN_WIDTH), lambda i: (i, 0))
    return pl.pallas_call(
        body, name="fill_dproj", grid=(nb,),
        in_specs=[pl.BlockSpec(memory_space=pl.ANY)] + [blk] * 5,
        out_specs=pl.BlockSpec((ROW_TILE, MAIN_COLS // 2), lambda i: (i, 0)),
        out_shape=jax.ShapeDtypeStruct(dproj_half.shape, BF16),
        input_output_aliases={0: 0},
        compiler_params=_params(("parallel",)),
    )(dproj_half, du, dzp, dq, dk, dv)


def _pool_bwd_local(e, dy_pool, pw, scale):
    lp = e.shape[0]
    nb = lp // ROW_TILE
    ng = len(POOL_WINDOWS)

    def body(uc_ref, up_ref, z_ref, dy_ref, pw_ref, sc_ref, dpc_ref, dz_ref, dsc_ref, dpw_ref):
        i = pl.program_id(0)

        @pl.when(i == 0)
        def _():
            dsc_ref[...] = jnp.zeros_like(dsc_ref)
            dpw_ref[...] = jnp.zeros_like(dpw_ref)

        u_cur = uc_ref[...].astype(F32)
        u_prev = jnp.where(i == 0, 0.0, up_ref[...].astype(F32))
        ps, invs = _pool_p(u_cur, u_prev, i)
        z = z_ref[...].astype(F32)
        sz = _sigmoid(z)
        dy = dy_ref[...]
        dypre = dy * (z * sz)
        dsilu = sz * (1.0 + z * (1.0 - sz))
        for g in range(ng):
            sl = slice(g * POOL_GROUP, (g + 1) * POOL_GROUP)
            pb = ps[g].astype(BF16)
            w = pw_ref[g]
            yraw = jnp.dot(pb, w, preferred_element_type=F32)
            sc = sc_ref[:, sl]
            dz_ref[:, sl] = (dy[:, sl] * (yraw * sc) * dsilu[:, sl]).astype(BF16)
            dsc_ref[:, sl] += jnp.sum(dypre[:, sl] * yraw, axis=0, keepdims=True)
            dyraw = (dypre[:, sl] * sc).astype(BF16)
            dpw_ref[g] += _dot_tn(pb, dyraw)
            dpc_ref[:, sl] = _dot_nt(dyraw, w) * invs[g]

    blk = (ROW_TILE, POOL_WIDTH)
    return pl.pallas_call(
        body, name="pool_bwd_local", grid=(nb,),
        in_specs=[pl.BlockSpec(blk, lambda i: (i, 0)), pl.BlockSpec(blk, lambda i: (jnp.maximum(i - 1, 0), 0)),
                  pl.BlockSpec(blk, lambda i: (i, 1)), pl.BlockSpec(blk, lambda i: (i, 0)),
                  pl.BlockSpec((ng, POOL_GROUP, POOL_GROUP), lambda i: (0, 0, 0)),
                  pl.BlockSpec((1, POOL_WIDTH), lambda i: (0, 0))],
        out_specs=[pl.BlockSpec(blk, lambda i: (i, 0)), pl.BlockSpec(blk, lambda i: (i, 0)),
                   pl.BlockSpec((1, POOL_WIDTH), lambda i: (0, 0)),
                   pl.BlockSpec((ng, POOL_GROUP, POOL_GROUP), lambda i: (0, 0, 0))],
        out_shape=[jax.ShapeDtypeStruct((lp, POOL_WIDTH), F32), jax.ShapeDtypeStruct((lp, POOL_WIDTH), BF16),
                   jax.ShapeDtypeStruct((1, POOL_WIDTH), F32),
                   jax.ShapeDtypeStruct((ng, POOL_GROUP, POOL_GROUP), F32)],
        compiler_params=_params(("arbitrary",)),
    )(e, e, e, dy_pool, pw, scale)


def _pool_bwd_window(dpc):
    lp = dpc.shape[0]
    nb = lp // ROW_TILE

    def body(cur_ref, nxt_ref, du_ref):
        i = pl.program_id(0)
        cur = cur_ref[...]
        nxt = jnp.where(i == nb - 1, 0.0, nxt_ref[...])
        pos = _pool_counts(i)
        for g, w in enumerate(POOL_WINDOWS):
            sl = slice(g * POOL_GROUP, (g + 1) * POOL_GROUP)
            xc = jnp.concatenate([cur[:, sl], nxt[:, sl]], axis=0)
            win = _leading_sums(xc, g + 1)[:ROW_TILE, :]
            dp = cur[:, sl] * jnp.minimum(pos + 1, w).astype(F32)
            du_ref[:, sl] = (win - dp).astype(BF16)

    blk = (ROW_TILE, POOL_WIDTH)
    return pl.pallas_call(
        body, name="pool_bwd_window", grid=(nb,),
        in_specs=[pl.BlockSpec(blk, lambda i: (i, 0)), pl.BlockSpec(blk, lambda i: (jnp.minimum(i + 1, nb - 1), 0))],
        out_specs=pl.BlockSpec(blk, lambda i: (i, 0)),
        out_shape=jax.ShapeDtypeStruct((lp, POOL_WIDTH), BF16),
        compiler_params=_params(("parallel",)),
    )(dpc, dpc)


def _attn_bwd(qkv, do, lse, delta, bias, nb):
    lk = qkv.shape[0]
    lp = nb * ROW_TILE
    nkb = lk // KV_TILE_BWD
    n_pairs = N_HEADS // 2
    per_kv = KV_TILE_BWD // ROW_TILE

    def body(q_ref, k_ref, v_ref, do_ref, lse_ref, dl_ref, b_ref, dq_ref, dk_ref, dv_ref, dc_ref, dcq_ref,
             dk_acc, dv_acc, dc_acc):
        jj = pl.program_id(1)

        @pl.when(jj == 0)
        def _():
            dq_ref[...] = jnp.zeros_like(dq_ref)
            dcq_ref[...] = jnp.zeros_like(dcq_ref)

        dk_acc[...] = jnp.zeros_like(dk_acc)
        dv_acc[...] = jnp.zeros_like(dv_acc)
        dc_acc[...] = jnp.zeros_like(dc_acc)

        def block(i, n_keys, masked):
            kb, vb = k_ref[:n_keys, :], v_ref[:n_keys, :]
            rows = pl.ds(pl.multiple_of(i * ROW_TILE, ROW_TILE), ROW_TILE)
            qs = _stack_heads(q_ref[rows, :])
            dos = _stack_heads(do_ref[rows, :])
            lse_i, dl_i = lse_ref[rows, :], dl_ref[rows, :]
            s = _dot_nt(qs, kb)
            dp = _dot_nt(dos, vb)
            if masked:
                valid = _causal(i, jj, 1, KV_TILE_BWD)[:, :n_keys]
            ps, dss, dcs, rowsums = [], [], [], []
            for hd in range(2):
                half = slice(hd * ROW_TILE, (hd + 1) * ROW_TILE)
                col = slice(hd * HEAD_DIM, hd * HEAD_DIM + 1)
                sh = s[half] - b_ref[i, 0, hd:hd + 1, :n_keys]
                if masked:
                    sh = jnp.where(valid, sh, NEG)
                p = jnp.exp(sh - lse_i[:, col])
                ds = p * (dp[half] - dl_i[:, col])
                ps.append(p.astype(BF16))
                dss.append(ds.astype(BF16))
                dcs.append(jnp.sum(ds, axis=0, keepdims=True))
                rowsums.append(jnp.sum(ds, axis=1, keepdims=True))
            dsb = jnp.concatenate(dss, axis=0)
            dv_acc[:n_keys, :] += _dot_tn(jnp.concatenate(ps, axis=0), dos)
            dk_acc[:n_keys, :] += _dot_tn(dsb, qs)
            dc_acc[:, :n_keys] -= jnp.concatenate(dcs, axis=0)
            dq_ref[rows, :] += _unstack_heads(jnp.dot(dsb, kb, preferred_element_type=F32))
            dcq_ref[rows, :] += _unstack_heads(jnp.broadcast_to(jnp.concatenate(rowsums, axis=0), (2 * ROW_TILE, LANES)))

        first_q = per_kv * jj
        for r in range(per_kv):
            @pl.when(first_q + r < nb)
            def _():
                block(first_q + r, (r + 1) * ROW_TILE, True)

        def rest(i, carry):
            block(i, KV_TILE_BWD, False)
            return carry

        lax.fori_loop(jnp.minimum(first_q + per_kv, nb), nb, rest, 0)
        dk_ref[...] = dk_acc[...].astype(BF16)
        dv_ref[...] = dv_acc[...].astype(BF16)
        dc_ref[0, 0] = dc_acc[...]

    whole = lambda rows: pl.BlockSpec((rows, LANES), lambda hp, jj: (0, hp))
    kv_blk = lambda off: pl.BlockSpec((KV_TILE_BWD, LANES), lambda hp, jj: (jj, off + hp))
    return pl.pallas_call(
        body, name="attn_bwd", grid=(n_pairs, nkb),
        in_specs=[whole(lk), kv_blk(n_pairs), kv_blk(2 * n_pairs), whole(lp), whole(lp), whole(lp),
                  pl.BlockSpec((nb, 1, 2, KV_TILE_BWD), lambda hp, jj: (0, hp, 0, jj))],
        out_specs=[whole(lp), kv_blk(0), kv_blk(0),
                   pl.BlockSpec((1, 1, 2, KV_TILE_BWD), lambda hp, jj: (hp, jj, 0, 0)), whole(lp)],
        out_shape=[jax.ShapeDtypeStruct((lp, ATTN_WIDTH), F32), jax.ShapeDtypeStruct((lk, ATTN_WIDTH), BF16),
                   jax.ShapeDtypeStruct((lk, ATTN_WIDTH), BF16),
                   jax.ShapeDtypeStruct((n_pairs, nkb, 2, KV_TILE_BWD), F32),
                   jax.ShapeDtypeStruct((lp, ATTN_WIDTH), F32)],
        scratch_shapes=[pltpu.VMEM((KV_TILE_BWD, LANES), F32), pltpu.VMEM((KV_TILE_BWD, LANES), F32),
                        pltpu.VMEM((2, KV_TILE_BWD), F32)],
        compiler_params=_params(("parallel", "arbitrary")),
    )(qkv, qkv, qkv, do, lse, delta, bias)


def _gates_bwd(dc, dcq, f, bfg):
    nb = f.shape[0] // ROW_TILE

    def body(dc_ref, dcq_ref, f_ref, b_ref, df_ref, db_ref, carry):
        step = pl.program_id(0)
        i = nb - 1 - step

        @pl.when(step == 0)
        def _():
            carry[...] = jnp.zeros_like(carry)
            db_ref[...] = jnp.zeros_like(db_ref)

        dcb = dc_ref[...]
        lane = lax.broadcasted_iota(jnp.int32, (ROW_TILE, F_COLS), 1)
        for h in range(N_HEADS):
            dcb = dcb + jnp.where(lane == h, dcq_ref[:, HEAD_DIM * h:HEAD_DIM * h + 1], 0.0)
        r_i = lax.broadcasted_iota(jnp.int32, (ROW_TILE, ROW_TILE), 0)
        c_i = lax.broadcasted_iota(jnp.int32, (ROW_TILE, ROW_TILE), 1)
        upper = (c_i >= r_i).astype(F32)
        dlf = jnp.dot(upper, dcb, precision=HIGHEST, preferred_element_type=F32) + carry[...]
        carry[...] = carry[...] + jnp.sum(dcb, axis=0, keepdims=True)
        logit = f_ref[...] + b_ref[...]
        dlogit = jnp.where(_valid_gate_mask(i), dlf * _sigmoid(-logit), 0.0)
        df_ref[...] = dlogit.astype(BF16)
        db_ref[...] += jnp.sum(dlogit, axis=0, keepdims=True)

    blk = pl.BlockSpec((ROW_TILE, F_COLS), lambda s: (nb - 1 - s, 0))
    wide = pl.BlockSpec((ROW_TILE, ATTN_WIDTH), lambda s: (nb - 1 - s, 0))
    one = pl.BlockSpec((1, F_COLS), lambda s: (0, 0))
    return pl.pallas_call(
        body, name="gates_bwd", grid=(nb,),
        in_specs=[blk, wide, blk, one], out_specs=[blk, one],
        out_shape=[jax.ShapeDtypeStruct(f.shape, BF16), jax.ShapeDtypeStruct((1, F_COLS), F32)],
        scratch_shapes=[pltpu.VMEM((1, F_COLS), F32)],
        compiler_params=_params(("arbitrary",)),
    )(dc, dcq, f, bfg)


def _input_bwd(dproj, df, wt_e, wt_qkv, wt_f, x2, metapad, dh1, g1, hs, sc=None):
    nb = x2.shape[0] // ROW_TILE + 1
    n = len(hs)
    sc_in, sc_shapes, sc_sems = _scatter_operands(hs, sc)
    nq = len(sc_in)

    def body(dp_ref, df_ref, we_ref, w_ref, wf_ref, x_ref, mp_ref, dh_ref, g_ref, *rest):
        h_refs, s_ref = rest[:n], (rest[n] if sc is not None else None)
        gx_ref, g0_ref, dg_ref = rest[nq:nq + 3]
        q_refs, sq_ref = rest[nq + 3:nq + 3 + n], (rest[nq + 3 + n] if sc is not None else None)
        sems = rest[2 * nq + 3:]
        i = pl.program_id(0)

        @pl.when(i == 0)
        def _():
            dg_ref[...] = jnp.zeros_like(dg_ref)
            own, sends, _ = _chip_scatter_plan(h_refs, s_ref, q_refs, sq_ref, *sems)
            for cp in own + sends:
                cp.start()

        dhn = (jnp.dot(dp_ref[:, :E_ZA], we_ref[:E_ZA, :], preferred_element_type=F32)
               + jnp.dot(dp_ref[:, E_ZA:MAIN_COLS // 2], w_ref[...], preferred_element_type=F32)
               + jnp.dot(dp_ref[:, MAIN_COLS // 2:], we_ref[E_ZA:, :], preferred_element_type=F32)
               + jnp.dot(df_ref[...], wf_ref[...], preferred_element_type=F32))
        h0 = jnp.where(i == 0, mp_ref[...], x_ref[...])
        r = lax.rsqrt(jnp.mean(h0 * h0, axis=-1, keepdims=True) + RMS_EPS)
        xhat = h0 * r
        dg_ref[...] += jnp.sum(dhn * xhat, axis=0, keepdims=True)
        dxhat = dhn * g_ref[...]
        dh0 = dh_ref[...] + r * (dxhat - xhat * jnp.mean(dxhat * xhat, axis=-1, keepdims=True))
        gx_ref[...] = dh0

        @pl.when(i == 0)
        def _():
            g0_ref[...] = dh0

        @pl.when(i == nb - 1)
        def _():
            own, sends, recvs = _chip_scatter_plan(h_refs, s_ref, q_refs, sq_ref, *sems)
            for cp in recvs:
                cp.wait_recv()
            for cp in sends:
                cp.wait_send()
            for cp in own:
                cp.wait()

    const = lambda shape: pl.BlockSpec(shape, lambda i: (0, 0))
    res = pl.pallas_call(
        body, name="input_bwd", grid=(nb,),
        in_specs=[pl.BlockSpec((ROW_TILE, MAIN_COLS), lambda i: (i, 0)), pl.BlockSpec((ROW_TILE, F_COLS), lambda i: (i, 0)),
                  const((E_COLS, D_MODEL)), const((QKV_COLS, D_MODEL)), const((F_COLS, D_MODEL)),
                  _tokens_spec(), const((ROW_TILE, D_MODEL)),
                  pl.BlockSpec((ROW_TILE, D_MODEL), lambda i: (i, 0)), const((1, D_MODEL))] + [HBM] * nq,
        out_specs=[_tokens_spec(), const((ROW_TILE, D_MODEL)), const((1, D_MODEL))] + [HBM] * nq,
        out_shape=[jax.ShapeDtypeStruct(x2.shape, F32), jax.ShapeDtypeStruct((ROW_TILE, D_MODEL), F32),
                   jax.ShapeDtypeStruct((1, D_MODEL), F32)] + sc_shapes,
        scratch_shapes=sc_sems,
        compiler_params=_params(("arbitrary",), vmem=56 * 1024 * 1024),
    )(dproj, df, wt_e, wt_qkv, wt_f, x2, metapad, dh1, g1, *_in_hbm(*sc_in))
    return res[:3], res[3:3 + n], (res[3 + n] if sc is not None else None)


def _adamw(w, g, m, v, name):
    rows, cols = w.shape
    if rows % 8 == 0:
        tr, tc = _row_tile8(rows), cols
    else:
        tr, tc = rows, (2 * LANES if cols % (2 * LANES) == 0 and rows > 8 else cols)

    def body(w_ref, g_ref, m_ref, v_ref, d_ref, mo_ref, vo_ref):
        g_ = g_ref[...]
        m_new = ADAM_B1 * m_ref[...] + (1.0 - ADAM_B1) * g_
        v_new = ADAM_B2 * v_ref[...] + (1.0 - ADAM_B2) * (g_ * g_)
        m_hat = m_new / (1.0 - ADAM_B1 ** ADAM_STEP)
        v_hat = v_new / (1.0 - ADAM_B2 ** ADAM_STEP)
        d_ref[...] = -ADAM_LR * (m_hat / (jnp.sqrt(v_hat) + ADAM_EPS) + ADAM_WD * w_ref[...])
        mo_ref[...] = m_new
        vo_ref[...] = v_new

    blk = pl.BlockSpec((tr, tc), lambda i, j: (i, j))
    return pl.pallas_call(
        body, name=name, grid=(rows // tr, cols // tc),
        in_specs=[blk] * 4, out_specs=[blk] * 3,
        out_shape=[jax.ShapeDtypeStruct(w.shape, F32)] * 3,
        compiler_params=_params(("parallel", "parallel")),
    )(w, g, m, v)


def _adamw_native(w3, g3, m3, v3, name):
    rows = w3.shape[0]
    tr = rows // 2
    blk = pl.BlockSpec((tr,) + w3.shape[1:], lambda i: (i, 0, 0))
    shape = jax.ShapeDtypeStruct(w3.shape, F32)

    def moments(g_ref, m_ref, v_ref, mo_ref, vo_ref):
        g_ = g_ref[...]
        mo_ref[...] = ADAM_B1 * m_ref[...] + (1.0 - ADAM_B1) * g_
        vo_ref[...] = ADAM_B2 * v_ref[...] + (1.0 - ADAM_B2) * (g_ * g_)

    new_m, new_v = pl.pallas_call(
        moments, name=name + "_moments", grid=(2,), in_specs=[blk] * 3, out_specs=[blk] * 2, out_shape=[shape] * 2,
        compiler_params=_params(("parallel",)),
    )(g3, m3, v3)

    def delta(w_ref, m_ref, v_ref, d_ref):
        m_hat = m_ref[...] / (1.0 - ADAM_B1 ** ADAM_STEP)
        v_hat = v_ref[...] / (1.0 - ADAM_B2 ** ADAM_STEP)
        d_ref[...] = -ADAM_LR * (m_hat / (jnp.sqrt(v_hat) + ADAM_EPS) + ADAM_WD * w_ref[...])

    d = pl.pallas_call(
        delta, name=name + "_delta", grid=(2,), in_specs=[blk] * 3, out_specs=blk, out_shape=shape,
        compiler_params=_params(("parallel",)),
    )(w3, new_m, new_v)
    return d, new_m, new_v


def _row_tile8(rows):
    best = rows
    for t in range(8, 257, 8):
        if rows % t == 0:
            best = t
    return best


def kernel(x, meta_tokens, norm_g, w_in, b_forget, pool_w, pool_scale, w_up_pool, w_up_attn, w_out, final_norm_g, loss_target, m_meta_tokens, m_norm_g, m_w_in, m_b_forget, m_pool_w, m_pool_scale, m_w_up_pool, m_w_up_attn, m_w_out, m_final_norm_g, v_meta_tokens, v_norm_g, v_w_in, v_b_forget, v_pool_w, v_pool_scale, v_w_up_pool, v_w_up_attn, v_w_out, v_final_norm_g):
    seq = x.shape[1]
    assert seq % ROW_TILE == 0 and x.shape[0] == 1
    lp = seq + ROW_TILE
    nb = lp // ROW_TILE
    lk = -(-lp // KV_TILE_BWD) * KV_TILE_BWD
    core = jnp.reshape(lax.axis_index("c"), (1,)).astype(jnp.int32)
    x2 = x[0]
    target = loss_target[0]
    sh_d = D_MODEL // N_CHIPS

    to_rows = lambda a: jnp.transpose(a, (2, 0, 1))
    from_rows = lambda a: jnp.transpose(a, (1, 2, 0))
    gf = final_norm_g.reshape(1, D_MODEL)
    bfg = jnp.pad(b_forget, ((0, 0), (0, F_COLS - N_HEADS)))
    pw_b = pool_w[0].astype(BF16)
    hn, (wg_in, meta_g) = _rmsnorm_fwd(x2, norm_g, lk, [jnp.transpose(w_in[0]).astype(BF16), meta_tokens], [1, 0])
    wt_e, wt_qkv, wt_f = _weight_sections(wg_in)
    meta_full = jnp.transpose(meta_g, (1, 0, 2)).reshape(N_META, D_MODEL)
    metapad = jnp.pad(meta_full, ((PAD_ROWS, 0), (0, 0)))

    tm = _row_tile(lp, 2200)
    e, (wg_up_p, wg_up_a, wg_out) = _mm_nt(
        hn, wt_e, BF16, "in_proj_gates", lp, E_COLS, tm, 512,
        gather=([w_up_pool[0].astype(BF16), w_up_attn[0].astype(BF16), w_out[0].astype(BF16)], [0, 0, 0]))
    wup_p = jnp.transpose(wg_up_p, (1, 0, 2)).reshape(POOL_WIDTH, D_MODEL)
    wup_a = jnp.transpose(wg_up_a, (1, 0, 2)).reshape(ATTN_WIDTH, D_MODEL)
    wout = wg_out.reshape(D_MODEL, D_MODEL)
    qkv = _mm_nt(hn, wt_qkv, BF16, "in_proj_qkv", lk, QKV_COLS, _row_tile(lk, 2600), 512, scale_first=HEAD_DIM ** -0.5)
    f = _mm_nt(hn, wt_f, F32, "in_proj_forget", lp, F_COLS, tm, F_COLS)
    c = _gates_fwd(f, bfg)
    c_t = jnp.transpose(c[:, :N_HEADS])
    c_first = jnp.transpose(c_t[:, ::ROW_TILE]).reshape(nb, N_HEADS // 2, 2, 1)
    c_keys = jnp.pad(c_t, ((0, 0), (0, lk - lp))).reshape(1, N_HEADS // 2, 2, lk)
    bias = jnp.where(jnp.arange(lk) < PAD_ROWS, -NEG, c_keys - c_first)
    y_pool = _pool_fwd(e, pw_b, pool_scale)
    o, lse = _attn_fwd(qkv, bias, nb)
    merged, y_attn = _merge_fwd(y_pool, o, e, wup_p, wup_a)
    dh1, loss_part, dgf = _head_fwd_bwd(merged, wout, x2, metapad, gf, target)

    dap, daa, dproj_half, do, delta, dy_pool = _merge_bwd(dh1, wout, y_pool, y_attn, wup_p, wup_a, e, o)
    dpc, dzp, dscale, dpw = _pool_bwd_local(e, dy_pool, pw_b, pool_scale)
    du = _pool_bwd_window(dpc)
    dq, dk, dv, dc4, dcq = _attn_bwd(qkv, do, lse, delta, bias, nb)
    dc = jnp.transpose(dc4, (1, 3, 0, 2)).reshape(lk, N_HEADS)[:lp]
    df, db = _gates_bwd(jnp.pad(dc, ((0, 0), (0, F_COLS - N_HEADS))), dcq, f, bfg)
    dproj = _fill_dproj(dproj_half, du, dzp, dq, dk, dv)
    tk = _row_tile(lp, 1100)
    dw_out = _mm_tn(merged, dh1, "grad_w_out", lp, 512, D_MODEL, tk)
    dw_up_p = _mm_tn(y_pool, dap, "grad_w_up_pool", lp, 512, D_MODEL, lp, chunks=N_CHIPS)
    dw_up_a = _mm_tn(y_attn, daa, "grad_w_up_attn", lp, 512, D_MODEL, lp, chunks=N_CHIPS)
    dwt_f = _mm_tn(df, hn, "grad_w_in_forget", lp, F_COLS, D_MODEL, tk)

    def pad8(a):
        return jnp.pad(a, ((0, (-a.shape[0]) % 8), (0, 0)))

    small_parts = [pad8(a) for a in (dgf.reshape(-1, LANES), dscale.reshape(-1, LANES), db, dpw.reshape(-1, LANES),
                                     loss_part)]
    small = jnp.concatenate(small_parts, axis=0)
    soffs = [0]
    for p in small_parts:
        soffs.append(soffs[-1] + p.shape[0])

    gs_rows = [dw_up_p, dw_up_a, dw_out.reshape(N_CHIPS, sh_d, D_MODEL)]
    half = MAIN_COLS // 2
    dwt_a, (*rb_rows, rb_f, sr) = _mm_tn(dproj, hn, "grad_w_in_a", lp, half // 4, D_MODEL, lp, m=half, m_off=0,
                                         swap=(gs_rows + [dwt_f, small], [1, 1, 1, 1, None]))
    *hs_rows, sc = _add_sibling_half(gs_rows, rb_rows, [1, 1, 1], small, sr, core)
    dwt_b, (rb_a,), (qs_rows, sq) = _mm_tn(dproj, hn, "grad_w_in_b", lp, half // 4, D_MODEL, lp, m=half, m_off=4,
                                           swap=([dwt_a], [1]), scatter=(hs_rows, sc))
    (rb_b,) = _sibling_exchange([dwt_b], [1])
    h_in = _add_halves_w_in([dwt_a, dwt_b, dwt_f], [rb_a, rb_b, rb_f], core)
    grad_axes = [2, 1, 1, 1]
    (grad_x, g_block0, dg1), (q_in,), _ = _input_bwd(dproj, df, wt_e, wt_qkv, wt_f, x2, metapad, dh1, norm_g, [h_in])
    late_parts = jnp.concatenate([pad8(dg1.reshape(-1, LANES)), g_block0[PAD_ROWS:].reshape(-1, LANES)], axis=0)
    g_w_in_rows, g_w_up_p, g_w_up_a, g_w_out, st, late = _reduce_allgather(
        [q_in] + list(qs_rows), grad_axes, sq, [True, False, False, False], late_parts)
    n_norm = D_MODEL // LANES
    g_norm = late[:n_norm].reshape(1, D_MODEL)
    chip = 2 * lax.axis_index("x") + lax.axis_index("y")
    g_meta = lax.dynamic_slice_in_dim(late[n_norm:].reshape(N_META, D_MODEL), chip * sh_d, sh_d, axis=1)

    spiece = lambda k, rows: st[soffs[k]:soffs[k] + rows]
    g_final = spiece(0, D_MODEL // LANES).reshape(1, D_MODEL)
    g_scale = spiece(1, POOL_WIDTH // LANES).reshape(1, POOL_WIDTH)
    g_bf = spiece(2, 1)
    g_pw = spiece(3, POOL_WIDTH)
    loss = st[soffs[4], 0]

    def pad_lanes(a):
        return jnp.pad(a, ((0, 0), (0, F_COLS - N_HEADS)))

    w_in_res = (g_w_in_rows,) + _adamw_native(to_rows(w_in), g_w_in_rows, to_rows(m_w_in), to_rows(v_w_in), "adamw_w_in")

    upd = [
        ("meta_tokens", meta_tokens, g_meta, m_meta_tokens, v_meta_tokens),
        ("norm_g", norm_g, g_norm, m_norm_g, v_norm_g),
        ("w_in", None, None, None, None),
        ("b_forget", pad_lanes(b_forget), g_bf, pad_lanes(m_b_forget), pad_lanes(v_b_forget)),
        ("pool_w", pool_w.reshape(-1, LANES), g_pw, m_pool_w.reshape(-1, LANES), v_pool_w.reshape(-1, LANES)),
        ("pool_scale", pool_scale, g_scale, m_pool_scale, v_pool_scale),
        ("w_up_pool", w_up_pool[0], g_w_up_p, m_w_up_pool[0], v_w_up_pool[0]),
        ("w_up_attn", w_up_attn[0], g_w_up_a, m_w_up_attn[0], v_w_up_attn[0]),
        ("w_out", w_out[0], g_w_out, m_w_out[0], v_w_out[0]),
        ("final_norm_g", gf, g_final, m_final_norm_g.reshape(1, D_MODEL), v_final_norm_g.reshape(1, D_MODEL)),
    ]
    shapes = [meta_tokens.shape, norm_g.shape, w_in.shape, b_forget.shape, pool_w.shape, pool_scale.shape,
              w_up_pool.shape, w_up_attn.shape, w_out.shape, final_norm_g.shape]
    grads, deltas, new_ms, new_vs = [], [], [], []
    for (name, w_, g_, m_in, v_in), shp in zip(upd, shapes):
        if name == "w_in":
            res = tuple(from_rows(a) for a in w_in_res)
        else:
            d_, mn_, vn_ = _adamw(w_, g_, m_in, v_in, "adamw_" + name)
            res = (g_, d_, mn_, vn_)
        if name == "b_forget":
            res = tuple(a[:, :N_HEADS] for a in res)
        for lst, a in zip((grads, deltas, new_ms, new_vs), res):
            lst.append(a.reshape(shp))

    return (loss, grad_x.reshape(x.shape), *grads, *deltas, *new_ms, *new_vs)
```

```python
import jax
import jax.numpy as jnp
from jax import lax
from jax.experimental import pallas as pl
from jax.experimental.pallas import tpu as pltpu

F32 = jnp.float32
BF16 = jnp.bfloat16
MESH = pl.DeviceIdType.MESH
HIGHEST = lax.Precision.HIGHEST
HBM = pl.BlockSpec(memory_space=pltpu.HBM)

D_MODEL = 1024
N_META = 16
POOL_WIDTH = 512
POOL_GROUP = 128
POOL_WINDOWS = (2, 4, 8, 16)
N_HEADS = 8
HEAD_DIM = 64
ATTN_WIDTH = 512
RMS_EPS = 1e-6
N_CHIPS = 4

ADAM_LR = 0.001
ADAM_B1 = 0.9
ADAM_B2 = 0.999
ADAM_EPS = 1e-08
ADAM_WD = 0.01
ADAM_STEP = 10

LANES = 128
ROW_TILE = 256
KV_TILE = 1024
KV_TILE_BWD = 1024
PAD_ROWS = ROW_TILE - N_META
NEG = -1e30

E_U, E_ZP, E_ZA, E_GP, E_GA, E_COLS = 0, 512, 1024, 1536, 2560, 3584
QKV_COLS = 3 * ATTN_WIDTH
MAIN_COLS = E_COLS + QKV_COLS
F_COLS = LANES
REF_U, REF_Q, REF_ZA, REF_F, REF_GP = 0, 1024, 2560, 3072, 3080
VMEM_LIMIT = 48 * 1024 * 1024


def _params(sem=None, vmem=VMEM_LIMIT):
    return pltpu.CompilerParams(dimension_semantics=sem, vmem_limit_bytes=vmem)


def _in_hbm(*arrays):
    return [pltpu.with_memory_space_constraint(a, pltpu.HBM) for a in arrays]


def _row_tile(n, target):
    best = 16
    for t in range(16, target + 1, 16):
        if n % t == 0:
            best = t
    return best


def _sigmoid(x):
    return 1.0 / (1.0 + jnp.exp(-x))


def _half(ref, axis, which):
    n = ref.shape[axis] // 2
    idx = [slice(None)] * len(ref.shape)
    idx[axis] = pl.ds(pl.multiple_of(which * n, n), n)
    return ref.at[tuple(idx)]


def _half_shape(shape, axis):
    s = list(shape)
    s[axis] //= 2
    return tuple(s)


def _gather_start(ins, outs, axes, send, recv, fsend, frecv, local):
    x, y, c = lax.axis_index("x"), lax.axis_index("y"), lax.axis_index("c")
    me = 2 * x + y
    for t in range(len(ins)):
        pltpu.make_async_copy(ins[t], outs[t].at[me], local.at[t]).start()
        for k, chip in enumerate([(1 - x, y), (x, 1 - y), (1 - x, 1 - y)]):
            pltpu.make_async_remote_copy(
                src_ref=_half(ins[t], axes[t], c), dst_ref=_half(outs[t].at[me], axes[t], c),
                send_sem=send.at[3 * t + k], recv_sem=recv.at[3 * t + k], device_id=(*chip, c), device_id_type=MESH).start()


def _gather_finish(ins, outs, axes, send, recv, fsend, frecv, local):
    x, y, c = lax.axis_index("x"), lax.axis_index("y"), lax.axis_index("c")
    me = 2 * x + y
    sibling = (x, y, 1 - c)
    chips = [(1 - x, y), (x, 1 - y), (1 - x, 1 - y)]
    n = len(ins)

    def over_ici(t, k, chip, dst_slot):
        return pltpu.make_async_remote_copy(
            src_ref=_half(ins[t], axes[t], c), dst_ref=_half(outs[t].at[dst_slot], axes[t], c),
            send_sem=send.at[3 * t + k], recv_sem=recv.at[3 * t + k], device_id=(*chip, c), device_id_type=MESH)

    def to_sibling(t, k, slot, which):
        return pltpu.make_async_remote_copy(
            src_ref=_half(outs[t].at[slot], axes[t], which), dst_ref=_half(outs[t].at[slot], axes[t], which),
            send_sem=fsend.at[3 * t + k], recv_sem=frecv.at[3 * t + k], device_id=sibling, device_id_type=MESH)

    forwards = []
    for t in range(n):
        for k, (px, py) in enumerate(chips):
            over_ici(t, k, (px, py), 2 * px + py).wait_recv()
            fw = to_sibling(t, k, 2 * px + py, c)
            fw.start()
            forwards.append(fw)
    for t in range(n):
        for k, (px, py) in enumerate(chips):
            to_sibling(t, k, 2 * px + py, 1 - c).wait_recv()
    for t in range(n):
        for k, chip in enumerate(chips):
            over_ici(t, k, chip, me).wait_send()
    for fw in forwards:
        fw.wait_send()
    for t in range(n):
        pltpu.make_async_copy(ins[t], outs[t].at[me], local.at[t]).wait()


def _gather_sems(n):
    return [pltpu.SemaphoreType.DMA((3 * n,)), pltpu.SemaphoreType.DMA((3 * n,)), pltpu.SemaphoreType.DMA((3 * n,)),
            pltpu.SemaphoreType.DMA((3 * n,)), pltpu.SemaphoreType.DMA((n,))]


def _gathered_shapes(shards):
    return [jax.ShapeDtypeStruct((N_CHIPS,) + s.shape, s.dtype) for s in shards]


def _swap_copies(srcs, dsts, axes, send, recv):
    x, y, c = lax.axis_index("x"), lax.axis_index("y"), lax.axis_index("c")
    return [pltpu.make_async_remote_copy(
        src_ref=srcs[t] if axes[t] is None else _half(srcs[t], axes[t], 1 - c), dst_ref=dsts[t],
        send_sem=send.at[t], recv_sem=recv.at[t], device_id=(x, y, 1 - c), device_id_type=MESH) for t in range(len(srcs))]


def _swap_shapes(srcs, axes):
    return [jax.ShapeDtypeStruct(g.shape if a is None else _half_shape(g.shape, a), g.dtype) for g, a in zip(srcs, axes)]


def _add_halves_w_in(parts, rbs, core):
    na, nb_rows, w = rbs[0].shape[0], parts[1].shape[0], rbs[0].shape[1]
    per = (na + nb_rows + N_HEADS) // N_CHIPS
    pieces = [[] for _ in range(N_CHIPS)]
    r = 0
    for src, lo, hi in ((0, 0, na), (1, 0, ATTN_WIDTH), (2, 0, N_HEADS), (1, ATTN_WIDTH, nb_rows)):
        while lo < hi:
            c, at = divmod(r, per)
            n = min(hi - lo, per - at)
            pieces[c].append((src, lo, n, at))
            lo, r = lo + n, r + n

    def body(core_ref, a_ref, b_ref, f_ref, ra_ref, rf_ref, o_ref, am, bm, fm, ar, br, fr, rows_buf, sems, send, recv):
        x, y, core_id = lax.axis_index("x"), lax.axis_index("y"), lax.axis_index("c")
        swap = pltpu.make_async_remote_copy(src_ref=_half(b_ref, 1, 1 - core_id), dst_ref=br, send_sem=send.at[0],
                                            recv_sem=recv.at[0], device_id=(x, y, 1 - core_id), device_id_type=MESH)
        swap.start()
        cols = pl.ds(pl.multiple_of(core_ref[0] * w, w), w)
        few = pl.ds(0, N_HEADS)
        loads = [(pltpu.make_async_copy(a_ref.at[:, cols], am, sems.at[0]), pltpu.make_async_copy(ra_ref, ar, sems.at[1])),
                 (pltpu.make_async_copy(b_ref.at[:, cols], bm, sems.at[2]),),
                 (pltpu.make_async_copy(f_ref.at[few, cols], fm, sems.at[3]),
                  pltpu.make_async_copy(rf_ref.at[few, :], fr, sems.at[4]))]
        for group in loads:
            for cp in group:
                cp.start()
        bufs = [(am, ar), (bm, br), (fm, fr)]
        there = set()
        for c in range(N_CHIPS):
            for src, lo, n, at in pieces[c]:
                if src not in there:
                    for cp in loads[src]:
                        cp.wait()
                    if src == 1:
                        swap.wait_recv()
                    there.add(src)
                mine, other = bufs[src]
                rows_buf[at:at + n, :] = mine[lo:lo + n, :] + other[lo:lo + n, :]
            o_ref[c] = rows_buf[0:per, :].astype(BF16)
        swap.wait_send()

    any_space = pl.BlockSpec(memory_space=pl.ANY)
    a, b, f = parts
    return pl.pallas_call(
        body, name="grad_add_sibling_w_in",
        grid_spec=pltpu.PrefetchScalarGridSpec(
            num_scalar_prefetch=1, grid=(1,),
            in_specs=[any_space, HBM, any_space, any_space, any_space],
            out_specs=pl.BlockSpec((N_CHIPS, per, w), lambda i, cr: (0, 0, 0)),
            scratch_shapes=[pltpu.VMEM((na, w), F32), pltpu.VMEM((nb_rows, w), F32), pltpu.VMEM((N_HEADS, w), F32),
                            pltpu.VMEM((na, w), F32), pltpu.VMEM((nb_rows, w), F32), pltpu.VMEM((N_HEADS, w), F32),
                            pltpu.VMEM((per + (-per) % 8, w), F32), pltpu.SemaphoreType.DMA((5,)),
                            pltpu.SemaphoreType.DMA((1,)), pltpu.SemaphoreType.DMA((1,))]),
        out_shape=jax.ShapeDtypeStruct((N_CHIPS, per, w), BF16),
        compiler_params=_params(("arbitrary",)),
    )(core, a, *_in_hbm(b), f, *rbs)


def _weight_sections(wg):
    per, d = wg.shape[1], wg.shape[2]
    n_ref = N_CHIPS * per
    units = [(0, 0, REF_U, REF_Q), (1, 0, REF_Q, REF_ZA), (0, REF_Q - REF_U, REF_ZA, REF_F), (2, 0, REF_F, REF_GP),
             (0, REF_Q - REF_U + REF_F - REF_ZA, REF_GP, n_ref)]
    plan = []
    for out, first, lo, hi in sorted(units, key=lambda u: u[2]):
        pieces, at = [], 0
        while lo < hi:
            c, src = divmod(lo, per)
            n = min(hi - lo, per - src)
            pieces.append((c, src, n, at))
            lo, at = lo + n, at + n
        plan.append((out, first, at, pieces))
    stage_rows = max(rows for _, _, rows, _ in plan)

    def body(g_ref, e_ref, q_ref, f_ref, gb, stage, sems):
        loads = [pltpu.make_async_copy(g_ref.at[c], gb.at[c], sems.at[c]) for c in range(N_CHIPS)]
        for cp in loads:
            cp.start()
        outs = [e_ref, q_ref, f_ref]
        there = set()
        for out, first, rows, pieces in plan:
            for c, _, _, _ in pieces:
                if c not in there:
                    loads[c].wait()
                    there.add(c)
            if len(pieces) == 1 and pieces[0][1] % 16 == 0 and rows % 16 == 0:
                c, src, _, _ = pieces[0]
                outs[out][first:first + rows, :] = gb[c, src:src + rows, :]
                continue
            for c, src, n, at in pieces:
                stage[at:at + n, :] = gb[c, src:src + n, :].astype(F32)
            padded = rows if rows % 16 == 0 else outs[out].shape[0] - first
            if padded > rows:
                stage[rows:padded, :] = jnp.zeros((padded - rows, d), F32)
            outs[out][first:first + padded, :] = stage[0:padded, :].astype(BF16)

    return pl.pallas_call(
        body, name="weight_sections",
        in_specs=[pl.BlockSpec(memory_space=pl.ANY)],
        out_shape=[jax.ShapeDtypeStruct((E_COLS, d), BF16), jax.ShapeDtypeStruct((QKV_COLS, d), BF16),
                   jax.ShapeDtypeStruct((F_COLS, d), BF16)],
        scratch_shapes=[pltpu.VMEM(wg.shape, BF16), pltpu.VMEM((stage_rows, d), F32), pltpu.SemaphoreType.DMA((N_CHIPS,))],
        compiler_params=_params(),
    )(wg)


def _add_sibling_half(gs, rbs, axes, small, sr, core):
    n = len(gs)

    def body(core_ref, *refs):
        g_refs, rb_refs = refs[:n], refs[n:2 * n]
        s_ref, sr_ref = refs[2 * n], refs[2 * n + 1]
        h_refs, sc_ref = refs[2 * n + 2:3 * n + 2], refs[3 * n + 2]
        for t in range(n):
            h_refs[t][...] = (g_refs[t][...] + rb_refs[t][...]).astype(BF16)
        sc_ref[...] = s_ref[...] + sr_ref[...]

    def mine(rb, axis):
        blk = (None,) + rb.shape[1:]
        if axis == 2:
            return pl.BlockSpec(blk, lambda j, cr: (j, 0, cr[0]))
        return pl.BlockSpec(blk, lambda j, cr: (j, cr[0], 0))

    chunk = lambda rb: pl.BlockSpec((None,) + rb.shape[1:], lambda j, cr: (j, 0, 0))
    whole = pl.BlockSpec(small.shape, lambda j, cr: (0, 0))
    return pl.pallas_call(
        body, name="grad_add_sibling",
        grid_spec=pltpu.PrefetchScalarGridSpec(
            num_scalar_prefetch=1, grid=(N_CHIPS,),
            in_specs=[mine(rb, a) for rb, a in zip(rbs, axes)] + [chunk(rb) for rb in rbs] + [whole, whole],
            out_specs=[chunk(rb) for rb in rbs] + [whole]),
        out_shape=[jax.ShapeDtypeStruct(rb.shape, BF16) for rb in rbs] + [jax.ShapeDtypeStruct(small.shape, F32)],
        compiler_params=_params(("arbitrary",)),
    )(core, *gs, *rbs, small, sr)


def _chip_scatter_plan(h_refs, s_ref, q_refs, sq_ref, send, recv, local):
    n = len(h_refs)
    nt = n + (s_ref is not None)
    x, y, c = lax.axis_index("x"), lax.axis_index("y"), lax.axis_index("c")
    me = 2 * x + y
    chips = [(1 - x, y), (x, 1 - y), (1 - x, 1 - y)]

    def copy(t, k, chip, src_slot, dst_slot):
        src = h_refs[t].at[src_slot] if t < n else s_ref
        dst = (q_refs[t] if t < n else sq_ref).at[dst_slot]
        return pltpu.make_async_remote_copy(src_ref=src, dst_ref=dst, send_sem=send.at[3 * t + k],
                                            recv_sem=recv.at[3 * t + k], device_id=(*chip, c), device_id_type=MESH)

    own = [pltpu.make_async_copy(h_refs[t].at[me], q_refs[t].at[me], local.at[t]) for t in range(n)]
    if s_ref is not None:
        own.append(pltpu.make_async_copy(s_ref, sq_ref.at[me], local.at[n]))
    sends = [copy(t, k, (px, py), 2 * px + py, me) for t in range(nt) for k, (px, py) in enumerate(chips)]
    recvs = [copy(t, k, (px, py), me, 2 * px + py) for t in range(nt) for k, (px, py) in enumerate(chips)]
    return own, sends, recvs


def _scatter_operands(hs, sc):
    small = [] if sc is None else [sc]
    nt = len(hs) + len(small)
    shapes = [jax.ShapeDtypeStruct(h.shape, h.dtype) for h in hs] + [
        jax.ShapeDtypeStruct((N_CHIPS,) + a.shape, a.dtype) for a in small]
    sems = [pltpu.SemaphoreType.DMA((3 * nt,)), pltpu.SemaphoreType.DMA((3 * nt,)), pltpu.SemaphoreType.DMA((nt,))]
    return list(hs) + small, shapes, sems


def _reduce_allgather(qs, axes, sq, as_rows, late):
    n = len(qs)
    shard_shapes, half_axes = [], []
    for q, a, rows_form in zip(qs, axes, as_rows):
        shape = [d * 2 if i == a - 1 else d for i, d in enumerate(q.shape[1:])]
        if rows_form:
            assert a == 2
            shape = [shape[0], 1, shape[1]]
        shard_shapes.append(tuple(shape))
        half_axes.append(2 if rows_form else a - 1)

    def body(*refs):
        q_refs, sq_ref, late_ref = refs[:n], refs[n], refs[n + 1]
        o_refs, st_ref, late_sum_ref = refs[n + 2:2 * n + 2], refs[2 * n + 2], refs[2 * n + 3]
        sib_buf, chip_buf, send, recv, late_send, late_recv = refs[2 * n + 4:]
        x, y, c = lax.axis_index("x"), lax.axis_index("y"), lax.axis_index("c")
        me = 2 * x + y
        chips = [(1 - x, y), (x, 1 - y), (1 - x, 1 - y)]

        def swap(t, which):
            return pltpu.make_async_remote_copy(
                src_ref=_half(o_refs[t], half_axes[t], which), dst_ref=_half(o_refs[t], half_axes[t], which),
                send_sem=send.at[t], recv_sem=recv.at[t], device_id=(x, y, 1 - c), device_id_type=MESH)

        def late_copy(k, chip, slot):
            return pltpu.make_async_remote_copy(src_ref=chip_buf.at[slot], dst_ref=chip_buf.at[slot], send_sem=late_send.at[1 + k],
                                                recv_sem=late_recv.at[1 + k], device_id=(*chip, c), device_id_type=MESH)

        late_swap = pltpu.make_async_remote_copy(src_ref=late_ref, dst_ref=sib_buf, send_sem=late_send.at[0],
                                                 recv_sem=late_recv.at[0], device_id=(x, y, 1 - c), device_id_type=MESH)
        late_swap.start()
        sent = []
        for t in range(n):
            q = q_refs[t]
            total = ((q[0].astype(F32) + q[1].astype(F32)) + q[2].astype(F32)) + q[3].astype(F32)
            if as_rows[t]:
                total = total.reshape(total.shape[0], 1, total.shape[1])
            _half(o_refs[t], half_axes[t], c)[...] = total
            cp = swap(t, c)
            cp.start()
            sent.append(cp)
            if t == 0:
                late_swap.wait()
                chip_buf[me] = late_ref[...] + sib_buf[...]
                late_sends = [late_copy(k, chip, me) for k, chip in enumerate(chips)]
                for lc in late_sends:
                    lc.start()
        st_ref[...] = ((sq_ref[0] + sq_ref[1]) + sq_ref[2]) + sq_ref[3]
        for t in range(n):
            swap(t, 1 - c).wait_recv()
        for cp in sent:
            cp.wait_send()
        for k, (px, py) in enumerate(chips):
            late_copy(k, (px, py), 2 * px + py).wait_recv()
        for lc in late_sends:
            lc.wait_send()
        late_sum_ref[...] = ((chip_buf[0] + chip_buf[1]) + chip_buf[2]) + chip_buf[3]

    vmem = pl.BlockSpec(memory_space=pltpu.VMEM)
    return pl.pallas_call(
        body, name="grad_reduce_allgather",
        in_specs=[vmem] * (n + 2), out_specs=[vmem] * (n + 2),
        out_shape=[jax.ShapeDtypeStruct(s, F32) for s in shard_shapes] + [jax.ShapeDtypeStruct(sq.shape[1:], F32),
                                                                         jax.ShapeDtypeStruct(late.shape, F32)],
        scratch_shapes=[pltpu.VMEM(late.shape, F32), pltpu.VMEM((N_CHIPS,) + late.shape, F32),
                        pltpu.SemaphoreType.DMA((n,)), pltpu.SemaphoreType.DMA((n,)),
                        pltpu.SemaphoreType.DMA((4,)), pltpu.SemaphoreType.DMA((4,))],
        compiler_params=_params(),
    )(*qs, sq, late)


def _dot_nt(a, b):
    return lax.dot_general(a, b, (((1,), (1,)), ((), ())), preferred_element_type=F32)


def _dot_tn(a, b):
    return lax.dot_general(a, b, (((0,), (0,)), ((), ())), preferred_element_type=F32)


def _mm_nt(a, bt, out_dtype, name, m, n, tm, tn, row_block=0, scale_first=None, gather=None):
    k = a.shape[1]
    shards, axes = gather if gather is not None else ([], [])
    ng = len(shards)
    grid = (m // tm, n // tn)

    def body(a_ref, b_ref, *rest):
        ins, o_ref, outs, sems = rest[:ng], rest[ng], rest[ng + 1:2 * ng + 1], rest[2 * ng + 1:]
        first = (pl.program_id(0) == 0) & (pl.program_id(1) == 0)
        last = (pl.program_id(0) == grid[0] - 1) & (pl.program_id(1) == grid[1] - 1)
        if ng:
            @pl.when(first)
            def _():
                _gather_start(ins, outs, axes, *sems)

        r = _dot_nt(a_ref[...], b_ref[...])
        if scale_first is not None:
            r = r * jnp.where(pl.program_id(1) == 0, scale_first, 1.0)
        o_ref[...] = r.astype(out_dtype)

        if ng:
            @pl.when(last)
            def _():
                _gather_finish(ins, outs, axes, *sems)

    res = pl.pallas_call(
        body, name=name, grid=grid,
        in_specs=[pl.BlockSpec((tm, k), lambda i, j: (i, 0)), pl.BlockSpec((tn, k), lambda i, j: (row_block + j, 0))]
        + [HBM] * ng,
        out_specs=[pl.BlockSpec((tm, tn), lambda i, j: (i, j))] + [HBM] * ng,
        out_shape=[jax.ShapeDtypeStruct((m, n), out_dtype)] + _gathered_shapes(shards),
        scratch_shapes=_gather_sems(ng) if ng else [],
        compiler_params=_params(("arbitrary", "arbitrary") if ng else ("parallel", "parallel")),
    )(a, bt, *_in_hbm(*shards))
    return (res[0], res[1:]) if ng else res[0]


def _mm_tn(a, b, name, k, tm, tn, tk, chunks=1, m=None, m_off=0, swap=None, scatter=None):
    m = a.shape[1] if m is None else m
    n = b.shape[1]
    cw = n // chunks
    srcs, axes = swap if swap is not None else ([], [])
    ns = len(srcs)
    hs, sc = scatter if scatter is not None else ([], None)
    sc_in, sc_shapes, sc_sems = _scatter_operands(hs, sc) if scatter is not None else ([], [], [])
    nh, nq = len(hs), len(sc_in)
    grid = (m // tm, n // tn, k // tk)

    def body(a_ref, b_ref, *rest):
        s_refs, h_refs = rest[:ns], rest[ns:ns + nq]
        o_ref = rest[ns + nq]
        d_refs, q_refs = rest[ns + nq + 1:2 * ns + nq + 1], rest[2 * ns + nq + 1:2 * ns + 2 * nq + 1]
        sems = rest[2 * ns + 2 * nq + 1:]
        swap_sems, scatter_sems = (sems[:2], sems[2:]) if ns else ((), sems)
        ids = [pl.program_id(d) for d in range(3)]

        def scatter_plan():
            small_in = h_refs[nh] if sc is not None else None
            small_out = q_refs[nh] if sc is not None else None
            return _chip_scatter_plan(h_refs[:nh], small_in, q_refs[:nh], small_out, *scatter_sems)

        if ns or nq:
            @pl.when((ids[0] == 0) & (ids[1] == 0) & (ids[2] == 0))
            def _():
                if ns:
                    for cp in _swap_copies(s_refs, d_refs, axes, *swap_sems):
                        cp.start()
                if nq:
                    own, sends, _ = scatter_plan()
                    for cp in own + sends:
                        cp.start()

        @pl.when(ids[2] == 0)
        def _():
            o_ref[...] = jnp.zeros_like(o_ref)
        r = _dot_tn(a_ref[...].astype(BF16), b_ref[...].astype(BF16))
        if chunks > 1:
            for c in range(chunks):
                o_ref[c] += r[:, c * cw:(c + 1) * cw]
        else:
            o_ref[...] += r

        if ns or nq:
            @pl.when((ids[0] == grid[0] - 1) & (ids[1] == grid[1] - 1) & (ids[2] == grid[2] - 1))
            def _():
                if ns:
                    for cp in _swap_copies(s_refs, d_refs, axes, *swap_sems):
                        cp.wait()
                if nq:
                    own, sends, recvs = scatter_plan()
                    for cp in recvs:
                        cp.wait_recv()
                    for cp in sends:
                        cp.wait_send()
                    for cp in own:
                        cp.wait()

    if chunks > 1:
        assert tn == n
        out_spec = pl.BlockSpec((chunks, tm, cw), lambda i, j, kk: (0, i, 0))
        out_shape = jax.ShapeDtypeStruct((chunks, m, cw), F32)
    else:
        out_spec = pl.BlockSpec((tm, tn), lambda i, j, kk: (i, j))
        out_shape = jax.ShapeDtypeStruct((m, n), F32)
    riders = ns + nq
    res = pl.pallas_call(
        body, name=name, grid=grid,
        in_specs=[pl.BlockSpec((tk, tm), lambda i, j, kk: (kk, m_off + i)), pl.BlockSpec((tk, tn), lambda i, j, kk: (kk, j))]
        + [HBM] * riders,
        out_specs=[out_spec] + [HBM] * riders, out_shape=[out_shape] + _swap_shapes(srcs, axes) + sc_shapes,
        scratch_shapes=([pltpu.SemaphoreType.DMA((ns,)), pltpu.SemaphoreType.DMA((ns,))] if ns else []) + sc_sems,
        compiler_params=_params(("arbitrary",) * 3 if riders else ("parallel", "parallel", "arbitrary")),
    )(a, b, *_in_hbm(*srcs, *sc_in))
    if not riders:
        return res[0]
    out = [res[0]]
    if ns:
        out.append(res[1:1 + ns])
    if nq:
        out.append((res[1 + ns:1 + ns + nh], res[1 + ns + nh] if sc is not None else None))
    return tuple(out)


def _tokens_spec():
    return pl.BlockSpec((ROW_TILE, D_MODEL), lambda i: (jnp.maximum(i - 1, 0), 0))


def _rmsnorm_fwd(x2, g1, lk, shards, axes):
    nb = x2.shape[0] // ROW_TILE + 1
    nblk = lk // ROW_TILE
    ng = len(shards)
    meta_cols = shards[-1].shape[1]

    def body(x_ref, g_ref, *rest):
        ins, hn_ref, outs = rest[:ng], rest[ng], rest[ng + 1:2 * ng + 1]
        sems, meta_buf, meta_sem = rest[2 * ng + 1:2 * ng + 6], rest[2 * ng + 6], rest[2 * ng + 7]
        s = pl.program_id(0)
        blk = (s + 1) % nblk

        @pl.when(s == 0)
        def _():
            _gather_start(ins, outs, axes, *sems)

        def normed(h):
            r = lax.rsqrt(jnp.mean(h * h, axis=-1, keepdims=True) + RMS_EPS)
            return ((h * r) * g_ref[...]).astype(BF16)

        @pl.when(s < nblk - 1)
        def _():
            hn_ref[...] = normed(jnp.where(blk >= nb, 0.0, x_ref[...]))

        @pl.when(s == nblk - 1)
        def _():
            _gather_finish(ins, outs, axes, *sems)
            fetch = pltpu.make_async_copy(outs[-1], meta_buf, meta_sem.at[0])
            fetch.start()
            fetch.wait()
            meta = jnp.concatenate([meta_buf[j] for j in range(N_CHIPS)], axis=1)
            hn_ref[...] = normed(jnp.concatenate([jnp.zeros((PAD_ROWS, D_MODEL), F32), meta], axis=0))

    res = pl.pallas_call(
        body, name="rmsnorm_fwd", grid=(nblk,),
        in_specs=[pl.BlockSpec((ROW_TILE, D_MODEL), lambda s: (jnp.clip((s + 1) % nblk - 1, 0, nb - 2), 0)),
                  pl.BlockSpec((1, D_MODEL), lambda s: (0, 0))] + [HBM] * ng,
        out_specs=[pl.BlockSpec((ROW_TILE, D_MODEL), lambda s: ((s + 1) % nblk, 0))] + [HBM] * ng,
        out_shape=[jax.ShapeDtypeStruct((lk, D_MODEL), BF16)] + _gathered_shapes(shards),
        scratch_shapes=_gather_sems(ng) + [pltpu.VMEM((N_CHIPS, N_META, meta_cols), F32), pltpu.SemaphoreType.DMA((1,))],
        compiler_params=_params(("arbitrary",)),
    )(x2, g1, *_in_hbm(*shards))
    return res[0], res[1:]


def _valid_gate_mask(i):
    row = i * ROW_TILE + lax.broadcasted_iota(jnp.int32, (ROW_TILE, F_COLS), 0)
    col = lax.broadcasted_iota(jnp.int32, (ROW_TILE, F_COLS), 1)
    return (row >= PAD_ROWS) & (col < N_HEADS)


def _gates_fwd(f, bfg):
    nb = f.shape[0] // ROW_TILE

    def body(f_ref, b_ref, c_ref, carry):
        i = pl.program_id(0)

        @pl.when(i == 0)
        def _():
            carry[...] = jnp.zeros_like(carry)

        logit = f_ref[...] + b_ref[...]
        lf = jnp.minimum(logit, 0.0) - jnp.log1p(jnp.exp(-jnp.abs(logit)))
        lf = jnp.where(_valid_gate_mask(i), lf, 0.0)
        r_i = lax.broadcasted_iota(jnp.int32, (ROW_TILE, ROW_TILE), 0)
        c_i = lax.broadcasted_iota(jnp.int32, (ROW_TILE, ROW_TILE), 1)
        tri = (c_i <= r_i).astype(F32)
        c_ref[...] = jnp.dot(tri, lf, precision=HIGHEST, preferred_element_type=F32) + carry[...]
        carry[...] = carry[...] + jnp.sum(lf, axis=0, keepdims=True)

    return pl.pallas_call(
        body, name="gates_fwd", grid=(nb,),
        in_specs=[pl.BlockSpec((ROW_TILE, F_COLS), lambda i: (i, 0)), pl.BlockSpec((1, F_COLS), lambda i: (0, 0))],
        out_specs=pl.BlockSpec((ROW_TILE, F_COLS), lambda i: (i, 0)),
        out_shape=jax.ShapeDtypeStruct(f.shape, F32),
        scratch_shapes=[pltpu.VMEM((1, F_COLS), F32)],
        compiler_params=_params(("arbitrary",)),
    )(f, bfg)


def _pool_counts(i):
    row = i * ROW_TILE + lax.broadcasted_iota(jnp.int32, (ROW_TILE, 1), 0)
    return jnp.maximum(row - PAD_ROWS, 0)


def _trailing_sums(xc, levels):
    acc = xc
    for lv in range(levels):
        acc = acc + pltpu.roll(acc, 1 << lv, 0)
    return acc


def _leading_sums(xc, levels):
    n = xc.shape[0]
    acc = xc
    for lv in range(levels):
        acc = acc + pltpu.roll(acc, n - (1 << lv), 0)
    return acc


def _pool_p(u_cur, u_prev, i):
    pos = _pool_counts(i)
    ps, invs = [], []
    for g, w in enumerate(POOL_WINDOWS):
        sl = slice(g * POOL_GROUP, (g + 1) * POOL_GROUP)
        cur = u_cur[:, sl]
        xc = jnp.concatenate([u_prev[:, sl], cur], axis=0)
        win = _trailing_sums(xc, g + 1)[ROW_TILE:, :]
        inv = 1.0 / jnp.minimum(pos + 1, w).astype(F32)
        ps.append(win * inv - cur)
        invs.append(inv)
    return ps, invs


def _pool_fwd(e, pw, scale):
    nb = e.shape[0] // ROW_TILE

    def body(uc_ref, up_ref, z_ref, pw_ref, sc_ref, y_ref):
        i = pl.program_id(0)
        u_cur = uc_ref[...].astype(F32)
        u_prev = jnp.where(i == 0, 0.0, up_ref[...].astype(F32))
        ps, _ = _pool_p(u_cur, u_prev, i)
        z = z_ref[...].astype(F32)
        gate = z * _sigmoid(z)
        for g in range(len(POOL_WINDOWS)):
            sl = slice(g * POOL_GROUP, (g + 1) * POOL_GROUP)
            yraw = jnp.dot(ps[g].astype(BF16), pw_ref[g], preferred_element_type=F32)
            y_ref[:, sl] = ((yraw * sc_ref[:, sl]) * gate[:, sl]).astype(BF16)

    blk = (ROW_TILE, POOL_WIDTH)
    return pl.pallas_call(
        body, name="pool_fwd", grid=(nb,),
        in_specs=[pl.BlockSpec(blk, lambda i: (i, 0)), pl.BlockSpec(blk, lambda i: (jnp.maximum(i - 1, 0), 0)),
                  pl.BlockSpec(blk, lambda i: (i, 1)),
                  pl.BlockSpec((len(POOL_WINDOWS), POOL_GROUP, POOL_GROUP), lambda i: (0, 0, 0)),
                  pl.BlockSpec((1, POOL_WIDTH), lambda i: (0, 0))],
        out_specs=pl.BlockSpec(blk, lambda i: (i, 0)),
        out_shape=jax.ShapeDtypeStruct((e.shape[0], POOL_WIDTH), BF16),
        compiler_params=_params(("parallel",)),
    )(e, e, e, pw, scale)


def _stack_heads(a):
    first = lax.broadcasted_iota(jnp.int32, a.shape, 1) < HEAD_DIM
    zero = jnp.zeros_like(a)
    return jnp.concatenate([jnp.where(first, a, zero), jnp.where(first, zero, a)], axis=0)


def _unstack_heads(a):
    rows = a.shape[0] // 2
    first = lax.broadcasted_iota(jnp.int32, (rows, LANES), 1) < HEAD_DIM
    return jnp.where(first, a[:rows], a[rows:])


def _causal(i, jj, stacked, kv_tile):
    r = lax.broadcasted_iota(jnp.int32, (stacked * ROW_TILE, kv_tile), 0)
    if stacked == 2:
        r = jnp.where(r >= ROW_TILE, r - ROW_TILE, r)
    kidx = jj * kv_tile + lax.broadcasted_iota(jnp.int32, (stacked * ROW_TILE, kv_tile), 1)
    return kidx <= i * ROW_TILE + r


def _stack_rows(b):
    n = b.shape[1]
    return jnp.concatenate([jnp.broadcast_to(b[0:1], (ROW_TILE, n)), jnp.broadcast_to(b[1:2], (ROW_TILE, n))], axis=0)


def _attn_fwd(qkv, bias, nb):
    lk = qkv.shape[0]
    lp = nb * ROW_TILE
    nkb = lk // KV_TILE
    n_pairs = N_HEADS // 2

    def body(q_ref, k_ref, v_ref, b_ref, o_ref, lse_ref):
        i = pl.program_id(1)
        qs = _stack_heads(q_ref[...])
        last = (i * ROW_TILE) // KV_TILE

        def block(jj, carry, masked, n_keys=KV_TILE):
            m, l, acc = carry
            rows = pl.ds(pl.multiple_of(jj * KV_TILE, KV_TILE), n_keys)
            s = _dot_nt(qs, k_ref[rows, :]) - _stack_rows(b_ref[0, 0, :, rows])
            if masked:
                s = jnp.where(_causal(i, jj, 2, KV_TILE)[:, :n_keys], s, NEG)
            m_new = jnp.maximum(m, jnp.max(s, axis=1, keepdims=True))
            alpha = jnp.exp(m - m_new)
            p = jnp.exp(s - m_new)
            l = alpha * l + jnp.sum(p, axis=1, keepdims=True)
            acc = alpha * acc + jnp.dot(p.astype(BF16), v_ref[rows, :], preferred_element_type=F32)
            return m_new, l, acc

        init = (jnp.full((2 * ROW_TILE, 1), NEG, F32), jnp.zeros((2 * ROW_TILE, 1), F32),
                jnp.zeros((2 * ROW_TILE, LANES), F32))
        def finish(carry):
            m, l, acc = carry
            o_ref[...] = _unstack_heads(acc / l)
            lse_ref[...] = _unstack_heads(jnp.broadcast_to(m + jnp.log(l), (2 * ROW_TILE, LANES)))

        carry = lax.fori_loop(0, last, lambda jj, c: block(jj, c, False), init)
        ends = [lambda c, n=(r + 1) * ROW_TILE: finish(block(last, c, True, n)) for r in range(KV_TILE // ROW_TILE)]
        lax.switch(i - last * (KV_TILE // ROW_TILE), ends, carry)

    return pl.pallas_call(
        body, name="attn_fwd", grid=(n_pairs, nb),
        in_specs=[pl.BlockSpec((ROW_TILE, LANES), lambda hp, i: (i, hp)),
                  pl.BlockSpec((lk, LANES), lambda hp, i: (0, n_pairs + hp)),
                  pl.BlockSpec((lk, LANES), lambda hp, i: (0, 2 * n_pairs + hp)),
                  pl.BlockSpec((1, 1, 2, lk), lambda hp, i: (i, hp, 0, 0))],
        out_specs=[pl.BlockSpec((ROW_TILE, LANES), lambda hp, i: (i, hp)),
                   pl.BlockSpec((ROW_TILE, LANES), lambda hp, i: (i, hp))],
        out_shape=[jax.ShapeDtypeStruct((lp, ATTN_WIDTH), F32), jax.ShapeDtypeStruct((lp, ATTN_WIDTH), F32)],
        compiler_params=_params(("parallel", "parallel")),
    )(qkv, qkv, qkv, bias)


def _merge_fwd(y_pool, o, e, wup_p, wup_a):
    lp = o.shape[0]
    nb = lp // ROW_TILE

    def body(yp_ref, o_ref, e_ref, wp_ref, wa_ref, mg_ref, ya_ref):
        za = e_ref[:, E_ZA:E_GP].astype(F32)
        ya = (o_ref[...] * (za * _sigmoid(za))).astype(BF16)
        ya_ref[...] = ya
        a_pool = jnp.dot(yp_ref[...], wp_ref[...], preferred_element_type=F32)
        a_attn = jnp.dot(ya, wa_ref[...], preferred_element_type=F32)
        mg_ref[...] = (_sigmoid(e_ref[:, E_GP:E_GA].astype(F32)) * a_pool
                       + _sigmoid(e_ref[:, E_GA:E_COLS].astype(F32)) * a_attn).astype(BF16)

    return pl.pallas_call(
        body, name="merge_fwd", grid=(nb,),
        in_specs=[pl.BlockSpec((ROW_TILE, POOL_WIDTH), lambda i: (i, 0)),
                  pl.BlockSpec((ROW_TILE, ATTN_WIDTH), lambda i: (i, 0)),
                  pl.BlockSpec((ROW_TILE, E_COLS), lambda i: (i, 0)),
                  pl.BlockSpec((POOL_WIDTH, D_MODEL), lambda i: (0, 0)),
                  pl.BlockSpec((ATTN_WIDTH, D_MODEL), lambda i: (0, 0))],
        out_specs=[pl.BlockSpec((ROW_TILE, D_MODEL), lambda i: (i, 0)),
                   pl.BlockSpec((ROW_TILE, ATTN_WIDTH), lambda i: (i, 0))],
        out_shape=[jax.ShapeDtypeStruct((lp, D_MODEL), BF16), jax.ShapeDtypeStruct((lp, ATTN_WIDTH), BF16)],
        compiler_params=_params(("parallel",)),
    )(y_pool, o, e, wup_p, wup_a)


def _head_fwd_bwd(merged, w_out, x2, metapad, gf, target):
    lp = merged.shape[0]
    nb = lp // ROW_TILE

    def body(mg_ref, w_ref, x_ref, mp_ref, g_ref, t_ref, dh_ref, loss_ref, dg_ref):
        i = pl.program_id(0)

        @pl.when(i == 0)
        def _():
            loss_ref[...] = jnp.zeros_like(loss_ref)
            dg_ref[...] = jnp.zeros_like(dg_ref)

        h0 = jnp.where(i == 0, mp_ref[...], x_ref[...])
        h1 = h0 + jnp.dot(mg_ref[...], w_ref[...], preferred_element_type=F32)
        r = lax.rsqrt(jnp.mean(h1 * h1, axis=-1, keepdims=True) + RMS_EPS)
        xhat = h1 * r
        g = g_ref[...]
        err = jnp.where(i == 0, 0.0, xhat * g - t_ref[...])
        loss_ref[...] += 0.5 * jnp.sum(jnp.mean(err * err, axis=-1, keepdims=True))
        dy = err / D_MODEL
        dg_ref[...] += jnp.sum(dy * xhat, axis=0, keepdims=True)
        dxhat = dy * g
        dh_ref[...] = r * (dxhat - xhat * jnp.mean(dxhat * xhat, axis=-1, keepdims=True))

    return pl.pallas_call(
        body, name="head_fwd_bwd", grid=(nb,),
        in_specs=[pl.BlockSpec((ROW_TILE, D_MODEL), lambda i: (i, 0)),
                  pl.BlockSpec((D_MODEL, D_MODEL), lambda i: (0, 0)),
                  _tokens_spec(), pl.BlockSpec((ROW_TILE, D_MODEL), lambda i: (0, 0)),
                  pl.BlockSpec((1, D_MODEL), lambda i: (0, 0)), _tokens_spec()],
        out_specs=[pl.BlockSpec((ROW_TILE, D_MODEL), lambda i: (i, 0)),
                   pl.BlockSpec((1, LANES), lambda i: (0, 0)), pl.BlockSpec((1, D_MODEL), lambda i: (0, 0))],
        out_shape=[jax.ShapeDtypeStruct((lp, D_MODEL), F32), jax.ShapeDtypeStruct((1, LANES), F32),
                   jax.ShapeDtypeStruct((1, D_MODEL), F32)],
        compiler_params=_params(("arbitrary",)),
    )(merged, w_out, x2, metapad, gf, target)


def _per_head_rowsum(t):
    head = lax.broadcasted_iota(jnp.int32, t.shape, 1) // HEAD_DIM
    out = jnp.zeros_like(t)
    for h in range(N_HEADS):
        sel = head == h
        out = jnp.where(sel, jnp.sum(jnp.where(sel, t, 0.0), axis=1, keepdims=True), out)
    return out


def _merge_bwd(dh1, w_out, y_pool, y_attn, wup_p, wup_a, e, o):
    lp = o.shape[0]
    nb = lp // ROW_TILE

    def body(dh_ref, wo_ref, yp_ref, ya_ref, wp_ref, wa_ref, e_ref, o_ref,
             dap_ref, daa_ref, dg_ref, do_ref, delta_ref, dyp_ref):
        dmerged = _dot_nt(dh_ref[...].astype(BF16), wo_ref[...])
        a_pool = jnp.dot(yp_ref[...], wp_ref[...], preferred_element_type=F32)
        a_attn = jnp.dot(ya_ref[...], wa_ref[...], preferred_element_type=F32)
        sp = _sigmoid(e_ref[:, E_GP:E_GA].astype(F32))
        sa = _sigmoid(e_ref[:, E_GA:E_COLS].astype(F32))
        dap = (dmerged * sp).astype(BF16)
        daa = (dmerged * sa).astype(BF16)
        dap_ref[...] = dap
        daa_ref[...] = daa
        dg_ref[:, ATTN_WIDTH:ATTN_WIDTH + D_MODEL] = (dmerged * a_pool * (sp * (1.0 - sp))).astype(BF16)
        dg_ref[:, ATTN_WIDTH + D_MODEL:] = (dmerged * a_attn * (sa * (1.0 - sa))).astype(BF16)
        dyp_ref[...] = _dot_nt(dap, wp_ref[...])
        dya = _dot_nt(daa, wa_ref[...])
        za = e_ref[:, E_ZA:E_GP].astype(F32)
        sz = _sigmoid(za)
        o = o_ref[...]
        do = dya * (za * sz)
        do_ref[...] = do.astype(BF16)
        dg_ref[:, :ATTN_WIDTH] = (dya * o * (sz * (1.0 + za * (1.0 - sz)))).astype(BF16)
        delta_ref[...] = _per_head_rowsum(do * o)

    row = lambda w: pl.BlockSpec((ROW_TILE, w), lambda i: (i, 0))
    full = lambda a: pl.BlockSpec(a.shape, lambda i: (0, 0))
    return pl.pallas_call(
        body, name="merge_bwd", grid=(nb,),
        in_specs=[row(D_MODEL), full(w_out), row(POOL_WIDTH), row(ATTN_WIDTH), full(wup_p), full(wup_a),
                  row(E_COLS), row(ATTN_WIDTH)],
        out_specs=[row(D_MODEL), row(D_MODEL), pl.BlockSpec((ROW_TILE, MAIN_COLS // 2), lambda i: (i, 1)),
                   row(ATTN_WIDTH), row(ATTN_WIDTH), row(POOL_WIDTH)],
        out_shape=[jax.ShapeDtypeStruct((lp, D_MODEL), BF16), jax.ShapeDtypeStruct((lp, D_MODEL), BF16),
                   jax.ShapeDtypeStruct((lp, MAIN_COLS), BF16),
                   jax.ShapeDtypeStruct((lp, ATTN_WIDTH), BF16), jax.ShapeDtypeStruct((lp, ATTN_WIDTH), F32),
                   jax.ShapeDtypeStruct((lp, POOL_WIDTH), F32)],
        compiler_params=_params(("parallel",)),
    )(dh1, w_out, y_pool, y_attn, wup_p, wup_a, e, o)


def _fill_dproj(dproj_half, du, dzp, dq, dk, dv):
    lp = dproj_half.shape[0]
    nb = lp // ROW_TILE

    def body(base_ref, du_ref, dzp_ref, dq_ref, dk_ref, dv_ref, o_ref):
        parts = [du_ref[...], dzp_ref[...], (dq_ref[...] * HEAD_DIM ** -0.5).astype(BF16), dk_ref[...], dv_ref[...]]
        for t, p in enumerate(parts):
            o_ref[:, t * ATTN_WIDTH:(t + 1) * ATTN_WIDTH] = p

    blk = pl.BlockSpec((ROW_TILE, ATTN_WIDTH), lambda i: (i, 0))
    return pl.pallas_call(
        body, name="fill_dproj", grid=(nb,),
        in_specs=[pl.BlockSpec(memory_space=pl.ANY)] + [blk] * 5,
        out_specs=pl.BlockSpec((ROW_TILE, MAIN_COLS // 2), lambda i: (i, 0)),
        out_shape=jax.ShapeDtypeStruct(dproj_half.shape, BF16),
        input_output_aliases={0: 0},
        compiler_params=_params(("parallel",)),
    )(dproj_half, du, dzp, dq, dk, dv)


def _pool_bwd_local(e, dy_pool, pw, scale):
    lp = e.shape[0]
    nb = lp // ROW_TILE
    ng = len(POOL_WINDOWS)

    def body(uc_ref, up_ref, z_ref, dy_ref, pw_ref, sc_ref, dpc_ref, dz_ref, dsc_ref, dpw_ref):
        i = pl.program_id(0)

        @pl.when(i == 0)
        def _():
            dsc_ref[...] = jnp.zeros_like(dsc_ref)
            dpw_ref[...] = jnp.zeros_like(dpw_ref)

        u_cur = uc_ref[...].astype(F32)
        u_prev = jnp.where(i == 0, 0.0, up_ref[...].astype(F32))
        ps, invs = _pool_p(u_cur, u_prev, i)
        z = z_ref[...].astype(F32)
        sz = _sigmoid(z)
        dy = dy_ref[...]
        dypre = dy * (z * sz)
        dsilu = sz * (1.0 + z * (1.0 - sz))
        for g in range(ng):
            sl = slice(g * POOL_GROUP, (g + 1) * POOL_GROUP)
            pb = ps[g].astype(BF16)
            w = pw_ref[g]
            yraw = jnp.dot(pb, w, preferred_element_type=F32)
            sc = sc_ref[:, sl]
            dz_ref[:, sl] = (dy[:, sl] * (yraw * sc) * dsilu[:, sl]).astype(BF16)
            dsc_ref[:, sl] += jnp.sum(dypre[:, sl] * yraw, axis=0, keepdims=True)
            dyraw = (dypre[:, sl] * sc).astype(BF16)
            dpw_ref[g] += _dot_tn(pb, dyraw)
            dpc_ref[:, sl] = _dot_nt(dyraw, w) * invs[g]

    blk = (ROW_TILE, POOL_WIDTH)
    return pl.pallas_call(
        body, name="pool_bwd_local", grid=(nb,),
        in_specs=[pl.BlockSpec(blk, lambda i: (i, 0)), pl.BlockSpec(blk, lambda i: (jnp.maximum(i - 1, 0), 0)),
                  pl.BlockSpec(blk, lambda i: (i, 1)), pl.BlockSpec(blk, lambda i: (i, 0)),
                  pl.BlockSpec((ng, POOL_GROUP, POOL_GROUP), lambda i: (0, 0, 0)),
                  pl.BlockSpec((1, POOL_WIDTH), lambda i: (0, 0))],
        out_specs=[pl.BlockSpec(blk, lambda i: (i, 0)), pl.BlockSpec(blk, lambda i: (i, 0)),
                   pl.BlockSpec((1, POOL_WIDTH), lambda i: (0, 0)),
                   pl.BlockSpec((ng, POOL_GROUP, POOL_GROUP), lambda i: (0, 0, 0))],
        out_shape=[jax.ShapeDtypeStruct((lp, POOL_WIDTH), F32), jax.ShapeDtypeStruct((lp, POOL_WIDTH), BF16),
                   jax.ShapeDtypeStruct((1, POOL_WIDTH), F32),
                   jax.ShapeDtypeStruct((ng, POOL_GROUP, POOL_GROUP), F32)],
        compiler_params=_params(("arbitrary",)),
    )(e, e, e, dy_pool, pw, scale)


def _pool_bwd_window(dpc):
    lp = dpc.shape[0]
    nb = lp // ROW_TILE

    def body(cur_ref, nxt_ref, du_ref):
        i = pl.program_id(0)
        cur = cur_ref[...]
        nxt = jnp.where(i == nb - 1, 0.0, nxt_ref[...])
        pos = _pool_counts(i)
        for g, w in enumerate(POOL_WINDOWS):
            sl = slice(g * POOL_GROUP, (g + 1) * POOL_GROUP)
            xc = jnp.concatenate([cur[:, sl], nxt[:, sl]], axis=0)
            win = _leading_sums(xc, g + 1)[:ROW_TILE, :]
            dp = cur[:, sl] * jnp.minimum(pos + 1, w).astype(F32)
            du_ref[:, sl] = (win - dp).astype(BF16)

    blk = (ROW_TILE, POOL_WIDTH)
    return pl.pallas_call(
        body, name="pool_bwd_window", grid=(nb,),
        in_specs=[pl.BlockSpec(blk, lambda i: (i, 0)), pl.BlockSpec(blk, lambda i: (jnp.minimum(i + 1, nb - 1), 0))],
        out_specs=pl.BlockSpec(blk, lambda i: (i, 0)),
        out_shape=jax.ShapeDtypeStruct((lp, POOL_WIDTH), BF16),
        compiler_params=_params(("parallel",)),
    )(dpc, dpc)


def _attn_bwd(qkv, do, lse, delta, bias, nb):
    lk = qkv.shape[0]
    lp = nb * ROW_TILE
    nkb = lk // KV_TILE_BWD
    n_pairs = N_HEADS // 2
    per_kv = KV_TILE_BWD // ROW_TILE

    def body(q_ref, k_ref, v_ref, do_ref, lse_ref, dl_ref, b_ref, dq_ref, dk_ref, dv_ref, dc_ref, dcq_ref,
             dk_acc, dv_acc, dc_acc):
        jj = pl.program_id(1)

        @pl.when(jj == 0)
        def _():
            dq_ref[...] = jnp.zeros_like(dq_ref)
            dcq_ref[...] = jnp.zeros_like(dcq_ref)

        dk_acc[...] = jnp.zeros_like(dk_acc)
        dv_acc[...] = jnp.zeros_like(dv_acc)
        dc_acc[...] = jnp.zeros_like(dc_acc)

        def block(i, n_keys, masked):
            kb, vb = k_ref[:n_keys, :], v_ref[:n_keys, :]
            rows = pl.ds(pl.multiple_of(i * ROW_TILE, ROW_TILE), ROW_TILE)
            qs = _stack_heads(q_ref[rows, :])
            dos = _stack_heads(do_ref[rows, :])
            lse_i, dl_i = lse_ref[rows, :], dl_ref[rows, :]
            s = _dot_nt(qs, kb)
            dp = _dot_nt(dos, vb)
            if masked:
                valid = _causal(i, jj, 1, KV_TILE_BWD)[:, :n_keys]
            ps, dss, dcs, rowsums = [], [], [], []
            for hd in range(2):
                half = slice(hd * ROW_TILE, (hd + 1) * ROW_TILE)
                col = slice(hd * HEAD_DIM, hd * HEAD_DIM + 1)
                sh = s[half] - b_ref[i, 0, hd:hd + 1, :n_keys]
                if masked:
                    sh = jnp.where(valid, sh, NEG)
                p = jnp.exp(sh - lse_i[:, col])
                ds = p * (dp[half] - dl_i[:, col])
                ps.append(p.astype(BF16))
                dss.append(ds.astype(BF16))
                dcs.append(jnp.sum(ds, axis=0, keepdims=True))
                rowsums.append(jnp.sum(ds, axis=1, keepdims=True))
            dsb = jnp.concatenate(dss, axis=0)
            dv_acc[:n_keys, :] += _dot_tn(jnp.concatenate(ps, axis=0), dos)
            dk_acc[:n_keys, :] += _dot_tn(dsb, qs)
            dc_acc[:, :n_keys] -= jnp.concatenate(dcs, axis=0)
            dq_ref[rows, :] += _unstack_heads(jnp.dot(dsb, kb, preferred_element_type=F32))
            dcq_ref[rows, :] += _unstack_heads(jnp.broadcast_to(jnp.concatenate(rowsums, axis=0), (2 * ROW_TILE, LANES)))

        first_q = per_kv * jj
        for r in range(per_kv):
            @pl.when(first_q + r < nb)
            def _():
                block(first_q + r, (r + 1) * ROW_TILE, True)

        def rest(i, carry):
            block(i, KV_TILE_BWD, False)
            return carry

        lax.fori_loop(jnp.minimum(first_q + per_kv, nb), nb, rest, 0)
        dk_ref[...] = dk_acc[...].astype(BF16)
        dv_ref[...] = dv_acc[...].astype(BF16)
        dc_ref[0, 0] = dc_acc[...]

    whole = lambda rows: pl.BlockSpec((rows, LANES), lambda hp, jj: (0, hp))
    kv_blk = lambda off: pl.BlockSpec((KV_TILE_BWD, LANES), lambda hp, jj: (jj, off + hp))
    return pl.pallas_call(
        body, name="attn_bwd", grid=(n_pairs, nkb),
        in_specs=[whole(lk), kv_blk(n_pairs), kv_blk(2 * n_pairs), whole(lp), whole(lp), whole(lp),
                  pl.BlockSpec((nb, 1, 2, KV_TILE_BWD), lambda hp, jj: (0, hp, 0, jj))],
        out_specs=[whole(lp), kv_blk(0), kv_blk(0),
                   pl.BlockSpec((1, 1, 2, KV_TILE_BWD), lambda hp, jj: (hp, jj, 0, 0)), whole(lp)],
        out_shape=[jax.ShapeDtypeStruct((lp, ATTN_WIDTH), F32), jax.ShapeDtypeStruct((lk, ATTN_WIDTH), BF16),
                   jax.ShapeDtypeStruct((lk, ATTN_WIDTH), BF16),
                   jax.ShapeDtypeStruct((n_pairs, nkb, 2, KV_TILE_BWD), F32),
                   jax.ShapeDtypeStruct((lp, ATTN_WIDTH), F32)],
        scratch_shapes=[pltpu.VMEM((KV_TILE_BWD, LANES), F32), pltpu.VMEM((KV_TILE_BWD, LANES), F32),
                        pltpu.VMEM((2, KV_TILE_BWD), F32)],
        compiler_params=_params(("parallel", "arbitrary")),
    )(qkv, qkv, qkv, do, lse, delta, bias)


def _gates_bwd(dc, dcq, f, bfg):
    nb = f.shape[0] // ROW_TILE

    def body(dc_ref, dcq_ref, f_ref, b_ref, df_ref, db_ref, carry):
        step = pl.program_id(0)
        i = nb - 1 - step

        @pl.when(step == 0)
        def _():
            carry[...] = jnp.zeros_like(carry)
            db_ref[...] = jnp.zeros_like(db_ref)

        dcb = dc_ref[...]
        lane = lax.broadcasted_iota(jnp.int32, (ROW_TILE, F_COLS), 1)
        for h in range(N_HEADS):
            dcb = dcb + jnp.where(lane == h, dcq_ref[:, HEAD_DIM * h:HEAD_DIM * h + 1], 0.0)
        r_i = lax.broadcasted_iota(jnp.int32, (ROW_TILE, ROW_TILE), 0)
        c_i = lax.broadcasted_iota(jnp.int32, (ROW_TILE, ROW_TILE), 1)
        upper = (c_i >= r_i).astype(F32)
        dlf = jnp.dot(upper, dcb, precision=HIGHEST, preferred_element_type=F32) + carry[...]
        carry[...] = carry[...] + jnp.sum(dcb, axis=0, keepdims=True)
        logit = f_ref[...] + b_ref[...]
        dlogit = jnp.where(_valid_gate_mask(i), dlf * _sigmoid(-logit), 0.0)
        df_ref[...] = dlogit.astype(BF16)
        db_ref[...] += jnp.sum(dlogit, axis=0, keepdims=True)

    blk = pl.BlockSpec((ROW_TILE, F_COLS), lambda s: (nb - 1 - s, 0))
    wide = pl.BlockSpec((ROW_TILE, ATTN_WIDTH), lambda s: (nb - 1 - s, 0))
    one = pl.BlockSpec((1, F_COLS), lambda s: (0, 0))
    return pl.pallas_call(
        body, name="gates_bwd", grid=(nb,),
        in_specs=[blk, wide, blk, one], out_specs=[blk, one],
        out_shape=[jax.ShapeDtypeStruct(f.shape, BF16), jax.ShapeDtypeStruct((1, F_COLS), F32)],
        scratch_shapes=[pltpu.VMEM((1, F_COLS), F32)],
        compiler_params=_params(("arbitrary",)),
    )(dc, dcq, f, bfg)


def _input_bwd(dproj, df, wt_e, wt_qkv, wt_f, x2, metapad, dh1, g1, hs, sc=None):
    nb = x2.shape[0] // ROW_TILE + 1
    n = len(hs)
    sc_in, sc_shapes, sc_sems = _scatter_operands(hs, sc)
    nq = len(sc_in)

    def body(dp_ref, df_ref, we_ref, w_ref, wf_ref, x_ref, mp_ref, dh_ref, g_ref, *rest):
        h_refs, s_ref = rest[:n], (rest[n] if sc is not None else None)
        gx_ref, g0_ref, dg_ref = rest[nq:nq + 3]
        q_refs, sq_ref = rest[nq + 3:nq + 3 + n], (rest[nq + 3 + n] if sc is not None else None)
        sems = rest[2 * nq + 3:]
        i = pl.program_id(0)

        @pl.when(i == 0)
        def _():
            dg_ref[...] = jnp.zeros_like(dg_ref)
            own, sends, _ = _chip_scatter_plan(h_refs, s_ref, q_refs, sq_ref, *sems)
            for cp in own + sends:
                cp.start()

        dhn = (jnp.dot(dp_ref[:, :E_ZA], we_ref[:E_ZA, :], preferred_element_type=F32)
               + jnp.dot(dp_ref[:, E_ZA:MAIN_COLS // 2], w_ref[...], preferred_element_type=F32)
               + jnp.dot(dp_ref[:, MAIN_COLS // 2:], we_ref[E_ZA:, :], preferred_element_type=F32)
               + jnp.dot(df_ref[...], wf_ref[...], preferred_element_type=F32))
        h0 = jnp.where(i == 0, mp_ref[...], x_ref[...])
        r = lax.rsqrt(jnp.mean(h0 * h0, axis=-1, keepdims=True) + RMS_EPS)
        xhat = h0 * r
        dg_ref[...] += jnp.sum(dhn * xhat, axis=0, keepdims=True)
        dxhat = dhn * g_ref[...]
        dh0 = dh_ref[...] + r * (dxhat - xhat * jnp.mean(dxhat * xhat, axis=-1, keepdims=True))
        gx_ref[...] = dh0

        @pl.when(i == 0)
        def _():
            g0_ref[...] = dh0

        @pl.when(i == nb - 1)
        def _():
            own, sends, recvs = _chip_scatter_plan(h_refs, s_ref, q_refs, sq_ref, *sems)
            for cp in recvs:
                cp.wait_recv()
            for cp in sends:
                cp.wait_send()
            for cp in own:
                cp.wait()

    const = lambda shape: pl.BlockSpec(shape, lambda i: (0, 0))
    res = pl.pallas_call(
        body, name="input_bwd", grid=(nb,),
        in_specs=[pl.BlockSpec((ROW_TILE, MAIN_COLS), lambda i: (i, 0)), pl.BlockSpec((ROW_TILE, F_COLS), lambda i: (i, 0)),
                  const((E_COLS, D_MODEL)), const((QKV_COLS, D_MODEL)), const((F_COLS, D_MODEL)),
                  _tokens_spec(), const((ROW_TILE, D_MODEL)),
                  pl.BlockSpec((ROW_TILE, D_MODEL), lambda i: (i, 0)), const((1, D_MODEL))] + [HBM] * nq,
        out_specs=[_tokens_spec(), const((ROW_TILE, D_MODEL)), const((1, D_MODEL))] + [HBM] * nq,
        out_shape=[jax.ShapeDtypeStruct(x2.shape, F32), jax.ShapeDtypeStruct((ROW_TILE, D_MODEL), F32),
                   jax.ShapeDtypeStruct((1, D_MODEL), F32)] + sc_shapes,
        scratch_shapes=sc_sems,
        compiler_params=_params(("arbitrary",), vmem=56 * 1024 * 1024),
    )(dproj, df, wt_e, wt_qkv, wt_f, x2, metapad, dh1, g1, *_in_hbm(*sc_in))
    return res[:3], res[3:3 + n], (res[3 + n] if sc is not None else None)


def _adamw(w, g, m, v, name):
    rows, cols = w.shape
    if rows % 8 == 0:
        tr, tc = _row_tile8(rows), cols
    else:
        tr, tc = rows, (2 * LANES if cols % (2 * LANES) == 0 and rows > 8 else cols)

    def body(w_ref, g_ref, m_ref, v_ref, d_ref, mo_ref, vo_ref):
        g_ = g_ref[...]
        m_new = ADAM_B1 * m_ref[...] + (1.0 - ADAM_B1) * g_
        v_new = ADAM_B2 * v_ref[...] + (1.0 - ADAM_B2) * (g_ * g_)
        m_hat = m_new / (1.0 - ADAM_B1 ** ADAM_STEP)
        v_hat = v_new / (1.0 - ADAM_B2 ** ADAM_STEP)
        d_ref[...] = -ADAM_LR * (m_hat / (jnp.sqrt(v_hat) + ADAM_EPS) + ADAM_WD * w_ref[...])
        mo_ref[...] = m_new
        vo_ref[...] = v_new

    blk = pl.BlockSpec((tr, tc), lambda i, j: (i, j))
    return pl.pallas_call(
        body, name=name, grid=(rows // tr, cols // tc),
        in_specs=[blk] * 4, out_specs=[blk] * 3,
        out_shape=[jax.ShapeDtypeStruct(w.shape, F32)] * 3,
        compiler_params=_params(("parallel", "parallel")),
    )(w, g, m, v)


def _adamw_native(w3, g3, m3, v3, name):
    rows = w3.shape[0]
    tr = rows // 2
    blk = pl.BlockSpec((tr,) + w3.shape[1:], lambda i: (i, 0, 0))
    shape = jax.ShapeDtypeStruct(w3.shape, F32)

    def moments(g_ref, m_ref, v_ref, mo_ref, vo_ref):
        g_ = g_ref[...]
        mo_ref[...] = ADAM_B1 * m_ref[...] + (1.0 - ADAM_B1) * g_
        vo_ref[...] = ADAM_B2 * v_ref[...] + (1.0 - ADAM_B2) * (g_ * g_)

    new_m, new_v = pl.pallas_call(
        moments, name=name + "_moments", grid=(2,), in_specs=[blk] * 3, out_specs=[blk] * 2, out_shape=[shape] * 2,
        compiler_params=_params(("parallel",)),
    )(g3, m3, v3)

    def delta(w_ref, m_ref, v_ref, d_ref):
        m_hat = m_ref[...] / (1.0 - ADAM_B1 ** ADAM_STEP)
        v_hat = v_ref[...] / (1.0 - ADAM_B2 ** ADAM_STEP)
        d_ref[...] = -ADAM_LR * (m_hat / (jnp.sqrt(v_hat) + ADAM_EPS) + ADAM_WD * w_ref[...])

    d = pl.pallas_call(
        delta, name=name + "_delta", grid=(2,), in_specs=[blk] * 3, out_specs=blk, out_shape=shape,
        compiler_params=_params(("parallel",)),
    )(w3, new_m, new_v)
    return d, new_m, new_v


def _row_tile8(rows):
    best = rows
    for t in range(8, 257, 8):
        if rows % t == 0:
            best = t
    return best


def kernel(x, meta_tokens, norm_g, w_in, b_forget, pool_w, pool_scale, w_up_pool, w_up_attn, w_out, final_norm_g, loss_target, m_meta_tokens, m_norm_g, m_w_in, m_b_forget, m_pool_w, m_pool_scale, m_w_up_pool, m_w_up_attn, m_w_out, m_final_norm_g, v_meta_tokens, v_norm_g, v_w_in, v_b_forget, v_pool_w, v_pool_scale, v_w_up_pool, v_w_up_attn, v_w_out, v_final_norm_g):
    seq = x.shape[1]
    assert seq % ROW_TILE == 0 and x.shape[0] == 1
    lp = seq + ROW_TILE
    nb = lp // ROW_TILE
    lk = -(-lp // KV_TILE_BWD) * KV_TILE_BWD
    core = jnp.reshape(lax.axis_index("c"), (1,)).astype(jnp.int32)
    x2 = x[0]
    target = loss_target[0]
    sh_d = D_MODEL // N_CHIPS

    to_rows = lambda a: jnp.transpose(a, (2, 0, 1))
    from_rows = lambda a: jnp.transpose(a, (1, 2, 0))
    gf = final_norm_g.reshape(1, D_MODEL)
    bfg = jnp.pad(b_forget, ((0, 0), (0, F_COLS - N_HEADS)))
    pw_b = pool_w[0].astype(BF16)
    hn, (wg_in, meta_g) = _rmsnorm_fwd(x2, norm_g, lk, [jnp.transpose(w_in[0]).astype(BF16), meta_tokens], [1, 0])
    wt_e, wt_qkv, wt_f = _weight_sections(wg_in)
    meta_full = jnp.transpose(meta_g, (1, 0, 2)).reshape(N_META, D_MODEL)
    metapad = jnp.pad(meta_full, ((PAD_ROWS, 0), (0, 0)))

    tm = _row_tile(lp, 2200)
    e, (wg_up_p, wg_up_a, wg_out) = _mm_nt(
        hn, wt_e, BF16, "in_proj_gates", lp, E_COLS, tm, 512,
        gather=([w_up_pool[0].astype(BF16), w_up_attn[0].astype(BF16), w_out[0].astype(BF16)], [0, 0, 0]))
    wup_p = jnp.transpose(wg_up_p, (1, 0, 2)).reshape(POOL_WIDTH, D_MODEL)
    wup_a = jnp.transpose(wg_up_a, (1, 0, 2)).reshape(ATTN_WIDTH, D_MODEL)
    wout = wg_out.reshape(D_MODEL, D_MODEL)
    qkv = _mm_nt(hn, wt_qkv, BF16, "in_proj_qkv", lk, QKV_COLS, _row_tile(lk, 2600), 512, scale_first=HEAD_DIM ** -0.5)
    f = _mm_nt(hn, wt_f, F32, "in_proj_forget", lp, F_COLS, tm, F_COLS)
    c = _gates_fwd(f, bfg)
    c_t = jnp.transpose(c[:, :N_HEADS])
    c_first = jnp.transpose(c_t[:, ::ROW_TILE]).reshape(nb, N_HEADS // 2, 2, 1)
    c_keys = jnp.pad(c_t, ((0, 0), (0, lk - lp))).reshape(1, N_HEADS // 2, 2, lk)
    bias = jnp.where(jnp.arange(lk) < PAD_ROWS, -NEG, c_keys - c_first)
    y_pool = _pool_fwd(e, pw_b, pool_scale)
    o, lse = _attn_fwd(qkv, bias, nb)
    merged, y_attn = _merge_fwd(y_pool, o, e, wup_p, wup_a)
    dh1, loss_part, dgf = _head_fwd_bwd(merged, wout, x2, metapad, gf, target)

    dap, daa, dproj_half, do, delta, dy_pool = _merge_bwd(dh1, wout, y_pool, y_attn, wup_p, wup_a, e, o)
    dpc, dzp, dscale, dpw = _pool_bwd_local(e, dy_pool, pw_b, pool_scale)
    du = _pool_bwd_window(dpc)
    dq, dk, dv, dc4, dcq = _attn_bwd(qkv, do, lse, delta, bias, nb)
    dc = jnp.transpose(dc4, (1, 3, 0, 2)).reshape(lk, N_HEADS)[:lp]
    df, db = _gates_bwd(jnp.pad(dc, ((0, 0), (0, F_COLS - N_HEADS))), dcq, f, bfg)
    dproj = _fill_dproj(dproj_half, du, dzp, dq, dk, dv)
    tk = _row_tile(lp, 1100)
    dw_out = _mm_tn(merged, dh1, "grad_w_out", lp, 512, D_MODEL, tk)
    dw_up_p = _mm_tn(y_pool, dap, "grad_w_up_pool", lp, 512, D_MODEL, lp, chunks=N_CHIPS)
    dw_up_a = _mm_tn(y_attn, daa, "grad_w_up_attn", lp, 512, D_MODEL, lp, chunks=N_CHIPS)
    dwt_f = _mm_tn(df, hn, "grad_w_in_forget", lp, F_COLS, D_MODEL, tk)

    def pad8(a):
        return jnp.pad(a, ((0, (-a.shape[0]) % 8), (0, 0)))

    small_parts = [pad8(a) for a in (dgf.reshape(-1, LANES), dscale.reshape(-1, LANES), db, dpw.reshape(-1, LANES),
                                     loss_part)]
    small = jnp.concatenate(small_parts, axis=0)
    soffs = [0]
    for p in small_parts:
        soffs.append(soffs[-1] + p.shape[0])

    gs_rows = [dw_up_p, dw_up_a, dw_out.reshape(N_CHIPS, sh_d, D_MODEL)]
    half = MAIN_COLS // 2
    dwt_a, (*rb_rows, rb_f, sr) = _mm_tn(dproj, hn, "grad_w_in_a", lp, half // 4, D_MODEL, lp, m=half, m_off=0,
                                         swap=(gs_rows + [dwt_f, small], [1, 1, 1, 1, None]))
    *hs_rows, sc = _add_sibling_half(gs_rows, rb_rows, [1, 1, 1], small, sr, core)
    dwt_b, (rb_a,), (qs_rows, sq) = _mm_tn(dproj, hn, "grad_w_in_b", lp, half // 4, D_MODEL, lp, m=half, m_off=4,
                                           swap=([dwt_a], [1]), scatter=(hs_rows, sc))
    h_in = _add_halves_w_in([dwt_a, dwt_b, dwt_f], [rb_a, rb_f], core)
    grad_axes = [2, 1, 1, 1]
    (grad_x, g_block0, dg1), (q_in,), _ = _input_bwd(dproj, df, wt_e, wt_qkv, wt_f, x2, metapad, dh1, norm_g, [h_in])
    late_parts = jnp.concatenate([pad8(dg1.reshape(-1, LANES)), g_block0[PAD_ROWS:].reshape(-1, LANES)], axis=0)
    g_w_in_rows, g_w_up_p, g_w_up_a, g_w_out, st, late = _reduce_allgather(
        [q_in] + list(qs_rows), grad_axes, sq, [True, False, False, False], late_parts)
    n_norm = D_MODEL // LANES
    g_norm = late[:n_norm].reshape(1, D_MODEL)
    chip = 2 * lax.axis_index("x") + lax.axis_index("y")
    g_meta = lax.dynamic_slice_in_dim(late[n_norm:].reshape(N_META, D_MODEL), chip * sh_d, sh_d, axis=1)

    spiece = lambda k, rows: st[soffs[k]:soffs[k] + rows]
    g_final = spiece(0, D_MODEL // LANES).reshape(1, D_MODEL)
    g_scale = spiece(1, POOL_WIDTH // LANES).reshape(1, POOL_WIDTH)
    g_bf = spiece(2, 1)
    g_pw = spiece(3, POOL_WIDTH)
    loss = st[soffs[4], 0]

    def pad_lanes(a):
        return jnp.pad(a, ((0, 0), (0, F_COLS - N_HEADS)))

    w_in_res = (g_w_in_rows,) + _adamw_native(to_rows(w_in), g_w_in_rows, to_rows(m_w_in), to_rows(v_w_in), "adamw_w_in")

    upd = [
        ("meta_tokens", meta_tokens, g_meta, m_meta_tokens, v_meta_tokens),
        ("norm_g", norm_g, g_norm, m_norm_g, v_norm_g),
        ("w_in", None, None, None, None),
        ("b_forget", pad_lanes(b_forget), g_bf, pad_lanes(m_b_forget), pad_lanes(v_b_forget)),
        ("pool_w", pool_w.reshape(-1, LANES), g_pw, m_pool_w.reshape(-1, LANES), v_pool_w.reshape(-1, LANES)),
        ("pool_scale", pool_scale, g_scale, m_pool_scale, v_pool_scale),
        ("w_up_pool", w_up_pool[0], g_w_up_p, m_w_up_pool[0], v_w_up_pool[0]),
        ("w_up_attn", w_up_attn[0], g_w_up_a, m_w_up_attn[0], v_w_up_attn[0]),
        ("w_out", w_out[0], g_w_out, m_w_out[0], v_w_out[0]),
        ("final_norm_g", gf, g_final, m_final_norm_g.reshape(1, D_MODEL), v_final_norm_g.reshape(1, D_MODEL)),
    ]
    shapes = [meta_tokens.shape, norm_g.shape, w_in.shape, b_forget.shape, pool_w.shape, pool_scale.shape,
              w_up_pool.shape, w_up_attn.shape, w_out.shape, final_norm_g.shape]
    grads, deltas, new_ms, new_vs = [], [], [], []
    for (name, w_, g_, m_in, v_in), shp in zip(upd, shapes):
        if name == "w_in":
            res = tuple(from_rows(a) for a in w_in_res)
        else:
            d_, mn_, vn_ = _adamw(w_, g_, m_in, v_in, "adamw_" + name)
            res = (g_, d_, mn_, vn_)
        if name == "b_forget":
            res = tuple(a[:, :N_HEADS] for a in res)
        for lst, a in zip((grads, deltas, new_ms, new_vs), res):
            lst.append(a.reshape(shp))

    return (loss, grad_x.reshape(x.shape), *grads, *deltas, *new_ms, *new_vs)
```

```python
import jax
import jax.numpy as jnp
from jax import lax
from jax.experimental import pallas as pl
from jax.experimental.pallas import tpu as pltpu

F32 = jnp.float32
BF16 = jnp.bfloat16
MESH = pl.DeviceIdType.MESH
HIGHEST = lax.Precision.HIGHEST
HBM = pl.BlockSpec(memory_space=pltpu.HBM)

D_MODEL = 1024
N_META = 16
POOL_WIDTH = 512
POOL_GROUP = 128
POOL_WINDOWS = (2, 4, 8, 16)
N_HEADS = 8
HEAD_DIM = 64
ATTN_WIDTH = 512
RMS_EPS = 1e-6
N_CHIPS = 4

ADAM_LR = 0.001
ADAM_B1 = 0.9
ADAM_B2 = 0.999
ADAM_EPS = 1e-08
ADAM_WD = 0.01
ADAM_STEP = 10

LANES = 128
ROW_TILE = 256
KV_TILE = 1024
KV_TILE_BWD = 1024
PAD_ROWS = ROW_TILE - N_META
NEG = -1e30

E_U, E_ZP, E_ZA, E_GP, E_GA, E_COLS = 0, 512, 1024, 1536, 2560, 3584
QKV_COLS = 3 * ATTN_WIDTH
MAIN_COLS = E_COLS + QKV_COLS
F_COLS = LANES
REF_U, REF_Q, REF_ZA, REF_F, REF_GP = 0, 1024, 2560, 3072, 3080
VMEM_LIMIT = 48 * 1024 * 1024


def _params(sem=None, vmem=VMEM_LIMIT):
    return pltpu.CompilerParams(dimension_semantics=sem, vmem_limit_bytes=vmem)


def _in_hbm(*arrays):
    return [pltpu.with_memory_space_constraint(a, pltpu.HBM) for a in arrays]


def _row_tile(n, target):
    best = 16
    for t in range(16, target + 1, 16):
        if n % t == 0:
            best = t
    return best


def _sigmoid(x):
    return 1.0 / (1.0 + jnp.exp(-x))


def _half(ref, axis, which):
    n = ref.shape[axis] // 2
    idx = [slice(None)] * len(ref.shape)
    idx[axis] = pl.ds(pl.multiple_of(which * n, n), n)
    return ref.at[tuple(idx)]


def _half_shape(shape, axis):
    s = list(shape)
    s[axis] //= 2
    return tuple(s)


def _gather_start(ins, outs, axes, send, recv, fsend, frecv, local):
    x, y, c = lax.axis_index("x"), lax.axis_index("y"), lax.axis_index("c")
    me = 2 * x + y
    for t in range(len(ins)):
        pltpu.make_async_copy(ins[t], outs[t].at[me], local.at[t]).start()
        for k, chip in enumerate([(1 - x, y), (x, 1 - y), (1 - x, 1 - y)]):
            pltpu.make_async_remote_copy(
                src_ref=_half(ins[t], axes[t], c), dst_ref=_half(outs[t].at[me], axes[t], c),
                send_sem=send.at[3 * t + k], recv_sem=recv.at[3 * t + k], device_id=(*chip, c), device_id_type=MESH).start()


def _gather_finish(ins, outs, axes, send, recv, fsend, frecv, local):
    x, y, c = lax.axis_index("x"), lax.axis_index("y"), lax.axis_index("c")
    me = 2 * x + y
    sibling = (x, y, 1 - c)
    chips = [(1 - x, y), (x, 1 - y), (1 - x, 1 - y)]
    n = len(ins)

    def over_ici(t, k, chip, dst_slot):
        return pltpu.make_async_remote_copy(
            src_ref=_half(ins[t], axes[t], c), dst_ref=_half(outs[t].at[dst_slot], axes[t], c),
            send_sem=send.at[3 * t + k], recv_sem=recv.at[3 * t + k], device_id=(*chip, c), device_id_type=MESH)

    def to_sibling(t, k, slot, which):
        return pltpu.make_async_remote_copy(
            src_ref=_half(outs[t].at[slot], axes[t], which), dst_ref=_half(outs[t].at[slot], axes[t], which),
            send_sem=fsend.at[3 * t + k], recv_sem=frecv.at[3 * t + k], device_id=sibling, device_id_type=MESH)

    forwards = []
    for t in range(n):
        for k, (px, py) in enumerate(chips):
            over_ici(t, k, (px, py), 2 * px + py).wait_recv()
            fw = to_sibling(t, k, 2 * px + py, c)
            fw.start()
            forwards.append(fw)
    for t in range(n):
        for k, (px, py) in enumerate(chips):
            to_sibling(t, k, 2 * px + py, 1 - c).wait_recv()
    for t in range(n):
        for k, chip in enumerate(chips):
            over_ici(t, k, chip, me).wait_send()
    for fw in forwards:
        fw.wait_send()
    for t in range(n):
        pltpu.make_async_copy(ins[t], outs[t].at[me], local.at[t]).wait()


def _gather_sems(n):
    return [pltpu.SemaphoreType.DMA((3 * n,)), pltpu.SemaphoreType.DMA((3 * n,)), pltpu.SemaphoreType.DMA((3 * n,)),
            pltpu.SemaphoreType.DMA((3 * n,)), pltpu.SemaphoreType.DMA((n,))]


def _gathered_shapes(shards):
    return [jax.ShapeDtypeStruct((N_CHIPS,) + s.shape, s.dtype) for s in shards]


def _swap_copies(srcs, dsts, axes, send, recv):
    x, y, c = lax.axis_index("x"), lax.axis_index("y"), lax.axis_index("c")
    return [pltpu.make_async_remote_copy(
        src_ref=srcs[t] if axes[t] is None else _half(srcs[t], axes[t], 1 - c), dst_ref=dsts[t],
        send_sem=send.at[t], recv_sem=recv.at[t], device_id=(x, y, 1 - c), device_id_type=MESH) for t in range(len(srcs))]


def _swap_shapes(srcs, axes):
    return [jax.ShapeDtypeStruct(g.shape if a is None else _half_shape(g.shape, a), g.dtype) for g, a in zip(srcs, axes)]


def _add_halves_w_in(parts, rbs, core):
    na, nb_rows, w = rbs[0].shape[0], parts[1].shape[0], rbs[0].shape[1]
    per = (na + nb_rows + N_HEADS) // N_CHIPS
    pieces = [[] for _ in range(N_CHIPS)]
    r = 0
    for src, lo, hi in ((0, 0, na), (1, 0, ATTN_WIDTH), (2, 0, N_HEADS), (1, ATTN_WIDTH, nb_rows)):
        while lo < hi:
            c, at = divmod(r, per)
            n = min(hi - lo, per - at)
            pieces[c].append((src, lo, n, at))
            lo, r = lo + n, r + n

    def body(core_ref, a_ref, b_ref, f_ref, ra_ref, rf_ref, o_ref, am, bm, fm, ar, br, fr, rows_buf, sems, send, recv):
        x, y, core_id = lax.axis_index("x"), lax.axis_index("y"), lax.axis_index("c")
        swap = pltpu.make_async_remote_copy(src_ref=_half(b_ref, 1, 1 - core_id), dst_ref=br, send_sem=send.at[0],
                                            recv_sem=recv.at[0], device_id=(x, y, 1 - core_id), device_id_type=MESH)
        swap.start()
        cols = pl.ds(pl.multiple_of(core_ref[0] * w, w), w)
        few = pl.ds(0, N_HEADS)
        loads = [(pltpu.make_async_copy(a_ref.at[:, cols], am, sems.at[0]), pltpu.make_async_copy(ra_ref, ar, sems.at[1])),
                 (pltpu.make_async_copy(b_ref.at[:, cols], bm, sems.at[2]),),
                 (pltpu.make_async_copy(f_ref.at[few, cols], fm, sems.at[3]),
                  pltpu.make_async_copy(rf_ref.at[few, :], fr, sems.at[4]))]
        for group in loads:
            for cp in group:
                cp.start()
        bufs = [(am, ar), (bm, br), (fm, fr)]
        there = set()
        for c in range(N_CHIPS):
            for src, lo, n, at in pieces[c]:
                if src not in there:
                    for cp in loads[src]:
                        cp.wait()
                    if src == 1:
                        swap.wait_recv()
                    there.add(src)
                mine, other = bufs[src]
                rows_buf[at:at + n, :] = mine[lo:lo + n, :] + other[lo:lo + n, :]
            o_ref[c] = rows_buf[0:per, :].astype(BF16)
        swap.wait_send()

    any_space = pl.BlockSpec(memory_space=pl.ANY)
    a, b, f = parts
    return pl.pallas_call(
        body, name="grad_add_sibling_w_in",
        grid_spec=pltpu.PrefetchScalarGridSpec(
            num_scalar_prefetch=1, grid=(1,),
            in_specs=[any_space, HBM, any_space, any_space, any_space],
            out_specs=pl.BlockSpec((N_CHIPS, per, w), lambda i, cr: (0, 0, 0)),
            scratch_shapes=[pltpu.VMEM((na, w), F32), pltpu.VMEM((nb_rows, w), F32), pltpu.VMEM((N_HEADS, w), F32),
                            pltpu.VMEM((na, w), F32), pltpu.VMEM((nb_rows, w), F32), pltpu.VMEM((N_HEADS, w), F32),
                            pltpu.VMEM((per + (-per) % 8, w), F32), pltpu.SemaphoreType.DMA((5,)),
                            pltpu.SemaphoreType.DMA((1,)), pltpu.SemaphoreType.DMA((1,))]),
        out_shape=jax.ShapeDtypeStruct((N_CHIPS, per, w), BF16),
        compiler_params=_params(("arbitrary",)),
    )(core, a, *_in_hbm(b), f, *rbs)


def _weight_sections(wg):
    per, d = wg.shape[1], wg.shape[2]
    n_ref = N_CHIPS * per
    units = [(0, 0, REF_U, REF_Q), (1, 0, REF_Q, REF_ZA), (0, REF_Q - REF_U, REF_ZA, REF_F), (2, 0, REF_F, REF_GP),
             (0, REF_Q - REF_U + REF_F - REF_ZA, REF_GP, n_ref)]
    plan = []
    for out, first, lo, hi in sorted(units, key=lambda u: u[2]):
        pieces, at = [], 0
        while lo < hi:
            c, src = divmod(lo, per)
            n = min(hi - lo, per - src)
            pieces.append((c, src, n, at))
            lo, at = lo + n, at + n
        plan.append((out, first, at, pieces))
    stage_rows = max(rows for _, _, rows, _ in plan)

    def body(g_ref, e_ref, q_ref, f_ref, gb, stage, sems):
        loads = [pltpu.make_async_copy(g_ref.at[c], gb.at[c], sems.at[c]) for c in range(N_CHIPS)]
        for cp in loads:
            cp.start()
        outs = [e_ref, q_ref, f_ref]
        there = set()
        for out, first, rows, pieces in plan:
            for c, _, _, _ in pieces:
                if c not in there:
                    loads[c].wait()
                    there.add(c)
            if len(pieces) == 1 and pieces[0][1] % 16 == 0 and rows % 16 == 0:
                c, src, _, _ = pieces[0]
                outs[out][first:first + rows, :] = gb[c, src:src + rows, :]
                continue
            for c, src, n, at in pieces:
                stage[at:at + n, :] = gb[c, src:src + n, :].astype(F32)
            padded = rows if rows % 16 == 0 else outs[out].shape[0] - first
            if padded > rows:
                stage[rows:padded, :] = jnp.zeros((padded - rows, d), F32)
            outs[out][first:first + padded, :] = stage[0:padded, :].astype(BF16)

    return pl.pallas_call(
        body, name="weight_sections",
        in_specs=[pl.BlockSpec(memory_space=pl.ANY)],
        out_shape=[jax.ShapeDtypeStruct((E_COLS, d), BF16), jax.ShapeDtypeStruct((QKV_COLS, d), BF16),
                   jax.ShapeDtypeStruct((F_COLS, d), BF16)],
        scratch_shapes=[pltpu.VMEM(wg.shape, BF16), pltpu.VMEM((stage_rows, d), F32), pltpu.SemaphoreType.DMA((N_CHIPS,))],
        compiler_params=_params(),
    )(wg)


def _add_sibling_half(gs, rbs, axes, small, sr, core):
    n = len(gs)

    def body(core_ref, *refs):
        g_refs, rb_refs = refs[:n], refs[n:2 * n]
        s_ref, sr_ref = refs[2 * n], refs[2 * n + 1]
        h_refs, sc_ref = refs[2 * n + 2:3 * n + 2], refs[3 * n + 2]
        for t in range(n):
            h_refs[t][...] = (g_refs[t][...] + rb_refs[t][...]).astype(BF16)
        sc_ref[...] = s_ref[...] + sr_ref[...]

    def mine(rb, axis):
        blk = (None,) + rb.shape[1:]
        if axis == 2:
            return pl.BlockSpec(blk, lambda j, cr: (j, 0, cr[0]))
        return pl.BlockSpec(blk, lambda j, cr: (j, cr[0], 0))

    chunk = lambda rb: pl.BlockSpec((None,) + rb.shape[1:], lambda j, cr: (j, 0, 0))
    whole = pl.BlockSpec(small.shape, lambda j, cr: (0, 0))
    return pl.pallas_call(
        body, name="grad_add_sibling",
        grid_spec=pltpu.PrefetchScalarGridSpec(
            num_scalar_prefetch=1, grid=(N_CHIPS,),
            in_specs=[mine(rb, a) for rb, a in zip(rbs, axes)] + [chunk(rb) for rb in rbs] + [whole, whole],
            out_specs=[chunk(rb) for rb in rbs] + [whole]),
        out_shape=[jax.ShapeDtypeStruct(rb.shape, BF16) for rb in rbs] + [jax.ShapeDtypeStruct(small.shape, F32)],
        compiler_params=_params(("arbitrary",)),
    )(core, *gs, *rbs, small, sr)


def _chip_scatter_plan(h_refs, s_ref, q_refs, sq_ref, send, recv, local):
    n = len(h_refs)
    nt = n + (s_ref is not None)
    x, y, c = lax.axis_index("x"), lax.axis_index("y"), lax.axis_index("c")
    me = 2 * x + y
    chips = [(1 - x, y), (x, 1 - y), (1 - x, 1 - y)]

    def copy(t, k, chip, src_slot, dst_slot):
        src = h_refs[t].at[src_slot] if t < n else s_ref
        dst = (q_refs[t] if t < n else sq_ref).at[dst_slot]
        return pltpu.make_async_remote_copy(src_ref=src, dst_ref=dst, send_sem=send.at[3 * t + k],
                                            recv_sem=recv.at[3 * t + k], device_id=(*chip, c), device_id_type=MESH)

    own = [pltpu.make_async_copy(h_refs[t].at[me], q_refs[t].at[me], local.at[t]) for t in range(n)]
    if s_ref is not None:
        own.append(pltpu.make_async_copy(s_ref, sq_ref.at[me], local.at[n]))
    sends = [copy(t, k, (px, py), 2 * px + py, me) for t in range(nt) for k, (px, py) in enumerate(chips)]
    recvs = [copy(t, k, (px, py), me, 2 * px + py) for t in range(nt) for k, (px, py) in enumerate(chips)]
    return own, sends, recvs


def _scatter_operands(hs, sc):
    small = [] if sc is None else [sc]
    nt = len(hs) + len(small)
    shapes = [jax.ShapeDtypeStruct(h.shape, h.dtype) for h in hs] + [
        jax.ShapeDtypeStruct((N_CHIPS,) + a.shape, a.dtype) for a in small]
    sems = [pltpu.SemaphoreType.DMA((3 * nt,)), pltpu.SemaphoreType.DMA((3 * nt,)), pltpu.SemaphoreType.DMA((nt,))]
    return list(hs) + small, shapes, sems


def _reduce_allgather(qs, axes, sq, as_rows, late):
    n = len(qs)
    shard_shapes, half_axes = [], []
    for q, a, rows_form in zip(qs, axes, as_rows):
        shape = [d * 2 if i == a - 1 else d for i, d in enumerate(q.shape[1:])]
        if rows_form:
            assert a == 2
            shape = [shape[0], 1, shape[1]]
        shard_shapes.append(tuple(shape))
        half_axes.append(2 if rows_form else a - 1)

    def body(*refs):
        q_refs, sq_ref, late_ref = refs[:n], refs[n], refs[n + 1]
        o_refs, st_ref, late_sum_ref = refs[n + 2:2 * n + 2], refs[2 * n + 2], refs[2 * n + 3]
        sib_buf, chip_buf, send, recv, late_send, late_recv = refs[2 * n + 4:]
        x, y, c = lax.axis_index("x"), lax.axis_index("y"), lax.axis_index("c")
        me = 2 * x + y
        chips = [(1 - x, y), (x, 1 - y), (1 - x, 1 - y)]

        def swap(t, which):
            return pltpu.make_async_remote_copy(
                src_ref=_half(o_refs[t], half_axes[t], which), dst_ref=_half(o_refs[t], half_axes[t], which),
                send_sem=send.at[t], recv_sem=recv.at[t], device_id=(x, y, 1 - c), device_id_type=MESH)

        def late_copy(k, chip, slot):
            return pltpu.make_async_remote_copy(src_ref=chip_buf.at[slot], dst_ref=chip_buf.at[slot], send_sem=late_send.at[1 + k],
                                                recv_sem=late_recv.at[1 + k], device_id=(*chip, c), device_id_type=MESH)

        late_swap = pltpu.make_async_remote_copy(src_ref=late_ref, dst_ref=sib_buf, send_sem=late_send.at[0],
                                                 recv_sem=late_recv.at[0], device_id=(x, y, 1 - c), device_id_type=MESH)
        late_swap.start()
        sent = []
        for t in range(n):
            q = q_refs[t]
            total = ((q[0].astype(F32) + q[1].astype(F32)) + q[2].astype(F32)) + q[3].astype(F32)
            if as_rows[t]:
                total = total.reshape(total.shape[0], 1, total.shape[1])
            _half(o_refs[t], half_axes[t], c)[...] = total
            cp = swap(t, c)
            cp.start()
            sent.append(cp)
            if t == 0:
                late_swap.wait()
                chip_buf[me] = late_ref[...] + sib_buf[...]
                late_sends = [late_copy(k, chip, me) for k, chip in enumerate(chips)]
                for lc in late_sends:
                    lc.start()
        st_ref[...] = ((sq_ref[0] + sq_ref[1]) + sq_ref[2]) + sq_ref[3]
        for t in range(n):
            swap(t, 1 - c).wait_recv()
        for cp in sent:
            cp.wait_send()
        for k, (px, py) in enumerate(chips):
            late_copy(k, (px, py), 2 * px + py).wait_recv()
        for lc in late_sends:
            lc.wait_send()
        late_sum_ref[...] = ((chip_buf[0] + chip_buf[1]) + chip_buf[2]) + chip_buf[3]

    vmem = pl.BlockSpec(memory_space=pltpu.VMEM)
    return pl.pallas_call(
        body, name="grad_reduce_allgather",
        in_specs=[vmem] * (n + 2), out_specs=[vmem] * (n + 2),
        out_shape=[jax.ShapeDtypeStruct(s, F32) for s in shard_shapes] + [jax.ShapeDtypeStruct(sq.shape[1:], F32),
                                                                         jax.ShapeDtypeStruct(late.shape, F32)],
        scratch_shapes=[pltpu.VMEM(late.shape, F32), pltpu.VMEM((N_CHIPS,) + late.shape, F32),
                        pltpu.SemaphoreType.DMA((n,)), pltpu.SemaphoreType.DMA((n,)),
                        pltpu.SemaphoreType.DMA((4,)), pltpu.SemaphoreType.DMA((4,))],
        compiler_params=_params(),
    )(*qs, sq, late)


def _dot_nt(a, b):
    return lax.dot_general(a, b, (((1,), (1,)), ((), ())), preferred_element_type=F32)


def _dot_tn(a, b):
    return lax.dot_general(a, b, (((0,), (0,)), ((), ())), preferred_element_type=F32)


def _mm_nt(a, bt, out_dtype, name, m, n, tm, tn, row_block=0, scale_first=None, gather=None):
    k = a.shape[1]
    shards, axes = gather if gather is not None else ([], [])
    ng = len(shards)
    grid = (m // tm, n // tn)

    def body(a_ref, b_ref, *rest):
        ins, o_ref, outs, sems = rest[:ng], rest[ng], rest[ng + 1:2 * ng + 1], rest[2 * ng + 1:]
        first = (pl.program_id(0) == 0) & (pl.program_id(1) == 0)
        last = (pl.program_id(0) == grid[0] - 1) & (pl.program_id(1) == grid[1] - 1)
        if ng:
            @pl.when(first)
            def _():
                _gather_start(ins, outs, axes, *sems)

        r = _dot_nt(a_ref[...], b_ref[...])
        if scale_first is not None:
            r = r * jnp.where(pl.program_id(1) == 0, scale_first, 1.0)
        o_ref[...] = r.astype(out_dtype)

        if ng:
            @pl.when(last)
            def _():
                _gather_finish(ins, outs, axes, *sems)

    res = pl.pallas_call(
        body, name=name, grid=grid,
        in_specs=[pl.BlockSpec((tm, k), lambda i, j: (i, 0)), pl.BlockSpec((tn, k), lambda i, j: (row_block + j, 0))]
        + [HBM] * ng,
        out_specs=[pl.BlockSpec((tm, tn), lambda i, j: (i, j))] + [HBM] * ng,
        out_shape=[jax.ShapeDtypeStruct((m, n), out_dtype)] + _gathered_shapes(shards),
        scratch_shapes=_gather_sems(ng) if ng else [],
        compiler_params=_params(("arbitrary", "arbitrary") if ng else ("parallel", "parallel")),
    )(a, bt, *_in_hbm(*shards))
    return (res[0], res[1:]) if ng else res[0]


def _mm_tn(a, b, name, k, tm, tn, tk, chunks=1, m=None, m_off=0, swap=None, scatter=None):
    m = a.shape[1] if m is None else m
    n = b.shape[1]
    cw = n // chunks
    srcs, axes = swap if swap is not None else ([], [])
    ns = len(srcs)
    hs, sc = scatter if scatter is not None else ([], None)
    sc_in, sc_shapes, sc_sems = _scatter_operands(hs, sc) if scatter is not None else ([], [], [])
    nh, nq = len(hs), len(sc_in)
    grid = (m // tm, n // tn, k // tk)

    def body(a_ref, b_ref, *rest):
        s_refs, h_refs = rest[:ns], rest[ns:ns + nq]
        o_ref = rest[ns + nq]
        d_refs, q_refs = rest[ns + nq + 1:2 * ns + nq + 1], rest[2 * ns + nq + 1:2 * ns + 2 * nq + 1]
        sems = rest[2 * ns + 2 * nq + 1:]
        swap_sems, scatter_sems = (sems[:2], sems[2:]) if ns else ((), sems)
        ids = [pl.program_id(d) for d in range(3)]

        def scatter_plan():
            small_in = h_refs[nh] if sc is not None else None
            small_out = q_refs[nh] if sc is not None else None
            return _chip_scatter_plan(h_refs[:nh], small_in, q_refs[:nh], small_out, *scatter_sems)

        if ns or nq:
            @pl.when((ids[0] == 0) & (ids[1] == 0) & (ids[2] == 0))
            def _():
                if ns:
                    for cp in _swap_copies(s_refs, d_refs, axes, *swap_sems):
                        cp.start()
                if nq:
                    own, sends, _ = scatter_plan()
                    for cp in own + sends:
                        cp.start()

        @pl.when(ids[2] == 0)
        def _():
            o_ref[...] = jnp.zeros_like(o_ref)
        r = _dot_tn(a_ref[...].astype(BF16), b_ref[...].astype(BF16))
        if chunks > 1:
            for c in range(chunks):
                o_ref[c] += r[:, c * cw:(c + 1) * cw]
        else:
            o_ref[...] += r

        if ns or nq:
            @pl.when((ids[0] == grid[0] - 1) & (ids[1] == grid[1] - 1) & (ids[2] == grid[2] - 1))
            def _():
                if ns:
                    for cp in _swap_copies(s_refs, d_refs, axes, *swap_sems):
                        cp.wait()
                if nq:
                    own, sends, recvs = scatter_plan()
                    for cp in recvs:
                        cp.wait_recv()
                    for cp in sends:
                        cp.wait_send()
                    for cp in own:
                        cp.wait()

    if chunks > 1:
        assert tn == n
        out_spec = pl.BlockSpec((chunks, tm, cw), lambda i, j, kk: (0, i, 0))
        out_shape = jax.ShapeDtypeStruct((chunks, m, cw), F32)
    else:
        out_spec = pl.BlockSpec((tm, tn), lambda i, j, kk: (i, j))
        out_shape = jax.ShapeDtypeStruct((m, n), F32)
    riders = ns + nq
    res = pl.pallas_call(
        body, name=name, grid=grid,
        in_specs=[pl.BlockSpec((tk, tm), lambda i, j, kk: (kk, m_off + i)), pl.BlockSpec((tk, tn), lambda i, j, kk: (kk, j))]
        + [HBM] * riders,
        out_specs=[out_spec] + [HBM] * riders, out_shape=[out_shape] + _swap_shapes(srcs, axes) + sc_shapes,
        scratch_shapes=([pltpu.SemaphoreType.DMA((ns,)), pltpu.SemaphoreType.DMA((ns,))] if ns else []) + sc_sems,
        compiler_params=_params(("arbitrary",) * 3 if riders else ("parallel", "parallel", "arbitrary")),
    )(a, b, *_in_hbm(*srcs, *sc_in))
    if not riders:
        return res[0]
    out = [res[0]]
    if ns:
        out.append(res[1:1 + ns])
    if nq:
        out.append((res[1 + ns:1 + ns + nh], res[1 + ns + nh] if sc is not None else None))
    return tuple(out)


def _tokens_spec():
    return pl.BlockSpec((ROW_TILE, D_MODEL), lambda i: (jnp.maximum(i - 1, 0), 0))


def _rmsnorm_fwd(x2, g1, lk, shards, axes):
    nb = x2.shape[0] // ROW_TILE + 1
    nblk = lk // ROW_TILE
    ng = len(shards)
    meta_cols = shards[-1].shape[1]

    def body(x_ref, g_ref, *rest):
        ins, hn_ref, outs = rest[:ng], rest[ng], rest[ng + 1:2 * ng + 1]
        sems, meta_buf, meta_sem = rest[2 * ng + 1:2 * ng + 6], rest[2 * ng + 6], rest[2 * ng + 7]
        s = pl.program_id(0)
        blk = (s + 1) % nblk

        @pl.when(s == 0)
        def _():
            _gather_start(ins, outs, axes, *sems)

        def normed(h):
            r = lax.rsqrt(jnp.mean(h * h, axis=-1, keepdims=True) + RMS_EPS)
            return ((h * r) * g_ref[...]).astype(BF16)

        @pl.when(s < nblk - 1)
        def _():
            hn_ref[...] = normed(jnp.where(blk >= nb, 0.0, x_ref[...]))

        @pl.when(s == nblk - 1)
        def _():
            _gather_finish(ins, outs, axes, *sems)
            fetch = pltpu.make_async_copy(outs[-1], meta_buf, meta_sem.at[0])
            fetch.start()
            fetch.wait()
            meta = jnp.concatenate([meta_buf[j] for j in range(N_CHIPS)], axis=1)
            hn_ref[...] = normed(jnp.concatenate([jnp.zeros((PAD_ROWS, D_MODEL), F32), meta], axis=0))

    res = pl.pallas_call(
        body, name="rmsnorm_fwd", grid=(nblk,),
        in_specs=[pl.BlockSpec((ROW_TILE, D_MODEL), lambda s: (jnp.clip((s + 1) % nblk - 1, 0, nb - 2), 0)),
                  pl.BlockSpec((1, D_MODEL), lambda s: (0, 0))] + [HBM] * ng,
        out_specs=[pl.BlockSpec((ROW_TILE, D_MODEL), lambda s: ((s + 1) % nblk, 0))] + [HBM] * ng,
        out_shape=[jax.ShapeDtypeStruct((lk, D_MODEL), BF16)] + _gathered_shapes(shards),
        scratch_shapes=_gather_sems(ng) + [pltpu.VMEM((N_CHIPS, N_META, meta_cols), F32), pltpu.SemaphoreType.DMA((1,))],
        compiler_params=_params(("arbitrary",)),
    )(x2, g1, *_in_hbm(*shards))
    return res[0], res[1:]


def _valid_gate_mask(i):
    row = i * ROW_TILE + lax.broadcasted_iota(jnp.int32, (ROW_TILE, F_COLS), 0)
    col = lax.broadcasted_iota(jnp.int32, (ROW_TILE, F_COLS), 1)
    return (row >= PAD_ROWS) & (col < N_HEADS)


def _gates_fwd(f, bfg):
    nb = f.shape[0] // ROW_TILE

    def body(f_ref, b_ref, c_ref, carry):
        i = pl.program_id(0)

        @pl.when(i == 0)
        def _():
            carry[...] = jnp.zeros_like(carry)

        logit = f_ref[...] + b_ref[...]
        lf = jnp.minimum(logit, 0.0) - jnp.log1p(jnp.exp(-jnp.abs(logit)))
        lf = jnp.where(_valid_gate_mask(i), lf, 0.0)
        r_i = lax.broadcasted_iota(jnp.int32, (ROW_TILE, ROW_TILE), 0)
        c_i = lax.broadcasted_iota(jnp.int32, (ROW_TILE, ROW_TILE), 1)
        tri = (c_i <= r_i).astype(F32)
        c_ref[...] = jnp.dot(tri, lf, precision=HIGHEST, preferred_element_type=F32) + carry[...]
        carry[...] = carry[...] + jnp.sum(lf, axis=0, keepdims=True)

    return pl.pallas_call(
        body, name="gates_fwd", grid=(nb,),
        in_specs=[pl.BlockSpec((ROW_TILE, F_COLS), lambda i: (i, 0)), pl.BlockSpec((1, F_COLS), lambda i: (0, 0))],
        out_specs=pl.BlockSpec((ROW_TILE, F_COLS), lambda i: (i, 0)),
        out_shape=jax.ShapeDtypeStruct(f.shape, F32),
        scratch_shapes=[pltpu.VMEM((1, F_COLS), F32)],
        compiler_params=_params(("arbitrary",)),
    )(f, bfg)


def _pool_counts(i):
    row = i * ROW_TILE + lax.broadcasted_iota(jnp.int32, (ROW_TILE, 1), 0)
    return jnp.maximum(row - PAD_ROWS, 0)


def _trailing_sums(xc, levels):
    acc = xc
    for lv in range(levels):
        acc = acc + pltpu.roll(acc, 1 << lv, 0)
    return acc


def _leading_sums(xc, levels):
    n = xc.shape[0]
    acc = xc
    for lv in range(levels):
        acc = acc + pltpu.roll(acc, n - (1 << lv), 0)
    return acc


def _pool_p(u_cur, u_prev, i):
    pos = _pool_counts(i)
    ps, invs = [], []
    for g, w in enumerate(POOL_WINDOWS):
        sl = slice(g * POOL_GROUP, (g + 1) * POOL_GROUP)
        cur = u_cur[:, sl]
        xc = jnp.concatenate([u_prev[:, sl], cur], axis=0)
        win = _trailing_sums(xc, g + 1)[ROW_TILE:, :]
        inv = 1.0 / jnp.minimum(pos + 1, w).astype(F32)
        ps.append(win * inv - cur)
        invs.append(inv)
    return ps, invs


def _pool_fwd(e, pw, scale):
    nb = e.shape[0] // ROW_TILE

    def body(uc_ref, up_ref, z_ref, pw_ref, sc_ref, y_ref):
        i = pl.program_id(0)
        u_cur = uc_ref[...].astype(F32)
        u_prev = jnp.where(i == 0, 0.0, up_ref[...].astype(F32))
        ps, _ = _pool_p(u_cur, u_prev, i)
        z = z_ref[...].astype(F32)
        gate = z * _sigmoid(z)
        for g in range(len(POOL_WINDOWS)):
            sl = slice(g * POOL_GROUP, (g + 1) * POOL_GROUP)
            yraw = jnp.dot(ps[g].astype(BF16), pw_ref[g], preferred_element_type=F32)
            y_ref[:, sl] = ((yraw * sc_ref[:, sl]) * gate[:, sl]).astype(BF16)

    blk = (ROW_TILE, POOL_WIDTH)
    return pl.pallas_call(
        body, name="pool_fwd", grid=(nb,),
        in_specs=[pl.BlockSpec(blk, lambda i: (i, 0)), pl.BlockSpec(blk, lambda i: (jnp.maximum(i - 1, 0), 0)),
                  pl.BlockSpec(blk, lambda i: (i, 1)),
                  pl.BlockSpec((len(POOL_WINDOWS), POOL_GROUP, POOL_GROUP), lambda i: (0, 0, 0)),
                  pl.BlockSpec((1, POOL_WIDTH), lambda i: (0, 0))],
        out_specs=pl.BlockSpec(blk, lambda i: (i, 0)),
        out_shape=jax.ShapeDtypeStruct((e.shape[0], POOL_WIDTH), BF16),
        compiler_params=_params(("parallel",)),
    )(e, e, e, pw, scale)


def _stack_heads(a):
    first = lax.broadcasted_iota(jnp.int32, a.shape, 1) < HEAD_DIM
    zero = jnp.zeros_like(a)
    return jnp.concatenate([jnp.where(first, a, zero), jnp.where(first, zero, a)], axis=0)


def _unstack_heads(a):
    rows = a.shape[0] // 2
    first = lax.broadcasted_iota(jnp.int32, (rows, LANES), 1) < HEAD_DIM
    return jnp.where(first, a[:rows], a[rows:])


def _causal(i, jj, stacked, kv_tile):
    r = lax.broadcasted_iota(jnp.int32, (stacked * ROW_TILE, kv_tile), 0)
    if stacked == 2:
        r = jnp.where(r >= ROW_TILE, r - ROW_TILE, r)
    kidx = jj * kv_tile + lax.broadcasted_iota(jnp.int32, (stacked * ROW_TILE, kv_tile), 1)
    return kidx <= i * ROW_TILE + r


def _stack_rows(b):
    n = b.shape[1]
    return jnp.concatenate([jnp.broadcast_to(b[0:1], (ROW_TILE, n)), jnp.broadcast_to(b[1:2], (ROW_TILE, n))], axis=0)


def _attn_fwd(qkv, bias, nb):
    lk = qkv.shape[0]
    lp = nb * ROW_TILE
    nkb = lk // KV_TILE
    n_pairs = N_HEADS // 2

    def body(q_ref, k_ref, v_ref, b_ref, o_ref, lse_ref):
        i = pl.program_id(1)
        qs = _stack_heads(q_ref[...])
        last = (i * ROW_TILE) // KV_TILE

        def block(jj, carry, masked, n_keys=KV_TILE):
            m, l, acc = carry
            rows = pl.ds(pl.multiple_of(jj * KV_TILE, KV_TILE), n_keys)
            s = _dot_nt(qs, k_ref[rows, :]) - _stack_rows(b_ref[0, 0, :, rows])
            if masked:
                s = jnp.where(_causal(i, jj, 2, KV_TILE)[:, :n_keys], s, NEG)
            m_new = jnp.maximum(m, jnp.max(s, axis=1, keepdims=True))
            alpha = jnp.exp(m - m_new)
            p = jnp.exp(s - m_new)
            l = alpha * l + jnp.sum(p, axis=1, keepdims=True)
            acc = alpha * acc + jnp.dot(p.astype(BF16), v_ref[rows, :], preferred_element_type=F32)
            return m_new, l, acc

        init = (jnp.full((2 * ROW_TILE, 1), NEG, F32), jnp.zeros((2 * ROW_TILE, 1), F32),
                jnp.zeros((2 * ROW_TILE, LANES), F32))
        def finish(carry):
            m, l, acc = carry
            o_ref[...] = _unstack_heads(acc / l)
            lse_ref[...] = _unstack_heads(jnp.broadcast_to(m + jnp.log(l), (2 * ROW_TILE, LANES)))

        carry = lax.fori_loop(0, last, lambda jj, c: block(jj, c, False), init)
        ends = [lambda c, n=(r + 1) * ROW_TILE: finish(block(last, c, True, n)) for r in range(KV_TILE // ROW_TILE)]
        lax.switch(i - last * (KV_TILE // ROW_TILE), ends, carry)

    return pl.pallas_call(
        body, name="attn_fwd", grid=(n_pairs, nb),
        in_specs=[pl.BlockSpec((ROW_TILE, LANES), lambda hp, i: (i, hp)),
                  pl.BlockSpec((lk, LANES), lambda hp, i: (0, n_pairs + hp)),
                  pl.BlockSpec((lk, LANES), lambda hp, i: (0, 2 * n_pairs + hp)),
                  pl.BlockSpec((1, 1, 2, lk), lambda hp, i: (i, hp, 0, 0))],
        out_specs=[pl.BlockSpec((ROW_TILE, LANES), lambda hp, i: (i, hp)),
                   pl.BlockSpec((ROW_TILE, LANES), lambda hp, i: (i, hp))],
        out_shape=[jax.ShapeDtypeStruct((lp, ATTN_WIDTH), F32), jax.ShapeDtypeStruct((lp, ATTN_WIDTH), F32)],
        compiler_params=_params(("parallel", "parallel")),
    )(qkv, qkv, qkv, bias)


def _merge_fwd(y_pool, o, e, wup_p, wup_a):
    lp = o.shape[0]
    nb = lp // ROW_TILE

    def body(yp_ref, o_ref, e_ref, wp_ref, wa_ref, mg_ref, ya_ref):
        za = e_ref[:, E_ZA:E_GP].astype(F32)
        ya = (o_ref[...] * (za * _sigmoid(za))).astype(BF16)
        ya_ref[...] = ya
        a_pool = jnp.dot(yp_ref[...], wp_ref[...], preferred_element_type=F32)
        a_attn = jnp.dot(ya, wa_ref[...], preferred_element_type=F32)
        mg_ref[...] = (_sigmoid(e_ref[:, E_GP:E_GA].astype(F32)) * a_pool
                       + _sigmoid(e_ref[:, E_GA:E_COLS].astype(F32)) * a_attn).astype(BF16)

    return pl.pallas_call(
        body, name="merge_fwd", grid=(nb,),
        in_specs=[pl.BlockSpec((ROW_TILE, POOL_WIDTH), lambda i: (i, 0)),
                  pl.BlockSpec((ROW_TILE, ATTN_WIDTH), lambda i: (i, 0)),
                  pl.BlockSpec((ROW_TILE, E_COLS), lambda i: (i, 0)),
                  pl.BlockSpec((POOL_WIDTH, D_MODEL), lambda i: (0, 0)),
                  pl.BlockSpec((ATTN_WIDTH, D_MODEL), lambda i: (0, 0))],
        out_specs=[pl.BlockSpec((ROW_TILE, D_MODEL), lambda i: (i, 0)),
                   pl.BlockSpec((ROW_TILE, ATTN_WIDTH), lambda i: (i, 0))],
        out_shape=[jax.ShapeDtypeStruct((lp, D_MODEL), BF16), jax.ShapeDtypeStruct((lp, ATTN_WIDTH), BF16)],
        compiler_params=_params(("parallel",)),
    )(y_pool, o, e, wup_p, wup_a)


def _head_fwd_bwd(merged, w_out, x2, metapad, gf, target):
    lp = merged.shape[0]
    nb = lp // ROW_TILE

    def body(mg_ref, w_ref, x_ref, mp_ref, g_ref, t_ref, dh_ref, loss_ref, dg_ref):
        i = pl.program_id(0)

        @pl.when(i == 0)
        def _():
            loss_ref[...] = jnp.zeros_like(loss_ref)
            dg_ref[...] = jnp.zeros_like(dg_ref)

        h0 = jnp.where(i == 0, mp_ref[...], x_ref[...])
        h1 = h0 + jnp.dot(mg_ref[...], w_ref[...], preferred_element_type=F32)
        r = lax.rsqrt(jnp.mean(h1 * h1, axis=-1, keepdims=True) + RMS_EPS)
        xhat = h1 * r
        g = g_ref[...]
        err = jnp.where(i == 0, 0.0, xhat * g - t_ref[...])
        loss_ref[...] += 0.5 * jnp.sum(jnp.mean(err * err, axis=-1, keepdims=True))
        dy = err / D_MODEL
        dg_ref[...] += jnp.sum(dy * xhat, axis=0, keepdims=True)
        dxhat = dy * g
        dh_ref[...] = r * (dxhat - xhat * jnp.mean(dxhat * xhat, axis=-1, keepdims=True))

    return pl.pallas_call(
        body, name="head_fwd_bwd", grid=(nb,),
        in_specs=[pl.BlockSpec((ROW_TILE, D_MODEL), lambda i: (i, 0)),
                  pl.BlockSpec((D_MODEL, D_MODEL), lambda i: (0, 0)),
                  _tokens_spec(), pl.BlockSpec((ROW_TILE, D_MODEL), lambda i: (0, 0)),
                  pl.BlockSpec((1, D_MODEL), lambda i: (0, 0)), _tokens_spec()],
        out_specs=[pl.BlockSpec((ROW_TILE, D_MODEL), lambda i: (i, 0)),
                   pl.BlockSpec((1, LANES), lambda i: (0, 0)), pl.BlockSpec((1, D_MODEL), lambda i: (0, 0))],
        out_shape=[jax.ShapeDtypeStruct((lp, D_MODEL), F32), jax.ShapeDtypeStruct((1, LANES), F32),
                   jax.ShapeDtypeStruct((1, D_MODEL), F32)],
        compiler_params=_params(("arbitrary",)),
    )(merged, w_out, x2, metapad, gf, target)


def _per_head_rowsum(t):
    head = lax.broadcasted_iota(jnp.int32, t.shape, 1) // HEAD_DIM
    out = jnp.zeros_like(t)
    for h in range(N_HEADS):
        sel = head == h
        out = jnp.where(sel, jnp.sum(jnp.where(sel, t, 0.0), axis=1, keepdims=True), out)
    return out


def _merge_bwd(dh1, w_out, y_pool, y_attn, wup_p, wup_a, e, o):
    lp = o.shape[0]
    nb = lp // ROW_TILE

    def body(dh_ref, wo_ref, yp_ref, ya_ref, wp_ref, wa_ref, e_ref, o_ref,
             dap_ref, daa_ref, dg_ref, do_ref, delta_ref, dyp_ref):
        dmerged = _dot_nt(dh_ref[...].astype(BF16), wo_ref[...])
        a_pool = jnp.dot(yp_ref[...], wp_ref[...], preferred_element_type=F32)
        a_attn = jnp.dot(ya_ref[...], wa_ref[...], preferred_element_type=F32)
        sp = _sigmoid(e_ref[:, E_GP:E_GA].astype(F32))
        sa = _sigmoid(e_ref[:, E_GA:E_COLS].astype(F32))
        dap = (dmerged * sp).astype(BF16)
        daa = (dmerged * sa).astype(BF16)
        dap_ref[...] = dap
        daa_ref[...] = daa
        dg_ref[:, ATTN_WIDTH:ATTN_WIDTH + D_MODEL] = (dmerged * a_pool * (sp * (1.0 - sp))).astype(BF16)
        dg_ref[:, ATTN_WIDTH + D_MODEL:] = (dmerged * a_attn * (sa * (1.0 - sa))).astype(BF16)
        dyp_ref[...] = _dot_nt(dap, wp_ref[...])
        dya = _dot_nt(daa, wa_ref[...])
        za = e_ref[:, E_ZA:E_GP].astype(F32)
        sz = _sigmoid(za)
        o = o_ref[...]
        do = dya * (za * sz)
        do_ref[...] = do.astype(BF16)
        dg_ref[:, :ATTN_WIDTH] = (dya * o * (sz * (1.0 + za * (1.0 - sz)))).astype(BF16)
        delta_ref[...] = _per_head_rowsum(do * o)

    row = lambda w: pl.BlockSpec((ROW_TILE, w), lambda i: (i, 0))
    full = lambda a: pl.BlockSpec(a.shape, lambda i: (0, 0))
    return pl.pallas_call(
        body, name="merge_bwd", grid=(nb,),
        in_specs=[row(D_MODEL), full(w_out), row(POOL_WIDTH), row(ATTN_WIDTH), full(wup_p), full(wup_a),
                  row(E_COLS), row(ATTN_WIDTH)],
        out_specs=[row(D_MODEL), row(D_MODEL), pl.BlockSpec((ROW_TILE, MAIN_COLS // 2), lambda i: (i, 1)),
                   row(ATTN_WIDTH), row(ATTN_WIDTH), row(POOL_WIDTH)],
        out_shape=[jax.ShapeDtypeStruct((lp, D_MODEL), BF16), jax.ShapeDtypeStruct((lp, D_MODEL), BF16),
                   jax.ShapeDtypeStruct((lp, MAIN_COLS), BF16),
                   jax.ShapeDtypeStruct((lp, ATTN_WIDTH), BF16), jax.ShapeDtypeStruct((lp, ATTN_WIDTH), F32),
                   jax.ShapeDtypeStruct((lp, POOL_WIDTH), F32)],
        compiler_params=_params(("parallel",)),
    )(dh1, w_out, y_pool, y_attn, wup_p, wup_a, e, o)


def _fill_dproj(dproj, dq, dk, dv):
    lp = dproj.shape[0]
    rows = _row_tile(lp, 1100)

    def body(base_ref, dq_ref, dk_ref, dv_ref, o_ref):
        j = pl.program_id(1)

        @pl.when(j == 0)
        def _():
            o_ref[...] = (dq_ref[...] * HEAD_DIM ** -0.5).astype(BF16)

        @pl.when(j == 1)
        def _():
            o_ref[...] = dk_ref[...]

        @pl.when(j == 2)
        def _():
            o_ref[...] = dv_ref[...]

    blk = pl.BlockSpec((rows, ATTN_WIDTH), lambda i, j: (i, 0))
    return pl.pallas_call(
        body, name="fill_dproj", grid=(lp // rows, 3),
        in_specs=[pl.BlockSpec(memory_space=pl.ANY)] + [blk] * 3,
        out_specs=pl.BlockSpec((rows, ATTN_WIDTH), lambda i, j: (i, 2 + j)),
        out_shape=jax.ShapeDtypeStruct(dproj.shape, BF16),
        input_output_aliases={0: 0},
        compiler_params=_params(("parallel", "arbitrary")),
    )(dproj, dq, dk, dv)


def _pool_bwd_local(e, dy_pool, pw, scale, dproj):
    lp = e.shape[0]
    nb = lp // ROW_TILE
    ng = len(POOL_WINDOWS)

    def body(uc_ref, up_ref, z_ref, dy_ref, pw_ref, sc_ref, base_ref, dpc_ref, dz_ref, dsc_ref, dpw_ref):
        i = pl.program_id(0)

        @pl.when(i == 0)
        def _():
            dsc_ref[...] = jnp.zeros_like(dsc_ref)
            dpw_ref[...] = jnp.zeros_like(dpw_ref)

        u_cur = uc_ref[...].astype(F32)
        u_prev = jnp.where(i == 0, 0.0, up_ref[...].astype(F32))
        ps, invs = _pool_p(u_cur, u_prev, i)
        z = z_ref[...].astype(F32)
        sz = _sigmoid(z)
        dy = dy_ref[...]
        dypre = dy * (z * sz)
        dsilu = sz * (1.0 + z * (1.0 - sz))
        for g in range(ng):
            sl = slice(g * POOL_GROUP, (g + 1) * POOL_GROUP)
            pb = ps[g].astype(BF16)
            w = pw_ref[g]
            yraw = jnp.dot(pb, w, preferred_element_type=F32)
            sc = sc_ref[:, sl]
            dz_ref[:, sl] = (dy[:, sl] * (yraw * sc) * dsilu[:, sl]).astype(BF16)
            dsc_ref[:, sl] += jnp.sum(dypre[:, sl] * yraw, axis=0, keepdims=True)
            dyraw = (dypre[:, sl] * sc).astype(BF16)
            dpw_ref[g] += _dot_tn(pb, dyraw)
            dpc_ref[:, sl] = _dot_nt(dyraw, w) * invs[g]

    blk = (ROW_TILE, POOL_WIDTH)
    return pl.pallas_call(
        body, name="pool_bwd_local", grid=(nb,),
        in_specs=[pl.BlockSpec(blk, lambda i: (i, 0)), pl.BlockSpec(blk, lambda i: (jnp.maximum(i - 1, 0), 0)),
                  pl.BlockSpec(blk, lambda i: (i, 1)), pl.BlockSpec(blk, lambda i: (i, 0)),
                  pl.BlockSpec((ng, POOL_GROUP, POOL_GROUP), lambda i: (0, 0, 0)),
                  pl.BlockSpec((1, POOL_WIDTH), lambda i: (0, 0)), pl.BlockSpec(memory_space=pl.ANY)],
        out_specs=[pl.BlockSpec(blk, lambda i: (i, 0)), pl.BlockSpec(blk, lambda i: (i, 1)),
                   pl.BlockSpec((1, POOL_WIDTH), lambda i: (0, 0)),
                   pl.BlockSpec((ng, POOL_GROUP, POOL_GROUP), lambda i: (0, 0, 0))],
        out_shape=[jax.ShapeDtypeStruct((lp, POOL_WIDTH), F32), jax.ShapeDtypeStruct(dproj.shape, BF16),
                   jax.ShapeDtypeStruct((1, POOL_WIDTH), F32),
                   jax.ShapeDtypeStruct((ng, POOL_GROUP, POOL_GROUP), F32)],
        input_output_aliases={6: 1},
        compiler_params=_params(("arbitrary",)),
    )(e, e, e, dy_pool, pw, scale, dproj)


def _pool_bwd_window(dpc, dproj):
    lp = dpc.shape[0]
    nb = lp // ROW_TILE

    def body(cur_ref, nxt_ref, base_ref, du_ref):
        i = pl.program_id(0)
        cur = cur_ref[...]
        nxt = jnp.where(i == nb - 1, 0.0, nxt_ref[...])
        pos = _pool_counts(i)
        for g, w in enumerate(POOL_WINDOWS):
            sl = slice(g * POOL_GROUP, (g + 1) * POOL_GROUP)
            xc = jnp.concatenate([cur[:, sl], nxt[:, sl]], axis=0)
            win = _leading_sums(xc, g + 1)[:ROW_TILE, :]
            dp = cur[:, sl] * jnp.minimum(pos + 1, w).astype(F32)
            du_ref[:, sl] = (win - dp).astype(BF16)

    blk = (ROW_TILE, POOL_WIDTH)
    return pl.pallas_call(
        body, name="pool_bwd_window", grid=(nb,),
        in_specs=[pl.BlockSpec(blk, lambda i: (i, 0)), pl.BlockSpec(blk, lambda i: (jnp.minimum(i + 1, nb - 1), 0)),
                  pl.BlockSpec(memory_space=pl.ANY)],
        out_specs=pl.BlockSpec(blk, lambda i: (i, 0)),
        out_shape=jax.ShapeDtypeStruct(dproj.shape, BF16),
        input_output_aliases={2: 0},
        compiler_params=_params(("parallel",)),
    )(dpc, dpc, dproj)


def _attn_bwd(qkv, do, lse, delta, bias, nb):
    lk = qkv.shape[0]
    lp = nb * ROW_TILE
    nkb = lk // KV_TILE_BWD
    n_pairs = N_HEADS // 2
    per_kv = KV_TILE_BWD // ROW_TILE

    def body(q_ref, k_ref, v_ref, do_ref, lse_ref, dl_ref, b_ref, dq_ref, dk_ref, dv_ref, dc_ref, dcq_ref,
             dk_acc, dv_acc, dc_acc):
        jj = pl.program_id(1)

        @pl.when(jj == 0)
        def _():
            dq_ref[...] = jnp.zeros_like(dq_ref)
            dcq_ref[...] = jnp.zeros_like(dcq_ref)

        dk_acc[...] = jnp.zeros_like(dk_acc)
        dv_acc[...] = jnp.zeros_like(dv_acc)
        dc_acc[...] = jnp.zeros_like(dc_acc)

        def block(i, n_keys, masked):
            kb, vb = k_ref[:n_keys, :], v_ref[:n_keys, :]
            rows = pl.ds(pl.multiple_of(i * ROW_TILE, ROW_TILE), ROW_TILE)
            qs = _stack_heads(q_ref[rows, :])
            dos = _stack_heads(do_ref[rows, :])
            lse_i, dl_i = lse_ref[rows, :], dl_ref[rows, :]
            s = _dot_nt(qs, kb)
            dp = _dot_nt(dos, vb)
            if masked:
                valid = _causal(i, jj, 1, KV_TILE_BWD)[:, :n_keys]
            ps, dss, dcs, rowsums = [], [], [], []
            for hd in range(2):
                half = slice(hd * ROW_TILE, (hd + 1) * ROW_TILE)
                col = slice(hd * HEAD_DIM, hd * HEAD_DIM + 1)
                sh = s[half] - b_ref[i, 0, hd:hd + 1, :n_keys]
                if masked:
                    sh = jnp.where(valid, sh, NEG)
                p = jnp.exp(sh - lse_i[:, col])
                ds = p * (dp[half] - dl_i[:, col])
                ps.append(p.astype(BF16))
                dss.append(ds.astype(BF16))
                dcs.append(jnp.sum(ds, axis=0, keepdims=True))
                rowsums.append(jnp.sum(ds, axis=1, keepdims=True))
            dsb = jnp.concatenate(dss, axis=0)
            dv_acc[:n_keys, :] += _dot_tn(jnp.concatenate(ps, axis=0), dos)
            dk_acc[:n_keys, :] += _dot_tn(dsb, qs)
            dc_acc[:, :n_keys] -= jnp.concatenate(dcs, axis=0)
            dq_ref[rows, :] += _unstack_heads(jnp.dot(dsb, kb, preferred_element_type=F32))
            dcq_ref[rows, :] += _unstack_heads(jnp.broadcast_to(jnp.concatenate(rowsums, axis=0), (2 * ROW_TILE, LANES)))

        first_q = per_kv * jj
        for r in range(per_kv):
            @pl.when(first_q + r < nb)
            def _():
                block(first_q + r, (r + 1) * ROW_TILE, True)

        def rest(i, carry):
            block(i, KV_TILE_BWD, False)
            return carry

        lax.fori_loop(jnp.minimum(first_q + per_kv, nb), nb, rest, 0)
        dk_ref[...] = dk_acc[...].astype(BF16)
        dv_ref[...] = dv_acc[...].astype(BF16)
        dc_ref[0, 0] = dc_acc[...]

    whole = lambda rows: pl.BlockSpec((rows, LANES), lambda hp, jj: (0, hp))
    kv_blk = lambda off: pl.BlockSpec((KV_TILE_BWD, LANES), lambda hp, jj: (jj, off + hp))
    return pl.pallas_call(
        body, name="attn_bwd", grid=(n_pairs, nkb),
        in_specs=[whole(lk), kv_blk(n_pairs), kv_blk(2 * n_pairs), whole(lp), whole(lp), whole(lp),
                  pl.BlockSpec((nb, 1, 2, KV_TILE_BWD), lambda hp, jj: (0, hp, 0, jj))],
        out_specs=[whole(lp), kv_blk(0), kv_blk(0),
                   pl.BlockSpec((1, 1, 2, KV_TILE_BWD), lambda hp, jj: (hp, jj, 0, 0)), whole(lp)],
        out_shape=[jax.ShapeDtypeStruct((lp, ATTN_WIDTH), F32), jax.ShapeDtypeStruct((lk, ATTN_WIDTH), BF16),
                   jax.ShapeDtypeStruct((lk, ATTN_WIDTH), BF16),
                   jax.ShapeDtypeStruct((n_pairs, nkb, 2, KV_TILE_BWD), F32),
                   jax.ShapeDtypeStruct((lp, ATTN_WIDTH), F32)],
        scratch_shapes=[pltpu.VMEM((KV_TILE_BWD, LANES), F32), pltpu.VMEM((KV_TILE_BWD, LANES), F32),
                        pltpu.VMEM((2, KV_TILE_BWD), F32)],
        compiler_params=_params(("parallel", "arbitrary")),
    )(qkv, qkv, qkv, do, lse, delta, bias)


def _gates_bwd(dc, dcq, f, bfg):
    nb = f.shape[0] // ROW_TILE

    def body(dc_ref, dcq_ref, f_ref, b_ref, df_ref, db_ref, carry):
        step = pl.program_id(0)
        i = nb - 1 - step

        @pl.when(step == 0)
        def _():
            carry[...] = jnp.zeros_like(carry)
            db_ref[...] = jnp.zeros_like(db_ref)

        dcb = dc_ref[...]
        lane = lax.broadcasted_iota(jnp.int32, (ROW_TILE, F_COLS), 1)
        for h in range(N_HEADS):
            dcb = dcb + jnp.where(lane == h, dcq_ref[:, HEAD_DIM * h:HEAD_DIM * h + 1], 0.0)
        r_i = lax.broadcasted_iota(jnp.int32, (ROW_TILE, ROW_TILE), 0)
        c_i = lax.broadcasted_iota(jnp.int32, (ROW_TILE, ROW_TILE), 1)
        upper = (c_i >= r_i).astype(F32)
        dlf = jnp.dot(upper, dcb, precision=HIGHEST, preferred_element_type=F32) + carry[...]
        carry[...] = carry[...] + jnp.sum(dcb, axis=0, keepdims=True)
        logit = f_ref[...] + b_ref[...]
        dlogit = jnp.where(_valid_gate_mask(i), dlf * _sigmoid(-logit), 0.0)
        df_ref[...] = dlogit.astype(BF16)
        db_ref[...] += jnp.sum(dlogit, axis=0, keepdims=True)

    blk = pl.BlockSpec((ROW_TILE, F_COLS), lambda s: (nb - 1 - s, 0))
    wide = pl.BlockSpec((ROW_TILE, ATTN_WIDTH), lambda s: (nb - 1 - s, 0))
    one = pl.BlockSpec((1, F_COLS), lambda s: (0, 0))
    return pl.pallas_call(
        body, name="gates_bwd", grid=(nb,),
        in_specs=[blk, wide, blk, one], out_specs=[blk, one],
        out_shape=[jax.ShapeDtypeStruct(f.shape, BF16), jax.ShapeDtypeStruct((1, F_COLS), F32)],
        scratch_shapes=[pltpu.VMEM((1, F_COLS), F32)],
        compiler_params=_params(("arbitrary",)),
    )(dc, dcq, f, bfg)


def _input_bwd(dproj, df, wt_e, wt_qkv, wt_f, x2, metapad, dh1, g1, hs, sc=None):
    nb = x2.shape[0] // ROW_TILE + 1
    n = len(hs)
    sc_in, sc_shapes, sc_sems = _scatter_operands(hs, sc)
    nq = len(sc_in)

    def body(dp_ref, df_ref, we_ref, w_ref, wf_ref, x_ref, mp_ref, dh_ref, g_ref, *rest):
        h_refs, s_ref = rest[:n], (rest[n] if sc is not None else None)
        gx_ref, g0_ref, dg_ref = rest[nq:nq + 3]
        q_refs, sq_ref = rest[nq + 3:nq + 3 + n], (rest[nq + 3 + n] if sc is not None else None)
        sems = rest[2 * nq + 3:]
        i = pl.program_id(0)

        @pl.when(i == 0)
        def _():
            dg_ref[...] = jnp.zeros_like(dg_ref)
            own, sends, _ = _chip_scatter_plan(h_refs, s_ref, q_refs, sq_ref, *sems)
            for cp in own + sends:
                cp.start()

        dhn = (jnp.dot(dp_ref[:, :E_ZA], we_ref[:E_ZA, :], preferred_element_type=F32)
               + jnp.dot(dp_ref[:, E_ZA:MAIN_COLS // 2], w_ref[...], preferred_element_type=F32)
               + jnp.dot(dp_ref[:, MAIN_COLS // 2:], we_ref[E_ZA:, :], preferred_element_type=F32)
               + jnp.dot(df_ref[...], wf_ref[...], preferred_element_type=F32))
        h0 = jnp.where(i == 0, mp_ref[...], x_ref[...])
        r = lax.rsqrt(jnp.mean(h0 * h0, axis=-1, keepdims=True) + RMS_EPS)
        xhat = h0 * r
        dg_ref[...] += jnp.sum(dhn * xhat, axis=0, keepdims=True)
        dxhat = dhn * g_ref[...]
        dh0 = dh_ref[...] + r * (dxhat - xhat * jnp.mean(dxhat * xhat, axis=-1, keepdims=True))
        gx_ref[...] = dh0

        @pl.when(i == 0)
        def _():
            g0_ref[...] = dh0

        @pl.when(i == nb - 1)
        def _():
            own, sends, recvs = _chip_scatter_plan(h_refs, s_ref, q_refs, sq_ref, *sems)
            for cp in recvs:
                cp.wait_recv()
            for cp in sends:
                cp.wait_send()
            for cp in own:
                cp.wait()

    const = lambda shape: pl.BlockSpec(shape, lambda i: (0, 0))
    res = pl.pallas_call(
        body, name="input_bwd", grid=(nb,),
        in_specs=[pl.BlockSpec((ROW_TILE, MAIN_COLS), lambda i: (i, 0)), pl.BlockSpec((ROW_TILE, F_COLS), lambda i: (i, 0)),
                  const((E_COLS, D_MODEL)), const((QKV_COLS, D_MODEL)), const((F_COLS, D_MODEL)),
                  _tokens_spec(), const((ROW_TILE, D_MODEL)),
                  pl.BlockSpec((ROW_TILE, D_MODEL), lambda i: (i, 0)), const((1, D_MODEL))] + [HBM] * nq,
        out_specs=[_tokens_spec(), const((ROW_TILE, D_MODEL)), const((1, D_MODEL))] + [HBM] * nq,
        out_shape=[jax.ShapeDtypeStruct(x2.shape, F32), jax.ShapeDtypeStruct((ROW_TILE, D_MODEL), F32),
                   jax.ShapeDtypeStruct((1, D_MODEL), F32)] + sc_shapes,
        scratch_shapes=sc_sems,
        compiler_params=_params(("arbitrary",), vmem=56 * 1024 * 1024),
    )(dproj, df, wt_e, wt_qkv, wt_f, x2, metapad, dh1, g1, *_in_hbm(*sc_in))
    return res[:3], res[3:3 + n], (res[3 + n] if sc is not None else None)


def _adamw(w, g, m, v, name):
    rows, cols = w.shape
    if rows % 8 == 0:
        tr, tc = _row_tile8(rows), cols
    else:
        tr, tc = rows, (2 * LANES if cols % (2 * LANES) == 0 and rows > 8 else cols)

    def body(w_ref, g_ref, m_ref, v_ref, d_ref, mo_ref, vo_ref):
        g_ = g_ref[...]
        m_new = ADAM_B1 * m_ref[...] + (1.0 - ADAM_B1) * g_
        v_new = ADAM_B2 * v_ref[...] + (1.0 - ADAM_B2) * (g_ * g_)
        m_hat = m_new / (1.0 - ADAM_B1 ** ADAM_STEP)
        v_hat = v_new / (1.0 - ADAM_B2 ** ADAM_STEP)
        d_ref[...] = -ADAM_LR * (m_hat / (jnp.sqrt(v_hat) + ADAM_EPS) + ADAM_WD * w_ref[...])
        mo_ref[...] = m_new
        vo_ref[...] = v_new

    blk = pl.BlockSpec((tr, tc), lambda i, j: (i, j))
    return pl.pallas_call(
        body, name=name, grid=(rows // tr, cols // tc),
        in_specs=[blk] * 4, out_specs=[blk] * 3,
        out_shape=[jax.ShapeDtypeStruct(w.shape, F32)] * 3,
        compiler_params=_params(("parallel", "parallel")),
    )(w, g, m, v)


def _adamw_native(w3, g3, m3, v3, name):
    rows = w3.shape[0]
    tr = rows // 2
    blk = pl.BlockSpec((tr,) + w3.shape[1:], lambda i: (i, 0, 0))
    shape = jax.ShapeDtypeStruct(w3.shape, F32)

    def moments(g_ref, m_ref, v_ref, mo_ref, vo_ref):
        g_ = g_ref[...]
        mo_ref[...] = ADAM_B1 * m_ref[...] + (1.0 - ADAM_B1) * g_
        vo_ref[...] = ADAM_B2 * v_ref[...] + (1.0 - ADAM_B2) * (g_ * g_)

    new_m, new_v = pl.pallas_call(
        moments, name=name + "_moments", grid=(2,), in_specs=[blk] * 3, out_specs=[blk] * 2, out_shape=[shape] * 2,
        compiler_params=_params(("parallel",)),
    )(g3, m3, v3)

    def delta(w_ref, m_ref, v_ref, d_ref):
        m_hat = m_ref[...] / (1.0 - ADAM_B1 ** ADAM_STEP)
        v_hat = v_ref[...] / (1.0 - ADAM_B2 ** ADAM_STEP)
        d_ref[...] = -ADAM_LR * (m_hat / (jnp.sqrt(v_hat) + ADAM_EPS) + ADAM_WD * w_ref[...])

    d = pl.pallas_call(
        delta, name=name + "_delta", grid=(2,), in_specs=[blk] * 3, out_specs=blk, out_shape=shape,
        compiler_params=_params(("parallel",)),
    )(w3, new_m, new_v)
    return d, new_m, new_v


def _row_tile8(rows):
    best = rows
    for t in range(8, 257, 8):
        if rows % t == 0:
            best = t
    return best


def kernel(x, meta_tokens, norm_g, w_in, b_forget, pool_w, pool_scale, w_up_pool, w_up_attn, w_out, final_norm_g, loss_target, m_meta_tokens, m_norm_g, m_w_in, m_b_forget, m_pool_w, m_pool_scale, m_w_up_pool, m_w_up_attn, m_w_out, m_final_norm_g, v_meta_tokens, v_norm_g, v_w_in, v_b_forget, v_pool_w, v_pool_scale, v_w_up_pool, v_w_up_attn, v_w_out, v_final_norm_g):
    seq = x.shape[1]
    assert seq % ROW_TILE == 0 and x.shape[0] == 1
    lp = seq + ROW_TILE
    nb = lp // ROW_TILE
    lk = -(-lp // KV_TILE_BWD) * KV_TILE_BWD
    core = jnp.reshape(lax.axis_index("c"), (1,)).astype(jnp.int32)
    x2 = x[0]
    target = loss_target[0]
    sh_d = D_MODEL // N_CHIPS

    to_rows = lambda a: jnp.transpose(a, (2, 0, 1))
    from_rows = lambda a: jnp.transpose(a, (1, 2, 0))
    gf = final_norm_g.reshape(1, D_MODEL)
    bfg = jnp.pad(b_forget, ((0, 0), (0, F_COLS - N_HEADS)))
    pw_b = pool_w[0].astype(BF16)
    hn, (wg_in, meta_g) = _rmsnorm_fwd(x2, norm_g, lk, [jnp.transpose(w_in[0]).astype(BF16), meta_tokens], [1, 0])
    wt_e, wt_qkv, wt_f = _weight_sections(wg_in)
    meta_full = jnp.transpose(meta_g, (1, 0, 2)).reshape(N_META, D_MODEL)
    metapad = jnp.pad(meta_full, ((PAD_ROWS, 0), (0, 0)))

    tm = _row_tile(lp, 2200)
    e, (wg_up_p, wg_up_a, wg_out) = _mm_nt(
        hn, wt_e, BF16, "in_proj_gates", lp, E_COLS, tm, 512,
        gather=([w_up_pool[0].astype(BF16), w_up_attn[0].astype(BF16), w_out[0].astype(BF16)], [0, 0, 0]))
    wup_p = jnp.transpose(wg_up_p, (1, 0, 2)).reshape(POOL_WIDTH, D_MODEL)
    wup_a = jnp.transpose(wg_up_a, (1, 0, 2)).reshape(ATTN_WIDTH, D_MODEL)
    wout = wg_out.reshape(D_MODEL, D_MODEL)
    qkv = _mm_nt(hn, wt_qkv, BF16, "in_proj_qkv", lk, QKV_COLS, _row_tile(lk, 2600), 512, scale_first=HEAD_DIM ** -0.5)
    f = _mm_nt(hn, wt_f, F32, "in_proj_forget", lp, F_COLS, tm, F_COLS)
    c = _gates_fwd(f, bfg)
    c_t = jnp.transpose(c[:, :N_HEADS])
    c_first = jnp.transpose(c_t[:, ::ROW_TILE]).reshape(nb, N_HEADS // 2, 2, 1)
    c_keys = jnp.pad(c_t, ((0, 0), (0, lk - lp))).reshape(1, N_HEADS // 2, 2, lk)
    bias = jnp.where(jnp.arange(lk) < PAD_ROWS, -NEG, c_keys - c_first)
    y_pool = _pool_fwd(e, pw_b, pool_scale)
    o, lse = _attn_fwd(qkv, bias, nb)
    merged, y_attn = _merge_fwd(y_pool, o, e, wup_p, wup_a)
    dh1, loss_part, dgf = _head_fwd_bwd(merged, wout, x2, metapad, gf, target)

    dap, daa, dproj_half, do, delta, dy_pool = _merge_bwd(dh1, wout, y_pool, y_attn, wup_p, wup_a, e, o)
    dpc, dproj_zp, dscale, dpw = _pool_bwd_local(e, dy_pool, pw_b, pool_scale, dproj_half)
    dproj_u = _pool_bwd_window(dpc, dproj_zp)
    dq, dk, dv, dc4, dcq = _attn_bwd(qkv, do, lse, delta, bias, nb)
    dc = jnp.transpose(dc4, (1, 3, 0, 2)).reshape(lk, N_HEADS)[:lp]
    df, db = _gates_bwd(jnp.pad(dc, ((0, 0), (0, F_COLS - N_HEADS))), dcq, f, bfg)
    dproj = _fill_dproj(dproj_u, dq, dk, dv)
    tk = _row_tile(lp, 1100)
    dw_out = _mm_tn(merged, dh1, "grad_w_out", lp, 512, D_MODEL, tk)
    dw_up_p = _mm_tn(y_pool, dap, "grad_w_up_pool", lp, 512, D_MODEL, lp, chunks=N_CHIPS)
    dw_up_a = _mm_tn(y_attn, daa, "grad_w_up_attn", lp, 512, D_MODEL, lp, chunks=N_CHIPS)
    dwt_f = _mm_tn(df, hn, "grad_w_in_forget", lp, F_COLS, D_MODEL, tk)

    def pad8(a):
        return jnp.pad(a, ((0, (-a.shape[0]) % 8), (0, 0)))

    small_parts = [pad8(a) for a in (dgf.reshape(-1, LANES), dscale.reshape(-1, LANES), db, dpw.reshape(-1, LANES),
                                     loss_part)]
    small = jnp.concatenate(small_parts, axis=0)
    soffs = [0]
    for p in small_parts:
        soffs.append(soffs[-1] + p.shape[0])

    gs_rows = [dw_up_p, dw_up_a, dw_out.reshape(N_CHIPS, sh_d, D_MODEL)]
    half = MAIN_COLS // 2
    dwt_a, (*rb_rows, rb_f, sr) = _mm_tn(dproj, hn, "grad_w_in_a", lp, half // 4, D_MODEL, lp, m=half, m_off=0,
                                         swap=(gs_rows + [dwt_f, small], [1, 1, 1, 1, None]))
    *hs_rows, sc = _add_sibling_half(gs_rows, rb_rows, [1, 1, 1], small, sr, core)
    dwt_b, (rb_a,), (qs_rows, sq) = _mm_tn(dproj, hn, "grad_w_in_b", lp, half // 4, D_MODEL, lp, m=half, m_off=4,
                                           swap=([dwt_a], [1]), scatter=(hs_rows, sc))
    h_in = _add_halves_w_in([dwt_a, dwt_b, dwt_f], [rb_a, rb_f], core)
    grad_axes = [2, 1, 1, 1]
    (grad_x, g_block0, dg1), (q_in,), _ = _input_bwd(dproj, df, wt_e, wt_qkv, wt_f, x2, metapad, dh1, norm_g, [h_in])
    late_parts = jnp.concatenate([pad8(dg1.reshape(-1, LANES)), g_block0[PAD_ROWS:].reshape(-1, LANES)], axis=0)
    g_w_in_rows, g_w_up_p, g_w_up_a, g_w_out, st, late = _reduce_allgather(
        [q_in] + list(qs_rows), grad_axes, sq, [True, False, False, False], late_parts)
    n_norm = D_MODEL // LANES
    g_norm = late[:n_norm].reshape(1, D_MODEL)
    chip = 2 * lax.axis_index("x") + lax.axis_index("y")
    g_meta = lax.dynamic_slice_in_dim(late[n_norm:].reshape(N_META, D_MODEL), chip * sh_d, sh_d, axis=1)

    spiece = lambda k, rows: st[soffs[k]:soffs[k] + rows]
    g_final = spiece(0, D_MODEL // LANES).reshape(1, D_MODEL)
    g_scale = spiece(1, POOL_WIDTH // LANES).reshape(1, POOL_WIDTH)
    g_bf = spiece(2, 1)
    g_pw = spiece(3, POOL_WIDTH)
    loss = st[soffs[4], 0]

    def pad_lanes(a):
        return jnp.pad(a, ((0, 0), (0, F_COLS - N_HEADS)))

    w_in_res = (g_w_in_rows,) + _adamw_native(to_rows(w_in), g_w_in_rows, to_rows(m_w_in), to_rows(v_w_in), "adamw_w_in")

    upd = [
        ("meta_tokens", meta_tokens, g_meta, m_meta_tokens, v_meta_tokens),
        ("norm_g", norm_g, g_norm, m_norm_g, v_norm_g),
        ("w_in", None, None, None, None),
        ("b_forget", pad_lanes(b_forget), g_bf, pad_lanes(m_b_forget), pad_lanes(v_b_forget)),
        ("pool_w", pool_w.reshape(-1, LANES), g_pw, m_pool_w.reshape(-1, LANES), v_pool_w.reshape(-1, LANES)),
        ("pool_scale", pool_scale, g_scale, m_pool_scale, v_pool_scale),
        ("w_up_pool", w_up_pool[0], g_w_up_p, m_w_up_pool[0], v_w_up_pool[0]),
        ("w_up_attn", w_up_attn[0], g_w_up_a, m_w_up_attn[0], v_w_up_attn[0]),
        ("w_out", w_out[0], g_w_out, m_w_out[0], v_w_out[0]),
        ("final_norm_g", gf, g_final, m_final_norm_g.reshape(1, D_MODEL), v_final_norm_g.reshape(1, D_MODEL)),
    ]
    shapes = [meta_tokens.shape, norm_g.shape, w_in.shape, b_forget.shape, pool_w.shape, pool_scale.shape,
              w_up_pool.shape, w_up_attn.shape, w_out.shape, final_norm_g.shape]
    grads, deltas, new_ms, new_vs = [], [], [], []
    for (name, w_, g_, m_in, v_in), shp in zip(upd, shapes):
        if name == "w_in":
            res = tuple(from_rows(a) for a in w_in_res)
        else:
            d_, mn_, vn_ = _adamw(w_, g_, m_in, v_in, "adamw_" + name)
            res = (g_, d_, mn_, vn_)
        if name == "b_forget":
            res = tuple(a[:, :N_HEADS] for a in res)
        for lst, a in zip((grads, deltas, new_ms, new_vs), res):
            lst.append(a.reshape(shp))

    return (loss, grad_x.reshape(x.shape), *grads, *deltas, *new_ms, *new_vs)
```

```python
import jax
import jax.numpy as jnp
from jax import lax
from jax.experimental import pallas as pl
from jax.experimental.pallas import tpu as pltpu

F32 = jnp.float32
BF16 = jnp.bfloat16
MESH = pl.DeviceIdType.MESH
HIGHEST = lax.Precision.HIGHEST
HBM = pl.BlockSpec(memory_space=pltpu.HBM)

D_MODEL = 1024
N_META = 16
POOL_WIDTH = 512
POOL_GROUP = 128
POOL_WINDOWS = (2, 4, 8, 16)
N_HEADS = 8
HEAD_DIM = 64
ATTN_WIDTH = 512
RMS_EPS = 1e-6
N_CHIPS = 4

ADAM_LR = 0.001
ADAM_B1 = 0.9
ADAM_B2 = 0.999
ADAM_EPS = 1e-08
ADAM_WD = 0.01
ADAM_STEP = 10

LANES = 128
ROW_TILE = 256
KV_TILE = 1024
KV_TILE_BWD = 1024
PAD_ROWS = ROW_TILE - N_META
NEG = -1e30

E_U, E_ZP, E_ZA, E_GP, E_GA, E_COLS = 0, 512, 1024, 1536, 2560, 3584
QKV_COLS = 3 * ATTN_WIDTH
MAIN_COLS = E_COLS + QKV_COLS
F_COLS = LANES
REF_U, REF_Q, REF_ZA, REF_F, REF_GP = 0, 1024, 2560, 3072, 3080
VMEM_LIMIT = 48 * 1024 * 1024


def _params(sem=None, vmem=VMEM_LIMIT):
    return pltpu.CompilerParams(dimension_semantics=sem, vmem_limit_bytes=vmem)


def _in_hbm(*arrays):
    return [pltpu.with_memory_space_constraint(a, pltpu.HBM) for a in arrays]


def _row_tile(n, target):
    best = 16
    for t in range(16, target + 1, 16):
        if n % t == 0:
            best = t
    return best


def _sigmoid(x):
    return 1.0 / (1.0 + jnp.exp(-x))


def _half(ref, axis, which):
    n = ref.shape[axis] // 2
    idx = [slice(None)] * len(ref.shape)
    idx[axis] = pl.ds(pl.multiple_of(which * n, n), n)
    return ref.at[tuple(idx)]


def _half_shape(shape, axis):
    s = list(shape)
    s[axis] //= 2
    return tuple(s)


def _gather_start(ins, outs, axes, send, recv, fsend, frecv, local):
    x, y, c = lax.axis_index("x"), lax.axis_index("y"), lax.axis_index("c")
    me = 2 * x + y
    for t in range(len(ins)):
        pltpu.make_async_copy(ins[t], outs[t].at[me], local.at[t]).start()
        for k, chip in enumerate([(1 - x, y), (x, 1 - y), (1 - x, 1 - y)]):
            pltpu.make_async_remote_copy(
                src_ref=_half(ins[t], axes[t], c), dst_ref=_half(outs[t].at[me], axes[t], c),
                send_sem=send.at[3 * t + k], recv_sem=recv.at[3 * t + k], device_id=(*chip, c), device_id_type=MESH).start()


def _gather_finish(ins, outs, axes, send, recv, fsend, frecv, local):
    x, y, c = lax.axis_index("x"), lax.axis_index("y"), lax.axis_index("c")
    me = 2 * x + y
    sibling = (x, y, 1 - c)
    chips = [(1 - x, y), (x, 1 - y), (1 - x, 1 - y)]
    n = len(ins)

    def over_ici(t, k, chip, dst_slot):
        return pltpu.make_async_remote_copy(
            src_ref=_half(ins[t], axes[t], c), dst_ref=_half(outs[t].at[dst_slot], axes[t], c),
            send_sem=send.at[3 * t + k], recv_sem=recv.at[3 * t + k], device_id=(*chip, c), device_id_type=MESH)

    def to_sibling(t, k, slot, which):
        return pltpu.make_async_remote_copy(
            src_ref=_half(outs[t].at[slot], axes[t], which), dst_ref=_half(outs[t].at[slot], axes[t], which),
            send_sem=fsend.at[3 * t + k], recv_sem=frecv.at[3 * t + k], device_id=sibling, device_id_type=MESH)

    forwards = []
    for t in range(n):
        for k, (px, py) in enumerate(chips):
            over_ici(t, k, (px, py), 2 * px + py).wait_recv()
            fw = to_sibling(t, k, 2 * px + py, c)
            fw.start()
            forwards.append(fw)
    for t in range(n):
        for k, (px, py) in enumerate(chips):
            to_sibling(t, k, 2 * px + py, 1 - c).wait_recv()
    for t in range(n):
        for k, chip in enumerate(chips):
            over_ici(t, k, chip, me).wait_send()
    for fw in forwards:
        fw.wait_send()
    for t in range(n):
        pltpu.make_async_copy(ins[t], outs[t].at[me], local.at[t]).wait()


def _gather_sems(n):
    return [pltpu.SemaphoreType.DMA((3 * n,)), pltpu.SemaphoreType.DMA((3 * n,)), pltpu.SemaphoreType.DMA((3 * n,)),
            pltpu.SemaphoreType.DMA((3 * n,)), pltpu.SemaphoreType.DMA((n,))]


def _gathered_shapes(shards):
    return [jax.ShapeDtypeStruct((N_CHIPS,) + s.shape, s.dtype) for s in shards]


def _swap_copies(srcs, dsts, axes, send, recv):
    x, y, c = lax.axis_index("x"), lax.axis_index("y"), lax.axis_index("c")
    return [pltpu.make_async_remote_copy(
        src_ref=srcs[t] if axes[t] is None else _half(srcs[t], axes[t], 1 - c), dst_ref=dsts[t],
        send_sem=send.at[t], recv_sem=recv.at[t], device_id=(x, y, 1 - c), device_id_type=MESH) for t in range(len(srcs))]


def _swap_shapes(srcs, axes):
    return [jax.ShapeDtypeStruct(g.shape if a is None else _half_shape(g.shape, a), g.dtype) for g, a in zip(srcs, axes)]


def _add_halves_w_in(parts, rbs, core):
    na, nb_rows, w = rbs[0].shape[0], parts[1].shape[0], rbs[0].shape[1]
    per = (na + nb_rows + N_HEADS) // N_CHIPS
    pieces = [[] for _ in range(N_CHIPS)]
    r = 0
    for src, lo, hi in ((0, 0, na), (1, 0, ATTN_WIDTH), (2, 0, N_HEADS), (1, ATTN_WIDTH, nb_rows)):
        while lo < hi:
            c, at = divmod(r, per)
            n = min(hi - lo, per - at)
            pieces[c].append((src, lo, n, at))
            lo, r = lo + n, r + n

    def body(core_ref, a_ref, b_ref, f_ref, ra_ref, rf_ref, o_ref, am, bm, fm, ar, br, fr, rows_buf, sems, send, recv):
        x, y, core_id = lax.axis_index("x"), lax.axis_index("y"), lax.axis_index("c")
        swap = pltpu.make_async_remote_copy(src_ref=_half(b_ref, 1, 1 - core_id), dst_ref=br, send_sem=send.at[0],
                                            recv_sem=recv.at[0], device_id=(x, y, 1 - core_id), device_id_type=MESH)
        swap.start()
        cols = pl.ds(pl.multiple_of(core_ref[0] * w, w), w)
        few = pl.ds(0, N_HEADS)
        loads = [(pltpu.make_async_copy(a_ref.at[:, cols], am, sems.at[0]), pltpu.make_async_copy(ra_ref, ar, sems.at[1])),
                 (pltpu.make_async_copy(b_ref.at[:, cols], bm, sems.at[2]),),
                 (pltpu.make_async_copy(f_ref.at[few, cols], fm, sems.at[3]),
                  pltpu.make_async_copy(rf_ref.at[few, :], fr, sems.at[4]))]
        for group in loads:
            for cp in group:
                cp.start()
        bufs = [(am, ar), (bm, br), (fm, fr)]
        there = set()
        for c in range(N_CHIPS):
            for src, lo, n, at in pieces[c]:
                if src not in there:
                    for cp in loads[src]:
                        cp.wait()
                    if src == 1:
                        swap.wait_recv()
                    there.add(src)
                mine, other = bufs[src]
                rows_buf[at:at + n, :] = mine[lo:lo + n, :] + other[lo:lo + n, :]
            o_ref[c] = rows_buf[0:per, :].astype(BF16)
        swap.wait_send()

    any_space = pl.BlockSpec(memory_space=pl.ANY)
    a, b, f = parts
    return pl.pallas_call(
        body, name="grad_add_sibling_w_in",
        grid_spec=pltpu.PrefetchScalarGridSpec(
            num_scalar_prefetch=1, grid=(1,),
            in_specs=[any_space, HBM, any_space, any_space, any_space],
            out_specs=pl.BlockSpec((N_CHIPS, per, w), lambda i, cr: (0, 0, 0)),
            scratch_shapes=[pltpu.VMEM((na, w), F32), pltpu.VMEM((nb_rows, w), F32), pltpu.VMEM((N_HEADS, w), F32),
                            pltpu.VMEM((na, w), F32), pltpu.VMEM((nb_rows, w), F32), pltpu.VMEM((N_HEADS, w), F32),
                            pltpu.VMEM((per + (-per) % 8, w), F32), pltpu.SemaphoreType.DMA((5,)),
                            pltpu.SemaphoreType.DMA((1,)), pltpu.SemaphoreType.DMA((1,))]),
        out_shape=jax.ShapeDtypeStruct((N_CHIPS, per, w), BF16),
        compiler_params=_params(("arbitrary",)),
    )(core, a, *_in_hbm(b), f, *rbs)


def _weight_sections(wg):
    per, d = wg.shape[1], wg.shape[2]
    n_ref = N_CHIPS * per
    units = [(0, 0, REF_U, REF_Q), (1, 0, REF_Q, REF_ZA), (0, REF_Q - REF_U, REF_ZA, REF_F), (2, 0, REF_F, REF_GP),
             (0, REF_Q - REF_U + REF_F - REF_ZA, REF_GP, n_ref)]
    plan = []
    for out, first, lo, hi in sorted(units, key=lambda u: u[2]):
        pieces, at = [], 0
        while lo < hi:
            c, src = divmod(lo, per)
            n = min(hi - lo, per - src)
            pieces.append((c, src, n, at))
            lo, at = lo + n, at + n
        plan.append((out, first, at, pieces))
    stage_rows = max(rows for _, _, rows, _ in plan)

    def body(g_ref, e_ref, q_ref, f_ref, gb, stage, sems):
        loads = [pltpu.make_async_copy(g_ref.at[c], gb.at[c], sems.at[c]) for c in range(N_CHIPS)]
        for cp in loads:
            cp.start()
        outs = [e_ref, q_ref, f_ref]
        there = set()
        for out, first, rows, pieces in plan:
            for c, _, _, _ in pieces:
                if c not in there:
                    loads[c].wait()
                    there.add(c)
            if len(pieces) == 1 and pieces[0][1] % 16 == 0 and rows % 16 == 0:
                c, src, _, _ = pieces[0]
                outs[out][first:first + rows, :] = gb[c, src:src + rows, :]
                continue
            for c, src, n, at in pieces:
                stage[at:at + n, :] = gb[c, src:src + n, :].astype(F32)
            padded = rows if rows % 16 == 0 else outs[out].shape[0] - first
            if padded > rows:
                stage[rows:padded, :] = jnp.zeros((padded - rows, d), F32)
            outs[out][first:first + padded, :] = stage[0:padded, :].astype(BF16)

    return pl.pallas_call(
        body, name="weight_sections",
        in_specs=[pl.BlockSpec(memory_space=pl.ANY)],
        out_shape=[jax.ShapeDtypeStruct((E_COLS, d), BF16), jax.ShapeDtypeStruct((QKV_COLS, d), BF16),
                   jax.ShapeDtypeStruct((F_COLS, d), BF16)],
        scratch_shapes=[pltpu.VMEM(wg.shape, BF16), pltpu.VMEM((stage_rows, d), F32), pltpu.SemaphoreType.DMA((N_CHIPS,))],
        compiler_params=_params(),
    )(wg)


def _add_sibling_half(gs, rbs, axes, small, sr, core):
    n = len(gs)

    def body(core_ref, *refs):
        g_refs, rb_refs = refs[:n], refs[n:2 * n]
        s_ref, sr_ref = refs[2 * n], refs[2 * n + 1]
        h_refs, sc_ref = refs[2 * n + 2:3 * n + 2], refs[3 * n + 2]
        for t in range(n):
            h_refs[t][...] = (g_refs[t][...] + rb_refs[t][...]).astype(BF16)
        sc_ref[...] = s_ref[...] + sr_ref[...]

    def mine(rb, axis):
        blk = (None,) + rb.shape[1:]
        if axis == 2:
            return pl.BlockSpec(blk, lambda j, cr: (j, 0, cr[0]))
        return pl.BlockSpec(blk, lambda j, cr: (j, cr[0], 0))

    chunk = lambda rb: pl.BlockSpec((None,) + rb.shape[1:], lambda j, cr: (j, 0, 0))
    whole = pl.BlockSpec(small.shape, lambda j, cr: (0, 0))
    return pl.pallas_call(
        body, name="grad_add_sibling",
        grid_spec=pltpu.PrefetchScalarGridSpec(
            num_scalar_prefetch=1, grid=(N_CHIPS,),
            in_specs=[mine(rb, a) for rb, a in zip(rbs, axes)] + [chunk(rb) for rb in rbs] + [whole, whole],
            out_specs=[chunk(rb) for rb in rbs] + [whole]),
        out_shape=[jax.ShapeDtypeStruct(rb.shape, BF16) for rb in rbs] + [jax.ShapeDtypeStruct(small.shape, F32)],
        compiler_params=_params(("arbitrary",)),
    )(core, *gs, *rbs, small, sr)


def _chip_scatter_plan(h_refs, s_ref, q_refs, sq_ref, send, recv, local):
    n = len(h_refs)
    nt = n + (s_ref is not None)
    x, y, c = lax.axis_index("x"), lax.axis_index("y"), lax.axis_index("c")
    me = 2 * x + y
    chips = [(1 - x, y), (x, 1 - y), (1 - x, 1 - y)]

    def copy(t, k, chip, src_slot, dst_slot):
        src = h_refs[t].at[src_slot] if t < n else s_ref
        dst = (q_refs[t] if t < n else sq_ref).at[dst_slot]
        return pltpu.make_async_remote_copy(src_ref=src, dst_ref=dst, send_sem=send.at[3 * t + k],
                                            recv_sem=recv.at[3 * t + k], device_id=(*chip, c), device_id_type=MESH)

    own = [pltpu.make_async_copy(h_refs[t].at[me], q_refs[t].at[me], local.at[t]) for t in range(n)]
    if s_ref is not None:
        own.append(pltpu.make_async_copy(s_ref, sq_ref.at[me], local.at[n]))
    sends = [copy(t, k, (px, py), 2 * px + py, me) for t in range(nt) for k, (px, py) in enumerate(chips)]
    recvs = [copy(t, k, (px, py), me, 2 * px + py) for t in range(nt) for k, (px, py) in enumerate(chips)]
    return own, sends, recvs


def _scatter_operands(hs, sc):
    small = [] if sc is None else [sc]
    nt = len(hs) + len(small)
    shapes = [jax.ShapeDtypeStruct(h.shape, h.dtype) for h in hs] + [
        jax.ShapeDtypeStruct((N_CHIPS,) + a.shape, a.dtype) for a in small]
    sems = [pltpu.SemaphoreType.DMA((3 * nt,)), pltpu.SemaphoreType.DMA((3 * nt,)), pltpu.SemaphoreType.DMA((nt,))]
    return list(hs) + small, shapes, sems


def _reduce_allgather(qs, axes, sq, as_rows, late):
    n = len(qs)
    shard_shapes, half_axes = [], []
    for q, a, rows_form in zip(qs, axes, as_rows):
        shape = [d * 2 if i == a - 1 else d for i, d in enumerate(q.shape[1:])]
        if rows_form:
            assert a == 2
            shape = [shape[0], 1, shape[1]]
        shard_shapes.append(tuple(shape))
        half_axes.append(2 if rows_form else a - 1)

    def body(*refs):
        q_refs, sq_ref, late_ref = refs[:n], refs[n], refs[n + 1]
        o_refs, st_ref, late_sum_ref = refs[n + 2:2 * n + 2], refs[2 * n + 2], refs[2 * n + 3]
        sib_buf, chip_buf, send, recv, late_send, late_recv = refs[2 * n + 4:]
        x, y, c = lax.axis_index("x"), lax.axis_index("y"), lax.axis_index("c")
        me = 2 * x + y
        chips = [(1 - x, y), (x, 1 - y), (1 - x, 1 - y)]

        def swap(t, which):
            return pltpu.make_async_remote_copy(
                src_ref=_half(o_refs[t], half_axes[t], which), dst_ref=_half(o_refs[t], half_axes[t], which),
                send_sem=send.at[t], recv_sem=recv.at[t], device_id=(x, y, 1 - c), device_id_type=MESH)

        def late_copy(k, chip, slot):
            return pltpu.make_async_remote_copy(src_ref=chip_buf.at[slot], dst_ref=chip_buf.at[slot], send_sem=late_send.at[1 + k],
                                                recv_sem=late_recv.at[1 + k], device_id=(*chip, c), device_id_type=MESH)

        late_swap = pltpu.make_async_remote_copy(src_ref=late_ref, dst_ref=sib_buf, send_sem=late_send.at[0],
                                                 recv_sem=late_recv.at[0], device_id=(x, y, 1 - c), device_id_type=MESH)
        late_swap.start()
        sent = []
        for t in range(n):
            q = q_refs[t]
            total = ((q[0].astype(F32) + q[1].astype(F32)) + q[2].astype(F32)) + q[3].astype(F32)
            if as_rows[t]:
                total = total.reshape(total.shape[0], 1, total.shape[1])
            _half(o_refs[t], half_axes[t], c)[...] = total
            cp = swap(t, c)
            cp.start()
            sent.append(cp)
            if t == 0:
                late_swap.wait()
                chip_buf[me] = late_ref[...] + sib_buf[...]
                late_sends = [late_copy(k, chip, me) for k, chip in enumerate(chips)]
                for lc in late_sends:
                    lc.start()
        st_ref[...] = ((sq_ref[0] + sq_ref[1]) + sq_ref[2]) + sq_ref[3]
        for t in range(n):
            swap(t, 1 - c).wait_recv()
        for cp in sent:
            cp.wait_send()
        for k, (px, py) in enumerate(chips):
            late_copy(k, (px, py), 2 * px + py).wait_recv()
        for lc in late_sends:
            lc.wait_send()
        late_sum_ref[...] = ((chip_buf[0] + chip_buf[1]) + chip_buf[2]) + chip_buf[3]

    vmem = pl.BlockSpec(memory_space=pltpu.VMEM)
    return pl.pallas_call(
        body, name="grad_reduce_allgather",
        in_specs=[vmem] * (n + 2), out_specs=[vmem] * (n + 2),
        out_shape=[jax.ShapeDtypeStruct(s, F32) for s in shard_shapes] + [jax.ShapeDtypeStruct(sq.shape[1:], F32),
                                                                         jax.ShapeDtypeStruct(late.shape, F32)],
        scratch_shapes=[pltpu.VMEM(late.shape, F32), pltpu.VMEM((N_CHIPS,) + late.shape, F32),
                        pltpu.SemaphoreType.DMA((n,)), pltpu.SemaphoreType.DMA((n,)),
                        pltpu.SemaphoreType.DMA((4,)), pltpu.SemaphoreType.DMA((4,))],
        compiler_params=_params(),
    )(*qs, sq, late)


def _dot_nt(a, b):
    return lax.dot_general(a, b, (((1,), (1,)), ((), ())), preferred_element_type=F32)


def _dot_tn(a, b):
    return lax.dot_general(a, b, (((0,), (0,)), ((), ())), preferred_element_type=F32)


def _mm_nt(a, bt, out_dtype, name, m, n, tm, tn, row_block=0, scale_first=None, gather=None):
    k = a.shape[1]
    shards, axes = gather if gather is not None else ([], [])
    ng = len(shards)
    grid = (m // tm, n // tn)

    def body(a_ref, b_ref, *rest):
        ins, o_ref, outs, sems = rest[:ng], rest[ng], rest[ng + 1:2 * ng + 1], rest[2 * ng + 1:]
        first = (pl.program_id(0) == 0) & (pl.program_id(1) == 0)
        last = (pl.program_id(0) == grid[0] - 1) & (pl.program_id(1) == grid[1] - 1)
        if ng:
            @pl.when(first)
            def _():
                _gather_start(ins, outs, axes, *sems)

        r = _dot_nt(a_ref[...], b_ref[...])
        if scale_first is not None:
            r = r * jnp.where(pl.program_id(1) == 0, scale_first, 1.0)
        o_ref[...] = r.astype(out_dtype)

        if ng:
            @pl.when(last)
            def _():
                _gather_finish(ins, outs, axes, *sems)

    res = pl.pallas_call(
        body, name=name, grid=grid,
        in_specs=[pl.BlockSpec((tm, k), lambda i, j: (i, 0)), pl.BlockSpec((tn, k), lambda i, j: (row_block + j, 0))]
        + [HBM] * ng,
        out_specs=[pl.BlockSpec((tm, tn), lambda i, j: (i, j))] + [HBM] * ng,
        out_shape=[jax.ShapeDtypeStruct((m, n), out_dtype)] + _gathered_shapes(shards),
        scratch_shapes=_gather_sems(ng) if ng else [],
        compiler_params=_params(("arbitrary", "arbitrary") if ng else ("parallel", "parallel")),
    )(a, bt, *_in_hbm(*shards))
    return (res[0], res[1:]) if ng else res[0]


def _mm_tn(a, b, name, k, tm, tn, tk, chunks=1, m=None, m_off=0, swap=None, scatter=None):
    m = a.shape[1] if m is None else m
    n = b.shape[1]
    cw = n // chunks
    srcs, axes = swap if swap is not None else ([], [])
    ns = len(srcs)
    hs, sc = scatter if scatter is not None else ([], None)
    sc_in, sc_shapes, sc_sems = _scatter_operands(hs, sc) if scatter is not None else ([], [], [])
    nh, nq = len(hs), len(sc_in)
    grid = (m // tm, n // tn, k // tk)

    def body(a_ref, b_ref, *rest):
        s_refs, h_refs = rest[:ns], rest[ns:ns + nq]
        o_ref = rest[ns + nq]
        d_refs, q_refs = rest[ns + nq + 1:2 * ns + nq + 1], rest[2 * ns + nq + 1:2 * ns + 2 * nq + 1]
        sems = rest[2 * ns + 2 * nq + 1:]
        swap_sems, scatter_sems = (sems[:2], sems[2:]) if ns else ((), sems)
        ids = [pl.program_id(d) for d in range(3)]

        def scatter_plan():
            small_in = h_refs[nh] if sc is not None else None
            small_out = q_refs[nh] if sc is not None else None
            return _chip_scatter_plan(h_refs[:nh], small_in, q_refs[:nh], small_out, *scatter_sems)

        if ns or nq:
            @pl.when((ids[0] == 0) & (ids[1] == 0) & (ids[2] == 0))
            def _():
                if ns:
                    for cp in _swap_copies(s_refs, d_refs, axes, *swap_sems):
                        cp.start()
                if nq:
                    own, sends, _ = scatter_plan()
                    for cp in own + sends:
                        cp.start()

        @pl.when(ids[2] == 0)
        def _():
            o_ref[...] = jnp.zeros_like(o_ref)
        r = _dot_tn(a_ref[...].astype(BF16), b_ref[...].astype(BF16))
        if chunks > 1:
            for c in range(chunks):
                o_ref[c] += r[:, c * cw:(c + 1) * cw]
        else:
            o_ref[...] += r

        if ns or nq:
            @pl.when((ids[0] == grid[0] - 1) & (ids[1] == grid[1] - 1) & (ids[2] == grid[2] - 1))
            def _():
                if ns:
                    for cp in _swap_copies(s_refs, d_refs, axes, *swap_sems):
                        cp.wait()
                if nq:
                    own, sends, recvs = scatter_plan()
                    for cp in recvs:
                        cp.wait_recv()
                    for cp in sends:
                        cp.wait_send()
                    for cp in own:
                        cp.wait()

    if chunks > 1:
        assert tn == n
        out_spec = pl.BlockSpec((chunks, tm, cw), lambda i, j, kk: (0, i, 0))
        out_shape = jax.ShapeDtypeStruct((chunks, m, cw), F32)
    else:
        out_spec = pl.BlockSpec((tm, tn), lambda i, j, kk: (i, j))
        out_shape = jax.ShapeDtypeStruct((m, n), F32)
    riders = ns + nq
    res = pl.pallas_call(
        body, name=name, grid=grid,
        in_specs=[pl.BlockSpec((tk, tm), lambda i, j, kk: (kk, m_off + i)), pl.BlockSpec((tk, tn), lambda i, j, kk: (kk, j))]
        + [HBM] * riders,
        out_specs=[out_spec] + [HBM] * riders, out_shape=[out_shape] + _swap_shapes(srcs, axes) + sc_shapes,
        scratch_shapes=([pltpu.SemaphoreType.DMA((ns,)), pltpu.SemaphoreType.DMA((ns,))] if ns else []) + sc_sems,
        compiler_params=_params(("arbitrary",) * 3 if riders else ("parallel", "parallel", "arbitrary")),
    )(a, b, *_in_hbm(*srcs, *sc_in))
    if not riders:
        return res[0]
    out = [res[0]]
    if ns:
        out.append(res[1:1 + ns])
    if nq:
        out.append((res[1 + ns:1 + ns + nh], res[1 + ns + nh] if sc is not None else None))
    return tuple(out)


def _tokens_spec():
    return pl.BlockSpec((ROW_TILE, D_MODEL), lambda i: (jnp.maximum(i - 1, 0), 0))


def _rmsnorm_fwd(x2, g1, lk, shards, axes):
    nb = x2.shape[0] // ROW_TILE + 1
    nblk = lk // ROW_TILE
    ng = len(shards)
    meta_cols = shards[-1].shape[1]

    def body(x_ref, g_ref, *rest):
        ins, hn_ref, outs = rest[:ng], rest[ng], rest[ng + 1:2 * ng + 1]
        sems, meta_buf, meta_sem = rest[2 * ng + 1:2 * ng + 6], rest[2 * ng + 6], rest[2 * ng + 7]
        s = pl.program_id(0)
        blk = (s + 1) % nblk

        @pl.when(s == 0)
        def _():
            _gather_start(ins, outs, axes, *sems)

        def normed(h):
            r = lax.rsqrt(jnp.mean(h * h, axis=-1, keepdims=True) + RMS_EPS)
            return ((h * r) * g_ref[...]).astype(BF16)

        @pl.when(s < nblk - 1)
        def _():
            hn_ref[...] = normed(jnp.where(blk >= nb, 0.0, x_ref[...]))

        @pl.when(s == nblk - 1)
        def _():
            _gather_finish(ins, outs, axes, *sems)
            fetch = pltpu.make_async_copy(outs[-1], meta_buf, meta_sem.at[0])
            fetch.start()
            fetch.wait()
            meta = jnp.concatenate([meta_buf[j] for j in range(N_CHIPS)], axis=1)
            hn_ref[...] = normed(jnp.concatenate([jnp.zeros((PAD_ROWS, D_MODEL), F32), meta], axis=0))

    res = pl.pallas_call(
        body, name="rmsnorm_fwd", grid=(nblk,),
        in_specs=[pl.BlockSpec((ROW_TILE, D_MODEL), lambda s: (jnp.clip((s + 1) % nblk - 1, 0, nb - 2), 0)),
                  pl.BlockSpec((1, D_MODEL), lambda s: (0, 0))] + [HBM] * ng,
        out_specs=[pl.BlockSpec((ROW_TILE, D_MODEL), lambda s: ((s + 1) % nblk, 0))] + [HBM] * ng,
        out_shape=[jax.ShapeDtypeStruct((lk, D_MODEL), BF16)] + _gathered_shapes(shards),
        scratch_shapes=_gather_sems(ng) + [pltpu.VMEM((N_CHIPS, N_META, meta_cols), F32), pltpu.SemaphoreType.DMA((1,))],
        compiler_params=_params(("arbitrary",)),
    )(x2, g1, *_in_hbm(*shards))
    return res[0], res[1:]


def _valid_gate_mask(i):
    row = i * ROW_TILE + lax.broadcasted_iota(jnp.int32, (ROW_TILE, F_COLS), 0)
    col = lax.broadcasted_iota(jnp.int32, (ROW_TILE, F_COLS), 1)
    return (row >= PAD_ROWS) & (col < N_HEADS)


def _gates_fwd(f, bfg):
    nb = f.shape[0] // ROW_TILE

    def body(f_ref, b_ref, c_ref, carry):
        i = pl.program_id(0)

        @pl.when(i == 0)
        def _():
            carry[...] = jnp.zeros_like(carry)

        logit = f_ref[...] + b_ref[...]
        lf = jnp.minimum(logit, 0.0) - jnp.log1p(jnp.exp(-jnp.abs(logit)))
        lf = jnp.where(_valid_gate_mask(i), lf, 0.0)
        r_i = lax.broadcasted_iota(jnp.int32, (ROW_TILE, ROW_TILE), 0)
        c_i = lax.broadcasted_iota(jnp.int32, (ROW_TILE, ROW_TILE), 1)
        tri = (c_i <= r_i).astype(F32)
        c_ref[...] = jnp.dot(tri, lf, precision=HIGHEST, preferred_element_type=F32) + carry[...]
        carry[...] = carry[...] + jnp.sum(lf, axis=0, keepdims=True)

    return pl.pallas_call(
        body, name="gates_fwd", grid=(nb,),
        in_specs=[pl.BlockSpec((ROW_TILE, F_COLS), lambda i: (i, 0)), pl.BlockSpec((1, F_COLS), lambda i: (0, 0))],
        out_specs=pl.BlockSpec((ROW_TILE, F_COLS), lambda i: (i, 0)),
        out_shape=jax.ShapeDtypeStruct(f.shape, F32),
        scratch_shapes=[pltpu.VMEM((1, F_COLS), F32)],
        compiler_params=_params(("arbitrary",)),
    )(f, bfg)


def _pool_counts(i):
    row = i * ROW_TILE + lax.broadcasted_iota(jnp.int32, (ROW_TILE, 1), 0)
    return jnp.maximum(row - PAD_ROWS, 0)


def _trailing_sums(xc, levels):
    acc = xc
    for lv in range(levels):
        acc = acc + pltpu.roll(acc, 1 << lv, 0)
    return acc


def _leading_sums(xc, levels):
    n = xc.shape[0]
    acc = xc
    for lv in range(levels):
        acc = acc + pltpu.roll(acc, n - (1 << lv), 0)
    return acc


def _pool_p(u_cur, u_prev, i):
    pos = _pool_counts(i)
    ps, invs = [], []
    for g, w in enumerate(POOL_WINDOWS):
        sl = slice(g * POOL_GROUP, (g + 1) * POOL_GROUP)
        cur = u_cur[:, sl]
        xc = jnp.concatenate([u_prev[:, sl], cur], axis=0)
        win = _trailing_sums(xc, g + 1)[ROW_TILE:, :]
        inv = 1.0 / jnp.minimum(pos + 1, w).astype(F32)
        ps.append(win * inv - cur)
        invs.append(inv)
    return ps, invs


def _pool_fwd(e, pw, scale):
    nb = e.shape[0] // ROW_TILE

    def body(uc_ref, up_ref, z_ref, pw_ref, sc_ref, y_ref):
        i = pl.program_id(0)
        u_cur = uc_ref[...].astype(F32)
        u_prev = jnp.where(i == 0, 0.0, up_ref[...].astype(F32))
        ps, _ = _pool_p(u_cur, u_prev, i)
        z = z_ref[...].astype(F32)
        gate = z * _sigmoid(z)
        for g in range(len(POOL_WINDOWS)):
            sl = slice(g * POOL_GROUP, (g + 1) * POOL_GROUP)
            yraw = jnp.dot(ps[g].astype(BF16), pw_ref[g], preferred_element_type=F32)
            y_ref[:, sl] = ((yraw * sc_ref[:, sl]) * gate[:, sl]).astype(BF16)

    blk = (ROW_TILE, POOL_WIDTH)
    return pl.pallas_call(
        body, name="pool_fwd", grid=(nb,),
        in_specs=[pl.BlockSpec(blk, lambda i: (i, 0)), pl.BlockSpec(blk, lambda i: (jnp.maximum(i - 1, 0), 0)),
                  pl.BlockSpec(blk, lambda i: (i, 1)),
                  pl.BlockSpec((len(POOL_WINDOWS), POOL_GROUP, POOL_GROUP), lambda i: (0, 0, 0)),
                  pl.BlockSpec((1, POOL_WIDTH), lambda i: (0, 0))],
        out_specs=pl.BlockSpec(blk, lambda i: (i, 0)),
        out_shape=jax.ShapeDtypeStruct((e.shape[0], POOL_WIDTH), BF16),
        compiler_params=_params(("parallel",)),
    )(e, e, e, pw, scale)


def _stack_heads(a):
    first = lax.broadcasted_iota(jnp.int32, a.shape, 1) < HEAD_DIM
    zero = jnp.zeros_like(a)
    return jnp.concatenate([jnp.where(first, a, zero), jnp.where(first, zero, a)], axis=0)


def _unstack_heads(a):
    rows = a.shape[0] // 2
    first = lax.broadcasted_iota(jnp.int32, (rows, LANES), 1) < HEAD_DIM
    return jnp.where(first, a[:rows], a[rows:])


def _causal(i, jj, stacked, kv_tile):
    r = lax.broadcasted_iota(jnp.int32, (stacked * ROW_TILE, kv_tile), 0)
    if stacked == 2:
        r = jnp.where(r >= ROW_TILE, r - ROW_TILE, r)
    kidx = jj * kv_tile + lax.broadcasted_iota(jnp.int32, (stacked * ROW_TILE, kv_tile), 1)
    return kidx <= i * ROW_TILE + r


def _stack_rows(b):
    n = b.shape[1]
    return jnp.concatenate([jnp.broadcast_to(b[0:1], (ROW_TILE, n)), jnp.broadcast_to(b[1:2], (ROW_TILE, n))], axis=0)


def _attn_fwd(qkv, bias, nb):
    lk = qkv.shape[0]
    lp = nb * ROW_TILE
    nkb = lk // KV_TILE
    n_pairs = N_HEADS // 2

    def body(q_ref, k_ref, v_ref, b_ref, o_ref, lse_ref):
        i = pl.program_id(1)
        qs = _stack_heads(q_ref[...])
        last = (i * ROW_TILE) // KV_TILE

        def block(jj, carry, masked, n_keys=KV_TILE):
            m, l, acc = carry
            rows = pl.ds(pl.multiple_of(jj * KV_TILE, KV_TILE), n_keys)
            s = _dot_nt(qs, k_ref[rows, :]) - _stack_rows(b_ref[0, 0, :, rows])
            if masked:
                s = jnp.where(_causal(i, jj, 2, KV_TILE)[:, :n_keys], s, NEG)
            m_new = jnp.maximum(m, jnp.max(s, axis=1, keepdims=True))
            alpha = jnp.exp(m - m_new)
            p = jnp.exp(s - m_new)
            l = alpha * l + jnp.sum(p, axis=1, keepdims=True)
            acc = alpha * acc + jnp.dot(p.astype(BF16), v_ref[rows, :], preferred_element_type=F32)
            return m_new, l, acc

        init = (jnp.full((2 * ROW_TILE, 1), NEG, F32), jnp.zeros((2 * ROW_TILE, 1), F32),
                jnp.zeros((2 * ROW_TILE, LANES), F32))
        def finish(carry):
            m, l, acc = carry
            o_ref[...] = _unstack_heads(acc / l)
            lse_ref[...] = _unstack_heads(jnp.broadcast_to(m + jnp.log(l), (2 * ROW_TILE, LANES)))

        carry = lax.fori_loop(0, last, lambda jj, c: block(jj, c, False), init)
        ends = [lambda c, n=(r + 1) * ROW_TILE: finish(block(last, c, True, n)) for r in range(KV_TILE // ROW_TILE)]
        lax.switch(i - last * (KV_TILE // ROW_TILE), ends, carry)

    return pl.pallas_call(
        body, name="attn_fwd", grid=(n_pairs, nb),
        in_specs=[pl.BlockSpec((ROW_TILE, LANES), lambda hp, i: (i, hp)),
                  pl.BlockSpec((lk, LANES), lambda hp, i: (0, n_pairs + hp)),
                  pl.BlockSpec((lk, LANES), lambda hp, i: (0, 2 * n_pairs + hp)),
                  pl.BlockSpec((1, 1, 2, lk), lambda hp, i: (i, hp, 0, 0))],
        out_specs=[pl.BlockSpec((ROW_TILE, LANES), lambda hp, i: (i, hp)),
                   pl.BlockSpec((ROW_TILE, LANES), lambda hp, i: (i, hp))],
        out_shape=[jax.ShapeDtypeStruct((lp, ATTN_WIDTH), F32), jax.ShapeDtypeStruct((lp, ATTN_WIDTH), F32)],
        compiler_params=_params(("parallel", "parallel")),
    )(qkv, qkv, qkv, bias)


def _merge_fwd(y_pool, o, e, wup_p, wup_a):
    lp = o.shape[0]
    nb = lp // ROW_TILE

    def body(yp_ref, o_ref, e_ref, wp_ref, wa_ref, mg_ref, ya_ref):
        za = e_ref[:, E_ZA:E_GP].astype(F32)
        ya = (o_ref[...] * (za * _sigmoid(za))).astype(BF16)
        ya_ref[...] = ya
        a_pool = jnp.dot(yp_ref[...], wp_ref[...], preferred_element_type=F32)
        a_attn = jnp.dot(ya, wa_ref[...], preferred_element_type=F32)
        mg_ref[...] = (_sigmoid(e_ref[:, E_GP:E_GA].astype(F32)) * a_pool
                       + _sigmoid(e_ref[:, E_GA:E_COLS].astype(F32)) * a_attn).astype(BF16)

    return pl.pallas_call(
        body, name="merge_fwd", grid=(nb,),
        in_specs=[pl.BlockSpec((ROW_TILE, POOL_WIDTH), lambda i: (i, 0)),
                  pl.BlockSpec((ROW_TILE, ATTN_WIDTH), lambda i: (i, 0)),
                  pl.BlockSpec((ROW_TILE, E_COLS), lambda i: (i, 0)),
                  pl.BlockSpec((POOL_WIDTH, D_MODEL), lambda i: (0, 0)),
                  pl.BlockSpec((ATTN_WIDTH, D_MODEL), lambda i: (0, 0))],
        out_specs=[pl.BlockSpec((ROW_TILE, D_MODEL), lambda i: (i, 0)),
                   pl.BlockSpec((ROW_TILE, ATTN_WIDTH), lambda i: (i, 0))],
        out_shape=[jax.ShapeDtypeStruct((lp, D_MODEL), BF16), jax.ShapeDtypeStruct((lp, ATTN_WIDTH), BF16)],
        compiler_params=_params(("parallel",)),
    )(y_pool, o, e, wup_p, wup_a)


def _head_fwd_bwd(merged, w_out, x2, metapad, gf, target):
    lp = merged.shape[0]
    nb = lp // ROW_TILE

    def body(mg_ref, w_ref, x_ref, mp_ref, g_ref, t_ref, dh_ref, loss_ref, dg_ref):
        i = pl.program_id(0)

        @pl.when(i == 0)
        def _():
            loss_ref[...] = jnp.zeros_like(loss_ref)
            dg_ref[...] = jnp.zeros_like(dg_ref)

        h0 = jnp.where(i == 0, mp_ref[...], x_ref[...])
        h1 = h0 + jnp.dot(mg_ref[...], w_ref[...], preferred_element_type=F32)
        r = lax.rsqrt(jnp.mean(h1 * h1, axis=-1, keepdims=True) + RMS_EPS)
        xhat = h1 * r
        g = g_ref[...]
        err = jnp.where(i == 0, 0.0, xhat * g - t_ref[...])
        loss_ref[...] += 0.5 * jnp.sum(jnp.mean(err * err, axis=-1, keepdims=True))
        dy = err / D_MODEL
        dg_ref[...] += jnp.sum(dy * xhat, axis=0, keepdims=True)
        dxhat = dy * g
        dh_ref[...] = r * (dxhat - xhat * jnp.mean(dxhat * xhat, axis=-1, keepdims=True))

    return pl.pallas_call(
        body, name="head_fwd_bwd", grid=(nb,),
        in_specs=[pl.BlockSpec((ROW_TILE, D_MODEL), lambda i: (i, 0)),
                  pl.BlockSpec((D_MODEL, D_MODEL), lambda i: (0, 0)),
                  _tokens_spec(), pl.BlockSpec((ROW_TILE, D_MODEL), lambda i: (0, 0)),
                  pl.BlockSpec((1, D_MODEL), lambda i: (0, 0)), _tokens_spec()],
        out_specs=[pl.BlockSpec((ROW_TILE, D_MODEL), lambda i: (i, 0)),
                   pl.BlockSpec((1, LANES), lambda i: (0, 0)), pl.BlockSpec((1, D_MODEL), lambda i: (0, 0))],
        out_shape=[jax.ShapeDtypeStruct((lp, D_MODEL), F32), jax.ShapeDtypeStruct((1, LANES), F32),
                   jax.ShapeDtypeStruct((1, D_MODEL), F32)],
        compiler_params=_params(("arbitrary",)),
    )(merged, w_out, x2, metapad, gf, target)


def _per_head_rowsum(t):
    head = lax.broadcasted_iota(jnp.int32, t.shape, 1) // HEAD_DIM
    out = jnp.zeros_like(t)
    for h in range(N_HEADS):
        sel = head == h
        out = jnp.where(sel, jnp.sum(jnp.where(sel, t, 0.0), axis=1, keepdims=True), out)
    return out


def _merge_bwd(dh1, w_out, y_pool, y_attn, wup_p, wup_a, e, o):
    lp = o.shape[0]
    nb = lp // ROW_TILE

    def body(dh_ref, wo_ref, yp_ref, ya_ref, wp_ref, wa_ref, e_ref, o_ref,
             dap_ref, daa_ref, dg_ref, do_ref, delta_ref, dyp_ref):
        dmerged = _dot_nt(dh_ref[...].astype(BF16), wo_ref[...])
        a_pool = jnp.dot(yp_ref[...], wp_ref[...], preferred_element_type=F32)
        a_attn = jnp.dot(ya_ref[...], wa_ref[...], preferred_element_type=F32)
        sp = _sigmoid(e_ref[:, E_GP:E_GA].astype(F32))
        sa = _sigmoid(e_ref[:, E_GA:E_COLS].astype(F32))
        dap = (dmerged * sp).astype(BF16)
        daa = (dmerged * sa).astype(BF16)
        dap_ref[...] = dap
        daa_ref[...] = daa
        dg_ref[:, ATTN_WIDTH:ATTN_WIDTH + D_MODEL] = (dmerged * a_pool * (sp * (1.0 - sp))).astype(BF16)
        dg_ref[:, ATTN_WIDTH + D_MODEL:] = (dmerged * a_attn * (sa * (1.0 - sa))).astype(BF16)
        dyp_ref[...] = _dot_nt(dap, wp_ref[...])
        dya = _dot_nt(daa, wa_ref[...])
        za = e_ref[:, E_ZA:E_GP].astype(F32)
        sz = _sigmoid(za)
        o = o_ref[...]
        do = dya * (za * sz)
        do_ref[...] = do.astype(BF16)
        dg_ref[:, :ATTN_WIDTH] = (dya * o * (sz * (1.0 + za * (1.0 - sz)))).astype(BF16)
        delta_ref[...] = _per_head_rowsum(do * o)

    row = lambda w: pl.BlockSpec((ROW_TILE, w), lambda i: (i, 0))
    full = lambda a: pl.BlockSpec(a.shape, lambda i: (0, 0))
    return pl.pallas_call(
        body, name="merge_bwd", grid=(nb,),
        in_specs=[row(D_MODEL), full(w_out), row(POOL_WIDTH), row(ATTN_WIDTH), full(wup_p), full(wup_a),
                  row(E_COLS), row(ATTN_WIDTH)],
        out_specs=[row(D_MODEL), row(D_MODEL), pl.BlockSpec((ROW_TILE, MAIN_COLS // 2), lambda i: (i, 1)),
                   row(ATTN_WIDTH), row(ATTN_WIDTH), row(POOL_WIDTH)],
        out_shape=[jax.ShapeDtypeStruct((lp, D_MODEL), BF16), jax.ShapeDtypeStruct((lp, D_MODEL), BF16),
                   jax.ShapeDtypeStruct((lp, MAIN_COLS), BF16),
                   jax.ShapeDtypeStruct((lp, ATTN_WIDTH), BF16), jax.ShapeDtypeStruct((lp, ATTN_WIDTH), F32),
                   jax.ShapeDtypeStruct((lp, POOL_WIDTH), F32)],
        compiler_params=_params(("parallel",)),
    )(dh1, w_out, y_pool, y_attn, wup_p, wup_a, e, o)


def _fill_dproj(dproj, dq, dk, dv):
    lp = dproj.shape[0]
    rows = _row_tile(lp, 1100)
    any_space = pl.BlockSpec(memory_space=pl.ANY)

    def kv_body(base_ref, dk_ref, dv_ref, o_ref, sems):
        cps = [pltpu.make_async_copy(src.at[pl.ds(0, lp), :], o_ref.at[:, pl.ds((3 + t) * ATTN_WIDTH, ATTN_WIDTH)], sems.at[t])
               for t, src in enumerate((dk_ref, dv_ref))]
        for cp in cps:
            cp.start()
        for cp in cps:
            cp.wait()

    with_kv = pl.pallas_call(
        kv_body, name="fill_dproj_kv",
        in_specs=[any_space] * 3, out_specs=any_space,
        out_shape=jax.ShapeDtypeStruct(dproj.shape, BF16),
        scratch_shapes=[pltpu.SemaphoreType.DMA((2,))],
        input_output_aliases={0: 0},
    )(dproj, dk, dv)

    def q_body(base_ref, dq_ref, o_ref):
        o_ref[...] = (dq_ref[...] * HEAD_DIM ** -0.5).astype(BF16)

    return pl.pallas_call(
        q_body, name="fill_dproj_q", grid=(lp // rows,),
        in_specs=[any_space, pl.BlockSpec((rows, ATTN_WIDTH), lambda i: (i, 0))],
        out_specs=pl.BlockSpec((rows, ATTN_WIDTH), lambda i: (i, 2)),
        out_shape=jax.ShapeDtypeStruct(dproj.shape, BF16),
        input_output_aliases={0: 0},
        compiler_params=_params(("parallel",)),
    )(with_kv, dq)


def _pool_bwd_local(e, dy_pool, pw, scale, dproj):
    lp = e.shape[0]
    nb = lp // ROW_TILE
    ng = len(POOL_WINDOWS)

    def body(uc_ref, up_ref, z_ref, dy_ref, pw_ref, sc_ref, base_ref, dpc_ref, dz_ref, dsc_ref, dpw_ref):
        i = pl.program_id(0)

        @pl.when(i == 0)
        def _():
            dsc_ref[...] = jnp.zeros_like(dsc_ref)
            dpw_ref[...] = jnp.zeros_like(dpw_ref)

        u_cur = uc_ref[...].astype(F32)
        u_prev = jnp.where(i == 0, 0.0, up_ref[...].astype(F32))
        ps, invs = _pool_p(u_cur, u_prev, i)
        z = z_ref[...].astype(F32)
        sz = _sigmoid(z)
        dy = dy_ref[...]
        dypre = dy * (z * sz)
        dsilu = sz * (1.0 + z * (1.0 - sz))
        for g in range(ng):
            sl = slice(g * POOL_GROUP, (g + 1) * POOL_GROUP)
            pb = ps[g].astype(BF16)
            w = pw_ref[g]
            yraw = jnp.dot(pb, w, preferred_element_type=F32)
            sc = sc_ref[:, sl]
            dz_ref[:, sl] = (dy[:, sl] * (yraw * sc) * dsilu[:, sl]).astype(BF16)
            dsc_ref[:, sl] += jnp.sum(dypre[:, sl] * yraw, axis=0, keepdims=True)
            dyraw = (dypre[:, sl] * sc).astype(BF16)
            dpw_ref[g] += _dot_tn(pb, dyraw)
            dpc_ref[:, sl] = _dot_nt(dyraw, w) * invs[g]

    blk = (ROW_TILE, POOL_WIDTH)
    return pl.pallas_call(
        body, name="pool_bwd_local", grid=(nb,),
        in_specs=[pl.BlockSpec(blk, lambda i: (i, 0)), pl.BlockSpec(blk, lambda i: (jnp.maximum(i - 1, 0), 0)),
                  pl.BlockSpec(blk, lambda i: (i, 1)), pl.BlockSpec(blk, lambda i: (i, 0)),
                  pl.BlockSpec((ng, POOL_GROUP, POOL_GROUP), lambda i: (0, 0, 0)),
                  pl.BlockSpec((1, POOL_WIDTH), lambda i: (0, 0)), pl.BlockSpec(memory_space=pl.ANY)],
        out_specs=[pl.BlockSpec(blk, lambda i: (i, 0)), pl.BlockSpec(blk, lambda i: (i, 1)),
                   pl.BlockSpec((1, POOL_WIDTH), lambda i: (0, 0)),
                   pl.BlockSpec((ng, POOL_GROUP, POOL_GROUP), lambda i: (0, 0, 0))],
        out_shape=[jax.ShapeDtypeStruct((lp, POOL_WIDTH), F32), jax.ShapeDtypeStruct(dproj.shape, BF16),
                   jax.ShapeDtypeStruct((1, POOL_WIDTH), F32),
                   jax.ShapeDtypeStruct((ng, POOL_GROUP, POOL_GROUP), F32)],
        input_output_aliases={6: 1},
        compiler_params=_params(("arbitrary",)),
    )(e, e, e, dy_pool, pw, scale, dproj)


def _pool_bwd_window(dpc, dproj):
    lp = dpc.shape[0]
    nb = lp // ROW_TILE

    def body(cur_ref, nxt_ref, base_ref, du_ref):
        i = pl.program_id(0)
        cur = cur_ref[...]
        nxt = jnp.where(i == nb - 1, 0.0, nxt_ref[...])
        pos = _pool_counts(i)
        for g, w in enumerate(POOL_WINDOWS):
            sl = slice(g * POOL_GROUP, (g + 1) * POOL_GROUP)
            xc = jnp.concatenate([cur[:, sl], nxt[:, sl]], axis=0)
            win = _leading_sums(xc, g + 1)[:ROW_TILE, :]
            dp = cur[:, sl] * jnp.minimum(pos + 1, w).astype(F32)
            du_ref[:, sl] = (win - dp).astype(BF16)

    blk = (ROW_TILE, POOL_WIDTH)
    return pl.pallas_call(
        body, name="pool_bwd_window", grid=(nb,),
        in_specs=[pl.BlockSpec(blk, lambda i: (i, 0)), pl.BlockSpec(blk, lambda i: (jnp.minimum(i + 1, nb - 1), 0)),
                  pl.BlockSpec(memory_space=pl.ANY)],
        out_specs=pl.BlockSpec(blk, lambda i: (i, 0)),
        out_shape=jax.ShapeDtypeStruct(dproj.shape, BF16),
        input_output_aliases={2: 0},
        compiler_params=_params(("parallel",)),
    )(dpc, dpc, dproj)


def _attn_bwd(qkv, do, lse, delta, bias, nb):
    lk = qkv.shape[0]
    lp = nb * ROW_TILE
    nkb = lk // KV_TILE_BWD
    n_pairs = N_HEADS // 2
    per_kv = KV_TILE_BWD // ROW_TILE

    def body(q_ref, k_ref, v_ref, do_ref, lse_ref, dl_ref, b_ref, dq_ref, dk_ref, dv_ref, dc_ref, dcq_ref,
             dk_acc, dv_acc, dc_acc):
        jj = pl.program_id(1)

        @pl.when(jj == 0)
        def _():
            dq_ref[...] = jnp.zeros_like(dq_ref)
            dcq_ref[...] = jnp.zeros_like(dcq_ref)

        dk_acc[...] = jnp.zeros_like(dk_acc)
        dv_acc[...] = jnp.zeros_like(dv_acc)
        dc_acc[...] = jnp.zeros_like(dc_acc)

        def block(i, n_keys, masked):
            kb, vb = k_ref[:n_keys, :], v_ref[:n_keys, :]
            rows = pl.ds(pl.multiple_of(i * ROW_TILE, ROW_TILE), ROW_TILE)
            qs = _stack_heads(q_ref[rows, :])
            dos = _stack_heads(do_ref[rows, :])
            lse_i, dl_i = lse_ref[rows, :], dl_ref[rows, :]
            s = _dot_nt(qs, kb)
            dp = _dot_nt(dos, vb)
            if masked:
                valid = _causal(i, jj, 1, KV_TILE_BWD)[:, :n_keys]
            ps, dss, dcs, rowsums = [], [], [], []
            for hd in range(2):
                half = slice(hd * ROW_TILE, (hd + 1) * ROW_TILE)
                col = slice(hd * HEAD_DIM, hd * HEAD_DIM + 1)
                sh = s[half] - b_ref[i, 0, hd:hd + 1, :n_keys]
                if masked:
                    sh = jnp.where(valid, sh, NEG)
                p = jnp.exp(sh - lse_i[:, col])
                ds = p * (dp[half] - dl_i[:, col])
                ps.append(p.astype(BF16))
                dss.append(ds.astype(BF16))
                dcs.append(jnp.sum(ds, axis=0, keepdims=True))
                rowsums.append(jnp.sum(ds, axis=1, keepdims=True))
            dsb = jnp.concatenate(dss, axis=0)
            dv_acc[:n_keys, :] += _dot_tn(jnp.concatenate(ps, axis=0), dos)
            dk_acc[:n_keys, :] += _dot_tn(dsb, qs)
            dc_acc[:, :n_keys] -= jnp.concatenate(dcs, axis=0)
            dq_ref[rows, :] += _unstack_heads(jnp.dot(dsb, kb, preferred_element_type=F32))
            dcq_ref[rows, :] += _unstack_heads(jnp.broadcast_to(jnp.concatenate(rowsums, axis=0), (2 * ROW_TILE, LANES)))

        first_q = per_kv * jj
        for r in range(per_kv):
            @pl.when(first_q + r < nb)
            def _():
                block(first_q + r, (r + 1) * ROW_TILE, True)

        def rest(i, carry):
            block(i, KV_TILE_BWD, False)
            return carry

        lax.fori_loop(jnp.minimum(first_q + per_kv, nb), nb, rest, 0)
        dk_ref[...] = dk_acc[...].astype(BF16)
        dv_ref[...] = dv_acc[...].astype(BF16)
        dc_ref[0, 0] = dc_acc[...]

    whole = lambda rows: pl.BlockSpec((rows, LANES), lambda hp, jj: (0, hp))
    kv_blk = lambda off: pl.BlockSpec((KV_TILE_BWD, LANES), lambda hp, jj: (jj, off + hp))
    return pl.pallas_call(
        body, name="attn_bwd", grid=(n_pairs, nkb),
        in_specs=[whole(lk), kv_blk(n_pairs), kv_blk(2 * n_pairs), whole(lp), whole(lp), whole(lp),
                  pl.BlockSpec((nb, 1, 2, KV_TILE_BWD), lambda hp, jj: (0, hp, 0, jj))],
        out_specs=[whole(lp), kv_blk(0), kv_blk(0),
                   pl.BlockSpec((1, 1, 2, KV_TILE_BWD), lambda hp, jj: (hp, jj, 0, 0)), whole(lp)],
        out_shape=[jax.ShapeDtypeStruct((lp, ATTN_WIDTH), F32), jax.ShapeDtypeStruct((lk, ATTN_WIDTH), BF16),
                   jax.ShapeDtypeStruct((lk, ATTN_WIDTH), BF16),
                   jax.ShapeDtypeStruct((n_pairs, nkb, 2, KV_TILE_BWD), F32),
                   jax.ShapeDtypeStruct((lp, ATTN_WIDTH), F32)],
        scratch_shapes=[pltpu.VMEM((KV_TILE_BWD, LANES), F32), pltpu.VMEM((KV_TILE_BWD, LANES), F32),
                        pltpu.VMEM((2, KV_TILE_BWD), F32)],
        compiler_params=_params(("parallel", "arbitrary")),
    )(qkv, qkv, qkv, do, lse, delta, bias)


def _gates_bwd(dc, dcq, f, bfg):
    nb = f.shape[0] // ROW_TILE

    def body(dc_ref, dcq_ref, f_ref, b_ref, df_ref, db_ref, carry):
        step = pl.program_id(0)
        i = nb - 1 - step

        @pl.when(step == 0)
        def _():
            carry[...] = jnp.zeros_like(carry)
            db_ref[...] = jnp.zeros_like(db_ref)

        dcb = dc_ref[...]
        lane = lax.broadcasted_iota(jnp.int32, (ROW_TILE, F_COLS), 1)
        for h in range(N_HEADS):
            dcb = dcb + jnp.where(lane == h, dcq_ref[:, HEAD_DIM * h:HEAD_DIM * h + 1], 0.0)
        r_i = lax.broadcasted_iota(jnp.int32, (ROW_TILE, ROW_TILE), 0)
        c_i = lax.broadcasted_iota(jnp.int32, (ROW_TILE, ROW_TILE), 1)
        upper = (c_i >= r_i).astype(F32)
        dlf = jnp.dot(upper, dcb, precision=HIGHEST, preferred_element_type=F32) + carry[...]
        carry[...] = carry[...] + jnp.sum(dcb, axis=0, keepdims=True)
        logit = f_ref[...] + b_ref[...]
        dlogit = jnp.where(_valid_gate_mask(i), dlf * _sigmoid(-logit), 0.0)
        df_ref[...] = dlogit.astype(BF16)
        db_ref[...] += jnp.sum(dlogit, axis=0, keepdims=True)

    blk = pl.BlockSpec((ROW_TILE, F_COLS), lambda s: (nb - 1 - s, 0))
    wide = pl.BlockSpec((ROW_TILE, ATTN_WIDTH), lambda s: (nb - 1 - s, 0))
    one = pl.BlockSpec((1, F_COLS), lambda s: (0, 0))
    return pl.pallas_call(
        body, name="gates_bwd", grid=(nb,),
        in_specs=[blk, wide, blk, one], out_specs=[blk, one],
        out_shape=[jax.ShapeDtypeStruct(f.shape, BF16), jax.ShapeDtypeStruct((1, F_COLS), F32)],
        scratch_shapes=[pltpu.VMEM((1, F_COLS), F32)],
        compiler_params=_params(("arbitrary",)),
    )(dc, dcq, f, bfg)


def _input_bwd(dproj, df, wt_e, wt_qkv, wt_f, x2, metapad, dh1, g1, hs, sc=None):
    nb = x2.shape[0] // ROW_TILE + 1
    n = len(hs)
    sc_in, sc_shapes, sc_sems = _scatter_operands(hs, sc)
    nq = len(sc_in)

    def body(dp_ref, df_ref, we_ref, w_ref, wf_ref, x_ref, mp_ref, dh_ref, g_ref, *rest):
        h_refs, s_ref = rest[:n], (rest[n] if sc is not None else None)
        gx_ref, g0_ref, dg_ref = rest[nq:nq + 3]
        q_refs, sq_ref = rest[nq + 3:nq + 3 + n], (rest[nq + 3 + n] if sc is not None else None)
        sems = rest[2 * nq + 3:]
        i = pl.program_id(0)

        @pl.when(i == 0)
        def _():
            dg_ref[...] = jnp.zeros_like(dg_ref)
            own, sends, _ = _chip_scatter_plan(h_refs, s_ref, q_refs, sq_ref, *sems)
            for cp in own + sends:
                cp.start()

        dhn = (jnp.dot(dp_ref[:, :E_ZA], we_ref[:E_ZA, :], preferred_element_type=F32)
               + jnp.dot(dp_ref[:, E_ZA:MAIN_COLS // 2], w_ref[...], preferred_element_type=F32)
               + jnp.dot(dp_ref[:, MAIN_COLS // 2:], we_ref[E_ZA:, :], preferred_element_type=F32)
               + jnp.dot(df_ref[...], wf_ref[...], preferred_element_type=F32))
        h0 = jnp.where(i == 0, mp_ref[...], x_ref[...])
        r = lax.rsqrt(jnp.mean(h0 * h0, axis=-1, keepdims=True) + RMS_EPS)
        xhat = h0 * r
        dg_ref[...] += jnp.sum(dhn * xhat, axis=0, keepdims=True)
        dxhat = dhn * g_ref[...]
        dh0 = dh_ref[...] + r * (dxhat - xhat * jnp.mean(dxhat * xhat, axis=-1, keepdims=True))
        gx_ref[...] = dh0

        @pl.when(i == 0)
        def _():
            g0_ref[...] = dh0

        @pl.when(i == nb - 1)
        def _():
            own, sends, recvs = _chip_scatter_plan(h_refs, s_ref, q_refs, sq_ref, *sems)
            for cp in recvs:
                cp.wait_recv()
            for cp in sends:
                cp.wait_send()
            for cp in own:
                cp.wait()

    const = lambda shape: pl.BlockSpec(shape, lambda i: (0, 0))
    res = pl.pallas_call(
        body, name="input_bwd", grid=(nb,),
        in_specs=[pl.BlockSpec((ROW_TILE, MAIN_COLS), lambda i: (i, 0)), pl.BlockSpec((ROW_TILE, F_COLS), lambda i: (i, 0)),
                  const((E_COLS, D_MODEL)), const((QKV_COLS, D_MODEL)), const((F_COLS, D_MODEL)),
                  _tokens_spec(), const((ROW_TILE, D_MODEL)),
                  pl.BlockSpec((ROW_TILE, D_MODEL), lambda i: (i, 0)), const((1, D_MODEL))] + [HBM] * nq,
        out_specs=[_tokens_spec(), const((ROW_TILE, D_MODEL)), const((1, D_MODEL))] + [HBM] * nq,
        out_shape=[jax.ShapeDtypeStruct(x2.shape, F32), jax.ShapeDtypeStruct((ROW_TILE, D_MODEL), F32),
                   jax.ShapeDtypeStruct((1, D_MODEL), F32)] + sc_shapes,
        scratch_shapes=sc_sems,
        compiler_params=_params(("arbitrary",), vmem=56 * 1024 * 1024),
    )(dproj, df, wt_e, wt_qkv, wt_f, x2, metapad, dh1, g1, *_in_hbm(*sc_in))
    return res[:3], res[3:3 + n], (res[3 + n] if sc is not None else None)


def _adamw(w, g, m, v, name):
    rows, cols = w.shape
    if rows % 8 == 0:
        tr, tc = _row_tile8(rows), cols
    else:
        tr, tc = rows, (2 * LANES if cols % (2 * LANES) == 0 and rows > 8 else cols)

    def body(w_ref, g_ref, m_ref, v_ref, d_ref, mo_ref, vo_ref):
        g_ = g_ref[...]
        m_new = ADAM_B1 * m_ref[...] + (1.0 - ADAM_B1) * g_
        v_new = ADAM_B2 * v_ref[...] + (1.0 - ADAM_B2) * (g_ * g_)
        m_hat = m_new / (1.0 - ADAM_B1 ** ADAM_STEP)
        v_hat = v_new / (1.0 - ADAM_B2 ** ADAM_STEP)
        d_ref[...] = -ADAM_LR * (m_hat / (jnp.sqrt(v_hat) + ADAM_EPS) + ADAM_WD * w_ref[...])
        mo_ref[...] = m_new
        vo_ref[...] = v_new

    blk = pl.BlockSpec((tr, tc), lambda i, j: (i, j))
    return pl.pallas_call(
        body, name=name, grid=(rows // tr, cols // tc),
        in_specs=[blk] * 4, out_specs=[blk] * 3,
        out_shape=[jax.ShapeDtypeStruct(w.shape, F32)] * 3,
        compiler_params=_params(("parallel", "parallel")),
    )(w, g, m, v)


def _adamw_native(w3, g3, m3, v3, name):
    rows = w3.shape[0]
    tr = rows // 2
    blk = pl.BlockSpec((tr,) + w3.shape[1:], lambda i: (i, 0, 0))
    shape = jax.ShapeDtypeStruct(w3.shape, F32)

    def moments(g_ref, m_ref, v_ref, mo_ref, vo_ref):
        g_ = g_ref[...]
        mo_ref[...] = ADAM_B1 * m_ref[...] + (1.0 - ADAM_B1) * g_
        vo_ref[...] = ADAM_B2 * v_ref[...] + (1.0 - ADAM_B2) * (g_ * g_)

    new_m, new_v = pl.pallas_call(
        moments, name=name + "_moments", grid=(2,), in_specs=[blk] * 3, out_specs=[blk] * 2, out_shape=[shape] * 2,
        compiler_params=_params(("parallel",)),
    )(g3, m3, v3)

    def delta(w_ref, m_ref, v_ref, d_ref):
        m_hat = m_ref[...] / (1.0 - ADAM_B1 ** ADAM_STEP)
        v_hat = v_ref[...] / (1.0 - ADAM_B2 ** ADAM_STEP)
        d_ref[...] = -ADAM_LR * (m_hat / (jnp.sqrt(v_hat) + ADAM_EPS) + ADAM_WD * w_ref[...])

    d = pl.pallas_call(
        delta, name=name + "_delta", grid=(2,), in_specs=[blk] * 3, out_specs=blk, out_shape=shape,
        compiler_params=_params(("parallel",)),
    )(w3, new_m, new_v)
    return d, new_m, new_v


def _row_tile8(rows):
    best = rows
    for t in range(8, 257, 8):
        if rows % t == 0:
            best = t
    return best


def kernel(x, meta_tokens, norm_g, w_in, b_forget, pool_w, pool_scale, w_up_pool, w_up_attn, w_out, final_norm_g, loss_target, m_meta_tokens, m_norm_g, m_w_in, m_b_forget, m_pool_w, m_pool_scale, m_w_up_pool, m_w_up_attn, m_w_out, m_final_norm_g, v_meta_tokens, v_norm_g, v_w_in, v_b_forget, v_pool_w, v_pool_scale, v_w_up_pool, v_w_up_attn, v_w_out, v_final_norm_g):
    seq = x.shape[1]
    assert seq % ROW_TILE == 0 and x.shape[0] == 1
    lp = seq + ROW_TILE
    nb = lp // ROW_TILE
    lk = -(-lp // KV_TILE_BWD) * KV_TILE_BWD
    core = jnp.reshape(lax.axis_index("c"), (1,)).astype(jnp.int32)
    x2 = x[0]
    target = loss_target[0]
    sh_d = D_MODEL // N_CHIPS

    to_rows = lambda a: jnp.transpose(a, (2, 0, 1))
    from_rows = lambda a: jnp.transpose(a, (1, 2, 0))
    gf = final_norm_g.reshape(1, D_MODEL)
    bfg = jnp.pad(b_forget, ((0, 0), (0, F_COLS - N_HEADS)))
    pw_b = pool_w[0].astype(BF16)
    hn, (wg_in, meta_g) = _rmsnorm_fwd(x2, norm_g, lk, [jnp.transpose(w_in[0]).astype(BF16), meta_tokens], [1, 0])
    wt_e, wt_qkv, wt_f = _weight_sections(wg_in)
    meta_full = jnp.transpose(meta_g, (1, 0, 2)).reshape(N_META, D_MODEL)
    metapad = jnp.pad(meta_full, ((PAD_ROWS, 0), (0, 0)))

    tm = _row_tile(lp, 2200)
    e, (wg_up_p, wg_up_a, wg_out) = _mm_nt(
        hn, wt_e, BF16, "in_proj_gates", lp, E_COLS, tm, 512,
        gather=([w_up_pool[0].astype(BF16), w_up_attn[0].astype(BF16), w_out[0].astype(BF16)], [0, 0, 0]))
    wup_p = jnp.transpose(wg_up_p, (1, 0, 2)).reshape(POOL_WIDTH, D_MODEL)
    wup_a = jnp.transpose(wg_up_a, (1, 0, 2)).reshape(ATTN_WIDTH, D_MODEL)
    wout = wg_out.reshape(D_MODEL, D_MODEL)
    qkv = _mm_nt(hn, wt_qkv, BF16, "in_proj_qkv", lk, QKV_COLS, _row_tile(lk, 2600), 512, scale_first=HEAD_DIM ** -0.5)
    f = _mm_nt(hn, wt_f, F32, "in_proj_forget", lp, F_COLS, tm, F_COLS)
    c = _gates_fwd(f, bfg)
    c_t = jnp.transpose(c[:, :N_HEADS])
    c_first = jnp.transpose(c_t[:, ::ROW_TILE]).reshape(nb, N_HEADS // 2, 2, 1)
    c_keys = jnp.pad(c_t, ((0, 0), (0, lk - lp))).reshape(1, N_HEADS // 2, 2, lk)
    bias = jnp.where(jnp.arange(lk) < PAD_ROWS, -NEG, c_keys - c_first)
    y_pool = _pool_fwd(e, pw_b, pool_scale)
    o, lse = _attn_fwd(qkv, bias, nb)
    merged, y_attn = _merge_fwd(y_pool, o, e, wup_p, wup_a)
    dh1, loss_part, dgf = _head_fwd_bwd(merged, wout, x2, metapad, gf, target)

    dap, daa, dproj_half, do, delta, dy_pool = _merge_bwd(dh1, wout, y_pool, y_attn, wup_p, wup_a, e, o)
    dpc, dproj_zp, dscale, dpw = _pool_bwd_local(e, dy_pool, pw_b, pool_scale, dproj_half)
    dproj_u = _pool_bwd_window(dpc, dproj_zp)
    dq, dk, dv, dc4, dcq = _attn_bwd(qkv, do, lse, delta, bias, nb)
    dc = jnp.transpose(dc4, (1, 3, 0, 2)).reshape(lk, N_HEADS)[:lp]
    df, db = _gates_bwd(jnp.pad(dc, ((0, 0), (0, F_COLS - N_HEADS))), dcq, f, bfg)
    dproj = _fill_dproj(dproj_u, dq, dk, dv)
    tk = _row_tile(lp, 1100)
    dw_out = _mm_tn(merged, dh1, "grad_w_out", lp, 512, D_MODEL, tk)
    dw_up_p = _mm_tn(y_pool, dap, "grad_w_up_pool", lp, 512, D_MODEL, lp, chunks=N_CHIPS)
    dw_up_a = _mm_tn(y_attn, daa, "grad_w_up_attn", lp, 512, D_MODEL, lp, chunks=N_CHIPS)
    dwt_f = _mm_tn(df, hn, "grad_w_in_forget", lp, F_COLS, D_MODEL, tk)

    def pad8(a):
        return jnp.pad(a, ((0, (-a.shape[0]) % 8), (0, 0)))

    small_parts = [pad8(a) for a in (dgf.reshape(-1, LANES), dscale.reshape(-1, LANES), db, dpw.reshape(-1, LANES),
                                     loss_part)]
    small = jnp.concatenate(small_parts, axis=0)
    soffs = [0]
    for p in small_parts:
        soffs.append(soffs[-1] + p.shape[0])

    gs_rows = [dw_up_p, dw_up_a, dw_out.reshape(N_CHIPS, sh_d, D_MODEL)]
    half = MAIN_COLS // 2
    dwt_a, (*rb_rows, rb_f, sr) = _mm_tn(dproj, hn, "grad_w_in_a", lp, half // 4, D_MODEL, lp, m=half, m_off=0,
                                         swap=(gs_rows + [dwt_f, small], [1, 1, 1, 1, None]))
    *hs_rows, sc = _add_sibling_half(gs_rows, rb_rows, [1, 1, 1], small, sr, core)
    dwt_b, (rb_a,), (qs_rows, sq) = _mm_tn(dproj, hn, "grad_w_in_b", lp, half // 4, D_MODEL, lp, m=half, m_off=4,
                                           swap=([dwt_a], [1]), scatter=(hs_rows, sc))
    h_in = _add_halves_w_in([dwt_a, dwt_b, dwt_f], [rb_a, rb_f], core)
    grad_axes = [2, 1, 1, 1]
    (grad_x, g_block0, dg1), (q_in,), _ = _input_bwd(dproj, df, wt_e, wt_qkv, wt_f, x2, metapad, dh1, norm_g, [h_in])
    late_parts = jnp.concatenate([pad8(dg1.reshape(-1, LANES)), g_block0[PAD_ROWS:].reshape(-1, LANES)], axis=0)
    g_w_in_rows, g_w_up_p, g_w_up_a, g_w_out, st, late = _reduce_allgather(
        [q_in] + list(qs_rows), grad_axes, sq, [True, False, False, False], late_parts)
    n_norm = D_MODEL // LANES
    g_norm = late[:n_norm].reshape(1, D_MODEL)
    chip = 2 * lax.axis_index("x") + lax.axis_index("y")
    g_meta = lax.dynamic_slice_in_dim(late[n_norm:].reshape(N_META, D_MODEL), chip * sh_d, sh_d, axis=1)

    spiece = lambda k, rows: st[soffs[k]:soffs[k] + rows]
    g_final = spiece(0, D_MODEL // LANES).reshape(1, D_MODEL)
    g_scale = spiece(1, POOL_WIDTH // LANES).reshape(1, POOL_WIDTH)
    g_bf = spiece(2, 1)
    g_pw = spiece(3, POOL_WIDTH)
    loss = st[soffs[4], 0]

    def pad_lanes(a):
        return jnp.pad(a, ((0, 0), (0, F_COLS - N_HEADS)))

    w_in_res = (g_w_in_rows,) + _adamw_native(to_rows(w_in), g_w_in_rows, to_rows(m_w_in), to_rows(v_w_in), "adamw_w_in")

    upd = [
        ("meta_tokens", meta_tokens, g_meta, m_meta_tokens, v_meta_tokens),
        ("norm_g", norm_g, g_norm, m_norm_g, v_norm_g),
        ("w_in", None, None, None, None),
        ("b_forget", pad_lanes(b_forget), g_bf, pad_lanes(m_b_forget), pad_lanes(v_b_forget)),
        ("pool_w", pool_w.reshape(-1, LANES), g_pw, m_pool_w.reshape(-1, LANES), v_pool_w.reshape(-1, LANES)),
        ("pool_scale", pool_scale, g_scale, m_pool_scale, v_pool_scale),
        ("w_up_pool", w_up_pool[0], g_w_up_p, m_w_up_pool[0], v_w_up_pool[0]),
        ("w_up_attn", w_up_attn[0], g_w_up_a, m_w_up_attn[0], v_w_up_attn[0]),
        ("w_out", w_out[0], g_w_out, m_w_out[0], v_w_out[0]),
        ("final_norm_g", gf, g_final, m_final_norm_g.reshape(1, D_MODEL), v_final_norm_g.reshape(1, D_MODEL)),
    ]
    shapes = [meta_tokens.shape, norm_g.shape, w_in.shape, b_forget.shape, pool_w.shape, pool_scale.shape,
              w_up_pool.shape, w_up_attn.shape, w_out.shape, final_norm_g.shape]
    grads, deltas, new_ms, new_vs = [], [], [], []
    for (name, w_, g_, m_in, v_in), shp in zip(upd, shapes):
        if name == "w_in":
            res = tuple(from_rows(a) for a in w_in_res)
        else:
            d_, mn_, vn_ = _adamw(w_, g_, m_in, v_in, "adamw_" + name)
            res = (g_, d_, mn_, vn_)
        if name == "b_forget":
            res = tuple(a[:, :N_HEADS] for a in res)
        for lst, a in zip((grads, deltas, new_ms, new_vs), res):
            lst.append(a.reshape(shp))

    return (loss, grad_x.reshape(x.shape), *grads, *deltas, *new_ms, *new_vs)
```

```python
import jax
import jax.numpy as jnp
from jax import lax
from jax.experimental import pallas as pl
from jax.experimental.pallas import tpu as pltpu

F32 = jnp.float32
BF16 = jnp.bfloat16
MESH = pl.DeviceIdType.MESH
HIGHEST = lax.Precision.HIGHEST
HBM = pl.BlockSpec(memory_space=pltpu.HBM)

D_MODEL = 1024
N_META = 16
POOL_WIDTH = 512
POOL_GROUP = 128
POOL_WINDOWS = (2, 4, 8, 16)
N_HEADS = 8
HEAD_DIM = 64
ATTN_WIDTH = 512
RMS_EPS = 1e-6
N_CHIPS = 4

ADAM_LR = 0.001
ADAM_B1 = 0.9
ADAM_B2 = 0.999
ADAM_EPS = 1e-08
ADAM_WD = 0.01
ADAM_STEP = 10

LANES = 128
ROW_TILE = 256
KV_TILE = 1024
KV_TILE_BWD = 1024
PAD_ROWS = ROW_TILE - N_META
NEG = -1e30

E_U, E_ZP, E_ZA, E_GP, E_GA, E_COLS = 0, 512, 1024, 1536, 2560, 3584
QKV_COLS = 3 * ATTN_WIDTH
MAIN_COLS = E_COLS + QKV_COLS
F_COLS = LANES
REF_U, REF_Q, REF_ZA, REF_F, REF_GP = 0, 1024, 2560, 3072, 3080
VMEM_LIMIT = 48 * 1024 * 1024


def _params(sem=None, vmem=VMEM_LIMIT):
    return pltpu.CompilerParams(dimension_semantics=sem, vmem_limit_bytes=vmem)


def _in_hbm(*arrays):
    return [pltpu.with_memory_space_constraint(a, pltpu.HBM) for a in arrays]


def _row_tile(n, target):
    best = 16
    for t in range(16, target + 1, 16):
        if n % t == 0:
            best = t
    return best


def _sigmoid(x):
    return 1.0 / (1.0 + jnp.exp(-x))


def _half(ref, axis, which):
    n = ref.shape[axis] // 2
    idx = [slice(None)] * len(ref.shape)
    idx[axis] = pl.ds(pl.multiple_of(which * n, n), n)
    return ref.at[tuple(idx)]


def _half_shape(shape, axis):
    s = list(shape)
    s[axis] //= 2
    return tuple(s)


def _gather_start(ins, outs, axes, send, recv, fsend, frecv, local):
    x, y, c = lax.axis_index("x"), lax.axis_index("y"), lax.axis_index("c")
    me = 2 * x + y
    for t in range(len(ins)):
        pltpu.make_async_copy(ins[t], outs[t].at[me], local.at[t]).start()
        for k, chip in enumerate([(1 - x, y), (x, 1 - y), (1 - x, 1 - y)]):
            pltpu.make_async_remote_copy(
                src_ref=_half(ins[t], axes[t], c), dst_ref=_half(outs[t].at[me], axes[t], c),
                send_sem=send.at[3 * t + k], recv_sem=recv.at[3 * t + k], device_id=(*chip, c), device_id_type=MESH).start()


def _gather_finish(ins, outs, axes, send, recv, fsend, frecv, local):
    x, y, c = lax.axis_index("x"), lax.axis_index("y"), lax.axis_index("c")
    me = 2 * x + y
    sibling = (x, y, 1 - c)
    chips = [(1 - x, y), (x, 1 - y), (1 - x, 1 - y)]
    n = len(ins)

    def over_ici(t, k, chip, dst_slot):
        return pltpu.make_async_remote_copy(
            src_ref=_half(ins[t], axes[t], c), dst_ref=_half(outs[t].at[dst_slot], axes[t], c),
            send_sem=send.at[3 * t + k], recv_sem=recv.at[3 * t + k], device_id=(*chip, c), device_id_type=MESH)

    def to_sibling(t, k, slot, which):
        return pltpu.make_async_remote_copy(
            src_ref=_half(outs[t].at[slot], axes[t], which), dst_ref=_half(outs[t].at[slot], axes[t], which),
            send_sem=fsend.at[3 * t + k], recv_sem=frecv.at[3 * t + k], device_id=sibling, device_id_type=MESH)

    forwards = []
    for t in range(n):
        for k, (px, py) in enumerate(chips):
            over_ici(t, k, (px, py), 2 * px + py).wait_recv()
            fw = to_sibling(t, k, 2 * px + py, c)
            fw.start()
            forwards.append(fw)
    for t in range(n):
        for k, (px, py) in enumerate(chips):
            to_sibling(t, k, 2 * px + py, 1 - c).wait_recv()
    for t in range(n):
        for k, chip in enumerate(chips):
            over_ici(t, k, chip, me).wait_send()
    for fw in forwards:
        fw.wait_send()
    for t in range(n):
        pltpu.make_async_copy(ins[t], outs[t].at[me], local.at[t]).wait()


def _gather_sems(n):
    return [pltpu.SemaphoreType.DMA((3 * n,)), pltpu.SemaphoreType.DMA((3 * n,)), pltpu.SemaphoreType.DMA((3 * n,)),
            pltpu.SemaphoreType.DMA((3 * n,)), pltpu.SemaphoreType.DMA((n,))]


def _gathered_shapes(shards):
    return [jax.ShapeDtypeStruct((N_CHIPS,) + s.shape, s.dtype) for s in shards]


def _swap_copies(srcs, dsts, axes, send, recv):
    x, y, c = lax.axis_index("x"), lax.axis_index("y"), lax.axis_index("c")
    return [pltpu.make_async_remote_copy(
        src_ref=srcs[t] if axes[t] is None else _half(srcs[t], axes[t], 1 - c), dst_ref=dsts[t],
        send_sem=send.at[t], recv_sem=recv.at[t], device_id=(x, y, 1 - c), device_id_type=MESH) for t in range(len(srcs))]


def _swap_shapes(srcs, axes):
    return [jax.ShapeDtypeStruct(g.shape if a is None else _half_shape(g.shape, a), g.dtype) for g, a in zip(srcs, axes)]


def _add_halves_w_in(parts, rbs, core):
    na, nb_rows, w = rbs[0].shape[0], parts[1].shape[0], rbs[0].shape[1]
    per = (na + nb_rows + N_HEADS) // N_CHIPS
    pieces = [[] for _ in range(N_CHIPS)]
    r = 0
    for src, lo, hi in ((0, 0, na), (1, 0, ATTN_WIDTH), (2, 0, N_HEADS), (1, ATTN_WIDTH, nb_rows)):
        while lo < hi:
            c, at = divmod(r, per)
            n = min(hi - lo, per - at)
            pieces[c].append((src, lo, n, at))
            lo, r = lo + n, r + n

    def body(core_ref, a_ref, b_ref, f_ref, ra_ref, rf_ref, o_ref, am, bm, fm, ar, br, fr, rows_buf, sems, send, recv):
        x, y, core_id = lax.axis_index("x"), lax.axis_index("y"), lax.axis_index("c")
        swap = pltpu.make_async_remote_copy(src_ref=_half(b_ref, 1, 1 - core_id), dst_ref=br, send_sem=send.at[0],
                                            recv_sem=recv.at[0], device_id=(x, y, 1 - core_id), device_id_type=MESH)
        swap.start()
        cols = pl.ds(pl.multiple_of(core_ref[0] * w, w), w)
        few = pl.ds(0, N_HEADS)
        loads = [(pltpu.make_async_copy(a_ref.at[:, cols], am, sems.at[0]), pltpu.make_async_copy(ra_ref, ar, sems.at[1])),
                 (pltpu.make_async_copy(b_ref.at[:, cols], bm, sems.at[2]),),
                 (pltpu.make_async_copy(f_ref.at[few, cols], fm, sems.at[3]),
                  pltpu.make_async_copy(rf_ref.at[few, :], fr, sems.at[4]))]
        for group in loads:
            for cp in group:
                cp.start()
        bufs = [(am, ar), (bm, br), (fm, fr)]
        there = set()
        for c in range(N_CHIPS):
            for src, lo, n, at in pieces[c]:
                if src not in there:
                    for cp in loads[src]:
                        cp.wait()
                    if src == 1:
                        swap.wait_recv()
                    there.add(src)
                mine, other = bufs[src]
                rows_buf[at:at + n, :] = mine[lo:lo + n, :] + other[lo:lo + n, :]
            o_ref[c] = rows_buf[0:per, :].astype(BF16)
        swap.wait_send()

    any_space = pl.BlockSpec(memory_space=pl.ANY)
    a, b, f = parts
    return pl.pallas_call(
        body, name="grad_add_sibling_w_in",
        grid_spec=pltpu.PrefetchScalarGridSpec(
            num_scalar_prefetch=1, grid=(1,),
            in_specs=[any_space, HBM, any_space, any_space, any_space],
            out_specs=pl.BlockSpec((N_CHIPS, per, w), lambda i, cr: (0, 0, 0)),
            scratch_shapes=[pltpu.VMEM((na, w), F32), pltpu.VMEM((nb_rows, w), F32), pltpu.VMEM((N_HEADS, w), F32),
                            pltpu.VMEM((na, w), F32), pltpu.VMEM((nb_rows, w), F32), pltpu.VMEM((N_HEADS, w), F32),
                            pltpu.VMEM((per + (-per) % 8, w), F32), pltpu.SemaphoreType.DMA((5,)),
                            pltpu.SemaphoreType.DMA((1,)), pltpu.SemaphoreType.DMA((1,))]),
        out_shape=jax.ShapeDtypeStruct((N_CHIPS, per, w), BF16),
        compiler_params=_params(("arbitrary",)),
    )(core, a, *_in_hbm(b), f, *rbs)


def _weight_sections(wg):
    per, d = wg.shape[1], wg.shape[2]
    n_ref = N_CHIPS * per
    units = [(0, 0, REF_U, REF_Q), (1, 0, REF_Q, REF_ZA), (0, REF_Q - REF_U, REF_ZA, REF_F), (2, 0, REF_F, REF_GP),
             (0, REF_Q - REF_U + REF_F - REF_ZA, REF_GP, n_ref)]
    plan = []
    for out, first, lo, hi in sorted(units, key=lambda u: u[2]):
        pieces, at = [], 0
        while lo < hi:
            c, src = divmod(lo, per)
            n = min(hi - lo, per - src)
            pieces.append((c, src, n, at))
            lo, at = lo + n, at + n
        plan.append((out, first, at, pieces))
    stage_rows = max(rows for _, _, rows, _ in plan)

    def body(g_ref, e_ref, q_ref, f_ref, gb, stage, sems):
        loads = [pltpu.make_async_copy(g_ref.at[c], gb.at[c], sems.at[c]) for c in range(N_CHIPS)]
        for cp in loads:
            cp.start()
        outs = [e_ref, q_ref, f_ref]
        there = set()
        for out, first, rows, pieces in plan:
            for c, _, _, _ in pieces:
                if c not in there:
                    loads[c].wait()
                    there.add(c)
            if len(pieces) == 1 and pieces[0][1] % 16 == 0 and rows % 16 == 0:
                c, src, _, _ = pieces[0]
                outs[out][first:first + rows, :] = gb[c, src:src + rows, :]
                continue
            for c, src, n, at in pieces:
                stage[at:at + n, :] = gb[c, src:src + n, :].astype(F32)
            padded = rows if rows % 16 == 0 else outs[out].shape[0] - first
            if padded > rows:
                stage[rows:padded, :] = jnp.zeros((padded - rows, d), F32)
            outs[out][first:first + padded, :] = stage[0:padded, :].astype(BF16)

    return pl.pallas_call(
        body, name="weight_sections",
        in_specs=[pl.BlockSpec(memory_space=pl.ANY)],
        out_shape=[jax.ShapeDtypeStruct((E_COLS, d), BF16), jax.ShapeDtypeStruct((QKV_COLS, d), BF16),
                   jax.ShapeDtypeStruct((F_COLS, d), BF16)],
        scratch_shapes=[pltpu.VMEM(wg.shape, BF16), pltpu.VMEM((stage_rows, d), F32), pltpu.SemaphoreType.DMA((N_CHIPS,))],
        compiler_params=_params(),
    )(wg)


def _add_sibling_half(gs, rbs, axes, small, sr, core):
    n = len(gs)

    def body(core_ref, *refs):
        g_refs, rb_refs = refs[:n], refs[n:2 * n]
        s_ref, sr_ref = refs[2 * n], refs[2 * n + 1]
        h_refs, sc_ref = refs[2 * n + 2:3 * n + 2], refs[3 * n + 2]
        for t in range(n):
            h_refs[t][...] = (g_refs[t][...] + rb_refs[t][...]).astype(BF16)
        sc_ref[...] = s_ref[...] + sr_ref[...]

    def mine(rb, axis):
        blk = (None,) + rb.shape[1:]
        if axis == 2:
            return pl.BlockSpec(blk, lambda j, cr: (j, 0, cr[0]))
        return pl.BlockSpec(blk, lambda j, cr: (j, cr[0], 0))

    chunk = lambda rb: pl.BlockSpec((None,) + rb.shape[1:], lambda j, cr: (j, 0, 0))
    whole = pl.BlockSpec(small.shape, lambda j, cr: (0, 0))
    return pl.pallas_call(
        body, name="grad_add_sibling",
        grid_spec=pltpu.PrefetchScalarGridSpec(
            num_scalar_prefetch=1, grid=(N_CHIPS,),
            in_specs=[mine(rb, a) for rb, a in zip(rbs, axes)] + [chunk(rb) for rb in rbs] + [whole, whole],
            out_specs=[chunk(rb) for rb in rbs] + [whole]),
        out_shape=[jax.ShapeDtypeStruct(rb.shape, BF16) for rb in rbs] + [jax.ShapeDtypeStruct(small.shape, F32)],
        compiler_params=_params(("arbitrary",)),
    )(core, *gs, *rbs, small, sr)


def _chip_scatter_plan(h_refs, s_ref, q_refs, sq_ref, send, recv, local):
    n = len(h_refs)
    nt = n + (s_ref is not None)
    x, y, c = lax.axis_index("x"), lax.axis_index("y"), lax.axis_index("c")
    me = 2 * x + y
    chips = [(1 - x, y), (x, 1 - y), (1 - x, 1 - y)]

    def copy(t, k, chip, src_slot, dst_slot):
        src = h_refs[t].at[src_slot] if t < n else s_ref
        dst = (q_refs[t] if t < n else sq_ref).at[dst_slot]
        return pltpu.make_async_remote_copy(src_ref=src, dst_ref=dst, send_sem=send.at[3 * t + k],
                                            recv_sem=recv.at[3 * t + k], device_id=(*chip, c), device_id_type=MESH)

    own = [pltpu.make_async_copy(h_refs[t].at[me], q_refs[t].at[me], local.at[t]) for t in range(n)]
    if s_ref is not None:
        own.append(pltpu.make_async_copy(s_ref, sq_ref.at[me], local.at[n]))
    sends = [copy(t, k, (px, py), 2 * px + py, me) for t in range(nt) for k, (px, py) in enumerate(chips)]
    recvs = [copy(t, k, (px, py), me, 2 * px + py) for t in range(nt) for k, (px, py) in enumerate(chips)]
    return own, sends, recvs


def _scatter_operands(hs, sc):
    small = [] if sc is None else [sc]
    nt = len(hs) + len(small)
    shapes = [jax.ShapeDtypeStruct(h.shape, h.dtype) for h in hs] + [
        jax.ShapeDtypeStruct((N_CHIPS,) + a.shape, a.dtype) for a in small]
    sems = [pltpu.SemaphoreType.DMA((3 * nt,)), pltpu.SemaphoreType.DMA((3 * nt,)), pltpu.SemaphoreType.DMA((nt,))]
    return list(hs) + small, shapes, sems


def _reduce_allgather(qs, axes, sq, as_rows, late):
    n = len(qs)
    shard_shapes, half_axes = [], []
    for q, a, rows_form in zip(qs, axes, as_rows):
        shape = [d * 2 if i == a - 1 else d for i, d in enumerate(q.shape[1:])]
        if rows_form:
            assert a == 2
            shape = [shape[0], 1, shape[1]]
        shard_shapes.append(tuple(shape))
        half_axes.append(2 if rows_form else a - 1)

    def body(*refs):
        q_refs, sq_ref, late_ref = refs[:n], refs[n], refs[n + 1]
        o_refs, st_ref, late_sum_ref = refs[n + 2:2 * n + 2], refs[2 * n + 2], refs[2 * n + 3]
        sib_buf, chip_buf, send, recv, late_send, late_recv = refs[2 * n + 4:]
        x, y, c = lax.axis_index("x"), lax.axis_index("y"), lax.axis_index("c")
        me = 2 * x + y
        chips = [(1 - x, y), (x, 1 - y), (1 - x, 1 - y)]

        def swap(t, which):
            return pltpu.make_async_remote_copy(
                src_ref=_half(o_refs[t], half_axes[t], which), dst_ref=_half(o_refs[t], half_axes[t], which),
                send_sem=send.at[t], recv_sem=recv.at[t], device_id=(x, y, 1 - c), device_id_type=MESH)

        def late_copy(k, chip, slot):
            return pltpu.make_async_remote_copy(src_ref=chip_buf.at[slot], dst_ref=chip_buf.at[slot], send_sem=late_send.at[1 + k],
                                                recv_sem=late_recv.at[1 + k], device_id=(*chip, c), device_id_type=MESH)

        late_swap = pltpu.make_async_remote_copy(src_ref=late_ref, dst_ref=sib_buf, send_sem=late_send.at[0],
                                                 recv_sem=late_recv.at[0], device_id=(x, y, 1 - c), device_id_type=MESH)
        late_swap.start()
        sent = []
        for t in range(n):
            q = q_refs[t]
            total = ((q[0].astype(F32) + q[1].astype(F32)) + q[2].astype(F32)) + q[3].astype(F32)
            if as_rows[t]:
                total = total.reshape(total.shape[0], 1, total.shape[1])
            _half(o_refs[t], half_axes[t], c)[...] = total
            cp = swap(t, c)
            cp.start()
            sent.append(cp)
            if t == 0:
                late_swap.wait()
                chip_buf[me] = late_ref[...] + sib_buf[...]
                late_sends = [late_copy(k, chip, me) for k, chip in enumerate(chips)]
                for lc in late_sends:
                    lc.start()
        st_ref[...] = ((sq_ref[0] + sq_ref[1]) + sq_ref[2]) + sq_ref[3]
        for t in range(n):
            swap(t, 1 - c).wait_recv()
        for cp in sent:
            cp.wait_send()
        for k, (px, py) in enumerate(chips):
            late_copy(k, (px, py), 2 * px + py).wait_recv()
        for lc in late_sends:
            lc.wait_send()
        late_sum_ref[...] = ((chip_buf[0] + chip_buf[1]) + chip_buf[2]) + chip_buf[3]

    vmem = pl.BlockSpec(memory_space=pltpu.VMEM)
    return pl.pallas_call(
        body, name="grad_reduce_allgather",
        in_specs=[vmem] * (n + 2), out_specs=[vmem] * (n + 2),
        out_shape=[jax.ShapeDtypeStruct(s, F32) for s in shard_shapes] + [jax.ShapeDtypeStruct(sq.shape[1:], F32),
                                                                         jax.ShapeDtypeStruct(late.shape, F32)],
        scratch_shapes=[pltpu.VMEM(late.shape, F32), pltpu.VMEM((N_CHIPS,) + late.shape, F32),
                        pltpu.SemaphoreType.DMA((n,)), pltpu.SemaphoreType.DMA((n,)),
                        pltpu.SemaphoreType.DMA((4,)), pltpu.SemaphoreType.DMA((4,))],
        compiler_params=_params(),
    )(*qs, sq, late)


def _dot_nt(a, b):
    return lax.dot_general(a, b, (((1,), (1,)), ((), ())), preferred_element_type=F32)


def _dot_tn(a, b):
    return lax.dot_general(a, b, (((0,), (0,)), ((), ())), preferred_element_type=F32)


def _mm_nt(a, bt, out_dtype, name, m, n, tm, tn, row_block=0, scale_first=None, gather=None):
    k = a.shape[1]
    shards, axes = gather if gather is not None else ([], [])
    ng = len(shards)
    grid = (m // tm, n // tn)

    def body(a_ref, b_ref, *rest):
        ins, o_ref, outs, sems = rest[:ng], rest[ng], rest[ng + 1:2 * ng + 1], rest[2 * ng + 1:]
        first = (pl.program_id(0) == 0) & (pl.program_id(1) == 0)
        last = (pl.program_id(0) == grid[0] - 1) & (pl.program_id(1) == grid[1] - 1)
        if ng:
            @pl.when(first)
            def _():
                _gather_start(ins, outs, axes, *sems)

        r = _dot_nt(a_ref[...], b_ref[...])
        if scale_first is not None:
            r = r * jnp.where(pl.program_id(1) == 0, scale_first, 1.0)
        o_ref[...] = r.astype(out_dtype)

        if ng:
            @pl.when(last)
            def _():
                _gather_finish(ins, outs, axes, *sems)

    res = pl.pallas_call(
        body, name=name, grid=grid,
        in_specs=[pl.BlockSpec((tm, k), lambda i, j: (i, 0)), pl.BlockSpec((tn, k), lambda i, j: (row_block + j, 0))]
        + [HBM] * ng,
        out_specs=[pl.BlockSpec((tm, tn), lambda i, j: (i, j))] + [HBM] * ng,
        out_shape=[jax.ShapeDtypeStruct((m, n), out_dtype)] + _gathered_shapes(shards),
        scratch_shapes=_gather_sems(ng) if ng else [],
        compiler_params=_params(("arbitrary", "arbitrary") if ng else ("parallel", "parallel")),
    )(a, bt, *_in_hbm(*shards))
    return (res[0], res[1:]) if ng else res[0]


def _mm_tn(a, b, name, k, tm, tn, tk, chunks=1, m=None, m_off=0, swap=None, scatter=None):
    m = a.shape[1] if m is None else m
    n = b.shape[1]
    cw = n // chunks
    srcs, axes = swap if swap is not None else ([], [])
    ns = len(srcs)
    hs, sc = scatter if scatter is not None else ([], None)
    sc_in, sc_shapes, sc_sems = _scatter_operands(hs, sc) if scatter is not None else ([], [], [])
    nh, nq = len(hs), len(sc_in)
    grid = (m // tm, n // tn, k // tk)

    def body(a_ref, b_ref, *rest):
        s_refs, h_refs = rest[:ns], rest[ns:ns + nq]
        o_ref = rest[ns + nq]
        d_refs, q_refs = rest[ns + nq + 1:2 * ns + nq + 1], rest[2 * ns + nq + 1:2 * ns + 2 * nq + 1]
        sems = rest[2 * ns + 2 * nq + 1:]
        swap_sems, scatter_sems = (sems[:2], sems[2:]) if ns else ((), sems)
        ids = [pl.program_id(d) for d in range(3)]

        def scatter_plan():
            small_in = h_refs[nh] if sc is not None else None
            small_out = q_refs[nh] if sc is not None else None
            return _chip_scatter_plan(h_refs[:nh], small_in, q_refs[:nh], small_out, *scatter_sems)

        if ns or nq:
            @pl.when((ids[0] == 0) & (ids[1] == 0) & (ids[2] == 0))
            def _():
                if ns:
                    for cp in _swap_copies(s_refs, d_refs, axes, *swap_sems):
                        cp.start()
                if nq:
                    own, sends, _ = scatter_plan()
                    for cp in own + sends:
                        cp.start()

        @pl.when(ids[2] == 0)
        def _():
            o_ref[...] = jnp.zeros_like(o_ref)
        r = _dot_tn(a_ref[...].astype(BF16), b_ref[...].astype(BF16))
        if chunks > 1:
            for c in range(chunks):
                o_ref[c] += r[:, c * cw:(c + 1) * cw]
        else:
            o_ref[...] += r

        if ns or nq:
            @pl.when((ids[0] == grid[0] - 1) & (ids[1] == grid[1] - 1) & (ids[2] == grid[2] - 1))
            def _():
                if ns:
                    for cp in _swap_copies(s_refs, d_refs, axes, *swap_sems):
                        cp.wait()
                if nq:
                    own, sends, recvs = scatter_plan()
                    for cp in recvs:
                        cp.wait_recv()
                    for cp in sends:
                        cp.wait_send()
                    for cp in own:
                        cp.wait()

    if chunks > 1:
        assert tn == n
        out_spec = pl.BlockSpec((chunks, tm, cw), lambda i, j, kk: (0, i, 0))
        out_shape = jax.ShapeDtypeStruct((chunks, m, cw), F32)
    else:
        out_spec = pl.BlockSpec((tm, tn), lambda i, j, kk: (i, j))
        out_shape = jax.ShapeDtypeStruct((m, n), F32)
    riders = ns + nq
    res = pl.pallas_call(
        body, name=name, grid=grid,
        in_specs=[pl.BlockSpec((tk, tm), lambda i, j, kk: (kk, m_off + i)), pl.BlockSpec((tk, tn), lambda i, j, kk: (kk, j))]
        + [HBM] * riders,
        out_specs=[out_spec] + [HBM] * riders, out_shape=[out_shape] + _swap_shapes(srcs, axes) + sc_shapes,
        scratch_shapes=([pltpu.SemaphoreType.DMA((ns,)), pltpu.SemaphoreType.DMA((ns,))] if ns else []) + sc_sems,
        compiler_params=_params(("arbitrary",) * 3 if riders else ("parallel", "parallel", "arbitrary")),
    )(a, b, *_in_hbm(*srcs, *sc_in))
    if not riders:
        return res[0]
    out = [res[0]]
    if ns:
        out.append(res[1:1 + ns])
    if nq:
        out.append((res[1 + ns:1 + ns + nh], res[1 + ns + nh] if sc is not None else None))
    return tuple(out)


def _tokens_spec():
    return pl.BlockSpec((ROW_TILE, D_MODEL), lambda i: (jnp.maximum(i - 1, 0), 0))


def _rmsnorm_fwd(x2, g1, lk, shards, axes):
    nb = x2.shape[0] // ROW_TILE + 1
    nblk = lk // ROW_TILE
    ng = len(shards)
    meta_cols = shards[-1].shape[1]

    def body(x_ref, g_ref, *rest):
        ins, hn_ref, outs = rest[:ng], rest[ng], rest[ng + 1:2 * ng + 1]
        sems, meta_buf, meta_sem = rest[2 * ng + 1:2 * ng + 6], rest[2 * ng + 6], rest[2 * ng + 7]
        s = pl.program_id(0)
        blk = (s + 1) % nblk

        @pl.when(s == 0)
        def _():
            _gather_start(ins, outs, axes, *sems)

        def normed(h):
            r = lax.rsqrt(jnp.mean(h * h, axis=-1, keepdims=True) + RMS_EPS)
            return ((h * r) * g_ref[...]).astype(BF16)

        @pl.when(s < nblk - 1)
        def _():
            hn_ref[...] = normed(jnp.where(blk >= nb, 0.0, x_ref[...]))

        @pl.when(s == nblk - 1)
        def _():
            _gather_finish(ins, outs, axes, *sems)
            fetch = pltpu.make_async_copy(outs[-1], meta_buf, meta_sem.at[0])
            fetch.start()
            fetch.wait()
            meta = jnp.concatenate([meta_buf[j] for j in range(N_CHIPS)], axis=1)
            hn_ref[...] = normed(jnp.concatenate([jnp.zeros((PAD_ROWS, D_MODEL), F32), meta], axis=0))

    res = pl.pallas_call(
        body, name="rmsnorm_fwd", grid=(nblk,),
        in_specs=[pl.BlockSpec((ROW_TILE, D_MODEL), lambda s: (jnp.clip((s + 1) % nblk - 1, 0, nb - 2), 0)),
                  pl.BlockSpec((1, D_MODEL), lambda s: (0, 0))] + [HBM] * ng,
        out_specs=[pl.BlockSpec((ROW_TILE, D_MODEL), lambda s: ((s + 1) % nblk, 0))] + [HBM] * ng,
        out_shape=[jax.ShapeDtypeStruct((lk, D_MODEL), BF16)] + _gathered_shapes(shards),
        scratch_shapes=_gather_sems(ng) + [pltpu.VMEM((N_CHIPS, N_META, meta_cols), F32), pltpu.SemaphoreType.DMA((1,))],
        compiler_params=_params(("arbitrary",)),
    )(x2, g1, *_in_hbm(*shards))
    return res[0], res[1:]


def _valid_gate_mask(i):
    row = i * ROW_TILE + lax.broadcasted_iota(jnp.int32, (ROW_TILE, F_COLS), 0)
    col = lax.broadcasted_iota(jnp.int32, (ROW_TILE, F_COLS), 1)
    return (row >= PAD_ROWS) & (col < N_HEADS)


def _gates_fwd(f, bfg):
    nb = f.shape[0] // ROW_TILE

    def body(f_ref, b_ref, c_ref, carry):
        i = pl.program_id(0)

        @pl.when(i == 0)
        def _():
            carry[...] = jnp.zeros_like(carry)

        logit = f_ref[...] + b_ref[...]
        lf = jnp.minimum(logit, 0.0) - jnp.log1p(jnp.exp(-jnp.abs(logit)))
        lf = jnp.where(_valid_gate_mask(i), lf, 0.0)
        r_i = lax.broadcasted_iota(jnp.int32, (ROW_TILE, ROW_TILE), 0)
        c_i = lax.broadcasted_iota(jnp.int32, (ROW_TILE, ROW_TILE), 1)
        tri = (c_i <= r_i).astype(F32)
        c_ref[...] = jnp.dot(tri, lf, precision=HIGHEST, preferred_element_type=F32) + carry[...]
        carry[...] = carry[...] + jnp.sum(lf, axis=0, keepdims=True)

    return pl.pallas_call(
        body, name="gates_fwd", grid=(nb,),
        in_specs=[pl.BlockSpec((ROW_TILE, F_COLS), lambda i: (i, 0)), pl.BlockSpec((1, F_COLS), lambda i: (0, 0))],
        out_specs=pl.BlockSpec((ROW_TILE, F_COLS), lambda i: (i, 0)),
        out_shape=jax.ShapeDtypeStruct(f.shape, F32),
        scratch_shapes=[pltpu.VMEM((1, F_COLS), F32)],
        compiler_params=_params(("arbitrary",)),
    )(f, bfg)


def _pool_counts(i):
    row = i * ROW_TILE + lax.broadcasted_iota(jnp.int32, (ROW_TILE, 1), 0)
    return jnp.maximum(row - PAD_ROWS, 0)


def _trailing_sums(xc, levels):
    acc = xc
    for lv in range(levels):
        acc = acc + pltpu.roll(acc, 1 << lv, 0)
    return acc


def _leading_sums(xc, levels):
    n = xc.shape[0]
    acc = xc
    for lv in range(levels):
        acc = acc + pltpu.roll(acc, n - (1 << lv), 0)
    return acc


def _pool_p(u_cur, u_prev, i):
    pos = _pool_counts(i)
    ps, invs = [], []
    for g, w in enumerate(POOL_WINDOWS):
        sl = slice(g * POOL_GROUP, (g + 1) * POOL_GROUP)
        cur = u_cur[:, sl]
        xc = jnp.concatenate([u_prev[:, sl], cur], axis=0)
        win = _trailing_sums(xc, g + 1)[ROW_TILE:, :]
        inv = 1.0 / jnp.minimum(pos + 1, w).astype(F32)
        ps.append(win * inv - cur)
        invs.append(inv)
    return ps, invs


def _pool_fwd(e, pw, scale):
    nb = e.shape[0] // ROW_TILE

    def body(uc_ref, up_ref, z_ref, pw_ref, sc_ref, y_ref):
        i = pl.program_id(0)
        u_cur = uc_ref[...].astype(F32)
        u_prev = jnp.where(i == 0, 0.0, up_ref[...].astype(F32))
        ps, _ = _pool_p(u_cur, u_prev, i)
        z = z_ref[...].astype(F32)
        gate = z * _sigmoid(z)
        for g in range(len(POOL_WINDOWS)):
            sl = slice(g * POOL_GROUP, (g + 1) * POOL_GROUP)
            yraw = jnp.dot(ps[g].astype(BF16), pw_ref[g], preferred_element_type=F32)
            y_ref[:, sl] = ((yraw * sc_ref[:, sl]) * gate[:, sl]).astype(BF16)

    blk = (ROW_TILE, POOL_WIDTH)
    return pl.pallas_call(
        body, name="pool_fwd", grid=(nb,),
        in_specs=[pl.BlockSpec(blk, lambda i: (i, 0)), pl.BlockSpec(blk, lambda i: (jnp.maximum(i - 1, 0), 0)),
                  pl.BlockSpec(blk, lambda i: (i, 1)),
                  pl.BlockSpec((len(POOL_WINDOWS), POOL_GROUP, POOL_GROUP), lambda i: (0, 0, 0)),
                  pl.BlockSpec((1, POOL_WIDTH), lambda i: (0, 0))],
        out_specs=pl.BlockSpec(blk, lambda i: (i, 0)),
        out_shape=jax.ShapeDtypeStruct((e.shape[0], POOL_WIDTH), BF16),
        compiler_params=_params(("parallel",)),
    )(e, e, e, pw, scale)


def _stack_heads(a):
    first = lax.broadcasted_iota(jnp.int32, a.shape, 1) < HEAD_DIM
    zero = jnp.zeros_like(a)
    return jnp.concatenate([jnp.where(first, a, zero), jnp.where(first, zero, a)], axis=0)


def _unstack_heads(a):
    rows = a.shape[0] // 2
    first = lax.broadcasted_iota(jnp.int32, (rows, LANES), 1) < HEAD_DIM
    return jnp.where(first, a[:rows], a[rows:])


def _causal(i, jj, stacked, kv_tile):
    r = lax.broadcasted_iota(jnp.int32, (stacked * ROW_TILE, kv_tile), 0)
    if stacked == 2:
        r = jnp.where(r >= ROW_TILE, r - ROW_TILE, r)
    kidx = jj * kv_tile + lax.broadcasted_iota(jnp.int32, (stacked * ROW_TILE, kv_tile), 1)
    return kidx <= i * ROW_TILE + r


def _stack_rows(b):
    n = b.shape[1]
    return jnp.concatenate([jnp.broadcast_to(b[0:1], (ROW_TILE, n)), jnp.broadcast_to(b[1:2], (ROW_TILE, n))], axis=0)


def _attn_fwd(qkv, bias, nb):
    lk = qkv.shape[0]
    lp = nb * ROW_TILE
    nkb = lk // KV_TILE
    n_pairs = N_HEADS // 2

    def body(q_ref, k_ref, v_ref, b_ref, o_ref, lse_ref):
        i = pl.program_id(1)
        qs = _stack_heads(q_ref[...])
        last = (i * ROW_TILE) // KV_TILE

        def block(jj, carry, masked, n_keys=KV_TILE):
            m, l, acc = carry
            rows = pl.ds(pl.multiple_of(jj * KV_TILE, KV_TILE), n_keys)
            s = _dot_nt(qs, k_ref[rows, :]) - _stack_rows(b_ref[0, 0, :, rows])
            if masked:
                s = jnp.where(_causal(i, jj, 2, KV_TILE)[:, :n_keys], s, NEG)
            m_new = jnp.maximum(m, jnp.max(s, axis=1, keepdims=True))
            alpha = jnp.exp(m - m_new)
            p = jnp.exp(s - m_new)
            l = alpha * l + jnp.sum(p, axis=1, keepdims=True)
            acc = alpha * acc + jnp.dot(p.astype(BF16), v_ref[rows, :], preferred_element_type=F32)
            return m_new, l, acc

        init = (jnp.full((2 * ROW_TILE, 1), NEG, F32), jnp.zeros((2 * ROW_TILE, 1), F32),
                jnp.zeros((2 * ROW_TILE, LANES), F32))
        def finish(carry):
            m, l, acc = carry
            o_ref[...] = _unstack_heads(acc / l)
            lse_ref[...] = _unstack_heads(jnp.broadcast_to(m + jnp.log(l), (2 * ROW_TILE, LANES)))

        carry = lax.fori_loop(0, last, lambda jj, c: block(jj, c, False), init)
        ends = [lambda c, n=(r + 1) * ROW_TILE: finish(block(last, c, True, n)) for r in range(KV_TILE // ROW_TILE)]
        lax.switch(i - last * (KV_TILE // ROW_TILE), ends, carry)

    return pl.pallas_call(
        body, name="attn_fwd", grid=(n_pairs, nb),
        in_specs=[pl.BlockSpec((ROW_TILE, LANES), lambda hp, i: (i, hp)),
                  pl.BlockSpec((lk, LANES), lambda hp, i: (0, n_pairs + hp)),
                  pl.BlockSpec((lk, LANES), lambda hp, i: (0, 2 * n_pairs + hp)),
                  pl.BlockSpec((1, 1, 2, lk), lambda hp, i: (i, hp, 0, 0))],
        out_specs=[pl.BlockSpec((ROW_TILE, LANES), lambda hp, i: (i, hp)),
                   pl.BlockSpec((ROW_TILE, LANES), lambda hp, i: (i, hp))],
        out_shape=[jax.ShapeDtypeStruct((lp, ATTN_WIDTH), F32), jax.ShapeDtypeStruct((lp, ATTN_WIDTH), F32)],
        compiler_params=_params(("parallel", "parallel")),
    )(qkv, qkv, qkv, bias)


def _merge_head_fwd_bwd(y_pool, o, e, wup_p, wup_a, w_out, x2, metapad, gf, target):
    lp = o.shape[0]
    nb = lp // ROW_TILE

    def body(yp_ref, o_ref, e_ref, wp_ref, wa_ref, w_ref, x_ref, mp_ref, g_ref, t_ref,
             mg_ref, ya_ref, dh_ref, loss_ref, dg_ref):
        i = pl.program_id(0)

        @pl.when(i == 0)
        def _():
            loss_ref[...] = jnp.zeros_like(loss_ref)
            dg_ref[...] = jnp.zeros_like(dg_ref)

        za = e_ref[:, E_ZA:E_GP].astype(F32)
        ya = (o_ref[...] * (za * _sigmoid(za))).astype(BF16)
        ya_ref[...] = ya
        a_pool = jnp.dot(yp_ref[...], wp_ref[...], preferred_element_type=F32)
        a_attn = jnp.dot(ya, wa_ref[...], preferred_element_type=F32)
        merged = (_sigmoid(e_ref[:, E_GP:E_GA].astype(F32)) * a_pool
                  + _sigmoid(e_ref[:, E_GA:E_COLS].astype(F32)) * a_attn).astype(BF16)
        mg_ref[...] = merged

        h0 = jnp.where(i == 0, mp_ref[...], x_ref[...])
        h1 = h0 + jnp.dot(merged, w_ref[...], preferred_element_type=F32)
        r = lax.rsqrt(jnp.mean(h1 * h1, axis=-1, keepdims=True) + RMS_EPS)
        xhat = h1 * r
        g = g_ref[...]
        err = jnp.where(i == 0, 0.0, xhat * g - t_ref[...])
        loss_ref[...] += 0.5 * jnp.sum(jnp.mean(err * err, axis=-1, keepdims=True))
        dy = err / D_MODEL
        dg_ref[...] += jnp.sum(dy * xhat, axis=0, keepdims=True)
        dxhat = dy * g
        dh_ref[...] = r * (dxhat - xhat * jnp.mean(dxhat * xhat, axis=-1, keepdims=True))

    row = lambda w: pl.BlockSpec((ROW_TILE, w), lambda i: (i, 0))
    const = lambda shape: pl.BlockSpec(shape, lambda i: (0, 0))
    return pl.pallas_call(
        body, name="merge_head_fwd_bwd", grid=(nb,),
        in_specs=[row(POOL_WIDTH), row(ATTN_WIDTH), row(E_COLS), const((POOL_WIDTH, D_MODEL)), const((ATTN_WIDTH, D_MODEL)),
                  const((D_MODEL, D_MODEL)), _tokens_spec(), const((ROW_TILE, D_MODEL)), const((1, D_MODEL)), _tokens_spec()],
        out_specs=[row(D_MODEL), row(ATTN_WIDTH), row(D_MODEL), const((1, LANES)), const((1, D_MODEL))],
        out_shape=[jax.ShapeDtypeStruct((lp, D_MODEL), BF16), jax.ShapeDtypeStruct((lp, ATTN_WIDTH), BF16),
                   jax.ShapeDtypeStruct((lp, D_MODEL), F32), jax.ShapeDtypeStruct((1, LANES), F32),
                   jax.ShapeDtypeStruct((1, D_MODEL), F32)],
        compiler_params=_params(("arbitrary",)),
    )(y_pool, o, e, wup_p, wup_a, w_out, x2, metapad, gf, target)


def _per_head_rowsum(t):
    head = lax.broadcasted_iota(jnp.int32, t.shape, 1) // HEAD_DIM
    out = jnp.zeros_like(t)
    for h in range(N_HEADS):
        sel = head == h
        out = jnp.where(sel, jnp.sum(jnp.where(sel, t, 0.0), axis=1, keepdims=True), out)
    return out


def _merge_bwd(dh1, w_out, y_pool, y_attn, wup_p, wup_a, e, o):
    lp = o.shape[0]
    nb = lp // ROW_TILE

    def body(dh_ref, wo_ref, yp_ref, ya_ref, wp_ref, wa_ref, e_ref, o_ref,
             dap_ref, daa_ref, dg_ref, do_ref, delta_ref, dyp_ref):
        dmerged = _dot_nt(dh_ref[...].astype(BF16), wo_ref[...])
        a_pool = jnp.dot(yp_ref[...], wp_ref[...], preferred_element_type=F32)
        a_attn = jnp.dot(ya_ref[...], wa_ref[...], preferred_element_type=F32)
        sp = _sigmoid(e_ref[:, E_GP:E_GA].astype(F32))
        sa = _sigmoid(e_ref[:, E_GA:E_COLS].astype(F32))
        dap = (dmerged * sp).astype(BF16)
        daa = (dmerged * sa).astype(BF16)
        dap_ref[...] = dap
        daa_ref[...] = daa
        dg_ref[:, ATTN_WIDTH:ATTN_WIDTH + D_MODEL] = (dmerged * a_pool * (sp * (1.0 - sp))).astype(BF16)
        dg_ref[:, ATTN_WIDTH + D_MODEL:] = (dmerged * a_attn * (sa * (1.0 - sa))).astype(BF16)
        dyp_ref[...] = _dot_nt(dap, wp_ref[...])
        dya = _dot_nt(daa, wa_ref[...])
        za = e_ref[:, E_ZA:E_GP].astype(F32)
        sz = _sigmoid(za)
        o = o_ref[...]
        do = dya * (za * sz)
        do_ref[...] = do.astype(BF16)
        dg_ref[:, :ATTN_WIDTH] = (dya * o * (sz * (1.0 + za * (1.0 - sz)))).astype(BF16)
        delta_ref[...] = _per_head_rowsum(do * o)

    row = lambda w: pl.BlockSpec((ROW_TILE, w), lambda i: (i, 0))
    full = lambda a: pl.BlockSpec(a.shape, lambda i: (0, 0))
    return pl.pallas_call(
        body, name="merge_bwd", grid=(nb,),
        in_specs=[row(D_MODEL), full(w_out), row(POOL_WIDTH), row(ATTN_WIDTH), full(wup_p), full(wup_a),
                  row(E_COLS), row(ATTN_WIDTH)],
        out_specs=[row(D_MODEL), row(D_MODEL), pl.BlockSpec((ROW_TILE, MAIN_COLS // 2), lambda i: (i, 1)),
                   row(ATTN_WIDTH), row(ATTN_WIDTH), row(POOL_WIDTH)],
        out_shape=[jax.ShapeDtypeStruct((lp, D_MODEL), BF16), jax.ShapeDtypeStruct((lp, D_MODEL), BF16),
                   jax.ShapeDtypeStruct((lp, MAIN_COLS), BF16),
                   jax.ShapeDtypeStruct((lp, ATTN_WIDTH), BF16), jax.ShapeDtypeStruct((lp, ATTN_WIDTH), F32),
                   jax.ShapeDtypeStruct((lp, POOL_WIDTH), F32)],
        compiler_params=_params(("parallel",)),
    )(dh1, w_out, y_pool, y_attn, wup_p, wup_a, e, o)


def _fill_dproj(dproj, dq, dk, dv):
    lp = dproj.shape[0]
    rows = _row_tile(lp, 1100)

    def body(base_ref, dq_ref, dk_ref, dv_ref, o_ref):
        j = pl.program_id(1)

        @pl.when(j == 0)
        def _():
            o_ref[...] = (dq_ref[...] * HEAD_DIM ** -0.5).astype(BF16)

        @pl.when(j == 1)
        def _():
            o_ref[...] = dk_ref[...]

        @pl.when(j == 2)
        def _():
            o_ref[...] = dv_ref[...]

    blk = pl.BlockSpec((rows, ATTN_WIDTH), lambda i, j: (i, 0))
    return pl.pallas_call(
        body, name="fill_dproj", grid=(lp // rows, 3),
        in_specs=[pl.BlockSpec(memory_space=pl.ANY)] + [blk] * 3,
        out_specs=pl.BlockSpec((rows, ATTN_WIDTH), lambda i, j: (i, 2 + j)),
        out_shape=jax.ShapeDtypeStruct(dproj.shape, BF16),
        input_output_aliases={0: 0},
        compiler_params=_params(("parallel", "arbitrary")),
    )(dproj, dq, dk, dv)


def _pool_bwd_local(e, dy_pool, pw, scale, dproj):
    lp = e.shape[0]
    nb = lp // ROW_TILE
    ng = len(POOL_WINDOWS)

    def body(uc_ref, up_ref, z_ref, dy_ref, pw_ref, sc_ref, base_ref, dpc_ref, dz_ref, dsc_ref, dpw_ref):
        i = pl.program_id(0)

        @pl.when(i == 0)
        def _():
            dsc_ref[...] = jnp.zeros_like(dsc_ref)
            dpw_ref[...] = jnp.zeros_like(dpw_ref)

        u_cur = uc_ref[...].astype(F32)
        u_prev = jnp.where(i == 0, 0.0, up_ref[...].astype(F32))
        ps, invs = _pool_p(u_cur, u_prev, i)
        z = z_ref[...].astype(F32)
        sz = _sigmoid(z)
        dy = dy_ref[...]
        dypre = dy * (z * sz)
        dsilu = sz * (1.0 + z * (1.0 - sz))
        for g in range(ng):
            sl = slice(g * POOL_GROUP, (g + 1) * POOL_GROUP)
            pb = ps[g].astype(BF16)
            w = pw_ref[g]
            yraw = jnp.dot(pb, w, preferred_element_type=F32)
            sc = sc_ref[:, sl]
            dz_ref[:, sl] = (dy[:, sl] * (yraw * sc) * dsilu[:, sl]).astype(BF16)
            dsc_ref[:, sl] += jnp.sum(dypre[:, sl] * yraw, axis=0, keepdims=True)
            dyraw = (dypre[:, sl] * sc).astype(BF16)
            dpw_ref[g] += _dot_tn(pb, dyraw)
            dpc_ref[:, sl] = _dot_nt(dyraw, w) * invs[g]

    blk = (ROW_TILE, POOL_WIDTH)
    return pl.pallas_call(
        body, name="pool_bwd_local", grid=(nb,),
        in_specs=[pl.BlockSpec(blk, lambda i: (i, 0)), pl.BlockSpec(blk, lambda i: (jnp.maximum(i - 1, 0), 0)),
                  pl.BlockSpec(blk, lambda i: (i, 1)), pl.BlockSpec(blk, lambda i: (i, 0)),
                  pl.BlockSpec((ng, POOL_GROUP, POOL_GROUP), lambda i: (0, 0, 0)),
                  pl.BlockSpec((1, POOL_WIDTH), lambda i: (0, 0)), pl.BlockSpec(memory_space=pl.ANY)],
        out_specs=[pl.BlockSpec(blk, lambda i: (i, 0)), pl.BlockSpec(blk, lambda i: (i, 1)),
                   pl.BlockSpec((1, POOL_WIDTH), lambda i: (0, 0)),
                   pl.BlockSpec((ng, POOL_GROUP, POOL_GROUP), lambda i: (0, 0, 0))],
        out_shape=[jax.ShapeDtypeStruct((lp, POOL_WIDTH), F32), jax.ShapeDtypeStruct(dproj.shape, BF16),
                   jax.ShapeDtypeStruct((1, POOL_WIDTH), F32),
                   jax.ShapeDtypeStruct((ng, POOL_GROUP, POOL_GROUP), F32)],
        input_output_aliases={6: 1},
        compiler_params=_params(("arbitrary",)),
    )(e, e, e, dy_pool, pw, scale, dproj)


def _pool_bwd_window(dpc, dproj):
    lp = dpc.shape[0]
    nb = lp // ROW_TILE

    def body(cur_ref, nxt_ref, base_ref, du_ref):
        i = pl.program_id(0)
        cur = cur_ref[...]
        nxt = jnp.where(i == nb - 1, 0.0, nxt_ref[...])
        pos = _pool_counts(i)
        for g, w in enumerate(POOL_WINDOWS):
            sl = slice(g * POOL_GROUP, (g + 1) * POOL_GROUP)
            xc = jnp.concatenate([cur[:, sl], nxt[:, sl]], axis=0)
            win = _leading_sums(xc, g + 1)[:ROW_TILE, :]
            dp = cur[:, sl] * jnp.minimum(pos + 1, w).astype(F32)
            du_ref[:, sl] = (win - dp).astype(BF16)

    blk = (ROW_TILE, POOL_WIDTH)
    return pl.pallas_call(
        body, name="pool_bwd_window", grid=(nb,),
        in_specs=[pl.BlockSpec(blk, lambda i: (i, 0)), pl.BlockSpec(blk, lambda i: (jnp.minimum(i + 1, nb - 1), 0)),
                  pl.BlockSpec(memory_space=pl.ANY)],
        out_specs=pl.BlockSpec(blk, lambda i: (i, 0)),
        out_shape=jax.ShapeDtypeStruct(dproj.shape, BF16),
        input_output_aliases={2: 0},
        compiler_params=_params(("parallel",)),
    )(dpc, dpc, dproj)


def _attn_bwd(qkv, do, lse, delta, bias, nb):
    lk = qkv.shape[0]
    lp = nb * ROW_TILE
    nkb = lk // KV_TILE_BWD
    n_pairs = N_HEADS // 2
    per_kv = KV_TILE_BWD // ROW_TILE

    def body(q_ref, k_ref, v_ref, do_ref, lse_ref, dl_ref, b_ref, dq_ref, dk_ref, dv_ref, dc_ref, dcq_ref,
             dk_acc, dv_acc, dc_acc):
        jj = pl.program_id(1)

        @pl.when(jj == 0)
        def _():
            dq_ref[...] = jnp.zeros_like(dq_ref)
            dcq_ref[...] = jnp.zeros_like(dcq_ref)

        dk_acc[...] = jnp.zeros_like(dk_acc)
        dv_acc[...] = jnp.zeros_like(dv_acc)
        dc_acc[...] = jnp.zeros_like(dc_acc)

        def block(i, n_keys, masked):
            kb, vb = k_ref[:n_keys, :], v_ref[:n_keys, :]
            rows = pl.ds(pl.multiple_of(i * ROW_TILE, ROW_TILE), ROW_TILE)
            qs = _stack_heads(q_ref[rows, :])
            dos = _stack_heads(do_ref[rows, :])
            lse_i, dl_i = lse_ref[rows, :], dl_ref[rows, :]
            s = _dot_nt(qs, kb)
            dp = _dot_nt(dos, vb)
            if masked:
                valid = _causal(i, jj, 1, KV_TILE_BWD)[:, :n_keys]
            ps, dss, dcs, rowsums = [], [], [], []
            for hd in range(2):
                half = slice(hd * ROW_TILE, (hd + 1) * ROW_TILE)
                col = slice(hd * HEAD_DIM, hd * HEAD_DIM + 1)
                sh = s[half] - b_ref[i, 0, hd:hd + 1, :n_keys]
                if masked:
                    sh = jnp.where(valid, sh, NEG)
                p = jnp.exp(sh - lse_i[:, col])
                ds = p * (dp[half] - dl_i[:, col])
                ps.append(p.astype(BF16))
                dss.append(ds.astype(BF16))
                dcs.append(jnp.sum(ds, axis=0, keepdims=True))
                rowsums.append(jnp.sum(ds, axis=1, keepdims=True))
            dsb = jnp.concatenate(dss, axis=0)
            dv_acc[:n_keys, :] += _dot_tn(jnp.concatenate(ps, axis=0), dos)
            dk_acc[:n_keys, :] += _dot_tn(dsb, qs)
            dc_acc[:, :n_keys] -= jnp.concatenate(dcs, axis=0)
            dq_ref[rows, :] += _unstack_heads(jnp.dot(dsb, kb, preferred_element_type=F32))
            dcq_ref[rows, :] += _unstack_heads(jnp.broadcast_to(jnp.concatenate(rowsums, axis=0), (2 * ROW_TILE, LANES)))

        first_q = per_kv * jj
        for r in range(per_kv):
            @pl.when(first_q + r < nb)
            def _():
                block(first_q + r, (r + 1) * ROW_TILE, True)

        def rest(i, carry):
            block(i, KV_TILE_BWD, False)
            return carry

        lax.fori_loop(jnp.minimum(first_q + per_kv, nb), nb, rest, 0)
        dk_ref[...] = dk_acc[...].astype(BF16)
        dv_ref[...] = dv_acc[...].astype(BF16)
        dc_ref[0, 0] = dc_acc[...]

    whole = lambda rows: pl.BlockSpec((rows, LANES), lambda hp, jj: (0, hp))
    kv_blk = lambda off: pl.BlockSpec((KV_TILE_BWD, LANES), lambda hp, jj: (jj, off + hp))
    return pl.pallas_call(
        body, name="attn_bwd", grid=(n_pairs, nkb),
        in_specs=[whole(lk), kv_blk(n_pairs), kv_blk(2 * n_pairs), whole(lp), whole(lp), whole(lp),
                  pl.BlockSpec((nb, 1, 2, KV_TILE_BWD), lambda hp, jj: (0, hp, 0, jj))],
        out_specs=[whole(lp), kv_blk(0), kv_blk(0),
                   pl.BlockSpec((1, 1, 2, KV_TILE_BWD), lambda hp, jj: (hp, jj, 0, 0)), whole(lp)],
        out_shape=[jax.ShapeDtypeStruct((lp, ATTN_WIDTH), F32), jax.ShapeDtypeStruct((lk, ATTN_WIDTH), BF16),
                   jax.ShapeDtypeStruct((lk, ATTN_WIDTH), BF16),
                   jax.ShapeDtypeStruct((n_pairs, nkb, 2, KV_TILE_BWD), F32),
                   jax.ShapeDtypeStruct((lp, ATTN_WIDTH), F32)],
        scratch_shapes=[pltpu.VMEM((KV_TILE_BWD, LANES), F32), pltpu.VMEM((KV_TILE_BWD, LANES), F32),
                        pltpu.VMEM((2, KV_TILE_BWD), F32)],
        compiler_params=_params(("parallel", "arbitrary")),
    )(qkv, qkv, qkv, do, lse, delta, bias)


def _gates_bwd(dc, dcq, f, bfg):
    nb = f.shape[0] // ROW_TILE

    def body(dc_ref, dcq_ref, f_ref, b_ref, df_ref, db_ref, carry):
        step = pl.program_id(0)
        i = nb - 1 - step

        @pl.when(step == 0)
        def _():
            carry[...] = jnp.zeros_like(carry)
            db_ref[...] = jnp.zeros_like(db_ref)

        dcb = dc_ref[...]
        lane = lax.broadcasted_iota(jnp.int32, (ROW_TILE, F_COLS), 1)
        for h in range(N_HEADS):
            dcb = dcb + jnp.where(lane == h, dcq_ref[:, HEAD_DIM * h:HEAD_DIM * h + 1], 0.0)
        r_i = lax.broadcasted_iota(jnp.int32, (ROW_TILE, ROW_TILE), 0)
        c_i = lax.broadcasted_iota(jnp.int32, (ROW_TILE, ROW_TILE), 1)
        upper = (c_i >= r_i).astype(F32)
        dlf = jnp.dot(upper, dcb, precision=HIGHEST, preferred_element_type=F32) + carry[...]
        carry[...] = carry[...] + jnp.sum(dcb, axis=0, keepdims=True)
        logit = f_ref[...] + b_ref[...]
        dlogit = jnp.where(_valid_gate_mask(i), dlf * _sigmoid(-logit), 0.0)
        df_ref[...] = dlogit.astype(BF16)
        db_ref[...] += jnp.sum(dlogit, axis=0, keepdims=True)

    blk = pl.BlockSpec((ROW_TILE, F_COLS), lambda s: (nb - 1 - s, 0))
    wide = pl.BlockSpec((ROW_TILE, ATTN_WIDTH), lambda s: (nb - 1 - s, 0))
    one = pl.BlockSpec((1, F_COLS), lambda s: (0, 0))
    return pl.pallas_call(
        body, name="gates_bwd", grid=(nb,),
        in_specs=[blk, wide, blk, one], out_specs=[blk, one],
        out_shape=[jax.ShapeDtypeStruct(f.shape, BF16), jax.ShapeDtypeStruct((1, F_COLS), F32)],
        scratch_shapes=[pltpu.VMEM((1, F_COLS), F32)],
        compiler_params=_params(("arbitrary",)),
    )(dc, dcq, f, bfg)


def _input_bwd(dproj, df, wt_e, wt_qkv, wt_f, x2, metapad, dh1, g1, hs, sc=None):
    nb = x2.shape[0] // ROW_TILE + 1
    n = len(hs)
    sc_in, sc_shapes, sc_sems = _scatter_operands(hs, sc)
    nq = len(sc_in)

    def body(dp_ref, df_ref, we_ref, w_ref, wf_ref, x_ref, mp_ref, dh_ref, g_ref, *rest):
        h_refs, s_ref = rest[:n], (rest[n] if sc is not None else None)
        gx_ref, g0_ref, dg_ref = rest[nq:nq + 3]
        q_refs, sq_ref = rest[nq + 3:nq + 3 + n], (rest[nq + 3 + n] if sc is not None else None)
        sems = rest[2 * nq + 3:]
        i = pl.program_id(0)

        @pl.when(i == 0)
        def _():
            dg_ref[...] = jnp.zeros_like(dg_ref)
            own, sends, _ = _chip_scatter_plan(h_refs, s_ref, q_refs, sq_ref, *sems)
            for cp in own + sends:
                cp.start()

        dhn = (jnp.dot(dp_ref[:, :E_ZA], we_ref[:E_ZA, :], preferred_element_type=F32)
               + jnp.dot(dp_ref[:, E_ZA:MAIN_COLS // 2], w_ref[...], preferred_element_type=F32)
               + jnp.dot(dp_ref[:, MAIN_COLS // 2:], we_ref[E_ZA:, :], preferred_element_type=F32)
               + jnp.dot(df_ref[...], wf_ref[...], preferred_element_type=F32))
        h0 = jnp.where(i == 0, mp_ref[...], x_ref[...])
        r = lax.rsqrt(jnp.mean(h0 * h0, axis=-1, keepdims=True) + RMS_EPS)
        xhat = h0 * r
        dg_ref[...] += jnp.sum(dhn * xhat, axis=0, keepdims=True)
        dxhat = dhn * g_ref[...]
        dh0 = dh_ref[...] + r * (dxhat - xhat * jnp.mean(dxhat * xhat, axis=-1, keepdims=True))
        gx_ref[...] = dh0

        @pl.when(i == 0)
        def _():
            g0_ref[...] = dh0

        @pl.when(i == nb - 1)
        def _():
            own, sends, recvs = _chip_scatter_plan(h_refs, s_ref, q_refs, sq_ref, *sems)
            for cp in recvs:
                cp.wait_recv()
            for cp in sends:
                cp.wait_send()
            for cp in own:
                cp.wait()

    const = lambda shape: pl.BlockSpec(shape, lambda i: (0, 0))
    res = pl.pallas_call(
        body, name="input_bwd", grid=(nb,),
        in_specs=[pl.BlockSpec((ROW_TILE, MAIN_COLS), lambda i: (i, 0)), pl.BlockSpec((ROW_TILE, F_COLS), lambda i: (i, 0)),
                  const((E_COLS, D_MODEL)), const((QKV_COLS, D_MODEL)), const((F_COLS, D_MODEL)),
                  _tokens_spec(), const((ROW_TILE, D_MODEL)),
                  pl.BlockSpec((ROW_TILE, D_MODEL), lambda i: (i, 0)), const((1, D_MODEL))] + [HBM] * nq,
        out_specs=[_tokens_spec(), const((ROW_TILE, D_MODEL)), const((1, D_MODEL))] + [HBM] * nq,
        out_shape=[jax.ShapeDtypeStruct(x2.shape, F32), jax.ShapeDtypeStruct((ROW_TILE, D_MODEL), F32),
                   jax.ShapeDtypeStruct((1, D_MODEL), F32)] + sc_shapes,
        scratch_shapes=sc_sems,
        compiler_params=_params(("arbitrary",), vmem=56 * 1024 * 1024),
    )(dproj, df, wt_e, wt_qkv, wt_f, x2, metapad, dh1, g1, *_in_hbm(*sc_in))
    return res[:3], res[3:3 + n], (res[3 + n] if sc is not None else None)


def _adamw(w, g, m, v, name):
    rows, cols = w.shape
    if rows % 8 == 0:
        tr, tc = _row_tile8(rows), cols
    else:
        tr, tc = rows, (2 * LANES if cols % (2 * LANES) == 0 and rows > 8 else cols)

    def body(w_ref, g_ref, m_ref, v_ref, d_ref, mo_ref, vo_ref):
        g_ = g_ref[...]
        m_new = ADAM_B1 * m_ref[...] + (1.0 - ADAM_B1) * g_
        v_new = ADAM_B2 * v_ref[...] + (1.0 - ADAM_B2) * (g_ * g_)
        m_hat = m_new / (1.0 - ADAM_B1 ** ADAM_STEP)
        v_hat = v_new / (1.0 - ADAM_B2 ** ADAM_STEP)
        d_ref[...] = -ADAM_LR * (m_hat / (jnp.sqrt(v_hat) + ADAM_EPS) + ADAM_WD * w_ref[...])
        mo_ref[...] = m_new
        vo_ref[...] = v_new

    blk = pl.BlockSpec((tr, tc), lambda i, j: (i, j))
    return pl.pallas_call(
        body, name=name, grid=(rows // tr, cols // tc),
        in_specs=[blk] * 4, out_specs=[blk] * 3,
        out_shape=[jax.ShapeDtypeStruct(w.shape, F32)] * 3,
        compiler_params=_params(("parallel", "parallel")),
    )(w, g, m, v)


def _adamw_native(w3, g3, m3, v3, name):
    rows = w3.shape[0]
    tr = rows // 2
    blk = pl.BlockSpec((tr,) + w3.shape[1:], lambda i: (i, 0, 0))
    shape = jax.ShapeDtypeStruct(w3.shape, F32)

    def moments(g_ref, m_ref, v_ref, mo_ref, vo_ref):
        g_ = g_ref[...]
        mo_ref[...] = ADAM_B1 * m_ref[...] + (1.0 - ADAM_B1) * g_
        vo_ref[...] = ADAM_B2 * v_ref[...] + (1.0 - ADAM_B2) * (g_ * g_)

    new_m, new_v = pl.pallas_call(
        moments, name=name + "_moments", grid=(2,), in_specs=[blk] * 3, out_specs=[blk] * 2, out_shape=[shape] * 2,
        compiler_params=_params(("parallel",)),
    )(g3, m3, v3)

    def delta(w_ref, m_ref, v_ref, d_ref):
        m_hat = m_ref[...] / (1.0 - ADAM_B1 ** ADAM_STEP)
        v_hat = v_ref[...] / (1.0 - ADAM_B2 ** ADAM_STEP)
        d_ref[...] = -ADAM_LR * (m_hat / (jnp.sqrt(v_hat) + ADAM_EPS) + ADAM_WD * w_ref[...])

    d = pl.pallas_call(
        delta, name=name + "_delta", grid=(2,), in_specs=[blk] * 3, out_specs=blk, out_shape=shape,
        compiler_params=_params(("parallel",)),
    )(w3, new_m, new_v)
    return d, new_m, new_v


def _row_tile8(rows):
    best = rows
    for t in range(8, 257, 8):
        if rows % t == 0:
            best = t
    return best


def kernel(x, meta_tokens, norm_g, w_in, b_forget, pool_w, pool_scale, w_up_pool, w_up_attn, w_out, final_norm_g, loss_target, m_meta_tokens, m_norm_g, m_w_in, m_b_forget, m_pool_w, m_pool_scale, m_w_up_pool, m_w_up_attn, m_w_out, m_final_norm_g, v_meta_tokens, v_norm_g, v_w_in, v_b_forget, v_pool_w, v_pool_scale, v_w_up_pool, v_w_up_attn, v_w_out, v_final_norm_g):
    seq = x.shape[1]
    assert seq % ROW_TILE == 0 and x.shape[0] == 1
    lp = seq + ROW_TILE
    nb = lp // ROW_TILE
    lk = -(-lp // KV_TILE_BWD) * KV_TILE_BWD
    core = jnp.reshape(lax.axis_index("c"), (1,)).astype(jnp.int32)
    x2 = x[0]
    target = loss_target[0]
    sh_d = D_MODEL // N_CHIPS

    to_rows = lambda a: jnp.transpose(a, (2, 0, 1))
    from_rows = lambda a: jnp.transpose(a, (1, 2, 0))
    gf = final_norm_g.reshape(1, D_MODEL)
    bfg = jnp.pad(b_forget, ((0, 0), (0, F_COLS - N_HEADS)))
    pw_b = pool_w[0].astype(BF16)
    hn, (wg_in, meta_g) = _rmsnorm_fwd(x2, norm_g, lk, [jnp.transpose(w_in[0]).astype(BF16), meta_tokens], [1, 0])
    wt_e, wt_qkv, wt_f = _weight_sections(wg_in)
    meta_full = jnp.transpose(meta_g, (1, 0, 2)).reshape(N_META, D_MODEL)
    metapad = jnp.pad(meta_full, ((PAD_ROWS, 0), (0, 0)))

    tm = _row_tile(lp, 2200)
    e, (wg_up_p, wg_up_a, wg_out) = _mm_nt(
        hn, wt_e, BF16, "in_proj_gates", lp, E_COLS, tm, 512,
        gather=([w_up_pool[0].astype(BF16), w_up_attn[0].astype(BF16), w_out[0].astype(BF16)], [0, 0, 0]))
    wup_p = jnp.transpose(wg_up_p, (1, 0, 2)).reshape(POOL_WIDTH, D_MODEL)
    wup_a = jnp.transpose(wg_up_a, (1, 0, 2)).reshape(ATTN_WIDTH, D_MODEL)
    wout = wg_out.reshape(D_MODEL, D_MODEL)
    qkv = _mm_nt(hn, wt_qkv, BF16, "in_proj_qkv", lk, QKV_COLS, _row_tile(lk, 2600), 512, scale_first=HEAD_DIM ** -0.5)
    f = _mm_nt(hn, wt_f, F32, "in_proj_forget", lp, F_COLS, tm, F_COLS)
    c = _gates_fwd(f, bfg)
    c_t = jnp.transpose(c[:, :N_HEADS])
    c_first = jnp.transpose(c_t[:, ::ROW_TILE]).reshape(nb, N_HEADS // 2, 2, 1)
    c_keys = jnp.pad(c_t, ((0, 0), (0, lk - lp))).reshape(1, N_HEADS // 2, 2, lk)
    bias = jnp.where(jnp.arange(lk) < PAD_ROWS, -NEG, c_keys - c_first)
    y_pool = _pool_fwd(e, pw_b, pool_scale)
    o, lse = _attn_fwd(qkv, bias, nb)
    merged, y_attn, dh1, loss_part, dgf = _merge_head_fwd_bwd(y_pool, o, e, wup_p, wup_a, wout, x2, metapad, gf, target)

    dap, daa, dproj_half, do, delta, dy_pool = _merge_bwd(dh1, wout, y_pool, y_attn, wup_p, wup_a, e, o)
    dpc, dproj_zp, dscale, dpw = _pool_bwd_local(e, dy_pool, pw_b, pool_scale, dproj_half)
    dproj_u = _pool_bwd_window(dpc, dproj_zp)
    dq, dk, dv, dc4, dcq = _attn_bwd(qkv, do, lse, delta, bias, nb)
    dc = jnp.transpose(dc4, (1, 3, 0, 2)).reshape(lk, N_HEADS)[:lp]
    df, db = _gates_bwd(jnp.pad(dc, ((0, 0), (0, F_COLS - N_HEADS))), dcq, f, bfg)
    dproj = _fill_dproj(dproj_u, dq, dk, dv)
    tk = _row_tile(lp, 1100)
    dw_out = _mm_tn(merged, dh1, "grad_w_out", lp, 512, D_MODEL, tk)
    dw_up_p = _mm_tn(y_pool, dap, "grad_w_up_pool", lp, 512, D_MODEL, lp, chunks=N_CHIPS)
    dw_up_a = _mm_tn(y_attn, daa, "grad_w_up_attn", lp, 512, D_MODEL, lp, chunks=N_CHIPS)
    dwt_f = _mm_tn(df, hn, "grad_w_in_forget", lp, F_COLS, D_MODEL, tk)

    def pad8(a):
        return jnp.pad(a, ((0, (-a.shape[0]) % 8), (0, 0)))

    small_parts = [pad8(a) for a in (dgf.reshape(-1, LANES), dscale.reshape(-1, LANES), db, dpw.reshape(-1, LANES),
                                     loss_part)]
    small = jnp.concatenate(small_parts, axis=0)
    soffs = [0]
    for p in small_parts:
        soffs.append(soffs[-1] + p.shape[0])

    gs_rows = [dw_up_p, dw_up_a, dw_out.reshape(N_CHIPS, sh_d, D_MODEL)]
    half = MAIN_COLS // 2
    dwt_a, (*rb_rows, rb_f, sr) = _mm_tn(dproj, hn, "grad_w_in_a", lp, half // 4, D_MODEL, lp, m=half, m_off=0,
                                         swap=(gs_rows + [dwt_f, small], [1, 1, 1, 1, None]))
    *hs_rows, sc = _add_sibling_half(gs_rows, rb_rows, [1, 1, 1], small, sr, core)
    dwt_b, (rb_a,), (qs_rows, sq) = _mm_tn(dproj, hn, "grad_w_in_b", lp, half // 4, D_MODEL, lp, m=half, m_off=4,
                                           swap=([dwt_a], [1]), scatter=(hs_rows, sc))
    h_in = _add_halves_w_in([dwt_a, dwt_b, dwt_f], [rb_a, rb_f], core)
    grad_axes = [2, 1, 1, 1]
    (grad_x, g_block0, dg1), (q_in,), _ = _input_bwd(dproj, df, wt_e, wt_qkv, wt_f, x2, metapad, dh1, norm_g, [h_in])
    late_parts = jnp.concatenate([pad8(dg1.reshape(-1, LANES)), g_block0[PAD_ROWS:].reshape(-1, LANES)], axis=0)
    g_w_in_rows, g_w_up_p, g_w_up_a, g_w_out, st, late = _reduce_allgather(
        [q_in] + list(qs_rows), grad_axes, sq, [True, False, False, False], late_parts)
    n_norm = D_MODEL // LANES
    g_norm = late[:n_norm].reshape(1, D_MODEL)
    chip = 2 * lax.axis_index("x") + lax.axis_index("y")
    g_meta = lax.dynamic_slice_in_dim(late[n_norm:].reshape(N_META, D_MODEL), chip * sh_d, sh_d, axis=1)

    spiece = lambda k, rows: st[soffs[k]:soffs[k] + rows]
    g_final = spiece(0, D_MODEL // LANES).reshape(1, D_MODEL)
    g_scale = spiece(1, POOL_WIDTH // LANES).reshape(1, POOL_WIDTH)
    g_bf = spiece(2, 1)
    g_pw = spiece(3, POOL_WIDTH)
    loss = st[soffs[4], 0]

    def pad_lanes(a):
        return jnp.pad(a, ((0, 0), (0, F_COLS - N_HEADS)))

    w_in_res = (g_w_in_rows,) + _adamw_native(to_rows(w_in), g_w_in_rows, to_rows(m_w_in), to_rows(v_w_in), "adamw_w_in")

    upd = [
        ("meta_tokens", meta_tokens, g_meta, m_meta_tokens, v_meta_tokens),
        ("norm_g", norm_g, g_norm, m_norm_g, v_norm_g),
        ("w_in", None, None, None, None),
        ("b_forget", pad_lanes(b_forget), g_bf, pad_lanes(m_b_forget), pad_lanes(v_b_forget)),
        ("pool_w", pool_w.reshape(-1, LANES), g_pw, m_pool_w.reshape(-1, LANES), v_pool_w.reshape(-1, LANES)),
        ("pool_scale", pool_scale, g_scale, m_pool_scale, v_pool_scale),
        ("w_up_pool", w_up_pool[0], g_w_up_p, m_w_up_pool[0], v_w_up_pool[0]),
        ("w_up_attn", w_up_attn[0], g_w_up_a, m_w_up_attn[0], v_w_up_attn[0]),
        ("w_out", w_out[0], g_w_out, m_w_out[0], v_w_out[0]),
        ("final_norm_g", gf, g_final, m_final_norm_g.reshape(1, D_MODEL), v_final_norm_g.reshape(1, D_MODEL)),
    ]
    shapes = [meta_tokens.shape, norm_g.shape, w_in.shape, b_forget.shape, pool_w.shape, pool_scale.shape,
              w_up_pool.shape, w_up_attn.shape, w_out.shape, final_norm_g.shape]
    grads, deltas, new_ms, new_vs = [], [], [], []
    for (name, w_, g_, m_in, v_in), shp in zip(upd, shapes):
        if name == "w_in":
            res = tuple(from_rows(a) for a in w_in_res)
        else:
            d_, mn_, vn_ = _adamw(w_, g_, m_in, v_in, "adamw_" + name)
            res = (g_, d_, mn_, vn_)
        if name == "b_forget":
            res = tuple(a[:, :N_HEADS] for a in res)
        for lst, a in zip((grads, deltas, new_ms, new_vs), res):
            lst.append(a.reshape(shp))

    return (loss, grad_x.reshape(x.shape), *grads, *deltas, *new_ms, *new_vs)
```

```python
import jax
import jax.numpy as jnp
from jax import lax
from jax.experimental import pallas as pl
from jax.experimental.pallas import tpu as pltpu

F32 = jnp.float32
BF16 = jnp.bfloat16
MESH = pl.DeviceIdType.MESH
HIGHEST = lax.Precision.HIGHEST
HBM = pl.BlockSpec(memory_space=pltpu.HBM)

D_MODEL = 1024
N_META = 16
POOL_WIDTH = 512
POOL_GROUP = 128
POOL_WINDOWS = (2, 4, 8, 16)
N_HEADS = 8
HEAD_DIM = 64
ATTN_WIDTH = 512
RMS_EPS = 1e-6
N_CHIPS = 4

ADAM_LR = 0.001
ADAM_B1 = 0.9
ADAM_B2 = 0.999
ADAM_EPS = 1e-08
ADAM_WD = 0.01
ADAM_STEP = 10

LANES = 128
ROW_TILE = 256
KV_TILE = 1024
KV_TILE_BWD = 1024
PAD_ROWS = ROW_TILE - N_META
NEG = -1e30

E_U, E_ZP, E_ZA, E_GP, E_GA, E_COLS = 0, 512, 1024, 1536, 2560, 3584
QKV_COLS = 3 * ATTN_WIDTH
MAIN_COLS = E_COLS + QKV_COLS
F_COLS = LANES
REF_U, REF_Q, REF_ZA, REF_F, REF_GP = 0, 1024, 2560, 3072, 3080
VMEM_LIMIT = 48 * 1024 * 1024


def _params(sem=None, vmem=VMEM_LIMIT):
    return pltpu.CompilerParams(dimension_semantics=sem, vmem_limit_bytes=vmem)


def _in_hbm(*arrays):
    return [pltpu.with_memory_space_constraint(a, pltpu.HBM) for a in arrays]


def _row_tile(n, target):
    best = 16
    for t in range(16, target + 1, 16):
        if n % t == 0:
            best = t
    return best


def _sigmoid(x):
    return 1.0 / (1.0 + jnp.exp(-x))


def _half(ref, axis, which):
    n = ref.shape[axis] // 2
    idx = [slice(None)] * len(ref.shape)
    idx[axis] = pl.ds(pl.multiple_of(which * n, n), n)
    return ref.at[tuple(idx)]


def _half_shape(shape, axis):
    s = list(shape)
    s[axis] //= 2
    return tuple(s)


def _gather_start(ins, outs, axes, send, recv, fsend, frecv, local):
    x, y, c = lax.axis_index("x"), lax.axis_index("y"), lax.axis_index("c")
    me = 2 * x + y
    for t in range(len(ins)):
        pltpu.make_async_copy(ins[t], outs[t].at[me], local.at[t]).start()
        for k, chip in enumerate([(1 - x, y), (x, 1 - y), (1 - x, 1 - y)]):
            pltpu.make_async_remote_copy(
                src_ref=_half(ins[t], axes[t], c), dst_ref=_half(outs[t].at[me], axes[t], c),
                send_sem=send.at[3 * t + k], recv_sem=recv.at[3 * t + k], device_id=(*chip, c), device_id_type=MESH).start()


def _gather_finish(ins, outs, axes, send, recv, fsend, frecv, local):
    x, y, c = lax.axis_index("x"), lax.axis_index("y"), lax.axis_index("c")
    me = 2 * x + y
    sibling = (x, y, 1 - c)
    chips = [(1 - x, y), (x, 1 - y), (1 - x, 1 - y)]
    n = len(ins)

    def over_ici(t, k, chip, dst_slot):
        return pltpu.make_async_remote_copy(
            src_ref=_half(ins[t], axes[t], c), dst_ref=_half(outs[t].at[dst_slot], axes[t], c),
            send_sem=send.at[3 * t + k], recv_sem=recv.at[3 * t + k], device_id=(*chip, c), device_id_type=MESH)

    def to_sibling(t, k, slot, which):
        return pltpu.make_async_remote_copy(
            src_ref=_half(outs[t].at[slot], axes[t], which), dst_ref=_half(outs[t].at[slot], axes[t], which),
            send_sem=fsend.at[3 * t + k], recv_sem=frecv.at[3 * t + k], device_id=sibling, device_id_type=MESH)

    forwards = []
    for t in range(n):
        for k, (px, py) in enumerate(chips):
            over_ici(t, k, (px, py), 2 * px + py).wait_recv()
            fw = to_sibling(t, k, 2 * px + py, c)
            fw.start()
            forwards.append(fw)
    for t in range(n):
        for k, (px, py) in enumerate(chips):
            to_sibling(t, k, 2 * px + py, 1 - c).wait_recv()
    for t in range(n):
        for k, chip in enumerate(chips):
            over_ici(t, k, chip, me).wait_send()
    for fw in forwards:
        fw.wait_send()
    for t in range(n):
        pltpu.make_async_copy(ins[t], outs[t].at[me], local.at[t]).wait()


def _gather_sems(n):
    return [pltpu.SemaphoreType.DMA((3 * n,)), pltpu.SemaphoreType.DMA((3 * n,)), pltpu.SemaphoreType.DMA((3 * n,)),
            pltpu.SemaphoreType.DMA((3 * n,)), pltpu.SemaphoreType.DMA((n,))]


def _gathered_shapes(shards):
    return [jax.ShapeDtypeStruct((N_CHIPS,) + s.shape, s.dtype) for s in shards]


def _swap_copies(srcs, dsts, axes, send, recv):
    x, y, c = lax.axis_index("x"), lax.axis_index("y"), lax.axis_index("c")
    return [pltpu.make_async_remote_copy(
        src_ref=srcs[t] if axes[t] is None else _half(srcs[t], axes[t], 1 - c), dst_ref=dsts[t],
        send_sem=send.at[t], recv_sem=recv.at[t], device_id=(x, y, 1 - c), device_id_type=MESH) for t in range(len(srcs))]


def _swap_shapes(srcs, axes):
    return [jax.ShapeDtypeStruct(g.shape if a is None else _half_shape(g.shape, a), g.dtype) for g, a in zip(srcs, axes)]


def _add_halves_w_in(parts, rbs, core):
    na, nb_rows, w = rbs[0].shape[0], parts[1].shape[0], rbs[0].shape[1]
    per = (na + nb_rows + N_HEADS) // N_CHIPS
    pieces = [[] for _ in range(N_CHIPS)]
    r = 0
    for src, lo, hi in ((0, 0, na), (1, 0, ATTN_WIDTH), (2, 0, N_HEADS), (1, ATTN_WIDTH, nb_rows)):
        while lo < hi:
            c, at = divmod(r, per)
            n = min(hi - lo, per - at)
            pieces[c].append((src, lo, n, at))
            lo, r = lo + n, r + n

    def body(core_ref, a_ref, b_ref, f_ref, ra_ref, rf_ref, o_ref, am, bm, fm, ar, br, fr, rows_buf, sems, send, recv):
        x, y, core_id = lax.axis_index("x"), lax.axis_index("y"), lax.axis_index("c")
        swap = pltpu.make_async_remote_copy(src_ref=_half(b_ref, 1, 1 - core_id), dst_ref=br, send_sem=send.at[0],
                                            recv_sem=recv.at[0], device_id=(x, y, 1 - core_id), device_id_type=MESH)
        swap.start()
        cols = pl.ds(pl.multiple_of(core_ref[0] * w, w), w)
        few = pl.ds(0, N_HEADS)
        loads = [(pltpu.make_async_copy(a_ref.at[:, cols], am, sems.at[0]), pltpu.make_async_copy(ra_ref, ar, sems.at[1])),
                 (pltpu.make_async_copy(b_ref.at[:, cols], bm, sems.at[2]),),
                 (pltpu.make_async_copy(f_ref.at[few, cols], fm, sems.at[3]),
                  pltpu.make_async_copy(rf_ref.at[few, :], fr, sems.at[4]))]
        for group in loads:
            for cp in group:
                cp.start()
        bufs = [(am, ar), (bm, br), (fm, fr)]
        there = set()
        for c in range(N_CHIPS):
            for src, lo, n, at in pieces[c]:
                if src not in there:
                    for cp in loads[src]:
                        cp.wait()
                    if src == 1:
                        swap.wait_recv()
                    there.add(src)
                mine, other = bufs[src]
                rows_buf[at:at + n, :] = mine[lo:lo + n, :] + other[lo:lo + n, :]
            o_ref[c] = rows_buf[0:per, :].astype(BF16)
        swap.wait_send()

    any_space = pl.BlockSpec(memory_space=pl.ANY)
    a, b, f = parts
    return pl.pallas_call(
        body, name="grad_add_sibling_w_in",
        grid_spec=pltpu.PrefetchScalarGridSpec(
            num_scalar_prefetch=1, grid=(1,),
            in_specs=[any_space, HBM, any_space, any_space, any_space],
            out_specs=pl.BlockSpec((N_CHIPS, per, w), lambda i, cr: (0, 0, 0)),
            scratch_shapes=[pltpu.VMEM((na, w), F32), pltpu.VMEM((nb_rows, w), F32), pltpu.VMEM((N_HEADS, w), F32),
                            pltpu.VMEM((na, w), F32), pltpu.VMEM((nb_rows, w), F32), pltpu.VMEM((N_HEADS, w), F32),
                            pltpu.VMEM((per + (-per) % 8, w), F32), pltpu.SemaphoreType.DMA((5,)),
                            pltpu.SemaphoreType.DMA((1,)), pltpu.SemaphoreType.DMA((1,))]),
        out_shape=jax.ShapeDtypeStruct((N_CHIPS, per, w), BF16),
        compiler_params=_params(("arbitrary",)),
    )(core, a, *_in_hbm(b), f, *rbs)


def _weight_sections(wg):
    per, d = wg.shape[1], wg.shape[2]
    n_ref = N_CHIPS * per
    units = [(0, 0, REF_U, REF_Q), (1, 0, REF_Q, REF_ZA), (0, REF_Q - REF_U, REF_ZA, REF_F), (2, 0, REF_F, REF_GP),
             (0, REF_Q - REF_U + REF_F - REF_ZA, REF_GP, n_ref)]
    plan = []
    for out, first, lo, hi in sorted(units, key=lambda u: u[2]):
        pieces, at = [], 0
        while lo < hi:
            c, src = divmod(lo, per)
            n = min(hi - lo, per - src)
            pieces.append((c, src, n, at))
            lo, at = lo + n, at + n
        plan.append((out, first, at, pieces))
    stage_rows = max(rows for _, _, rows, _ in plan)

    def body(g_ref, e_ref, q_ref, f_ref, gb, stage, sems):
        loads = [pltpu.make_async_copy(g_ref.at[c], gb.at[c], sems.at[c]) for c in range(N_CHIPS)]
        for cp in loads:
            cp.start()
        outs = [e_ref, q_ref, f_ref]
        there = set()
        for out, first, rows, pieces in plan:
            for c, _, _, _ in pieces:
                if c not in there:
                    loads[c].wait()
                    there.add(c)
            if len(pieces) == 1 and pieces[0][1] % 16 == 0 and rows % 16 == 0:
                c, src, _, _ = pieces[0]
                outs[out][first:first + rows, :] = gb[c, src:src + rows, :]
                continue
            for c, src, n, at in pieces:
                stage[at:at + n, :] = gb[c, src:src + n, :].astype(F32)
            padded = rows if rows % 16 == 0 else outs[out].shape[0] - first
            if padded > rows:
                stage[rows:padded, :] = jnp.zeros((padded - rows, d), F32)
            outs[out][first:first + padded, :] = stage[0:padded, :].astype(BF16)

    return pl.pallas_call(
        body, name="weight_sections",
        in_specs=[pl.BlockSpec(memory_space=pl.ANY)],
        out_shape=[jax.ShapeDtypeStruct((E_COLS, d), BF16), jax.ShapeDtypeStruct((QKV_COLS, d), BF16),
                   jax.ShapeDtypeStruct((F_COLS, d), BF16)],
        scratch_shapes=[pltpu.VMEM(wg.shape, BF16), pltpu.VMEM((stage_rows, d), F32), pltpu.SemaphoreType.DMA((N_CHIPS,))],
        compiler_params=_params(),
    )(wg)


def _add_sibling_half(gs, rbs, axes, small, sr, core):
    n = len(gs)

    def body(core_ref, *refs):
        g_refs, rb_refs = refs[:n], refs[n:2 * n]
        s_ref, sr_ref = refs[2 * n], refs[2 * n + 1]
        h_refs, sc_ref = refs[2 * n + 2:3 * n + 2], refs[3 * n + 2]
        for t in range(n):
            h_refs[t][...] = (g_refs[t][...] + rb_refs[t][...]).astype(BF16)
        sc_ref[...] = s_ref[...] + sr_ref[...]

    def mine(rb, axis):
        blk = (None,) + rb.shape[1:]
        if axis == 2:
            return pl.BlockSpec(blk, lambda j, cr: (j, 0, cr[0]))
        return pl.BlockSpec(blk, lambda j, cr: (j, cr[0], 0))

    chunk = lambda rb: pl.BlockSpec((None,) + rb.shape[1:], lambda j, cr: (j, 0, 0))
    whole = pl.BlockSpec(small.shape, lambda j, cr: (0, 0))
    return pl.pallas_call(
        body, name="grad_add_sibling",
        grid_spec=pltpu.PrefetchScalarGridSpec(
            num_scalar_prefetch=1, grid=(N_CHIPS,),
            in_specs=[mine(rb, a) for rb, a in zip(rbs, axes)] + [chunk(rb) for rb in rbs] + [whole, whole],
            out_specs=[chunk(rb) for rb in rbs] + [whole]),
        out_shape=[jax.ShapeDtypeStruct(rb.shape, BF16) for rb in rbs] + [jax.ShapeDtypeStruct(small.shape, F32)],
        compiler_params=_params(("arbitrary",)),
    )(core, *gs, *rbs, small, sr)


def _chip_scatter_plan(h_refs, s_ref, q_refs, sq_ref, send, recv, local):
    n = len(h_refs)
    nt = n + (s_ref is not None)
    x, y, c = lax.axis_index("x"), lax.axis_index("y"), lax.axis_index("c")
    me = 2 * x + y
    chips = [(1 - x, y), (x, 1 - y), (1 - x, 1 - y)]

    def copy(t, k, chip, src_slot, dst_slot):
        src = h_refs[t].at[src_slot] if t < n else s_ref
        dst = (q_refs[t] if t < n else sq_ref).at[dst_slot]
        return pltpu.make_async_remote_copy(src_ref=src, dst_ref=dst, send_sem=send.at[3 * t + k],
                                            recv_sem=recv.at[3 * t + k], device_id=(*chip, c), device_id_type=MESH)

    own = [pltpu.make_async_copy(h_refs[t].at[me], q_refs[t].at[me], local.at[t]) for t in range(n)]
    if s_ref is not None:
        own.append(pltpu.make_async_copy(s_ref, sq_ref.at[me], local.at[n]))
    sends = [copy(t, k, (px, py), 2 * px + py, me) for t in range(nt) for k, (px, py) in enumerate(chips)]
    recvs = [copy(t, k, (px, py), me, 2 * px + py) for t in range(nt) for k, (px, py) in enumerate(chips)]
    return own, sends, recvs


def _scatter_operands(hs, sc):
    small = [] if sc is None else [sc]
    nt = len(hs) + len(small)
    shapes = [jax.ShapeDtypeStruct(h.shape, h.dtype) for h in hs] + [
        jax.ShapeDtypeStruct((N_CHIPS,) + a.shape, a.dtype) for a in small]
    sems = [pltpu.SemaphoreType.DMA((3 * nt,)), pltpu.SemaphoreType.DMA((3 * nt,)), pltpu.SemaphoreType.DMA((nt,))]
    return list(hs) + small, shapes, sems


def _reduce_allgather(qs, axes, sq, as_rows, late):
    n = len(qs)
    shard_shapes, half_axes = [], []
    for q, a, rows_form in zip(qs, axes, as_rows):
        shape = [d * 2 if i == a - 1 else d for i, d in enumerate(q.shape[1:])]
        if rows_form:
            assert a == 2
            shape = [shape[0], 1, shape[1]]
        shard_shapes.append(tuple(shape))
        half_axes.append(2 if rows_form else a - 1)

    def body(*refs):
        q_refs, sq_ref, late_ref = refs[:n], refs[n], refs[n + 1]
        o_refs, st_ref, late_sum_ref = refs[n + 2:2 * n + 2], refs[2 * n + 2], refs[2 * n + 3]
        sib_buf, chip_buf, send, recv, late_send, late_recv = refs[2 * n + 4:]
        x, y, c = lax.axis_index("x"), lax.axis_index("y"), lax.axis_index("c")
        me = 2 * x + y
        chips = [(1 - x, y), (x, 1 - y), (1 - x, 1 - y)]

        def swap(t, which):
            return pltpu.make_async_remote_copy(
                src_ref=_half(o_refs[t], half_axes[t], which), dst_ref=_half(o_refs[t], half_axes[t], which),
                send_sem=send.at[t], recv_sem=recv.at[t], device_id=(x, y, 1 - c), device_id_type=MESH)

        def late_copy(k, chip, slot):
            return pltpu.make_async_remote_copy(src_ref=chip_buf.at[slot], dst_ref=chip_buf.at[slot], send_sem=late_send.at[1 + k],
                                                recv_sem=late_recv.at[1 + k], device_id=(*chip, c), device_id_type=MESH)

        late_swap = pltpu.make_async_remote_copy(src_ref=late_ref, dst_ref=sib_buf, send_sem=late_send.at[0],
                                                 recv_sem=late_recv.at[0], device_id=(x, y, 1 - c), device_id_type=MESH)
        late_swap.start()
        sent = []
        for t in range(n):
            q = q_refs[t]
            total = ((q[0].astype(F32) + q[1].astype(F32)) + q[2].astype(F32)) + q[3].astype(F32)
            if as_rows[t]:
                total = total.reshape(total.shape[0], 1, total.shape[1])
            _half(o_refs[t], half_axes[t], c)[...] = total
            cp = swap(t, c)
            cp.start()
            sent.append(cp)
            if t == 0:
                late_swap.wait()
                chip_buf[me] = late_ref[...] + sib_buf[...]
                late_sends = [late_copy(k, chip, me) for k, chip in enumerate(chips)]
                for lc in late_sends:
                    lc.start()
        st_ref[...] = ((sq_ref[0] + sq_ref[1]) + sq_ref[2]) + sq_ref[3]
        for t in range(n):
            swap(t, 1 - c).wait_recv()
        for cp in sent:
            cp.wait_send()
        for k, (px, py) in enumerate(chips):
            late_copy(k, (px, py), 2 * px + py).wait_recv()
        for lc in late_sends:
            lc.wait_send()
        late_sum_ref[...] = ((chip_buf[0] + chip_buf[1]) + chip_buf[2]) + chip_buf[3]

    vmem = pl.BlockSpec(memory_space=pltpu.VMEM)
    return pl.pallas_call(
        body, name="grad_reduce_allgather",
        in_specs=[vmem] * (n + 2), out_specs=[vmem] * (n + 2),
        out_shape=[jax.ShapeDtypeStruct(s, F32) for s in shard_shapes] + [jax.ShapeDtypeStruct(sq.shape[1:], F32),
                                                                         jax.ShapeDtypeStruct(late.shape, F32)],
        scratch_shapes=[pltpu.VMEM(late.shape, F32), pltpu.VMEM((N_CHIPS,) + late.shape, F32),
                        pltpu.SemaphoreType.DMA((n,)), pltpu.SemaphoreType.DMA((n,)),
                        pltpu.SemaphoreType.DMA((4,)), pltpu.SemaphoreType.DMA((4,))],
        compiler_params=_params(),
    )(*qs, sq, late)


def _dot_nt(a, b):
    return lax.dot_general(a, b, (((1,), (1,)), ((), ())), preferred_element_type=F32)


def _dot_tn(a, b):
    return lax.dot_general(a, b, (((0,), (0,)), ((), ())), preferred_element_type=F32)


def _mm_nt(a, bt, out_dtype, name, m, n, tm, tn, row_block=0, scale_first=None, gather=None):
    k = a.shape[1]
    shards, axes = gather if gather is not None else ([], [])
    ng = len(shards)
    grid = (m // tm, n // tn)

    def body(a_ref, b_ref, *rest):
        ins, o_ref, outs, sems = rest[:ng], rest[ng], rest[ng + 1:2 * ng + 1], rest[2 * ng + 1:]
        first = (pl.program_id(0) == 0) & (pl.program_id(1) == 0)
        last = (pl.program_id(0) == grid[0] - 1) & (pl.program_id(1) == grid[1] - 1)
        if ng:
            @pl.when(first)
            def _():
                _gather_start(ins, outs, axes, *sems)

        r = _dot_nt(a_ref[...], b_ref[...])
        if scale_first is not None:
            r = r * jnp.where(pl.program_id(1) == 0, scale_first, 1.0)
        o_ref[...] = r.astype(out_dtype)

        if ng:
            @pl.when(last)
            def _():
                _gather_finish(ins, outs, axes, *sems)

    res = pl.pallas_call(
        body, name=name, grid=grid,
        in_specs=[pl.BlockSpec((tm, k), lambda i, j: (i, 0)), pl.BlockSpec((tn, k), lambda i, j: (row_block + j, 0))]
        + [HBM] * ng,
        out_specs=[pl.BlockSpec((tm, tn), lambda i, j: (i, j))] + [HBM] * ng,
        out_shape=[jax.ShapeDtypeStruct((m, n), out_dtype)] + _gathered_shapes(shards),
        scratch_shapes=_gather_sems(ng) if ng else [],
        compiler_params=_params(("arbitrary", "arbitrary") if ng else ("parallel", "parallel")),
    )(a, bt, *_in_hbm(*shards))
    return (res[0], res[1:]) if ng else res[0]


def _mm_tn(a, b, name, k, tm, tn, tk, chunks=1, m=None, m_off=0, swap=None, scatter=None):
    m = a.shape[1] if m is None else m
    n = b.shape[1]
    cw = n // chunks
    srcs, axes = swap if swap is not None else ([], [])
    ns = len(srcs)
    hs, sc = scatter if scatter is not None else ([], None)
    sc_in, sc_shapes, sc_sems = _scatter_operands(hs, sc) if scatter is not None else ([], [], [])
    nh, nq = len(hs), len(sc_in)
    grid = (m // tm, n // tn, k // tk)

    def body(a_ref, b_ref, *rest):
        s_refs, h_refs = rest[:ns], rest[ns:ns + nq]
        o_ref = rest[ns + nq]
        d_refs, q_refs = rest[ns + nq + 1:2 * ns + nq + 1], rest[2 * ns + nq + 1:2 * ns + 2 * nq + 1]
        sems = rest[2 * ns + 2 * nq + 1:]
        swap_sems, scatter_sems = (sems[:2], sems[2:]) if ns else ((), sems)
        ids = [pl.program_id(d) for d in range(3)]

        def scatter_plan():
            small_in = h_refs[nh] if sc is not None else None
            small_out = q_refs[nh] if sc is not None else None
            return _chip_scatter_plan(h_refs[:nh], small_in, q_refs[:nh], small_out, *scatter_sems)

        if ns or nq:
            @pl.when((ids[0] == 0) & (ids[1] == 0) & (ids[2] == 0))
            def _():
                if ns:
                    for cp in _swap_copies(s_refs, d_refs, axes, *swap_sems):
                        cp.start()
                if nq:
                    own, sends, _ = scatter_plan()
                    for cp in own + sends:
                        cp.start()

        @pl.when(ids[2] == 0)
        def _():
            o_ref[...] = jnp.zeros_like(o_ref)
        r = _dot_tn(a_ref[...].astype(BF16), b_ref[...].astype(BF16))
        if chunks > 1:
            for c in range(chunks):
                o_ref[c] += r[:, c * cw:(c + 1) * cw]
        else:
            o_ref[...] += r

        if ns or nq:
            @pl.when((ids[0] == grid[0] - 1) & (ids[1] == grid[1] - 1) & (ids[2] == grid[2] - 1))
            def _():
                if ns:
                    for cp in _swap_copies(s_refs, d_refs, axes, *swap_sems):
                        cp.wait()
                if nq:
                    own, sends, recvs = scatter_plan()
                    for cp in recvs:
                        cp.wait_recv()
                    for cp in sends:
                        cp.wait_send()
                    for cp in own:
                        cp.wait()

    if chunks > 1:
        assert tn == n
        out_spec = pl.BlockSpec((chunks, tm, cw), lambda i, j, kk: (0, i, 0))
        out_shape = jax.ShapeDtypeStruct((chunks, m, cw), F32)
    else:
        out_spec = pl.BlockSpec((tm, tn), lambda i, j, kk: (i, j))
        out_shape = jax.ShapeDtypeStruct((m, n), F32)
    riders = ns + nq
    res = pl.pallas_call(
        body, name=name, grid=grid,
        in_specs=[pl.BlockSpec((tk, tm), lambda i, j, kk: (kk, m_off + i)), pl.BlockSpec((tk, tn), lambda i, j, kk: (kk, j))]
        + [HBM] * riders,
        out_specs=[out_spec] + [HBM] * riders, out_shape=[out_shape] + _swap_shapes(srcs, axes) + sc_shapes,
        scratch_shapes=([pltpu.SemaphoreType.DMA((ns,)), pltpu.SemaphoreType.DMA((ns,))] if ns else []) + sc_sems,
        compiler_params=_params(("arbitrary",) * 3 if riders else ("parallel", "parallel", "arbitrary")),
    )(a, b, *_in_hbm(*srcs, *sc_in))
    if not riders:
        return res[0]
    out = [res[0]]
    if ns:
        out.append(res[1:1 + ns])
    if nq:
        out.append((res[1 + ns:1 + ns + nh], res[1 + ns + nh] if sc is not None else None))
    return tuple(out)


def _tokens_spec():
    return pl.BlockSpec((ROW_TILE, D_MODEL), lambda i: (jnp.maximum(i - 1, 0), 0))


def _rmsnorm_fwd(x2, g1, lk, shards, axes):
    nb = x2.shape[0] // ROW_TILE + 1
    nblk = lk // ROW_TILE
    ng = len(shards)
    meta_cols = shards[-1].shape[1]

    def body(x_ref, g_ref, *rest):
        ins, hn_ref, outs = rest[:ng], rest[ng], rest[ng + 1:2 * ng + 1]
        sems, meta_buf, meta_sem = rest[2 * ng + 1:2 * ng + 6], rest[2 * ng + 6], rest[2 * ng + 7]
        s = pl.program_id(0)
        blk = (s + 1) % nblk

        @pl.when(s == 0)
        def _():
            _gather_start(ins, outs, axes, *sems)

        def normed(h):
            r = lax.rsqrt(jnp.mean(h * h, axis=-1, keepdims=True) + RMS_EPS)
            return ((h * r) * g_ref[...]).astype(BF16)

        @pl.when(s < nblk - 1)
        def _():
            hn_ref[...] = normed(jnp.where(blk >= nb, 0.0, x_ref[...]))

        @pl.when(s == nblk - 1)
        def _():
            _gather_finish(ins, outs, axes, *sems)
            fetch = pltpu.make_async_copy(outs[-1], meta_buf, meta_sem.at[0])
            fetch.start()
            fetch.wait()
            meta = jnp.concatenate([meta_buf[j] for j in range(N_CHIPS)], axis=1)
            hn_ref[...] = normed(jnp.concatenate([jnp.zeros((PAD_ROWS, D_MODEL), F32), meta], axis=0))

    res = pl.pallas_call(
        body, name="rmsnorm_fwd", grid=(nblk,),
        in_specs=[pl.BlockSpec((ROW_TILE, D_MODEL), lambda s: (jnp.clip((s + 1) % nblk - 1, 0, nb - 2), 0)),
                  pl.BlockSpec((1, D_MODEL), lambda s: (0, 0))] + [HBM] * ng,
        out_specs=[pl.BlockSpec((ROW_TILE, D_MODEL), lambda s: ((s + 1) % nblk, 0))] + [HBM] * ng,
        out_shape=[jax.ShapeDtypeStruct((lk, D_MODEL), BF16)] + _gathered_shapes(shards),
        scratch_shapes=_gather_sems(ng) + [pltpu.VMEM((N_CHIPS, N_META, meta_cols), F32), pltpu.SemaphoreType.DMA((1,))],
        compiler_params=_params(("arbitrary",)),
    )(x2, g1, *_in_hbm(*shards))
    return res[0], res[1:]


def _valid_gate_mask(i):
    row = i * ROW_TILE + lax.broadcasted_iota(jnp.int32, (ROW_TILE, F_COLS), 0)
    col = lax.broadcasted_iota(jnp.int32, (ROW_TILE, F_COLS), 1)
    return (row >= PAD_ROWS) & (col < N_HEADS)


def _gates_fwd(f, bfg):
    nb = f.shape[0] // ROW_TILE

    def body(f_ref, b_ref, c_ref, carry):
        i = pl.program_id(0)

        @pl.when(i == 0)
        def _():
            carry[...] = jnp.zeros_like(carry)

        logit = f_ref[...] + b_ref[...]
        lf = jnp.minimum(logit, 0.0) - jnp.log1p(jnp.exp(-jnp.abs(logit)))
        lf = jnp.where(_valid_gate_mask(i), lf, 0.0)
        r_i = lax.broadcasted_iota(jnp.int32, (ROW_TILE, ROW_TILE), 0)
        c_i = lax.broadcasted_iota(jnp.int32, (ROW_TILE, ROW_TILE), 1)
        tri = (c_i <= r_i).astype(F32)
        c_ref[...] = jnp.dot(tri, lf, precision=HIGHEST, preferred_element_type=F32) + carry[...]
        carry[...] = carry[...] + jnp.sum(lf, axis=0, keepdims=True)

    return pl.pallas_call(
        body, name="gates_fwd", grid=(nb,),
        in_specs=[pl.BlockSpec((ROW_TILE, F_COLS), lambda i: (i, 0)), pl.BlockSpec((1, F_COLS), lambda i: (0, 0))],
        out_specs=pl.BlockSpec((ROW_TILE, F_COLS), lambda i: (i, 0)),
        out_shape=jax.ShapeDtypeStruct(f.shape, F32),
        scratch_shapes=[pltpu.VMEM((1, F_COLS), F32)],
        compiler_params=_params(("arbitrary",)),
    )(f, bfg)


def _pool_counts(i):
    row = i * ROW_TILE + lax.broadcasted_iota(jnp.int32, (ROW_TILE, 1), 0)
    return jnp.maximum(row - PAD_ROWS, 0)


def _trailing_sums(xc, levels):
    acc = xc
    for lv in range(levels):
        acc = acc + pltpu.roll(acc, 1 << lv, 0)
    return acc


def _leading_sums(xc, levels):
    n = xc.shape[0]
    acc = xc
    for lv in range(levels):
        acc = acc + pltpu.roll(acc, n - (1 << lv), 0)
    return acc


def _pool_p(u_cur, u_prev, i):
    pos = _pool_counts(i)
    ps, invs = [], []
    for g, w in enumerate(POOL_WINDOWS):
        sl = slice(g * POOL_GROUP, (g + 1) * POOL_GROUP)
        cur = u_cur[:, sl]
        xc = jnp.concatenate([u_prev[:, sl], cur], axis=0)
        win = _trailing_sums(xc, g + 1)[ROW_TILE:, :]
        inv = 1.0 / jnp.minimum(pos + 1, w).astype(F32)
        ps.append(win * inv - cur)
        invs.append(inv)
    return ps, invs


def _pool_fwd(e, pw, scale):
    nb = e.shape[0] // ROW_TILE

    def body(uc_ref, up_ref, z_ref, pw_ref, sc_ref, y_ref):
        i = pl.program_id(0)
        u_cur = uc_ref[...].astype(F32)
        u_prev = jnp.where(i == 0, 0.0, up_ref[...].astype(F32))
        ps, _ = _pool_p(u_cur, u_prev, i)
        z = z_ref[...].astype(F32)
        gate = z * _sigmoid(z)
        for g in range(len(POOL_WINDOWS)):
            sl = slice(g * POOL_GROUP, (g + 1) * POOL_GROUP)
            yraw = jnp.dot(ps[g].astype(BF16), pw_ref[g], preferred_element_type=F32)
            y_ref[:, sl] = ((yraw * sc_ref[:, sl]) * gate[:, sl]).astype(BF16)

    blk = (ROW_TILE, POOL_WIDTH)
    return pl.pallas_call(
        body, name="pool_fwd", grid=(nb,),
        in_specs=[pl.BlockSpec(blk, lambda i: (i, 0)), pl.BlockSpec(blk, lambda i: (jnp.maximum(i - 1, 0), 0)),
                  pl.BlockSpec(blk, lambda i: (i, 1)),
                  pl.BlockSpec((len(POOL_WINDOWS), POOL_GROUP, POOL_GROUP), lambda i: (0, 0, 0)),
                  pl.BlockSpec((1, POOL_WIDTH), lambda i: (0, 0))],
        out_specs=pl.BlockSpec(blk, lambda i: (i, 0)),
        out_shape=jax.ShapeDtypeStruct((e.shape[0], POOL_WIDTH), BF16),
        compiler_params=_params(("parallel",)),
    )(e, e, e, pw, scale)


def _stack_heads(a):
    first = lax.broadcasted_iota(jnp.int32, a.shape, 1) < HEAD_DIM
    zero = jnp.zeros_like(a)
    return jnp.concatenate([jnp.where(first, a, zero), jnp.where(first, zero, a)], axis=0)


def _unstack_heads(a):
    rows = a.shape[0] // 2
    first = lax.broadcasted_iota(jnp.int32, (rows, LANES), 1) < HEAD_DIM
    return jnp.where(first, a[:rows], a[rows:])


def _causal(i, jj, stacked, kv_tile):
    r = lax.broadcasted_iota(jnp.int32, (stacked * ROW_TILE, kv_tile), 0)
    if stacked == 2:
        r = jnp.where(r >= ROW_TILE, r - ROW_TILE, r)
    kidx = jj * kv_tile + lax.broadcasted_iota(jnp.int32, (stacked * ROW_TILE, kv_tile), 1)
    return kidx <= i * ROW_TILE + r


def _stack_rows(b):
    n = b.shape[1]
    return jnp.concatenate([jnp.broadcast_to(b[0:1], (ROW_TILE, n)), jnp.broadcast_to(b[1:2], (ROW_TILE, n))], axis=0)


def _attn_fwd(qkv, bias, nb):
    lk = qkv.shape[0]
    lp = nb * ROW_TILE
    nkb = lk // KV_TILE
    n_pairs = N_HEADS // 2

    def body(q_ref, k_ref, v_ref, b_ref, o_ref, lse_ref):
        i = pl.program_id(1)
        qs = _stack_heads(q_ref[...])
        last = (i * ROW_TILE) // KV_TILE

        def block(jj, carry, masked, n_keys=KV_TILE):
            m, l, acc = carry
            rows = pl.ds(pl.multiple_of(jj * KV_TILE, KV_TILE), n_keys)
            s = _dot_nt(qs, k_ref[rows, :]) - _stack_rows(b_ref[0, 0, :, rows])
            if masked:
                s = jnp.where(_causal(i, jj, 2, KV_TILE)[:, :n_keys], s, NEG)
            m_new = jnp.maximum(m, jnp.max(s, axis=1, keepdims=True))
            alpha = jnp.exp(m - m_new)
            p = jnp.exp(s - m_new)
            l = alpha * l + jnp.sum(p, axis=1, keepdims=True)
            acc = alpha * acc + jnp.dot(p.astype(BF16), v_ref[rows, :], preferred_element_type=F32)
            return m_new, l, acc

        init = (jnp.full((2 * ROW_TILE, 1), NEG, F32), jnp.zeros((2 * ROW_TILE, 1), F32),
                jnp.zeros((2 * ROW_TILE, LANES), F32))
        def finish(carry):
            m, l, acc = carry
            o_ref[...] = _unstack_heads(acc / l)
            lse_ref[...] = _unstack_heads(jnp.broadcast_to(m + jnp.log(l), (2 * ROW_TILE, LANES)))

        carry = lax.fori_loop(0, last, lambda jj, c: block(jj, c, False), init)
        ends = [lambda c, n=(r + 1) * ROW_TILE: finish(block(last, c, True, n)) for r in range(KV_TILE // ROW_TILE)]
        lax.switch(i - last * (KV_TILE // ROW_TILE), ends, carry)

    return pl.pallas_call(
        body, name="attn_fwd", grid=(n_pairs, nb),
        in_specs=[pl.BlockSpec((ROW_TILE, LANES), lambda hp, i: (i, hp)),
                  pl.BlockSpec((lk, LANES), lambda hp, i: (0, n_pairs + hp)),
                  pl.BlockSpec((lk, LANES), lambda hp, i: (0, 2 * n_pairs + hp)),
                  pl.BlockSpec((1, 1, 2, lk), lambda hp, i: (i, hp, 0, 0))],
        out_specs=[pl.BlockSpec((ROW_TILE, LANES), lambda hp, i: (i, hp)),
                   pl.BlockSpec((ROW_TILE, LANES), lambda hp, i: (i, hp))],
        out_shape=[jax.ShapeDtypeStruct((lp, ATTN_WIDTH), F32), jax.ShapeDtypeStruct((lp, ATTN_WIDTH), F32)],
        compiler_params=_params(("parallel", "parallel")),
    )(qkv, qkv, qkv, bias)


def _merge_head_fwd_bwd(y_pool, o, e, wup_p, wup_a, w_out, x2, metapad, gf, target):
    lp = o.shape[0]
    nb = lp // ROW_TILE

    def body(yp_ref, o_ref, e_ref, wp_ref, wa_ref, w_ref, x_ref, mp_ref, g_ref, t_ref,
             mg_ref, ya_ref, dh_ref, loss_ref, dg_ref):
        i = pl.program_id(0)

        @pl.when(i == 0)
        def _():
            loss_ref[...] = jnp.zeros_like(loss_ref)
            dg_ref[...] = jnp.zeros_like(dg_ref)

        za = e_ref[:, E_ZA:E_GP].astype(F32)
        ya = (o_ref[...] * (za * _sigmoid(za))).astype(BF16)
        ya_ref[...] = ya
        a_pool = jnp.dot(yp_ref[...], wp_ref[...], preferred_element_type=F32)
        a_attn = jnp.dot(ya, wa_ref[...], preferred_element_type=F32)
        merged = (_sigmoid(e_ref[:, E_GP:E_GA].astype(F32)) * a_pool
                  + _sigmoid(e_ref[:, E_GA:E_COLS].astype(F32)) * a_attn).astype(BF16)
        mg_ref[...] = merged

        h0 = jnp.where(i == 0, mp_ref[...], x_ref[...])
        h1 = h0 + jnp.dot(merged, w_ref[...], preferred_element_type=F32)
        r = lax.rsqrt(jnp.mean(h1 * h1, axis=-1, keepdims=True) + RMS_EPS)
        xhat = h1 * r
        g = g_ref[...]
        err = jnp.where(i == 0, 0.0, xhat * g - t_ref[...])
        loss_ref[...] += 0.5 * jnp.sum(jnp.mean(err * err, axis=-1, keepdims=True))
        dy = err / D_MODEL
        dg_ref[...] += jnp.sum(dy * xhat, axis=0, keepdims=True)
        dxhat = dy * g
        dh_ref[...] = r * (dxhat - xhat * jnp.mean(dxhat * xhat, axis=-1, keepdims=True))

    row = lambda w: pl.BlockSpec((ROW_TILE, w), lambda i: (i, 0))
    const = lambda shape: pl.BlockSpec(shape, lambda i: (0, 0))
    return pl.pallas_call(
        body, name="merge_head_fwd_bwd", grid=(nb,),
        in_specs=[row(POOL_WIDTH), row(ATTN_WIDTH), row(E_COLS), const((POOL_WIDTH, D_MODEL)), const((ATTN_WIDTH, D_MODEL)),
                  const((D_MODEL, D_MODEL)), _tokens_spec(), const((ROW_TILE, D_MODEL)), const((1, D_MODEL)), _tokens_spec()],
        out_specs=[row(D_MODEL), row(ATTN_WIDTH), row(D_MODEL), const((1, LANES)), const((1, D_MODEL))],
        out_shape=[jax.ShapeDtypeStruct((lp, D_MODEL), BF16), jax.ShapeDtypeStruct((lp, ATTN_WIDTH), BF16),
                   jax.ShapeDtypeStruct((lp, D_MODEL), F32), jax.ShapeDtypeStruct((1, LANES), F32),
                   jax.ShapeDtypeStruct((1, D_MODEL), F32)],
        compiler_params=_params(("arbitrary",)),
    )(y_pool, o, e, wup_p, wup_a, w_out, x2, metapad, gf, target)


def _per_head_rowsum(t):
    head = lax.broadcasted_iota(jnp.int32, t.shape, 1) // HEAD_DIM
    out = jnp.zeros_like(t)
    for h in range(N_HEADS):
        sel = head == h
        out = jnp.where(sel, jnp.sum(jnp.where(sel, t, 0.0), axis=1, keepdims=True), out)
    return out


def _merge_bwd(dh1, w_out, y_pool, y_attn, wup_p, wup_a, e, o):
    lp = o.shape[0]
    nb = lp // ROW_TILE

    def body(dh_ref, wo_ref, yp_ref, ya_ref, wp_ref, wa_ref, e_ref, o_ref,
             dap_ref, daa_ref, dg_ref, do_ref, delta_ref, dyp_ref):
        dmerged = _dot_nt(dh_ref[...].astype(BF16), wo_ref[...])
        a_pool = jnp.dot(yp_ref[...], wp_ref[...], preferred_element_type=F32)
        a_attn = jnp.dot(ya_ref[...], wa_ref[...], preferred_element_type=F32)
        sp = _sigmoid(e_ref[:, E_GP:E_GA].astype(F32))
        sa = _sigmoid(e_ref[:, E_GA:E_COLS].astype(F32))
        dap = (dmerged * sp).astype(BF16)
        daa = (dmerged * sa).astype(BF16)
        dap_ref[...] = dap
        daa_ref[...] = daa
        dg_ref[:, ATTN_WIDTH:ATTN_WIDTH + D_MODEL] = (dmerged * a_pool * (sp * (1.0 - sp))).astype(BF16)
        dg_ref[:, ATTN_WIDTH + D_MODEL:] = (dmerged * a_attn * (sa * (1.0 - sa))).astype(BF16)
        dyp_ref[...] = _dot_nt(dap, wp_ref[...])
        dya = _dot_nt(daa, wa_ref[...])
        za = e_ref[:, E_ZA:E_GP].astype(F32)
        sz = _sigmoid(za)
        o = o_ref[...]
        do = dya * (za * sz)
        do_ref[...] = do.astype(BF16)
        dg_ref[:, :ATTN_WIDTH] = (dya * o * (sz * (1.0 + za * (1.0 - sz)))).astype(BF16)
        delta_ref[...] = _per_head_rowsum(do * o)

    row = lambda w: pl.BlockSpec((ROW_TILE, w), lambda i: (i, 0))
    full = lambda a: pl.BlockSpec(a.shape, lambda i: (0, 0))
    return pl.pallas_call(
        body, name="merge_bwd", grid=(nb,),
        in_specs=[row(D_MODEL), full(w_out), row(POOL_WIDTH), row(ATTN_WIDTH), full(wup_p), full(wup_a),
                  row(E_COLS), row(ATTN_WIDTH)],
        out_specs=[row(D_MODEL), row(D_MODEL), pl.BlockSpec((ROW_TILE, MAIN_COLS // 2), lambda i: (i, 1)),
                   row(ATTN_WIDTH), row(ATTN_WIDTH), row(POOL_WIDTH)],
        out_shape=[jax.ShapeDtypeStruct((lp, D_MODEL), BF16), jax.ShapeDtypeStruct((lp, D_MODEL), BF16),
                   jax.ShapeDtypeStruct((lp, MAIN_COLS), BF16),
                   jax.ShapeDtypeStruct((lp, ATTN_WIDTH), BF16), jax.ShapeDtypeStruct((lp, ATTN_WIDTH), F32),
                   jax.ShapeDtypeStruct((lp, POOL_WIDTH), F32)],
        compiler_params=_params(("parallel",)),
    )(dh1, w_out, y_pool, y_attn, wup_p, wup_a, e, o)


def _fill_dproj(dproj, dq, dk, dv):
    lp = dproj.shape[0]
    rows = _row_tile(lp, 1100)

    def body(base_ref, dq_ref, dk_ref, dv_ref, o_ref):
        j = pl.program_id(1)

        @pl.when(j == 0)
        def _():
            o_ref[...] = (dq_ref[...] * HEAD_DIM ** -0.5).astype(BF16)

        @pl.when(j == 1)
        def _():
            o_ref[...] = dk_ref[...]

        @pl.when(j == 2)
        def _():
            o_ref[...] = dv_ref[...]

    blk = pl.BlockSpec((rows, ATTN_WIDTH), lambda i, j: (i, 0))
    return pl.pallas_call(
        body, name="fill_dproj", grid=(lp // rows, 3),
        in_specs=[pl.BlockSpec(memory_space=pl.ANY)] + [blk] * 3,
        out_specs=pl.BlockSpec((rows, ATTN_WIDTH), lambda i, j: (i, 2 + j)),
        out_shape=jax.ShapeDtypeStruct(dproj.shape, BF16),
        input_output_aliases={0: 0},
        compiler_params=_params(("parallel", "arbitrary")),
    )(dproj, dq, dk, dv)


def _pool_bwd(e, dy_pool, pw, scale, dproj):
    lp = e.shape[0]
    nb = lp // ROW_TILE
    ng = len(POOL_WINDOWS)

    def body(uc_ref, up_ref, z_ref, dy_ref, pw_ref, sc_ref, base_ref, o_ref, dsc_ref, dpw_ref,
             cur_ref, nxt_ref, dz_cur_ref, dz_nxt_ref):
        i = pl.program_id(0)

        @pl.when(i == 0)
        def _():
            dsc_ref[...] = jnp.zeros_like(dsc_ref)
            dpw_ref[...] = jnp.zeros_like(dpw_ref)

        @pl.when(i < nb)
        def _():
            u_cur = uc_ref[...].astype(F32)
            u_prev = jnp.where(i == 0, 0.0, up_ref[...].astype(F32))
            ps, invs = _pool_p(u_cur, u_prev, i)
            z = z_ref[...].astype(F32)
            sz = _sigmoid(z)
            dy = dy_ref[...]
            dypre = dy * (z * sz)
            dsilu = sz * (1.0 + z * (1.0 - sz))
            for g in range(ng):
                sl = slice(g * POOL_GROUP, (g + 1) * POOL_GROUP)
                pb = ps[g].astype(BF16)
                w = pw_ref[g]
                yraw = jnp.dot(pb, w, preferred_element_type=F32)
                sc = sc_ref[:, sl]
                dz_nxt_ref[:, sl] = (dy[:, sl] * (yraw * sc) * dsilu[:, sl]).astype(BF16)
                dsc_ref[:, sl] += jnp.sum(dypre[:, sl] * yraw, axis=0, keepdims=True)
                dyraw = (dypre[:, sl] * sc).astype(BF16)
                dpw_ref[g] += _dot_tn(pb, dyraw)
                nxt_ref[:, sl] = _dot_nt(dyraw, w) * invs[g]

        @pl.when(i == nb)
        def _():
            nxt_ref[...] = jnp.zeros_like(nxt_ref)

        @pl.when(i > 0)
        def _():
            cur, nxt = cur_ref[...], nxt_ref[...]
            pos = _pool_counts(i - 1)
            for g, w in enumerate(POOL_WINDOWS):
                sl = slice(g * POOL_GROUP, (g + 1) * POOL_GROUP)
                xc = jnp.concatenate([cur[:, sl], nxt[:, sl]], axis=0)
                win = _leading_sums(xc, g + 1)[:ROW_TILE, :]
                dp = cur[:, sl] * jnp.minimum(pos + 1, w).astype(F32)
                o_ref[:, sl] = (win - dp).astype(BF16)
            o_ref[:, POOL_WIDTH:] = dz_cur_ref[...]

        cur_ref[...] = nxt_ref[...]
        dz_cur_ref[...] = dz_nxt_ref[...]

    blk = (ROW_TILE, POOL_WIDTH)
    at = lambda i: jnp.minimum(i, nb - 1)
    return pl.pallas_call(
        body, name="pool_bwd", grid=(nb + 1,),
        in_specs=[pl.BlockSpec(blk, lambda i: (at(i), 0)), pl.BlockSpec(blk, lambda i: (jnp.maximum(at(i) - 1, 0), 0)),
                  pl.BlockSpec(blk, lambda i: (at(i), 1)), pl.BlockSpec(blk, lambda i: (at(i), 0)),
                  pl.BlockSpec((ng, POOL_GROUP, POOL_GROUP), lambda i: (0, 0, 0)),
                  pl.BlockSpec((1, POOL_WIDTH), lambda i: (0, 0)), pl.BlockSpec(memory_space=pl.ANY)],
        out_specs=[pl.BlockSpec((ROW_TILE, 2 * POOL_WIDTH), lambda i: (jnp.maximum(i - 1, 0), 0)),
                   pl.BlockSpec((1, POOL_WIDTH), lambda i: (0, 0)),
                   pl.BlockSpec((ng, POOL_GROUP, POOL_GROUP), lambda i: (0, 0, 0))],
        out_shape=[jax.ShapeDtypeStruct(dproj.shape, BF16), jax.ShapeDtypeStruct((1, POOL_WIDTH), F32),
                   jax.ShapeDtypeStruct((ng, POOL_GROUP, POOL_GROUP), F32)],
        scratch_shapes=[pltpu.VMEM(blk, F32), pltpu.VMEM(blk, F32), pltpu.VMEM(blk, BF16), pltpu.VMEM(blk, BF16)],
        input_output_aliases={6: 0},
        compiler_params=_params(("arbitrary",)),
    )(e, e, e, dy_pool, pw, scale, dproj)


def _attn_bwd(qkv, do, lse, delta, bias, nb):
    lk = qkv.shape[0]
    lp = nb * ROW_TILE
    nkb = lk // KV_TILE_BWD
    n_pairs = N_HEADS // 2
    per_kv = KV_TILE_BWD // ROW_TILE

    def body(q_ref, k_ref, v_ref, do_ref, lse_ref, dl_ref, b_ref, dq_ref, dk_ref, dv_ref, dc_ref, dcq_ref,
             dk_acc, dv_acc, dc_acc):
        jj = pl.program_id(1)

        @pl.when(jj == 0)
        def _():
            dq_ref[...] = jnp.zeros_like(dq_ref)
            dcq_ref[...] = jnp.zeros_like(dcq_ref)

        dk_acc[...] = jnp.zeros_like(dk_acc)
        dv_acc[...] = jnp.zeros_like(dv_acc)
        dc_acc[...] = jnp.zeros_like(dc_acc)

        def block(i, n_keys, masked):
            kb, vb = k_ref[:n_keys, :], v_ref[:n_keys, :]
            rows = pl.ds(pl.multiple_of(i * ROW_TILE, ROW_TILE), ROW_TILE)
            qs = _stack_heads(q_ref[rows, :])
            dos = _stack_heads(do_ref[rows, :])
            lse_i, dl_i = lse_ref[rows, :], dl_ref[rows, :]
            s = _dot_nt(qs, kb)
            dp = _dot_nt(dos, vb)
            if masked:
                valid = _causal(i, jj, 1, KV_TILE_BWD)[:, :n_keys]
            ps, dss, dcs, rowsums = [], [], [], []
            for hd in range(2):
                half = slice(hd * ROW_TILE, (hd + 1) * ROW_TILE)
                col = slice(hd * HEAD_DIM, hd * HEAD_DIM + 1)
                sh = s[half] - b_ref[i, 0, hd:hd + 1, :n_keys]
                if masked:
                    sh = jnp.where(valid, sh, NEG)
                p = jnp.exp(sh - lse_i[:, col])
                ds = p * (dp[half] - dl_i[:, col])
                ps.append(p.astype(BF16))
                dss.append(ds.astype(BF16))
                dcs.append(jnp.sum(ds, axis=0, keepdims=True))
                rowsums.append(jnp.sum(ds, axis=1, keepdims=True))
            dsb = jnp.concatenate(dss, axis=0)
            dv_acc[:n_keys, :] += _dot_tn(jnp.concatenate(ps, axis=0), dos)
            dk_acc[:n_keys, :] += _dot_tn(dsb, qs)
            dc_acc[:, :n_keys] -= jnp.concatenate(dcs, axis=0)
            dq_ref[rows, :] += _unstack_heads(jnp.dot(dsb, kb, preferred_element_type=F32))
            dcq_ref[rows, :] += _unstack_heads(jnp.broadcast_to(jnp.concatenate(rowsums, axis=0), (2 * ROW_TILE, LANES)))

        first_q = per_kv * jj
        for r in range(per_kv):
            @pl.when(first_q + r < nb)
            def _():
                block(first_q + r, (r + 1) * ROW_TILE, True)

        def rest(i, carry):
            block(i, KV_TILE_BWD, False)
            return carry

        lax.fori_loop(jnp.minimum(first_q + per_kv, nb), nb, rest, 0)
        dk_ref[...] = dk_acc[...].astype(BF16)
        dv_ref[...] = dv_acc[...].astype(BF16)
        dc_ref[0, 0] = dc_acc[...]

    whole = lambda rows: pl.BlockSpec((rows, LANES), lambda hp, jj: (0, hp))
    kv_blk = lambda off: pl.BlockSpec((KV_TILE_BWD, LANES), lambda hp, jj: (jj, off + hp))
    return pl.pallas_call(
        body, name="attn_bwd", grid=(n_pairs, nkb),
        in_specs=[whole(lk), kv_blk(n_pairs), kv_blk(2 * n_pairs), whole(lp), whole(lp), whole(lp),
                  pl.BlockSpec((nb, 1, 2, KV_TILE_BWD), lambda hp, jj: (0, hp, 0, jj))],
        out_specs=[whole(lp), kv_blk(0), kv_blk(0),
                   pl.BlockSpec((1, 1, 2, KV_TILE_BWD), lambda hp, jj: (hp, jj, 0, 0)), whole(lp)],
        out_shape=[jax.ShapeDtypeStruct((lp, ATTN_WIDTH), F32), jax.ShapeDtypeStruct((lk, ATTN_WIDTH), BF16),
                   jax.ShapeDtypeStruct((lk, ATTN_WIDTH), BF16),
                   jax.ShapeDtypeStruct((n_pairs, nkb, 2, KV_TILE_BWD), F32),
                   jax.ShapeDtypeStruct((lp, ATTN_WIDTH), F32)],
        scratch_shapes=[pltpu.VMEM((KV_TILE_BWD, LANES), F32), pltpu.VMEM((KV_TILE_BWD, LANES), F32),
                        pltpu.VMEM((2, KV_TILE_BWD), F32)],
        compiler_params=_params(("parallel", "arbitrary")),
    )(qkv, qkv, qkv, do, lse, delta, bias)


def _gates_bwd(dc, dcq, f, bfg):
    nb = f.shape[0] // ROW_TILE

    def body(dc_ref, dcq_ref, f_ref, b_ref, df_ref, db_ref, carry):
        step = pl.program_id(0)
        i = nb - 1 - step

        @pl.when(step == 0)
        def _():
            carry[...] = jnp.zeros_like(carry)
            db_ref[...] = jnp.zeros_like(db_ref)

        dcb = dc_ref[...]
        lane = lax.broadcasted_iota(jnp.int32, (ROW_TILE, F_COLS), 1)
        for h in range(N_HEADS):
            dcb = dcb + jnp.where(lane == h, dcq_ref[:, HEAD_DIM * h:HEAD_DIM * h + 1], 0.0)
        r_i = lax.broadcasted_iota(jnp.int32, (ROW_TILE, ROW_TILE), 0)
        c_i = lax.broadcasted_iota(jnp.int32, (ROW_TILE, ROW_TILE), 1)
        upper = (c_i >= r_i).astype(F32)
        dlf = jnp.dot(upper, dcb, precision=HIGHEST, preferred_element_type=F32) + carry[...]
        carry[...] = carry[...] + jnp.sum(dcb, axis=0, keepdims=True)
        logit = f_ref[...] + b_ref[...]
        dlogit = jnp.where(_valid_gate_mask(i), dlf * _sigmoid(-logit), 0.0)
        df_ref[...] = dlogit.astype(BF16)
        db_ref[...] += jnp.sum(dlogit, axis=0, keepdims=True)

    blk = pl.BlockSpec((ROW_TILE, F_COLS), lambda s: (nb - 1 - s, 0))
    wide = pl.BlockSpec((ROW_TILE, ATTN_WIDTH), lambda s: (nb - 1 - s, 0))
    one = pl.BlockSpec((1, F_COLS), lambda s: (0, 0))
    return pl.pallas_call(
        body, name="gates_bwd", grid=(nb,),
        in_specs=[blk, wide, blk, one], out_specs=[blk, one],
        out_shape=[jax.ShapeDtypeStruct(f.shape, BF16), jax.ShapeDtypeStruct((1, F_COLS), F32)],
        scratch_shapes=[pltpu.VMEM((1, F_COLS), F32)],
        compiler_params=_params(("arbitrary",)),
    )(dc, dcq, f, bfg)


def _input_bwd(dproj, df, wt_e, wt_qkv, wt_f, x2, metapad, dh1, g1, hs, sc=None):
    nb = x2.shape[0] // ROW_TILE + 1
    n = len(hs)
    sc_in, sc_shapes, sc_sems = _scatter_operands(hs, sc)
    nq = len(sc_in)

    def body(dp_ref, df_ref, we_ref, w_ref, wf_ref, x_ref, mp_ref, dh_ref, g_ref, *rest):
        h_refs, s_ref = rest[:n], (rest[n] if sc is not None else None)
        gx_ref, g0_ref, dg_ref = rest[nq:nq + 3]
        q_refs, sq_ref = rest[nq + 3:nq + 3 + n], (rest[nq + 3 + n] if sc is not None else None)
        sems = rest[2 * nq + 3:]
        i = pl.program_id(0)

        @pl.when(i == 0)
        def _():
            dg_ref[...] = jnp.zeros_like(dg_ref)
            own, sends, _ = _chip_scatter_plan(h_refs, s_ref, q_refs, sq_ref, *sems)
            for cp in own + sends:
                cp.start()

        dhn = (jnp.dot(dp_ref[:, :E_ZA], we_ref[:E_ZA, :], preferred_element_type=F32)
               + jnp.dot(dp_ref[:, E_ZA:MAIN_COLS // 2], w_ref[...], preferred_element_type=F32)
               + jnp.dot(dp_ref[:, MAIN_COLS // 2:], we_ref[E_ZA:, :], preferred_element_type=F32)
               + jnp.dot(df_ref[...], wf_ref[...], preferred_element_type=F32))
        h0 = jnp.where(i == 0, mp_ref[...], x_ref[...])
        r = lax.rsqrt(jnp.mean(h0 * h0, axis=-1, keepdims=True) + RMS_EPS)
        xhat = h0 * r
        dg_ref[...] += jnp.sum(dhn * xhat, axis=0, keepdims=True)
        dxhat = dhn * g_ref[...]
        dh0 = dh_ref[...] + r * (dxhat - xhat * jnp.mean(dxhat * xhat, axis=-1, keepdims=True))
        gx_ref[...] = dh0

        @pl.when(i == 0)
        def _():
            g0_ref[...] = dh0

        @pl.when(i == nb - 1)
        def _():
            own, sends, recvs = _chip_scatter_plan(h_refs, s_ref, q_refs, sq_ref, *sems)
            for cp in recvs:
                cp.wait_recv()
            for cp in sends:
                cp.wait_send()
            for cp in own:
                cp.wait()

    const = lambda shape: pl.BlockSpec(shape, lambda i: (0, 0))
    res = pl.pallas_call(
        body, name="input_bwd", grid=(nb,),
        in_specs=[pl.BlockSpec((ROW_TILE, MAIN_COLS), lambda i: (i, 0)), pl.BlockSpec((ROW_TILE, F_COLS), lambda i: (i, 0)),
                  const((E_COLS, D_MODEL)), const((QKV_COLS, D_MODEL)), const((F_COLS, D_MODEL)),
                  _tokens_spec(), const((ROW_TILE, D_MODEL)),
                  pl.BlockSpec((ROW_TILE, D_MODEL), lambda i: (i, 0)), const((1, D_MODEL))] + [HBM] * nq,
        out_specs=[_tokens_spec(), const((ROW_TILE, D_MODEL)), const((1, D_MODEL))] + [HBM] * nq,
        out_shape=[jax.ShapeDtypeStruct(x2.shape, F32), jax.ShapeDtypeStruct((ROW_TILE, D_MODEL), F32),
                   jax.ShapeDtypeStruct((1, D_MODEL), F32)] + sc_shapes,
        scratch_shapes=sc_sems,
        compiler_params=_params(("arbitrary",), vmem=56 * 1024 * 1024),
    )(dproj, df, wt_e, wt_qkv, wt_f, x2, metapad, dh1, g1, *_in_hbm(*sc_in))
    return res[:3], res[3:3 + n], (res[3 + n] if sc is not None else None)


def _adamw(w, g, m, v, name):
    rows, cols = w.shape
    if rows % 8 == 0:
        tr, tc = _row_tile8(rows), cols
    else:
        tr, tc = rows, (2 * LANES if cols % (2 * LANES) == 0 and rows > 8 else cols)

    def body(w_ref, g_ref, m_ref, v_ref, d_ref, mo_ref, vo_ref):
        g_ = g_ref[...]
        m_new = ADAM_B1 * m_ref[...] + (1.0 - ADAM_B1) * g_
        v_new = ADAM_B2 * v_ref[...] + (1.0 - ADAM_B2) * (g_ * g_)
        m_hat = m_new / (1.0 - ADAM_B1 ** ADAM_STEP)
        v_hat = v_new / (1.0 - ADAM_B2 ** ADAM_STEP)
        d_ref[...] = -ADAM_LR * (m_hat / (jnp.sqrt(v_hat) + ADAM_EPS) + ADAM_WD * w_ref[...])
        mo_ref[...] = m_new
        vo_ref[...] = v_new

    blk = pl.BlockSpec((tr, tc), lambda i, j: (i, j))
    return pl.pallas_call(
        body, name=name, grid=(rows // tr, cols // tc),
        in_specs=[blk] * 4, out_specs=[blk] * 3,
        out_shape=[jax.ShapeDtypeStruct(w.shape, F32)] * 3,
        compiler_params=_params(("parallel", "parallel")),
    )(w, g, m, v)


def _adamw_native(w3, g3, m3, v3, name):
    rows = w3.shape[0]
    tr = rows // 2
    blk = pl.BlockSpec((tr,) + w3.shape[1:], lambda i: (i, 0, 0))
    shape = jax.ShapeDtypeStruct(w3.shape, F32)

    def moments(g_ref, m_ref, v_ref, mo_ref, vo_ref):
        g_ = g_ref[...]
        mo_ref[...] = ADAM_B1 * m_ref[...] + (1.0 - ADAM_B1) * g_
        vo_ref[...] = ADAM_B2 * v_ref[...] + (1.0 - ADAM_B2) * (g_ * g_)

    new_m, new_v = pl.pallas_call(
        moments, name=name + "_moments", grid=(2,), in_specs=[blk] * 3, out_specs=[blk] * 2, out_shape=[shape] * 2,
        compiler_params=_params(("parallel",)),
    )(g3, m3, v3)

    def delta(w_ref, m_ref, v_ref, d_ref):
        m_hat = m_ref[...] / (1.0 - ADAM_B1 ** ADAM_STEP)
        v_hat = v_ref[...] / (1.0 - ADAM_B2 ** ADAM_STEP)
        d_ref[...] = -ADAM_LR * (m_hat / (jnp.sqrt(v_hat) + ADAM_EPS) + ADAM_WD * w_ref[...])

    d = pl.pallas_call(
        delta, name=name + "_delta", grid=(2,), in_specs=[blk] * 3, out_specs=blk, out_shape=shape,
        compiler_params=_params(("parallel",)),
    )(w3, new_m, new_v)
    return d, new_m, new_v


def _row_tile8(rows):
    best = rows
    for t in range(8, 257, 8):
        if rows % t == 0:
            best = t
    return best


def kernel(x, meta_tokens, norm_g, w_in, b_forget, pool_w, pool_scale, w_up_pool, w_up_attn, w_out, final_norm_g, loss_target, m_meta_tokens, m_norm_g, m_w_in, m_b_forget, m_pool_w, m_pool_scale, m_w_up_pool, m_w_up_attn, m_w_out, m_final_norm_g, v_meta_tokens, v_norm_g, v_w_in, v_b_forget, v_pool_w, v_pool_scale, v_w_up_pool, v_w_up_attn, v_w_out, v_final_norm_g):
    seq = x.shape[1]
    assert seq % ROW_TILE == 0 and x.shape[0] == 1
    lp = seq + ROW_TILE
    nb = lp // ROW_TILE
    lk = -(-lp // KV_TILE_BWD) * KV_TILE_BWD
    core = jnp.reshape(lax.axis_index("c"), (1,)).astype(jnp.int32)
    x2 = x[0]
    target = loss_target[0]
    sh_d = D_MODEL // N_CHIPS

    to_rows = lambda a: jnp.transpose(a, (2, 0, 1))
    from_rows = lambda a: jnp.transpose(a, (1, 2, 0))
    gf = final_norm_g.reshape(1, D_MODEL)
    bfg = jnp.pad(b_forget, ((0, 0), (0, F_COLS - N_HEADS)))
    pw_b = pool_w[0].astype(BF16)
    hn, (wg_in, meta_g) = _rmsnorm_fwd(x2, norm_g, lk, [jnp.transpose(w_in[0]).astype(BF16), meta_tokens], [1, 0])
    wt_e, wt_qkv, wt_f = _weight_sections(wg_in)
    meta_full = jnp.transpose(meta_g, (1, 0, 2)).reshape(N_META, D_MODEL)
    metapad = jnp.pad(meta_full, ((PAD_ROWS, 0), (0, 0)))

    tm = _row_tile(lp, 2200)
    e, (wg_up_p, wg_up_a, wg_out) = _mm_nt(
        hn, wt_e, BF16, "in_proj_gates", lp, E_COLS, tm, 512,
        gather=([w_up_pool[0].astype(BF16), w_up_attn[0].astype(BF16), w_out[0].astype(BF16)], [0, 0, 0]))
    wup_p = jnp.transpose(wg_up_p, (1, 0, 2)).reshape(POOL_WIDTH, D_MODEL)
    wup_a = jnp.transpose(wg_up_a, (1, 0, 2)).reshape(ATTN_WIDTH, D_MODEL)
    wout = wg_out.reshape(D_MODEL, D_MODEL)
    qkv = _mm_nt(hn, wt_qkv, BF16, "in_proj_qkv", lk, QKV_COLS, _row_tile(lk, 2600), 512, scale_first=HEAD_DIM ** -0.5)
    f = _mm_nt(hn, wt_f, F32, "in_proj_forget", lp, F_COLS, tm, F_COLS)
    c = _gates_fwd(f, bfg)
    c_t = jnp.transpose(c[:, :N_HEADS])
    c_first = jnp.transpose(c_t[:, ::ROW_TILE]).reshape(nb, N_HEADS // 2, 2, 1)
    c_keys = jnp.pad(c_t, ((0, 0), (0, lk - lp))).reshape(1, N_HEADS // 2, 2, lk)
    bias = jnp.where(jnp.arange(lk) < PAD_ROWS, -NEG, c_keys - c_first)
    y_pool = _pool_fwd(e, pw_b, pool_scale)
    o, lse = _attn_fwd(qkv, bias, nb)
    merged, y_attn, dh1, loss_part, dgf = _merge_head_fwd_bwd(y_pool, o, e, wup_p, wup_a, wout, x2, metapad, gf, target)

    dap, daa, dproj_half, do, delta, dy_pool = _merge_bwd(dh1, wout, y_pool, y_attn, wup_p, wup_a, e, o)
    dproj_u, dscale, dpw = _pool_bwd(e, dy_pool, pw_b, pool_scale, dproj_half)
    dq, dk, dv, dc4, dcq = _attn_bwd(qkv, do, lse, delta, bias, nb)
    dc = jnp.transpose(dc4, (1, 3, 0, 2)).reshape(lk, N_HEADS)[:lp]
    df, db = _gates_bwd(jnp.pad(dc, ((0, 0), (0, F_COLS - N_HEADS))), dcq, f, bfg)
    dproj = _fill_dproj(dproj_u, dq, dk, dv)
    tk = _row_tile(lp, 1100)
    dw_out = _mm_tn(merged, dh1, "grad_w_out", lp, 512, D_MODEL, tk)
    dw_up_p = _mm_tn(y_pool, dap, "grad_w_up_pool", lp, 512, D_MODEL, lp, chunks=N_CHIPS)
    dw_up_a = _mm_tn(y_attn, daa, "grad_w_up_attn", lp, 512, D_MODEL, lp, chunks=N_CHIPS)
    dwt_f = _mm_tn(df, hn, "grad_w_in_forget", lp, F_COLS, D_MODEL, tk)

    def pad8(a):
        return jnp.pad(a, ((0, (-a.shape[0]) % 8), (0, 0)))

    small_parts = [pad8(a) for a in (dgf.reshape(-1, LANES), dscale.reshape(-1, LANES), db, dpw.reshape(-1, LANES),
                                     loss_part)]
    small = jnp.concatenate(small_parts, axis=0)
    soffs = [0]
    for p in small_parts:
        soffs.append(soffs[-1] + p.shape[0])

    gs_rows = [dw_up_p, dw_up_a, dw_out.reshape(N_CHIPS, sh_d, D_MODEL)]
    half = MAIN_COLS // 2
    dwt_a, (*rb_rows, rb_f, sr) = _mm_tn(dproj, hn, "grad_w_in_a", lp, half // 4, D_MODEL, lp, m=half, m_off=0,
                                         swap=(gs_rows + [dwt_f, small], [1, 1, 1, 1, None]))
    *hs_rows, sc = _add_sibling_half(gs_rows, rb_rows, [1, 1, 1], small, sr, core)
    dwt_b, (rb_a,), (qs_rows, sq) = _mm_tn(dproj, hn, "grad_w_in_b", lp, half // 4, D_MODEL, lp, m=half, m_off=4,
                                           swap=([dwt_a], [1]), scatter=(hs_rows, sc))
    h_in = _add_halves_w_in([dwt_a, dwt_b, dwt_f], [rb_a, rb_f], core)
    grad_axes = [2, 1, 1, 1]
    (grad_x, g_block0, dg1), (q_in,), _ = _input_bwd(dproj, df, wt_e, wt_qkv, wt_f, x2, metapad, dh1, norm_g, [h_in])
    late_parts = jnp.concatenate([pad8(dg1.reshape(-1, LANES)), g_block0[PAD_ROWS:].reshape(-1, LANES)], axis=0)
    g_w_in_rows, g_w_up_p, g_w_up_a, g_w_out, st, late = _reduce_allgather(
        [q_in] + list(qs_rows), grad_axes, sq, [True, False, False, False], late_parts)
    n_norm = D_MODEL // LANES
    g_norm = late[:n_norm].reshape(1, D_MODEL)
    chip = 2 * lax.axis_index("x") + lax.axis_index("y")
    g_meta = lax.dynamic_slice_in_dim(late[n_norm:].reshape(N_META, D_MODEL), chip * sh_d, sh_d, axis=1)

    spiece = lambda k, rows: st[soffs[k]:soffs[k] + rows]
    g_final = spiece(0, D_MODEL // LANES).reshape(1, D_MODEL)
    g_scale = spiece(1, POOL_WIDTH // LANES).reshape(1, POOL_WIDTH)
    g_bf = spiece(2, 1)
    g_pw = spiece(3, POOL_WIDTH)
    loss = st[soffs[4], 0]

    def pad_lanes(a):
        return jnp.pad(a, ((0, 0), (0, F_COLS - N_HEADS)))

    w_in_res = (g_w_in_rows,) + _adamw_native(to_rows(w_in), g_w_in_rows, to_rows(m_w_in), to_rows(v_w_in), "adamw_w_in")

    upd = [
        ("meta_tokens", meta_tokens, g_meta, m_meta_tokens, v_meta_tokens),
        ("norm_g", norm_g, g_norm, m_norm_g, v_norm_g),
        ("w_in", None, None, None, None),
        ("b_forget", pad_lanes(b_forget), g_bf, pad_lanes(m_b_forget), pad_lanes(v_b_forget)),
        ("pool_w", pool_w.reshape(-1, LANES), g_pw, m_pool_w.reshape(-1, LANES), v_pool_w.reshape(-1, LANES)),
        ("pool_scale", pool_scale, g_scale, m_pool_scale, v_pool_scale),
        ("w_up_pool", w_up_pool[0], g_w_up_p, m_w_up_pool[0], v_w_up_pool[0]),
        ("w_up_attn", w_up_attn[0], g_w_up_a, m_w_up_attn[0], v_w_up_attn[0]),
        ("w_out", w_out[0], g_w_out, m_w_out[0], v_w_out[0]),
        ("final_norm_g", gf, g_final, m_final_norm_g.reshape(1, D_MODEL), v_final_norm_g.reshape(1, D_MODEL)),
    ]
    shapes = [meta_tokens.shape, norm_g.shape, w_in.shape, b_forget.shape, pool_w.shape, pool_scale.shape,
              w_up_pool.shape, w_up_attn.shape, w_out.shape, final_norm_g.shape]
    grads, deltas, new_ms, new_vs = [], [], [], []
    for (name, w_, g_, m_in, v_in), shp in zip(upd, shapes):
        if name == "w_in":
            res = tuple(from_rows(a) for a in w_in_res)
        else:
            d_, mn_, vn_ = _adamw(w_, g_, m_in, v_in, "adamw_" + name)
            res = (g_, d_, mn_, vn_)
        if name == "b_forget":
            res = tuple(a[:, :N_HEADS] for a in res)
        for lst, a in zip((grads, deltas, new_ms, new_vs), res):
            lst.append(a.reshape(shp))

    return (loss, grad_x.reshape(x.shape), *grads, *deltas, *new_ms, *new_vs)
```

```python
import jax
import jax.numpy as jnp
from jax import lax
from jax.experimental import pallas as pl
from jax.experimental.pallas import tpu as pltpu

F32 = jnp.float32
BF16 = jnp.bfloat16
MESH = pl.DeviceIdType.MESH
HIGHEST = lax.Precision.HIGHEST
HBM = pl.BlockSpec(memory_space=pltpu.HBM)

D_MODEL = 1024
N_META = 16
POOL_WIDTH = 512
POOL_GROUP = 128
POOL_WINDOWS = (2, 4, 8, 16)
N_HEADS = 8
HEAD_DIM = 64
ATTN_WIDTH = 512
RMS_EPS = 1e-6
N_CHIPS = 4

ADAM_LR = 0.001
ADAM_B1 = 0.9
ADAM_B2 = 0.999
ADAM_EPS = 1e-08
ADAM_WD = 0.01
ADAM_STEP = 10

LANES = 128
ROW_TILE = 256
KV_TILE = 1024
KV_TILE_BWD = 1024
PAD_ROWS = ROW_TILE - N_META
NEG = -1e30

E_U, E_ZP, E_ZA, E_GP, E_GA, E_COLS = 0, 512, 1024, 1536, 2560, 3584
QKV_COLS = 3 * ATTN_WIDTH
MAIN_COLS = E_COLS + QKV_COLS
F_COLS = LANES
REF_U, REF_Q, REF_ZA, REF_F, REF_GP = 0, 1024, 2560, 3072, 3080
VMEM_LIMIT = 48 * 1024 * 1024


def _params(sem=None, vmem=VMEM_LIMIT):
    return pltpu.CompilerParams(dimension_semantics=sem, vmem_limit_bytes=vmem)


def _in_hbm(*arrays):
    return [pltpu.with_memory_space_constraint(a, pltpu.HBM) for a in arrays]


def _row_tile(n, target):
    best = 16
    for t in range(16, target + 1, 16):
        if n % t == 0:
            best = t
    return best


def _sigmoid(x):
    return 1.0 / (1.0 + jnp.exp(-x))


def _half(ref, axis, which):
    n = ref.shape[axis] // 2
    idx = [slice(None)] * len(ref.shape)
    idx[axis] = pl.ds(pl.multiple_of(which * n, n), n)
    return ref.at[tuple(idx)]


def _half_shape(shape, axis):
    s = list(shape)
    s[axis] //= 2
    return tuple(s)


def _gather_start(ins, outs, axes, send, recv, fsend, frecv, local):
    x, y, c = lax.axis_index("x"), lax.axis_index("y"), lax.axis_index("c")
    me = 2 * x + y
    for t in range(len(ins)):
        pltpu.make_async_copy(ins[t], outs[t].at[me], local.at[t]).start()
        for k, chip in enumerate([(1 - x, y), (x, 1 - y), (1 - x, 1 - y)]):
            pltpu.make_async_remote_copy(
                src_ref=_half(ins[t], axes[t], c), dst_ref=_half(outs[t].at[me], axes[t], c),
                send_sem=send.at[3 * t + k], recv_sem=recv.at[3 * t + k], device_id=(*chip, c), device_id_type=MESH).start()


def _gather_finish(ins, outs, axes, send, recv, fsend, frecv, local):
    x, y, c = lax.axis_index("x"), lax.axis_index("y"), lax.axis_index("c")
    me = 2 * x + y
    sibling = (x, y, 1 - c)
    chips = [(1 - x, y), (x, 1 - y), (1 - x, 1 - y)]
    n = len(ins)

    def over_ici(t, k, chip, dst_slot):
        return pltpu.make_async_remote_copy(
            src_ref=_half(ins[t], axes[t], c), dst_ref=_half(outs[t].at[dst_slot], axes[t], c),
            send_sem=send.at[3 * t + k], recv_sem=recv.at[3 * t + k], device_id=(*chip, c), device_id_type=MESH)

    def to_sibling(t, k, slot, which):
        return pltpu.make_async_remote_copy(
            src_ref=_half(outs[t].at[slot], axes[t], which), dst_ref=_half(outs[t].at[slot], axes[t], which),
            send_sem=fsend.at[3 * t + k], recv_sem=frecv.at[3 * t + k], device_id=sibling, device_id_type=MESH)

    forwards = []
    for t in range(n):
        for k, (px, py) in enumerate(chips):
            over_ici(t, k, (px, py), 2 * px + py).wait_recv()
            fw = to_sibling(t, k, 2 * px + py, c)
            fw.start()
            forwards.append(fw)
    for t in range(n):
        for k, (px, py) in enumerate(chips):
            to_sibling(t, k, 2 * px + py, 1 - c).wait_recv()
    for t in range(n):
        for k, chip in enumerate(chips):
            over_ici(t, k, chip, me).wait_send()
    for fw in forwards:
        fw.wait_send()
    for t in range(n):
        pltpu.make_async_copy(ins[t], outs[t].at[me], local.at[t]).wait()


def _gather_sems(n):
    return [pltpu.SemaphoreType.DMA((3 * n,)), pltpu.SemaphoreType.DMA((3 * n,)), pltpu.SemaphoreType.DMA((3 * n,)),
            pltpu.SemaphoreType.DMA((3 * n,)), pltpu.SemaphoreType.DMA((n,))]


def _gathered_shapes(shards):
    return [jax.ShapeDtypeStruct((N_CHIPS,) + s.shape, s.dtype) for s in shards]


def _swap_copies(srcs, dsts, axes, send, recv):
    x, y, c = lax.axis_index("x"), lax.axis_index("y"), lax.axis_index("c")
    return [pltpu.make_async_remote_copy(
        src_ref=srcs[t] if axes[t] is None else _half(srcs[t], axes[t], 1 - c), dst_ref=dsts[t],
        send_sem=send.at[t], recv_sem=recv.at[t], device_id=(x, y, 1 - c), device_id_type=MESH) for t in range(len(srcs))]


def _swap_shapes(srcs, axes):
    return [jax.ShapeDtypeStruct(g.shape if a is None else _half_shape(g.shape, a), g.dtype) for g, a in zip(srcs, axes)]


def _add_halves_w_in(parts, rbs, core):
    na, nb_rows, w = rbs[0].shape[0], parts[1].shape[0], rbs[0].shape[1]
    per = (na + nb_rows + N_HEADS) // N_CHIPS
    pieces = [[] for _ in range(N_CHIPS)]
    r = 0
    for src, lo, hi in ((0, 0, na), (1, 0, ATTN_WIDTH), (2, 0, N_HEADS), (1, ATTN_WIDTH, nb_rows)):
        while lo < hi:
            c, at = divmod(r, per)
            n = min(hi - lo, per - at)
            pieces[c].append((src, lo, n, at))
            lo, r = lo + n, r + n

    def body(core_ref, a_ref, b_ref, f_ref, ra_ref, rf_ref, o_ref, am, bm, fm, ar, br, fr, rows_buf, sems, send, recv):
        x, y, core_id = lax.axis_index("x"), lax.axis_index("y"), lax.axis_index("c")
        swap = pltpu.make_async_remote_copy(src_ref=_half(b_ref, 1, 1 - core_id), dst_ref=br, send_sem=send.at[0],
                                            recv_sem=recv.at[0], device_id=(x, y, 1 - core_id), device_id_type=MESH)
        swap.start()
        cols = pl.ds(pl.multiple_of(core_ref[0] * w, w), w)
        few = pl.ds(0, N_HEADS)
        loads = [(pltpu.make_async_copy(a_ref.at[:, cols], am, sems.at[0]), pltpu.make_async_copy(ra_ref, ar, sems.at[1])),
                 (pltpu.make_async_copy(b_ref.at[:, cols], bm, sems.at[2]),),
                 (pltpu.make_async_copy(f_ref.at[few, cols], fm, sems.at[3]),
                  pltpu.make_async_copy(rf_ref.at[few, :], fr, sems.at[4]))]
        for group in loads:
            for cp in group:
                cp.start()
        bufs = [(am, ar), (bm, br), (fm, fr)]
        there = set()
        for c in range(N_CHIPS):
            for src, lo, n, at in pieces[c]:
                if src not in there:
                    for cp in loads[src]:
                        cp.wait()
                    if src == 1:
                        swap.wait_recv()
                    there.add(src)
                mine, other = bufs[src]
                rows_buf[at:at + n, :] = mine[lo:lo + n, :] + other[lo:lo + n, :]
            o_ref[c] = rows_buf[0:per, :].astype(BF16)
        swap.wait_send()

    any_space = pl.BlockSpec(memory_space=pl.ANY)
    a, b, f = parts
    return pl.pallas_call(
        body, name="grad_add_sibling_w_in",
        grid_spec=pltpu.PrefetchScalarGridSpec(
            num_scalar_prefetch=1, grid=(1,),
            in_specs=[any_space, HBM, any_space, any_space, any_space],
            out_specs=pl.BlockSpec((N_CHIPS, per, w), lambda i, cr: (0, 0, 0)),
            scratch_shapes=[pltpu.VMEM((na, w), F32), pltpu.VMEM((nb_rows, w), F32), pltpu.VMEM((N_HEADS, w), F32),
                            pltpu.VMEM((na, w), F32), pltpu.VMEM((nb_rows, w), F32), pltpu.VMEM((N_HEADS, w), F32),
                            pltpu.VMEM((per + (-per) % 8, w), F32), pltpu.SemaphoreType.DMA((5,)),
                            pltpu.SemaphoreType.DMA((1,)), pltpu.SemaphoreType.DMA((1,))]),
        out_shape=jax.ShapeDtypeStruct((N_CHIPS, per, w), BF16),
        compiler_params=_params(("arbitrary",)),
    )(core, a, *_in_hbm(b), f, *rbs)


def _weight_sections(wg):
    per, d = wg.shape[1], wg.shape[2]
    n_ref = N_CHIPS * per
    units = [(0, 0, REF_U, REF_Q), (1, 0, REF_Q, REF_ZA), (0, REF_Q - REF_U, REF_ZA, REF_F), (2, 0, REF_F, REF_GP),
             (0, REF_Q - REF_U + REF_F - REF_ZA, REF_GP, n_ref)]
    plan = []
    for out, first, lo, hi in sorted(units, key=lambda u: u[2]):
        pieces, at = [], 0
        while lo < hi:
            c, src = divmod(lo, per)
            n = min(hi - lo, per - src)
            pieces.append((c, src, n, at))
            lo, at = lo + n, at + n
        plan.append((out, first, at, pieces))
    stage_rows = max(rows for _, _, rows, _ in plan)

    def body(g_ref, e_ref, q_ref, f_ref, gb, stage, sems):
        loads = [pltpu.make_async_copy(g_ref.at[c], gb.at[c], sems.at[c]) for c in range(N_CHIPS)]
        for cp in loads:
            cp.start()
        outs = [e_ref, q_ref, f_ref]
        there = set()
        for out, first, rows, pieces in plan:
            for c, _, _, _ in pieces:
                if c not in there:
                    loads[c].wait()
                    there.add(c)
            if len(pieces) == 1 and pieces[0][1] % 16 == 0 and rows % 16 == 0:
                c, src, _, _ = pieces[0]
                outs[out][first:first + rows, :] = gb[c, src:src + rows, :]
                continue
            for c, src, n, at in pieces:
                stage[at:at + n, :] = gb[c, src:src + n, :].astype(F32)
            padded = rows if rows % 16 == 0 else outs[out].shape[0] - first
            if padded > rows:
                stage[rows:padded, :] = jnp.zeros((padded - rows, d), F32)
            outs[out][first:first + padded, :] = stage[0:padded, :].astype(BF16)

    return pl.pallas_call(
        body, name="weight_sections",
        in_specs=[pl.BlockSpec(memory_space=pl.ANY)],
        out_shape=[jax.ShapeDtypeStruct((E_COLS, d), BF16), jax.ShapeDtypeStruct((QKV_COLS, d), BF16),
                   jax.ShapeDtypeStruct((F_COLS, d), BF16)],
        scratch_shapes=[pltpu.VMEM(wg.shape, BF16), pltpu.VMEM((stage_rows, d), F32), pltpu.SemaphoreType.DMA((N_CHIPS,))],
        compiler_params=_params(),
    )(wg)


def _add_sibling_half(gs, rbs, axes, small, sr, core):
    n = len(gs)

    def body(core_ref, *refs):
        g_refs, rb_refs = refs[:n], refs[n:2 * n]
        s_ref, sr_ref = refs[2 * n], refs[2 * n + 1]
        h_refs, sc_ref = refs[2 * n + 2:3 * n + 2], refs[3 * n + 2]
        for t in range(n):
            h_refs[t][...] = (g_refs[t][...] + rb_refs[t][...]).astype(BF16)
        sc_ref[...] = s_ref[...] + sr_ref[...]

    def mine(rb, axis):
        blk = (None,) + rb.shape[1:]
        if axis == 2:
            return pl.BlockSpec(blk, lambda j, cr: (j, 0, cr[0]))
        return pl.BlockSpec(blk, lambda j, cr: (j, cr[0], 0))

    chunk = lambda rb: pl.BlockSpec((None,) + rb.shape[1:], lambda j, cr: (j, 0, 0))
    whole = pl.BlockSpec(small.shape, lambda j, cr: (0, 0))
    return pl.pallas_call(
        body, name="grad_add_sibling",
        grid_spec=pltpu.PrefetchScalarGridSpec(
            num_scalar_prefetch=1, grid=(N_CHIPS,),
            in_specs=[mine(rb, a) for rb, a in zip(rbs, axes)] + [chunk(rb) for rb in rbs] + [whole, whole],
            out_specs=[chunk(rb) for rb in rbs] + [whole]),
        out_shape=[jax.ShapeDtypeStruct(rb.shape, BF16) for rb in rbs] + [jax.ShapeDtypeStruct(small.shape, F32)],
        compiler_params=_params(("arbitrary",)),
    )(core, *gs, *rbs, small, sr)


def _chip_scatter_plan(h_refs, s_ref, q_refs, sq_ref, send, recv, local):
    n = len(h_refs)
    nt = n + (s_ref is not None)
    x, y, c = lax.axis_index("x"), lax.axis_index("y"), lax.axis_index("c")
    me = 2 * x + y
    chips = [(1 - x, y), (x, 1 - y), (1 - x, 1 - y)]

    def copy(t, k, chip, src_slot, dst_slot):
        src = h_refs[t].at[src_slot] if t < n else s_ref
        dst = (q_refs[t] if t < n else sq_ref).at[dst_slot]
        return pltpu.make_async_remote_copy(src_ref=src, dst_ref=dst, send_sem=send.at[3 * t + k],
                                            recv_sem=recv.at[3 * t + k], device_id=(*chip, c), device_id_type=MESH)

    own = [pltpu.make_async_copy(h_refs[t].at[me], q_refs[t].at[me], local.at[t]) for t in range(n)]
    if s_ref is not None:
        own.append(pltpu.make_async_copy(s_ref, sq_ref.at[me], local.at[n]))
    sends = [copy(t, k, (px, py), 2 * px + py, me) for t in range(nt) for k, (px, py) in enumerate(chips)]
    recvs = [copy(t, k, (px, py), me, 2 * px + py) for t in range(nt) for k, (px, py) in enumerate(chips)]
    return own, sends, recvs


def _scatter_operands(hs, sc):
    small = [] if sc is None else [sc]
    nt = len(hs) + len(small)
    shapes = [jax.ShapeDtypeStruct(h.shape, h.dtype) for h in hs] + [
        jax.ShapeDtypeStruct((N_CHIPS,) + a.shape, a.dtype) for a in small]
    sems = [pltpu.SemaphoreType.DMA((3 * nt,)), pltpu.SemaphoreType.DMA((3 * nt,)), pltpu.SemaphoreType.DMA((nt,))]
    return list(hs) + small, shapes, sems


def _reduce_allgather(qs, axes, sq, as_rows, late):
    n = len(qs)
    shard_shapes, half_axes = [], []
    for q, a, rows_form in zip(qs, axes, as_rows):
        shape = [d * 2 if i == a - 1 else d for i, d in enumerate(q.shape[1:])]
        if rows_form:
            assert a == 2
            shape = [shape[0], 1, shape[1]]
        shard_shapes.append(tuple(shape))
        half_axes.append(2 if rows_form else a - 1)

    def body(*refs):
        q_refs, sq_ref, late_ref = refs[:n], refs[n], refs[n + 1]
        o_refs, st_ref, late_sum_ref = refs[n + 2:2 * n + 2], refs[2 * n + 2], refs[2 * n + 3]
        sib_buf, chip_buf, send, recv, late_send, late_recv = refs[2 * n + 4:]
        x, y, c = lax.axis_index("x"), lax.axis_index("y"), lax.axis_index("c")
        me = 2 * x + y
        chips = [(1 - x, y), (x, 1 - y), (1 - x, 1 - y)]

        def swap(t, which):
            return pltpu.make_async_remote_copy(
                src_ref=_half(o_refs[t], half_axes[t], which), dst_ref=_half(o_refs[t], half_axes[t], which),
                send_sem=send.at[t], recv_sem=recv.at[t], device_id=(x, y, 1 - c), device_id_type=MESH)

        def late_copy(k, chip, slot):
            return pltpu.make_async_remote_copy(src_ref=chip_buf.at[slot], dst_ref=chip_buf.at[slot], send_sem=late_send.at[1 + k],
                                                recv_sem=late_recv.at[1 + k], device_id=(*chip, c), device_id_type=MESH)

        late_swap = pltpu.make_async_remote_copy(src_ref=late_ref, dst_ref=sib_buf, send_sem=late_send.at[0],
                                                 recv_sem=late_recv.at[0], device_id=(x, y, 1 - c), device_id_type=MESH)
        late_swap.start()
        sent = []
        for t in range(n):
            q = q_refs[t]
            total = ((q[0].astype(F32) + q[1].astype(F32)) + q[2].astype(F32)) + q[3].astype(F32)
            if as_rows[t]:
                total = total.reshape(total.shape[0], 1, total.shape[1])
            _half(o_refs[t], half_axes[t], c)[...] = total
            cp = swap(t, c)
            cp.start()
            sent.append(cp)
            if t == 0:
                late_swap.wait()
                chip_buf[me] = late_ref[...] + sib_buf[...]
                late_sends = [late_copy(k, chip, me) for k, chip in enumerate(chips)]
                for lc in late_sends:
                    lc.start()
        st_ref[...] = ((sq_ref[0] + sq_ref[1]) + sq_ref[2]) + sq_ref[3]
        for t in range(n):
            swap(t, 1 - c).wait_recv()
        for cp in sent:
            cp.wait_send()
        for k, (px, py) in enumerate(chips):
            late_copy(k, (px, py), 2 * px + py).wait_recv()
        for lc in late_sends:
            lc.wait_send()
        late_sum_ref[...] = ((chip_buf[0] + chip_buf[1]) + chip_buf[2]) + chip_buf[3]

    vmem = pl.BlockSpec(memory_space=pltpu.VMEM)
    return pl.pallas_call(
        body, name="grad_reduce_allgather",
        in_specs=[vmem] * (n + 2), out_specs=[vmem] * (n + 2),
        out_shape=[jax.ShapeDtypeStruct(s, F32) for s in shard_shapes] + [jax.ShapeDtypeStruct(sq.shape[1:], F32),
                                                                         jax.ShapeDtypeStruct(late.shape, F32)],
        scratch_shapes=[pltpu.VMEM(late.shape, F32), pltpu.VMEM((N_CHIPS,) + late.shape, F32),
                        pltpu.SemaphoreType.DMA((n,)), pltpu.SemaphoreType.DMA((n,)),
                        pltpu.SemaphoreType.DMA((4,)), pltpu.SemaphoreType.DMA((4,))],
        compiler_params=_params(),
    )(*qs, sq, late)


def _dot_nt(a, b):
    return lax.dot_general(a, b, (((1,), (1,)), ((), ())), preferred_element_type=F32)


def _dot_tn(a, b):
    return lax.dot_general(a, b, (((0,), (0,)), ((), ())), preferred_element_type=F32)


def _mm_nt(a, bt, out_dtype, name, m, n, tm, tn, row_block=0, scale_first=None, gather=None):
    k = a.shape[1]
    shards, axes = gather if gather is not None else ([], [])
    ng = len(shards)
    grid = (m // tm, n // tn)

    def body(a_ref, b_ref, *rest):
        ins, o_ref, outs, sems = rest[:ng], rest[ng], rest[ng + 1:2 * ng + 1], rest[2 * ng + 1:]
        first = (pl.program_id(0) == 0) & (pl.program_id(1) == 0)
        last = (pl.program_id(0) == grid[0] - 1) & (pl.program_id(1) == grid[1] - 1)
        if ng:
            @pl.when(first)
            def _():
                _gather_start(ins, outs, axes, *sems)

        r = _dot_nt(a_ref[...], b_ref[...])
        if scale_first is not None:
            r = r * jnp.where(pl.program_id(1) == 0, scale_first, 1.0)
        o_ref[...] = r.astype(out_dtype)

        if ng:
            @pl.when(last)
            def _():
                _gather_finish(ins, outs, axes, *sems)

    res = pl.pallas_call(
        body, name=name, grid=grid,
        in_specs=[pl.BlockSpec((tm, k), lambda i, j: (i, 0)), pl.BlockSpec((tn, k), lambda i, j: (row_block + j, 0))]
        + [HBM] * ng,
        out_specs=[pl.BlockSpec((tm, tn), lambda i, j: (i, j))] + [HBM] * ng,
        out_shape=[jax.ShapeDtypeStruct((m, n), out_dtype)] + _gathered_shapes(shards),
        scratch_shapes=_gather_sems(ng) if ng else [],
        compiler_params=_params(("arbitrary", "arbitrary") if ng else ("parallel", "parallel")),
    )(a, bt, *_in_hbm(*shards))
    return (res[0], res[1:]) if ng else res[0]


def _mm_tn(a, b, name, k, tm, tn, tk, chunks=1, m=None, m_off=0, swap=None, scatter=None):
    m = a.shape[1] if m is None else m
    n = b.shape[1]
    cw = n // chunks
    srcs, axes = swap if swap is not None else ([], [])
    ns = len(srcs)
    hs, sc = scatter if scatter is not None else ([], None)
    sc_in, sc_shapes, sc_sems = _scatter_operands(hs, sc) if scatter is not None else ([], [], [])
    nh, nq = len(hs), len(sc_in)
    grid = (m // tm, n // tn, k // tk)

    def body(a_ref, b_ref, *rest):
        s_refs, h_refs = rest[:ns], rest[ns:ns + nq]
        o_ref = rest[ns + nq]
        d_refs, q_refs = rest[ns + nq + 1:2 * ns + nq + 1], rest[2 * ns + nq + 1:2 * ns + 2 * nq + 1]
        sems = rest[2 * ns + 2 * nq + 1:]
        swap_sems, scatter_sems = (sems[:2], sems[2:]) if ns else ((), sems)
        ids = [pl.program_id(d) for d in range(3)]

        def scatter_plan():
            small_in = h_refs[nh] if sc is not None else None
            small_out = q_refs[nh] if sc is not None else None
            return _chip_scatter_plan(h_refs[:nh], small_in, q_refs[:nh], small_out, *scatter_sems)

        if ns or nq:
            @pl.when((ids[0] == 0) & (ids[1] == 0) & (ids[2] == 0))
            def _():
                if ns:
                    for cp in _swap_copies(s_refs, d_refs, axes, *swap_sems):
                        cp.start()
                if nq:
                    own, sends, _ = scatter_plan()
                    for cp in own + sends:
                        cp.start()

        @pl.when(ids[2] == 0)
        def _():
            o_ref[...] = jnp.zeros_like(o_ref)
        r = _dot_tn(a_ref[...].astype(BF16), b_ref[...].astype(BF16))
        if chunks > 1:
            for c in range(chunks):
                o_ref[c] += r[:, c * cw:(c + 1) * cw]
        else:
            o_ref[...] += r

        if ns or nq:
            @pl.when((ids[0] == grid[0] - 1) & (ids[1] == grid[1] - 1) & (ids[2] == grid[2] - 1))
            def _():
                if ns:
                    for cp in _swap_copies(s_refs, d_refs, axes, *swap_sems):
                        cp.wait()
                if nq:
                    own, sends, recvs = scatter_plan()
                    for cp in recvs:
                        cp.wait_recv()
                    for cp in sends:
                        cp.wait_send()
                    for cp in own:
                        cp.wait()

    if chunks > 1:
        assert tn == n
        out_spec = pl.BlockSpec((chunks, tm, cw), lambda i, j, kk: (0, i, 0))
        out_shape = jax.ShapeDtypeStruct((chunks, m, cw), F32)
    else:
        out_spec = pl.BlockSpec((tm, tn), lambda i, j, kk: (i, j))
        out_shape = jax.ShapeDtypeStruct((m, n), F32)
    riders = ns + nq
    res = pl.pallas_call(
        body, name=name, grid=grid,
        in_specs=[pl.BlockSpec((tk, tm), lambda i, j, kk: (kk, m_off + i)), pl.BlockSpec((tk, tn), lambda i, j, kk: (kk, j))]
        + [HBM] * riders,
        out_specs=[out_spec] + [HBM] * riders, out_shape=[out_shape] + _swap_shapes(srcs, axes) + sc_shapes,
        scratch_shapes=([pltpu.SemaphoreType.DMA((ns,)), pltpu.SemaphoreType.DMA((ns,))] if ns else []) + sc_sems,
        compiler_params=_params(("arbitrary",) * 3 if riders else ("parallel", "parallel", "arbitrary")),
    )(a, b, *_in_hbm(*srcs, *sc_in))
    if not riders:
        return res[0]
    out = [res[0]]
    if ns:
        out.append(res[1:1 + ns])
    if nq:
        out.append((res[1 + ns:1 + ns + nh], res[1 + ns + nh] if sc is not None else None))
    return tuple(out)


def _tokens_spec():
    return pl.BlockSpec((ROW_TILE, D_MODEL), lambda i: (jnp.maximum(i - 1, 0), 0))


def _rmsnorm_fwd(x2, g1, lk, shards, axes):
    nb = x2.shape[0] // ROW_TILE + 1
    nblk = lk // ROW_TILE
    ng = len(shards)
    meta_cols = shards[-1].shape[1]

    def body(x_ref, g_ref, *rest):
        ins, hn_ref, outs = rest[:ng], rest[ng], rest[ng + 1:2 * ng + 1]
        sems, meta_buf, meta_sem = rest[2 * ng + 1:2 * ng + 6], rest[2 * ng + 6], rest[2 * ng + 7]
        s = pl.program_id(0)
        blk = (s + 1) % nblk

        @pl.when(s == 0)
        def _():
            _gather_start(ins, outs, axes, *sems)

        def normed(h):
            r = lax.rsqrt(jnp.mean(h * h, axis=-1, keepdims=True) + RMS_EPS)
            return ((h * r) * g_ref[...]).astype(BF16)

        @pl.when(s < nblk - 1)
        def _():
            hn_ref[...] = normed(jnp.where(blk >= nb, 0.0, x_ref[...]))

        @pl.when(s == nblk - 1)
        def _():
            _gather_finish(ins, outs, axes, *sems)
            fetch = pltpu.make_async_copy(outs[-1], meta_buf, meta_sem.at[0])
            fetch.start()
            fetch.wait()
            meta = jnp.concatenate([meta_buf[j] for j in range(N_CHIPS)], axis=1)
            hn_ref[...] = normed(jnp.concatenate([jnp.zeros((PAD_ROWS, D_MODEL), F32), meta], axis=0))

    res = pl.pallas_call(
        body, name="rmsnorm_fwd", grid=(nblk,),
        in_specs=[pl.BlockSpec((ROW_TILE, D_MODEL), lambda s: (jnp.clip((s + 1) % nblk - 1, 0, nb - 2), 0)),
                  pl.BlockSpec((1, D_MODEL), lambda s: (0, 0))] + [HBM] * ng,
        out_specs=[pl.BlockSpec((ROW_TILE, D_MODEL), lambda s: ((s + 1) % nblk, 0))] + [HBM] * ng,
        out_shape=[jax.ShapeDtypeStruct((lk, D_MODEL), BF16)] + _gathered_shapes(shards),
        scratch_shapes=_gather_sems(ng) + [pltpu.VMEM((N_CHIPS, N_META, meta_cols), F32), pltpu.SemaphoreType.DMA((1,))],
        compiler_params=_params(("arbitrary",)),
    )(x2, g1, *_in_hbm(*shards))
    return res[0], res[1:]


def _valid_gate_mask(i):
    row = i * ROW_TILE + lax.broadcasted_iota(jnp.int32, (ROW_TILE, F_COLS), 0)
    col = lax.broadcasted_iota(jnp.int32, (ROW_TILE, F_COLS), 1)
    return (row >= PAD_ROWS) & (col < N_HEADS)


def _gates_fwd(f, bfg):
    nb = f.shape[0] // ROW_TILE

    def body(f_ref, b_ref, c_ref, carry):
        i = pl.program_id(0)

        @pl.when(i == 0)
        def _():
            carry[...] = jnp.zeros_like(carry)

        logit = f_ref[...] + b_ref[...]
        lf = jnp.minimum(logit, 0.0) - jnp.log1p(jnp.exp(-jnp.abs(logit)))
        lf = jnp.where(_valid_gate_mask(i), lf, 0.0)
        r_i = lax.broadcasted_iota(jnp.int32, (ROW_TILE, ROW_TILE), 0)
        c_i = lax.broadcasted_iota(jnp.int32, (ROW_TILE, ROW_TILE), 1)
        tri = (c_i <= r_i).astype(F32)
        c_ref[...] = jnp.dot(tri, lf, precision=HIGHEST, preferred_element_type=F32) + carry[...]
        carry[...] = carry[...] + jnp.sum(lf, axis=0, keepdims=True)

    return pl.pallas_call(
        body, name="gates_fwd", grid=(nb,),
        in_specs=[pl.BlockSpec((ROW_TILE, F_COLS), lambda i: (i, 0)), pl.BlockSpec((1, F_COLS), lambda i: (0, 0))],
        out_specs=pl.BlockSpec((ROW_TILE, F_COLS), lambda i: (i, 0)),
        out_shape=jax.ShapeDtypeStruct(f.shape, F32),
        scratch_shapes=[pltpu.VMEM((1, F_COLS), F32)],
        compiler_params=_params(("arbitrary",)),
    )(f, bfg)


def _pool_counts(i):
    row = i * ROW_TILE + lax.broadcasted_iota(jnp.int32, (ROW_TILE, 1), 0)
    return jnp.maximum(row - PAD_ROWS, 0)


def _trailing_sums(xc, levels):
    acc = xc
    for lv in range(levels):
        acc = acc + pltpu.roll(acc, 1 << lv, 0)
    return acc


def _leading_sums(xc, levels):
    n = xc.shape[0]
    acc = xc
    for lv in range(levels):
        acc = acc + pltpu.roll(acc, n - (1 << lv), 0)
    return acc


def _pool_p(u_cur, u_prev, i):
    pos = _pool_counts(i)
    ps, invs = [], []
    for g, w in enumerate(POOL_WINDOWS):
        sl = slice(g * POOL_GROUP, (g + 1) * POOL_GROUP)
        cur = u_cur[:, sl]
        xc = jnp.concatenate([u_prev[:, sl], cur], axis=0)
        win = _trailing_sums(xc, g + 1)[ROW_TILE:, :]
        inv = 1.0 / jnp.minimum(pos + 1, w).astype(F32)
        ps.append(win * inv - cur)
        invs.append(inv)
    return ps, invs


def _stack_heads(a):
    first = lax.broadcasted_iota(jnp.int32, a.shape, 1) < HEAD_DIM
    zero = jnp.zeros_like(a)
    return jnp.concatenate([jnp.where(first, a, zero), jnp.where(first, zero, a)], axis=0)


def _unstack_heads(a):
    rows = a.shape[0] // 2
    first = lax.broadcasted_iota(jnp.int32, (rows, LANES), 1) < HEAD_DIM
    return jnp.where(first, a[:rows], a[rows:])


def _causal(i, jj, stacked, kv_tile):
    r = lax.broadcasted_iota(jnp.int32, (stacked * ROW_TILE, kv_tile), 0)
    if stacked == 2:
        r = jnp.where(r >= ROW_TILE, r - ROW_TILE, r)
    kidx = jj * kv_tile + lax.broadcasted_iota(jnp.int32, (stacked * ROW_TILE, kv_tile), 1)
    return kidx <= i * ROW_TILE + r


def _stack_rows(b):
    n = b.shape[1]
    return jnp.concatenate([jnp.broadcast_to(b[0:1], (ROW_TILE, n)), jnp.broadcast_to(b[1:2], (ROW_TILE, n))], axis=0)


def _attn_fwd(qkv, bias, nb):
    lk = qkv.shape[0]
    lp = nb * ROW_TILE
    nkb = lk // KV_TILE
    n_pairs = N_HEADS // 2

    def body(q_ref, k_ref, v_ref, b_ref, o_ref, lse_ref):
        i = pl.program_id(1)
        qs = _stack_heads(q_ref[...])
        last = (i * ROW_TILE) // KV_TILE

        def block(jj, carry, masked, n_keys=KV_TILE):
            m, l, acc = carry
            rows = pl.ds(pl.multiple_of(jj * KV_TILE, KV_TILE), n_keys)
            s = _dot_nt(qs, k_ref[rows, :]) - _stack_rows(b_ref[0, 0, :, rows])
            if masked:
                s = jnp.where(_causal(i, jj, 2, KV_TILE)[:, :n_keys], s, NEG)
            m_new = jnp.maximum(m, jnp.max(s, axis=1, keepdims=True))
            alpha = jnp.exp(m - m_new)
            p = jnp.exp(s - m_new)
            l = alpha * l + jnp.sum(p, axis=1, keepdims=True)
            acc = alpha * acc + jnp.dot(p.astype(BF16), v_ref[rows, :], preferred_element_type=F32)
            return m_new, l, acc

        init = (jnp.full((2 * ROW_TILE, 1), NEG, F32), jnp.zeros((2 * ROW_TILE, 1), F32),
                jnp.zeros((2 * ROW_TILE, LANES), F32))
        def finish(carry):
            m, l, acc = carry
            o_ref[...] = _unstack_heads(acc / l)
            lse_ref[...] = _unstack_heads(jnp.broadcast_to(m + jnp.log(l), (2 * ROW_TILE, LANES)))

        carry = lax.fori_loop(0, last, lambda jj, c: block(jj, c, False), init)
        ends = [lambda c, n=(r + 1) * ROW_TILE: finish(block(last, c, True, n)) for r in range(KV_TILE // ROW_TILE)]
        lax.switch(i - last * (KV_TILE // ROW_TILE), ends, carry)

    return pl.pallas_call(
        body, name="attn_fwd", grid=(n_pairs, nb),
        in_specs=[pl.BlockSpec((ROW_TILE, LANES), lambda hp, i: (i, hp)),
                  pl.BlockSpec((lk, LANES), lambda hp, i: (0, n_pairs + hp)),
                  pl.BlockSpec((lk, LANES), lambda hp, i: (0, 2 * n_pairs + hp)),
                  pl.BlockSpec((1, 1, 2, lk), lambda hp, i: (i, hp, 0, 0))],
        out_specs=[pl.BlockSpec((ROW_TILE, LANES), lambda hp, i: (i, hp)),
                   pl.BlockSpec((ROW_TILE, LANES), lambda hp, i: (i, hp))],
        out_shape=[jax.ShapeDtypeStruct((lp, ATTN_WIDTH), F32), jax.ShapeDtypeStruct((lp, ATTN_WIDTH), F32)],
        compiler_params=_params(("parallel", "parallel")),
    )(qkv, qkv, qkv, bias)


def _merge_head_fwd_bwd(o, e, pw, scale, wup_p, wup_a, w_out, x2, metapad, gf, target):
    lp = o.shape[0]
    nb = lp // ROW_TILE

    def body(up_ref, o_ref, e_ref, pw_ref, sc_ref, wp_ref, wa_ref, w_ref, x_ref, mp_ref, g_ref, t_ref,
             yp_ref, mg_ref, ya_ref, dh_ref, loss_ref, dg_ref):
        i = pl.program_id(0)
        u_cur = e_ref[:, E_U:E_ZP].astype(F32)
        u_prev = jnp.where(i == 0, 0.0, up_ref[...].astype(F32))
        ps, _ = _pool_p(u_cur, u_prev, i)
        zp = e_ref[:, E_ZP:E_ZA].astype(F32)
        gate = zp * _sigmoid(zp)
        for g in range(len(POOL_WINDOWS)):
            sl = slice(g * POOL_GROUP, (g + 1) * POOL_GROUP)
            yraw = jnp.dot(ps[g].astype(BF16), pw_ref[g], preferred_element_type=F32)
            yp_ref[:, sl] = ((yraw * sc_ref[:, sl]) * gate[:, sl]).astype(BF16)

        @pl.when(i == 0)
        def _():
            loss_ref[...] = jnp.zeros_like(loss_ref)
            dg_ref[...] = jnp.zeros_like(dg_ref)

        za = e_ref[:, E_ZA:E_GP].astype(F32)
        ya = (o_ref[...] * (za * _sigmoid(za))).astype(BF16)
        ya_ref[...] = ya
        a_pool = jnp.dot(yp_ref[...], wp_ref[...], preferred_element_type=F32)
        a_attn = jnp.dot(ya, wa_ref[...], preferred_element_type=F32)
        merged = (_sigmoid(e_ref[:, E_GP:E_GA].astype(F32)) * a_pool
                  + _sigmoid(e_ref[:, E_GA:E_COLS].astype(F32)) * a_attn).astype(BF16)
        mg_ref[...] = merged

        h0 = jnp.where(i == 0, mp_ref[...], x_ref[...])
        h1 = h0 + jnp.dot(merged, w_ref[...], preferred_element_type=F32)
        r = lax.rsqrt(jnp.mean(h1 * h1, axis=-1, keepdims=True) + RMS_EPS)
        xhat = h1 * r
        g = g_ref[...]
        err = jnp.where(i == 0, 0.0, xhat * g - t_ref[...])
        loss_ref[...] += 0.5 * jnp.sum(jnp.mean(err * err, axis=-1, keepdims=True))
        dy = err / D_MODEL
        dg_ref[...] += jnp.sum(dy * xhat, axis=0, keepdims=True)
        dxhat = dy * g
        dh_ref[...] = r * (dxhat - xhat * jnp.mean(dxhat * xhat, axis=-1, keepdims=True))

    row = lambda w: pl.BlockSpec((ROW_TILE, w), lambda i: (i, 0))
    const = lambda shape: pl.BlockSpec(shape, lambda i: (0, 0))
    return pl.pallas_call(
        body, name="merge_head_fwd_bwd", grid=(nb,),
        in_specs=[pl.BlockSpec((ROW_TILE, POOL_WIDTH), lambda i: (jnp.maximum(i - 1, 0), 0)), row(ATTN_WIDTH), row(E_COLS),
                  pl.BlockSpec((len(POOL_WINDOWS), POOL_GROUP, POOL_GROUP), lambda i: (0, 0, 0)), const((1, POOL_WIDTH)),
                  const((POOL_WIDTH, D_MODEL)), const((ATTN_WIDTH, D_MODEL)),
                  const((D_MODEL, D_MODEL)), _tokens_spec(), const((ROW_TILE, D_MODEL)), const((1, D_MODEL)), _tokens_spec()],
        out_specs=[row(POOL_WIDTH), row(D_MODEL), row(ATTN_WIDTH), row(D_MODEL), const((1, LANES)), const((1, D_MODEL))],
        out_shape=[jax.ShapeDtypeStruct((lp, POOL_WIDTH), BF16), jax.ShapeDtypeStruct((lp, D_MODEL), BF16),
                   jax.ShapeDtypeStruct((lp, ATTN_WIDTH), BF16),
                   jax.ShapeDtypeStruct((lp, D_MODEL), F32), jax.ShapeDtypeStruct((1, LANES), F32),
                   jax.ShapeDtypeStruct((1, D_MODEL), F32)],
        compiler_params=_params(("arbitrary",)),
    )(e, o, e, pw, scale, wup_p, wup_a, w_out, x2, metapad, gf, target)


def _per_head_rowsum(t):
    head = lax.broadcasted_iota(jnp.int32, t.shape, 1) // HEAD_DIM
    out = jnp.zeros_like(t)
    for h in range(N_HEADS):
        sel = head == h
        out = jnp.where(sel, jnp.sum(jnp.where(sel, t, 0.0), axis=1, keepdims=True), out)
    return out


def _merge_bwd(dh1, w_out, y_pool, y_attn, wup_p, wup_a, e, o):
    lp = o.shape[0]
    nb = lp // ROW_TILE

    def body(dh_ref, wo_ref, yp_ref, ya_ref, wp_ref, wa_ref, e_ref, o_ref,
             dap_ref, daa_ref, dg_ref, do_ref, delta_ref, dyp_ref):
        dmerged = _dot_nt(dh_ref[...].astype(BF16), wo_ref[...])
        a_pool = jnp.dot(yp_ref[...], wp_ref[...], preferred_element_type=F32)
        a_attn = jnp.dot(ya_ref[...], wa_ref[...], preferred_element_type=F32)
        sp = _sigmoid(e_ref[:, E_GP:E_GA].astype(F32))
        sa = _sigmoid(e_ref[:, E_GA:E_COLS].astype(F32))
        dap = (dmerged * sp).astype(BF16)
        daa = (dmerged * sa).astype(BF16)
        dap_ref[...] = dap
        daa_ref[...] = daa
        dg_ref[:, ATTN_WIDTH:ATTN_WIDTH + D_MODEL] = (dmerged * a_pool * (sp * (1.0 - sp))).astype(BF16)
        dg_ref[:, ATTN_WIDTH + D_MODEL:] = (dmerged * a_attn * (sa * (1.0 - sa))).astype(BF16)
        dyp_ref[...] = _dot_nt(dap, wp_ref[...])
        dya = _dot_nt(daa, wa_ref[...])
        za = e_ref[:, E_ZA:E_GP].astype(F32)
        sz = _sigmoid(za)
        o = o_ref[...]
        do = dya * (za * sz)
        do_ref[...] = do.astype(BF16)
        dg_ref[:, :ATTN_WIDTH] = (dya * o * (sz * (1.0 + za * (1.0 - sz)))).astype(BF16)
        delta_ref[...] = _per_head_rowsum(do * o)

    row = lambda w: pl.BlockSpec((ROW_TILE, w), lambda i: (i, 0))
    full = lambda a: pl.BlockSpec(a.shape, lambda i: (0, 0))
    return pl.pallas_call(
        body, name="merge_bwd", grid=(nb,),
        in_specs=[row(D_MODEL), full(w_out), row(POOL_WIDTH), row(ATTN_WIDTH), full(wup_p), full(wup_a),
                  row(E_COLS), row(ATTN_WIDTH)],
        out_specs=[row(D_MODEL), row(D_MODEL), pl.BlockSpec((ROW_TILE, MAIN_COLS // 2), lambda i: (i, 1)),
                   row(ATTN_WIDTH), row(ATTN_WIDTH), row(POOL_WIDTH)],
        out_shape=[jax.ShapeDtypeStruct((lp, D_MODEL), BF16), jax.ShapeDtypeStruct((lp, D_MODEL), BF16),
                   jax.ShapeDtypeStruct((lp, MAIN_COLS), BF16),
                   jax.ShapeDtypeStruct((lp, ATTN_WIDTH), BF16), jax.ShapeDtypeStruct((lp, ATTN_WIDTH), F32),
                   jax.ShapeDtypeStruct((lp, POOL_WIDTH), F32)],
        compiler_params=_params(("parallel",)),
    )(dh1, w_out, y_pool, y_attn, wup_p, wup_a, e, o)


def _fill_dproj(dproj, dq, dk, dv):
    lp = dproj.shape[0]
    rows = _row_tile(lp, 1100)

    def body(base_ref, dq_ref, dk_ref, dv_ref, o_ref):
        j = pl.program_id(1)

        @pl.when(j == 0)
        def _():
            o_ref[...] = (dq_ref[...] * HEAD_DIM ** -0.5).astype(BF16)

        @pl.when(j == 1)
        def _():
            o_ref[...] = dk_ref[...]

        @pl.when(j == 2)
        def _():
            o_ref[...] = dv_ref[...]

    blk = pl.BlockSpec((rows, ATTN_WIDTH), lambda i, j: (i, 0))
    return pl.pallas_call(
        body, name="fill_dproj", grid=(lp // rows, 3),
        in_specs=[pl.BlockSpec(memory_space=pl.ANY)] + [blk] * 3,
        out_specs=pl.BlockSpec((rows, ATTN_WIDTH), lambda i, j: (i, 2 + j)),
        out_shape=jax.ShapeDtypeStruct(dproj.shape, BF16),
        input_output_aliases={0: 0},
        compiler_params=_params(("parallel", "arbitrary")),
    )(dproj, dq, dk, dv)


def _pool_bwd(e, dy_pool, pw, scale, dproj):
    lp = e.shape[0]
    nb = lp // ROW_TILE
    ng = len(POOL_WINDOWS)

    def body(uc_ref, up_ref, z_ref, dy_ref, pw_ref, sc_ref, base_ref, o_ref, dsc_ref, dpw_ref,
             cur_ref, nxt_ref, dz_cur_ref, dz_nxt_ref):
        i = pl.program_id(0)

        @pl.when(i == 0)
        def _():
            dsc_ref[...] = jnp.zeros_like(dsc_ref)
            dpw_ref[...] = jnp.zeros_like(dpw_ref)

        @pl.when(i < nb)
        def _():
            u_cur = uc_ref[...].astype(F32)
            u_prev = jnp.where(i == 0, 0.0, up_ref[...].astype(F32))
            ps, invs = _pool_p(u_cur, u_prev, i)
            z = z_ref[...].astype(F32)
            sz = _sigmoid(z)
            dy = dy_ref[...]
            dypre = dy * (z * sz)
            dsilu = sz * (1.0 + z * (1.0 - sz))
            for g in range(ng):
                sl = slice(g * POOL_GROUP, (g + 1) * POOL_GROUP)
                pb = ps[g].astype(BF16)
                w = pw_ref[g]
                yraw = jnp.dot(pb, w, preferred_element_type=F32)
                sc = sc_ref[:, sl]
                dz_nxt_ref[:, sl] = (dy[:, sl] * (yraw * sc) * dsilu[:, sl]).astype(BF16)
                dsc_ref[:, sl] += jnp.sum(dypre[:, sl] * yraw, axis=0, keepdims=True)
                dyraw = (dypre[:, sl] * sc).astype(BF16)
                dpw_ref[g] += _dot_tn(pb, dyraw)
                nxt_ref[:, sl] = _dot_nt(dyraw, w) * invs[g]

        @pl.when(i == nb)
        def _():
            nxt_ref[...] = jnp.zeros_like(nxt_ref)

        @pl.when(i > 0)
        def _():
            cur, nxt = cur_ref[...], nxt_ref[...]
            pos = _pool_counts(i - 1)
            for g, w in enumerate(POOL_WINDOWS):
                sl = slice(g * POOL_GROUP, (g + 1) * POOL_GROUP)
                xc = jnp.concatenate([cur[:, sl], nxt[:, sl]], axis=0)
                win = _leading_sums(xc, g + 1)[:ROW_TILE, :]
                dp = cur[:, sl] * jnp.minimum(pos + 1, w).astype(F32)
                o_ref[:, sl] = (win - dp).astype(BF16)
            o_ref[:, POOL_WIDTH:] = dz_cur_ref[...]

        cur_ref[...] = nxt_ref[...]
        dz_cur_ref[...] = dz_nxt_ref[...]

    blk = (ROW_TILE, POOL_WIDTH)
    at = lambda i: jnp.minimum(i, nb - 1)
    return pl.pallas_call(
        body, name="pool_bwd", grid=(nb + 1,),
        in_specs=[pl.BlockSpec(blk, lambda i: (at(i), 0)), pl.BlockSpec(blk, lambda i: (jnp.maximum(at(i) - 1, 0), 0)),
                  pl.BlockSpec(blk, lambda i: (at(i), 1)), pl.BlockSpec(blk, lambda i: (at(i), 0)),
                  pl.BlockSpec((ng, POOL_GROUP, POOL_GROUP), lambda i: (0, 0, 0)),
                  pl.BlockSpec((1, POOL_WIDTH), lambda i: (0, 0)), pl.BlockSpec(memory_space=pl.ANY)],
        out_specs=[pl.BlockSpec((ROW_TILE, 2 * POOL_WIDTH), lambda i: (jnp.maximum(i - 1, 0), 0)),
                   pl.BlockSpec((1, POOL_WIDTH), lambda i: (0, 0)),
                   pl.BlockSpec((ng, POOL_GROUP, POOL_GROUP), lambda i: (0, 0, 0))],
        out_shape=[jax.ShapeDtypeStruct(dproj.shape, BF16), jax.ShapeDtypeStruct((1, POOL_WIDTH), F32),
                   jax.ShapeDtypeStruct((ng, POOL_GROUP, POOL_GROUP), F32)],
        scratch_shapes=[pltpu.VMEM(blk, F32), pltpu.VMEM(blk, F32), pltpu.VMEM(blk, BF16), pltpu.VMEM(blk, BF16)],
        input_output_aliases={6: 0},
        compiler_params=_params(("arbitrary",)),
    )(e, e, e, dy_pool, pw, scale, dproj)


def _attn_bwd(qkv, do, lse, delta, bias, nb):
    lk = qkv.shape[0]
    lp = nb * ROW_TILE
    nkb = lk // KV_TILE_BWD
    n_pairs = N_HEADS // 2
    per_kv = KV_TILE_BWD // ROW_TILE

    def body(q_ref, k_ref, v_ref, do_ref, lse_ref, dl_ref, b_ref, dq_ref, dk_ref, dv_ref, dc_ref, dcq_ref,
             dk_acc, dv_acc, dc_acc):
        jj = pl.program_id(1)

        @pl.when(jj == 0)
        def _():
            dq_ref[...] = jnp.zeros_like(dq_ref)
            dcq_ref[...] = jnp.zeros_like(dcq_ref)

        dk_acc[...] = jnp.zeros_like(dk_acc)
        dv_acc[...] = jnp.zeros_like(dv_acc)
        dc_acc[...] = jnp.zeros_like(dc_acc)

        def block(i, n_keys, masked):
            kb, vb = k_ref[:n_keys, :], v_ref[:n_keys, :]
            rows = pl.ds(pl.multiple_of(i * ROW_TILE, ROW_TILE), ROW_TILE)
            qs = _stack_heads(q_ref[rows, :])
            dos = _stack_heads(do_ref[rows, :])
            lse_i, dl_i = lse_ref[rows, :], dl_ref[rows, :]
            s = _dot_nt(qs, kb)
            dp = _dot_nt(dos, vb)
            if masked:
                valid = _causal(i, jj, 1, KV_TILE_BWD)[:, :n_keys]
            ps, dss, dcs, rowsums = [], [], [], []
            for hd in range(2):
                half = slice(hd * ROW_TILE, (hd + 1) * ROW_TILE)
                col = slice(hd * HEAD_DIM, hd * HEAD_DIM + 1)
                sh = s[half] - b_ref[i, 0, hd:hd + 1, :n_keys]
                if masked:
                    sh = jnp.where(valid, sh, NEG)
                p = jnp.exp(sh - lse_i[:, col])
                ds = p * (dp[half] - dl_i[:, col])
                ps.append(p.astype(BF16))
                dss.append(ds.astype(BF16))
                dcs.append(jnp.sum(ds, axis=0, keepdims=True))
                rowsums.append(jnp.sum(ds, axis=1, keepdims=True))
            dsb = jnp.concatenate(dss, axis=0)
            dv_acc[:n_keys, :] += _dot_tn(jnp.concatenate(ps, axis=0), dos)
            dk_acc[:n_keys, :] += _dot_tn(dsb, qs)
            dc_acc[:, :n_keys] -= jnp.concatenate(dcs, axis=0)
            dq_ref[rows, :] += _unstack_heads(jnp.dot(dsb, kb, preferred_element_type=F32))
            dcq_ref[rows, :] += _unstack_heads(jnp.broadcast_to(jnp.concatenate(rowsums, axis=0), (2 * ROW_TILE, LANES)))

        first_q = per_kv * jj
        for r in range(per_kv):
            @pl.when(first_q + r < nb)
            def _():
                block(first_q + r, (r + 1) * ROW_TILE, True)

        def rest(i, carry):
            block(i, KV_TILE_BWD, False)
            return carry

        lax.fori_loop(jnp.minimum(first_q + per_kv, nb), nb, rest, 0)
        dk_ref[...] = dk_acc[...].astype(BF16)
        dv_ref[...] = dv_acc[...].astype(BF16)
        dc_ref[0, 0] = dc_acc[...]

    whole = lambda rows: pl.BlockSpec((rows, LANES), lambda hp, jj: (0, hp))
    kv_blk = lambda off: pl.BlockSpec((KV_TILE_BWD, LANES), lambda hp, jj: (jj, off + hp))
    return pl.pallas_call(
        body, name="attn_bwd", grid=(n_pairs, nkb),
        in_specs=[whole(lk), kv_blk(n_pairs), kv_blk(2 * n_pairs), whole(lp), whole(lp), whole(lp),
                  pl.BlockSpec((nb, 1, 2, KV_TILE_BWD), lambda hp, jj: (0, hp, 0, jj))],
        out_specs=[whole(lp), kv_blk(0), kv_blk(0),
                   pl.BlockSpec((1, 1, 2, KV_TILE_BWD), lambda hp, jj: (hp, jj, 0, 0)), whole(lp)],
        out_shape=[jax.ShapeDtypeStruct((lp, ATTN_WIDTH), F32), jax.ShapeDtypeStruct((lk, ATTN_WIDTH), BF16),
                   jax.ShapeDtypeStruct((lk, ATTN_WIDTH), BF16),
                   jax.ShapeDtypeStruct((n_pairs, nkb, 2, KV_TILE_BWD), F32),
                   jax.ShapeDtypeStruct((lp, ATTN_WIDTH), F32)],
        scratch_shapes=[pltpu.VMEM((KV_TILE_BWD, LANES), F32), pltpu.VMEM((KV_TILE_BWD, LANES), F32),
                        pltpu.VMEM((2, KV_TILE_BWD), F32)],
        compiler_params=_params(("parallel", "arbitrary")),
    )(qkv, qkv, qkv, do, lse, delta, bias)


def _gates_bwd(dc, dcq, f, bfg):
    nb = f.shape[0] // ROW_TILE

    def body(dc_ref, dcq_ref, f_ref, b_ref, df_ref, db_ref, carry):
        step = pl.program_id(0)
        i = nb - 1 - step

        @pl.when(step == 0)
        def _():
            carry[...] = jnp.zeros_like(carry)
            db_ref[...] = jnp.zeros_like(db_ref)

        dcb = dc_ref[...]
        lane = lax.broadcasted_iota(jnp.int32, (ROW_TILE, F_COLS), 1)
        for h in range(N_HEADS):
            dcb = dcb + jnp.where(lane == h, dcq_ref[:, HEAD_DIM * h:HEAD_DIM * h + 1], 0.0)
        r_i = lax.broadcasted_iota(jnp.int32, (ROW_TILE, ROW_TILE), 0)
        c_i = lax.broadcasted_iota(jnp.int32, (ROW_TILE, ROW_TILE), 1)
        upper = (c_i >= r_i).astype(F32)
        dlf = jnp.dot(upper, dcb, precision=HIGHEST, preferred_element_type=F32) + carry[...]
        carry[...] = carry[...] + jnp.sum(dcb, axis=0, keepdims=True)
        logit = f_ref[...] + b_ref[...]
        dlogit = jnp.where(_valid_gate_mask(i), dlf * _sigmoid(-logit), 0.0)
        df_ref[...] = dlogit.astype(BF16)
        db_ref[...] += jnp.sum(dlogit, axis=0, keepdims=True)

    blk = pl.BlockSpec((ROW_TILE, F_COLS), lambda s: (nb - 1 - s, 0))
    wide = pl.BlockSpec((ROW_TILE, ATTN_WIDTH), lambda s: (nb - 1 - s, 0))
    one = pl.BlockSpec((1, F_COLS), lambda s: (0, 0))
    return pl.pallas_call(
        body, name="gates_bwd", grid=(nb,),
        in_specs=[blk, wide, blk, one], out_specs=[blk, one],
        out_shape=[jax.ShapeDtypeStruct(f.shape, BF16), jax.ShapeDtypeStruct((1, F_COLS), F32)],
        scratch_shapes=[pltpu.VMEM((1, F_COLS), F32)],
        compiler_params=_params(("arbitrary",)),
    )(dc, dcq, f, bfg)


def _input_bwd(dproj, df, wt_e, wt_qkv, wt_f, x2, metapad, dh1, g1, hs, sc=None):
    nb = x2.shape[0] // ROW_TILE + 1
    n = len(hs)
    sc_in, sc_shapes, sc_sems = _scatter_operands(hs, sc)
    nq = len(sc_in)

    def body(dp_ref, df_ref, we_ref, w_ref, wf_ref, x_ref, mp_ref, dh_ref, g_ref, *rest):
        h_refs, s_ref = rest[:n], (rest[n] if sc is not None else None)
        gx_ref, g0_ref, dg_ref = rest[nq:nq + 3]
        q_refs, sq_ref = rest[nq + 3:nq + 3 + n], (rest[nq + 3 + n] if sc is not None else None)
        sems = rest[2 * nq + 3:]
        i = pl.program_id(0)

        @pl.when(i == 0)
        def _():
            dg_ref[...] = jnp.zeros_like(dg_ref)
            own, sends, _ = _chip_scatter_plan(h_refs, s_ref, q_refs, sq_ref, *sems)
            for cp in own + sends:
                cp.start()

        dhn = (jnp.dot(dp_ref[:, :E_ZA], we_ref[:E_ZA, :], preferred_element_type=F32)
               + jnp.dot(dp_ref[:, E_ZA:MAIN_COLS // 2], w_ref[...], preferred_element_type=F32)
               + jnp.dot(dp_ref[:, MAIN_COLS // 2:], we_ref[E_ZA:, :], preferred_element_type=F32)
               + jnp.dot(df_ref[...], wf_ref[...], preferred_element_type=F32))
        h0 = jnp.where(i == 0, mp_ref[...], x_ref[...])
        r = lax.rsqrt(jnp.mean(h0 * h0, axis=-1, keepdims=True) + RMS_EPS)
        xhat = h0 * r
        dg_ref[...] += jnp.sum(dhn * xhat, axis=0, keepdims=True)
        dxhat = dhn * g_ref[...]
        dh0 = dh_ref[...] + r * (dxhat - xhat * jnp.mean(dxhat * xhat, axis=-1, keepdims=True))
        gx_ref[...] = dh0

        @pl.when(i == 0)
        def _():
            g0_ref[...] = dh0

        @pl.when(i == nb - 1)
        def _():
            own, sends, recvs = _chip_scatter_plan(h_refs, s_ref, q_refs, sq_ref, *sems)
            for cp in recvs:
                cp.wait_recv()
            for cp in sends:
                cp.wait_send()
            for cp in own:
                cp.wait()

    const = lambda shape: pl.BlockSpec(shape, lambda i: (0, 0))
    res = pl.pallas_call(
        body, name="input_bwd", grid=(nb,),
        in_specs=[pl.BlockSpec((ROW_TILE, MAIN_COLS), lambda i: (i, 0)), pl.BlockSpec((ROW_TILE, F_COLS), lambda i: (i, 0)),
                  const((E_COLS, D_MODEL)), const((QKV_COLS, D_MODEL)), const((F_COLS, D_MODEL)),
                  _tokens_spec(), const((ROW_TILE, D_MODEL)),
                  pl.BlockSpec((ROW_TILE, D_MODEL), lambda i: (i, 0)), const((1, D_MODEL))] + [HBM] * nq,
        out_specs=[_tokens_spec(), const((ROW_TILE, D_MODEL)), const((1, D_MODEL))] + [HBM] * nq,
        out_shape=[jax.ShapeDtypeStruct(x2.shape, F32), jax.ShapeDtypeStruct((ROW_TILE, D_MODEL), F32),
                   jax.ShapeDtypeStruct((1, D_MODEL), F32)] + sc_shapes,
        scratch_shapes=sc_sems,
        compiler_params=_params(("arbitrary",), vmem=56 * 1024 * 1024),
    )(dproj, df, wt_e, wt_qkv, wt_f, x2, metapad, dh1, g1, *_in_hbm(*sc_in))
    return res[:3], res[3:3 + n], (res[3 + n] if sc is not None else None)


def _adamw(w, g, m, v, name):
    rows, cols = w.shape
    if rows % 8 == 0:
        tr, tc = _row_tile8(rows), cols
    else:
        tr, tc = rows, (2 * LANES if cols % (2 * LANES) == 0 and rows > 8 else cols)

    def body(w_ref, g_ref, m_ref, v_ref, d_ref, mo_ref, vo_ref):
        g_ = g_ref[...]
        m_new = ADAM_B1 * m_ref[...] + (1.0 - ADAM_B1) * g_
        v_new = ADAM_B2 * v_ref[...] + (1.0 - ADAM_B2) * (g_ * g_)
        m_hat = m_new / (1.0 - ADAM_B1 ** ADAM_STEP)
        v_hat = v_new / (1.0 - ADAM_B2 ** ADAM_STEP)
        d_ref[...] = -ADAM_LR * (m_hat / (jnp.sqrt(v_hat) + ADAM_EPS) + ADAM_WD * w_ref[...])
        mo_ref[...] = m_new
        vo_ref[...] = v_new

    blk = pl.BlockSpec((tr, tc), lambda i, j: (i, j))
    return pl.pallas_call(
        body, name=name, grid=(rows // tr, cols // tc),
        in_specs=[blk] * 4, out_specs=[blk] * 3,
        out_shape=[jax.ShapeDtypeStruct(w.shape, F32)] * 3,
        compiler_params=_params(("parallel", "parallel")),
    )(w, g, m, v)


def _adamw_native(w3, g3, m3, v3, name):
    rows = w3.shape[0]
    tr = rows // 2
    blk = pl.BlockSpec((tr,) + w3.shape[1:], lambda i: (i, 0, 0))
    shape = jax.ShapeDtypeStruct(w3.shape, F32)

    def moments(g_ref, m_ref, v_ref, mo_ref, vo_ref):
        g_ = g_ref[...]
        mo_ref[...] = ADAM_B1 * m_ref[...] + (1.0 - ADAM_B1) * g_
        vo_ref[...] = ADAM_B2 * v_ref[...] + (1.0 - ADAM_B2) * (g_ * g_)

    new_m, new_v = pl.pallas_call(
        moments, name=name + "_moments", grid=(2,), in_specs=[blk] * 3, out_specs=[blk] * 2, out_shape=[shape] * 2,
        compiler_params=_params(("parallel",)),
    )(g3, m3, v3)

    def delta(w_ref, m_ref, v_ref, d_ref):
        m_hat = m_ref[...] / (1.0 - ADAM_B1 ** ADAM_STEP)
        v_hat = v_ref[...] / (1.0 - ADAM_B2 ** ADAM_STEP)
        d_ref[...] = -ADAM_LR * (m_hat / (jnp.sqrt(v_hat) + ADAM_EPS) + ADAM_WD * w_ref[...])

    d = pl.pallas_call(
        delta, name=name + "_delta", grid=(2,), in_specs=[blk] * 3, out_specs=blk, out_shape=shape,
        compiler_params=_params(("parallel",)),
    )(w3, new_m, new_v)
    return d, new_m, new_v


def _row_tile8(rows):
    best = rows
    for t in range(8, 257, 8):
        if rows % t == 0:
            best = t
    return best


def kernel(x, meta_tokens, norm_g, w_in, b_forget, pool_w, pool_scale, w_up_pool, w_up_attn, w_out, final_norm_g, loss_target, m_meta_tokens, m_norm_g, m_w_in, m_b_forget, m_pool_w, m_pool_scale, m_w_up_pool, m_w_up_attn, m_w_out, m_final_norm_g, v_meta_tokens, v_norm_g, v_w_in, v_b_forget, v_pool_w, v_pool_scale, v_w_up_pool, v_w_up_attn, v_w_out, v_final_norm_g):
    seq = x.shape[1]
    assert seq % ROW_TILE == 0 and x.shape[0] == 1
    lp = seq + ROW_TILE
    nb = lp // ROW_TILE
    lk = -(-lp // KV_TILE_BWD) * KV_TILE_BWD
    core = jnp.reshape(lax.axis_index("c"), (1,)).astype(jnp.int32)
    x2 = x[0]
    target = loss_target[0]
    sh_d = D_MODEL // N_CHIPS

    to_rows = lambda a: jnp.transpose(a, (2, 0, 1))
    from_rows = lambda a: jnp.transpose(a, (1, 2, 0))
    gf = final_norm_g.reshape(1, D_MODEL)
    bfg = jnp.pad(b_forget, ((0, 0), (0, F_COLS - N_HEADS)))
    pw_b = pool_w[0].astype(BF16)
    hn, (wg_in, meta_g) = _rmsnorm_fwd(x2, norm_g, lk, [jnp.transpose(w_in[0]).astype(BF16), meta_tokens], [1, 0])
    wt_e, wt_qkv, wt_f = _weight_sections(wg_in)
    meta_full = jnp.transpose(meta_g, (1, 0, 2)).reshape(N_META, D_MODEL)
    metapad = jnp.pad(meta_full, ((PAD_ROWS, 0), (0, 0)))

    tm = _row_tile(lp, 2200)
    e, (wg_up_p, wg_up_a, wg_out) = _mm_nt(
        hn, wt_e, BF16, "in_proj_gates", lp, E_COLS, tm, 512,
        gather=([w_up_pool[0].astype(BF16), w_up_attn[0].astype(BF16), w_out[0].astype(BF16)], [0, 0, 0]))
    wup_p = jnp.transpose(wg_up_p, (1, 0, 2)).reshape(POOL_WIDTH, D_MODEL)
    wup_a = jnp.transpose(wg_up_a, (1, 0, 2)).reshape(ATTN_WIDTH, D_MODEL)
    wout = wg_out.reshape(D_MODEL, D_MODEL)
    qkv = _mm_nt(hn, wt_qkv, BF16, "in_proj_qkv", lk, QKV_COLS, _row_tile(lk, 2600), 512, scale_first=HEAD_DIM ** -0.5)
    f = _mm_nt(hn, wt_f, F32, "in_proj_forget", lp, F_COLS, tm, F_COLS)
    c = _gates_fwd(f, bfg)
    c_t = jnp.transpose(c[:, :N_HEADS])
    c_first = jnp.transpose(c_t[:, ::ROW_TILE]).reshape(nb, N_HEADS // 2, 2, 1)
    c_keys = jnp.pad(c_t, ((0, 0), (0, lk - lp))).reshape(1, N_HEADS // 2, 2, lk)
    bias = jnp.where(jnp.arange(lk) < PAD_ROWS, -NEG, c_keys - c_first)
    o, lse = _attn_fwd(qkv, bias, nb)
    y_pool, merged, y_attn, dh1, loss_part, dgf = _merge_head_fwd_bwd(o, e, pw_b, pool_scale, wup_p, wup_a, wout, x2,
                                                                      metapad, gf, target)

    dap, daa, dproj_half, do, delta, dy_pool = _merge_bwd(dh1, wout, y_pool, y_attn, wup_p, wup_a, e, o)
    dproj_u, dscale, dpw = _pool_bwd(e, dy_pool, pw_b, pool_scale, dproj_half)
    dq, dk, dv, dc4, dcq = _attn_bwd(qkv, do, lse, delta, bias, nb)
    dc = jnp.transpose(dc4, (1, 3, 0, 2)).reshape(lk, N_HEADS)[:lp]
    df, db = _gates_bwd(jnp.pad(dc, ((0, 0), (0, F_COLS - N_HEADS))), dcq, f, bfg)
    dproj = _fill_dproj(dproj_u, dq, dk, dv)
    tk = _row_tile(lp, 1100)
    dw_out = _mm_tn(merged, dh1, "grad_w_out", lp, 512, D_MODEL, tk)
    dw_up_p = _mm_tn(y_pool, dap, "grad_w_up_pool", lp, 512, D_MODEL, lp, chunks=N_CHIPS)
    dw_up_a = _mm_tn(y_attn, daa, "grad_w_up_attn", lp, 512, D_MODEL, lp, chunks=N_CHIPS)
    dwt_f = _mm_tn(df, hn, "grad_w_in_forget", lp, F_COLS, D_MODEL, tk)

    def pad8(a):
        return jnp.pad(a, ((0, (-a.shape[0]) % 8), (0, 0)))

    small_parts = [pad8(a) for a in (dgf.reshape(-1, LANES), dscale.reshape(-1, LANES), db, dpw.reshape(-1, LANES),
                                     loss_part)]
    small = jnp.concatenate(small_parts, axis=0)
    soffs = [0]
    for p in small_parts:
        soffs.append(soffs[-1] + p.shape[0])

    gs_rows = [dw_up_p, dw_up_a, dw_out.reshape(N_CHIPS, sh_d, D_MODEL)]
    half = MAIN_COLS // 2
    dwt_a, (*rb_rows, rb_f, sr) = _mm_tn(dproj, hn, "grad_w_in_a", lp, half // 4, D_MODEL, lp, m=half, m_off=0,
                                         swap=(gs_rows + [dwt_f, small], [1, 1, 1, 1, None]))
    *hs_rows, sc = _add_sibling_half(gs_rows, rb_rows, [1, 1, 1], small, sr, core)
    dwt_b, (rb_a,), (qs_rows, sq) = _mm_tn(dproj, hn, "grad_w_in_b", lp, half // 4, D_MODEL, lp, m=half, m_off=4,
                                           swap=([dwt_a], [1]), scatter=(hs_rows, sc))
    h_in = _add_halves_w_in([dwt_a, dwt_b, dwt_f], [rb_a, rb_f], core)
    grad_axes = [2, 1, 1, 1]
    (grad_x, g_block0, dg1), (q_in,), _ = _input_bwd(dproj, df, wt_e, wt_qkv, wt_f, x2, metapad, dh1, norm_g, [h_in])
    late_parts = jnp.concatenate([pad8(dg1.reshape(-1, LANES)), g_block0[PAD_ROWS:].reshape(-1, LANES)], axis=0)
    g_w_in_rows, g_w_up_p, g_w_up_a, g_w_out, st, late = _reduce_allgather(
        [q_in] + list(qs_rows), grad_axes, sq, [True, False, False, False], late_parts)
    n_norm = D_MODEL // LANES
    g_norm = late[:n_norm].reshape(1, D_MODEL)
    chip = 2 * lax.axis_index("x") + lax.axis_index("y")
    g_meta = lax.dynamic_slice_in_dim(late[n_norm:].reshape(N_META, D_MODEL), chip * sh_d, sh_d, axis=1)

    spiece = lambda k, rows: st[soffs[k]:soffs[k] + rows]
    g_final = spiece(0, D_MODEL // LANES).reshape(1, D_MODEL)
    g_scale = spiece(1, POOL_WIDTH // LANES).reshape(1, POOL_WIDTH)
    g_bf = spiece(2, 1)
    g_pw = spiece(3, POOL_WIDTH)
    loss = st[soffs[4], 0]

    def pad_lanes(a):
        return jnp.pad(a, ((0, 0), (0, F_COLS - N_HEADS)))

    w_in_res = (g_w_in_rows,) + _adamw_native(to_rows(w_in), g_w_in_rows, to_rows(m_w_in), to_rows(v_w_in), "adamw_w_in")

    upd = [
        ("meta_tokens", meta_tokens, g_meta, m_meta_tokens, v_meta_tokens),
        ("norm_g", norm_g, g_norm, m_norm_g, v_norm_g),
        ("w_in", None, None, None, None),
        ("b_forget", pad_lanes(b_forget), g_bf, pad_lanes(m_b_forget), pad_lanes(v_b_forget)),
        ("pool_w", pool_w.reshape(-1, LANES), g_pw, m_pool_w.reshape(-1, LANES), v_pool_w.reshape(-1, LANES)),
        ("pool_scale", pool_scale, g_scale, m_pool_scale, v_pool_scale),
        ("w_up_pool", w_up_pool[0], g_w_up_p, m_w_up_pool[0], v_w_up_pool[0]),
        ("w_up_attn", w_up_attn[0], g_w_up_a, m_w_up_attn[0], v_w_up_attn[0]),
        ("w_out", w_out[0], g_w_out, m_w_out[0], v_w_out[0]),
        ("final_norm_g", gf, g_final, m_final_norm_g.reshape(1, D_MODEL), v_final_norm_g.reshape(1, D_MODEL)),
    ]
    shapes = [meta_tokens.shape, norm_g.shape, w_in.shape, b_forget.shape, pool_w.shape, pool_scale.shape,
              w_up_pool.shape, w_up_attn.shape, w_out.shape, final_norm_g.shape]
    grads, deltas, new_ms, new_vs = [], [], [], []
    for (name, w_, g_, m_in, v_in), shp in zip(upd, shapes):
        if name == "w_in":
            res = tuple(from_rows(a) for a in w_in_res)
        else:
            d_, mn_, vn_ = _adamw(w_, g_, m_in, v_in, "adamw_" + name)
            res = (g_, d_, mn_, vn_)
        if name == "b_forget":
            res = tuple(a[:, :N_HEADS] for a in res)
        for lst, a in zip((grads, deltas, new_ms, new_vs), res):
            lst.append(a.reshape(shp))

    return (loss, grad_x.reshape(x.shape), *grads, *deltas, *new_ms, *new_vs)
```

```python
import jax
import jax.numpy as jnp
from jax import lax
from jax.experimental import pallas as pl
from jax.experimental.pallas import tpu as pltpu

F32 = jnp.float32
BF16 = jnp.bfloat16
MESH = pl.DeviceIdType.MESH
HIGHEST = lax.Precision.HIGHEST
HBM = pl.BlockSpec(memory_space=pltpu.HBM)

D_MODEL = 1024
N_META = 16
POOL_WIDTH = 512
POOL_GROUP = 128
POOL_WINDOWS = (2, 4, 8, 16)
N_HEADS = 8
HEAD_DIM = 64
ATTN_WIDTH = 512
RMS_EPS = 1e-6
N_CHIPS = 4

ADAM_LR = 0.001
ADAM_B1 = 0.9
ADAM_B2 = 0.999
ADAM_EPS = 1e-08
ADAM_WD = 0.01
ADAM_STEP = 10

LANES = 128
ROW_TILE = 256
KV_TILE = 1024
KV_TILE_BWD = 1024
PAD_ROWS = ROW_TILE - N_META
NEG = -1e30

E_U, E_ZP, E_ZA, E_GP, E_GA, E_COLS = 0, 512, 1024, 1536, 2560, 3584
QKV_COLS = 3 * ATTN_WIDTH
MAIN_COLS = E_COLS + QKV_COLS
F_COLS = LANES
REF_U, REF_Q, REF_ZA, REF_F, REF_GP = 0, 1024, 2560, 3072, 3080
VMEM_LIMIT = 48 * 1024 * 1024


def _params(sem=None, vmem=VMEM_LIMIT):
    return pltpu.CompilerParams(dimension_semantics=sem, vmem_limit_bytes=vmem)


def _in_hbm(*arrays):
    return [pltpu.with_memory_space_constraint(a, pltpu.HBM) for a in arrays]


def _row_tile(n, target):
    best = 16
    for t in range(16, target + 1, 16):
        if n % t == 0:
            best = t
    return best


def _sigmoid(x):
    return 1.0 / (1.0 + jnp.exp(-x))


def _half(ref, axis, which):
    n = ref.shape[axis] // 2
    idx = [slice(None)] * len(ref.shape)
    idx[axis] = pl.ds(pl.multiple_of(which * n, n), n)
    return ref.at[tuple(idx)]


def _half_shape(shape, axis):
    s = list(shape)
    s[axis] //= 2
    return tuple(s)


def _gather_start(ins, outs, axes, send, recv, fsend, frecv, local):
    x, y, c = lax.axis_index("x"), lax.axis_index("y"), lax.axis_index("c")
    me = 2 * x + y
    for t in range(len(ins)):
        pltpu.make_async_copy(ins[t], outs[t].at[me], local.at[t]).start()
        for k, chip in enumerate([(1 - x, y), (x, 1 - y), (1 - x, 1 - y)]):
            pltpu.make_async_remote_copy(
                src_ref=_half(ins[t], axes[t], c), dst_ref=_half(outs[t].at[me], axes[t], c),
                send_sem=send.at[3 * t + k], recv_sem=recv.at[3 * t + k], device_id=(*chip, c), device_id_type=MESH).start()


def _gather_finish(ins, outs, axes, send, recv, fsend, frecv, local):
    x, y, c = lax.axis_index("x"), lax.axis_index("y"), lax.axis_index("c")
    me = 2 * x + y
    sibling = (x, y, 1 - c)
    chips = [(1 - x, y), (x, 1 - y), (1 - x, 1 - y)]
    n = len(ins)

    def over_ici(t, k, chip, dst_slot):
        return pltpu.make_async_remote_copy(
            src_ref=_half(ins[t], axes[t], c), dst_ref=_half(outs[t].at[dst_slot], axes[t], c),
            send_sem=send.at[3 * t + k], recv_sem=recv.at[3 * t + k], device_id=(*chip, c), device_id_type=MESH)

    def to_sibling(t, k, slot, which):
        return pltpu.make_async_remote_copy(
            src_ref=_half(outs[t].at[slot], axes[t], which), dst_ref=_half(outs[t].at[slot], axes[t], which),
            send_sem=fsend.at[3 * t + k], recv_sem=frecv.at[3 * t + k], device_id=sibling, device_id_type=MESH)

    forwards = []
    for t in range(n):
        for k, (px, py) in enumerate(chips):
            over_ici(t, k, (px, py), 2 * px + py).wait_recv()
            fw = to_sibling(t, k, 2 * px + py, c)
            fw.start()
            forwards.append(fw)
    for t in range(n):
        for k, (px, py) in enumerate(chips):
            to_sibling(t, k, 2 * px + py, 1 - c).wait_recv()
    for t in range(n):
        for k, chip in enumerate(chips):
            over_ici(t, k, chip, me).wait_send()
    for fw in forwards:
        fw.wait_send()
    for t in range(n):
        pltpu.make_async_copy(ins[t], outs[t].at[me], local.at[t]).wait()


def _gather_sems(n):
    return [pltpu.SemaphoreType.DMA((3 * n,)), pltpu.SemaphoreType.DMA((3 * n,)), pltpu.SemaphoreType.DMA((3 * n,)),
            pltpu.SemaphoreType.DMA((3 * n,)), pltpu.SemaphoreType.DMA((n,))]


def _gathered_shapes(shards):
    return [jax.ShapeDtypeStruct((N_CHIPS,) + s.shape, s.dtype) for s in shards]


def _swap_copies(srcs, dsts, axes, send, recv):
    x, y, c = lax.axis_index("x"), lax.axis_index("y"), lax.axis_index("c")
    return [pltpu.make_async_remote_copy(
        src_ref=srcs[t] if axes[t] is None else _half(srcs[t], axes[t], 1 - c), dst_ref=dsts[t],
        send_sem=send.at[t], recv_sem=recv.at[t], device_id=(x, y, 1 - c), device_id_type=MESH) for t in range(len(srcs))]


def _swap_shapes(srcs, axes):
    return [jax.ShapeDtypeStruct(g.shape if a is None else _half_shape(g.shape, a), g.dtype) for g, a in zip(srcs, axes)]


def _add_halves_w_in(parts, rbs, core):
    na, nb_rows, w = rbs[0].shape[0], parts[1].shape[0], rbs[0].shape[1]
    per = (na + nb_rows + N_HEADS) // N_CHIPS
    pieces = [[] for _ in range(N_CHIPS)]
    r = 0
    for src, lo, hi in ((0, 0, na), (1, 0, ATTN_WIDTH), (2, 0, N_HEADS), (1, ATTN_WIDTH, nb_rows)):
        while lo < hi:
            c, at = divmod(r, per)
            n = min(hi - lo, per - at)
            pieces[c].append((src, lo, n, at))
            lo, r = lo + n, r + n

    def body(core_ref, a_ref, b_ref, f_ref, ra_ref, rf_ref, o_ref, am, bm, fm, ar, br, fr, rows_buf, sems, send, recv):
        x, y, core_id = lax.axis_index("x"), lax.axis_index("y"), lax.axis_index("c")
        swap = pltpu.make_async_remote_copy(src_ref=_half(b_ref, 1, 1 - core_id), dst_ref=br, send_sem=send.at[0],
                                            recv_sem=recv.at[0], device_id=(x, y, 1 - core_id), device_id_type=MESH)
        swap.start()
        cols = pl.ds(pl.multiple_of(core_ref[0] * w, w), w)
        few = pl.ds(0, N_HEADS)
        loads = [(pltpu.make_async_copy(a_ref.at[:, cols], am, sems.at[0]), pltpu.make_async_copy(ra_ref, ar, sems.at[1])),
                 (pltpu.make_async_copy(b_ref.at[:, cols], bm, sems.at[2]),),
                 (pltpu.make_async_copy(f_ref.at[few, cols], fm, sems.at[3]),
                  pltpu.make_async_copy(rf_ref.at[few, :], fr, sems.at[4]))]
        for group in loads:
            for cp in group:
                cp.start()
        bufs = [(am, ar), (bm, br), (fm, fr)]
        there = set()
        for c in range(N_CHIPS):
            for src, lo, n, at in pieces[c]:
                if src not in there:
                    for cp in loads[src]:
                        cp.wait()
                    if src == 1:
                        swap.wait_recv()
                    there.add(src)
                mine, other = bufs[src]
                rows_buf[at:at + n, :] = mine[lo:lo + n, :] + other[lo:lo + n, :]
            o_ref[c] = rows_buf[0:per, :].astype(BF16)
        swap.wait_send()

    any_space = pl.BlockSpec(memory_space=pl.ANY)
    a, b, f = parts
    return pl.pallas_call(
        body, name="grad_add_sibling_w_in",
        grid_spec=pltpu.PrefetchScalarGridSpec(
            num_scalar_prefetch=1, grid=(1,),
            in_specs=[any_space, HBM, any_space, any_space, any_space],
            out_specs=pl.BlockSpec((N_CHIPS, per, w), lambda i, cr: (0, 0, 0)),
            scratch_shapes=[pltpu.VMEM((na, w), F32), pltpu.VMEM((nb_rows, w), F32), pltpu.VMEM((N_HEADS, w), F32),
                            pltpu.VMEM((na, w), F32), pltpu.VMEM((nb_rows, w), F32), pltpu.VMEM((N_HEADS, w), F32),
                            pltpu.VMEM((per + (-per) % 8, w), F32), pltpu.SemaphoreType.DMA((5,)),
                            pltpu.SemaphoreType.DMA((1,)), pltpu.SemaphoreType.DMA((1,))]),
        out_shape=jax.ShapeDtypeStruct((N_CHIPS, per, w), BF16),
        compiler_params=_params(("arbitrary",)),
    )(core, a, *_in_hbm(b), f, *rbs)


def _weight_sections(wg):
    per, d = wg.shape[1], wg.shape[2]
    n_ref = N_CHIPS * per
    units = [(0, 0, REF_U, REF_Q), (1, 0, REF_Q, REF_ZA), (0, REF_Q - REF_U, REF_ZA, REF_F), (2, 0, REF_F, REF_GP),
             (0, REF_Q - REF_U + REF_F - REF_ZA, REF_GP, n_ref)]
    plan = []
    for out, first, lo, hi in sorted(units, key=lambda u: u[2]):
        pieces, at = [], 0
        while lo < hi:
            c, src = divmod(lo, per)
            n = min(hi - lo, per - src)
            pieces.append((c, src, n, at))
            lo, at = lo + n, at + n
        plan.append((out, first, at, pieces))
    stage_rows = max(rows for _, _, rows, _ in plan)

    def body(g_ref, e_ref, q_ref, f_ref, gb, stage, sems):
        loads = [pltpu.make_async_copy(g_ref.at[c], gb.at[c], sems.at[c]) for c in range(N_CHIPS)]
        for cp in loads:
            cp.start()
        outs = [e_ref, q_ref, f_ref]
        there = set()
        for out, first, rows, pieces in plan:
            for c, _, _, _ in pieces:
                if c not in there:
                    loads[c].wait()
                    there.add(c)
            if len(pieces) == 1 and pieces[0][1] % 16 == 0 and rows % 16 == 0:
                c, src, _, _ = pieces[0]
                outs[out][first:first + rows, :] = gb[c, src:src + rows, :]
                continue
            for c, src, n, at in pieces:
                stage[at:at + n, :] = gb[c, src:src + n, :].astype(F32)
            padded = rows if rows % 16 == 0 else outs[out].shape[0] - first
            if padded > rows:
                stage[rows:padded, :] = jnp.zeros((padded - rows, d), F32)
            outs[out][first:first + padded, :] = stage[0:padded, :].astype(BF16)

    return pl.pallas_call(
        body, name="weight_sections",
        in_specs=[pl.BlockSpec(memory_space=pl.ANY)],
        out_shape=[jax.ShapeDtypeStruct((E_COLS, d), BF16), jax.ShapeDtypeStruct((QKV_COLS, d), BF16),
                   jax.ShapeDtypeStruct((F_COLS, d), BF16)],
        scratch_shapes=[pltpu.VMEM(wg.shape, BF16), pltpu.VMEM((stage_rows, d), F32), pltpu.SemaphoreType.DMA((N_CHIPS,))],
        compiler_params=_params(),
    )(wg)


def _add_sibling_half(gs, rbs, axes, small, sr, core):
    n = len(gs)

    def body(core_ref, *refs):
        g_refs, rb_refs = refs[:n], refs[n:2 * n]
        s_ref, sr_ref = refs[2 * n], refs[2 * n + 1]
        h_refs, sc_ref = refs[2 * n + 2:3 * n + 2], refs[3 * n + 2]
        for t in range(n):
            h_refs[t][...] = (g_refs[t][...] + rb_refs[t][...]).astype(BF16)
        sc_ref[...] = s_ref[...] + sr_ref[...]

    def mine(rb, axis):
        blk = (None,) + rb.shape[1:]
        if axis == 2:
            return pl.BlockSpec(blk, lambda j, cr: (j, 0, cr[0]))
        return pl.BlockSpec(blk, lambda j, cr: (j, cr[0], 0))

    chunk = lambda rb: pl.BlockSpec((None,) + rb.shape[1:], lambda j, cr: (j, 0, 0))
    whole = pl.BlockSpec(small.shape, lambda j, cr: (0, 0))
    return pl.pallas_call(
        body, name="grad_add_sibling",
        grid_spec=pltpu.PrefetchScalarGridSpec(
            num_scalar_prefetch=1, grid=(N_CHIPS,),
            in_specs=[mine(rb, a) for rb, a in zip(rbs, axes)] + [chunk(rb) for rb in rbs] + [whole, whole],
            out_specs=[chunk(rb) for rb in rbs] + [whole]),
        out_shape=[jax.ShapeDtypeStruct(rb.shape, BF16) for rb in rbs] + [jax.ShapeDtypeStruct(small.shape, F32)],
        compiler_params=_params(("arbitrary",)),
    )(core, *gs, *rbs, small, sr)


def _chip_scatter_plan(h_refs, s_ref, q_refs, sq_ref, send, recv, local):
    n = len(h_refs)
    nt = n + (s_ref is not None)
    x, y, c = lax.axis_index("x"), lax.axis_index("y"), lax.axis_index("c")
    me = 2 * x + y
    chips = [(1 - x, y), (x, 1 - y), (1 - x, 1 - y)]

    def copy(t, k, chip, src_slot, dst_slot):
        src = h_refs[t].at[src_slot] if t < n else s_ref
        dst = (q_refs[t] if t < n else sq_ref).at[dst_slot]
        return pltpu.make_async_remote_copy(src_ref=src, dst_ref=dst, send_sem=send.at[3 * t + k],
                                            recv_sem=recv.at[3 * t + k], device_id=(*chip, c), device_id_type=MESH)

    own = [pltpu.make_async_copy(h_refs[t].at[me], q_refs[t].at[me], local.at[t]) for t in range(n)]
    if s_ref is not None:
        own.append(pltpu.make_async_copy(s_ref, sq_ref.at[me], local.at[n]))
    sends = [copy(t, k, (px, py), 2 * px + py, me) for t in range(nt) for k, (px, py) in enumerate(chips)]
    recvs = [copy(t, k, (px, py), me, 2 * px + py) for t in range(nt) for k, (px, py) in enumerate(chips)]
    return own, sends, recvs


def _scatter_operands(hs, sc):
    small = [] if sc is None else [sc]
    nt = len(hs) + len(small)
    shapes = [jax.ShapeDtypeStruct(h.shape, h.dtype) for h in hs] + [
        jax.ShapeDtypeStruct((N_CHIPS,) + a.shape, a.dtype) for a in small]
    sems = [pltpu.SemaphoreType.DMA((3 * nt,)), pltpu.SemaphoreType.DMA((3 * nt,)), pltpu.SemaphoreType.DMA((nt,))]
    return list(hs) + small, shapes, sems


def _reduce_allgather(qs, axes, sq, as_rows, late):
    n = len(qs)
    shard_shapes, half_axes = [], []
    for q, a, rows_form in zip(qs, axes, as_rows):
        shape = [d * 2 if i == a - 1 else d for i, d in enumerate(q.shape[1:])]
        if rows_form:
            assert a == 2
            shape = [shape[0], 1, shape[1]]
        shard_shapes.append(tuple(shape))
        half_axes.append(2 if rows_form else a - 1)

    def body(*refs):
        q_refs, sq_ref, late_ref = refs[:n], refs[n], refs[n + 1]
        o_refs, st_ref, late_sum_ref = refs[n + 2:2 * n + 2], refs[2 * n + 2], refs[2 * n + 3]
        sib_buf, chip_buf, send, recv, late_send, late_recv = refs[2 * n + 4:]
        x, y, c = lax.axis_index("x"), lax.axis_index("y"), lax.axis_index("c")
        me = 2 * x + y
        chips = [(1 - x, y), (x, 1 - y), (1 - x, 1 - y)]

        def swap(t, which):
            return pltpu.make_async_remote_copy(
                src_ref=_half(o_refs[t], half_axes[t], which), dst_ref=_half(o_refs[t], half_axes[t], which),
                send_sem=send.at[t], recv_sem=recv.at[t], device_id=(x, y, 1 - c), device_id_type=MESH)

        def late_copy(k, chip, slot):
            return pltpu.make_async_remote_copy(src_ref=chip_buf.at[slot], dst_ref=chip_buf.at[slot], send_sem=late_send.at[1 + k],
                                                recv_sem=late_recv.at[1 + k], device_id=(*chip, c), device_id_type=MESH)

        late_swap = pltpu.make_async_remote_copy(src_ref=late_ref, dst_ref=sib_buf, send_sem=late_send.at[0],
                                                 recv_sem=late_recv.at[0], device_id=(x, y, 1 - c), device_id_type=MESH)
        late_swap.start()
        sent = []
        for t in range(n):
            q = q_refs[t]
            total = ((q[0].astype(F32) + q[1].astype(F32)) + q[2].astype(F32)) + q[3].astype(F32)
            if as_rows[t]:
                total = total.reshape(total.shape[0], 1, total.shape[1])
            _half(o_refs[t], half_axes[t], c)[...] = total
            cp = swap(t, c)
            cp.start()
            sent.append(cp)
            if t == 0:
                late_swap.wait()
                chip_buf[me] = late_ref[...] + sib_buf[...]
                late_sends = [late_copy(k, chip, me) for k, chip in enumerate(chips)]
                for lc in late_sends:
                    lc.start()
        st_ref[...] = ((sq_ref[0] + sq_ref[1]) + sq_ref[2]) + sq_ref[3]
        for t in range(n):
            swap(t, 1 - c).wait_recv()
        for cp in sent:
            cp.wait_send()
        for k, (px, py) in enumerate(chips):
            late_copy(k, (px, py), 2 * px + py).wait_recv()
        for lc in late_sends:
            lc.wait_send()
        late_sum_ref[...] = ((chip_buf[0] + chip_buf[1]) + chip_buf[2]) + chip_buf[3]

    vmem = pl.BlockSpec(memory_space=pltpu.VMEM)
    return pl.pallas_call(
        body, name="grad_reduce_allgather",
        in_specs=[vmem] * (n + 2), out_specs=[vmem] * (n + 2),
        out_shape=[jax.ShapeDtypeStruct(s, F32) for s in shard_shapes] + [jax.ShapeDtypeStruct(sq.shape[1:], F32),
                                                                         jax.ShapeDtypeStruct(late.shape, F32)],
        scratch_shapes=[pltpu.VMEM(late.shape, F32), pltpu.VMEM((N_CHIPS,) + late.shape, F32),
                        pltpu.SemaphoreType.DMA((n,)), pltpu.SemaphoreType.DMA((n,)),
                        pltpu.SemaphoreType.DMA((4,)), pltpu.SemaphoreType.DMA((4,))],
        compiler_params=_params(),
    )(*qs, sq, late)


def _dot_nt(a, b):
    return lax.dot_general(a, b, (((1,), (1,)), ((), ())), preferred_element_type=F32)


def _dot_tn(a, b):
    return lax.dot_general(a, b, (((0,), (0,)), ((), ())), preferred_element_type=F32)


def _mm_nt(a, bt, out_dtype, name, m, n, tm, tn, row_block=0, scale_first=None, gather=None):
    k = a.shape[1]
    shards, axes = gather if gather is not None else ([], [])
    ng = len(shards)
    grid = (m // tm, n // tn)

    def body(a_ref, b_ref, *rest):
        ins, o_ref, outs, sems = rest[:ng], rest[ng], rest[ng + 1:2 * ng + 1], rest[2 * ng + 1:]
        first = (pl.program_id(0) == 0) & (pl.program_id(1) == 0)
        last = (pl.program_id(0) == grid[0] - 1) & (pl.program_id(1) == grid[1] - 1)
        if ng:
            @pl.when(first)
            def _():
                _gather_start(ins, outs, axes, *sems)

        r = _dot_nt(a_ref[...], b_ref[...])
        if scale_first is not None:
            r = r * jnp.where(pl.program_id(1) == 0, scale_first, 1.0)
        o_ref[...] = r.astype(out_dtype)

        if ng:
            @pl.when(last)
            def _():
                _gather_finish(ins, outs, axes, *sems)

    res = pl.pallas_call(
        body, name=name, grid=grid,
        in_specs=[pl.BlockSpec((tm, k), lambda i, j: (i, 0)), pl.BlockSpec((tn, k), lambda i, j: (row_block + j, 0))]
        + [HBM] * ng,
        out_specs=[pl.BlockSpec((tm, tn), lambda i, j: (i, j))] + [HBM] * ng,
        out_shape=[jax.ShapeDtypeStruct((m, n), out_dtype)] + _gathered_shapes(shards),
        scratch_shapes=_gather_sems(ng) if ng else [],
        compiler_params=_params(("arbitrary", "arbitrary") if ng else ("parallel", "parallel")),
    )(a, bt, *_in_hbm(*shards))
    return (res[0], res[1:]) if ng else res[0]


def _mm_tn(a, b, name, k, tm, tn, tk, chunks=1, m=None, m_off=0, swap=None, scatter=None):
    m = a.shape[1] if m is None else m
    n = b.shape[1]
    cw = n // chunks
    srcs, axes = swap if swap is not None else ([], [])
    ns = len(srcs)
    hs, sc = scatter if scatter is not None else ([], None)
    sc_in, sc_shapes, sc_sems = _scatter_operands(hs, sc) if scatter is not None else ([], [], [])
    nh, nq = len(hs), len(sc_in)
    grid = (m // tm, n // tn, k // tk)

    def body(a_ref, b_ref, *rest):
        s_refs, h_refs = rest[:ns], rest[ns:ns + nq]
        o_ref = rest[ns + nq]
        d_refs, q_refs = rest[ns + nq + 1:2 * ns + nq + 1], rest[2 * ns + nq + 1:2 * ns + 2 * nq + 1]
        sems = rest[2 * ns + 2 * nq + 1:]
        swap_sems, scatter_sems = (sems[:2], sems[2:]) if ns else ((), sems)
        ids = [pl.program_id(d) for d in range(3)]

        def scatter_plan():
            small_in = h_refs[nh] if sc is not None else None
            small_out = q_refs[nh] if sc is not None else None
            return _chip_scatter_plan(h_refs[:nh], small_in, q_refs[:nh], small_out, *scatter_sems)

        if ns or nq:
            @pl.when((ids[0] == 0) & (ids[1] == 0) & (ids[2] == 0))
            def _():
                if ns:
                    for cp in _swap_copies(s_refs, d_refs, axes, *swap_sems):
                        cp.start()
                if nq:
                    own, sends, _ = scatter_plan()
                    for cp in own + sends:
                        cp.start()

        @pl.when(ids[2] == 0)
        def _():
            o_ref[...] = jnp.zeros_like(o_ref)
        r = _dot_tn(a_ref[...].astype(BF16), b_ref[...].astype(BF16))
        if chunks > 1:
            for c in range(chunks):
                o_ref[c] += r[:, c * cw:(c + 1) * cw]
        else:
            o_ref[...] += r

        if ns or nq:
            @pl.when((ids[0] == grid[0] - 1) & (ids[1] == grid[1] - 1) & (ids[2] == grid[2] - 1))
            def _():
                if ns:
                    for cp in _swap_copies(s_refs, d_refs, axes, *swap_sems):
                        cp.wait()
                if nq:
                    own, sends, recvs = scatter_plan()
                    for cp in recvs:
                        cp.wait_recv()
                    for cp in sends:
                        cp.wait_send()
                    for cp in own:
                        cp.wait()

    if chunks > 1:
        assert tn == n
        out_spec = pl.BlockSpec((chunks, tm, cw), lambda i, j, kk: (0, i, 0))
        out_shape = jax.ShapeDtypeStruct((chunks, m, cw), F32)
    else:
        out_spec = pl.BlockSpec((tm, tn), lambda i, j, kk: (i, j))
        out_shape = jax.ShapeDtypeStruct((m, n), F32)
    riders = ns + nq
    res = pl.pallas_call(
        body, name=name, grid=grid,
        in_specs=[pl.BlockSpec((tk, tm), lambda i, j, kk: (kk, m_off + i)), pl.BlockSpec((tk, tn), lambda i, j, kk: (kk, j))]
        + [HBM] * riders,
        out_specs=[out_spec] + [HBM] * riders, out_shape=[out_shape] + _swap_shapes(srcs, axes) + sc_shapes,
        scratch_shapes=([pltpu.SemaphoreType.DMA((ns,)), pltpu.SemaphoreType.DMA((ns,))] if ns else []) + sc_sems,
        compiler_params=_params(("arbitrary",) * 3 if riders else ("parallel", "parallel", "arbitrary")),
    )(a, b, *_in_hbm(*srcs, *sc_in))
    if not riders:
        return res[0]
    out = [res[0]]
    if ns:
        out.append(res[1:1 + ns])
    if nq:
        out.append((res[1 + ns:1 + ns + nh], res[1 + ns + nh] if sc is not None else None))
    return tuple(out)


def _tokens_spec():
    return pl.BlockSpec((ROW_TILE, D_MODEL), lambda i: (jnp.maximum(i - 1, 0), 0))


def _rmsnorm_fwd(x2, g1, lk, shards, axes):
    nb = x2.shape[0] // ROW_TILE + 1
    nblk = lk // ROW_TILE
    ng = len(shards)
    meta_cols = shards[-1].shape[1]

    def body(x_ref, g_ref, *rest):
        ins, hn_ref, outs = rest[:ng], rest[ng], rest[ng + 1:2 * ng + 1]
        sems, meta_buf, meta_sem = rest[2 * ng + 1:2 * ng + 6], rest[2 * ng + 6], rest[2 * ng + 7]
        s = pl.program_id(0)
        blk = (s + 1) % nblk

        @pl.when(s == 0)
        def _():
            _gather_start(ins, outs, axes, *sems)

        def normed(h):
            r = lax.rsqrt(jnp.mean(h * h, axis=-1, keepdims=True) + RMS_EPS)
            return ((h * r) * g_ref[...]).astype(BF16)

        @pl.when(s < nblk - 1)
        def _():
            hn_ref[...] = normed(jnp.where(blk >= nb, 0.0, x_ref[...]))

        @pl.when(s == nblk - 1)
        def _():
            _gather_finish(ins, outs, axes, *sems)
            fetch = pltpu.make_async_copy(outs[-1], meta_buf, meta_sem.at[0])
            fetch.start()
            fetch.wait()
            meta = jnp.concatenate([meta_buf[j] for j in range(N_CHIPS)], axis=1)
            hn_ref[...] = normed(jnp.concatenate([jnp.zeros((PAD_ROWS, D_MODEL), F32), meta], axis=0))

    res = pl.pallas_call(
        body, name="rmsnorm_fwd", grid=(nblk,),
        in_specs=[pl.BlockSpec((ROW_TILE, D_MODEL), lambda s: (jnp.clip((s + 1) % nblk - 1, 0, nb - 2), 0)),
                  pl.BlockSpec((1, D_MODEL), lambda s: (0, 0))] + [HBM] * ng,
        out_specs=[pl.BlockSpec((ROW_TILE, D_MODEL), lambda s: ((s + 1) % nblk, 0))] + [HBM] * ng,
        out_shape=[jax.ShapeDtypeStruct((lk, D_MODEL), BF16)] + _gathered_shapes(shards),
        scratch_shapes=_gather_sems(ng) + [pltpu.VMEM((N_CHIPS, N_META, meta_cols), F32), pltpu.SemaphoreType.DMA((1,))],
        compiler_params=_params(("arbitrary",)),
    )(x2, g1, *_in_hbm(*shards))
    return res[0], res[1:]


def _valid_gate_mask(i):
    row = i * ROW_TILE + lax.broadcasted_iota(jnp.int32, (ROW_TILE, F_COLS), 0)
    col = lax.broadcasted_iota(jnp.int32, (ROW_TILE, F_COLS), 1)
    return (row >= PAD_ROWS) & (col < N_HEADS)


def _gates_fwd(f, bfg):
    nb = f.shape[0] // ROW_TILE

    def body(f_ref, b_ref, c_ref, carry):
        i = pl.program_id(0)

        @pl.when(i == 0)
        def _():
            carry[...] = jnp.zeros_like(carry)

        logit = f_ref[...] + b_ref[...]
        lf = jnp.minimum(logit, 0.0) - jnp.log1p(jnp.exp(-jnp.abs(logit)))
        lf = jnp.where(_valid_gate_mask(i), lf, 0.0)
        r_i = lax.broadcasted_iota(jnp.int32, (ROW_TILE, ROW_TILE), 0)
        c_i = lax.broadcasted_iota(jnp.int32, (ROW_TILE, ROW_TILE), 1)
        tri = (c_i <= r_i).astype(F32)
        c_ref[...] = jnp.dot(tri, lf, precision=HIGHEST, preferred_element_type=F32) + carry[...]
        carry[...] = carry[...] + jnp.sum(lf, axis=0, keepdims=True)

    return pl.pallas_call(
        body, name="gates_fwd", grid=(nb,),
        in_specs=[pl.BlockSpec((ROW_TILE, F_COLS), lambda i: (i, 0)), pl.BlockSpec((1, F_COLS), lambda i: (0, 0))],
        out_specs=pl.BlockSpec((ROW_TILE, F_COLS), lambda i: (i, 0)),
        out_shape=jax.ShapeDtypeStruct(f.shape, F32),
        scratch_shapes=[pltpu.VMEM((1, F_COLS), F32)],
        compiler_params=_params(("arbitrary",)),
    )(f, bfg)


def _pool_counts(i):
    row = i * ROW_TILE + lax.broadcasted_iota(jnp.int32, (ROW_TILE, 1), 0)
    return jnp.maximum(row - PAD_ROWS, 0)


def _trailing_sums(xc, levels):
    acc = xc
    for lv in range(levels):
        acc = acc + pltpu.roll(acc, 1 << lv, 0)
    return acc


def _leading_sums(xc, levels):
    n = xc.shape[0]
    acc = xc
    for lv in range(levels):
        acc = acc + pltpu.roll(acc, n - (1 << lv), 0)
    return acc


def _pool_p(u_cur, u_prev, i):
    pos = _pool_counts(i)
    ps, invs = [], []
    for g, w in enumerate(POOL_WINDOWS):
        sl = slice(g * POOL_GROUP, (g + 1) * POOL_GROUP)
        cur = u_cur[:, sl]
        xc = jnp.concatenate([u_prev[:, sl], cur], axis=0)
        win = _trailing_sums(xc, g + 1)[ROW_TILE:, :]
        inv = 1.0 / jnp.minimum(pos + 1, w).astype(F32)
        ps.append(win * inv - cur)
        invs.append(inv)
    return ps, invs


def _stack_heads(a):
    first = lax.broadcasted_iota(jnp.int32, a.shape, 1) < HEAD_DIM
    zero = jnp.zeros_like(a)
    return jnp.concatenate([jnp.where(first, a, zero), jnp.where(first, zero, a)], axis=0)


def _unstack_heads(a):
    rows = a.shape[0] // 2
    first = lax.broadcasted_iota(jnp.int32, (rows, LANES), 1) < HEAD_DIM
    return jnp.where(first, a[:rows], a[rows:])


def _causal(i, jj, stacked, kv_tile):
    r = lax.broadcasted_iota(jnp.int32, (stacked * ROW_TILE, kv_tile), 0)
    if stacked == 2:
        r = jnp.where(r >= ROW_TILE, r - ROW_TILE, r)
    kidx = jj * kv_tile + lax.broadcasted_iota(jnp.int32, (stacked * ROW_TILE, kv_tile), 1)
    return kidx <= i * ROW_TILE + r


def _stack_rows(b):
    n = b.shape[1]
    return jnp.concatenate([jnp.broadcast_to(b[0:1], (ROW_TILE, n)), jnp.broadcast_to(b[1:2], (ROW_TILE, n))], axis=0)


def _attn_fwd(qkv, bias, nb):
    lk = qkv.shape[0]
    lp = nb * ROW_TILE
    nkb = lk // KV_TILE
    n_pairs = N_HEADS // 2

    def body(q_ref, k_ref, v_ref, b_ref, o_ref, lse_ref):
        i = pl.program_id(1)
        qs = _stack_heads(q_ref[...])
        last = (i * ROW_TILE) // KV_TILE

        def block(jj, carry, masked, n_keys=KV_TILE):
            m, l, acc = carry
            rows = pl.ds(pl.multiple_of(jj * KV_TILE, KV_TILE), n_keys)
            s = _dot_nt(qs, k_ref[rows, :]) - _stack_rows(b_ref[0, 0, :, rows])
            if masked:
                s = jnp.where(_causal(i, jj, 2, KV_TILE)[:, :n_keys], s, NEG)
            m_new = jnp.maximum(m, jnp.max(s, axis=1, keepdims=True))
            alpha = jnp.exp(m - m_new)
            p = jnp.exp(s - m_new)
            l = alpha * l + jnp.sum(p, axis=1, keepdims=True)
            acc = alpha * acc + jnp.dot(p.astype(BF16), v_ref[rows, :], preferred_element_type=F32)
            return m_new, l, acc

        init = (jnp.full((2 * ROW_TILE, 1), NEG, F32), jnp.zeros((2 * ROW_TILE, 1), F32),
                jnp.zeros((2 * ROW_TILE, LANES), F32))
        def finish(carry):
            m, l, acc = carry
            o_ref[...] = _unstack_heads(acc / l)
            lse_ref[...] = _unstack_heads(jnp.broadcast_to(m + jnp.log(l), (2 * ROW_TILE, LANES)))

        carry = lax.fori_loop(0, last, lambda jj, c: block(jj, c, False), init)
        ends = [lambda c, n=(r + 1) * ROW_TILE: finish(block(last, c, True, n)) for r in range(KV_TILE // ROW_TILE)]
        lax.switch(i - last * (KV_TILE // ROW_TILE), ends, carry)

    return pl.pallas_call(
        body, name="attn_fwd", grid=(n_pairs, nb),
        in_specs=[pl.BlockSpec((ROW_TILE, LANES), lambda hp, i: (i, hp)),
                  pl.BlockSpec((lk, LANES), lambda hp, i: (0, n_pairs + hp)),
                  pl.BlockSpec((lk, LANES), lambda hp, i: (0, 2 * n_pairs + hp)),
                  pl.BlockSpec((1, 1, 2, lk), lambda hp, i: (i, hp, 0, 0))],
        out_specs=[pl.BlockSpec((ROW_TILE, LANES), lambda hp, i: (i, hp)),
                   pl.BlockSpec((ROW_TILE, LANES), lambda hp, i: (i, hp))],
        out_shape=[jax.ShapeDtypeStruct((lp, ATTN_WIDTH), F32), jax.ShapeDtypeStruct((lp, ATTN_WIDTH), F32)],
        compiler_params=_params(("parallel", "parallel")),
    )(qkv, qkv, qkv, bias)


def _merge_head_fwd_bwd(o, e, pw, scale, wup_p, wup_a, w_out, x2, metapad, gf, target):
    lp = o.shape[0]
    nb = lp // ROW_TILE

    def body(up_ref, o_ref, e_ref, pw_ref, sc_ref, wp_ref, wa_ref, w_ref, x_ref, mp_ref, g_ref, t_ref,
             yp_ref, mg_ref, ya_ref, dh_ref, loss_ref, dg_ref):
        i = pl.program_id(0)
        u_cur = e_ref[:, E_U:E_ZP].astype(F32)
        u_prev = jnp.where(i == 0, 0.0, up_ref[...].astype(F32))
        ps, _ = _pool_p(u_cur, u_prev, i)
        zp = e_ref[:, E_ZP:E_ZA].astype(F32)
        gate = zp * _sigmoid(zp)
        for g in range(len(POOL_WINDOWS)):
            sl = slice(g * POOL_GROUP, (g + 1) * POOL_GROUP)
            yraw = jnp.dot(ps[g].astype(BF16), pw_ref[g], preferred_element_type=F32)
            yp_ref[:, sl] = ((yraw * sc_ref[:, sl]) * gate[:, sl]).astype(BF16)

        @pl.when(i == 0)
        def _():
            loss_ref[...] = jnp.zeros_like(loss_ref)
            dg_ref[...] = jnp.zeros_like(dg_ref)

        za = e_ref[:, E_ZA:E_GP].astype(F32)
        ya = (o_ref[...] * (za * _sigmoid(za))).astype(BF16)
        ya_ref[...] = ya
        a_pool = jnp.dot(yp_ref[...], wp_ref[...], preferred_element_type=F32)
        a_attn = jnp.dot(ya, wa_ref[...], preferred_element_type=F32)
        merged = (_sigmoid(e_ref[:, E_GP:E_GA].astype(F32)) * a_pool
                  + _sigmoid(e_ref[:, E_GA:E_COLS].astype(F32)) * a_attn).astype(BF16)
        mg_ref[...] = merged

        h0 = jnp.where(i == 0, mp_ref[...], x_ref[...])
        h1 = h0 + jnp.dot(merged, w_ref[...], preferred_element_type=F32)
        r = lax.rsqrt(jnp.mean(h1 * h1, axis=-1, keepdims=True) + RMS_EPS)
        xhat = h1 * r
        g = g_ref[...]
        err = jnp.where(i == 0, 0.0, xhat * g - t_ref[...])
        loss_ref[...] += 0.5 * jnp.sum(jnp.mean(err * err, axis=-1, keepdims=True))
        dy = err / D_MODEL
        dg_ref[...] += jnp.sum(dy * xhat, axis=0, keepdims=True)
        dxhat = dy * g
        dh_ref[...] = r * (dxhat - xhat * jnp.mean(dxhat * xhat, axis=-1, keepdims=True))

    row = lambda w: pl.BlockSpec((ROW_TILE, w), lambda i: (i, 0))
    const = lambda shape: pl.BlockSpec(shape, lambda i: (0, 0))
    return pl.pallas_call(
        body, name="merge_head_fwd_bwd", grid=(nb,),
        in_specs=[pl.BlockSpec((ROW_TILE, POOL_WIDTH), lambda i: (jnp.maximum(i - 1, 0), 0)), row(ATTN_WIDTH), row(E_COLS),
                  pl.BlockSpec((len(POOL_WINDOWS), POOL_GROUP, POOL_GROUP), lambda i: (0, 0, 0)), const((1, POOL_WIDTH)),
                  const((POOL_WIDTH, D_MODEL)), const((ATTN_WIDTH, D_MODEL)),
                  const((D_MODEL, D_MODEL)), _tokens_spec(), const((ROW_TILE, D_MODEL)), const((1, D_MODEL)), _tokens_spec()],
        out_specs=[row(POOL_WIDTH), row(D_MODEL), row(ATTN_WIDTH), row(D_MODEL), const((1, LANES)), const((1, D_MODEL))],
        out_shape=[jax.ShapeDtypeStruct((lp, POOL_WIDTH), BF16), jax.ShapeDtypeStruct((lp, D_MODEL), BF16),
                   jax.ShapeDtypeStruct((lp, ATTN_WIDTH), BF16),
                   jax.ShapeDtypeStruct((lp, D_MODEL), F32), jax.ShapeDtypeStruct((1, LANES), F32),
                   jax.ShapeDtypeStruct((1, D_MODEL), F32)],
        compiler_params=_params(("arbitrary",)),
    )(e, o, e, pw, scale, wup_p, wup_a, w_out, x2, metapad, gf, target)


def _per_head_rowsum(t):
    head = lax.broadcasted_iota(jnp.int32, t.shape, 1) // HEAD_DIM
    out = jnp.zeros_like(t)
    for h in range(N_HEADS):
        sel = head == h
        out = jnp.where(sel, jnp.sum(jnp.where(sel, t, 0.0), axis=1, keepdims=True), out)
    return out


def _merge_bwd(dh1, w_out, y_pool, y_attn, wup_p, wup_a, e, o):
    lp = o.shape[0]
    nb = lp // ROW_TILE

    def body(dh_ref, wo_ref, yp_ref, ya_ref, wp_ref, wa_ref, e_ref, o_ref,
             dap_ref, daa_ref, dg_ref, do_ref, delta_ref, dyp_ref):
        dmerged = _dot_nt(dh_ref[...].astype(BF16), wo_ref[...])
        a_pool = jnp.dot(yp_ref[...], wp_ref[...], preferred_element_type=F32)
        a_attn = jnp.dot(ya_ref[...], wa_ref[...], preferred_element_type=F32)
        sp = _sigmoid(e_ref[:, E_GP:E_GA].astype(F32))
        sa = _sigmoid(e_ref[:, E_GA:E_COLS].astype(F32))
        dap = (dmerged * sp).astype(BF16)
        daa = (dmerged * sa).astype(BF16)
        dap_ref[...] = dap
        daa_ref[...] = daa
        dg_ref[:, ATTN_WIDTH:ATTN_WIDTH + D_MODEL] = (dmerged * a_pool * (sp * (1.0 - sp))).astype(BF16)
        dg_ref[:, ATTN_WIDTH + D_MODEL:] = (dmerged * a_attn * (sa * (1.0 - sa))).astype(BF16)
        dyp_ref[...] = _dot_nt(dap, wp_ref[...])
        dya = _dot_nt(daa, wa_ref[...])
        za = e_ref[:, E_ZA:E_GP].astype(F32)
        sz = _sigmoid(za)
        o = o_ref[...]
        do = dya * (za * sz)
        do_ref[...] = do.astype(BF16)
        dg_ref[:, :ATTN_WIDTH] = (dya * o * (sz * (1.0 + za * (1.0 - sz)))).astype(BF16)
        delta_ref[...] = _per_head_rowsum(do * o)

    row = lambda w: pl.BlockSpec((ROW_TILE, w), lambda i: (i, 0))
    full = lambda a: pl.BlockSpec(a.shape, lambda i: (0, 0))
    return pl.pallas_call(
        body, name="merge_bwd", grid=(nb,),
        in_specs=[row(D_MODEL), full(w_out), row(POOL_WIDTH), row(ATTN_WIDTH), full(wup_p), full(wup_a),
                  row(E_COLS), row(ATTN_WIDTH)],
        out_specs=[row(D_MODEL), row(D_MODEL), pl.BlockSpec((ROW_TILE, MAIN_COLS // 2), lambda i: (i, 1)),
                   row(ATTN_WIDTH), row(ATTN_WIDTH), row(POOL_WIDTH)],
        out_shape=[jax.ShapeDtypeStruct((lp, D_MODEL), BF16), jax.ShapeDtypeStruct((lp, D_MODEL), BF16),
                   jax.ShapeDtypeStruct((lp, MAIN_COLS), BF16),
                   jax.ShapeDtypeStruct((lp, ATTN_WIDTH), BF16), jax.ShapeDtypeStruct((lp, ATTN_WIDTH), F32),
                   jax.ShapeDtypeStruct((lp, POOL_WIDTH), F32)],
        compiler_params=_params(("parallel",)),
    )(dh1, w_out, y_pool, y_attn, wup_p, wup_a, e, o)


def _pool_bwd(e, dy_pool, pw, scale, dproj, dq, dk, dv):
    lp = e.shape[0]
    nb = lp // ROW_TILE
    ng = len(POOL_WINDOWS)

    def body(uc_ref, up_ref, z_ref, dy_ref, pw_ref, sc_ref, base_ref, dq_ref, dk_ref, dv_ref, o_ref, dsc_ref, dpw_ref,
             cur_ref, nxt_ref, dz_cur_ref, dz_nxt_ref):
        i = pl.program_id(0)

        @pl.when(i == 0)
        def _():
            dsc_ref[...] = jnp.zeros_like(dsc_ref)
            dpw_ref[...] = jnp.zeros_like(dpw_ref)

        @pl.when(i < nb)
        def _():
            u_cur = uc_ref[...].astype(F32)
            u_prev = jnp.where(i == 0, 0.0, up_ref[...].astype(F32))
            ps, invs = _pool_p(u_cur, u_prev, i)
            z = z_ref[...].astype(F32)
            sz = _sigmoid(z)
            dy = dy_ref[...]
            dypre = dy * (z * sz)
            dsilu = sz * (1.0 + z * (1.0 - sz))
            for g in range(ng):
                sl = slice(g * POOL_GROUP, (g + 1) * POOL_GROUP)
                pb = ps[g].astype(BF16)
                w = pw_ref[g]
                yraw = jnp.dot(pb, w, preferred_element_type=F32)
                sc = sc_ref[:, sl]
                dz_nxt_ref[:, sl] = (dy[:, sl] * (yraw * sc) * dsilu[:, sl]).astype(BF16)
                dsc_ref[:, sl] += jnp.sum(dypre[:, sl] * yraw, axis=0, keepdims=True)
                dyraw = (dypre[:, sl] * sc).astype(BF16)
                dpw_ref[g] += _dot_tn(pb, dyraw)
                nxt_ref[:, sl] = _dot_nt(dyraw, w) * invs[g]

        @pl.when(i == nb)
        def _():
            nxt_ref[...] = jnp.zeros_like(nxt_ref)

        @pl.when(i > 0)
        def _():
            cur, nxt = cur_ref[...], nxt_ref[...]
            pos = _pool_counts(i - 1)
            for g, w in enumerate(POOL_WINDOWS):
                sl = slice(g * POOL_GROUP, (g + 1) * POOL_GROUP)
                xc = jnp.concatenate([cur[:, sl], nxt[:, sl]], axis=0)
                win = _leading_sums(xc, g + 1)[:ROW_TILE, :]
                dp = cur[:, sl] * jnp.minimum(pos + 1, w).astype(F32)
                o_ref[:, sl] = (win - dp).astype(BF16)
            o_ref[:, POOL_WIDTH:2 * POOL_WIDTH] = dz_cur_ref[...]
            parts = [(dq_ref[...] * HEAD_DIM ** -0.5).astype(BF16), dk_ref[...], dv_ref[...]]
            for t, part in enumerate(parts):
                o_ref[:, (2 + t) * ATTN_WIDTH:(3 + t) * ATTN_WIDTH] = part

        cur_ref[...] = nxt_ref[...]
        dz_cur_ref[...] = dz_nxt_ref[...]

    blk = (ROW_TILE, POOL_WIDTH)
    at = lambda i: jnp.minimum(i, nb - 1)
    return pl.pallas_call(
        body, name="pool_bwd", grid=(nb + 1,),
        in_specs=[pl.BlockSpec(blk, lambda i: (at(i), 0)), pl.BlockSpec(blk, lambda i: (jnp.maximum(at(i) - 1, 0), 0)),
                  pl.BlockSpec(blk, lambda i: (at(i), 1)), pl.BlockSpec(blk, lambda i: (at(i), 0)),
                  pl.BlockSpec((ng, POOL_GROUP, POOL_GROUP), lambda i: (0, 0, 0)),
                  pl.BlockSpec((1, POOL_WIDTH), lambda i: (0, 0)), pl.BlockSpec(memory_space=pl.ANY)]
        + [pl.BlockSpec((ROW_TILE, ATTN_WIDTH), lambda i: (jnp.maximum(i - 1, 0), 0))] * 3,
        out_specs=[pl.BlockSpec((ROW_TILE, MAIN_COLS // 2), lambda i: (jnp.maximum(i - 1, 0), 0)),
                   pl.BlockSpec((1, POOL_WIDTH), lambda i: (0, 0)),
                   pl.BlockSpec((ng, POOL_GROUP, POOL_GROUP), lambda i: (0, 0, 0))],
        out_shape=[jax.ShapeDtypeStruct(dproj.shape, BF16), jax.ShapeDtypeStruct((1, POOL_WIDTH), F32),
                   jax.ShapeDtypeStruct((ng, POOL_GROUP, POOL_GROUP), F32)],
        scratch_shapes=[pltpu.VMEM(blk, F32), pltpu.VMEM(blk, F32), pltpu.VMEM(blk, BF16), pltpu.VMEM(blk, BF16)],
        input_output_aliases={6: 0},
        compiler_params=_params(("arbitrary",)),
    )(e, e, e, dy_pool, pw, scale, dproj, dq, dk, dv)


def _attn_bwd(qkv, do, lse, delta, bias, nb):
    lk = qkv.shape[0]
    lp = nb * ROW_TILE
    nkb = lk // KV_TILE_BWD
    n_pairs = N_HEADS // 2
    per_kv = KV_TILE_BWD // ROW_TILE

    def body(q_ref, k_ref, v_ref, do_ref, lse_ref, dl_ref, b_ref, dq_ref, dk_ref, dv_ref, dc_ref, dcq_ref,
             dk_acc, dv_acc, dc_acc):
        jj = pl.program_id(1)

        @pl.when(jj == 0)
        def _():
            dq_ref[...] = jnp.zeros_like(dq_ref)
            dcq_ref[...] = jnp.zeros_like(dcq_ref)

        dk_acc[...] = jnp.zeros_like(dk_acc)
        dv_acc[...] = jnp.zeros_like(dv_acc)
        dc_acc[...] = jnp.zeros_like(dc_acc)

        def block(i, n_keys, masked):
            kb, vb = k_ref[:n_keys, :], v_ref[:n_keys, :]
            rows = pl.ds(pl.multiple_of(i * ROW_TILE, ROW_TILE), ROW_TILE)
            qs = _stack_heads(q_ref[rows, :])
            dos = _stack_heads(do_ref[rows, :])
            lse_i, dl_i = lse_ref[rows, :], dl_ref[rows, :]
            s = _dot_nt(qs, kb)
            dp = _dot_nt(dos, vb)
            if masked:
                valid = _causal(i, jj, 1, KV_TILE_BWD)[:, :n_keys]
            ps, dss, dcs, rowsums = [], [], [], []
            for hd in range(2):
                half = slice(hd * ROW_TILE, (hd + 1) * ROW_TILE)
                col = slice(hd * HEAD_DIM, hd * HEAD_DIM + 1)
                sh = s[half] - b_ref[i, 0, hd:hd + 1, :n_keys]
                if masked:
                    sh = jnp.where(valid, sh, NEG)
                p = jnp.exp(sh - lse_i[:, col])
                ds = p * (dp[half] - dl_i[:, col])
                ps.append(p.astype(BF16))
                dss.append(ds.astype(BF16))
                dcs.append(jnp.sum(ds, axis=0, keepdims=True))
                rowsums.append(jnp.sum(ds, axis=1, keepdims=True))
            dsb = jnp.concatenate(dss, axis=0)
            dv_acc[:n_keys, :] += _dot_tn(jnp.concatenate(ps, axis=0), dos)
            dk_acc[:n_keys, :] += _dot_tn(dsb, qs)
            dc_acc[:, :n_keys] -= jnp.concatenate(dcs, axis=0)
            dq_ref[rows, :] += _unstack_heads(jnp.dot(dsb, kb, preferred_element_type=F32))
            dcq_ref[rows, :] += _unstack_heads(jnp.broadcast_to(jnp.concatenate(rowsums, axis=0), (2 * ROW_TILE, LANES)))

        first_q = per_kv * jj
        for r in range(per_kv):
            @pl.when(first_q + r < nb)
            def _():
                block(first_q + r, (r + 1) * ROW_TILE, True)

        def rest(i, carry):
            block(i, KV_TILE_BWD, False)
            return carry

        lax.fori_loop(jnp.minimum(first_q + per_kv, nb), nb, rest, 0)
        dk_ref[...] = dk_acc[...].astype(BF16)
        dv_ref[...] = dv_acc[...].astype(BF16)
        dc_ref[0, 0] = dc_acc[...]

    whole = lambda rows: pl.BlockSpec((rows, LANES), lambda hp, jj: (0, hp))
    kv_blk = lambda off: pl.BlockSpec((KV_TILE_BWD, LANES), lambda hp, jj: (jj, off + hp))
    return pl.pallas_call(
        body, name="attn_bwd", grid=(n_pairs, nkb),
        in_specs=[whole(lk), kv_blk(n_pairs), kv_blk(2 * n_pairs), whole(lp), whole(lp), whole(lp),
                  pl.BlockSpec((nb, 1, 2, KV_TILE_BWD), lambda hp, jj: (0, hp, 0, jj))],
        out_specs=[whole(lp), kv_blk(0), kv_blk(0),
                   pl.BlockSpec((1, 1, 2, KV_TILE_BWD), lambda hp, jj: (hp, jj, 0, 0)), whole(lp)],
        out_shape=[jax.ShapeDtypeStruct((lp, ATTN_WIDTH), F32), jax.ShapeDtypeStruct((lk, ATTN_WIDTH), BF16),
                   jax.ShapeDtypeStruct((lk, ATTN_WIDTH), BF16),
                   jax.ShapeDtypeStruct((n_pairs, nkb, 2, KV_TILE_BWD), F32),
                   jax.ShapeDtypeStruct((lp, ATTN_WIDTH), F32)],
        scratch_shapes=[pltpu.VMEM((KV_TILE_BWD, LANES), F32), pltpu.VMEM((KV_TILE_BWD, LANES), F32),
                        pltpu.VMEM((2, KV_TILE_BWD), F32)],
        compiler_params=_params(("parallel", "arbitrary")),
    )(qkv, qkv, qkv, do, lse, delta, bias)


def _gates_bwd(dc, dcq, f, bfg):
    nb = f.shape[0] // ROW_TILE

    def body(dc_ref, dcq_ref, f_ref, b_ref, df_ref, db_ref, carry):
        step = pl.program_id(0)
        i = nb - 1 - step

        @pl.when(step == 0)
        def _():
            carry[...] = jnp.zeros_like(carry)
            db_ref[...] = jnp.zeros_like(db_ref)

        dcb = dc_ref[...]
        lane = lax.broadcasted_iota(jnp.int32, (ROW_TILE, F_COLS), 1)
        for h in range(N_HEADS):
            dcb = dcb + jnp.where(lane == h, dcq_ref[:, HEAD_DIM * h:HEAD_DIM * h + 1], 0.0)
        r_i = lax.broadcasted_iota(jnp.int32, (ROW_TILE, ROW_TILE), 0)
        c_i = lax.broadcasted_iota(jnp.int32, (ROW_TILE, ROW_TILE), 1)
        upper = (c_i >= r_i).astype(F32)
        dlf = jnp.dot(upper, dcb, precision=HIGHEST, preferred_element_type=F32) + carry[...]
        carry[...] = carry[...] + jnp.sum(dcb, axis=0, keepdims=True)
        logit = f_ref[...] + b_ref[...]
        dlogit = jnp.where(_valid_gate_mask(i), dlf * _sigmoid(-logit), 0.0)
        df_ref[...] = dlogit.astype(BF16)
        db_ref[...] += jnp.sum(dlogit, axis=0, keepdims=True)

    blk = pl.BlockSpec((ROW_TILE, F_COLS), lambda s: (nb - 1 - s, 0))
    wide = pl.BlockSpec((ROW_TILE, ATTN_WIDTH), lambda s: (nb - 1 - s, 0))
    one = pl.BlockSpec((1, F_COLS), lambda s: (0, 0))
    return pl.pallas_call(
        body, name="gates_bwd", grid=(nb,),
        in_specs=[blk, wide, blk, one], out_specs=[blk, one],
        out_shape=[jax.ShapeDtypeStruct(f.shape, BF16), jax.ShapeDtypeStruct((1, F_COLS), F32)],
        scratch_shapes=[pltpu.VMEM((1, F_COLS), F32)],
        compiler_params=_params(("arbitrary",)),
    )(dc, dcq, f, bfg)


def _input_bwd(dproj, df, wt_e, wt_qkv, wt_f, x2, metapad, dh1, g1, hs, sc=None):
    nb = x2.shape[0] // ROW_TILE + 1
    n = len(hs)
    sc_in, sc_shapes, sc_sems = _scatter_operands(hs, sc)
    nq = len(sc_in)

    def body(dp_ref, df_ref, we_ref, w_ref, wf_ref, x_ref, mp_ref, dh_ref, g_ref, *rest):
        h_refs, s_ref = rest[:n], (rest[n] if sc is not None else None)
        gx_ref, g0_ref, dg_ref = rest[nq:nq + 3]
        q_refs, sq_ref = rest[nq + 3:nq + 3 + n], (rest[nq + 3 + n] if sc is not None else None)
        sems = rest[2 * nq + 3:]
        i = pl.program_id(0)

        @pl.when(i == 0)
        def _():
            dg_ref[...] = jnp.zeros_like(dg_ref)
            own, sends, _ = _chip_scatter_plan(h_refs, s_ref, q_refs, sq_ref, *sems)
            for cp in own + sends:
                cp.start()

        dhn = (jnp.dot(dp_ref[:, :E_ZA], we_ref[:E_ZA, :], preferred_element_type=F32)
               + jnp.dot(dp_ref[:, E_ZA:MAIN_COLS // 2], w_ref[...], preferred_element_type=F32)
               + jnp.dot(dp_ref[:, MAIN_COLS // 2:], we_ref[E_ZA:, :], preferred_element_type=F32)
               + jnp.dot(df_ref[...], wf_ref[...], preferred_element_type=F32))
        h0 = jnp.where(i == 0, mp_ref[...], x_ref[...])
        r = lax.rsqrt(jnp.mean(h0 * h0, axis=-1, keepdims=True) + RMS_EPS)
        xhat = h0 * r
        dg_ref[...] += jnp.sum(dhn * xhat, axis=0, keepdims=True)
        dxhat = dhn * g_ref[...]
        dh0 = dh_ref[...] + r * (dxhat - xhat * jnp.mean(dxhat * xhat, axis=-1, keepdims=True))
        gx_ref[...] = dh0

        @pl.when(i == 0)
        def _():
            g0_ref[...] = dh0

        @pl.when(i == nb - 1)
        def _():
            own, sends, recvs = _chip_scatter_plan(h_refs, s_ref, q_refs, sq_ref, *sems)
            for cp in recvs:
                cp.wait_recv()
            for cp in sends:
                cp.wait_send()
            for cp in own:
                cp.wait()

    const = lambda shape: pl.BlockSpec(shape, lambda i: (0, 0))
    res = pl.pallas_call(
        body, name="input_bwd", grid=(nb,),
        in_specs=[pl.BlockSpec((ROW_TILE, MAIN_COLS), lambda i: (i, 0)), pl.BlockSpec((ROW_TILE, F_COLS), lambda i: (i, 0)),
                  const((E_COLS, D_MODEL)), const((QKV_COLS, D_MODEL)), const((F_COLS, D_MODEL)),
                  _tokens_spec(), const((ROW_TILE, D_MODEL)),
                  pl.BlockSpec((ROW_TILE, D_MODEL), lambda i: (i, 0)), const((1, D_MODEL))] + [HBM] * nq,
        out_specs=[_tokens_spec(), const((ROW_TILE, D_MODEL)), const((1, D_MODEL))] + [HBM] * nq,
        out_shape=[jax.ShapeDtypeStruct(x2.shape, F32), jax.ShapeDtypeStruct((ROW_TILE, D_MODEL), F32),
                   jax.ShapeDtypeStruct((1, D_MODEL), F32)] + sc_shapes,
        scratch_shapes=sc_sems,
        compiler_params=_params(("arbitrary",), vmem=56 * 1024 * 1024),
    )(dproj, df, wt_e, wt_qkv, wt_f, x2, metapad, dh1, g1, *_in_hbm(*sc_in))
    return res[:3], res[3:3 + n], (res[3 + n] if sc is not None else None)


def _adamw(w, g, m, v, name):
    rows, cols = w.shape
    if rows % 8 == 0:
        tr, tc = _row_tile8(rows), cols
    else:
        tr, tc = rows, (2 * LANES if cols % (2 * LANES) == 0 and rows > 8 else cols)

    def body(w_ref, g_ref, m_ref, v_ref, d_ref, mo_ref, vo_ref):
        g_ = g_ref[...]
        m_new = ADAM_B1 * m_ref[...] + (1.0 - ADAM_B1) * g_
        v_new = ADAM_B2 * v_ref[...] + (1.0 - ADAM_B2) * (g_ * g_)
        m_hat = m_new / (1.0 - ADAM_B1 ** ADAM_STEP)
        v_hat = v_new / (1.0 - ADAM_B2 ** ADAM_STEP)
        d_ref[...] = -ADAM_LR * (m_hat / (jnp.sqrt(v_hat) + ADAM_EPS) + ADAM_WD * w_ref[...])
        mo_ref[...] = m_new
        vo_ref[...] = v_new

    blk = pl.BlockSpec((tr, tc), lambda i, j: (i, j))
    return pl.pallas_call(
        body, name=name, grid=(rows // tr, cols // tc),
        in_specs=[blk] * 4, out_specs=[blk] * 3,
        out_shape=[jax.ShapeDtypeStruct(w.shape, F32)] * 3,
        compiler_params=_params(("parallel", "parallel")),
    )(w, g, m, v)


def _adamw_native(w3, g3, m3, v3, name):
    rows = w3.shape[0]
    tr = rows // 2
    blk = pl.BlockSpec((tr,) + w3.shape[1:], lambda i: (i, 0, 0))
    shape = jax.ShapeDtypeStruct(w3.shape, F32)

    def moments(g_ref, m_ref, v_ref, mo_ref, vo_ref):
        g_ = g_ref[...]
        mo_ref[...] = ADAM_B1 * m_ref[...] + (1.0 - ADAM_B1) * g_
        vo_ref[...] = ADAM_B2 * v_ref[...] + (1.0 - ADAM_B2) * (g_ * g_)

    new_m, new_v = pl.pallas_call(
        moments, name=name + "_moments", grid=(2,), in_specs=[blk] * 3, out_specs=[blk] * 2, out_shape=[shape] * 2,
        compiler_params=_params(("parallel",)),
    )(g3, m3, v3)

    def delta(w_ref, m_ref, v_ref, d_ref):
        m_hat = m_ref[...] / (1.0 - ADAM_B1 ** ADAM_STEP)
        v_hat = v_ref[...] / (1.0 - ADAM_B2 ** ADAM_STEP)
        d_ref[...] = -ADAM_LR * (m_hat / (jnp.sqrt(v_hat) + ADAM_EPS) + ADAM_WD * w_ref[...])

    d = pl.pallas_call(
        delta, name=name + "_delta", grid=(2,), in_specs=[blk] * 3, out_specs=blk, out_shape=shape,
        compiler_params=_params(("parallel",)),
    )(w3, new_m, new_v)
    return d, new_m, new_v


def _row_tile8(rows):
    best = rows
    for t in range(8, 257, 8):
        if rows % t == 0:
            best = t
    return best


def kernel(x, meta_tokens, norm_g, w_in, b_forget, pool_w, pool_scale, w_up_pool, w_up_attn, w_out, final_norm_g, loss_target, m_meta_tokens, m_norm_g, m_w_in, m_b_forget, m_pool_w, m_pool_scale, m_w_up_pool, m_w_up_attn, m_w_out, m_final_norm_g, v_meta_tokens, v_norm_g, v_w_in, v_b_forget, v_pool_w, v_pool_scale, v_w_up_pool, v_w_up_attn, v_w_out, v_final_norm_g):
    seq = x.shape[1]
    assert seq % ROW_TILE == 0 and x.shape[0] == 1
    lp = seq + ROW_TILE
    nb = lp // ROW_TILE
    lk = -(-lp // KV_TILE_BWD) * KV_TILE_BWD
    core = jnp.reshape(lax.axis_index("c"), (1,)).astype(jnp.int32)
    x2 = x[0]
    target = loss_target[0]
    sh_d = D_MODEL // N_CHIPS

    to_rows = lambda a: jnp.transpose(a, (2, 0, 1))
    from_rows = lambda a: jnp.transpose(a, (1, 2, 0))
    gf = final_norm_g.reshape(1, D_MODEL)
    bfg = jnp.pad(b_forget, ((0, 0), (0, F_COLS - N_HEADS)))
    pw_b = pool_w[0].astype(BF16)
    hn, (wg_in, meta_g) = _rmsnorm_fwd(x2, norm_g, lk, [jnp.transpose(w_in[0]).astype(BF16), meta_tokens], [1, 0])
    wt_e, wt_qkv, wt_f = _weight_sections(wg_in)
    meta_full = jnp.transpose(meta_g, (1, 0, 2)).reshape(N_META, D_MODEL)
    metapad = jnp.pad(meta_full, ((PAD_ROWS, 0), (0, 0)))

    tm = _row_tile(lp, 2200)
    e, (wg_up_p, wg_up_a, wg_out) = _mm_nt(
        hn, wt_e, BF16, "in_proj_gates", lp, E_COLS, tm, 512,
        gather=([w_up_pool[0].astype(BF16), w_up_attn[0].astype(BF16), w_out[0].astype(BF16)], [0, 0, 0]))
    wup_p = jnp.transpose(wg_up_p, (1, 0, 2)).reshape(POOL_WIDTH, D_MODEL)
    wup_a = jnp.transpose(wg_up_a, (1, 0, 2)).reshape(ATTN_WIDTH, D_MODEL)
    wout = wg_out.reshape(D_MODEL, D_MODEL)
    qkv = _mm_nt(hn, wt_qkv, BF16, "in_proj_qkv", lk, QKV_COLS, _row_tile(lk, 2600), 512, scale_first=HEAD_DIM ** -0.5)
    f = _mm_nt(hn, wt_f, F32, "in_proj_forget", lp, F_COLS, tm, F_COLS)
    c = _gates_fwd(f, bfg)
    c_t = jnp.transpose(c[:, :N_HEADS])
    c_first = jnp.transpose(c_t[:, ::ROW_TILE]).reshape(nb, N_HEADS // 2, 2, 1)
    c_keys = jnp.pad(c_t, ((0, 0), (0, lk - lp))).reshape(1, N_HEADS // 2, 2, lk)
    bias = jnp.where(jnp.arange(lk) < PAD_ROWS, -NEG, c_keys - c_first)
    o, lse = _attn_fwd(qkv, bias, nb)
    y_pool, merged, y_attn, dh1, loss_part, dgf = _merge_head_fwd_bwd(o, e, pw_b, pool_scale, wup_p, wup_a, wout, x2,
                                                                      metapad, gf, target)

    dap, daa, dproj_half, do, delta, dy_pool = _merge_bwd(dh1, wout, y_pool, y_attn, wup_p, wup_a, e, o)
    dq, dk, dv, dc4, dcq = _attn_bwd(qkv, do, lse, delta, bias, nb)
    dc = jnp.transpose(dc4, (1, 3, 0, 2)).reshape(lk, N_HEADS)[:lp]
    df, db = _gates_bwd(jnp.pad(dc, ((0, 0), (0, F_COLS - N_HEADS))), dcq, f, bfg)
    dproj, dscale, dpw = _pool_bwd(e, dy_pool, pw_b, pool_scale, dproj_half, dq, dk, dv)
    tk = _row_tile(lp, 1100)
    dw_out = _mm_tn(merged, dh1, "grad_w_out", lp, 512, D_MODEL, tk)
    dw_up_p = _mm_tn(y_pool, dap, "grad_w_up_pool", lp, 512, D_MODEL, lp, chunks=N_CHIPS)
    dw_up_a = _mm_tn(y_attn, daa, "grad_w_up_attn", lp, 512, D_MODEL, lp, chunks=N_CHIPS)
    dwt_f = _mm_tn(df, hn, "grad_w_in_forget", lp, F_COLS, D_MODEL, tk)

    def pad8(a):
        return jnp.pad(a, ((0, (-a.shape[0]) % 8), (0, 0)))

    small_parts = [pad8(a) for a in (dgf.reshape(-1, LANES), dscale.reshape(-1, LANES), db, dpw.reshape(-1, LANES),
                                     loss_part)]
    small = jnp.concatenate(small_parts, axis=0)
    soffs = [0]
    for p in small_parts:
        soffs.append(soffs[-1] + p.shape[0])

    gs_rows = [dw_up_p, dw_up_a, dw_out.reshape(N_CHIPS, sh_d, D_MODEL)]
    half = MAIN_COLS // 2
    dwt_a, (*rb_rows, rb_f, sr) = _mm_tn(dproj, hn, "grad_w_in_a", lp, half // 4, D_MODEL, lp, m=half, m_off=0,
                                         swap=(gs_rows + [dwt_f, small], [1, 1, 1, 1, None]))
    *hs_rows, sc = _add_sibling_half(gs_rows, rb_rows, [1, 1, 1], small, sr, core)
    dwt_b, (rb_a,), (qs_rows, sq) = _mm_tn(dproj, hn, "grad_w_in_b", lp, half // 4, D_MODEL, lp, m=half, m_off=4,
                                           swap=([dwt_a], [1]), scatter=(hs_rows, sc))
    h_in = _add_halves_w_in([dwt_a, dwt_b, dwt_f], [rb_a, rb_f], core)
    grad_axes = [2, 1, 1, 1]
    (grad_x, g_block0, dg1), (q_in,), _ = _input_bwd(dproj, df, wt_e, wt_qkv, wt_f, x2, metapad, dh1, norm_g, [h_in])
    late_parts = jnp.concatenate([pad8(dg1.reshape(-1, LANES)), g_block0[PAD_ROWS:].reshape(-1, LANES)], axis=0)
    g_w_in_rows, g_w_up_p, g_w_up_a, g_w_out, st, late = _reduce_allgather(
        [q_in] + list(qs_rows), grad_axes, sq, [True, False, False, False], late_parts)
    n_norm = D_MODEL // LANES
    g_norm = late[:n_norm].reshape(1, D_MODEL)
    chip = 2 * lax.axis_index("x") + lax.axis_index("y")
    g_meta = lax.dynamic_slice_in_dim(late[n_norm:].reshape(N_META, D_MODEL), chip * sh_d, sh_d, axis=1)

    spiece = lambda k, rows: st[soffs[k]:soffs[k] + rows]
    g_final = spiece(0, D_MODEL // LANES).reshape(1, D_MODEL)
    g_scale = spiece(1, POOL_WIDTH // LANES).reshape(1, POOL_WIDTH)
    g_bf = spiece(2, 1)
    g_pw = spiece(3, POOL_WIDTH)
    loss = st[soffs[4], 0]

    def pad_lanes(a):
        return jnp.pad(a, ((0, 0), (0, F_COLS - N_HEADS)))

    w_in_res = (g_w_in_rows,) + _adamw_native(to_rows(w_in), g_w_in_rows, to_rows(m_w_in), to_rows(v_w_in), "adamw_w_in")

    upd = [
        ("meta_tokens", meta_tokens, g_meta, m_meta_tokens, v_meta_tokens),
        ("norm_g", norm_g, g_norm, m_norm_g, v_norm_g),
        ("w_in", None, None, None, None),
        ("b_forget", pad_lanes(b_forget), g_bf, pad_lanes(m_b_forget), pad_lanes(v_b_forget)),
        ("pool_w", pool_w.reshape(-1, LANES), g_pw, m_pool_w.reshape(-1, LANES), v_pool_w.reshape(-1, LANES)),
        ("pool_scale", pool_scale, g_scale, m_pool_scale, v_pool_scale),
        ("w_up_pool", w_up_pool[0], g_w_up_p, m_w_up_pool[0], v_w_up_pool[0]),
        ("w_up_attn", w_up_attn[0], g_w_up_a, m_w_up_attn[0], v_w_up_attn[0]),
        ("w_out", w_out[0], g_w_out, m_w_out[0], v_w_out[0]),
        ("final_norm_g", gf, g_final, m_final_norm_g.reshape(1, D_MODEL), v_final_norm_g.reshape(1, D_MODEL)),
    ]
    shapes = [meta_tokens.shape, norm_g.shape, w_in.shape, b_forget.shape, pool_w.shape, pool_scale.shape,
              w_up_pool.shape, w_up_attn.shape, w_out.shape, final_norm_g.shape]
    grads, deltas, new_ms, new_vs = [], [], [], []
    for (name, w_, g_, m_in, v_in), shp in zip(upd, shapes):
        if name == "w_in":
            res = tuple(from_rows(a) for a in w_in_res)
        else:
            d_, mn_, vn_ = _adamw(w_, g_, m_in, v_in, "adamw_" + name)
            res = (g_, d_, mn_, vn_)
        if name == "b_forget":
            res = tuple(a[:, :N_HEADS] for a in res)
        for lst, a in zip((grads, deltas, new_ms, new_vs), res):
            lst.append(a.reshape(shp))

    return (loss, grad_x.reshape(x.shape), *grads, *deltas, *new_ms, *new_vs)
```

```python
import jax
import jax.numpy as jnp
from jax import lax
from jax.experimental import pallas as pl
from jax.experimental.pallas import tpu as pltpu

F32 = jnp.float32
BF16 = jnp.bfloat16
MESH = pl.DeviceIdType.MESH
HIGHEST = lax.Precision.HIGHEST
HBM = pl.BlockSpec(memory_space=pltpu.HBM)

D_MODEL = 1024
N_META = 16
POOL_WIDTH = 512
POOL_GROUP = 128
POOL_WINDOWS = (2, 4, 8, 16)
N_HEADS = 8
HEAD_DIM = 64
ATTN_WIDTH = 512
RMS_EPS = 1e-6
N_CHIPS = 4

ADAM_LR = 0.001
ADAM_B1 = 0.9
ADAM_B2 = 0.999
ADAM_EPS = 1e-08
ADAM_WD = 0.01
ADAM_STEP = 10

LANES = 128
ROW_TILE = 256
KV_TILE = 1024
KV_TILE_BWD = 1024
PAD_ROWS = ROW_TILE - N_META
NEG = -1e30

E_U, E_ZP, E_ZA, E_GP, E_GA, E_COLS = 0, 512, 1024, 1536, 2560, 3584
QKV_COLS = 3 * ATTN_WIDTH
MAIN_COLS = E_COLS + QKV_COLS
F_COLS = LANES
REF_U, REF_Q, REF_ZA, REF_F, REF_GP = 0, 1024, 2560, 3072, 3080
VMEM_LIMIT = 48 * 1024 * 1024


def _params(sem=None, vmem=VMEM_LIMIT):
    return pltpu.CompilerParams(dimension_semantics=sem, vmem_limit_bytes=vmem)


def _in_hbm(*arrays):
    return [pltpu.with_memory_space_constraint(a, pltpu.HBM) for a in arrays]


def _row_tile(n, target):
    best = 16
    for t in range(16, target + 1, 16):
        if n % t == 0:
            best = t
    return best


def _sigmoid(x):
    return 1.0 / (1.0 + jnp.exp(-x))


def _half(ref, axis, which):
    n = ref.shape[axis] // 2
    idx = [slice(None)] * len(ref.shape)
    idx[axis] = pl.ds(pl.multiple_of(which * n, n), n)
    return ref.at[tuple(idx)]


def _half_shape(shape, axis):
    s = list(shape)
    s[axis] //= 2
    return tuple(s)


def _gather_start(ins, outs, axes, send, recv, fsend, frecv, local):
    x, y, c = lax.axis_index("x"), lax.axis_index("y"), lax.axis_index("c")
    me = 2 * x + y
    for t in range(len(ins)):
        pltpu.make_async_copy(ins[t], outs[t].at[me], local.at[t]).start()
        for k, chip in enumerate([(1 - x, y), (x, 1 - y), (1 - x, 1 - y)]):
            pltpu.make_async_remote_copy(
                src_ref=_half(ins[t], axes[t], c), dst_ref=_half(outs[t].at[me], axes[t], c),
                send_sem=send.at[3 * t + k], recv_sem=recv.at[3 * t + k], device_id=(*chip, c), device_id_type=MESH).start()


def _gather_finish(ins, outs, axes, send, recv, fsend, frecv, local):
    x, y, c = lax.axis_index("x"), lax.axis_index("y"), lax.axis_index("c")
    me = 2 * x + y
    sibling = (x, y, 1 - c)
    chips = [(1 - x, y), (x, 1 - y), (1 - x, 1 - y)]
    n = len(ins)

    def over_ici(t, k, chip, dst_slot):
        return pltpu.make_async_remote_copy(
            src_ref=_half(ins[t], axes[t], c), dst_ref=_half(outs[t].at[dst_slot], axes[t], c),
            send_sem=send.at[3 * t + k], recv_sem=recv.at[3 * t + k], device_id=(*chip, c), device_id_type=MESH)

    def to_sibling(t, k, slot, which):
        return pltpu.make_async_remote_copy(
            src_ref=_half(outs[t].at[slot], axes[t], which), dst_ref=_half(outs[t].at[slot], axes[t], which),
            send_sem=fsend.at[3 * t + k], recv_sem=frecv.at[3 * t + k], device_id=sibling, device_id_type=MESH)

    forwards = []
    for t in range(n):
        for k, (px, py) in enumerate(chips):
            over_ici(t, k, (px, py), 2 * px + py).wait_recv()
            fw = to_sibling(t, k, 2 * px + py, c)
            fw.start()
            forwards.append(fw)
    for t in range(n):
        for k, (px, py) in enumerate(chips):
            to_sibling(t, k, 2 * px + py, 1 - c).wait_recv()
    for t in range(n):
        for k, chip in enumerate(chips):
            over_ici(t, k, chip, me).wait_send()
    for fw in forwards:
        fw.wait_send()
    for t in range(n):
        pltpu.make_async_copy(ins[t], outs[t].at[me], local.at[t]).wait()


def _gather_sems(n):
    return [pltpu.SemaphoreType.DMA((3 * n,)), pltpu.SemaphoreType.DMA((3 * n,)), pltpu.SemaphoreType.DMA((3 * n,)),
            pltpu.SemaphoreType.DMA((3 * n,)), pltpu.SemaphoreType.DMA((n,))]


def _gathered_shapes(shards):
    return [jax.ShapeDtypeStruct((N_CHIPS,) + s.shape, s.dtype) for s in shards]


def _swap_copies(srcs, dsts, axes, send, recv):
    x, y, c = lax.axis_index("x"), lax.axis_index("y"), lax.axis_index("c")
    return [pltpu.make_async_remote_copy(
        src_ref=srcs[t] if axes[t] is None else _half(srcs[t], axes[t], 1 - c), dst_ref=dsts[t],
        send_sem=send.at[t], recv_sem=recv.at[t], device_id=(x, y, 1 - c), device_id_type=MESH) for t in range(len(srcs))]


def _swap_shapes(srcs, axes):
    return [jax.ShapeDtypeStruct(g.shape if a is None else _half_shape(g.shape, a), g.dtype) for g, a in zip(srcs, axes)]


def _add_halves_w_in(parts, rbs, core):
    na, nb_rows, w = rbs[0].shape[0], parts[1].shape[0], rbs[0].shape[1]
    per = (na + nb_rows + N_HEADS) // N_CHIPS
    pieces = [[] for _ in range(N_CHIPS)]
    r = 0
    for src, lo, hi in ((0, 0, na), (1, 0, ATTN_WIDTH), (2, 0, N_HEADS), (1, ATTN_WIDTH, nb_rows)):
        while lo < hi:
            c, at = divmod(r, per)
            n = min(hi - lo, per - at)
            pieces[c].append((src, lo, n, at))
            lo, r = lo + n, r + n

    def body(core_ref, a_ref, b_ref, f_ref, ra_ref, rf_ref, o_ref, am, bm, fm, ar, br, fr, rows_buf, sems, send, recv):
        x, y, core_id = lax.axis_index("x"), lax.axis_index("y"), lax.axis_index("c")
        swap = pltpu.make_async_remote_copy(src_ref=_half(b_ref, 1, 1 - core_id), dst_ref=br, send_sem=send.at[0],
                                            recv_sem=recv.at[0], device_id=(x, y, 1 - core_id), device_id_type=MESH)
        swap.start()
        cols = pl.ds(pl.multiple_of(core_ref[0] * w, w), w)
        few = pl.ds(0, N_HEADS)
        loads = [(pltpu.make_async_copy(a_ref.at[:, cols], am, sems.at[0]), pltpu.make_async_copy(ra_ref, ar, sems.at[1])),
                 (pltpu.make_async_copy(b_ref.at[:, cols], bm, sems.at[2]),),
                 (pltpu.make_async_copy(f_ref.at[few, cols], fm, sems.at[3]),
                  pltpu.make_async_copy(rf_ref.at[few, :], fr, sems.at[4]))]
        for group in loads:
            for cp in group:
                cp.start()
        bufs = [(am, ar), (bm, br), (fm, fr)]
        there = set()
        for c in range(N_CHIPS):
            for src, lo, n, at in pieces[c]:
                if src not in there:
                    for cp in loads[src]:
                        cp.wait()
                    if src == 1:
                        swap.wait_recv()
                    there.add(src)
                mine, other = bufs[src]
                rows_buf[at:at + n, :] = mine[lo:lo + n, :] + other[lo:lo + n, :]
            o_ref[c] = rows_buf[0:per, :].astype(BF16)
        swap.wait_send()

    any_space = pl.BlockSpec(memory_space=pl.ANY)
    a, b, f = parts
    return pl.pallas_call(
        body, name="grad_add_sibling_w_in",
        grid_spec=pltpu.PrefetchScalarGridSpec(
            num_scalar_prefetch=1, grid=(1,),
            in_specs=[any_space, HBM, any_space, any_space, any_space],
            out_specs=pl.BlockSpec((N_CHIPS, per, w), lambda i, cr: (0, 0, 0)),
            scratch_shapes=[pltpu.VMEM((na, w), F32), pltpu.VMEM((nb_rows, w), F32), pltpu.VMEM((N_HEADS, w), F32),
                            pltpu.VMEM((na, w), F32), pltpu.VMEM((nb_rows, w), F32), pltpu.VMEM((N_HEADS, w), F32),
                            pltpu.VMEM((per + (-per) % 8, w), F32), pltpu.SemaphoreType.DMA((5,)),
                            pltpu.SemaphoreType.DMA((1,)), pltpu.SemaphoreType.DMA((1,))]),
        out_shape=jax.ShapeDtypeStruct((N_CHIPS, per, w), BF16),
        compiler_params=_params(("arbitrary",)),
    )(core, a, *_in_hbm(b), f, *rbs)


def _weight_sections(wg):
    per, d = wg.shape[1], wg.shape[2]
    n_ref = N_CHIPS * per
    units = [(0, 0, REF_U, REF_Q), (1, 0, REF_Q, REF_ZA), (0, REF_Q - REF_U, REF_ZA, REF_F), (2, 0, REF_F, REF_GP),
             (0, REF_Q - REF_U + REF_F - REF_ZA, REF_GP, n_ref)]
    plan = []
    for out, first, lo, hi in sorted(units, key=lambda u: u[2]):
        pieces, at = [], 0
        while lo < hi:
            c, src = divmod(lo, per)
            n = min(hi - lo, per - src)
            pieces.append((c, src, n, at))
            lo, at = lo + n, at + n
        plan.append((out, first, at, pieces))
    stage_rows = max(rows for _, _, rows, _ in plan)

    def body(g_ref, e_ref, q_ref, f_ref, gb, stage, sems):
        loads = [pltpu.make_async_copy(g_ref.at[c], gb.at[c], sems.at[c]) for c in range(N_CHIPS)]
        for cp in loads:
            cp.start()
        outs = [e_ref, q_ref, f_ref]
        there = set()
        for out, first, rows, pieces in plan:
            for c, _, _, _ in pieces:
                if c not in there:
                    loads[c].wait()
                    there.add(c)
            if len(pieces) == 1 and pieces[0][1] % 16 == 0 and rows % 16 == 0:
                c, src, _, _ = pieces[0]
                outs[out][first:first + rows, :] = gb[c, src:src + rows, :]
                continue
            for c, src, n, at in pieces:
                stage[at:at + n, :] = gb[c, src:src + n, :].astype(F32)
            padded = rows if rows % 16 == 0 else outs[out].shape[0] - first
            if padded > rows:
                stage[rows:padded, :] = jnp.zeros((padded - rows, d), F32)
            outs[out][first:first + padded, :] = stage[0:padded, :].astype(BF16)

    return pl.pallas_call(
        body, name="weight_sections",
        in_specs=[pl.BlockSpec(memory_space=pl.ANY)],
        out_shape=[jax.ShapeDtypeStruct((E_COLS, d), BF16), jax.ShapeDtypeStruct((QKV_COLS, d), BF16),
                   jax.ShapeDtypeStruct((F_COLS, d), BF16)],
        scratch_shapes=[pltpu.VMEM(wg.shape, BF16), pltpu.VMEM((stage_rows, d), F32), pltpu.SemaphoreType.DMA((N_CHIPS,))],
        compiler_params=_params(),
    )(wg)


def _add_sibling_half(gs, rbs, axes, small, sr, core):
    n = len(gs)

    def body(core_ref, *refs):
        g_refs, rb_refs = refs[:n], refs[n:2 * n]
        s_ref, sr_ref = refs[2 * n], refs[2 * n + 1]
        h_refs, sc_ref = refs[2 * n + 2:3 * n + 2], refs[3 * n + 2]
        for t in range(n):
            h_refs[t][...] = (g_refs[t][...] + rb_refs[t][...]).astype(BF16)
        sc_ref[...] = s_ref[...] + sr_ref[...]

    def mine(rb, axis):
        blk = (None,) + rb.shape[1:]
        if axis == 2:
            return pl.BlockSpec(blk, lambda j, cr: (j, 0, cr[0]))
        return pl.BlockSpec(blk, lambda j, cr: (j, cr[0], 0))

    chunk = lambda rb: pl.BlockSpec((None,) + rb.shape[1:], lambda j, cr: (j, 0, 0))
    whole = pl.BlockSpec(small.shape, lambda j, cr: (0, 0))
    return pl.pallas_call(
        body, name="grad_add_sibling",
        grid_spec=pltpu.PrefetchScalarGridSpec(
            num_scalar_prefetch=1, grid=(N_CHIPS,),
            in_specs=[mine(rb, a) for rb, a in zip(rbs, axes)] + [chunk(rb) for rb in rbs] + [whole, whole],
            out_specs=[chunk(rb) for rb in rbs] + [whole]),
        out_shape=[jax.ShapeDtypeStruct(rb.shape, BF16) for rb in rbs] + [jax.ShapeDtypeStruct(small.shape, F32)],
        compiler_params=_params(("arbitrary",)),
    )(core, *gs, *rbs, small, sr)


def _chip_scatter_plan(h_refs, s_ref, q_refs, sq_ref, send, recv, local):
    n = len(h_refs)
    nt = n + (s_ref is not None)
    x, y, c = lax.axis_index("x"), lax.axis_index("y"), lax.axis_index("c")
    me = 2 * x + y
    chips = [(1 - x, y), (x, 1 - y), (1 - x, 1 - y)]

    def copy(t, k, chip, src_slot, dst_slot):
        src = h_refs[t].at[src_slot] if t < n else s_ref
        dst = (q_refs[t] if t < n else sq_ref).at[dst_slot]
        return pltpu.make_async_remote_copy(src_ref=src, dst_ref=dst, send_sem=send.at[3 * t + k],
                                            recv_sem=recv.at[3 * t + k], device_id=(*chip, c), device_id_type=MESH)

    own = [pltpu.make_async_copy(h_refs[t].at[me], q_refs[t].at[me], local.at[t]) for t in range(n)]
    if s_ref is not None:
        own.append(pltpu.make_async_copy(s_ref, sq_ref.at[me], local.at[n]))
    sends = [copy(t, k, (px, py), 2 * px + py, me) for t in range(nt) for k, (px, py) in enumerate(chips)]
    recvs = [copy(t, k, (px, py), me, 2 * px + py) for t in range(nt) for k, (px, py) in enumerate(chips)]
    return own, sends, recvs


def _scatter_operands(hs, sc):
    small = [] if sc is None else [sc]
    nt = len(hs) + len(small)
    shapes = [jax.ShapeDtypeStruct(h.shape, h.dtype) for h in hs] + [
        jax.ShapeDtypeStruct((N_CHIPS,) + a.shape, a.dtype) for a in small]
    sems = [pltpu.SemaphoreType.DMA((3 * nt,)), pltpu.SemaphoreType.DMA((3 * nt,)), pltpu.SemaphoreType.DMA((nt,))]
    return list(hs) + small, shapes, sems


def _reduce_allgather(qs, axes, sq, as_rows, late):
    n = len(qs)
    shard_shapes, half_axes = [], []
    for q, a, rows_form in zip(qs, axes, as_rows):
        shape = [d * 2 if i == a - 1 else d for i, d in enumerate(q.shape[1:])]
        if rows_form:
            assert a == 2
            shape = [shape[0], 1, shape[1]]
        shard_shapes.append(tuple(shape))
        half_axes.append(2 if rows_form else a - 1)

    def body(*refs):
        q_refs, sq_ref, late_ref = refs[:n], refs[n], refs[n + 1]
        o_refs, st_ref, late_sum_ref = refs[n + 2:2 * n + 2], refs[2 * n + 2], refs[2 * n + 3]
        sib_buf, chip_buf, send, recv, late_send, late_recv = refs[2 * n + 4:]
        x, y, c = lax.axis_index("x"), lax.axis_index("y"), lax.axis_index("c")
        me = 2 * x + y
        chips = [(1 - x, y), (x, 1 - y), (1 - x, 1 - y)]

        def swap(t, which):
            return pltpu.make_async_remote_copy(
                src_ref=_half(o_refs[t], half_axes[t], which), dst_ref=_half(o_refs[t], half_axes[t], which),
                send_sem=send.at[t], recv_sem=recv.at[t], device_id=(x, y, 1 - c), device_id_type=MESH)

        def late_copy(k, chip, slot):
            return pltpu.make_async_remote_copy(src_ref=chip_buf.at[slot], dst_ref=chip_buf.at[slot], send_sem=late_send.at[1 + k],
                                                recv_sem=late_recv.at[1 + k], device_id=(*chip, c), device_id_type=MESH)

        late_swap = pltpu.make_async_remote_copy(src_ref=late_ref, dst_ref=sib_buf, send_sem=late_send.at[0],
                                                 recv_sem=late_recv.at[0], device_id=(x, y, 1 - c), device_id_type=MESH)
        late_swap.start()
        sent = []
        for t in range(n):
            q = q_refs[t]
            total = ((q[0].astype(F32) + q[1].astype(F32)) + q[2].astype(F32)) + q[3].astype(F32)
            if as_rows[t]:
                total = total.reshape(total.shape[0], 1, total.shape[1])
            _half(o_refs[t], half_axes[t], c)[...] = total
            cp = swap(t, c)
            cp.start()
            sent.append(cp)
            if t == 0:
                late_swap.wait()
                chip_buf[me] = late_ref[...] + sib_buf[...]
                late_sends = [late_copy(k, chip, me) for k, chip in enumerate(chips)]
                for lc in late_sends:
                    lc.start()
        st_ref[...] = ((sq_ref[0] + sq_ref[1]) + sq_ref[2]) + sq_ref[3]
        for t in range(n):
            swap(t, 1 - c).wait_recv()
        for cp in sent:
            cp.wait_send()
        for k, (px, py) in enumerate(chips):
            late_copy(k, (px, py), 2 * px + py).wait_recv()
        for lc in late_sends:
            lc.wait_send()
        late_sum_ref[...] = ((chip_buf[0] + chip_buf[1]) + chip_buf[2]) + chip_buf[3]

    vmem = pl.BlockSpec(memory_space=pltpu.VMEM)
    return pl.pallas_call(
        body, name="grad_reduce_allgather",
        in_specs=[vmem] * (n + 2), out_specs=[vmem] * (n + 2),
        out_shape=[jax.ShapeDtypeStruct(s, F32) for s in shard_shapes] + [jax.ShapeDtypeStruct(sq.shape[1:], F32),
                                                                         jax.ShapeDtypeStruct(late.shape, F32)],
        scratch_shapes=[pltpu.VMEM(late.shape, F32), pltpu.VMEM((N_CHIPS,) + late.shape, F32),
                        pltpu.SemaphoreType.DMA((n,)), pltpu.SemaphoreType.DMA((n,)),
                        pltpu.SemaphoreType.DMA((4,)), pltpu.SemaphoreType.DMA((4,))],
        compiler_params=_params(),
    )(*qs, sq, late)


def _dot_nt(a, b):
    return lax.dot_general(a, b, (((1,), (1,)), ((), ())), preferred_element_type=F32)


def _dot_tn(a, b):
    return lax.dot_general(a, b, (((0,), (0,)), ((), ())), preferred_element_type=F32)


def _mm_nt(a, bt, out_dtype, name, m, n, tm, tn, row_block=0, scale_first=None, gather=None):
    k = a.shape[1]
    shards, axes = gather if gather is not None else ([], [])
    ng = len(shards)
    grid = (m // tm, n // tn)

    def body(a_ref, b_ref, *rest):
        ins, o_ref, outs, sems = rest[:ng], rest[ng], rest[ng + 1:2 * ng + 1], rest[2 * ng + 1:]
        first = (pl.program_id(0) == 0) & (pl.program_id(1) == 0)
        last = (pl.program_id(0) == grid[0] - 1) & (pl.program_id(1) == grid[1] - 1)
        if ng:
            @pl.when(first)
            def _():
                _gather_start(ins, outs, axes, *sems)

        r = _dot_nt(a_ref[...], b_ref[...])
        if scale_first is not None:
            r = r * jnp.where(pl.program_id(1) == 0, scale_first, 1.0)
        o_ref[...] = r.astype(out_dtype)

        if ng:
            @pl.when(last)
            def _():
                _gather_finish(ins, outs, axes, *sems)

    res = pl.pallas_call(
        body, name=name, grid=grid,
        in_specs=[pl.BlockSpec((tm, k), lambda i, j: (i, 0)), pl.BlockSpec((tn, k), lambda i, j: (row_block + j, 0))]
        + [HBM] * ng,
        out_specs=[pl.BlockSpec((tm, tn), lambda i, j: (i, j))] + [HBM] * ng,
        out_shape=[jax.ShapeDtypeStruct((m, n), out_dtype)] + _gathered_shapes(shards),
        scratch_shapes=_gather_sems(ng) if ng else [],
        compiler_params=_params(("arbitrary", "arbitrary") if ng else ("parallel", "parallel")),
    )(a, bt, *_in_hbm(*shards))
    return (res[0], res[1:]) if ng else res[0]


def _mm_tn(a, b, name, k, tm, tn, tk, chunks=1, m=None, m_off=0, swap=None, scatter=None):
    m = a.shape[1] if m is None else m
    n = b.shape[1]
    cw = n // chunks
    srcs, axes = swap if swap is not None else ([], [])
    ns = len(srcs)
    hs, sc = scatter if scatter is not None else ([], None)
    sc_in, sc_shapes, sc_sems = _scatter_operands(hs, sc) if scatter is not None else ([], [], [])
    nh, nq = len(hs), len(sc_in)
    grid = (m // tm, n // tn, k // tk)

    def body(a_ref, b_ref, *rest):
        s_refs, h_refs = rest[:ns], rest[ns:ns + nq]
        o_ref = rest[ns + nq]
        d_refs, q_refs = rest[ns + nq + 1:2 * ns + nq + 1], rest[2 * ns + nq + 1:2 * ns + 2 * nq + 1]
        sems = rest[2 * ns + 2 * nq + 1:]
        swap_sems, scatter_sems = (sems[:2], sems[2:]) if ns else ((), sems)
        ids = [pl.program_id(d) for d in range(3)]

        def scatter_plan():
            small_in = h_refs[nh] if sc is not None else None
            small_out = q_refs[nh] if sc is not None else None
            return _chip_scatter_plan(h_refs[:nh], small_in, q_refs[:nh], small_out, *scatter_sems)

        if ns or nq:
            @pl.when((ids[0] == 0) & (ids[1] == 0) & (ids[2] == 0))
            def _():
                if ns:
                    for cp in _swap_copies(s_refs, d_refs, axes, *swap_sems):
                        cp.start()
                if nq:
                    own, sends, _ = scatter_plan()
                    for cp in own + sends:
                        cp.start()

        @pl.when(ids[2] == 0)
        def _():
            o_ref[...] = jnp.zeros_like(o_ref)
        r = _dot_tn(a_ref[...].astype(BF16), b_ref[...].astype(BF16))
        if chunks > 1:
            for c in range(chunks):
                o_ref[c] += r[:, c * cw:(c + 1) * cw]
        else:
            o_ref[...] += r

        if ns or nq:
            @pl.when((ids[0] == grid[0] - 1) & (ids[1] == grid[1] - 1) & (ids[2] == grid[2] - 1))
            def _():
                if ns:
                    for cp in _swap_copies(s_refs, d_refs, axes, *swap_sems):
                        cp.wait()
                if nq:
                    own, sends, recvs = scatter_plan()
                    for cp in recvs:
                        cp.wait_recv()
                    for cp in sends:
                        cp.wait_send()
                    for cp in own:
                        cp.wait()

    if chunks > 1:
        assert tn == n
        out_spec = pl.BlockSpec((chunks, tm, cw), lambda i, j, kk: (0, i, 0))
        out_shape = jax.ShapeDtypeStruct((chunks, m, cw), F32)
    else:
        out_spec = pl.BlockSpec((tm, tn), lambda i, j, kk: (i, j))
        out_shape = jax.ShapeDtypeStruct((m, n), F32)
    riders = ns + nq
    res = pl.pallas_call(
        body, name=name, grid=grid,
        in_specs=[pl.BlockSpec((tk, tm), lambda i, j, kk: (kk, m_off + i)), pl.BlockSpec((tk, tn), lambda i, j, kk: (kk, j))]
        + [HBM] * riders,
        out_specs=[out_spec] + [HBM] * riders, out_shape=[out_shape] + _swap_shapes(srcs, axes) + sc_shapes,
        scratch_shapes=([pltpu.SemaphoreType.DMA((ns,)), pltpu.SemaphoreType.DMA((ns,))] if ns else []) + sc_sems,
        compiler_params=_params(("arbitrary",) * 3 if riders else ("parallel", "parallel", "arbitrary")),
    )(a, b, *_in_hbm(*srcs, *sc_in))
    if not riders:
        return res[0]
    out = [res[0]]
    if ns:
        out.append(res[1:1 + ns])
    if nq:
        out.append((res[1 + ns:1 + ns + nh], res[1 + ns + nh] if sc is not None else None))
    return tuple(out)


def _tokens_spec():
    return pl.BlockSpec((ROW_TILE, D_MODEL), lambda i: (jnp.maximum(i - 1, 0), 0))


def _rmsnorm_fwd(x2, g1, lk, shards, axes):
    nb = x2.shape[0] // ROW_TILE + 1
    nblk = lk // ROW_TILE
    ng = len(shards)
    meta_cols = shards[-1].shape[1]

    def body(x_ref, g_ref, *rest):
        ins, hn_ref, outs = rest[:ng], rest[ng], rest[ng + 1:2 * ng + 1]
        sems, meta_buf, meta_sem = rest[2 * ng + 1:2 * ng + 6], rest[2 * ng + 6], rest[2 * ng + 7]
        s = pl.program_id(0)
        blk = (s + 1) % nblk

        @pl.when(s == 0)
        def _():
            _gather_start(ins, outs, axes, *sems)

        def normed(h):
            r = lax.rsqrt(jnp.mean(h * h, axis=-1, keepdims=True) + RMS_EPS)
            return ((h * r) * g_ref[...]).astype(BF16)

        @pl.when(s < nblk - 1)
        def _():
            hn_ref[...] = normed(jnp.where(blk >= nb, 0.0, x_ref[...]))

        @pl.when(s == nblk - 1)
        def _():
            _gather_finish(ins, outs, axes, *sems)
            fetch = pltpu.make_async_copy(outs[-1], meta_buf, meta_sem.at[0])
            fetch.start()
            fetch.wait()
            meta = jnp.concatenate([meta_buf[j] for j in range(N_CHIPS)], axis=1)
            hn_ref[...] = normed(jnp.concatenate([jnp.zeros((PAD_ROWS, D_MODEL), F32), meta], axis=0))

    res = pl.pallas_call(
        body, name="rmsnorm_fwd", grid=(nblk,),
        in_specs=[pl.BlockSpec((ROW_TILE, D_MODEL), lambda s: (jnp.clip((s + 1) % nblk - 1, 0, nb - 2), 0)),
                  pl.BlockSpec((1, D_MODEL), lambda s: (0, 0))] + [HBM] * ng,
        out_specs=[pl.BlockSpec((ROW_TILE, D_MODEL), lambda s: ((s + 1) % nblk, 0))] + [HBM] * ng,
        out_shape=[jax.ShapeDtypeStruct((lk, D_MODEL), BF16)] + _gathered_shapes(shards),
        scratch_shapes=_gather_sems(ng) + [pltpu.VMEM((N_CHIPS, N_META, meta_cols), F32), pltpu.SemaphoreType.DMA((1,))],
        compiler_params=_params(("arbitrary",)),
    )(x2, g1, *_in_hbm(*shards))
    return res[0], res[1:]


def _valid_gate_mask(i):
    row = i * ROW_TILE + lax.broadcasted_iota(jnp.int32, (ROW_TILE, F_COLS), 0)
    col = lax.broadcasted_iota(jnp.int32, (ROW_TILE, F_COLS), 1)
    return (row >= PAD_ROWS) & (col < N_HEADS)


def _gates_fwd(f, bfg):
    nb = f.shape[0] // ROW_TILE

    def body(f_ref, b_ref, c_ref, carry):
        i = pl.program_id(0)

        @pl.when(i == 0)
        def _():
            carry[...] = jnp.zeros_like(carry)

        logit = f_ref[...] + b_ref[...]
        lf = jnp.minimum(logit, 0.0) - jnp.log1p(jnp.exp(-jnp.abs(logit)))
        lf = jnp.where(_valid_gate_mask(i), lf, 0.0)
        r_i = lax.broadcasted_iota(jnp.int32, (ROW_TILE, ROW_TILE), 0)
        c_i = lax.broadcasted_iota(jnp.int32, (ROW_TILE, ROW_TILE), 1)
        tri = (c_i <= r_i).astype(F32)
        c_ref[...] = jnp.dot(tri, lf, precision=HIGHEST, preferred_element_type=F32) + carry[...]
        carry[...] = carry[...] + jnp.sum(lf, axis=0, keepdims=True)

    return pl.pallas_call(
        body, name="gates_fwd", grid=(nb,),
        in_specs=[pl.BlockSpec((ROW_TILE, F_COLS), lambda i: (i, 0)), pl.BlockSpec((1, F_COLS), lambda i: (0, 0))],
        out_specs=pl.BlockSpec((ROW_TILE, F_COLS), lambda i: (i, 0)),
        out_shape=jax.ShapeDtypeStruct(f.shape, F32),
        scratch_shapes=[pltpu.VMEM((1, F_COLS), F32)],
        compiler_params=_params(("arbitrary",)),
    )(f, bfg)


def _pool_counts(i):
    row = i * ROW_TILE + lax.broadcasted_iota(jnp.int32, (ROW_TILE, 1), 0)
    return jnp.maximum(row - PAD_ROWS, 0)


def _trailing_sums(xc, levels):
    acc = xc
    for lv in range(levels):
        acc = acc + pltpu.roll(acc, 1 << lv, 0)
    return acc


def _leading_sums(xc, levels):
    n = xc.shape[0]
    acc = xc
    for lv in range(levels):
        acc = acc + pltpu.roll(acc, n - (1 << lv), 0)
    return acc


def _pool_p(u_cur, u_prev, i):
    pos = _pool_counts(i)
    ps, invs = [], []
    for g, w in enumerate(POOL_WINDOWS):
        sl = slice(g * POOL_GROUP, (g + 1) * POOL_GROUP)
        cur = u_cur[:, sl]
        xc = jnp.concatenate([u_prev[:, sl], cur], axis=0)
        win = _trailing_sums(xc, g + 1)[ROW_TILE:, :]
        inv = 1.0 / jnp.minimum(pos + 1, w).astype(F32)
        ps.append(win * inv - cur)
        invs.append(inv)
    return ps, invs


def _stack_heads(a):
    first = lax.broadcasted_iota(jnp.int32, a.shape, 1) < HEAD_DIM
    zero = jnp.zeros_like(a)
    return jnp.concatenate([jnp.where(first, a, zero), jnp.where(first, zero, a)], axis=0)


def _unstack_heads(a):
    rows = a.shape[0] // 2
    first = lax.broadcasted_iota(jnp.int32, (rows, LANES), 1) < HEAD_DIM
    return jnp.where(first, a[:rows], a[rows:])


def _causal(i, jj, stacked, kv_tile):
    r = lax.broadcasted_iota(jnp.int32, (stacked * ROW_TILE, kv_tile), 0)
    if stacked == 2:
        r = jnp.where(r >= ROW_TILE, r - ROW_TILE, r)
    kidx = jj * kv_tile + lax.broadcasted_iota(jnp.int32, (stacked * ROW_TILE, kv_tile), 1)
    return kidx <= i * ROW_TILE + r


def _stack_rows(b):
    n = b.shape[1]
    return jnp.concatenate([jnp.broadcast_to(b[0:1], (ROW_TILE, n)), jnp.broadcast_to(b[1:2], (ROW_TILE, n))], axis=0)


def _attn_fwd(qkv, bias, nb):
    lk = qkv.shape[0]
    lp = nb * ROW_TILE
    nkb = lk // KV_TILE
    n_pairs = N_HEADS // 2

    def body(q_ref, k_ref, v_ref, b_ref, o_ref, lse_ref):
        i = pl.program_id(1)
        qs = _stack_heads(q_ref[...])
        last = (i * ROW_TILE) // KV_TILE

        def block(jj, carry, masked, n_keys=KV_TILE):
            m, l, acc = carry
            rows = pl.ds(pl.multiple_of(jj * KV_TILE, KV_TILE), n_keys)
            s = _dot_nt(qs, k_ref[rows, :]) - _stack_rows(b_ref[0, 0, :, rows])
            if masked:
                s = jnp.where(_causal(i, jj, 2, KV_TILE)[:, :n_keys], s, NEG)
            m_new = jnp.maximum(m, jnp.max(s, axis=1, keepdims=True))
            alpha = jnp.exp(m - m_new)
            p = jnp.exp(s - m_new)
            l = alpha * l + jnp.sum(p, axis=1, keepdims=True)
            acc = alpha * acc + jnp.dot(p.astype(BF16), v_ref[rows, :], preferred_element_type=F32)
            return m_new, l, acc

        init = (jnp.full((2 * ROW_TILE, 1), NEG, F32), jnp.zeros((2 * ROW_TILE, 1), F32),
                jnp.zeros((2 * ROW_TILE, LANES), F32))
        def finish(carry):
            m, l, acc = carry
            o_ref[...] = _unstack_heads(acc / l)
            lse_ref[...] = _unstack_heads(jnp.broadcast_to(m + jnp.log(l), (2 * ROW_TILE, LANES)))

        carry = lax.fori_loop(0, last, lambda jj, c: block(jj, c, False), init)
        ends = [lambda c, n=(r + 1) * ROW_TILE: finish(block(last, c, True, n)) for r in range(KV_TILE // ROW_TILE)]
        lax.switch(i - last * (KV_TILE // ROW_TILE), ends, carry)

    return pl.pallas_call(
        body, name="attn_fwd", grid=(n_pairs, nb),
        in_specs=[pl.BlockSpec((ROW_TILE, LANES), lambda hp, i: (i, hp)),
                  pl.BlockSpec((lk, LANES), lambda hp, i: (0, n_pairs + hp)),
                  pl.BlockSpec((lk, LANES), lambda hp, i: (0, 2 * n_pairs + hp)),
                  pl.BlockSpec((1, 1, 2, lk), lambda hp, i: (i, hp, 0, 0))],
        out_specs=[pl.BlockSpec((ROW_TILE, LANES), lambda hp, i: (i, hp)),
                   pl.BlockSpec((ROW_TILE, LANES), lambda hp, i: (i, hp))],
        out_shape=[jax.ShapeDtypeStruct((lp, ATTN_WIDTH), F32), jax.ShapeDtypeStruct((lp, ATTN_WIDTH), F32)],
        compiler_params=_params(("parallel", "parallel")),
    )(qkv, qkv, qkv, bias)


def _merge_head_fwd_bwd(o, e, pw, scale, wup_p, wup_a, w_out, x2, metapad, gf, target):
    lp = o.shape[0]
    nb = lp // ROW_TILE

    def body(up_ref, o_ref, e_ref, pw_ref, sc_ref, wp_ref, wa_ref, w_ref, x_ref, mp_ref, g_ref, t_ref,
             yp_ref, mg_ref, ya_ref, dh_ref, loss_ref, dg_ref, dap_ref, daa_ref, dgate_ref, do_ref, delta_ref, dyp_ref):
        i = pl.program_id(0)
        u_cur = e_ref[:, E_U:E_ZP].astype(F32)
        u_prev = jnp.where(i == 0, 0.0, up_ref[...].astype(F32))
        ps, _ = _pool_p(u_cur, u_prev, i)
        zp = e_ref[:, E_ZP:E_ZA].astype(F32)
        gate = zp * _sigmoid(zp)
        for g in range(len(POOL_WINDOWS)):
            sl = slice(g * POOL_GROUP, (g + 1) * POOL_GROUP)
            yraw = jnp.dot(ps[g].astype(BF16), pw_ref[g], preferred_element_type=F32)
            yp_ref[:, sl] = ((yraw * sc_ref[:, sl]) * gate[:, sl]).astype(BF16)

        @pl.when(i == 0)
        def _():
            loss_ref[...] = jnp.zeros_like(loss_ref)
            dg_ref[...] = jnp.zeros_like(dg_ref)

        za = e_ref[:, E_ZA:E_GP].astype(F32)
        sz = _sigmoid(za)
        o = o_ref[...]
        ya = (o * (za * sz)).astype(BF16)
        ya_ref[...] = ya
        a_pool = jnp.dot(yp_ref[...], wp_ref[...], preferred_element_type=F32)
        a_attn = jnp.dot(ya, wa_ref[...], preferred_element_type=F32)
        sp = _sigmoid(e_ref[:, E_GP:E_GA].astype(F32))
        sa = _sigmoid(e_ref[:, E_GA:E_COLS].astype(F32))
        merged = (sp * a_pool + sa * a_attn).astype(BF16)
        mg_ref[...] = merged

        h0 = jnp.where(i == 0, mp_ref[...], x_ref[...])
        h1 = h0 + jnp.dot(merged, w_ref[...], preferred_element_type=F32)
        r = lax.rsqrt(jnp.mean(h1 * h1, axis=-1, keepdims=True) + RMS_EPS)
        xhat = h1 * r
        g = g_ref[...]
        err = jnp.where(i == 0, 0.0, xhat * g - t_ref[...])
        loss_ref[...] += 0.5 * jnp.sum(jnp.mean(err * err, axis=-1, keepdims=True))
        dy = err / D_MODEL
        dg_ref[...] += jnp.sum(dy * xhat, axis=0, keepdims=True)
        dxhat = dy * g
        dh = r * (dxhat - xhat * jnp.mean(dxhat * xhat, axis=-1, keepdims=True))
        dh_ref[...] = dh

        dmerged = _dot_nt(dh.astype(BF16), w_ref[...])
        dap = (dmerged * sp).astype(BF16)
        daa = (dmerged * sa).astype(BF16)
        dap_ref[...] = dap
        daa_ref[...] = daa
        dgate_ref[:, ATTN_WIDTH:ATTN_WIDTH + D_MODEL] = (dmerged * a_pool * (sp * (1.0 - sp))).astype(BF16)
        dgate_ref[:, ATTN_WIDTH + D_MODEL:] = (dmerged * a_attn * (sa * (1.0 - sa))).astype(BF16)
        dyp_ref[...] = _dot_nt(dap, wp_ref[...])
        dya = _dot_nt(daa, wa_ref[...])
        do = dya * (za * sz)
        do_ref[...] = do.astype(BF16)
        dgate_ref[:, :ATTN_WIDTH] = (dya * o * (sz * (1.0 + za * (1.0 - sz)))).astype(BF16)
        delta_ref[...] = _per_head_rowsum(do * o)

    row = lambda w: pl.BlockSpec((ROW_TILE, w), lambda i: (i, 0))
    const = lambda shape: pl.BlockSpec(shape, lambda i: (0, 0))
    return pl.pallas_call(
        body, name="merge_head_fwd_bwd", grid=(nb,),
        in_specs=[pl.BlockSpec((ROW_TILE, POOL_WIDTH), lambda i: (jnp.maximum(i - 1, 0), 0)), row(ATTN_WIDTH), row(E_COLS),
                  pl.BlockSpec((len(POOL_WINDOWS), POOL_GROUP, POOL_GROUP), lambda i: (0, 0, 0)), const((1, POOL_WIDTH)),
                  const((POOL_WIDTH, D_MODEL)), const((ATTN_WIDTH, D_MODEL)),
                  const((D_MODEL, D_MODEL)), _tokens_spec(), const((ROW_TILE, D_MODEL)), const((1, D_MODEL)), _tokens_spec()],
        out_specs=[row(POOL_WIDTH), row(D_MODEL), row(ATTN_WIDTH), row(D_MODEL), const((1, LANES)), const((1, D_MODEL)),
                   row(D_MODEL), row(D_MODEL), pl.BlockSpec((ROW_TILE, MAIN_COLS // 2), lambda i: (i, 1)),
                   row(ATTN_WIDTH), row(ATTN_WIDTH), row(POOL_WIDTH)],
        out_shape=[jax.ShapeDtypeStruct((lp, POOL_WIDTH), BF16), jax.ShapeDtypeStruct((lp, D_MODEL), BF16),
                   jax.ShapeDtypeStruct((lp, ATTN_WIDTH), BF16),
                   jax.ShapeDtypeStruct((lp, D_MODEL), F32), jax.ShapeDtypeStruct((1, LANES), F32),
                   jax.ShapeDtypeStruct((1, D_MODEL), F32),
                   jax.ShapeDtypeStruct((lp, D_MODEL), BF16), jax.ShapeDtypeStruct((lp, D_MODEL), BF16),
                   jax.ShapeDtypeStruct((lp, MAIN_COLS), BF16),
                   jax.ShapeDtypeStruct((lp, ATTN_WIDTH), BF16), jax.ShapeDtypeStruct((lp, ATTN_WIDTH), F32),
                   jax.ShapeDtypeStruct((lp, POOL_WIDTH), F32)],
        compiler_params=_params(("arbitrary",)),
    )(e, o, e, pw, scale, wup_p, wup_a, w_out, x2, metapad, gf, target)


def _per_head_rowsum(t):
    head = lax.broadcasted_iota(jnp.int32, t.shape, 1) // HEAD_DIM
    out = jnp.zeros_like(t)
    for h in range(N_HEADS):
        sel = head == h
        out = jnp.where(sel, jnp.sum(jnp.where(sel, t, 0.0), axis=1, keepdims=True), out)
    return out


def _merge_bwd(dh1, w_out, y_pool, y_attn, wup_p, wup_a, e, o):
    lp = o.shape[0]
    nb = lp // ROW_TILE

    def body(dh_ref, wo_ref, yp_ref, ya_ref, wp_ref, wa_ref, e_ref, o_ref,
             dap_ref, daa_ref, dg_ref, do_ref, delta_ref, dyp_ref):
        dmerged = _dot_nt(dh_ref[...].astype(BF16), wo_ref[...])
        a_pool = jnp.dot(yp_ref[...], wp_ref[...], preferred_element_type=F32)
        a_attn = jnp.dot(ya_ref[...], wa_ref[...], preferred_element_type=F32)
        sp = _sigmoid(e_ref[:, E_GP:E_GA].astype(F32))
        sa = _sigmoid(e_ref[:, E_GA:E_COLS].astype(F32))
        dap = (dmerged * sp).astype(BF16)
        daa = (dmerged * sa).astype(BF16)
        dap_ref[...] = dap
        daa_ref[...] = daa
        dg_ref[:, ATTN_WIDTH:ATTN_WIDTH + D_MODEL] = (dmerged * a_pool * (sp * (1.0 - sp))).astype(BF16)
        dg_ref[:, ATTN_WIDTH + D_MODEL:] = (dmerged * a_attn * (sa * (1.0 - sa))).astype(BF16)
        dyp_ref[...] = _dot_nt(dap, wp_ref[...])
        dya = _dot_nt(daa, wa_ref[...])
        za = e_ref[:, E_ZA:E_GP].astype(F32)
        sz = _sigmoid(za)
        o = o_ref[...]
        do = dya * (za * sz)
        do_ref[...] = do.astype(BF16)
        dg_ref[:, :ATTN_WIDTH] = (dya * o * (sz * (1.0 + za * (1.0 - sz)))).astype(BF16)
        delta_ref[...] = _per_head_rowsum(do * o)

    row = lambda w: pl.BlockSpec((ROW_TILE, w), lambda i: (i, 0))
    full = lambda a: pl.BlockSpec(a.shape, lambda i: (0, 0))
    return pl.pallas_call(
        body, name="merge_bwd", grid=(nb,),
        in_specs=[row(D_MODEL), full(w_out), row(POOL_WIDTH), row(ATTN_WIDTH), full(wup_p), full(wup_a),
                  row(E_COLS), row(ATTN_WIDTH)],
        out_specs=[row(D_MODEL), row(D_MODEL), pl.BlockSpec((ROW_TILE, MAIN_COLS // 2), lambda i: (i, 1)),
                   row(ATTN_WIDTH), row(ATTN_WIDTH), row(POOL_WIDTH)],
        out_shape=[jax.ShapeDtypeStruct((lp, D_MODEL), BF16), jax.ShapeDtypeStruct((lp, D_MODEL), BF16),
                   jax.ShapeDtypeStruct((lp, MAIN_COLS), BF16),
                   jax.ShapeDtypeStruct((lp, ATTN_WIDTH), BF16), jax.ShapeDtypeStruct((lp, ATTN_WIDTH), F32),
                   jax.ShapeDtypeStruct((lp, POOL_WIDTH), F32)],
        compiler_params=_params(("parallel",)),
    )(dh1, w_out, y_pool, y_attn, wup_p, wup_a, e, o)


def _pool_bwd(e, dy_pool, pw, scale, dproj, dq, dk, dv):
    lp = e.shape[0]
    nb = lp // ROW_TILE
    ng = len(POOL_WINDOWS)

    def body(uc_ref, up_ref, z_ref, dy_ref, pw_ref, sc_ref, base_ref, dq_ref, dk_ref, dv_ref, o_ref, dsc_ref, dpw_ref,
             cur_ref, nxt_ref, dz_cur_ref, dz_nxt_ref):
        i = pl.program_id(0)

        @pl.when(i == 0)
        def _():
            dsc_ref[...] = jnp.zeros_like(dsc_ref)
            dpw_ref[...] = jnp.zeros_like(dpw_ref)

        @pl.when(i < nb)
        def _():
            u_cur = uc_ref[...].astype(F32)
            u_prev = jnp.where(i == 0, 0.0, up_ref[...].astype(F32))
            ps, invs = _pool_p(u_cur, u_prev, i)
            z = z_ref[...].astype(F32)
            sz = _sigmoid(z)
            dy = dy_ref[...]
            dypre = dy * (z * sz)
            dsilu = sz * (1.0 + z * (1.0 - sz))
            for g in range(ng):
                sl = slice(g * POOL_GROUP, (g + 1) * POOL_GROUP)
                pb = ps[g].astype(BF16)
                w = pw_ref[g]
                yraw = jnp.dot(pb, w, preferred_element_type=F32)
                sc = sc_ref[:, sl]
                dz_nxt_ref[:, sl] = (dy[:, sl] * (yraw * sc) * dsilu[:, sl]).astype(BF16)
                dsc_ref[:, sl] += jnp.sum(dypre[:, sl] * yraw, axis=0, keepdims=True)
                dyraw = (dypre[:, sl] * sc).astype(BF16)
                dpw_ref[g] += _dot_tn(pb, dyraw)
                nxt_ref[:, sl] = _dot_nt(dyraw, w) * invs[g]

        @pl.when(i == nb)
        def _():
            nxt_ref[...] = jnp.zeros_like(nxt_ref)

        @pl.when(i > 0)
        def _():
            cur, nxt = cur_ref[...], nxt_ref[...]
            pos = _pool_counts(i - 1)
            for g, w in enumerate(POOL_WINDOWS):
                sl = slice(g * POOL_GROUP, (g + 1) * POOL_GROUP)
                xc = jnp.concatenate([cur[:, sl], nxt[:, sl]], axis=0)
                win = _leading_sums(xc, g + 1)[:ROW_TILE, :]
                dp = cur[:, sl] * jnp.minimum(pos + 1, w).astype(F32)
                o_ref[:, sl] = (win - dp).astype(BF16)
            o_ref[:, POOL_WIDTH:2 * POOL_WIDTH] = dz_cur_ref[...]
            parts = [(dq_ref[...] * HEAD_DIM ** -0.5).astype(BF16), dk_ref[...], dv_ref[...]]
            for t, part in enumerate(parts):
                o_ref[:, (2 + t) * ATTN_WIDTH:(3 + t) * ATTN_WIDTH] = part

        cur_ref[...] = nxt_ref[...]
        dz_cur_ref[...] = dz_nxt_ref[...]

    blk = (ROW_TILE, POOL_WIDTH)
    at = lambda i: jnp.minimum(i, nb - 1)
    return pl.pallas_call(
        body, name="pool_bwd", grid=(nb + 1,),
        in_specs=[pl.BlockSpec(blk, lambda i: (at(i), 0)), pl.BlockSpec(blk, lambda i: (jnp.maximum(at(i) - 1, 0), 0)),
                  pl.BlockSpec(blk, lambda i: (at(i), 1)), pl.BlockSpec(blk, lambda i: (at(i), 0)),
                  pl.BlockSpec((ng, POOL_GROUP, POOL_GROUP), lambda i: (0, 0, 0)),
                  pl.BlockSpec((1, POOL_WIDTH), lambda i: (0, 0)), pl.BlockSpec(memory_space=pl.ANY)]
        + [pl.BlockSpec((ROW_TILE, ATTN_WIDTH), lambda i: (jnp.maximum(i - 1, 0), 0))] * 3,
        out_specs=[pl.BlockSpec((ROW_TILE, MAIN_COLS // 2), lambda i: (jnp.maximum(i - 1, 0), 0)),
                   pl.BlockSpec((1, POOL_WIDTH), lambda i: (0, 0)),
                   pl.BlockSpec((ng, POOL_GROUP, POOL_GROUP), lambda i: (0, 0, 0))],
        out_shape=[jax.ShapeDtypeStruct(dproj.shape, BF16), jax.ShapeDtypeStruct((1, POOL_WIDTH), F32),
                   jax.ShapeDtypeStruct((ng, POOL_GROUP, POOL_GROUP), F32)],
        scratch_shapes=[pltpu.VMEM(blk, F32), pltpu.VMEM(blk, F32), pltpu.VMEM(blk, BF16), pltpu.VMEM(blk, BF16)],
        input_output_aliases={6: 0},
        compiler_params=_params(("arbitrary",)),
    )(e, e, e, dy_pool, pw, scale, dproj, dq, dk, dv)


def _attn_bwd(qkv, do, lse, delta, bias, nb):
    lk = qkv.shape[0]
    lp = nb * ROW_TILE
    nkb = lk // KV_TILE_BWD
    n_pairs = N_HEADS // 2
    per_kv = KV_TILE_BWD // ROW_TILE

    def body(q_ref, k_ref, v_ref, do_ref, lse_ref, dl_ref, b_ref, dq_ref, dk_ref, dv_ref, dc_ref, dcq_ref,
             dk_acc, dv_acc, dc_acc):
        jj = pl.program_id(1)

        @pl.when(jj == 0)
        def _():
            dq_ref[...] = jnp.zeros_like(dq_ref)
            dcq_ref[...] = jnp.zeros_like(dcq_ref)

        dk_acc[...] = jnp.zeros_like(dk_acc)
        dv_acc[...] = jnp.zeros_like(dv_acc)
        dc_acc[...] = jnp.zeros_like(dc_acc)

        def block(i, n_keys, masked):
            kb, vb = k_ref[:n_keys, :], v_ref[:n_keys, :]
            rows = pl.ds(pl.multiple_of(i * ROW_TILE, ROW_TILE), ROW_TILE)
            qs = _stack_heads(q_ref[rows, :])
            dos = _stack_heads(do_ref[rows, :])
            lse_i, dl_i = lse_ref[rows, :], dl_ref[rows, :]
            s = _dot_nt(qs, kb)
            dp = _dot_nt(dos, vb)
            if masked:
                valid = _causal(i, jj, 1, KV_TILE_BWD)[:, :n_keys]
            ps, dss, dcs, rowsums = [], [], [], []
            for hd in range(2):
                half = slice(hd * ROW_TILE, (hd + 1) * ROW_TILE)
                col = slice(hd * HEAD_DIM, hd * HEAD_DIM + 1)
                sh = s[half] - b_ref[i, 0, hd:hd + 1, :n_keys]
                if masked:
                    sh = jnp.where(valid, sh, NEG)
                p = jnp.exp(sh - lse_i[:, col])
                ds = p * (dp[half] - dl_i[:, col])
                ps.append(p.astype(BF16))
                dss.append(ds.astype(BF16))
                dcs.append(jnp.sum(ds, axis=0, keepdims=True))
                rowsums.append(jnp.sum(ds, axis=1, keepdims=True))
            dsb = jnp.concatenate(dss, axis=0)
            dv_acc[:n_keys, :] += _dot_tn(jnp.concatenate(ps, axis=0), dos)
            dk_acc[:n_keys, :] += _dot_tn(dsb, qs)
            dc_acc[:, :n_keys] -= jnp.concatenate(dcs, axis=0)
            dq_ref[rows, :] += _unstack_heads(jnp.dot(dsb, kb, preferred_element_type=F32))
            dcq_ref[rows, :] += _unstack_heads(jnp.broadcast_to(jnp.concatenate(rowsums, axis=0), (2 * ROW_TILE, LANES)))

        first_q = per_kv * jj
        for r in range(per_kv):
            @pl.when(first_q + r < nb)
            def _():
                block(first_q + r, (r + 1) * ROW_TILE, True)

        def rest(i, carry):
            block(i, KV_TILE_BWD, False)
            return carry

        lax.fori_loop(jnp.minimum(first_q + per_kv, nb), nb, rest, 0)
        dk_ref[...] = dk_acc[...].astype(BF16)
        dv_ref[...] = dv_acc[...].astype(BF16)
        dc_ref[0, 0] = dc_acc[...]

    whole = lambda rows: pl.BlockSpec((rows, LANES), lambda hp, jj: (0, hp))
    kv_blk = lambda off: pl.BlockSpec((KV_TILE_BWD, LANES), lambda hp, jj: (jj, off + hp))
    return pl.pallas_call(
        body, name="attn_bwd", grid=(n_pairs, nkb),
        in_specs=[whole(lk), kv_blk(n_pairs), kv_blk(2 * n_pairs), whole(lp), whole(lp), whole(lp),
                  pl.BlockSpec((nb, 1, 2, KV_TILE_BWD), lambda hp, jj: (0, hp, 0, jj))],
        out_specs=[whole(lp), kv_blk(0), kv_blk(0),
                   pl.BlockSpec((1, 1, 2, KV_TILE_BWD), lambda hp, jj: (hp, jj, 0, 0)), whole(lp)],
        out_shape=[jax.ShapeDtypeStruct((lp, ATTN_WIDTH), F32), jax.ShapeDtypeStruct((lk, ATTN_WIDTH), BF16),
                   jax.ShapeDtypeStruct((lk, ATTN_WIDTH), BF16),
                   jax.ShapeDtypeStruct((n_pairs, nkb, 2, KV_TILE_BWD), F32),
                   jax.ShapeDtypeStruct((lp, ATTN_WIDTH), F32)],
        scratch_shapes=[pltpu.VMEM((KV_TILE_BWD, LANES), F32), pltpu.VMEM((KV_TILE_BWD, LANES), F32),
                        pltpu.VMEM((2, KV_TILE_BWD), F32)],
        compiler_params=_params(("parallel", "arbitrary")),
    )(qkv, qkv, qkv, do, lse, delta, bias)


def _gates_bwd(dc, dcq, f, bfg):
    nb = f.shape[0] // ROW_TILE

    def body(dc_ref, dcq_ref, f_ref, b_ref, df_ref, db_ref, carry):
        step = pl.program_id(0)
        i = nb - 1 - step

        @pl.when(step == 0)
        def _():
            carry[...] = jnp.zeros_like(carry)
            db_ref[...] = jnp.zeros_like(db_ref)

        dcb = dc_ref[...]
        lane = lax.broadcasted_iota(jnp.int32, (ROW_TILE, F_COLS), 1)
        for h in range(N_HEADS):
            dcb = dcb + jnp.where(lane == h, dcq_ref[:, HEAD_DIM * h:HEAD_DIM * h + 1], 0.0)
        r_i = lax.broadcasted_iota(jnp.int32, (ROW_TILE, ROW_TILE), 0)
        c_i = lax.broadcasted_iota(jnp.int32, (ROW_TILE, ROW_TILE), 1)
        upper = (c_i >= r_i).astype(F32)
        dlf = jnp.dot(upper, dcb, precision=HIGHEST, preferred_element_type=F32) + carry[...]
        carry[...] = carry[...] + jnp.sum(dcb, axis=0, keepdims=True)
        logit = f_ref[...] + b_ref[...]
        dlogit = jnp.where(_valid_gate_mask(i), dlf * _sigmoid(-logit), 0.0)
        df_ref[...] = dlogit.astype(BF16)
        db_ref[...] += jnp.sum(dlogit, axis=0, keepdims=True)

    blk = pl.BlockSpec((ROW_TILE, F_COLS), lambda s: (nb - 1 - s, 0))
    wide = pl.BlockSpec((ROW_TILE, ATTN_WIDTH), lambda s: (nb - 1 - s, 0))
    one = pl.BlockSpec((1, F_COLS), lambda s: (0, 0))
    return pl.pallas_call(
        body, name="gates_bwd", grid=(nb,),
        in_specs=[blk, wide, blk, one], out_specs=[blk, one],
        out_shape=[jax.ShapeDtypeStruct(f.shape, BF16), jax.ShapeDtypeStruct((1, F_COLS), F32)],
        scratch_shapes=[pltpu.VMEM((1, F_COLS), F32)],
        compiler_params=_params(("arbitrary",)),
    )(dc, dcq, f, bfg)


def _input_bwd(dproj, df, wt_e, wt_qkv, wt_f, x2, metapad, dh1, g1, hs, sc=None):
    nb = x2.shape[0] // ROW_TILE + 1
    n = len(hs)
    sc_in, sc_shapes, sc_sems = _scatter_operands(hs, sc)
    nq = len(sc_in)

    def body(dp_ref, df_ref, we_ref, w_ref, wf_ref, x_ref, mp_ref, dh_ref, g_ref, *rest):
        h_refs, s_ref = rest[:n], (rest[n] if sc is not None else None)
        gx_ref, g0_ref, dg_ref = rest[nq:nq + 3]
        q_refs, sq_ref = rest[nq + 3:nq + 3 + n], (rest[nq + 3 + n] if sc is not None else None)
        sems = rest[2 * nq + 3:]
        i = pl.program_id(0)

        @pl.when(i == 0)
        def _():
            dg_ref[...] = jnp.zeros_like(dg_ref)
            own, sends, _ = _chip_scatter_plan(h_refs, s_ref, q_refs, sq_ref, *sems)
            for cp in own + sends:
                cp.start()

        dhn = (jnp.dot(dp_ref[:, :E_ZA], we_ref[:E_ZA, :], preferred_element_type=F32)
               + jnp.dot(dp_ref[:, E_ZA:MAIN_COLS // 2], w_ref[...], preferred_element_type=F32)
               + jnp.dot(dp_ref[:, MAIN_COLS // 2:], we_ref[E_ZA:, :], preferred_element_type=F32)
               + jnp.dot(df_ref[...], wf_ref[...], preferred_element_type=F32))
        h0 = jnp.where(i == 0, mp_ref[...], x_ref[...])
        r = lax.rsqrt(jnp.mean(h0 * h0, axis=-1, keepdims=True) + RMS_EPS)
        xhat = h0 * r
        dg_ref[...] += jnp.sum(dhn * xhat, axis=0, keepdims=True)
        dxhat = dhn * g_ref[...]
        dh0 = dh_ref[...] + r * (dxhat - xhat * jnp.mean(dxhat * xhat, axis=-1, keepdims=True))
        gx_ref[...] = dh0

        @pl.when(i == 0)
        def _():
            g0_ref[...] = dh0

        @pl.when(i == nb - 1)
        def _():
            own, sends, recvs = _chip_scatter_plan(h_refs, s_ref, q_refs, sq_ref, *sems)
            for cp in recvs:
                cp.wait_recv()
            for cp in sends:
                cp.wait_send()
            for cp in own:
                cp.wait()

    const = lambda shape: pl.BlockSpec(shape, lambda i: (0, 0))
    res = pl.pallas_call(
        body, name="input_bwd", grid=(nb,),
        in_specs=[pl.BlockSpec((ROW_TILE, MAIN_COLS), lambda i: (i, 0)), pl.BlockSpec((ROW_TILE, F_COLS), lambda i: (i, 0)),
                  const((E_COLS, D_MODEL)), const((QKV_COLS, D_MODEL)), const((F_COLS, D_MODEL)),
                  _tokens_spec(), const((ROW_TILE, D_MODEL)),
                  pl.BlockSpec((ROW_TILE, D_MODEL), lambda i: (i, 0)), const((1, D_MODEL))] + [HBM] * nq,
        out_specs=[_tokens_spec(), const((ROW_TILE, D_MODEL)), const((1, D_MODEL))] + [HBM] * nq,
        out_shape=[jax.ShapeDtypeStruct(x2.shape, F32), jax.ShapeDtypeStruct((ROW_TILE, D_MODEL), F32),
                   jax.ShapeDtypeStruct((1, D_MODEL), F32)] + sc_shapes,
        scratch_shapes=sc_sems,
        compiler_params=_params(("arbitrary",), vmem=56 * 1024 * 1024),
    )(dproj, df, wt_e, wt_qkv, wt_f, x2, metapad, dh1, g1, *_in_hbm(*sc_in))
    return res[:3], res[3:3 + n], (res[3 + n] if sc is not None else None)


def _adamw(w, g, m, v, name):
    rows, cols = w.shape
    if rows % 8 == 0:
        tr, tc = _row_tile8(rows), cols
    else:
        tr, tc = rows, (2 * LANES if cols % (2 * LANES) == 0 and rows > 8 else cols)

    def body(w_ref, g_ref, m_ref, v_ref, d_ref, mo_ref, vo_ref):
        g_ = g_ref[...]
        m_new = ADAM_B1 * m_ref[...] + (1.0 - ADAM_B1) * g_
        v_new = ADAM_B2 * v_ref[...] + (1.0 - ADAM_B2) * (g_ * g_)
        m_hat = m_new / (1.0 - ADAM_B1 ** ADAM_STEP)
        v_hat = v_new / (1.0 - ADAM_B2 ** ADAM_STEP)
        d_ref[...] = -ADAM_LR * (m_hat / (jnp.sqrt(v_hat) + ADAM_EPS) + ADAM_WD * w_ref[...])
        mo_ref[...] = m_new
        vo_ref[...] = v_new

    blk = pl.BlockSpec((tr, tc), lambda i, j: (i, j))
    return pl.pallas_call(
        body, name=name, grid=(rows // tr, cols // tc),
        in_specs=[blk] * 4, out_specs=[blk] * 3,
        out_shape=[jax.ShapeDtypeStruct(w.shape, F32)] * 3,
        compiler_params=_params(("parallel", "parallel")),
    )(w, g, m, v)


def _adamw_native(w3, g3, m3, v3, name):
    rows = w3.shape[0]
    tr = rows // 2
    blk = pl.BlockSpec((tr,) + w3.shape[1:], lambda i: (i, 0, 0))
    shape = jax.ShapeDtypeStruct(w3.shape, F32)

    def moments(g_ref, m_ref, v_ref, mo_ref, vo_ref):
        g_ = g_ref[...]
        mo_ref[...] = ADAM_B1 * m_ref[...] + (1.0 - ADAM_B1) * g_
        vo_ref[...] = ADAM_B2 * v_ref[...] + (1.0 - ADAM_B2) * (g_ * g_)

    new_m, new_v = pl.pallas_call(
        moments, name=name + "_moments", grid=(2,), in_specs=[blk] * 3, out_specs=[blk] * 2, out_shape=[shape] * 2,
        compiler_params=_params(("parallel",)),
    )(g3, m3, v3)

    def delta(w_ref, m_ref, v_ref, d_ref):
        m_hat = m_ref[...] / (1.0 - ADAM_B1 ** ADAM_STEP)
        v_hat = v_ref[...] / (1.0 - ADAM_B2 ** ADAM_STEP)
        d_ref[...] = -ADAM_LR * (m_hat / (jnp.sqrt(v_hat) + ADAM_EPS) + ADAM_WD * w_ref[...])

    d = pl.pallas_call(
        delta, name=name + "_delta", grid=(2,), in_specs=[blk] * 3, out_specs=blk, out_shape=shape,
        compiler_params=_params(("parallel",)),
    )(w3, new_m, new_v)
    return d, new_m, new_v


def _row_tile8(rows):
    best = rows
    for t in range(8, 257, 8):
        if rows % t == 0:
            best = t
    return best


def kernel(x, meta_tokens, norm_g, w_in, b_forget, pool_w, pool_scale, w_up_pool, w_up_attn, w_out, final_norm_g, loss_target, m_meta_tokens, m_norm_g, m_w_in, m_b_forget, m_pool_w, m_pool_scale, m_w_up_pool, m_w_up_attn, m_w_out, m_final_norm_g, v_meta_tokens, v_norm_g, v_w_in, v_b_forget, v_pool_w, v_pool_scale, v_w_up_pool, v_w_up_attn, v_w_out, v_final_norm_g):
    seq = x.shape[1]
    assert seq % ROW_TILE == 0 and x.shape[0] == 1
    lp = seq + ROW_TILE
    nb = lp // ROW_TILE
    lk = -(-lp // KV_TILE_BWD) * KV_TILE_BWD
    core = jnp.reshape(lax.axis_index("c"), (1,)).astype(jnp.int32)
    x2 = x[0]
    target = loss_target[0]
    sh_d = D_MODEL // N_CHIPS

    to_rows = lambda a: jnp.transpose(a, (2, 0, 1))
    from_rows = lambda a: jnp.transpose(a, (1, 2, 0))
    gf = final_norm_g.reshape(1, D_MODEL)
    bfg = jnp.pad(b_forget, ((0, 0), (0, F_COLS - N_HEADS)))
    pw_b = pool_w[0].astype(BF16)
    hn, (wg_in, meta_g) = _rmsnorm_fwd(x2, norm_g, lk, [jnp.transpose(w_in[0]).astype(BF16), meta_tokens], [1, 0])
    wt_e, wt_qkv, wt_f = _weight_sections(wg_in)
    meta_full = jnp.transpose(meta_g, (1, 0, 2)).reshape(N_META, D_MODEL)
    metapad = jnp.pad(meta_full, ((PAD_ROWS, 0), (0, 0)))

    tm = _row_tile(lp, 2200)
    e, (wg_up_p, wg_up_a, wg_out) = _mm_nt(
        hn, wt_e, BF16, "in_proj_gates", lp, E_COLS, tm, 512,
        gather=([w_up_pool[0].astype(BF16), w_up_attn[0].astype(BF16), w_out[0].astype(BF16)], [0, 0, 0]))
    wup_p = jnp.transpose(wg_up_p, (1, 0, 2)).reshape(POOL_WIDTH, D_MODEL)
    wup_a = jnp.transpose(wg_up_a, (1, 0, 2)).reshape(ATTN_WIDTH, D_MODEL)
    wout = wg_out.reshape(D_MODEL, D_MODEL)
    qkv = _mm_nt(hn, wt_qkv, BF16, "in_proj_qkv", lk, QKV_COLS, _row_tile(lk, 2600), 512, scale_first=HEAD_DIM ** -0.5)
    f = _mm_nt(hn, wt_f, F32, "in_proj_forget", lp, F_COLS, tm, F_COLS)
    c = _gates_fwd(f, bfg)
    c_t = jnp.transpose(c[:, :N_HEADS])
    c_first = jnp.transpose(c_t[:, ::ROW_TILE]).reshape(nb, N_HEADS // 2, 2, 1)
    c_keys = jnp.pad(c_t, ((0, 0), (0, lk - lp))).reshape(1, N_HEADS // 2, 2, lk)
    bias = jnp.where(jnp.arange(lk) < PAD_ROWS, -NEG, c_keys - c_first)
    o, lse = _attn_fwd(qkv, bias, nb)
    (y_pool, merged, y_attn, dh1, loss_part, dgf, dap, daa, dproj_half, do, delta, dy_pool) = _merge_head_fwd_bwd(
        o, e, pw_b, pool_scale, wup_p, wup_a, wout, x2, metapad, gf, target)
    dq, dk, dv, dc4, dcq = _attn_bwd(qkv, do, lse, delta, bias, nb)
    dc = jnp.transpose(dc4, (1, 3, 0, 2)).reshape(lk, N_HEADS)[:lp]
    df, db = _gates_bwd(jnp.pad(dc, ((0, 0), (0, F_COLS - N_HEADS))), dcq, f, bfg)
    dproj, dscale, dpw = _pool_bwd(e, dy_pool, pw_b, pool_scale, dproj_half, dq, dk, dv)
    tk = _row_tile(lp, 1100)
    dw_out = _mm_tn(merged, dh1, "grad_w_out", lp, 512, D_MODEL, tk)
    dw_up_p = _mm_tn(y_pool, dap, "grad_w_up_pool", lp, 512, D_MODEL, lp, chunks=N_CHIPS)
    dw_up_a = _mm_tn(y_attn, daa, "grad_w_up_attn", lp, 512, D_MODEL, lp, chunks=N_CHIPS)
    dwt_f = _mm_tn(df, hn, "grad_w_in_forget", lp, F_COLS, D_MODEL, tk)

    def pad8(a):
        return jnp.pad(a, ((0, (-a.shape[0]) % 8), (0, 0)))

    small_parts = [pad8(a) for a in (dgf.reshape(-1, LANES), dscale.reshape(-1, LANES), db, dpw.reshape(-1, LANES),
                                     loss_part)]
    small = jnp.concatenate(small_parts, axis=0)
    soffs = [0]
    for p in small_parts:
        soffs.append(soffs[-1] + p.shape[0])

    gs_rows = [dw_up_p, dw_up_a, dw_out.reshape(N_CHIPS, sh_d, D_MODEL)]
    half = MAIN_COLS // 2
    dwt_a, (*rb_rows, rb_f, sr) = _mm_tn(dproj, hn, "grad_w_in_a", lp, half // 4, D_MODEL, lp, m=half, m_off=0,
                                         swap=(gs_rows + [dwt_f, small], [1, 1, 1, 1, None]))
    *hs_rows, sc = _add_sibling_half(gs_rows, rb_rows, [1, 1, 1], small, sr, core)
    dwt_b, (rb_a,), (qs_rows, sq) = _mm_tn(dproj, hn, "grad_w_in_b", lp, half // 4, D_MODEL, lp, m=half, m_off=4,
                                           swap=([dwt_a], [1]), scatter=(hs_rows, sc))
    h_in = _add_halves_w_in([dwt_a, dwt_b, dwt_f], [rb_a, rb_f], core)
    grad_axes = [2, 1, 1, 1]
    (grad_x, g_block0, dg1), (q_in,), _ = _input_bwd(dproj, df, wt_e, wt_qkv, wt_f, x2, metapad, dh1, norm_g, [h_in])
    late_parts = jnp.concatenate([pad8(dg1.reshape(-1, LANES)), g_block0[PAD_ROWS:].reshape(-1, LANES)], axis=0)
    g_w_in_rows, g_w_up_p, g_w_up_a, g_w_out, st, late = _reduce_allgather(
        [q_in] + list(qs_rows), grad_axes, sq, [True, False, False, False], late_parts)
    n_norm = D_MODEL // LANES
    g_norm = late[:n_norm].reshape(1, D_MODEL)
    chip = 2 * lax.axis_index("x") + lax.axis_index("y")
    g_meta = lax.dynamic_slice_in_dim(late[n_norm:].reshape(N_META, D_MODEL), chip * sh_d, sh_d, axis=1)

    spiece = lambda k, rows: st[soffs[k]:soffs[k] + rows]
    g_final = spiece(0, D_MODEL // LANES).reshape(1, D_MODEL)
    g_scale = spiece(1, POOL_WIDTH // LANES).reshape(1, POOL_WIDTH)
    g_bf = spiece(2, 1)
    g_pw = spiece(3, POOL_WIDTH)
    loss = st[soffs[4], 0]

    def pad_lanes(a):
        return jnp.pad(a, ((0, 0), (0, F_COLS - N_HEADS)))

    w_in_res = (g_w_in_rows,) + _adamw_native(to_rows(w_in), g_w_in_rows, to_rows(m_w_in), to_rows(v_w_in), "adamw_w_in")

    upd = [
        ("meta_tokens", meta_tokens, g_meta, m_meta_tokens, v_meta_tokens),
        ("norm_g", norm_g, g_norm, m_norm_g, v_norm_g),
        ("w_in", None, None, None, None),
        ("b_forget", pad_lanes(b_forget), g_bf, pad_lanes(m_b_forget), pad_lanes(v_b_forget)),
        ("pool_w", pool_w.reshape(-1, LANES), g_pw, m_pool_w.reshape(-1, LANES), v_pool_w.reshape(-1, LANES)),
        ("pool_scale", pool_scale, g_scale, m_pool_scale, v_pool_scale),
        ("w_up_pool", w_up_pool[0], g_w_up_p, m_w_up_pool[0], v_w_up_pool[0]),
        ("w_up_attn", w_up_attn[0], g_w_up_a, m_w_up_attn[0], v_w_up_attn[0]),
        ("w_out", w_out[0], g_w_out, m_w_out[0], v_w_out[0]),
        ("final_norm_g", gf, g_final, m_final_norm_g.reshape(1, D_MODEL), v_final_norm_g.reshape(1, D_MODEL)),
    ]
    shapes = [meta_tokens.shape, norm_g.shape, w_in.shape, b_forget.shape, pool_w.shape, pool_scale.shape,
              w_up_pool.shape, w_up_attn.shape, w_out.shape, final_norm_g.shape]
    grads, deltas, new_ms, new_vs = [], [], [], []
    for (name, w_, g_, m_in, v_in), shp in zip(upd, shapes):
        if name == "w_in":
            res = tuple(from_rows(a) for a in w_in_res)
        else:
            d_, mn_, vn_ = _adamw(w_, g_, m_in, v_in, "adamw_" + name)
            res = (g_, d_, mn_, vn_)
        if name == "b_forget":
            res = tuple(a[:, :N_HEADS] for a in res)
        for lst, a in zip((grads, deltas, new_ms, new_vs), res):
            lst.append(a.reshape(shp))

    return (loss, grad_x.reshape(x.shape), *grads, *deltas, *new_ms, *new_vs)
```
